```python
import jax, jax.numpy as jnp
from jax import lax
import numpy as np

D_MODEL = 2048
BATCH = 8
SEQ = 2048
DEPTH = 1

D_MIX = D_MODEL
D_CONV = D_MIX // 2
D_ATTN = D_MIX - D_CONV
HEAD_DIM = 64
N_HEADS = D_ATTN // HEAD_DIM
CONV_WIDTH = 3
DILATED_BRANCHES = ((128, 1), (512, 4), (2048, 16))
N_PROJ = 4 * D_CONV + 4 * D_ATTN
EPS = 1e-6
NEG_INF = -1e30

kernel_name = "hybrid_conv_dilated_attn_block"


def rms_norm(x, gain):
    xf = x.astype(jnp.float32)
    y = xf * lax.rsqrt(jnp.mean(xf * xf, axis=-1, keepdims=True) + EPS)
    return (y * gain.astype(jnp.float32)).astype(x.dtype)


def alibi_slopes(n_heads):
    return 2.0 ** (-8.0 * jnp.arange(1, n_heads + 1, dtype=jnp.float32) / n_heads)


def short_conv_centred(u, w, b):
    ch = u.shape[-1]
    pad = CONV_WIDTH // 2
    y = lax.conv_general_dilated(
        u, w[:, None, :].astype(u.dtype), window_strides=(1,), padding=((pad, pad),),
        dimension_numbers=("NWC", "WIO", "NWC"), feature_group_count=ch)
    return y + b.astype(u.dtype)


def dilated_branch(q, k, v, slopes, window, dilation):
    bsz, n_heads, seq, hd = q.shape
    r = dilation
    n = (window // 2) // r
    L = seq // r
    nb = -(-L // n)
    Lp = nb * n

    def to_res(t):
        return t.reshape(bsz, n_heads, L, r, hd).transpose(0, 1, 3, 2, 4)

    qb = jnp.pad(to_res(q), ((0, 0), (0, 0), (0, 0), (0, Lp - L), (0, 0)))
    qb = qb.reshape(bsz, n_heads, r, nb, n, hd)
    kv_pad = ((0, 0), (0, 0), (0, 0), (n, Lp - L + n), (0, 0))
    kb = jnp.pad(to_res(k), kv_pad).reshape(bsz, n_heads, r, nb + 2, n, hd)
    vb = jnp.pad(to_res(v), kv_pad).reshape(bsz, n_heads, r, nb + 2, n, hd)
    shifts = (slice(0, nb), slice(1, nb + 1), slice(2, nb + 2))

    s = jnp.concatenate(
        [jnp.einsum("bhrnqd,bhrnkd->bhrnqk", qb, kb[:, :, :, sl]) for sl in shifts],
        axis=-1).astype(jnp.float32) * (hd ** -0.5)

    qi = jnp.arange(n)
    ki = jnp.arange(3 * n)
    blk = jnp.arange(nb)
    off = ki[None, :] - n - qi[:, None]
    key_idx = blk[:, None] * n + ki[None, :] - n
    valid = (jnp.abs(off) <= n)[None] & ((key_idx >= 0) & (key_idx < L))[:, None, :]
    dist = (r * jnp.abs(off)).astype(jnp.float32)
    bias = -slopes[:, None, None] * dist[None]
    s = jnp.where(valid[None, None, None], s + bias[None, :, None, None], NEG_INF)

    m = jnp.max(s, axis=-1, keepdims=True)
    p = jnp.exp(s - m)
    den = jnp.sum(p, axis=-1, keepdims=True)
    o = jnp.einsum("bhrnqk,bhrnkd->bhrnqd", p[..., 0:n], vb[:, :, :, shifts[0]].astype(jnp.float32))
    o = o + jnp.einsum("bhrnqk,bhrnkd->bhrnqd", p[..., n:2 * n], vb[:, :, :, shifts[1]].astype(jnp.float32))
    o = o + jnp.einsum("bhrnqk,bhrnkd->bhrnqd", p[..., 2 * n:], vb[:, :, :, shifts[2]].astype(jnp.float32))
    o = o / den
    lse = (m + jnp.log(den))[..., 0]

    o = o.reshape(bsz, n_heads, r, Lp, hd)[:, :, :, :L]
    o = o.transpose(0, 1, 3, 2, 4).reshape(bsz, n_heads, seq, hd)
    lse = lse.reshape(bsz, n_heads, r, Lp)[..., :L]
    lse = lse.transpose(0, 1, 3, 2).reshape(bsz, n_heads, seq)
    return o, lse


def dilated_mixture_attention(q, k, v):
    slopes = alibi_slopes(q.shape[1])
    outs, lses = [], []
    for window, dilation in DILATED_BRANCHES:
        o, lse = dilated_branch(q, k, v, slopes, window, dilation)
        outs.append(o)
        lses.append(lse)
    alpha = jax.nn.softmax(jnp.stack(lses, axis=0), axis=0)
    return jnp.sum(alpha[..., None] * jnp.stack(outs, axis=0), axis=0)


def hybrid_layer(x, mod, g_pre, w_in, conv_w, conv_b, g_conv, g_attn, w_out, g_post):
    bsz, seq, _ = x.shape
    shift, scale, gate = jnp.split(mod, 3, axis=-1)
    h = rms_norm(x, g_pre) * (1.0 + scale[:, None, :]) + shift[:, None, :]

    proj = jnp.einsum("bsd,dn->bsn", h, w_in)
    cuts = np.cumsum([D_CONV, D_CONV, D_CONV, D_CONV, D_ATTN, D_ATTN, D_ATTN])
    u, b_gate, c_gate, z_c, q, k, v, z_a = jnp.split(proj, cuts, axis=-1)

    y_c = b_gate * short_conv_centred(c_gate * u, conv_w, conv_b)
    y_c = rms_norm(y_c, g_conv) * jax.nn.silu(z_c)

    def heads(t):
        return t.reshape(bsz, seq, N_HEADS, HEAD_DIM).transpose(0, 2, 1, 3)
    o = dilated_mixture_attention(heads(q), heads(k), heads(v))
    y_a = o.transpose(0, 2, 1, 3).reshape(bsz, seq, D_ATTN).astype(x.dtype)
    y_a = rms_norm(y_a, g_attn) * jax.nn.silu(z_a)

    y = jnp.einsum("bsn,nd->bsd", jnp.concatenate([y_c, y_a], axis=-1), w_out)
    return x + gate[:, None, :] * rms_norm(y, g_post)


def _fwd_setup_inputs(seed: int = 0) -> dict:
    key = jax.random.key(seed)
    ks = jax.random.split(key, 14)
    f32 = jnp.float32
    x = jax.random.normal(ks[0], (BATCH, SEQ, D_MODEL), f32)
    c = jax.random.normal(ks[1], (BATCH, D_MODEL), f32)
    w_ada = jax.random.normal(ks[2], (DEPTH, D_MODEL, 3 * D_MODEL), f32) * D_MODEL ** -0.5
    b_ada = 0.02 * jax.random.normal(ks[3], (DEPTH, 3 * D_MODEL), f32)
    g_pre = 1.0 + 0.05 * jax.random.normal(ks[4], (DEPTH, D_MODEL), f32)
    w_in = jax.random.normal(ks[5], (DEPTH, D_MODEL, N_PROJ), f32) * D_MODEL ** -0.5
    conv_w = jax.random.normal(ks[6], (DEPTH, CONV_WIDTH, D_CONV), f32) * CONV_WIDTH ** -0.5
    conv_b = 0.02 * jax.random.normal(ks[7], (DEPTH, D_CONV), f32)
    g_conv = 1.0 + 0.05 * jax.random.normal(ks[8], (DEPTH, D_CONV), f32)
    g_attn = 1.0 + 0.05 * jax.random.normal(ks[9], (DEPTH, D_ATTN), f32)
    w_out = jax.random.normal(ks[10], (DEPTH, D_MIX, D_MODEL), f32) * D_MIX ** -0.5
    g_post = 1.0 + 0.05 * jax.random.normal(ks[11], (DEPTH, D_MODEL), f32)
    return {"x": x, "c": c, "w_ada": w_ada, "b_ada": b_ada, "g_pre": g_pre,
            "w_in": w_in, "conv_w": conv_w, "conv_b": conv_b, "g_conv": g_conv,
            "g_attn": g_attn, "w_out": w_out, "g_post": g_post}


def _fwd_reference(x, c, w_ada, b_ada, g_pre, w_in, conv_w, conv_b, g_conv, g_attn, w_out, g_post):
    c_act = jax.nn.silu(c)
    for layer in range(DEPTH):
        mod = jnp.einsum("bd,dn->bn", c_act, w_ada[layer]) + b_ada[layer]
        x = hybrid_layer(x, mod, g_pre[layer], w_in[layer], conv_w[layer], conv_b[layer],
                         g_conv[layer], g_attn[layer], w_out[layer], g_post[layer])
    return x


import jax as _jax
import jax.numpy as _jnp

TWIN_FORMAT = 'train_step'
FWD_PARAMS = ['x', 'c', 'w_ada', 'b_ada', 'g_pre', 'w_in', 'conv_w', 'conv_b', 'g_conv', 'g_attn', 'w_out', 'g_post']
TWIN_WEIGHTS = ['w_ada', 'b_ada', 'g_pre', 'w_in', 'conv_w', 'conv_b', 'g_conv', 'g_attn', 'w_out', 'g_post']
TWIN_DIFF_INPUT = 'x'
TWIN_INPUTS = ['x', 'c', 'w_ada', 'b_ada', 'g_pre', 'w_in', 'conv_w', 'conv_b', 'g_conv', 'g_attn', 'w_out', 'g_post', 'loss_target', 'm_w_ada', 'm_b_ada', 'm_g_pre', 'm_w_in', 'm_conv_w', 'm_conv_b', 'm_g_conv', 'm_g_attn', 'm_w_out', 'm_g_post', 'v_w_ada', 'v_b_ada', 'v_g_pre', 'v_w_in', 'v_conv_w', 'v_conv_b', 'v_g_conv', 'v_g_attn', 'v_w_out', 'v_g_post']
TWIN_OUTPUTS = ['loss', 'grad_x', 'grad_w_ada', 'grad_b_ada', 'grad_g_pre', 'grad_w_in', 'grad_conv_w', 'grad_conv_b', 'grad_g_conv', 'grad_g_attn', 'grad_w_out', 'grad_g_post', 'delta_w_ada', 'delta_b_ada', 'delta_g_pre', 'delta_w_in', 'delta_conv_w', 'delta_conv_b', 'delta_g_conv', 'delta_g_attn', 'delta_w_out', 'delta_g_post', 'new_m_w_ada', 'new_m_b_ada', 'new_m_g_pre', 'new_m_w_in', 'new_m_conv_w', 'new_m_conv_b', 'new_m_g_conv', 'new_m_g_attn', 'new_m_w_out', 'new_m_g_post', 'new_v_w_ada', 'new_v_b_ada', 'new_v_g_pre', 'new_v_w_in', 'new_v_conv_w', 'new_v_conv_b', 'new_v_g_conv', 'new_v_g_attn', 'new_v_w_out', 'new_v_g_post']
TWIN_LEAF_KINDS = {'loss': 'loss', 'grad_x': 'grad_x', 'grad_w_ada': 'grad_w', 'grad_b_ada': 'grad_w', 'grad_g_pre': 'grad_w', 'grad_w_in': 'grad_w', 'grad_conv_w': 'grad_w', 'grad_conv_b': 'grad_w', 'grad_g_conv': 'grad_w', 'grad_g_attn': 'grad_w', 'grad_w_out': 'grad_w', 'grad_g_post': 'grad_w', 'delta_w_ada': 'delta_w', 'delta_b_ada': 'delta_w', 'delta_g_pre': 'delta_w', 'delta_w_in': 'delta_w', 'delta_conv_w': 'delta_w', 'delta_conv_b': 'delta_w', 'delta_g_conv': 'delta_w', 'delta_g_attn': 'delta_w', 'delta_w_out': 'delta_w', 'delta_g_post': 'delta_w', 'new_m_w_ada': 'new_m', 'new_m_b_ada': 'new_m', 'new_m_g_pre': 'new_m', 'new_m_w_in': 'new_m', 'new_m_conv_w': 'new_m', 'new_m_conv_b': 'new_m', 'new_m_g_conv': 'new_m', 'new_m_g_attn': 'new_m', 'new_m_w_out': 'new_m', 'new_m_g_post': 'new_m', 'new_v_w_ada': 'new_v', 'new_v_b_ada': 'new_v', 'new_v_g_pre': 'new_v', 'new_v_w_in': 'new_v', 'new_v_conv_w': 'new_v', 'new_v_conv_b': 'new_v', 'new_v_g_conv': 'new_v', 'new_v_g_attn': 'new_v', 'new_v_w_out': 'new_v', 'new_v_g_post': 'new_v'}


def _forward(args):
    return _fwd_reference(*[args[k] for k in FWD_PARAMS])


def _output_shape():
    out = _jax.eval_shape(lambda: _forward(_fwd_setup_inputs(0)))
    return out.shape, out.dtype

N_MICROBATCH = 1
ADAM_LR = 0.001
ADAM_B1 = 0.9
ADAM_B2 = 0.999
ADAM_EPS = 1e-08
ADAM_WD = 0.01
ADAM_STEP = 10
PER_EXAMPLE_BATCH_AXIS = {'x': 0, 'c': 0, 'loss_target': 0}
SHARED_INPUTS = []
_WEIGHT_DTYPES = {'w_ada': _jnp.float32, 'b_ada': _jnp.float32, 'g_pre': _jnp.float32, 'w_in': _jnp.float32, 'conv_w': _jnp.float32, 'conv_b': _jnp.float32, 'g_conv': _jnp.float32, 'g_attn': _jnp.float32, 'w_out': _jnp.float32, 'g_post': _jnp.float32}
MOMENT_SCALE = {'w_ada': 8.684978e-01, 'b_ada': 1.636497e+00, 'g_pre': 1.056158e-01, 'w_in': 1.128592e-01, 'conv_w': 7.510561e-02, 'conv_b': 8.058122e-02, 'g_conv': 1.012003e-01, 'g_attn': 2.793621e-01, 'w_out': 1.918789e-01, 'g_post': 3.536228e+00}


def _to_microbatches(a, axis):
    t = _jnp.moveaxis(a, axis, 0)
    t = t.reshape((N_MICROBATCH, t.shape[0] // N_MICROBATCH) + t.shape[1:])
    return _jnp.moveaxis(t, 1, axis + 1)


def setup_inputs(seed: int = 0) -> dict:
    inp = _fwd_setup_inputs(seed)
    key = _jax.random.fold_in(_jax.random.key(seed), 7919)
    shape, _ = _output_shape()
    out = dict(inp)
    out["loss_target"] = _jax.random.normal(_jax.random.fold_in(key, 0), shape, _jnp.float32)
    for i, name in enumerate(TWIN_WEIGHTS):
        w = inp[name].astype(_jnp.float32)
        if MOMENT_SCALE is None:
            s = _jnp.sqrt(_jnp.mean(_jnp.square(w)) + 1e-30)
        else:
            s = MOMENT_SCALE[name]
        km, kv = _jax.random.split(_jax.random.fold_in(key, i + 1))
        out[name] = w
        out["m_" + name] = s * _jax.random.normal(km, w.shape, _jnp.float32)
        out["v_" + name] = (s * s) * _jax.random.uniform(kv, w.shape, _jnp.float32, 0.5, 1.5)
    if N_MICROBATCH > 1:
        for name, axis in PER_EXAMPLE_BATCH_AXIS.items():
            out[name] = _to_microbatches(out[name], axis)
    return {'x': out['x'], 'c': out['c'], 'w_ada': out['w_ada'], 'b_ada': out['b_ada'], 'g_pre': out['g_pre'], 'w_in': out['w_in'], 'conv_w': out['conv_w'], 'conv_b': out['conv_b'], 'g_conv': out['g_conv'], 'g_attn': out['g_attn'], 'w_out': out['w_out'], 'g_post': out['g_post'], 'loss_target': out['loss_target'], 'm_w_ada': out['m_w_ada'], 'm_b_ada': out['m_b_ada'], 'm_g_pre': out['m_g_pre'], 'm_w_in': out['m_w_in'], 'm_conv_w': out['m_conv_w'], 'm_conv_b': out['m_conv_b'], 'm_g_conv': out['m_g_conv'], 'm_g_attn': out['m_g_attn'], 'm_w_out': out['m_w_out'], 'm_g_post': out['m_g_post'], 'v_w_ada': out['v_w_ada'], 'v_b_ada': out['v_b_ada'], 'v_g_pre': out['v_g_pre'], 'v_w_in': out['v_w_in'], 'v_conv_w': out['v_conv_w'], 'v_conv_b': out['v_conv_b'], 'v_g_conv': out['v_g_conv'], 'v_g_attn': out['v_g_attn'], 'v_w_out': out['v_w_out'], 'v_g_post': out['v_g_post']}


def _loss(weights, diff, rest, loss_target):
    with _jax.named_scope("forward"):
        args = {**rest, TWIN_DIFF_INPUT: diff, **{k: w.astype(_WEIGHT_DTYPES[k]) for k, w in weights.items()}}
        y = _forward(args)
    with _jax.named_scope("loss_head"):
        err = _jnp.square(y.astype(_jnp.float32) - loss_target)
        return 0.5 * _jnp.sum(_jnp.mean(err, axis=-1)) if err.ndim else 0.5 * err


def _adamw(w, g, m, v):
    m = ADAM_B1 * m + (1.0 - ADAM_B1) * g
    v = ADAM_B2 * v + (1.0 - ADAM_B2) * _jnp.square(g)
    m_hat = m / (1.0 - ADAM_B1 ** ADAM_STEP)
    v_hat = v / (1.0 - ADAM_B2 ** ADAM_STEP)
    delta = -ADAM_LR * (m_hat / (_jnp.sqrt(v_hat) + ADAM_EPS) + ADAM_WD * w)
    return delta, m, v


def reference(x, c, w_ada, b_ada, g_pre, w_in, conv_w, conv_b, g_conv, g_attn, w_out, g_post, loss_target, m_w_ada, m_b_ada, m_g_pre, m_w_in, m_conv_w, m_conv_b, m_g_conv, m_g_attn, m_w_out, m_g_post, v_w_ada, v_b_ada, v_g_pre, v_w_in, v_conv_w, v_conv_b, v_g_conv, v_g_attn, v_w_out, v_g_post):
    given = dict(x=x, c=c, w_ada=w_ada, b_ada=b_ada, g_pre=g_pre, w_in=w_in, conv_w=conv_w, conv_b=conv_b, g_conv=g_conv, g_attn=g_attn, w_out=w_out, g_post=g_post, loss_target=loss_target, m_w_ada=m_w_ada, m_b_ada=m_b_ada, m_g_pre=m_g_pre, m_w_in=m_w_in, m_conv_w=m_conv_w, m_conv_b=m_conv_b, m_g_conv=m_g_conv, m_g_attn=m_g_attn, m_w_out=m_w_out, m_g_post=m_g_post, v_w_ada=v_w_ada, v_b_ada=v_b_ada, v_g_pre=v_g_pre, v_w_in=v_w_in, v_conv_w=v_conv_w, v_conv_b=v_conv_b, v_g_conv=v_g_conv, v_g_attn=v_g_attn, v_w_out=v_w_out, v_g_post=v_g_post)
    weights = {n: given[n] for n in TWIN_WEIGHTS}
    shared = {n: given[n] for n in SHARED_INPUTS}
    per_example = {n: given[n] for n in ['x', 'c']}
    grad_fn = _jax.value_and_grad(_loss, argnums=(0, 1))

    def one_microbatch(ex, loss_target):
        ex = dict(ex)
        diff = ex.pop(TWIN_DIFF_INPUT)
        return grad_fn(weights, diff, {**shared, **ex}, loss_target)

    if N_MICROBATCH == 1:
        loss, (grad_w, grad_x) = one_microbatch(per_example, given["loss_target"])
    else:
        def body(carry, xs):
            loss_sum, grad_sum = carry
            l_k, (gw_k, gx_k) = one_microbatch(xs[0], xs[1])
            with _jax.named_scope("update"):
                return (loss_sum + l_k, _jax.tree.map(_jnp.add, grad_sum, gw_k)), gx_k

        init = (_jnp.zeros((), _jnp.float32), _jax.tree.map(_jnp.zeros_like, weights))
        (loss, grad_w), grad_x = _jax.lax.scan(body, init, (per_example, given["loss_target"]))
    with _jax.named_scope("update"):
        delta_w, new_m, new_v = {}, {}, {}
        for n in TWIN_WEIGHTS:
            delta_w[n], new_m[n], new_v[n] = _adamw(weights[n], grad_w[n], given["m_" + n], given["v_" + n])
    return (loss, grad_x, *[grad_w[n] for n in TWIN_WEIGHTS], *[delta_w[n] for n in TWIN_WEIGHTS],
            *[new_m[n] for n in TWIN_WEIGHTS], *[new_v[n] for n in TWIN_WEIGHTS])
```

```python
import functools

import jax
import jax.numpy as jnp
from jax import lax
from jax.experimental import pallas as pl
from jax.experimental.pallas import tpu as pltpu

F32 = jnp.float32
BF16 = jnp.bfloat16
MESH = pl.DeviceIdType.MESH
HBM = pl.BlockSpec(memory_space=pltpu.HBM)
VMEM = pl.BlockSpec(memory_space=pltpu.VMEM)

HEAD_DIM = 64
PAIR = 2 * HEAD_DIM
BRANCHES = ((128, 1), (512, 4), (2048, 16))
SIDE = 64
EPS = 1e-6
NEG_INF = -1e30
N_CHIPS = 4
N_DEV = 8

ADAM_LR = 0.001
ADAM_B1 = 0.9
ADAM_B2 = 0.999
ADAM_EPS = 1e-08
ADAM_WD = 0.01
ADAM_STEP = 10

VMEM_LIMIT_BYTES = 56 * 1024 * 1024
ROW_TILE = 256
COL_TILE = 512
CONV_TILE = 256
ATT_BQ = 128
ATT_KW = 256
SMALL_ALIGN = 1024


def _params(semantics=None, vmem=True):
    kw = {}
    if semantics is not None:
        kw["dimension_semantics"] = semantics
    if vmem:
        kw["vmem_limit_bytes"] = VMEM_LIMIT_BYTES
    return pltpu.CompilerParams(**kw)


def _silu(z):
    return z * jax.nn.sigmoid(z)


def _silu_grad(z):
    s = jax.nn.sigmoid(z)
    return s * (1.0 + z * (1.0 - s))


def _my_place():
    return lax.axis_index("x"), lax.axis_index("y"), lax.axis_index("c")


def _flip(a, bit):
    return 1 - a if bit else a


def _allgather8(v, name):
    rows_per, n = v.shape

    def body(v_ref, out_ref, send_sems, recv_sems):
        x, y, c = _my_place()
        me = 4 * x + 2 * y + c

        def rows(idx):
            return out_ref.at[pl.ds(pl.multiple_of(idx * rows_per, rows_per), rows_per), :]

        out_ref[pl.ds(pl.multiple_of(me * rows_per, rows_per), rows_per), :] = v_ref[...]
        copies = []
        for k in range(1, N_DEV):
            peer = (_flip(x, k & 4), _flip(y, k & 2), _flip(c, k & 1))
            cp = pltpu.make_async_remote_copy(
                src_ref=v_ref, dst_ref=rows(me), send_sem=send_sems.at[k - 1], recv_sem=recv_sems.at[k - 1],
                device_id=peer, device_id_type=MESH)
            cp.start()
            copies.append((cp, peer))
        for k, (cp, peer) in enumerate(copies):
            src = 4 * peer[0] + 2 * peer[1] + peer[2]
            pltpu.make_async_remote_copy(
                src_ref=v_ref, dst_ref=rows(src), send_sem=send_sems.at[k], recv_sem=recv_sems.at[k],
                device_id=peer, device_id_type=MESH).wait_recv()
        for cp, _ in copies:
            cp.wait_send()

    return pl.pallas_call(
        body, name=name,
        out_shape=jax.ShapeDtypeStruct((N_DEV * rows_per, n), v.dtype),
        in_specs=[VMEM], out_specs=VMEM,
        scratch_shapes=[pltpu.SemaphoreType.DMA((N_DEV - 1,)), pltpu.SemaphoreType.DMA((N_DEV - 1,))],
    )(v)


def _chip_of(x, y):
    return 2 * x + y


def _gather_weights(win_bf, wout_bf):
    shards = (win_bf, wout_bf)
    halves = tuple(s.shape[0] // 2 for s in shards)

    def body(win_ref, wout_ref, winf_ref, woutf_ref, local_sems, send_sems, recv_sems, fwd_send, fwd_recv):
        x, y, c = _my_place()
        me = _chip_of(x, y)
        srcs = (win_ref, wout_ref)
        dsts = (winf_ref, woutf_ref)

        def half(ref, chip, t, which):
            return ref.at[chip, pl.ds(pl.multiple_of(which * halves[t], halves[t]), halves[t]), :]

        local = [pltpu.make_async_copy(srcs[t], dsts[t].at[me], local_sems.at[t]) for t in range(2)]
        for cp in local:
            cp.start()
        started = []
        for k in (1, 2, 3):
            peer = (_flip(x, k & 2), _flip(y, k & 1), c)
            for t in range(2):
                idx = 2 * (k - 1) + t
                cp = pltpu.make_async_remote_copy(
                    src_ref=srcs[t].at[pl.ds(pl.multiple_of(c * halves[t], halves[t]), halves[t]), :],
                    dst_ref=half(dsts[t], me, t, c),
                    send_sem=send_sems.at[idx], recv_sem=recv_sems.at[idx], device_id=peer, device_id_type=MESH)
                cp.start()
                started.append(cp)
        sibling = (x, y, 1 - c)
        for k in (1, 2, 3):
            peer = (_flip(x, k & 2), _flip(y, k & 1), c)
            src_chip = _chip_of(peer[0], peer[1])
            for t in range(2):
                idx = 2 * (k - 1) + t
                landed = half(dsts[t], src_chip, t, c)
                pltpu.make_async_remote_copy(
                    src_ref=landed, dst_ref=landed, send_sem=send_sems.at[idx], recv_sem=recv_sems.at[idx],
                    device_id=peer, device_id_type=MESH).wait_recv()
                fw = pltpu.make_async_remote_copy(
                    src_ref=landed, dst_ref=landed, send_sem=fwd_send.at[idx], recv_sem=fwd_recv.at[idx],
                    device_id=sibling, device_id_type=MESH)
                fw.start()
                started.append(fw)
        for k in (1, 2, 3):
            src_chip = _chip_of(_flip(x, k & 2), _flip(y, k & 1))
            for t in range(2):
                idx = 2 * (k - 1) + t
                other = half(dsts[t], src_chip, t, 1 - c)
                pltpu.make_async_remote_copy(
                    src_ref=other, dst_ref=other, send_sem=fwd_send.at[idx], recv_sem=fwd_recv.at[idx],
                    device_id=sibling, device_id_type=MESH).wait_recv()
        for cp in started:
            cp.wait_send()
        for cp in local:
            cp.wait()

    return pl.pallas_call(
        body, name="gather_weights",
        out_shape=tuple(jax.ShapeDtypeStruct((N_CHIPS,) + s.shape, s.dtype) for s in shards),
        in_specs=[HBM, HBM], out_specs=(HBM, HBM),
        scratch_shapes=[pltpu.SemaphoreType.DMA((2,))] + [pltpu.SemaphoreType.DMA((6,))] * 4,
    )(win_bf, wout_bf)


def _swap_halves(gin, gout):
    grads = (gin, gout)
    halves = tuple(g.shape[1] // 2 for g in grads)

    def body(gin_ref, gout_ref, rin_ref, rout_ref, send_sems, recv_sems):
        x, y, c = _my_place()
        sibling = (x, y, 1 - c)
        srcs = (gin_ref, gout_ref)
        dsts = (rin_ref, rout_ref)
        copies = []
        for t in range(2):
            theirs = srcs[t].at[:, pl.ds(pl.multiple_of((1 - c) * halves[t], halves[t]), halves[t]), :]
            cp = pltpu.make_async_remote_copy(
                src_ref=theirs, dst_ref=dsts[t], send_sem=send_sems.at[t], recv_sem=recv_sems.at[t],
                device_id=sibling, device_id_type=MESH)
            cp.start()
            copies.append(cp)
        for cp in copies:
            cp.wait()

    return pl.pallas_call(
        body, name="rs_swap_halves",
        out_shape=tuple(jax.ShapeDtypeStruct((N_CHIPS, g.shape[1] // 2, g.shape[2]), g.dtype) for g in grads),
        in_specs=[HBM, HBM], out_specs=(HBM, HBM),
        scratch_shapes=[pltpu.SemaphoreType.DMA((2,)), pltpu.SemaphoreType.DMA((2,))],
    )(gin, gout)


def _send_to_owners(cin, cout):
    sums = (cin, cout)

    def body(cin_ref, cout_ref, rin_ref, rout_ref, send_sems, recv_sems):
        x, y, c = _my_place()
        srcs = (cin_ref, cout_ref)
        dsts = (rin_ref, rout_ref)
        copies = []
        for k in (1, 2, 3):
            peer = (_flip(x, k & 2), _flip(y, k & 1), c)
            owner = _chip_of(peer[0], peer[1])
            for t in range(2):
                idx = 2 * (k - 1) + t
                cp = pltpu.make_async_remote_copy(
                    src_ref=srcs[t].at[owner], dst_ref=dsts[t].at[k - 1],
                    send_sem=send_sems.at[idx], recv_sem=recv_sems.at[idx], device_id=peer, device_id_type=MESH)
                cp.start()
                copies.append(cp)
        for cp in copies:
            cp.wait()

    return pl.pallas_call(
        body, name="rs_send_to_owners",
        out_shape=tuple(jax.ShapeDtypeStruct((N_CHIPS - 1,) + s.shape[1:], s.dtype) for s in sums),
        in_specs=[HBM, HBM], out_specs=(HBM, HBM),
        scratch_shapes=[pltpu.SemaphoreType.DMA((6,)), pltpu.SemaphoreType.DMA((6,))],
    )(cin, cout)


def _join_halves(hin, hout):
    parts = (hin, hout)

    def body(hin_ref, hout_ref, fin_ref, fout_ref, local_sems, send_sems, recv_sems):
        x, y, c = _my_place()
        sibling = (x, y, 1 - c)
        srcs = (hin_ref, hout_ref)
        dsts = (fin_ref, fout_ref)
        copies = []
        for t in range(2):
            rows = parts[t].shape[0]
            mine = dsts[t].at[pl.ds(pl.multiple_of(c * rows, rows), rows), :]
            lc = pltpu.make_async_copy(srcs[t], mine, local_sems.at[t])
            lc.start()
            cp = pltpu.make_async_remote_copy(
                src_ref=srcs[t], dst_ref=mine, send_sem=send_sems.at[t], recv_sem=recv_sems.at[t],
                device_id=sibling, device_id_type=MESH)
            cp.start()
            copies.append((lc, cp, rows))
        for t, (lc, cp, rows) in enumerate(copies):
            theirs = dsts[t].at[pl.ds(pl.multiple_of((1 - c) * rows, rows), rows), :]
            pltpu.make_async_remote_copy(
                src_ref=srcs[t], dst_ref=theirs, send_sem=send_sems.at[t], recv_sem=recv_sems.at[t],
                device_id=sibling, device_id_type=MESH).wait_recv()
            cp.wait_send()
            lc.wait()

    return pl.pallas_call(
        body, name="rs_join_halves",
        out_shape=tuple(jax.ShapeDtypeStruct((2 * p.shape[0], p.shape[1]), p.dtype) for p in parts),
        in_specs=[HBM, HBM], out_specs=(HBM, HBM),
        scratch_shapes=[pltpu.SemaphoreType.DMA((2,))] * 3,
    )(hin, hout)


def _cast_bf16(w, name):
    rows, cols = w.shape
    tr = min(rows, ROW_TILE)

    def body(w_ref, o_ref):
        o_ref[...] = w_ref[...].astype(BF16)

    return pl.pallas_call(
        body, name=name, grid=(rows // tr,),
        out_shape=jax.ShapeDtypeStruct(w.shape, BF16),
        in_specs=[pl.BlockSpec((tr, cols), lambda i: (i, 0))],
        out_specs=pl.BlockSpec((tr, cols), lambda i: (i, 0)),
        compiler_params=_params(("parallel",)),
    )(w)


def _ada_partial(c_all, w_ada):
    d_model, wa = w_ada.shape
    tn = 512 if wa % 512 == 0 else 256

    def body(c_ref, w_ref, o_ref):
        o_ref[...] = jnp.dot(_silu(c_ref[...]), w_ref[...], precision=lax.Precision.HIGHEST,
                             preferred_element_type=F32)

    return pl.pallas_call(
        body, name="ada_partial", grid=(wa // tn,),
        out_shape=jax.ShapeDtypeStruct((N_DEV, wa), F32),
        in_specs=[pl.BlockSpec((N_DEV, d_model), lambda i: (0, 0)), pl.BlockSpec((d_model, tn), lambda i: (0, i))],
        out_specs=pl.BlockSpec((N_DEV, tn), lambda i: (0, i)),
        compiler_params=_params(("parallel",)),
    )(c_all, w_ada)


def _prenorm(x, mod, g_pre):
    t, d = x.shape
    tb = ROW_TILE

    def body(x_ref, mod_ref, g_ref, h_ref, ht_ref):
        xv = x_ref[...]
        r = lax.rsqrt(jnp.mean(xv * xv, axis=-1, keepdims=True) + EPS)
        h = (xv * r) * g_ref[...] * (1.0 + mod_ref[1:2, :]) + mod_ref[0:1, :]
        h_ref[...] = h.astype(BF16)
        ht_ref[...] = h.T.astype(BF16)

    return pl.pallas_call(
        body, name="prenorm", grid=(t // tb,),
        out_shape=(jax.ShapeDtypeStruct((t, d), BF16), jax.ShapeDtypeStruct((d, t), BF16)),
        in_specs=[pl.BlockSpec((tb, d), lambda i: (i, 0)), pl.BlockSpec((3, d), lambda i: (0, 0)),
                  pl.BlockSpec((1, d), lambda i: (0, 0))],
        out_specs=(pl.BlockSpec((tb, d), lambda i: (i, 0)), pl.BlockSpec((d, tb), lambda i: (0, i))),
        compiler_params=_params(("parallel",)),
    )(x, mod, g_pre)


def _pairs_to_cols(ref, n_pairs):
    return jnp.concatenate([ref[i] for i in range(n_pairs)], axis=1)


def _proj_conv(h, winf, dc):
    t, d = h.shape
    ws = winf.shape[2]
    tn = COL_TILE
    nt = ws // tn

    def body(a_ref, b_ref, o_ref):
        o_ref[...] = jnp.dot(a_ref[...], b_ref[...], preferred_element_type=F32).astype(BF16)

    return pl.pallas_call(
        body, name="proj_conv", grid=(2, nt),
        out_shape=jax.ShapeDtypeStruct((t, 2 * ws), BF16),
        in_specs=[pl.BlockSpec((t, d), lambda j, n: (0, 0)), pl.BlockSpec((None, d, tn), lambda j, n: (j, 0, n))],
        out_specs=pl.BlockSpec((t, tn), lambda j, n: (0, j * nt + n)),
        compiler_params=_params(("parallel", "parallel")),
    )(h, winf)


def _proj_attn(h, winf, da):
    t, d = h.shape
    ws = winf.shape[2]
    tn = COL_TILE
    nt = ws // tn
    per_comp = da // tn
    pairs = tn // PAIR

    def body(a_ref, b_ref, o_ref):
        res = jnp.dot(a_ref[...], b_ref[...], preferred_element_type=F32).astype(BF16)
        for i in range(pairs):
            o_ref[i] = res[:, i * PAIR:(i + 1) * PAIR]

    return pl.pallas_call(
        body, name="proj_attn", grid=(2, nt),
        out_shape=jax.ShapeDtypeStruct((4, da // PAIR, t, PAIR), BF16),
        in_specs=[pl.BlockSpec((t, d), lambda j, n: (0, 0)), pl.BlockSpec((None, d, tn), lambda j, n: (2 + j, 0, n))],
        out_specs=pl.BlockSpec((None, pairs, t, PAIR), lambda j, n: (2 * j + n // per_comp, n % per_comp, 0, 0)),
        compiler_params=_params(("parallel", "parallel")),
    )(h, winf)


def _shift_rows(a, rows):
    idx = lax.broadcasted_iota(jnp.int32, a.shape, 0)
    prev = jnp.where(idx == 0, 0.0, pltpu.roll(a, 1, 0))
    nxt = jnp.where(idx == rows - 1, 0.0, pltpu.roll(a, rows - 1, 0))
    return prev, nxt


def _conv_fwd(conv_proj, conv_w, conv_b, dc):
    t = conv_proj.shape[0]
    ct = CONV_TILE
    nct = dc // ct

    def body(u_ref, cg_ref, w_ref, b_ref, co_ref):
        a = cg_ref[...].astype(F32) * u_ref[...].astype(F32)
        prev, nxt = _shift_rows(a, t)
        co_ref[...] = w_ref[0:1, :] * prev + w_ref[1:2, :] * a + w_ref[2:3, :] * nxt + b_ref[...]

    return pl.pallas_call(
        body, name="conv_fwd", grid=(nct,),
        out_shape=jax.ShapeDtypeStruct((t, dc), F32),
        in_specs=[pl.BlockSpec((t, ct), lambda i: (0, i)), pl.BlockSpec((t, ct), lambda i: (0, 2 * nct + i)),
                  pl.BlockSpec((3, ct), lambda i: (0, i)), pl.BlockSpec((1, ct), lambda i: (0, i))],
        out_specs=pl.BlockSpec((t, ct), lambda i: (0, i)),
        compiler_params=_params(("parallel",)),
    )(conv_proj, conv_proj, conv_w, conv_b)


def _band_bias(base, shift, slope_row):
    arel = jnp.abs(base + shift)
    return jnp.where(arel <= SIDE, arel.astype(F32) * slope_row, NEG_INF)


def _block_window(qb, seq_len, kw):
    q0 = pl.multiple_of(qb * ATT_BQ, ATT_BQ)
    ks = pl.multiple_of(jnp.clip(q0 - SIDE, 0, seq_len - kw), SIDE)
    return q0, ks


def _attn_views(arr, t):
    lead = arr.shape[:-2]
    return [arr.reshape(lead + (t // r, r * PAIR)) for _, r in BRANCHES]


def _attn_fwd(attn_proj, slopes):
    _, hp, t, _ = attn_proj.shape
    views = _attn_views(attn_proj, t)

    def body(*refs):
        qkv = refs[0:3]
        sl_ref = refs[3]
        outs = refs[4:]
        for b, (_, r) in enumerate(BRANCHES):
            seq_len = t // r
            kw = min(ATT_KW, seq_len)
            v_ref = qkv[b]
            o_ref, l_ref = outs[2 * b], outs[2 * b + 1]
            base = (lax.broadcasted_iota(jnp.int32, (ATT_BQ, kw), 1)
                    - lax.broadcasted_iota(jnp.int32, (ATT_BQ, kw), 0))
            nsl = [-(sl_ref[hh:hh + 1, 0:kw] * float(r)) for hh in range(2)]
            for res in range(r):
                cs = slice(res * PAIR, (res + 1) * PAIR)

                def block(qb, carry, cs=cs, kw=kw, seq_len=seq_len, v_ref=v_ref, o_ref=o_ref, l_ref=l_ref,
                          base=base, nsl=nsl):
                    q0, ks = _block_window(qb, seq_len, kw)
                    q = (v_ref[0, pl.ds(q0, ATT_BQ), cs].astype(F32) * (HEAD_DIM ** -0.5)).astype(BF16)
                    k = v_ref[1, pl.ds(ks, kw), cs]
                    v = v_ref[2, pl.ds(ks, kw), cs]
                    o_parts, l_parts = [], []
                    for hh in range(2):
                        hs = slice(hh * HEAD_DIM, (hh + 1) * HEAD_DIM)
                        s = lax.dot_general(q[:, hs], k[:, hs], (((1,), (1,)), ((), ())), preferred_element_type=F32)
                        s = s + _band_bias(base, ks - q0, nsl[hh])
                        m = jnp.max(s, axis=-1, keepdims=True)
                        p = jnp.exp(s - m)
                        den = jnp.sum(p, axis=-1, keepdims=True)
                        o = jnp.dot(p.astype(BF16), v[:, hs], preferred_element_type=F32) * (1.0 / den)
                        o_parts.append(o)
                        l_parts.append(jnp.broadcast_to(m + jnp.log(den), (ATT_BQ, HEAD_DIM)))
                    o_ref[pl.ds(q0, ATT_BQ), cs] = jnp.concatenate(o_parts, axis=1)
                    l_ref[pl.ds(q0, ATT_BQ), cs] = jnp.concatenate(l_parts, axis=1)
                    return carry

                lax.fori_loop(0, seq_len // ATT_BQ, block, 0)

    in_specs = []
    for b, (_, r) in enumerate(BRANCHES):
        in_specs.append(pl.BlockSpec((3, None, t // r, r * PAIR), lambda h: (0, h, 0, 0)))
    in_specs.append(pl.BlockSpec((None, 8, ATT_KW), lambda h: (h, 0, 0)))
    out_shape, out_specs = [], []
    for _, r in BRANCHES:
        for _ in range(2):
            out_shape.append(jax.ShapeDtypeStruct((hp, t // r, r * PAIR), F32))
            out_specs.append(pl.BlockSpec((None, t // r, r * PAIR), lambda h: (h, 0, 0)))
    outs = pl.pallas_call(
        body, name="attn_fwd", grid=(hp,),
        out_shape=tuple(out_shape), in_specs=in_specs, out_specs=tuple(out_specs),
        compiler_params=_params(("parallel",)),
    )(*views, slopes)
    return [o.reshape(hp, t, PAIR) for o in outs]


def _attn_bwd(attn_proj, o_mix, d_o, lse, slopes):
    _, hp, t, _ = attn_proj.shape
    qkv_views = _attn_views(attn_proj, t)
    o_views = _attn_views(o_mix, t)
    do_views = _attn_views(d_o, t)
    lse_views = _attn_views(lse, t)

    def body(*refs):
        sl_ref = refs[12]
        outs = refs[13:]
        for b, (_, r) in enumerate(BRANCHES):
            seq_len = t // r
            kw = min(ATT_KW, seq_len)
            v_ref, o_ref, do_ref, lse_ref = refs[b], refs[3 + b], refs[6 + b], refs[9 + b]
            dq_ref, dk_ref, dv_ref = outs[3 * b], outs[3 * b + 1], outs[3 * b + 2]
            dk_ref[...] = jnp.zeros(dk_ref.shape, F32)
            dv_ref[...] = jnp.zeros(dv_ref.shape, F32)
            base = (lax.broadcasted_iota(jnp.int32, (ATT_BQ, kw), 1)
                    - lax.broadcasted_iota(jnp.int32, (ATT_BQ, kw), 0))
            nsl = [-(sl_ref[hh:hh + 1, 0:kw] * float(r)) for hh in range(2)]
            for res in range(r):
                cs = slice(res * PAIR, (res + 1) * PAIR)

                def block(qb, carry, cs=cs, kw=kw, seq_len=seq_len, v_ref=v_ref, o_ref=o_ref, do_ref=do_ref,
                          lse_ref=lse_ref, dq_ref=dq_ref, dk_ref=dk_ref, dv_ref=dv_ref, base=base, nsl=nsl):
                    q0, ks = _block_window(qb, seq_len, kw)
                    q = (v_ref[0, pl.ds(q0, ATT_BQ), cs].astype(F32) * (HEAD_DIM ** -0.5)).astype(BF16)
                    k = v_ref[1, pl.ds(ks, kw), cs]
                    v = v_ref[2, pl.ds(ks, kw), cs]
                    ov = o_ref[pl.ds(q0, ATT_BQ), cs]
                    dov = do_ref[pl.ds(q0, ATT_BQ), cs]
                    lv = lse_ref[pl.ds(q0, ATT_BQ), cs]
                    dq_parts, dk_parts, dv_parts = [], [], []
                    for hh in range(2):
                        hs = slice(hh * HEAD_DIM, (hh + 1) * HEAD_DIM)
                        s = lax.dot_general(q[:, hs], k[:, hs], (((1,), (1,)), ((), ())), preferred_element_type=F32)
                        s = s + _band_bias(base, ks - q0, nsl[hh])
                        p = jnp.exp(s - lv[:, hh * HEAD_DIM:hh * HEAD_DIM + 1])
                        do_h = dov[:, hs]
                        delta = jnp.sum(do_h.astype(F32) * ov[:, hs], axis=-1, keepdims=True)
                        dp = lax.dot_general(do_h, v[:, hs], (((1,), (1,)), ((), ())), preferred_element_type=F32)
                        ds = (p * (dp - delta)).astype(BF16)
                        pt = p.T.astype(BF16)
                        dst = ds.astype(F32).T.astype(BF16)
                        dv_parts.append(jnp.dot(pt, do_h, preferred_element_type=F32))
                        dk_parts.append(jnp.dot(dst, q[:, hs], preferred_element_type=F32))
                        dq_parts.append(jnp.dot(ds, k[:, hs], preferred_element_type=F32) * (HEAD_DIM ** -0.5))
                    dq_ref[pl.ds(q0, ATT_BQ), cs] = jnp.concatenate(dq_parts, axis=1)
                    dk_ref[pl.ds(ks, kw), cs] += jnp.concatenate(dk_parts, axis=1)
                    dv_ref[pl.ds(ks, kw), cs] += jnp.concatenate(dv_parts, axis=1)
                    return carry

                lax.fori_loop(0, seq_len // ATT_BQ, block, 0)

    in_specs = []
    for _, r in BRANCHES:
        in_specs.append(pl.BlockSpec((3, None, t // r, r * PAIR), lambda h: (0, h, 0, 0)))
    for _ in range(3):
        for _, r in BRANCHES:
            in_specs.append(pl.BlockSpec((None, t // r, r * PAIR), lambda h: (h, 0, 0)))
    in_specs.append(pl.BlockSpec((None, 8, ATT_KW), lambda h: (h, 0, 0)))
    out_shape, out_specs = [], []
    for _, r in BRANCHES:
        for _ in range(3):
            out_shape.append(jax.ShapeDtypeStruct((hp, t // r, r * PAIR), F32))
            out_specs.append(pl.BlockSpec((None, t // r, r * PAIR), lambda h: (h, 0, 0)))
    outs = pl.pallas_call(
        body, name="attn_bwd", grid=(hp,),
        out_shape=tuple(out_shape), in_specs=in_specs, out_specs=tuple(out_specs),
        compiler_params=_params(("parallel",)),
    )(*qkv_views, *o_views, *do_views, *lse_views, slopes)
    return [o.reshape(hp, t, PAIR) for o in outs]


def _mix_fwd(co, conv_proj, attn_proj, branch_outs, g_conv, g_attn_pairs):
    t, dc = co.shape
    hp = attn_proj.shape[1]
    da = hp * PAIR
    tb = ROW_TILE

    def body(co_ref, bg_ref, zc_ref, za_ref, o1, l1, o2, l2, o3, l3, gc_ref, ga_ref,
             ycat_ref, ycatt_ref, omix_ref, lse_ref):
        p = bg_ref[...].astype(F32) * co_ref[...]
        rc = lax.rsqrt(jnp.mean(p * p, axis=-1, keepdims=True) + EPS)
        yc = (p * rc) * gc_ref[...] * _silu(zc_ref[...].astype(F32))
        ycat_ref[:, 0:dc] = yc.astype(BF16)
        ycatt_ref[0:dc, :] = yc.T.astype(BF16)
        ssq = jnp.zeros((tb, 1), F32)
        for h in range(hp):
            la, lb, lc = l1[h], l2[h], l3[h]
            m = jnp.maximum(jnp.maximum(la, lb), lc)
            wa, wb, wc = jnp.exp(la - m), jnp.exp(lb - m), jnp.exp(lc - m)
            den = wa + wb + wc
            o = (wa * o1[h] + wb * o2[h] + wc * o3[h]) * (1.0 / den)
            omix_ref[h] = o
            lse_ref[h] = m + jnp.log(den)
            ssq = ssq + jnp.sum(o * o, axis=-1, keepdims=True)
        ra = lax.rsqrt(ssq * (1.0 / da) + EPS)
        for h in range(hp):
            ya = (omix_ref[h] * ra) * ga_ref[h] * _silu(za_ref[h].astype(F32))
            ycat_ref[:, dc + h * PAIR:dc + (h + 1) * PAIR] = ya.astype(BF16)
            ycatt_ref[dc + h * PAIR:dc + (h + 1) * PAIR, :] = ya.T.astype(BF16)

    nct = dc // dc
    pair_spec = pl.BlockSpec((hp, tb, PAIR), lambda i: (0, i, 0))
    return pl.pallas_call(
        body, name="mix_fwd", grid=(t // tb,),
        out_shape=(jax.ShapeDtypeStruct((t, dc + da), BF16), jax.ShapeDtypeStruct((dc + da, t), BF16),
                   jax.ShapeDtypeStruct((hp, t, PAIR), F32), jax.ShapeDtypeStruct((hp, t, PAIR), F32)),
        in_specs=[pl.BlockSpec((tb, dc), lambda i: (i, 0)),
                  pl.BlockSpec((tb, dc), lambda i: (i, 1 * nct)),
                  pl.BlockSpec((tb, dc), lambda i: (i, 3 * nct)),
                  pl.BlockSpec((None, hp, tb, PAIR), lambda i: (3, 0, i, 0)),
                  pair_spec, pair_spec, pair_spec, pair_spec, pair_spec, pair_spec,
                  pl.BlockSpec((1, dc), lambda i: (0, 0)),
                  pl.BlockSpec((hp, 1, PAIR), lambda i: (0, 0, 0))],
        out_specs=(pl.BlockSpec((tb, dc + da), lambda i: (i, 0)), pl.BlockSpec((dc + da, tb), lambda i: (0, i)),
                   pair_spec, pair_spec),
        compiler_params=_params(("parallel",)),
    )(co, conv_proj, conv_proj, attn_proj, *branch_outs, g_conv, g_attn_pairs)


def _out_fwd_bwd(ycat, woutf, x, target, mod, g_post):
    t, d = x.shape
    n = ycat.shape[1]
    tb = ROW_TILE

    def body(a_ref, w_ref, x_ref, tg_ref, mod_ref, g_ref, dout_ref, dy_ref, acc_ref):
        y = jnp.dot(a_ref[...], w_ref[...], preferred_element_type=F32)
        r = lax.rsqrt(jnp.mean(y * y, axis=-1, keepdims=True) + EPS)
        nh = y * r
        gate = mod_ref[2:3, :]
        nrm = nh * g_ref[...]
        err = x_ref[...] + gate * nrm - tg_ref[...]
        dout = err * (1.0 / d)
        dout_ref[...] = dout
        dn = dout * gate
        a = dn * g_ref[...]
        dy = r * (a - nh * jnp.mean(a * nh, axis=-1, keepdims=True))
        dy_ref[...] = dy.astype(BF16)
        loss = 0.5 * jnp.sum(jnp.sum(err * err, axis=-1, keepdims=True) * (1.0 / d), axis=0, keepdims=True)
        part = jnp.concatenate(
            [jnp.sum(dout * nrm, axis=0, keepdims=True), jnp.sum(dn * nh, axis=0, keepdims=True),
             jnp.broadcast_to(loss, (1, d)), jnp.zeros((5, d), F32)], axis=0)

        @pl.when(pl.program_id(0) == 0)
        def _():
            acc_ref[...] = jnp.zeros(acc_ref.shape, F32)

        acc_ref[...] += part

    return pl.pallas_call(
        body, name="out_fwd_bwd", grid=(t // tb,),
        out_shape=(jax.ShapeDtypeStruct((t, d), F32), jax.ShapeDtypeStruct((t, d), BF16),
                   jax.ShapeDtypeStruct((8, d), F32)),
        in_specs=[pl.BlockSpec((tb, n), lambda i: (i, 0)), pl.BlockSpec((n, d), lambda i: (0, 0)),
                  pl.BlockSpec((tb, d), lambda i: (i, 0)), pl.BlockSpec((tb, d), lambda i: (i, 0)),
                  pl.BlockSpec((3, d), lambda i: (0, 0)), pl.BlockSpec((1, d), lambda i: (0, 0))],
        out_specs=(pl.BlockSpec((tb, d), lambda i: (i, 0)), pl.BlockSpec((tb, d), lambda i: (i, 0)),
                   pl.BlockSpec((8, d), lambda i: (0, 0))),
        compiler_params=_params(("arbitrary",)),
    )(ycat, woutf, x, target, mod, g_post)


def _matmul_nt(a, b, out_dtype, name):
    m, k = a.shape
    n = b.shape[0]
    tn = COL_TILE

    def body(a_ref, b_ref, o_ref):
        o_ref[...] = lax.dot_general(a_ref[...], b_ref[...], (((1,), (1,)), ((), ())),
                                     preferred_element_type=F32).astype(out_dtype)

    return pl.pallas_call(
        body, name=name, grid=(n // tn,),
        out_shape=jax.ShapeDtypeStruct((m, n), out_dtype),
        in_specs=[pl.BlockSpec((m, k), lambda i: (0, 0)), pl.BlockSpec((tn, k), lambda i: (i, 0))],
        out_specs=pl.BlockSpec((m, tn), lambda i: (0, i)),
        compiler_params=_params(("parallel",)),
    )(a, b)


def _matmul_nn(a, b, out_dtype, name):
    m, k = a.shape
    n = b.shape[1]
    tn = COL_TILE

    def body(a_ref, b_ref, o_ref):
        o_ref[...] = jnp.dot(a_ref[...], b_ref[...], preferred_element_type=F32).astype(out_dtype)

    return pl.pallas_call(
        body, name=name, grid=(n // tn,),
        out_shape=jax.ShapeDtypeStruct((m, n), out_dtype),
        in_specs=[pl.BlockSpec((m, k), lambda i: (0, 0)), pl.BlockSpec((k, tn), lambda i: (0, i))],
        out_specs=pl.BlockSpec((m, tn), lambda i: (0, i)),
        compiler_params=_params(("parallel",)),
    )(a, b)


def _mix_bwd(dycat, co, conv_proj, attn_proj, o_mix, g_conv, g_attn_pairs):
    t, dc = co.shape
    hp = attn_proj.shape[1]
    da = hp * PAIR
    tb = ROW_TILE

    def body(dy_ref, co_ref, bg_ref, zc_ref, za_ref, om_ref, gc_ref, ga_ref,
             dcp_ref, dco_ref, dza_ref, do_ref, dgc_ref, dga_ref):
        first = pl.program_id(0) == 0
        cov = co_ref[...]
        bg = bg_ref[...].astype(F32)
        zc = zc_ref[...].astype(F32)
        p = bg * cov
        rc = lax.rsqrt(jnp.mean(p * p, axis=-1, keepdims=True) + EPS)
        nh = p * rc
        dyc = dy_ref[:, 0:dc].astype(F32)
        dn = dyc * _silu(zc)
        a = dn * gc_ref[...]
        dp = rc * (a - nh * jnp.mean(a * nh, axis=-1, keepdims=True))
        dcp_ref[:, 0:dc] = jnp.zeros((tb, dc), BF16)
        dcp_ref[:, dc:2 * dc] = (dp * cov).astype(BF16)
        dcp_ref[:, 2 * dc:3 * dc] = jnp.zeros((tb, dc), BF16)
        dcp_ref[:, 3 * dc:4 * dc] = (dyc * nh * gc_ref[...] * _silu_grad(zc)).astype(BF16)
        dco_ref[...] = dp * bg

        @pl.when(first)
        def _():
            dgc_ref[...] = jnp.zeros(dgc_ref.shape, F32)
            dga_ref[...] = jnp.zeros(dga_ref.shape, F32)

        dgc_ref[...] += jnp.sum(dn * nh, axis=0, keepdims=True)

        ssq = jnp.zeros((tb, 1), F32)
        for h in range(hp):
            o = om_ref[h]
            ssq = ssq + jnp.sum(o * o, axis=-1, keepdims=True)
        ra = lax.rsqrt(ssq * (1.0 / da) + EPS)
        dot_an = jnp.zeros((tb, 1), F32)
        for h in range(hp):
            nha = om_ref[h] * ra
            za = za_ref[h].astype(F32)
            dya = dy_ref[:, dc + h * PAIR:dc + (h + 1) * PAIR].astype(F32)
            dna = dya * _silu(za)
            dza_ref[h] = (dya * nha * ga_ref[h] * _silu_grad(za)).astype(BF16)
            dga_ref[h] += jnp.sum(dna * nha, axis=0, keepdims=True)
            dot_an = dot_an + jnp.sum(dna * ga_ref[h] * nha, axis=-1, keepdims=True)
        mean_an = dot_an * (1.0 / da)
        for h in range(hp):
            nha = om_ref[h] * ra
            za = za_ref[h].astype(F32)
            dya = dy_ref[:, dc + h * PAIR:dc + (h + 1) * PAIR].astype(F32)
            aa = dya * _silu(za) * ga_ref[h]
            do_ref[h] = (ra * (aa - nha * mean_an)).astype(BF16)

    pair_spec = pl.BlockSpec((hp, tb, PAIR), lambda i: (0, i, 0))
    return pl.pallas_call(
        body, name="mix_bwd", grid=(t // tb,),
        out_shape=(jax.ShapeDtypeStruct((t, 4 * dc), BF16), jax.ShapeDtypeStruct((t, dc), F32),
                   jax.ShapeDtypeStruct((hp, t, PAIR), BF16), jax.ShapeDtypeStruct((hp, t, PAIR), BF16),
                   jax.ShapeDtypeStruct((1, dc), F32), jax.ShapeDtypeStruct((hp, 1, PAIR), F32)),
        in_specs=[pl.BlockSpec((tb, dc + da), lambda i: (i, 0)),
                  pl.BlockSpec((tb, dc), lambda i: (i, 0)),
                  pl.BlockSpec((tb, dc), lambda i: (i, 1)),
                  pl.BlockSpec((tb, dc), lambda i: (i, 3)),
                  pl.BlockSpec((None, hp, tb, PAIR), lambda i: (3, 0, i, 0)),
                  pair_spec,
                  pl.BlockSpec((1, dc), lambda i: (0, 0)),
                  pl.BlockSpec((hp, 1, PAIR), lambda i: (0, 0, 0))],
        out_specs=(pl.BlockSpec((tb, 4 * dc), lambda i: (i, 0)), pl.BlockSpec((tb, dc), lambda i: (i, 0)),
                   pair_spec, pair_spec,
                   pl.BlockSpec((1, dc), lambda i: (0, 0)), pl.BlockSpec((hp, 1, PAIR), lambda i: (0, 0, 0))),
        compiler_params=_params(("arbitrary",)),
    )(dycat, co, conv_proj, conv_proj, attn_proj, o_mix, g_conv, g_attn_pairs)


def _conv_bwd(dconv_proj, dco, conv_proj, conv_w, dc):
    t = dco.shape[0]
    ct = CONV_TILE
    nct = dc // ct

    def body(dcp_in_ref, dco_ref, u_ref, cg_ref, w_ref, dcp_ref, acc_ref):
        del dcp_in_ref
        which = pl.program_id(1)
        g = dco_ref[...]
        u = u_ref[...].astype(F32)
        cg = cg_ref[...].astype(F32)
        g_prev, g_next = _shift_rows(g, t)
        da = w_ref[0:1, :] * g_next + w_ref[1:2, :] * g + w_ref[2:3, :] * g_prev
        dcp_ref[...] = (da * jnp.where(which == 0, cg, u)).astype(BF16)
        a = cg * u
        a_prev, a_next = _shift_rows(a, t)
        acc_ref[...] = jnp.concatenate(
            [jnp.sum(g * a_prev, axis=0, keepdims=True), jnp.sum(g * a, axis=0, keepdims=True),
             jnp.sum(g * a_next, axis=0, keepdims=True), jnp.sum(g, axis=0, keepdims=True),
             jnp.zeros((4, ct), F32)], axis=0)

    return pl.pallas_call(
        body, name="conv_bwd", grid=(nct, 2),
        out_shape=(jax.ShapeDtypeStruct(dconv_proj.shape, BF16), jax.ShapeDtypeStruct((8, dc), F32)),
        in_specs=[HBM,
                  pl.BlockSpec((t, ct), lambda i, s: (0, i)),
                  pl.BlockSpec((t, ct), lambda i, s: (0, i)),
                  pl.BlockSpec((t, ct), lambda i, s: (0, 2 * nct + i)),
                  pl.BlockSpec((3, ct), lambda i, s: (0, i))],
        out_specs=(pl.BlockSpec((t, ct), lambda i, s: (0, 2 * s * nct + i)),
                   pl.BlockSpec((8, ct), lambda i, s: (0, i))),
        input_output_aliases={0: 0},
        compiler_params=_params(("arbitrary", "arbitrary")),
    )(dconv_proj, dco, conv_proj, conv_proj, conv_w)


def _sum_attn_grads(parts, dza):
    hp, t, _ = dza.shape
    tb = ROW_TILE

    def body(*refs):
        o_ref = refs[10]
        for comp in range(3):
            o_ref[comp] = (refs[comp][...] + refs[3 + comp][...] + refs[6 + comp][...]).astype(BF16)
        o_ref[3] = refs[9][...]

    spec = pl.BlockSpec((None, tb, PAIR), lambda h, i: (h, i, 0))
    return pl.pallas_call(
        body, name="sum_attn_grads", grid=(hp, t // tb),
        out_shape=jax.ShapeDtypeStruct((4, hp, t, PAIR), BF16),
        in_specs=[spec] * 10,
        out_specs=pl.BlockSpec((4, None, tb, PAIR), lambda h, i: (0, h, i, 0)),
        compiler_params=_params(("parallel", "parallel")),
    )(*parts, dza)


def _dw_in(ht, dconv_proj, dattn, ws, da):
    d, t = ht.shape
    tn = COL_TILE
    nt = ws // tn
    per_comp = da // tn
    pairs = tn // PAIR

    def body_conv(a_ref, b_ref, o_ref):
        o_ref[...] = jnp.dot(a_ref[...], b_ref[...], preferred_element_type=F32).astype(BF16)

    gin = pl.pallas_call(
        body_conv, name="dw_in_conv", grid=(2, nt),
        out_shape=jax.ShapeDtypeStruct((N_CHIPS, d, ws), BF16),
        in_specs=[pl.BlockSpec((d, t), lambda j, n: (0, 0)), pl.BlockSpec((t, tn), lambda j, n: (0, j * nt + n))],
        out_specs=pl.BlockSpec((None, d, tn), lambda j, n: (j, 0, n)),
        compiler_params=_params(("parallel", "parallel")),
    )(ht, dconv_proj)

    def body_attn(g_ref, a_ref, b_ref, o_ref):
        del g_ref
        o_ref[...] = jnp.dot(a_ref[...], _pairs_to_cols(b_ref, pairs), preferred_element_type=F32).astype(BF16)

    return pl.pallas_call(
        body_attn, name="dw_in_attn", grid=(2, nt),
        out_shape=jax.ShapeDtypeStruct((N_CHIPS, d, ws), BF16),
        in_specs=[HBM, pl.BlockSpec((d, t), lambda j, n: (0, 0)),
                  pl.BlockSpec((None, pairs, t, PAIR), lambda j, n: (2 * j + n // per_comp, n % per_comp, 0, 0))],
        out_specs=pl.BlockSpec((None, d, tn), lambda j, n: (2 + j, 0, n)),
        input_output_aliases={0: 0},
        compiler_params=_params(("parallel", "parallel")),
    )(gin, ht, dattn)


def _dh(dconv_proj, dattn, winf, da):
    t = dconv_proj.shape[0]
    _, d, ws = winf.shape
    tm = 1024
    tk = COL_TILE
    nk = ws // tk
    per_comp = da // tk
    pairs = tk // PAIR

    def body_conv(a_ref, b_ref, o_ref):
        @pl.when((pl.program_id(1) == 0) & (pl.program_id(2) == 0))
        def _():
            o_ref[...] = jnp.zeros(o_ref.shape, F32)

        o_ref[...] += lax.dot_general(a_ref[...], b_ref[...], (((1,), (1,)), ((), ())), preferred_element_type=F32)

    part = pl.pallas_call(
        body_conv, name="dh_conv", grid=(t // tm, 2, nk),
        out_shape=jax.ShapeDtypeStruct((t, d), F32),
        in_specs=[pl.BlockSpec((tm, tk), lambda m, j, k: (m, j * nk + k)),
                  pl.BlockSpec((None, d, tk), lambda m, j, k: (j, 0, k))],
        out_specs=pl.BlockSpec((tm, d), lambda m, j, k: (m, 0)),
        compiler_params=_params(("parallel", "arbitrary", "arbitrary")),
    )(dconv_proj, winf)

    def body_attn(p_ref, a_ref, b_ref, o_ref):
        @pl.when((pl.program_id(1) == 0) & (pl.program_id(2) == 0))
        def _():
            o_ref[...] = p_ref[...]

        o_ref[...] += lax.dot_general(_pairs_to_cols(a_ref, pairs), b_ref[...], (((1,), (1,)), ((), ())),
                                      preferred_element_type=F32)

    return pl.pallas_call(
        body_attn, name="dh_attn", grid=(t // tm, 2, nk),
        out_shape=jax.ShapeDtypeStruct((t, d), F32),
        in_specs=[pl.BlockSpec((tm, d), lambda m, j, k: (m, 0)),
                  pl.BlockSpec((None, pairs, tm, PAIR), lambda m, j, k: (2 * j + k // per_comp, k % per_comp, m, 0)),
                  pl.BlockSpec((None, d, tk), lambda m, j, k: (2 + j, 0, k))],
        out_specs=pl.BlockSpec((tm, d), lambda m, j, k: (m, 0)),
        compiler_params=_params(("parallel", "arbitrary", "arbitrary")),
    )(part, dattn, winf)


def _prenorm_bwd(x, dh, dout, mod, g_pre):
    t, d = x.shape
    tb = ROW_TILE

    def body(x_ref, dh_ref, dout_ref, mod_ref, g_ref, gx_ref, acc_ref):
        xv = x_ref[...]
        dhv = dh_ref[...]
        r = lax.rsqrt(jnp.mean(xv * xv, axis=-1, keepdims=True) + EPS)
        xh = xv * r
        one_scale = 1.0 + mod_ref[1:2, :]
        a = dhv * one_scale * g_ref[...]
        gx_ref[...] = dout_ref[...] + r * (a - xh * jnp.mean(a * xh, axis=-1, keepdims=True))
        part = jnp.concatenate(
            [jnp.sum(dhv, axis=0, keepdims=True), jnp.sum(dhv * xh * g_ref[...], axis=0, keepdims=True),
             jnp.sum(dhv * xh * one_scale, axis=0, keepdims=True), jnp.zeros((5, d), F32)], axis=0)

        @pl.when(pl.program_id(0) == 0)
        def _():
            acc_ref[...] = jnp.zeros(acc_ref.shape, F32)

        acc_ref[...] += part

    return pl.pallas_call(
        body, name="prenorm_bwd", grid=(t // tb,),
        out_shape=(jax.ShapeDtypeStruct((t, d), F32), jax.ShapeDtypeStruct((8, d), F32)),
        in_specs=[pl.BlockSpec((tb, d), lambda i: (i, 0)), pl.BlockSpec((tb, d), lambda i: (i, 0)),
                  pl.BlockSpec((tb, d), lambda i: (i, 0)), pl.BlockSpec((3, d), lambda i: (0, 0)),
                  pl.BlockSpec((1, d), lambda i: (0, 0))],
        out_specs=(pl.BlockSpec((tb, d), lambda i: (i, 0)), pl.BlockSpec((8, d), lambda i: (0, 0))),
        compiler_params=_params(("arbitrary",)),
    )(x, dh, dout, mod, g_pre)


def _chip_sums(place, g, rsib, name):
    _, rows, cols = g.shape
    half = rows // 2
    tr = min(half, ROW_TILE)
    nt = half // tr

    def body(place_ref, g_ref, r_ref, o_ref):
        del place_ref
        o_ref[...] = (g_ref[...].astype(F32) + r_ref[...].astype(F32)).astype(BF16)

    grid_spec = pltpu.PrefetchScalarGridSpec(
        num_scalar_prefetch=1, grid=(N_CHIPS, nt),
        in_specs=[pl.BlockSpec((None, tr, cols), lambda j, i, p: (j, p[1] * nt + i, 0)),
                  pl.BlockSpec((None, tr, cols), lambda j, i, p: (j, i, 0))],
        out_specs=pl.BlockSpec((None, tr, cols), lambda j, i, p: (j, i, 0)))
    return pl.pallas_call(
        body, name=name, grid_spec=grid_spec,
        out_shape=jax.ShapeDtypeStruct((N_CHIPS, half, cols), BF16),
        compiler_params=_params(("parallel", "parallel")),
    )(place, g, rsib)


def _owner_sum(place, g, rsib, rici, name):
    _, rows, cols = g.shape
    half = rows // 2
    tr = min(half, ROW_TILE)
    nt = half // tr

    def body(place_ref, g_ref, r_ref, i_ref, o_ref):
        del place_ref
        acc = g_ref[...].astype(F32) + r_ref[...].astype(F32)
        for k in range(N_CHIPS - 1):
            acc = acc + i_ref[k].astype(F32)
        o_ref[...] = acc

    grid_spec = pltpu.PrefetchScalarGridSpec(
        num_scalar_prefetch=1, grid=(nt,),
        in_specs=[pl.BlockSpec((None, tr, cols), lambda i, p: (p[0], p[1] * nt + i, 0)),
                  pl.BlockSpec((None, tr, cols), lambda i, p: (p[0], i, 0)),
                  pl.BlockSpec((N_CHIPS - 1, tr, cols), lambda i, p: (0, i, 0))],
        out_specs=pl.BlockSpec((tr, cols), lambda i, p: (i, 0)))
    return pl.pallas_call(
        body, name=name, grid_spec=grid_spec,
        out_shape=jax.ShapeDtypeStruct((half, cols), F32),
        compiler_params=_params(("parallel",)),
    )(place, g, rsib, rici)


def _adam_math(w, g, m, v):
    m2 = ADAM_B1 * m + (1.0 - ADAM_B1) * g
    v2 = ADAM_B2 * v + (1.0 - ADAM_B2) * (g * g)
    m_hat = m2 / (1.0 - ADAM_B1 ** ADAM_STEP)
    v_hat = v2 / (1.0 - ADAM_B2 ** ADAM_STEP)
    delta = -ADAM_LR * (m_hat / (jnp.sqrt(v_hat) + ADAM_EPS) + ADAM_WD * w)
    return delta, m2, v2


def _adamw(w, g, m, v, name):
    rows, cols = w.shape
    tr = min(rows, ROW_TILE)

    def body(w_ref, g_ref, m_ref, v_ref, d_ref, m2_ref, v2_ref):
        d_ref[...], m2_ref[...], v2_ref[...] = _adam_math(w_ref[...], g_ref[...], m_ref[...], v_ref[...])

    spec = pl.BlockSpec((tr, cols), lambda i: (i, 0))
    return pl.pallas_call(
        body, name=name, grid=(rows // tr,),
        out_shape=(jax.ShapeDtypeStruct(w.shape, F32),) * 3,
        in_specs=[spec] * 4, out_specs=(spec,) * 3,
        compiler_params=_params(("parallel",)),
    )(w, g, m, v)


def _ada_grad_adamw(c_all_t, dmod_cols, w, m, v):
    d, wa = w.shape
    tn = 256

    def body(ct_ref, dm_ref, w_ref, m_ref, v_ref, g_ref, d_ref, m2_ref, v2_ref):
        act = _silu(ct_ref[...])
        g = act[:, 0:1] * dm_ref[0:1, :]
        for b in range(1, N_DEV):
            g = g + act[:, b:b + 1] * dm_ref[b:b + 1, :]
        g_ref[...] = g
        d_ref[...], m2_ref[...], v2_ref[...] = _adam_math(w_ref[...], g, m_ref[...], v_ref[...])

    spec = pl.BlockSpec((d, tn), lambda i: (0, i))
    return pl.pallas_call(
        body, name="ada_grad_adamw", grid=(wa // tn,),
        out_shape=(jax.ShapeDtypeStruct(w.shape, F32),) * 4,
        in_specs=[pl.BlockSpec((d, N_DEV), lambda i: (0, 0)), pl.BlockSpec((N_DEV, tn), lambda i: (0, i)),
                  spec, spec, spec],
        out_specs=(spec,) * 4,
        compiler_params=_params(("parallel",)),
    )(c_all_t, dmod_cols, w, m, v)


def _sum_devices(gathered):
    n = gathered.shape[1]

    def body(g_ref, o_ref):
        acc = g_ref[0:8, :]
        for dev in range(1, N_DEV):
            acc = acc + g_ref[8 * dev:8 * dev + 8, :]
        o_ref[...] = acc

    return pl.pallas_call(
        body, name="sum_devices",
        out_shape=jax.ShapeDtypeStruct((8, n), F32),
        in_specs=[VMEM], out_specs=VMEM,
    )(gathered)


def _pack_small(pieces):
    flat = [p.reshape(-1).astype(F32) for p in pieces]
    offsets, total = [], 0
    for p in flat:
        offsets.append(total)
        total += p.shape[0]
    padded = -(-total // SMALL_ALIGN) * SMALL_ALIGN
    if padded > total:
        flat.append(jnp.zeros((padded - total,), F32))
    return jnp.concatenate(flat).reshape(8, padded // 8), offsets


def _alibi_slope_rows(n_heads):
    slopes = 2.0 ** (-8.0 * jnp.arange(1, n_heads + 1, dtype=F32) / n_heads)
    rows = jnp.zeros((n_heads // 2, 8), F32).at[:, 0:2].set(slopes.reshape(n_heads // 2, 2))
    return jnp.broadcast_to(rows[:, :, None], (n_heads // 2, 8, ATT_KW))


def kernel(x, c, w_ada, b_ada, g_pre, w_in, conv_w, conv_b, g_conv, g_attn, w_out, g_post, loss_target, m_w_ada, m_b_ada, m_g_pre, m_w_in, m_conv_w, m_conv_b, m_g_conv, m_g_attn, m_w_out, m_g_post, v_w_ada, v_b_ada, v_g_pre, v_w_in, v_conv_w, v_conv_b, v_g_conv, v_g_attn, v_w_out, v_g_post):
    t, d = x.shape[1], x.shape[2]
    dc = conv_b.shape[1]
    da = g_attn.shape[1]
    hp = da // PAIR
    ws = w_in.shape[2]
    wa = w_ada.shape[2]
    cws = conv_w.shape[2]
    assert t % ROW_TILE == 0 and d % ROW_TILE == 0 and dc % COL_TILE == 0 and da % COL_TILE == 0
    assert ws == 2 * dc and dc == da and t // BRANCHES[-1][1] >= ATT_BQ

    mx, my, mc = _my_place()
    chip = _chip_of(mx, my)
    dev = 2 * chip + mc
    place = jnp.stack([chip, mc]).astype(jnp.int32)

    x2, tgt2 = x[0], loss_target[0]
    w_ada2, w_in2, w_out2 = w_ada[0], w_in[0], w_out[0]

    packed, offs = _pack_small([c[0], conv_w[0]])
    seen = _allgather8(packed, "gather_inputs").reshape(N_DEV, -1)
    c_all = seen[:, offs[0]:offs[0] + d]
    conv_w_full = seen[0::2, offs[1]:offs[1] + 3 * cws].reshape(N_CHIPS, 3, cws).transpose(1, 0, 2).reshape(3, dc)

    ada_part = _ada_partial(c_all, w_ada2)
    ada_seen = _allgather8(ada_part, "gather_ada").reshape(N_DEV, N_DEV, wa)
    mod_flat = lax.dynamic_index_in_dim(ada_seen[0::2], dev, axis=1, keepdims=False).reshape(1, 3 * d) + b_ada
    mod = mod_flat.reshape(3, d)

    winf, woutf4 = _gather_weights(_cast_bf16(w_in2, "cast_w_in"), _cast_bf16(w_out2, "cast_w_out"))
    woutf = woutf4.reshape(dc + da, d)

    h, ht = _prenorm(x2, mod, g_pre)
    conv_proj = _proj_conv(h, winf, dc)
    attn_proj = _proj_attn(h, winf, da)
    slopes = _alibi_slope_rows(da // HEAD_DIM)
    co = _conv_fwd(conv_proj, conv_w_full, conv_b, dc)
    branch_outs = _attn_fwd(attn_proj, slopes)
    g_attn_pairs = g_attn.reshape(hp, 1, PAIR)
    ycat, ycat_t, o_mix, lse = _mix_fwd(co, conv_proj, attn_proj, branch_outs, g_conv, g_attn_pairs)
    dout, dy, post_sums = _out_fwd_bwd(ycat, woutf, x2, tgt2, mod, g_post)

    gout = _matmul_nn(ycat_t, dy, BF16, "dw_out").reshape(N_CHIPS, (dc + da) // N_CHIPS, d)
    dycat = _matmul_nt(dy, woutf, BF16, "dycat")
    dconv_proj, dco, dza, d_o, dg_conv, dg_attn = _mix_bwd(dycat, co, conv_proj, attn_proj, o_mix, g_conv, g_attn_pairs)
    dconv_proj, conv_sums = _conv_bwd(dconv_proj, dco, conv_proj, conv_w_full, dc)
    attn_parts = _attn_bwd(attn_proj, o_mix, d_o, lse, slopes)
    dattn = _sum_attn_grads(attn_parts, dza)
    gin = _dw_in(ht, dconv_proj, dattn, ws, da)
    dh = _dh(dconv_proj, dattn, winf, da)
    grad_x, pre_sums = _prenorm_bwd(x2, dh, dout, mod, g_pre)

    rsib_in, rsib_out = _swap_halves(gin, gout)
    csum_in = _chip_sums(place, gin, rsib_in, "rs_chip_sum_in")
    csum_out = _chip_sums(place, gout, rsib_out, "rs_chip_sum_out")
    rici_in, rici_out = _send_to_owners(csum_in, csum_out)
    half_in = _owner_sum(place, gin, rsib_in, rici_in, "rs_owner_sum_in")
    half_out = _owner_sum(place, gout, rsib_out, rici_out, "rs_owner_sum_out")
    grad_w_in, grad_w_out = _join_halves(half_in, half_out)

    small, so = _pack_small([
        pre_sums[0], pre_sums[1], post_sums[0],
        pre_sums[2], conv_sums[0:3], conv_sums[3], dg_conv, dg_attn, post_sums[1], post_sums[2, 0:128]])
    small_seen = _allgather8(small, "gather_small")
    total = _sum_devices(small_seen).reshape(-1)
    dmod_all = small_seen.reshape(N_DEV, -1)[:, 0:3 * d]
    loss = total[so[9]]
    grad_b_ada = total[0:3 * d].reshape(1, 3 * d)
    grad_g_pre = total[so[3]:so[3] + d].reshape(1, d)
    grad_conv_w_full = total[so[4]:so[4] + 3 * dc].reshape(3, dc)
    grad_conv_w = lax.dynamic_slice_in_dim(grad_conv_w_full, chip * cws, cws, axis=1).reshape(1, 3, cws)
    grad_conv_b = total[so[5]:so[5] + dc].reshape(1, dc)
    grad_g_conv = total[so[6]:so[6] + dc].reshape(1, dc)
    grad_g_attn = total[so[7]:so[7] + da].reshape(1, da)
    grad_g_post = total[so[8]:so[8] + d].reshape(1, d)

    dmod_cols = lax.dynamic_slice_in_dim(dmod_all, chip * wa, wa, axis=1)
    grad_w_ada, delta_w_ada, new_m_w_ada, new_v_w_ada = _ada_grad_adamw(c_all.T, dmod_cols, w_ada2, m_w_ada[0], v_w_ada[0])
    delta_w_in, new_m_w_in, new_v_w_in = _adamw(w_in2, grad_w_in, m_w_in[0], v_w_in[0], "adamw_w_in")
    delta_w_out, new_m_w_out, new_v_w_out = _adamw(w_out2, grad_w_out, m_w_out[0], v_w_out[0], "adamw_w_out")

    small_w = [b_ada, g_pre, conv_w, conv_b, g_conv, g_attn, g_post]
    small_g = [grad_b_ada, grad_g_pre, grad_conv_w, grad_conv_b, grad_g_conv, grad_g_attn, grad_g_post]
    small_m = [m_b_ada, m_g_pre, m_conv_w, m_conv_b, m_g_conv, m_g_attn, m_g_post]
    small_v = [v_b_ada, v_g_pre, v_conv_w, v_conv_b, v_g_conv, v_g_attn, v_g_post]
    pw, po = _pack_small(small_w)
    pg, _ = _pack_small(small_g)
    pm, _ = _pack_small(small_m)
    pv, _ = _pack_small(small_v)
    sd, sm, sv = (a.reshape(-1) for a in _adamw(pw, pg, pm, pv, "adamw_small"))

    def unpack(flat):
        return [flat[o:o + w.size].reshape(w.shape) for o, w in zip(po, small_w)]

    d_small, m_small, v_small = unpack(sd), unpack(sm), unpack(sv)

    def lead(a):
        return a.reshape((1,) + a.shape)

    grads = [lead(grad_w_ada), grad_b_ada, grad_g_pre, lead(grad_w_in), grad_conv_w, grad_conv_b, grad_g_conv,
             grad_g_attn, lead(grad_w_out), grad_g_post]
    deltas = [lead(delta_w_ada), d_small[0], d_small[1], lead(delta_w_in), d_small[2], d_small[3], d_small[4],
              d_small[5], lead(delta_w_out), d_small[6]]
    new_ms = [lead(new_m_w_ada), m_small[0], m_small[1], lead(new_m_w_in), m_small[2], m_small[3], m_small[4],
              m_small[5], lead(new_m_w_out), m_small[6]]
    new_vs = [lead(new_v_w_ada), v_small[0], v_small[1], lead(new_v_w_in), v_small[2], v_small[3], v_small[4],
              v_small[5], lead(new_v_w_out), v_small[6]]
    return (loss, lead(grad_x), *grads, *deltas, *new_ms, *new_vs)
```

```python
import jax
import jax.numpy as jnp
from jax import lax
from jax.experimental import pallas as pl
from jax.experimental.pallas import tpu as pltpu

F32 = jnp.float32
BF16 = jnp.bfloat16
MESH = pl.DeviceIdType.MESH
HBM = pl.BlockSpec(memory_space=pltpu.HBM)
VMEM = pl.BlockSpec(memory_space=pltpu.VMEM)

HEAD_DIM = 64
PAIR = 2 * HEAD_DIM
BRANCHES = ((128, 1), (512, 4), (2048, 16))
SIDE = 64
EPS = 1e-6
NEG_INF = -1e30
N_CHIPS = 4
N_DEV = 8

ADAM_LR = 0.001
ADAM_B1 = 0.9
ADAM_B2 = 0.999
ADAM_EPS = 1e-08
ADAM_WD = 0.01
ADAM_STEP = 10

VMEM_LIMIT_BYTES = 56 * 1024 * 1024
ROW_TILE = 256
COL_TILE = 512
CONV_TILE = 256
ATT_BQ = 128
ATT_KW = ATT_BQ + 2 * SIDE
ATT_UNROLL = 4
SMALL_ALIGN = 1024


def _params(semantics=None):
    kw = {"vmem_limit_bytes": VMEM_LIMIT_BYTES}
    if semantics is not None:
        kw["dimension_semantics"] = semantics
    return pltpu.CompilerParams(**kw)


def _silu(z):
    return z * jax.nn.sigmoid(z)


def _silu_grad(z):
    s = jax.nn.sigmoid(z)
    return s * (1.0 + z * (1.0 - s))


def _my_place():
    return lax.axis_index("x"), lax.axis_index("y"), lax.axis_index("c")


def _flip(a, bit):
    return 1 - a if bit else a


def _chip_of(x, y):
    return 2 * x + y


def _allgather8(v, name):
    rows_per, n = v.shape

    def body(v_ref, out_ref, send_sems, recv_sems):
        x, y, c = _my_place()
        me = 4 * x + 2 * y + c

        def rows(idx):
            return out_ref.at[pl.ds(pl.multiple_of(idx * rows_per, rows_per), rows_per), :]

        out_ref[pl.ds(pl.multiple_of(me * rows_per, rows_per), rows_per), :] = v_ref[...]
        copies = []
        for k in range(1, N_DEV):
            peer = (_flip(x, k & 4), _flip(y, k & 2), _flip(c, k & 1))
            cp = pltpu.make_async_remote_copy(
                src_ref=v_ref, dst_ref=rows(me), send_sem=send_sems.at[k - 1], recv_sem=recv_sems.at[k - 1],
                device_id=peer, device_id_type=MESH)
            cp.start()
            copies.append((cp, peer))
        for k, (cp, peer) in enumerate(copies):
            src = 4 * peer[0] + 2 * peer[1] + peer[2]
            pltpu.make_async_remote_copy(
                src_ref=v_ref, dst_ref=rows(src), send_sem=send_sems.at[k], recv_sem=recv_sems.at[k],
                device_id=peer, device_id_type=MESH).wait_recv()
        for cp, _ in copies:
            cp.wait_send()

    return pl.pallas_call(
        body, name=name,
        out_shape=jax.ShapeDtypeStruct((N_DEV * rows_per, n), v.dtype),
        in_specs=[VMEM], out_specs=VMEM,
        scratch_shapes=[pltpu.SemaphoreType.DMA((N_DEV - 1,)), pltpu.SemaphoreType.DMA((N_DEV - 1,))],
    )(v)


def _gather_weights(win_slots, wout_slots):
    slots = (win_slots, wout_slots)
    halves = tuple(s.shape[1] // 2 for s in slots)

    def body(win_in, wout_in, winf_ref, woutf_ref, send_sems, recv_sems, fwd_send, fwd_recv):
        del win_in, wout_in
        x, y, c = _my_place()
        me = _chip_of(x, y)
        bufs = (winf_ref, woutf_ref)

        def half(t, chip, which):
            return bufs[t].at[chip, pl.ds(pl.multiple_of(which * halves[t], halves[t]), halves[t]), :]

        started = []
        for k in (1, 2, 3):
            peer = (_flip(x, k & 2), _flip(y, k & 1), c)
            for t in range(2):
                idx = 2 * (k - 1) + t
                cp = pltpu.make_async_remote_copy(
                    src_ref=half(t, me, c), dst_ref=half(t, me, c),
                    send_sem=send_sems.at[idx], recv_sem=recv_sems.at[idx], device_id=peer, device_id_type=MESH)
                cp.start()
                started.append(cp)
        sibling = (x, y, 1 - c)
        for k in (1, 2, 3):
            peer = (_flip(x, k & 2), _flip(y, k & 1), c)
            src_chip = _chip_of(peer[0], peer[1])
            for t in range(2):
                idx = 2 * (k - 1) + t
                landed = half(t, src_chip, c)
                pltpu.make_async_remote_copy(
                    src_ref=landed, dst_ref=landed, send_sem=send_sems.at[idx], recv_sem=recv_sems.at[idx],
                    device_id=peer, device_id_type=MESH).wait_recv()
                fw = pltpu.make_async_remote_copy(
                    src_ref=landed, dst_ref=landed, send_sem=fwd_send.at[idx], recv_sem=fwd_recv.at[idx],
                    device_id=sibling, device_id_type=MESH)
                fw.start()
                started.append(fw)
        for k in (1, 2, 3):
            src_chip = _chip_of(_flip(x, k & 2), _flip(y, k & 1))
            for t in range(2):
                idx = 2 * (k - 1) + t
                other = half(t, src_chip, 1 - c)
                pltpu.make_async_remote_copy(
                    src_ref=other, dst_ref=other, send_sem=fwd_send.at[idx], recv_sem=fwd_recv.at[idx],
                    device_id=sibling, device_id_type=MESH).wait_recv()
        for cp in started:
            cp.wait_send()

    return pl.pallas_call(
        body, name="gather_weights",
        out_shape=tuple(jax.ShapeDtypeStruct(s.shape, s.dtype) for s in slots),
        in_specs=[HBM, HBM], out_specs=(HBM, HBM),
        input_output_aliases={0: 0, 1: 1},
        scratch_shapes=[pltpu.SemaphoreType.DMA((6,))] * 4,
    )(win_slots, wout_slots)


def _swap_halves(gin, gout):
    grads = (gin, gout)
    halves = tuple(g.shape[1] // 2 for g in grads)

    def body(gin_ref, gout_ref, rin_ref, rout_ref, send_sems, recv_sems):
        x, y, c = _my_place()
        sibling = (x, y, 1 - c)
        srcs = (gin_ref, gout_ref)
        dsts = (rin_ref, rout_ref)
        copies = []
        for t in range(2):
            theirs = srcs[t].at[:, pl.ds(pl.multiple_of((1 - c) * halves[t], halves[t]), halves[t]), :]
            cp = pltpu.make_async_remote_copy(
                src_ref=theirs, dst_ref=dsts[t], send_sem=send_sems.at[t], recv_sem=recv_sems.at[t],
                device_id=sibling, device_id_type=MESH)
            cp.start()
            copies.append(cp)
        for cp in copies:
            cp.wait()

    return pl.pallas_call(
        body, name="rs_swap_halves",
        out_shape=tuple(jax.ShapeDtypeStruct((N_CHIPS, g.shape[1] // 2, g.shape[2]), g.dtype) for g in grads),
        in_specs=[HBM, HBM], out_specs=(HBM, HBM),
        scratch_shapes=[pltpu.SemaphoreType.DMA((2,)), pltpu.SemaphoreType.DMA((2,))],
    )(gin, gout)


def _send_to_owners(cin, cout):
    sums = (cin, cout)

    def body(cin_ref, cout_ref, rin_ref, rout_ref, send_sems, recv_sems):
        x, y, c = _my_place()
        srcs = (cin_ref, cout_ref)
        dsts = (rin_ref, rout_ref)
        copies = []
        for k in (1, 2, 3):
            peer = (_flip(x, k & 2), _flip(y, k & 1), c)
            owner = _chip_of(peer[0], peer[1])
            for t in range(2):
                idx = 2 * (k - 1) + t
                cp = pltpu.make_async_remote_copy(
                    src_ref=srcs[t].at[owner], dst_ref=dsts[t].at[k - 1],
                    send_sem=send_sems.at[idx], recv_sem=recv_sems.at[idx], device_id=peer, device_id_type=MESH)
                cp.start()
                copies.append(cp)
        for cp in copies:
            cp.wait()

    return pl.pallas_call(
        body, name="rs_send_to_owners",
        out_shape=tuple(jax.ShapeDtypeStruct((N_CHIPS - 1,) + s.shape[1:], s.dtype) for s in sums),
        in_specs=[HBM, HBM], out_specs=(HBM, HBM),
        scratch_shapes=[pltpu.SemaphoreType.DMA((6,)), pltpu.SemaphoreType.DMA((6,))],
    )(cin, cout)


def _join_halves(fin, fout):
    fulls = (fin, fout)

    def body(fin_in, fout_in, fin_ref, fout_ref, send_sems, recv_sems):
        del fin_in, fout_in
        x, y, c = _my_place()
        sibling = (x, y, 1 - c)
        bufs = (fin_ref, fout_ref)
        copies = []
        for t in range(2):
            rows = fulls[t].shape[0] // 2
            mine = bufs[t].at[pl.ds(pl.multiple_of(c * rows, rows), rows), :]
            cp = pltpu.make_async_remote_copy(
                src_ref=mine, dst_ref=mine, send_sem=send_sems.at[t], recv_sem=recv_sems.at[t],
                device_id=sibling, device_id_type=MESH)
            cp.start()
            copies.append((cp, rows))
        for t, (cp, rows) in enumerate(copies):
            theirs = bufs[t].at[pl.ds(pl.multiple_of((1 - c) * rows, rows), rows), :]
            pltpu.make_async_remote_copy(
                src_ref=theirs, dst_ref=theirs, send_sem=send_sems.at[t], recv_sem=recv_sems.at[t],
                device_id=sibling, device_id_type=MESH).wait_recv()
            cp.wait_send()

    return pl.pallas_call(
        body, name="rs_join_halves",
        out_shape=tuple(jax.ShapeDtypeStruct(f.shape, f.dtype) for f in fulls),
        in_specs=[HBM, HBM], out_specs=(HBM, HBM),
        input_output_aliases={0: 0, 1: 1},
        scratch_shapes=[pltpu.SemaphoreType.DMA((2,))] * 2,
    )(fin, fout)


def _cast_into_slot(place, w, name):
    rows, cols = w.shape
    tr = min(rows, ROW_TILE)

    def body(place_ref, w_ref, o_ref):
        del place_ref
        o_ref[...] = w_ref[...].astype(BF16)

    grid_spec = pltpu.PrefetchScalarGridSpec(
        num_scalar_prefetch=1, grid=(rows // tr,),
        in_specs=[pl.BlockSpec((tr, cols), lambda i, p: (i, 0))],
        out_specs=pl.BlockSpec((None, tr, cols), lambda i, p: (p[0], i, 0)))
    return pl.pallas_call(
        body, name=name, grid_spec=grid_spec,
        out_shape=jax.ShapeDtypeStruct((N_CHIPS, rows, cols), BF16),
        compiler_params=_params(("parallel",)),
    )(place, w)


def _ada_partial(c_all, w_ada):
    d_model, wa = w_ada.shape
    tn = 512 if wa % 512 == 0 else 256

    def body(c_ref, w_ref, o_ref):
        o_ref[...] = jnp.dot(_silu(c_ref[...]), w_ref[...], precision=lax.Precision.HIGHEST,
                             preferred_element_type=F32)

    return pl.pallas_call(
        body, name="ada_partial", grid=(wa // tn,),
        out_shape=jax.ShapeDtypeStruct((N_DEV, wa), F32),
        in_specs=[pl.BlockSpec((N_DEV, d_model), lambda i: (0, 0)), pl.BlockSpec((d_model, tn), lambda i: (0, i))],
        out_specs=pl.BlockSpec((N_DEV, tn), lambda i: (0, i)),
        compiler_params=_params(("parallel",)),
    )(c_all, w_ada)


def _prenorm(x, mod, g_pre):
    t, d = x.shape
    tb = ROW_TILE

    def body(x_ref, mod_ref, g_ref, h_ref, ht_ref):
        xv = x_ref[...]
        r = lax.rsqrt(jnp.mean(xv * xv, axis=-1, keepdims=True) + EPS)
        h = (xv * r) * g_ref[...] * (1.0 + mod_ref[1:2, :]) + mod_ref[0:1, :]
        h_ref[...] = h.astype(BF16)
        ht_ref[...] = h.T.astype(BF16)

    return pl.pallas_call(
        body, name="prenorm", grid=(t // tb,),
        out_shape=(jax.ShapeDtypeStruct((t, d), BF16), jax.ShapeDtypeStruct((d, t), BF16)),
        in_specs=[pl.BlockSpec((tb, d), lambda i: (i, 0)), pl.BlockSpec((3, d), lambda i: (0, 0)),
                  pl.BlockSpec((1, d), lambda i: (0, 0))],
        out_specs=(pl.BlockSpec((tb, d), lambda i: (i, 0)), pl.BlockSpec((d, tb), lambda i: (0, i))),
        compiler_params=_params(("parallel",)),
    )(x, mod, g_pre)


def _pairs_to_cols(ref, n_pairs):
    return jnp.concatenate([ref[i] for i in range(n_pairs)], axis=1)


def _proj_conv(h, winf):
    t, d = h.shape
    ws = winf.shape[2]
    tn = COL_TILE
    nt = ws // tn

    def body(a_ref, b_ref, o_ref):
        o_ref[...] = jnp.dot(a_ref[...], b_ref[...], preferred_element_type=F32).astype(BF16)

    return pl.pallas_call(
        body, name="proj_conv", grid=(2, nt),
        out_shape=jax.ShapeDtypeStruct((t, 2 * ws), BF16),
        in_specs=[pl.BlockSpec((t, d), lambda j, n: (0, 0)), pl.BlockSpec((None, d, tn), lambda j, n: (j, 0, n))],
        out_specs=pl.BlockSpec((t, tn), lambda j, n: (0, j * nt + n)),
        compiler_params=_params(("parallel", "parallel")),
    )(h, winf)


def _proj_attn(h, winf, da):
    t, d = h.shape
    ws = winf.shape[2]
    tn = COL_TILE
    nt = ws // tn
    per_comp = da // tn
    pairs = tn // PAIR

    def body(a_ref, b_ref, o_ref):
        res = jnp.dot(a_ref[...], b_ref[...], preferred_element_type=F32).astype(BF16)
        for i in range(pairs):
            o_ref[i] = res[:, i * PAIR:(i + 1) * PAIR]

    return pl.pallas_call(
        body, name="proj_attn", grid=(2, nt),
        out_shape=jax.ShapeDtypeStruct((4, da // PAIR, t, PAIR), BF16),
        in_specs=[pl.BlockSpec((t, d), lambda j, n: (0, 0)), pl.BlockSpec((None, d, tn), lambda j, n: (2 + j, 0, n))],
        out_specs=pl.BlockSpec((None, pairs, t, PAIR), lambda j, n: (2 * j + n // per_comp, n % per_comp, 0, 0)),
        compiler_params=_params(("parallel", "parallel")),
    )(h, winf)


def _shift_rows(a, rows):
    idx = lax.broadcasted_iota(jnp.int32, a.shape, 0)
    prev = jnp.where(idx == 0, 0.0, pltpu.roll(a, 1, 0))
    nxt = jnp.where(idx == rows - 1, 0.0, pltpu.roll(a, rows - 1, 0))
    return prev, nxt


def _conv_fwd(conv_proj, conv_w, conv_b, dc):
    t = conv_proj.shape[0]
    ct = CONV_TILE
    nct = dc // ct

    def body(u_ref, cg_ref, w_ref, b_ref, co_ref):
        a = cg_ref[...].astype(F32) * u_ref[...].astype(F32)
        prev, nxt = _shift_rows(a, t)
        co_ref[...] = w_ref[0:1, :] * prev + w_ref[1:2, :] * a + w_ref[2:3, :] * nxt + b_ref[...]

    return pl.pallas_call(
        body, name="conv_fwd", grid=(nct,),
        out_shape=jax.ShapeDtypeStruct((t, dc), F32),
        in_specs=[pl.BlockSpec((t, ct), lambda i: (0, i)), pl.BlockSpec((t, ct), lambda i: (0, 2 * nct + i)),
                  pl.BlockSpec((3, ct), lambda i: (0, i)), pl.BlockSpec((1, ct), lambda i: (0, i))],
        out_specs=pl.BlockSpec((t, ct), lambda i: (0, i)),
        compiler_params=_params(("parallel",)),
    )(conv_proj, conv_proj, conv_w, conv_b)


def _to_residue_major(src_ref, dst_ref, r, scale=None):
    t = src_ref.shape[0]
    seq = t // r
    for res in range(r):
        rows = src_ref[pl.ds(res, seq, stride=r), :] if r > 1 else src_ref[...]
        if scale is not None:
            rows = rows * scale
        dst_ref[res * seq:(res + 1) * seq, :] = rows.astype(dst_ref.dtype)


def _fill_bias_tiles(bias_ref, sl_ref, r, kw):
    base = (lax.broadcasted_iota(jnp.int32, (ATT_BQ, kw), 1) - lax.broadcasted_iota(jnp.int32, (ATT_BQ, kw), 0))
    for hh in range(2):
        slope = -(sl_ref[hh:hh + 1, 0:kw] * float(r))
        for e, shift in enumerate((0, -SIDE, ATT_BQ - kw)):
            arel = jnp.abs(base + shift)
            bias_ref[hh, e, :, 0:kw] = jnp.where(arel <= SIDE, arel.astype(F32) * slope, NEG_INF)


def _block_place(g, seq_len, kw):
    nqb = seq_len // ATT_BQ
    if nqb == 1:
        row = pl.multiple_of(g * ATT_BQ, ATT_BQ)
        return row, row, 0
    res = g // nqb
    qb = g - res * nqb
    q0 = qb * ATT_BQ
    ks = jnp.clip(q0 - SIDE, 0, seq_len - kw)
    edge = jnp.where(qb == 0, 0, jnp.where(qb == nqb - 1, 2, 1))
    return (pl.multiple_of(res * seq_len + q0, ATT_BQ), pl.multiple_of(res * seq_len + ks, SIDE), edge)


def _attn_fwd(attn_proj, slopes):
    _, hp, t, _ = attn_proj.shape
    n_blocks = t // ATT_BQ

    def body(qkv_ref, sl_ref, o_ref, lse_ref, stage, dil, bias, o_res, l_res, o_tok, l_tok):
        for b, (_, r) in enumerate(BRANCHES):
            seq_len = t // r
            kw = min(ATT_KW, seq_len)
            for comp in range(3):
                stage[...] = qkv_ref[comp].astype(F32)
                _to_residue_major(stage, dil.at[comp], r, scale=HEAD_DIM ** -0.5 if comp == 0 else None)
            _fill_bias_tiles(bias, sl_ref, r, kw)
            o_dst, l_dst = (o_tok.at[b], l_tok.at[b]) if r == 1 else (o_res, l_res)

            def block(g, carry, seq_len=seq_len, kw=kw, o_dst=o_dst, l_dst=l_dst):
                qrow, krow, edge = _block_place(g, seq_len, kw)
                q = dil[0, pl.ds(qrow, ATT_BQ), :]
                k = dil[1, pl.ds(krow, kw), :]
                v = dil[2, pl.ds(krow, kw), :]
                o_parts, l_parts = [], []
                for hh in range(2):
                    hs = slice(hh * HEAD_DIM, (hh + 1) * HEAD_DIM)
                    s = lax.dot_general(q[:, hs], k[:, hs], (((1,), (1,)), ((), ())), preferred_element_type=F32)
                    s = s + bias[hh, edge, :, 0:kw]
                    m = jnp.max(s, axis=-1, keepdims=True)
                    p = jnp.exp(s - m)
                    den = jnp.sum(p, axis=-1, keepdims=True)
                    o = jnp.dot(p.astype(BF16), v[:, hs], preferred_element_type=F32) * (1.0 / den)
                    o_parts.append(o)
                    l_parts.append(jnp.broadcast_to(m + jnp.log(den), (ATT_BQ, HEAD_DIM)))
                o_dst[pl.ds(qrow, ATT_BQ), :] = jnp.concatenate(o_parts, axis=1)
                l_dst[pl.ds(qrow, ATT_BQ), :] = jnp.concatenate(l_parts, axis=1)
                return carry

            lax.fori_loop(0, n_blocks, block, 0, unroll=ATT_UNROLL)
            if r > 1:
                for res in range(r):
                    rows = slice(res * seq_len, (res + 1) * seq_len)
                    o_tok[b, pl.ds(res, seq_len, stride=r), :] = o_res[rows, :]
                    l_tok[b, pl.ds(res, seq_len, stride=r), :] = l_res[rows, :]

        def merge(i, carry):
            rows = pl.ds(pl.multiple_of(i * ROW_TILE, ROW_TILE), ROW_TILE)
            la, lb, lc = l_tok[0, rows, :], l_tok[1, rows, :], l_tok[2, rows, :]
            m = jnp.maximum(jnp.maximum(la, lb), lc)
            wa, wb, wc = jnp.exp(la - m), jnp.exp(lb - m), jnp.exp(lc - m)
            den = wa + wb + wc
            o_ref[rows, :] = (wa * o_tok[0, rows, :] + wb * o_tok[1, rows, :] + wc * o_tok[2, rows, :]) * (1.0 / den)
            lse_ref[rows, :] = m + jnp.log(den)
            return carry

        lax.fori_loop(0, t // ROW_TILE, merge, 0)

    pair_spec = pl.BlockSpec((None, t, PAIR), lambda h: (h, 0, 0))
    return pl.pallas_call(
        body, name="attn_fwd", grid=(hp,),
        out_shape=(jax.ShapeDtypeStruct((hp, t, PAIR), F32), jax.ShapeDtypeStruct((hp, t, PAIR), F32)),
        in_specs=[pl.BlockSpec((3, None, t, PAIR), lambda h: (0, h, 0, 0)),
                  pl.BlockSpec((None, 8, ATT_KW), lambda h: (h, 0, 0))],
        out_specs=(pair_spec, pair_spec),
        scratch_shapes=[pltpu.VMEM((t, PAIR), F32), pltpu.VMEM((3, t, PAIR), BF16),
                        pltpu.VMEM((2, 3, ATT_BQ, ATT_KW), F32),
                        pltpu.VMEM((t, PAIR), F32), pltpu.VMEM((t, PAIR), F32),
                        pltpu.VMEM((3, t, PAIR), F32), pltpu.VMEM((3, t, PAIR), F32)],
        compiler_params=_params(("parallel",)),
    )(attn_proj, slopes)


def _attn_bwd(dattn, attn_proj, d_o, lse, delta, slopes):
    _, hp, t, _ = attn_proj.shape
    n_blocks = t // ATT_BQ

    def body(dattn_in, qkv_ref, do_ref, lse_ref, dl_ref, sl_ref, out_ref,
             stage, dil, lse_res, dl_res, bias, acc, tot):
        del dattn_in
        for b, (_, r) in enumerate(BRANCHES):
            seq_len = t // r
            kw = min(ATT_KW, seq_len)
            for comp in range(3):
                stage[...] = qkv_ref[comp].astype(F32)
                _to_residue_major(stage, dil.at[comp], r, scale=HEAD_DIM ** -0.5 if comp == 0 else None)
            stage[...] = do_ref[...].astype(F32)
            _to_residue_major(stage, dil.at[3], r)
            _to_residue_major(lse_ref, lse_res, r)
            _to_residue_major(dl_ref, dl_res, r)
            _fill_bias_tiles(bias, sl_ref, r, kw)
            acc[1] = jnp.zeros((t, PAIR), F32)
            acc[2] = jnp.zeros((t, PAIR), F32)

            def block(g, carry, seq_len=seq_len, kw=kw):
                qrow, krow, edge = _block_place(g, seq_len, kw)
                q = dil[0, pl.ds(qrow, ATT_BQ), :]
                k = dil[1, pl.ds(krow, kw), :]
                v = dil[2, pl.ds(krow, kw), :]
                dov = dil[3, pl.ds(qrow, ATT_BQ), :]
                lv = lse_res[pl.ds(qrow, ATT_BQ), :]
                dlv = dl_res[pl.ds(qrow, ATT_BQ), :]
                dq_parts, dk_parts, dv_parts = [], [], []
                for hh in range(2):
                    hs = slice(hh * HEAD_DIM, (hh + 1) * HEAD_DIM)
                    col = slice(hh * HEAD_DIM, hh * HEAD_DIM + 1)
                    s = lax.dot_general(q[:, hs], k[:, hs], (((1,), (1,)), ((), ())), preferred_element_type=F32)
                    p = jnp.exp(s + bias[hh, edge, :, 0:kw] - lv[:, col])
                    dp = lax.dot_general(dov[:, hs], v[:, hs], (((1,), (1,)), ((), ())), preferred_element_type=F32)
                    ds = p * (dp - dlv[:, col])
                    dv_parts.append(jnp.dot(p.T.astype(BF16), dov[:, hs], preferred_element_type=F32))
                    dk_parts.append(jnp.dot(ds.T.astype(BF16), q[:, hs], preferred_element_type=F32))
                    dq_parts.append(jnp.dot(ds.astype(BF16), k[:, hs], preferred_element_type=F32)
                                    * (HEAD_DIM ** -0.5))
                acc[0, pl.ds(qrow, ATT_BQ), :] = jnp.concatenate(dq_parts, axis=1)
                acc[1, pl.ds(krow, kw), :] += jnp.concatenate(dk_parts, axis=1)
                acc[2, pl.ds(krow, kw), :] += jnp.concatenate(dv_parts, axis=1)
                return carry

            lax.fori_loop(0, n_blocks, block, 0, unroll=ATT_UNROLL)
            for comp in range(3):
                if r == 1:
                    tot[comp] = acc[comp]
                else:
                    for res in range(r):
                        tok = pl.ds(res, seq_len, stride=r)
                        tot[comp, tok, :] = tot[comp, tok, :] + acc[comp, res * seq_len:(res + 1) * seq_len, :]
        for comp in range(3):
            out_ref[comp] = tot[comp].astype(BF16)

    pair_spec = pl.BlockSpec((None, t, PAIR), lambda h: (h, 0, 0))
    return pl.pallas_call(
        body, name="attn_bwd", grid=(hp,),
        out_shape=jax.ShapeDtypeStruct(dattn.shape, BF16),
        in_specs=[HBM, pl.BlockSpec((3, None, t, PAIR), lambda h: (0, h, 0, 0)), pair_spec, pair_spec, pair_spec,
                  pl.BlockSpec((None, 8, ATT_KW), lambda h: (h, 0, 0))],
        out_specs=pl.BlockSpec((3, None, t, PAIR), lambda h: (0, h, 0, 0)),
        input_output_aliases={0: 0},
        scratch_shapes=[pltpu.VMEM((t, PAIR), F32), pltpu.VMEM((4, t, PAIR), BF16),
                        pltpu.VMEM((t, PAIR), F32), pltpu.VMEM((t, PAIR), F32),
                        pltpu.VMEM((2, 3, ATT_BQ, ATT_KW), F32),
                        pltpu.VMEM((3, t, PAIR), F32), pltpu.VMEM((3, t, PAIR), F32)],
        compiler_params=_params(("parallel",)),
    )(dattn, attn_proj, d_o, lse, delta, slopes)


def _mix_fwd(co, conv_proj, attn_proj, o_mix, g_conv, g_attn_pairs):
    t, dc = co.shape
    hp = attn_proj.shape[1]
    da = hp * PAIR
    tb = ROW_TILE

    def body(co_ref, bg_ref, zc_ref, za_ref, om_ref, gc_ref, ga_ref, ycat_ref, ycatt_ref):
        p = bg_ref[...].astype(F32) * co_ref[...]
        rc = lax.rsqrt(jnp.mean(p * p, axis=-1, keepdims=True) + EPS)
        yc = (p * rc) * gc_ref[...] * _silu(zc_ref[...].astype(F32))
        ycat_ref[:, 0:dc] = yc.astype(BF16)
        ycatt_ref[0:dc, :] = yc.T.astype(BF16)
        ssq = jnp.zeros((tb, 1), F32)
        for h in range(hp):
            o = om_ref[h]
            ssq = ssq + jnp.sum(o * o, axis=-1, keepdims=True)
        ra = lax.rsqrt(ssq * (1.0 / da) + EPS)
        for h in range(hp):
            ya = (om_ref[h] * ra) * ga_ref[h] * _silu(za_ref[h].astype(F32))
            ycat_ref[:, dc + h * PAIR:dc + (h + 1) * PAIR] = ya.astype(BF16)
            ycatt_ref[dc + h * PAIR:dc + (h + 1) * PAIR, :] = ya.T.astype(BF16)

    pair_spec = pl.BlockSpec((hp, tb, PAIR), lambda i: (0, i, 0))
    return pl.pallas_call(
        body, name="mix_fwd", grid=(t // tb,),
        out_shape=(jax.ShapeDtypeStruct((t, dc + da), BF16), jax.ShapeDtypeStruct((dc + da, t), BF16)),
        in_specs=[pl.BlockSpec((tb, dc), lambda i: (i, 0)),
                  pl.BlockSpec((tb, dc), lambda i: (i, 1)),
                  pl.BlockSpec((tb, dc), lambda i: (i, 3)),
                  pl.BlockSpec((None, hp, tb, PAIR), lambda i: (3, 0, i, 0)),
                  pair_spec,
                  pl.BlockSpec((1, dc), lambda i: (0, 0)),
                  pl.BlockSpec((hp, 1, PAIR), lambda i: (0, 0, 0))],
        out_specs=(pl.BlockSpec((tb, dc + da), lambda i: (i, 0)), pl.BlockSpec((dc + da, tb), lambda i: (0, i))),
        compiler_params=_params(("parallel",)),
    )(co, conv_proj, conv_proj, attn_proj, o_mix, g_conv, g_attn_pairs)


def _out_fwd_bwd(ycat, woutf, x, target, mod, g_post):
    t, d = x.shape
    n = ycat.shape[1]
    tb = ROW_TILE

    def body(a_ref, w_ref, x_ref, tg_ref, mod_ref, g_ref, dout_ref, dy_ref, acc_ref):
        y = jnp.dot(a_ref[...], w_ref[...], preferred_element_type=F32)
        r = lax.rsqrt(jnp.mean(y * y, axis=-1, keepdims=True) + EPS)
        nh = y * r
        gate = mod_ref[2:3, :]
        nrm = nh * g_ref[...]
        err = x_ref[...] + gate * nrm - tg_ref[...]
        dout = err * (1.0 / d)
        dout_ref[...] = dout
        dn = dout * gate
        a = dn * g_ref[...]
        dy = r * (a - nh * jnp.mean(a * nh, axis=-1, keepdims=True))
        dy_ref[...] = dy.astype(BF16)
        loss = 0.5 * jnp.sum(jnp.sum(err * err, axis=-1, keepdims=True) * (1.0 / d), axis=0, keepdims=True)
        part = jnp.concatenate(
            [jnp.sum(dout * nrm, axis=0, keepdims=True), jnp.sum(dn * nh, axis=0, keepdims=True),
             jnp.broadcast_to(loss, (1, d)), jnp.zeros((5, d), F32)], axis=0)

        @pl.when(pl.program_id(0) == 0)
        def _():
            acc_ref[...] = jnp.zeros(acc_ref.shape, F32)

        acc_ref[...] += part

    return pl.pallas_call(
        body, name="out_fwd_bwd", grid=(t // tb,),
        out_shape=(jax.ShapeDtypeStruct((t, d), F32), jax.ShapeDtypeStruct((t, d), BF16),
                   jax.ShapeDtypeStruct((8, d), F32)),
        in_specs=[pl.BlockSpec((tb, n), lambda i: (i, 0)), pl.BlockSpec((n, d), lambda i: (0, 0)),
                  pl.BlockSpec((tb, d), lambda i: (i, 0)), pl.BlockSpec((tb, d), lambda i: (i, 0)),
                  pl.BlockSpec((3, d), lambda i: (0, 0)), pl.BlockSpec((1, d), lambda i: (0, 0))],
        out_specs=(pl.BlockSpec((tb, d), lambda i: (i, 0)), pl.BlockSpec((tb, d), lambda i: (i, 0)),
                   pl.BlockSpec((8, d), lambda i: (0, 0))),
        compiler_params=_params(("arbitrary",)),
    )(ycat, woutf, x, target, mod, g_post)


def _matmul_nt(a, b, out_dtype, name):
    m, k = a.shape
    n = b.shape[0]
    tn = COL_TILE

    def body(a_ref, b_ref, o_ref):
        o_ref[...] = lax.dot_general(a_ref[...], b_ref[...], (((1,), (1,)), ((), ())),
                                     preferred_element_type=F32).astype(out_dtype)

    return pl.pallas_call(
        body, name=name, grid=(n // tn,),
        out_shape=jax.ShapeDtypeStruct((m, n), out_dtype),
        in_specs=[pl.BlockSpec((m, k), lambda i: (0, 0)), pl.BlockSpec((tn, k), lambda i: (i, 0))],
        out_specs=pl.BlockSpec((m, tn), lambda i: (0, i)),
        compiler_params=_params(("parallel",)),
    )(a, b)


def _matmul_nn(a, b, out_dtype, name):
    m, k = a.shape
    n = b.shape[1]
    tn = COL_TILE

    def body(a_ref, b_ref, o_ref):
        o_ref[...] = jnp.dot(a_ref[...], b_ref[...], preferred_element_type=F32).astype(out_dtype)

    return pl.pallas_call(
        body, name=name, grid=(n // tn,),
        out_shape=jax.ShapeDtypeStruct((m, n), out_dtype),
        in_specs=[pl.BlockSpec((m, k), lambda i: (0, 0)), pl.BlockSpec((k, tn), lambda i: (0, i))],
        out_specs=pl.BlockSpec((m, tn), lambda i: (0, i)),
        compiler_params=_params(("parallel",)),
    )(a, b)


def _mix_bwd(dycat, co, conv_proj, attn_proj, o_mix, g_conv, g_attn_pairs):
    t, dc = co.shape
    hp = attn_proj.shape[1]
    da = hp * PAIR
    tb = ROW_TILE

    def body(dy_ref, co_ref, bg_ref, zc_ref, za_ref, om_ref, gc_ref, ga_ref,
             dcp_ref, dco_ref, dza_ref, do_ref, dl_ref, dgc_ref, dga_ref):
        first = pl.program_id(0) == 0
        cov = co_ref[...]
        bg = bg_ref[...].astype(F32)
        zc = zc_ref[...].astype(F32)
        p = bg * cov
        rc = lax.rsqrt(jnp.mean(p * p, axis=-1, keepdims=True) + EPS)
        nh = p * rc
        dyc = dy_ref[:, 0:dc].astype(F32)
        dn = dyc * _silu(zc)
        a = dn * gc_ref[...]
        dp = rc * (a - nh * jnp.mean(a * nh, axis=-1, keepdims=True))
        dcp_ref[:, 0:dc] = jnp.zeros((tb, dc), BF16)
        dcp_ref[:, dc:2 * dc] = (dp * cov).astype(BF16)
        dcp_ref[:, 2 * dc:3 * dc] = jnp.zeros((tb, dc), BF16)
        dcp_ref[:, 3 * dc:4 * dc] = (dyc * nh * gc_ref[...] * _silu_grad(zc)).astype(BF16)
        dco_ref[...] = dp * bg

        @pl.when(first)
        def _():
            dgc_ref[...] = jnp.zeros(dgc_ref.shape, F32)
            dga_ref[...] = jnp.zeros(dga_ref.shape, F32)

        dgc_ref[...] += jnp.sum(dn * nh, axis=0, keepdims=True)

        ssq = jnp.zeros((tb, 1), F32)
        for h in range(hp):
            o = om_ref[h]
            ssq = ssq + jnp.sum(o * o, axis=-1, keepdims=True)
        ra = lax.rsqrt(ssq * (1.0 / da) + EPS)
        dot_an = jnp.zeros((tb, 1), F32)
        for h in range(hp):
            nha = om_ref[h] * ra
            za = za_ref[h].astype(F32)
            dya = dy_ref[:, dc + h * PAIR:dc + (h + 1) * PAIR].astype(F32)
            dna = dya * _silu(za)
            dza_ref[h] = (dya * nha * ga_ref[h] * _silu_grad(za)).astype(BF16)
            dga_ref[h] += jnp.sum(dna * nha, axis=0, keepdims=True)
            dot_an = dot_an + jnp.sum(dna * ga_ref[h] * nha, axis=-1, keepdims=True)
        mean_an = dot_an * (1.0 / da)
        first_head = lax.broadcasted_iota(jnp.int32, (tb, PAIR), 1) < HEAD_DIM
        for h in range(hp):
            o = om_ref[h]
            nha = o * ra
            za = za_ref[h].astype(F32)
            dya = dy_ref[:, dc + h * PAIR:dc + (h + 1) * PAIR].astype(F32)
            aa = dya * _silu(za) * ga_ref[h]
            d_o = ra * (aa - nha * mean_an)
            do_ref[h] = d_o.astype(BF16)
            prod = d_o * o
            both = jnp.sum(prod, axis=-1, keepdims=True)
            head0 = jnp.sum(jnp.where(first_head, prod, 0.0), axis=-1, keepdims=True)
            dl_ref[h] = jnp.where(first_head, head0, both - head0)

    pair_spec = pl.BlockSpec((hp, tb, PAIR), lambda i: (0, i, 0))
    return pl.pallas_call(
        body, name="mix_bwd", grid=(t // tb,),
        out_shape=(jax.ShapeDtypeStruct((t, 4 * dc), BF16), jax.ShapeDtypeStruct((t, dc), F32),
                   jax.ShapeDtypeStruct((4, hp, t, PAIR), BF16), jax.ShapeDtypeStruct((hp, t, PAIR), BF16),
                   jax.ShapeDtypeStruct((hp, t, PAIR), F32),
                   jax.ShapeDtypeStruct((1, dc), F32), jax.ShapeDtypeStruct((hp, 1, PAIR), F32)),
        in_specs=[pl.BlockSpec((tb, dc + da), lambda i: (i, 0)),
                  pl.BlockSpec((tb, dc), lambda i: (i, 0)),
                  pl.BlockSpec((tb, dc), lambda i: (i, 1)),
                  pl.BlockSpec((tb, dc), lambda i: (i, 3)),
                  pl.BlockSpec((None, hp, tb, PAIR), lambda i: (3, 0, i, 0)),
                  pair_spec,
                  pl.BlockSpec((1, dc), lambda i: (0, 0)),
                  pl.BlockSpec((hp, 1, PAIR), lambda i: (0, 0, 0))],
        out_specs=(pl.BlockSpec((tb, 4 * dc), lambda i: (i, 0)), pl.BlockSpec((tb, dc), lambda i: (i, 0)),
                   pl.BlockSpec((None, hp, tb, PAIR), lambda i: (3, 0, i, 0)), pair_spec, pair_spec,
                   pl.BlockSpec((1, dc), lambda i: (0, 0)), pl.BlockSpec((hp, 1, PAIR), lambda i: (0, 0, 0))),
        compiler_params=_params(("arbitrary",)),
    )(dycat, co, conv_proj, conv_proj, attn_proj, o_mix, g_conv, g_attn_pairs)


def _conv_bwd(dconv_proj, dco, conv_proj, conv_w, dc):
    t = dco.shape[0]
    ct = CONV_TILE
    nct = dc // ct

    def body(dcp_in_ref, dco_ref, u_ref, cg_ref, w_ref, dcp_ref, acc_ref):
        del dcp_in_ref
        which = pl.program_id(1)
        g = dco_ref[...]
        u = u_ref[...].astype(F32)
        cg = cg_ref[...].astype(F32)
        g_prev, g_next = _shift_rows(g, t)
        da = w_ref[0:1, :] * g_next + w_ref[1:2, :] * g + w_ref[2:3, :] * g_prev
        dcp_ref[...] = (da * jnp.where(which == 0, cg, u)).astype(BF16)
        a = cg * u
        a_prev, a_next = _shift_rows(a, t)
        acc_ref[...] = jnp.concatenate(
            [jnp.sum(g * a_prev, axis=0, keepdims=True), jnp.sum(g * a, axis=0, keepdims=True),
             jnp.sum(g * a_next, axis=0, keepdims=True), jnp.sum(g, axis=0, keepdims=True),
             jnp.zeros((4, ct), F32)], axis=0)

    return pl.pallas_call(
        body, name="conv_bwd", grid=(nct, 2),
        out_shape=(jax.ShapeDtypeStruct(dconv_proj.shape, BF16), jax.ShapeDtypeStruct((8, dc), F32)),
        in_specs=[HBM,
                  pl.BlockSpec((t, ct), lambda i, s: (0, i)),
                  pl.BlockSpec((t, ct), lambda i, s: (0, i)),
                  pl.BlockSpec((t, ct), lambda i, s: (0, 2 * nct + i)),
                  pl.BlockSpec((3, ct), lambda i, s: (0, i))],
        out_specs=(pl.BlockSpec((t, ct), lambda i, s: (0, 2 * s * nct + i)),
                   pl.BlockSpec((8, ct), lambda i, s: (0, i))),
        input_output_aliases={0: 0},
        compiler_params=_params(("arbitrary", "arbitrary")),
    )(dconv_proj, dco, conv_proj, conv_proj, conv_w)


def _dw_in(ht, dconv_proj, dattn, ws, da):
    d, t = ht.shape
    tn = COL_TILE
    nt = ws // tn
    per_comp = da // tn
    pairs = tn // PAIR

    def body_conv(a_ref, b_ref, o_ref):
        o_ref[...] = jnp.dot(a_ref[...], b_ref[...], preferred_element_type=F32).astype(BF16)

    gin = pl.pallas_call(
        body_conv, name="dw_in_conv", grid=(2, nt),
        out_shape=jax.ShapeDtypeStruct((N_CHIPS, d, ws), BF16),
        in_specs=[pl.BlockSpec((d, t), lambda j, n: (0, 0)), pl.BlockSpec((t, tn), lambda j, n: (0, j * nt + n))],
        out_specs=pl.BlockSpec((None, d, tn), lambda j, n: (j, 0, n)),
        compiler_params=_params(("parallel", "parallel")),
    )(ht, dconv_proj)

    def body_attn(g_ref, a_ref, b_ref, o_ref):
        del g_ref
        o_ref[...] = jnp.dot(a_ref[...], _pairs_to_cols(b_ref, pairs), preferred_element_type=F32).astype(BF16)

    return pl.pallas_call(
        body_attn, name="dw_in_attn", grid=(2, nt),
        out_shape=jax.ShapeDtypeStruct((N_CHIPS, d, ws), BF16),
        in_specs=[HBM, pl.BlockSpec((d, t), lambda j, n: (0, 0)),
                  pl.BlockSpec((None, pairs, t, PAIR), lambda j, n: (2 * j + n // per_comp, n % per_comp, 0, 0))],
        out_specs=pl.BlockSpec((None, d, tn), lambda j, n: (2 + j, 0, n)),
        input_output_aliases={0: 0},
        compiler_params=_params(("parallel", "parallel")),
    )(gin, ht, dattn)


def _dh(dconv_proj, dattn, winf, da):
    t = dconv_proj.shape[0]
    _, d, ws = winf.shape
    tm = 1024
    tk = COL_TILE
    nk = ws // tk
    per_comp = da // tk
    pairs = tk // PAIR

    def body_conv(a_ref, b_ref, o_ref):
        @pl.when((pl.program_id(1) == 0) & (pl.program_id(2) == 0))
        def _():
            o_ref[...] = jnp.zeros(o_ref.shape, F32)

        o_ref[...] += lax.dot_general(a_ref[...], b_ref[...], (((1,), (1,)), ((), ())), preferred_element_type=F32)

    part = pl.pallas_call(
        body_conv, name="dh_conv", grid=(t // tm, 2, nk),
        out_shape=jax.ShapeDtypeStruct((t, d), F32),
        in_specs=[pl.BlockSpec((tm, tk), lambda m, j, k: (m, j * nk + k)),
                  pl.BlockSpec((None, d, tk), lambda m, j, k: (j, 0, k))],
        out_specs=pl.BlockSpec((tm, d), lambda m, j, k: (m, 0)),
        compiler_params=_params(("parallel", "arbitrary", "arbitrary")),
    )(dconv_proj, winf)

    def body_attn(p_ref, a_ref, b_ref, o_ref):
        @pl.when((pl.program_id(1) == 0) & (pl.program_id(2) == 0))
        def _():
            o_ref[...] = p_ref[...]

        o_ref[...] += lax.dot_general(_pairs_to_cols(a_ref, pairs), b_ref[...], (((1,), (1,)), ((), ())),
                                      preferred_element_type=F32)

    return pl.pallas_call(
        body_attn, name="dh_attn", grid=(t // tm, 2, nk),
        out_shape=jax.ShapeDtypeStruct((t, d), F32),
        in_specs=[pl.BlockSpec((tm, d), lambda m, j, k: (m, 0)),
                  pl.BlockSpec((None, pairs, tm, PAIR), lambda m, j, k: (2 * j + k // per_comp, k % per_comp, m, 0)),
                  pl.BlockSpec((None, d, tk), lambda m, j, k: (2 + j, 0, k))],
        out_specs=pl.BlockSpec((tm, d), lambda m, j, k: (m, 0)),
        compiler_params=_params(("parallel", "arbitrary", "arbitrary")),
    )(part, dattn, winf)


def _prenorm_bwd(x, dh, dout, mod, g_pre):
    t, d = x.shape
    tb = ROW_TILE

    def body(x_ref, dh_ref, dout_ref, mod_ref, g_ref, gx_ref, acc_ref):
        xv = x_ref[...]
        dhv = dh_ref[...]
        r = lax.rsqrt(jnp.mean(xv * xv, axis=-1, keepdims=True) + EPS)
        xh = xv * r
        one_scale = 1.0 + mod_ref[1:2, :]
        a = dhv * one_scale * g_ref[...]
        gx_ref[...] = dout_ref[...] + r * (a - xh * jnp.mean(a * xh, axis=-1, keepdims=True))
        part = jnp.concatenate(
            [jnp.sum(dhv, axis=0, keepdims=True), jnp.sum(dhv * xh * g_ref[...], axis=0, keepdims=True),
             jnp.sum(dhv * xh * one_scale, axis=0, keepdims=True), jnp.zeros((5, d), F32)], axis=0)

        @pl.when(pl.program_id(0) == 0)
        def _():
            acc_ref[...] = jnp.zeros(acc_ref.shape, F32)

        acc_ref[...] += part

    return pl.pallas_call(
        body, name="prenorm_bwd", grid=(t // tb,),
        out_shape=(jax.ShapeDtypeStruct((t, d), F32), jax.ShapeDtypeStruct((8, d), F32)),
        in_specs=[pl.BlockSpec((tb, d), lambda i: (i, 0)), pl.BlockSpec((tb, d), lambda i: (i, 0)),
                  pl.BlockSpec((tb, d), lambda i: (i, 0)), pl.BlockSpec((3, d), lambda i: (0, 0)),
                  pl.BlockSpec((1, d), lambda i: (0, 0))],
        out_specs=(pl.BlockSpec((tb, d), lambda i: (i, 0)), pl.BlockSpec((8, d), lambda i: (0, 0))),
        compiler_params=_params(("arbitrary",)),
    )(x, dh, dout, mod, g_pre)


def _chip_sums(place, g, rsib, name):
    _, rows, cols = g.shape
    half = rows // 2
    tr = min(half, ROW_TILE)
    nt = half // tr

    def body(place_ref, g_ref, r_ref, o_ref):
        del place_ref
        o_ref[...] = (g_ref[...].astype(F32) + r_ref[...].astype(F32)).astype(BF16)

    grid_spec = pltpu.PrefetchScalarGridSpec(
        num_scalar_prefetch=1, grid=(N_CHIPS, nt),
        in_specs=[pl.BlockSpec((None, tr, cols), lambda j, i, p: (j, p[1] * nt + i, 0)),
                  pl.BlockSpec((None, tr, cols), lambda j, i, p: (j, i, 0))],
        out_specs=pl.BlockSpec((None, tr, cols), lambda j, i, p: (j, i, 0)))
    return pl.pallas_call(
        body, name=name, grid_spec=grid_spec,
        out_shape=jax.ShapeDtypeStruct((N_CHIPS, half, cols), BF16),
        compiler_params=_params(("parallel", "parallel")),
    )(place, g, rsib)


def _owner_sum(place, g, rsib, rici, name):
    _, rows, cols = g.shape
    half = rows // 2
    tr = min(half, ROW_TILE)
    nt = half // tr

    def body(place_ref, g_ref, r_ref, i_ref, o_ref):
        del place_ref
        acc = g_ref[...].astype(F32) + r_ref[...].astype(F32)
        for k in range(N_CHIPS - 1):
            acc = acc + i_ref[k].astype(F32)
        o_ref[...] = acc

    grid_spec = pltpu.PrefetchScalarGridSpec(
        num_scalar_prefetch=1, grid=(nt,),
        in_specs=[pl.BlockSpec((None, tr, cols), lambda i, p: (p[0], p[1] * nt + i, 0)),
                  pl.BlockSpec((None, tr, cols), lambda i, p: (p[0], i, 0)),
                  pl.BlockSpec((N_CHIPS - 1, tr, cols), lambda i, p: (0, i, 0))],
        out_specs=pl.BlockSpec((tr, cols), lambda i, p: (p[1] * nt + i, 0)))
    return pl.pallas_call(
        body, name=name, grid_spec=grid_spec,
        out_shape=jax.ShapeDtypeStruct((rows, cols), F32),
        compiler_params=_params(("parallel",)),
    )(place, g, rsib, rici)


def _adam_math(w, g, m, v):
    m2 = ADAM_B1 * m + (1.0 - ADAM_B1) * g
    v2 = ADAM_B2 * v + (1.0 - ADAM_B2) * (g * g)
    m_hat = m2 / (1.0 - ADAM_B1 ** ADAM_STEP)
    v_hat = v2 / (1.0 - ADAM_B2 ** ADAM_STEP)
    delta = -ADAM_LR * (m_hat / (jnp.sqrt(v_hat) + ADAM_EPS) + ADAM_WD * w)
    return delta, m2, v2


def _adamw(w, g, m, v, name):
    rows, cols = w.shape
    tr = min(rows, ROW_TILE)

    def body(w_ref, g_ref, m_ref, v_ref, d_ref, m2_ref, v2_ref):
        d_ref[...], m2_ref[...], v2_ref[...] = _adam_math(w_ref[...], g_ref[...], m_ref[...], v_ref[...])

    spec = pl.BlockSpec((tr, cols), lambda i: (i, 0))
    return pl.pallas_call(
        body, name=name, grid=(rows // tr,),
        out_shape=(jax.ShapeDtypeStruct(w.shape, F32),) * 3,
        in_specs=[spec] * 4, out_specs=(spec,) * 3,
        compiler_params=_params(("parallel",)),
    )(w, g, m, v)


def _ada_grad_adamw(c_all_t, dmod_cols, w, m, v):
    d, wa = w.shape
    tn = 256

    def body(ct_ref, dm_ref, w_ref, m_ref, v_ref, g_ref, d_ref, m2_ref, v2_ref):
        act = _silu(ct_ref[...])
        g = act[:, 0:1] * dm_ref[0:1, :]
        for b in range(1, N_DEV):
            g = g + act[:, b:b + 1] * dm_ref[b:b + 1, :]
        g_ref[...] = g
        d_ref[...], m2_ref[...], v2_ref[...] = _adam_math(w_ref[...], g, m_ref[...], v_ref[...])

    spec = pl.BlockSpec((d, tn), lambda i: (0, i))
    return pl.pallas_call(
        body, name="ada_grad_adamw", grid=(wa // tn,),
        out_shape=(jax.ShapeDtypeStruct(w.shape, F32),) * 4,
        in_specs=[pl.BlockSpec((d, N_DEV), lambda i: (0, 0)), pl.BlockSpec((N_DEV, tn), lambda i: (0, i)),
                  spec, spec, spec],
        out_specs=(spec,) * 4,
        compiler_params=_params(("parallel",)),
    )(c_all_t, dmod_cols, w, m, v)


def _sum_devices(gathered):
    n = gathered.shape[1]

    def body(g_ref, o_ref):
        acc = g_ref[0:8, :]
        for dev in range(1, N_DEV):
            acc = acc + g_ref[8 * dev:8 * dev + 8, :]
        o_ref[...] = acc

    return pl.pallas_call(
        body, name="sum_devices",
        out_shape=jax.ShapeDtypeStruct((8, n), F32),
        in_specs=[VMEM], out_specs=VMEM,
    )(gathered)


def _pack_small(pieces):
    flat = [p.reshape(-1).astype(F32) for p in pieces]
    offsets, total = [], 0
    for p in flat:
        offsets.append(total)
        total += p.shape[0]
    padded = -(-total // SMALL_ALIGN) * SMALL_ALIGN
    if padded > total:
        flat.append(jnp.zeros((padded - total,), F32))
    return jnp.concatenate(flat).reshape(8, padded // 8), offsets


def _alibi_slope_rows(n_heads):
    slopes = 2.0 ** (-8.0 * jnp.arange(1, n_heads + 1, dtype=F32) / n_heads)
    rows = jnp.zeros((n_heads // 2, 8), F32).at[:, 0:2].set(slopes.reshape(n_heads // 2, 2))
    return jnp.broadcast_to(rows[:, :, None], (n_heads // 2, 8, ATT_KW))


def kernel(x, c, w_ada, b_ada, g_pre, w_in, conv_w, conv_b, g_conv, g_attn, w_out, g_post, loss_target, m_w_ada, m_b_ada, m_g_pre, m_w_in, m_conv_w, m_conv_b, m_g_conv, m_g_attn, m_w_out, m_g_post, v_w_ada, v_b_ada, v_g_pre, v_w_in, v_conv_w, v_conv_b, v_g_conv, v_g_attn, v_w_out, v_g_post):
    t, d = x.shape[1], x.shape[2]
    dc = conv_b.shape[1]
    da = g_attn.shape[1]
    hp = da // PAIR
    ws = w_in.shape[2]
    wa = w_ada.shape[2]
    cws = conv_w.shape[2]
    assert t % ROW_TILE == 0 and d % ROW_TILE == 0 and dc % COL_TILE == 0 and da % COL_TILE == 0
    assert ws == 2 * dc and dc == da and t // BRANCHES[-1][1] >= ATT_BQ

    mx, my, mc = _my_place()
    chip = _chip_of(mx, my)
    dev = 2 * chip + mc
    place = jnp.stack([chip, mc]).astype(jnp.int32)

    x2, tgt2 = x[0], loss_target[0]
    w_ada2, w_in2, w_out2 = w_ada[0], w_in[0], w_out[0]

    packed, offs = _pack_small([c[0], conv_w[0]])
    seen = _allgather8(packed, "gather_inputs").reshape(N_DEV, -1)
    c_all = seen[:, offs[0]:offs[0] + d]
    conv_w_full = seen[0::2, offs[1]:offs[1] + 3 * cws].reshape(N_CHIPS, 3, cws).transpose(1, 0, 2).reshape(3, dc)

    ada_part = _ada_partial(c_all, w_ada2)
    ada_seen = _allgather8(ada_part, "gather_ada").reshape(N_DEV, N_DEV, wa)
    mod_flat = lax.dynamic_index_in_dim(ada_seen[0::2], dev, axis=1, keepdims=False).reshape(1, 3 * d) + b_ada
    mod = mod_flat.reshape(3, d)

    winf, woutf4 = _gather_weights(_cast_into_slot(place, w_in2, "cast_w_in"),
                                   _cast_into_slot(place, w_out2, "cast_w_out"))
    woutf = woutf4.reshape(dc + da, d)

    h, ht = _prenorm(x2, mod, g_pre)
    conv_proj = _proj_conv(h, winf)
    attn_proj = _proj_attn(h, winf, da)
    slopes = _alibi_slope_rows(da // HEAD_DIM)
    co = _conv_fwd(conv_proj, conv_w_full, conv_b, dc)
    o_mix, lse = _attn_fwd(attn_proj, slopes)
    g_attn_pairs = g_attn.reshape(hp, 1, PAIR)
    ycat, ycat_t = _mix_fwd(co, conv_proj, attn_proj, o_mix, g_conv, g_attn_pairs)
    dout, dy, post_sums = _out_fwd_bwd(ycat, woutf, x2, tgt2, mod, g_post)

    gout = _matmul_nn(ycat_t, dy, BF16, "dw_out").reshape(N_CHIPS, (dc + da) // N_CHIPS, d)
    dycat = _matmul_nt(dy, woutf, BF16, "dycat")
    dconv_proj, dco, dattn, d_o, delta, dg_conv, dg_attn = _mix_bwd(
        dycat, co, conv_proj, attn_proj, o_mix, g_conv, g_attn_pairs)
    dconv_proj, conv_sums = _conv_bwd(dconv_proj, dco, conv_proj, conv_w_full, dc)
    dattn = _attn_bwd(dattn, attn_proj, d_o, lse, delta, slopes)
    gin = _dw_in(ht, dconv_proj, dattn, ws, da)
    dh = _dh(dconv_proj, dattn, winf, da)
    grad_x, pre_sums = _prenorm_bwd(x2, dh, dout, mod, g_pre)

    rsib_in, rsib_out = _swap_halves(gin, gout)
    csum_in = _chip_sums(place, gin, rsib_in, "rs_chip_sum_in")
    csum_out = _chip_sums(place, gout, rsib_out, "rs_chip_sum_out")
    rici_in, rici_out = _send_to_owners(csum_in, csum_out)
    half_in = _owner_sum(place, gin, rsib_in, rici_in, "rs_owner_sum_in")
    half_out = _owner_sum(place, gout, rsib_out, rici_out, "rs_owner_sum_out")
    grad_w_in, grad_w_out = _join_halves(half_in, half_out)

    small, so = _pack_small([
        pre_sums[0], pre_sums[1], post_sums[0],
        pre_sums[2], conv_sums[0:3], conv_sums[3], dg_conv, dg_attn, post_sums[1], post_sums[2, 0:128]])
    small_seen = _allgather8(small, "gather_small")
    total = _sum_devices(small_seen).reshape(-1)
    dmod_all = small_seen.reshape(N_DEV, -1)[:, 0:3 * d]
    loss = total[so[9]]
    grad_b_ada = total[0:3 * d].reshape(1, 3 * d)
    grad_g_pre = total[so[3]:so[3] + d].reshape(1, d)
    grad_conv_w_full = total[so[4]:so[4] + 3 * dc].reshape(3, dc)
    grad_conv_w = lax.dynamic_slice_in_dim(grad_conv_w_full, chip * cws, cws, axis=1).reshape(1, 3, cws)
    grad_conv_b = total[so[5]:so[5] + dc].reshape(1, dc)
    grad_g_conv = total[so[6]:so[6] + dc].reshape(1, dc)
    grad_g_attn = total[so[7]:so[7] + da].reshape(1, da)
    grad_g_post = total[so[8]:so[8] + d].reshape(1, d)

    dmod_cols = lax.dynamic_slice_in_dim(dmod_all, chip * wa, wa, axis=1)
    grad_w_ada, delta_w_ada, new_m_w_ada, new_v_w_ada = _ada_grad_adamw(c_all.T, dmod_cols, w_ada2, m_w_ada[0], v_w_ada[0])
    delta_w_in, new_m_w_in, new_v_w_in = _adamw(w_in2, grad_w_in, m_w_in[0], v_w_in[0], "adamw_w_in")
    delta_w_out, new_m_w_out, new_v_w_out = _adamw(w_out2, grad_w_out, m_w_out[0], v_w_out[0], "adamw_w_out")

    small_w = [b_ada, g_pre, conv_w, conv_b, g_conv, g_attn, g_post]
    small_g = [grad_b_ada, grad_g_pre, grad_conv_w, grad_conv_b, grad_g_conv, grad_g_attn, grad_g_post]
    small_m = [m_b_ada, m_g_pre, m_conv_w, m_conv_b, m_g_conv, m_g_attn, m_g_post]
    small_v = [v_b_ada, v_g_pre, v_conv_w, v_conv_b, v_g_conv, v_g_attn, v_g_post]
    pw, po = _pack_small(small_w)
    pg, _ = _pack_small(small_g)
    pm, _ = _pack_small(small_m)
    pv, _ = _pack_small(small_v)
    sd, sm, sv = (a.reshape(-1) for a in _adamw(pw, pg, pm, pv, "adamw_small"))

    def unpack(flat):
        return [flat[o:o + w.size].reshape(w.shape) for o, w in zip(po, small_w)]

    d_small, m_small, v_small = unpack(sd), unpack(sm), unpack(sv)

    def lead(a):
        return a.reshape((1,) + a.shape)

    grads = [lead(grad_w_ada), grad_b_ada, grad_g_pre, lead(grad_w_in), grad_conv_w, grad_conv_b, grad_g_conv,
             grad_g_attn, lead(grad_w_out), grad_g_post]
    deltas = [lead(delta_w_ada), d_small[0], d_small[1], lead(delta_w_in), d_small[2], d_small[3], d_small[4],
              d_small[5], lead(delta_w_out), d_small[6]]
    new_ms = [lead(new_m_w_ada), m_small[0], m_small[1], lead(new_m_w_in), m_small[2], m_small[3], m_small[4],
              m_small[5], lead(new_m_w_out), m_small[6]]
    new_vs = [lead(new_v_w_ada), v_small[0], v_small[1], lead(new_v_w_in), v_small[2], v_small[3], v_small[4],
              v_small[5], lead(new_v_w_out), v_small[6]]
    return (loss, lead(grad_x), *grads, *deltas, *new_ms, *new_vs)
```

```python
import jax
import jax.numpy as jnp
from jax import lax
from jax.experimental import pallas as pl
from jax.experimental.pallas import tpu as pltpu

F32 = jnp.float32
BF16 = jnp.bfloat16
MESH = pl.DeviceIdType.MESH
HBM = pl.BlockSpec(memory_space=pltpu.HBM)
VMEM = pl.BlockSpec(memory_space=pltpu.VMEM)

HEAD_DIM = 64
PAIR = 2 * HEAD_DIM
BRANCHES = ((128, 1), (512, 4), (2048, 16))
SIDE = 64
EPS = 1e-6
NEG_INF = -1e30
N_CHIPS = 4
N_DEV = 8

ADAM_LR = 0.001
ADAM_B1 = 0.9
ADAM_B2 = 0.999
ADAM_EPS = 1e-08
ADAM_WD = 0.01
ADAM_STEP = 10

VMEM_LIMIT_BYTES = 56 * 1024 * 1024
ROW_TILE = 256
COL_TILE = 512
CONV_TILE = 256
ATT_BQ = 128
ATT_KW = ATT_BQ + 2 * SIDE
ATT_UNROLL = 4
ATT_UNROLL_BWD = 2
SMALL_ALIGN = 1024


def _params(semantics=None):
    kw = {"vmem_limit_bytes": VMEM_LIMIT_BYTES}
    if semantics is not None:
        kw["dimension_semantics"] = semantics
    return pltpu.CompilerParams(**kw)


def _silu(z):
    return z * jax.nn.sigmoid(z)


def _silu_grad(z):
    s = jax.nn.sigmoid(z)
    return s * (1.0 + z * (1.0 - s))


def _my_place():
    return lax.axis_index("x"), lax.axis_index("y"), lax.axis_index("c")


def _flip(a, bit):
    return 1 - a if bit else a


def _chip_of(x, y):
    return 2 * x + y


def _allgather8(v, name):
    rows_per, n = v.shape

    def body(v_ref, out_ref, send_sems, recv_sems):
        x, y, c = _my_place()
        me = 4 * x + 2 * y + c

        def rows(idx):
            return out_ref.at[pl.ds(pl.multiple_of(idx * rows_per, rows_per), rows_per), :]

        out_ref[pl.ds(pl.multiple_of(me * rows_per, rows_per), rows_per), :] = v_ref[...]
        copies = []
        for k in range(1, N_DEV):
            peer = (_flip(x, k & 4), _flip(y, k & 2), _flip(c, k & 1))
            cp = pltpu.make_async_remote_copy(
                src_ref=v_ref, dst_ref=rows(me), send_sem=send_sems.at[k - 1], recv_sem=recv_sems.at[k - 1],
                device_id=peer, device_id_type=MESH)
            cp.start()
            copies.append((cp, peer))
        for k, (cp, peer) in enumerate(copies):
            src = 4 * peer[0] + 2 * peer[1] + peer[2]
            pltpu.make_async_remote_copy(
                src_ref=v_ref, dst_ref=rows(src), send_sem=send_sems.at[k], recv_sem=recv_sems.at[k],
                device_id=peer, device_id_type=MESH).wait_recv()
        for cp, _ in copies:
            cp.wait_send()

    return pl.pallas_call(
        body, name=name,
        out_shape=jax.ShapeDtypeStruct((N_DEV * rows_per, n), v.dtype),
        in_specs=[VMEM], out_specs=VMEM,
        scratch_shapes=[pltpu.SemaphoreType.DMA((N_DEV - 1,)), pltpu.SemaphoreType.DMA((N_DEV - 1,))],
    )(v)


def _gather_weights(win_slots, wout_slots):
    slots = (win_slots, wout_slots)
    halves = tuple(s.shape[1] // 2 for s in slots)

    def body(win_in, wout_in, winf_ref, woutf_ref, send_sems, recv_sems, fwd_send, fwd_recv):
        del win_in, wout_in
        x, y, c = _my_place()
        me = _chip_of(x, y)
        bufs = (winf_ref, woutf_ref)

        def half(t, chip, which):
            return bufs[t].at[chip, pl.ds(pl.multiple_of(which * halves[t], halves[t]), halves[t]), :]

        started = []
        for k in (1, 2, 3):
            peer = (_flip(x, k & 2), _flip(y, k & 1), c)
            for t in range(2):
                idx = 2 * (k - 1) + t
                cp = pltpu.make_async_remote_copy(
                    src_ref=half(t, me, c), dst_ref=half(t, me, c),
                    send_sem=send_sems.at[idx], recv_sem=recv_sems.at[idx], device_id=peer, device_id_type=MESH)
                cp.start()
                started.append(cp)
        sibling = (x, y, 1 - c)
        for k in (1, 2, 3):
            peer = (_flip(x, k & 2), _flip(y, k & 1), c)
            src_chip = _chip_of(peer[0], peer[1])
            for t in range(2):
                idx = 2 * (k - 1) + t
                landed = half(t, src_chip, c)
                pltpu.make_async_remote_copy(
                    src_ref=landed, dst_ref=landed, send_sem=send_sems.at[idx], recv_sem=recv_sems.at[idx],
                    device_id=peer, device_id_type=MESH).wait_recv()
                fw = pltpu.make_async_remote_copy(
                    src_ref=landed, dst_ref=landed, send_sem=fwd_send.at[idx], recv_sem=fwd_recv.at[idx],
                    device_id=sibling, device_id_type=MESH)
                fw.start()
                started.append(fw)
        for k in (1, 2, 3):
            src_chip = _chip_of(_flip(x, k & 2), _flip(y, k & 1))
            for t in range(2):
                idx = 2 * (k - 1) + t
                other = half(t, src_chip, 1 - c)
                pltpu.make_async_remote_copy(
                    src_ref=other, dst_ref=other, send_sem=fwd_send.at[idx], recv_sem=fwd_recv.at[idx],
                    device_id=sibling, device_id_type=MESH).wait_recv()
        for cp in started:
            cp.wait_send()

    return pl.pallas_call(
        body, name="gather_weights",
        out_shape=tuple(jax.ShapeDtypeStruct(s.shape, s.dtype) for s in slots),
        in_specs=[HBM, HBM], out_specs=(HBM, HBM),
        input_output_aliases={0: 0, 1: 1},
        scratch_shapes=[pltpu.SemaphoreType.DMA((6,))] * 4,
    )(win_slots, wout_slots)


def _swap_halves(gin, gout):
    grads = (gin, gout)
    halves = tuple(g.shape[1] // 2 for g in grads)

    def body(gin_ref, gout_ref, rin_ref, rout_ref, send_sems, recv_sems):
        x, y, c = _my_place()
        sibling = (x, y, 1 - c)
        srcs = (gin_ref, gout_ref)
        dsts = (rin_ref, rout_ref)
        copies = []
        for t in range(2):
            theirs = srcs[t].at[:, pl.ds(pl.multiple_of((1 - c) * halves[t], halves[t]), halves[t]), :]
            cp = pltpu.make_async_remote_copy(
                src_ref=theirs, dst_ref=dsts[t], send_sem=send_sems.at[t], recv_sem=recv_sems.at[t],
                device_id=sibling, device_id_type=MESH)
            cp.start()
            copies.append(cp)
        for cp in copies:
            cp.wait()

    return pl.pallas_call(
        body, name="rs_swap_halves",
        out_shape=tuple(jax.ShapeDtypeStruct((N_CHIPS, g.shape[1] // 2, g.shape[2]), g.dtype) for g in grads),
        in_specs=[HBM, HBM], out_specs=(HBM, HBM),
        scratch_shapes=[pltpu.SemaphoreType.DMA((2,)), pltpu.SemaphoreType.DMA((2,))],
    )(gin, gout)


def _send_to_owners(cin, cout):
    sums = (cin, cout)

    def body(cin_ref, cout_ref, rin_ref, rout_ref, send_sems, recv_sems):
        x, y, c = _my_place()
        srcs = (cin_ref, cout_ref)
        dsts = (rin_ref, rout_ref)
        copies = []
        for k in (1, 2, 3):
            peer = (_flip(x, k & 2), _flip(y, k & 1), c)
            owner = _chip_of(peer[0], peer[1])
            for t in range(2):
                idx = 2 * (k - 1) + t
                cp = pltpu.make_async_remote_copy(
                    src_ref=srcs[t].at[owner], dst_ref=dsts[t].at[k - 1],
                    send_sem=send_sems.at[idx], recv_sem=recv_sems.at[idx], device_id=peer, device_id_type=MESH)
                cp.start()
                copies.append(cp)
        for cp in copies:
            cp.wait()

    return pl.pallas_call(
        body, name="rs_send_to_owners",
        out_shape=tuple(jax.ShapeDtypeStruct((N_CHIPS - 1,) + s.shape[1:], s.dtype) for s in sums),
        in_specs=[HBM, HBM], out_specs=(HBM, HBM),
        scratch_shapes=[pltpu.SemaphoreType.DMA((6,)), pltpu.SemaphoreType.DMA((6,))],
    )(cin, cout)


def _join_halves(fin, fout):
    fulls = (fin, fout)

    def body(fin_in, fout_in, fin_ref, fout_ref, send_sems, recv_sems):
        del fin_in, fout_in
        x, y, c = _my_place()
        sibling = (x, y, 1 - c)
        bufs = (fin_ref, fout_ref)
        copies = []
        for t in range(2):
            rows = fulls[t].shape[0] // 2
            mine = bufs[t].at[pl.ds(pl.multiple_of(c * rows, rows), rows), :]
            cp = pltpu.make_async_remote_copy(
                src_ref=mine, dst_ref=mine, send_sem=send_sems.at[t], recv_sem=recv_sems.at[t],
                device_id=sibling, device_id_type=MESH)
            cp.start()
            copies.append((cp, rows))
        for t, (cp, rows) in enumerate(copies):
            theirs = bufs[t].at[pl.ds(pl.multiple_of((1 - c) * rows, rows), rows), :]
            pltpu.make_async_remote_copy(
                src_ref=theirs, dst_ref=theirs, send_sem=send_sems.at[t], recv_sem=recv_sems.at[t],
                device_id=sibling, device_id_type=MESH).wait_recv()
            cp.wait_send()

    return pl.pallas_call(
        body, name="rs_join_halves",
        out_shape=tuple(jax.ShapeDtypeStruct(f.shape, f.dtype) for f in fulls),
        in_specs=[HBM, HBM], out_specs=(HBM, HBM),
        input_output_aliases={0: 0, 1: 1},
        scratch_shapes=[pltpu.SemaphoreType.DMA((2,))] * 2,
    )(fin, fout)


def _cast_into_slot(place, w, name):
    rows, cols = w.shape
    tr = min(rows, ROW_TILE)

    def body(place_ref, w_ref, o_ref):
        del place_ref
        o_ref[...] = w_ref[...].astype(BF16)

    grid_spec = pltpu.PrefetchScalarGridSpec(
        num_scalar_prefetch=1, grid=(rows // tr,),
        in_specs=[pl.BlockSpec((tr, cols), lambda i, p: (i, 0))],
        out_specs=pl.BlockSpec((None, tr, cols), lambda i, p: (p[0], i, 0)))
    return pl.pallas_call(
        body, name=name, grid_spec=grid_spec,
        out_shape=jax.ShapeDtypeStruct((N_CHIPS, rows, cols), BF16),
        compiler_params=_params(("parallel",)),
    )(place, w)


def _ada_partial(c_all, w_ada):
    d_model, wa = w_ada.shape
    tn = 512 if wa % 512 == 0 else 256

    def body(c_ref, w_ref, o_ref):
        o_ref[...] = jnp.dot(_silu(c_ref[...]), w_ref[...], precision=lax.Precision.HIGHEST,
                             preferred_element_type=F32)

    return pl.pallas_call(
        body, name="ada_partial", grid=(wa // tn,),
        out_shape=jax.ShapeDtypeStruct((N_DEV, wa), F32),
        in_specs=[pl.BlockSpec((N_DEV, d_model), lambda i: (0, 0)), pl.BlockSpec((d_model, tn), lambda i: (0, i))],
        out_specs=pl.BlockSpec((N_DEV, tn), lambda i: (0, i)),
        compiler_params=_params(("parallel",)),
    )(c_all, w_ada)


def _prenorm(x, mod, g_pre):
    t, d = x.shape
    tb = ROW_TILE

    def body(x_ref, mod_ref, g_ref, h_ref, ht_ref):
        xv = x_ref[...]
        r = lax.rsqrt(jnp.mean(xv * xv, axis=-1, keepdims=True) + EPS)
        h = (xv * r) * g_ref[...] * (1.0 + mod_ref[1:2, :]) + mod_ref[0:1, :]
        h_ref[...] = h.astype(BF16)
        ht_ref[...] = h.T.astype(BF16)

    return pl.pallas_call(
        body, name="prenorm", grid=(t // tb,),
        out_shape=(jax.ShapeDtypeStruct((t, d), BF16), jax.ShapeDtypeStruct((d, t), BF16)),
        in_specs=[pl.BlockSpec((tb, d), lambda i: (i, 0)), pl.BlockSpec((3, d), lambda i: (0, 0)),
                  pl.BlockSpec((1, d), lambda i: (0, 0))],
        out_specs=(pl.BlockSpec((tb, d), lambda i: (i, 0)), pl.BlockSpec((d, tb), lambda i: (0, i))),
        compiler_params=_params(("parallel",)),
    )(x, mod, g_pre)


def _pairs_to_cols(ref, n_pairs):
    return jnp.concatenate([ref[i] for i in range(n_pairs)], axis=1)


def _proj_conv(h, winf):
    t, d = h.shape
    ws = winf.shape[2]
    tn = COL_TILE
    nt = ws // tn

    def body(a_ref, b_ref, o_ref):
        o_ref[...] = jnp.dot(a_ref[...], b_ref[...], preferred_element_type=F32).astype(BF16)

    return pl.pallas_call(
        body, name="proj_conv", grid=(2, nt),
        out_shape=jax.ShapeDtypeStruct((t, 2 * ws), BF16),
        in_specs=[pl.BlockSpec((t, d), lambda j, n: (0, 0)), pl.BlockSpec((None, d, tn), lambda j, n: (j, 0, n))],
        out_specs=pl.BlockSpec((t, tn), lambda j, n: (0, j * nt + n)),
        compiler_params=_params(("parallel", "parallel")),
    )(h, winf)


def _proj_attn(h, winf, da):
    t, d = h.shape
    ws = winf.shape[2]
    tn = COL_TILE
    nt = ws // tn
    per_comp = da // tn
    pairs = tn // PAIR

    def body(a_ref, b_ref, o_ref):
        res = jnp.dot(a_ref[...], b_ref[...], preferred_element_type=F32).astype(BF16)
        for i in range(pairs):
            o_ref[i] = res[:, i * PAIR:(i + 1) * PAIR]

    return pl.pallas_call(
        body, name="proj_attn", grid=(2, nt),
        out_shape=jax.ShapeDtypeStruct((4, da // PAIR, t, PAIR), BF16),
        in_specs=[pl.BlockSpec((t, d), lambda j, n: (0, 0)), pl.BlockSpec((None, d, tn), lambda j, n: (2 + j, 0, n))],
        out_specs=pl.BlockSpec((None, pairs, t, PAIR), lambda j, n: (2 * j + n // per_comp, n % per_comp, 0, 0)),
        compiler_params=_params(("parallel", "parallel")),
    )(h, winf)


def _shift_rows(a, rows):
    idx = lax.broadcasted_iota(jnp.int32, a.shape, 0)
    prev = jnp.where(idx == 0, 0.0, pltpu.roll(a, 1, 0))
    nxt = jnp.where(idx == rows - 1, 0.0, pltpu.roll(a, rows - 1, 0))
    return prev, nxt


def _conv_fwd(conv_proj, conv_w, conv_b, dc):
    t = conv_proj.shape[0]
    ct = CONV_TILE
    nct = dc // ct

    def body(u_ref, cg_ref, w_ref, b_ref, co_ref):
        a = cg_ref[...].astype(F32) * u_ref[...].astype(F32)
        prev, nxt = _shift_rows(a, t)
        co_ref[...] = w_ref[0:1, :] * prev + w_ref[1:2, :] * a + w_ref[2:3, :] * nxt + b_ref[...]

    return pl.pallas_call(
        body, name="conv_fwd", grid=(nct,),
        out_shape=jax.ShapeDtypeStruct((t, dc), F32),
        in_specs=[pl.BlockSpec((t, ct), lambda i: (0, i)), pl.BlockSpec((t, ct), lambda i: (0, 2 * nct + i)),
                  pl.BlockSpec((3, ct), lambda i: (0, i)), pl.BlockSpec((1, ct), lambda i: (0, i))],
        out_specs=pl.BlockSpec((t, ct), lambda i: (0, i)),
        compiler_params=_params(("parallel",)),
    )(conv_proj, conv_proj, conv_w, conv_b)


def _to_residue_major(src_ref, dst_ref, r, scale=None):
    t = src_ref.shape[0]
    seq = t // r
    for res in range(r):
        rows = src_ref[pl.ds(res, seq, stride=r), :] if r > 1 else src_ref[...]
        if scale is not None:
            rows = rows * scale
        dst_ref[res * seq:(res + 1) * seq, :] = rows.astype(dst_ref.dtype)


def _fill_bias_tiles(bias_ref, sl_ref, r, kw, transposed=False):
    shape = (kw, ATT_BQ) if transposed else (ATT_BQ, kw)
    key_axis, query_axis = (0, 1) if transposed else (1, 0)
    base = lax.broadcasted_iota(jnp.int32, shape, key_axis) - lax.broadcasted_iota(jnp.int32, shape, query_axis)
    for hh in range(2):
        slope = -(sl_ref[hh:hh + 1, 0:shape[1]] * float(r))
        for e, shift in enumerate((0, -SIDE, ATT_BQ - kw)):
            arel = jnp.abs(base + shift)
            bias_ref[hh, e, 0:shape[0], 0:shape[1]] = jnp.where(arel <= SIDE, arel.astype(F32) * slope, NEG_INF)


def _first_head_lanes():
    return lax.broadcasted_iota(jnp.int32, (1, PAIR), 1) < HEAD_DIM


def _only_head(x, first, hh):
    return jnp.where(first if hh == 0 else jnp.logical_not(first), x, jnp.zeros_like(x))


def _all_lanes(x, first, hh):
    other = pltpu.roll(x, HEAD_DIM, 1)
    return jnp.where(first, x, other) if hh == 0 else jnp.where(first, other, x)


def _block_place(g, seq_len, kw):
    nqb = seq_len // ATT_BQ
    if nqb == 1:
        row = pl.multiple_of(g * ATT_BQ, ATT_BQ)
        return row, row, 0
    res = g // nqb
    qb = g - res * nqb
    q0 = qb * ATT_BQ
    ks = jnp.clip(q0 - SIDE, 0, seq_len - kw)
    edge = jnp.where(qb == 0, 0, jnp.where(qb == nqb - 1, 2, 1))
    return (pl.multiple_of(res * seq_len + q0, ATT_BQ), pl.multiple_of(res * seq_len + ks, SIDE), edge)


def _attn_fwd(attn_proj, slopes):
    _, hp, t, _ = attn_proj.shape
    n_blocks = t // ATT_BQ

    def body(qkv_ref, sl_ref, o_ref, lse_ref, stage, dil, bias, o_res, l_res, o_tok, l_tok):
        for b, (_, r) in enumerate(BRANCHES):
            seq_len = t // r
            kw = min(ATT_KW, seq_len)
            for comp in range(3):
                stage[...] = qkv_ref[comp].astype(F32)
                _to_residue_major(stage, dil.at[comp], r, scale=HEAD_DIM ** -0.5 if comp == 0 else None)
            _fill_bias_tiles(bias, sl_ref, r, kw)
            o_dst, l_dst = (o_tok.at[b], l_tok.at[b]) if r == 1 else (o_res, l_res)
            first = _first_head_lanes()

            def block(g, carry, seq_len=seq_len, kw=kw, o_dst=o_dst, l_dst=l_dst, first=first):
                qrow, krow, edge = _block_place(g, seq_len, kw)
                q = dil[0, pl.ds(qrow, ATT_BQ), :]
                k = dil[1, pl.ds(krow, kw), :]
                v = dil[2, pl.ds(krow, kw), :]
                ones = jnp.ones((kw, PAIR), BF16)
                both, tops = None, []
                for hh in range(2):
                    s = lax.dot_general(_only_head(q, first, hh), k, (((1,), (1,)), ((), ())),
                                        preferred_element_type=F32)
                    s = s + bias[hh, edge, :, 0:kw]
                    m = jnp.max(s, axis=-1, keepdims=True)
                    p = jnp.exp(s - m).astype(BF16)
                    rhs = jnp.concatenate([_only_head(v, first, hh), _only_head(ones, first, hh)], axis=1)
                    part = jnp.dot(p, rhs, preferred_element_type=F32)
                    both = part if both is None else both + part
                    tops.append(m)
                den = both[:, PAIR:]
                o_dst[pl.ds(qrow, ATT_BQ), :] = both[:, 0:PAIR] / den
                l_dst[pl.ds(qrow, ATT_BQ), :] = jnp.where(first, tops[0], tops[1]) + jnp.log(den)
                return carry

            lax.fori_loop(0, n_blocks, block, 0, unroll=ATT_UNROLL)
            if r > 1:
                for res in range(r):
                    rows = slice(res * seq_len, (res + 1) * seq_len)
                    o_tok[b, pl.ds(res, seq_len, stride=r), :] = o_res[rows, :]
                    l_tok[b, pl.ds(res, seq_len, stride=r), :] = l_res[rows, :]

        def merge(i, carry):
            rows = pl.ds(pl.multiple_of(i * ROW_TILE, ROW_TILE), ROW_TILE)
            la, lb, lc = l_tok[0, rows, :], l_tok[1, rows, :], l_tok[2, rows, :]
            m = jnp.maximum(jnp.maximum(la, lb), lc)
            wa, wb, wc = jnp.exp(la - m), jnp.exp(lb - m), jnp.exp(lc - m)
            den = wa + wb + wc
            o_ref[rows, :] = (wa * o_tok[0, rows, :] + wb * o_tok[1, rows, :] + wc * o_tok[2, rows, :]) * (1.0 / den)
            lse_ref[rows, :] = m + jnp.log(den)
            return carry

        lax.fori_loop(0, t // ROW_TILE, merge, 0)

    pair_spec = pl.BlockSpec((None, t, PAIR), lambda h: (h, 0, 0))
    return pl.pallas_call(
        body, name="attn_fwd", grid=(hp,),
        out_shape=(jax.ShapeDtypeStruct((hp, t, PAIR), F32), jax.ShapeDtypeStruct((hp, t, PAIR), F32)),
        in_specs=[pl.BlockSpec((3, None, t, PAIR), lambda h: (0, h, 0, 0)),
                  pl.BlockSpec((None, 8, ATT_KW), lambda h: (h, 0, 0))],
        out_specs=(pair_spec, pair_spec),
        scratch_shapes=[pltpu.VMEM((t, PAIR), F32), pltpu.VMEM((3, t, PAIR), BF16),
                        pltpu.VMEM((2, 3, ATT_BQ, ATT_KW), F32),
                        pltpu.VMEM((t, PAIR), F32), pltpu.VMEM((t, PAIR), F32),
                        pltpu.VMEM((3, t, PAIR), F32), pltpu.VMEM((3, t, PAIR), F32)],
        compiler_params=_params(("parallel",)),
    )(attn_proj, slopes)


def _attn_bwd(dattn, attn_proj, d_o, lse, delta, slopes):
    _, hp, t, _ = attn_proj.shape
    n_blocks = t // ATT_BQ

    def body(dattn_in, qkv_ref, do_ref, lse_ref, dl_ref, sl_ref, out_ref,
             stage, dil, lse_res, dl_res, lse_row, dl_row, bias, bias_t, acc, tot):
        del dattn_in
        for b, (_, r) in enumerate(BRANCHES):
            seq_len = t // r
            kw = min(ATT_KW, seq_len)
            for comp in range(3):
                stage[...] = qkv_ref[comp].astype(F32)
                _to_residue_major(stage, dil.at[comp], r, scale=HEAD_DIM ** -0.5 if comp == 0 else None)
            stage[...] = do_ref[...].astype(F32)
            _to_residue_major(stage, dil.at[3], r)
            _to_residue_major(lse_ref, lse_res, r)
            _to_residue_major(dl_ref, dl_res, r)
            for g in range(n_blocks):
                rows = slice(g * ATT_BQ, (g + 1) * ATT_BQ)
                for src, dst in ((lse_res, lse_row), (dl_res, dl_row)):
                    flipped = src[rows, :].T
                    dst[g, 0:1, :] = flipped[0:1, :]
                    dst[g, 1:2, :] = flipped[HEAD_DIM:HEAD_DIM + 1, :]
            _fill_bias_tiles(bias, sl_ref, r, kw)
            _fill_bias_tiles(bias_t, sl_ref, r, kw, transposed=True)
            acc[1] = jnp.zeros((t, PAIR), F32)
            acc[2] = jnp.zeros((t, PAIR), F32)
            first = _first_head_lanes()

            def block(g, carry, seq_len=seq_len, kw=kw, first=first):
                qrow, krow, edge = _block_place(g, seq_len, kw)
                nt = (((1,), (1,)), ((), ()))
                q = dil[0, pl.ds(qrow, ATT_BQ), :]
                k = dil[1, pl.ds(krow, kw), :]
                v = dil[2, pl.ds(krow, kw), :]
                dov = dil[3, pl.ds(qrow, ATT_BQ), :]
                lv = lse_res[pl.ds(qrow, ATT_BQ), :]
                dlv = dl_res[pl.ds(qrow, ATT_BQ), :]
                dq = dk = dv = None
                for hh in range(2):
                    qm, km = _only_head(q, first, hh), _only_head(k, first, hh)
                    vm, dom = _only_head(v, first, hh), _only_head(dov, first, hh)
                    lse_cols = jnp.concatenate([_all_lanes(lv, first, hh)] * (kw // PAIR), axis=1)
                    dl_cols = jnp.concatenate([_all_lanes(dlv, first, hh)] * (kw // PAIR), axis=1)
                    s = lax.dot_general(qm, k, nt, preferred_element_type=F32)
                    p = jnp.exp(s + bias[hh, edge, :, 0:kw] - lse_cols)
                    dp = lax.dot_general(dom, v, nt, preferred_element_type=F32)
                    ds = (p * (dp - dl_cols)).astype(BF16)
                    s_t = lax.dot_general(km, q, nt, preferred_element_type=F32)
                    p_t = jnp.exp(s_t + bias_t[hh, edge, 0:kw, :] - lse_row[g, hh:hh + 1, :])
                    dp_t = lax.dot_general(vm, dov, nt, preferred_element_type=F32)
                    ds_t = (p_t * (dp_t - dl_row[g, hh:hh + 1, :])).astype(BF16)
                    dq_h = jnp.dot(ds, km, preferred_element_type=F32)
                    dk_h = jnp.dot(ds_t, qm, preferred_element_type=F32)
                    dv_h = jnp.dot(p_t.astype(BF16), dom, preferred_element_type=F32)
                    dq, dk, dv = (dq_h, dk_h, dv_h) if dq is None else (dq + dq_h, dk + dk_h, dv + dv_h)
                acc[0, pl.ds(qrow, ATT_BQ), :] = dq * (HEAD_DIM ** -0.5)
                acc[1, pl.ds(krow, kw), :] += dk
                acc[2, pl.ds(krow, kw), :] += dv
                return carry

            lax.fori_loop(0, n_blocks, block, 0, unroll=ATT_UNROLL_BWD)
            for comp in range(3):
                if r == 1:
                    tot[comp] = acc[comp]
                else:
                    for res in range(r):
                        tok = pl.ds(res, seq_len, stride=r)
                        tot[comp, tok, :] = tot[comp, tok, :] + acc[comp, res * seq_len:(res + 1) * seq_len, :]
        for comp in range(3):
            out_ref[comp] = tot[comp].astype(BF16)

    pair_spec = pl.BlockSpec((None, t, PAIR), lambda h: (h, 0, 0))
    return pl.pallas_call(
        body, name="attn_bwd", grid=(hp,),
        out_shape=jax.ShapeDtypeStruct(dattn.shape, BF16),
        in_specs=[HBM, pl.BlockSpec((3, None, t, PAIR), lambda h: (0, h, 0, 0)), pair_spec, pair_spec, pair_spec,
                  pl.BlockSpec((None, 8, ATT_KW), lambda h: (h, 0, 0))],
        out_specs=pl.BlockSpec((3, None, t, PAIR), lambda h: (0, h, 0, 0)),
        input_output_aliases={0: 0},
        scratch_shapes=[pltpu.VMEM((t, PAIR), F32), pltpu.VMEM((4, t, PAIR), BF16),
                        pltpu.VMEM((t, PAIR), F32), pltpu.VMEM((t, PAIR), F32),
                        pltpu.VMEM((n_blocks, 8, ATT_BQ), F32), pltpu.VMEM((n_blocks, 8, ATT_BQ), F32),
                        pltpu.VMEM((2, 3, ATT_BQ, ATT_KW), F32), pltpu.VMEM((2, 3, ATT_KW, ATT_BQ), F32),
                        pltpu.VMEM((3, t, PAIR), F32), pltpu.VMEM((3, t, PAIR), F32)],
        compiler_params=_params(("parallel",)),
    )(dattn, attn_proj, d_o, lse, delta, slopes)


def _mix_fwd(co, conv_proj, attn_proj, o_mix, g_conv, g_attn_pairs):
    t, dc = co.shape
    hp = attn_proj.shape[1]
    da = hp * PAIR
    tb = ROW_TILE

    def body(co_ref, bg_ref, zc_ref, za_ref, om_ref, gc_ref, ga_ref, ycat_ref, ycatt_ref):
        p = bg_ref[...].astype(F32) * co_ref[...]
        rc = lax.rsqrt(jnp.mean(p * p, axis=-1, keepdims=True) + EPS)
        yc = (p * rc) * gc_ref[...] * _silu(zc_ref[...].astype(F32))
        ycat_ref[:, 0:dc] = yc.astype(BF16)
        ycatt_ref[0:dc, :] = yc.T.astype(BF16)
        ssq = jnp.zeros((tb, 1), F32)
        for h in range(hp):
            o = om_ref[h]
            ssq = ssq + jnp.sum(o * o, axis=-1, keepdims=True)
        ra = lax.rsqrt(ssq * (1.0 / da) + EPS)
        for h in range(hp):
            ya = (om_ref[h] * ra) * ga_ref[h] * _silu(za_ref[h].astype(F32))
            ycat_ref[:, dc + h * PAIR:dc + (h + 1) * PAIR] = ya.astype(BF16)
            ycatt_ref[dc + h * PAIR:dc + (h + 1) * PAIR, :] = ya.T.astype(BF16)

    pair_spec = pl.BlockSpec((hp, tb, PAIR), lambda i: (0, i, 0))
    return pl.pallas_call(
        body, name="mix_fwd", grid=(t // tb,),
        out_shape=(jax.ShapeDtypeStruct((t, dc + da), BF16), jax.ShapeDtypeStruct((dc + da, t), BF16)),
        in_specs=[pl.BlockSpec((tb, dc), lambda i: (i, 0)),
                  pl.BlockSpec((tb, dc), lambda i: (i, 1)),
                  pl.BlockSpec((tb, dc), lambda i: (i, 3)),
                  pl.BlockSpec((None, hp, tb, PAIR), lambda i: (3, 0, i, 0)),
                  pair_spec,
                  pl.BlockSpec((1, dc), lambda i: (0, 0)),
                  pl.BlockSpec((hp, 1, PAIR), lambda i: (0, 0, 0))],
        out_specs=(pl.BlockSpec((tb, dc + da), lambda i: (i, 0)), pl.BlockSpec((dc + da, tb), lambda i: (0, i))),
        compiler_params=_params(("parallel",)),
    )(co, conv_proj, conv_proj, attn_proj, o_mix, g_conv, g_attn_pairs)


def _out_fwd_bwd(ycat, woutf, x, target, mod, g_post):
    t, d = x.shape
    n = ycat.shape[1]
    tb = ROW_TILE

    def body(a_ref, w_ref, x_ref, tg_ref, mod_ref, g_ref, dout_ref, dy_ref, acc_ref):
        y = jnp.dot(a_ref[...], w_ref[...], preferred_element_type=F32)
        r = lax.rsqrt(jnp.mean(y * y, axis=-1, keepdims=True) + EPS)
        nh = y * r
        gate = mod_ref[2:3, :]
        nrm = nh * g_ref[...]
        err = x_ref[...] + gate * nrm - tg_ref[...]
        dout = err * (1.0 / d)
        dout_ref[...] = dout
        dn = dout * gate
        a = dn * g_ref[...]
        dy = r * (a - nh * jnp.mean(a * nh, axis=-1, keepdims=True))
        dy_ref[...] = dy.astype(BF16)
        loss = 0.5 * jnp.sum(jnp.sum(err * err, axis=-1, keepdims=True) * (1.0 / d), axis=0, keepdims=True)
        part = jnp.concatenate(
            [jnp.sum(dout * nrm, axis=0, keepdims=True), jnp.sum(dn * nh, axis=0, keepdims=True),
             jnp.broadcast_to(loss, (1, d)), jnp.zeros((5, d), F32)], axis=0)

        @pl.when(pl.program_id(0) == 0)
        def _():
            acc_ref[...] = jnp.zeros(acc_ref.shape, F32)

        acc_ref[...] += part

    return pl.pallas_call(
        body, name="out_fwd_bwd", grid=(t // tb,),
        out_shape=(jax.ShapeDtypeStruct((t, d), F32), jax.ShapeDtypeStruct((t, d), BF16),
                   jax.ShapeDtypeStruct((8, d), F32)),
        in_specs=[pl.BlockSpec((tb, n), lambda i: (i, 0)), pl.BlockSpec((n, d), lambda i: (0, 0)),
                  pl.BlockSpec((tb, d), lambda i: (i, 0)), pl.BlockSpec((tb, d), lambda i: (i, 0)),
                  pl.BlockSpec((3, d), lambda i: (0, 0)), pl.BlockSpec((1, d), lambda i: (0, 0))],
        out_specs=(pl.BlockSpec((tb, d), lambda i: (i, 0)), pl.BlockSpec((tb, d), lambda i: (i, 0)),
                   pl.BlockSpec((8, d), lambda i: (0, 0))),
        compiler_params=_params(("arbitrary",)),
    )(ycat, woutf, x, target, mod, g_post)


def _matmul_nt(a, b, out_dtype, name):
    m, k = a.shape
    n = b.shape[0]
    tn = COL_TILE

    def body(a_ref, b_ref, o_ref):
        o_ref[...] = lax.dot_general(a_ref[...], b_ref[...], (((1,), (1,)), ((), ())),
                                     preferred_element_type=F32).astype(out_dtype)

    return pl.pallas_call(
        body, name=name, grid=(n // tn,),
        out_shape=jax.ShapeDtypeStruct((m, n), out_dtype),
        in_specs=[pl.BlockSpec((m, k), lambda i: (0, 0)), pl.BlockSpec((tn, k), lambda i: (i, 0))],
        out_specs=pl.BlockSpec((m, tn), lambda i: (0, i)),
        compiler_params=_params(("parallel",)),
    )(a, b)


def _matmul_nn(a, b, out_dtype, name):
    m, k = a.shape
    n = b.shape[1]
    tn = COL_TILE

    def body(a_ref, b_ref, o_ref):
        o_ref[...] = jnp.dot(a_ref[...], b_ref[...], preferred_element_type=F32).astype(out_dtype)

    return pl.pallas_call(
        body, name=name, grid=(n // tn,),
        out_shape=jax.ShapeDtypeStruct((m, n), out_dtype),
        in_specs=[pl.BlockSpec((m, k), lambda i: (0, 0)), pl.BlockSpec((k, tn), lambda i: (0, i))],
        out_specs=pl.BlockSpec((m, tn), lambda i: (0, i)),
        compiler_params=_params(("parallel",)),
    )(a, b)


def _mix_bwd(dycat, co, conv_proj, attn_proj, o_mix, g_conv, g_attn_pairs):
    t, dc = co.shape
    hp = attn_proj.shape[1]
    da = hp * PAIR
    tb = ROW_TILE

    def body(dy_ref, co_ref, bg_ref, zc_ref, za_ref, om_ref, gc_ref, ga_ref,
             dcp_ref, dco_ref, dza_ref, do_ref, dl_ref, dgc_ref, dga_ref):
        first = pl.program_id(0) == 0
        cov = co_ref[...]
        bg = bg_ref[...].astype(F32)
        zc = zc_ref[...].astype(F32)
        p = bg * cov
        rc = lax.rsqrt(jnp.mean(p * p, axis=-1, keepdims=True) + EPS)
        nh = p * rc
        dyc = dy_ref[:, 0:dc].astype(F32)
        dn = dyc * _silu(zc)
        a = dn * gc_ref[...]
        dp = rc * (a - nh * jnp.mean(a * nh, axis=-1, keepdims=True))
        dcp_ref[:, 0:dc] = jnp.zeros((tb, dc), BF16)
        dcp_ref[:, dc:2 * dc] = (dp * cov).astype(BF16)
        dcp_ref[:, 2 * dc:3 * dc] = jnp.zeros((tb, dc), BF16)
        dcp_ref[:, 3 * dc:4 * dc] = (dyc * nh * gc_ref[...] * _silu_grad(zc)).astype(BF16)
        dco_ref[...] = dp * bg

        @pl.when(first)
        def _():
            dgc_ref[...] = jnp.zeros(dgc_ref.shape, F32)
            dga_ref[...] = jnp.zeros(dga_ref.shape, F32)

        dgc_ref[...] += jnp.sum(dn * nh, axis=0, keepdims=True)

        ssq = jnp.zeros((tb, 1), F32)
        for h in range(hp):
            o = om_ref[h]
            ssq = ssq + jnp.sum(o * o, axis=-1, keepdims=True)
        ra = lax.rsqrt(ssq * (1.0 / da) + EPS)
        dot_an = jnp.zeros((tb, 1), F32)
        for h in range(hp):
            nha = om_ref[h] * ra
            za = za_ref[h].astype(F32)
            dya = dy_ref[:, dc + h * PAIR:dc + (h + 1) * PAIR].astype(F32)
            dna = dya * _silu(za)
            dza_ref[h] = (dya * nha * ga_ref[h] * _silu_grad(za)).astype(BF16)
            dga_ref[h] += jnp.sum(dna * nha, axis=0, keepdims=True)
            dot_an = dot_an + jnp.sum(dna * ga_ref[h] * nha, axis=-1, keepdims=True)
        mean_an = dot_an * (1.0 / da)
        first_head = lax.broadcasted_iota(jnp.int32, (tb, PAIR), 1) < HEAD_DIM
        for h in range(hp):
            o = om_ref[h]
            nha = o * ra
            za = za_ref[h].astype(F32)
            dya = dy_ref[:, dc + h * PAIR:dc + (h + 1) * PAIR].astype(F32)
            aa = dya * _silu(za) * ga_ref[h]
            d_o = ra * (aa - nha * mean_an)
            do_ref[h] = d_o.astype(BF16)
            prod = d_o * o
            both = jnp.sum(prod, axis=-1, keepdims=True)
            head0 = jnp.sum(jnp.where(first_head, prod, 0.0), axis=-1, keepdims=True)
            dl_ref[h] = jnp.where(first_head, head0, both - head0)

    pair_spec = pl.BlockSpec((hp, tb, PAIR), lambda i: (0, i, 0))
    return pl.pallas_call(
        body, name="mix_bwd", grid=(t // tb,),
        out_shape=(jax.ShapeDtypeStruct((t, 4 * dc), BF16), jax.ShapeDtypeStruct((t, dc), F32),
                   jax.ShapeDtypeStruct((4, hp, t, PAIR), BF16), jax.ShapeDtypeStruct((hp, t, PAIR), BF16),
                   jax.ShapeDtypeStruct((hp, t, PAIR), F32),
                   jax.ShapeDtypeStruct((1, dc), F32), jax.ShapeDtypeStruct((hp, 1, PAIR), F32)),
        in_specs=[pl.BlockSpec((tb, dc + da), lambda i: (i, 0)),
                  pl.BlockSpec((tb, dc), lambda i: (i, 0)),
                  pl.BlockSpec((tb, dc), lambda i: (i, 1)),
                  pl.BlockSpec((tb, dc), lambda i: (i, 3)),
                  pl.BlockSpec((None, hp, tb, PAIR), lambda i: (3, 0, i, 0)),
                  pair_spec,
                  pl.BlockSpec((1, dc), lambda i: (0, 0)),
                  pl.BlockSpec((hp, 1, PAIR), lambda i: (0, 0, 0))],
        out_specs=(pl.BlockSpec((tb, 4 * dc), lambda i: (i, 0)), pl.BlockSpec((tb, dc), lambda i: (i, 0)),
                   pl.BlockSpec((None, hp, tb, PAIR), lambda i: (3, 0, i, 0)), pair_spec, pair_spec,
                   pl.BlockSpec((1, dc), lambda i: (0, 0)), pl.BlockSpec((hp, 1, PAIR), lambda i: (0, 0, 0))),
        compiler_params=_params(("arbitrary",)),
    )(dycat, co, conv_proj, conv_proj, attn_proj, o_mix, g_conv, g_attn_pairs)


def _conv_bwd(dconv_proj, dco, conv_proj, conv_w, dc):
    t = dco.shape[0]
    ct = CONV_TILE
    nct = dc // ct

    def body(dcp_in_ref, dco_ref, u_ref, cg_ref, w_ref, dcp_ref, acc_ref):
        del dcp_in_ref
        which = pl.program_id(1)
        g = dco_ref[...]
        u = u_ref[...].astype(F32)
        cg = cg_ref[...].astype(F32)
        g_prev, g_next = _shift_rows(g, t)
        da = w_ref[0:1, :] * g_next + w_ref[1:2, :] * g + w_ref[2:3, :] * g_prev
        dcp_ref[...] = (da * jnp.where(which == 0, cg, u)).astype(BF16)
        a = cg * u
        a_prev, a_next = _shift_rows(a, t)
        acc_ref[...] = jnp.concatenate(
            [jnp.sum(g * a_prev, axis=0, keepdims=True), jnp.sum(g * a, axis=0, keepdims=True),
             jnp.sum(g * a_next, axis=0, keepdims=True), jnp.sum(g, axis=0, keepdims=True),
             jnp.zeros((4, ct), F32)], axis=0)

    return pl.pallas_call(
        body, name="conv_bwd", grid=(nct, 2),
        out_shape=(jax.ShapeDtypeStruct(dconv_proj.shape, BF16), jax.ShapeDtypeStruct((8, dc), F32)),
        in_specs=[HBM,
                  pl.BlockSpec((t, ct), lambda i, s: (0, i)),
                  pl.BlockSpec((t, ct), lambda i, s: (0, i)),
                  pl.BlockSpec((t, ct), lambda i, s: (0, 2 * nct + i)),
                  pl.BlockSpec((3, ct), lambda i, s: (0, i))],
        out_specs=(pl.BlockSpec((t, ct), lambda i, s: (0, 2 * s * nct + i)),
                   pl.BlockSpec((8, ct), lambda i, s: (0, i))),
        input_output_aliases={0: 0},
        compiler_params=_params(("arbitrary", "arbitrary")),
    )(dconv_proj, dco, conv_proj, conv_proj, conv_w)


def _dw_in(ht, dconv_proj, dattn, ws, da):
    d, t = ht.shape
    tn = COL_TILE
    nt = ws // tn
    per_comp = da // tn
    pairs = tn // PAIR

    def body_conv(a_ref, b_ref, o_ref):
        o_ref[...] = jnp.dot(a_ref[...], b_ref[...], preferred_element_type=F32).astype(BF16)

    gin = pl.pallas_call(
        body_conv, name="dw_in_conv", grid=(2, nt),
        out_shape=jax.ShapeDtypeStruct((N_CHIPS, d, ws), BF16),
        in_specs=[pl.BlockSpec((d, t), lambda j, n: (0, 0)), pl.BlockSpec((t, tn), lambda j, n: (0, j * nt + n))],
        out_specs=pl.BlockSpec((None, d, tn), lambda j, n: (j, 0, n)),
        compiler_params=_params(("parallel", "parallel")),
    )(ht, dconv_proj)

    def body_attn(g_ref, a_ref, b_ref, o_ref):
        del g_ref
        o_ref[...] = jnp.dot(a_ref[...], _pairs_to_cols(b_ref, pairs), preferred_element_type=F32).astype(BF16)

    return pl.pallas_call(
        body_attn, name="dw_in_attn", grid=(2, nt),
        out_shape=jax.ShapeDtypeStruct((N_CHIPS, d, ws), BF16),
        in_specs=[HBM, pl.BlockSpec((d, t), lambda j, n: (0, 0)),
                  pl.BlockSpec((None, pairs, t, PAIR), lambda j, n: (2 * j + n // per_comp, n % per_comp, 0, 0))],
        out_specs=pl.BlockSpec((None, d, tn), lambda j, n: (2 + j, 0, n)),
        input_output_aliases={0: 0},
        compiler_params=_params(("parallel", "parallel")),
    )(gin, ht, dattn)


def _dh(dconv_proj, dattn, winf, da):
    t = dconv_proj.shape[0]
    _, d, ws = winf.shape
    tm = 1024
    tk = COL_TILE
    nk = ws // tk
    per_comp = da // tk
    pairs = tk // PAIR

    def body_conv(a_ref, b_ref, o_ref):
        @pl.when((pl.program_id(1) == 0) & (pl.program_id(2) == 0))
        def _():
            o_ref[...] = jnp.zeros(o_ref.shape, F32)

        o_ref[...] += lax.dot_general(a_ref[...], b_ref[...], (((1,), (1,)), ((), ())), preferred_element_type=F32)

    part = pl.pallas_call(
        body_conv, name="dh_conv", grid=(t // tm, 2, nk),
        out_shape=jax.ShapeDtypeStruct((t, d), F32),
        in_specs=[pl.BlockSpec((tm, tk), lambda m, j, k: (m, j * nk + k)),
                  pl.BlockSpec((None, d, tk), lambda m, j, k: (j, 0, k))],
        out_specs=pl.BlockSpec((tm, d), lambda m, j, k: (m, 0)),
        compiler_params=_params(("parallel", "arbitrary", "arbitrary")),
    )(dconv_proj, winf)

    def body_attn(p_ref, a_ref, b_ref, o_ref):
        @pl.when((pl.program_id(1) == 0) & (pl.program_id(2) == 0))
        def _():
            o_ref[...] = p_ref[...]

        o_ref[...] += lax.dot_general(_pairs_to_cols(a_ref, pairs), b_ref[...], (((1,), (1,)), ((), ())),
                                      preferred_element_type=F32)

    return pl.pallas_call(
        body_attn, name="dh_attn", grid=(t // tm, 2, nk),
        out_shape=jax.ShapeDtypeStruct((t, d), F32),
        in_specs=[pl.BlockSpec((tm, d), lambda m, j, k: (m, 0)),
                  pl.BlockSpec((None, pairs, tm, PAIR), lambda m, j, k: (2 * j + k // per_comp, k % per_comp, m, 0)),
                  pl.BlockSpec((None, d, tk), lambda m, j, k: (2 + j, 0, k))],
        out_specs=pl.BlockSpec((tm, d), lambda m, j, k: (m, 0)),
        compiler_params=_params(("parallel", "arbitrary", "arbitrary")),
    )(part, dattn, winf)


def _prenorm_bwd(x, dh, dout, mod, g_pre):
    t, d = x.shape
    tb = ROW_TILE

    def body(x_ref, dh_ref, dout_ref, mod_ref, g_ref, gx_ref, acc_ref):
        xv = x_ref[...]
        dhv = dh_ref[...]
        r = lax.rsqrt(jnp.mean(xv * xv, axis=-1, keepdims=True) + EPS)
        xh = xv * r
        one_scale = 1.0 + mod_ref[1:2, :]
        a = dhv * one_scale * g_ref[...]
        gx_ref[...] = dout_ref[...] + r * (a - xh * jnp.mean(a * xh, axis=-1, keepdims=True))
        part = jnp.concatenate(
            [jnp.sum(dhv, axis=0, keepdims=True), jnp.sum(dhv * xh * g_ref[...], axis=0, keepdims=True),
             jnp.sum(dhv * xh * one_scale, axis=0, keepdims=True), jnp.zeros((5, d), F32)], axis=0)

        @pl.when(pl.program_id(0) == 0)
        def _():
            acc_ref[...] = jnp.zeros(acc_ref.shape, F32)

        acc_ref[...] += part

    return pl.pallas_call(
        body, name="prenorm_bwd", grid=(t // tb,),
        out_shape=(jax.ShapeDtypeStruct((t, d), F32), jax.ShapeDtypeStruct((8, d), F32)),
        in_specs=[pl.BlockSpec((tb, d), lambda i: (i, 0)), pl.BlockSpec((tb, d), lambda i: (i, 0)),
                  pl.BlockSpec((tb, d), lambda i: (i, 0)), pl.BlockSpec((3, d), lambda i: (0, 0)),
                  pl.BlockSpec((1, d), lambda i: (0, 0))],
        out_specs=(pl.BlockSpec((tb, d), lambda i: (i, 0)), pl.BlockSpec((8, d), lambda i: (0, 0))),
        compiler_params=_params(("arbitrary",)),
    )(x, dh, dout, mod, g_pre)


def _chip_sums(place, g, rsib, name):
    _, rows, cols = g.shape
    half = rows // 2
    tr = min(half, ROW_TILE)
    nt = half // tr

    def body(place_ref, g_ref, r_ref, o_ref):
        del place_ref
        o_ref[...] = (g_ref[...].astype(F32) + r_ref[...].astype(F32)).astype(BF16)

    grid_spec = pltpu.PrefetchScalarGridSpec(
        num_scalar_prefetch=1, grid=(N_CHIPS, nt),
        in_specs=[pl.BlockSpec((None, tr, cols), lambda j, i, p: (j, p[1] * nt + i, 0)),
                  pl.BlockSpec((None, tr, cols), lambda j, i, p: (j, i, 0))],
        out_specs=pl.BlockSpec((None, tr, cols), lambda j, i, p: (j, i, 0)))
    return pl.pallas_call(
        body, name=name, grid_spec=grid_spec,
        out_shape=jax.ShapeDtypeStruct((N_CHIPS, half, cols), BF16),
        compiler_params=_params(("parallel", "parallel")),
    )(place, g, rsib)


def _owner_sum(place, g, rsib, rici, name):
    _, rows, cols = g.shape
    half = rows // 2
    tr = min(half, ROW_TILE)
    nt = half // tr

    def body(place_ref, g_ref, r_ref, i_ref, o_ref):
        del place_ref
        acc = g_ref[...].astype(F32) + r_ref[...].astype(F32)
        for k in range(N_CHIPS - 1):
            acc = acc + i_ref[k].astype(F32)
        o_ref[...] = acc

    grid_spec = pltpu.PrefetchScalarGridSpec(
        num_scalar_prefetch=1, grid=(nt,),
        in_specs=[pl.BlockSpec((None, tr, cols), lambda i, p: (p[0], p[1] * nt + i, 0)),
                  pl.BlockSpec((None, tr, cols), lambda i, p: (p[0], i, 0)),
                  pl.BlockSpec((N_CHIPS - 1, tr, cols), lambda i, p: (0, i, 0))],
        out_specs=pl.BlockSpec((tr, cols), lambda i, p: (p[1] * nt + i, 0)))
    return pl.pallas_call(
        body, name=name, grid_spec=grid_spec,
        out_shape=jax.ShapeDtypeStruct((rows, cols), F32),
        compiler_params=_params(("parallel",)),
    )(place, g, rsib, rici)


def _adam_math(w, g, m, v):
    m2 = ADAM_B1 * m + (1.0 - ADAM_B1) * g
    v2 = ADAM_B2 * v + (1.0 - ADAM_B2) * (g * g)
    m_hat = m2 / (1.0 - ADAM_B1 ** ADAM_STEP)
    v_hat = v2 / (1.0 - ADAM_B2 ** ADAM_STEP)
    delta = -ADAM_LR * (m_hat / (jnp.sqrt(v_hat) + ADAM_EPS) + ADAM_WD * w)
    return delta, m2, v2


def _adamw(w, g, m, v, name):
    rows, cols = w.shape
    tr = min(rows, ROW_TILE)

    def body(w_ref, g_ref, m_ref, v_ref, d_ref, m2_ref, v2_ref):
        d_ref[...], m2_ref[...], v2_ref[...] = _adam_math(w_ref[...], g_ref[...], m_ref[...], v_ref[...])

    spec = pl.BlockSpec((tr, cols), lambda i: (i, 0))
    return pl.pallas_call(
        body, name=name, grid=(rows // tr,),
        out_shape=(jax.ShapeDtypeStruct(w.shape, F32),) * 3,
        in_specs=[spec] * 4, out_specs=(spec,) * 3,
        compiler_params=_params(("parallel",)),
    )(w, g, m, v)


def _ada_grad_adamw(c_all_t, dmod_cols, w, m, v):
    d, wa = w.shape
    tn = 256

    def body(ct_ref, dm_ref, w_ref, m_ref, v_ref, g_ref, d_ref, m2_ref, v2_ref):
        act = _silu(ct_ref[...])
        g = act[:, 0:1] * dm_ref[0:1, :]
        for b in range(1, N_DEV):
            g = g + act[:, b:b + 1] * dm_ref[b:b + 1, :]
        g_ref[...] = g
        d_ref[...], m2_ref[...], v2_ref[...] = _adam_math(w_ref[...], g, m_ref[...], v_ref[...])

    spec = pl.BlockSpec((d, tn), lambda i: (0, i))
    return pl.pallas_call(
        body, name="ada_grad_adamw", grid=(wa // tn,),
        out_shape=(jax.ShapeDtypeStruct(w.shape, F32),) * 4,
        in_specs=[pl.BlockSpec((d, N_DEV), lambda i: (0, 0)), pl.BlockSpec((N_DEV, tn), lambda i: (0, i)),
                  spec, spec, spec],
        out_specs=(spec,) * 4,
        compiler_params=_params(("parallel",)),
    )(c_all_t, dmod_cols, w, m, v)


def _sum_devices(gathered):
    n = gathered.shape[1]

    def body(g_ref, o_ref):
        acc = g_ref[0:8, :]
        for dev in range(1, N_DEV):
            acc = acc + g_ref[8 * dev:8 * dev + 8, :]
        o_ref[...] = acc

    return pl.pallas_call(
        body, name="sum_devices",
        out_shape=jax.ShapeDtypeStruct((8, n), F32),
        in_specs=[VMEM], out_specs=VMEM,
    )(gathered)


def _pack_small(pieces):
    flat = [p.reshape(-1).astype(F32) for p in pieces]
    offsets, total = [], 0
    for p in flat:
        offsets.append(total)
        total += p.shape[0]
    padded = -(-total // SMALL_ALIGN) * SMALL_ALIGN
    if padded > total:
        flat.append(jnp.zeros((padded - total,), F32))
    return jnp.concatenate(flat).reshape(8, padded // 8), offsets


def _alibi_slope_rows(n_heads):
    slopes = 2.0 ** (-8.0 * jnp.arange(1, n_heads + 1, dtype=F32) / n_heads)
    rows = jnp.zeros((n_heads // 2, 8), F32).at[:, 0:2].set(slopes.reshape(n_heads // 2, 2))
    return jnp.broadcast_to(rows[:, :, None], (n_heads // 2, 8, ATT_KW))


def kernel(x, c, w_ada, b_ada, g_pre, w_in, conv_w, conv_b, g_conv, g_attn, w_out, g_post, loss_target, m_w_ada, m_b_ada, m_g_pre, m_w_in, m_conv_w, m_conv_b, m_g_conv, m_g_attn, m_w_out, m_g_post, v_w_ada, v_b_ada, v_g_pre, v_w_in, v_conv_w, v_conv_b, v_g_conv, v_g_attn, v_w_out, v_g_post):
    t, d = x.shape[1], x.shape[2]
    dc = conv_b.shape[1]
    da = g_attn.shape[1]
    hp = da // PAIR
    ws = w_in.shape[2]
    wa = w_ada.shape[2]
    cws = conv_w.shape[2]
    assert t % ROW_TILE == 0 and d % ROW_TILE == 0 and dc % COL_TILE == 0 and da % COL_TILE == 0
    assert ws == 2 * dc and dc == da and t // BRANCHES[-1][1] >= ATT_BQ

    mx, my, mc = _my_place()
    chip = _chip_of(mx, my)
    dev = 2 * chip + mc
    place = jnp.stack([chip, mc]).astype(jnp.int32)

    x2, tgt2 = x[0], loss_target[0]
    w_ada2, w_in2, w_out2 = w_ada[0], w_in[0], w_out[0]

    packed, offs = _pack_small([c[0], conv_w[0]])
    seen = _allgather8(packed, "gather_inputs").reshape(N_DEV, -1)
    c_all = seen[:, offs[0]:offs[0] + d]
    conv_w_full = seen[0::2, offs[1]:offs[1] + 3 * cws].reshape(N_CHIPS, 3, cws).transpose(1, 0, 2).reshape(3, dc)

    ada_part = _ada_partial(c_all, w_ada2)
    ada_seen = _allgather8(ada_part, "gather_ada").reshape(N_DEV, N_DEV, wa)
    mod_flat = lax.dynamic_index_in_dim(ada_seen[0::2], dev, axis=1, keepdims=False).reshape(1, 3 * d) + b_ada
    mod = mod_flat.reshape(3, d)

    winf, woutf4 = _gather_weights(_cast_into_slot(place, w_in2, "cast_w_in"),
                                   _cast_into_slot(place, w_out2, "cast_w_out"))
    woutf = woutf4.reshape(dc + da, d)

    h, ht = _prenorm(x2, mod, g_pre)
    conv_proj = _proj_conv(h, winf)
    attn_proj = _proj_attn(h, winf, da)
    slopes = _alibi_slope_rows(da // HEAD_DIM)
    co = _conv_fwd(conv_proj, conv_w_full, conv_b, dc)
    o_mix, lse = _attn_fwd(attn_proj, slopes)
    g_attn_pairs = g_attn.reshape(hp, 1, PAIR)
    ycat, ycat_t = _mix_fwd(co, conv_proj, attn_proj, o_mix, g_conv, g_attn_pairs)
    dout, dy, post_sums = _out_fwd_bwd(ycat, woutf, x2, tgt2, mod, g_post)

    gout = _matmul_nn(ycat_t, dy, BF16, "dw_out").reshape(N_CHIPS, (dc + da) // N_CHIPS, d)
    dycat = _matmul_nt(dy, woutf, BF16, "dycat")
    dconv_proj, dco, dattn, d_o, delta, dg_conv, dg_attn = _mix_bwd(
        dycat, co, conv_proj, attn_proj, o_mix, g_conv, g_attn_pairs)
    dconv_proj, conv_sums = _conv_bwd(dconv_proj, dco, conv_proj, conv_w_full, dc)
    dattn = _attn_bwd(dattn, attn_proj, d_o, lse, delta, slopes)
    gin = _dw_in(ht, dconv_proj, dattn, ws, da)
    dh = _dh(dconv_proj, dattn, winf, da)
    grad_x, pre_sums = _prenorm_bwd(x2, dh, dout, mod, g_pre)

    rsib_in, rsib_out = _swap_halves(gin, gout)
    csum_in = _chip_sums(place, gin, rsib_in, "rs_chip_sum_in")
    csum_out = _chip_sums(place, gout, rsib_out, "rs_chip_sum_out")
    rici_in, rici_out = _send_to_owners(csum_in, csum_out)
    half_in = _owner_sum(place, gin, rsib_in, rici_in, "rs_owner_sum_in")
    half_out = _owner_sum(place, gout, rsib_out, rici_out, "rs_owner_sum_out")
    grad_w_in, grad_w_out = _join_halves(half_in, half_out)

    small, so = _pack_small([
        pre_sums[0], pre_sums[1], post_sums[0],
        pre_sums[2], conv_sums[0:3], conv_sums[3], dg_conv, dg_attn, post_sums[1], post_sums[2, 0:128]])
    small_seen = _allgather8(small, "gather_small")
    total = _sum_devices(small_seen).reshape(-1)
    dmod_all = small_seen.reshape(N_DEV, -1)[:, 0:3 * d]
    loss = total[so[9]]
    grad_b_ada = total[0:3 * d].reshape(1, 3 * d)
    grad_g_pre = total[so[3]:so[3] + d].reshape(1, d)
    grad_conv_w_full = total[so[4]:so[4] + 3 * dc].reshape(3, dc)
    grad_conv_w = lax.dynamic_slice_in_dim(grad_conv_w_full, chip * cws, cws, axis=1).reshape(1, 3, cws)
    grad_conv_b = total[so[5]:so[5] + dc].reshape(1, dc)
    grad_g_conv = total[so[6]:so[6] + dc].reshape(1, dc)
    grad_g_attn = total[so[7]:so[7] + da].reshape(1, da)
    grad_g_post = total[so[8]:so[8] + d].reshape(1, d)

    dmod_cols = lax.dynamic_slice_in_dim(dmod_all, chip * wa, wa, axis=1)
    grad_w_ada, delta_w_ada, new_m_w_ada, new_v_w_ada = _ada_grad_adamw(c_all.T, dmod_cols, w_ada2, m_w_ada[0], v_w_ada[0])
    delta_w_in, new_m_w_in, new_v_w_in = _adamw(w_in2, grad_w_in, m_w_in[0], v_w_in[0], "adamw_w_in")
    delta_w_out, new_m_w_out, new_v_w_out = _adamw(w_out2, grad_w_out, m_w_out[0], v_w_out[0], "adamw_w_out")

    small_w = [b_ada, g_pre, conv_w, conv_b, g_conv, g_attn, g_post]
    small_g = [grad_b_ada, grad_g_pre, grad_conv_w, grad_conv_b, grad_g_conv, grad_g_attn, grad_g_post]
    small_m = [m_b_ada, m_g_pre, m_conv_w, m_conv_b, m_g_conv, m_g_attn, m_g_post]
    small_v = [v_b_ada, v_g_pre, v_conv_w, v_conv_b, v_g_conv, v_g_attn, v_g_post]
    pw, po = _pack_small(small_w)
    pg, _ = _pack_small(small_g)
    pm, _ = _pack_small(small_m)
    pv, _ = _pack_small(small_v)
    sd, sm, sv = (a.reshape(-1) for a in _adamw(pw, pg, pm, pv, "adamw_small"))

    def unpack(flat):
        return [flat[o:o + w.size].reshape(w.shape) for o, w in zip(po, small_w)]

    d_small, m_small, v_small = unpack(sd), unpack(sm), unpack(sv)

    def lead(a):
        return a.reshape((1,) + a.shape)

    grads = [lead(grad_w_ada), grad_b_ada, grad_g_pre, lead(grad_w_in), grad_conv_w, grad_conv_b, grad_g_conv,
             grad_g_attn, lead(grad_w_out), grad_g_post]
    deltas = [lead(delta_w_ada), d_small[0], d_small[1], lead(delta_w_in), d_small[2], d_small[3], d_small[4],
              d_small[5], lead(delta_w_out), d_small[6]]
    new_ms = [lead(new_m_w_ada), m_small[0], m_small[1], lead(new_m_w_in), m_small[2], m_small[3], m_small[4],
              m_small[5], lead(new_m_w_out), m_small[6]]
    new_vs = [lead(new_v_w_ada), v_small[0], v_small[1], lead(new_v_w_in), v_small[2], v_small[3], v_small[4],
              v_small[5], lead(new_v_w_out), v_small[6]]
    return (loss, lead(grad_x), *grads, *deltas, *new_ms, *new_vs)
```

```python
import jax
import jax.numpy as jnp
from jax import lax
from jax.experimental import pallas as pl
from jax.experimental.pallas import tpu as pltpu

F32 = jnp.float32
BF16 = jnp.bfloat16
MESH = pl.DeviceIdType.MESH
HBM = pl.BlockSpec(memory_space=pltpu.HBM)
VMEM = pl.BlockSpec(memory_space=pltpu.VMEM)
ANY = pl.BlockSpec(memory_space=pl.ANY)
SEM = pl.BlockSpec(memory_space=pltpu.SEMAPHORE)
EFFECT = pltpu.SideEffectType.DATAFLOW_SIDE_EFFECTING
TOKEN = jax.ShapeDtypeStruct((8, 128), jnp.float32)

HEAD_DIM = 64
PAIR = 2 * HEAD_DIM
BRANCHES = ((128, 1), (512, 4), (2048, 16))
SIDE = 64
EPS = 1e-6
NEG_INF = -1e30
N_CHIPS = 4
N_DEV = 8

ADAM_LR = 0.001
ADAM_B1 = 0.9
ADAM_B2 = 0.999
ADAM_EPS = 1e-08
ADAM_WD = 0.01
ADAM_STEP = 10

VMEM_LIMIT_BYTES = 56 * 1024 * 1024
ROW_TILE = 256
COL_TILE = 512
CONV_TILE = 256
ATT_BQ = 128
ATT_KW = ATT_BQ + 2 * SIDE
ATT_UNROLL = 4
ATT_UNROLL_BWD = 2
SMALL_ALIGN = 1024


def _params(semantics=None):
    kw = {"vmem_limit_bytes": VMEM_LIMIT_BYTES}
    if semantics is not None:
        kw["dimension_semantics"] = semantics
    return pltpu.CompilerParams(**kw)


def _silu(z):
    return z * jax.nn.sigmoid(z)


def _silu_grad(z):
    s = jax.nn.sigmoid(z)
    return s * (1.0 + z * (1.0 - s))


def _my_place():
    return lax.axis_index("x"), lax.axis_index("y"), lax.axis_index("c")


def _flip(a, bit):
    return 1 - a if bit else a


def _chip_of(x, y):
    return 2 * x + y


def _allgather8(v, name):
    rows_per, n = v.shape

    def body(v_ref, out_ref, send_sems, recv_sems):
        x, y, c = _my_place()
        me = 4 * x + 2 * y + c

        def rows(idx):
            return out_ref.at[pl.ds(pl.multiple_of(idx * rows_per, rows_per), rows_per), :]

        out_ref[pl.ds(pl.multiple_of(me * rows_per, rows_per), rows_per), :] = v_ref[...]
        copies = []
        for k in range(1, N_DEV):
            peer = (_flip(x, k & 4), _flip(y, k & 2), _flip(c, k & 1))
            cp = pltpu.make_async_remote_copy(
                src_ref=v_ref, dst_ref=rows(me), send_sem=send_sems.at[k - 1], recv_sem=recv_sems.at[k - 1],
                device_id=peer, device_id_type=MESH)
            cp.start()
            copies.append((cp, peer))
        for k, (cp, peer) in enumerate(copies):
            src = 4 * peer[0] + 2 * peer[1] + peer[2]
            pltpu.make_async_remote_copy(
                src_ref=v_ref, dst_ref=rows(src), send_sem=send_sems.at[k], recv_sem=recv_sems.at[k],
                device_id=peer, device_id_type=MESH).wait_recv()
        for cp, _ in copies:
            cp.wait_send()

    return pl.pallas_call(
        body, name=name,
        out_shape=jax.ShapeDtypeStruct((N_DEV * rows_per, n), v.dtype),
        in_specs=[VMEM], out_specs=VMEM,
        scratch_shapes=[pltpu.SemaphoreType.DMA((N_DEV - 1,)), pltpu.SemaphoreType.DMA((N_DEV - 1,))],
    )(v)


def _half_rows(ref, chip, which, half):
    return ref.at[chip, pl.ds(pl.multiple_of(which * half, half), half), :]


def _ici_peers(x, y, c):
    peers = [(_flip(x, k & 2), _flip(y, k & 1), c) for k in (1, 2, 3)]
    return [(peer, _chip_of(peer[0], peer[1])) for peer in peers]


def _gather_start(win_slots, wout_slots):
    slots = (win_slots, wout_slots)

    def body(win_in, wout_in, winf_ref, woutf_ref, send_in, recv_in, send_out, recv_out, token_ref):
        del win_in, wout_in
        x, y, c = _my_place()
        me = _chip_of(x, y)
        for buf, ssem, rsem in ((winf_ref, send_in, recv_in), (woutf_ref, send_out, recv_out)):
            mine = _half_rows(buf, me, c, buf.shape[1] // 2)
            for k, (peer, _) in enumerate(_ici_peers(x, y, c)):
                pltpu.make_async_remote_copy(
                    src_ref=mine, dst_ref=mine, send_sem=ssem.at[k], recv_sem=rsem.at[k],
                    device_id=peer, device_id_type=MESH).start()
        token_ref[...] = jnp.zeros(token_ref.shape, F32)

    sems = pltpu.SemaphoreType.DMA((N_CHIPS - 1,))
    return pl.pallas_call(
        body, name="gather_start",
        out_shape=tuple(jax.ShapeDtypeStruct(s.shape, s.dtype) for s in slots) + (sems,) * 4 + (TOKEN,),
        in_specs=[HBM, HBM], out_specs=(HBM, HBM, SEM, SEM, SEM, SEM, VMEM),
        input_output_aliases={0: 0, 1: 1},
        compiler_params=pltpu.CompilerParams(has_side_effects=EFFECT),
    )(win_slots, wout_slots)


def _gather_wait(buf, send_sems, recv_sems, after, name):
    half = buf.shape[1] // 2

    def body(buf_in, send_ref, recv_ref, after_ref, buf_ref):
        del buf_in, after_ref
        x, y, c = _my_place()
        mine = _half_rows(buf_ref, _chip_of(x, y), c, half)
        for k, (peer, src_chip) in enumerate(_ici_peers(x, y, c)):
            cp = pltpu.make_async_remote_copy(
                src_ref=mine, dst_ref=_half_rows(buf_ref, src_chip, c, half),
                send_sem=send_ref.at[k], recv_sem=recv_ref.at[k], device_id=peer, device_id_type=MESH)
            cp.wait_send()
            cp.wait_recv()

    return pl.pallas_call(
        body, name=name,
        out_shape=jax.ShapeDtypeStruct(buf.shape, buf.dtype),
        in_specs=[HBM, SEM, SEM, ANY], out_specs=HBM,
        input_output_aliases={0: 0},
        compiler_params=pltpu.CompilerParams(has_side_effects=EFFECT),
    )(buf, send_sems, recv_sems, after)


def _forward_halves(buf, name):
    half = buf.shape[1] // 2

    def body(buf_in, buf_ref, send_sems, recv_sems):
        del buf_in
        x, y, c = _my_place()
        sibling = (x, y, 1 - c)
        started = []
        for k, (_, src_chip) in enumerate(_ici_peers(x, y, c)):
            landed = _half_rows(buf_ref, src_chip, c, half)
            fw = pltpu.make_async_remote_copy(
                src_ref=landed, dst_ref=landed, send_sem=send_sems.at[k], recv_sem=recv_sems.at[k],
                device_id=sibling, device_id_type=MESH)
            fw.start()
            started.append(fw)
        for k, (_, src_chip) in enumerate(_ici_peers(x, y, c)):
            other = _half_rows(buf_ref, src_chip, 1 - c, half)
            pltpu.make_async_remote_copy(
                src_ref=other, dst_ref=other, send_sem=send_sems.at[k], recv_sem=recv_sems.at[k],
                device_id=sibling, device_id_type=MESH).wait_recv()
        for fw in started:
            fw.wait_send()

    return pl.pallas_call(
        body, name=name,
        out_shape=jax.ShapeDtypeStruct(buf.shape, buf.dtype),
        in_specs=[HBM], out_specs=HBM,
        input_output_aliases={0: 0},
        scratch_shapes=[pltpu.SemaphoreType.DMA((N_CHIPS - 1,))] * 2,
    )(buf)


def _swap_halves(g, name):
    half = g.shape[1] // 2

    def body(g_ref, r_ref, send_sem, recv_sem):
        x, y, c = _my_place()
        theirs = g_ref.at[:, pl.ds(pl.multiple_of((1 - c) * half, half), half), :]
        cp = pltpu.make_async_remote_copy(
            src_ref=theirs, dst_ref=r_ref, send_sem=send_sem, recv_sem=recv_sem,
            device_id=(x, y, 1 - c), device_id_type=MESH)
        cp.start()
        cp.wait()

    return pl.pallas_call(
        body, name=name,
        out_shape=jax.ShapeDtypeStruct((N_CHIPS, half, g.shape[2]), g.dtype),
        in_specs=[HBM], out_specs=HBM,
        scratch_shapes=[pltpu.SemaphoreType.DMA, pltpu.SemaphoreType.DMA],
    )(g)


def _owners_start(csum, name):
    land = pltpu.with_memory_space_constraint(lax.empty((N_CHIPS - 1,) + csum.shape[1:], csum.dtype), pltpu.HBM)

    def body(csum_ref, land_ref, send_sems, recv_sems, csum_thru, land_thru, token_ref):
        del csum_thru, land_thru
        x, y, c = _my_place()
        for k, (peer, owner) in enumerate(_ici_peers(x, y, c)):
            pltpu.make_async_remote_copy(
                src_ref=csum_ref.at[owner], dst_ref=land_ref.at[k], send_sem=send_sems.at[k], recv_sem=recv_sems.at[k],
                device_id=peer, device_id_type=MESH).start()
        token_ref[...] = jnp.zeros(token_ref.shape, F32)

    sems = pltpu.SemaphoreType.DMA((N_CHIPS - 1,))
    return pl.pallas_call(
        body, name=name,
        out_shape=(sems, sems, jax.ShapeDtypeStruct(csum.shape, csum.dtype),
                   jax.ShapeDtypeStruct(land.shape, land.dtype), TOKEN),
        in_specs=[HBM, HBM], out_specs=(SEM, SEM, HBM, HBM, VMEM),
        input_output_aliases={0: 2, 1: 3},
        compiler_params=pltpu.CompilerParams(has_side_effects=EFFECT),
    )(pltpu.with_memory_space_constraint(csum, pltpu.HBM), land)


def _owners_wait(send_sems, recv_sems, csum, land, after, name):
    def body(csum_ref, land_ref, send_ref, recv_ref, *rest):
        del rest
        x, y, c = _my_place()
        for k, (peer, owner) in enumerate(_ici_peers(x, y, c)):
            cp = pltpu.make_async_remote_copy(
                src_ref=csum_ref.at[owner], dst_ref=land_ref.at[k], send_sem=send_ref.at[k], recv_sem=recv_ref.at[k],
                device_id=peer, device_id_type=MESH)
            cp.wait_send()
            cp.wait_recv()

    return pl.pallas_call(
        body, name=name,
        out_shape=(jax.ShapeDtypeStruct(csum.shape, csum.dtype), jax.ShapeDtypeStruct(land.shape, land.dtype)),
        in_specs=[HBM, HBM, SEM, SEM] + [ANY] * len(after), out_specs=(HBM, HBM),
        input_output_aliases={0: 0, 1: 1},
        compiler_params=pltpu.CompilerParams(has_side_effects=EFFECT),
    )(csum, land, send_sems, recv_sems, *after)[1]


def _join_halves(full, name):
    rows = full.shape[0] // 2

    def body(full_in, full_ref, send_sem, recv_sem):
        del full_in
        x, y, c = _my_place()
        sibling = (x, y, 1 - c)
        mine = full_ref.at[pl.ds(pl.multiple_of(c * rows, rows), rows), :]
        theirs = full_ref.at[pl.ds(pl.multiple_of((1 - c) * rows, rows), rows), :]
        cp = pltpu.make_async_remote_copy(
            src_ref=mine, dst_ref=mine, send_sem=send_sem, recv_sem=recv_sem, device_id=sibling, device_id_type=MESH)
        cp.start()
        pltpu.make_async_remote_copy(
            src_ref=theirs, dst_ref=theirs, send_sem=send_sem, recv_sem=recv_sem,
            device_id=sibling, device_id_type=MESH).wait_recv()
        cp.wait_send()

    return pl.pallas_call(
        body, name=name,
        out_shape=jax.ShapeDtypeStruct(full.shape, full.dtype),
        in_specs=[HBM], out_specs=HBM,
        input_output_aliases={0: 0},
        scratch_shapes=[pltpu.SemaphoreType.DMA, pltpu.SemaphoreType.DMA],
    )(full)


def _cast_into_slot(place, w, name):
    rows, cols = w.shape
    tr = min(rows, ROW_TILE)

    def body(place_ref, w_ref, o_ref):
        del place_ref
        o_ref[...] = w_ref[...].astype(BF16)

    grid_spec = pltpu.PrefetchScalarGridSpec(
        num_scalar_prefetch=1, grid=(rows // tr,),
        in_specs=[pl.BlockSpec((tr, cols), lambda i, p: (i, 0))],
        out_specs=pl.BlockSpec((None, tr, cols), lambda i, p: (p[0], i, 0)))
    return pl.pallas_call(
        body, name=name, grid_spec=grid_spec,
        out_shape=jax.ShapeDtypeStruct((N_CHIPS, rows, cols), BF16),
        compiler_params=_params(("parallel",)),
    )(place, w)


def _ada_partial(c_all, w_ada):
    d_model, wa = w_ada.shape
    tn = 512 if wa % 512 == 0 else 256

    def body(c_ref, w_ref, o_ref):
        o_ref[...] = jnp.dot(_silu(c_ref[...]), w_ref[...], precision=lax.Precision.HIGHEST,
                             preferred_element_type=F32)

    return pl.pallas_call(
        body, name="ada_partial", grid=(wa // tn,),
        out_shape=jax.ShapeDtypeStruct((N_DEV, wa), F32),
        in_specs=[pl.BlockSpec((N_DEV, d_model), lambda i: (0, 0)), pl.BlockSpec((d_model, tn), lambda i: (0, i))],
        out_specs=pl.BlockSpec((N_DEV, tn), lambda i: (0, i)),
        compiler_params=_params(("parallel",)),
    )(c_all, w_ada)


def _prenorm(x, mod, g_pre):
    t, d = x.shape
    tb = ROW_TILE

    def body(x_ref, mod_ref, g_ref, h_ref, ht_ref):
        xv = x_ref[...]
        r = lax.rsqrt(jnp.mean(xv * xv, axis=-1, keepdims=True) + EPS)
        h = (xv * r) * g_ref[...] * (1.0 + mod_ref[1:2, :]) + mod_ref[0:1, :]
        h_ref[...] = h.astype(BF16)
        ht_ref[...] = h.T.astype(BF16)

    return pl.pallas_call(
        body, name="prenorm", grid=(t // tb,),
        out_shape=(jax.ShapeDtypeStruct((t, d), BF16), jax.ShapeDtypeStruct((d, t), BF16)),
        in_specs=[pl.BlockSpec((tb, d), lambda i: (i, 0)), pl.BlockSpec((3, d), lambda i: (0, 0)),
                  pl.BlockSpec((1, d), lambda i: (0, 0))],
        out_specs=(pl.BlockSpec((tb, d), lambda i: (i, 0)), pl.BlockSpec((d, tb), lambda i: (0, i))),
        compiler_params=_params(("parallel",)),
    )(x, mod, g_pre)


def _pairs_to_cols(ref, n_pairs):
    return jnp.concatenate([ref[i] for i in range(n_pairs)], axis=1)


def _proj_conv(h, winf):
    t, d = h.shape
    ws = winf.shape[2]
    tn = COL_TILE
    nt = ws // tn

    def body(a_ref, b_ref, o_ref):
        o_ref[...] = jnp.dot(a_ref[...], b_ref[...], preferred_element_type=F32).astype(BF16)

    return pl.pallas_call(
        body, name="proj_conv", grid=(2, nt),
        out_shape=jax.ShapeDtypeStruct((t, 2 * ws), BF16),
        in_specs=[pl.BlockSpec((t, d), lambda j, n: (0, 0)), pl.BlockSpec((None, d, tn), lambda j, n: (j, 0, n))],
        out_specs=pl.BlockSpec((t, tn), lambda j, n: (0, j * nt + n)),
        compiler_params=_params(("parallel", "parallel")),
    )(h, winf)


def _proj_attn(h, winf, da):
    t, d = h.shape
    ws = winf.shape[2]
    tn = COL_TILE
    nt = ws // tn
    per_comp = da // tn
    pairs = tn // PAIR

    def body(a_ref, b_ref, o_ref):
        res = jnp.dot(a_ref[...], b_ref[...], preferred_element_type=F32).astype(BF16)
        for i in range(pairs):
            o_ref[i] = res[:, i * PAIR:(i + 1) * PAIR]

    return pl.pallas_call(
        body, name="proj_attn", grid=(2, nt),
        out_shape=jax.ShapeDtypeStruct((4, da // PAIR, t, PAIR), BF16),
        in_specs=[pl.BlockSpec((t, d), lambda j, n: (0, 0)), pl.BlockSpec((None, d, tn), lambda j, n: (2 + j, 0, n))],
        out_specs=pl.BlockSpec((None, pairs, t, PAIR), lambda j, n: (2 * j + n // per_comp, n % per_comp, 0, 0)),
        compiler_params=_params(("parallel", "parallel")),
    )(h, winf)


def _shift_rows(a, rows):
    idx = lax.broadcasted_iota(jnp.int32, a.shape, 0)
    prev = jnp.where(idx == 0, 0.0, pltpu.roll(a, 1, 0))
    nxt = jnp.where(idx == rows - 1, 0.0, pltpu.roll(a, rows - 1, 0))
    return prev, nxt


def _conv_fwd(conv_proj, conv_w, conv_b, dc):
    t = conv_proj.shape[0]
    ct = CONV_TILE
    nct = dc // ct

    def body(u_ref, cg_ref, w_ref, b_ref, co_ref):
        a = cg_ref[...].astype(F32) * u_ref[...].astype(F32)
        prev, nxt = _shift_rows(a, t)
        co_ref[...] = w_ref[0:1, :] * prev + w_ref[1:2, :] * a + w_ref[2:3, :] * nxt + b_ref[...]

    return pl.pallas_call(
        body, name="conv_fwd", grid=(nct,),
        out_shape=jax.ShapeDtypeStruct((t, dc), F32),
        in_specs=[pl.BlockSpec((t, ct), lambda i: (0, i)), pl.BlockSpec((t, ct), lambda i: (0, 2 * nct + i)),
                  pl.BlockSpec((3, ct), lambda i: (0, i)), pl.BlockSpec((1, ct), lambda i: (0, i))],
        out_specs=pl.BlockSpec((t, ct), lambda i: (0, i)),
        compiler_params=_params(("parallel",)),
    )(conv_proj, conv_proj, conv_w, conv_b)


def _to_residue_major(src_ref, dst_ref, r, scale=None):
    t = src_ref.shape[0]
    seq = t // r
    for res in range(r):
        rows = src_ref[pl.ds(res, seq, stride=r), :] if r > 1 else src_ref[...]
        if scale is not None:
            rows = rows * scale
        dst_ref[res * seq:(res + 1) * seq, :] = rows.astype(dst_ref.dtype)


def _fill_bias_tiles(bias_ref, sl_ref, r, kw, transposed=False):
    shape = (kw, ATT_BQ) if transposed else (ATT_BQ, kw)
    key_axis, query_axis = (0, 1) if transposed else (1, 0)
    base = lax.broadcasted_iota(jnp.int32, shape, key_axis) - lax.broadcasted_iota(jnp.int32, shape, query_axis)
    for hh in range(2):
        slope = -(sl_ref[hh:hh + 1, 0:shape[1]] * float(r))
        for e, shift in enumerate((0, -SIDE, ATT_BQ - kw)):
            arel = jnp.abs(base + shift)
            bias_ref[hh, e, 0:shape[0], 0:shape[1]] = jnp.where(arel <= SIDE, arel.astype(F32) * slope, NEG_INF)


def _first_head_lanes():
    return lax.broadcasted_iota(jnp.int32, (1, PAIR), 1) < HEAD_DIM


def _only_head(x, first, hh):
    return jnp.where(first if hh == 0 else jnp.logical_not(first), x, jnp.zeros_like(x))


def _all_lanes(x, first, hh):
    other = pltpu.roll(x, HEAD_DIM, 1)
    return jnp.where(first, x, other) if hh == 0 else jnp.where(first, other, x)


def _block_place(g, seq_len, kw):
    nqb = seq_len // ATT_BQ
    if nqb == 1:
        row = pl.multiple_of(g * ATT_BQ, ATT_BQ)
        return row, row, 0
    res = g // nqb
    qb = g - res * nqb
    q0 = qb * ATT_BQ
    ks = jnp.clip(q0 - SIDE, 0, seq_len - kw)
    edge = jnp.where(qb == 0, 0, jnp.where(qb == nqb - 1, 2, 1))
    return (pl.multiple_of(res * seq_len + q0, ATT_BQ), pl.multiple_of(res * seq_len + ks, SIDE), edge)


def _attn_fwd(attn_proj, slopes):
    _, hp, t, _ = attn_proj.shape
    n_blocks = t // ATT_BQ

    def body(qkv_ref, sl_ref, o_ref, lse_ref, stage, dil, bias, o_res, l_res, o_tok, l_tok):
        for b, (_, r) in enumerate(BRANCHES):
            seq_len = t // r
            kw = min(ATT_KW, seq_len)
            for comp in range(3):
                stage[...] = qkv_ref[comp].astype(F32)
                _to_residue_major(stage, dil.at[comp], r, scale=HEAD_DIM ** -0.5 if comp == 0 else None)
            _fill_bias_tiles(bias, sl_ref, r, kw)
            o_dst, l_dst = (o_tok.at[b], l_tok.at[b]) if r == 1 else (o_res, l_res)
            first = _first_head_lanes()

            def block(g, carry, seq_len=seq_len, kw=kw, o_dst=o_dst, l_dst=l_dst, first=first):
                qrow, krow, edge = _block_place(g, seq_len, kw)
                q = dil[0, pl.ds(qrow, ATT_BQ), :]
                k = dil[1, pl.ds(krow, kw), :]
                v = dil[2, pl.ds(krow, kw), :]
                ones = jnp.ones((kw, PAIR), BF16)
                both, tops = None, []
                for hh in range(2):
                    s = lax.dot_general(_only_head(q, first, hh), k, (((1,), (1,)), ((), ())),
                                        preferred_element_type=F32)
                    s = s + bias[hh, edge, :, 0:kw]
                    m = jnp.max(s, axis=-1, keepdims=True)
                    p = jnp.exp(s - m).astype(BF16)
                    rhs = jnp.concatenate([_only_head(v, first, hh), _only_head(ones, first, hh)], axis=1)
                    part = jnp.dot(p, rhs, preferred_element_type=F32)
                    both = part if both is None else both + part
                    tops.append(m)
                den = both[:, PAIR:]
                o_dst[pl.ds(qrow, ATT_BQ), :] = both[:, 0:PAIR] / den
                l_dst[pl.ds(qrow, ATT_BQ), :] = jnp.where(first, tops[0], tops[1]) + jnp.log(den)
                return carry

            lax.fori_loop(0, n_blocks, block, 0, unroll=ATT_UNROLL)
            if r > 1:
                for res in range(r):
                    rows = slice(res * seq_len, (res + 1) * seq_len)
                    o_tok[b, pl.ds(res, seq_len, stride=r), :] = o_res[rows, :]
                    l_tok[b, pl.ds(res, seq_len, stride=r), :] = l_res[rows, :]

        def merge(i, carry):
            rows = pl.ds(pl.multiple_of(i * ROW_TILE, ROW_TILE), ROW_TILE)
            la, lb, lc = l_tok[0, rows, :], l_tok[1, rows, :], l_tok[2, rows, :]
            m = jnp.maximum(jnp.maximum(la, lb), lc)
            wa, wb, wc = jnp.exp(la - m), jnp.exp(lb - m), jnp.exp(lc - m)
            den = wa + wb + wc
            o_ref[rows, :] = (wa * o_tok[0, rows, :] + wb * o_tok[1, rows, :] + wc * o_tok[2, rows, :]) * (1.0 / den)
            lse_ref[rows, :] = m + jnp.log(den)
            return carry

        lax.fori_loop(0, t // ROW_TILE, merge, 0)

    pair_spec = pl.BlockSpec((None, t, PAIR), lambda h: (h, 0, 0))
    return pl.pallas_call(
        body, name="attn_fwd", grid=(hp,),
        out_shape=(jax.ShapeDtypeStruct((hp, t, PAIR), F32), jax.ShapeDtypeStruct((hp, t, PAIR), F32)),
        in_specs=[pl.BlockSpec((3, None, t, PAIR), lambda h: (0, h, 0, 0)),
                  pl.BlockSpec((None, 8, ATT_KW), lambda h: (h, 0, 0))],
        out_specs=(pair_spec, pair_spec),
        scratch_shapes=[pltpu.VMEM((t, PAIR), F32), pltpu.VMEM((3, t, PAIR), BF16),
                        pltpu.VMEM((2, 3, ATT_BQ, ATT_KW), F32),
                        pltpu.VMEM((t, PAIR), F32), pltpu.VMEM((t, PAIR), F32),
                        pltpu.VMEM((3, t, PAIR), F32), pltpu.VMEM((3, t, PAIR), F32)],
        compiler_params=_params(("parallel",)),
    )(attn_proj, slopes)


def _attn_bwd(dattn, attn_proj, d_o, lse, delta, slopes, after):
    _, hp, t, _ = attn_proj.shape
    n_blocks = t // ATT_BQ

    def body(dattn_in, qkv_ref, do_ref, lse_ref, dl_ref, sl_ref, after_ref, out_ref,
             stage, dil, lse_res, dl_res, lse_row, dl_row, bias, bias_t, acc, tot):
        del dattn_in, after_ref
        for b, (_, r) in enumerate(BRANCHES):
            seq_len = t // r
            kw = min(ATT_KW, seq_len)
            for comp in range(3):
                stage[...] = qkv_ref[comp].astype(F32)
                _to_residue_major(stage, dil.at[comp], r, scale=HEAD_DIM ** -0.5 if comp == 0 else None)
            stage[...] = do_ref[...].astype(F32)
            _to_residue_major(stage, dil.at[3], r)
            _to_residue_major(lse_ref, lse_res, r)
            _to_residue_major(dl_ref, dl_res, r)
            for g in range(n_blocks):
                rows = slice(g * ATT_BQ, (g + 1) * ATT_BQ)
                for src, dst in ((lse_res, lse_row), (dl_res, dl_row)):
                    flipped = src[rows, :].T
                    dst[g, 0:1, :] = flipped[0:1, :]
                    dst[g, 1:2, :] = flipped[HEAD_DIM:HEAD_DIM + 1, :]
            _fill_bias_tiles(bias, sl_ref, r, kw)
            _fill_bias_tiles(bias_t, sl_ref, r, kw, transposed=True)
            acc[1] = jnp.zeros((t, PAIR), F32)
            acc[2] = jnp.zeros((t, PAIR), F32)
            first = _first_head_lanes()

            def block(g, carry, seq_len=seq_len, kw=kw, first=first):
                qrow, krow, edge = _block_place(g, seq_len, kw)
                nt = (((1,), (1,)), ((), ()))
                q = dil[0, pl.ds(qrow, ATT_BQ), :]
                k = dil[1, pl.ds(krow, kw), :]
                v = dil[2, pl.ds(krow, kw), :]
                dov = dil[3, pl.ds(qrow, ATT_BQ), :]
                lv = lse_res[pl.ds(qrow, ATT_BQ), :]
                dlv = dl_res[pl.ds(qrow, ATT_BQ), :]
                dq = dk = dv = None
                for hh in range(2):
                    qm, km = _only_head(q, first, hh), _only_head(k, first, hh)
                    vm, dom = _only_head(v, first, hh), _only_head(dov, first, hh)
                    lse_cols = jnp.concatenate([_all_lanes(lv, first, hh)] * (kw // PAIR), axis=1)
                    dl_cols = jnp.concatenate([_all_lanes(dlv, first, hh)] * (kw // PAIR), axis=1)
                    s = lax.dot_general(qm, k, nt, preferred_element_type=F32)
                    p = jnp.exp(s + bias[hh, edge, :, 0:kw] - lse_cols)
                    dp = lax.dot_general(dom, v, nt, preferred_element_type=F32)
                    ds = (p * (dp - dl_cols)).astype(BF16)
                    s_t = lax.dot_general(km, q, nt, preferred_element_type=F32)
                    p_t = jnp.exp(s_t + bias_t[hh, edge, 0:kw, :] - lse_row[g, hh:hh + 1, :])
                    dp_t = lax.dot_general(vm, dov, nt, preferred_element_type=F32)
                    ds_t = (p_t * (dp_t - dl_row[g, hh:hh + 1, :])).astype(BF16)
                    dq_h = jnp.dot(ds, km, preferred_element_type=F32)
                    dk_h = jnp.dot(ds_t, qm, preferred_element_type=F32)
                    dv_h = jnp.dot(p_t.astype(BF16), dom, preferred_element_type=F32)
                    dq, dk, dv = (dq_h, dk_h, dv_h) if dq is None else (dq + dq_h, dk + dk_h, dv + dv_h)
                acc[0, pl.ds(qrow, ATT_BQ), :] = dq * (HEAD_DIM ** -0.5)
                acc[1, pl.ds(krow, kw), :] += dk
                acc[2, pl.ds(krow, kw), :] += dv
                return carry

            lax.fori_loop(0, n_blocks, block, 0, unroll=ATT_UNROLL_BWD)
            for comp in range(3):
                if r == 1:
                    tot[comp] = acc[comp]
                else:
                    for res in range(r):
                        tok = pl.ds(res, seq_len, stride=r)
                        tot[comp, tok, :] = tot[comp, tok, :] + acc[comp, res * seq_len:(res + 1) * seq_len, :]
        for comp in range(3):
            out_ref[comp] = tot[comp].astype(BF16)

    pair_spec = pl.BlockSpec((None, t, PAIR), lambda h: (h, 0, 0))
    return pl.pallas_call(
        body, name="attn_bwd", grid=(hp,),
        out_shape=jax.ShapeDtypeStruct(dattn.shape, BF16),
        in_specs=[HBM, pl.BlockSpec((3, None, t, PAIR), lambda h: (0, h, 0, 0)), pair_spec, pair_spec, pair_spec,
                  pl.BlockSpec((None, 8, ATT_KW), lambda h: (h, 0, 0)), ANY],
        out_specs=pl.BlockSpec((3, None, t, PAIR), lambda h: (0, h, 0, 0)),
        input_output_aliases={0: 0},
        scratch_shapes=[pltpu.VMEM((t, PAIR), F32), pltpu.VMEM((4, t, PAIR), BF16),
                        pltpu.VMEM((t, PAIR), F32), pltpu.VMEM((t, PAIR), F32),
                        pltpu.VMEM((n_blocks, 8, ATT_BQ), F32), pltpu.VMEM((n_blocks, 8, ATT_BQ), F32),
                        pltpu.VMEM((2, 3, ATT_BQ, ATT_KW), F32), pltpu.VMEM((2, 3, ATT_KW, ATT_BQ), F32),
                        pltpu.VMEM((3, t, PAIR), F32), pltpu.VMEM((3, t, PAIR), F32)],
        compiler_params=_params(("parallel",)),
    )(dattn, attn_proj, d_o, lse, delta, slopes, after)


def _mix_fwd(co, conv_proj, attn_proj, o_mix, g_conv, g_attn_pairs):
    t, dc = co.shape
    hp = attn_proj.shape[1]
    da = hp * PAIR
    tb = ROW_TILE

    def body(co_ref, bg_ref, zc_ref, za_ref, om_ref, gc_ref, ga_ref, ycat_ref, ycatt_ref):
        p = bg_ref[...].astype(F32) * co_ref[...]
        rc = lax.rsqrt(jnp.mean(p * p, axis=-1, keepdims=True) + EPS)
        yc = (p * rc) * gc_ref[...] * _silu(zc_ref[...].astype(F32))
        ycat_ref[:, 0:dc] = yc.astype(BF16)
        ycatt_ref[0:dc, :] = yc.T.astype(BF16)
        ssq = jnp.zeros((tb, 1), F32)
        for h in range(hp):
            o = om_ref[h]
            ssq = ssq + jnp.sum(o * o, axis=-1, keepdims=True)
        ra = lax.rsqrt(ssq * (1.0 / da) + EPS)
        for h in range(hp):
            ya = (om_ref[h] * ra) * ga_ref[h] * _silu(za_ref[h].astype(F32))
            ycat_ref[:, dc + h * PAIR:dc + (h + 1) * PAIR] = ya.astype(BF16)
            ycatt_ref[dc + h * PAIR:dc + (h + 1) * PAIR, :] = ya.T.astype(BF16)

    pair_spec = pl.BlockSpec((hp, tb, PAIR), lambda i: (0, i, 0))
    return pl.pallas_call(
        body, name="mix_fwd", grid=(t // tb,),
        out_shape=(jax.ShapeDtypeStruct((t, dc + da), BF16), jax.ShapeDtypeStruct((dc + da, t), BF16)),
        in_specs=[pl.BlockSpec((tb, dc), lambda i: (i, 0)),
                  pl.BlockSpec((tb, dc), lambda i: (i, 1)),
                  pl.BlockSpec((tb, dc), lambda i: (i, 3)),
                  pl.BlockSpec((None, hp, tb, PAIR), lambda i: (3, 0, i, 0)),
                  pair_spec,
                  pl.BlockSpec((1, dc), lambda i: (0, 0)),
                  pl.BlockSpec((hp, 1, PAIR), lambda i: (0, 0, 0))],
        out_specs=(pl.BlockSpec((tb, dc + da), lambda i: (i, 0)), pl.BlockSpec((dc + da, tb), lambda i: (0, i))),
        compiler_params=_params(("parallel",)),
    )(co, conv_proj, conv_proj, attn_proj, o_mix, g_conv, g_attn_pairs)


def _out_fwd_bwd(ycat, woutf, x, target, mod, g_post):
    t, d = x.shape
    n = ycat.shape[1]
    tb = ROW_TILE

    def body(a_ref, w_ref, x_ref, tg_ref, mod_ref, g_ref, dout_ref, dy_ref, acc_ref):
        y = jnp.dot(a_ref[...], w_ref[...], preferred_element_type=F32)
        r = lax.rsqrt(jnp.mean(y * y, axis=-1, keepdims=True) + EPS)
        nh = y * r
        gate = mod_ref[2:3, :]
        nrm = nh * g_ref[...]
        err = x_ref[...] + gate * nrm - tg_ref[...]
        dout = err * (1.0 / d)
        dout_ref[...] = dout
        dn = dout * gate
        a = dn * g_ref[...]
        dy = r * (a - nh * jnp.mean(a * nh, axis=-1, keepdims=True))
        dy_ref[...] = dy.astype(BF16)
        loss = 0.5 * jnp.sum(jnp.sum(err * err, axis=-1, keepdims=True) * (1.0 / d), axis=0, keepdims=True)
        part = jnp.concatenate(
            [jnp.sum(dout * nrm, axis=0, keepdims=True), jnp.sum(dn * nh, axis=0, keepdims=True),
             jnp.broadcast_to(loss, (1, d)), jnp.zeros((5, d), F32)], axis=0)

        @pl.when(pl.program_id(0) == 0)
        def _():
            acc_ref[...] = jnp.zeros(acc_ref.shape, F32)

        acc_ref[...] += part

    return pl.pallas_call(
        body, name="out_fwd_bwd", grid=(t // tb,),
        out_shape=(jax.ShapeDtypeStruct((t, d), F32), jax.ShapeDtypeStruct((t, d), BF16),
                   jax.ShapeDtypeStruct((8, d), F32)),
        in_specs=[pl.BlockSpec((tb, n), lambda i: (i, 0)), pl.BlockSpec((n, d), lambda i: (0, 0)),
                  pl.BlockSpec((tb, d), lambda i: (i, 0)), pl.BlockSpec((tb, d), lambda i: (i, 0)),
                  pl.BlockSpec((3, d), lambda i: (0, 0)), pl.BlockSpec((1, d), lambda i: (0, 0))],
        out_specs=(pl.BlockSpec((tb, d), lambda i: (i, 0)), pl.BlockSpec((tb, d), lambda i: (i, 0)),
                   pl.BlockSpec((8, d), lambda i: (0, 0))),
        compiler_params=_params(("arbitrary",)),
    )(ycat, woutf, x, target, mod, g_post)


def _matmul_nt(a, b, out_dtype, name):
    m, k = a.shape
    n = b.shape[0]
    tn = COL_TILE

    def body(a_ref, b_ref, o_ref):
        o_ref[...] = lax.dot_general(a_ref[...], b_ref[...], (((1,), (1,)), ((), ())),
                                     preferred_element_type=F32).astype(out_dtype)

    return pl.pallas_call(
        body, name=name, grid=(n // tn,),
        out_shape=jax.ShapeDtypeStruct((m, n), out_dtype),
        in_specs=[pl.BlockSpec((m, k), lambda i: (0, 0)), pl.BlockSpec((tn, k), lambda i: (i, 0))],
        out_specs=pl.BlockSpec((m, tn), lambda i: (0, i)),
        compiler_params=_params(("parallel",)),
    )(a, b)


def _matmul_nn(a, b, out_dtype, name):
    m, k = a.shape
    n = b.shape[1]
    tn = COL_TILE

    def body(a_ref, b_ref, o_ref):
        o_ref[...] = jnp.dot(a_ref[...], b_ref[...], preferred_element_type=F32).astype(out_dtype)

    return pl.pallas_call(
        body, name=name, grid=(n // tn,),
        out_shape=jax.ShapeDtypeStruct((m, n), out_dtype),
        in_specs=[pl.BlockSpec((m, k), lambda i: (0, 0)), pl.BlockSpec((k, tn), lambda i: (0, i))],
        out_specs=pl.BlockSpec((m, tn), lambda i: (0, i)),
        compiler_params=_params(("parallel",)),
    )(a, b)


def _mix_bwd(dycat, co, conv_proj, attn_proj, o_mix, g_conv, g_attn_pairs):
    t, dc = co.shape
    hp = attn_proj.shape[1]
    da = hp * PAIR
    tb = ROW_TILE

    def body(dy_ref, co_ref, bg_ref, zc_ref, za_ref, om_ref, gc_ref, ga_ref,
             dcp_ref, dco_ref, dza_ref, do_ref, dl_ref, dgc_ref, dga_ref):
        first = pl.program_id(0) == 0
        cov = co_ref[...]
        bg = bg_ref[...].astype(F32)
        zc = zc_ref[...].astype(F32)
        p = bg * cov
        rc = lax.rsqrt(jnp.mean(p * p, axis=-1, keepdims=True) + EPS)
        nh = p * rc
        dyc = dy_ref[:, 0:dc].astype(F32)
        dn = dyc * _silu(zc)
        a = dn * gc_ref[...]
        dp = rc * (a - nh * jnp.mean(a * nh, axis=-1, keepdims=True))
        dcp_ref[:, 0:dc] = jnp.zeros((tb, dc), BF16)
        dcp_ref[:, dc:2 * dc] = (dp * cov).astype(BF16)
        dcp_ref[:, 2 * dc:3 * dc] = jnp.zeros((tb, dc), BF16)
        dcp_ref[:, 3 * dc:4 * dc] = (dyc * nh * gc_ref[...] * _silu_grad(zc)).astype(BF16)
        dco_ref[...] = dp * bg

        @pl.when(first)
        def _():
            dgc_ref[...] = jnp.zeros(dgc_ref.shape, F32)
            dga_ref[...] = jnp.zeros(dga_ref.shape, F32)

        dgc_ref[...] += jnp.sum(dn * nh, axis=0, keepdims=True)

        ssq = jnp.zeros((tb, 1), F32)
        for h in range(hp):
            o = om_ref[h]
            ssq = ssq + jnp.sum(o * o, axis=-1, keepdims=True)
        ra = lax.rsqrt(ssq * (1.0 / da) + EPS)
        dot_an = jnp.zeros((tb, 1), F32)
        for h in range(hp):
            nha = om_ref[h] * ra
            za = za_ref[h].astype(F32)
            dya = dy_ref[:, dc + h * PAIR:dc + (h + 1) * PAIR].astype(F32)
            dna = dya * _silu(za)
            dza_ref[h] = (dya * nha * ga_ref[h] * _silu_grad(za)).astype(BF16)
            dga_ref[h] += jnp.sum(dna * nha, axis=0, keepdims=True)
            dot_an = dot_an + jnp.sum(dna * ga_ref[h] * nha, axis=-1, keepdims=True)
        mean_an = dot_an * (1.0 / da)
        first_head = lax.broadcasted_iota(jnp.int32, (tb, PAIR), 1) < HEAD_DIM
        for h in range(hp):
            o = om_ref[h]
            nha = o * ra
            za = za_ref[h].astype(F32)
            dya = dy_ref[:, dc + h * PAIR:dc + (h + 1) * PAIR].astype(F32)
            aa = dya * _silu(za) * ga_ref[h]
            d_o = ra * (aa - nha * mean_an)
            do_ref[h] = d_o.astype(BF16)
            prod = d_o * o
            both = jnp.sum(prod, axis=-1, keepdims=True)
            head0 = jnp.sum(jnp.where(first_head, prod, 0.0), axis=-1, keepdims=True)
            dl_ref[h] = jnp.where(first_head, head0, both - head0)

    pair_spec = pl.BlockSpec((hp, tb, PAIR), lambda i: (0, i, 0))
    return pl.pallas_call(
        body, name="mix_bwd", grid=(t // tb,),
        out_shape=(jax.ShapeDtypeStruct((t, 4 * dc), BF16), jax.ShapeDtypeStruct((t, dc), F32),
                   jax.ShapeDtypeStruct((4, hp, t, PAIR), BF16), jax.ShapeDtypeStruct((hp, t, PAIR), BF16),
                   jax.ShapeDtypeStruct((hp, t, PAIR), F32),
                   jax.ShapeDtypeStruct((1, dc), F32), jax.ShapeDtypeStruct((hp, 1, PAIR), F32)),
        in_specs=[pl.BlockSpec((tb, dc + da), lambda i: (i, 0)),
                  pl.BlockSpec((tb, dc), lambda i: (i, 0)),
                  pl.BlockSpec((tb, dc), lambda i: (i, 1)),
                  pl.BlockSpec((tb, dc), lambda i: (i, 3)),
                  pl.BlockSpec((None, hp, tb, PAIR), lambda i: (3, 0, i, 0)),
                  pair_spec,
                  pl.BlockSpec((1, dc), lambda i: (0, 0)),
                  pl.BlockSpec((hp, 1, PAIR), lambda i: (0, 0, 0))],
        out_specs=(pl.BlockSpec((tb, 4 * dc), lambda i: (i, 0)), pl.BlockSpec((tb, dc), lambda i: (i, 0)),
                   pl.BlockSpec((None, hp, tb, PAIR), lambda i: (3, 0, i, 0)), pair_spec, pair_spec,
                   pl.BlockSpec((1, dc), lambda i: (0, 0)), pl.BlockSpec((hp, 1, PAIR), lambda i: (0, 0, 0))),
        compiler_params=_params(("arbitrary",)),
    )(dycat, co, conv_proj, conv_proj, attn_proj, o_mix, g_conv, g_attn_pairs)


def _conv_bwd(dconv_proj, dco, conv_proj, conv_w, dc, after):
    t = dco.shape[0]
    ct = CONV_TILE
    nct = dc // ct

    def body(dcp_in_ref, dco_ref, u_ref, cg_ref, w_ref, after_ref, dcp_ref, acc_ref):
        del dcp_in_ref, after_ref
        which = pl.program_id(1)
        g = dco_ref[...]
        u = u_ref[...].astype(F32)
        cg = cg_ref[...].astype(F32)
        g_prev, g_next = _shift_rows(g, t)
        da = w_ref[0:1, :] * g_next + w_ref[1:2, :] * g + w_ref[2:3, :] * g_prev
        dcp_ref[...] = (da * jnp.where(which == 0, cg, u)).astype(BF16)
        a = cg * u
        a_prev, a_next = _shift_rows(a, t)
        acc_ref[...] = jnp.concatenate(
            [jnp.sum(g * a_prev, axis=0, keepdims=True), jnp.sum(g * a, axis=0, keepdims=True),
             jnp.sum(g * a_next, axis=0, keepdims=True), jnp.sum(g, axis=0, keepdims=True),
             jnp.zeros((4, ct), F32)], axis=0)

    return pl.pallas_call(
        body, name="conv_bwd", grid=(nct, 2),
        out_shape=(jax.ShapeDtypeStruct(dconv_proj.shape, BF16), jax.ShapeDtypeStruct((8, dc), F32)),
        in_specs=[HBM,
                  pl.BlockSpec((t, ct), lambda i, s: (0, i)),
                  pl.BlockSpec((t, ct), lambda i, s: (0, i)),
                  pl.BlockSpec((t, ct), lambda i, s: (0, 2 * nct + i)),
                  pl.BlockSpec((3, ct), lambda i, s: (0, i)), ANY],
        out_specs=(pl.BlockSpec((t, ct), lambda i, s: (0, 2 * s * nct + i)),
                   pl.BlockSpec((8, ct), lambda i, s: (0, i))),
        input_output_aliases={0: 0},
        compiler_params=_params(("arbitrary", "arbitrary")),
    )(dconv_proj, dco, conv_proj, conv_proj, conv_w, after)


def _dw_in(ht, dconv_proj, dattn, ws, da):
    d, t = ht.shape
    tn = COL_TILE
    nt = ws // tn
    per_comp = da // tn
    pairs = tn // PAIR

    def body_conv(a_ref, b_ref, o_ref):
        o_ref[...] = jnp.dot(a_ref[...], b_ref[...], preferred_element_type=F32).astype(BF16)

    gin = pl.pallas_call(
        body_conv, name="dw_in_conv", grid=(2, nt),
        out_shape=jax.ShapeDtypeStruct((N_CHIPS, d, ws), BF16),
        in_specs=[pl.BlockSpec((d, t), lambda j, n: (0, 0)), pl.BlockSpec((t, tn), lambda j, n: (0, j * nt + n))],
        out_specs=pl.BlockSpec((None, d, tn), lambda j, n: (j, 0, n)),
        compiler_params=_params(("parallel", "parallel")),
    )(ht, dconv_proj)

    def body_attn(g_ref, a_ref, b_ref, o_ref):
        del g_ref
        o_ref[...] = jnp.dot(a_ref[...], _pairs_to_cols(b_ref, pairs), preferred_element_type=F32).astype(BF16)

    return pl.pallas_call(
        body_attn, name="dw_in_attn", grid=(2, nt),
        out_shape=jax.ShapeDtypeStruct((N_CHIPS, d, ws), BF16),
        in_specs=[HBM, pl.BlockSpec((d, t), lambda j, n: (0, 0)),
                  pl.BlockSpec((None, pairs, t, PAIR), lambda j, n: (2 * j + n // per_comp, n % per_comp, 0, 0))],
        out_specs=pl.BlockSpec((None, d, tn), lambda j, n: (2 + j, 0, n)),
        input_output_aliases={0: 0},
        compiler_params=_params(("parallel", "parallel")),
    )(gin, ht, dattn)


def _dh(dconv_proj, dattn, winf, da, after):
    t = dconv_proj.shape[0]
    _, d, ws = winf.shape
    tm = 1024
    tk = COL_TILE
    nk = ws // tk
    per_comp = da // tk
    pairs = tk // PAIR

    def body_conv(a_ref, b_ref, after_ref, o_ref):
        del after_ref

        @pl.when((pl.program_id(1) == 0) & (pl.program_id(2) == 0))
        def _():
            o_ref[...] = jnp.zeros(o_ref.shape, F32)

        o_ref[...] += lax.dot_general(a_ref[...], b_ref[...], (((1,), (1,)), ((), ())), preferred_element_type=F32)

    part = pl.pallas_call(
        body_conv, name="dh_conv", grid=(t // tm, 2, nk),
        out_shape=jax.ShapeDtypeStruct((t, d), F32),
        in_specs=[pl.BlockSpec((tm, tk), lambda m, j, k: (m, j * nk + k)),
                  pl.BlockSpec((None, d, tk), lambda m, j, k: (j, 0, k)), ANY],
        out_specs=pl.BlockSpec((tm, d), lambda m, j, k: (m, 0)),
        compiler_params=_params(("parallel", "arbitrary", "arbitrary")),
    )(dconv_proj, winf, after)

    def body_attn(p_ref, a_ref, b_ref, o_ref):
        @pl.when((pl.program_id(1) == 0) & (pl.program_id(2) == 0))
        def _():
            o_ref[...] = p_ref[...]

        o_ref[...] += lax.dot_general(_pairs_to_cols(a_ref, pairs), b_ref[...], (((1,), (1,)), ((), ())),
                                      preferred_element_type=F32)

    return pl.pallas_call(
        body_attn, name="dh_attn", grid=(t // tm, 2, nk),
        out_shape=jax.ShapeDtypeStruct((t, d), F32),
        in_specs=[pl.BlockSpec((tm, d), lambda m, j, k: (m, 0)),
                  pl.BlockSpec((None, pairs, tm, PAIR), lambda m, j, k: (2 * j + k // per_comp, k % per_comp, m, 0)),
                  pl.BlockSpec((None, d, tk), lambda m, j, k: (2 + j, 0, k))],
        out_specs=pl.BlockSpec((tm, d), lambda m, j, k: (m, 0)),
        compiler_params=_params(("parallel", "arbitrary", "arbitrary")),
    )(part, dattn, winf)


def _prenorm_bwd(x, dh, dout, mod, g_pre):
    t, d = x.shape
    tb = ROW_TILE

    def body(x_ref, dh_ref, dout_ref, mod_ref, g_ref, gx_ref, acc_ref):
        xv = x_ref[...]
        dhv = dh_ref[...]
        r = lax.rsqrt(jnp.mean(xv * xv, axis=-1, keepdims=True) + EPS)
        xh = xv * r
        one_scale = 1.0 + mod_ref[1:2, :]
        a = dhv * one_scale * g_ref[...]
        gx_ref[...] = dout_ref[...] + r * (a - xh * jnp.mean(a * xh, axis=-1, keepdims=True))
        part = jnp.concatenate(
            [jnp.sum(dhv, axis=0, keepdims=True), jnp.sum(dhv * xh * g_ref[...], axis=0, keepdims=True),
             jnp.sum(dhv * xh * one_scale, axis=0, keepdims=True), jnp.zeros((5, d), F32)], axis=0)

        @pl.when(pl.program_id(0) == 0)
        def _():
            acc_ref[...] = jnp.zeros(acc_ref.shape, F32)

        acc_ref[...] += part

    return pl.pallas_call(
        body, name="prenorm_bwd", grid=(t // tb,),
        out_shape=(jax.ShapeDtypeStruct((t, d), F32), jax.ShapeDtypeStruct((8, d), F32)),
        in_specs=[pl.BlockSpec((tb, d), lambda i: (i, 0)), pl.BlockSpec((tb, d), lambda i: (i, 0)),
                  pl.BlockSpec((tb, d), lambda i: (i, 0)), pl.BlockSpec((3, d), lambda i: (0, 0)),
                  pl.BlockSpec((1, d), lambda i: (0, 0))],
        out_specs=(pl.BlockSpec((tb, d), lambda i: (i, 0)), pl.BlockSpec((8, d), lambda i: (0, 0))),
        compiler_params=_params(("arbitrary",)),
    )(x, dh, dout, mod, g_pre)


def _chip_sums(place, g, rsib, name):
    _, rows, cols = g.shape
    half = rows // 2
    tr = min(half, ROW_TILE)
    nt = half // tr

    def body(place_ref, g_ref, r_ref, o_ref):
        del place_ref
        o_ref[...] = (g_ref[...].astype(F32) + r_ref[...].astype(F32)).astype(BF16)

    grid_spec = pltpu.PrefetchScalarGridSpec(
        num_scalar_prefetch=1, grid=(N_CHIPS, nt),
        in_specs=[pl.BlockSpec((None, tr, cols), lambda j, i, p: (j, p[1] * nt + i, 0)),
                  pl.BlockSpec((None, tr, cols), lambda j, i, p: (j, i, 0))],
        out_specs=pl.BlockSpec((None, tr, cols), lambda j, i, p: (j, i, 0)))
    return pl.pallas_call(
        body, name=name, grid_spec=grid_spec,
        out_shape=jax.ShapeDtypeStruct((N_CHIPS, half, cols), BF16),
        compiler_params=_params(("parallel", "parallel")),
    )(place, g, rsib)


def _owner_sum(place, g, rsib, rici, name):
    _, rows, cols = g.shape
    half = rows // 2
    tr = min(half, ROW_TILE)
    nt = half // tr

    def body(place_ref, g_ref, r_ref, i_ref, o_ref):
        del place_ref
        acc = g_ref[...].astype(F32) + r_ref[...].astype(F32)
        for k in range(N_CHIPS - 1):
            acc = acc + i_ref[k].astype(F32)
        o_ref[...] = acc

    grid_spec = pltpu.PrefetchScalarGridSpec(
        num_scalar_prefetch=1, grid=(nt,),
        in_specs=[pl.BlockSpec((None, tr, cols), lambda i, p: (p[0], p[1] * nt + i, 0)),
                  pl.BlockSpec((None, tr, cols), lambda i, p: (p[0], i, 0)),
                  pl.BlockSpec((N_CHIPS - 1, tr, cols), lambda i, p: (0, i, 0))],
        out_specs=pl.BlockSpec((tr, cols), lambda i, p: (p[1] * nt + i, 0)))
    return pl.pallas_call(
        body, name=name, grid_spec=grid_spec,
        out_shape=jax.ShapeDtypeStruct((rows, cols), F32),
        compiler_params=_params(("parallel",)),
    )(place, g, rsib, rici)


def _adam_math(w, g, m, v):
    m2 = ADAM_B1 * m + (1.0 - ADAM_B1) * g
    v2 = ADAM_B2 * v + (1.0 - ADAM_B2) * (g * g)
    m_hat = m2 / (1.0 - ADAM_B1 ** ADAM_STEP)
    v_hat = v2 / (1.0 - ADAM_B2 ** ADAM_STEP)
    delta = -ADAM_LR * (m_hat / (jnp.sqrt(v_hat) + ADAM_EPS) + ADAM_WD * w)
    return delta, m2, v2


def _adamw(w, g, m, v, name):
    rows, cols = w.shape
    tr = min(rows, ROW_TILE)

    def body(w_ref, g_ref, m_ref, v_ref, d_ref, m2_ref, v2_ref):
        d_ref[...], m2_ref[...], v2_ref[...] = _adam_math(w_ref[...], g_ref[...], m_ref[...], v_ref[...])

    spec = pl.BlockSpec((tr, cols), lambda i: (i, 0))
    return pl.pallas_call(
        body, name=name, grid=(rows // tr,),
        out_shape=(jax.ShapeDtypeStruct(w.shape, F32),) * 3,
        in_specs=[spec] * 4, out_specs=(spec,) * 3,
        compiler_params=_params(("parallel",)),
    )(w, g, m, v)


def _ada_grad_adamw(c_all_t, dmod_cols, w, m, v):
    d, wa = w.shape
    tn = 256

    def body(ct_ref, dm_ref, w_ref, m_ref, v_ref, g_ref, d_ref, m2_ref, v2_ref):
        act = _silu(ct_ref[...])
        g = act[:, 0:1] * dm_ref[0:1, :]
        for b in range(1, N_DEV):
            g = g + act[:, b:b + 1] * dm_ref[b:b + 1, :]
        g_ref[...] = g
        d_ref[...], m2_ref[...], v2_ref[...] = _adam_math(w_ref[...], g, m_ref[...], v_ref[...])

    spec = pl.BlockSpec((d, tn), lambda i: (0, i))
    return pl.pallas_call(
        body, name="ada_grad_adamw", grid=(wa // tn,),
        out_shape=(jax.ShapeDtypeStruct(w.shape, F32),) * 4,
        in_specs=[pl.BlockSpec((d, N_DEV), lambda i: (0, 0)), pl.BlockSpec((N_DEV, tn), lambda i: (0, i)),
                  spec, spec, spec],
        out_specs=(spec,) * 4,
        compiler_params=_params(("parallel",)),
    )(c_all_t, dmod_cols, w, m, v)


def _sum_devices(gathered):
    n = gathered.shape[1]

    def body(g_ref, o_ref):
        acc = g_ref[0:8, :]
        for dev in range(1, N_DEV):
            acc = acc + g_ref[8 * dev:8 * dev + 8, :]
        o_ref[...] = acc

    return pl.pallas_call(
        body, name="sum_devices",
        out_shape=jax.ShapeDtypeStruct((8, n), F32),
        in_specs=[VMEM], out_specs=VMEM,
    )(gathered)


def _pack_small(pieces):
    flat = [p.reshape(-1).astype(F32) for p in pieces]
    offsets, total = [], 0
    for p in flat:
        offsets.append(total)
        total += p.shape[0]
    padded = -(-total // SMALL_ALIGN) * SMALL_ALIGN
    if padded > total:
        flat.append(jnp.zeros((padded - total,), F32))
    return jnp.concatenate(flat).reshape(8, padded // 8), offsets


def _alibi_slope_rows(n_heads):
    slopes = 2.0 ** (-8.0 * jnp.arange(1, n_heads + 1, dtype=F32) / n_heads)
    rows = jnp.zeros((n_heads // 2, 8), F32).at[:, 0:2].set(slopes.reshape(n_heads // 2, 2))
    return jnp.broadcast_to(rows[:, :, None], (n_heads // 2, 8, ATT_KW))


def kernel(x, c, w_ada, b_ada, g_pre, w_in, conv_w, conv_b, g_conv, g_attn, w_out, g_post, loss_target, m_w_ada, m_b_ada, m_g_pre, m_w_in, m_conv_w, m_conv_b, m_g_conv, m_g_attn, m_w_out, m_g_post, v_w_ada, v_b_ada, v_g_pre, v_w_in, v_conv_w, v_conv_b, v_g_conv, v_g_attn, v_w_out, v_g_post):
    t, d = x.shape[1], x.shape[2]
    dc = conv_b.shape[1]
    da = g_attn.shape[1]
    hp = da // PAIR
    ws = w_in.shape[2]
    wa = w_ada.shape[2]
    cws = conv_w.shape[2]
    assert t % ROW_TILE == 0 and d % ROW_TILE == 0 and dc % COL_TILE == 0 and da % COL_TILE == 0
    assert ws == 2 * dc and dc == da and t // BRANCHES[-1][1] >= ATT_BQ

    mx, my, mc = _my_place()
    chip = _chip_of(mx, my)
    dev = 2 * chip + mc
    place = jnp.stack([chip, mc]).astype(jnp.int32)

    x2, tgt2 = x[0], loss_target[0]
    w_ada2, w_in2, w_out2 = w_ada[0], w_in[0], w_out[0]

    win_flight, wout_flight, send_in, recv_in, send_out, recv_out, started = _gather_start(
        _cast_into_slot(place, w_in2, "cast_w_in"), _cast_into_slot(place, w_out2, "cast_w_out"))

    packed, offs = _pack_small([c[0] + started[0, 0], conv_w[0]])
    seen = _allgather8(packed, "gather_inputs").reshape(N_DEV, -1)
    c_all = seen[:, offs[0]:offs[0] + d]
    conv_w_full = seen[0::2, offs[1]:offs[1] + 3 * cws].reshape(N_CHIPS, 3, cws).transpose(1, 0, 2).reshape(3, dc)

    ada_part = _ada_partial(c_all, w_ada2)
    ada_seen = _allgather8(ada_part, "gather_ada").reshape(N_DEV, N_DEV, wa)
    mod_flat = lax.dynamic_index_in_dim(ada_seen[0::2], dev, axis=1, keepdims=False).reshape(1, 3 * d) + b_ada
    mod = mod_flat.reshape(3, d)

    h, ht = _prenorm(x2, mod, g_pre)
    winf = _forward_halves(_gather_wait(win_flight, send_in, recv_in, h, "gather_wait_w_in"), "forward_w_in")
    conv_proj = _proj_conv(h, winf)
    attn_proj = _proj_attn(h, winf, da)
    slopes = _alibi_slope_rows(da // HEAD_DIM)
    co = _conv_fwd(conv_proj, conv_w_full, conv_b, dc)
    o_mix, lse = _attn_fwd(attn_proj, slopes)
    g_attn_pairs = g_attn.reshape(hp, 1, PAIR)
    ycat, ycat_t = _mix_fwd(co, conv_proj, attn_proj, o_mix, g_conv, g_attn_pairs)
    woutf4 = _forward_halves(_gather_wait(wout_flight, send_out, recv_out, ycat, "gather_wait_w_out"), "forward_w_out")
    woutf = woutf4.reshape(dc + da, d)
    dout, dy, post_sums = _out_fwd_bwd(ycat, woutf, x2, tgt2, mod, g_post)

    gout = _matmul_nn(ycat_t, dy, BF16, "dw_out").reshape(N_CHIPS, (dc + da) // N_CHIPS, d)
    rsib_out = _swap_halves(gout, "rs_swap_halves_out")
    csum_out = _chip_sums(place, gout, rsib_out, "rs_chip_sum_out")
    ssem_out, rsem_out, csum_out, land_out, sent_out = _owners_start(csum_out, "rs_owners_start_out")
    dycat = _matmul_nt(dy, woutf, BF16, "dycat")
    dconv_proj, dco, dattn, d_o, delta, dg_conv, dg_attn = _mix_bwd(
        dycat, co, conv_proj, attn_proj, o_mix, g_conv, g_attn_pairs)
    dconv_proj, conv_sums = _conv_bwd(dconv_proj, dco, conv_proj, conv_w_full, dc, sent_out)
    dattn = _attn_bwd(dattn, attn_proj, d_o, lse, delta, slopes, sent_out)
    gin = _dw_in(ht, dconv_proj, dattn, ws, da)
    rsib_in = _swap_halves(gin, "rs_swap_halves_in")
    csum_in = _chip_sums(place, gin, rsib_in, "rs_chip_sum_in")
    ssem_in, rsem_in, csum_in, land_in, sent_in = _owners_start(csum_in, "rs_owners_start_in")
    dh = _dh(dconv_proj, dattn, winf, da, sent_in)
    grad_x, pre_sums = _prenorm_bwd(x2, dh, dout, mod, g_pre)

    rici_out = _owners_wait(ssem_out, rsem_out, csum_out, land_out, [grad_x], "rs_owners_wait_out")
    grad_w_out = _join_halves(_owner_sum(place, gout, rsib_out, rici_out, "rs_owner_sum_out"), "rs_join_halves_out")

    small, so = _pack_small([
        pre_sums[0], pre_sums[1], post_sums[0],
        pre_sums[2], conv_sums[0:3], conv_sums[3], dg_conv, dg_attn, post_sums[1], post_sums[2, 0:128]])
    small_seen = _allgather8(small, "gather_small")
    total = _sum_devices(small_seen).reshape(-1)
    dmod_all = small_seen.reshape(N_DEV, -1)[:, 0:3 * d]
    loss = total[so[9]]
    grad_b_ada = total[0:3 * d].reshape(1, 3 * d)
    grad_g_pre = total[so[3]:so[3] + d].reshape(1, d)
    grad_conv_w_full = total[so[4]:so[4] + 3 * dc].reshape(3, dc)
    grad_conv_w = lax.dynamic_slice_in_dim(grad_conv_w_full, chip * cws, cws, axis=1).reshape(1, 3, cws)
    grad_conv_b = total[so[5]:so[5] + dc].reshape(1, dc)
    grad_g_conv = total[so[6]:so[6] + dc].reshape(1, dc)
    grad_g_attn = total[so[7]:so[7] + da].reshape(1, da)
    grad_g_post = total[so[8]:so[8] + d].reshape(1, d)

    dmod_cols = lax.dynamic_slice_in_dim(dmod_all, chip * wa, wa, axis=1)
    grad_w_ada, delta_w_ada, new_m_w_ada, new_v_w_ada = _ada_grad_adamw(c_all.T, dmod_cols, w_ada2, m_w_ada[0], v_w_ada[0])
    delta_w_out, new_m_w_out, new_v_w_out = _adamw(w_out2, grad_w_out, m_w_out[0], v_w_out[0], "adamw_w_out")

    small_w = [b_ada, g_pre, conv_w, conv_b, g_conv, g_attn, g_post]
    small_g = [grad_b_ada, grad_g_pre, grad_conv_w, grad_conv_b, grad_g_conv, grad_g_attn, grad_g_post]
    small_m = [m_b_ada, m_g_pre, m_conv_w, m_conv_b, m_g_conv, m_g_attn, m_g_post]
    small_v = [v_b_ada, v_g_pre, v_conv_w, v_conv_b, v_g_conv, v_g_attn, v_g_post]
    pw, po = _pack_small(small_w)
    pg, _ = _pack_small(small_g)
    pm, _ = _pack_small(small_m)
    pv, _ = _pack_small(small_v)
    sd, sm, sv = (a.reshape(-1) for a in _adamw(pw, pg, pm, pv, "adamw_small"))

    def unpack(flat):
        return [flat[o:o + w.size].reshape(w.shape) for o, w in zip(po, small_w)]

    d_small, m_small, v_small = unpack(sd), unpack(sm), unpack(sv)

    rici_in = _owners_wait(ssem_in, rsem_in, csum_in, land_in, [sd, delta_w_out, delta_w_ada], "rs_owners_wait_in")
    grad_w_in = _join_halves(_owner_sum(place, gin, rsib_in, rici_in, "rs_owner_sum_in"), "rs_join_halves_in")
    delta_w_in, new_m_w_in, new_v_w_in = _adamw(w_in2, grad_w_in, m_w_in[0], v_w_in[0], "adamw_w_in")

    def lead(a):
        return a.reshape((1,) + a.shape)

    grads = [lead(grad_w_ada), grad_b_ada, grad_g_pre, lead(grad_w_in), grad_conv_w, grad_conv_b, grad_g_conv,
             grad_g_attn, lead(grad_w_out), grad_g_post]
    deltas = [lead(delta_w_ada), d_small[0], d_small[1], lead(delta_w_in), d_small[2], d_small[3], d_small[4],
              d_small[5], lead(delta_w_out), d_small[6]]
    new_ms = [lead(new_m_w_ada), m_small[0], m_small[1], lead(new_m_w_in), m_small[2], m_small[3], m_small[4],
              m_small[5], lead(new_m_w_out), m_small[6]]
    new_vs = [lead(new_v_w_ada), v_small[0], v_small[1], lead(new_v_w_in), v_small[2], v_small[3], v_small[4],
              v_small[5], lead(new_v_w_out), v_small[6]]
    return (loss, lead(grad_x), *grads, *deltas, *new_ms, *new_vs)
```

```python
import jax
import jax.numpy as jnp
from jax import lax
from jax.experimental import pallas as pl
from jax.experimental.pallas import tpu as pltpu

F32 = jnp.float32
BF16 = jnp.bfloat16
MESH = pl.DeviceIdType.MESH
HBM = pl.BlockSpec(memory_space=pltpu.HBM)
VMEM = pl.BlockSpec(memory_space=pltpu.VMEM)
ANY = pl.BlockSpec(memory_space=pl.ANY)
SEM = pl.BlockSpec(memory_space=pltpu.SEMAPHORE)
EFFECT = pltpu.SideEffectType.DATAFLOW_SIDE_EFFECTING
TOKEN = jax.ShapeDtypeStruct((8, 128), jnp.float32)

HEAD_DIM = 64
PAIR = 2 * HEAD_DIM
BRANCHES = ((128, 1), (512, 4), (2048, 16))
SIDE = 64
EPS = 1e-6
NEG_INF = -1e30
N_CHIPS = 4
N_DEV = 8

ADAM_LR = 0.001
ADAM_B1 = 0.9
ADAM_B2 = 0.999
ADAM_EPS = 1e-08
ADAM_WD = 0.01
ADAM_STEP = 10

VMEM_LIMIT_BYTES = 56 * 1024 * 1024
ROW_TILE = 256
COL_TILE = 512
CONV_TILE = 256
ATT_BQ = 128
ATT_KW = ATT_BQ + 2 * SIDE
ATT_UNROLL = 4
ATT_UNROLL_BWD = 2
SMALL_ALIGN = 1024


def _params(semantics=None):
    kw = {"vmem_limit_bytes": VMEM_LIMIT_BYTES}
    if semantics is not None:
        kw["dimension_semantics"] = semantics
    return pltpu.CompilerParams(**kw)


def _silu(z):
    return z * jax.nn.sigmoid(z)


def _silu_grad(z):
    s = jax.nn.sigmoid(z)
    return s * (1.0 + z * (1.0 - s))


def _my_place():
    return lax.axis_index("x"), lax.axis_index("y"), lax.axis_index("c")


def _flip(a, bit):
    return 1 - a if bit else a


def _chip_of(x, y):
    return 2 * x + y


def _allgather8(v, name):
    rows_per, n = v.shape

    def body(v_ref, out_ref, send_sems, recv_sems):
        x, y, c = _my_place()
        me = 4 * x + 2 * y + c

        def rows(idx):
            return out_ref.at[pl.ds(pl.multiple_of(idx * rows_per, rows_per), rows_per), :]

        out_ref[pl.ds(pl.multiple_of(me * rows_per, rows_per), rows_per), :] = v_ref[...]
        copies = []
        for k in range(1, N_DEV):
            peer = (_flip(x, k & 4), _flip(y, k & 2), _flip(c, k & 1))
            cp = pltpu.make_async_remote_copy(
                src_ref=v_ref, dst_ref=rows(me), send_sem=send_sems.at[k - 1], recv_sem=recv_sems.at[k - 1],
                device_id=peer, device_id_type=MESH)
            cp.start()
            copies.append((cp, peer))
        for k, (cp, peer) in enumerate(copies):
            src = 4 * peer[0] + 2 * peer[1] + peer[2]
            pltpu.make_async_remote_copy(
                src_ref=v_ref, dst_ref=rows(src), send_sem=send_sems.at[k], recv_sem=recv_sems.at[k],
                device_id=peer, device_id_type=MESH).wait_recv()
        for cp, _ in copies:
            cp.wait_send()

    return pl.pallas_call(
        body, name=name,
        out_shape=jax.ShapeDtypeStruct((N_DEV * rows_per, n), v.dtype),
        in_specs=[VMEM], out_specs=VMEM,
        scratch_shapes=[pltpu.SemaphoreType.DMA((N_DEV - 1,)), pltpu.SemaphoreType.DMA((N_DEV - 1,))],
    )(v)


def _half_rows(ref, chip, which, half):
    return ref.at[chip, pl.ds(pl.multiple_of(which * half, half), half), :]


def _ici_peers(x, y, c):
    peers = [(_flip(x, k & 2), _flip(y, k & 1), c) for k in (1, 2, 3)]
    return [(peer, _chip_of(peer[0], peer[1])) for peer in peers]


def _part_rows(ref, chip, core, part):
    quarter = ref.shape[1] // 4
    return ref.at[chip, pl.ds(pl.multiple_of((2 * core + part) * quarter, quarter), quarter), :]


def _neighbours(x, y, c):
    return [((x, 1 - y, c), _chip_of(x, 1 - y)), ((1 - x, y, c), _chip_of(1 - x, y)),
            ((1 - x, 1 - y, c), _chip_of(1 - x, 1 - y))]


def _start_direct(buf, send_sems, recv_sems):
    x, y, c = _my_place()
    me = _chip_of(x, y)
    for n, (peer, _) in enumerate(_neighbours(x, y, c)[0:2]):
        for part in ((0, 1), (1, 0))[n]:
            piece = _part_rows(buf, me, c, part)
            pltpu.make_async_remote_copy(
                src_ref=piece, dst_ref=piece, send_sem=send_sems.at[2 * n + part], recv_sem=recv_sems.at[2 * n + part],
                device_id=peer, device_id_type=MESH).start()


def _relay(buf, recv_sems, relay_send, relay_recv):
    x, y, c = _my_place()
    nbrs = _neighbours(x, y, c)
    for n in range(2):
        part = n
        piece = _part_rows(buf, nbrs[n][1], c, part)
        pltpu.make_async_remote_copy(
            src_ref=piece, dst_ref=piece, send_sem=relay_send.at[part], recv_sem=recv_sems.at[2 * n + part],
            device_id=nbrs[n][0], device_id_type=MESH).wait_recv()
        pltpu.make_async_remote_copy(
            src_ref=piece, dst_ref=piece, send_sem=relay_send.at[part], recv_sem=relay_recv.at[part],
            device_id=nbrs[1 - n][0], device_id_type=MESH).start()


def _gather_start(win_slots, after):
    def body(win_in, after_ref, win_ref, send_sems, recv_sems, token_ref):
        del win_in, after_ref
        _start_direct(win_ref, send_sems, recv_sems)
        token_ref[...] = jnp.zeros(token_ref.shape, F32)

    sems = pltpu.SemaphoreType.DMA((4,))
    return pl.pallas_call(
        body, name="gather_start",
        out_shape=(jax.ShapeDtypeStruct(win_slots.shape, win_slots.dtype), sems, sems, TOKEN),
        in_specs=[HBM, ANY], out_specs=(HBM, SEM, SEM, VMEM),
        input_output_aliases={0: 0},
        compiler_params=pltpu.CompilerParams(has_side_effects=EFFECT),
    )(win_slots, after)


def _gather_relay_in(win, wout_slots, recv_in, after):
    def body(win_in, wout_in, recv_in_ref, after_ref, win_ref, wout_ref, relay_send, relay_recv, send_out, recv_out):
        del win_in, wout_in, after_ref
        _relay(win_ref, recv_in_ref, relay_send, relay_recv)
        _start_direct(wout_ref, send_out, recv_out)

    two, four = pltpu.SemaphoreType.DMA((2,)), pltpu.SemaphoreType.DMA((4,))
    return pl.pallas_call(
        body, name="gather_relay_w_in",
        out_shape=(jax.ShapeDtypeStruct(win.shape, win.dtype), jax.ShapeDtypeStruct(wout_slots.shape, wout_slots.dtype),
                   two, two, four, four),
        in_specs=[HBM, HBM, SEM, ANY], out_specs=(HBM, HBM, SEM, SEM, SEM, SEM),
        input_output_aliases={0: 0, 1: 1},
        compiler_params=pltpu.CompilerParams(has_side_effects=EFFECT),
    )(win, wout_slots, recv_in, after)


def _gather_relay_out(wout, recv_out, after):
    def body(wout_in, recv_out_ref, after_ref, wout_ref, relay_send, relay_recv):
        del wout_in, after_ref
        _relay(wout_ref, recv_out_ref, relay_send, relay_recv)

    two = pltpu.SemaphoreType.DMA((2,))
    return pl.pallas_call(
        body, name="gather_relay_w_out",
        out_shape=(jax.ShapeDtypeStruct(wout.shape, wout.dtype), two, two),
        in_specs=[HBM, SEM, ANY], out_specs=(HBM, SEM, SEM),
        input_output_aliases={0: 0},
        compiler_params=pltpu.CompilerParams(has_side_effects=EFFECT),
    )(wout, recv_out, after)


def _gather_wait(buf, send_sems, recv_sems, relay_send, relay_recv, after, name):
    def body(buf_in, send_ref, recv_ref, rsend_ref, rrecv_ref, after_ref, buf_ref):
        del buf_in, after_ref
        x, y, c = _my_place()
        me = _chip_of(x, y)
        nbrs = _neighbours(x, y, c)
        for n in range(2):
            peer, chip = nbrs[n]
            second = 1 - n
            pltpu.make_async_remote_copy(
                src_ref=_part_rows(buf_ref, me, c, second), dst_ref=_part_rows(buf_ref, chip, c, second),
                send_sem=send_ref.at[2 * n + second], recv_sem=recv_ref.at[2 * n + second],
                device_id=peer, device_id_type=MESH).wait_recv()
            pltpu.make_async_remote_copy(
                src_ref=_part_rows(buf_ref, chip, c, n), dst_ref=_part_rows(buf_ref, nbrs[2][1], c, n),
                send_sem=rsend_ref.at[n], recv_sem=rrecv_ref.at[n],
                device_id=nbrs[1 - n][0], device_id_type=MESH).wait_recv()
        for n in range(2):
            peer, chip = nbrs[n]
            for part in range(2):
                piece = _part_rows(buf_ref, me, c, part)
                pltpu.make_async_remote_copy(
                    src_ref=piece, dst_ref=piece, send_sem=send_ref.at[2 * n + part], recv_sem=recv_ref.at[2 * n + part],
                    device_id=peer, device_id_type=MESH).wait_send()
            relayed = _part_rows(buf_ref, chip, c, n)
            pltpu.make_async_remote_copy(
                src_ref=relayed, dst_ref=relayed, send_sem=rsend_ref.at[n], recv_sem=rrecv_ref.at[n],
                device_id=nbrs[1 - n][0], device_id_type=MESH).wait_send()

    return pl.pallas_call(
        body, name=name,
        out_shape=jax.ShapeDtypeStruct(buf.shape, buf.dtype),
        in_specs=[HBM, SEM, SEM, SEM, SEM, ANY], out_specs=HBM,
        input_output_aliases={0: 0},
        compiler_params=pltpu.CompilerParams(has_side_effects=EFFECT),
    )(buf, send_sems, recv_sems, relay_send, relay_recv, after)


def _forward_halves(buf, name):
    half = buf.shape[1] // 2

    def body(buf_in, buf_ref, send_sems, recv_sems):
        del buf_in
        x, y, c = _my_place()
        sibling = (x, y, 1 - c)
        started = []
        for k, (_, src_chip) in enumerate(_ici_peers(x, y, c)):
            landed = _half_rows(buf_ref, src_chip, c, half)
            fw = pltpu.make_async_remote_copy(
                src_ref=landed, dst_ref=landed, send_sem=send_sems.at[k], recv_sem=recv_sems.at[k],
                device_id=sibling, device_id_type=MESH)
            fw.start()
            started.append(fw)
        for k, (_, src_chip) in enumerate(_ici_peers(x, y, c)):
            other = _half_rows(buf_ref, src_chip, 1 - c, half)
            pltpu.make_async_remote_copy(
                src_ref=other, dst_ref=other, send_sem=send_sems.at[k], recv_sem=recv_sems.at[k],
                device_id=sibling, device_id_type=MESH).wait_recv()
        for fw in started:
            fw.wait_send()

    return pl.pallas_call(
        body, name=name,
        out_shape=jax.ShapeDtypeStruct(buf.shape, buf.dtype),
        in_specs=[HBM], out_specs=HBM,
        input_output_aliases={0: 0},
        scratch_shapes=[pltpu.SemaphoreType.DMA((N_CHIPS - 1,))] * 2,
    )(buf)


def _swap_halves(g, name):
    half = g.shape[1] // 2

    def body(g_ref, r_ref, send_sem, recv_sem):
        x, y, c = _my_place()
        theirs = g_ref.at[:, pl.ds(pl.multiple_of((1 - c) * half, half), half), :]
        cp = pltpu.make_async_remote_copy(
            src_ref=theirs, dst_ref=r_ref, send_sem=send_sem, recv_sem=recv_sem,
            device_id=(x, y, 1 - c), device_id_type=MESH)
        cp.start()
        cp.wait()

    return pl.pallas_call(
        body, name=name,
        out_shape=jax.ShapeDtypeStruct((N_CHIPS, half, g.shape[2]), g.dtype),
        in_specs=[HBM], out_specs=HBM,
        scratch_shapes=[pltpu.SemaphoreType.DMA, pltpu.SemaphoreType.DMA],
    )(g)


def _owners_start(csum, name):
    land = pltpu.with_memory_space_constraint(lax.empty((N_CHIPS - 1,) + csum.shape[1:], csum.dtype), pltpu.HBM)

    def body(csum_ref, land_ref, send_sems, recv_sems, csum_thru, land_thru, token_ref):
        del csum_thru, land_thru
        x, y, c = _my_place()
        for k, (peer, owner) in enumerate(_ici_peers(x, y, c)):
            pltpu.make_async_remote_copy(
                src_ref=csum_ref.at[owner], dst_ref=land_ref.at[k], send_sem=send_sems.at[k], recv_sem=recv_sems.at[k],
                device_id=peer, device_id_type=MESH).start()
        token_ref[...] = jnp.zeros(token_ref.shape, F32)

    sems = pltpu.SemaphoreType.DMA((N_CHIPS - 1,))
    return pl.pallas_call(
        body, name=name,
        out_shape=(sems, sems, jax.ShapeDtypeStruct(csum.shape, csum.dtype),
                   jax.ShapeDtypeStruct(land.shape, land.dtype), TOKEN),
        in_specs=[HBM, HBM], out_specs=(SEM, SEM, HBM, HBM, VMEM),
        input_output_aliases={0: 2, 1: 3},
        compiler_params=pltpu.CompilerParams(has_side_effects=EFFECT),
    )(pltpu.with_memory_space_constraint(csum, pltpu.HBM), land)


def _owners_wait(send_sems, recv_sems, csum, land, after, name):
    def body(csum_ref, land_ref, send_ref, recv_ref, *rest):
        del rest
        x, y, c = _my_place()
        for k, (peer, owner) in enumerate(_ici_peers(x, y, c)):
            cp = pltpu.make_async_remote_copy(
                src_ref=csum_ref.at[owner], dst_ref=land_ref.at[k], send_sem=send_ref.at[k], recv_sem=recv_ref.at[k],
                device_id=peer, device_id_type=MESH)
            cp.wait_send()
            cp.wait_recv()

    return pl.pallas_call(
        body, name=name,
        out_shape=(jax.ShapeDtypeStruct(csum.shape, csum.dtype), jax.ShapeDtypeStruct(land.shape, land.dtype)),
        in_specs=[HBM, HBM, SEM, SEM] + [ANY] * len(after), out_specs=(HBM, HBM),
        input_output_aliases={0: 0, 1: 1},
        compiler_params=pltpu.CompilerParams(has_side_effects=EFFECT),
    )(csum, land, send_sems, recv_sems, *after)[1]


def _join_halves(full, name):
    rows = full.shape[0] // 2

    def body(full_in, full_ref, send_sem, recv_sem):
        del full_in
        x, y, c = _my_place()
        sibling = (x, y, 1 - c)
        mine = full_ref.at[pl.ds(pl.multiple_of(c * rows, rows), rows), :]
        theirs = full_ref.at[pl.ds(pl.multiple_of((1 - c) * rows, rows), rows), :]
        cp = pltpu.make_async_remote_copy(
            src_ref=mine, dst_ref=mine, send_sem=send_sem, recv_sem=recv_sem, device_id=sibling, device_id_type=MESH)
        cp.start()
        pltpu.make_async_remote_copy(
            src_ref=theirs, dst_ref=theirs, send_sem=send_sem, recv_sem=recv_sem,
            device_id=sibling, device_id_type=MESH).wait_recv()
        cp.wait_send()

    return pl.pallas_call(
        body, name=name,
        out_shape=jax.ShapeDtypeStruct(full.shape, full.dtype),
        in_specs=[HBM], out_specs=HBM,
        input_output_aliases={0: 0},
        scratch_shapes=[pltpu.SemaphoreType.DMA, pltpu.SemaphoreType.DMA],
    )(full)


def _cast_into_slot(place, w, name):
    rows, cols = w.shape
    tr = min(rows, ROW_TILE)

    def body(place_ref, w_ref, o_ref):
        del place_ref
        o_ref[...] = w_ref[...].astype(BF16)

    grid_spec = pltpu.PrefetchScalarGridSpec(
        num_scalar_prefetch=1, grid=(rows // tr,),
        in_specs=[pl.BlockSpec((tr, cols), lambda i, p: (i, 0))],
        out_specs=pl.BlockSpec((None, tr, cols), lambda i, p: (p[0], i, 0)))
    return pl.pallas_call(
        body, name=name, grid_spec=grid_spec,
        out_shape=jax.ShapeDtypeStruct((N_CHIPS, rows, cols), BF16),
        compiler_params=_params(("parallel",)),
    )(place, w)


def _ada_partial(c_all, w_ada):
    d_model, wa = w_ada.shape
    tn = 512 if wa % 512 == 0 else 256

    def body(c_ref, w_ref, o_ref):
        o_ref[...] = jnp.dot(_silu(c_ref[...]), w_ref[...], precision=lax.Precision.HIGHEST,
                             preferred_element_type=F32)

    return pl.pallas_call(
        body, name="ada_partial", grid=(wa // tn,),
        out_shape=jax.ShapeDtypeStruct((N_DEV, wa), F32),
        in_specs=[pl.BlockSpec((N_DEV, d_model), lambda i: (0, 0)), pl.BlockSpec((d_model, tn), lambda i: (0, i))],
        out_specs=pl.BlockSpec((N_DEV, tn), lambda i: (0, i)),
        compiler_params=_params(("parallel",)),
    )(c_all, w_ada)


def _prenorm(x, mod, g_pre):
    t, d = x.shape
    tb = ROW_TILE

    def body(x_ref, mod_ref, g_ref, h_ref, ht_ref):
        xv = x_ref[...]
        r = lax.rsqrt(jnp.mean(xv * xv, axis=-1, keepdims=True) + EPS)
        h = (xv * r) * g_ref[...] * (1.0 + mod_ref[1:2, :]) + mod_ref[0:1, :]
        h_ref[...] = h.astype(BF16)
        ht_ref[...] = h.T.astype(BF16)

    return pl.pallas_call(
        body, name="prenorm", grid=(t // tb,),
        out_shape=(jax.ShapeDtypeStruct((t, d), BF16), jax.ShapeDtypeStruct((d, t), BF16)),
        in_specs=[pl.BlockSpec((tb, d), lambda i: (i, 0)), pl.BlockSpec((3, d), lambda i: (0, 0)),
                  pl.BlockSpec((1, d), lambda i: (0, 0))],
        out_specs=(pl.BlockSpec((tb, d), lambda i: (i, 0)), pl.BlockSpec((d, tb), lambda i: (0, i))),
        compiler_params=_params(("parallel",)),
    )(x, mod, g_pre)


def _pairs_to_cols(ref, n_pairs):
    return jnp.concatenate([ref[i] for i in range(n_pairs)], axis=1)


def _proj_conv(h, winf):
    t, d = h.shape
    ws = winf.shape[2]
    tn = COL_TILE
    nt = ws // tn

    def body(a_ref, b_ref, o_ref):
        o_ref[...] = jnp.dot(a_ref[...], b_ref[...], preferred_element_type=F32).astype(BF16)

    return pl.pallas_call(
        body, name="proj_conv", grid=(2, nt),
        out_shape=jax.ShapeDtypeStruct((t, 2 * ws), BF16),
        in_specs=[pl.BlockSpec((t, d), lambda j, n: (0, 0)), pl.BlockSpec((None, d, tn), lambda j, n: (j, 0, n))],
        out_specs=pl.BlockSpec((t, tn), lambda j, n: (0, j * nt + n)),
        compiler_params=_params(("parallel", "parallel")),
    )(h, winf)


def _proj_attn(h, winf, da):
    t, d = h.shape
    ws = winf.shape[2]
    tn = COL_TILE
    nt = ws // tn
    per_comp = da // tn
    pairs = tn // PAIR

    def body(a_ref, b_ref, o_ref):
        res = jnp.dot(a_ref[...], b_ref[...], preferred_element_type=F32).astype(BF16)
        for i in range(pairs):
            o_ref[i] = res[:, i * PAIR:(i + 1) * PAIR]

    return pl.pallas_call(
        body, name="proj_attn", grid=(2, nt),
        out_shape=jax.ShapeDtypeStruct((4, da // PAIR, t, PAIR), BF16),
        in_specs=[pl.BlockSpec((t, d), lambda j, n: (0, 0)), pl.BlockSpec((None, d, tn), lambda j, n: (2 + j, 0, n))],
        out_specs=pl.BlockSpec((None, pairs, t, PAIR), lambda j, n: (2 * j + n // per_comp, n % per_comp, 0, 0)),
        compiler_params=_params(("parallel", "parallel")),
    )(h, winf)


def _shift_rows(a, rows):
    idx = lax.broadcasted_iota(jnp.int32, a.shape, 0)
    prev = jnp.where(idx == 0, 0.0, pltpu.roll(a, 1, 0))
    nxt = jnp.where(idx == rows - 1, 0.0, pltpu.roll(a, rows - 1, 0))
    return prev, nxt


def _conv_fwd(conv_proj, conv_w, conv_b, dc):
    t = conv_proj.shape[0]
    ct = CONV_TILE
    nct = dc // ct

    def body(u_ref, cg_ref, w_ref, b_ref, co_ref):
        a = cg_ref[...].astype(F32) * u_ref[...].astype(F32)
        prev, nxt = _shift_rows(a, t)
        co_ref[...] = w_ref[0:1, :] * prev + w_ref[1:2, :] * a + w_ref[2:3, :] * nxt + b_ref[...]

    return pl.pallas_call(
        body, name="conv_fwd", grid=(nct,),
        out_shape=jax.ShapeDtypeStruct((t, dc), F32),
        in_specs=[pl.BlockSpec((t, ct), lambda i: (0, i)), pl.BlockSpec((t, ct), lambda i: (0, 2 * nct + i)),
                  pl.BlockSpec((3, ct), lambda i: (0, i)), pl.BlockSpec((1, ct), lambda i: (0, i))],
        out_specs=pl.BlockSpec((t, ct), lambda i: (0, i)),
        compiler_params=_params(("parallel",)),
    )(conv_proj, conv_proj, conv_w, conv_b)


def _to_residue_major(src_ref, dst_ref, r, scale=None):
    t = src_ref.shape[0]
    seq = t // r
    for res in range(r):
        rows = src_ref[pl.ds(res, seq, stride=r), :] if r > 1 else src_ref[...]
        if scale is not None:
            rows = rows * scale
        dst_ref[res * seq:(res + 1) * seq, :] = rows.astype(dst_ref.dtype)


def _fill_bias_tiles(bias_ref, sl_ref, r, kw, transposed=False):
    shape = (kw, ATT_BQ) if transposed else (ATT_BQ, kw)
    key_axis, query_axis = (0, 1) if transposed else (1, 0)
    base = lax.broadcasted_iota(jnp.int32, shape, key_axis) - lax.broadcasted_iota(jnp.int32, shape, query_axis)
    for hh in range(2):
        slope = -(sl_ref[hh:hh + 1, 0:shape[1]] * float(r))
        for e, shift in enumerate((0, -SIDE, ATT_BQ - kw)):
            arel = jnp.abs(base + shift)
            bias_ref[hh, e, 0:shape[0], 0:shape[1]] = jnp.where(arel <= SIDE, arel.astype(F32) * slope, NEG_INF)


def _first_head_lanes():
    return lax.broadcasted_iota(jnp.int32, (1, PAIR), 1) < HEAD_DIM


def _only_head(x, first, hh):
    return jnp.where(first if hh == 0 else jnp.logical_not(first), x, jnp.zeros_like(x))


def _all_lanes(x, first, hh):
    other = pltpu.roll(x, HEAD_DIM, 1)
    return jnp.where(first, x, other) if hh == 0 else jnp.where(first, other, x)


def _block_place(g, seq_len, kw):
    nqb = seq_len // ATT_BQ
    if nqb == 1:
        row = pl.multiple_of(g * ATT_BQ, ATT_BQ)
        return row, row, 0
    res = g // nqb
    qb = g - res * nqb
    q0 = qb * ATT_BQ
    ks = jnp.clip(q0 - SIDE, 0, seq_len - kw)
    edge = jnp.where(qb == 0, 0, jnp.where(qb == nqb - 1, 2, 1))
    return (pl.multiple_of(res * seq_len + q0, ATT_BQ), pl.multiple_of(res * seq_len + ks, SIDE), edge)


def _attn_fwd(attn_proj, slopes):
    _, hp, t, _ = attn_proj.shape
    n_blocks = t // ATT_BQ

    def body(qkv_ref, sl_ref, o_ref, lse_ref, stage, dil, bias, o_res, l_res, o_tok, l_tok):
        for b, (_, r) in enumerate(BRANCHES):
            seq_len = t // r
            kw = min(ATT_KW, seq_len)
            for comp in range(3):
                stage[...] = qkv_ref[comp].astype(F32)
                _to_residue_major(stage, dil.at[comp], r, scale=HEAD_DIM ** -0.5 if comp == 0 else None)
            _fill_bias_tiles(bias, sl_ref, r, kw)
            o_dst, l_dst = (o_tok.at[b], l_tok.at[b]) if r == 1 else (o_res, l_res)
            first = _first_head_lanes()

            def block(g, carry, seq_len=seq_len, kw=kw, o_dst=o_dst, l_dst=l_dst, first=first):
                qrow, krow, edge = _block_place(g, seq_len, kw)
                q = dil[0, pl.ds(qrow, ATT_BQ), :]
                k = dil[1, pl.ds(krow, kw), :]
                v = dil[2, pl.ds(krow, kw), :]
                ones = jnp.ones((kw, PAIR), BF16)
                both, tops = None, []
                for hh in range(2):
                    s = lax.dot_general(_only_head(q, first, hh), k, (((1,), (1,)), ((), ())),
                                        preferred_element_type=F32)
                    s = s + bias[hh, edge, :, 0:kw]
                    m = jnp.max(s, axis=-1, keepdims=True)
                    p = jnp.exp(s - m).astype(BF16)
                    rhs = jnp.concatenate([_only_head(v, first, hh), _only_head(ones, first, hh)], axis=1)
                    part = jnp.dot(p, rhs, preferred_element_type=F32)
                    both = part if both is None else both + part
                    tops.append(m)
                den = both[:, PAIR:]
                o_dst[pl.ds(qrow, ATT_BQ), :] = both[:, 0:PAIR] / den
                l_dst[pl.ds(qrow, ATT_BQ), :] = jnp.where(first, tops[0], tops[1]) + jnp.log(den)
                return carry

            lax.fori_loop(0, n_blocks, block, 0, unroll=ATT_UNROLL)
            if r > 1:
                for res in range(r):
                    rows = slice(res * seq_len, (res + 1) * seq_len)
                    o_tok[b, pl.ds(res, seq_len, stride=r), :] = o_res[rows, :]
                    l_tok[b, pl.ds(res, seq_len, stride=r), :] = l_res[rows, :]

        def merge(i, carry):
            rows = pl.ds(pl.multiple_of(i * ROW_TILE, ROW_TILE), ROW_TILE)
            la, lb, lc = l_tok[0, rows, :], l_tok[1, rows, :], l_tok[2, rows, :]
            m = jnp.maximum(jnp.maximum(la, lb), lc)
            wa, wb, wc = jnp.exp(la - m), jnp.exp(lb - m), jnp.exp(lc - m)
            den = wa + wb + wc
            o_ref[rows, :] = (wa * o_tok[0, rows, :] + wb * o_tok[1, rows, :] + wc * o_tok[2, rows, :]) * (1.0 / den)
            lse_ref[rows, :] = m + jnp.log(den)
            return carry

        lax.fori_loop(0, t // ROW_TILE, merge, 0)

    pair_spec = pl.BlockSpec((None, t, PAIR), lambda h: (h, 0, 0))
    return pl.pallas_call(
        body, name="attn_fwd", grid=(hp,),
        out_shape=(jax.ShapeDtypeStruct((hp, t, PAIR), F32), jax.ShapeDtypeStruct((hp, t, PAIR), F32)),
        in_specs=[pl.BlockSpec((3, None, t, PAIR), lambda h: (0, h, 0, 0)),
                  pl.BlockSpec((None, 8, ATT_KW), lambda h: (h, 0, 0))],
        out_specs=(pair_spec, pair_spec),
        scratch_shapes=[pltpu.VMEM((t, PAIR), F32), pltpu.VMEM((3, t, PAIR), BF16),
                        pltpu.VMEM((2, 3, ATT_BQ, ATT_KW), F32),
                        pltpu.VMEM((t, PAIR), F32), pltpu.VMEM((t, PAIR), F32),
                        pltpu.VMEM((3, t, PAIR), F32), pltpu.VMEM((3, t, PAIR), F32)],
        compiler_params=_params(("parallel",)),
    )(attn_proj, slopes)


def _attn_bwd(dattn, attn_proj, d_o, lse, delta, slopes, after):
    _, hp, t, _ = attn_proj.shape
    n_blocks = t // ATT_BQ

    def body(dattn_in, qkv_ref, do_ref, lse_ref, dl_ref, sl_ref, after_ref, out_ref,
             stage, dil, lse_res, dl_res, lse_row, dl_row, bias, bias_t, acc, tot):
        del dattn_in, after_ref
        for b, (_, r) in enumerate(BRANCHES):
            seq_len = t // r
            kw = min(ATT_KW, seq_len)
            for comp in range(3):
                stage[...] = qkv_ref[comp].astype(F32)
                _to_residue_major(stage, dil.at[comp], r, scale=HEAD_DIM ** -0.5 if comp == 0 else None)
            stage[...] = do_ref[...].astype(F32)
            _to_residue_major(stage, dil.at[3], r)
            _to_residue_major(lse_ref, lse_res, r)
            _to_residue_major(dl_ref, dl_res, r)
            for g in range(n_blocks):
                rows = slice(g * ATT_BQ, (g + 1) * ATT_BQ)
                for src, dst in ((lse_res, lse_row), (dl_res, dl_row)):
                    flipped = src[rows, :].T
                    dst[g, 0:1, :] = flipped[0:1, :]
                    dst[g, 1:2, :] = flipped[HEAD_DIM:HEAD_DIM + 1, :]
            _fill_bias_tiles(bias, sl_ref, r, kw)
            _fill_bias_tiles(bias_t, sl_ref, r, kw, transposed=True)
            acc[1] = jnp.zeros((t, PAIR), F32)
            acc[2] = jnp.zeros((t, PAIR), F32)
            first = _first_head_lanes()

            def block(g, carry, seq_len=seq_len, kw=kw, first=first):
                qrow, krow, edge = _block_place(g, seq_len, kw)
                nt = (((1,), (1,)), ((), ()))
                q = dil[0, pl.ds(qrow, ATT_BQ), :]
                k = dil[1, pl.ds(krow, kw), :]
                v = dil[2, pl.ds(krow, kw), :]
                dov = dil[3, pl.ds(qrow, ATT_BQ), :]
                lv = lse_res[pl.ds(qrow, ATT_BQ), :]
                dlv = dl_res[pl.ds(qrow, ATT_BQ), :]
                dq = dk = dv = None
                for hh in range(2):
                    qm, km = _only_head(q, first, hh), _only_head(k, first, hh)
                    vm, dom = _only_head(v, first, hh), _only_head(dov, first, hh)
                    lse_cols = jnp.concatenate([_all_lanes(lv, first, hh)] * (kw // PAIR), axis=1)
                    dl_cols = jnp.concatenate([_all_lanes(dlv, first, hh)] * (kw // PAIR), axis=1)
                    s = lax.dot_general(qm, k, nt, preferred_element_type=F32)
                    p = jnp.exp(s + bias[hh, edge, :, 0:kw] - lse_cols)
                    dp = lax.dot_general(dom, v, nt, preferred_element_type=F32)
                    ds = (p * (dp - dl_cols)).astype(BF16)
                    s_t = lax.dot_general(km, q, nt, preferred_element_type=F32)
                    p_t = jnp.exp(s_t + bias_t[hh, edge, 0:kw, :] - lse_row[g, hh:hh + 1, :])
                    dp_t = lax.dot_general(vm, dov, nt, preferred_element_type=F32)
                    ds_t = (p_t * (dp_t - dl_row[g, hh:hh + 1, :])).astype(BF16)
                    dq_h = jnp.dot(ds, km, preferred_element_type=F32)
                    dk_h = jnp.dot(ds_t, qm, preferred_element_type=F32)
                    dv_h = jnp.dot(p_t.astype(BF16), dom, preferred_element_type=F32)
                    dq, dk, dv = (dq_h, dk_h, dv_h) if dq is None else (dq + dq_h, dk + dk_h, dv + dv_h)
                acc[0, pl.ds(qrow, ATT_BQ), :] = dq * (HEAD_DIM ** -0.5)
                acc[1, pl.ds(krow, kw), :] += dk
                acc[2, pl.ds(krow, kw), :] += dv
                return carry

            lax.fori_loop(0, n_blocks, block, 0, unroll=ATT_UNROLL_BWD)
            for comp in range(3):
                if r == 1:
                    tot[comp] = acc[comp]
                else:
                    for res in range(r):
                        tok = pl.ds(res, seq_len, stride=r)
                        tot[comp, tok, :] = tot[comp, tok, :] + acc[comp, res * seq_len:(res + 1) * seq_len, :]
        for comp in range(3):
            out_ref[comp] = tot[comp].astype(BF16)

    pair_spec = pl.BlockSpec((None, t, PAIR), lambda h: (h, 0, 0))
    return pl.pallas_call(
        body, name="attn_bwd", grid=(hp,),
        out_shape=jax.ShapeDtypeStruct(dattn.shape, BF16),
        in_specs=[HBM, pl.BlockSpec((3, None, t, PAIR), lambda h: (0, h, 0, 0)), pair_spec, pair_spec, pair_spec,
                  pl.BlockSpec((None, 8, ATT_KW), lambda h: (h, 0, 0)), ANY],
        out_specs=pl.BlockSpec((3, None, t, PAIR), lambda h: (0, h, 0, 0)),
        input_output_aliases={0: 0},
        scratch_shapes=[pltpu.VMEM((t, PAIR), F32), pltpu.VMEM((4, t, PAIR), BF16),
                        pltpu.VMEM((t, PAIR), F32), pltpu.VMEM((t, PAIR), F32),
                        pltpu.VMEM((n_blocks, 8, ATT_BQ), F32), pltpu.VMEM((n_blocks, 8, ATT_BQ), F32),
                        pltpu.VMEM((2, 3, ATT_BQ, ATT_KW), F32), pltpu.VMEM((2, 3, ATT_KW, ATT_BQ), F32),
                        pltpu.VMEM((3, t, PAIR), F32), pltpu.VMEM((3, t, PAIR), F32)],
        compiler_params=_params(("parallel",)),
    )(dattn, attn_proj, d_o, lse, delta, slopes, after)


def _mix_fwd(co, conv_proj, attn_proj, o_mix, g_conv, g_attn_pairs):
    t, dc = co.shape
    hp = attn_proj.shape[1]
    da = hp * PAIR
    tb = ROW_TILE

    def body(co_ref, bg_ref, zc_ref, za_ref, om_ref, gc_ref, ga_ref, ycat_ref, ycatt_ref):
        p = bg_ref[...].astype(F32) * co_ref[...]
        rc = lax.rsqrt(jnp.mean(p * p, axis=-1, keepdims=True) + EPS)
        yc = (p * rc) * gc_ref[...] * _silu(zc_ref[...].astype(F32))
        ycat_ref[:, 0:dc] = yc.astype(BF16)
        ycatt_ref[0:dc, :] = yc.T.astype(BF16)
        ssq = jnp.zeros((tb, 1), F32)
        for h in range(hp):
            o = om_ref[h]
            ssq = ssq + jnp.sum(o * o, axis=-1, keepdims=True)
        ra = lax.rsqrt(ssq * (1.0 / da) + EPS)
        for h in range(hp):
            ya = (om_ref[h] * ra) * ga_ref[h] * _silu(za_ref[h].astype(F32))
            ycat_ref[:, dc + h * PAIR:dc + (h + 1) * PAIR] = ya.astype(BF16)
            ycatt_ref[dc + h * PAIR:dc + (h + 1) * PAIR, :] = ya.T.astype(BF16)

    pair_spec = pl.BlockSpec((hp, tb, PAIR), lambda i: (0, i, 0))
    return pl.pallas_call(
        body, name="mix_fwd", grid=(t // tb,),
        out_shape=(jax.ShapeDtypeStruct((t, dc + da), BF16), jax.ShapeDtypeStruct((dc + da, t), BF16)),
        in_specs=[pl.BlockSpec((tb, dc), lambda i: (i, 0)),
                  pl.BlockSpec((tb, dc), lambda i: (i, 1)),
                  pl.BlockSpec((tb, dc), lambda i: (i, 3)),
                  pl.BlockSpec((None, hp, tb, PAIR), lambda i: (3, 0, i, 0)),
                  pair_spec,
                  pl.BlockSpec((1, dc), lambda i: (0, 0)),
                  pl.BlockSpec((hp, 1, PAIR), lambda i: (0, 0, 0))],
        out_specs=(pl.BlockSpec((tb, dc + da), lambda i: (i, 0)), pl.BlockSpec((dc + da, tb), lambda i: (0, i))),
        compiler_params=_params(("parallel",)),
    )(co, conv_proj, conv_proj, attn_proj, o_mix, g_conv, g_attn_pairs)


def _out_fwd_bwd(ycat, woutf, x, target, mod, g_post):
    t, d = x.shape
    n = ycat.shape[1]
    tb = ROW_TILE

    def body(a_ref, w_ref, x_ref, tg_ref, mod_ref, g_ref, dout_ref, dy_ref, acc_ref):
        y = jnp.dot(a_ref[...], w_ref[...], preferred_element_type=F32)
        r = lax.rsqrt(jnp.mean(y * y, axis=-1, keepdims=True) + EPS)
        nh = y * r
        gate = mod_ref[2:3, :]
        nrm = nh * g_ref[...]
        err = x_ref[...] + gate * nrm - tg_ref[...]
        dout = err * (1.0 / d)
        dout_ref[...] = dout
        dn = dout * gate
        a = dn * g_ref[...]
        dy = r * (a - nh * jnp.mean(a * nh, axis=-1, keepdims=True))
        dy_ref[...] = dy.astype(BF16)
        loss = 0.5 * jnp.sum(jnp.sum(err * err, axis=-1, keepdims=True) * (1.0 / d), axis=0, keepdims=True)
        part = jnp.concatenate(
            [jnp.sum(dout * nrm, axis=0, keepdims=True), jnp.sum(dn * nh, axis=0, keepdims=True),
             jnp.broadcast_to(loss, (1, d)), jnp.zeros((5, d), F32)], axis=0)

        @pl.when(pl.program_id(0) == 0)
        def _():
            acc_ref[...] = jnp.zeros(acc_ref.shape, F32)

        acc_ref[...] += part

    return pl.pallas_call(
        body, name="out_fwd_bwd", grid=(t // tb,),
        out_shape=(jax.ShapeDtypeStruct((t, d), F32), jax.ShapeDtypeStruct((t, d), BF16),
                   jax.ShapeDtypeStruct((8, d), F32)),
        in_specs=[pl.BlockSpec((tb, n), lambda i: (i, 0)), pl.BlockSpec((n, d), lambda i: (0, 0)),
                  pl.BlockSpec((tb, d), lambda i: (i, 0)), pl.BlockSpec((tb, d), lambda i: (i, 0)),
                  pl.BlockSpec((3, d), lambda i: (0, 0)), pl.BlockSpec((1, d), lambda i: (0, 0))],
        out_specs=(pl.BlockSpec((tb, d), lambda i: (i, 0)), pl.BlockSpec((tb, d), lambda i: (i, 0)),
                   pl.BlockSpec((8, d), lambda i: (0, 0))),
        compiler_params=_params(("arbitrary",)),
    )(ycat, woutf, x, target, mod, g_post)


def _matmul_nt(a, b, out_dtype, name):
    m, k = a.shape
    n = b.shape[0]
    tn = COL_TILE

    def body(a_ref, b_ref, o_ref):
        o_ref[...] = lax.dot_general(a_ref[...], b_ref[...], (((1,), (1,)), ((), ())),
                                     preferred_element_type=F32).astype(out_dtype)

    return pl.pallas_call(
        body, name=name, grid=(n // tn,),
        out_shape=jax.ShapeDtypeStruct((m, n), out_dtype),
        in_specs=[pl.BlockSpec((m, k), lambda i: (0, 0)), pl.BlockSpec((tn, k), lambda i: (i, 0))],
        out_specs=pl.BlockSpec((m, tn), lambda i: (0, i)),
        compiler_params=_params(("parallel",)),
    )(a, b)


def _matmul_nn(a, b, out_dtype, name):
    m, k = a.shape
    n = b.shape[1]
    tn = COL_TILE

    def body(a_ref, b_ref, o_ref):
        o_ref[...] = jnp.dot(a_ref[...], b_ref[...], preferred_element_type=F32).astype(out_dtype)

    return pl.pallas_call(
        body, name=name, grid=(n // tn,),
        out_shape=jax.ShapeDtypeStruct((m, n), out_dtype),
        in_specs=[pl.BlockSpec((m, k), lambda i: (0, 0)), pl.BlockSpec((k, tn), lambda i: (0, i))],
        out_specs=pl.BlockSpec((m, tn), lambda i: (0, i)),
        compiler_params=_params(("parallel",)),
    )(a, b)


def _mix_bwd(dycat, co, conv_proj, attn_proj, o_mix, g_conv, g_attn_pairs):
    t, dc = co.shape
    hp = attn_proj.shape[1]
    da = hp * PAIR
    tb = ROW_TILE

    def body(dy_ref, co_ref, bg_ref, zc_ref, za_ref, om_ref, gc_ref, ga_ref,
             dcp_ref, dco_ref, dza_ref, do_ref, dl_ref, dgc_ref, dga_ref):
        first = pl.program_id(0) == 0
        cov = co_ref[...]
        bg = bg_ref[...].astype(F32)
        zc = zc_ref[...].astype(F32)
        p = bg * cov
        rc = lax.rsqrt(jnp.mean(p * p, axis=-1, keepdims=True) + EPS)
        nh = p * rc
        dyc = dy_ref[:, 0:dc].astype(F32)
        dn = dyc * _silu(zc)
        a = dn * gc_ref[...]
        dp = rc * (a - nh * jnp.mean(a * nh, axis=-1, keepdims=True))
        dcp_ref[:, 0:dc] = jnp.zeros((tb, dc), BF16)
        dcp_ref[:, dc:2 * dc] = (dp * cov).astype(BF16)
        dcp_ref[:, 2 * dc:3 * dc] = jnp.zeros((tb, dc), BF16)
        dcp_ref[:, 3 * dc:4 * dc] = (dyc * nh * gc_ref[...] * _silu_grad(zc)).astype(BF16)
        dco_ref[...] = dp * bg

        @pl.when(first)
        def _():
            dgc_ref[...] = jnp.zeros(dgc_ref.shape, F32)
            dga_ref[...] = jnp.zeros(dga_ref.shape, F32)

        dgc_ref[...] += jnp.sum(dn * nh, axis=0, keepdims=True)

        ssq = jnp.zeros((tb, 1), F32)
        for h in range(hp):
            o = om_ref[h]
            ssq = ssq + jnp.sum(o * o, axis=-1, keepdims=True)
        ra = lax.rsqrt(ssq * (1.0 / da) + EPS)
        dot_an = jnp.zeros((tb, 1), F32)
        for h in range(hp):
            nha = om_ref[h] * ra
            za = za_ref[h].astype(F32)
            dya = dy_ref[:, dc + h * PAIR:dc + (h + 1) * PAIR].astype(F32)
            dna = dya * _silu(za)
            dza_ref[h] = (dya * nha * ga_ref[h] * _silu_grad(za)).astype(BF16)
            dga_ref[h] += jnp.sum(dna * nha, axis=0, keepdims=True)
            dot_an = dot_an + jnp.sum(dna * ga_ref[h] * nha, axis=-1, keepdims=True)
        mean_an = dot_an * (1.0 / da)
        first_head = lax.broadcasted_iota(jnp.int32, (tb, PAIR), 1) < HEAD_DIM
        for h in range(hp):
            o = om_ref[h]
            nha = o * ra
            za = za_ref[h].astype(F32)
            dya = dy_ref[:, dc + h * PAIR:dc + (h + 1) * PAIR].astype(F32)
            aa = dya * _silu(za) * ga_ref[h]
            d_o = ra * (aa - nha * mean_an)
            do_ref[h] = d_o.astype(BF16)
            prod = d_o * o
            both = jnp.sum(prod, axis=-1, keepdims=True)
            head0 = jnp.sum(jnp.where(first_head, prod, 0.0), axis=-1, keepdims=True)
            dl_ref[h] = jnp.where(first_head, head0, both - head0)

    pair_spec = pl.BlockSpec((hp, tb, PAIR), lambda i: (0, i, 0))
    return pl.pallas_call(
        body, name="mix_bwd", grid=(t // tb,),
        out_shape=(jax.ShapeDtypeStruct((t, 4 * dc), BF16), jax.ShapeDtypeStruct((t, dc), F32),
                   jax.ShapeDtypeStruct((4, hp, t, PAIR), BF16), jax.ShapeDtypeStruct((hp, t, PAIR), BF16),
                   jax.ShapeDtypeStruct((hp, t, PAIR), F32),
                   jax.ShapeDtypeStruct((1, dc), F32), jax.ShapeDtypeStruct((hp, 1, PAIR), F32)),
        in_specs=[pl.BlockSpec((tb, dc + da), lambda i: (i, 0)),
                  pl.BlockSpec((tb, dc), lambda i: (i, 0)),
                  pl.BlockSpec((tb, dc), lambda i: (i, 1)),
                  pl.BlockSpec((tb, dc), lambda i: (i, 3)),
                  pl.BlockSpec((None, hp, tb, PAIR), lambda i: (3, 0, i, 0)),
                  pair_spec,
                  pl.BlockSpec((1, dc), lambda i: (0, 0)),
                  pl.BlockSpec((hp, 1, PAIR), lambda i: (0, 0, 0))],
        out_specs=(pl.BlockSpec((tb, 4 * dc), lambda i: (i, 0)), pl.BlockSpec((tb, dc), lambda i: (i, 0)),
                   pl.BlockSpec((None, hp, tb, PAIR), lambda i: (3, 0, i, 0)), pair_spec, pair_spec,
                   pl.BlockSpec((1, dc), lambda i: (0, 0)), pl.BlockSpec((hp, 1, PAIR), lambda i: (0, 0, 0))),
        compiler_params=_params(("arbitrary",)),
    )(dycat, co, conv_proj, conv_proj, attn_proj, o_mix, g_conv, g_attn_pairs)


def _conv_bwd(dconv_proj, dco, conv_proj, conv_w, dc, after):
    t = dco.shape[0]
    ct = CONV_TILE
    nct = dc // ct

    def body(dcp_in_ref, dco_ref, u_ref, cg_ref, w_ref, after_ref, dcp_ref, acc_ref):
        del dcp_in_ref, after_ref
        which = pl.program_id(1)
        g = dco_ref[...]
        u = u_ref[...].astype(F32)
        cg = cg_ref[...].astype(F32)
        g_prev, g_next = _shift_rows(g, t)
        da = w_ref[0:1, :] * g_next + w_ref[1:2, :] * g + w_ref[2:3, :] * g_prev
        dcp_ref[...] = (da * jnp.where(which == 0, cg, u)).astype(BF16)
        a = cg * u
        a_prev, a_next = _shift_rows(a, t)
        acc_ref[...] = jnp.concatenate(
            [jnp.sum(g * a_prev, axis=0, keepdims=True), jnp.sum(g * a, axis=0, keepdims=True),
             jnp.sum(g * a_next, axis=0, keepdims=True), jnp.sum(g, axis=0, keepdims=True),
             jnp.zeros((4, ct), F32)], axis=0)

    return pl.pallas_call(
        body, name="conv_bwd", grid=(nct, 2),
        out_shape=(jax.ShapeDtypeStruct(dconv_proj.shape, BF16), jax.ShapeDtypeStruct((8, dc), F32)),
        in_specs=[HBM,
                  pl.BlockSpec((t, ct), lambda i, s: (0, i)),
                  pl.BlockSpec((t, ct), lambda i, s: (0, i)),
                  pl.BlockSpec((t, ct), lambda i, s: (0, 2 * nct + i)),
                  pl.BlockSpec((3, ct), lambda i, s: (0, i)), ANY],
        out_specs=(pl.BlockSpec((t, ct), lambda i, s: (0, 2 * s * nct + i)),
                   pl.BlockSpec((8, ct), lambda i, s: (0, i))),
        input_output_aliases={0: 0},
        compiler_params=_params(("arbitrary", "arbitrary")),
    )(dconv_proj, dco, conv_proj, conv_proj, conv_w, after)


def _dw_in(ht, dconv_proj, dattn, ws, da):
    d, t = ht.shape
    tn = COL_TILE
    nt = ws // tn
    per_comp = da // tn
    pairs = tn // PAIR

    def body_conv(a_ref, b_ref, o_ref):
        o_ref[...] = jnp.dot(a_ref[...], b_ref[...], preferred_element_type=F32).astype(BF16)

    gin = pl.pallas_call(
        body_conv, name="dw_in_conv", grid=(2, nt),
        out_shape=jax.ShapeDtypeStruct((N_CHIPS, d, ws), BF16),
        in_specs=[pl.BlockSpec((d, t), lambda j, n: (0, 0)), pl.BlockSpec((t, tn), lambda j, n: (0, j * nt + n))],
        out_specs=pl.BlockSpec((None, d, tn), lambda j, n: (j, 0, n)),
        compiler_params=_params(("parallel", "parallel")),
    )(ht, dconv_proj)

    def body_attn(g_ref, a_ref, b_ref, o_ref):
        del g_ref
        o_ref[...] = jnp.dot(a_ref[...], _pairs_to_cols(b_ref, pairs), preferred_element_type=F32).astype(BF16)

    return pl.pallas_call(
        body_attn, name="dw_in_attn", grid=(2, nt),
        out_shape=jax.ShapeDtypeStruct((N_CHIPS, d, ws), BF16),
        in_specs=[HBM, pl.BlockSpec((d, t), lambda j, n: (0, 0)),
                  pl.BlockSpec((None, pairs, t, PAIR), lambda j, n: (2 * j + n // per_comp, n % per_comp, 0, 0))],
        out_specs=pl.BlockSpec((None, d, tn), lambda j, n: (2 + j, 0, n)),
        input_output_aliases={0: 0},
        compiler_params=_params(("parallel", "parallel")),
    )(gin, ht, dattn)


def _dh(dconv_proj, dattn, winf, da, after):
    t = dconv_proj.shape[0]
    _, d, ws = winf.shape
    tm = 1024
    tk = COL_TILE
    nk = ws // tk
    per_comp = da // tk
    pairs = tk // PAIR

    def body_conv(a_ref, b_ref, after_ref, o_ref):
        del after_ref

        @pl.when((pl.program_id(1) == 0) & (pl.program_id(2) == 0))
        def _():
            o_ref[...] = jnp.zeros(o_ref.shape, F32)

        o_ref[...] += lax.dot_general(a_ref[...], b_ref[...], (((1,), (1,)), ((), ())), preferred_element_type=F32)

    part = pl.pallas_call(
        body_conv, name="dh_conv", grid=(t // tm, 2, nk),
        out_shape=jax.ShapeDtypeStruct((t, d), F32),
        in_specs=[pl.BlockSpec((tm, tk), lambda m, j, k: (m, j * nk + k)),
                  pl.BlockSpec((None, d, tk), lambda m, j, k: (j, 0, k)), ANY],
        out_specs=pl.BlockSpec((tm, d), lambda m, j, k: (m, 0)),
        compiler_params=_params(("parallel", "arbitrary", "arbitrary")),
    )(dconv_proj, winf, after)

    def body_attn(p_ref, a_ref, b_ref, o_ref):
        @pl.when((pl.program_id(1) == 0) & (pl.program_id(2) == 0))
        def _():
            o_ref[...] = p_ref[...]

        o_ref[...] += lax.dot_general(_pairs_to_cols(a_ref, pairs), b_ref[...], (((1,), (1,)), ((), ())),
                                      preferred_element_type=F32)

    return pl.pallas_call(
        body_attn, name="dh_attn", grid=(t // tm, 2, nk),
        out_shape=jax.ShapeDtypeStruct((t, d), F32),
        in_specs=[pl.BlockSpec((tm, d), lambda m, j, k: (m, 0)),
                  pl.BlockSpec((None, pairs, tm, PAIR), lambda m, j, k: (2 * j + k // per_comp, k % per_comp, m, 0)),
                  pl.BlockSpec((None, d, tk), lambda m, j, k: (2 + j, 0, k))],
        out_specs=pl.BlockSpec((tm, d), lambda m, j, k: (m, 0)),
        compiler_params=_params(("parallel", "arbitrary", "arbitrary")),
    )(part, dattn, winf)


def _prenorm_bwd(x, dh, dout, mod, g_pre):
    t, d = x.shape
    tb = ROW_TILE

    def body(x_ref, dh_ref, dout_ref, mod_ref, g_ref, gx_ref, acc_ref):
        xv = x_ref[...]
        dhv = dh_ref[...]
        r = lax.rsqrt(jnp.mean(xv * xv, axis=-1, keepdims=True) + EPS)
        xh = xv * r
        one_scale = 1.0 + mod_ref[1:2, :]
        a = dhv * one_scale * g_ref[...]
        gx_ref[...] = dout_ref[...] + r * (a - xh * jnp.mean(a * xh, axis=-1, keepdims=True))
        part = jnp.concatenate(
            [jnp.sum(dhv, axis=0, keepdims=True), jnp.sum(dhv * xh * g_ref[...], axis=0, keepdims=True),
             jnp.sum(dhv * xh * one_scale, axis=0, keepdims=True), jnp.zeros((5, d), F32)], axis=0)

        @pl.when(pl.program_id(0) == 0)
        def _():
            acc_ref[...] = jnp.zeros(acc_ref.shape, F32)

        acc_ref[...] += part

    return pl.pallas_call(
        body, name="prenorm_bwd", grid=(t // tb,),
        out_shape=(jax.ShapeDtypeStruct((t, d), F32), jax.ShapeDtypeStruct((8, d), F32)),
        in_specs=[pl.BlockSpec((tb, d), lambda i: (i, 0)), pl.BlockSpec((tb, d), lambda i: (i, 0)),
                  pl.BlockSpec((tb, d), lambda i: (i, 0)), pl.BlockSpec((3, d), lambda i: (0, 0)),
                  pl.BlockSpec((1, d), lambda i: (0, 0))],
        out_specs=(pl.BlockSpec((tb, d), lambda i: (i, 0)), pl.BlockSpec((8, d), lambda i: (0, 0))),
        compiler_params=_params(("arbitrary",)),
    )(x, dh, dout, mod, g_pre)


def _chip_sums(place, g, rsib, name):
    _, rows, cols = g.shape
    half = rows // 2
    tr = min(half, ROW_TILE)
    nt = half // tr

    def body(place_ref, g_ref, r_ref, o_ref):
        del place_ref
        o_ref[...] = (g_ref[...].astype(F32) + r_ref[...].astype(F32)).astype(BF16)

    grid_spec = pltpu.PrefetchScalarGridSpec(
        num_scalar_prefetch=1, grid=(N_CHIPS, nt),
        in_specs=[pl.BlockSpec((None, tr, cols), lambda j, i, p: (j, p[1] * nt + i, 0)),
                  pl.BlockSpec((None, tr, cols), lambda j, i, p: (j, i, 0))],
        out_specs=pl.BlockSpec((None, tr, cols), lambda j, i, p: (j, i, 0)))
    return pl.pallas_call(
        body, name=name, grid_spec=grid_spec,
        out_shape=jax.ShapeDtypeStruct((N_CHIPS, half, cols), BF16),
        compiler_params=_params(("parallel", "parallel")),
    )(place, g, rsib)


def _owner_sum(place, g, rsib, rici, name):
    _, rows, cols = g.shape
    half = rows // 2
    tr = min(half, ROW_TILE)
    nt = half // tr

    def body(place_ref, g_ref, r_ref, i_ref, o_ref):
        del place_ref
        acc = g_ref[...].astype(F32) + r_ref[...].astype(F32)
        for k in range(N_CHIPS - 1):
            acc = acc + i_ref[k].astype(F32)
        o_ref[...] = acc

    grid_spec = pltpu.PrefetchScalarGridSpec(
        num_scalar_prefetch=1, grid=(nt,),
        in_specs=[pl.BlockSpec((None, tr, cols), lambda i, p: (p[0], p[1] * nt + i, 0)),
                  pl.BlockSpec((None, tr, cols), lambda i, p: (p[0], i, 0)),
                  pl.BlockSpec((N_CHIPS - 1, tr, cols), lambda i, p: (0, i, 0))],
        out_specs=pl.BlockSpec((tr, cols), lambda i, p: (p[1] * nt + i, 0)))
    return pl.pallas_call(
        body, name=name, grid_spec=grid_spec,
        out_shape=jax.ShapeDtypeStruct((rows, cols), F32),
        compiler_params=_params(("parallel",)),
    )(place, g, rsib, rici)


def _adam_math(w, g, m, v):
    m2 = ADAM_B1 * m + (1.0 - ADAM_B1) * g
    v2 = ADAM_B2 * v + (1.0 - ADAM_B2) * (g * g)
    m_hat = m2 / (1.0 - ADAM_B1 ** ADAM_STEP)
    v_hat = v2 / (1.0 - ADAM_B2 ** ADAM_STEP)
    delta = -ADAM_LR * (m_hat / (jnp.sqrt(v_hat) + ADAM_EPS) + ADAM_WD * w)
    return delta, m2, v2


def _adamw(w, g, m, v, name):
    rows, cols = w.shape
    tr = min(rows, ROW_TILE)

    def body(w_ref, g_ref, m_ref, v_ref, d_ref, m2_ref, v2_ref):
        d_ref[...], m2_ref[...], v2_ref[...] = _adam_math(w_ref[...], g_ref[...], m_ref[...], v_ref[...])

    spec = pl.BlockSpec((tr, cols), lambda i: (i, 0))
    return pl.pallas_call(
        body, name=name, grid=(rows // tr,),
        out_shape=(jax.ShapeDtypeStruct(w.shape, F32),) * 3,
        in_specs=[spec] * 4, out_specs=(spec,) * 3,
        compiler_params=_params(("parallel",)),
    )(w, g, m, v)


def _ada_grad_adamw(c_all_t, dmod_cols, w, m, v):
    d, wa = w.shape
    tn = 256

    def body(ct_ref, dm_ref, w_ref, m_ref, v_ref, g_ref, d_ref, m2_ref, v2_ref):
        act = _silu(ct_ref[...])
        g = act[:, 0:1] * dm_ref[0:1, :]
        for b in range(1, N_DEV):
            g = g + act[:, b:b + 1] * dm_ref[b:b + 1, :]
        g_ref[...] = g
        d_ref[...], m2_ref[...], v2_ref[...] = _adam_math(w_ref[...], g, m_ref[...], v_ref[...])

    spec = pl.BlockSpec((d, tn), lambda i: (0, i))
    return pl.pallas_call(
        body, name="ada_grad_adamw", grid=(wa // tn,),
        out_shape=(jax.ShapeDtypeStruct(w.shape, F32),) * 4,
        in_specs=[pl.BlockSpec((d, N_DEV), lambda i: (0, 0)), pl.BlockSpec((N_DEV, tn), lambda i: (0, i)),
                  spec, spec, spec],
        out_specs=(spec,) * 4,
        compiler_params=_params(("parallel",)),
    )(c_all_t, dmod_cols, w, m, v)


def _sum_devices(gathered):
    n = gathered.shape[1]

    def body(g_ref, o_ref):
        acc = g_ref[0:8, :]
        for dev in range(1, N_DEV):
            acc = acc + g_ref[8 * dev:8 * dev + 8, :]
        o_ref[...] = acc

    return pl.pallas_call(
        body, name="sum_devices",
        out_shape=jax.ShapeDtypeStruct((8, n), F32),
        in_specs=[VMEM], out_specs=VMEM,
    )(gathered)


def _pack_small(pieces):
    flat = [p.reshape(-1).astype(F32) for p in pieces]
    offsets, total = [], 0
    for p in flat:
        offsets.append(total)
        total += p.shape[0]
    padded = -(-total // SMALL_ALIGN) * SMALL_ALIGN
    if padded > total:
        flat.append(jnp.zeros((padded - total,), F32))
    return jnp.concatenate(flat).reshape(8, padded // 8), offsets


def _alibi_slope_rows(n_heads):
    slopes = 2.0 ** (-8.0 * jnp.arange(1, n_heads + 1, dtype=F32) / n_heads)
    rows = jnp.zeros((n_heads // 2, 8), F32).at[:, 0:2].set(slopes.reshape(n_heads // 2, 2))
    return jnp.broadcast_to(rows[:, :, None], (n_heads // 2, 8, ATT_KW))


def kernel(x, c, w_ada, b_ada, g_pre, w_in, conv_w, conv_b, g_conv, g_attn, w_out, g_post, loss_target, m_w_ada, m_b_ada, m_g_pre, m_w_in, m_conv_w, m_conv_b, m_g_conv, m_g_attn, m_w_out, m_g_post, v_w_ada, v_b_ada, v_g_pre, v_w_in, v_conv_w, v_conv_b, v_g_conv, v_g_attn, v_w_out, v_g_post):
    t, d = x.shape[1], x.shape[2]
    dc = conv_b.shape[1]
    da = g_attn.shape[1]
    hp = da // PAIR
    ws = w_in.shape[2]
    wa = w_ada.shape[2]
    cws = conv_w.shape[2]
    assert t % ROW_TILE == 0 and d % ROW_TILE == 0 and dc % COL_TILE == 0 and da % COL_TILE == 0
    assert ws == 2 * dc and dc == da and t // BRANCHES[-1][1] >= ATT_BQ

    mx, my, mc = _my_place()
    chip = _chip_of(mx, my)
    dev = 2 * chip + mc
    place = jnp.stack([chip, mc]).astype(jnp.int32)

    x2, tgt2 = x[0], loss_target[0]
    w_ada2, w_in2, w_out2 = w_ada[0], w_in[0], w_out[0]

    packed, offs = _pack_small([c[0], conv_w[0]])
    seen = _allgather8(packed, "gather_inputs").reshape(N_DEV, -1)
    c_all = seen[:, offs[0]:offs[0] + d]
    conv_w_full = seen[0::2, offs[1]:offs[1] + 3 * cws].reshape(N_CHIPS, 3, cws).transpose(1, 0, 2).reshape(3, dc)

    ada_part = _ada_partial(c_all, w_ada2)
    ada_seen = _allgather8(ada_part, "gather_ada").reshape(N_DEV, N_DEV, wa)
    mod_flat = lax.dynamic_index_in_dim(ada_seen[0::2], dev, axis=1, keepdims=False).reshape(1, 3 * d) + b_ada
    mod = mod_flat.reshape(3, d)

    win_flight, send_in, recv_in, started = _gather_start(_cast_into_slot(place, w_in2, "cast_w_in"), mod)

    h, ht = _prenorm(x2, mod + started[0, 0], g_pre)
    win_flight, wout_flight, relay_send_in, relay_recv_in, send_out, recv_out = _gather_relay_in(
        win_flight, _cast_into_slot(place, w_out2, "cast_w_out"), recv_in, h)
    winf = _forward_halves(
        _gather_wait(win_flight, send_in, recv_in, relay_send_in, relay_recv_in, h, "gather_wait_w_in"), "forward_w_in")
    conv_proj = _proj_conv(h, winf)
    attn_proj = _proj_attn(h, winf, da)
    slopes = _alibi_slope_rows(da // HEAD_DIM)
    co = _conv_fwd(conv_proj, conv_w_full, conv_b, dc)
    wout_flight, relay_send_out, relay_recv_out = _gather_relay_out(wout_flight, recv_out, co)
    o_mix, lse = _attn_fwd(attn_proj, slopes)
    g_attn_pairs = g_attn.reshape(hp, 1, PAIR)
    ycat, ycat_t = _mix_fwd(co, conv_proj, attn_proj, o_mix, g_conv, g_attn_pairs)
    woutf4 = _forward_halves(
        _gather_wait(wout_flight, send_out, recv_out, relay_send_out, relay_recv_out, ycat, "gather_wait_w_out"),
        "forward_w_out")
    woutf = woutf4.reshape(dc + da, d)
    dout, dy, post_sums = _out_fwd_bwd(ycat, woutf, x2, tgt2, mod, g_post)

    gout = _matmul_nn(ycat_t, dy, BF16, "dw_out").reshape(N_CHIPS, (dc + da) // N_CHIPS, d)
    rsib_out = _swap_halves(gout, "rs_swap_halves_out")
    csum_out = _chip_sums(place, gout, rsib_out, "rs_chip_sum_out")
    ssem_out, rsem_out, csum_out, land_out, sent_out = _owners_start(csum_out, "rs_owners_start_out")
    dycat = _matmul_nt(dy, woutf, BF16, "dycat")
    dconv_proj, dco, dattn, d_o, delta, dg_conv, dg_attn = _mix_bwd(
        dycat, co, conv_proj, attn_proj, o_mix, g_conv, g_attn_pairs)
    dconv_proj, conv_sums = _conv_bwd(dconv_proj, dco, conv_proj, conv_w_full, dc, sent_out)
    dattn = _attn_bwd(dattn, attn_proj, d_o, lse, delta, slopes, sent_out)
    gin = _dw_in(ht, dconv_proj, dattn, ws, da)
    rsib_in = _swap_halves(gin, "rs_swap_halves_in")
    csum_in = _chip_sums(place, gin, rsib_in, "rs_chip_sum_in")
    ssem_in, rsem_in, csum_in, land_in, sent_in = _owners_start(csum_in, "rs_owners_start_in")
    dh = _dh(dconv_proj, dattn, winf, da, sent_in)
    grad_x, pre_sums = _prenorm_bwd(x2, dh, dout, mod, g_pre)

    rici_out = _owners_wait(ssem_out, rsem_out, csum_out, land_out, [grad_x], "rs_owners_wait_out")
    grad_w_out = _join_halves(_owner_sum(place, gout, rsib_out, rici_out, "rs_owner_sum_out"), "rs_join_halves_out")

    small, so = _pack_small([
        pre_sums[0], pre_sums[1], post_sums[0],
        pre_sums[2], conv_sums[0:3], conv_sums[3], dg_conv, dg_attn, post_sums[1], post_sums[2, 0:128]])
    small_seen = _allgather8(small, "gather_small")
    total = _sum_devices(small_seen).reshape(-1)
    dmod_all = small_seen.reshape(N_DEV, -1)[:, 0:3 * d]
    loss = total[so[9]]
    grad_b_ada = total[0:3 * d].reshape(1, 3 * d)
    grad_g_pre = total[so[3]:so[3] + d].reshape(1, d)
    grad_conv_w_full = total[so[4]:so[4] + 3 * dc].reshape(3, dc)
    grad_conv_w = lax.dynamic_slice_in_dim(grad_conv_w_full, chip * cws, cws, axis=1).reshape(1, 3, cws)
    grad_conv_b = total[so[5]:so[5] + dc].reshape(1, dc)
    grad_g_conv = total[so[6]:so[6] + dc].reshape(1, dc)
    grad_g_attn = total[so[7]:so[7] + da].reshape(1, da)
    grad_g_post = total[so[8]:so[8] + d].reshape(1, d)

    dmod_cols = lax.dynamic_slice_in_dim(dmod_all, chip * wa, wa, axis=1)
    grad_w_ada, delta_w_ada, new_m_w_ada, new_v_w_ada = _ada_grad_adamw(c_all.T, dmod_cols, w_ada2, m_w_ada[0], v_w_ada[0])
    delta_w_out, new_m_w_out, new_v_w_out = _adamw(w_out2, grad_w_out, m_w_out[0], v_w_out[0], "adamw_w_out")

    small_w = [b_ada, g_pre, conv_w, conv_b, g_conv, g_attn, g_post]
    small_g = [grad_b_ada, grad_g_pre, grad_conv_w, grad_conv_b, grad_g_conv, grad_g_attn, grad_g_post]
    small_m = [m_b_ada, m_g_pre, m_conv_w, m_conv_b, m_g_conv, m_g_attn, m_g_post]
    small_v = [v_b_ada, v_g_pre, v_conv_w, v_conv_b, v_g_conv, v_g_attn, v_g_post]
    pw, po = _pack_small(small_w)
    pg, _ = _pack_small(small_g)
    pm, _ = _pack_small(small_m)
    pv, _ = _pack_small(small_v)
    sd, sm, sv = (a.reshape(-1) for a in _adamw(pw, pg, pm, pv, "adamw_small"))

    def unpack(flat):
        return [flat[o:o + w.size].reshape(w.shape) for o, w in zip(po, small_w)]

    d_small, m_small, v_small = unpack(sd), unpack(sm), unpack(sv)

    rici_in = _owners_wait(ssem_in, rsem_in, csum_in, land_in, [sd, delta_w_out, delta_w_ada], "rs_owners_wait_in")
    grad_w_in = _join_halves(_owner_sum(place, gin, rsib_in, rici_in, "rs_owner_sum_in"), "rs_join_halves_in")
    delta_w_in, new_m_w_in, new_v_w_in = _adamw(w_in2, grad_w_in, m_w_in[0], v_w_in[0], "adamw_w_in")

    def lead(a):
        return a.reshape((1,) + a.shape)

    grads = [lead(grad_w_ada), grad_b_ada, grad_g_pre, lead(grad_w_in), grad_conv_w, grad_conv_b, grad_g_conv,
             grad_g_attn, lead(grad_w_out), grad_g_post]
    deltas = [lead(delta_w_ada), d_small[0], d_small[1], lead(delta_w_in), d_small[2], d_small[3], d_small[4],
              d_small[5], lead(delta_w_out), d_small[6]]
    new_ms = [lead(new_m_w_ada), m_small[0], m_small[1], lead(new_m_w_in), m_small[2], m_small[3], m_small[4],
              m_small[5], lead(new_m_w_out), m_small[6]]
    new_vs = [lead(new_v_w_ada), v_small[0], v_small[1], lead(new_v_w_in), v_small[2], v_small[3], v_small[4],
              v_small[5], lead(new_v_w_out), v_small[6]]
    return (loss, lead(grad_x), *grads, *deltas, *new_ms, *new_vs)
```

```python
import jax
import jax.numpy as jnp
from jax import lax
from jax.experimental import pallas as pl
from jax.experimental.pallas import tpu as pltpu

F32 = jnp.float32
BF16 = jnp.bfloat16
MESH = pl.DeviceIdType.MESH
HBM = pl.BlockSpec(memory_space=pltpu.HBM)
VMEM = pl.BlockSpec(memory_space=pltpu.VMEM)
ANY = pl.BlockSpec(memory_space=pl.ANY)
SEM = pl.BlockSpec(memory_space=pltpu.SEMAPHORE)
EFFECT = pltpu.SideEffectType.DATAFLOW_SIDE_EFFECTING
TOKEN = jax.ShapeDtypeStruct((8, 128), jnp.float32)

HEAD_DIM = 64
PAIR = 2 * HEAD_DIM
BRANCHES = ((128, 1), (512, 4), (2048, 16))
SIDE = 64
EPS = 1e-6
NEG_INF = -1e30
N_CHIPS = 4
N_DEV = 8

ADAM_LR = 0.001
ADAM_B1 = 0.9
ADAM_B2 = 0.999
ADAM_EPS = 1e-08
ADAM_WD = 0.01
ADAM_STEP = 10

VMEM_LIMIT_BYTES = 56 * 1024 * 1024
ROW_TILE = 256
COL_TILE = 512
CONV_TILE = 256
ATT_BQ = 128
ATT_KW = ATT_BQ + 2 * SIDE
ATT_UNROLL = 4
ATT_UNROLL_BWD = 4
SMALL_ALIGN = 1024


def _params(semantics=None):
    kw = {"vmem_limit_bytes": VMEM_LIMIT_BYTES}
    if semantics is not None:
        kw["dimension_semantics"] = semantics
    return pltpu.CompilerParams(**kw)


def _silu(z):
    return z * jax.nn.sigmoid(z)


def _silu_grad(z):
    s = jax.nn.sigmoid(z)
    return s * (1.0 + z * (1.0 - s))


def _my_place():
    return lax.axis_index("x"), lax.axis_index("y"), lax.axis_index("c")


def _flip(a, bit):
    return 1 - a if bit else a


def _chip_of(x, y):
    return 2 * x + y


def _allgather8(v, name):
    rows_per, n = v.shape

    def body(v_ref, out_ref, send_sems, recv_sems):
        x, y, c = _my_place()
        me = 4 * x + 2 * y + c

        def rows(idx):
            return out_ref.at[pl.ds(pl.multiple_of(idx * rows_per, rows_per), rows_per), :]

        out_ref[pl.ds(pl.multiple_of(me * rows_per, rows_per), rows_per), :] = v_ref[...]
        copies = []
        for k in range(1, N_DEV):
            peer = (_flip(x, k & 4), _flip(y, k & 2), _flip(c, k & 1))
            cp = pltpu.make_async_remote_copy(
                src_ref=v_ref, dst_ref=rows(me), send_sem=send_sems.at[k - 1], recv_sem=recv_sems.at[k - 1],
                device_id=peer, device_id_type=MESH)
            cp.start()
            copies.append((cp, peer))
        for k, (cp, peer) in enumerate(copies):
            src = 4 * peer[0] + 2 * peer[1] + peer[2]
            pltpu.make_async_remote_copy(
                src_ref=v_ref, dst_ref=rows(src), send_sem=send_sems.at[k], recv_sem=recv_sems.at[k],
                device_id=peer, device_id_type=MESH).wait_recv()
        for cp, _ in copies:
            cp.wait_send()

    return pl.pallas_call(
        body, name=name,
        out_shape=jax.ShapeDtypeStruct((N_DEV * rows_per, n), v.dtype),
        in_specs=[VMEM], out_specs=VMEM,
        scratch_shapes=[pltpu.SemaphoreType.DMA((N_DEV - 1,)), pltpu.SemaphoreType.DMA((N_DEV - 1,))],
    )(v)


def _half_rows(ref, chip, which, half):
    return ref.at[chip, pl.ds(pl.multiple_of(which * half, half), half), :]


def _ici_peers(x, y, c):
    peers = [(_flip(x, k & 2), _flip(y, k & 1), c) for k in (1, 2, 3)]
    return [(peer, _chip_of(peer[0], peer[1])) for peer in peers]


def _part_rows(ref, chip, core, part):
    quarter = ref.shape[1] // 4
    return ref.at[chip, pl.ds(pl.multiple_of((2 * core + part) * quarter, quarter), quarter), :]


def _neighbours(x, y, c):
    return [((x, 1 - y, c), _chip_of(x, 1 - y)), ((1 - x, y, c), _chip_of(1 - x, y)),
            ((1 - x, 1 - y, c), _chip_of(1 - x, 1 - y))]


def _start_direct(buf, send_sems, recv_sems):
    x, y, c = _my_place()
    me = _chip_of(x, y)
    for n, (peer, _) in enumerate(_neighbours(x, y, c)[0:2]):
        for part in ((0, 1), (1, 0))[n]:
            piece = _part_rows(buf, me, c, part)
            pltpu.make_async_remote_copy(
                src_ref=piece, dst_ref=piece, send_sem=send_sems.at[2 * n + part], recv_sem=recv_sems.at[2 * n + part],
                device_id=peer, device_id_type=MESH).start()


def _relay(buf, recv_sems, relay_send, relay_recv):
    x, y, c = _my_place()
    nbrs = _neighbours(x, y, c)
    for n in range(2):
        part = n
        piece = _part_rows(buf, nbrs[n][1], c, part)
        pltpu.make_async_remote_copy(
            src_ref=piece, dst_ref=piece, send_sem=relay_send.at[part], recv_sem=recv_sems.at[2 * n + part],
            device_id=nbrs[n][0], device_id_type=MESH).wait_recv()
        pltpu.make_async_remote_copy(
            src_ref=piece, dst_ref=piece, send_sem=relay_send.at[part], recv_sem=relay_recv.at[part],
            device_id=nbrs[1 - n][0], device_id_type=MESH).start()


def _gather_start(win_slots, after):
    def body(win_in, after_ref, win_ref, send_sems, recv_sems, token_ref):
        del win_in, after_ref
        _start_direct(win_ref, send_sems, recv_sems)
        token_ref[...] = jnp.zeros(token_ref.shape, F32)

    sems = pltpu.SemaphoreType.DMA((4,))
    return pl.pallas_call(
        body, name="gather_start",
        out_shape=(jax.ShapeDtypeStruct(win_slots.shape, win_slots.dtype), sems, sems, TOKEN),
        in_specs=[HBM, ANY], out_specs=(HBM, SEM, SEM, VMEM),
        input_output_aliases={0: 0},
        compiler_params=pltpu.CompilerParams(has_side_effects=EFFECT),
    )(win_slots, after)


def _gather_relay_in(win, wout_slots, recv_in, after):
    def body(win_in, wout_in, recv_in_ref, after_ref, win_ref, wout_ref, relay_send, relay_recv, send_out, recv_out):
        del win_in, wout_in, after_ref
        _relay(win_ref, recv_in_ref, relay_send, relay_recv)
        _start_direct(wout_ref, send_out, recv_out)

    two, four = pltpu.SemaphoreType.DMA((2,)), pltpu.SemaphoreType.DMA((4,))
    return pl.pallas_call(
        body, name="gather_relay_w_in",
        out_shape=(jax.ShapeDtypeStruct(win.shape, win.dtype), jax.ShapeDtypeStruct(wout_slots.shape, wout_slots.dtype),
                   two, two, four, four),
        in_specs=[HBM, HBM, SEM, ANY], out_specs=(HBM, HBM, SEM, SEM, SEM, SEM),
        input_output_aliases={0: 0, 1: 1},
        compiler_params=pltpu.CompilerParams(has_side_effects=EFFECT),
    )(win, wout_slots, recv_in, after)


def _gather_relay_out(wout, recv_out, after):
    def body(wout_in, recv_out_ref, after_ref, wout_ref, relay_send, relay_recv):
        del wout_in, after_ref
        _relay(wout_ref, recv_out_ref, relay_send, relay_recv)

    two = pltpu.SemaphoreType.DMA((2,))
    return pl.pallas_call(
        body, name="gather_relay_w_out",
        out_shape=(jax.ShapeDtypeStruct(wout.shape, wout.dtype), two, two),
        in_specs=[HBM, SEM, ANY], out_specs=(HBM, SEM, SEM),
        input_output_aliases={0: 0},
        compiler_params=pltpu.CompilerParams(has_side_effects=EFFECT),
    )(wout, recv_out, after)


def _gather_wait(buf, send_sems, recv_sems, relay_send, relay_recv, after, name):
    def body(buf_in, send_ref, recv_ref, rsend_ref, rrecv_ref, after_ref, buf_ref):
        del buf_in, after_ref
        x, y, c = _my_place()
        me = _chip_of(x, y)
        nbrs = _neighbours(x, y, c)
        for n in range(2):
            peer, chip = nbrs[n]
            second = 1 - n
            pltpu.make_async_remote_copy(
                src_ref=_part_rows(buf_ref, me, c, second), dst_ref=_part_rows(buf_ref, chip, c, second),
                send_sem=send_ref.at[2 * n + second], recv_sem=recv_ref.at[2 * n + second],
                device_id=peer, device_id_type=MESH).wait_recv()
            pltpu.make_async_remote_copy(
                src_ref=_part_rows(buf_ref, chip, c, n), dst_ref=_part_rows(buf_ref, nbrs[2][1], c, n),
                send_sem=rsend_ref.at[n], recv_sem=rrecv_ref.at[n],
                device_id=nbrs[1 - n][0], device_id_type=MESH).wait_recv()
        for n in range(2):
            peer, chip = nbrs[n]
            for part in range(2):
                piece = _part_rows(buf_ref, me, c, part)
                pltpu.make_async_remote_copy(
                    src_ref=piece, dst_ref=piece, send_sem=send_ref.at[2 * n + part], recv_sem=recv_ref.at[2 * n + part],
                    device_id=peer, device_id_type=MESH).wait_send()
            relayed = _part_rows(buf_ref, chip, c, n)
            pltpu.make_async_remote_copy(
                src_ref=relayed, dst_ref=relayed, send_sem=rsend_ref.at[n], recv_sem=rrecv_ref.at[n],
                device_id=nbrs[1 - n][0], device_id_type=MESH).wait_send()

    return pl.pallas_call(
        body, name=name,
        out_shape=jax.ShapeDtypeStruct(buf.shape, buf.dtype),
        in_specs=[HBM, SEM, SEM, SEM, SEM, ANY], out_specs=HBM,
        input_output_aliases={0: 0},
        compiler_params=pltpu.CompilerParams(has_side_effects=EFFECT),
    )(buf, send_sems, recv_sems, relay_send, relay_recv, after)


def _forward_halves(buf, name):
    half = buf.shape[1] // 2

    def body(buf_in, buf_ref, send_sems, recv_sems):
        del buf_in
        x, y, c = _my_place()
        sibling = (x, y, 1 - c)
        started = []
        for k, (_, src_chip) in enumerate(_ici_peers(x, y, c)):
            landed = _half_rows(buf_ref, src_chip, c, half)
            fw = pltpu.make_async_remote_copy(
                src_ref=landed, dst_ref=landed, send_sem=send_sems.at[k], recv_sem=recv_sems.at[k],
                device_id=sibling, device_id_type=MESH)
            fw.start()
            started.append(fw)
        for k, (_, src_chip) in enumerate(_ici_peers(x, y, c)):
            other = _half_rows(buf_ref, src_chip, 1 - c, half)
            pltpu.make_async_remote_copy(
                src_ref=other, dst_ref=other, send_sem=send_sems.at[k], recv_sem=recv_sems.at[k],
                device_id=sibling, device_id_type=MESH).wait_recv()
        for fw in started:
            fw.wait_send()

    return pl.pallas_call(
        body, name=name,
        out_shape=jax.ShapeDtypeStruct(buf.shape, buf.dtype),
        in_specs=[HBM], out_specs=HBM,
        input_output_aliases={0: 0},
        scratch_shapes=[pltpu.SemaphoreType.DMA((N_CHIPS - 1,))] * 2,
    )(buf)


def _swap_halves(g, name):
    half = g.shape[1] // 2

    def body(g_ref, r_ref, send_sem, recv_sem):
        x, y, c = _my_place()
        theirs = g_ref.at[:, pl.ds(pl.multiple_of((1 - c) * half, half), half), :]
        cp = pltpu.make_async_remote_copy(
            src_ref=theirs, dst_ref=r_ref, send_sem=send_sem, recv_sem=recv_sem,
            device_id=(x, y, 1 - c), device_id_type=MESH)
        cp.start()
        cp.wait()

    return pl.pallas_call(
        body, name=name,
        out_shape=jax.ShapeDtypeStruct((N_CHIPS, half, g.shape[2]), g.dtype),
        in_specs=[HBM], out_specs=HBM,
        scratch_shapes=[pltpu.SemaphoreType.DMA, pltpu.SemaphoreType.DMA],
    )(g)


def _owners_start(csum, name):
    land = pltpu.with_memory_space_constraint(lax.empty((N_CHIPS - 1,) + csum.shape[1:], csum.dtype), pltpu.HBM)

    def body(csum_ref, land_ref, send_sems, recv_sems, csum_thru, land_thru, token_ref):
        del csum_thru, land_thru
        x, y, c = _my_place()
        for k, (peer, owner) in enumerate(_ici_peers(x, y, c)):
            pltpu.make_async_remote_copy(
                src_ref=csum_ref.at[owner], dst_ref=land_ref.at[k], send_sem=send_sems.at[k], recv_sem=recv_sems.at[k],
                device_id=peer, device_id_type=MESH).start()
        token_ref[...] = jnp.zeros(token_ref.shape, F32)

    sems = pltpu.SemaphoreType.DMA((N_CHIPS - 1,))
    return pl.pallas_call(
        body, name=name,
        out_shape=(sems, sems, jax.ShapeDtypeStruct(csum.shape, csum.dtype),
                   jax.ShapeDtypeStruct(land.shape, land.dtype), TOKEN),
        in_specs=[HBM, HBM], out_specs=(SEM, SEM, HBM, HBM, VMEM),
        input_output_aliases={0: 2, 1: 3},
        compiler_params=pltpu.CompilerParams(has_side_effects=EFFECT),
    )(pltpu.with_memory_space_constraint(csum, pltpu.HBM), land)


def _owners_wait(send_sems, recv_sems, csum, land, after, name):
    def body(csum_ref, land_ref, send_ref, recv_ref, *rest):
        del rest
        x, y, c = _my_place()
        for k, (peer, owner) in enumerate(_ici_peers(x, y, c)):
            cp = pltpu.make_async_remote_copy(
                src_ref=csum_ref.at[owner], dst_ref=land_ref.at[k], send_sem=send_ref.at[k], recv_sem=recv_ref.at[k],
                device_id=peer, device_id_type=MESH)
            cp.wait_send()
            cp.wait_recv()

    return pl.pallas_call(
        body, name=name,
        out_shape=(jax.ShapeDtypeStruct(csum.shape, csum.dtype), jax.ShapeDtypeStruct(land.shape, land.dtype)),
        in_specs=[HBM, HBM, SEM, SEM] + [ANY] * len(after), out_specs=(HBM, HBM),
        input_output_aliases={0: 0, 1: 1},
        compiler_params=pltpu.CompilerParams(has_side_effects=EFFECT),
    )(csum, land, send_sems, recv_sems, *after)[1]


def _join_halves(full, name):
    rows = full.shape[0] // 2

    def body(full_in, full_ref, send_sem, recv_sem):
        del full_in
        x, y, c = _my_place()
        sibling = (x, y, 1 - c)
        mine = full_ref.at[pl.ds(pl.multiple_of(c * rows, rows), rows), :]
        theirs = full_ref.at[pl.ds(pl.multiple_of((1 - c) * rows, rows), rows), :]
        cp = pltpu.make_async_remote_copy(
            src_ref=mine, dst_ref=mine, send_sem=send_sem, recv_sem=recv_sem, device_id=sibling, device_id_type=MESH)
        cp.start()
        pltpu.make_async_remote_copy(
            src_ref=theirs, dst_ref=theirs, send_sem=send_sem, recv_sem=recv_sem,
            device_id=sibling, device_id_type=MESH).wait_recv()
        cp.wait_send()

    return pl.pallas_call(
        body, name=name,
        out_shape=jax.ShapeDtypeStruct(full.shape, full.dtype),
        in_specs=[HBM], out_specs=HBM,
        input_output_aliases={0: 0},
        scratch_shapes=[pltpu.SemaphoreType.DMA, pltpu.SemaphoreType.DMA],
    )(full)


def _cast_into_slot(place, w, name):
    rows, cols = w.shape
    tr = min(rows, ROW_TILE)

    def body(place_ref, w_ref, o_ref):
        del place_ref
        o_ref[...] = w_ref[...].astype(BF16)

    grid_spec = pltpu.PrefetchScalarGridSpec(
        num_scalar_prefetch=1, grid=(rows // tr,),
        in_specs=[pl.BlockSpec((tr, cols), lambda i, p: (i, 0))],
        out_specs=pl.BlockSpec((None, tr, cols), lambda i, p: (p[0], i, 0)))
    return pl.pallas_call(
        body, name=name, grid_spec=grid_spec,
        out_shape=jax.ShapeDtypeStruct((N_CHIPS, rows, cols), BF16),
        compiler_params=_params(("parallel",)),
    )(place, w)


def _ada_partial(c_all, w_ada):
    d_model, wa = w_ada.shape
    tn = 512 if wa % 512 == 0 else 256

    def body(c_ref, w_ref, o_ref):
        o_ref[...] = jnp.dot(_silu(c_ref[...]), w_ref[...], precision=lax.Precision.HIGHEST,
                             preferred_element_type=F32)

    return pl.pallas_call(
        body, name="ada_partial", grid=(wa // tn,),
        out_shape=jax.ShapeDtypeStruct((N_DEV, wa), F32),
        in_specs=[pl.BlockSpec((N_DEV, d_model), lambda i: (0, 0)), pl.BlockSpec((d_model, tn), lambda i: (0, i))],
        out_specs=pl.BlockSpec((N_DEV, tn), lambda i: (0, i)),
        compiler_params=_params(("parallel",)),
    )(c_all, w_ada)


def _prenorm(x, mod, g_pre):
    t, d = x.shape
    tb = ROW_TILE

    def body(x_ref, mod_ref, g_ref, h_ref, ht_ref):
        xv = x_ref[...]
        r = lax.rsqrt(jnp.mean(xv * xv, axis=-1, keepdims=True) + EPS)
        h = (xv * r) * g_ref[...] * (1.0 + mod_ref[1:2, :]) + mod_ref[0:1, :]
        h_ref[...] = h.astype(BF16)
        ht_ref[...] = h.T.astype(BF16)

    return pl.pallas_call(
        body, name="prenorm", grid=(t // tb,),
        out_shape=(jax.ShapeDtypeStruct((t, d), BF16), jax.ShapeDtypeStruct((d, t), BF16)),
        in_specs=[pl.BlockSpec((tb, d), lambda i: (i, 0)), pl.BlockSpec((3, d), lambda i: (0, 0)),
                  pl.BlockSpec((1, d), lambda i: (0, 0))],
        out_specs=(pl.BlockSpec((tb, d), lambda i: (i, 0)), pl.BlockSpec((d, tb), lambda i: (0, i))),
        compiler_params=_params(("parallel",)),
    )(x, mod, g_pre)


def _pairs_to_cols(ref, n_pairs):
    return jnp.concatenate([ref[i] for i in range(n_pairs)], axis=1)


def _proj_conv(h, winf):
    t, d = h.shape
    ws = winf.shape[2]
    tn = COL_TILE
    nt = ws // tn

    def body(a_ref, b_ref, o_ref):
        o_ref[...] = jnp.dot(a_ref[...], b_ref[...], preferred_element_type=F32).astype(BF16)

    return pl.pallas_call(
        body, name="proj_conv", grid=(2, nt),
        out_shape=jax.ShapeDtypeStruct((t, 2 * ws), BF16),
        in_specs=[pl.BlockSpec((t, d), lambda j, n: (0, 0)), pl.BlockSpec((None, d, tn), lambda j, n: (j, 0, n))],
        out_specs=pl.BlockSpec((t, tn), lambda j, n: (0, j * nt + n)),
        compiler_params=_params(("parallel", "parallel")),
    )(h, winf)


def _proj_attn(h, winf, da):
    t, d = h.shape
    ws = winf.shape[2]
    tn = COL_TILE
    nt = ws // tn
    per_comp = da // tn
    pairs = tn // PAIR

    def body(a_ref, b_ref, o_ref):
        res = jnp.dot(a_ref[...], b_ref[...], preferred_element_type=F32).astype(BF16)
        for i in range(pairs):
            o_ref[i] = res[:, i * PAIR:(i + 1) * PAIR]

    return pl.pallas_call(
        body, name="proj_attn", grid=(2, nt),
        out_shape=jax.ShapeDtypeStruct((4, da // PAIR, t, PAIR), BF16),
        in_specs=[pl.BlockSpec((t, d), lambda j, n: (0, 0)), pl.BlockSpec((None, d, tn), lambda j, n: (2 + j, 0, n))],
        out_specs=pl.BlockSpec((None, pairs, t, PAIR), lambda j, n: (2 * j + n // per_comp, n % per_comp, 0, 0)),
        compiler_params=_params(("parallel", "parallel")),
    )(h, winf)


def _shift_rows(a, rows):
    idx = lax.broadcasted_iota(jnp.int32, a.shape, 0)
    prev = jnp.where(idx == 0, 0.0, pltpu.roll(a, 1, 0))
    nxt = jnp.where(idx == rows - 1, 0.0, pltpu.roll(a, rows - 1, 0))
    return prev, nxt


def _conv_fwd(conv_proj, conv_w, conv_b, dc):
    t = conv_proj.shape[0]
    ct = CONV_TILE
    nct = dc // ct

    def body(u_ref, cg_ref, w_ref, b_ref, co_ref):
        a = cg_ref[...].astype(F32) * u_ref[...].astype(F32)
        prev, nxt = _shift_rows(a, t)
        co_ref[...] = w_ref[0:1, :] * prev + w_ref[1:2, :] * a + w_ref[2:3, :] * nxt + b_ref[...]

    return pl.pallas_call(
        body, name="conv_fwd", grid=(nct,),
        out_shape=jax.ShapeDtypeStruct((t, dc), F32),
        in_specs=[pl.BlockSpec((t, ct), lambda i: (0, i)), pl.BlockSpec((t, ct), lambda i: (0, 2 * nct + i)),
                  pl.BlockSpec((3, ct), lambda i: (0, i)), pl.BlockSpec((1, ct), lambda i: (0, i))],
        out_specs=pl.BlockSpec((t, ct), lambda i: (0, i)),
        compiler_params=_params(("parallel",)),
    )(conv_proj, conv_proj, conv_w, conv_b)


def _to_residue_major(src_ref, dst_ref, r):
    seq = src_ref.shape[0] // r
    for res in range(r):
        dst_ref[res * seq:(res + 1) * seq, :] = src_ref[pl.ds(res, seq, stride=r), :].astype(dst_ref.dtype)


def _branch_operands(token_refs, stage, dil, r):
    if r == 1:
        return list(token_refs)
    for i, ref in enumerate(token_refs):
        stage[...] = ref[...].astype(F32)
        _to_residue_major(stage, dil.at[i], r)
    return [dil.at[i] for i in range(len(token_refs))]


def _scaled_queries(q):
    return (q.astype(F32) * (HEAD_DIM ** -0.5)).astype(BF16)


BLOCK_SHIFTS = (0, -SIDE, None)


def _band_bias(rel, slope):
    arel = jnp.abs(rel)
    return jnp.where(arel <= SIDE, arel.astype(F32) * slope, NEG_INF)


def _fill_bias_tiles(bias_ref, sl_ref, r, kw):
    base = lax.broadcasted_iota(jnp.int32, (ATT_BQ, kw), 1) - lax.broadcasted_iota(jnp.int32, (ATT_BQ, kw), 0)
    for hh in range(2):
        slope = -(sl_ref[hh:hh + 1, 0:kw] * float(r))
        for e, shift in enumerate(BLOCK_SHIFTS):
            shift = ATT_BQ - kw if shift is None else shift
            bias_ref[hh, e, :, 0:kw] = _band_bias(base + shift, slope)


def _fill_stacked_bias_tiles(bias_ref, sl_ref, r, kw):
    base = lax.broadcasted_iota(jnp.int32, (kw, ATT_BQ), 0) - lax.broadcasted_iota(jnp.int32, (kw, ATT_BQ), 1)
    for hh in range(2):
        slope = -(sl_ref[hh:hh + 1, 0:ATT_BQ] * float(r))
        for e, shift in enumerate(BLOCK_SHIFTS):
            shift = ATT_BQ - kw if shift is None else shift
            bias_ref[e, 0:kw, hh * ATT_BQ:(hh + 1) * ATT_BQ] = _band_bias(base + shift, slope)


def _first_head_lanes():
    return lax.broadcasted_iota(jnp.int32, (1, PAIR), 1) < HEAD_DIM


def _only_head(x, first, hh):
    return jnp.where(first if hh == 0 else jnp.logical_not(first), x, jnp.zeros_like(x))


def _block_place(g, seq_len, kw):
    nqb = seq_len // ATT_BQ
    if nqb == 1:
        row = pl.multiple_of(g * ATT_BQ, ATT_BQ)
        return row, row, 0
    res = g // nqb
    qb = g - res * nqb
    q0 = qb * ATT_BQ
    ks = jnp.clip(q0 - SIDE, 0, seq_len - kw)
    edge = jnp.where(qb == 0, 0, jnp.where(qb == nqb - 1, 2, 1))
    return (pl.multiple_of(res * seq_len + q0, ATT_BQ), pl.multiple_of(res * seq_len + ks, SIDE), edge)


def _attn_fwd(attn_proj, slopes):
    _, hp, t, _ = attn_proj.shape
    n_blocks = t // ATT_BQ

    def body(qkv_ref, sl_ref, o_ref, lse_ref, stage, dil, bias, o_res, l_res, o_tok, l_tok):
        for b, (_, r) in enumerate(BRANCHES):
            seq_len = t // r
            kw = min(ATT_KW, seq_len)
            ops = _branch_operands([qkv_ref.at[comp] for comp in range(3)], stage, dil, r)
            _fill_bias_tiles(bias, sl_ref, r, kw)
            o_dst, l_dst = (o_tok.at[b], l_tok.at[b]) if r == 1 else (o_res, l_res)
            first = _first_head_lanes()

            def block(g, carry, seq_len=seq_len, kw=kw, o_dst=o_dst, l_dst=l_dst, first=first, ops=ops):
                qrow, krow, edge = _block_place(g, seq_len, kw)
                q = _scaled_queries(ops[0][pl.ds(qrow, ATT_BQ), :])
                k = ops[1][pl.ds(krow, kw), :]
                v = ops[2][pl.ds(krow, kw), :]
                ones = jnp.ones((kw, PAIR), BF16)
                both, tops = None, []
                for hh in range(2):
                    s = lax.dot_general(_only_head(q, first, hh), k, (((1,), (1,)), ((), ())),
                                        preferred_element_type=F32)
                    s = s + bias[hh, edge, :, 0:kw]
                    m = jnp.max(s, axis=-1, keepdims=True)
                    p = jnp.exp(s - m).astype(BF16)
                    rhs = jnp.concatenate([_only_head(v, first, hh), _only_head(ones, first, hh)], axis=1)
                    part = jnp.dot(p, rhs, preferred_element_type=F32)
                    both = part if both is None else both + part
                    tops.append(m)
                den = both[:, PAIR:]
                o_dst[pl.ds(qrow, ATT_BQ), :] = both[:, 0:PAIR] / den
                l_dst[pl.ds(qrow, ATT_BQ), :] = jnp.where(first, tops[0], tops[1]) + jnp.log(den)
                return carry

            lax.fori_loop(0, n_blocks, block, 0, unroll=ATT_UNROLL)
            if r > 1:
                for res in range(r):
                    rows = slice(res * seq_len, (res + 1) * seq_len)
                    o_tok[b, pl.ds(res, seq_len, stride=r), :] = o_res[rows, :]
                    l_tok[b, pl.ds(res, seq_len, stride=r), :] = l_res[rows, :]

        def merge(i, carry):
            rows = pl.ds(pl.multiple_of(i * ROW_TILE, ROW_TILE), ROW_TILE)
            la, lb, lc = l_tok[0, rows, :], l_tok[1, rows, :], l_tok[2, rows, :]
            m = jnp.maximum(jnp.maximum(la, lb), lc)
            wa, wb, wc = jnp.exp(la - m), jnp.exp(lb - m), jnp.exp(lc - m)
            den = wa + wb + wc
            o_ref[rows, :] = (wa * o_tok[0, rows, :] + wb * o_tok[1, rows, :] + wc * o_tok[2, rows, :]) * (1.0 / den)
            lse_ref[rows, :] = m + jnp.log(den)
            return carry

        lax.fori_loop(0, t // ROW_TILE, merge, 0)

    pair_spec = pl.BlockSpec((None, t, PAIR), lambda h: (h, 0, 0))
    return pl.pallas_call(
        body, name="attn_fwd", grid=(hp,),
        out_shape=(jax.ShapeDtypeStruct((hp, t, PAIR), F32), jax.ShapeDtypeStruct((hp, t, PAIR), F32)),
        in_specs=[pl.BlockSpec((3, None, t, PAIR), lambda h: (0, h, 0, 0)),
                  pl.BlockSpec((None, 8, ATT_KW), lambda h: (h, 0, 0))],
        out_specs=(pair_spec, pair_spec),
        scratch_shapes=[pltpu.VMEM((t, PAIR), F32), pltpu.VMEM((3, t, PAIR), BF16),
                        pltpu.VMEM((2, 3, ATT_BQ, ATT_KW), F32),
                        pltpu.VMEM((t, PAIR), F32), pltpu.VMEM((t, PAIR), F32),
                        pltpu.VMEM((3, t, PAIR), F32), pltpu.VMEM((3, t, PAIR), F32)],
        compiler_params=_params(("parallel",)),
    )(attn_proj, slopes)


def _attn_bwd(dattn, attn_proj, d_o, lse, delta, slopes, after):
    _, hp, t, _ = attn_proj.shape
    n_blocks = t // ATT_BQ

    def body(dattn_in, qkv_ref, do_ref, lse_ref, dl_ref, sl_ref, after_ref, out_ref,
             stage, dil, packed, packed_res, row_vecs, bias_t, acc, tot):
        del dattn_in, after_ref
        first = _first_head_lanes()
        lane = lax.broadcasted_iota(jnp.int32, (1, PAIR), 1)
        packed[...] = jnp.where((lane & (HEAD_DIM - 1)) < HEAD_DIM // 2, lse_ref[...], dl_ref[...])
        for b, (_, r) in enumerate(BRANCHES):
            seq_len = t // r
            kw = min(ATT_KW, seq_len)
            ops = _branch_operands([qkv_ref.at[comp] for comp in range(3)] + [do_ref], stage, dil, r)
            scalars = packed
            if r > 1:
                _to_residue_major(packed, packed_res, r)
                scalars = packed_res
            for g in range(n_blocks):
                flipped = scalars[g * ATT_BQ:(g + 1) * ATT_BQ, :].T
                for row in range(4):
                    row_vecs[g, row:row + 1, :] = flipped[row * (HEAD_DIM // 2):row * (HEAD_DIM // 2) + 1, :]
            _fill_stacked_bias_tiles(bias_t, sl_ref, r, kw)
            acc[1] = jnp.zeros((t, PAIR), F32)
            acc[2] = jnp.zeros((t, PAIR), F32)

            def block(g, carry, seq_len=seq_len, kw=kw, ops=ops):
                qrow, krow, edge = _block_place(g, seq_len, kw)
                nt = (((1,), (1,)), ((), ()))
                q = _scaled_queries(ops[0][pl.ds(qrow, ATT_BQ), :])
                k = ops[1][pl.ds(krow, kw), :]
                v = ops[2][pl.ds(krow, kw), :]
                dov = ops[3][pl.ds(qrow, ATT_BQ), :]
                q2 = jnp.concatenate([_only_head(q, first, 0), _only_head(q, first, 1)], axis=0)
                do2 = jnp.concatenate([_only_head(dov, first, 0), _only_head(dov, first, 1)], axis=0)
                rows = row_vecs[g]
                lse2 = jnp.concatenate([rows[0:1, :], rows[2:3, :]], axis=1)
                dl2 = jnp.concatenate([rows[1:2, :], rows[3:4, :]], axis=1)
                s_t = lax.dot_general(k, q2, nt, preferred_element_type=F32)
                p_t = jnp.exp(s_t + bias_t[edge, 0:kw, :] - lse2)
                dp_t = lax.dot_general(v, do2, nt, preferred_element_type=F32)
                ds_t = p_t * (dp_t - dl2)
                dv = jnp.dot(p_t.astype(BF16), do2, preferred_element_type=F32)
                dk = jnp.dot(ds_t.astype(BF16), q2, preferred_element_type=F32)
                ds = ds_t.T.astype(BF16)
                dq = (jnp.dot(ds[0:ATT_BQ, :], _only_head(k, first, 0), preferred_element_type=F32)
                      + jnp.dot(ds[ATT_BQ:2 * ATT_BQ, :], _only_head(k, first, 1), preferred_element_type=F32))
                acc[0, pl.ds(qrow, ATT_BQ), :] = dq * (HEAD_DIM ** -0.5)
                acc[1, pl.ds(krow, kw), :] += dk
                acc[2, pl.ds(krow, kw), :] += dv
                return carry

            lax.fori_loop(0, n_blocks, block, 0, unroll=ATT_UNROLL_BWD)
            for comp in range(3):
                if r == 1:
                    tot[comp] = acc[comp]
                else:
                    for res in range(r):
                        tok = pl.ds(res, seq_len, stride=r)
                        tot[comp, tok, :] = tot[comp, tok, :] + acc[comp, res * seq_len:(res + 1) * seq_len, :]
        for comp in range(3):
            out_ref[comp] = tot[comp].astype(BF16)

    pair_spec = pl.BlockSpec((None, t, PAIR), lambda h: (h, 0, 0))
    return pl.pallas_call(
        body, name="attn_bwd", grid=(hp,),
        out_shape=jax.ShapeDtypeStruct(dattn.shape, BF16),
        in_specs=[HBM, pl.BlockSpec((3, None, t, PAIR), lambda h: (0, h, 0, 0)), pair_spec, pair_spec, pair_spec,
                  pl.BlockSpec((None, 8, ATT_KW), lambda h: (h, 0, 0)), ANY],
        out_specs=pl.BlockSpec((3, None, t, PAIR), lambda h: (0, h, 0, 0)),
        input_output_aliases={0: 0},
        scratch_shapes=[pltpu.VMEM((t, PAIR), F32), pltpu.VMEM((4, t, PAIR), BF16),
                        pltpu.VMEM((t, PAIR), F32), pltpu.VMEM((t, PAIR), F32),
                        pltpu.VMEM((n_blocks, 8, ATT_BQ), F32), pltpu.VMEM((3, ATT_KW, 2 * ATT_BQ), F32),
                        pltpu.VMEM((3, t, PAIR), F32), pltpu.VMEM((3, t, PAIR), F32)],
        compiler_params=_params(("parallel",)),
    )(dattn, attn_proj, d_o, lse, delta, slopes, after)


def _mix_fwd(co, conv_proj, attn_proj, o_mix, g_conv, g_attn_pairs):
    t, dc = co.shape
    hp = attn_proj.shape[1]
    da = hp * PAIR
    tb = ROW_TILE

    def body(co_ref, bg_ref, zc_ref, za_ref, om_ref, gc_ref, ga_ref, ycat_ref, ycatt_ref):
        p = bg_ref[...].astype(F32) * co_ref[...]
        rc = lax.rsqrt(jnp.mean(p * p, axis=-1, keepdims=True) + EPS)
        yc = (p * rc) * gc_ref[...] * _silu(zc_ref[...].astype(F32))
        ycat_ref[:, 0:dc] = yc.astype(BF16)
        ycatt_ref[0:dc, :] = yc.T.astype(BF16)
        ssq = jnp.zeros((tb, 1), F32)
        for h in range(hp):
            o = om_ref[h]
            ssq = ssq + jnp.sum(o * o, axis=-1, keepdims=True)
        ra = lax.rsqrt(ssq * (1.0 / da) + EPS)
        for h in range(hp):
            ya = (om_ref[h] * ra) * ga_ref[h] * _silu(za_ref[h].astype(F32))
            ycat_ref[:, dc + h * PAIR:dc + (h + 1) * PAIR] = ya.astype(BF16)
            ycatt_ref[dc + h * PAIR:dc + (h + 1) * PAIR, :] = ya.T.astype(BF16)

    pair_spec = pl.BlockSpec((hp, tb, PAIR), lambda i: (0, i, 0))
    return pl.pallas_call(
        body, name="mix_fwd", grid=(t // tb,),
        out_shape=(jax.ShapeDtypeStruct((t, dc + da), BF16), jax.ShapeDtypeStruct((dc + da, t), BF16)),
        in_specs=[pl.BlockSpec((tb, dc), lambda i: (i, 0)),
                  pl.BlockSpec((tb, dc), lambda i: (i, 1)),
                  pl.BlockSpec((tb, dc), lambda i: (i, 3)),
                  pl.BlockSpec((None, hp, tb, PAIR), lambda i: (3, 0, i, 0)),
                  pair_spec,
                  pl.BlockSpec((1, dc), lambda i: (0, 0)),
                  pl.BlockSpec((hp, 1, PAIR), lambda i: (0, 0, 0))],
        out_specs=(pl.BlockSpec((tb, dc + da), lambda i: (i, 0)), pl.BlockSpec((dc + da, tb), lambda i: (0, i))),
        compiler_params=_params(("parallel",)),
    )(co, conv_proj, conv_proj, attn_proj, o_mix, g_conv, g_attn_pairs)


def _out_fwd_bwd(ycat, woutf, x, target, mod, g_post):
    t, d = x.shape
    n = ycat.shape[1]
    tb = ROW_TILE

    def body(a_ref, w_ref, x_ref, tg_ref, mod_ref, g_ref, dout_ref, dy_ref, acc_ref):
        y = jnp.dot(a_ref[...], w_ref[...], preferred_element_type=F32)
        r = lax.rsqrt(jnp.mean(y * y, axis=-1, keepdims=True) + EPS)
        nh = y * r
        gate = mod_ref[2:3, :]
        nrm = nh * g_ref[...]
        err = x_ref[...] + gate * nrm - tg_ref[...]
        dout = err * (1.0 / d)
        dout_ref[...] = dout
        dn = dout * gate
        a = dn * g_ref[...]
        dy = r * (a - nh * jnp.mean(a * nh, axis=-1, keepdims=True))
        dy_ref[...] = dy.astype(BF16)
        loss = 0.5 * jnp.sum(jnp.sum(err * err, axis=-1, keepdims=True) * (1.0 / d), axis=0, keepdims=True)
        part = jnp.concatenate(
            [jnp.sum(dout * nrm, axis=0, keepdims=True), jnp.sum(dn * nh, axis=0, keepdims=True),
             jnp.broadcast_to(loss, (1, d)), jnp.zeros((5, d), F32)], axis=0)

        @pl.when(pl.program_id(0) == 0)
        def _():
            acc_ref[...] = jnp.zeros(acc_ref.shape, F32)

        acc_ref[...] += part

    return pl.pallas_call(
        body, name="out_fwd_bwd", grid=(t // tb,),
        out_shape=(jax.ShapeDtypeStruct((t, d), F32), jax.ShapeDtypeStruct((t, d), BF16),
                   jax.ShapeDtypeStruct((8, d), F32)),
        in_specs=[pl.BlockSpec((tb, n), lambda i: (i, 0)), pl.BlockSpec((n, d), lambda i: (0, 0)),
                  pl.BlockSpec((tb, d), lambda i: (i, 0)), pl.BlockSpec((tb, d), lambda i: (i, 0)),
                  pl.BlockSpec((3, d), lambda i: (0, 0)), pl.BlockSpec((1, d), lambda i: (0, 0))],
        out_specs=(pl.BlockSpec((tb, d), lambda i: (i, 0)), pl.BlockSpec((tb, d), lambda i: (i, 0)),
                   pl.BlockSpec((8, d), lambda i: (0, 0))),
        compiler_params=_params(("arbitrary",)),
    )(ycat, woutf, x, target, mod, g_post)


def _matmul_nt(a, b, out_dtype, name):
    m, k = a.shape
    n = b.shape[0]
    tn = COL_TILE

    def body(a_ref, b_ref, o_ref):
        o_ref[...] = lax.dot_general(a_ref[...], b_ref[...], (((1,), (1,)), ((), ())),
                                     preferred_element_type=F32).astype(out_dtype)

    return pl.pallas_call(
        body, name=name, grid=(n // tn,),
        out_shape=jax.ShapeDtypeStruct((m, n), out_dtype),
        in_specs=[pl.BlockSpec((m, k), lambda i: (0, 0)), pl.BlockSpec((tn, k), lambda i: (i, 0))],
        out_specs=pl.BlockSpec((m, tn), lambda i: (0, i)),
        compiler_params=_params(("parallel",)),
    )(a, b)


def _matmul_nn(a, b, out_dtype, name):
    m, k = a.shape
    n = b.shape[1]
    tn = COL_TILE

    def body(a_ref, b_ref, o_ref):
        o_ref[...] = jnp.dot(a_ref[...], b_ref[...], preferred_element_type=F32).astype(out_dtype)

    return pl.pallas_call(
        body, name=name, grid=(n // tn,),
        out_shape=jax.ShapeDtypeStruct((m, n), out_dtype),
        in_specs=[pl.BlockSpec((m, k), lambda i: (0, 0)), pl.BlockSpec((k, tn), lambda i: (0, i))],
        out_specs=pl.BlockSpec((m, tn), lambda i: (0, i)),
        compiler_params=_params(("parallel",)),
    )(a, b)


def _mix_bwd(dycat, co, conv_proj, attn_proj, o_mix, g_conv, g_attn_pairs):
    t, dc = co.shape
    hp = attn_proj.shape[1]
    da = hp * PAIR
    tb = ROW_TILE

    def body(dy_ref, co_ref, bg_ref, zc_ref, za_ref, om_ref, gc_ref, ga_ref,
             dcp_ref, dco_ref, dza_ref, do_ref, dl_ref, dgc_ref, dga_ref):
        first = pl.program_id(0) == 0
        cov = co_ref[...]
        bg = bg_ref[...].astype(F32)
        zc = zc_ref[...].astype(F32)
        p = bg * cov
        rc = lax.rsqrt(jnp.mean(p * p, axis=-1, keepdims=True) + EPS)
        nh = p * rc
        dyc = dy_ref[:, 0:dc].astype(F32)
        dn = dyc * _silu(zc)
        a = dn * gc_ref[...]
        dp = rc * (a - nh * jnp.mean(a * nh, axis=-1, keepdims=True))
        dcp_ref[:, 0:dc] = jnp.zeros((tb, dc), BF16)
        dcp_ref[:, dc:2 * dc] = (dp * cov).astype(BF16)
        dcp_ref[:, 2 * dc:3 * dc] = jnp.zeros((tb, dc), BF16)
        dcp_ref[:, 3 * dc:4 * dc] = (dyc * nh * gc_ref[...] * _silu_grad(zc)).astype(BF16)
        dco_ref[...] = dp * bg

        @pl.when(first)
        def _():
            dgc_ref[...] = jnp.zeros(dgc_ref.shape, F32)
            dga_ref[...] = jnp.zeros(dga_ref.shape, F32)

        dgc_ref[...] += jnp.sum(dn * nh, axis=0, keepdims=True)

        ssq = jnp.zeros((tb, 1), F32)
        for h in range(hp):
            o = om_ref[h]
            ssq = ssq + jnp.sum(o * o, axis=-1, keepdims=True)
        ra = lax.rsqrt(ssq * (1.0 / da) + EPS)
        dot_an = jnp.zeros((tb, 1), F32)
        for h in range(hp):
            nha = om_ref[h] * ra
            za = za_ref[h].astype(F32)
            dya = dy_ref[:, dc + h * PAIR:dc + (h + 1) * PAIR].astype(F32)
            dna = dya * _silu(za)
            dza_ref[h] = (dya * nha * ga_ref[h] * _silu_grad(za)).astype(BF16)
            dga_ref[h] += jnp.sum(dna * nha, axis=0, keepdims=True)
            dot_an = dot_an + jnp.sum(dna * ga_ref[h] * nha, axis=-1, keepdims=True)
        mean_an = dot_an * (1.0 / da)
        first_head = lax.broadcasted_iota(jnp.int32, (tb, PAIR), 1) < HEAD_DIM
        for h in range(hp):
            o = om_ref[h]
            nha = o * ra
            za = za_ref[h].astype(F32)
            dya = dy_ref[:, dc + h * PAIR:dc + (h + 1) * PAIR].astype(F32)
            aa = dya * _silu(za) * ga_ref[h]
            d_o = ra * (aa - nha * mean_an)
            do_ref[h] = d_o.astype(BF16)
            prod = d_o * o
            both = jnp.sum(prod, axis=-1, keepdims=True)
            head0 = jnp.sum(jnp.where(first_head, prod, 0.0), axis=-1, keepdims=True)
            dl_ref[h] = jnp.where(first_head, head0, both - head0)

    pair_spec = pl.BlockSpec((hp, tb, PAIR), lambda i: (0, i, 0))
    return pl.pallas_call(
        body, name="mix_bwd", grid=(t // tb,),
        out_shape=(jax.ShapeDtypeStruct((t, 4 * dc), BF16), jax.ShapeDtypeStruct((t, dc), F32),
                   jax.ShapeDtypeStruct((4, hp, t, PAIR), BF16), jax.ShapeDtypeStruct((hp, t, PAIR), BF16),
                   jax.ShapeDtypeStruct((hp, t, PAIR), F32),
                   jax.ShapeDtypeStruct((1, dc), F32), jax.ShapeDtypeStruct((hp, 1, PAIR), F32)),
        in_specs=[pl.BlockSpec((tb, dc + da), lambda i: (i, 0)),
                  pl.BlockSpec((tb, dc), lambda i: (i, 0)),
                  pl.BlockSpec((tb, dc), lambda i: (i, 1)),
                  pl.BlockSpec((tb, dc), lambda i: (i, 3)),
                  pl.BlockSpec((None, hp, tb, PAIR), lambda i: (3, 0, i, 0)),
                  pair_spec,
                  pl.BlockSpec((1, dc), lambda i: (0, 0)),
                  pl.BlockSpec((hp, 1, PAIR), lambda i: (0, 0, 0))],
        out_specs=(pl.BlockSpec((tb, 4 * dc), lambda i: (i, 0)), pl.BlockSpec((tb, dc), lambda i: (i, 0)),
                   pl.BlockSpec((None, hp, tb, PAIR), lambda i: (3, 0, i, 0)), pair_spec, pair_spec,
                   pl.BlockSpec((1, dc), lambda i: (0, 0)), pl.BlockSpec((hp, 1, PAIR), lambda i: (0, 0, 0))),
        compiler_params=_params(("arbitrary",)),
    )(dycat, co, conv_proj, conv_proj, attn_proj, o_mix, g_conv, g_attn_pairs)


def _conv_bwd(dconv_proj, dco, conv_proj, conv_w, dc, after):
    t = dco.shape[0]
    ct = CONV_TILE
    nct = dc // ct

    def body(dcp_in_ref, dco_ref, u_ref, cg_ref, w_ref, after_ref, dcp_ref, acc_ref):
        del dcp_in_ref, after_ref
        which = pl.program_id(1)
        g = dco_ref[...]
        u = u_ref[...].astype(F32)
        cg = cg_ref[...].astype(F32)
        g_prev, g_next = _shift_rows(g, t)
        da = w_ref[0:1, :] * g_next + w_ref[1:2, :] * g + w_ref[2:3, :] * g_prev
        dcp_ref[...] = (da * jnp.where(which == 0, cg, u)).astype(BF16)
        a = cg * u
        a_prev, a_next = _shift_rows(a, t)
        acc_ref[...] = jnp.concatenate(
            [jnp.sum(g * a_prev, axis=0, keepdims=True), jnp.sum(g * a, axis=0, keepdims=True),
             jnp.sum(g * a_next, axis=0, keepdims=True), jnp.sum(g, axis=0, keepdims=True),
             jnp.zeros((4, ct), F32)], axis=0)

    return pl.pallas_call(
        body, name="conv_bwd", grid=(nct, 2),
        out_shape=(jax.ShapeDtypeStruct(dconv_proj.shape, BF16), jax.ShapeDtypeStruct((8, dc), F32)),
        in_specs=[HBM,
                  pl.BlockSpec((t, ct), lambda i, s: (0, i)),
                  pl.BlockSpec((t, ct), lambda i, s: (0, i)),
                  pl.BlockSpec((t, ct), lambda i, s: (0, 2 * nct + i)),
                  pl.BlockSpec((3, ct), lambda i, s: (0, i)), ANY],
        out_specs=(pl.BlockSpec((t, ct), lambda i, s: (0, 2 * s * nct + i)),
                   pl.BlockSpec((8, ct), lambda i, s: (0, i))),
        input_output_aliases={0: 0},
        compiler_params=_params(("arbitrary", "arbitrary")),
    )(dconv_proj, dco, conv_proj, conv_proj, conv_w, after)


def _dw_in(ht, dconv_proj, dattn, ws, da):
    d, t = ht.shape
    tn = COL_TILE
    nt = ws // tn
    per_comp = da // tn
    pairs = tn // PAIR

    def body_conv(a_ref, b_ref, o_ref):
        o_ref[...] = jnp.dot(a_ref[...], b_ref[...], preferred_element_type=F32).astype(BF16)

    gin = pl.pallas_call(
        body_conv, name="dw_in_conv", grid=(2, nt),
        out_shape=jax.ShapeDtypeStruct((N_CHIPS, d, ws), BF16),
        in_specs=[pl.BlockSpec((d, t), lambda j, n: (0, 0)), pl.BlockSpec((t, tn), lambda j, n: (0, j * nt + n))],
        out_specs=pl.BlockSpec((None, d, tn), lambda j, n: (j, 0, n)),
        compiler_params=_params(("parallel", "parallel")),
    )(ht, dconv_proj)

    def body_attn(g_ref, a_ref, b_ref, o_ref):
        del g_ref
        o_ref[...] = jnp.dot(a_ref[...], _pairs_to_cols(b_ref, pairs), preferred_element_type=F32).astype(BF16)

    return pl.pallas_call(
        body_attn, name="dw_in_attn", grid=(2, nt),
        out_shape=jax.ShapeDtypeStruct((N_CHIPS, d, ws), BF16),
        in_specs=[HBM, pl.BlockSpec((d, t), lambda j, n: (0, 0)),
                  pl.BlockSpec((None, pairs, t, PAIR), lambda j, n: (2 * j + n // per_comp, n % per_comp, 0, 0))],
        out_specs=pl.BlockSpec((None, d, tn), lambda j, n: (2 + j, 0, n)),
        input_output_aliases={0: 0},
        compiler_params=_params(("parallel", "parallel")),
    )(gin, ht, dattn)


def _dh(dconv_proj, dattn, winf, after):
    t = dconv_proj.shape[0]
    _, d, ws = winf.shape
    hp = dattn.shape[1]
    tm = tn = COL_TILE
    nt = (((1,), (1,)), ((), ()))

    def body(conv_ref, attn_ref, w_ref, after_ref, o_ref):
        del after_ref
        acc = lax.dot_general(conv_ref[:, 0:ws], w_ref[0], nt, preferred_element_type=F32)
        acc = acc + lax.dot_general(conv_ref[:, ws:2 * ws], w_ref[1], nt, preferred_element_type=F32)
        for j in range(2):
            cols = jnp.concatenate([attn_ref[2 * j + comp, h] for comp in range(2) for h in range(hp)], axis=1)
            acc = acc + lax.dot_general(cols, w_ref[2 + j], nt, preferred_element_type=F32)
        o_ref[...] = acc

    return pl.pallas_call(
        body, name="dh", grid=(d // tn, t // tm),
        out_shape=jax.ShapeDtypeStruct((t, d), F32),
        in_specs=[pl.BlockSpec((tm, 2 * ws), lambda n, m: (m, 0)),
                  pl.BlockSpec((4, hp, tm, PAIR), lambda n, m: (0, 0, m, 0)),
                  pl.BlockSpec((N_CHIPS, tn, ws), lambda n, m: (0, n, 0)), ANY],
        out_specs=pl.BlockSpec((tm, tn), lambda n, m: (m, n)),
        compiler_params=_params(("parallel", "parallel")),
    )(dconv_proj, dattn, winf, after)


def _prenorm_bwd(x, dh, dout, mod, g_pre):
    t, d = x.shape
    tb = ROW_TILE

    def body(x_ref, dh_ref, dout_ref, mod_ref, g_ref, gx_ref, acc_ref):
        xv = x_ref[...]
        dhv = dh_ref[...]
        r = lax.rsqrt(jnp.mean(xv * xv, axis=-1, keepdims=True) + EPS)
        xh = xv * r
        one_scale = 1.0 + mod_ref[1:2, :]
        a = dhv * one_scale * g_ref[...]
        gx_ref[...] = dout_ref[...] + r * (a - xh * jnp.mean(a * xh, axis=-1, keepdims=True))
        part = jnp.concatenate(
            [jnp.sum(dhv, axis=0, keepdims=True), jnp.sum(dhv * xh * g_ref[...], axis=0, keepdims=True),
             jnp.sum(dhv * xh * one_scale, axis=0, keepdims=True), jnp.zeros((5, d), F32)], axis=0)

        @pl.when(pl.program_id(0) == 0)
        def _():
            acc_ref[...] = jnp.zeros(acc_ref.shape, F32)

        acc_ref[...] += part

    return pl.pallas_call(
        body, name="prenorm_bwd", grid=(t // tb,),
        out_shape=(jax.ShapeDtypeStruct((t, d), F32), jax.ShapeDtypeStruct((8, d), F32)),
        in_specs=[pl.BlockSpec((tb, d), lambda i: (i, 0)), pl.BlockSpec((tb, d), lambda i: (i, 0)),
                  pl.BlockSpec((tb, d), lambda i: (i, 0)), pl.BlockSpec((3, d), lambda i: (0, 0)),
                  pl.BlockSpec((1, d), lambda i: (0, 0))],
        out_specs=(pl.BlockSpec((tb, d), lambda i: (i, 0)), pl.BlockSpec((8, d), lambda i: (0, 0))),
        compiler_params=_params(("arbitrary",)),
    )(x, dh, dout, mod, g_pre)


def _chip_sums(place, g, rsib, name):
    _, rows, cols = g.shape
    half = rows // 2
    tr = min(half, ROW_TILE)
    nt = half // tr

    def body(place_ref, g_ref, r_ref, o_ref):
        del place_ref
        o_ref[...] = (g_ref[...].astype(F32) + r_ref[...].astype(F32)).astype(BF16)

    grid_spec = pltpu.PrefetchScalarGridSpec(
        num_scalar_prefetch=1, grid=(N_CHIPS, nt),
        in_specs=[pl.BlockSpec((None, tr, cols), lambda j, i, p: (j, p[1] * nt + i, 0)),
                  pl.BlockSpec((None, tr, cols), lambda j, i, p: (j, i, 0))],
        out_specs=pl.BlockSpec((None, tr, cols), lambda j, i, p: (j, i, 0)))
    return pl.pallas_call(
        body, name=name, grid_spec=grid_spec,
        out_shape=jax.ShapeDtypeStruct((N_CHIPS, half, cols), BF16),
        compiler_params=_params(("parallel", "parallel")),
    )(place, g, rsib)


def _owner_sum(place, g, rsib, rici, name):
    _, rows, cols = g.shape
    half = rows // 2
    tr = min(half, ROW_TILE)
    nt = half // tr

    def body(place_ref, g_ref, r_ref, i_ref, o_ref):
        del place_ref
        acc = g_ref[...].astype(F32) + r_ref[...].astype(F32)
        for k in range(N_CHIPS - 1):
            acc = acc + i_ref[k].astype(F32)
        o_ref[...] = acc

    grid_spec = pltpu.PrefetchScalarGridSpec(
        num_scalar_prefetch=1, grid=(nt,),
        in_specs=[pl.BlockSpec((None, tr, cols), lambda i, p: (p[0], p[1] * nt + i, 0)),
                  pl.BlockSpec((None, tr, cols), lambda i, p: (p[0], i, 0)),
                  pl.BlockSpec((N_CHIPS - 1, tr, cols), lambda i, p: (0, i, 0))],
        out_specs=pl.BlockSpec((tr, cols), lambda i, p: (p[1] * nt + i, 0)))
    return pl.pallas_call(
        body, name=name, grid_spec=grid_spec,
        out_shape=jax.ShapeDtypeStruct((rows, cols), F32),
        compiler_params=_params(("parallel",)),
    )(place, g, rsib, rici)


def _adam_math(w, g, m, v):
    m2 = ADAM_B1 * m + (1.0 - ADAM_B1) * g
    v2 = ADAM_B2 * v + (1.0 - ADAM_B2) * (g * g)
    m_hat = m2 / (1.0 - ADAM_B1 ** ADAM_STEP)
    v_hat = v2 / (1.0 - ADAM_B2 ** ADAM_STEP)
    delta = -ADAM_LR * (m_hat / (jnp.sqrt(v_hat) + ADAM_EPS) + ADAM_WD * w)
    return delta, m2, v2


def _adamw(w, g, m, v, name):
    rows, cols = w.shape
    tr = min(rows, ROW_TILE)

    def body(w_ref, g_ref, m_ref, v_ref, go_ref, d_ref, m2_ref, v2_ref):
        g = g_ref[...]
        go_ref[...] = g
        d_ref[...], m2_ref[...], v2_ref[...] = _adam_math(w_ref[...], g, m_ref[...], v_ref[...])

    spec = pl.BlockSpec((tr, cols), lambda i: (i, 0))
    return pl.pallas_call(
        body, name=name, grid=(rows // tr,),
        out_shape=(jax.ShapeDtypeStruct(w.shape, F32),) * 4,
        in_specs=[spec] * 4, out_specs=(spec,) * 4,
        compiler_params=_params(("parallel",)),
    )(w, g, m, v)


def _ada_grad_adamw(c_all_t, dmod_cols, w, m, v):
    d, wa = w.shape
    tr = ROW_TILE

    def body(ct_ref, dm_ref, w_ref, m_ref, v_ref, g_ref, d_ref, m2_ref, v2_ref):
        act = _silu(ct_ref[...])
        g = act[:, 0:1] * dm_ref[0:1, :]
        for b in range(1, N_DEV):
            g = g + act[:, b:b + 1] * dm_ref[b:b + 1, :]
        g_ref[...] = g
        d_ref[...], m2_ref[...], v2_ref[...] = _adam_math(w_ref[...], g, m_ref[...], v_ref[...])

    spec = pl.BlockSpec((tr, wa), lambda i: (i, 0))
    return pl.pallas_call(
        body, name="ada_grad_adamw", grid=(d // tr,),
        out_shape=(jax.ShapeDtypeStruct(w.shape, F32),) * 4,
        in_specs=[pl.BlockSpec((tr, N_DEV), lambda i: (i, 0)), pl.BlockSpec((N_DEV, wa), lambda i: (0, 0)),
                  spec, spec, spec],
        out_specs=(spec,) * 4,
        compiler_params=_params(("parallel",)),
    )(c_all_t, dmod_cols, w, m, v)


def _sum_devices(gathered):
    n = gathered.shape[1]

    def body(g_ref, o_ref):
        acc = g_ref[0:8, :]
        for dev in range(1, N_DEV):
            acc = acc + g_ref[8 * dev:8 * dev + 8, :]
        o_ref[...] = acc

    return pl.pallas_call(
        body, name="sum_devices",
        out_shape=jax.ShapeDtypeStruct((8, n), F32),
        in_specs=[VMEM], out_specs=VMEM,
    )(gathered)


def _pack_small(pieces):
    flat = [p.reshape(-1).astype(F32) for p in pieces]
    offsets, total = [], 0
    for p in flat:
        offsets.append(total)
        total += p.shape[0]
    padded = -(-total // SMALL_ALIGN) * SMALL_ALIGN
    if padded > total:
        flat.append(jnp.zeros((padded - total,), F32))
    return jnp.concatenate(flat).reshape(8, padded // 8), offsets


def _alibi_slope_rows(n_heads):
    slopes = 2.0 ** (-8.0 * jnp.arange(1, n_heads + 1, dtype=F32) / n_heads)
    rows = jnp.zeros((n_heads // 2, 8), F32).at[:, 0:2].set(slopes.reshape(n_heads // 2, 2))
    return jnp.broadcast_to(rows[:, :, None], (n_heads // 2, 8, ATT_KW))


def kernel(x, c, w_ada, b_ada, g_pre, w_in, conv_w, conv_b, g_conv, g_attn, w_out, g_post, loss_target, m_w_ada, m_b_ada, m_g_pre, m_w_in, m_conv_w, m_conv_b, m_g_conv, m_g_attn, m_w_out, m_g_post, v_w_ada, v_b_ada, v_g_pre, v_w_in, v_conv_w, v_conv_b, v_g_conv, v_g_attn, v_w_out, v_g_post):
    t, d = x.shape[1], x.shape[2]
    dc = conv_b.shape[1]
    da = g_attn.shape[1]
    hp = da // PAIR
    ws = w_in.shape[2]
    wa = w_ada.shape[2]
    cws = conv_w.shape[2]
    assert t % ROW_TILE == 0 and d % ROW_TILE == 0 and dc % COL_TILE == 0 and da % COL_TILE == 0
    assert ws == 2 * dc and dc == da and t // BRANCHES[-1][1] >= ATT_BQ

    mx, my, mc = _my_place()
    chip = _chip_of(mx, my)
    dev = 2 * chip + mc
    place = jnp.stack([chip, mc]).astype(jnp.int32)

    x2, tgt2 = x[0], loss_target[0]
    w_ada2, w_in2, w_out2 = w_ada[0], w_in[0], w_out[0]

    packed, offs = _pack_small([c[0], conv_w[0]])
    seen = _allgather8(packed, "gather_inputs").reshape(N_DEV, -1)
    c_all = seen[:, offs[0]:offs[0] + d]
    conv_w_full = seen[0::2, offs[1]:offs[1] + 3 * cws].reshape(N_CHIPS, 3, cws).transpose(1, 0, 2).reshape(3, dc)

    ada_part = _ada_partial(c_all, w_ada2)
    ada_seen = _allgather8(ada_part, "gather_ada").reshape(N_DEV, N_DEV, wa)
    mod_flat = lax.dynamic_index_in_dim(ada_seen[0::2], dev, axis=1, keepdims=False).reshape(1, 3 * d) + b_ada
    mod = mod_flat.reshape(3, d)

    win_flight, send_in, recv_in, started = _gather_start(_cast_into_slot(place, w_in2, "cast_w_in"), mod)

    h, ht = _prenorm(x2, mod + started[0, 0], g_pre)
    win_flight, wout_flight, relay_send_in, relay_recv_in, send_out, recv_out = _gather_relay_in(
        win_flight, _cast_into_slot(place, w_out2, "cast_w_out"), recv_in, h)
    winf = _forward_halves(
        _gather_wait(win_flight, send_in, recv_in, relay_send_in, relay_recv_in, h, "gather_wait_w_in"), "forward_w_in")
    conv_proj = _proj_conv(h, winf)
    attn_proj = _proj_attn(h, winf, da)
    slopes = _alibi_slope_rows(da // HEAD_DIM)
    co = _conv_fwd(conv_proj, conv_w_full, conv_b, dc)
    wout_flight, relay_send_out, relay_recv_out = _gather_relay_out(wout_flight, recv_out, co)
    o_mix, lse = _attn_fwd(attn_proj, slopes)
    g_attn_pairs = g_attn.reshape(hp, 1, PAIR)
    ycat, ycat_t = _mix_fwd(co, conv_proj, attn_proj, o_mix, g_conv, g_attn_pairs)
    woutf4 = _forward_halves(
        _gather_wait(wout_flight, send_out, recv_out, relay_send_out, relay_recv_out, ycat, "gather_wait_w_out"),
        "forward_w_out")
    woutf = woutf4.reshape(dc + da, d)
    dout, dy, post_sums = _out_fwd_bwd(ycat, woutf, x2, tgt2, mod, g_post)

    gout = _matmul_nn(ycat_t, dy, BF16, "dw_out").reshape(N_CHIPS, (dc + da) // N_CHIPS, d)
    rsib_out = _swap_halves(gout, "rs_swap_halves_out")
    csum_out = _chip_sums(place, gout, rsib_out, "rs_chip_sum_out")
    ssem_out, rsem_out, csum_out, land_out, sent_out = _owners_start(csum_out, "rs_owners_start_out")
    dycat = _matmul_nt(dy, woutf, BF16, "dycat")
    dconv_proj, dco, dattn, d_o, delta, dg_conv, dg_attn = _mix_bwd(
        dycat, co, conv_proj, attn_proj, o_mix, g_conv, g_attn_pairs)
    dconv_proj, conv_sums = _conv_bwd(dconv_proj, dco, conv_proj, conv_w_full, dc, sent_out)
    dattn = _attn_bwd(dattn, attn_proj, d_o, lse, delta, slopes, sent_out)
    gin = _dw_in(ht, dconv_proj, dattn, ws, da)
    rsib_in = _swap_halves(gin, "rs_swap_halves_in")
    csum_in = _chip_sums(place, gin, rsib_in, "rs_chip_sum_in")
    ssem_in, rsem_in, csum_in, land_in, sent_in = _owners_start(csum_in, "rs_owners_start_in")
    dh = _dh(dconv_proj, dattn, winf, sent_in)
    grad_x, pre_sums = _prenorm_bwd(x2, dh, dout, mod, g_pre)

    rici_out = _owners_wait(ssem_out, rsem_out, csum_out, land_out, [grad_x], "rs_owners_wait_out")
    grad_w_out = _join_halves(_owner_sum(place, gout, rsib_out, rici_out, "rs_owner_sum_out"), "rs_join_halves_out")

    small, so = _pack_small([
        pre_sums[0], pre_sums[1], post_sums[0],
        pre_sums[2], conv_sums[0:3], conv_sums[3], dg_conv, dg_attn, post_sums[1], post_sums[2, 0:128]])
    small_seen = _allgather8(small, "gather_small")
    total = _sum_devices(small_seen).reshape(-1)
    dmod_all = small_seen.reshape(N_DEV, -1)[:, 0:3 * d]
    loss = total[so[9]]
    grad_b_ada = total[0:3 * d].reshape(1, 3 * d)
    grad_g_pre = total[so[3]:so[3] + d].reshape(1, d)
    grad_conv_w_full = total[so[4]:so[4] + 3 * dc].reshape(3, dc)
    grad_conv_w = lax.dynamic_slice_in_dim(grad_conv_w_full, chip * cws, cws, axis=1).reshape(1, 3, cws)
    grad_conv_b = total[so[5]:so[5] + dc].reshape(1, dc)
    grad_g_conv = total[so[6]:so[6] + dc].reshape(1, dc)
    grad_g_attn = total[so[7]:so[7] + da].reshape(1, da)
    grad_g_post = total[so[8]:so[8] + d].reshape(1, d)

    dmod_cols = lax.dynamic_slice_in_dim(dmod_all, chip * wa, wa, axis=1)
    grad_w_ada, delta_w_ada, new_m_w_ada, new_v_w_ada = _ada_grad_adamw(c_all.T, dmod_cols, w_ada2, m_w_ada[0], v_w_ada[0])
    grad_w_out, delta_w_out, new_m_w_out, new_v_w_out = _adamw(
        w_out2, grad_w_out, m_w_out[0], v_w_out[0], "adamw_w_out")

    small_w = [b_ada, g_pre, conv_w, conv_b, g_conv, g_attn, g_post]
    small_g = [grad_b_ada, grad_g_pre, grad_conv_w, grad_conv_b, grad_g_conv, grad_g_attn, grad_g_post]
    small_m = [m_b_ada, m_g_pre, m_conv_w, m_conv_b, m_g_conv, m_g_attn, m_g_post]
    small_v = [v_b_ada, v_g_pre, v_conv_w, v_conv_b, v_g_conv, v_g_attn, v_g_post]
    pw, po = _pack_small(small_w)
    pg, _ = _pack_small(small_g)
    pm, _ = _pack_small(small_m)
    pv, _ = _pack_small(small_v)
    sd, sm, sv = (a.reshape(-1) for a in _adamw(pw, pg, pm, pv, "adamw_small")[1:])

    def unpack(flat):
        return [flat[o:o + w.size].reshape(w.shape) for o, w in zip(po, small_w)]

    d_small, m_small, v_small = unpack(sd), unpack(sm), unpack(sv)

    rici_in = _owners_wait(ssem_in, rsem_in, csum_in, land_in, [sd, delta_w_out, delta_w_ada], "rs_owners_wait_in")
    grad_w_in = _join_halves(_owner_sum(place, gin, rsib_in, rici_in, "rs_owner_sum_in"), "rs_join_halves_in")
    grad_w_in, delta_w_in, new_m_w_in, new_v_w_in = _adamw(w_in2, grad_w_in, m_w_in[0], v_w_in[0], "adamw_w_in")

    def lead(a):
        return a.reshape((1,) + a.shape)

    grads = [lead(grad_w_ada), grad_b_ada, grad_g_pre, lead(grad_w_in), grad_conv_w, grad_conv_b, grad_g_conv,
             grad_g_attn, lead(grad_w_out), grad_g_post]
    deltas = [lead(delta_w_ada), d_small[0], d_small[1], lead(delta_w_in), d_small[2], d_small[3], d_small[4],
              d_small[5], lead(delta_w_out), d_small[6]]
    new_ms = [lead(new_m_w_ada), m_small[0], m_small[1], lead(new_m_w_in), m_small[2], m_small[3], m_small[4],
              m_small[5], lead(new_m_w_out), m_small[6]]
    new_vs = [lead(new_v_w_ada), v_small[0], v_small[1], lead(new_v_w_in), v_small[2], v_small[3], v_small[4],
              v_small[5], lead(new_v_w_out), v_small[6]]
    return (loss, lead(grad_x), *grads, *deltas, *new_ms, *new_vs)
```

```python
import functools

import jax
import jax.numpy as jnp
from jax import lax
from jax.experimental import pallas as pl
from jax.experimental.pallas import tpu as pltpu

F32 = jnp.float32
BF16 = jnp.bfloat16
MESH = pl.DeviceIdType.MESH
HBM = pl.BlockSpec(memory_space=pltpu.HBM)
VMEM = pl.BlockSpec(memory_space=pltpu.VMEM)
ANY = pl.BlockSpec(memory_space=pl.ANY)
SEM = pl.BlockSpec(memory_space=pltpu.SEMAPHORE)
EFFECT = pltpu.SideEffectType.DATAFLOW_SIDE_EFFECTING
TOKEN = jax.ShapeDtypeStruct((8, 128), jnp.float32)

HEAD_DIM = 64
PAIR = 2 * HEAD_DIM
BRANCHES = ((128, 1), (512, 4), (2048, 16))
SIDE = 64
EPS = 1e-6
NEG_INF = -1e30
N_CHIPS = 4
N_DEV = 8

ADAM_LR = 0.001
ADAM_B1 = 0.9
ADAM_B2 = 0.999
ADAM_EPS = 1e-08
ADAM_WD = 0.01
ADAM_STEP = 10

VMEM_LIMIT_BYTES = 56 * 1024 * 1024
ROW_TILE = 256
COL_TILE = 512
CONV_TILE = 256
ATT_BQ = 128
ATT_KW = ATT_BQ + 2 * SIDE
ATT_UNROLL = 4
ATT_UNROLL_BWD = 4
SMALL_ALIGN = 1024


def _params(semantics=None):
    kw = {"vmem_limit_bytes": VMEM_LIMIT_BYTES}
    if semantics is not None:
        kw["dimension_semantics"] = semantics
    return pltpu.CompilerParams(**kw)


def _silu(z):
    return z * jax.nn.sigmoid(z)


def _silu_grad(z):
    s = jax.nn.sigmoid(z)
    return s * (1.0 + z * (1.0 - s))


def _my_place():
    return lax.axis_index("x"), lax.axis_index("y"), lax.axis_index("c")


def _flip(a, bit):
    return 1 - a if bit else a


def _chip_of(x, y):
    return 2 * x + y


def _allgather8(v, name):
    rows_per, n = v.shape

    def body(v_ref, out_ref, send_sems, recv_sems):
        x, y, c = _my_place()
        me = 4 * x + 2 * y + c

        def rows(idx):
            return out_ref.at[pl.ds(pl.multiple_of(idx * rows_per, rows_per), rows_per), :]

        out_ref[pl.ds(pl.multiple_of(me * rows_per, rows_per), rows_per), :] = v_ref[...]
        copies = []
        for k in range(1, N_DEV):
            peer = (_flip(x, k & 4), _flip(y, k & 2), _flip(c, k & 1))
            cp = pltpu.make_async_remote_copy(
                src_ref=v_ref, dst_ref=rows(me), send_sem=send_sems.at[k - 1], recv_sem=recv_sems.at[k - 1],
                device_id=peer, device_id_type=MESH)
            cp.start()
            copies.append((cp, peer))
        for k, (cp, peer) in enumerate(copies):
            src = 4 * peer[0] + 2 * peer[1] + peer[2]
            pltpu.make_async_remote_copy(
                src_ref=v_ref, dst_ref=rows(src), send_sem=send_sems.at[k], recv_sem=recv_sems.at[k],
                device_id=peer, device_id_type=MESH).wait_recv()
        for cp, _ in copies:
            cp.wait_send()

    return pl.pallas_call(
        body, name=name,
        out_shape=jax.ShapeDtypeStruct((N_DEV * rows_per, n), v.dtype),
        in_specs=[VMEM], out_specs=VMEM,
        scratch_shapes=[pltpu.SemaphoreType.DMA((N_DEV - 1,)), pltpu.SemaphoreType.DMA((N_DEV - 1,))],
    )(v)


def _half_rows(ref, chip, which, half):
    return ref.at[chip, pl.ds(pl.multiple_of(which * half, half), half), :]


def _ici_peers(x, y, c):
    peers = [(_flip(x, k & 2), _flip(y, k & 1), c) for k in (1, 2, 3)]
    return [(peer, _chip_of(peer[0], peer[1])) for peer in peers]


def _part_rows(ref, chip, core, part):
    quarter = ref.shape[1] // 4
    return ref.at[chip, pl.ds(pl.multiple_of((2 * core + part) * quarter, quarter), quarter), :]


def _neighbours(x, y, c):
    return [((x, 1 - y, c), _chip_of(x, 1 - y)), ((1 - x, y, c), _chip_of(1 - x, y)),
            ((1 - x, 1 - y, c), _chip_of(1 - x, 1 - y))]


def _start_direct(buf, send_sems, recv_sems):
    x, y, c = _my_place()
    me = _chip_of(x, y)
    for n, (peer, _) in enumerate(_neighbours(x, y, c)[0:2]):
        for part in ((0, 1), (1, 0))[n]:
            piece = _part_rows(buf, me, c, part)
            pltpu.make_async_remote_copy(
                src_ref=piece, dst_ref=piece, send_sem=send_sems.at[2 * n + part], recv_sem=recv_sems.at[2 * n + part],
                device_id=peer, device_id_type=MESH).start()


def _relay(buf, recv_sems, relay_send, relay_recv):
    x, y, c = _my_place()
    nbrs = _neighbours(x, y, c)
    for n in range(2):
        part = n
        piece = _part_rows(buf, nbrs[n][1], c, part)
        pltpu.make_async_remote_copy(
            src_ref=piece, dst_ref=piece, send_sem=relay_send.at[part], recv_sem=recv_sems.at[2 * n + part],
            device_id=nbrs[n][0], device_id_type=MESH).wait_recv()
        pltpu.make_async_remote_copy(
            src_ref=piece, dst_ref=piece, send_sem=relay_send.at[part], recv_sem=relay_recv.at[part],
            device_id=nbrs[1 - n][0], device_id_type=MESH).start()


def _gather_start(win_slots, after):
    def body(win_in, after_ref, win_ref, send_sems, recv_sems, token_ref):
        del win_in, after_ref
        _start_direct(win_ref, send_sems, recv_sems)
        token_ref[...] = jnp.zeros(token_ref.shape, F32)

    sems = pltpu.SemaphoreType.DMA((4,))
    return pl.pallas_call(
        body, name="gather_start",
        out_shape=(jax.ShapeDtypeStruct(win_slots.shape, win_slots.dtype), sems, sems, TOKEN),
        in_specs=[HBM, ANY], out_specs=(HBM, SEM, SEM, VMEM),
        input_output_aliases={0: 0},
        compiler_params=pltpu.CompilerParams(has_side_effects=EFFECT),
    )(win_slots, after)


def _gather_relay_in(win, wout_slots, recv_in, after):
    def body(win_in, wout_in, recv_in_ref, after_ref, win_ref, wout_ref, relay_send, relay_recv, send_out, recv_out):
        del win_in, wout_in, after_ref
        _relay(win_ref, recv_in_ref, relay_send, relay_recv)
        _start_direct(wout_ref, send_out, recv_out)

    two, four = pltpu.SemaphoreType.DMA((2,)), pltpu.SemaphoreType.DMA((4,))
    return pl.pallas_call(
        body, name="gather_relay_w_in",
        out_shape=(jax.ShapeDtypeStruct(win.shape, win.dtype), jax.ShapeDtypeStruct(wout_slots.shape, wout_slots.dtype),
                   two, two, four, four),
        in_specs=[HBM, HBM, SEM, ANY], out_specs=(HBM, HBM, SEM, SEM, SEM, SEM),
        input_output_aliases={0: 0, 1: 1},
        compiler_params=pltpu.CompilerParams(has_side_effects=EFFECT),
    )(win, wout_slots, recv_in, after)


def _gather_relay_out(wout, recv_out, after):
    def body(wout_in, recv_out_ref, after_ref, wout_ref, relay_send, relay_recv):
        del wout_in, after_ref
        _relay(wout_ref, recv_out_ref, relay_send, relay_recv)

    two = pltpu.SemaphoreType.DMA((2,))
    return pl.pallas_call(
        body, name="gather_relay_w_out",
        out_shape=(jax.ShapeDtypeStruct(wout.shape, wout.dtype), two, two),
        in_specs=[HBM, SEM, ANY], out_specs=(HBM, SEM, SEM),
        input_output_aliases={0: 0},
        compiler_params=pltpu.CompilerParams(has_side_effects=EFFECT),
    )(wout, recv_out, after)


def _gather_wait_direct(buf, send_sems, recv_sems, after, name):
    def body(buf_in, send_ref, recv_ref, after_ref, buf_ref):
        del buf_in, after_ref
        x, y, c = _my_place()
        me = _chip_of(x, y)
        for n, (peer, chip) in enumerate(_neighbours(x, y, c)[0:2]):
            second = 1 - n
            pltpu.make_async_remote_copy(
                src_ref=_part_rows(buf_ref, me, c, second), dst_ref=_part_rows(buf_ref, chip, c, second),
                send_sem=send_ref.at[2 * n + second], recv_sem=recv_ref.at[2 * n + second],
                device_id=peer, device_id_type=MESH).wait_recv()
            for part in range(2):
                piece = _part_rows(buf_ref, me, c, part)
                pltpu.make_async_remote_copy(
                    src_ref=piece, dst_ref=piece, send_sem=send_ref.at[2 * n + part], recv_sem=recv_ref.at[2 * n + part],
                    device_id=peer, device_id_type=MESH).wait_send()

    return pl.pallas_call(
        body, name=name,
        out_shape=jax.ShapeDtypeStruct(buf.shape, buf.dtype),
        in_specs=[HBM, SEM, SEM, ANY], out_specs=HBM,
        input_output_aliases={0: 0},
        compiler_params=pltpu.CompilerParams(has_side_effects=EFFECT),
    )(buf, send_sems, recv_sems, after)


def _gather_wait_relayed(buf, relay_send, relay_recv, after, name):
    def body(buf_in, rsend_ref, rrecv_ref, after_ref, buf_ref):
        del buf_in, after_ref
        x, y, c = _my_place()
        nbrs = _neighbours(x, y, c)
        for n in range(2):
            relayed = _part_rows(buf_ref, nbrs[n][1], c, n)
            cp = pltpu.make_async_remote_copy(
                src_ref=relayed, dst_ref=_part_rows(buf_ref, nbrs[2][1], c, n),
                send_sem=rsend_ref.at[n], recv_sem=rrecv_ref.at[n], device_id=nbrs[1 - n][0], device_id_type=MESH)
            cp.wait_recv()
            cp.wait_send()

    return pl.pallas_call(
        body, name=name,
        out_shape=jax.ShapeDtypeStruct(buf.shape, buf.dtype),
        in_specs=[HBM, SEM, SEM, ANY], out_specs=HBM,
        input_output_aliases={0: 0},
        compiler_params=pltpu.CompilerParams(has_side_effects=EFFECT),
    )(buf, relay_send, relay_recv, after)


def _forward_halves(buf, which, name):
    half = buf.shape[1] // 2

    def body(buf_in, buf_ref, send_sems, recv_sems):
        del buf_in
        x, y, c = _my_place()
        sibling = (x, y, 1 - c)
        chips = [_neighbours(x, y, c)[n][1] for n in which]
        started = []
        for k, src_chip in enumerate(chips):
            landed = _half_rows(buf_ref, src_chip, c, half)
            fw = pltpu.make_async_remote_copy(
                src_ref=landed, dst_ref=landed, send_sem=send_sems.at[k], recv_sem=recv_sems.at[k],
                device_id=sibling, device_id_type=MESH)
            fw.start()
            started.append(fw)
        for k, src_chip in enumerate(chips):
            other = _half_rows(buf_ref, src_chip, 1 - c, half)
            pltpu.make_async_remote_copy(
                src_ref=other, dst_ref=other, send_sem=send_sems.at[k], recv_sem=recv_sems.at[k],
                device_id=sibling, device_id_type=MESH).wait_recv()
        for fw in started:
            fw.wait_send()

    return pl.pallas_call(
        body, name=name,
        out_shape=jax.ShapeDtypeStruct(buf.shape, buf.dtype),
        in_specs=[HBM], out_specs=HBM,
        input_output_aliases={0: 0},
        scratch_shapes=[pltpu.SemaphoreType.DMA((len(which),))] * 2,
    )(buf)


def _swap_halves(g, name):
    half = g.shape[1] // 2

    def body(g_ref, r_ref, send_sem, recv_sem):
        x, y, c = _my_place()
        theirs = g_ref.at[:, pl.ds(pl.multiple_of((1 - c) * half, half), half), :]
        cp = pltpu.make_async_remote_copy(
            src_ref=theirs, dst_ref=r_ref, send_sem=send_sem, recv_sem=recv_sem,
            device_id=(x, y, 1 - c), device_id_type=MESH)
        cp.start()
        cp.wait()

    return pl.pallas_call(
        body, name=name,
        out_shape=jax.ShapeDtypeStruct((N_CHIPS, half, g.shape[2]), g.dtype),
        in_specs=[HBM], out_specs=HBM,
        scratch_shapes=[pltpu.SemaphoreType.DMA, pltpu.SemaphoreType.DMA],
    )(g)


def _owners_start(csum, name):
    land = pltpu.with_memory_space_constraint(lax.empty((N_CHIPS - 1,) + csum.shape[1:], csum.dtype), pltpu.HBM)

    def body(csum_ref, land_ref, send_sems, recv_sems, csum_thru, land_thru, token_ref):
        del csum_thru, land_thru
        x, y, c = _my_place()
        for k, (peer, owner) in enumerate(_ici_peers(x, y, c)):
            pltpu.make_async_remote_copy(
                src_ref=csum_ref.at[owner], dst_ref=land_ref.at[k], send_sem=send_sems.at[k], recv_sem=recv_sems.at[k],
                device_id=peer, device_id_type=MESH).start()
        token_ref[...] = jnp.zeros(token_ref.shape, F32)

    sems = pltpu.SemaphoreType.DMA((N_CHIPS - 1,))
    return pl.pallas_call(
        body, name=name,
        out_shape=(sems, sems, jax.ShapeDtypeStruct(csum.shape, csum.dtype),
                   jax.ShapeDtypeStruct(land.shape, land.dtype), TOKEN),
        in_specs=[HBM, HBM], out_specs=(SEM, SEM, HBM, HBM, VMEM),
        input_output_aliases={0: 2, 1: 3},
        compiler_params=pltpu.CompilerParams(has_side_effects=EFFECT),
    )(pltpu.with_memory_space_constraint(csum, pltpu.HBM), land)


def _owners_wait(send_sems, recv_sems, csum, land, after, name):
    def body(csum_ref, land_ref, send_ref, recv_ref, *rest):
        del rest
        x, y, c = _my_place()
        for k, (peer, owner) in enumerate(_ici_peers(x, y, c)):
            cp = pltpu.make_async_remote_copy(
                src_ref=csum_ref.at[owner], dst_ref=land_ref.at[k], send_sem=send_ref.at[k], recv_sem=recv_ref.at[k],
                device_id=peer, device_id_type=MESH)
            cp.wait_send()
            cp.wait_recv()

    return pl.pallas_call(
        body, name=name,
        out_shape=(jax.ShapeDtypeStruct(csum.shape, csum.dtype), jax.ShapeDtypeStruct(land.shape, land.dtype)),
        in_specs=[HBM, HBM, SEM, SEM] + [ANY] * len(after), out_specs=(HBM, HBM),
        input_output_aliases={0: 0, 1: 1},
        compiler_params=pltpu.CompilerParams(has_side_effects=EFFECT),
    )(csum, land, send_sems, recv_sems, *after)[1]


def _join_halves(full, name):
    rows = full.shape[0] // 2

    def body(full_in, full_ref, send_sem, recv_sem):
        del full_in
        x, y, c = _my_place()
        sibling = (x, y, 1 - c)
        mine = full_ref.at[pl.ds(pl.multiple_of(c * rows, rows), rows), :]
        theirs = full_ref.at[pl.ds(pl.multiple_of((1 - c) * rows, rows), rows), :]
        cp = pltpu.make_async_remote_copy(
            src_ref=mine, dst_ref=mine, send_sem=send_sem, recv_sem=recv_sem, device_id=sibling, device_id_type=MESH)
        cp.start()
        pltpu.make_async_remote_copy(
            src_ref=theirs, dst_ref=theirs, send_sem=send_sem, recv_sem=recv_sem,
            device_id=sibling, device_id_type=MESH).wait_recv()
        cp.wait_send()

    return pl.pallas_call(
        body, name=name,
        out_shape=jax.ShapeDtypeStruct(full.shape, full.dtype),
        in_specs=[HBM], out_specs=HBM,
        input_output_aliases={0: 0},
        scratch_shapes=[pltpu.SemaphoreType.DMA, pltpu.SemaphoreType.DMA],
    )(full)


def _cast_into_slot(place, w, name):
    rows, cols = w.shape
    tr = min(rows, ROW_TILE)

    def body(place_ref, w_ref, o_ref):
        del place_ref
        o_ref[...] = w_ref[...].astype(BF16)

    grid_spec = pltpu.PrefetchScalarGridSpec(
        num_scalar_prefetch=1, grid=(rows // tr,),
        in_specs=[pl.BlockSpec((tr, cols), lambda i, p: (i, 0))],
        out_specs=pl.BlockSpec((None, tr, cols), lambda i, p: (p[0], i, 0)))
    return pl.pallas_call(
        body, name=name, grid_spec=grid_spec,
        out_shape=jax.ShapeDtypeStruct((N_CHIPS, rows, cols), BF16),
        compiler_params=_params(("parallel",)),
    )(place, w)


def _ada_partial(c_all, w_ada):
    d_model, wa = w_ada.shape
    tn = 512 if wa % 512 == 0 else 256

    def body(c_ref, w_ref, o_ref):
        o_ref[...] = jnp.dot(_silu(c_ref[...]), w_ref[...], precision=lax.Precision.HIGHEST,
                             preferred_element_type=F32)

    return pl.pallas_call(
        body, name="ada_partial", grid=(wa // tn,),
        out_shape=jax.ShapeDtypeStruct((N_DEV, wa), F32),
        in_specs=[pl.BlockSpec((N_DEV, d_model), lambda i: (0, 0)), pl.BlockSpec((d_model, tn), lambda i: (0, i))],
        out_specs=pl.BlockSpec((N_DEV, tn), lambda i: (0, i)),
        compiler_params=_params(("parallel",)),
    )(c_all, w_ada)


def _prenorm(x, mod, g_pre):
    t, d = x.shape
    tb = ROW_TILE

    def body(x_ref, mod_ref, g_ref, h_ref, ht_ref):
        xv = x_ref[...]
        r = lax.rsqrt(jnp.mean(xv * xv, axis=-1, keepdims=True) + EPS)
        h = (xv * r) * g_ref[...] * (1.0 + mod_ref[1:2, :]) + mod_ref[0:1, :]
        h_ref[...] = h.astype(BF16)
        ht_ref[...] = h.T.astype(BF16)

    return pl.pallas_call(
        body, name="prenorm", grid=(t // tb,),
        out_shape=(jax.ShapeDtypeStruct((t, d), BF16), jax.ShapeDtypeStruct((d, t), BF16)),
        in_specs=[pl.BlockSpec((tb, d), lambda i: (i, 0)), pl.BlockSpec((3, d), lambda i: (0, 0)),
                  pl.BlockSpec((1, d), lambda i: (0, 0))],
        out_specs=(pl.BlockSpec((tb, d), lambda i: (i, 0)), pl.BlockSpec((d, tb), lambda i: (0, i))),
        compiler_params=_params(("parallel",)),
    )(x, mod, g_pre)


def _proj_chunk(proj, h, w, chunk, name):
    t, d = h.shape
    ws = w.shape[-1]
    tn = COL_TILE
    nt = ws // tn

    def body(chunk_ref, *refs):
        del chunk_ref
        a_ref, b_ref, o_ref = refs[-3:]
        o_ref[...] = jnp.dot(a_ref[...], b_ref[...].astype(BF16), preferred_element_type=F32).astype(BF16)

    if w.ndim == 3:
        w_spec = pl.BlockSpec((None, d, tn), lambda n, ch: (ch[0], 0, n))
    else:
        w_spec = pl.BlockSpec((d, tn), lambda n, ch: (0, n))
    first = proj is None
    grid_spec = pltpu.PrefetchScalarGridSpec(
        num_scalar_prefetch=1, grid=(nt,),
        in_specs=([] if first else [HBM]) + [pl.BlockSpec((t, d), lambda n, ch: (0, 0)), w_spec],
        out_specs=pl.BlockSpec((t, tn), lambda n, ch: (0, ch[0] * nt + n)))
    return pl.pallas_call(
        body, name=name, grid_spec=grid_spec,
        out_shape=jax.ShapeDtypeStruct((t, N_CHIPS * ws), BF16),
        input_output_aliases={} if first else {1: 0},
        compiler_params=_params(("parallel",)),
    )(*([chunk] if first else [chunk, proj]), h, w)


def _shift_rows(a, rows):
    idx = lax.broadcasted_iota(jnp.int32, a.shape, 0)
    prev = jnp.where(idx == 0, 0.0, pltpu.roll(a, 1, 0))
    nxt = jnp.where(idx == rows - 1, 0.0, pltpu.roll(a, rows - 1, 0))
    return prev, nxt


def _conv_fwd(conv_proj, conv_w, conv_b, dc):
    t = conv_proj.shape[0]
    ct = CONV_TILE
    nct = dc // ct

    def body(u_ref, cg_ref, w_ref, b_ref, co_ref):
        a = cg_ref[...].astype(F32) * u_ref[...].astype(F32)
        prev, nxt = _shift_rows(a, t)
        co_ref[...] = w_ref[0:1, :] * prev + w_ref[1:2, :] * a + w_ref[2:3, :] * nxt + b_ref[...]

    return pl.pallas_call(
        body, name="conv_fwd", grid=(nct,),
        out_shape=jax.ShapeDtypeStruct((t, dc), F32),
        in_specs=[pl.BlockSpec((t, ct), lambda i: (0, i)), pl.BlockSpec((t, ct), lambda i: (0, 2 * nct + i)),
                  pl.BlockSpec((3, ct), lambda i: (0, i)), pl.BlockSpec((1, ct), lambda i: (0, i))],
        out_specs=pl.BlockSpec((t, ct), lambda i: (0, i)),
        compiler_params=_params(("parallel",)),
    )(conv_proj, conv_proj, conv_w, conv_b)


def _to_residue_major(src_ref, dst_ref, r):
    seq = src_ref.shape[0] // r
    for res in range(r):
        dst_ref[res * seq:(res + 1) * seq, :] = src_ref[pl.ds(res, seq, stride=r), :].astype(dst_ref.dtype)


def _branch_operands(token_refs, stage, dil, r):
    if r == 1:
        return list(token_refs)
    for i, ref in enumerate(token_refs):
        stage[...] = ref[...].astype(F32)
        _to_residue_major(stage, dil.at[i], r)
    return [dil.at[i] for i in range(len(token_refs))]


def _scaled_queries(q):
    return (q.astype(F32) * (HEAD_DIM ** -0.5)).astype(BF16)


BLOCK_SHIFTS = (0, -SIDE, None)


def _band_bias(rel, slope):
    arel = jnp.abs(rel)
    return jnp.where(arel <= SIDE, arel.astype(F32) * slope, NEG_INF)


def _fill_bias_tiles(bias_ref, sl_ref, r, kw):
    base = lax.broadcasted_iota(jnp.int32, (ATT_BQ, kw), 1) - lax.broadcasted_iota(jnp.int32, (ATT_BQ, kw), 0)
    for hh in range(2):
        slope = -(sl_ref[hh:hh + 1, 0:kw] * float(r))
        for e, shift in enumerate(BLOCK_SHIFTS):
            shift = ATT_BQ - kw if shift is None else shift
            bias_ref[hh, e, :, 0:kw] = _band_bias(base + shift, slope)


def _fill_stacked_bias_tiles(bias_ref, sl_ref, r, kw):
    base = lax.broadcasted_iota(jnp.int32, (kw, ATT_BQ), 0) - lax.broadcasted_iota(jnp.int32, (kw, ATT_BQ), 1)
    for hh in range(2):
        slope = -(sl_ref[hh:hh + 1, 0:ATT_BQ] * float(r))
        for e, shift in enumerate(BLOCK_SHIFTS):
            shift = ATT_BQ - kw if shift is None else shift
            bias_ref[e, 0:kw, hh * ATT_BQ:(hh + 1) * ATT_BQ] = _band_bias(base + shift, slope)


def _first_head_lanes():
    return lax.broadcasted_iota(jnp.int32, (1, PAIR), 1) < HEAD_DIM


def _only_head(x, first, hh):
    return jnp.where(first if hh == 0 else jnp.logical_not(first), x, jnp.zeros_like(x))


def _block_place(g, seq_len, kw):
    nqb = seq_len // ATT_BQ
    if nqb == 1:
        row = pl.multiple_of(g * ATT_BQ, ATT_BQ)
        return row, row, 0
    res = g // nqb
    qb = g - res * nqb
    q0 = qb * ATT_BQ
    ks = jnp.clip(q0 - SIDE, 0, seq_len - kw)
    edge = jnp.where(qb == 0, 0, jnp.where(qb == nqb - 1, 2, 1))
    return (pl.multiple_of(res * seq_len + q0, ATT_BQ), pl.multiple_of(res * seq_len + ks, SIDE), edge)


def _qkv_specs(dc, da, t, index):
    return [pl.BlockSpec((t, PAIR), functools.partial(index, (4 * dc + comp * da) // PAIR)) for comp in range(3)]


def _attn_fwd(proj, slopes, dc, da):
    t = proj.shape[0]
    hp = da // PAIR
    n_blocks = t // ATT_BQ

    def body(q_ref, k_ref, v_ref, sl_ref, o_ref, lse_ref, stage, dil, bias, o_res, l_res, o_tok, l_tok):
        for b, (_, r) in enumerate(BRANCHES):
            seq_len = t // r
            kw = min(ATT_KW, seq_len)
            ops = _branch_operands([q_ref, k_ref, v_ref], stage, dil, r)
            _fill_bias_tiles(bias, sl_ref, r, kw)
            o_dst, l_dst = (o_tok.at[b], l_tok.at[b]) if r == 1 else (o_res, l_res)
            first = _first_head_lanes()

            def block(g, carry, seq_len=seq_len, kw=kw, o_dst=o_dst, l_dst=l_dst, first=first, ops=ops):
                qrow, krow, edge = _block_place(g, seq_len, kw)
                q = _scaled_queries(ops[0][pl.ds(qrow, ATT_BQ), :])
                k = ops[1][pl.ds(krow, kw), :]
                v = ops[2][pl.ds(krow, kw), :]
                ones = jnp.ones((kw, PAIR), BF16)
                both, tops = None, []
                for hh in range(2):
                    s = lax.dot_general(_only_head(q, first, hh), k, (((1,), (1,)), ((), ())),
                                        preferred_element_type=F32)
                    s = s + bias[hh, edge, :, 0:kw]
                    m = jnp.max(s, axis=-1, keepdims=True)
                    p = jnp.exp(s - m).astype(BF16)
                    rhs = jnp.concatenate([_only_head(v, first, hh), _only_head(ones, first, hh)], axis=1)
                    part = jnp.dot(p, rhs, preferred_element_type=F32)
                    both = part if both is None else both + part
                    tops.append(m)
                den = both[:, PAIR:]
                o_dst[pl.ds(qrow, ATT_BQ), :] = both[:, 0:PAIR] / den
                l_dst[pl.ds(qrow, ATT_BQ), :] = jnp.where(first, tops[0], tops[1]) + jnp.log(den)
                return carry

            lax.fori_loop(0, n_blocks, block, 0, unroll=ATT_UNROLL)
            if r > 1:
                for res in range(r):
                    rows = slice(res * seq_len, (res + 1) * seq_len)
                    o_tok[b, pl.ds(res, seq_len, stride=r), :] = o_res[rows, :]
                    l_tok[b, pl.ds(res, seq_len, stride=r), :] = l_res[rows, :]

        def merge(i, carry):
            rows = pl.ds(pl.multiple_of(i * ROW_TILE, ROW_TILE), ROW_TILE)
            la, lb, lc = l_tok[0, rows, :], l_tok[1, rows, :], l_tok[2, rows, :]
            m = jnp.maximum(jnp.maximum(la, lb), lc)
            wa, wb, wc = jnp.exp(la - m), jnp.exp(lb - m), jnp.exp(lc - m)
            den = wa + wb + wc
            o_ref[rows, :] = (wa * o_tok[0, rows, :] + wb * o_tok[1, rows, :] + wc * o_tok[2, rows, :]) * (1.0 / den)
            lse_ref[rows, :] = m + jnp.log(den)
            return carry

        lax.fori_loop(0, t // ROW_TILE, merge, 0)

    pair_spec = pl.BlockSpec((None, t, PAIR), lambda h: (h, 0, 0))
    return pl.pallas_call(
        body, name="attn_fwd", grid=(hp,),
        out_shape=(jax.ShapeDtypeStruct((hp, t, PAIR), F32), jax.ShapeDtypeStruct((hp, t, PAIR), F32)),
        in_specs=_qkv_specs(dc, da, t, lambda first, h: (0, first + h))
        + [pl.BlockSpec((None, 8, ATT_KW), lambda h: (h, 0, 0))],
        out_specs=(pair_spec, pair_spec),
        scratch_shapes=[pltpu.VMEM((t, PAIR), F32), pltpu.VMEM((3, t, PAIR), BF16),
                        pltpu.VMEM((2, 3, ATT_BQ, ATT_KW), F32),
                        pltpu.VMEM((t, PAIR), F32), pltpu.VMEM((t, PAIR), F32),
                        pltpu.VMEM((3, t, PAIR), F32), pltpu.VMEM((3, t, PAIR), F32)],
        compiler_params=_params(("parallel",)),
    )(proj, proj, proj, slopes)


def _attn_bwd(dproj, proj, d_o, lse, delta, slopes, dc, da, after):
    t = proj.shape[0]
    hp = da // PAIR
    n_blocks = t // ATT_BQ

    def all_branches(q_ref, k_ref, v_ref, do_ref, lse_ref, dl_ref, sl_ref,
                     stage, dil, packed, packed_res, row_vecs, bias_t, acc, tot):
        first = _first_head_lanes()
        lane = lax.broadcasted_iota(jnp.int32, (1, PAIR), 1)
        packed[...] = jnp.where((lane & (HEAD_DIM - 1)) < HEAD_DIM // 2, lse_ref[...], dl_ref[...])
        for b, (_, r) in enumerate(BRANCHES):
            seq_len = t // r
            kw = min(ATT_KW, seq_len)
            ops = _branch_operands([q_ref, k_ref, v_ref, do_ref], stage, dil, r)
            scalars = packed
            if r > 1:
                _to_residue_major(packed, packed_res, r)
                scalars = packed_res
            for g in range(n_blocks):
                flipped = scalars[g * ATT_BQ:(g + 1) * ATT_BQ, :].T
                for row in range(4):
                    row_vecs[g, row:row + 1, :] = flipped[row * (HEAD_DIM // 2):row * (HEAD_DIM // 2) + 1, :]
            _fill_stacked_bias_tiles(bias_t, sl_ref, r, kw)
            acc[1] = jnp.zeros((t, PAIR), F32)
            acc[2] = jnp.zeros((t, PAIR), F32)

            def block(g, carry, seq_len=seq_len, kw=kw, ops=ops):
                qrow, krow, edge = _block_place(g, seq_len, kw)
                nt = (((1,), (1,)), ((), ()))
                q = _scaled_queries(ops[0][pl.ds(qrow, ATT_BQ), :])
                k = ops[1][pl.ds(krow, kw), :]
                v = ops[2][pl.ds(krow, kw), :]
                dov = ops[3][pl.ds(qrow, ATT_BQ), :]
                q2 = jnp.concatenate([_only_head(q, first, 0), _only_head(q, first, 1)], axis=0)
                do2 = jnp.concatenate([_only_head(dov, first, 0), _only_head(dov, first, 1)], axis=0)
                rows = row_vecs[g]
                lse2 = jnp.concatenate([rows[0:1, :], rows[2:3, :]], axis=1)
                dl2 = jnp.concatenate([rows[1:2, :], rows[3:4, :]], axis=1)
                s_t = lax.dot_general(k, q2, nt, preferred_element_type=F32)
                p_t = jnp.exp(s_t + bias_t[edge, 0:kw, :] - lse2)
                dp_t = lax.dot_general(v, do2, nt, preferred_element_type=F32)
                ds_t = p_t * (dp_t - dl2)
                dv = jnp.dot(p_t.astype(BF16), do2, preferred_element_type=F32)
                dk = jnp.dot(ds_t.astype(BF16), q2, preferred_element_type=F32)
                ds = ds_t.T.astype(BF16)
                dq = (jnp.dot(ds[0:ATT_BQ, :], _only_head(k, first, 0), preferred_element_type=F32)
                      + jnp.dot(ds[ATT_BQ:2 * ATT_BQ, :], _only_head(k, first, 1), preferred_element_type=F32))
                acc[0, pl.ds(qrow, ATT_BQ), :] = dq * (HEAD_DIM ** -0.5)
                acc[1, pl.ds(krow, kw), :] += dk
                acc[2, pl.ds(krow, kw), :] += dv
                return carry

            lax.fori_loop(0, n_blocks, block, 0, unroll=ATT_UNROLL_BWD)
            for comp in range(3):
                if r == 1:
                    tot[comp] = acc[comp]
                else:
                    for res in range(r):
                        tok = pl.ds(res, seq_len, stride=r)
                        tot[comp, tok, :] = tot[comp, tok, :] + acc[comp, res * seq_len:(res + 1) * seq_len, :]

    def body(dproj_in, q_ref, k_ref, v_ref, do_ref, lse_ref, dl_ref, sl_ref, after_ref, out_ref, *scratch):
        del dproj_in, after_ref

        @pl.when(pl.program_id(1) == 0)
        def _():
            all_branches(q_ref, k_ref, v_ref, do_ref, lse_ref, dl_ref, sl_ref, *scratch)

        out_ref[...] = scratch[-1][pl.program_id(1)].astype(BF16)

    pair_spec = pl.BlockSpec((None, t, PAIR), lambda h, comp: (h, 0, 0))
    first_q = (4 * dc) // PAIR
    return pl.pallas_call(
        body, name="attn_bwd", grid=(hp, 3),
        out_shape=jax.ShapeDtypeStruct(dproj.shape, BF16),
        in_specs=[HBM] + _qkv_specs(dc, da, t, lambda first, h, comp: (0, first + h))
        + [pair_spec, pair_spec, pair_spec, pl.BlockSpec((None, 8, ATT_KW), lambda h, comp: (h, 0, 0)), ANY],
        out_specs=pl.BlockSpec((t, PAIR), lambda h, comp: (0, first_q + comp * hp + h)),
        input_output_aliases={0: 0},
        scratch_shapes=[pltpu.VMEM((t, PAIR), F32), pltpu.VMEM((4, t, PAIR), BF16),
                        pltpu.VMEM((t, PAIR), F32), pltpu.VMEM((t, PAIR), F32),
                        pltpu.VMEM((n_blocks, 8, ATT_BQ), F32), pltpu.VMEM((3, ATT_KW, 2 * ATT_BQ), F32),
                        pltpu.VMEM((3, t, PAIR), F32), pltpu.VMEM((3, t, PAIR), F32)],
        compiler_params=_params(("parallel", "arbitrary")),
    )(dproj, proj, proj, proj, d_o, lse, delta, slopes, after)


def _mix_fwd(co, proj, o_mix, g_conv, g_attn_pairs):
    t, dc = co.shape
    hp = o_mix.shape[0]
    da = hp * PAIR
    tb = ROW_TILE

    def body(co_ref, bg_ref, zc_ref, za_ref, om_ref, gc_ref, ga_ref, ycat_ref, ycatt_ref):
        p = bg_ref[...].astype(F32) * co_ref[...]
        rc = lax.rsqrt(jnp.mean(p * p, axis=-1, keepdims=True) + EPS)
        yc = (p * rc) * gc_ref[...] * _silu(zc_ref[...].astype(F32))
        ycat_ref[:, 0:dc] = yc.astype(BF16)
        ycatt_ref[0:dc, :] = yc.T.astype(BF16)
        ssq = jnp.zeros((tb, 1), F32)
        for h in range(hp):
            o = om_ref[h]
            ssq = ssq + jnp.sum(o * o, axis=-1, keepdims=True)
        ra = lax.rsqrt(ssq * (1.0 / da) + EPS)
        for h in range(hp):
            ya = (om_ref[h] * ra) * ga_ref[h] * _silu(za_ref[:, h * PAIR:(h + 1) * PAIR].astype(F32))
            ycat_ref[:, dc + h * PAIR:dc + (h + 1) * PAIR] = ya.astype(BF16)
            ycatt_ref[dc + h * PAIR:dc + (h + 1) * PAIR, :] = ya.T.astype(BF16)

    pair_spec = pl.BlockSpec((hp, tb, PAIR), lambda i: (0, i, 0))
    return pl.pallas_call(
        body, name="mix_fwd", grid=(t // tb,),
        out_shape=(jax.ShapeDtypeStruct((t, dc + da), BF16), jax.ShapeDtypeStruct((dc + da, t), BF16)),
        in_specs=[pl.BlockSpec((tb, dc), lambda i: (i, 0)),
                  pl.BlockSpec((tb, dc), lambda i: (i, 1)),
                  pl.BlockSpec((tb, dc), lambda i: (i, 3)),
                  pl.BlockSpec((tb, da), lambda i: (i, 7)),
                  pair_spec,
                  pl.BlockSpec((1, dc), lambda i: (0, 0)),
                  pl.BlockSpec((hp, 1, PAIR), lambda i: (0, 0, 0))],
        out_specs=(pl.BlockSpec((tb, dc + da), lambda i: (i, 0)), pl.BlockSpec((dc + da, tb), lambda i: (0, i))),
        compiler_params=_params(("parallel",)),
    )(co, proj, proj, proj, o_mix, g_conv, g_attn_pairs)


def _out_fwd_bwd(ycat, woutf, x, target, mod, g_post):
    t, d = x.shape
    n = ycat.shape[1]
    tb = ROW_TILE

    def body(a_ref, w_ref, x_ref, tg_ref, mod_ref, g_ref, dout_ref, dy_ref, acc_ref):
        y = jnp.dot(a_ref[...], w_ref[...], preferred_element_type=F32)
        r = lax.rsqrt(jnp.mean(y * y, axis=-1, keepdims=True) + EPS)
        nh = y * r
        gate = mod_ref[2:3, :]
        nrm = nh * g_ref[...]
        err = x_ref[...] + gate * nrm - tg_ref[...]
        dout = err * (1.0 / d)
        dout_ref[...] = dout
        dn = dout * gate
        a = dn * g_ref[...]
        dy = r * (a - nh * jnp.mean(a * nh, axis=-1, keepdims=True))
        dy_ref[...] = dy.astype(BF16)
        loss = 0.5 * jnp.sum(jnp.sum(err * err, axis=-1, keepdims=True) * (1.0 / d), axis=0, keepdims=True)
        part = jnp.concatenate(
            [jnp.sum(dout * nrm, axis=0, keepdims=True), jnp.sum(dn * nh, axis=0, keepdims=True),
             jnp.broadcast_to(loss, (1, d)), jnp.zeros((5, d), F32)], axis=0)

        @pl.when(pl.program_id(0) == 0)
        def _():
            acc_ref[...] = jnp.zeros(acc_ref.shape, F32)

        acc_ref[...] += part

    return pl.pallas_call(
        body, name="out_fwd_bwd", grid=(t // tb,),
        out_shape=(jax.ShapeDtypeStruct((t, d), F32), jax.ShapeDtypeStruct((t, d), BF16),
                   jax.ShapeDtypeStruct((8, d), F32)),
        in_specs=[pl.BlockSpec((tb, n), lambda i: (i, 0)), pl.BlockSpec((n, d), lambda i: (0, 0)),
                  pl.BlockSpec((tb, d), lambda i: (i, 0)), pl.BlockSpec((tb, d), lambda i: (i, 0)),
                  pl.BlockSpec((3, d), lambda i: (0, 0)), pl.BlockSpec((1, d), lambda i: (0, 0))],
        out_specs=(pl.BlockSpec((tb, d), lambda i: (i, 0)), pl.BlockSpec((tb, d), lambda i: (i, 0)),
                   pl.BlockSpec((8, d), lambda i: (0, 0))),
        compiler_params=_params(("arbitrary",)),
    )(ycat, woutf, x, target, mod, g_post)


def _matmul_nt(a, b, out_dtype, name):
    m, k = a.shape
    n = b.shape[0]
    tn = COL_TILE

    def body(a_ref, b_ref, o_ref):
        o_ref[...] = lax.dot_general(a_ref[...], b_ref[...], (((1,), (1,)), ((), ())),
                                     preferred_element_type=F32).astype(out_dtype)

    return pl.pallas_call(
        body, name=name, grid=(n // tn,),
        out_shape=jax.ShapeDtypeStruct((m, n), out_dtype),
        in_specs=[pl.BlockSpec((m, k), lambda i: (0, 0)), pl.BlockSpec((tn, k), lambda i: (i, 0))],
        out_specs=pl.BlockSpec((m, tn), lambda i: (0, i)),
        compiler_params=_params(("parallel",)),
    )(a, b)


def _matmul_nn(a, b, out_dtype, name):
    m, k = a.shape
    n = b.shape[1]
    tn = COL_TILE

    def body(a_ref, b_ref, o_ref):
        o_ref[...] = jnp.dot(a_ref[...], b_ref[...], preferred_element_type=F32).astype(out_dtype)

    return pl.pallas_call(
        body, name=name, grid=(n // tn,),
        out_shape=jax.ShapeDtypeStruct((m, n), out_dtype),
        in_specs=[pl.BlockSpec((m, k), lambda i: (0, 0)), pl.BlockSpec((k, tn), lambda i: (0, i))],
        out_specs=pl.BlockSpec((m, tn), lambda i: (0, i)),
        compiler_params=_params(("parallel",)),
    )(a, b)


def _mix_bwd(dycat, co, proj, o_mix, g_conv, g_attn_pairs):
    t, dc = co.shape
    hp = o_mix.shape[0]
    da = hp * PAIR
    tb = ROW_TILE

    def body(dy_ref, co_ref, bg_ref, zc_ref, za_ref, om_ref, gc_ref, ga_ref,
             dcp_ref, dco_ref, do_ref, dl_ref, dgc_ref, dga_ref):
        first = pl.program_id(0) == 0
        cov = co_ref[...]
        bg = bg_ref[...].astype(F32)
        zc = zc_ref[...].astype(F32)
        p = bg * cov
        rc = lax.rsqrt(jnp.mean(p * p, axis=-1, keepdims=True) + EPS)
        nh = p * rc
        dyc = dy_ref[:, 0:dc].astype(F32)
        dn = dyc * _silu(zc)
        a = dn * gc_ref[...]
        dp = rc * (a - nh * jnp.mean(a * nh, axis=-1, keepdims=True))
        dcp_ref[:, 0:dc] = jnp.zeros((tb, dc), BF16)
        dcp_ref[:, dc:2 * dc] = (dp * cov).astype(BF16)
        dcp_ref[:, 2 * dc:3 * dc] = jnp.zeros((tb, dc), BF16)
        dcp_ref[:, 3 * dc:4 * dc] = (dyc * nh * gc_ref[...] * _silu_grad(zc)).astype(BF16)
        dcp_ref[:, 4 * dc:4 * dc + 3 * da] = jnp.zeros((tb, 3 * da), BF16)
        dco_ref[...] = dp * bg

        @pl.when(first)
        def _():
            dgc_ref[...] = jnp.zeros(dgc_ref.shape, F32)
            dga_ref[...] = jnp.zeros(dga_ref.shape, F32)

        dgc_ref[...] += jnp.sum(dn * nh, axis=0, keepdims=True)

        ssq = jnp.zeros((tb, 1), F32)
        for h in range(hp):
            o = om_ref[h]
            ssq = ssq + jnp.sum(o * o, axis=-1, keepdims=True)
        ra = lax.rsqrt(ssq * (1.0 / da) + EPS)
        dot_an = jnp.zeros((tb, 1), F32)
        for h in range(hp):
            nha = om_ref[h] * ra
            za = za_ref[:, h * PAIR:(h + 1) * PAIR].astype(F32)
            dya = dy_ref[:, dc + h * PAIR:dc + (h + 1) * PAIR].astype(F32)
            dna = dya * _silu(za)
            dza = (dya * nha * ga_ref[h] * _silu_grad(za)).astype(BF16)
            dcp_ref[:, 4 * dc + 3 * da + h * PAIR:4 * dc + 3 * da + (h + 1) * PAIR] = dza
            dga_ref[h] += jnp.sum(dna * nha, axis=0, keepdims=True)
            dot_an = dot_an + jnp.sum(dna * ga_ref[h] * nha, axis=-1, keepdims=True)
        mean_an = dot_an * (1.0 / da)
        first_head = lax.broadcasted_iota(jnp.int32, (tb, PAIR), 1) < HEAD_DIM
        for h in range(hp):
            o = om_ref[h]
            nha = o * ra
            za = za_ref[:, h * PAIR:(h + 1) * PAIR].astype(F32)
            dya = dy_ref[:, dc + h * PAIR:dc + (h + 1) * PAIR].astype(F32)
            aa = dya * _silu(za) * ga_ref[h]
            d_o = ra * (aa - nha * mean_an)
            do_ref[h] = d_o.astype(BF16)
            prod = d_o * o
            both = jnp.sum(prod, axis=-1, keepdims=True)
            head0 = jnp.sum(jnp.where(first_head, prod, 0.0), axis=-1, keepdims=True)
            dl_ref[h] = jnp.where(first_head, head0, both - head0)

    pair_spec = pl.BlockSpec((hp, tb, PAIR), lambda i: (0, i, 0))
    return pl.pallas_call(
        body, name="mix_bwd", grid=(t // tb,),
        out_shape=(jax.ShapeDtypeStruct((t, 4 * dc + 4 * da), BF16), jax.ShapeDtypeStruct((t, dc), F32),
                   jax.ShapeDtypeStruct((hp, t, PAIR), BF16), jax.ShapeDtypeStruct((hp, t, PAIR), F32),
                   jax.ShapeDtypeStruct((1, dc), F32), jax.ShapeDtypeStruct((hp, 1, PAIR), F32)),
        in_specs=[pl.BlockSpec((tb, dc + da), lambda i: (i, 0)),
                  pl.BlockSpec((tb, dc), lambda i: (i, 0)),
                  pl.BlockSpec((tb, dc), lambda i: (i, 1)),
                  pl.BlockSpec((tb, dc), lambda i: (i, 3)),
                  pl.BlockSpec((tb, da), lambda i: (i, 7)),
                  pair_spec,
                  pl.BlockSpec((1, dc), lambda i: (0, 0)),
                  pl.BlockSpec((hp, 1, PAIR), lambda i: (0, 0, 0))],
        out_specs=(pl.BlockSpec((tb, 4 * dc + 4 * da), lambda i: (i, 0)), pl.BlockSpec((tb, dc), lambda i: (i, 0)),
                   pair_spec, pair_spec,
                   pl.BlockSpec((1, dc), lambda i: (0, 0)), pl.BlockSpec((hp, 1, PAIR), lambda i: (0, 0, 0))),
        compiler_params=_params(("arbitrary",)),
    )(dycat, co, proj, proj, proj, o_mix, g_conv, g_attn_pairs)


def _conv_bwd(dconv_proj, dco, conv_proj, conv_w, dc, after):
    t = dco.shape[0]
    ct = CONV_TILE
    nct = dc // ct

    def body(dcp_in_ref, dco_ref, u_ref, cg_ref, w_ref, after_ref, dcp_ref, acc_ref):
        del dcp_in_ref, after_ref
        which = pl.program_id(1)
        g = dco_ref[...]
        u = u_ref[...].astype(F32)
        cg = cg_ref[...].astype(F32)
        g_prev, g_next = _shift_rows(g, t)
        da = w_ref[0:1, :] * g_next + w_ref[1:2, :] * g + w_ref[2:3, :] * g_prev
        dcp_ref[...] = (da * jnp.where(which == 0, cg, u)).astype(BF16)
        a = cg * u
        a_prev, a_next = _shift_rows(a, t)
        acc_ref[...] = jnp.concatenate(
            [jnp.sum(g * a_prev, axis=0, keepdims=True), jnp.sum(g * a, axis=0, keepdims=True),
             jnp.sum(g * a_next, axis=0, keepdims=True), jnp.sum(g, axis=0, keepdims=True),
             jnp.zeros((4, ct), F32)], axis=0)

    return pl.pallas_call(
        body, name="conv_bwd", grid=(nct, 2),
        out_shape=(jax.ShapeDtypeStruct(dconv_proj.shape, BF16), jax.ShapeDtypeStruct((8, dc), F32)),
        in_specs=[HBM,
                  pl.BlockSpec((t, ct), lambda i, s: (0, i)),
                  pl.BlockSpec((t, ct), lambda i, s: (0, i)),
                  pl.BlockSpec((t, ct), lambda i, s: (0, 2 * nct + i)),
                  pl.BlockSpec((3, ct), lambda i, s: (0, i)), ANY],
        out_specs=(pl.BlockSpec((t, ct), lambda i, s: (0, 2 * s * nct + i)),
                   pl.BlockSpec((8, ct), lambda i, s: (0, i))),
        input_output_aliases={0: 0},
        compiler_params=_params(("arbitrary", "arbitrary")),
    )(dconv_proj, dco, conv_proj, conv_proj, conv_w, after)


def _dw_in(ht, dproj):
    d, t = ht.shape
    ws = dproj.shape[1] // N_CHIPS
    tn = COL_TILE
    nt = ws // tn

    def body(a_ref, b_ref, o_ref):
        o_ref[...] = jnp.dot(a_ref[...], b_ref[...], preferred_element_type=F32).astype(BF16)

    return pl.pallas_call(
        body, name="dw_in", grid=(N_CHIPS, nt),
        out_shape=jax.ShapeDtypeStruct((N_CHIPS, d, ws), BF16),
        in_specs=[pl.BlockSpec((d, t), lambda j, n: (0, 0)), pl.BlockSpec((t, tn), lambda j, n: (0, j * nt + n))],
        out_specs=pl.BlockSpec((None, d, tn), lambda j, n: (j, 0, n)),
        compiler_params=_params(("parallel", "parallel")),
    )(ht, dproj)


def _dh(dproj, winf, after):
    t = dproj.shape[0]
    _, d, ws = winf.shape
    tm = tn = COL_TILE
    nt = (((1,), (1,)), ((), ()))

    def body(a_ref, w_ref, after_ref, o_ref):
        del after_ref
        acc = lax.dot_general(a_ref[:, 0:ws], w_ref[0], nt, preferred_element_type=F32)
        for j in range(1, N_CHIPS):
            acc = acc + lax.dot_general(a_ref[:, j * ws:(j + 1) * ws], w_ref[j], nt, preferred_element_type=F32)
        o_ref[...] = acc

    return pl.pallas_call(
        body, name="dh", grid=(d // tn, t // tm),
        out_shape=jax.ShapeDtypeStruct((t, d), F32),
        in_specs=[pl.BlockSpec((tm, N_CHIPS * ws), lambda n, m: (m, 0)),
                  pl.BlockSpec((N_CHIPS, tn, ws), lambda n, m: (0, n, 0)), ANY],
        out_specs=pl.BlockSpec((tm, tn), lambda n, m: (m, n)),
        compiler_params=_params(("parallel", "parallel")),
    )(dproj, winf, after)


def _prenorm_bwd(x, dh, dout, mod, g_pre):
    t, d = x.shape
    tb = ROW_TILE

    def body(x_ref, dh_ref, dout_ref, mod_ref, g_ref, gx_ref, acc_ref):
        xv = x_ref[...]
        dhv = dh_ref[...]
        r = lax.rsqrt(jnp.mean(xv * xv, axis=-1, keepdims=True) + EPS)
        xh = xv * r
        one_scale = 1.0 + mod_ref[1:2, :]
        a = dhv * one_scale * g_ref[...]
        gx_ref[...] = dout_ref[...] + r * (a - xh * jnp.mean(a * xh, axis=-1, keepdims=True))
        part = jnp.concatenate(
            [jnp.sum(dhv, axis=0, keepdims=True), jnp.sum(dhv * xh * g_ref[...], axis=0, keepdims=True),
             jnp.sum(dhv * xh * one_scale, axis=0, keepdims=True), jnp.zeros((5, d), F32)], axis=0)

        @pl.when(pl.program_id(0) == 0)
        def _():
            acc_ref[...] = jnp.zeros(acc_ref.shape, F32)

        acc_ref[...] += part

    return pl.pallas_call(
        body, name="prenorm_bwd", grid=(t // tb,),
        out_shape=(jax.ShapeDtypeStruct((t, d), F32), jax.ShapeDtypeStruct((8, d), F32)),
        in_specs=[pl.BlockSpec((tb, d), lambda i: (i, 0)), pl.BlockSpec((tb, d), lambda i: (i, 0)),
                  pl.BlockSpec((tb, d), lambda i: (i, 0)), pl.BlockSpec((3, d), lambda i: (0, 0)),
                  pl.BlockSpec((1, d), lambda i: (0, 0))],
        out_specs=(pl.BlockSpec((tb, d), lambda i: (i, 0)), pl.BlockSpec((8, d), lambda i: (0, 0))),
        compiler_params=_params(("arbitrary",)),
    )(x, dh, dout, mod, g_pre)


def _chip_sums(place, g, rsib, name):
    _, rows, cols = g.shape
    half = rows // 2
    tr = min(half, ROW_TILE)
    nt = half // tr

    def body(place_ref, g_ref, r_ref, o_ref):
        del place_ref
        o_ref[...] = (g_ref[...].astype(F32) + r_ref[...].astype(F32)).astype(BF16)

    grid_spec = pltpu.PrefetchScalarGridSpec(
        num_scalar_prefetch=1, grid=(N_CHIPS, nt),
        in_specs=[pl.BlockSpec((None, tr, cols), lambda j, i, p: (j, p[1] * nt + i, 0)),
                  pl.BlockSpec((None, tr, cols), lambda j, i, p: (j, i, 0))],
        out_specs=pl.BlockSpec((None, tr, cols), lambda j, i, p: (j, i, 0)))
    return pl.pallas_call(
        body, name=name, grid_spec=grid_spec,
        out_shape=jax.ShapeDtypeStruct((N_CHIPS, half, cols), BF16),
        compiler_params=_params(("parallel", "parallel")),
    )(place, g, rsib)


def _owner_sum(place, g, rsib, rici, name):
    _, rows, cols = g.shape
    half = rows // 2
    tr = min(half, ROW_TILE)
    nt = half // tr

    def body(place_ref, g_ref, r_ref, i_ref, o_ref):
        del place_ref
        acc = g_ref[...].astype(F32) + r_ref[...].astype(F32)
        for k in range(N_CHIPS - 1):
            acc = acc + i_ref[k].astype(F32)
        o_ref[...] = acc

    grid_spec = pltpu.PrefetchScalarGridSpec(
        num_scalar_prefetch=1, grid=(nt,),
        in_specs=[pl.BlockSpec((None, tr, cols), lambda i, p: (p[0], p[1] * nt + i, 0)),
                  pl.BlockSpec((None, tr, cols), lambda i, p: (p[0], i, 0)),
                  pl.BlockSpec((N_CHIPS - 1, tr, cols), lambda i, p: (0, i, 0))],
        out_specs=pl.BlockSpec((tr, cols), lambda i, p: (p[1] * nt + i, 0)))
    return pl.pallas_call(
        body, name=name, grid_spec=grid_spec,
        out_shape=jax.ShapeDtypeStruct((rows, cols), F32),
        compiler_params=_params(("parallel",)),
    )(place, g, rsib, rici)


def _adam_math(w, g, m, v):
    m2 = ADAM_B1 * m + (1.0 - ADAM_B1) * g
    v2 = ADAM_B2 * v + (1.0 - ADAM_B2) * (g * g)
    m_hat = m2 / (1.0 - ADAM_B1 ** ADAM_STEP)
    v_hat = v2 / (1.0 - ADAM_B2 ** ADAM_STEP)
    delta = -ADAM_LR * (m_hat / (jnp.sqrt(v_hat) + ADAM_EPS) + ADAM_WD * w)
    return delta, m2, v2


def _adamw(w, g, m, v, name):
    rows, cols = w.shape
    tr = min(rows, ROW_TILE)

    def body(w_ref, g_ref, m_ref, v_ref, go_ref, d_ref, m2_ref, v2_ref):
        g = g_ref[...]
        go_ref[...] = g
        d_ref[...], m2_ref[...], v2_ref[...] = _adam_math(w_ref[...], g, m_ref[...], v_ref[...])

    spec = pl.BlockSpec((tr, cols), lambda i: (i, 0))
    return pl.pallas_call(
        body, name=name, grid=(rows // tr,),
        out_shape=(jax.ShapeDtypeStruct(w.shape, F32),) * 4,
        in_specs=[spec] * 4, out_specs=(spec,) * 4,
        compiler_params=_params(("parallel",)),
    )(w, g, m, v)


def _ada_grad_adamw(c_all_t, dmod_cols, w, m, v):
    d, wa = w.shape
    tr = ROW_TILE

    def body(ct_ref, dm_ref, w_ref, m_ref, v_ref, g_ref, d_ref, m2_ref, v2_ref):
        act = _silu(ct_ref[...])
        g = act[:, 0:1] * dm_ref[0:1, :]
        for b in range(1, N_DEV):
            g = g + act[:, b:b + 1] * dm_ref[b:b + 1, :]
        g_ref[...] = g
        d_ref[...], m2_ref[...], v2_ref[...] = _adam_math(w_ref[...], g, m_ref[...], v_ref[...])

    spec = pl.BlockSpec((tr, wa), lambda i: (i, 0))
    return pl.pallas_call(
        body, name="ada_grad_adamw", grid=(d // tr,),
        out_shape=(jax.ShapeDtypeStruct(w.shape, F32),) * 4,
        in_specs=[pl.BlockSpec((tr, N_DEV), lambda i: (i, 0)), pl.BlockSpec((N_DEV, wa), lambda i: (0, 0)),
                  spec, spec, spec],
        out_specs=(spec,) * 4,
        compiler_params=_params(("parallel",)),
    )(c_all_t, dmod_cols, w, m, v)


def _sum_devices(gathered):
    n = gathered.shape[1]

    def body(g_ref, o_ref):
        acc = g_ref[0:8, :]
        for dev in range(1, N_DEV):
            acc = acc + g_ref[8 * dev:8 * dev + 8, :]
        o_ref[...] = acc

    return pl.pallas_call(
        body, name="sum_devices",
        out_shape=jax.ShapeDtypeStruct((8, n), F32),
        in_specs=[VMEM], out_specs=VMEM,
    )(gathered)


def _pack_small(pieces):
    flat = [p.reshape(-1).astype(F32) for p in pieces]
    offsets, total = [], 0
    for p in flat:
        offsets.append(total)
        total += p.shape[0]
    padded = -(-total // SMALL_ALIGN) * SMALL_ALIGN
    if padded > total:
        flat.append(jnp.zeros((padded - total,), F32))
    return jnp.concatenate(flat).reshape(8, padded // 8), offsets


def _alibi_slope_rows(n_heads):
    slopes = 2.0 ** (-8.0 * jnp.arange(1, n_heads + 1, dtype=F32) / n_heads)
    rows = jnp.zeros((n_heads // 2, 8), F32).at[:, 0:2].set(slopes.reshape(n_heads // 2, 2))
    return jnp.broadcast_to(rows[:, :, None], (n_heads // 2, 8, ATT_KW))


def kernel(x, c, w_ada, b_ada, g_pre, w_in, conv_w, conv_b, g_conv, g_attn, w_out, g_post, loss_target, m_w_ada, m_b_ada, m_g_pre, m_w_in, m_conv_w, m_conv_b, m_g_conv, m_g_attn, m_w_out, m_g_post, v_w_ada, v_b_ada, v_g_pre, v_w_in, v_conv_w, v_conv_b, v_g_conv, v_g_attn, v_w_out, v_g_post):
    t, d = x.shape[1], x.shape[2]
    dc = conv_b.shape[1]
    da = g_attn.shape[1]
    hp = da // PAIR
    ws = w_in.shape[2]
    wa = w_ada.shape[2]
    cws = conv_w.shape[2]
    assert t % ROW_TILE == 0 and d % ROW_TILE == 0 and dc % COL_TILE == 0 and da % COL_TILE == 0
    assert ws == 2 * dc and dc == da and t // BRANCHES[-1][1] >= ATT_BQ

    mx, my, mc = _my_place()
    chip = _chip_of(mx, my)
    dev = 2 * chip + mc
    place = jnp.stack([chip, mc]).astype(jnp.int32)

    x2, tgt2 = x[0], loss_target[0]
    w_ada2, w_in2, w_out2 = w_ada[0], w_in[0], w_out[0]

    packed, offs = _pack_small([c[0], conv_w[0]])
    seen = _allgather8(packed, "gather_inputs").reshape(N_DEV, -1)
    c_all = seen[:, offs[0]:offs[0] + d]
    conv_w_full = seen[0::2, offs[1]:offs[1] + 3 * cws].reshape(N_CHIPS, 3, cws).transpose(1, 0, 2).reshape(3, dc)

    ada_part = _ada_partial(c_all, w_ada2)
    ada_seen = _allgather8(ada_part, "gather_ada").reshape(N_DEV, N_DEV, wa)
    mod_flat = lax.dynamic_index_in_dim(ada_seen[0::2], dev, axis=1, keepdims=False).reshape(1, 3 * d) + b_ada
    mod = mod_flat.reshape(3, d)

    win_flight, send_in, recv_in, started = _gather_start(_cast_into_slot(place, w_in2, "cast_w_in"), mod)

    y_chip, x_chip, d_chip = (_chip_of(mx, 1 - my), _chip_of(1 - mx, my), _chip_of(1 - mx, 1 - my))
    chunk_ids = [jnp.reshape(j, (1,)).astype(jnp.int32) for j in (chip, y_chip, x_chip, d_chip)]
    h, ht = _prenorm(x2, mod + started[0, 0], g_pre)
    proj = _proj_chunk(None, h, w_in2, chunk_ids[0], "proj_own")
    win_flight, wout_flight, relay_send_in, relay_recv_in, send_out, recv_out = _gather_relay_in(
        win_flight, _cast_into_slot(place, w_out2, "cast_w_out"), recv_in, proj)
    win_flight = _forward_halves(
        _gather_wait_direct(win_flight, send_in, recv_in, proj, "gather_wait_w_in_direct"), (0, 1), "forward_w_in_direct")
    proj = _proj_chunk(proj, h, win_flight, chunk_ids[1], "proj_y")
    proj = _proj_chunk(proj, h, win_flight, chunk_ids[2], "proj_x")
    winf = _forward_halves(
        _gather_wait_relayed(win_flight, relay_send_in, relay_recv_in, proj, "gather_wait_w_in_relayed"),
        (2,), "forward_w_in_relayed")
    proj = _proj_chunk(proj, h, winf, chunk_ids[3], "proj_diagonal")
    slopes = _alibi_slope_rows(da // HEAD_DIM)
    co = _conv_fwd(proj, conv_w_full, conv_b, dc)
    wout_flight, relay_send_out, relay_recv_out = _gather_relay_out(wout_flight, recv_out, co)
    o_mix, lse = _attn_fwd(proj, slopes, dc, da)
    g_attn_pairs = g_attn.reshape(hp, 1, PAIR)
    ycat, ycat_t = _mix_fwd(co, proj, o_mix, g_conv, g_attn_pairs)
    wout_flight = _gather_wait_direct(wout_flight, send_out, recv_out, ycat, "gather_wait_w_out_direct")
    wout_flight = _gather_wait_relayed(wout_flight, relay_send_out, relay_recv_out, ycat, "gather_wait_w_out_relayed")
    woutf = _forward_halves(wout_flight, (0, 1, 2), "forward_w_out").reshape(dc + da, d)
    dout, dy, post_sums = _out_fwd_bwd(ycat, woutf, x2, tgt2, mod, g_post)

    gout = _matmul_nn(ycat_t, dy, BF16, "dw_out").reshape(N_CHIPS, (dc + da) // N_CHIPS, d)
    rsib_out = _swap_halves(gout, "rs_swap_halves_out")
    csum_out = _chip_sums(place, gout, rsib_out, "rs_chip_sum_out")
    ssem_out, rsem_out, csum_out, land_out, sent_out = _owners_start(csum_out, "rs_owners_start_out")
    dycat = _matmul_nt(dy, woutf, BF16, "dycat")
    dproj, dco, d_o, delta, dg_conv, dg_attn = _mix_bwd(dycat, co, proj, o_mix, g_conv, g_attn_pairs)
    dproj, conv_sums = _conv_bwd(dproj, dco, proj, conv_w_full, dc, sent_out)
    dproj = _attn_bwd(dproj, proj, d_o, lse, delta, slopes, dc, da, sent_out)
    gin = _dw_in(ht, dproj)
    rsib_in = _swap_halves(gin, "rs_swap_halves_in")
    csum_in = _chip_sums(place, gin, rsib_in, "rs_chip_sum_in")
    ssem_in, rsem_in, csum_in, land_in, sent_in = _owners_start(csum_in, "rs_owners_start_in")
    dh = _dh(dproj, winf, sent_in)
    grad_x, pre_sums = _prenorm_bwd(x2, dh, dout, mod, g_pre)

    rici_out = _owners_wait(ssem_out, rsem_out, csum_out, land_out, [grad_x], "rs_owners_wait_out")
    grad_w_out = _join_halves(_owner_sum(place, gout, rsib_out, rici_out, "rs_owner_sum_out"), "rs_join_halves_out")

    small, so = _pack_small([
        pre_sums[0], pre_sums[1], post_sums[0],
        pre_sums[2], conv_sums[0:3], conv_sums[3], dg_conv, dg_attn, post_sums[1], post_sums[2, 0:128]])
    small_seen = _allgather8(small, "gather_small")
    total = _sum_devices(small_seen).reshape(-1)
    dmod_all = small_seen.reshape(N_DEV, -1)[:, 0:3 * d]
    loss = total[so[9]]
    grad_b_ada = total[0:3 * d].reshape(1, 3 * d)
    grad_g_pre = total[so[3]:so[3] + d].reshape(1, d)
    grad_conv_w_full = total[so[4]:so[4] + 3 * dc].reshape(3, dc)
    grad_conv_w = lax.dynamic_slice_in_dim(grad_conv_w_full, chip * cws, cws, axis=1).reshape(1, 3, cws)
    grad_conv_b = total[so[5]:so[5] + dc].reshape(1, dc)
    grad_g_conv = total[so[6]:so[6] + dc].reshape(1, dc)
    grad_g_attn = total[so[7]:so[7] + da].reshape(1, da)
    grad_g_post = total[so[8]:so[8] + d].reshape(1, d)

    dmod_cols = lax.dynamic_slice_in_dim(dmod_all, chip * wa, wa, axis=1)
    grad_w_ada, delta_w_ada, new_m_w_ada, new_v_w_ada = _ada_grad_adamw(c_all.T, dmod_cols, w_ada2, m_w_ada[0], v_w_ada[0])
    grad_w_out, delta_w_out, new_m_w_out, new_v_w_out = _adamw(
        w_out2, grad_w_out, m_w_out[0], v_w_out[0], "adamw_w_out")

    small_w = [b_ada, g_pre, conv_w, conv_b, g_conv, g_attn, g_post]
    small_g = [grad_b_ada, grad_g_pre, grad_conv_w, grad_conv_b, grad_g_conv, grad_g_attn, grad_g_post]
    small_m = [m_b_ada, m_g_pre, m_conv_w, m_conv_b, m_g_conv, m_g_attn, m_g_post]
    small_v = [v_b_ada, v_g_pre, v_conv_w, v_conv_b, v_g_conv, v_g_attn, v_g_post]
    pw, po = _pack_small(small_w)
    pg, _ = _pack_small(small_g)
    pm, _ = _pack_small(small_m)
    pv, _ = _pack_small(small_v)
    sd, sm, sv = (a.reshape(-1) for a in _adamw(pw, pg, pm, pv, "adamw_small")[1:])

    def unpack(flat):
        return [flat[o:o + w.size].reshape(w.shape) for o, w in zip(po, small_w)]

    d_small, m_small, v_small = unpack(sd), unpack(sm), unpack(sv)

    rici_in = _owners_wait(ssem_in, rsem_in, csum_in, land_in, [sd, delta_w_out, delta_w_ada], "rs_owners_wait_in")
    grad_w_in = _join_halves(_owner_sum(place, gin, rsib_in, rici_in, "rs_owner_sum_in"), "rs_join_halves_in")
    grad_w_in, delta_w_in, new_m_w_in, new_v_w_in = _adamw(w_in2, grad_w_in, m_w_in[0], v_w_in[0], "adamw_w_in")

    def lead(a):
        return a.reshape((1,) + a.shape)

    grads = [lead(grad_w_ada), grad_b_ada, grad_g_pre, lead(grad_w_in), grad_conv_w, grad_conv_b, grad_g_conv,
             grad_g_attn, lead(grad_w_out), grad_g_post]
    deltas = [lead(delta_w_ada), d_small[0], d_small[1], lead(delta_w_in), d_small[2], d_small[3], d_small[4],
              d_small[5], lead(delta_w_out), d_small[6]]
    new_ms = [lead(new_m_w_ada), m_small[0], m_small[1], lead(new_m_w_in), m_small[2], m_small[3], m_small[4],
              m_small[5], lead(new_m_w_out), m_small[6]]
    new_vs = [lead(new_v_w_ada), v_small[0], v_small[1], lead(new_v_w_in), v_small[2], v_small[3], v_small[4],
              v_small[5], lead(new_v_w_out), v_small[6]]
    return (loss, lead(grad_x), *grads, *deltas, *new_ms, *new_vs)
```

```python
import functools

import jax
import jax.numpy as jnp
from jax import lax
from jax.experimental import pallas as pl
from jax.experimental.pallas import tpu as pltpu

F32 = jnp.float32
BF16 = jnp.bfloat16
MESH = pl.DeviceIdType.MESH
HBM = pl.BlockSpec(memory_space=pltpu.HBM)
VMEM = pl.BlockSpec(memory_space=pltpu.VMEM)
ANY = pl.BlockSpec(memory_space=pl.ANY)
SEM = pl.BlockSpec(memory_space=pltpu.SEMAPHORE)
EFFECT = pltpu.SideEffectType.DATAFLOW_SIDE_EFFECTING
TOKEN = jax.ShapeDtypeStruct((8, 128), jnp.float32)

HEAD_DIM = 64
PAIR = 2 * HEAD_DIM
BRANCHES = ((128, 1), (512, 4), (2048, 16))
SIDE = 64
EPS = 1e-6
NEG_INF = -1e30
N_CHIPS = 4
N_DEV = 8

ADAM_LR = 0.001
ADAM_B1 = 0.9
ADAM_B2 = 0.999
ADAM_EPS = 1e-08
ADAM_WD = 0.01
ADAM_STEP = 10

VMEM_LIMIT_BYTES = 56 * 1024 * 1024
ROW_TILE = 256
COL_TILE = 512
CONV_TILE = 256
ATT_BQ = 128
ATT_KW = ATT_BQ + 2 * SIDE
ATT_UNROLL = 4
ATT_UNROLL_BWD = 4
SMALL_ALIGN = 1024


def _params(semantics=None):
    kw = {"vmem_limit_bytes": VMEM_LIMIT_BYTES}
    if semantics is not None:
        kw["dimension_semantics"] = semantics
    return pltpu.CompilerParams(**kw)


def _silu(z):
    return z * jax.nn.sigmoid(z)


def _silu_grad(z):
    s = jax.nn.sigmoid(z)
    return s * (1.0 + z * (1.0 - s))


def _my_place():
    return lax.axis_index("x"), lax.axis_index("y"), lax.axis_index("c")


def _flip(a, bit):
    return 1 - a if bit else a


def _chip_of(x, y):
    return 2 * x + y


def _allgather8(v, name, after=()):
    rows_per, n = v.shape

    def body(v_ref, *rest):
        out_ref, send_sems, recv_sems = rest[len(after):]
        x, y, c = _my_place()
        me = 4 * x + 2 * y + c

        def rows(idx):
            return out_ref.at[pl.ds(pl.multiple_of(idx * rows_per, rows_per), rows_per), :]

        out_ref[pl.ds(pl.multiple_of(me * rows_per, rows_per), rows_per), :] = v_ref[...]
        copies = []
        for k in range(1, N_DEV):
            peer = (_flip(x, k & 4), _flip(y, k & 2), _flip(c, k & 1))
            cp = pltpu.make_async_remote_copy(
                src_ref=v_ref, dst_ref=rows(me), send_sem=send_sems.at[k - 1], recv_sem=recv_sems.at[k - 1],
                device_id=peer, device_id_type=MESH)
            cp.start()
            copies.append((cp, peer))
        for k, (cp, peer) in enumerate(copies):
            src = 4 * peer[0] + 2 * peer[1] + peer[2]
            pltpu.make_async_remote_copy(
                src_ref=v_ref, dst_ref=rows(src), send_sem=send_sems.at[k], recv_sem=recv_sems.at[k],
                device_id=peer, device_id_type=MESH).wait_recv()
        for cp, _ in copies:
            cp.wait_send()

    return pl.pallas_call(
        body, name=name,
        out_shape=jax.ShapeDtypeStruct((N_DEV * rows_per, n), v.dtype),
        in_specs=[VMEM] + [ANY] * len(after), out_specs=VMEM,
        scratch_shapes=[pltpu.SemaphoreType.DMA((N_DEV - 1,)), pltpu.SemaphoreType.DMA((N_DEV - 1,))],
    )(v, *after)


def _half_rows(ref, chip, which, half):
    return ref.at[chip, pl.ds(pl.multiple_of(which * half, half), half), :]


def _ici_peers(x, y, c):
    peers = [(_flip(x, k & 2), _flip(y, k & 1), c) for k in (1, 2, 3)]
    return [(peer, _chip_of(peer[0], peer[1])) for peer in peers]


def _part_rows(ref, chip, core, part):
    quarter = ref.shape[1] // 4
    return ref.at[chip, pl.ds(pl.multiple_of((2 * core + part) * quarter, quarter), quarter), :]


def _neighbours(x, y, c):
    return [((x, 1 - y, c), _chip_of(x, 1 - y)), ((1 - x, y, c), _chip_of(1 - x, y)),
            ((1 - x, 1 - y, c), _chip_of(1 - x, 1 - y))]


def _start_direct(buf, send_sems, recv_sems):
    x, y, c = _my_place()
    me = _chip_of(x, y)
    for n, (peer, _) in enumerate(_neighbours(x, y, c)[0:2]):
        for part in ((0, 1), (1, 0))[n]:
            piece = _part_rows(buf, me, c, part)
            pltpu.make_async_remote_copy(
                src_ref=piece, dst_ref=piece, send_sem=send_sems.at[2 * n + part], recv_sem=recv_sems.at[2 * n + part],
                device_id=peer, device_id_type=MESH).start()


def _relay(buf, recv_sems, relay_send, relay_recv):
    x, y, c = _my_place()
    nbrs = _neighbours(x, y, c)
    for n in range(2):
        part = n
        piece = _part_rows(buf, nbrs[n][1], c, part)
        pltpu.make_async_remote_copy(
            src_ref=piece, dst_ref=piece, send_sem=relay_send.at[part], recv_sem=recv_sems.at[2 * n + part],
            device_id=nbrs[n][0], device_id_type=MESH).wait_recv()
        pltpu.make_async_remote_copy(
            src_ref=piece, dst_ref=piece, send_sem=relay_send.at[part], recv_sem=relay_recv.at[part],
            device_id=nbrs[1 - n][0], device_id_type=MESH).start()


def _gather_start(win_slots, after):
    def body(win_in, after_ref, win_ref, send_sems, recv_sems, token_ref):
        del win_in, after_ref
        _start_direct(win_ref, send_sems, recv_sems)
        token_ref[...] = jnp.zeros(token_ref.shape, F32)

    sems = pltpu.SemaphoreType.DMA((4,))
    return pl.pallas_call(
        body, name="gather_start",
        out_shape=(jax.ShapeDtypeStruct(win_slots.shape, win_slots.dtype), sems, sems, TOKEN),
        in_specs=[HBM, ANY], out_specs=(HBM, SEM, SEM, VMEM),
        input_output_aliases={0: 0},
        compiler_params=pltpu.CompilerParams(has_side_effects=EFFECT),
    )(win_slots, after)


def _gather_relay_in(win, wout_slots, recv_in, after):
    def body(win_in, wout_in, recv_in_ref, after_ref, win_ref, wout_ref, relay_send, relay_recv, send_out, recv_out):
        del win_in, wout_in, after_ref
        _relay(win_ref, recv_in_ref, relay_send, relay_recv)
        _start_direct(wout_ref, send_out, recv_out)

    two, four = pltpu.SemaphoreType.DMA((2,)), pltpu.SemaphoreType.DMA((4,))
    return pl.pallas_call(
        body, name="gather_relay_w_in",
        out_shape=(jax.ShapeDtypeStruct(win.shape, win.dtype), jax.ShapeDtypeStruct(wout_slots.shape, wout_slots.dtype),
                   two, two, four, four),
        in_specs=[HBM, HBM, SEM, ANY], out_specs=(HBM, HBM, SEM, SEM, SEM, SEM),
        input_output_aliases={0: 0, 1: 1},
        compiler_params=pltpu.CompilerParams(has_side_effects=EFFECT),
    )(win, wout_slots, recv_in, after)


def _gather_relay_out(wout, recv_out, after):
    def body(wout_in, recv_out_ref, after_ref, wout_ref, relay_send, relay_recv):
        del wout_in, after_ref
        _relay(wout_ref, recv_out_ref, relay_send, relay_recv)

    two = pltpu.SemaphoreType.DMA((2,))
    return pl.pallas_call(
        body, name="gather_relay_w_out",
        out_shape=(jax.ShapeDtypeStruct(wout.shape, wout.dtype), two, two),
        in_specs=[HBM, SEM, ANY], out_specs=(HBM, SEM, SEM),
        input_output_aliases={0: 0},
        compiler_params=pltpu.CompilerParams(has_side_effects=EFFECT),
    )(wout, recv_out, after)


def _gather_wait_direct(buf, send_sems, recv_sems, after, name):
    def body(buf_in, send_ref, recv_ref, after_ref, buf_ref):
        del buf_in, after_ref
        x, y, c = _my_place()
        me = _chip_of(x, y)
        for n, (peer, chip) in enumerate(_neighbours(x, y, c)[0:2]):
            second = 1 - n
            pltpu.make_async_remote_copy(
                src_ref=_part_rows(buf_ref, me, c, second), dst_ref=_part_rows(buf_ref, chip, c, second),
                send_sem=send_ref.at[2 * n + second], recv_sem=recv_ref.at[2 * n + second],
                device_id=peer, device_id_type=MESH).wait_recv()
            for part in range(2):
                piece = _part_rows(buf_ref, me, c, part)
                pltpu.make_async_remote_copy(
                    src_ref=piece, dst_ref=piece, send_sem=send_ref.at[2 * n + part], recv_sem=recv_ref.at[2 * n + part],
                    device_id=peer, device_id_type=MESH).wait_send()

    return pl.pallas_call(
        body, name=name,
        out_shape=jax.ShapeDtypeStruct(buf.shape, buf.dtype),
        in_specs=[HBM, SEM, SEM, ANY], out_specs=HBM,
        input_output_aliases={0: 0},
        compiler_params=pltpu.CompilerParams(has_side_effects=EFFECT),
    )(buf, send_sems, recv_sems, after)


def _gather_wait_relayed(buf, relay_send, relay_recv, after, name):
    def body(buf_in, rsend_ref, rrecv_ref, after_ref, buf_ref):
        del buf_in, after_ref
        x, y, c = _my_place()
        nbrs = _neighbours(x, y, c)
        for n in range(2):
            relayed = _part_rows(buf_ref, nbrs[n][1], c, n)
            cp = pltpu.make_async_remote_copy(
                src_ref=relayed, dst_ref=_part_rows(buf_ref, nbrs[2][1], c, n),
                send_sem=rsend_ref.at[n], recv_sem=rrecv_ref.at[n], device_id=nbrs[1 - n][0], device_id_type=MESH)
            cp.wait_recv()
            cp.wait_send()

    return pl.pallas_call(
        body, name=name,
        out_shape=jax.ShapeDtypeStruct(buf.shape, buf.dtype),
        in_specs=[HBM, SEM, SEM, ANY], out_specs=HBM,
        input_output_aliases={0: 0},
        compiler_params=pltpu.CompilerParams(has_side_effects=EFFECT),
    )(buf, relay_send, relay_recv, after)


def _forward_halves(buf, which, name):
    half = buf.shape[1] // 2

    def body(buf_in, buf_ref, send_sems, recv_sems):
        del buf_in
        x, y, c = _my_place()
        sibling = (x, y, 1 - c)
        chips = [_neighbours(x, y, c)[n][1] for n in which]
        started = []
        for k, src_chip in enumerate(chips):
            landed = _half_rows(buf_ref, src_chip, c, half)
            fw = pltpu.make_async_remote_copy(
                src_ref=landed, dst_ref=landed, send_sem=send_sems.at[k], recv_sem=recv_sems.at[k],
                device_id=sibling, device_id_type=MESH)
            fw.start()
            started.append(fw)
        for k, src_chip in enumerate(chips):
            other = _half_rows(buf_ref, src_chip, 1 - c, half)
            pltpu.make_async_remote_copy(
                src_ref=other, dst_ref=other, send_sem=send_sems.at[k], recv_sem=recv_sems.at[k],
                device_id=sibling, device_id_type=MESH).wait_recv()
        for fw in started:
            fw.wait_send()

    return pl.pallas_call(
        body, name=name,
        out_shape=jax.ShapeDtypeStruct(buf.shape, buf.dtype),
        in_specs=[HBM], out_specs=HBM,
        input_output_aliases={0: 0},
        scratch_shapes=[pltpu.SemaphoreType.DMA((len(which),))] * 2,
    )(buf)


def _swap_halves(g, name):
    half = g.shape[1] // 2

    def body(g_ref, r_ref, send_sem, recv_sem):
        x, y, c = _my_place()
        theirs = g_ref.at[:, pl.ds(pl.multiple_of((1 - c) * half, half), half), :]
        cp = pltpu.make_async_remote_copy(
            src_ref=theirs, dst_ref=r_ref, send_sem=send_sem, recv_sem=recv_sem,
            device_id=(x, y, 1 - c), device_id_type=MESH)
        cp.start()
        cp.wait()

    return pl.pallas_call(
        body, name=name,
        out_shape=jax.ShapeDtypeStruct((N_CHIPS, half, g.shape[2]), g.dtype),
        in_specs=[HBM], out_specs=HBM,
        scratch_shapes=[pltpu.SemaphoreType.DMA, pltpu.SemaphoreType.DMA],
    )(g)


def _owners_start(csum, name):
    land = pltpu.with_memory_space_constraint(lax.empty((N_CHIPS - 1,) + csum.shape[1:], csum.dtype), pltpu.HBM)

    def body(csum_ref, land_ref, send_sems, recv_sems, csum_thru, land_thru, token_ref):
        del csum_thru, land_thru
        x, y, c = _my_place()
        for k, (peer, owner) in enumerate(_ici_peers(x, y, c)):
            pltpu.make_async_remote_copy(
                src_ref=csum_ref.at[owner], dst_ref=land_ref.at[k], send_sem=send_sems.at[k], recv_sem=recv_sems.at[k],
                device_id=peer, device_id_type=MESH).start()
        token_ref[...] = jnp.zeros(token_ref.shape, F32)

    sems = pltpu.SemaphoreType.DMA((N_CHIPS - 1,))
    return pl.pallas_call(
        body, name=name,
        out_shape=(sems, sems, jax.ShapeDtypeStruct(csum.shape, csum.dtype),
                   jax.ShapeDtypeStruct(land.shape, land.dtype), TOKEN),
        in_specs=[HBM, HBM], out_specs=(SEM, SEM, HBM, HBM, VMEM),
        input_output_aliases={0: 2, 1: 3},
        compiler_params=pltpu.CompilerParams(has_side_effects=EFFECT),
    )(pltpu.with_memory_space_constraint(csum, pltpu.HBM), land)


def _owners_wait(send_sems, recv_sems, csum, land, after, name):
    def body(csum_ref, land_ref, send_ref, recv_ref, *rest):
        del rest
        x, y, c = _my_place()
        for k, (peer, owner) in enumerate(_ici_peers(x, y, c)):
            cp = pltpu.make_async_remote_copy(
                src_ref=csum_ref.at[owner], dst_ref=land_ref.at[k], send_sem=send_ref.at[k], recv_sem=recv_ref.at[k],
                device_id=peer, device_id_type=MESH)
            cp.wait_send()
            cp.wait_recv()

    return pl.pallas_call(
        body, name=name,
        out_shape=(jax.ShapeDtypeStruct(csum.shape, csum.dtype), jax.ShapeDtypeStruct(land.shape, land.dtype)),
        in_specs=[HBM, HBM, SEM, SEM] + [ANY] * len(after), out_specs=(HBM, HBM),
        input_output_aliases={0: 0, 1: 1},
        compiler_params=pltpu.CompilerParams(has_side_effects=EFFECT),
    )(csum, land, send_sems, recv_sems, *after)[1]


def _join_halves(full, name):
    rows = full.shape[0] // 2

    def body(full_in, full_ref, send_sem, recv_sem):
        del full_in
        x, y, c = _my_place()
        sibling = (x, y, 1 - c)
        mine = full_ref.at[pl.ds(pl.multiple_of(c * rows, rows), rows), :]
        theirs = full_ref.at[pl.ds(pl.multiple_of((1 - c) * rows, rows), rows), :]
        cp = pltpu.make_async_remote_copy(
            src_ref=mine, dst_ref=mine, send_sem=send_sem, recv_sem=recv_sem, device_id=sibling, device_id_type=MESH)
        cp.start()
        pltpu.make_async_remote_copy(
            src_ref=theirs, dst_ref=theirs, send_sem=send_sem, recv_sem=recv_sem,
            device_id=sibling, device_id_type=MESH).wait_recv()
        cp.wait_send()

    return pl.pallas_call(
        body, name=name,
        out_shape=jax.ShapeDtypeStruct(full.shape, full.dtype),
        in_specs=[HBM], out_specs=HBM,
        input_output_aliases={0: 0},
        scratch_shapes=[pltpu.SemaphoreType.DMA, pltpu.SemaphoreType.DMA],
    )(full)


def _cast_into_slot(place, w, name):
    rows, cols = w.shape
    tr = min(rows, ROW_TILE)

    def body(place_ref, w_ref, o_ref):
        del place_ref
        o_ref[...] = w_ref[...].astype(BF16)

    grid_spec = pltpu.PrefetchScalarGridSpec(
        num_scalar_prefetch=1, grid=(rows // tr,),
        in_specs=[pl.BlockSpec((tr, cols), lambda i, p: (i, 0))],
        out_specs=pl.BlockSpec((None, tr, cols), lambda i, p: (p[0], i, 0)))
    return pl.pallas_call(
        body, name=name, grid_spec=grid_spec,
        out_shape=jax.ShapeDtypeStruct((N_CHIPS, rows, cols), BF16),
        compiler_params=_params(("parallel",)),
    )(place, w)


def _ada_partial(c_all, w_ada):
    d_model, wa = w_ada.shape
    tn = 512 if wa % 512 == 0 else 256

    def body(c_ref, w_ref, o_ref):
        o_ref[...] = jnp.dot(_silu(c_ref[...]), w_ref[...], precision=lax.Precision.HIGHEST,
                             preferred_element_type=F32)

    return pl.pallas_call(
        body, name="ada_partial", grid=(wa // tn,),
        out_shape=jax.ShapeDtypeStruct((N_DEV, wa), F32),
        in_specs=[pl.BlockSpec((N_DEV, d_model), lambda i: (0, 0)), pl.BlockSpec((d_model, tn), lambda i: (0, i))],
        out_specs=pl.BlockSpec((N_DEV, tn), lambda i: (0, i)),
        compiler_params=_params(("parallel",)),
    )(c_all, w_ada)


def _prenorm(x, mod, g_pre):
    t, d = x.shape
    tb = ROW_TILE

    def body(x_ref, mod_ref, g_ref, h_ref, ht_ref):
        xv = x_ref[...]
        r = lax.rsqrt(jnp.mean(xv * xv, axis=-1, keepdims=True) + EPS)
        h = (xv * r) * g_ref[...] * (1.0 + mod_ref[1:2, :]) + mod_ref[0:1, :]
        h_ref[...] = h.astype(BF16)
        ht_ref[...] = h.T.astype(BF16)

    return pl.pallas_call(
        body, name="prenorm", grid=(t // tb,),
        out_shape=(jax.ShapeDtypeStruct((t, d), BF16), jax.ShapeDtypeStruct((d, t), BF16)),
        in_specs=[pl.BlockSpec((tb, d), lambda i: (i, 0)), pl.BlockSpec((3, d), lambda i: (0, 0)),
                  pl.BlockSpec((1, d), lambda i: (0, 0))],
        out_specs=(pl.BlockSpec((tb, d), lambda i: (i, 0)), pl.BlockSpec((d, tb), lambda i: (0, i))),
        compiler_params=_params(("parallel",)),
    )(x, mod, g_pre)


def _proj_chunk(proj, h, w, chunk, name):
    t, d = h.shape
    ws = w.shape[-1]
    tn = COL_TILE
    nt = ws // tn

    def body(chunk_ref, *refs):
        del chunk_ref
        a_ref, b_ref, o_ref = refs[-3:]
        o_ref[...] = jnp.dot(a_ref[...], b_ref[...].astype(BF16), preferred_element_type=F32).astype(BF16)

    if w.ndim == 3:
        w_spec = pl.BlockSpec((None, d, tn), lambda n, ch: (ch[0], 0, n))
    else:
        w_spec = pl.BlockSpec((d, tn), lambda n, ch: (0, n))
    first = proj is None
    grid_spec = pltpu.PrefetchScalarGridSpec(
        num_scalar_prefetch=1, grid=(nt,),
        in_specs=([] if first else [HBM]) + [pl.BlockSpec((t, d), lambda n, ch: (0, 0)), w_spec],
        out_specs=pl.BlockSpec((t, tn), lambda n, ch: (0, ch[0] * nt + n)))
    return pl.pallas_call(
        body, name=name, grid_spec=grid_spec,
        out_shape=jax.ShapeDtypeStruct((t, N_CHIPS * ws), BF16),
        input_output_aliases={} if first else {1: 0},
        compiler_params=_params(("parallel",)),
    )(*([chunk] if first else [chunk, proj]), h, w)


def _shift_rows(a, rows):
    idx = lax.broadcasted_iota(jnp.int32, a.shape, 0)
    prev = jnp.where(idx == 0, 0.0, pltpu.roll(a, 1, 0))
    nxt = jnp.where(idx == rows - 1, 0.0, pltpu.roll(a, rows - 1, 0))
    return prev, nxt


def _conv_fwd(conv_proj, conv_w, conv_b, dc):
    t = conv_proj.shape[0]
    ct = CONV_TILE
    nct = dc // ct

    def body(u_ref, cg_ref, w_ref, b_ref, co_ref):
        a = cg_ref[...].astype(F32) * u_ref[...].astype(F32)
        prev, nxt = _shift_rows(a, t)
        co_ref[...] = w_ref[0:1, :] * prev + w_ref[1:2, :] * a + w_ref[2:3, :] * nxt + b_ref[...]

    return pl.pallas_call(
        body, name="conv_fwd", grid=(nct,),
        out_shape=jax.ShapeDtypeStruct((t, dc), F32),
        in_specs=[pl.BlockSpec((t, ct), lambda i: (0, i)), pl.BlockSpec((t, ct), lambda i: (0, 2 * nct + i)),
                  pl.BlockSpec((3, ct), lambda i: (0, i)), pl.BlockSpec((1, ct), lambda i: (0, i))],
        out_specs=pl.BlockSpec((t, ct), lambda i: (0, i)),
        compiler_params=_params(("parallel",)),
    )(conv_proj, conv_proj, conv_w, conv_b)


def _to_residue_major(src_ref, dst_ref, r):
    seq = src_ref.shape[0] // r
    for res in range(r):
        dst_ref[res * seq:(res + 1) * seq, :] = src_ref[pl.ds(res, seq, stride=r), :].astype(dst_ref.dtype)


def _branch_operands(token_refs, stage, dil, r):
    if r == 1:
        return list(token_refs)
    for i, ref in enumerate(token_refs):
        stage[...] = ref[...].astype(F32)
        _to_residue_major(stage, dil.at[i], r)
    return [dil.at[i] for i in range(len(token_refs))]


def _scaled_queries(q):
    return (q.astype(F32) * (HEAD_DIM ** -0.5)).astype(BF16)


BLOCK_SHIFTS = (0, -SIDE, None)


def _band_bias(rel, slope):
    arel = jnp.abs(rel)
    return jnp.where(arel <= SIDE, arel.astype(F32) * slope, NEG_INF)


def _fill_bias_tiles(bias_ref, sl_ref, r, kw):
    base = lax.broadcasted_iota(jnp.int32, (ATT_BQ, kw), 1) - lax.broadcasted_iota(jnp.int32, (ATT_BQ, kw), 0)
    for hh in range(2):
        slope = -(sl_ref[hh:hh + 1, 0:kw] * float(r))
        for e, shift in enumerate(BLOCK_SHIFTS):
            shift = ATT_BQ - kw if shift is None else shift
            bias_ref[hh, e, :, 0:kw] = _band_bias(base + shift, slope)


def _fill_stacked_bias_tiles(bias_ref, sl_ref, r, kw):
    base = lax.broadcasted_iota(jnp.int32, (kw, ATT_BQ), 0) - lax.broadcasted_iota(jnp.int32, (kw, ATT_BQ), 1)
    for hh in range(2):
        slope = -(sl_ref[hh:hh + 1, 0:ATT_BQ] * float(r))
        for e, shift in enumerate(BLOCK_SHIFTS):
            shift = ATT_BQ - kw if shift is None else shift
            bias_ref[e, 0:kw, hh * ATT_BQ:(hh + 1) * ATT_BQ] = _band_bias(base + shift, slope)


def _first_head_lanes():
    return lax.broadcasted_iota(jnp.int32, (1, PAIR), 1) < HEAD_DIM


def _only_head(x, first, hh):
    return jnp.where(first if hh == 0 else jnp.logical_not(first), x, jnp.zeros_like(x))


def _block_place(g, seq_len, kw):
    nqb = seq_len // ATT_BQ
    if nqb == 1:
        row = pl.multiple_of(g * ATT_BQ, ATT_BQ)
        return row, row, 0
    res = g // nqb
    qb = g - res * nqb
    q0 = qb * ATT_BQ
    ks = jnp.clip(q0 - SIDE, 0, seq_len - kw)
    edge = jnp.where(qb == 0, 0, jnp.where(qb == nqb - 1, 2, 1))
    return (pl.multiple_of(res * seq_len + q0, ATT_BQ), pl.multiple_of(res * seq_len + ks, SIDE), edge)


def _qkv_specs(dc, da, t, index):
    return [pl.BlockSpec((t, PAIR), functools.partial(index, (4 * dc + comp * da) // PAIR)) for comp in range(3)]


def _attn_fwd(proj, slopes, dc, da):
    t = proj.shape[0]
    hp = da // PAIR
    n_blocks = t // ATT_BQ

    def body(q_ref, k_ref, v_ref, sl_ref, o_ref, lse_ref, stage, dil, bias, o_res, l_res, o_tok, l_tok):
        for b, (_, r) in enumerate(BRANCHES):
            seq_len = t // r
            kw = min(ATT_KW, seq_len)
            ops = _branch_operands([q_ref, k_ref, v_ref], stage, dil, r)
            _fill_bias_tiles(bias, sl_ref, r, kw)
            o_dst, l_dst = (o_tok.at[b], l_tok.at[b]) if r == 1 else (o_res, l_res)
            first = _first_head_lanes()

            def blocks(trip, carry, seq_len=seq_len, kw=kw, o_dst=o_dst, l_dst=l_dst, first=first, ops=ops):
                nt = (((1,), (1,)), ((), ()))
                places = [_block_place(trip * ATT_UNROLL + i, seq_len, kw) for i in range(ATT_UNROLL)]
                ones = jnp.ones((kw, PAIR), BF16)
                chains = [(i, hh) for i in range(ATT_UNROLL) for hh in range(2)]
                qs = [_scaled_queries(ops[0][pl.ds(qrow, ATT_BQ), :]) for qrow, _, _ in places]
                ks = [ops[1][pl.ds(krow, kw), :] for _, krow, _ in places]
                vs = [ops[2][pl.ds(krow, kw), :] for _, krow, _ in places]
                ss = [lax.dot_general(_only_head(qs[i], first, hh), ks[i], nt, preferred_element_type=F32)
                      + bias[hh, places[i][2], :, 0:kw] for i, hh in chains]
                tops = [jnp.max(s, axis=-1, keepdims=True) for s in ss]
                ps = [jnp.exp(s - m).astype(BF16) for s, m in zip(ss, tops)]
                parts = [jnp.dot(p, jnp.concatenate([_only_head(vs[i], first, hh), _only_head(ones, first, hh)], axis=1),
                                 preferred_element_type=F32) for p, (i, hh) in zip(ps, chains)]
                for i, (qrow, _, _) in enumerate(places):
                    both = parts[2 * i] + parts[2 * i + 1]
                    den = both[:, PAIR:]
                    o_dst[pl.ds(qrow, ATT_BQ), :] = both[:, 0:PAIR] / den
                    l_dst[pl.ds(qrow, ATT_BQ), :] = jnp.where(first, tops[2 * i], tops[2 * i + 1]) + jnp.log(den)
                return carry

            lax.fori_loop(0, n_blocks // ATT_UNROLL, blocks, 0)
            if r > 1:
                for res in range(r):
                    rows = slice(res * seq_len, (res + 1) * seq_len)
                    o_tok[b, pl.ds(res, seq_len, stride=r), :] = o_res[rows, :]
                    l_tok[b, pl.ds(res, seq_len, stride=r), :] = l_res[rows, :]

        def merge(i, carry):
            rows = pl.ds(pl.multiple_of(i * ROW_TILE, ROW_TILE), ROW_TILE)
            la, lb, lc = l_tok[0, rows, :], l_tok[1, rows, :], l_tok[2, rows, :]
            m = jnp.maximum(jnp.maximum(la, lb), lc)
            wa, wb, wc = jnp.exp(la - m), jnp.exp(lb - m), jnp.exp(lc - m)
            den = wa + wb + wc
            o_ref[rows, :] = (wa * o_tok[0, rows, :] + wb * o_tok[1, rows, :] + wc * o_tok[2, rows, :]) * (1.0 / den)
            lse_ref[rows, :] = m + jnp.log(den)
            return carry

        lax.fori_loop(0, t // ROW_TILE, merge, 0)

    pair_spec = pl.BlockSpec((None, t, PAIR), lambda h: (h, 0, 0))
    return pl.pallas_call(
        body, name="attn_fwd", grid=(hp,),
        out_shape=(jax.ShapeDtypeStruct((hp, t, PAIR), F32), jax.ShapeDtypeStruct((hp, t, PAIR), F32)),
        in_specs=_qkv_specs(dc, da, t, lambda first, h: (0, first + h))
        + [pl.BlockSpec((None, 8, ATT_KW), lambda h: (h, 0, 0))],
        out_specs=(pair_spec, pair_spec),
        scratch_shapes=[pltpu.VMEM((t, PAIR), F32), pltpu.VMEM((3, t, PAIR), BF16),
                        pltpu.VMEM((2, 3, ATT_BQ, ATT_KW), F32),
                        pltpu.VMEM((t, PAIR), F32), pltpu.VMEM((t, PAIR), F32),
                        pltpu.VMEM((3, t, PAIR), F32), pltpu.VMEM((3, t, PAIR), F32)],
        compiler_params=_params(("parallel",)),
    )(proj, proj, proj, slopes)


def _attn_bwd(dproj, proj, d_o, lse, delta, slopes, dc, da, after):
    t = proj.shape[0]
    hp = da // PAIR
    n_blocks = t // ATT_BQ

    def all_branches(q_ref, k_ref, v_ref, do_ref, lse_ref, dl_ref, sl_ref,
                     stage, dil, packed, packed_res, row_vecs, bias_t, acc, tot):
        first = _first_head_lanes()
        lane = lax.broadcasted_iota(jnp.int32, (1, PAIR), 1)
        packed[...] = jnp.where((lane & (HEAD_DIM - 1)) < HEAD_DIM // 2, lse_ref[...], dl_ref[...])
        for b, (_, r) in enumerate(BRANCHES):
            seq_len = t // r
            kw = min(ATT_KW, seq_len)
            ops = _branch_operands([q_ref, k_ref, v_ref, do_ref], stage, dil, r)
            scalars = packed
            if r > 1:
                _to_residue_major(packed, packed_res, r)
                scalars = packed_res
            for g in range(n_blocks):
                flipped = scalars[g * ATT_BQ:(g + 1) * ATT_BQ, :].T
                for row in range(4):
                    row_vecs[g, row:row + 1, :] = flipped[row * (HEAD_DIM // 2):row * (HEAD_DIM // 2) + 1, :]
            _fill_stacked_bias_tiles(bias_t, sl_ref, r, kw)
            acc[1] = jnp.zeros((t, PAIR), F32)
            acc[2] = jnp.zeros((t, PAIR), F32)

            def blocks(trip, carry, seq_len=seq_len, kw=kw, ops=ops):
                nt = (((1,), (1,)), ((), ()))
                group = range(ATT_UNROLL_BWD)
                places = [_block_place(trip * ATT_UNROLL_BWD + i, seq_len, kw) for i in group]
                ks, vs, q2s, do2s, lse2s, dl2s = [], [], [], [], [], []
                for i, (qrow, krow, _) in zip(group, places):
                    q = _scaled_queries(ops[0][pl.ds(qrow, ATT_BQ), :])
                    dov = ops[3][pl.ds(qrow, ATT_BQ), :]
                    ks.append(ops[1][pl.ds(krow, kw), :])
                    vs.append(ops[2][pl.ds(krow, kw), :])
                    q2s.append(jnp.concatenate([_only_head(q, first, 0), _only_head(q, first, 1)], axis=0))
                    do2s.append(jnp.concatenate([_only_head(dov, first, 0), _only_head(dov, first, 1)], axis=0))
                    rows = row_vecs[trip * ATT_UNROLL_BWD + i]
                    lse2s.append(jnp.concatenate([rows[0:1, :], rows[2:3, :]], axis=1))
                    dl2s.append(jnp.concatenate([rows[1:2, :], rows[3:4, :]], axis=1))
                s_ts = [lax.dot_general(ks[i], q2s[i], nt, preferred_element_type=F32) for i in group]
                dp_ts = [lax.dot_general(vs[i], do2s[i], nt, preferred_element_type=F32) for i in group]
                p_ts = [jnp.exp(s_ts[i] + bias_t[places[i][2], 0:kw, :] - lse2s[i]) for i in group]
                ds_ts = [p_ts[i] * (dp_ts[i] - dl2s[i]) for i in group]
                dvs = [jnp.dot(p_ts[i].astype(BF16), do2s[i], preferred_element_type=F32) for i in group]
                dks = [jnp.dot(ds_ts[i].astype(BF16), q2s[i], preferred_element_type=F32) for i in group]
                dss = [ds_ts[i].T.astype(BF16) for i in group]
                dqs = [jnp.dot(dss[i][0:ATT_BQ, :], _only_head(ks[i], first, 0), preferred_element_type=F32)
                       + jnp.dot(dss[i][ATT_BQ:2 * ATT_BQ, :], _only_head(ks[i], first, 1), preferred_element_type=F32)
                       for i in group]
                for i, (qrow, krow, _) in zip(group, places):
                    acc[0, pl.ds(qrow, ATT_BQ), :] = dqs[i] * (HEAD_DIM ** -0.5)
                    acc[1, pl.ds(krow, kw), :] += dks[i]
                    acc[2, pl.ds(krow, kw), :] += dvs[i]
                return carry

            lax.fori_loop(0, n_blocks // ATT_UNROLL_BWD, blocks, 0)
            for comp in range(3):
                if r == 1:
                    tot[comp] = acc[comp]
                else:
                    for res in range(r):
                        tok = pl.ds(res, seq_len, stride=r)
                        tot[comp, tok, :] = tot[comp, tok, :] + acc[comp, res * seq_len:(res + 1) * seq_len, :]

    first_q = (4 * dc) // PAIR

    def body(dproj_in, q_ref, k_ref, v_ref, do_ref, lse_ref, dl_ref, sl_ref, after_ref, out_ref, *scratch):
        del dproj_in, after_ref
        work, out_stage, out_sems = scratch[:-2], scratch[-2], scratch[-1]
        h = pl.program_id(0)
        all_branches(q_ref, k_ref, v_ref, do_ref, lse_ref, dl_ref, sl_ref, *work)

        def out_copy(comp):
            cols = pl.ds(pl.multiple_of((first_q + comp * hp + h) * PAIR, PAIR), PAIR)
            return pltpu.make_async_copy(out_stage.at[comp], out_ref.at[:, cols], out_sems.at[comp])

        @pl.when(h > 0)
        def _():
            for comp in range(3):
                out_copy(comp).wait()

        for comp in range(3):
            out_stage[comp] = work[-1][comp].astype(BF16)
            out_copy(comp).start()

        @pl.when(h == hp - 1)
        def _():
            for comp in range(3):
                out_copy(comp).wait()

    pair_spec = pl.BlockSpec((None, t, PAIR), lambda h: (h, 0, 0))
    return pl.pallas_call(
        body, name="attn_bwd", grid=(hp,),
        out_shape=jax.ShapeDtypeStruct(dproj.shape, BF16),
        in_specs=[HBM] + _qkv_specs(dc, da, t, lambda first, h: (0, first + h))
        + [pair_spec, pair_spec, pair_spec, pl.BlockSpec((None, 8, ATT_KW), lambda h: (h, 0, 0)), ANY],
        out_specs=ANY,
        input_output_aliases={0: 0},
        scratch_shapes=[pltpu.VMEM((t, PAIR), F32), pltpu.VMEM((4, t, PAIR), BF16),
                        pltpu.VMEM((t, PAIR), F32), pltpu.VMEM((t, PAIR), F32),
                        pltpu.VMEM((n_blocks, 8, ATT_BQ), F32), pltpu.VMEM((3, ATT_KW, 2 * ATT_BQ), F32),
                        pltpu.VMEM((3, t, PAIR), F32), pltpu.VMEM((3, t, PAIR), F32),
                        pltpu.VMEM((3, t, PAIR), BF16), pltpu.SemaphoreType.DMA((3,))],
        compiler_params=_params(("arbitrary",)),
    )(dproj, proj, proj, proj, d_o, lse, delta, slopes, after)


def _mix_fwd(co, proj, o_mix, g_conv, g_attn_pairs):
    t, dc = co.shape
    hp = o_mix.shape[0]
    da = hp * PAIR
    tb = ROW_TILE

    def body(co_ref, bg_ref, zc_ref, za_ref, om_ref, gc_ref, ga_ref, ycat_ref, ycatt_ref):
        p = bg_ref[...].astype(F32) * co_ref[...]
        rc = lax.rsqrt(jnp.mean(p * p, axis=-1, keepdims=True) + EPS)
        yc = (p * rc) * gc_ref[...] * _silu(zc_ref[...].astype(F32))
        ycat_ref[:, 0:dc] = yc.astype(BF16)
        ycatt_ref[0:dc, :] = yc.T.astype(BF16)
        ssq = jnp.zeros((tb, 1), F32)
        for h in range(hp):
            o = om_ref[h]
            ssq = ssq + jnp.sum(o * o, axis=-1, keepdims=True)
        ra = lax.rsqrt(ssq * (1.0 / da) + EPS)
        for h in range(hp):
            ya = (om_ref[h] * ra) * ga_ref[h] * _silu(za_ref[:, h * PAIR:(h + 1) * PAIR].astype(F32))
            ycat_ref[:, dc + h * PAIR:dc + (h + 1) * PAIR] = ya.astype(BF16)
            ycatt_ref[dc + h * PAIR:dc + (h + 1) * PAIR, :] = ya.T.astype(BF16)

    pair_spec = pl.BlockSpec((hp, tb, PAIR), lambda i: (0, i, 0))
    return pl.pallas_call(
        body, name="mix_fwd", grid=(t // tb,),
        out_shape=(jax.ShapeDtypeStruct((t, dc + da), BF16), jax.ShapeDtypeStruct((dc + da, t), BF16)),
        in_specs=[pl.BlockSpec((tb, dc), lambda i: (i, 0)),
                  pl.BlockSpec((tb, dc), lambda i: (i, 1)),
                  pl.BlockSpec((tb, dc), lambda i: (i, 3)),
                  pl.BlockSpec((tb, da), lambda i: (i, 7)),
                  pair_spec,
                  pl.BlockSpec((1, dc), lambda i: (0, 0)),
                  pl.BlockSpec((hp, 1, PAIR), lambda i: (0, 0, 0))],
        out_specs=(pl.BlockSpec((tb, dc + da), lambda i: (i, 0)), pl.BlockSpec((dc + da, tb), lambda i: (0, i))),
        compiler_params=_params(("parallel",)),
    )(co, proj, proj, proj, o_mix, g_conv, g_attn_pairs)


def _out_fwd_bwd(ycat, woutf, x, target, mod, g_post):
    t, d = x.shape
    n = ycat.shape[1]
    tb = ROW_TILE

    def body(a_ref, w_ref, x_ref, tg_ref, mod_ref, g_ref, dout_ref, dy_ref, acc_ref):
        y = jnp.dot(a_ref[...], w_ref[...], preferred_element_type=F32)
        r = lax.rsqrt(jnp.mean(y * y, axis=-1, keepdims=True) + EPS)
        nh = y * r
        gate = mod_ref[2:3, :]
        nrm = nh * g_ref[...]
        err = x_ref[...] + gate * nrm - tg_ref[...]
        dout = err * (1.0 / d)
        dout_ref[...] = dout
        dn = dout * gate
        a = dn * g_ref[...]
        dy = r * (a - nh * jnp.mean(a * nh, axis=-1, keepdims=True))
        dy_ref[...] = dy.astype(BF16)
        loss = 0.5 * jnp.sum(jnp.sum(err * err, axis=-1, keepdims=True) * (1.0 / d), axis=0, keepdims=True)
        part = jnp.concatenate(
            [jnp.sum(dout * nrm, axis=0, keepdims=True), jnp.sum(dn * nh, axis=0, keepdims=True),
             jnp.broadcast_to(loss, (1, d)), jnp.zeros((5, d), F32)], axis=0)

        @pl.when(pl.program_id(0) == 0)
        def _():
            acc_ref[...] = jnp.zeros(acc_ref.shape, F32)

        acc_ref[...] += part

    return pl.pallas_call(
        body, name="out_fwd_bwd", grid=(t // tb,),
        out_shape=(jax.ShapeDtypeStruct((t, d), F32), jax.ShapeDtypeStruct((t, d), BF16),
                   jax.ShapeDtypeStruct((8, d), F32)),
        in_specs=[pl.BlockSpec((tb, n), lambda i: (i, 0)), pl.BlockSpec((n, d), lambda i: (0, 0)),
                  pl.BlockSpec((tb, d), lambda i: (i, 0)), pl.BlockSpec((tb, d), lambda i: (i, 0)),
                  pl.BlockSpec((3, d), lambda i: (0, 0)), pl.BlockSpec((1, d), lambda i: (0, 0))],
        out_specs=(pl.BlockSpec((tb, d), lambda i: (i, 0)), pl.BlockSpec((tb, d), lambda i: (i, 0)),
                   pl.BlockSpec((8, d), lambda i: (0, 0))),
        compiler_params=_params(("arbitrary",)),
    )(ycat, woutf, x, target, mod, g_post)


def _matmul_nt(a, b, out_dtype, name):
    m, k = a.shape
    n = b.shape[0]
    tn = COL_TILE

    def body(a_ref, b_ref, o_ref):
        o_ref[...] = lax.dot_general(a_ref[...], b_ref[...], (((1,), (1,)), ((), ())),
                                     preferred_element_type=F32).astype(out_dtype)

    return pl.pallas_call(
        body, name=name, grid=(n // tn,),
        out_shape=jax.ShapeDtypeStruct((m, n), out_dtype),
        in_specs=[pl.BlockSpec((m, k), lambda i: (0, 0)), pl.BlockSpec((tn, k), lambda i: (i, 0))],
        out_specs=pl.BlockSpec((m, tn), lambda i: (0, i)),
        compiler_params=_params(("parallel",)),
    )(a, b)


def _matmul_nn(a, b, out_dtype, name):
    m, k = a.shape
    n = b.shape[1]
    tn = COL_TILE

    def body(a_ref, b_ref, o_ref):
        o_ref[...] = jnp.dot(a_ref[...], b_ref[...], preferred_element_type=F32).astype(out_dtype)

    return pl.pallas_call(
        body, name=name, grid=(n // tn,),
        out_shape=jax.ShapeDtypeStruct((m, n), out_dtype),
        in_specs=[pl.BlockSpec((m, k), lambda i: (0, 0)), pl.BlockSpec((k, tn), lambda i: (0, i))],
        out_specs=pl.BlockSpec((m, tn), lambda i: (0, i)),
        compiler_params=_params(("parallel",)),
    )(a, b)


def _mix_bwd(dycat, co, proj, o_mix, g_conv, g_attn_pairs):
    t, dc = co.shape
    hp = o_mix.shape[0]
    da = hp * PAIR
    tb = ROW_TILE

    def body(dy_ref, co_ref, bg_ref, zc_ref, za_ref, om_ref, gc_ref, ga_ref,
             dcp_ref, dco_ref, do_ref, dl_ref, dgc_ref, dga_ref):
        first = pl.program_id(0) == 0
        cov = co_ref[...]
        bg = bg_ref[...].astype(F32)
        zc = zc_ref[...].astype(F32)
        p = bg * cov
        rc = lax.rsqrt(jnp.mean(p * p, axis=-1, keepdims=True) + EPS)
        nh = p * rc
        dyc = dy_ref[:, 0:dc].astype(F32)
        dn = dyc * _silu(zc)
        a = dn * gc_ref[...]
        dp = rc * (a - nh * jnp.mean(a * nh, axis=-1, keepdims=True))
        dcp_ref[:, 0:dc] = jnp.zeros((tb, dc), BF16)
        dcp_ref[:, dc:2 * dc] = (dp * cov).astype(BF16)
        dcp_ref[:, 2 * dc:3 * dc] = jnp.zeros((tb, dc), BF16)
        dcp_ref[:, 3 * dc:4 * dc] = (dyc * nh * gc_ref[...] * _silu_grad(zc)).astype(BF16)
        dcp_ref[:, 4 * dc:4 * dc + 3 * da] = jnp.zeros((tb, 3 * da), BF16)
        dco_ref[...] = dp * bg

        @pl.when(first)
        def _():
            dgc_ref[...] = jnp.zeros(dgc_ref.shape, F32)
            dga_ref[...] = jnp.zeros(dga_ref.shape, F32)

        dgc_ref[...] += jnp.sum(dn * nh, axis=0, keepdims=True)

        ssq = jnp.zeros((tb, 1), F32)
        for h in range(hp):
            o = om_ref[h]
            ssq = ssq + jnp.sum(o * o, axis=-1, keepdims=True)
        ra = lax.rsqrt(ssq * (1.0 / da) + EPS)
        dot_an = jnp.zeros((tb, 1), F32)
        for h in range(hp):
            nha = om_ref[h] * ra
            za = za_ref[:, h * PAIR:(h + 1) * PAIR].astype(F32)
            dya = dy_ref[:, dc + h * PAIR:dc + (h + 1) * PAIR].astype(F32)
            dna = dya * _silu(za)
            dza = (dya * nha * ga_ref[h] * _silu_grad(za)).astype(BF16)
            dcp_ref[:, 4 * dc + 3 * da + h * PAIR:4 * dc + 3 * da + (h + 1) * PAIR] = dza
            dga_ref[h] += jnp.sum(dna * nha, axis=0, keepdims=True)
            dot_an = dot_an + jnp.sum(dna * ga_ref[h] * nha, axis=-1, keepdims=True)
        mean_an = dot_an * (1.0 / da)
        first_head = lax.broadcasted_iota(jnp.int32, (tb, PAIR), 1) < HEAD_DIM
        for h in range(hp):
            o = om_ref[h]
            nha = o * ra
            za = za_ref[:, h * PAIR:(h + 1) * PAIR].astype(F32)
            dya = dy_ref[:, dc + h * PAIR:dc + (h + 1) * PAIR].astype(F32)
            aa = dya * _silu(za) * ga_ref[h]
            d_o = ra * (aa - nha * mean_an)
            do_ref[h] = d_o.astype(BF16)
            prod = d_o * o
            both = jnp.sum(prod, axis=-1, keepdims=True)
            head0 = jnp.sum(jnp.where(first_head, prod, 0.0), axis=-1, keepdims=True)
            dl_ref[h] = jnp.where(first_head, head0, both - head0)

    pair_spec = pl.BlockSpec((hp, tb, PAIR), lambda i: (0, i, 0))
    return pl.pallas_call(
        body, name="mix_bwd", grid=(t // tb,),
        out_shape=(jax.ShapeDtypeStruct((t, 4 * dc + 4 * da), BF16), jax.ShapeDtypeStruct((t, dc), F32),
                   jax.ShapeDtypeStruct((hp, t, PAIR), BF16), jax.ShapeDtypeStruct((hp, t, PAIR), F32),
                   jax.ShapeDtypeStruct((1, dc), F32), jax.ShapeDtypeStruct((hp, 1, PAIR), F32)),
        in_specs=[pl.BlockSpec((tb, dc + da), lambda i: (i, 0)),
                  pl.BlockSpec((tb, dc), lambda i: (i, 0)),
                  pl.BlockSpec((tb, dc), lambda i: (i, 1)),
                  pl.BlockSpec((tb, dc), lambda i: (i, 3)),
                  pl.BlockSpec((tb, da), lambda i: (i, 7)),
                  pair_spec,
                  pl.BlockSpec((1, dc), lambda i: (0, 0)),
                  pl.BlockSpec((hp, 1, PAIR), lambda i: (0, 0, 0))],
        out_specs=(pl.BlockSpec((tb, 4 * dc + 4 * da), lambda i: (i, 0)), pl.BlockSpec((tb, dc), lambda i: (i, 0)),
                   pair_spec, pair_spec,
                   pl.BlockSpec((1, dc), lambda i: (0, 0)), pl.BlockSpec((hp, 1, PAIR), lambda i: (0, 0, 0))),
        compiler_params=_params(("arbitrary",)),
    )(dycat, co, proj, proj, proj, o_mix, g_conv, g_attn_pairs)


def _conv_bwd(dconv_proj, dco, conv_proj, conv_w, dc, after):
    t = dco.shape[0]
    ct = CONV_TILE
    nct = dc // ct

    def body(dcp_in_ref, dco_ref, u_ref, cg_ref, w_ref, after_ref, dcp_ref, acc_ref):
        del dcp_in_ref, after_ref
        which = pl.program_id(1)
        g = dco_ref[...]
        u = u_ref[...].astype(F32)
        cg = cg_ref[...].astype(F32)
        g_prev, g_next = _shift_rows(g, t)
        da = w_ref[0:1, :] * g_next + w_ref[1:2, :] * g + w_ref[2:3, :] * g_prev
        dcp_ref[...] = (da * jnp.where(which == 0, cg, u)).astype(BF16)
        a = cg * u
        a_prev, a_next = _shift_rows(a, t)
        acc_ref[...] = jnp.concatenate(
            [jnp.sum(g * a_prev, axis=0, keepdims=True), jnp.sum(g * a, axis=0, keepdims=True),
             jnp.sum(g * a_next, axis=0, keepdims=True), jnp.sum(g, axis=0, keepdims=True),
             jnp.zeros((4, ct), F32)], axis=0)

    return pl.pallas_call(
        body, name="conv_bwd", grid=(nct, 2),
        out_shape=(jax.ShapeDtypeStruct(dconv_proj.shape, BF16), jax.ShapeDtypeStruct((8, dc), F32)),
        in_specs=[HBM,
                  pl.BlockSpec((t, ct), lambda i, s: (0, i)),
                  pl.BlockSpec((t, ct), lambda i, s: (0, i)),
                  pl.BlockSpec((t, ct), lambda i, s: (0, 2 * nct + i)),
                  pl.BlockSpec((3, ct), lambda i, s: (0, i)), ANY],
        out_specs=(pl.BlockSpec((t, ct), lambda i, s: (0, 2 * s * nct + i)),
                   pl.BlockSpec((8, ct), lambda i, s: (0, i))),
        input_output_aliases={0: 0},
        compiler_params=_params(("arbitrary", "arbitrary")),
    )(dconv_proj, dco, conv_proj, conv_proj, conv_w, after)


def _dw_in(ht, dproj):
    d, t = ht.shape
    ws = dproj.shape[1] // N_CHIPS
    tn = COL_TILE
    nt = ws // tn

    def body(a_ref, b_ref, o_ref):
        o_ref[...] = jnp.dot(a_ref[...], b_ref[...], preferred_element_type=F32).astype(BF16)

    return pl.pallas_call(
        body, name="dw_in", grid=(N_CHIPS, nt),
        out_shape=jax.ShapeDtypeStruct((N_CHIPS, d, ws), BF16),
        in_specs=[pl.BlockSpec((d, t), lambda j, n: (0, 0)), pl.BlockSpec((t, tn), lambda j, n: (0, j * nt + n))],
        out_specs=pl.BlockSpec((None, d, tn), lambda j, n: (j, 0, n)),
        compiler_params=_params(("parallel", "parallel")),
    )(ht, dproj)


def _dh(dproj, winf, after):
    t = dproj.shape[0]
    _, d, ws = winf.shape
    tm = tn = COL_TILE
    nt = (((1,), (1,)), ((), ()))

    def body(a_ref, w_ref, after_ref, o_ref):
        del after_ref
        acc = lax.dot_general(a_ref[:, 0:ws], w_ref[0], nt, preferred_element_type=F32)
        for j in range(1, N_CHIPS):
            acc = acc + lax.dot_general(a_ref[:, j * ws:(j + 1) * ws], w_ref[j], nt, preferred_element_type=F32)
        o_ref[...] = acc

    return pl.pallas_call(
        body, name="dh", grid=(d // tn, t // tm),
        out_shape=jax.ShapeDtypeStruct((t, d), F32),
        in_specs=[pl.BlockSpec((tm, N_CHIPS * ws), lambda n, m: (m, 0)),
                  pl.BlockSpec((N_CHIPS, tn, ws), lambda n, m: (0, n, 0)), ANY],
        out_specs=pl.BlockSpec((tm, tn), lambda n, m: (m, n)),
        compiler_params=_params(("parallel", "parallel")),
    )(dproj, winf, after)


def _prenorm_bwd(x, dh, dout, mod, g_pre):
    t, d = x.shape
    tb = ROW_TILE

    def body(x_ref, dh_ref, dout_ref, mod_ref, g_ref, gx_ref, acc_ref):
        xv = x_ref[...]
        dhv = dh_ref[...]
        r = lax.rsqrt(jnp.mean(xv * xv, axis=-1, keepdims=True) + EPS)
        xh = xv * r
        one_scale = 1.0 + mod_ref[1:2, :]
        a = dhv * one_scale * g_ref[...]
        gx_ref[...] = dout_ref[...] + r * (a - xh * jnp.mean(a * xh, axis=-1, keepdims=True))
        part = jnp.concatenate(
            [jnp.sum(dhv, axis=0, keepdims=True), jnp.sum(dhv * xh * g_ref[...], axis=0, keepdims=True),
             jnp.sum(dhv * xh * one_scale, axis=0, keepdims=True), jnp.zeros((5, d), F32)], axis=0)

        @pl.when(pl.program_id(0) == 0)
        def _():
            acc_ref[...] = jnp.zeros(acc_ref.shape, F32)

        acc_ref[...] += part

    return pl.pallas_call(
        body, name="prenorm_bwd", grid=(t // tb,),
        out_shape=(jax.ShapeDtypeStruct((t, d), F32), jax.ShapeDtypeStruct((8, d), F32)),
        in_specs=[pl.BlockSpec((tb, d), lambda i: (i, 0)), pl.BlockSpec((tb, d), lambda i: (i, 0)),
                  pl.BlockSpec((tb, d), lambda i: (i, 0)), pl.BlockSpec((3, d), lambda i: (0, 0)),
                  pl.BlockSpec((1, d), lambda i: (0, 0))],
        out_specs=(pl.BlockSpec((tb, d), lambda i: (i, 0)), pl.BlockSpec((8, d), lambda i: (0, 0))),
        compiler_params=_params(("arbitrary",)),
    )(x, dh, dout, mod, g_pre)


def _chip_sums(place, g, rsib, name):
    _, rows, cols = g.shape
    half = rows // 2
    tr = min(half, ROW_TILE)
    nt = half // tr

    def body(place_ref, g_ref, r_ref, o_ref):
        del place_ref
        o_ref[...] = (g_ref[...].astype(F32) + r_ref[...].astype(F32)).astype(BF16)

    grid_spec = pltpu.PrefetchScalarGridSpec(
        num_scalar_prefetch=1, grid=(N_CHIPS, nt),
        in_specs=[pl.BlockSpec((None, tr, cols), lambda j, i, p: (j, p[1] * nt + i, 0)),
                  pl.BlockSpec((None, tr, cols), lambda j, i, p: (j, i, 0))],
        out_specs=pl.BlockSpec((None, tr, cols), lambda j, i, p: (j, i, 0)))
    return pl.pallas_call(
        body, name=name, grid_spec=grid_spec,
        out_shape=jax.ShapeDtypeStruct((N_CHIPS, half, cols), BF16),
        compiler_params=_params(("parallel", "parallel")),
    )(place, g, rsib)


def _owner_sum(place, g, rsib, rici, name):
    _, rows, cols = g.shape
    half = rows // 2
    tr = min(half, ROW_TILE)
    nt = half // tr

    def body(place_ref, g_ref, r_ref, i_ref, o_ref):
        del place_ref
        acc = g_ref[...].astype(F32) + r_ref[...].astype(F32)
        for k in range(N_CHIPS - 1):
            acc = acc + i_ref[k].astype(F32)
        o_ref[...] = acc

    grid_spec = pltpu.PrefetchScalarGridSpec(
        num_scalar_prefetch=1, grid=(nt,),
        in_specs=[pl.BlockSpec((None, tr, cols), lambda i, p: (p[0], p[1] * nt + i, 0)),
                  pl.BlockSpec((None, tr, cols), lambda i, p: (p[0], i, 0)),
                  pl.BlockSpec((N_CHIPS - 1, tr, cols), lambda i, p: (0, i, 0))],
        out_specs=pl.BlockSpec((tr, cols), lambda i, p: (p[1] * nt + i, 0)))
    return pl.pallas_call(
        body, name=name, grid_spec=grid_spec,
        out_shape=jax.ShapeDtypeStruct((rows, cols), F32),
        compiler_params=_params(("parallel",)),
    )(place, g, rsib, rici)


def _adam_math(w, g, m, v):
    m2 = ADAM_B1 * m + (1.0 - ADAM_B1) * g
    v2 = ADAM_B2 * v + (1.0 - ADAM_B2) * (g * g)
    m_hat = m2 / (1.0 - ADAM_B1 ** ADAM_STEP)
    v_hat = v2 / (1.0 - ADAM_B2 ** ADAM_STEP)
    delta = -ADAM_LR * (m_hat / (jnp.sqrt(v_hat) + ADAM_EPS) + ADAM_WD * w)
    return delta, m2, v2


def _adamw(w, g, m, v, name):
    rows, cols = w.shape
    tr = min(rows, ROW_TILE)

    def body(w_ref, g_ref, m_ref, v_ref, go_ref, d_ref, m2_ref, v2_ref):
        g = g_ref[...]
        go_ref[...] = g
        d_ref[...], m2_ref[...], v2_ref[...] = _adam_math(w_ref[...], g, m_ref[...], v_ref[...])

    spec = pl.BlockSpec((tr, cols), lambda i: (i, 0))
    return pl.pallas_call(
        body, name=name, grid=(rows // tr,),
        out_shape=(jax.ShapeDtypeStruct(w.shape, F32),) * 4,
        in_specs=[spec] * 4, out_specs=(spec,) * 4,
        compiler_params=_params(("parallel",)),
    )(w, g, m, v)


def _ada_grad_adamw(c_all_t, dmod_cols, w, m, v):
    d, wa = w.shape
    tr = ROW_TILE

    def body(ct_ref, dm_ref, w_ref, m_ref, v_ref, g_ref, d_ref, m2_ref, v2_ref):
        act = _silu(ct_ref[...])
        g = act[:, 0:1] * dm_ref[0:1, :]
        for b in range(1, N_DEV):
            g = g + act[:, b:b + 1] * dm_ref[b:b + 1, :]
        g_ref[...] = g
        d_ref[...], m2_ref[...], v2_ref[...] = _adam_math(w_ref[...], g, m_ref[...], v_ref[...])

    spec = pl.BlockSpec((tr, wa), lambda i: (i, 0))
    return pl.pallas_call(
        body, name="ada_grad_adamw", grid=(d // tr,),
        out_shape=(jax.ShapeDtypeStruct(w.shape, F32),) * 4,
        in_specs=[pl.BlockSpec((tr, N_DEV), lambda i: (i, 0)), pl.BlockSpec((N_DEV, wa), lambda i: (0, 0)),
                  spec, spec, spec],
        out_specs=(spec,) * 4,
        compiler_params=_params(("parallel",)),
    )(c_all_t, dmod_cols, w, m, v)


def _sum_devices(gathered):
    n = gathered.shape[1]

    def body(g_ref, o_ref):
        acc = g_ref[0:8, :]
        for dev in range(1, N_DEV):
            acc = acc + g_ref[8 * dev:8 * dev + 8, :]
        o_ref[...] = acc

    return pl.pallas_call(
        body, name="sum_devices",
        out_shape=jax.ShapeDtypeStruct((8, n), F32),
        in_specs=[VMEM], out_specs=VMEM,
    )(gathered)


def _pack_small(pieces):
    flat = [p.reshape(-1).astype(F32) for p in pieces]
    offsets, total = [], 0
    for p in flat:
        offsets.append(total)
        total += p.shape[0]
    padded = -(-total // SMALL_ALIGN) * SMALL_ALIGN
    if padded > total:
        flat.append(jnp.zeros((padded - total,), F32))
    return jnp.concatenate(flat).reshape(8, padded // 8), offsets


def _alibi_slope_rows(n_heads):
    slopes = 2.0 ** (-8.0 * jnp.arange(1, n_heads + 1, dtype=F32) / n_heads)
    rows = jnp.zeros((n_heads // 2, 8), F32).at[:, 0:2].set(slopes.reshape(n_heads // 2, 2))
    return jnp.broadcast_to(rows[:, :, None], (n_heads // 2, 8, ATT_KW))


def kernel(x, c, w_ada, b_ada, g_pre, w_in, conv_w, conv_b, g_conv, g_attn, w_out, g_post, loss_target, m_w_ada, m_b_ada, m_g_pre, m_w_in, m_conv_w, m_conv_b, m_g_conv, m_g_attn, m_w_out, m_g_post, v_w_ada, v_b_ada, v_g_pre, v_w_in, v_conv_w, v_conv_b, v_g_conv, v_g_attn, v_w_out, v_g_post):
    t, d = x.shape[1], x.shape[2]
    dc = conv_b.shape[1]
    da = g_attn.shape[1]
    hp = da // PAIR
    ws = w_in.shape[2]
    wa = w_ada.shape[2]
    cws = conv_w.shape[2]
    assert t % ROW_TILE == 0 and d % ROW_TILE == 0 and dc % COL_TILE == 0 and da % COL_TILE == 0
    assert ws == 2 * dc and dc == da and t // BRANCHES[-1][1] >= ATT_BQ

    mx, my, mc = _my_place()
    chip = _chip_of(mx, my)
    dev = 2 * chip + mc
    place = jnp.stack([chip, mc]).astype(jnp.int32)

    x2, tgt2 = x[0], loss_target[0]
    w_ada2, w_in2, w_out2 = w_ada[0], w_in[0], w_out[0]

    win_slots = _cast_into_slot(place, w_in2, "cast_w_in")
    packed, offs = _pack_small([c[0], conv_w[0]])
    seen = _allgather8(packed, "gather_inputs", after=(win_slots,)).reshape(N_DEV, -1)
    c_all = seen[:, offs[0]:offs[0] + d]
    conv_w_full = seen[0::2, offs[1]:offs[1] + 3 * cws].reshape(N_CHIPS, 3, cws).transpose(1, 0, 2).reshape(3, dc)

    ada_part = _ada_partial(c_all, w_ada2)
    ada_seen = _allgather8(ada_part, "gather_ada").reshape(N_DEV, N_DEV, wa)
    mod_flat = lax.dynamic_index_in_dim(ada_seen[0::2], dev, axis=1, keepdims=False).reshape(1, 3 * d) + b_ada
    mod = mod_flat.reshape(3, d)

    win_flight, send_in, recv_in, started = _gather_start(win_slots, mod)

    y_chip, x_chip, d_chip = (_chip_of(mx, 1 - my), _chip_of(1 - mx, my), _chip_of(1 - mx, 1 - my))
    chunk_ids = [jnp.reshape(j, (1,)).astype(jnp.int32) for j in (chip, y_chip, x_chip, d_chip)]
    h, ht = _prenorm(x2, mod + started[0, 0], g_pre)
    proj = _proj_chunk(None, h, w_in2, chunk_ids[0], "proj_own")
    win_flight, wout_flight, relay_send_in, relay_recv_in, send_out, recv_out = _gather_relay_in(
        win_flight, _cast_into_slot(place, w_out2, "cast_w_out"), recv_in, proj)
    win_flight = _forward_halves(
        _gather_wait_direct(win_flight, send_in, recv_in, proj, "gather_wait_w_in_direct"), (0, 1), "forward_w_in_direct")
    proj = _proj_chunk(proj, h, win_flight, chunk_ids[1], "proj_y")
    proj = _proj_chunk(proj, h, win_flight, chunk_ids[2], "proj_x")
    winf = _forward_halves(
        _gather_wait_relayed(win_flight, relay_send_in, relay_recv_in, proj, "gather_wait_w_in_relayed"),
        (2,), "forward_w_in_relayed")
    proj = _proj_chunk(proj, h, winf, chunk_ids[3], "proj_diagonal")
    slopes = _alibi_slope_rows(da // HEAD_DIM)
    co = _conv_fwd(proj, conv_w_full, conv_b, dc)
    wout_flight, relay_send_out, relay_recv_out = _gather_relay_out(wout_flight, recv_out, co)
    o_mix, lse = _attn_fwd(proj, slopes, dc, da)
    g_attn_pairs = g_attn.reshape(hp, 1, PAIR)
    ycat, ycat_t = _mix_fwd(co, proj, o_mix, g_conv, g_attn_pairs)
    wout_flight = _gather_wait_direct(wout_flight, send_out, recv_out, ycat, "gather_wait_w_out_direct")
    wout_flight = _gather_wait_relayed(wout_flight, relay_send_out, relay_recv_out, ycat, "gather_wait_w_out_relayed")
    woutf = _forward_halves(wout_flight, (0, 1, 2), "forward_w_out").reshape(dc + da, d)
    dout, dy, post_sums = _out_fwd_bwd(ycat, woutf, x2, tgt2, mod, g_post)

    gout = _matmul_nn(ycat_t, dy, BF16, "dw_out").reshape(N_CHIPS, (dc + da) // N_CHIPS, d)
    rsib_out = _swap_halves(gout, "rs_swap_halves_out")
    csum_out = _chip_sums(place, gout, rsib_out, "rs_chip_sum_out")
    ssem_out, rsem_out, csum_out, land_out, sent_out = _owners_start(csum_out, "rs_owners_start_out")
    dycat = _matmul_nt(dy, woutf, BF16, "dycat")
    dproj, dco, d_o, delta, dg_conv, dg_attn = _mix_bwd(dycat, co, proj, o_mix, g_conv, g_attn_pairs)
    dproj, conv_sums = _conv_bwd(dproj, dco, proj, conv_w_full, dc, sent_out)
    dproj = _attn_bwd(dproj, proj, d_o, lse, delta, slopes, dc, da, sent_out)
    gin = _dw_in(ht, dproj)
    rsib_in = _swap_halves(gin, "rs_swap_halves_in")
    csum_in = _chip_sums(place, gin, rsib_in, "rs_chip_sum_in")
    ssem_in, rsem_in, csum_in, land_in, sent_in = _owners_start(csum_in, "rs_owners_start_in")
    dh = _dh(dproj, winf, sent_in)
    grad_x, pre_sums = _prenorm_bwd(x2, dh, dout, mod, g_pre)

    rici_out = _owners_wait(ssem_out, rsem_out, csum_out, land_out, [grad_x], "rs_owners_wait_out")
    grad_w_out = _join_halves(_owner_sum(place, gout, rsib_out, rici_out, "rs_owner_sum_out"), "rs_join_halves_out")

    small, so = _pack_small([
        pre_sums[0], pre_sums[1], post_sums[0],
        pre_sums[2], conv_sums[0:3], conv_sums[3], dg_conv, dg_attn, post_sums[1], post_sums[2, 0:128]])
    small_seen = _allgather8(small, "gather_small")
    total = _sum_devices(small_seen).reshape(-1)
    dmod_all = small_seen.reshape(N_DEV, -1)[:, 0:3 * d]
    loss = total[so[9]]
    grad_b_ada = total[0:3 * d].reshape(1, 3 * d)
    grad_g_pre = total[so[3]:so[3] + d].reshape(1, d)
    grad_conv_w_full = total[so[4]:so[4] + 3 * dc].reshape(3, dc)
    grad_conv_w = lax.dynamic_slice_in_dim(grad_conv_w_full, chip * cws, cws, axis=1).reshape(1, 3, cws)
    grad_conv_b = total[so[5]:so[5] + dc].reshape(1, dc)
    grad_g_conv = total[so[6]:so[6] + dc].reshape(1, dc)
    grad_g_attn = total[so[7]:so[7] + da].reshape(1, da)
    grad_g_post = total[so[8]:so[8] + d].reshape(1, d)

    dmod_cols = lax.dynamic_slice_in_dim(dmod_all, chip * wa, wa, axis=1)
    grad_w_ada, delta_w_ada, new_m_w_ada, new_v_w_ada = _ada_grad_adamw(c_all.T, dmod_cols, w_ada2, m_w_ada[0], v_w_ada[0])
    grad_w_out, delta_w_out, new_m_w_out, new_v_w_out = _adamw(
        w_out2, grad_w_out, m_w_out[0], v_w_out[0], "adamw_w_out")

    small_w = [b_ada, g_pre, conv_w, conv_b, g_conv, g_attn, g_post]
    small_g = [grad_b_ada, grad_g_pre, grad_conv_w, grad_conv_b, grad_g_conv, grad_g_attn, grad_g_post]
    small_m = [m_b_ada, m_g_pre, m_conv_w, m_conv_b, m_g_conv, m_g_attn, m_g_post]
    small_v = [v_b_ada, v_g_pre, v_conv_w, v_conv_b, v_g_conv, v_g_attn, v_g_post]
    pw, po = _pack_small(small_w)
    pg, _ = _pack_small(small_g)
    pm, _ = _pack_small(small_m)
    pv, _ = _pack_small(small_v)
    sd, sm, sv = (a.reshape(-1) for a in _adamw(pw, pg, pm, pv, "adamw_small")[1:])

    def unpack(flat):
        return [flat[o:o + w.size].reshape(w.shape) for o, w in zip(po, small_w)]

    d_small, m_small, v_small = unpack(sd), unpack(sm), unpack(sv)

    rici_in = _owners_wait(ssem_in, rsem_in, csum_in, land_in, [sd, delta_w_out, delta_w_ada], "rs_owners_wait_in")
    grad_w_in = _join_halves(_owner_sum(place, gin, rsib_in, rici_in, "rs_owner_sum_in"), "rs_join_halves_in")
    grad_w_in, delta_w_in, new_m_w_in, new_v_w_in = _adamw(w_in2, grad_w_in, m_w_in[0], v_w_in[0], "adamw_w_in")

    def lead(a):
        return a.reshape((1,) + a.shape)

    grads = [lead(grad_w_ada), grad_b_ada, grad_g_pre, lead(grad_w_in), grad_conv_w, grad_conv_b, grad_g_conv,
             grad_g_attn, lead(grad_w_out), grad_g_post]
    deltas = [lead(delta_w_ada), d_small[0], d_small[1], lead(delta_w_in), d_small[2], d_small[3], d_small[4],
              d_small[5], lead(delta_w_out), d_small[6]]
    new_ms = [lead(new_m_w_ada), m_small[0], m_small[1], lead(new_m_w_in), m_small[2], m_small[3], m_small[4],
              m_small[5], lead(new_m_w_out), m_small[6]]
    new_vs = [lead(new_v_w_ada), v_small[0], v_small[1], lead(new_v_w_in), v_small[2], v_small[3], v_small[4],
              v_small[5], lead(new_v_w_out), v_small[6]]
    return (loss, lead(grad_x), *grads, *deltas, *new_ms, *new_vs)
```

```python
import functools

import jax
import jax.numpy as jnp
from jax import lax
from jax.experimental import pallas as pl
from jax.experimental.pallas import tpu as pltpu

F32 = jnp.float32
BF16 = jnp.bfloat16
MESH = pl.DeviceIdType.MESH
HBM = pl.BlockSpec(memory_space=pltpu.HBM)
VMEM = pl.BlockSpec(memory_space=pltpu.VMEM)
ANY = pl.BlockSpec(memory_space=pl.ANY)
SEM = pl.BlockSpec(memory_space=pltpu.SEMAPHORE)
EFFECT = pltpu.SideEffectType.DATAFLOW_SIDE_EFFECTING
TOKEN = jax.ShapeDtypeStruct((8, 128), jnp.float32)

HEAD_DIM = 64
PAIR = 2 * HEAD_DIM
BRANCHES = ((128, 1), (512, 4), (2048, 16))
SIDE = 64
EPS = 1e-6
NEG_INF = -1e30
N_CHIPS = 4
N_DEV = 8

ADAM_LR = 0.001
ADAM_B1 = 0.9
ADAM_B2 = 0.999
ADAM_EPS = 1e-08
ADAM_WD = 0.01
ADAM_STEP = 10

VMEM_LIMIT_BYTES = 56 * 1024 * 1024
ROW_TILE = 256
COL_TILE = 512
CONV_TILE = 256
ATT_BQ = 128
ATT_KW = ATT_BQ + 2 * SIDE
ATT_UNROLL = 4
ATT_UNROLL_BWD = 4
SMALL_ALIGN = 1024


def _params(semantics=None):
    kw = {"vmem_limit_bytes": VMEM_LIMIT_BYTES}
    if semantics is not None:
        kw["dimension_semantics"] = semantics
    return pltpu.CompilerParams(**kw)


def _silu(z):
    return z * jax.nn.sigmoid(z)


def _silu_grad(z):
    s = jax.nn.sigmoid(z)
    return s * (1.0 + z * (1.0 - s))


def _my_place():
    return lax.axis_index("x"), lax.axis_index("y"), lax.axis_index("c")


def _flip(a, bit):
    return 1 - a if bit else a


def _chip_of(x, y):
    return 2 * x + y


def _allgather8(v, name, after=()):
    rows_per, n = v.shape

    def body(v_ref, *rest):
        out_ref, send_sems, recv_sems = rest[len(after):]
        x, y, c = _my_place()
        me = 4 * x + 2 * y + c

        def rows(idx):
            return out_ref.at[pl.ds(pl.multiple_of(idx * rows_per, rows_per), rows_per), :]

        out_ref[pl.ds(pl.multiple_of(me * rows_per, rows_per), rows_per), :] = v_ref[...]
        copies = []
        for k in range(1, N_DEV):
            peer = (_flip(x, k & 4), _flip(y, k & 2), _flip(c, k & 1))
            cp = pltpu.make_async_remote_copy(
                src_ref=v_ref, dst_ref=rows(me), send_sem=send_sems.at[k - 1], recv_sem=recv_sems.at[k - 1],
                device_id=peer, device_id_type=MESH)
            cp.start()
            copies.append((cp, peer))
        for k, (cp, peer) in enumerate(copies):
            src = 4 * peer[0] + 2 * peer[1] + peer[2]
            pltpu.make_async_remote_copy(
                src_ref=v_ref, dst_ref=rows(src), send_sem=send_sems.at[k], recv_sem=recv_sems.at[k],
                device_id=peer, device_id_type=MESH).wait_recv()
        for cp, _ in copies:
            cp.wait_send()

    return pl.pallas_call(
        body, name=name,
        out_shape=jax.ShapeDtypeStruct((N_DEV * rows_per, n), v.dtype),
        in_specs=[VMEM] + [ANY] * len(after), out_specs=VMEM,
        scratch_shapes=[pltpu.SemaphoreType.DMA((N_DEV - 1,)), pltpu.SemaphoreType.DMA((N_DEV - 1,))],
    )(v, *after)


def _allgather8_start(v, me, name):
    rows_per, n = v.shape
    land = lax.dynamic_update_slice(jnp.zeros((N_DEV * rows_per, n), v.dtype), v, (me * rows_per, 0))

    def body(v_ref, land_ref, send_sems, recv_sems, v_thru, land_thru, token_ref):
        del v_thru, land_thru
        x, y, c = _my_place()
        mine = land_ref.at[pl.ds(pl.multiple_of((4 * x + 2 * y + c) * rows_per, rows_per), rows_per), :]
        for k in range(1, N_DEV):
            peer = (_flip(x, k & 4), _flip(y, k & 2), _flip(c, k & 1))
            pltpu.make_async_remote_copy(
                src_ref=v_ref, dst_ref=mine, send_sem=send_sems.at[k - 1], recv_sem=recv_sems.at[k - 1],
                device_id=peer, device_id_type=MESH).start()
        token_ref[...] = jnp.zeros(token_ref.shape, F32)

    sems = pltpu.SemaphoreType.DMA((N_DEV - 1,))
    return pl.pallas_call(
        body, name=name,
        out_shape=(sems, sems, jax.ShapeDtypeStruct(v.shape, v.dtype), jax.ShapeDtypeStruct(land.shape, land.dtype), TOKEN),
        in_specs=[HBM, HBM], out_specs=(SEM, SEM, HBM, HBM, VMEM),
        input_output_aliases={0: 2, 1: 3},
        compiler_params=pltpu.CompilerParams(has_side_effects=EFFECT),
    )(pltpu.with_memory_space_constraint(v, pltpu.HBM), pltpu.with_memory_space_constraint(land, pltpu.HBM))


def _allgather8_wait(send_sems, recv_sems, v, land, after, name):
    rows_per = v.shape[0]

    def body(v_ref, land_ref, send_ref, recv_ref, *rest):
        del rest
        x, y, c = _my_place()
        for k in range(1, N_DEV):
            peer = (_flip(x, k & 4), _flip(y, k & 2), _flip(c, k & 1))
            src = 4 * peer[0] + 2 * peer[1] + peer[2]
            cp = pltpu.make_async_remote_copy(
                src_ref=v_ref, dst_ref=land_ref.at[pl.ds(pl.multiple_of(src * rows_per, rows_per), rows_per), :],
                send_sem=send_ref.at[k - 1], recv_sem=recv_ref.at[k - 1], device_id=peer, device_id_type=MESH)
            cp.wait_send()
            cp.wait_recv()

    return pl.pallas_call(
        body, name=name,
        out_shape=(jax.ShapeDtypeStruct(v.shape, v.dtype), jax.ShapeDtypeStruct(land.shape, land.dtype)),
        in_specs=[HBM, HBM, SEM, SEM] + [ANY] * len(after), out_specs=(HBM, HBM),
        input_output_aliases={0: 0, 1: 1},
        compiler_params=pltpu.CompilerParams(has_side_effects=EFFECT),
    )(v, land, send_sems, recv_sems, *after)[1]


def _half_rows(ref, chip, which, half):
    return ref.at[chip, pl.ds(pl.multiple_of(which * half, half), half), :]


def _ici_peers(x, y, c):
    peers = [(_flip(x, k & 2), _flip(y, k & 1), c) for k in (1, 2, 3)]
    return [(peer, _chip_of(peer[0], peer[1])) for peer in peers]


def _part_rows(ref, chip, core, part):
    quarter = ref.shape[1] // 4
    return ref.at[chip, pl.ds(pl.multiple_of((2 * core + part) * quarter, quarter), quarter), :]


def _neighbours(x, y, c):
    return [((x, 1 - y, c), _chip_of(x, 1 - y)), ((1 - x, y, c), _chip_of(1 - x, y)),
            ((1 - x, 1 - y, c), _chip_of(1 - x, 1 - y))]


def _start_direct(buf, send_sems, recv_sems):
    x, y, c = _my_place()
    me = _chip_of(x, y)
    for n, (peer, _) in enumerate(_neighbours(x, y, c)[0:2]):
        for part in ((0, 1), (1, 0))[n]:
            piece = _part_rows(buf, me, c, part)
            pltpu.make_async_remote_copy(
                src_ref=piece, dst_ref=piece, send_sem=send_sems.at[2 * n + part], recv_sem=recv_sems.at[2 * n + part],
                device_id=peer, device_id_type=MESH).start()


def _relay(buf, recv_sems, relay_send, relay_recv):
    x, y, c = _my_place()
    nbrs = _neighbours(x, y, c)
    for n in range(2):
        part = n
        piece = _part_rows(buf, nbrs[n][1], c, part)
        pltpu.make_async_remote_copy(
            src_ref=piece, dst_ref=piece, send_sem=relay_send.at[part], recv_sem=recv_sems.at[2 * n + part],
            device_id=nbrs[n][0], device_id_type=MESH).wait_recv()
        pltpu.make_async_remote_copy(
            src_ref=piece, dst_ref=piece, send_sem=relay_send.at[part], recv_sem=relay_recv.at[part],
            device_id=nbrs[1 - n][0], device_id_type=MESH).start()


def _gather_start(win_slots, after):
    def body(win_in, after_ref, win_ref, send_sems, recv_sems, token_ref):
        del win_in, after_ref
        _start_direct(win_ref, send_sems, recv_sems)
        token_ref[...] = jnp.zeros(token_ref.shape, F32)

    sems = pltpu.SemaphoreType.DMA((4,))
    return pl.pallas_call(
        body, name="gather_start",
        out_shape=(jax.ShapeDtypeStruct(win_slots.shape, win_slots.dtype), sems, sems, TOKEN),
        in_specs=[HBM, ANY], out_specs=(HBM, SEM, SEM, VMEM),
        input_output_aliases={0: 0},
        compiler_params=pltpu.CompilerParams(has_side_effects=EFFECT),
    )(win_slots, after)


def _gather_relay_in(win, wout_slots, recv_in, after):
    def body(win_in, wout_in, recv_in_ref, after_ref, win_ref, wout_ref, relay_send, relay_recv, send_out, recv_out):
        del win_in, wout_in, after_ref
        _relay(win_ref, recv_in_ref, relay_send, relay_recv)
        _start_direct(wout_ref, send_out, recv_out)

    two, four = pltpu.SemaphoreType.DMA((2,)), pltpu.SemaphoreType.DMA((4,))
    return pl.pallas_call(
        body, name="gather_relay_w_in",
        out_shape=(jax.ShapeDtypeStruct(win.shape, win.dtype), jax.ShapeDtypeStruct(wout_slots.shape, wout_slots.dtype),
                   two, two, four, four),
        in_specs=[HBM, HBM, SEM, ANY], out_specs=(HBM, HBM, SEM, SEM, SEM, SEM),
        input_output_aliases={0: 0, 1: 1},
        compiler_params=pltpu.CompilerParams(has_side_effects=EFFECT),
    )(win, wout_slots, recv_in, after)


def _gather_relay_out(wout, recv_out, after):
    def body(wout_in, recv_out_ref, after_ref, wout_ref, relay_send, relay_recv):
        del wout_in, after_ref
        _relay(wout_ref, recv_out_ref, relay_send, relay_recv)

    two = pltpu.SemaphoreType.DMA((2,))
    return pl.pallas_call(
        body, name="gather_relay_w_out",
        out_shape=(jax.ShapeDtypeStruct(wout.shape, wout.dtype), two, two),
        in_specs=[HBM, SEM, ANY], out_specs=(HBM, SEM, SEM),
        input_output_aliases={0: 0},
        compiler_params=pltpu.CompilerParams(has_side_effects=EFFECT),
    )(wout, recv_out, after)


def _gather_wait_direct(buf, send_sems, recv_sems, after, name):
    def body(buf_in, send_ref, recv_ref, after_ref, buf_ref):
        del buf_in, after_ref
        x, y, c = _my_place()
        me = _chip_of(x, y)
        for n, (peer, chip) in enumerate(_neighbours(x, y, c)[0:2]):
            second = 1 - n
            pltpu.make_async_remote_copy(
                src_ref=_part_rows(buf_ref, me, c, second), dst_ref=_part_rows(buf_ref, chip, c, second),
                send_sem=send_ref.at[2 * n + second], recv_sem=recv_ref.at[2 * n + second],
                device_id=peer, device_id_type=MESH).wait_recv()
            for part in range(2):
                piece = _part_rows(buf_ref, me, c, part)
                pltpu.make_async_remote_copy(
                    src_ref=piece, dst_ref=piece, send_sem=send_ref.at[2 * n + part], recv_sem=recv_ref.at[2 * n + part],
                    device_id=peer, device_id_type=MESH).wait_send()

    return pl.pallas_call(
        body, name=name,
        out_shape=jax.ShapeDtypeStruct(buf.shape, buf.dtype),
        in_specs=[HBM, SEM, SEM, ANY], out_specs=HBM,
        input_output_aliases={0: 0},
        compiler_params=pltpu.CompilerParams(has_side_effects=EFFECT),
    )(buf, send_sems, recv_sems, after)


def _gather_wait_relayed(buf, relay_send, relay_recv, after, name):
    def body(buf_in, rsend_ref, rrecv_ref, after_ref, buf_ref):
        del buf_in, after_ref
        x, y, c = _my_place()
        nbrs = _neighbours(x, y, c)
        for n in range(2):
            relayed = _part_rows(buf_ref, nbrs[n][1], c, n)
            cp = pltpu.make_async_remote_copy(
                src_ref=relayed, dst_ref=_part_rows(buf_ref, nbrs[2][1], c, n),
                send_sem=rsend_ref.at[n], recv_sem=rrecv_ref.at[n], device_id=nbrs[1 - n][0], device_id_type=MESH)
            cp.wait_recv()
            cp.wait_send()

    return pl.pallas_call(
        body, name=name,
        out_shape=jax.ShapeDtypeStruct(buf.shape, buf.dtype),
        in_specs=[HBM, SEM, SEM, ANY], out_specs=HBM,
        input_output_aliases={0: 0},
        compiler_params=pltpu.CompilerParams(has_side_effects=EFFECT),
    )(buf, relay_send, relay_recv, after)


def _forward_halves(buf, which, name):
    half = buf.shape[1] // 2

    def body(buf_in, buf_ref, send_sems, recv_sems):
        del buf_in
        x, y, c = _my_place()
        sibling = (x, y, 1 - c)
        chips = [_neighbours(x, y, c)[n][1] for n in which]
        started = []
        for k, src_chip in enumerate(chips):
            landed = _half_rows(buf_ref, src_chip, c, half)
            fw = pltpu.make_async_remote_copy(
                src_ref=landed, dst_ref=landed, send_sem=send_sems.at[k], recv_sem=recv_sems.at[k],
                device_id=sibling, device_id_type=MESH)
            fw.start()
            started.append(fw)
        for k, src_chip in enumerate(chips):
            other = _half_rows(buf_ref, src_chip, 1 - c, half)
            pltpu.make_async_remote_copy(
                src_ref=other, dst_ref=other, send_sem=send_sems.at[k], recv_sem=recv_sems.at[k],
                device_id=sibling, device_id_type=MESH).wait_recv()
        for fw in started:
            fw.wait_send()

    return pl.pallas_call(
        body, name=name,
        out_shape=jax.ShapeDtypeStruct(buf.shape, buf.dtype),
        in_specs=[HBM], out_specs=HBM,
        input_output_aliases={0: 0},
        scratch_shapes=[pltpu.SemaphoreType.DMA((len(which),))] * 2,
    )(buf)


def _dw_swapped(a, b, row_chunks, col_chunks, name):
    r, t = a.shape
    c_all = b.shape[1]
    chunks = row_chunks * col_chunks
    rq, cq = r // row_chunks, c_all // col_chunks
    half = rq // 2
    tn = COL_TILE
    nt = cq // tn
    steps = col_chunks * nt

    def body(a_ref, b_ref, mine_ref, sib_ref, stage, send_sems, recv_sems):
        x, y, c = _my_place()
        j, n = pl.program_id(0), pl.program_id(1)
        step = j * nt + n
        slot = step % 2
        res = jnp.dot(a_ref[...], b_ref[...], preferred_element_type=F32).astype(BF16)

        def landing(jj, nn):
            cols = pl.ds(pl.multiple_of(nn * tn, tn), tn)
            return sib_ref.at[:, :, cols] if col_chunks == 1 else sib_ref.at[pl.ds(jj, 1), :, cols]

        def copy(slot_, step_, jj, nn):
            return pltpu.make_async_remote_copy(
                src_ref=stage.at[slot_], dst_ref=landing(jj, nn), send_sem=send_sems.at[slot_],
                recv_sem=recv_sems.at[step_], device_id=(x, y, 1 - c), device_id_type=MESH)

        @pl.when(step >= 2)
        def _():
            copy(slot, step, j, n).wait_send()

        for q in range(row_chunks):
            lo = res[q * rq:q * rq + half, :]
            hi = res[q * rq + half:(q + 1) * rq, :]
            mine_ref[q] = jnp.where(c == 0, lo, hi)
            stage[slot, q] = jnp.where(c == 0, hi, lo)
        copy(slot, step, j, n).start()

        @pl.when(step == steps - 1)
        def _():
            for s in range(max(steps - 2, 0), steps):
                copy(s % 2, s, j, n).wait_send()
            for s in range(steps):
                copy(s % 2, s, j, n).wait_recv()

    shape = jax.ShapeDtypeStruct((chunks, half, cq), BF16)
    return pl.pallas_call(
        body, name=name, grid=(col_chunks, nt),
        out_shape=(shape, shape),
        in_specs=[pl.BlockSpec((r, t), lambda j, n: (0, 0)), pl.BlockSpec((t, tn), lambda j, n: (0, j * nt + n))],
        out_specs=(pl.BlockSpec((row_chunks, half, tn), lambda j, n: (j, 0, n)), ANY),
        scratch_shapes=[pltpu.VMEM((2, row_chunks, half, tn), BF16), pltpu.SemaphoreType.DMA((2,)),
                        pltpu.SemaphoreType.DMA((steps,))],
        compiler_params=_params(("arbitrary", "arbitrary")),
    )(a, b)


def _owners_start(csum, name):
    land = pltpu.with_memory_space_constraint(lax.empty((N_CHIPS - 1,) + csum.shape[1:], csum.dtype), pltpu.HBM)

    def body(csum_ref, land_ref, send_sems, recv_sems, csum_thru, land_thru, token_ref):
        del csum_thru, land_thru
        x, y, c = _my_place()
        for k, (peer, owner) in enumerate(_ici_peers(x, y, c)):
            pltpu.make_async_remote_copy(
                src_ref=csum_ref.at[owner], dst_ref=land_ref.at[k], send_sem=send_sems.at[k], recv_sem=recv_sems.at[k],
                device_id=peer, device_id_type=MESH).start()
        token_ref[...] = jnp.zeros(token_ref.shape, F32)

    sems = pltpu.SemaphoreType.DMA((N_CHIPS - 1,))
    return pl.pallas_call(
        body, name=name,
        out_shape=(sems, sems, jax.ShapeDtypeStruct(csum.shape, csum.dtype),
                   jax.ShapeDtypeStruct(land.shape, land.dtype), TOKEN),
        in_specs=[HBM, HBM], out_specs=(SEM, SEM, HBM, HBM, VMEM),
        input_output_aliases={0: 2, 1: 3},
        compiler_params=pltpu.CompilerParams(has_side_effects=EFFECT),
    )(pltpu.with_memory_space_constraint(csum, pltpu.HBM), land)


def _owners_wait(send_sems, recv_sems, csum, land, after, name):
    def body(csum_ref, land_ref, send_ref, recv_ref, *rest):
        del rest
        x, y, c = _my_place()
        for k, (peer, owner) in enumerate(_ici_peers(x, y, c)):
            cp = pltpu.make_async_remote_copy(
                src_ref=csum_ref.at[owner], dst_ref=land_ref.at[k], send_sem=send_ref.at[k], recv_sem=recv_ref.at[k],
                device_id=peer, device_id_type=MESH)
            cp.wait_send()
            cp.wait_recv()

    return pl.pallas_call(
        body, name=name,
        out_shape=(jax.ShapeDtypeStruct(csum.shape, csum.dtype), jax.ShapeDtypeStruct(land.shape, land.dtype)),
        in_specs=[HBM, HBM, SEM, SEM] + [ANY] * len(after), out_specs=(HBM, HBM),
        input_output_aliases={0: 0, 1: 1},
        compiler_params=pltpu.CompilerParams(has_side_effects=EFFECT),
    )(csum, land, send_sems, recv_sems, *after)[1]


def _join_halves(full, name):
    rows = full.shape[0] // 2

    def body(full_in, full_ref, send_sem, recv_sem):
        del full_in
        x, y, c = _my_place()
        sibling = (x, y, 1 - c)
        mine = full_ref.at[pl.ds(pl.multiple_of(c * rows, rows), rows), :]
        theirs = full_ref.at[pl.ds(pl.multiple_of((1 - c) * rows, rows), rows), :]
        cp = pltpu.make_async_remote_copy(
            src_ref=mine, dst_ref=mine, send_sem=send_sem, recv_sem=recv_sem, device_id=sibling, device_id_type=MESH)
        cp.start()
        pltpu.make_async_remote_copy(
            src_ref=theirs, dst_ref=theirs, send_sem=send_sem, recv_sem=recv_sem,
            device_id=sibling, device_id_type=MESH).wait_recv()
        cp.wait_send()

    return pl.pallas_call(
        body, name=name,
        out_shape=jax.ShapeDtypeStruct(full.shape, full.dtype),
        in_specs=[HBM], out_specs=HBM,
        input_output_aliases={0: 0},
        scratch_shapes=[pltpu.SemaphoreType.DMA, pltpu.SemaphoreType.DMA],
    )(full)


def _cast_into_slot(place, w, name):
    rows, cols = w.shape
    tr = min(rows, ROW_TILE)

    def body(place_ref, w_ref, o_ref):
        del place_ref
        o_ref[...] = w_ref[...].astype(BF16)

    grid_spec = pltpu.PrefetchScalarGridSpec(
        num_scalar_prefetch=1, grid=(rows // tr,),
        in_specs=[pl.BlockSpec((tr, cols), lambda i, p: (i, 0))],
        out_specs=pl.BlockSpec((None, tr, cols), lambda i, p: (p[0], i, 0)))
    return pl.pallas_call(
        body, name=name, grid_spec=grid_spec,
        out_shape=jax.ShapeDtypeStruct((N_CHIPS, rows, cols), BF16),
        compiler_params=_params(("parallel",)),
    )(place, w)


def _ada_partial(c_all, w_ada):
    d_model, wa = w_ada.shape
    tn = 512 if wa % 512 == 0 else 256

    def body(c_ref, w_ref, o_ref):
        o_ref[...] = jnp.dot(_silu(c_ref[...]), w_ref[...], precision=lax.Precision.HIGHEST,
                             preferred_element_type=F32)

    return pl.pallas_call(
        body, name="ada_partial", grid=(wa // tn,),
        out_shape=jax.ShapeDtypeStruct((N_DEV, wa), F32),
        in_specs=[pl.BlockSpec((N_DEV, d_model), lambda i: (0, 0)), pl.BlockSpec((d_model, tn), lambda i: (0, i))],
        out_specs=pl.BlockSpec((N_DEV, tn), lambda i: (0, i)),
        compiler_params=_params(("parallel",)),
    )(c_all, w_ada)


def _prenorm(x, mod, g_pre):
    t, d = x.shape
    tb = ROW_TILE

    def body(x_ref, mod_ref, g_ref, h_ref, ht_ref):
        xv = x_ref[...]
        r = lax.rsqrt(jnp.mean(xv * xv, axis=-1, keepdims=True) + EPS)
        h = (xv * r) * g_ref[...] * (1.0 + mod_ref[1:2, :]) + mod_ref[0:1, :]
        h_ref[...] = h.astype(BF16)
        ht_ref[...] = h.T.astype(BF16)

    return pl.pallas_call(
        body, name="prenorm", grid=(t // tb,),
        out_shape=(jax.ShapeDtypeStruct((t, d), BF16), jax.ShapeDtypeStruct((d, t), BF16)),
        in_specs=[pl.BlockSpec((tb, d), lambda i: (i, 0)), pl.BlockSpec((3, d), lambda i: (0, 0)),
                  pl.BlockSpec((1, d), lambda i: (0, 0))],
        out_specs=(pl.BlockSpec((tb, d), lambda i: (i, 0)), pl.BlockSpec((d, tb), lambda i: (0, i))),
        compiler_params=_params(("parallel",)),
    )(x, mod, g_pre)


def _proj_chunks(proj, h, w, chunks, name):
    t, d = h.shape
    ws = w.shape[-1]
    tn = COL_TILE
    nt = ws // tn

    def body(chunk_ref, *refs):
        del chunk_ref
        a_ref, b_ref, o_ref = refs[-3:]
        o_ref[...] = jnp.dot(a_ref[...], b_ref[...].astype(BF16), preferred_element_type=F32).astype(BF16)

    if w.ndim == 3:
        w_spec = pl.BlockSpec((None, d, tn), lambda i, n, ch: (ch[i], 0, n))
    else:
        w_spec = pl.BlockSpec((d, tn), lambda i, n, ch: (0, n))
    first = proj is None
    grid_spec = pltpu.PrefetchScalarGridSpec(
        num_scalar_prefetch=1, grid=(chunks.shape[0], nt),
        in_specs=([] if first else [HBM]) + [pl.BlockSpec((t, d), lambda i, n, ch: (0, 0)), w_spec],
        out_specs=pl.BlockSpec((t, tn), lambda i, n, ch: (0, ch[i] * nt + n)))
    return pl.pallas_call(
        body, name=name, grid_spec=grid_spec,
        out_shape=jax.ShapeDtypeStruct((t, N_CHIPS * ws), BF16),
        input_output_aliases={} if first else {1: 0},
        compiler_params=_params(("parallel", "parallel")),
    )(*([chunks] if first else [chunks, proj]), h, w)


def _shift_rows(a, rows):
    idx = lax.broadcasted_iota(jnp.int32, a.shape, 0)
    prev = jnp.where(idx == 0, 0.0, pltpu.roll(a, 1, 0))
    nxt = jnp.where(idx == rows - 1, 0.0, pltpu.roll(a, rows - 1, 0))
    return prev, nxt


def _conv_fwd(conv_proj, conv_w, conv_b, dc):
    t = conv_proj.shape[0]
    ct = CONV_TILE
    nct = dc // ct

    def body(u_ref, cg_ref, w_ref, b_ref, co_ref):
        a = cg_ref[...].astype(F32) * u_ref[...].astype(F32)
        prev, nxt = _shift_rows(a, t)
        co_ref[...] = w_ref[0:1, :] * prev + w_ref[1:2, :] * a + w_ref[2:3, :] * nxt + b_ref[...]

    return pl.pallas_call(
        body, name="conv_fwd", grid=(nct,),
        out_shape=jax.ShapeDtypeStruct((t, dc), F32),
        in_specs=[pl.BlockSpec((t, ct), lambda i: (0, i)), pl.BlockSpec((t, ct), lambda i: (0, 2 * nct + i)),
                  pl.BlockSpec((3, ct), lambda i: (0, i)), pl.BlockSpec((1, ct), lambda i: (0, i))],
        out_specs=pl.BlockSpec((t, ct), lambda i: (0, i)),
        compiler_params=_params(("parallel",)),
    )(conv_proj, conv_proj, conv_w, conv_b)


def _to_residue_major(src_ref, dst_ref, r):
    seq = src_ref.shape[0] // r
    for res in range(r):
        dst_ref[res * seq:(res + 1) * seq, :] = src_ref[pl.ds(res, seq, stride=r), :].astype(dst_ref.dtype)


def _branch_operands(token_refs, stage, dil, r):
    if r == 1:
        return list(token_refs)
    for i, ref in enumerate(token_refs):
        stage[...] = ref[...].astype(F32)
        _to_residue_major(stage, dil.at[i], r)
    return [dil.at[i] for i in range(len(token_refs))]


def _scaled_queries(q):
    return (q.astype(F32) * (HEAD_DIM ** -0.5)).astype(BF16)


BLOCK_SHIFTS = (0, -SIDE, None)


def _band_bias(rel, slope):
    arel = jnp.abs(rel)
    return jnp.where(arel <= SIDE, arel.astype(F32) * slope, NEG_INF)


def _fill_bias_tiles(bias_ref, sl_ref, r, kw):
    base = lax.broadcasted_iota(jnp.int32, (ATT_BQ, kw), 1) - lax.broadcasted_iota(jnp.int32, (ATT_BQ, kw), 0)
    for hh in range(2):
        slope = -(sl_ref[hh:hh + 1, 0:kw] * float(r))
        for e, shift in enumerate(BLOCK_SHIFTS):
            shift = ATT_BQ - kw if shift is None else shift
            bias_ref[hh, e, :, 0:kw] = _band_bias(base + shift, slope)


def _fill_stacked_bias_tiles(bias_ref, sl_ref, r, kw):
    base = lax.broadcasted_iota(jnp.int32, (kw, ATT_BQ), 0) - lax.broadcasted_iota(jnp.int32, (kw, ATT_BQ), 1)
    for hh in range(2):
        slope = -(sl_ref[hh:hh + 1, 0:ATT_BQ] * float(r))
        for e, shift in enumerate(BLOCK_SHIFTS):
            shift = ATT_BQ - kw if shift is None else shift
            bias_ref[e, 0:kw, hh * ATT_BQ:(hh + 1) * ATT_BQ] = _band_bias(base + shift, slope)


def _first_head_lanes():
    return lax.broadcasted_iota(jnp.int32, (1, PAIR), 1) < HEAD_DIM


def _only_head(x, first, hh):
    return jnp.where(first if hh == 0 else jnp.logical_not(first), x, jnp.zeros_like(x))


def _block_place(g, seq_len, kw):
    nqb = seq_len // ATT_BQ
    if nqb == 1:
        row = pl.multiple_of(g * ATT_BQ, ATT_BQ)
        return row, row, 0
    res = g // nqb
    qb = g - res * nqb
    q0 = qb * ATT_BQ
    ks = jnp.clip(q0 - SIDE, 0, seq_len - kw)
    edge = jnp.where(qb == 0, 0, jnp.where(qb == nqb - 1, 2, 1))
    return (pl.multiple_of(res * seq_len + q0, ATT_BQ), pl.multiple_of(res * seq_len + ks, SIDE), edge)


def _qkv_specs(dc, da, t, index):
    return [pl.BlockSpec((t, PAIR), functools.partial(index, (4 * dc + comp * da) // PAIR)) for comp in range(3)]


def _attn_fwd(proj, slopes, dc, da):
    t = proj.shape[0]
    hp = da // PAIR
    n_blocks = t // ATT_BQ

    def body(q_ref, k_ref, v_ref, sl_ref, o_ref, lse_ref, stage, dil, bias, o_res, l_res, o_tok, l_tok):
        for b, (_, r) in enumerate(BRANCHES):
            seq_len = t // r
            kw = min(ATT_KW, seq_len)
            ops = _branch_operands([q_ref, k_ref, v_ref], stage, dil, r)
            _fill_bias_tiles(bias, sl_ref, r, kw)
            o_dst, l_dst = (o_tok.at[b], l_tok.at[b]) if r == 1 else (o_res, l_res)
            first = _first_head_lanes()

            def blocks(trip, carry, seq_len=seq_len, kw=kw, o_dst=o_dst, l_dst=l_dst, first=first, ops=ops):
                nt = (((1,), (1,)), ((), ()))
                places = [_block_place(trip * ATT_UNROLL + i, seq_len, kw) for i in range(ATT_UNROLL)]
                ones = jnp.ones((kw, PAIR), BF16)
                chains = [(i, hh) for i in range(ATT_UNROLL) for hh in range(2)]
                qs = [_scaled_queries(ops[0][pl.ds(qrow, ATT_BQ), :]) for qrow, _, _ in places]
                ks = [ops[1][pl.ds(krow, kw), :] for _, krow, _ in places]
                vs = [ops[2][pl.ds(krow, kw), :] for _, krow, _ in places]
                ss = [lax.dot_general(_only_head(qs[i], first, hh), ks[i], nt, preferred_element_type=F32)
                      + bias[hh, places[i][2], :, 0:kw] for i, hh in chains]
                tops = [jnp.max(s, axis=-1, keepdims=True) for s in ss]
                ps = [jnp.exp(s - m).astype(BF16) for s, m in zip(ss, tops)]
                parts = [jnp.dot(p, jnp.concatenate([_only_head(vs[i], first, hh), _only_head(ones, first, hh)], axis=1),
                                 preferred_element_type=F32) for p, (i, hh) in zip(ps, chains)]
                for i, (qrow, _, _) in enumerate(places):
                    both = parts[2 * i] + parts[2 * i + 1]
                    den = both[:, PAIR:]
                    o_dst[pl.ds(qrow, ATT_BQ), :] = both[:, 0:PAIR] / den
                    l_dst[pl.ds(qrow, ATT_BQ), :] = jnp.where(first, tops[2 * i], tops[2 * i + 1]) + jnp.log(den)
                return carry

            lax.fori_loop(0, n_blocks // ATT_UNROLL, blocks, 0)
            if r > 1:
                for res in range(r):
                    rows = slice(res * seq_len, (res + 1) * seq_len)
                    o_tok[b, pl.ds(res, seq_len, stride=r), :] = o_res[rows, :]
                    l_tok[b, pl.ds(res, seq_len, stride=r), :] = l_res[rows, :]

        def merge(i, carry):
            rows = pl.ds(pl.multiple_of(i * ROW_TILE, ROW_TILE), ROW_TILE)
            la, lb, lc = l_tok[0, rows, :], l_tok[1, rows, :], l_tok[2, rows, :]
            m = jnp.maximum(jnp.maximum(la, lb), lc)
            wa, wb, wc = jnp.exp(la - m), jnp.exp(lb - m), jnp.exp(lc - m)
            den = wa + wb + wc
            o_ref[rows, :] = (wa * o_tok[0, rows, :] + wb * o_tok[1, rows, :] + wc * o_tok[2, rows, :]) * (1.0 / den)
            lse_ref[rows, :] = m + jnp.log(den)
            return carry

        lax.fori_loop(0, t // ROW_TILE, merge, 0)

    pair_spec = pl.BlockSpec((None, t, PAIR), lambda h: (h, 0, 0))
    return pl.pallas_call(
        body, name="attn_fwd", grid=(hp,),
        out_shape=(jax.ShapeDtypeStruct((hp, t, PAIR), F32), jax.ShapeDtypeStruct((hp, t, PAIR), F32)),
        in_specs=_qkv_specs(dc, da, t, lambda first, h: (0, first + h))
        + [pl.BlockSpec((None, 8, ATT_KW), lambda h: (h, 0, 0))],
        out_specs=(pair_spec, pair_spec),
        scratch_shapes=[pltpu.VMEM((t, PAIR), F32), pltpu.VMEM((3, t, PAIR), BF16),
                        pltpu.VMEM((2, 3, ATT_BQ, ATT_KW), F32),
                        pltpu.VMEM((t, PAIR), F32), pltpu.VMEM((t, PAIR), F32),
                        pltpu.VMEM((3, t, PAIR), F32), pltpu.VMEM((3, t, PAIR), F32)],
        compiler_params=_params(("parallel",)),
    )(proj, proj, proj, slopes)


def _attn_bwd(dproj, proj, d_o, lse, delta, slopes, dc, da, after):
    t = proj.shape[0]
    hp = da // PAIR
    n_blocks = t // ATT_BQ

    def all_branches(q_ref, k_ref, v_ref, do_ref, lse_ref, dl_ref, sl_ref,
                     stage, dil, packed, packed_res, row_vecs, bias_t, acc, tot):
        first = _first_head_lanes()
        lane = lax.broadcasted_iota(jnp.int32, (1, PAIR), 1)
        packed[...] = jnp.where((lane & (HEAD_DIM - 1)) < HEAD_DIM // 2, lse_ref[...], dl_ref[...])
        for b, (_, r) in enumerate(BRANCHES):
            seq_len = t // r
            kw = min(ATT_KW, seq_len)
            ops = _branch_operands([q_ref, k_ref, v_ref, do_ref], stage, dil, r)
            scalars = packed
            if r > 1:
                _to_residue_major(packed, packed_res, r)
                scalars = packed_res
            for g in range(n_blocks):
                flipped = scalars[g * ATT_BQ:(g + 1) * ATT_BQ, :].T
                for row in range(4):
                    row_vecs[g, row:row + 1, :] = flipped[row * (HEAD_DIM // 2):row * (HEAD_DIM // 2) + 1, :]
            _fill_stacked_bias_tiles(bias_t, sl_ref, r, kw)
            acc[1] = jnp.zeros((t, PAIR), F32)
            acc[2] = jnp.zeros((t, PAIR), F32)

            def blocks(trip, carry, seq_len=seq_len, kw=kw, ops=ops):
                nt = (((1,), (1,)), ((), ()))
                group = range(ATT_UNROLL_BWD)
                places = [_block_place(trip * ATT_UNROLL_BWD + i, seq_len, kw) for i in group]
                ks, vs, q2s, do2s, lse2s, dl2s = [], [], [], [], [], []
                for i, (qrow, krow, _) in zip(group, places):
                    q = _scaled_queries(ops[0][pl.ds(qrow, ATT_BQ), :])
                    dov = ops[3][pl.ds(qrow, ATT_BQ), :]
                    ks.append(ops[1][pl.ds(krow, kw), :])
                    vs.append(ops[2][pl.ds(krow, kw), :])
                    q2s.append(jnp.concatenate([_only_head(q, first, 0), _only_head(q, first, 1)], axis=0))
                    do2s.append(jnp.concatenate([_only_head(dov, first, 0), _only_head(dov, first, 1)], axis=0))
                    rows = row_vecs[trip * ATT_UNROLL_BWD + i]
                    lse2s.append(jnp.concatenate([rows[0:1, :], rows[2:3, :]], axis=1))
                    dl2s.append(jnp.concatenate([rows[1:2, :], rows[3:4, :]], axis=1))
                s_ts = [lax.dot_general(ks[i], q2s[i], nt, preferred_element_type=F32) for i in group]
                dp_ts = [lax.dot_general(vs[i], do2s[i], nt, preferred_element_type=F32) for i in group]
                p_ts = [jnp.exp(s_ts[i] + bias_t[places[i][2], 0:kw, :] - lse2s[i]) for i in group]
                ds_ts = [p_ts[i] * (dp_ts[i] - dl2s[i]) for i in group]
                dvs = [jnp.dot(p_ts[i].astype(BF16), do2s[i], preferred_element_type=F32) for i in group]
                dks = [jnp.dot(ds_ts[i].astype(BF16), q2s[i], preferred_element_type=F32) for i in group]
                dss = [ds_ts[i].T.astype(BF16) for i in group]
                dqs = [jnp.dot(dss[i][0:ATT_BQ, :], _only_head(ks[i], first, 0), preferred_element_type=F32)
                       + jnp.dot(dss[i][ATT_BQ:2 * ATT_BQ, :], _only_head(ks[i], first, 1), preferred_element_type=F32)
                       for i in group]
                for i, (qrow, krow, _) in zip(group, places):
                    acc[0, pl.ds(qrow, ATT_BQ), :] = dqs[i] * (HEAD_DIM ** -0.5)
                    acc[1, pl.ds(krow, kw), :] += dks[i]
                    acc[2, pl.ds(krow, kw), :] += dvs[i]
                return carry

            lax.fori_loop(0, n_blocks // ATT_UNROLL_BWD, blocks, 0)
            for comp in range(3):
                if r == 1:
                    tot[comp] = acc[comp]
                else:
                    for res in range(r):
                        tok = pl.ds(res, seq_len, stride=r)
                        tot[comp, tok, :] = tot[comp, tok, :] + acc[comp, res * seq_len:(res + 1) * seq_len, :]

    first_q = (4 * dc) // PAIR

    def body(dproj_in, q_ref, k_ref, v_ref, do_ref, lse_ref, dl_ref, sl_ref, after_ref, out_ref, *scratch):
        del dproj_in, after_ref
        work, out_stage, out_sems = scratch[:-2], scratch[-2], scratch[-1]
        h = pl.program_id(0)
        all_branches(q_ref, k_ref, v_ref, do_ref, lse_ref, dl_ref, sl_ref, *work)

        def out_copy(comp):
            cols = pl.ds(pl.multiple_of((first_q + comp * hp + h) * PAIR, PAIR), PAIR)
            return pltpu.make_async_copy(out_stage.at[comp], out_ref.at[:, cols], out_sems.at[comp])

        @pl.when(h > 0)
        def _():
            for comp in range(3):
                out_copy(comp).wait()

        for comp in range(3):
            out_stage[comp] = work[-1][comp].astype(BF16)
            out_copy(comp).start()

        @pl.when(h == hp - 1)
        def _():
            for comp in range(3):
                out_copy(comp).wait()

    pair_spec = pl.BlockSpec((None, t, PAIR), lambda h: (h, 0, 0))
    return pl.pallas_call(
        body, name="attn_bwd", grid=(hp,),
        out_shape=jax.ShapeDtypeStruct(dproj.shape, BF16),
        in_specs=[HBM] + _qkv_specs(dc, da, t, lambda first, h: (0, first + h))
        + [pair_spec, pair_spec, pair_spec, pl.BlockSpec((None, 8, ATT_KW), lambda h: (h, 0, 0)), ANY],
        out_specs=ANY,
        input_output_aliases={0: 0},
        scratch_shapes=[pltpu.VMEM((t, PAIR), F32), pltpu.VMEM((4, t, PAIR), BF16),
                        pltpu.VMEM((t, PAIR), F32), pltpu.VMEM((t, PAIR), F32),
                        pltpu.VMEM((n_blocks, 8, ATT_BQ), F32), pltpu.VMEM((3, ATT_KW, 2 * ATT_BQ), F32),
                        pltpu.VMEM((3, t, PAIR), F32), pltpu.VMEM((3, t, PAIR), F32),
                        pltpu.VMEM((3, t, PAIR), BF16), pltpu.SemaphoreType.DMA((3,))],
        compiler_params=_params(("arbitrary",)),
    )(dproj, proj, proj, proj, d_o, lse, delta, slopes, after)


def _mix_fwd(co, proj, o_mix, g_conv, g_attn_pairs):
    t, dc = co.shape
    hp = o_mix.shape[0]
    da = hp * PAIR
    tb = ROW_TILE

    def body(co_ref, bg_ref, zc_ref, za_ref, om_ref, gc_ref, ga_ref, ycat_ref, ycatt_ref):
        p = bg_ref[...].astype(F32) * co_ref[...]
        rc = lax.rsqrt(jnp.mean(p * p, axis=-1, keepdims=True) + EPS)
        yc = (p * rc) * gc_ref[...] * _silu(zc_ref[...].astype(F32))
        ycat_ref[:, 0:dc] = yc.astype(BF16)
        ycatt_ref[0:dc, :] = yc.T.astype(BF16)
        ssq = jnp.zeros((tb, 1), F32)
        for h in range(hp):
            o = om_ref[h]
            ssq = ssq + jnp.sum(o * o, axis=-1, keepdims=True)
        ra = lax.rsqrt(ssq * (1.0 / da) + EPS)
        for h in range(hp):
            ya = (om_ref[h] * ra) * ga_ref[h] * _silu(za_ref[:, h * PAIR:(h + 1) * PAIR].astype(F32))
            ycat_ref[:, dc + h * PAIR:dc + (h + 1) * PAIR] = ya.astype(BF16)
            ycatt_ref[dc + h * PAIR:dc + (h + 1) * PAIR, :] = ya.T.astype(BF16)

    pair_spec = pl.BlockSpec((hp, tb, PAIR), lambda i: (0, i, 0))
    return pl.pallas_call(
        body, name="mix_fwd", grid=(t // tb,),
        out_shape=(jax.ShapeDtypeStruct((t, dc + da), BF16), jax.ShapeDtypeStruct((dc + da, t), BF16)),
        in_specs=[pl.BlockSpec((tb, dc), lambda i: (i, 0)),
                  pl.BlockSpec((tb, dc), lambda i: (i, 1)),
                  pl.BlockSpec((tb, dc), lambda i: (i, 3)),
                  pl.BlockSpec((tb, da), lambda i: (i, 7)),
                  pair_spec,
                  pl.BlockSpec((1, dc), lambda i: (0, 0)),
                  pl.BlockSpec((hp, 1, PAIR), lambda i: (0, 0, 0))],
        out_specs=(pl.BlockSpec((tb, dc + da), lambda i: (i, 0)), pl.BlockSpec((dc + da, tb), lambda i: (0, i))),
        compiler_params=_params(("parallel",)),
    )(co, proj, proj, proj, o_mix, g_conv, g_attn_pairs)


def _out_fwd_bwd(ycat, woutf, x, target, mod, g_post):
    t, d = x.shape
    n = ycat.shape[1]
    tb = ROW_TILE

    def body(a_ref, w_ref, x_ref, tg_ref, mod_ref, g_ref, dout_ref, dy_ref, acc_ref):
        y = jnp.dot(a_ref[...], w_ref[...], preferred_element_type=F32)
        r = lax.rsqrt(jnp.mean(y * y, axis=-1, keepdims=True) + EPS)
        nh = y * r
        gate = mod_ref[2:3, :]
        nrm = nh * g_ref[...]
        err = x_ref[...] + gate * nrm - tg_ref[...]
        dout = err * (1.0 / d)
        dout_ref[...] = dout
        dn = dout * gate
        a = dn * g_ref[...]
        dy = r * (a - nh * jnp.mean(a * nh, axis=-1, keepdims=True))
        dy_ref[...] = dy.astype(BF16)
        loss = 0.5 * jnp.sum(jnp.sum(err * err, axis=-1, keepdims=True) * (1.0 / d), axis=0, keepdims=True)
        part = jnp.concatenate(
            [jnp.sum(dout * nrm, axis=0, keepdims=True), jnp.sum(dn * nh, axis=0, keepdims=True),
             jnp.broadcast_to(loss, (1, d)), jnp.zeros((5, d), F32)], axis=0)

        @pl.when(pl.program_id(0) == 0)
        def _():
            acc_ref[...] = jnp.zeros(acc_ref.shape, F32)

        acc_ref[...] += part

    return pl.pallas_call(
        body, name="out_fwd_bwd", grid=(t // tb,),
        out_shape=(jax.ShapeDtypeStruct((t, d), F32), jax.ShapeDtypeStruct((t, d), BF16),
                   jax.ShapeDtypeStruct((8, d), F32)),
        in_specs=[pl.BlockSpec((tb, n), lambda i: (i, 0)), pl.BlockSpec((n, d), lambda i: (0, 0)),
                  pl.BlockSpec((tb, d), lambda i: (i, 0)), pl.BlockSpec((tb, d), lambda i: (i, 0)),
                  pl.BlockSpec((3, d), lambda i: (0, 0)), pl.BlockSpec((1, d), lambda i: (0, 0))],
        out_specs=(pl.BlockSpec((tb, d), lambda i: (i, 0)), pl.BlockSpec((tb, d), lambda i: (i, 0)),
                   pl.BlockSpec((8, d), lambda i: (0, 0))),
        compiler_params=_params(("arbitrary",)),
    )(ycat, woutf, x, target, mod, g_post)


def _matmul_nt(a, b, out_dtype, name):
    m, k = a.shape
    n = b.shape[0]
    tn = COL_TILE

    def body(a_ref, b_ref, o_ref):
        o_ref[...] = lax.dot_general(a_ref[...], b_ref[...], (((1,), (1,)), ((), ())),
                                     preferred_element_type=F32).astype(out_dtype)

    return pl.pallas_call(
        body, name=name, grid=(n // tn,),
        out_shape=jax.ShapeDtypeStruct((m, n), out_dtype),
        in_specs=[pl.BlockSpec((m, k), lambda i: (0, 0)), pl.BlockSpec((tn, k), lambda i: (i, 0))],
        out_specs=pl.BlockSpec((m, tn), lambda i: (0, i)),
        compiler_params=_params(("parallel",)),
    )(a, b)


def _mix_bwd(dycat, co, proj, o_mix, g_conv, g_attn_pairs):
    t, dc = co.shape
    hp = o_mix.shape[0]
    da = hp * PAIR
    tb = ROW_TILE

    def body(dy_ref, co_ref, bg_ref, zc_ref, za_ref, om_ref, gc_ref, ga_ref,
             dcp_ref, dco_ref, do_ref, dl_ref, dgc_ref, dga_ref):
        first = pl.program_id(0) == 0
        cov = co_ref[...]
        bg = bg_ref[...].astype(F32)
        zc = zc_ref[...].astype(F32)
        p = bg * cov
        rc = lax.rsqrt(jnp.mean(p * p, axis=-1, keepdims=True) + EPS)
        nh = p * rc
        dyc = dy_ref[:, 0:dc].astype(F32)
        dn = dyc * _silu(zc)
        a = dn * gc_ref[...]
        dp = rc * (a - nh * jnp.mean(a * nh, axis=-1, keepdims=True))
        dcp_ref[:, 0:dc] = jnp.zeros((tb, dc), BF16)
        dcp_ref[:, dc:2 * dc] = (dp * cov).astype(BF16)
        dcp_ref[:, 2 * dc:3 * dc] = jnp.zeros((tb, dc), BF16)
        dcp_ref[:, 3 * dc:4 * dc] = (dyc * nh * gc_ref[...] * _silu_grad(zc)).astype(BF16)
        dcp_ref[:, 4 * dc:4 * dc + 3 * da] = jnp.zeros((tb, 3 * da), BF16)
        dco_ref[...] = dp * bg

        @pl.when(first)
        def _():
            dgc_ref[...] = jnp.zeros(dgc_ref.shape, F32)
            dga_ref[...] = jnp.zeros(dga_ref.shape, F32)

        dgc_ref[...] += jnp.sum(dn * nh, axis=0, keepdims=True)

        ssq = jnp.zeros((tb, 1), F32)
        for h in range(hp):
            o = om_ref[h]
            ssq = ssq + jnp.sum(o * o, axis=-1, keepdims=True)
        ra = lax.rsqrt(ssq * (1.0 / da) + EPS)
        dot_an = jnp.zeros((tb, 1), F32)
        for h in range(hp):
            nha = om_ref[h] * ra
            za = za_ref[:, h * PAIR:(h + 1) * PAIR].astype(F32)
            dya = dy_ref[:, dc + h * PAIR:dc + (h + 1) * PAIR].astype(F32)
            dna = dya * _silu(za)
            dza = (dya * nha * ga_ref[h] * _silu_grad(za)).astype(BF16)
            dcp_ref[:, 4 * dc + 3 * da + h * PAIR:4 * dc + 3 * da + (h + 1) * PAIR] = dza
            dga_ref[h] += jnp.sum(dna * nha, axis=0, keepdims=True)
            dot_an = dot_an + jnp.sum(dna * ga_ref[h] * nha, axis=-1, keepdims=True)
        mean_an = dot_an * (1.0 / da)
        first_head = lax.broadcasted_iota(jnp.int32, (tb, PAIR), 1) < HEAD_DIM
        for h in range(hp):
            o = om_ref[h]
            nha = o * ra
            za = za_ref[:, h * PAIR:(h + 1) * PAIR].astype(F32)
            dya = dy_ref[:, dc + h * PAIR:dc + (h + 1) * PAIR].astype(F32)
            aa = dya * _silu(za) * ga_ref[h]
            d_o = ra * (aa - nha * mean_an)
            do_ref[h] = d_o.astype(BF16)
            prod = d_o * o
            both = jnp.sum(prod, axis=-1, keepdims=True)
            head0 = jnp.sum(jnp.where(first_head, prod, 0.0), axis=-1, keepdims=True)
            dl_ref[h] = jnp.where(first_head, head0, both - head0)

    pair_spec = pl.BlockSpec((hp, tb, PAIR), lambda i: (0, i, 0))
    return pl.pallas_call(
        body, name="mix_bwd", grid=(t // tb,),
        out_shape=(jax.ShapeDtypeStruct((t, 4 * dc + 4 * da), BF16), jax.ShapeDtypeStruct((t, dc), F32),
                   jax.ShapeDtypeStruct((hp, t, PAIR), BF16), jax.ShapeDtypeStruct((hp, t, PAIR), F32),
                   jax.ShapeDtypeStruct((1, dc), F32), jax.ShapeDtypeStruct((hp, 1, PAIR), F32)),
        in_specs=[pl.BlockSpec((tb, dc + da), lambda i: (i, 0)),
                  pl.BlockSpec((tb, dc), lambda i: (i, 0)),
                  pl.BlockSpec((tb, dc), lambda i: (i, 1)),
                  pl.BlockSpec((tb, dc), lambda i: (i, 3)),
                  pl.BlockSpec((tb, da), lambda i: (i, 7)),
                  pair_spec,
                  pl.BlockSpec((1, dc), lambda i: (0, 0)),
                  pl.BlockSpec((hp, 1, PAIR), lambda i: (0, 0, 0))],
        out_specs=(pl.BlockSpec((tb, 4 * dc + 4 * da), lambda i: (i, 0)), pl.BlockSpec((tb, dc), lambda i: (i, 0)),
                   pair_spec, pair_spec,
                   pl.BlockSpec((1, dc), lambda i: (0, 0)), pl.BlockSpec((hp, 1, PAIR), lambda i: (0, 0, 0))),
        compiler_params=_params(("arbitrary",)),
    )(dycat, co, proj, proj, proj, o_mix, g_conv, g_attn_pairs)


def _conv_bwd(dconv_proj, dco, conv_proj, conv_w, dc, after):
    t = dco.shape[0]
    ct = CONV_TILE
    nct = dc // ct

    def body(dcp_in_ref, dco_ref, u_ref, cg_ref, w_ref, after_ref, dcp_ref, acc_ref):
        del dcp_in_ref, after_ref
        which = pl.program_id(1)
        g = dco_ref[...]
        u = u_ref[...].astype(F32)
        cg = cg_ref[...].astype(F32)
        g_prev, g_next = _shift_rows(g, t)
        da = w_ref[0:1, :] * g_next + w_ref[1:2, :] * g + w_ref[2:3, :] * g_prev
        dcp_ref[...] = (da * jnp.where(which == 0, cg, u)).astype(BF16)
        a = cg * u
        a_prev, a_next = _shift_rows(a, t)
        acc_ref[...] = jnp.concatenate(
            [jnp.sum(g * a_prev, axis=0, keepdims=True), jnp.sum(g * a, axis=0, keepdims=True),
             jnp.sum(g * a_next, axis=0, keepdims=True), jnp.sum(g, axis=0, keepdims=True),
             jnp.zeros((4, ct), F32)], axis=0)

    return pl.pallas_call(
        body, name="conv_bwd", grid=(nct, 2),
        out_shape=(jax.ShapeDtypeStruct(dconv_proj.shape, BF16), jax.ShapeDtypeStruct((8, dc), F32)),
        in_specs=[HBM,
                  pl.BlockSpec((t, ct), lambda i, s: (0, i)),
                  pl.BlockSpec((t, ct), lambda i, s: (0, i)),
                  pl.BlockSpec((t, ct), lambda i, s: (0, 2 * nct + i)),
                  pl.BlockSpec((3, ct), lambda i, s: (0, i)), ANY],
        out_specs=(pl.BlockSpec((t, ct), lambda i, s: (0, 2 * s * nct + i)),
                   pl.BlockSpec((8, ct), lambda i, s: (0, i))),
        input_output_aliases={0: 0},
        compiler_params=_params(("arbitrary", "arbitrary")),
    )(dconv_proj, dco, conv_proj, conv_proj, conv_w, after)


def _dh(dproj, winf, after):
    t = dproj.shape[0]
    _, d, ws = winf.shape
    tm = tn = COL_TILE
    nt = (((1,), (1,)), ((), ()))

    def body(a_ref, w_ref, after_ref, o_ref):
        del after_ref
        acc = lax.dot_general(a_ref[:, 0:ws], w_ref[0], nt, preferred_element_type=F32)
        for j in range(1, N_CHIPS):
            acc = acc + lax.dot_general(a_ref[:, j * ws:(j + 1) * ws], w_ref[j], nt, preferred_element_type=F32)
        o_ref[...] = acc

    return pl.pallas_call(
        body, name="dh", grid=(d // tn, t // tm),
        out_shape=jax.ShapeDtypeStruct((t, d), F32),
        in_specs=[pl.BlockSpec((tm, N_CHIPS * ws), lambda n, m: (m, 0)),
                  pl.BlockSpec((N_CHIPS, tn, ws), lambda n, m: (0, n, 0)), ANY],
        out_specs=pl.BlockSpec((tm, tn), lambda n, m: (m, n)),
        compiler_params=_params(("parallel", "parallel")),
    )(dproj, winf, after)


def _prenorm_bwd(x, dh, dout, mod, g_pre):
    t, d = x.shape
    tb = ROW_TILE

    def body(x_ref, dh_ref, dout_ref, mod_ref, g_ref, gx_ref, acc_ref):
        xv = x_ref[...]
        dhv = dh_ref[...]
        r = lax.rsqrt(jnp.mean(xv * xv, axis=-1, keepdims=True) + EPS)
        xh = xv * r
        one_scale = 1.0 + mod_ref[1:2, :]
        a = dhv * one_scale * g_ref[...]
        gx_ref[...] = dout_ref[...] + r * (a - xh * jnp.mean(a * xh, axis=-1, keepdims=True))
        part = jnp.concatenate(
            [jnp.sum(dhv, axis=0, keepdims=True), jnp.sum(dhv * xh * g_ref[...], axis=0, keepdims=True),
             jnp.sum(dhv * xh * one_scale, axis=0, keepdims=True), jnp.zeros((5, d), F32)], axis=0)

        @pl.when(pl.program_id(0) == 0)
        def _():
            acc_ref[...] = jnp.zeros(acc_ref.shape, F32)

        acc_ref[...] += part

    return pl.pallas_call(
        body, name="prenorm_bwd", grid=(t // tb,),
        out_shape=(jax.ShapeDtypeStruct((t, d), F32), jax.ShapeDtypeStruct((8, d), F32)),
        in_specs=[pl.BlockSpec((tb, d), lambda i: (i, 0)), pl.BlockSpec((tb, d), lambda i: (i, 0)),
                  pl.BlockSpec((tb, d), lambda i: (i, 0)), pl.BlockSpec((3, d), lambda i: (0, 0)),
                  pl.BlockSpec((1, d), lambda i: (0, 0))],
        out_specs=(pl.BlockSpec((tb, d), lambda i: (i, 0)), pl.BlockSpec((8, d), lambda i: (0, 0))),
        compiler_params=_params(("arbitrary",)),
    )(x, dh, dout, mod, g_pre)


def _chip_sums(mine, rsib, name):
    _, half, cols = mine.shape
    tr = min(half, ROW_TILE)

    def body(g_ref, r_ref, o_ref):
        o_ref[...] = (g_ref[...].astype(F32) + r_ref[...].astype(F32)).astype(BF16)

    spec = pl.BlockSpec((None, tr, cols), lambda j, i: (j, i, 0))
    return pl.pallas_call(
        body, name=name, grid=(N_CHIPS, half // tr),
        out_shape=jax.ShapeDtypeStruct(mine.shape, BF16),
        in_specs=[spec, spec], out_specs=spec,
        compiler_params=_params(("parallel", "parallel")),
    )(mine, rsib)


def _owner_sum(place, mine, rsib, rici, name):
    _, half, cols = mine.shape
    rows = 2 * half
    tr = min(half, ROW_TILE)
    nt = half // tr

    def body(place_ref, g_ref, r_ref, i_ref, o_ref):
        del place_ref
        acc = g_ref[...].astype(F32) + r_ref[...].astype(F32)
        for k in range(N_CHIPS - 1):
            acc = acc + i_ref[k].astype(F32)
        o_ref[...] = acc

    grid_spec = pltpu.PrefetchScalarGridSpec(
        num_scalar_prefetch=1, grid=(nt,),
        in_specs=[pl.BlockSpec((None, tr, cols), lambda i, p: (p[0], i, 0)),
                  pl.BlockSpec((None, tr, cols), lambda i, p: (p[0], i, 0)),
                  pl.BlockSpec((N_CHIPS - 1, tr, cols), lambda i, p: (0, i, 0))],
        out_specs=pl.BlockSpec((tr, cols), lambda i, p: (p[1] * nt + i, 0)))
    return pl.pallas_call(
        body, name=name, grid_spec=grid_spec,
        out_shape=jax.ShapeDtypeStruct((rows, cols), F32),
        compiler_params=_params(("parallel",)),
    )(place, mine, rsib, rici)


def _adam_math(w, g, m, v):
    m2 = ADAM_B1 * m + (1.0 - ADAM_B1) * g
    v2 = ADAM_B2 * v + (1.0 - ADAM_B2) * (g * g)
    m_hat = m2 / (1.0 - ADAM_B1 ** ADAM_STEP)
    v_hat = v2 / (1.0 - ADAM_B2 ** ADAM_STEP)
    delta = -ADAM_LR * (m_hat / (jnp.sqrt(v_hat) + ADAM_EPS) + ADAM_WD * w)
    return delta, m2, v2


def _adamw(w, g, m, v, name):
    rows, cols = w.shape
    tr = min(rows, ROW_TILE)

    def body(w_ref, g_ref, m_ref, v_ref, go_ref, d_ref, m2_ref, v2_ref):
        g = g_ref[...]
        go_ref[...] = g
        d_ref[...], m2_ref[...], v2_ref[...] = _adam_math(w_ref[...], g, m_ref[...], v_ref[...])

    spec = pl.BlockSpec((tr, cols), lambda i: (i, 0))
    return pl.pallas_call(
        body, name=name, grid=(rows // tr,),
        out_shape=(jax.ShapeDtypeStruct(w.shape, F32),) * 4,
        in_specs=[spec] * 4, out_specs=(spec,) * 4,
        compiler_params=_params(("parallel",)),
    )(w, g, m, v)


def _ada_grad_adamw(c_all_t, dmod_cols, w, m, v):
    d, wa = w.shape
    tr = ROW_TILE

    def body(ct_ref, dm_ref, w_ref, m_ref, v_ref, g_ref, d_ref, m2_ref, v2_ref):
        act = _silu(ct_ref[...])
        g = act[:, 0:1] * dm_ref[0:1, :]
        for b in range(1, N_DEV):
            g = g + act[:, b:b + 1] * dm_ref[b:b + 1, :]
        g_ref[...] = g
        d_ref[...], m2_ref[...], v2_ref[...] = _adam_math(w_ref[...], g, m_ref[...], v_ref[...])

    spec = pl.BlockSpec((tr, wa), lambda i: (i, 0))
    return pl.pallas_call(
        body, name="ada_grad_adamw", grid=(d // tr,),
        out_shape=(jax.ShapeDtypeStruct(w.shape, F32),) * 4,
        in_specs=[pl.BlockSpec((tr, N_DEV), lambda i: (i, 0)), pl.BlockSpec((N_DEV, wa), lambda i: (0, 0)),
                  spec, spec, spec],
        out_specs=(spec,) * 4,
        compiler_params=_params(("parallel",)),
    )(c_all_t, dmod_cols, w, m, v)


def _sum_devices(gathered):
    n = gathered.shape[1]

    def body(g_ref, o_ref):
        acc = g_ref[0:8, :]
        for dev in range(1, N_DEV):
            acc = acc + g_ref[8 * dev:8 * dev + 8, :]
        o_ref[...] = acc

    return pl.pallas_call(
        body, name="sum_devices",
        out_shape=jax.ShapeDtypeStruct((8, n), F32),
        in_specs=[VMEM], out_specs=VMEM,
    )(gathered)


def _pack_small(pieces):
    flat = [p.reshape(-1).astype(F32) for p in pieces]
    offsets, total = [], 0
    for p in flat:
        offsets.append(total)
        total += p.shape[0]
    padded = -(-total // SMALL_ALIGN) * SMALL_ALIGN
    if padded > total:
        flat.append(jnp.zeros((padded - total,), F32))
    return jnp.concatenate(flat).reshape(8, padded // 8), offsets


def _alibi_slope_rows(n_heads):
    slopes = 2.0 ** (-8.0 * jnp.arange(1, n_heads + 1, dtype=F32) / n_heads)
    rows = jnp.zeros((n_heads // 2, 8), F32).at[:, 0:2].set(slopes.reshape(n_heads // 2, 2))
    return jnp.broadcast_to(rows[:, :, None], (n_heads // 2, 8, ATT_KW))


def kernel(x, c, w_ada, b_ada, g_pre, w_in, conv_w, conv_b, g_conv, g_attn, w_out, g_post, loss_target, m_w_ada, m_b_ada, m_g_pre, m_w_in, m_conv_w, m_conv_b, m_g_conv, m_g_attn, m_w_out, m_g_post, v_w_ada, v_b_ada, v_g_pre, v_w_in, v_conv_w, v_conv_b, v_g_conv, v_g_attn, v_w_out, v_g_post):
    t, d = x.shape[1], x.shape[2]
    dc = conv_b.shape[1]
    da = g_attn.shape[1]
    hp = da // PAIR
    ws = w_in.shape[2]
    wa = w_ada.shape[2]
    cws = conv_w.shape[2]
    assert t % ROW_TILE == 0 and d % ROW_TILE == 0 and dc % COL_TILE == 0 and da % COL_TILE == 0
    assert ws == 2 * dc and dc == da and t // BRANCHES[-1][1] >= ATT_BQ

    mx, my, mc = _my_place()
    chip = _chip_of(mx, my)
    dev = 2 * chip + mc
    place = jnp.stack([chip, mc]).astype(jnp.int32)

    x2, tgt2 = x[0], loss_target[0]
    w_ada2, w_in2, w_out2 = w_ada[0], w_in[0], w_out[0]

    win_slots = _cast_into_slot(place, w_in2, "cast_w_in")
    packed, offs = _pack_small([c[0], conv_w[0]])
    seen = _allgather8(packed, "gather_inputs", after=(win_slots,)).reshape(N_DEV, -1)
    c_all = seen[:, offs[0]:offs[0] + d]
    conv_w_full = seen[0::2, offs[1]:offs[1] + 3 * cws].reshape(N_CHIPS, 3, cws).transpose(1, 0, 2).reshape(3, dc)

    ada_part = _ada_partial(c_all, w_ada2)
    ada_seen = _allgather8(ada_part, "gather_ada").reshape(N_DEV, N_DEV, wa)
    mod_flat = lax.dynamic_index_in_dim(ada_seen[0::2], dev, axis=1, keepdims=False).reshape(1, 3 * d) + b_ada
    mod = mod_flat.reshape(3, d)

    win_flight, send_in, recv_in, started = _gather_start(win_slots, mod)

    y_chip, x_chip, d_chip = (_chip_of(mx, 1 - my), _chip_of(1 - mx, my), _chip_of(1 - mx, 1 - my))
    own_chunk, near_chunks, far_chunk = (jnp.stack(js).astype(jnp.int32) for js in ([chip], [y_chip, x_chip], [d_chip]))
    h, ht = _prenorm(x2, mod + started[0, 0], g_pre)
    proj = _proj_chunks(None, h, w_in2, own_chunk, "proj_own")
    win_flight, wout_flight, relay_send_in, relay_recv_in, send_out, recv_out = _gather_relay_in(
        win_flight, _cast_into_slot(place, w_out2, "cast_w_out"), recv_in, proj)
    win_flight = _forward_halves(
        _gather_wait_direct(win_flight, send_in, recv_in, proj, "gather_wait_w_in_direct"), (0, 1), "forward_w_in_direct")
    proj = _proj_chunks(proj, h, win_flight, near_chunks, "proj_neighbours")
    winf = _forward_halves(
        _gather_wait_relayed(win_flight, relay_send_in, relay_recv_in, proj, "gather_wait_w_in_relayed"),
        (2,), "forward_w_in_relayed")
    proj = _proj_chunks(proj, h, winf, far_chunk, "proj_diagonal")
    slopes = _alibi_slope_rows(da // HEAD_DIM)
    co = _conv_fwd(proj, conv_w_full, conv_b, dc)
    wout_flight, relay_send_out, relay_recv_out = _gather_relay_out(wout_flight, recv_out, co)
    o_mix, lse = _attn_fwd(proj, slopes, dc, da)
    g_attn_pairs = g_attn.reshape(hp, 1, PAIR)
    ycat, ycat_t = _mix_fwd(co, proj, o_mix, g_conv, g_attn_pairs)
    wout_flight = _gather_wait_direct(wout_flight, send_out, recv_out, ycat, "gather_wait_w_out_direct")
    wout_flight = _gather_wait_relayed(wout_flight, relay_send_out, relay_recv_out, ycat, "gather_wait_w_out_relayed")
    woutf = _forward_halves(wout_flight, (0, 1, 2), "forward_w_out").reshape(dc + da, d)
    dout, dy, post_sums = _out_fwd_bwd(ycat, woutf, x2, tgt2, mod, g_post)

    gout, rsib_out = _dw_swapped(ycat_t, dy, N_CHIPS, 1, "dw_out")
    csum_out = _chip_sums(gout, rsib_out, "rs_chip_sum_out")
    ssem_out, rsem_out, csum_out, land_out, sent_out = _owners_start(csum_out, "rs_owners_start_out")
    dycat = _matmul_nt(dy, woutf, BF16, "dycat")
    dproj, dco, d_o, delta, dg_conv, dg_attn = _mix_bwd(dycat, co, proj, o_mix, g_conv, g_attn_pairs)
    dproj, conv_sums = _conv_bwd(dproj, dco, proj, conv_w_full, dc, sent_out)
    dproj = _attn_bwd(dproj, proj, d_o, lse, delta, slopes, dc, da, sent_out)
    gin, rsib_in = _dw_swapped(ht, dproj, 1, N_CHIPS, "dw_in")
    csum_in = _chip_sums(gin, rsib_in, "rs_chip_sum_in")
    ssem_in, rsem_in, csum_in, land_in, sent_in = _owners_start(csum_in, "rs_owners_start_in")
    dh = _dh(dproj, winf, sent_in)
    grad_x, pre_sums = _prenorm_bwd(x2, dh, dout, mod, g_pre)

    small, so = _pack_small([
        pre_sums[0], pre_sums[1], post_sums[0],
        pre_sums[2], conv_sums[0:3], conv_sums[3], dg_conv, dg_attn, post_sums[1], post_sums[2, 0:128]])
    ssem_small, rsem_small, small, land_small, sent_small = _allgather8_start(small, dev, "gather_small_start")

    rici_out = _owners_wait(ssem_out, rsem_out, csum_out, land_out, [grad_x, sent_small], "rs_owners_wait_out")
    grad_w_out = _join_halves(_owner_sum(place, gout, rsib_out, rici_out, "rs_owner_sum_out"), "rs_join_halves_out")
    grad_w_out, delta_w_out, new_m_w_out, new_v_w_out = _adamw(
        w_out2, grad_w_out, m_w_out[0], v_w_out[0], "adamw_w_out")

    small_seen = _allgather8_wait(ssem_small, rsem_small, small, land_small, [delta_w_out], "gather_small_wait")
    total = _sum_devices(small_seen).reshape(-1)
    dmod_all = small_seen.reshape(N_DEV, -1)[:, 0:3 * d]
    loss = total[so[9]]
    grad_b_ada = total[0:3 * d].reshape(1, 3 * d)
    grad_g_pre = total[so[3]:so[3] + d].reshape(1, d)
    grad_conv_w_full = total[so[4]:so[4] + 3 * dc].reshape(3, dc)
    grad_conv_w = lax.dynamic_slice_in_dim(grad_conv_w_full, chip * cws, cws, axis=1).reshape(1, 3, cws)
    grad_conv_b = total[so[5]:so[5] + dc].reshape(1, dc)
    grad_g_conv = total[so[6]:so[6] + dc].reshape(1, dc)
    grad_g_attn = total[so[7]:so[7] + da].reshape(1, da)
    grad_g_post = total[so[8]:so[8] + d].reshape(1, d)

    dmod_cols = lax.dynamic_slice_in_dim(dmod_all, chip * wa, wa, axis=1)
    grad_w_ada, delta_w_ada, new_m_w_ada, new_v_w_ada = _ada_grad_adamw(c_all.T, dmod_cols, w_ada2, m_w_ada[0], v_w_ada[0])

    small_w = [b_ada, g_pre, conv_w, conv_b, g_conv, g_attn, g_post]
    small_g = [grad_b_ada, grad_g_pre, grad_conv_w, grad_conv_b, grad_g_conv, grad_g_attn, grad_g_post]
    small_m = [m_b_ada, m_g_pre, m_conv_w, m_conv_b, m_g_conv, m_g_attn, m_g_post]
    small_v = [v_b_ada, v_g_pre, v_conv_w, v_conv_b, v_g_conv, v_g_attn, v_g_post]
    pw, po = _pack_small(small_w)
    pg, _ = _pack_small(small_g)
    pm, _ = _pack_small(small_m)
    pv, _ = _pack_small(small_v)
    sd, sm, sv = (a.reshape(-1) for a in _adamw(pw, pg, pm, pv, "adamw_small")[1:])

    def unpack(flat):
        return [flat[o:o + w.size].reshape(w.shape) for o, w in zip(po, small_w)]

    d_small, m_small, v_small = unpack(sd), unpack(sm), unpack(sv)

    rici_in = _owners_wait(ssem_in, rsem_in, csum_in, land_in, [sd, delta_w_out, delta_w_ada], "rs_owners_wait_in")
    grad_w_in = _join_halves(_owner_sum(place, gin, rsib_in, rici_in, "rs_owner_sum_in"), "rs_join_halves_in")
    grad_w_in, delta_w_in, new_m_w_in, new_v_w_in = _adamw(w_in2, grad_w_in, m_w_in[0], v_w_in[0], "adamw_w_in")

    def lead(a):
        return a.reshape((1,) + a.shape)

    grads = [lead(grad_w_ada), grad_b_ada, grad_g_pre, lead(grad_w_in), grad_conv_w, grad_conv_b, grad_g_conv,
             grad_g_attn, lead(grad_w_out), grad_g_post]
    deltas = [lead(delta_w_ada), d_small[0], d_small[1], lead(delta_w_in), d_small[2], d_small[3], d_small[4],
              d_small[5], lead(delta_w_out), d_small[6]]
    new_ms = [lead(new_m_w_ada), m_small[0], m_small[1], lead(new_m_w_in), m_small[2], m_small[3], m_small[4],
              m_small[5], lead(new_m_w_out), m_small[6]]
    new_vs = [lead(new_v_w_ada), v_small[0], v_small[1], lead(new_v_w_in), v_small[2], v_small[3], v_small[4],
              v_small[5], lead(new_v_w_out), v_small[6]]
    return (loss, lead(grad_x), *grads, *deltas, *new_ms, *new_vs)
```

```python
import functools

import jax
import jax.numpy as jnp
from jax import lax
from jax.experimental import pallas as pl
from jax.experimental.pallas import tpu as pltpu

F32 = jnp.float32
BF16 = jnp.bfloat16
MESH = pl.DeviceIdType.MESH
HBM = pl.BlockSpec(memory_space=pltpu.HBM)
VMEM = pl.BlockSpec(memory_space=pltpu.VMEM)
ANY = pl.BlockSpec(memory_space=pl.ANY)
SEM = pl.BlockSpec(memory_space=pltpu.SEMAPHORE)
EFFECT = pltpu.SideEffectType.DATAFLOW_SIDE_EFFECTING
TOKEN = jax.ShapeDtypeStruct((8, 128), jnp.float32)

HEAD_DIM = 64
PAIR = 2 * HEAD_DIM
BRANCHES = ((128, 1), (512, 4), (2048, 16))
SIDE = 64
EPS = 1e-6
NEG_INF = -1e30
N_CHIPS = 4
N_DEV = 8

ADAM_LR = 0.001
ADAM_B1 = 0.9
ADAM_B2 = 0.999
ADAM_EPS = 1e-08
ADAM_WD = 0.01
ADAM_STEP = 10

VMEM_LIMIT_BYTES = 56 * 1024 * 1024
ROW_TILE = 256
COL_TILE = 512
CONV_TILE = 256
ATT_BQ = 128
ATT_KW = ATT_BQ + 2 * SIDE
ATT_UNROLL = 4
ATT_UNROLL_BWD = 4
SMALL_ALIGN = 1024


def _params(semantics=None):
    kw = {"vmem_limit_bytes": VMEM_LIMIT_BYTES}
    if semantics is not None:
        kw["dimension_semantics"] = semantics
    return pltpu.CompilerParams(**kw)


def _silu(z):
    return z * jax.nn.sigmoid(z)


def _silu_grad(z):
    s = jax.nn.sigmoid(z)
    return s * (1.0 + z * (1.0 - s))


def _my_place():
    return lax.axis_index("x"), lax.axis_index("y"), lax.axis_index("c")


def _flip(a, bit):
    return 1 - a if bit else a


def _chip_of(x, y):
    return 2 * x + y


def _allgather8(v, name, after=()):
    rows_per, n = v.shape

    def body(v_ref, *rest):
        out_ref, send_sems, recv_sems = rest[len(after):]
        x, y, c = _my_place()
        me = 4 * x + 2 * y + c

        def rows(idx):
            return out_ref.at[pl.ds(pl.multiple_of(idx * rows_per, rows_per), rows_per), :]

        out_ref[pl.ds(pl.multiple_of(me * rows_per, rows_per), rows_per), :] = v_ref[...]
        copies = []
        for k in range(1, N_DEV):
            peer = (_flip(x, k & 4), _flip(y, k & 2), _flip(c, k & 1))
            cp = pltpu.make_async_remote_copy(
                src_ref=v_ref, dst_ref=rows(me), send_sem=send_sems.at[k - 1], recv_sem=recv_sems.at[k - 1],
                device_id=peer, device_id_type=MESH)
            cp.start()
            copies.append((cp, peer))
        for k, (cp, peer) in enumerate(copies):
            src = 4 * peer[0] + 2 * peer[1] + peer[2]
            pltpu.make_async_remote_copy(
                src_ref=v_ref, dst_ref=rows(src), send_sem=send_sems.at[k], recv_sem=recv_sems.at[k],
                device_id=peer, device_id_type=MESH).wait_recv()
        for cp, _ in copies:
            cp.wait_send()

    return pl.pallas_call(
        body, name=name,
        out_shape=jax.ShapeDtypeStruct((N_DEV * rows_per, n), v.dtype),
        in_specs=[VMEM] + [ANY] * len(after), out_specs=VMEM,
        scratch_shapes=[pltpu.SemaphoreType.DMA((N_DEV - 1,)), pltpu.SemaphoreType.DMA((N_DEV - 1,))],
    )(v, *after)


def _allgather8_start(v, me, name):
    rows_per, n = v.shape
    land = lax.dynamic_update_slice(jnp.zeros((N_DEV * rows_per, n), v.dtype), v, (me * rows_per, 0))

    def body(v_ref, land_ref, send_sems, recv_sems, v_thru, land_thru, token_ref):
        del v_thru, land_thru
        x, y, c = _my_place()
        mine = land_ref.at[pl.ds(pl.multiple_of((4 * x + 2 * y + c) * rows_per, rows_per), rows_per), :]
        for k in range(1, N_DEV):
            peer = (_flip(x, k & 4), _flip(y, k & 2), _flip(c, k & 1))
            pltpu.make_async_remote_copy(
                src_ref=v_ref, dst_ref=mine, send_sem=send_sems.at[k - 1], recv_sem=recv_sems.at[k - 1],
                device_id=peer, device_id_type=MESH).start()
        token_ref[...] = jnp.zeros(token_ref.shape, F32)

    sems = pltpu.SemaphoreType.DMA((N_DEV - 1,))
    return pl.pallas_call(
        body, name=name,
        out_shape=(sems, sems, jax.ShapeDtypeStruct(v.shape, v.dtype), jax.ShapeDtypeStruct(land.shape, land.dtype), TOKEN),
        in_specs=[HBM, HBM], out_specs=(SEM, SEM, HBM, HBM, VMEM),
        input_output_aliases={0: 2, 1: 3},
        compiler_params=pltpu.CompilerParams(has_side_effects=EFFECT),
    )(pltpu.with_memory_space_constraint(v, pltpu.HBM), pltpu.with_memory_space_constraint(land, pltpu.HBM))


def _allgather8_wait(send_sems, recv_sems, v, land, after, name):
    rows_per = v.shape[0]

    def body(v_ref, land_ref, send_ref, recv_ref, *rest):
        del rest
        x, y, c = _my_place()
        for k in range(1, N_DEV):
            peer = (_flip(x, k & 4), _flip(y, k & 2), _flip(c, k & 1))
            src = 4 * peer[0] + 2 * peer[1] + peer[2]
            cp = pltpu.make_async_remote_copy(
                src_ref=v_ref, dst_ref=land_ref.at[pl.ds(pl.multiple_of(src * rows_per, rows_per), rows_per), :],
                send_sem=send_ref.at[k - 1], recv_sem=recv_ref.at[k - 1], device_id=peer, device_id_type=MESH)
            cp.wait_send()
            cp.wait_recv()

    return pl.pallas_call(
        body, name=name,
        out_shape=(jax.ShapeDtypeStruct(v.shape, v.dtype), jax.ShapeDtypeStruct(land.shape, land.dtype)),
        in_specs=[HBM, HBM, SEM, SEM] + [ANY] * len(after), out_specs=(HBM, HBM),
        input_output_aliases={0: 0, 1: 1},
        compiler_params=pltpu.CompilerParams(has_side_effects=EFFECT),
    )(v, land, send_sems, recv_sems, *after)[1]


def _half_rows(ref, chip, which, half):
    return ref.at[chip, pl.ds(pl.multiple_of(which * half, half), half), :]


def _ici_peers(x, y, c):
    peers = [(_flip(x, k & 2), _flip(y, k & 1), c) for k in (1, 2, 3)]
    return [(peer, _chip_of(peer[0], peer[1])) for peer in peers]


def _part_rows(ref, chip, core, part):
    quarter = ref.shape[1] // 4
    return ref.at[chip, pl.ds(pl.multiple_of((2 * core + part) * quarter, quarter), quarter), :]


def _neighbours(x, y, c):
    return [((x, 1 - y, c), _chip_of(x, 1 - y)), ((1 - x, y, c), _chip_of(1 - x, y)),
            ((1 - x, 1 - y, c), _chip_of(1 - x, 1 - y))]


def _start_direct(buf, send_sems, recv_sems):
    x, y, c = _my_place()
    me = _chip_of(x, y)
    for n, (peer, _) in enumerate(_neighbours(x, y, c)[0:2]):
        for part in ((0, 1), (1, 0))[n]:
            piece = _part_rows(buf, me, c, part)
            pltpu.make_async_remote_copy(
                src_ref=piece, dst_ref=piece, send_sem=send_sems.at[2 * n + part], recv_sem=recv_sems.at[2 * n + part],
                device_id=peer, device_id_type=MESH).start()


def _relay(buf, recv_sems, relay_send, relay_recv):
    x, y, c = _my_place()
    nbrs = _neighbours(x, y, c)
    for n in range(2):
        part = n
        piece = _part_rows(buf, nbrs[n][1], c, part)
        pltpu.make_async_remote_copy(
            src_ref=piece, dst_ref=piece, send_sem=relay_send.at[part], recv_sem=recv_sems.at[2 * n + part],
            device_id=nbrs[n][0], device_id_type=MESH).wait_recv()
        pltpu.make_async_remote_copy(
            src_ref=piece, dst_ref=piece, send_sem=relay_send.at[part], recv_sem=relay_recv.at[part],
            device_id=nbrs[1 - n][0], device_id_type=MESH).start()


def _gather_start(win_slots, after):
    def body(win_in, after_ref, win_ref, send_sems, recv_sems, token_ref):
        del win_in, after_ref
        _start_direct(win_ref, send_sems, recv_sems)
        token_ref[...] = jnp.zeros(token_ref.shape, F32)

    sems = pltpu.SemaphoreType.DMA((4,))
    return pl.pallas_call(
        body, name="gather_start",
        out_shape=(jax.ShapeDtypeStruct(win_slots.shape, win_slots.dtype), sems, sems, TOKEN),
        in_specs=[HBM, ANY], out_specs=(HBM, SEM, SEM, VMEM),
        input_output_aliases={0: 0},
        compiler_params=pltpu.CompilerParams(has_side_effects=EFFECT),
    )(win_slots, after)


def _gather_relay_in(win, wout_slots, recv_in, after):
    def body(win_in, wout_in, recv_in_ref, after_ref, win_ref, wout_ref, relay_send, relay_recv, send_out, recv_out):
        del win_in, wout_in, after_ref
        _relay(win_ref, recv_in_ref, relay_send, relay_recv)
        _start_direct(wout_ref, send_out, recv_out)

    two, four = pltpu.SemaphoreType.DMA((2,)), pltpu.SemaphoreType.DMA((4,))
    return pl.pallas_call(
        body, name="gather_relay_w_in",
        out_shape=(jax.ShapeDtypeStruct(win.shape, win.dtype), jax.ShapeDtypeStruct(wout_slots.shape, wout_slots.dtype),
                   two, two, four, four),
        in_specs=[HBM, HBM, SEM, ANY], out_specs=(HBM, HBM, SEM, SEM, SEM, SEM),
        input_output_aliases={0: 0, 1: 1},
        compiler_params=pltpu.CompilerParams(has_side_effects=EFFECT),
    )(win, wout_slots, recv_in, after)


def _gather_relay_out(wout, recv_out, after):
    def body(wout_in, recv_out_ref, after_ref, wout_ref, relay_send, relay_recv):
        del wout_in, after_ref
        _relay(wout_ref, recv_out_ref, relay_send, relay_recv)

    two = pltpu.SemaphoreType.DMA((2,))
    return pl.pallas_call(
        body, name="gather_relay_w_out",
        out_shape=(jax.ShapeDtypeStruct(wout.shape, wout.dtype), two, two),
        in_specs=[HBM, SEM, ANY], out_specs=(HBM, SEM, SEM),
        input_output_aliases={0: 0},
        compiler_params=pltpu.CompilerParams(has_side_effects=EFFECT),
    )(wout, recv_out, after)


def _gather_wait_direct(buf, send_sems, recv_sems, after, name):
    def body(buf_in, send_ref, recv_ref, after_ref, buf_ref):
        del buf_in, after_ref
        x, y, c = _my_place()
        me = _chip_of(x, y)
        for n, (peer, chip) in enumerate(_neighbours(x, y, c)[0:2]):
            second = 1 - n
            pltpu.make_async_remote_copy(
                src_ref=_part_rows(buf_ref, me, c, second), dst_ref=_part_rows(buf_ref, chip, c, second),
                send_sem=send_ref.at[2 * n + second], recv_sem=recv_ref.at[2 * n + second],
                device_id=peer, device_id_type=MESH).wait_recv()
            for part in range(2):
                piece = _part_rows(buf_ref, me, c, part)
                pltpu.make_async_remote_copy(
                    src_ref=piece, dst_ref=piece, send_sem=send_ref.at[2 * n + part], recv_sem=recv_ref.at[2 * n + part],
                    device_id=peer, device_id_type=MESH).wait_send()

    return pl.pallas_call(
        body, name=name,
        out_shape=jax.ShapeDtypeStruct(buf.shape, buf.dtype),
        in_specs=[HBM, SEM, SEM, ANY], out_specs=HBM,
        input_output_aliases={0: 0},
        compiler_params=pltpu.CompilerParams(has_side_effects=EFFECT),
    )(buf, send_sems, recv_sems, after)


def _gather_wait_relayed(buf, relay_send, relay_recv, after, name):
    def body(buf_in, rsend_ref, rrecv_ref, after_ref, buf_ref):
        del buf_in, after_ref
        x, y, c = _my_place()
        nbrs = _neighbours(x, y, c)
        for n in range(2):
            relayed = _part_rows(buf_ref, nbrs[n][1], c, n)
            cp = pltpu.make_async_remote_copy(
                src_ref=relayed, dst_ref=_part_rows(buf_ref, nbrs[2][1], c, n),
                send_sem=rsend_ref.at[n], recv_sem=rrecv_ref.at[n], device_id=nbrs[1 - n][0], device_id_type=MESH)
            cp.wait_recv()
            cp.wait_send()

    return pl.pallas_call(
        body, name=name,
        out_shape=jax.ShapeDtypeStruct(buf.shape, buf.dtype),
        in_specs=[HBM, SEM, SEM, ANY], out_specs=HBM,
        input_output_aliases={0: 0},
        compiler_params=pltpu.CompilerParams(has_side_effects=EFFECT),
    )(buf, relay_send, relay_recv, after)


def _forward_halves(buf, which, name):
    half = buf.shape[1] // 2

    def body(buf_in, buf_ref, send_sems, recv_sems):
        del buf_in
        x, y, c = _my_place()
        sibling = (x, y, 1 - c)
        chips = [_neighbours(x, y, c)[n][1] for n in which]
        started = []
        for k, src_chip in enumerate(chips):
            landed = _half_rows(buf_ref, src_chip, c, half)
            fw = pltpu.make_async_remote_copy(
                src_ref=landed, dst_ref=landed, send_sem=send_sems.at[k], recv_sem=recv_sems.at[k],
                device_id=sibling, device_id_type=MESH)
            fw.start()
            started.append(fw)
        for k, src_chip in enumerate(chips):
            other = _half_rows(buf_ref, src_chip, 1 - c, half)
            pltpu.make_async_remote_copy(
                src_ref=other, dst_ref=other, send_sem=send_sems.at[k], recv_sem=recv_sems.at[k],
                device_id=sibling, device_id_type=MESH).wait_recv()
        for fw in started:
            fw.wait_send()

    return pl.pallas_call(
        body, name=name,
        out_shape=jax.ShapeDtypeStruct(buf.shape, buf.dtype),
        in_specs=[HBM], out_specs=HBM,
        input_output_aliases={0: 0},
        scratch_shapes=[pltpu.SemaphoreType.DMA((len(which),))] * 2,
    )(buf)


def _dw_swapped(a, b, row_chunks, col_chunks, name):
    r, t = a.shape
    c_all = b.shape[1]
    chunks = row_chunks * col_chunks
    rq, cq = r // row_chunks, c_all // col_chunks
    half = rq // 2
    tn = COL_TILE
    nt = cq // tn
    steps = col_chunks * nt

    def body(a_ref, b_ref, mine_ref, sib_ref, stage, send_sems, recv_sems):
        x, y, c = _my_place()
        j, n = pl.program_id(0), pl.program_id(1)
        step = j * nt + n
        slot = step % 2
        res = jnp.dot(a_ref[...], b_ref[...], preferred_element_type=F32).astype(BF16)

        def landing(jj, nn):
            cols = pl.ds(pl.multiple_of(nn * tn, tn), tn)
            return sib_ref.at[:, :, cols] if col_chunks == 1 else sib_ref.at[pl.ds(jj, 1), :, cols]

        def copy(slot_, step_, jj, nn):
            return pltpu.make_async_remote_copy(
                src_ref=stage.at[slot_], dst_ref=landing(jj, nn), send_sem=send_sems.at[slot_],
                recv_sem=recv_sems.at[step_], device_id=(x, y, 1 - c), device_id_type=MESH)

        @pl.when(step >= 2)
        def _():
            copy(slot, step, j, n).wait_send()

        for q in range(row_chunks):
            lo = res[q * rq:q * rq + half, :]
            hi = res[q * rq + half:(q + 1) * rq, :]
            mine_ref[q] = jnp.where(c == 0, lo, hi)
            stage[slot, q] = jnp.where(c == 0, hi, lo)
        copy(slot, step, j, n).start()

        @pl.when(step == steps - 1)
        def _():
            for s in range(max(steps - 2, 0), steps):
                copy(s % 2, s, j, n).wait_send()
            for s in range(steps):
                copy(s % 2, s, j, n).wait_recv()

    shape = jax.ShapeDtypeStruct((chunks, half, cq), BF16)
    return pl.pallas_call(
        body, name=name, grid=(col_chunks, nt),
        out_shape=(shape, shape),
        in_specs=[pl.BlockSpec((r, t), lambda j, n: (0, 0)), pl.BlockSpec((t, tn), lambda j, n: (0, j * nt + n))],
        out_specs=(pl.BlockSpec((row_chunks, half, tn), lambda j, n: (j, 0, n)), ANY),
        scratch_shapes=[pltpu.VMEM((2, row_chunks, half, tn), BF16), pltpu.SemaphoreType.DMA((2,)),
                        pltpu.SemaphoreType.DMA((steps,))],
        compiler_params=_params(("arbitrary", "arbitrary")),
    )(a, b)


def _owners_start(csum, name):
    land = pltpu.with_memory_space_constraint(lax.empty((N_CHIPS - 1,) + csum.shape[1:], csum.dtype), pltpu.HBM)

    def body(csum_ref, land_ref, send_sems, recv_sems, csum_thru, land_thru, token_ref):
        del csum_thru, land_thru
        x, y, c = _my_place()
        for k, (peer, owner) in enumerate(_ici_peers(x, y, c)):
            pltpu.make_async_remote_copy(
                src_ref=csum_ref.at[owner], dst_ref=land_ref.at[k], send_sem=send_sems.at[k], recv_sem=recv_sems.at[k],
                device_id=peer, device_id_type=MESH).start()
        token_ref[...] = jnp.zeros(token_ref.shape, F32)

    sems = pltpu.SemaphoreType.DMA((N_CHIPS - 1,))
    return pl.pallas_call(
        body, name=name,
        out_shape=(sems, sems, jax.ShapeDtypeStruct(csum.shape, csum.dtype),
                   jax.ShapeDtypeStruct(land.shape, land.dtype), TOKEN),
        in_specs=[HBM, HBM], out_specs=(SEM, SEM, HBM, HBM, VMEM),
        input_output_aliases={0: 2, 1: 3},
        compiler_params=pltpu.CompilerParams(has_side_effects=EFFECT),
    )(pltpu.with_memory_space_constraint(csum, pltpu.HBM), land)


def _owners_wait(send_sems, recv_sems, csum, land, after, name):
    def body(csum_ref, land_ref, send_ref, recv_ref, *rest):
        del rest
        x, y, c = _my_place()
        for k, (peer, owner) in enumerate(_ici_peers(x, y, c)):
            cp = pltpu.make_async_remote_copy(
                src_ref=csum_ref.at[owner], dst_ref=land_ref.at[k], send_sem=send_ref.at[k], recv_sem=recv_ref.at[k],
                device_id=peer, device_id_type=MESH)
            cp.wait_send()
            cp.wait_recv()

    return pl.pallas_call(
        body, name=name,
        out_shape=(jax.ShapeDtypeStruct(csum.shape, csum.dtype), jax.ShapeDtypeStruct(land.shape, land.dtype)),
        in_specs=[HBM, HBM, SEM, SEM] + [ANY] * len(after), out_specs=(HBM, HBM),
        input_output_aliases={0: 0, 1: 1},
        compiler_params=pltpu.CompilerParams(has_side_effects=EFFECT),
    )(csum, land, send_sems, recv_sems, *after)[1]


def _owners_start_relayed(csum, name):
    part = csum.shape[1] // 2
    land = pltpu.with_memory_space_constraint(lax.empty((N_CHIPS - 1,) + csum.shape[1:], csum.dtype), pltpu.HBM)
    relay = pltpu.with_memory_space_constraint(lax.empty((2, part, csum.shape[2]), csum.dtype), pltpu.HBM)

    def body(csum_ref, land_ref, relay_ref, send_sems, recv_sems, csum_thru, land_thru, relay_thru, token_ref):
        del csum_thru, land_thru, relay_thru
        x, y, c = _my_place()
        nbrs = _neighbours(x, y, c)
        for n, (peer, chip) in enumerate(nbrs[0:2]):
            pltpu.make_async_remote_copy(
                src_ref=csum_ref.at[nbrs[2][1], pl.ds(n * part, part), :], dst_ref=relay_ref.at[n],
                send_sem=send_sems.at[2 + n], recv_sem=recv_sems.at[2 + n], device_id=peer, device_id_type=MESH).start()
            pltpu.make_async_remote_copy(
                src_ref=csum_ref.at[chip], dst_ref=land_ref.at[n], send_sem=send_sems.at[n], recv_sem=recv_sems.at[n],
                device_id=peer, device_id_type=MESH).start()
        token_ref[...] = jnp.zeros(token_ref.shape, F32)

    sems = pltpu.SemaphoreType.DMA((4,))
    return pl.pallas_call(
        body, name=name,
        out_shape=(sems, sems, jax.ShapeDtypeStruct(csum.shape, csum.dtype),
                   jax.ShapeDtypeStruct(land.shape, land.dtype), jax.ShapeDtypeStruct(relay.shape, relay.dtype), TOKEN),
        in_specs=[HBM, HBM, HBM], out_specs=(SEM, SEM, HBM, HBM, HBM, VMEM),
        input_output_aliases={0: 2, 1: 3, 2: 4},
        compiler_params=pltpu.CompilerParams(has_side_effects=EFFECT),
    )(pltpu.with_memory_space_constraint(csum, pltpu.HBM), land, relay)


def _owners_relay(land, relay, recv_sems, after, name):
    part = relay.shape[1]

    def body(land_ref, relay_ref, recv_ref, after_ref, land_thru, relay_thru, fwd_send, fwd_recv, token_ref):
        del after_ref, land_thru, relay_thru
        x, y, c = _my_place()
        nbrs = _neighbours(x, y, c)
        for n in range(2):
            piece = relay_ref.at[n]
            pltpu.make_async_remote_copy(
                src_ref=piece, dst_ref=piece, send_sem=fwd_send.at[n], recv_sem=recv_ref.at[2 + n],
                device_id=nbrs[n][0], device_id_type=MESH).wait_recv()
            pltpu.make_async_remote_copy(
                src_ref=piece, dst_ref=land_ref.at[2, pl.ds(n * part, part), :], send_sem=fwd_send.at[n],
                recv_sem=fwd_recv.at[n], device_id=nbrs[1 - n][0], device_id_type=MESH).start()
        token_ref[...] = jnp.zeros(token_ref.shape, F32)

    two = pltpu.SemaphoreType.DMA((2,))
    return pl.pallas_call(
        body, name=name,
        out_shape=(jax.ShapeDtypeStruct(land.shape, land.dtype), jax.ShapeDtypeStruct(relay.shape, relay.dtype),
                   two, two, TOKEN),
        in_specs=[HBM, HBM, SEM, ANY], out_specs=(HBM, HBM, SEM, SEM, VMEM),
        input_output_aliases={0: 0, 1: 1},
        compiler_params=pltpu.CompilerParams(has_side_effects=EFFECT),
    )(land, relay, recv_sems, after)


def _owners_wait_relayed(send_sems, recv_sems, fwd_send, fwd_recv, csum, land, relay, after, name):
    part = relay.shape[1]

    def body(csum_ref, land_ref, relay_ref, send_ref, recv_ref, fsend_ref, frecv_ref, *rest):
        del rest
        x, y, c = _my_place()
        nbrs = _neighbours(x, y, c)
        for n, (peer, chip) in enumerate(nbrs[0:2]):
            direct = pltpu.make_async_remote_copy(
                src_ref=csum_ref.at[chip], dst_ref=land_ref.at[n], send_sem=send_ref.at[n], recv_sem=recv_ref.at[n],
                device_id=peer, device_id_type=MESH)
            direct.wait_recv()
            direct.wait_send()
            pltpu.make_async_remote_copy(
                src_ref=csum_ref.at[nbrs[2][1], pl.ds(n * part, part), :], dst_ref=relay_ref.at[n],
                send_sem=send_ref.at[2 + n], recv_sem=recv_ref.at[2 + n], device_id=peer, device_id_type=MESH).wait_send()
            passed = pltpu.make_async_remote_copy(
                src_ref=relay_ref.at[n], dst_ref=land_ref.at[2, pl.ds(n * part, part), :], send_sem=fsend_ref.at[n],
                recv_sem=frecv_ref.at[n], device_id=nbrs[1 - n][0], device_id_type=MESH)
            passed.wait_recv()
            passed.wait_send()

    return pl.pallas_call(
        body, name=name,
        out_shape=tuple(jax.ShapeDtypeStruct(a.shape, a.dtype) for a in (csum, land, relay)),
        in_specs=[HBM, HBM, HBM, SEM, SEM, SEM, SEM] + [ANY] * len(after), out_specs=(HBM, HBM, HBM),
        input_output_aliases={0: 0, 1: 1, 2: 2},
        compiler_params=pltpu.CompilerParams(has_side_effects=EFFECT),
    )(csum, land, relay, send_sems, recv_sems, fwd_send, fwd_recv, *after)[1]


def _join_halves(full, name):
    rows = full.shape[0] // 2

    def body(full_in, full_ref, send_sem, recv_sem):
        del full_in
        x, y, c = _my_place()
        sibling = (x, y, 1 - c)
        mine = full_ref.at[pl.ds(pl.multiple_of(c * rows, rows), rows), :]
        theirs = full_ref.at[pl.ds(pl.multiple_of((1 - c) * rows, rows), rows), :]
        cp = pltpu.make_async_remote_copy(
            src_ref=mine, dst_ref=mine, send_sem=send_sem, recv_sem=recv_sem, device_id=sibling, device_id_type=MESH)
        cp.start()
        pltpu.make_async_remote_copy(
            src_ref=theirs, dst_ref=theirs, send_sem=send_sem, recv_sem=recv_sem,
            device_id=sibling, device_id_type=MESH).wait_recv()
        cp.wait_send()

    return pl.pallas_call(
        body, name=name,
        out_shape=jax.ShapeDtypeStruct(full.shape, full.dtype),
        in_specs=[HBM], out_specs=HBM,
        input_output_aliases={0: 0},
        scratch_shapes=[pltpu.SemaphoreType.DMA, pltpu.SemaphoreType.DMA],
    )(full)


def _cast_into_slot(place, w, name):
    rows, cols = w.shape
    tr = min(rows, ROW_TILE)

    def body(place_ref, w_ref, o_ref):
        del place_ref
        o_ref[...] = w_ref[...].astype(BF16)

    grid_spec = pltpu.PrefetchScalarGridSpec(
        num_scalar_prefetch=1, grid=(rows // tr,),
        in_specs=[pl.BlockSpec((tr, cols), lambda i, p: (i, 0))],
        out_specs=pl.BlockSpec((None, tr, cols), lambda i, p: (p[0], i, 0)))
    return pl.pallas_call(
        body, name=name, grid_spec=grid_spec,
        out_shape=jax.ShapeDtypeStruct((N_CHIPS, rows, cols), BF16),
        compiler_params=_params(("parallel",)),
    )(place, w)


def _ada_partial(c_all, w_ada):
    d_model, wa = w_ada.shape
    tn = 512 if wa % 512 == 0 else 256

    def body(c_ref, w_ref, o_ref):
        o_ref[...] = jnp.dot(_silu(c_ref[...]), w_ref[...], precision=lax.Precision.HIGHEST,
                             preferred_element_type=F32)

    return pl.pallas_call(
        body, name="ada_partial", grid=(wa // tn,),
        out_shape=jax.ShapeDtypeStruct((N_DEV, wa), F32),
        in_specs=[pl.BlockSpec((N_DEV, d_model), lambda i: (0, 0)), pl.BlockSpec((d_model, tn), lambda i: (0, i))],
        out_specs=pl.BlockSpec((N_DEV, tn), lambda i: (0, i)),
        compiler_params=_params(("parallel",)),
    )(c_all, w_ada)


def _prenorm(x, mod, g_pre):
    t, d = x.shape
    tb = ROW_TILE

    def body(x_ref, mod_ref, g_ref, h_ref, ht_ref):
        xv = x_ref[...]
        r = lax.rsqrt(jnp.mean(xv * xv, axis=-1, keepdims=True) + EPS)
        h = (xv * r) * g_ref[...] * (1.0 + mod_ref[1:2, :]) + mod_ref[0:1, :]
        h_ref[...] = h.astype(BF16)
        ht_ref[...] = h.T.astype(BF16)

    return pl.pallas_call(
        body, name="prenorm", grid=(t // tb,),
        out_shape=(jax.ShapeDtypeStruct((t, d), BF16), jax.ShapeDtypeStruct((d, t), BF16)),
        in_specs=[pl.BlockSpec((tb, d), lambda i: (i, 0)), pl.BlockSpec((3, d), lambda i: (0, 0)),
                  pl.BlockSpec((1, d), lambda i: (0, 0))],
        out_specs=(pl.BlockSpec((tb, d), lambda i: (i, 0)), pl.BlockSpec((d, tb), lambda i: (0, i))),
        compiler_params=_params(("parallel",)),
    )(x, mod, g_pre)


def _proj_chunks(proj, h, w, chunks, name):
    t, d = h.shape
    ws = w.shape[-1]
    tn = COL_TILE
    nt = ws // tn

    def body(chunk_ref, *refs):
        del chunk_ref
        a_ref, b_ref, o_ref = refs[-3:]
        o_ref[...] = jnp.dot(a_ref[...], b_ref[...].astype(BF16), preferred_element_type=F32).astype(BF16)

    if w.ndim == 3:
        w_spec = pl.BlockSpec((None, d, tn), lambda i, n, ch: (ch[i], 0, n))
    else:
        w_spec = pl.BlockSpec((d, tn), lambda i, n, ch: (0, n))
    first = proj is None
    grid_spec = pltpu.PrefetchScalarGridSpec(
        num_scalar_prefetch=1, grid=(chunks.shape[0], nt),
        in_specs=([] if first else [HBM]) + [pl.BlockSpec((t, d), lambda i, n, ch: (0, 0)), w_spec],
        out_specs=pl.BlockSpec((t, tn), lambda i, n, ch: (0, ch[i] * nt + n)))
    return pl.pallas_call(
        body, name=name, grid_spec=grid_spec,
        out_shape=jax.ShapeDtypeStruct((t, N_CHIPS * ws), BF16),
        input_output_aliases={} if first else {1: 0},
        compiler_params=_params(("parallel", "parallel")),
    )(*([chunks] if first else [chunks, proj]), h, w)


def _shift_rows(a, rows):
    idx = lax.broadcasted_iota(jnp.int32, a.shape, 0)
    prev = jnp.where(idx == 0, 0.0, pltpu.roll(a, 1, 0))
    nxt = jnp.where(idx == rows - 1, 0.0, pltpu.roll(a, rows - 1, 0))
    return prev, nxt


def _conv_fwd(conv_proj, conv_w, conv_b, dc):
    t = conv_proj.shape[0]
    ct = CONV_TILE
    nct = dc // ct

    def body(u_ref, cg_ref, w_ref, b_ref, co_ref):
        a = cg_ref[...].astype(F32) * u_ref[...].astype(F32)
        prev, nxt = _shift_rows(a, t)
        co_ref[...] = w_ref[0:1, :] * prev + w_ref[1:2, :] * a + w_ref[2:3, :] * nxt + b_ref[...]

    return pl.pallas_call(
        body, name="conv_fwd", grid=(nct,),
        out_shape=jax.ShapeDtypeStruct((t, dc), F32),
        in_specs=[pl.BlockSpec((t, ct), lambda i: (0, i)), pl.BlockSpec((t, ct), lambda i: (0, 2 * nct + i)),
                  pl.BlockSpec((3, ct), lambda i: (0, i)), pl.BlockSpec((1, ct), lambda i: (0, i))],
        out_specs=pl.BlockSpec((t, ct), lambda i: (0, i)),
        compiler_params=_params(("parallel",)),
    )(conv_proj, conv_proj, conv_w, conv_b)


def _to_residue_major(src_ref, dst_ref, r):
    seq = src_ref.shape[0] // r
    for res in range(r):
        dst_ref[res * seq:(res + 1) * seq, :] = src_ref[pl.ds(res, seq, stride=r), :].astype(dst_ref.dtype)


def _branch_operands(token_refs, stage, dil, r):
    if r == 1:
        return list(token_refs)
    for i, ref in enumerate(token_refs):
        stage[...] = ref[...].astype(F32)
        _to_residue_major(stage, dil.at[i], r)
    return [dil.at[i] for i in range(len(token_refs))]


def _scaled_queries(q):
    return (q.astype(F32) * (HEAD_DIM ** -0.5)).astype(BF16)


BLOCK_SHIFTS = (0, -SIDE, None)


def _band_bias(rel, slope):
    arel = jnp.abs(rel)
    return jnp.where(arel <= SIDE, arel.astype(F32) * slope, NEG_INF)


def _fill_bias_tiles(bias_ref, sl_ref, r, kw):
    base = lax.broadcasted_iota(jnp.int32, (ATT_BQ, kw), 1) - lax.broadcasted_iota(jnp.int32, (ATT_BQ, kw), 0)
    for hh in range(2):
        slope = -(sl_ref[hh:hh + 1, 0:kw] * float(r))
        for e, shift in enumerate(BLOCK_SHIFTS):
            shift = ATT_BQ - kw if shift is None else shift
            bias_ref[hh, e, :, 0:kw] = _band_bias(base + shift, slope)


def _fill_stacked_bias_tiles(bias_ref, sl_ref, r, kw):
    base = lax.broadcasted_iota(jnp.int32, (kw, ATT_BQ), 0) - lax.broadcasted_iota(jnp.int32, (kw, ATT_BQ), 1)
    for hh in range(2):
        slope = -(sl_ref[hh:hh + 1, 0:ATT_BQ] * float(r))
        for e, shift in enumerate(BLOCK_SHIFTS):
            shift = ATT_BQ - kw if shift is None else shift
            bias_ref[e, 0:kw, hh * ATT_BQ:(hh + 1) * ATT_BQ] = _band_bias(base + shift, slope)


def _first_head_lanes():
    return lax.broadcasted_iota(jnp.int32, (1, PAIR), 1) < HEAD_DIM


def _only_head(x, first, hh):
    return jnp.where(first if hh == 0 else jnp.logical_not(first), x, jnp.zeros_like(x))


def _block_place(g, seq_len, kw):
    nqb = seq_len // ATT_BQ
    if nqb == 1:
        row = pl.multiple_of(g * ATT_BQ, ATT_BQ)
        return row, row, 0
    res = g // nqb
    qb = g - res * nqb
    q0 = qb * ATT_BQ
    ks = jnp.clip(q0 - SIDE, 0, seq_len - kw)
    edge = jnp.where(qb == 0, 0, jnp.where(qb == nqb - 1, 2, 1))
    return (pl.multiple_of(res * seq_len + q0, ATT_BQ), pl.multiple_of(res * seq_len + ks, SIDE), edge)


def _qkv_specs(dc, da, t, index):
    return [pl.BlockSpec((t, PAIR), functools.partial(index, (4 * dc + comp * da) // PAIR)) for comp in range(3)]


def _attn_fwd(proj, slopes, dc, da):
    t = proj.shape[0]
    hp = da // PAIR
    n_blocks = t // ATT_BQ

    def body(q_ref, k_ref, v_ref, sl_ref, o_ref, lse_ref, stage, dil, bias, o_res, l_res, o_tok, l_tok):
        for b, (_, r) in enumerate(BRANCHES):
            seq_len = t // r
            kw = min(ATT_KW, seq_len)
            ops = _branch_operands([q_ref, k_ref, v_ref], stage, dil, r)
            _fill_bias_tiles(bias, sl_ref, r, kw)
            o_dst, l_dst = (o_tok.at[b], l_tok.at[b]) if r == 1 else (o_res, l_res)
            first = _first_head_lanes()

            def blocks(trip, carry, seq_len=seq_len, kw=kw, o_dst=o_dst, l_dst=l_dst, first=first, ops=ops):
                nt = (((1,), (1,)), ((), ()))
                places = [_block_place(trip * ATT_UNROLL + i, seq_len, kw) for i in range(ATT_UNROLL)]
                ones = jnp.ones((kw, PAIR), BF16)
                chains = [(i, hh) for i in range(ATT_UNROLL) for hh in range(2)]
                qs = [_scaled_queries(ops[0][pl.ds(qrow, ATT_BQ), :]) for qrow, _, _ in places]
                ks = [ops[1][pl.ds(krow, kw), :] for _, krow, _ in places]
                vs = [ops[2][pl.ds(krow, kw), :] for _, krow, _ in places]
                ss = [lax.dot_general(_only_head(qs[i], first, hh), ks[i], nt, preferred_element_type=F32)
                      + bias[hh, places[i][2], :, 0:kw] for i, hh in chains]
                tops = [jnp.max(s, axis=-1, keepdims=True) for s in ss]
                ps = [jnp.exp(s - m).astype(BF16) for s, m in zip(ss, tops)]
                parts = [jnp.dot(p, jnp.concatenate([_only_head(vs[i], first, hh), _only_head(ones, first, hh)], axis=1),
                                 preferred_element_type=F32) for p, (i, hh) in zip(ps, chains)]
                for i, (qrow, _, _) in enumerate(places):
                    both = parts[2 * i] + parts[2 * i + 1]
                    den = both[:, PAIR:]
                    o_dst[pl.ds(qrow, ATT_BQ), :] = both[:, 0:PAIR] / den
                    l_dst[pl.ds(qrow, ATT_BQ), :] = jnp.where(first, tops[2 * i], tops[2 * i + 1]) + jnp.log(den)
                return carry

            lax.fori_loop(0, n_blocks // ATT_UNROLL, blocks, 0)
            if r > 1:
                for res in range(r):
                    rows = slice(res * seq_len, (res + 1) * seq_len)
                    o_tok[b, pl.ds(res, seq_len, stride=r), :] = o_res[rows, :]
                    l_tok[b, pl.ds(res, seq_len, stride=r), :] = l_res[rows, :]

        def merge(i, carry):
            rows = pl.ds(pl.multiple_of(i * ROW_TILE, ROW_TILE), ROW_TILE)
            la, lb, lc = l_tok[0, rows, :], l_tok[1, rows, :], l_tok[2, rows, :]
            m = jnp.maximum(jnp.maximum(la, lb), lc)
            wa, wb, wc = jnp.exp(la - m), jnp.exp(lb - m), jnp.exp(lc - m)
            den = wa + wb + wc
            o_ref[rows, :] = (wa * o_tok[0, rows, :] + wb * o_tok[1, rows, :] + wc * o_tok[2, rows, :]) * (1.0 / den)
            lse_ref[rows, :] = m + jnp.log(den)
            return carry

        lax.fori_loop(0, t // ROW_TILE, merge, 0)

    pair_spec = pl.BlockSpec((None, t, PAIR), lambda h: (h, 0, 0))
    return pl.pallas_call(
        body, name="attn_fwd", grid=(hp,),
        out_shape=(jax.ShapeDtypeStruct((hp, t, PAIR), F32), jax.ShapeDtypeStruct((hp, t, PAIR), F32)),
        in_specs=_qkv_specs(dc, da, t, lambda first, h: (0, first + h))
        + [pl.BlockSpec((None, 8, ATT_KW), lambda h: (h, 0, 0))],
        out_specs=(pair_spec, pair_spec),
        scratch_shapes=[pltpu.VMEM((t, PAIR), F32), pltpu.VMEM((3, t, PAIR), BF16),
                        pltpu.VMEM((2, 3, ATT_BQ, ATT_KW), F32),
                        pltpu.VMEM((t, PAIR), F32), pltpu.VMEM((t, PAIR), F32),
                        pltpu.VMEM((3, t, PAIR), F32), pltpu.VMEM((3, t, PAIR), F32)],
        compiler_params=_params(("parallel",)),
    )(proj, proj, proj, slopes)


def _attn_bwd(dproj, proj, d_o, lse, delta, slopes, dc, da, after):
    t = proj.shape[0]
    hp = da // PAIR
    n_blocks = t // ATT_BQ

    def all_branches(q_ref, k_ref, v_ref, do_ref, lse_ref, dl_ref, sl_ref,
                     stage, dil, packed, packed_res, row_vecs, bias_t, acc, tot):
        first = _first_head_lanes()
        lane = lax.broadcasted_iota(jnp.int32, (1, PAIR), 1)
        packed[...] = jnp.where((lane & (HEAD_DIM - 1)) < HEAD_DIM // 2, lse_ref[...], dl_ref[...])
        for b, (_, r) in enumerate(BRANCHES):
            seq_len = t // r
            kw = min(ATT_KW, seq_len)
            ops = _branch_operands([q_ref, k_ref, v_ref, do_ref], stage, dil, r)
            scalars = packed
            if r > 1:
                _to_residue_major(packed, packed_res, r)
                scalars = packed_res
            for g in range(n_blocks):
                flipped = scalars[g * ATT_BQ:(g + 1) * ATT_BQ, :].T
                for row in range(4):
                    row_vecs[g, row:row + 1, :] = flipped[row * (HEAD_DIM // 2):row * (HEAD_DIM // 2) + 1, :]
            _fill_stacked_bias_tiles(bias_t, sl_ref, r, kw)
            acc[1] = jnp.zeros((t, PAIR), F32)
            acc[2] = jnp.zeros((t, PAIR), F32)

            def blocks(trip, carry, seq_len=seq_len, kw=kw, ops=ops):
                nt = (((1,), (1,)), ((), ()))
                group = range(ATT_UNROLL_BWD)
                places = [_block_place(trip * ATT_UNROLL_BWD + i, seq_len, kw) for i in group]
                ks, vs, q2s, do2s, lse2s, dl2s = [], [], [], [], [], []
                for i, (qrow, krow, _) in zip(group, places):
                    q = _scaled_queries(ops[0][pl.ds(qrow, ATT_BQ), :])
                    dov = ops[3][pl.ds(qrow, ATT_BQ), :]
                    ks.append(ops[1][pl.ds(krow, kw), :])
                    vs.append(ops[2][pl.ds(krow, kw), :])
                    q2s.append(jnp.concatenate([_only_head(q, first, 0), _only_head(q, first, 1)], axis=0))
                    do2s.append(jnp.concatenate([_only_head(dov, first, 0), _only_head(dov, first, 1)], axis=0))
                    rows = row_vecs[trip * ATT_UNROLL_BWD + i]
                    lse2s.append(jnp.concatenate([rows[0:1, :], rows[2:3, :]], axis=1))
                    dl2s.append(jnp.concatenate([rows[1:2, :], rows[3:4, :]], axis=1))
                s_ts = [lax.dot_general(ks[i], q2s[i], nt, preferred_element_type=F32) for i in group]
                dp_ts = [lax.dot_general(vs[i], do2s[i], nt, preferred_element_type=F32) for i in group]
                p_ts = [jnp.exp(s_ts[i] + bias_t[places[i][2], 0:kw, :] - lse2s[i]) for i in group]
                ds_ts = [p_ts[i] * (dp_ts[i] - dl2s[i]) for i in group]
                dvs = [jnp.dot(p_ts[i].astype(BF16), do2s[i], preferred_element_type=F32) for i in group]
                dks = [jnp.dot(ds_ts[i].astype(BF16), q2s[i], preferred_element_type=F32) for i in group]
                dss = [ds_ts[i].T.astype(BF16) for i in group]
                dqs = [jnp.dot(dss[i][0:ATT_BQ, :], _only_head(ks[i], first, 0), preferred_element_type=F32)
                       + jnp.dot(dss[i][ATT_BQ:2 * ATT_BQ, :], _only_head(ks[i], first, 1), preferred_element_type=F32)
                       for i in group]
                for i, (qrow, krow, _) in zip(group, places):
                    acc[0, pl.ds(qrow, ATT_BQ), :] = dqs[i] * (HEAD_DIM ** -0.5)
                    acc[1, pl.ds(krow, kw), :] += dks[i]
                    acc[2, pl.ds(krow, kw), :] += dvs[i]
                return carry

            lax.fori_loop(0, n_blocks // ATT_UNROLL_BWD, blocks, 0)
            for comp in range(3):
                if r == 1:
                    tot[comp] = acc[comp]
                else:
                    for res in range(r):
                        tok = pl.ds(res, seq_len, stride=r)
                        tot[comp, tok, :] = tot[comp, tok, :] + acc[comp, res * seq_len:(res + 1) * seq_len, :]

    first_q = (4 * dc) // PAIR

    def body(dproj_in, q_ref, k_ref, v_ref, do_ref, lse_ref, dl_ref, sl_ref, after_ref, out_ref, *scratch):
        del dproj_in, after_ref
        work, out_stage, out_sems = scratch[:-2], scratch[-2], scratch[-1]
        h = pl.program_id(0)
        all_branches(q_ref, k_ref, v_ref, do_ref, lse_ref, dl_ref, sl_ref, *work)

        def out_copy(comp):
            cols = pl.ds(pl.multiple_of((first_q + comp * hp + h) * PAIR, PAIR), PAIR)
            return pltpu.make_async_copy(out_stage.at[comp], out_ref.at[:, cols], out_sems.at[comp])

        @pl.when(h > 0)
        def _():
            for comp in range(3):
                out_copy(comp).wait()

        for comp in range(3):
            out_stage[comp] = work[-1][comp].astype(BF16)
            out_copy(comp).start()

        @pl.when(h == hp - 1)
        def _():
            for comp in range(3):
                out_copy(comp).wait()

    pair_spec = pl.BlockSpec((None, t, PAIR), lambda h: (h, 0, 0))
    return pl.pallas_call(
        body, name="attn_bwd", grid=(hp,),
        out_shape=jax.ShapeDtypeStruct(dproj.shape, BF16),
        in_specs=[HBM] + _qkv_specs(dc, da, t, lambda first, h: (0, first + h))
        + [pair_spec, pair_spec, pair_spec, pl.BlockSpec((None, 8, ATT_KW), lambda h: (h, 0, 0)), ANY],
        out_specs=ANY,
        input_output_aliases={0: 0},
        scratch_shapes=[pltpu.VMEM((t, PAIR), F32), pltpu.VMEM((4, t, PAIR), BF16),
                        pltpu.VMEM((t, PAIR), F32), pltpu.VMEM((t, PAIR), F32),
                        pltpu.VMEM((n_blocks, 8, ATT_BQ), F32), pltpu.VMEM((3, ATT_KW, 2 * ATT_BQ), F32),
                        pltpu.VMEM((3, t, PAIR), F32), pltpu.VMEM((3, t, PAIR), F32),
                        pltpu.VMEM((3, t, PAIR), BF16), pltpu.SemaphoreType.DMA((3,))],
        compiler_params=_params(("arbitrary",)),
    )(dproj, proj, proj, proj, d_o, lse, delta, slopes, after)


def _mix_fwd(co, proj, o_mix, g_conv, g_attn_pairs):
    t, dc = co.shape
    hp = o_mix.shape[0]
    da = hp * PAIR
    tb = ROW_TILE

    def body(co_ref, bg_ref, zc_ref, za_ref, om_ref, gc_ref, ga_ref, ycat_ref, ycatt_ref):
        p = bg_ref[...].astype(F32) * co_ref[...]
        rc = lax.rsqrt(jnp.mean(p * p, axis=-1, keepdims=True) + EPS)
        yc = (p * rc) * gc_ref[...] * _silu(zc_ref[...].astype(F32))
        ycat_ref[:, 0:dc] = yc.astype(BF16)
        ycatt_ref[0:dc, :] = yc.T.astype(BF16)
        ssq = jnp.zeros((tb, 1), F32)
        for h in range(hp):
            o = om_ref[h]
            ssq = ssq + jnp.sum(o * o, axis=-1, keepdims=True)
        ra = lax.rsqrt(ssq * (1.0 / da) + EPS)
        for h in range(hp):
            ya = (om_ref[h] * ra) * ga_ref[h] * _silu(za_ref[:, h * PAIR:(h + 1) * PAIR].astype(F32))
            ycat_ref[:, dc + h * PAIR:dc + (h + 1) * PAIR] = ya.astype(BF16)
            ycatt_ref[dc + h * PAIR:dc + (h + 1) * PAIR, :] = ya.T.astype(BF16)

    pair_spec = pl.BlockSpec((hp, tb, PAIR), lambda i: (0, i, 0))
    return pl.pallas_call(
        body, name="mix_fwd", grid=(t // tb,),
        out_shape=(jax.ShapeDtypeStruct((t, dc + da), BF16), jax.ShapeDtypeStruct((dc + da, t), BF16)),
        in_specs=[pl.BlockSpec((tb, dc), lambda i: (i, 0)),
                  pl.BlockSpec((tb, dc), lambda i: (i, 1)),
                  pl.BlockSpec((tb, dc), lambda i: (i, 3)),
                  pl.BlockSpec((tb, da), lambda i: (i, 7)),
                  pair_spec,
                  pl.BlockSpec((1, dc), lambda i: (0, 0)),
                  pl.BlockSpec((hp, 1, PAIR), lambda i: (0, 0, 0))],
        out_specs=(pl.BlockSpec((tb, dc + da), lambda i: (i, 0)), pl.BlockSpec((dc + da, tb), lambda i: (0, i))),
        compiler_params=_params(("parallel",)),
    )(co, proj, proj, proj, o_mix, g_conv, g_attn_pairs)


def _out_fwd_bwd(ycat, woutf, x, target, mod, g_post):
    t, d = x.shape
    n = ycat.shape[1]
    tb = ROW_TILE

    def body(a_ref, w_ref, x_ref, tg_ref, mod_ref, g_ref, dout_ref, dy_ref, acc_ref):
        y = jnp.dot(a_ref[...], w_ref[...], preferred_element_type=F32)
        r = lax.rsqrt(jnp.mean(y * y, axis=-1, keepdims=True) + EPS)
        nh = y * r
        gate = mod_ref[2:3, :]
        nrm = nh * g_ref[...]
        err = x_ref[...] + gate * nrm - tg_ref[...]
        dout = err * (1.0 / d)
        dout_ref[...] = dout
        dn = dout * gate
        a = dn * g_ref[...]
        dy = r * (a - nh * jnp.mean(a * nh, axis=-1, keepdims=True))
        dy_ref[...] = dy.astype(BF16)
        loss = 0.5 * jnp.sum(jnp.sum(err * err, axis=-1, keepdims=True) * (1.0 / d), axis=0, keepdims=True)
        part = jnp.concatenate(
            [jnp.sum(dout * nrm, axis=0, keepdims=True), jnp.sum(dn * nh, axis=0, keepdims=True),
             jnp.broadcast_to(loss, (1, d)), jnp.zeros((5, d), F32)], axis=0)

        @pl.when(pl.program_id(0) == 0)
        def _():
            acc_ref[...] = jnp.zeros(acc_ref.shape, F32)

        acc_ref[...] += part

    return pl.pallas_call(
        body, name="out_fwd_bwd", grid=(t // tb,),
        out_shape=(jax.ShapeDtypeStruct((t, d), F32), jax.ShapeDtypeStruct((t, d), BF16),
                   jax.ShapeDtypeStruct((8, d), F32)),
        in_specs=[pl.BlockSpec((tb, n), lambda i: (i, 0)), pl.BlockSpec((n, d), lambda i: (0, 0)),
                  pl.BlockSpec((tb, d), lambda i: (i, 0)), pl.BlockSpec((tb, d), lambda i: (i, 0)),
                  pl.BlockSpec((3, d), lambda i: (0, 0)), pl.BlockSpec((1, d), lambda i: (0, 0))],
        out_specs=(pl.BlockSpec((tb, d), lambda i: (i, 0)), pl.BlockSpec((tb, d), lambda i: (i, 0)),
                   pl.BlockSpec((8, d), lambda i: (0, 0))),
        compiler_params=_params(("arbitrary",)),
    )(ycat, woutf, x, target, mod, g_post)


def _matmul_nt(a, b, out_dtype, name):
    m, k = a.shape
    n = b.shape[0]
    tn = COL_TILE

    def body(a_ref, b_ref, o_ref):
        o_ref[...] = lax.dot_general(a_ref[...], b_ref[...], (((1,), (1,)), ((), ())),
                                     preferred_element_type=F32).astype(out_dtype)

    return pl.pallas_call(
        body, name=name, grid=(n // tn,),
        out_shape=jax.ShapeDtypeStruct((m, n), out_dtype),
        in_specs=[pl.BlockSpec((m, k), lambda i: (0, 0)), pl.BlockSpec((tn, k), lambda i: (i, 0))],
        out_specs=pl.BlockSpec((m, tn), lambda i: (0, i)),
        compiler_params=_params(("parallel",)),
    )(a, b)


def _mix_bwd(dycat, co, proj, o_mix, g_conv, g_attn_pairs):
    t, dc = co.shape
    hp = o_mix.shape[0]
    da = hp * PAIR
    tb = ROW_TILE

    def body(dy_ref, co_ref, bg_ref, zc_ref, za_ref, om_ref, gc_ref, ga_ref,
             dcp_ref, dco_ref, do_ref, dl_ref, dgc_ref, dga_ref):
        first = pl.program_id(0) == 0
        cov = co_ref[...]
        bg = bg_ref[...].astype(F32)
        zc = zc_ref[...].astype(F32)
        p = bg * cov
        rc = lax.rsqrt(jnp.mean(p * p, axis=-1, keepdims=True) + EPS)
        nh = p * rc
        dyc = dy_ref[:, 0:dc].astype(F32)
        dn = dyc * _silu(zc)
        a = dn * gc_ref[...]
        dp = rc * (a - nh * jnp.mean(a * nh, axis=-1, keepdims=True))
        dcp_ref[:, 0:dc] = jnp.zeros((tb, dc), BF16)
        dcp_ref[:, dc:2 * dc] = (dp * cov).astype(BF16)
        dcp_ref[:, 2 * dc:3 * dc] = jnp.zeros((tb, dc), BF16)
        dcp_ref[:, 3 * dc:4 * dc] = (dyc * nh * gc_ref[...] * _silu_grad(zc)).astype(BF16)
        dcp_ref[:, 4 * dc:4 * dc + 3 * da] = jnp.zeros((tb, 3 * da), BF16)
        dco_ref[...] = dp * bg

        @pl.when(first)
        def _():
            dgc_ref[...] = jnp.zeros(dgc_ref.shape, F32)
            dga_ref[...] = jnp.zeros(dga_ref.shape, F32)

        dgc_ref[...] += jnp.sum(dn * nh, axis=0, keepdims=True)

        ssq = jnp.zeros((tb, 1), F32)
        for h in range(hp):
            o = om_ref[h]
            ssq = ssq + jnp.sum(o * o, axis=-1, keepdims=True)
        ra = lax.rsqrt(ssq * (1.0 / da) + EPS)
        dot_an = jnp.zeros((tb, 1), F32)
        for h in range(hp):
            nha = om_ref[h] * ra
            za = za_ref[:, h * PAIR:(h + 1) * PAIR].astype(F32)
            dya = dy_ref[:, dc + h * PAIR:dc + (h + 1) * PAIR].astype(F32)
            dna = dya * _silu(za)
            dza = (dya * nha * ga_ref[h] * _silu_grad(za)).astype(BF16)
            dcp_ref[:, 4 * dc + 3 * da + h * PAIR:4 * dc + 3 * da + (h + 1) * PAIR] = dza
            dga_ref[h] += jnp.sum(dna * nha, axis=0, keepdims=True)
            dot_an = dot_an + jnp.sum(dna * ga_ref[h] * nha, axis=-1, keepdims=True)
        mean_an = dot_an * (1.0 / da)
        first_head = lax.broadcasted_iota(jnp.int32, (tb, PAIR), 1) < HEAD_DIM
        for h in range(hp):
            o = om_ref[h]
            nha = o * ra
            za = za_ref[:, h * PAIR:(h + 1) * PAIR].astype(F32)
            dya = dy_ref[:, dc + h * PAIR:dc + (h + 1) * PAIR].astype(F32)
            aa = dya * _silu(za) * ga_ref[h]
            d_o = ra * (aa - nha * mean_an)
            do_ref[h] = d_o.astype(BF16)
            prod = d_o * o
            both = jnp.sum(prod, axis=-1, keepdims=True)
            head0 = jnp.sum(jnp.where(first_head, prod, 0.0), axis=-1, keepdims=True)
            dl_ref[h] = jnp.where(first_head, head0, both - head0)

    pair_spec = pl.BlockSpec((hp, tb, PAIR), lambda i: (0, i, 0))
    return pl.pallas_call(
        body, name="mix_bwd", grid=(t // tb,),
        out_shape=(jax.ShapeDtypeStruct((t, 4 * dc + 4 * da), BF16), jax.ShapeDtypeStruct((t, dc), F32),
                   jax.ShapeDtypeStruct((hp, t, PAIR), BF16), jax.ShapeDtypeStruct((hp, t, PAIR), F32),
                   jax.ShapeDtypeStruct((1, dc), F32), jax.ShapeDtypeStruct((hp, 1, PAIR), F32)),
        in_specs=[pl.BlockSpec((tb, dc + da), lambda i: (i, 0)),
                  pl.BlockSpec((tb, dc), lambda i: (i, 0)),
                  pl.BlockSpec((tb, dc), lambda i: (i, 1)),
                  pl.BlockSpec((tb, dc), lambda i: (i, 3)),
                  pl.BlockSpec((tb, da), lambda i: (i, 7)),
                  pair_spec,
                  pl.BlockSpec((1, dc), lambda i: (0, 0)),
                  pl.BlockSpec((hp, 1, PAIR), lambda i: (0, 0, 0))],
        out_specs=(pl.BlockSpec((tb, 4 * dc + 4 * da), lambda i: (i, 0)), pl.BlockSpec((tb, dc), lambda i: (i, 0)),
                   pair_spec, pair_spec,
                   pl.BlockSpec((1, dc), lambda i: (0, 0)), pl.BlockSpec((hp, 1, PAIR), lambda i: (0, 0, 0))),
        compiler_params=_params(("arbitrary",)),
    )(dycat, co, proj, proj, proj, o_mix, g_conv, g_attn_pairs)


def _conv_bwd(dconv_proj, dco, conv_proj, conv_w, dc, after):
    t = dco.shape[0]
    ct = CONV_TILE
    nct = dc // ct

    def body(dcp_in_ref, dco_ref, u_ref, cg_ref, w_ref, after_ref, dcp_ref, acc_ref):
        del dcp_in_ref, after_ref
        which = pl.program_id(1)
        g = dco_ref[...]
        u = u_ref[...].astype(F32)
        cg = cg_ref[...].astype(F32)
        g_prev, g_next = _shift_rows(g, t)
        da = w_ref[0:1, :] * g_next + w_ref[1:2, :] * g + w_ref[2:3, :] * g_prev
        dcp_ref[...] = (da * jnp.where(which == 0, cg, u)).astype(BF16)
        a = cg * u
        a_prev, a_next = _shift_rows(a, t)
        acc_ref[...] = jnp.concatenate(
            [jnp.sum(g * a_prev, axis=0, keepdims=True), jnp.sum(g * a, axis=0, keepdims=True),
             jnp.sum(g * a_next, axis=0, keepdims=True), jnp.sum(g, axis=0, keepdims=True),
             jnp.zeros((4, ct), F32)], axis=0)

    return pl.pallas_call(
        body, name="conv_bwd", grid=(nct, 2),
        out_shape=(jax.ShapeDtypeStruct(dconv_proj.shape, BF16), jax.ShapeDtypeStruct((8, dc), F32)),
        in_specs=[HBM,
                  pl.BlockSpec((t, ct), lambda i, s: (0, i)),
                  pl.BlockSpec((t, ct), lambda i, s: (0, i)),
                  pl.BlockSpec((t, ct), lambda i, s: (0, 2 * nct + i)),
                  pl.BlockSpec((3, ct), lambda i, s: (0, i)), ANY],
        out_specs=(pl.BlockSpec((t, ct), lambda i, s: (0, 2 * s * nct + i)),
                   pl.BlockSpec((8, ct), lambda i, s: (0, i))),
        input_output_aliases={0: 0},
        compiler_params=_params(("arbitrary", "arbitrary")),
    )(dconv_proj, dco, conv_proj, conv_proj, conv_w, after)


def _dh(dproj, winf, after):
    t = dproj.shape[0]
    _, d, ws = winf.shape
    tm = tn = COL_TILE
    nt = (((1,), (1,)), ((), ()))

    def body(a_ref, w_ref, after_ref, o_ref):
        del after_ref
        acc = lax.dot_general(a_ref[:, 0:ws], w_ref[0], nt, preferred_element_type=F32)
        for j in range(1, N_CHIPS):
            acc = acc + lax.dot_general(a_ref[:, j * ws:(j + 1) * ws], w_ref[j], nt, preferred_element_type=F32)
        o_ref[...] = acc

    return pl.pallas_call(
        body, name="dh", grid=(d // tn, t // tm),
        out_shape=jax.ShapeDtypeStruct((t, d), F32),
        in_specs=[pl.BlockSpec((tm, N_CHIPS * ws), lambda n, m: (m, 0)),
                  pl.BlockSpec((N_CHIPS, tn, ws), lambda n, m: (0, n, 0)), ANY],
        out_specs=pl.BlockSpec((tm, tn), lambda n, m: (m, n)),
        compiler_params=_params(("parallel", "parallel")),
    )(dproj, winf, after)


def _prenorm_bwd(x, dh, dout, mod, g_pre, after):
    t, d = x.shape
    tb = ROW_TILE

    def body(x_ref, dh_ref, dout_ref, mod_ref, g_ref, after_ref, gx_ref, acc_ref):
        del after_ref
        xv = x_ref[...]
        dhv = dh_ref[...]
        r = lax.rsqrt(jnp.mean(xv * xv, axis=-1, keepdims=True) + EPS)
        xh = xv * r
        one_scale = 1.0 + mod_ref[1:2, :]
        a = dhv * one_scale * g_ref[...]
        gx_ref[...] = dout_ref[...] + r * (a - xh * jnp.mean(a * xh, axis=-1, keepdims=True))
        part = jnp.concatenate(
            [jnp.sum(dhv, axis=0, keepdims=True), jnp.sum(dhv * xh * g_ref[...], axis=0, keepdims=True),
             jnp.sum(dhv * xh * one_scale, axis=0, keepdims=True), jnp.zeros((5, d), F32)], axis=0)

        @pl.when(pl.program_id(0) == 0)
        def _():
            acc_ref[...] = jnp.zeros(acc_ref.shape, F32)

        acc_ref[...] += part

    return pl.pallas_call(
        body, name="prenorm_bwd", grid=(t // tb,),
        out_shape=(jax.ShapeDtypeStruct((t, d), F32), jax.ShapeDtypeStruct((8, d), F32)),
        in_specs=[pl.BlockSpec((tb, d), lambda i: (i, 0)), pl.BlockSpec((tb, d), lambda i: (i, 0)),
                  pl.BlockSpec((tb, d), lambda i: (i, 0)), pl.BlockSpec((3, d), lambda i: (0, 0)),
                  pl.BlockSpec((1, d), lambda i: (0, 0)), ANY],
        out_specs=(pl.BlockSpec((tb, d), lambda i: (i, 0)), pl.BlockSpec((8, d), lambda i: (0, 0))),
        compiler_params=_params(("arbitrary",)),
    )(x, dh, dout, mod, g_pre, after)


def _chip_sums(mine, rsib, name):
    _, half, cols = mine.shape
    tr = min(half, ROW_TILE)

    def body(g_ref, r_ref, o_ref):
        o_ref[...] = (g_ref[...].astype(F32) + r_ref[...].astype(F32)).astype(BF16)

    spec = pl.BlockSpec((None, tr, cols), lambda j, i: (j, i, 0))
    return pl.pallas_call(
        body, name=name, grid=(N_CHIPS, half // tr),
        out_shape=jax.ShapeDtypeStruct(mine.shape, BF16),
        in_specs=[spec, spec], out_specs=spec,
        compiler_params=_params(("parallel", "parallel")),
    )(mine, rsib)


def _owner_sum(place, mine, rsib, rici, name):
    _, half, cols = mine.shape
    rows = 2 * half
    tr = min(half, ROW_TILE)
    nt = half // tr

    def body(place_ref, g_ref, r_ref, i_ref, o_ref):
        del place_ref
        acc = g_ref[...].astype(F32) + r_ref[...].astype(F32)
        for k in range(N_CHIPS - 1):
            acc = acc + i_ref[k].astype(F32)
        o_ref[...] = acc

    grid_spec = pltpu.PrefetchScalarGridSpec(
        num_scalar_prefetch=1, grid=(nt,),
        in_specs=[pl.BlockSpec((None, tr, cols), lambda i, p: (p[0], i, 0)),
                  pl.BlockSpec((None, tr, cols), lambda i, p: (p[0], i, 0)),
                  pl.BlockSpec((N_CHIPS - 1, tr, cols), lambda i, p: (0, i, 0))],
        out_specs=pl.BlockSpec((tr, cols), lambda i, p: (p[1] * nt + i, 0)))
    return pl.pallas_call(
        body, name=name, grid_spec=grid_spec,
        out_shape=jax.ShapeDtypeStruct((rows, cols), F32),
        compiler_params=_params(("parallel",)),
    )(place, mine, rsib, rici)


def _adam_math(w, g, m, v):
    m2 = ADAM_B1 * m + (1.0 - ADAM_B1) * g
    v2 = ADAM_B2 * v + (1.0 - ADAM_B2) * (g * g)
    m_hat = m2 / (1.0 - ADAM_B1 ** ADAM_STEP)
    v_hat = v2 / (1.0 - ADAM_B2 ** ADAM_STEP)
    delta = -ADAM_LR * (m_hat / (jnp.sqrt(v_hat) + ADAM_EPS) + ADAM_WD * w)
    return delta, m2, v2


def _adamw(w, g, m, v, name):
    rows, cols = w.shape
    tr = min(rows, ROW_TILE)

    def body(w_ref, g_ref, m_ref, v_ref, go_ref, d_ref, m2_ref, v2_ref):
        g = g_ref[...]
        go_ref[...] = g
        d_ref[...], m2_ref[...], v2_ref[...] = _adam_math(w_ref[...], g, m_ref[...], v_ref[...])

    spec = pl.BlockSpec((tr, cols), lambda i: (i, 0))
    return pl.pallas_call(
        body, name=name, grid=(rows // tr,),
        out_shape=(jax.ShapeDtypeStruct(w.shape, F32),) * 4,
        in_specs=[spec] * 4, out_specs=(spec,) * 4,
        compiler_params=_params(("parallel",)),
    )(w, g, m, v)


def _ada_grad_adamw(c_all_t, dmod_cols, w, m, v):
    d, wa = w.shape
    tr = ROW_TILE

    def body(ct_ref, dm_ref, w_ref, m_ref, v_ref, g_ref, d_ref, m2_ref, v2_ref):
        act = _silu(ct_ref[...])
        g = act[:, 0:1] * dm_ref[0:1, :]
        for b in range(1, N_DEV):
            g = g + act[:, b:b + 1] * dm_ref[b:b + 1, :]
        g_ref[...] = g
        d_ref[...], m2_ref[...], v2_ref[...] = _adam_math(w_ref[...], g, m_ref[...], v_ref[...])

    spec = pl.BlockSpec((tr, wa), lambda i: (i, 0))
    return pl.pallas_call(
        body, name="ada_grad_adamw", grid=(d // tr,),
        out_shape=(jax.ShapeDtypeStruct(w.shape, F32),) * 4,
        in_specs=[pl.BlockSpec((tr, N_DEV), lambda i: (i, 0)), pl.BlockSpec((N_DEV, wa), lambda i: (0, 0)),
                  spec, spec, spec],
        out_specs=(spec,) * 4,
        compiler_params=_params(("parallel",)),
    )(c_all_t, dmod_cols, w, m, v)


def _sum_devices(gathered):
    n = gathered.shape[1]

    def body(g_ref, o_ref):
        acc = g_ref[0:8, :]
        for dev in range(1, N_DEV):
            acc = acc + g_ref[8 * dev:8 * dev + 8, :]
        o_ref[...] = acc

    return pl.pallas_call(
        body, name="sum_devices",
        out_shape=jax.ShapeDtypeStruct((8, n), F32),
        in_specs=[VMEM], out_specs=VMEM,
    )(gathered)


def _pack_small(pieces):
    flat = [p.reshape(-1).astype(F32) for p in pieces]
    offsets, total = [], 0
    for p in flat:
        offsets.append(total)
        total += p.shape[0]
    padded = -(-total // SMALL_ALIGN) * SMALL_ALIGN
    if padded > total:
        flat.append(jnp.zeros((padded - total,), F32))
    return jnp.concatenate(flat).reshape(8, padded // 8), offsets


def _alibi_slope_rows(n_heads):
    slopes = 2.0 ** (-8.0 * jnp.arange(1, n_heads + 1, dtype=F32) / n_heads)
    rows = jnp.zeros((n_heads // 2, 8), F32).at[:, 0:2].set(slopes.reshape(n_heads // 2, 2))
    return jnp.broadcast_to(rows[:, :, None], (n_heads // 2, 8, ATT_KW))


def kernel(x, c, w_ada, b_ada, g_pre, w_in, conv_w, conv_b, g_conv, g_attn, w_out, g_post, loss_target, m_w_ada, m_b_ada, m_g_pre, m_w_in, m_conv_w, m_conv_b, m_g_conv, m_g_attn, m_w_out, m_g_post, v_w_ada, v_b_ada, v_g_pre, v_w_in, v_conv_w, v_conv_b, v_g_conv, v_g_attn, v_w_out, v_g_post):
    t, d = x.shape[1], x.shape[2]
    dc = conv_b.shape[1]
    da = g_attn.shape[1]
    hp = da // PAIR
    ws = w_in.shape[2]
    wa = w_ada.shape[2]
    cws = conv_w.shape[2]
    assert t % ROW_TILE == 0 and d % ROW_TILE == 0 and dc % COL_TILE == 0 and da % COL_TILE == 0
    assert ws == 2 * dc and dc == da and t // BRANCHES[-1][1] >= ATT_BQ

    mx, my, mc = _my_place()
    chip = _chip_of(mx, my)
    dev = 2 * chip + mc
    place = jnp.stack([chip, mc]).astype(jnp.int32)

    x2, tgt2 = x[0], loss_target[0]
    w_ada2, w_in2, w_out2 = w_ada[0], w_in[0], w_out[0]

    win_slots = _cast_into_slot(place, w_in2, "cast_w_in")
    packed, offs = _pack_small([c[0], conv_w[0]])
    seen = _allgather8(packed, "gather_inputs", after=(win_slots,)).reshape(N_DEV, -1)
    c_all = seen[:, offs[0]:offs[0] + d]
    conv_w_full = seen[0::2, offs[1]:offs[1] + 3 * cws].reshape(N_CHIPS, 3, cws).transpose(1, 0, 2).reshape(3, dc)

    ada_part = _ada_partial(c_all, w_ada2)
    ada_seen = _allgather8(ada_part, "gather_ada").reshape(N_DEV, N_DEV, wa)
    mod_flat = lax.dynamic_index_in_dim(ada_seen[0::2], dev, axis=1, keepdims=False).reshape(1, 3 * d) + b_ada
    mod = mod_flat.reshape(3, d)

    win_flight, send_in, recv_in, started = _gather_start(win_slots, mod)

    y_chip, x_chip, d_chip = (_chip_of(mx, 1 - my), _chip_of(1 - mx, my), _chip_of(1 - mx, 1 - my))
    own_chunk, near_chunks, far_chunk = (jnp.stack(js).astype(jnp.int32) for js in ([chip], [y_chip, x_chip], [d_chip]))
    h, ht = _prenorm(x2, mod + started[0, 0], g_pre)
    proj = _proj_chunks(None, h, w_in2, own_chunk, "proj_own")
    win_flight, wout_flight, relay_send_in, relay_recv_in, send_out, recv_out = _gather_relay_in(
        win_flight, _cast_into_slot(place, w_out2, "cast_w_out"), recv_in, proj)
    win_flight = _forward_halves(
        _gather_wait_direct(win_flight, send_in, recv_in, proj, "gather_wait_w_in_direct"), (0, 1), "forward_w_in_direct")
    proj = _proj_chunks(proj, h, win_flight, near_chunks, "proj_neighbours")
    winf = _forward_halves(
        _gather_wait_relayed(win_flight, relay_send_in, relay_recv_in, proj, "gather_wait_w_in_relayed"),
        (2,), "forward_w_in_relayed")
    proj = _proj_chunks(proj, h, winf, far_chunk, "proj_diagonal")
    slopes = _alibi_slope_rows(da // HEAD_DIM)
    co = _conv_fwd(proj, conv_w_full, conv_b, dc)
    wout_flight, relay_send_out, relay_recv_out = _gather_relay_out(wout_flight, recv_out, co)
    o_mix, lse = _attn_fwd(proj, slopes, dc, da)
    g_attn_pairs = g_attn.reshape(hp, 1, PAIR)
    ycat, ycat_t = _mix_fwd(co, proj, o_mix, g_conv, g_attn_pairs)
    wout_flight = _gather_wait_direct(wout_flight, send_out, recv_out, ycat, "gather_wait_w_out_direct")
    wout_flight = _gather_wait_relayed(wout_flight, relay_send_out, relay_recv_out, ycat, "gather_wait_w_out_relayed")
    woutf = _forward_halves(wout_flight, (0, 1, 2), "forward_w_out").reshape(dc + da, d)
    dout, dy, post_sums = _out_fwd_bwd(ycat, woutf, x2, tgt2, mod, g_post)

    gout, rsib_out = _dw_swapped(ycat_t, dy, N_CHIPS, 1, "dw_out")
    csum_out = _chip_sums(gout, rsib_out, "rs_chip_sum_out")
    ssem_out, rsem_out, csum_out, land_out, sent_out = _owners_start(csum_out, "rs_owners_start_out")
    dycat = _matmul_nt(dy, woutf, BF16, "dycat")
    dproj, dco, d_o, delta, dg_conv, dg_attn = _mix_bwd(dycat, co, proj, o_mix, g_conv, g_attn_pairs)
    dproj, conv_sums = _conv_bwd(dproj, dco, proj, conv_w_full, dc, sent_out)
    dproj = _attn_bwd(dproj, proj, d_o, lse, delta, slopes, dc, da, sent_out)
    gin, rsib_in = _dw_swapped(ht, dproj, 1, N_CHIPS, "dw_in")
    csum_in = _chip_sums(gin, rsib_in, "rs_chip_sum_in")
    ssem_in, rsem_in, csum_in, land_in, relay_in, sent_in = _owners_start_relayed(csum_in, "rs_owners_start_in")
    dh = _dh(dproj, winf, sent_in)
    land_in, relay_in, fsend_in, frecv_in, relayed_in = _owners_relay(land_in, relay_in, rsem_in, dh, "rs_owners_relay_in")
    grad_x, pre_sums = _prenorm_bwd(x2, dh, dout, mod, g_pre, relayed_in)

    small, so = _pack_small([
        pre_sums[0], pre_sums[1], post_sums[0],
        pre_sums[2], conv_sums[0:3], conv_sums[3], dg_conv, dg_attn, post_sums[1], post_sums[2, 0:128]])
    ssem_small, rsem_small, small, land_small, sent_small = _allgather8_start(small, dev, "gather_small_start")

    rici_out = _owners_wait(ssem_out, rsem_out, csum_out, land_out, [grad_x, sent_small], "rs_owners_wait_out")
    grad_w_out = _join_halves(_owner_sum(place, gout, rsib_out, rici_out, "rs_owner_sum_out"), "rs_join_halves_out")
    grad_w_out, delta_w_out, new_m_w_out, new_v_w_out = _adamw(
        w_out2, grad_w_out, m_w_out[0], v_w_out[0], "adamw_w_out")

    small_seen = _allgather8_wait(ssem_small, rsem_small, small, land_small, [delta_w_out], "gather_small_wait")
    total = _sum_devices(small_seen).reshape(-1)
    dmod_all = small_seen.reshape(N_DEV, -1)[:, 0:3 * d]
    loss = total[so[9]]
    grad_b_ada = total[0:3 * d].reshape(1, 3 * d)
    grad_g_pre = total[so[3]:so[3] + d].reshape(1, d)
    grad_conv_w_full = total[so[4]:so[4] + 3 * dc].reshape(3, dc)
    grad_conv_w = lax.dynamic_slice_in_dim(grad_conv_w_full, chip * cws, cws, axis=1).reshape(1, 3, cws)
    grad_conv_b = total[so[5]:so[5] + dc].reshape(1, dc)
    grad_g_conv = total[so[6]:so[6] + dc].reshape(1, dc)
    grad_g_attn = total[so[7]:so[7] + da].reshape(1, da)
    grad_g_post = total[so[8]:so[8] + d].reshape(1, d)

    dmod_cols = lax.dynamic_slice_in_dim(dmod_all, chip * wa, wa, axis=1)
    grad_w_ada, delta_w_ada, new_m_w_ada, new_v_w_ada = _ada_grad_adamw(c_all.T, dmod_cols, w_ada2, m_w_ada[0], v_w_ada[0])

    small_w = [b_ada, g_pre, conv_w, conv_b, g_conv, g_attn, g_post]
    small_g = [grad_b_ada, grad_g_pre, grad_conv_w, grad_conv_b, grad_g_conv, grad_g_attn, grad_g_post]
    small_m = [m_b_ada, m_g_pre, m_conv_w, m_conv_b, m_g_conv, m_g_attn, m_g_post]
    small_v = [v_b_ada, v_g_pre, v_conv_w, v_conv_b, v_g_conv, v_g_attn, v_g_post]
    pw, po = _pack_small(small_w)
    pg, _ = _pack_small(small_g)
    pm, _ = _pack_small(small_m)
    pv, _ = _pack_small(small_v)
    sd, sm, sv = (a.reshape(-1) for a in _adamw(pw, pg, pm, pv, "adamw_small")[1:])

    def unpack(flat):
        return [flat[o:o + w.size].reshape(w.shape) for o, w in zip(po, small_w)]

    d_small, m_small, v_small = unpack(sd), unpack(sm), unpack(sv)

    rici_in = _owners_wait_relayed(ssem_in, rsem_in, fsend_in, frecv_in, csum_in, land_in, relay_in,
                                   [sd, delta_w_out, delta_w_ada], "rs_owners_wait_in")
    grad_w_in = _join_halves(_owner_sum(place, gin, rsib_in, rici_in, "rs_owner_sum_in"), "rs_join_halves_in")
    grad_w_in, delta_w_in, new_m_w_in, new_v_w_in = _adamw(w_in2, grad_w_in, m_w_in[0], v_w_in[0], "adamw_w_in")

    def lead(a):
        return a.reshape((1,) + a.shape)

    grads = [lead(grad_w_ada), grad_b_ada, grad_g_pre, lead(grad_w_in), grad_conv_w, grad_conv_b, grad_g_conv,
             grad_g_attn, lead(grad_w_out), grad_g_post]
    deltas = [lead(delta_w_ada), d_small[0], d_small[1], lead(delta_w_in), d_small[2], d_small[3], d_small[4],
              d_small[5], lead(delta_w_out), d_small[6]]
    new_ms = [lead(new_m_w_ada), m_small[0], m_small[1], lead(new_m_w_in), m_small[2], m_small[3], m_small[4],
              m_small[5], lead(new_m_w_out), m_small[6]]
    new_vs = [lead(new_v_w_ada), v_small[0], v_small[1], lead(new_v_w_in), v_small[2], v_small[3], v_small[4],
              v_small[5], lead(new_v_w_out), v_small[6]]
    return (loss, lead(grad_x), *grads, *deltas, *new_ms, *new_vs)
```

```python
import functools

import jax
import jax.numpy as jnp
from jax import lax
from jax.experimental import pallas as pl
from jax.experimental.pallas import tpu as pltpu

F32 = jnp.float32
BF16 = jnp.bfloat16
MESH = pl.DeviceIdType.MESH
HBM = pl.BlockSpec(memory_space=pltpu.HBM)
VMEM = pl.BlockSpec(memory_space=pltpu.VMEM)
ANY = pl.BlockSpec(memory_space=pl.ANY)
SEM = pl.BlockSpec(memory_space=pltpu.SEMAPHORE)
EFFECT = pltpu.SideEffectType.DATAFLOW_SIDE_EFFECTING
SUBLANES, LANES = 8, 128
TOKEN = jax.ShapeDtypeStruct((SUBLANES, LANES), jnp.float32)

HEAD_DIM = 64
PAIR = 2 * HEAD_DIM
assert PAIR == LANES
BRANCHES = ((128, 1), (512, 4), (2048, 16))
SIDE = 64
EPS = 1e-6
NEG_INF = -1e30
N_CHIPS = 4
N_DEV = 8

ADAM_LR = 0.001
ADAM_B1 = 0.9
ADAM_B2 = 0.999
ADAM_EPS = 1e-08
ADAM_WD = 0.01
ADAM_STEP = 10

VMEM_LIMIT_BYTES = 56 * 1024 * 1024
ROW_TILE = 256
COL_TILE = 512
CONV_TILE = 256
ATT_BQ = 128
ATT_KW = ATT_BQ + 2 * SIDE
ATT_UNROLL = 4
SMALL_ALIGN = SUBLANES * LANES


def _params(semantics=None):
    kw = {"vmem_limit_bytes": VMEM_LIMIT_BYTES}
    if semantics is not None:
        kw["dimension_semantics"] = semantics
    return pltpu.CompilerParams(**kw)


def _silu(z):
    return z * jax.nn.sigmoid(z)


def _silu_grad(z):
    s = jax.nn.sigmoid(z)
    return s * (1.0 + z * (1.0 - s))


def _my_place():
    return lax.axis_index("x"), lax.axis_index("y"), lax.axis_index("c")


def _flip(a, bit):
    return 1 - a if bit else a


def _chip_of(x, y):
    return 2 * x + y


def _allgather8(v, name, after=()):
    rows_per, n = v.shape

    def body(v_ref, *rest):
        out_ref, send_sems, recv_sems = rest[len(after):]
        x, y, c = _my_place()
        me = 4 * x + 2 * y + c

        def rows(idx):
            return out_ref.at[pl.ds(pl.multiple_of(idx * rows_per, rows_per), rows_per), :]

        out_ref[pl.ds(pl.multiple_of(me * rows_per, rows_per), rows_per), :] = v_ref[...]
        copies = []
        for k in range(1, N_DEV):
            peer = (_flip(x, k & 4), _flip(y, k & 2), _flip(c, k & 1))
            cp = pltpu.make_async_remote_copy(
                src_ref=v_ref, dst_ref=rows(me), send_sem=send_sems.at[k - 1], recv_sem=recv_sems.at[k - 1],
                device_id=peer, device_id_type=MESH)
            cp.start()
            copies.append((cp, peer))
        for k, (cp, peer) in enumerate(copies):
            src = 4 * peer[0] + 2 * peer[1] + peer[2]
            pltpu.make_async_remote_copy(
                src_ref=v_ref, dst_ref=rows(src), send_sem=send_sems.at[k], recv_sem=recv_sems.at[k],
                device_id=peer, device_id_type=MESH).wait_recv()
        for cp, _ in copies:
            cp.wait_send()

    return pl.pallas_call(
        body, name=name,
        out_shape=jax.ShapeDtypeStruct((N_DEV * rows_per, n), v.dtype),
        in_specs=[VMEM] + [ANY] * len(after), out_specs=VMEM,
        scratch_shapes=[pltpu.SemaphoreType.DMA((N_DEV - 1,)), pltpu.SemaphoreType.DMA((N_DEV - 1,))],
    )(v, *after)


def _allgather8_start(v, me, name):
    rows_per, n = v.shape
    land = lax.dynamic_update_slice(jnp.zeros((N_DEV * rows_per, n), v.dtype), v, (me * rows_per, 0))

    def body(v_ref, land_ref, send_sems, recv_sems, v_thru, land_thru, token_ref):
        del v_thru, land_thru
        x, y, c = _my_place()
        mine = land_ref.at[pl.ds(pl.multiple_of((4 * x + 2 * y + c) * rows_per, rows_per), rows_per), :]
        for k in range(1, N_DEV):
            peer = (_flip(x, k & 4), _flip(y, k & 2), _flip(c, k & 1))
            pltpu.make_async_remote_copy(
                src_ref=v_ref, dst_ref=mine, send_sem=send_sems.at[k - 1], recv_sem=recv_sems.at[k - 1],
                device_id=peer, device_id_type=MESH).start()
        token_ref[...] = jnp.zeros(token_ref.shape, F32)

    sems = pltpu.SemaphoreType.DMA((N_DEV - 1,))
    return pl.pallas_call(
        body, name=name,
        out_shape=(sems, sems, jax.ShapeDtypeStruct(v.shape, v.dtype), jax.ShapeDtypeStruct(land.shape, land.dtype), TOKEN),
        in_specs=[HBM, HBM], out_specs=(SEM, SEM, HBM, HBM, VMEM),
        input_output_aliases={0: 2, 1: 3},
        compiler_params=pltpu.CompilerParams(has_side_effects=EFFECT),
    )(pltpu.with_memory_space_constraint(v, pltpu.HBM), pltpu.with_memory_space_constraint(land, pltpu.HBM))


def _allgather8_wait(send_sems, recv_sems, v, land, after, name):
    rows_per = v.shape[0]

    def body(v_ref, land_ref, send_ref, recv_ref, *rest):
        del rest
        x, y, c = _my_place()
        for k in range(1, N_DEV):
            peer = (_flip(x, k & 4), _flip(y, k & 2), _flip(c, k & 1))
            src = 4 * peer[0] + 2 * peer[1] + peer[2]
            cp = pltpu.make_async_remote_copy(
                src_ref=v_ref, dst_ref=land_ref.at[pl.ds(pl.multiple_of(src * rows_per, rows_per), rows_per), :],
                send_sem=send_ref.at[k - 1], recv_sem=recv_ref.at[k - 1], device_id=peer, device_id_type=MESH)
            cp.wait_send()
            cp.wait_recv()

    return pl.pallas_call(
        body, name=name,
        out_shape=(jax.ShapeDtypeStruct(v.shape, v.dtype), jax.ShapeDtypeStruct(land.shape, land.dtype)),
        in_specs=[HBM, HBM, SEM, SEM] + [ANY] * len(after), out_specs=(HBM, HBM),
        input_output_aliases={0: 0, 1: 1},
        compiler_params=pltpu.CompilerParams(has_side_effects=EFFECT),
    )(v, land, send_sems, recv_sems, *after)[1]


def _half_rows(ref, chip, which, half):
    return ref.at[chip, pl.ds(pl.multiple_of(which * half, half), half), :]


def _ici_peers(x, y, c):
    peers = [(_flip(x, k & 2), _flip(y, k & 1), c) for k in (1, 2, 3)]
    return [(peer, _chip_of(peer[0], peer[1])) for peer in peers]


def _part_rows(ref, chip, core, part):
    quarter = ref.shape[1] // 4
    return ref.at[chip, pl.ds(pl.multiple_of((2 * core + part) * quarter, quarter), quarter), :]


def _neighbours(x, y, c):
    return [((x, 1 - y, c), _chip_of(x, 1 - y)), ((1 - x, y, c), _chip_of(1 - x, y)),
            ((1 - x, 1 - y, c), _chip_of(1 - x, 1 - y))]


def _start_direct(buf, send_sems, recv_sems):
    x, y, c = _my_place()
    me = _chip_of(x, y)
    for n, (peer, _) in enumerate(_neighbours(x, y, c)[0:2]):
        for part in ((0, 1), (1, 0))[n]:
            piece = _part_rows(buf, me, c, part)
            pltpu.make_async_remote_copy(
                src_ref=piece, dst_ref=piece, send_sem=send_sems.at[2 * n + part], recv_sem=recv_sems.at[2 * n + part],
                device_id=peer, device_id_type=MESH).start()


def _relay(buf, recv_sems, relay_send, relay_recv):
    x, y, c = _my_place()
    nbrs = _neighbours(x, y, c)
    for n in range(2):
        part = n
        piece = _part_rows(buf, nbrs[n][1], c, part)
        pltpu.make_async_remote_copy(
            src_ref=piece, dst_ref=piece, send_sem=relay_send.at[part], recv_sem=recv_sems.at[2 * n + part],
            device_id=nbrs[n][0], device_id_type=MESH).wait_recv()
        pltpu.make_async_remote_copy(
            src_ref=piece, dst_ref=piece, send_sem=relay_send.at[part], recv_sem=relay_recv.at[part],
            device_id=nbrs[1 - n][0], device_id_type=MESH).start()


def _gather_start(win_slots, after):
    def body(win_in, after_ref, win_ref, send_sems, recv_sems, token_ref):
        del win_in, after_ref
        _start_direct(win_ref, send_sems, recv_sems)
        token_ref[...] = jnp.zeros(token_ref.shape, F32)

    sems = pltpu.SemaphoreType.DMA((4,))
    return pl.pallas_call(
        body, name="gather_start",
        out_shape=(jax.ShapeDtypeStruct(win_slots.shape, win_slots.dtype), sems, sems, TOKEN),
        in_specs=[HBM, ANY], out_specs=(HBM, SEM, SEM, VMEM),
        input_output_aliases={0: 0},
        compiler_params=pltpu.CompilerParams(has_side_effects=EFFECT),
    )(win_slots, after)


def _gather_relay_in(win, wout_slots, recv_in, after):
    def body(win_in, wout_in, recv_in_ref, after_ref, win_ref, wout_ref, relay_send, relay_recv, send_out, recv_out):
        del win_in, wout_in, after_ref
        _relay(win_ref, recv_in_ref, relay_send, relay_recv)
        _start_direct(wout_ref, send_out, recv_out)

    two, four = pltpu.SemaphoreType.DMA((2,)), pltpu.SemaphoreType.DMA((4,))
    return pl.pallas_call(
        body, name="gather_relay_w_in",
        out_shape=(jax.ShapeDtypeStruct(win.shape, win.dtype), jax.ShapeDtypeStruct(wout_slots.shape, wout_slots.dtype),
                   two, two, four, four),
        in_specs=[HBM, HBM, SEM, ANY], out_specs=(HBM, HBM, SEM, SEM, SEM, SEM),
        input_output_aliases={0: 0, 1: 1},
        compiler_params=pltpu.CompilerParams(has_side_effects=EFFECT),
    )(win, wout_slots, recv_in, after)


def _gather_relay_out(wout, recv_out, after):
    def body(wout_in, recv_out_ref, after_ref, wout_ref, relay_send, relay_recv):
        del wout_in, after_ref
        _relay(wout_ref, recv_out_ref, relay_send, relay_recv)

    two = pltpu.SemaphoreType.DMA((2,))
    return pl.pallas_call(
        body, name="gather_relay_w_out",
        out_shape=(jax.ShapeDtypeStruct(wout.shape, wout.dtype), two, two),
        in_specs=[HBM, SEM, ANY], out_specs=(HBM, SEM, SEM),
        input_output_aliases={0: 0},
        compiler_params=pltpu.CompilerParams(has_side_effects=EFFECT),
    )(wout, recv_out, after)


def _gather_wait_direct(buf, send_sems, recv_sems, after, name):
    def body(buf_in, send_ref, recv_ref, after_ref, buf_ref):
        del buf_in, after_ref
        x, y, c = _my_place()
        me = _chip_of(x, y)
        for n, (peer, chip) in enumerate(_neighbours(x, y, c)[0:2]):
            second = 1 - n
            pltpu.make_async_remote_copy(
                src_ref=_part_rows(buf_ref, me, c, second), dst_ref=_part_rows(buf_ref, chip, c, second),
                send_sem=send_ref.at[2 * n + second], recv_sem=recv_ref.at[2 * n + second],
                device_id=peer, device_id_type=MESH).wait_recv()
            for part in range(2):
                piece = _part_rows(buf_ref, me, c, part)
                pltpu.make_async_remote_copy(
                    src_ref=piece, dst_ref=piece, send_sem=send_ref.at[2 * n + part], recv_sem=recv_ref.at[2 * n + part],
                    device_id=peer, device_id_type=MESH).wait_send()

    return pl.pallas_call(
        body, name=name,
        out_shape=jax.ShapeDtypeStruct(buf.shape, buf.dtype),
        in_specs=[HBM, SEM, SEM, ANY], out_specs=HBM,
        input_output_aliases={0: 0},
        compiler_params=pltpu.CompilerParams(has_side_effects=EFFECT),
    )(buf, send_sems, recv_sems, after)


def _gather_wait_relayed(buf, relay_send, relay_recv, after, name):
    def body(buf_in, rsend_ref, rrecv_ref, after_ref, buf_ref):
        del buf_in, after_ref
        x, y, c = _my_place()
        nbrs = _neighbours(x, y, c)
        for n in range(2):
            relayed = _part_rows(buf_ref, nbrs[n][1], c, n)
            cp = pltpu.make_async_remote_copy(
                src_ref=relayed, dst_ref=_part_rows(buf_ref, nbrs[2][1], c, n),
                send_sem=rsend_ref.at[n], recv_sem=rrecv_ref.at[n], device_id=nbrs[1 - n][0], device_id_type=MESH)
            cp.wait_recv()
            cp.wait_send()

    return pl.pallas_call(
        body, name=name,
        out_shape=jax.ShapeDtypeStruct(buf.shape, buf.dtype),
        in_specs=[HBM, SEM, SEM, ANY], out_specs=HBM,
        input_output_aliases={0: 0},
        compiler_params=pltpu.CompilerParams(has_side_effects=EFFECT),
    )(buf, relay_send, relay_recv, after)


def _forward_halves(buf, which, name):
    half = buf.shape[1] // 2

    def body(buf_in, buf_ref, send_sems, recv_sems):
        del buf_in
        x, y, c = _my_place()
        sibling = (x, y, 1 - c)
        chips = [_neighbours(x, y, c)[n][1] for n in which]
        started = []
        for k, src_chip in enumerate(chips):
            landed = _half_rows(buf_ref, src_chip, c, half)
            fw = pltpu.make_async_remote_copy(
                src_ref=landed, dst_ref=landed, send_sem=send_sems.at[k], recv_sem=recv_sems.at[k],
                device_id=sibling, device_id_type=MESH)
            fw.start()
            started.append(fw)
        for k, src_chip in enumerate(chips):
            other = _half_rows(buf_ref, src_chip, 1 - c, half)
            pltpu.make_async_remote_copy(
                src_ref=other, dst_ref=other, send_sem=send_sems.at[k], recv_sem=recv_sems.at[k],
                device_id=sibling, device_id_type=MESH).wait_recv()
        for fw in started:
            fw.wait_send()

    return pl.pallas_call(
        body, name=name,
        out_shape=jax.ShapeDtypeStruct(buf.shape, buf.dtype),
        in_specs=[HBM], out_specs=HBM,
        input_output_aliases={0: 0},
        scratch_shapes=[pltpu.SemaphoreType.DMA((len(which),))] * 2,
    )(buf)


def _dw_swapped(a, b, row_chunks, col_chunks, name):
    r, t = a.shape
    c_all = b.shape[1]
    chunks = row_chunks * col_chunks
    rq, cq = r // row_chunks, c_all // col_chunks
    half = rq // 2
    tn = COL_TILE
    nt = cq // tn
    steps = col_chunks * nt

    def body(a_ref, b_ref, mine_ref, sib_ref, stage, send_sems, recv_sems):
        x, y, c = _my_place()
        j, n = pl.program_id(0), pl.program_id(1)
        step = j * nt + n
        slot = step % 2
        res = jnp.dot(a_ref[...], b_ref[...], preferred_element_type=F32).astype(BF16)

        def landing(jj, nn):
            cols = pl.ds(pl.multiple_of(nn * tn, tn), tn)
            return sib_ref.at[:, :, cols] if col_chunks == 1 else sib_ref.at[pl.ds(jj, 1), :, cols]

        def copy(slot_, step_, jj, nn):
            return pltpu.make_async_remote_copy(
                src_ref=stage.at[slot_], dst_ref=landing(jj, nn), send_sem=send_sems.at[slot_],
                recv_sem=recv_sems.at[step_], device_id=(x, y, 1 - c), device_id_type=MESH)

        @pl.when(step >= 2)
        def _():
            copy(slot, step, j, n).wait_send()

        for q in range(row_chunks):
            lo = res[q * rq:q * rq + half, :]
            hi = res[q * rq + half:(q + 1) * rq, :]
            mine_ref[q] = jnp.where(c == 0, lo, hi)
            stage[slot, q] = jnp.where(c == 0, hi, lo)
        copy(slot, step, j, n).start()

        @pl.when(step == steps - 1)
        def _():
            for s in range(max(steps - 2, 0), steps):
                copy(s % 2, s, j, n).wait_send()
            for s in range(steps):
                copy(s % 2, s, j, n).wait_recv()

    shape = jax.ShapeDtypeStruct((chunks, half, cq), BF16)
    return pl.pallas_call(
        body, name=name, grid=(col_chunks, nt),
        out_shape=(shape, shape),
        in_specs=[pl.BlockSpec((r, t), lambda j, n: (0, 0)), pl.BlockSpec((t, tn), lambda j, n: (0, j * nt + n))],
        out_specs=(pl.BlockSpec((row_chunks, half, tn), lambda j, n: (j, 0, n)), ANY),
        scratch_shapes=[pltpu.VMEM((2, row_chunks, half, tn), BF16), pltpu.SemaphoreType.DMA((2,)),
                        pltpu.SemaphoreType.DMA((steps,))],
        compiler_params=_params(("arbitrary", "arbitrary")),
    )(a, b)


def _owners_start(csum, name):
    land = pltpu.with_memory_space_constraint(lax.empty((N_CHIPS - 1,) + csum.shape[1:], csum.dtype), pltpu.HBM)

    def body(csum_ref, land_ref, send_sems, recv_sems, csum_thru, land_thru, token_ref):
        del csum_thru, land_thru
        x, y, c = _my_place()
        for k, (peer, owner) in enumerate(_ici_peers(x, y, c)):
            pltpu.make_async_remote_copy(
                src_ref=csum_ref.at[owner], dst_ref=land_ref.at[k], send_sem=send_sems.at[k], recv_sem=recv_sems.at[k],
                device_id=peer, device_id_type=MESH).start()
        token_ref[...] = jnp.zeros(token_ref.shape, F32)

    sems = pltpu.SemaphoreType.DMA((N_CHIPS - 1,))
    return pl.pallas_call(
        body, name=name,
        out_shape=(sems, sems, jax.ShapeDtypeStruct(csum.shape, csum.dtype),
                   jax.ShapeDtypeStruct(land.shape, land.dtype), TOKEN),
        in_specs=[HBM, HBM], out_specs=(SEM, SEM, HBM, HBM, VMEM),
        input_output_aliases={0: 2, 1: 3},
        compiler_params=pltpu.CompilerParams(has_side_effects=EFFECT),
    )(pltpu.with_memory_space_constraint(csum, pltpu.HBM), land)


def _owners_wait(send_sems, recv_sems, csum, land, after, name):
    def body(csum_ref, land_ref, send_ref, recv_ref, *rest):
        del rest
        x, y, c = _my_place()
        for k, (peer, owner) in enumerate(_ici_peers(x, y, c)):
            cp = pltpu.make_async_remote_copy(
                src_ref=csum_ref.at[owner], dst_ref=land_ref.at[k], send_sem=send_ref.at[k], recv_sem=recv_ref.at[k],
                device_id=peer, device_id_type=MESH)
            cp.wait_send()
            cp.wait_recv()

    return pl.pallas_call(
        body, name=name,
        out_shape=(jax.ShapeDtypeStruct(csum.shape, csum.dtype), jax.ShapeDtypeStruct(land.shape, land.dtype)),
        in_specs=[HBM, HBM, SEM, SEM] + [ANY] * len(after), out_specs=(HBM, HBM),
        input_output_aliases={0: 0, 1: 1},
        compiler_params=pltpu.CompilerParams(has_side_effects=EFFECT),
    )(csum, land, send_sems, recv_sems, *after)[1]


def _join_halves(full, name):
    rows = full.shape[0] // 2

    def body(full_in, full_ref, send_sem, recv_sem):
        del full_in
        x, y, c = _my_place()
        sibling = (x, y, 1 - c)
        mine = full_ref.at[pl.ds(pl.multiple_of(c * rows, rows), rows), :]
        theirs = full_ref.at[pl.ds(pl.multiple_of((1 - c) * rows, rows), rows), :]
        cp = pltpu.make_async_remote_copy(
            src_ref=mine, dst_ref=mine, send_sem=send_sem, recv_sem=recv_sem, device_id=sibling, device_id_type=MESH)
        cp.start()
        pltpu.make_async_remote_copy(
            src_ref=theirs, dst_ref=theirs, send_sem=send_sem, recv_sem=recv_sem,
            device_id=sibling, device_id_type=MESH).wait_recv()
        cp.wait_send()

    return pl.pallas_call(
        body, name=name,
        out_shape=jax.ShapeDtypeStruct(full.shape, full.dtype),
        in_specs=[HBM], out_specs=HBM,
        input_output_aliases={0: 0},
        scratch_shapes=[pltpu.SemaphoreType.DMA, pltpu.SemaphoreType.DMA],
    )(full)


def _cast_into_slot(place, w, name):
    rows, cols = w.shape
    tr = min(rows, ROW_TILE)

    def body(place_ref, w_ref, o_ref):
        del place_ref
        o_ref[...] = w_ref[...].astype(BF16)

    grid_spec = pltpu.PrefetchScalarGridSpec(
        num_scalar_prefetch=1, grid=(rows // tr,),
        in_specs=[pl.BlockSpec((tr, cols), lambda i, p: (i, 0))],
        out_specs=pl.BlockSpec((None, tr, cols), lambda i, p: (p[0], i, 0)))
    return pl.pallas_call(
        body, name=name, grid_spec=grid_spec,
        out_shape=jax.ShapeDtypeStruct((N_CHIPS, rows, cols), BF16),
        compiler_params=_params(("parallel",)),
    )(place, w)


def _ada_partial(c_all, w_ada):
    d_model, wa = w_ada.shape
    tn = 512 if wa % 512 == 0 else 256

    def body(c_ref, w_ref, o_ref):
        o_ref[...] = jnp.dot(_silu(c_ref[...]), w_ref[...], precision=lax.Precision.HIGHEST,
                             preferred_element_type=F32)

    return pl.pallas_call(
        body, name="ada_partial", grid=(wa // tn,),
        out_shape=jax.ShapeDtypeStruct((N_DEV, wa), F32),
        in_specs=[pl.BlockSpec((N_DEV, d_model), lambda i: (0, 0)), pl.BlockSpec((d_model, tn), lambda i: (0, i))],
        out_specs=pl.BlockSpec((N_DEV, tn), lambda i: (0, i)),
        compiler_params=_params(("parallel",)),
    )(c_all, w_ada)


def _prenorm(x, mod, g_pre):
    t, d = x.shape
    tb = ROW_TILE

    def body(x_ref, mod_ref, g_ref, h_ref, ht_ref):
        xv = x_ref[...]
        r = lax.rsqrt(jnp.mean(xv * xv, axis=-1, keepdims=True) + EPS)
        h = (xv * r) * g_ref[...] * (1.0 + mod_ref[1:2, :]) + mod_ref[0:1, :]
        h_ref[...] = h.astype(BF16)
        ht_ref[...] = h.T.astype(BF16)

    return pl.pallas_call(
        body, name="prenorm", grid=(t // tb,),
        out_shape=(jax.ShapeDtypeStruct((t, d), BF16), jax.ShapeDtypeStruct((d, t), BF16)),
        in_specs=[pl.BlockSpec((tb, d), lambda i: (i, 0)), pl.BlockSpec((3, d), lambda i: (0, 0)),
                  pl.BlockSpec((1, d), lambda i: (0, 0))],
        out_specs=(pl.BlockSpec((tb, d), lambda i: (i, 0)), pl.BlockSpec((d, tb), lambda i: (0, i))),
        compiler_params=_params(("parallel",)),
    )(x, mod, g_pre)


def _proj_chunks(proj, h, w, chunks, name):
    t, d = h.shape
    ws = w.shape[-1]
    tn = COL_TILE
    nt = ws // tn

    def body(chunk_ref, *refs):
        del chunk_ref
        a_ref, b_ref, o_ref = refs[-3:]
        o_ref[...] = jnp.dot(a_ref[...], b_ref[...].astype(BF16), preferred_element_type=F32).astype(BF16)

    if w.ndim == 3:
        w_spec = pl.BlockSpec((None, d, tn), lambda i, n, ch: (ch[i], 0, n))
    else:
        w_spec = pl.BlockSpec((d, tn), lambda i, n, ch: (0, n))
    first = proj is None
    grid_spec = pltpu.PrefetchScalarGridSpec(
        num_scalar_prefetch=1, grid=(chunks.shape[0], nt),
        in_specs=([] if first else [HBM]) + [pl.BlockSpec((t, d), lambda i, n, ch: (0, 0)), w_spec],
        out_specs=pl.BlockSpec((t, tn), lambda i, n, ch: (0, ch[i] * nt + n)))
    return pl.pallas_call(
        body, name=name, grid_spec=grid_spec,
        out_shape=jax.ShapeDtypeStruct((t, N_CHIPS * ws), BF16),
        input_output_aliases={} if first else {1: 0},
        compiler_params=_params(("parallel", "parallel")),
    )(*([chunks] if first else [chunks, proj]), h, w)


def _shift_rows(a, rows):
    idx = lax.broadcasted_iota(jnp.int32, a.shape, 0)
    prev = jnp.where(idx == 0, 0.0, pltpu.roll(a, 1, 0))
    nxt = jnp.where(idx == rows - 1, 0.0, pltpu.roll(a, rows - 1, 0))
    return prev, nxt


def _conv_fwd(conv_proj, conv_w, conv_b, dc):
    t = conv_proj.shape[0]
    ct = CONV_TILE
    nct = dc // ct

    def body(u_ref, cg_ref, w_ref, b_ref, co_ref):
        a = cg_ref[...].astype(F32) * u_ref[...].astype(F32)
        prev, nxt = _shift_rows(a, t)
        co_ref[...] = w_ref[0:1, :] * prev + w_ref[1:2, :] * a + w_ref[2:3, :] * nxt + b_ref[...]

    return pl.pallas_call(
        body, name="conv_fwd", grid=(nct,),
        out_shape=jax.ShapeDtypeStruct((t, dc), F32),
        in_specs=[pl.BlockSpec((t, ct), lambda i: (0, i)), pl.BlockSpec((t, ct), lambda i: (0, 2 * nct + i)),
                  pl.BlockSpec((3, ct), lambda i: (0, i)), pl.BlockSpec((1, ct), lambda i: (0, i))],
        out_specs=pl.BlockSpec((t, ct), lambda i: (0, i)),
        compiler_params=_params(("parallel",)),
    )(conv_proj, conv_proj, conv_w, conv_b)


def _to_residue_major(src_ref, dst_ref, r):
    seq = src_ref.shape[0] // r
    for res in range(r):
        dst_ref[res * seq:(res + 1) * seq, :] = src_ref[pl.ds(res, seq, stride=r), :].astype(dst_ref.dtype)


def _branch_operands(token_refs, stage, dil, r):
    if r == 1:
        return list(token_refs)
    for i, ref in enumerate(token_refs):
        stage[...] = ref[...].astype(F32)
        _to_residue_major(stage, dil.at[i], r)
    return [dil.at[i] for i in range(len(token_refs))]


def _scaled_queries(q):
    return (q.astype(F32) * (HEAD_DIM ** -0.5)).astype(BF16)


BLOCK_SHIFTS = (0, -SIDE, None)


def _band_bias(rel, slope):
    arel = jnp.abs(rel)
    return jnp.where(arel <= SIDE, arel.astype(F32) * slope, NEG_INF)


def _fill_bias_tiles(bias_ref, sl_ref, r, kw):
    base = lax.broadcasted_iota(jnp.int32, (ATT_BQ, kw), 1) - lax.broadcasted_iota(jnp.int32, (ATT_BQ, kw), 0)
    for hh in range(2):
        slope = -(sl_ref[hh:hh + 1, 0:kw] * float(r))
        for e, shift in enumerate(BLOCK_SHIFTS):
            shift = ATT_BQ - kw if shift is None else shift
            bias_ref[hh, e, :, 0:kw] = _band_bias(base + shift, slope)


def _fill_stacked_bias_tiles(bias_ref, sl_ref, r, kw):
    base = lax.broadcasted_iota(jnp.int32, (kw, ATT_BQ), 0) - lax.broadcasted_iota(jnp.int32, (kw, ATT_BQ), 1)
    for hh in range(2):
        slope = -(sl_ref[hh:hh + 1, 0:ATT_BQ] * float(r))
        for e, shift in enumerate(BLOCK_SHIFTS):
            shift = ATT_BQ - kw if shift is None else shift
            bias_ref[e, 0:kw, hh * ATT_BQ:(hh + 1) * ATT_BQ] = _band_bias(base + shift, slope)


def _first_head_lanes():
    return lax.broadcasted_iota(jnp.int32, (1, PAIR), 1) < HEAD_DIM


def _only_head(x, first, hh):
    return jnp.where(first if hh == 0 else jnp.logical_not(first), x, jnp.zeros_like(x))


def _block_place(g, seq_len, kw):
    nqb = seq_len // ATT_BQ
    if nqb == 1:
        row = pl.multiple_of(g * ATT_BQ, ATT_BQ)
        return row, row, 0
    res = g // nqb
    qb = g - res * nqb
    q0 = qb * ATT_BQ
    ks = jnp.clip(q0 - SIDE, 0, seq_len - kw)
    edge = jnp.where(qb == 0, 0, jnp.where(qb == nqb - 1, 2, 1))
    return (pl.multiple_of(res * seq_len + q0, ATT_BQ), pl.multiple_of(res * seq_len + ks, SIDE), edge)


def _qkv_specs(dc, da, t, index):
    return [pl.BlockSpec((t, PAIR), functools.partial(index, (4 * dc + comp * da) // PAIR)) for comp in range(3)]


def _attn_fwd(proj, slopes, dc, da):
    t = proj.shape[0]
    hp = da // PAIR
    n_blocks = t // ATT_BQ

    def body(q_ref, k_ref, v_ref, sl_ref, o_ref, lse_ref, stage, dil, bias, o_res, l_res, o_tok, l_tok):
        for b, (_, r) in enumerate(BRANCHES):
            seq_len = t // r
            kw = min(ATT_KW, seq_len)
            ops = _branch_operands([q_ref, k_ref, v_ref], stage, dil, r)
            _fill_bias_tiles(bias, sl_ref, r, kw)
            o_dst, l_dst = (o_tok.at[b], l_tok.at[b]) if r == 1 else (o_res, l_res)
            first = _first_head_lanes()

            def blocks(trip, carry, seq_len=seq_len, kw=kw, o_dst=o_dst, l_dst=l_dst, first=first, ops=ops):
                nt = (((1,), (1,)), ((), ()))
                places = [_block_place(trip * ATT_UNROLL + i, seq_len, kw) for i in range(ATT_UNROLL)]
                chains = [(i, hh) for i in range(ATT_UNROLL) for hh in range(2)]
                qs = [_scaled_queries(ops[0][pl.ds(qrow, ATT_BQ), :]) for qrow, _, _ in places]
                ks = [ops[1][pl.ds(krow, kw), :] for _, krow, _ in places]
                vs = [ops[2][pl.ds(krow, kw), :] for _, krow, _ in places]
                ss = [lax.dot_general(_only_head(qs[i], first, hh), ks[i], nt, preferred_element_type=F32)
                      + bias[hh, places[i][2], :, 0:kw] for i, hh in chains]
                tops = [jnp.max(s, axis=-1, keepdims=True) for s in ss]
                ps = [jnp.exp(s - m) for s, m in zip(ss, tops)]
                dens = [jnp.sum(p, axis=-1, keepdims=True) for p in ps]
                for i, (qrow, _, _) in enumerate(places):
                    weights = jnp.concatenate([ps[2 * i].astype(BF16), ps[2 * i + 1].astype(BF16)], axis=1)
                    values = jnp.concatenate([_only_head(vs[i], first, 0), _only_head(vs[i], first, 1)], axis=0)
                    den = jnp.where(first, dens[2 * i], dens[2 * i + 1])
                    o_dst[pl.ds(qrow, ATT_BQ), :] = jnp.dot(weights, values, preferred_element_type=F32) / den
                    l_dst[pl.ds(qrow, ATT_BQ), :] = jnp.where(first, tops[2 * i], tops[2 * i + 1]) + jnp.log(den)
                return carry

            lax.fori_loop(0, n_blocks // ATT_UNROLL, blocks, 0)
            if r > 1:
                for res in range(r):
                    rows = slice(res * seq_len, (res + 1) * seq_len)
                    o_tok[b, pl.ds(res, seq_len, stride=r), :] = o_res[rows, :]
                    l_tok[b, pl.ds(res, seq_len, stride=r), :] = l_res[rows, :]

        def merge(i, carry):
            rows = pl.ds(pl.multiple_of(i * ROW_TILE, ROW_TILE), ROW_TILE)
            la, lb, lc = l_tok[0, rows, :], l_tok[1, rows, :], l_tok[2, rows, :]
            m = jnp.maximum(jnp.maximum(la, lb), lc)
            wa, wb, wc = jnp.exp(la - m), jnp.exp(lb - m), jnp.exp(lc - m)
            den = wa + wb + wc
            o_ref[rows, :] = (wa * o_tok[0, rows, :] + wb * o_tok[1, rows, :] + wc * o_tok[2, rows, :]) * (1.0 / den)
            lse_ref[rows, :] = m + jnp.log(den)
            return carry

        lax.fori_loop(0, t // ROW_TILE, merge, 0)

    pair_spec = pl.BlockSpec((None, t, PAIR), lambda h: (h, 0, 0))
    return pl.pallas_call(
        body, name="attn_fwd", grid=(hp,),
        out_shape=(jax.ShapeDtypeStruct((hp, t, PAIR), F32), jax.ShapeDtypeStruct((hp, t, PAIR), F32)),
        in_specs=_qkv_specs(dc, da, t, lambda first, h: (0, first + h))
        + [pl.BlockSpec((None, 8, ATT_KW), lambda h: (h, 0, 0))],
        out_specs=(pair_spec, pair_spec),
        scratch_shapes=[pltpu.VMEM((t, PAIR), F32), pltpu.VMEM((3, t, PAIR), BF16),
                        pltpu.VMEM((2, 3, ATT_BQ, ATT_KW), F32),
                        pltpu.VMEM((t, PAIR), F32), pltpu.VMEM((t, PAIR), F32),
                        pltpu.VMEM((3, t, PAIR), F32), pltpu.VMEM((3, t, PAIR), F32)],
        compiler_params=_params(("parallel",)),
    )(proj, proj, proj, slopes)


def _attn_bwd(dproj, proj, d_o, lse, delta, slopes, dc, da, after):
    t = proj.shape[0]
    hp = da // PAIR
    n_blocks = t // ATT_BQ

    def all_branches(q_ref, k_ref, v_ref, do_ref, lse_ref, dl_ref, sl_ref,
                     stage, dil, packed, packed_res, row_vecs, bias_t, acc, tot):
        first = _first_head_lanes()
        lane = lax.broadcasted_iota(jnp.int32, (1, PAIR), 1)
        packed[...] = jnp.where((lane & (HEAD_DIM - 1)) < HEAD_DIM // 2, lse_ref[...], dl_ref[...])
        for b, (_, r) in enumerate(BRANCHES):
            seq_len = t // r
            kw = min(ATT_KW, seq_len)
            ops = _branch_operands([q_ref, k_ref, v_ref, do_ref], stage, dil, r)
            scalars = packed
            if r > 1:
                _to_residue_major(packed, packed_res, r)
                scalars = packed_res
            for g in range(n_blocks):
                flipped = scalars[g * ATT_BQ:(g + 1) * ATT_BQ, :].T
                for row in range(4):
                    row_vecs[g, row:row + 1, :] = flipped[row * (HEAD_DIM // 2):row * (HEAD_DIM // 2) + 1, :]
            _fill_stacked_bias_tiles(bias_t, sl_ref, r, kw)
            acc[1] = jnp.zeros((t, PAIR), F32)
            acc[2] = jnp.zeros((t, PAIR), F32)

            def blocks(trip, carry, seq_len=seq_len, kw=kw, ops=ops):
                nt = (((1,), (1,)), ((), ()))
                group = range(ATT_UNROLL)
                places = [_block_place(trip * ATT_UNROLL + i, seq_len, kw) for i in group]
                ks, vs, q2s, do2s, lse2s, dl2s = [], [], [], [], [], []
                for i, (qrow, krow, _) in zip(group, places):
                    q = _scaled_queries(ops[0][pl.ds(qrow, ATT_BQ), :])
                    dov = ops[3][pl.ds(qrow, ATT_BQ), :]
                    ks.append(ops[1][pl.ds(krow, kw), :])
                    vs.append(ops[2][pl.ds(krow, kw), :])
                    q2s.append(jnp.concatenate([_only_head(q, first, 0), _only_head(q, first, 1)], axis=0))
                    do2s.append(jnp.concatenate([_only_head(dov, first, 0), _only_head(dov, first, 1)], axis=0))
                    rows = row_vecs[trip * ATT_UNROLL + i]
                    lse2s.append(jnp.concatenate([rows[0:1, :], rows[2:3, :]], axis=1))
                    dl2s.append(jnp.concatenate([rows[1:2, :], rows[3:4, :]], axis=1))
                s_ts = [lax.dot_general(ks[i], q2s[i], nt, preferred_element_type=F32) for i in group]
                dp_ts = [lax.dot_general(vs[i], do2s[i], nt, preferred_element_type=F32) for i in group]
                p_ts = [jnp.exp(s_ts[i] + bias_t[places[i][2], 0:kw, :] - lse2s[i]) for i in group]
                ds_ts = [p_ts[i] * (dp_ts[i] - dl2s[i]) for i in group]
                dvs = [jnp.dot(p_ts[i].astype(BF16), do2s[i], preferred_element_type=F32) for i in group]
                dks = [jnp.dot(ds_ts[i].astype(BF16), q2s[i], preferred_element_type=F32) for i in group]
                dss = [ds_ts[i].T.astype(BF16) for i in group]
                dqs = [jnp.dot(dss[i][0:ATT_BQ, :], _only_head(ks[i], first, 0), preferred_element_type=F32)
                       + jnp.dot(dss[i][ATT_BQ:2 * ATT_BQ, :], _only_head(ks[i], first, 1), preferred_element_type=F32)
                       for i in group]
                for i, (qrow, krow, _) in zip(group, places):
                    acc[0, pl.ds(qrow, ATT_BQ), :] = dqs[i] * (HEAD_DIM ** -0.5)
                    acc[1, pl.ds(krow, kw), :] += dks[i]
                    acc[2, pl.ds(krow, kw), :] += dvs[i]
                return carry

            lax.fori_loop(0, n_blocks // ATT_UNROLL, blocks, 0)
            for comp in range(3):
                if r == 1:
                    tot[comp] = acc[comp]
                else:
                    for res in range(r):
                        tok = pl.ds(res, seq_len, stride=r)
                        tot[comp, tok, :] = tot[comp, tok, :] + acc[comp, res * seq_len:(res + 1) * seq_len, :]

    first_q = (4 * dc) // PAIR

    def body(dproj_in, q_ref, k_ref, v_ref, do_ref, lse_ref, dl_ref, sl_ref, after_ref, out_ref, *scratch):
        del dproj_in, after_ref
        work, out_stage, out_sems = scratch[:-2], scratch[-2], scratch[-1]
        h = pl.program_id(0)
        all_branches(q_ref, k_ref, v_ref, do_ref, lse_ref, dl_ref, sl_ref, *work)

        def out_copy(comp):
            cols = pl.ds(pl.multiple_of((first_q + comp * hp + h) * PAIR, PAIR), PAIR)
            return pltpu.make_async_copy(out_stage.at[comp], out_ref.at[:, cols], out_sems.at[comp])

        @pl.when(h > 0)
        def _():
            for comp in range(3):
                out_copy(comp).wait()

        for comp in range(3):
            out_stage[comp] = work[-1][comp].astype(BF16)
            out_copy(comp).start()

        @pl.when(h == hp - 1)
        def _():
            for comp in range(3):
                out_copy(comp).wait()

    pair_spec = pl.BlockSpec((None, t, PAIR), lambda h: (h, 0, 0))
    return pl.pallas_call(
        body, name="attn_bwd", grid=(hp,),
        out_shape=jax.ShapeDtypeStruct(dproj.shape, BF16),
        in_specs=[HBM] + _qkv_specs(dc, da, t, lambda first, h: (0, first + h))
        + [pair_spec, pair_spec, pair_spec, pl.BlockSpec((None, 8, ATT_KW), lambda h: (h, 0, 0)), ANY],
        out_specs=ANY,
        input_output_aliases={0: 0},
        scratch_shapes=[pltpu.VMEM((t, PAIR), F32), pltpu.VMEM((4, t, PAIR), BF16),
                        pltpu.VMEM((t, PAIR), F32), pltpu.VMEM((t, PAIR), F32),
                        pltpu.VMEM((n_blocks, 8, ATT_BQ), F32), pltpu.VMEM((3, ATT_KW, 2 * ATT_BQ), F32),
                        pltpu.VMEM((3, t, PAIR), F32), pltpu.VMEM((3, t, PAIR), F32),
                        pltpu.VMEM((3, t, PAIR), BF16), pltpu.SemaphoreType.DMA((3,))],
        compiler_params=_params(("arbitrary",)),
    )(dproj, proj, proj, proj, d_o, lse, delta, slopes, after)


def _mix_fwd(co, proj, o_mix, g_conv, g_attn_pairs):
    t, dc = co.shape
    hp = o_mix.shape[0]
    da = hp * PAIR
    tb = ROW_TILE

    def body(co_ref, bg_ref, zc_ref, za_ref, om_ref, gc_ref, ga_ref, ycat_ref, ycatt_ref):
        p = bg_ref[...].astype(F32) * co_ref[...]
        rc = lax.rsqrt(jnp.mean(p * p, axis=-1, keepdims=True) + EPS)
        yc = (p * rc) * gc_ref[...] * _silu(zc_ref[...].astype(F32))
        ycat_ref[:, 0:dc] = yc.astype(BF16)
        ycatt_ref[0:dc, :] = yc.T.astype(BF16)
        ssq = jnp.zeros((tb, 1), F32)
        for h in range(hp):
            o = om_ref[h]
            ssq = ssq + jnp.sum(o * o, axis=-1, keepdims=True)
        ra = lax.rsqrt(ssq * (1.0 / da) + EPS)
        for h in range(hp):
            ya = (om_ref[h] * ra) * ga_ref[h] * _silu(za_ref[:, h * PAIR:(h + 1) * PAIR].astype(F32))
            ycat_ref[:, dc + h * PAIR:dc + (h + 1) * PAIR] = ya.astype(BF16)
            ycatt_ref[dc + h * PAIR:dc + (h + 1) * PAIR, :] = ya.T.astype(BF16)

    pair_spec = pl.BlockSpec((hp, tb, PAIR), lambda i: (0, i, 0))
    return pl.pallas_call(
        body, name="mix_fwd", grid=(t // tb,),
        out_shape=(jax.ShapeDtypeStruct((t, dc + da), BF16), jax.ShapeDtypeStruct((dc + da, t), BF16)),
        in_specs=[pl.BlockSpec((tb, dc), lambda i: (i, 0)),
                  pl.BlockSpec((tb, dc), lambda i: (i, 1)),
                  pl.BlockSpec((tb, dc), lambda i: (i, 3)),
                  pl.BlockSpec((tb, da), lambda i: (i, 7)),
                  pair_spec,
                  pl.BlockSpec((1, dc), lambda i: (0, 0)),
                  pl.BlockSpec((hp, 1, PAIR), lambda i: (0, 0, 0))],
        out_specs=(pl.BlockSpec((tb, dc + da), lambda i: (i, 0)), pl.BlockSpec((dc + da, tb), lambda i: (0, i))),
        compiler_params=_params(("parallel",)),
    )(co, proj, proj, proj, o_mix, g_conv, g_attn_pairs)


def _out_fwd_bwd(ycat, woutf, x, target, mod, g_post):
    t, d = x.shape
    n = ycat.shape[1]
    tb = ROW_TILE

    def body(a_ref, w_ref, x_ref, tg_ref, mod_ref, g_ref, dout_ref, dy_ref, acc_ref):
        y = jnp.dot(a_ref[...], w_ref[...], preferred_element_type=F32)
        r = lax.rsqrt(jnp.mean(y * y, axis=-1, keepdims=True) + EPS)
        nh = y * r
        gate = mod_ref[2:3, :]
        nrm = nh * g_ref[...]
        err = x_ref[...] + gate * nrm - tg_ref[...]
        dout = err * (1.0 / d)
        dout_ref[...] = dout
        dn = dout * gate
        a = dn * g_ref[...]
        dy = r * (a - nh * jnp.mean(a * nh, axis=-1, keepdims=True))
        dy_ref[...] = dy.astype(BF16)
        loss = 0.5 * jnp.sum(jnp.sum(err * err, axis=-1, keepdims=True) * (1.0 / d), axis=0, keepdims=True)
        part = jnp.concatenate(
            [jnp.sum(dout * nrm, axis=0, keepdims=True), jnp.sum(dn * nh, axis=0, keepdims=True),
             jnp.broadcast_to(loss, (1, d)), jnp.zeros((5, d), F32)], axis=0)

        @pl.when(pl.program_id(0) == 0)
        def _():
            acc_ref[...] = jnp.zeros(acc_ref.shape, F32)

        acc_ref[...] += part

    return pl.pallas_call(
        body, name="out_fwd_bwd", grid=(t // tb,),
        out_shape=(jax.ShapeDtypeStruct((t, d), F32), jax.ShapeDtypeStruct((t, d), BF16),
                   jax.ShapeDtypeStruct((8, d), F32)),
        in_specs=[pl.BlockSpec((tb, n), lambda i: (i, 0)), pl.BlockSpec((n, d), lambda i: (0, 0)),
                  pl.BlockSpec((tb, d), lambda i: (i, 0)), pl.BlockSpec((tb, d), lambda i: (i, 0)),
                  pl.BlockSpec((3, d), lambda i: (0, 0)), pl.BlockSpec((1, d), lambda i: (0, 0))],
        out_specs=(pl.BlockSpec((tb, d), lambda i: (i, 0)), pl.BlockSpec((tb, d), lambda i: (i, 0)),
                   pl.BlockSpec((8, d), lambda i: (0, 0))),
        compiler_params=_params(("arbitrary",)),
    )(ycat, woutf, x, target, mod, g_post)


def _matmul_nt(a, b, out_dtype, name):
    m, k = a.shape
    n = b.shape[0]
    tn = COL_TILE

    def body(a_ref, b_ref, o_ref):
        o_ref[...] = lax.dot_general(a_ref[...], b_ref[...], (((1,), (1,)), ((), ())),
                                     preferred_element_type=F32).astype(out_dtype)

    return pl.pallas_call(
        body, name=name, grid=(n // tn,),
        out_shape=jax.ShapeDtypeStruct((m, n), out_dtype),
        in_specs=[pl.BlockSpec((m, k), lambda i: (0, 0)), pl.BlockSpec((tn, k), lambda i: (i, 0))],
        out_specs=pl.BlockSpec((m, tn), lambda i: (0, i)),
        compiler_params=_params(("parallel",)),
    )(a, b)


def _mix_bwd(dycat, co, proj, o_mix, g_conv, g_attn_pairs):
    t, dc = co.shape
    hp = o_mix.shape[0]
    da = hp * PAIR
    tb = ROW_TILE

    def body(dy_ref, co_ref, bg_ref, zc_ref, za_ref, om_ref, gc_ref, ga_ref,
             dcp_ref, dco_ref, do_ref, dl_ref, dgc_ref, dga_ref):
        first = pl.program_id(0) == 0
        cov = co_ref[...]
        bg = bg_ref[...].astype(F32)
        zc = zc_ref[...].astype(F32)
        p = bg * cov
        rc = lax.rsqrt(jnp.mean(p * p, axis=-1, keepdims=True) + EPS)
        nh = p * rc
        dyc = dy_ref[:, 0:dc].astype(F32)
        dn = dyc * _silu(zc)
        a = dn * gc_ref[...]
        dp = rc * (a - nh * jnp.mean(a * nh, axis=-1, keepdims=True))
        dcp_ref[:, 0:dc] = jnp.zeros((tb, dc), BF16)
        dcp_ref[:, dc:2 * dc] = (dp * cov).astype(BF16)
        dcp_ref[:, 2 * dc:3 * dc] = jnp.zeros((tb, dc), BF16)
        dcp_ref[:, 3 * dc:4 * dc] = (dyc * nh * gc_ref[...] * _silu_grad(zc)).astype(BF16)
        dcp_ref[:, 4 * dc:4 * dc + 3 * da] = jnp.zeros((tb, 3 * da), BF16)
        dco_ref[...] = dp * bg

        @pl.when(first)
        def _():
            dgc_ref[...] = jnp.zeros(dgc_ref.shape, F32)
            dga_ref[...] = jnp.zeros(dga_ref.shape, F32)

        dgc_ref[...] += jnp.sum(dn * nh, axis=0, keepdims=True)

        ssq = jnp.zeros((tb, 1), F32)
        for h in range(hp):
            o = om_ref[h]
            ssq = ssq + jnp.sum(o * o, axis=-1, keepdims=True)
        ra = lax.rsqrt(ssq * (1.0 / da) + EPS)
        dot_an = jnp.zeros((tb, 1), F32)
        for h in range(hp):
            nha = om_ref[h] * ra
            za = za_ref[:, h * PAIR:(h + 1) * PAIR].astype(F32)
            dya = dy_ref[:, dc + h * PAIR:dc + (h + 1) * PAIR].astype(F32)
            dna = dya * _silu(za)
            dza = (dya * nha * ga_ref[h] * _silu_grad(za)).astype(BF16)
            dcp_ref[:, 4 * dc + 3 * da + h * PAIR:4 * dc + 3 * da + (h + 1) * PAIR] = dza
            dga_ref[h] += jnp.sum(dna * nha, axis=0, keepdims=True)
            dot_an = dot_an + jnp.sum(dna * ga_ref[h] * nha, axis=-1, keepdims=True)
        mean_an = dot_an * (1.0 / da)
        first_head = lax.broadcasted_iota(jnp.int32, (tb, PAIR), 1) < HEAD_DIM
        for h in range(hp):
            o = om_ref[h]
            nha = o * ra
            za = za_ref[:, h * PAIR:(h + 1) * PAIR].astype(F32)
            dya = dy_ref[:, dc + h * PAIR:dc + (h + 1) * PAIR].astype(F32)
            aa = dya * _silu(za) * ga_ref[h]
            d_o = ra * (aa - nha * mean_an)
            do_ref[h] = d_o.astype(BF16)
            prod = d_o * o
            both = jnp.sum(prod, axis=-1, keepdims=True)
            head0 = jnp.sum(jnp.where(first_head, prod, 0.0), axis=-1, keepdims=True)
            dl_ref[h] = jnp.where(first_head, head0, both - head0)

    pair_spec = pl.BlockSpec((hp, tb, PAIR), lambda i: (0, i, 0))
    return pl.pallas_call(
        body, name="mix_bwd", grid=(t // tb,),
        out_shape=(jax.ShapeDtypeStruct((t, 4 * dc + 4 * da), BF16), jax.ShapeDtypeStruct((t, dc), F32),
                   jax.ShapeDtypeStruct((hp, t, PAIR), BF16), jax.ShapeDtypeStruct((hp, t, PAIR), F32),
                   jax.ShapeDtypeStruct((1, dc), F32), jax.ShapeDtypeStruct((hp, 1, PAIR), F32)),
        in_specs=[pl.BlockSpec((tb, dc + da), lambda i: (i, 0)),
                  pl.BlockSpec((tb, dc), lambda i: (i, 0)),
                  pl.BlockSpec((tb, dc), lambda i: (i, 1)),
                  pl.BlockSpec((tb, dc), lambda i: (i, 3)),
                  pl.BlockSpec((tb, da), lambda i: (i, 7)),
                  pair_spec,
                  pl.BlockSpec((1, dc), lambda i: (0, 0)),
                  pl.BlockSpec((hp, 1, PAIR), lambda i: (0, 0, 0))],
        out_specs=(pl.BlockSpec((tb, 4 * dc + 4 * da), lambda i: (i, 0)), pl.BlockSpec((tb, dc), lambda i: (i, 0)),
                   pair_spec, pair_spec,
                   pl.BlockSpec((1, dc), lambda i: (0, 0)), pl.BlockSpec((hp, 1, PAIR), lambda i: (0, 0, 0))),
        compiler_params=_params(("arbitrary",)),
    )(dycat, co, proj, proj, proj, o_mix, g_conv, g_attn_pairs)


def _conv_bwd(dconv_proj, dco, conv_proj, conv_w, dc, after):
    t = dco.shape[0]
    ct = CONV_TILE
    nct = dc // ct

    def body(dcp_in_ref, dco_ref, u_ref, cg_ref, w_ref, after_ref, dcp_ref, acc_ref):
        del dcp_in_ref, after_ref
        which = pl.program_id(1)
        g = dco_ref[...]
        u = u_ref[...].astype(F32)
        cg = cg_ref[...].astype(F32)
        g_prev, g_next = _shift_rows(g, t)
        da = w_ref[0:1, :] * g_next + w_ref[1:2, :] * g + w_ref[2:3, :] * g_prev
        dcp_ref[...] = (da * jnp.where(which == 0, cg, u)).astype(BF16)
        a = cg * u
        a_prev, a_next = _shift_rows(a, t)
        acc_ref[...] = jnp.concatenate(
            [jnp.sum(g * a_prev, axis=0, keepdims=True), jnp.sum(g * a, axis=0, keepdims=True),
             jnp.sum(g * a_next, axis=0, keepdims=True), jnp.sum(g, axis=0, keepdims=True),
             jnp.zeros((4, ct), F32)], axis=0)

    return pl.pallas_call(
        body, name="conv_bwd", grid=(nct, 2),
        out_shape=(jax.ShapeDtypeStruct(dconv_proj.shape, BF16), jax.ShapeDtypeStruct((8, dc), F32)),
        in_specs=[HBM,
                  pl.BlockSpec((t, ct), lambda i, s: (0, i)),
                  pl.BlockSpec((t, ct), lambda i, s: (0, i)),
                  pl.BlockSpec((t, ct), lambda i, s: (0, 2 * nct + i)),
                  pl.BlockSpec((3, ct), lambda i, s: (0, i)), ANY],
        out_specs=(pl.BlockSpec((t, ct), lambda i, s: (0, 2 * s * nct + i)),
                   pl.BlockSpec((8, ct), lambda i, s: (0, i))),
        input_output_aliases={0: 0},
        compiler_params=_params(("arbitrary", "arbitrary")),
    )(dconv_proj, dco, conv_proj, conv_proj, conv_w, after)


def _dh(dproj, winf, after):
    t = dproj.shape[0]
    _, d, ws = winf.shape
    tm = tn = COL_TILE
    nt = (((1,), (1,)), ((), ()))

    def body(a_ref, w_ref, after_ref, o_ref):
        del after_ref
        acc = lax.dot_general(a_ref[:, 0:ws], w_ref[0], nt, preferred_element_type=F32)
        for j in range(1, N_CHIPS):
            acc = acc + lax.dot_general(a_ref[:, j * ws:(j + 1) * ws], w_ref[j], nt, preferred_element_type=F32)
        o_ref[...] = acc

    return pl.pallas_call(
        body, name="dh", grid=(d // tn, t // tm),
        out_shape=jax.ShapeDtypeStruct((t, d), F32),
        in_specs=[pl.BlockSpec((tm, N_CHIPS * ws), lambda n, m: (m, 0)),
                  pl.BlockSpec((N_CHIPS, tn, ws), lambda n, m: (0, n, 0)), ANY],
        out_specs=pl.BlockSpec((tm, tn), lambda n, m: (m, n)),
        compiler_params=_params(("parallel", "parallel")),
    )(dproj, winf, after)


def _prenorm_bwd(x, dh, dout, mod, g_pre):
    t, d = x.shape
    tb = ROW_TILE

    def body(x_ref, dh_ref, dout_ref, mod_ref, g_ref, gx_ref, acc_ref):
        xv = x_ref[...]
        dhv = dh_ref[...]
        r = lax.rsqrt(jnp.mean(xv * xv, axis=-1, keepdims=True) + EPS)
        xh = xv * r
        one_scale = 1.0 + mod_ref[1:2, :]
        a = dhv * one_scale * g_ref[...]
        gx_ref[...] = dout_ref[...] + r * (a - xh * jnp.mean(a * xh, axis=-1, keepdims=True))
        part = jnp.concatenate(
            [jnp.sum(dhv, axis=0, keepdims=True), jnp.sum(dhv * xh * g_ref[...], axis=0, keepdims=True),
             jnp.sum(dhv * xh * one_scale, axis=0, keepdims=True), jnp.zeros((5, d), F32)], axis=0)

        @pl.when(pl.program_id(0) == 0)
        def _():
            acc_ref[...] = jnp.zeros(acc_ref.shape, F32)

        acc_ref[...] += part

    return pl.pallas_call(
        body, name="prenorm_bwd", grid=(t // tb,),
        out_shape=(jax.ShapeDtypeStruct((t, d), F32), jax.ShapeDtypeStruct((8, d), F32)),
        in_specs=[pl.BlockSpec((tb, d), lambda i: (i, 0)), pl.BlockSpec((tb, d), lambda i: (i, 0)),
                  pl.BlockSpec((tb, d), lambda i: (i, 0)), pl.BlockSpec((3, d), lambda i: (0, 0)),
                  pl.BlockSpec((1, d), lambda i: (0, 0))],
        out_specs=(pl.BlockSpec((tb, d), lambda i: (i, 0)), pl.BlockSpec((8, d), lambda i: (0, 0))),
        compiler_params=_params(("arbitrary",)),
    )(x, dh, dout, mod, g_pre)


def _chip_sums(mine, rsib, name):
    _, half, cols = mine.shape
    tr = min(half, ROW_TILE)

    def body(g_ref, r_ref, o_ref):
        o_ref[...] = (g_ref[...].astype(F32) + r_ref[...].astype(F32)).astype(BF16)

    spec = pl.BlockSpec((None, tr, cols), lambda j, i: (j, i, 0))
    return pl.pallas_call(
        body, name=name, grid=(N_CHIPS, half // tr),
        out_shape=jax.ShapeDtypeStruct(mine.shape, BF16),
        in_specs=[spec, spec], out_specs=spec,
        compiler_params=_params(("parallel", "parallel")),
    )(mine, rsib)


def _owner_sum(place, mine, rsib, rici, name):
    _, half, cols = mine.shape
    rows = 2 * half
    tr = min(half, ROW_TILE)
    nt = half // tr

    def body(place_ref, g_ref, r_ref, i_ref, o_ref):
        del place_ref
        acc = g_ref[...].astype(F32) + r_ref[...].astype(F32)
        for k in range(N_CHIPS - 1):
            acc = acc + i_ref[k].astype(F32)
        o_ref[...] = acc

    grid_spec = pltpu.PrefetchScalarGridSpec(
        num_scalar_prefetch=1, grid=(nt,),
        in_specs=[pl.BlockSpec((None, tr, cols), lambda i, p: (p[0], i, 0)),
                  pl.BlockSpec((None, tr, cols), lambda i, p: (p[0], i, 0)),
                  pl.BlockSpec((N_CHIPS - 1, tr, cols), lambda i, p: (0, i, 0))],
        out_specs=pl.BlockSpec((tr, cols), lambda i, p: (p[1] * nt + i, 0)))
    return pl.pallas_call(
        body, name=name, grid_spec=grid_spec,
        out_shape=jax.ShapeDtypeStruct((rows, cols), F32),
        compiler_params=_params(("parallel",)),
    )(place, mine, rsib, rici)


def _adam_math(w, g, m, v):
    m2 = ADAM_B1 * m + (1.0 - ADAM_B1) * g
    v2 = ADAM_B2 * v + (1.0 - ADAM_B2) * (g * g)
    m_hat = m2 / (1.0 - ADAM_B1 ** ADAM_STEP)
    v_hat = v2 / (1.0 - ADAM_B2 ** ADAM_STEP)
    delta = -ADAM_LR * (m_hat / (jnp.sqrt(v_hat) + ADAM_EPS) + ADAM_WD * w)
    return delta, m2, v2


def _adamw(w, g, m, v, name):
    rows, cols = w.shape
    tr = min(rows, ROW_TILE)

    def body(w_ref, g_ref, m_ref, v_ref, go_ref, d_ref, m2_ref, v2_ref):
        g = g_ref[...]
        go_ref[...] = g
        d_ref[...], m2_ref[...], v2_ref[...] = _adam_math(w_ref[...], g, m_ref[...], v_ref[...])

    spec = pl.BlockSpec((tr, cols), lambda i: (i, 0))
    return pl.pallas_call(
        body, name=name, grid=(rows // tr,),
        out_shape=(jax.ShapeDtypeStruct(w.shape, F32),) * 4,
        in_specs=[spec] * 4, out_specs=(spec,) * 4,
        compiler_params=_params(("parallel",)),
    )(w, g, m, v)


def _ada_grad_adamw(c_all_t, dmod_cols, w, m, v):
    d, wa = w.shape
    tr = ROW_TILE

    def body(ct_ref, dm_ref, w_ref, m_ref, v_ref, g_ref, d_ref, m2_ref, v2_ref):
        act = _silu(ct_ref[...])
        g = act[:, 0:1] * dm_ref[0:1, :]
        for b in range(1, N_DEV):
            g = g + act[:, b:b + 1] * dm_ref[b:b + 1, :]
        g_ref[...] = g
        d_ref[...], m2_ref[...], v2_ref[...] = _adam_math(w_ref[...], g, m_ref[...], v_ref[...])

    spec = pl.BlockSpec((tr, wa), lambda i: (i, 0))
    return pl.pallas_call(
        body, name="ada_grad_adamw", grid=(d // tr,),
        out_shape=(jax.ShapeDtypeStruct(w.shape, F32),) * 4,
        in_specs=[pl.BlockSpec((tr, N_DEV), lambda i: (i, 0)), pl.BlockSpec((N_DEV, wa), lambda i: (0, 0)),
                  spec, spec, spec],
        out_specs=(spec,) * 4,
        compiler_params=_params(("parallel",)),
    )(c_all_t, dmod_cols, w, m, v)


def _sum_devices(gathered):
    n = gathered.shape[1]

    def body(g_ref, o_ref):
        acc = g_ref[0:8, :]
        for dev in range(1, N_DEV):
            acc = acc + g_ref[8 * dev:8 * dev + 8, :]
        o_ref[...] = acc

    return pl.pallas_call(
        body, name="sum_devices",
        out_shape=jax.ShapeDtypeStruct((8, n), F32),
        in_specs=[VMEM], out_specs=VMEM,
    )(gathered)


def _pack_small(pieces):
    flat = [p.reshape(-1).astype(F32) for p in pieces]
    offsets, total = [], 0
    for p in flat:
        offsets.append(total)
        total += p.shape[0]
    padded = -(-total // SMALL_ALIGN) * SMALL_ALIGN
    if padded > total:
        flat.append(jnp.zeros((padded - total,), F32))
    return jnp.concatenate(flat).reshape(8, padded // 8), offsets


def _alibi_slope_rows(n_heads):
    slopes = 2.0 ** (-8.0 * jnp.arange(1, n_heads + 1, dtype=F32) / n_heads)
    rows = jnp.zeros((n_heads // 2, 8), F32).at[:, 0:2].set(slopes.reshape(n_heads // 2, 2))
    return jnp.broadcast_to(rows[:, :, None], (n_heads // 2, 8, ATT_KW))


def kernel(x, c, w_ada, b_ada, g_pre, w_in, conv_w, conv_b, g_conv, g_attn, w_out, g_post, loss_target, m_w_ada, m_b_ada, m_g_pre, m_w_in, m_conv_w, m_conv_b, m_g_conv, m_g_attn, m_w_out, m_g_post, v_w_ada, v_b_ada, v_g_pre, v_w_in, v_conv_w, v_conv_b, v_g_conv, v_g_attn, v_w_out, v_g_post):
    t, d = x.shape[1], x.shape[2]
    dc = conv_b.shape[1]
    da = g_attn.shape[1]
    hp = da // PAIR
    ws = w_in.shape[2]
    wa = w_ada.shape[2]
    cws = conv_w.shape[2]
    assert t % ROW_TILE == 0 and d % ROW_TILE == 0 and dc % COL_TILE == 0 and da % COL_TILE == 0
    assert ws == 2 * dc and dc == da and t // BRANCHES[-1][1] >= ATT_BQ

    mx, my, mc = _my_place()
    chip = _chip_of(mx, my)
    dev = 2 * chip + mc
    place = jnp.stack([chip, mc]).astype(jnp.int32)

    x2, tgt2 = x[0], loss_target[0]
    w_ada2, w_in2, w_out2 = w_ada[0], w_in[0], w_out[0]

    win_slots = _cast_into_slot(place, w_in2, "cast_w_in")
    packed, offs = _pack_small([c[0], conv_w[0]])
    seen = _allgather8(packed, "gather_inputs", after=(win_slots,)).reshape(N_DEV, -1)
    c_all = seen[:, offs[0]:offs[0] + d]
    conv_w_full = seen[0::2, offs[1]:offs[1] + 3 * cws].reshape(N_CHIPS, 3, cws).transpose(1, 0, 2).reshape(3, dc)

    ada_part = _ada_partial(c_all, w_ada2)
    ada_seen = _allgather8(ada_part, "gather_ada").reshape(N_DEV, N_DEV, wa)
    mod_flat = lax.dynamic_index_in_dim(ada_seen[0::2], dev, axis=1, keepdims=False).reshape(1, 3 * d) + b_ada
    mod = mod_flat.reshape(3, d)

    win_flight, send_in, recv_in, started = _gather_start(win_slots, mod)

    y_chip, x_chip, d_chip = (_chip_of(mx, 1 - my), _chip_of(1 - mx, my), _chip_of(1 - mx, 1 - my))
    own_chunk, near_chunks, far_chunk = (jnp.stack(js).astype(jnp.int32) for js in ([chip], [y_chip, x_chip], [d_chip]))
    h, ht = _prenorm(x2, mod + started[0, 0], g_pre)
    proj = _proj_chunks(None, h, w_in2, own_chunk, "proj_own")
    win_flight, wout_flight, relay_send_in, relay_recv_in, send_out, recv_out = _gather_relay_in(
        win_flight, _cast_into_slot(place, w_out2, "cast_w_out"), recv_in, proj)
    win_flight = _forward_halves(
        _gather_wait_direct(win_flight, send_in, recv_in, proj, "gather_wait_w_in_direct"), (0, 1), "forward_w_in_direct")
    proj = _proj_chunks(proj, h, win_flight, near_chunks, "proj_neighbours")
    winf = _forward_halves(
        _gather_wait_relayed(win_flight, relay_send_in, relay_recv_in, proj, "gather_wait_w_in_relayed"),
        (2,), "forward_w_in_relayed")
    proj = _proj_chunks(proj, h, winf, far_chunk, "proj_diagonal")
    slopes = _alibi_slope_rows(da // HEAD_DIM)
    co = _conv_fwd(proj, conv_w_full, conv_b, dc)
    wout_flight, relay_send_out, relay_recv_out = _gather_relay_out(wout_flight, recv_out, co)
    o_mix, lse = _attn_fwd(proj, slopes, dc, da)
    g_attn_pairs = g_attn.reshape(hp, 1, PAIR)
    ycat, ycat_t = _mix_fwd(co, proj, o_mix, g_conv, g_attn_pairs)
    wout_flight = _gather_wait_direct(wout_flight, send_out, recv_out, ycat, "gather_wait_w_out_direct")
    wout_flight = _gather_wait_relayed(wout_flight, relay_send_out, relay_recv_out, ycat, "gather_wait_w_out_relayed")
    woutf = _forward_halves(wout_flight, (0, 1, 2), "forward_w_out").reshape(dc + da, d)
    dout, dy, post_sums = _out_fwd_bwd(ycat, woutf, x2, tgt2, mod, g_post)

    gout, rsib_out = _dw_swapped(ycat_t, dy, N_CHIPS, 1, "dw_out")
    csum_out = _chip_sums(gout, rsib_out, "rs_chip_sum_out")
    ssem_out, rsem_out, csum_out, land_out, sent_out = _owners_start(csum_out, "rs_owners_start_out")
    dycat = _matmul_nt(dy, woutf, BF16, "dycat")
    dproj, dco, d_o, delta, dg_conv, dg_attn = _mix_bwd(dycat, co, proj, o_mix, g_conv, g_attn_pairs)
    dproj, conv_sums = _conv_bwd(dproj, dco, proj, conv_w_full, dc, sent_out)
    dproj = _attn_bwd(dproj, proj, d_o, lse, delta, slopes, dc, da, sent_out)
    gin, rsib_in = _dw_swapped(ht, dproj, 1, N_CHIPS, "dw_in")
    csum_in = _chip_sums(gin, rsib_in, "rs_chip_sum_in")
    ssem_in, rsem_in, csum_in, land_in, sent_in = _owners_start(csum_in, "rs_owners_start_in")
    dh = _dh(dproj, winf, sent_in)
    grad_x, pre_sums = _prenorm_bwd(x2, dh, dout, mod, g_pre)

    small, so = _pack_small([
        pre_sums[0], pre_sums[1], post_sums[0],
        pre_sums[2], conv_sums[0:3], conv_sums[3], dg_conv, dg_attn, post_sums[1], post_sums[2, 0:128]])
    ssem_small, rsem_small, small, land_small, sent_small = _allgather8_start(small, dev, "gather_small_start")

    rici_out = _owners_wait(ssem_out, rsem_out, csum_out, land_out, [grad_x, sent_small], "rs_owners_wait_out")
    grad_w_out = _join_halves(_owner_sum(place, gout, rsib_out, rici_out, "rs_owner_sum_out"), "rs_join_halves_out")
    grad_w_out, delta_w_out, new_m_w_out, new_v_w_out = _adamw(
        w_out2, grad_w_out, m_w_out[0], v_w_out[0], "adamw_w_out")

    small_seen = _allgather8_wait(ssem_small, rsem_small, small, land_small, [delta_w_out], "gather_small_wait")
    total = _sum_devices(small_seen).reshape(-1)
    dmod_all = small_seen.reshape(N_DEV, -1)[:, 0:3 * d]
    loss = total[so[9]]
    grad_b_ada = total[0:3 * d].reshape(1, 3 * d)
    grad_g_pre = total[so[3]:so[3] + d].reshape(1, d)
    grad_conv_w_full = total[so[4]:so[4] + 3 * dc].reshape(3, dc)
    grad_conv_w = lax.dynamic_slice_in_dim(grad_conv_w_full, chip * cws, cws, axis=1).reshape(1, 3, cws)
    grad_conv_b = total[so[5]:so[5] + dc].reshape(1, dc)
    grad_g_conv = total[so[6]:so[6] + dc].reshape(1, dc)
    grad_g_attn = total[so[7]:so[7] + da].reshape(1, da)
    grad_g_post = total[so[8]:so[8] + d].reshape(1, d)

    dmod_cols = lax.dynamic_slice_in_dim(dmod_all, chip * wa, wa, axis=1)
    grad_w_ada, delta_w_ada, new_m_w_ada, new_v_w_ada = _ada_grad_adamw(c_all.T, dmod_cols, w_ada2, m_w_ada[0], v_w_ada[0])

    small_w = [b_ada, g_pre, conv_w, conv_b, g_conv, g_attn, g_post]
    small_g = [grad_b_ada, grad_g_pre, grad_conv_w, grad_conv_b, grad_g_conv, grad_g_attn, grad_g_post]
    small_m = [m_b_ada, m_g_pre, m_conv_w, m_conv_b, m_g_conv, m_g_attn, m_g_post]
    small_v = [v_b_ada, v_g_pre, v_conv_w, v_conv_b, v_g_conv, v_g_attn, v_g_post]
    pw, po = _pack_small(small_w)
    pg, _ = _pack_small(small_g)
    pm, _ = _pack_small(small_m)
    pv, _ = _pack_small(small_v)
    sd, sm, sv = (a.reshape(-1) for a in _adamw(pw, pg, pm, pv, "adamw_small")[1:])

    def unpack(flat):
        return [flat[o:o + w.size].reshape(w.shape) for o, w in zip(po, small_w)]

    d_small, m_small, v_small = unpack(sd), unpack(sm), unpack(sv)

    rici_in = _owners_wait(ssem_in, rsem_in, csum_in, land_in, [sd, delta_w_out, delta_w_ada], "rs_owners_wait_in")
    grad_w_in = _join_halves(_owner_sum(place, gin, rsib_in, rici_in, "rs_owner_sum_in"), "rs_join_halves_in")
    grad_w_in, delta_w_in, new_m_w_in, new_v_w_in = _adamw(w_in2, grad_w_in, m_w_in[0], v_w_in[0], "adamw_w_in")

    def lead(a):
        return a.reshape((1,) + a.shape)

    grads = [lead(grad_w_ada), grad_b_ada, grad_g_pre, lead(grad_w_in), grad_conv_w, grad_conv_b, grad_g_conv,
             grad_g_attn, lead(grad_w_out), grad_g_post]
    deltas = [lead(delta_w_ada), d_small[0], d_small[1], lead(delta_w_in), d_small[2], d_small[3], d_small[4],
              d_small[5], lead(delta_w_out), d_small[6]]
    new_ms = [lead(new_m_w_ada), m_small[0], m_small[1], lead(new_m_w_in), m_small[2], m_small[3], m_small[4],
              m_small[5], lead(new_m_w_out), m_small[6]]
    new_vs = [lead(new_v_w_ada), v_small[0], v_small[1], lead(new_v_w_in), v_small[2], v_small[3], v_small[4],
              v_small[5], lead(new_v_w_out), v_small[6]]
    return (loss, lead(grad_x), *grads, *deltas, *new_ms, *new_vs)
```

```python
import functools

import jax
import jax.numpy as jnp
from jax import lax
from jax.experimental import pallas as pl
from jax.experimental.pallas import tpu as pltpu

F32 = jnp.float32
BF16 = jnp.bfloat16
MESH = pl.DeviceIdType.MESH
HBM = pl.BlockSpec(memory_space=pltpu.HBM)
VMEM = pl.BlockSpec(memory_space=pltpu.VMEM)
ANY = pl.BlockSpec(memory_space=pl.ANY)
SEM = pl.BlockSpec(memory_space=pltpu.SEMAPHORE)
EFFECT = pltpu.SideEffectType.DATAFLOW_SIDE_EFFECTING
SUBLANES, LANES = 8, 128
TOKEN = jax.ShapeDtypeStruct((SUBLANES, LANES), jnp.float32)

HEAD_DIM = 64
PAIR = 2 * HEAD_DIM
assert PAIR == LANES
BRANCHES = ((128, 1), (512, 4), (2048, 16))
SIDE = 64
EPS = 1e-6
NEG_INF = -1e30
N_CHIPS = 4
N_DEV = 8

ADAM_LR = 0.001
ADAM_B1 = 0.9
ADAM_B2 = 0.999
ADAM_EPS = 1e-08
ADAM_WD = 0.01
ADAM_STEP = 10

VMEM_LIMIT_BYTES = 56 * 1024 * 1024
ROW_TILE = 256
COL_TILE = 512
CONV_TILE = 256
ATT_BQ = 128
ATT_KW = ATT_BQ + 2 * SIDE
ATT_UNROLL = 4
SMALL_ALIGN = SUBLANES * LANES


def _params(semantics=None):
    kw = {"vmem_limit_bytes": VMEM_LIMIT_BYTES}
    if semantics is not None:
        kw["dimension_semantics"] = semantics
    return pltpu.CompilerParams(**kw)


def _silu(z):
    return z * jax.nn.sigmoid(z)


def _silu_grad(z):
    s = jax.nn.sigmoid(z)
    return s * (1.0 + z * (1.0 - s))


def _my_place():
    return lax.axis_index("x"), lax.axis_index("y"), lax.axis_index("c")


def _flip(a, bit):
    return 1 - a if bit else a


def _chip_of(x, y):
    return 2 * x + y


def _allgather8(v, name, after=()):
    rows_per, n = v.shape

    def body(v_ref, *rest):
        out_ref, send_sems, recv_sems = rest[len(after):]
        x, y, c = _my_place()
        me = 4 * x + 2 * y + c

        def rows(idx):
            return out_ref.at[pl.ds(pl.multiple_of(idx * rows_per, rows_per), rows_per), :]

        out_ref[pl.ds(pl.multiple_of(me * rows_per, rows_per), rows_per), :] = v_ref[...]
        copies = []
        for k in range(1, N_DEV):
            peer = (_flip(x, k & 4), _flip(y, k & 2), _flip(c, k & 1))
            cp = pltpu.make_async_remote_copy(
                src_ref=v_ref, dst_ref=rows(me), send_sem=send_sems.at[k - 1], recv_sem=recv_sems.at[k - 1],
                device_id=peer, device_id_type=MESH)
            cp.start()
            copies.append((cp, peer))
        for k, (cp, peer) in enumerate(copies):
            src = 4 * peer[0] + 2 * peer[1] + peer[2]
            pltpu.make_async_remote_copy(
                src_ref=v_ref, dst_ref=rows(src), send_sem=send_sems.at[k], recv_sem=recv_sems.at[k],
                device_id=peer, device_id_type=MESH).wait_recv()
        for cp, _ in copies:
            cp.wait_send()

    return pl.pallas_call(
        body, name=name,
        out_shape=jax.ShapeDtypeStruct((N_DEV * rows_per, n), v.dtype),
        in_specs=[VMEM] + [ANY] * len(after), out_specs=VMEM,
        scratch_shapes=[pltpu.SemaphoreType.DMA((N_DEV - 1,)), pltpu.SemaphoreType.DMA((N_DEV - 1,))],
    )(v, *after)


def _allgather8_start(v, me, name):
    rows_per, n = v.shape
    land = lax.dynamic_update_slice(jnp.zeros((N_DEV * rows_per, n), v.dtype), v, (me * rows_per, 0))

    def body(v_ref, land_ref, send_sems, recv_sems, v_thru, land_thru, token_ref):
        del v_thru, land_thru
        x, y, c = _my_place()
        mine = land_ref.at[pl.ds(pl.multiple_of((4 * x + 2 * y + c) * rows_per, rows_per), rows_per), :]
        for k in range(1, N_DEV):
            peer = (_flip(x, k & 4), _flip(y, k & 2), _flip(c, k & 1))
            pltpu.make_async_remote_copy(
                src_ref=v_ref, dst_ref=mine, send_sem=send_sems.at[k - 1], recv_sem=recv_sems.at[k - 1],
                device_id=peer, device_id_type=MESH).start()
        token_ref[...] = jnp.zeros(token_ref.shape, F32)

    sems = pltpu.SemaphoreType.DMA((N_DEV - 1,))
    return pl.pallas_call(
        body, name=name,
        out_shape=(sems, sems, jax.ShapeDtypeStruct(v.shape, v.dtype), jax.ShapeDtypeStruct(land.shape, land.dtype), TOKEN),
        in_specs=[HBM, HBM], out_specs=(SEM, SEM, HBM, HBM, VMEM),
        input_output_aliases={0: 2, 1: 3},
        compiler_params=pltpu.CompilerParams(has_side_effects=EFFECT),
    )(pltpu.with_memory_space_constraint(v, pltpu.HBM), pltpu.with_memory_space_constraint(land, pltpu.HBM))


def _allgather8_wait(send_sems, recv_sems, v, land, after, name):
    rows_per = v.shape[0]

    def body(v_ref, land_ref, send_ref, recv_ref, *rest):
        del rest
        x, y, c = _my_place()
        for k in range(1, N_DEV):
            peer = (_flip(x, k & 4), _flip(y, k & 2), _flip(c, k & 1))
            src = 4 * peer[0] + 2 * peer[1] + peer[2]
            cp = pltpu.make_async_remote_copy(
                src_ref=v_ref, dst_ref=land_ref.at[pl.ds(pl.multiple_of(src * rows_per, rows_per), rows_per), :],
                send_sem=send_ref.at[k - 1], recv_sem=recv_ref.at[k - 1], device_id=peer, device_id_type=MESH)
            cp.wait_send()
            cp.wait_recv()

    return pl.pallas_call(
        body, name=name,
        out_shape=(jax.ShapeDtypeStruct(v.shape, v.dtype), jax.ShapeDtypeStruct(land.shape, land.dtype)),
        in_specs=[HBM, HBM, SEM, SEM] + [ANY] * len(after), out_specs=(HBM, HBM),
        input_output_aliases={0: 0, 1: 1},
        compiler_params=pltpu.CompilerParams(has_side_effects=EFFECT),
    )(v, land, send_sems, recv_sems, *after)[1]


def _half_rows(ref, chip, which, half):
    return ref.at[chip, pl.ds(pl.multiple_of(which * half, half), half), :]


def _ici_peers(x, y, c):
    peers = [(_flip(x, k & 2), _flip(y, k & 1), c) for k in (1, 2, 3)]
    return [(peer, _chip_of(peer[0], peer[1])) for peer in peers]


def _part_rows(ref, chip, core, part):
    quarter = ref.shape[1] // 4
    return ref.at[chip, pl.ds(pl.multiple_of((2 * core + part) * quarter, quarter), quarter), :]


def _neighbours(x, y, c):
    return [((x, 1 - y, c), _chip_of(x, 1 - y)), ((1 - x, y, c), _chip_of(1 - x, y)),
            ((1 - x, 1 - y, c), _chip_of(1 - x, 1 - y))]


def _start_direct(buf, send_sems, recv_sems):
    x, y, c = _my_place()
    me = _chip_of(x, y)
    for n, (peer, _) in enumerate(_neighbours(x, y, c)[0:2]):
        for part in ((0, 1), (1, 0))[n]:
            piece = _part_rows(buf, me, c, part)
            pltpu.make_async_remote_copy(
                src_ref=piece, dst_ref=piece, send_sem=send_sems.at[2 * n + part], recv_sem=recv_sems.at[2 * n + part],
                device_id=peer, device_id_type=MESH).start()


def _relay(buf, recv_sems, relay_send, relay_recv):
    x, y, c = _my_place()
    nbrs = _neighbours(x, y, c)
    for n in range(2):
        part = n
        piece = _part_rows(buf, nbrs[n][1], c, part)
        pltpu.make_async_remote_copy(
            src_ref=piece, dst_ref=piece, send_sem=relay_send.at[part], recv_sem=recv_sems.at[2 * n + part],
            device_id=nbrs[n][0], device_id_type=MESH).wait_recv()
        pltpu.make_async_remote_copy(
            src_ref=piece, dst_ref=piece, send_sem=relay_send.at[part], recv_sem=relay_recv.at[part],
            device_id=nbrs[1 - n][0], device_id_type=MESH).start()


def _gather_start(win_slots, after):
    def body(win_in, after_ref, win_ref, send_sems, recv_sems, token_ref):
        del win_in, after_ref
        _start_direct(win_ref, send_sems, recv_sems)
        token_ref[...] = jnp.zeros(token_ref.shape, F32)

    sems = pltpu.SemaphoreType.DMA((4,))
    return pl.pallas_call(
        body, name="gather_start",
        out_shape=(jax.ShapeDtypeStruct(win_slots.shape, win_slots.dtype), sems, sems, TOKEN),
        in_specs=[HBM, ANY], out_specs=(HBM, SEM, SEM, VMEM),
        input_output_aliases={0: 0},
        compiler_params=pltpu.CompilerParams(has_side_effects=EFFECT),
    )(win_slots, after)


def _gather_relay_in(win, wout_slots, recv_in, after):
    def body(win_in, wout_in, recv_in_ref, after_ref, win_ref, wout_ref, relay_send, relay_recv, send_out, recv_out):
        del win_in, wout_in, after_ref
        _relay(win_ref, recv_in_ref, relay_send, relay_recv)
        _start_direct(wout_ref, send_out, recv_out)

    two, four = pltpu.SemaphoreType.DMA((2,)), pltpu.SemaphoreType.DMA((4,))
    return pl.pallas_call(
        body, name="gather_relay_w_in",
        out_shape=(jax.ShapeDtypeStruct(win.shape, win.dtype), jax.ShapeDtypeStruct(wout_slots.shape, wout_slots.dtype),
                   two, two, four, four),
        in_specs=[HBM, HBM, SEM, ANY], out_specs=(HBM, HBM, SEM, SEM, SEM, SEM),
        input_output_aliases={0: 0, 1: 1},
        compiler_params=pltpu.CompilerParams(has_side_effects=EFFECT),
    )(win, wout_slots, recv_in, after)


def _gather_relay_out(wout, recv_out, after):
    def body(wout_in, recv_out_ref, after_ref, wout_ref, relay_send, relay_recv):
        del wout_in, after_ref
        _relay(wout_ref, recv_out_ref, relay_send, relay_recv)

    two = pltpu.SemaphoreType.DMA((2,))
    return pl.pallas_call(
        body, name="gather_relay_w_out",
        out_shape=(jax.ShapeDtypeStruct(wout.shape, wout.dtype), two, two),
        in_specs=[HBM, SEM, ANY], out_specs=(HBM, SEM, SEM),
        input_output_aliases={0: 0},
        compiler_params=pltpu.CompilerParams(has_side_effects=EFFECT),
    )(wout, recv_out, after)


def _gather_wait_direct(buf, send_sems, recv_sems, after, name):
    def body(buf_in, send_ref, recv_ref, after_ref, buf_ref):
        del buf_in, after_ref
        x, y, c = _my_place()
        me = _chip_of(x, y)
        for n, (peer, chip) in enumerate(_neighbours(x, y, c)[0:2]):
            second = 1 - n
            pltpu.make_async_remote_copy(
                src_ref=_part_rows(buf_ref, me, c, second), dst_ref=_part_rows(buf_ref, chip, c, second),
                send_sem=send_ref.at[2 * n + second], recv_sem=recv_ref.at[2 * n + second],
                device_id=peer, device_id_type=MESH).wait_recv()
            for part in range(2):
                piece = _part_rows(buf_ref, me, c, part)
                pltpu.make_async_remote_copy(
                    src_ref=piece, dst_ref=piece, send_sem=send_ref.at[2 * n + part], recv_sem=recv_ref.at[2 * n + part],
                    device_id=peer, device_id_type=MESH).wait_send()

    return pl.pallas_call(
        body, name=name,
        out_shape=jax.ShapeDtypeStruct(buf.shape, buf.dtype),
        in_specs=[HBM, SEM, SEM, ANY], out_specs=HBM,
        input_output_aliases={0: 0},
        compiler_params=pltpu.CompilerParams(has_side_effects=EFFECT),
    )(buf, send_sems, recv_sems, after)


def _gather_wait_relayed(buf, relay_send, relay_recv, after, name):
    def body(buf_in, rsend_ref, rrecv_ref, after_ref, buf_ref):
        del buf_in, after_ref
        x, y, c = _my_place()
        nbrs = _neighbours(x, y, c)
        for n in range(2):
            relayed = _part_rows(buf_ref, nbrs[n][1], c, n)
            cp = pltpu.make_async_remote_copy(
                src_ref=relayed, dst_ref=_part_rows(buf_ref, nbrs[2][1], c, n),
                send_sem=rsend_ref.at[n], recv_sem=rrecv_ref.at[n], device_id=nbrs[1 - n][0], device_id_type=MESH)
            cp.wait_recv()
            cp.wait_send()

    return pl.pallas_call(
        body, name=name,
        out_shape=jax.ShapeDtypeStruct(buf.shape, buf.dtype),
        in_specs=[HBM, SEM, SEM, ANY], out_specs=HBM,
        input_output_aliases={0: 0},
        compiler_params=pltpu.CompilerParams(has_side_effects=EFFECT),
    )(buf, relay_send, relay_recv, after)


def _forward_halves(buf, which, name):
    half = buf.shape[1] // 2

    def body(buf_in, buf_ref, send_sems, recv_sems):
        del buf_in
        x, y, c = _my_place()
        sibling = (x, y, 1 - c)
        chips = [_neighbours(x, y, c)[n][1] for n in which]
        started = []
        for k, src_chip in enumerate(chips):
            landed = _half_rows(buf_ref, src_chip, c, half)
            fw = pltpu.make_async_remote_copy(
                src_ref=landed, dst_ref=landed, send_sem=send_sems.at[k], recv_sem=recv_sems.at[k],
                device_id=sibling, device_id_type=MESH)
            fw.start()
            started.append(fw)
        for k, src_chip in enumerate(chips):
            other = _half_rows(buf_ref, src_chip, 1 - c, half)
            pltpu.make_async_remote_copy(
                src_ref=other, dst_ref=other, send_sem=send_sems.at[k], recv_sem=recv_sems.at[k],
                device_id=sibling, device_id_type=MESH).wait_recv()
        for fw in started:
            fw.wait_send()

    return pl.pallas_call(
        body, name=name,
        out_shape=jax.ShapeDtypeStruct(buf.shape, buf.dtype),
        in_specs=[HBM], out_specs=HBM,
        input_output_aliases={0: 0},
        scratch_shapes=[pltpu.SemaphoreType.DMA((len(which),))] * 2,
    )(buf)


def _dw_swapped(a, b, row_chunks, col_chunks, name):
    r, t = a.shape
    c_all = b.shape[1]
    chunks = row_chunks * col_chunks
    rq, cq = r // row_chunks, c_all // col_chunks
    half = rq // 2
    tn = COL_TILE
    nt = cq // tn
    steps = col_chunks * nt

    def body(a_ref, b_ref, mine_ref, sib_ref, stage, send_sems, recv_sems):
        x, y, c = _my_place()
        j, n = pl.program_id(0), pl.program_id(1)
        step = j * nt + n
        slot = step % 2
        res = jnp.dot(a_ref[...], b_ref[...], preferred_element_type=F32).astype(BF16)

        def landing(jj, nn):
            cols = pl.ds(pl.multiple_of(nn * tn, tn), tn)
            return sib_ref.at[:, :, cols] if col_chunks == 1 else sib_ref.at[pl.ds(jj, 1), :, cols]

        def copy(slot_, step_, jj, nn):
            return pltpu.make_async_remote_copy(
                src_ref=stage.at[slot_], dst_ref=landing(jj, nn), send_sem=send_sems.at[slot_],
                recv_sem=recv_sems.at[step_], device_id=(x, y, 1 - c), device_id_type=MESH)

        @pl.when(step >= 2)
        def _():
            copy(slot, step, j, n).wait_send()

        for q in range(row_chunks):
            lo = res[q * rq:q * rq + half, :]
            hi = res[q * rq + half:(q + 1) * rq, :]
            mine_ref[q] = jnp.where(c == 0, lo, hi)
            stage[slot, q] = jnp.where(c == 0, hi, lo)
        copy(slot, step, j, n).start()

        @pl.when(step == steps - 1)
        def _():
            for s in range(max(steps - 2, 0), steps):
                copy(s % 2, s, j, n).wait_send()
            for s in range(steps):
                copy(s % 2, s, j, n).wait_recv()

    shape = jax.ShapeDtypeStruct((chunks, half, cq), BF16)
    return pl.pallas_call(
        body, name=name, grid=(col_chunks, nt),
        out_shape=(shape, shape),
        in_specs=[pl.BlockSpec((r, t), lambda j, n: (0, 0)), pl.BlockSpec((t, tn), lambda j, n: (0, j * nt + n))],
        out_specs=(pl.BlockSpec((row_chunks, half, tn), lambda j, n: (j, 0, n)), ANY),
        scratch_shapes=[pltpu.VMEM((2, row_chunks, half, tn), BF16), pltpu.SemaphoreType.DMA((2,)),
                        pltpu.SemaphoreType.DMA((steps,))],
        compiler_params=_params(("arbitrary", "arbitrary")),
    )(a, b)


def _owners_start(csum, name):
    land = pltpu.with_memory_space_constraint(lax.empty((N_CHIPS - 1,) + csum.shape[1:], csum.dtype), pltpu.HBM)

    def body(csum_ref, land_ref, send_sems, recv_sems, csum_thru, land_thru, token_ref):
        del csum_thru, land_thru
        x, y, c = _my_place()
        for k, (peer, owner) in enumerate(_ici_peers(x, y, c)):
            pltpu.make_async_remote_copy(
                src_ref=csum_ref.at[owner], dst_ref=land_ref.at[k], send_sem=send_sems.at[k], recv_sem=recv_sems.at[k],
                device_id=peer, device_id_type=MESH).start()
        token_ref[...] = jnp.zeros(token_ref.shape, F32)

    sems = pltpu.SemaphoreType.DMA((N_CHIPS - 1,))
    return pl.pallas_call(
        body, name=name,
        out_shape=(sems, sems, jax.ShapeDtypeStruct(csum.shape, csum.dtype),
                   jax.ShapeDtypeStruct(land.shape, land.dtype), TOKEN),
        in_specs=[HBM, HBM], out_specs=(SEM, SEM, HBM, HBM, VMEM),
        input_output_aliases={0: 2, 1: 3},
        compiler_params=pltpu.CompilerParams(has_side_effects=EFFECT),
    )(pltpu.with_memory_space_constraint(csum, pltpu.HBM), land)


def _owners_wait(send_sems, recv_sems, csum, land, after, name):
    def body(csum_ref, land_ref, send_ref, recv_ref, *rest):
        del rest
        x, y, c = _my_place()
        for k, (peer, owner) in enumerate(_ici_peers(x, y, c)):
            cp = pltpu.make_async_remote_copy(
                src_ref=csum_ref.at[owner], dst_ref=land_ref.at[k], send_sem=send_ref.at[k], recv_sem=recv_ref.at[k],
                device_id=peer, device_id_type=MESH)
            cp.wait_send()
            cp.wait_recv()

    return pl.pallas_call(
        body, name=name,
        out_shape=(jax.ShapeDtypeStruct(csum.shape, csum.dtype), jax.ShapeDtypeStruct(land.shape, land.dtype)),
        in_specs=[HBM, HBM, SEM, SEM] + [ANY] * len(after), out_specs=(HBM, HBM),
        input_output_aliases={0: 0, 1: 1},
        compiler_params=pltpu.CompilerParams(has_side_effects=EFFECT),
    )(csum, land, send_sems, recv_sems, *after)[1]


def _join_halves(full, name):
    rows = full.shape[0] // 2

    def body(full_in, full_ref, send_sem, recv_sem):
        del full_in
        x, y, c = _my_place()
        sibling = (x, y, 1 - c)
        mine = full_ref.at[pl.ds(pl.multiple_of(c * rows, rows), rows), :]
        theirs = full_ref.at[pl.ds(pl.multiple_of((1 - c) * rows, rows), rows), :]
        cp = pltpu.make_async_remote_copy(
            src_ref=mine, dst_ref=mine, send_sem=send_sem, recv_sem=recv_sem, device_id=sibling, device_id_type=MESH)
        cp.start()
        pltpu.make_async_remote_copy(
            src_ref=theirs, dst_ref=theirs, send_sem=send_sem, recv_sem=recv_sem,
            device_id=sibling, device_id_type=MESH).wait_recv()
        cp.wait_send()

    return pl.pallas_call(
        body, name=name,
        out_shape=jax.ShapeDtypeStruct(full.shape, full.dtype),
        in_specs=[HBM], out_specs=HBM,
        input_output_aliases={0: 0},
        scratch_shapes=[pltpu.SemaphoreType.DMA, pltpu.SemaphoreType.DMA],
    )(full)


def _cast_into_slot(place, w, name):
    rows, cols = w.shape
    tr = min(rows, ROW_TILE)

    def body(place_ref, w_ref, o_ref):
        del place_ref
        o_ref[...] = w_ref[...].astype(BF16)

    grid_spec = pltpu.PrefetchScalarGridSpec(
        num_scalar_prefetch=1, grid=(rows // tr,),
        in_specs=[pl.BlockSpec((tr, cols), lambda i, p: (i, 0))],
        out_specs=pl.BlockSpec((None, tr, cols), lambda i, p: (p[0], i, 0)))
    return pl.pallas_call(
        body, name=name, grid_spec=grid_spec,
        out_shape=jax.ShapeDtypeStruct((N_CHIPS, rows, cols), BF16),
        compiler_params=_params(("parallel",)),
    )(place, w)


def _ada_partial(c_all, w_ada):
    d_model, wa = w_ada.shape
    tn = 512 if wa % 512 == 0 else 256

    def body(c_ref, w_ref, o_ref):
        o_ref[...] = jnp.dot(_silu(c_ref[...]), w_ref[...], precision=lax.Precision.HIGHEST,
                             preferred_element_type=F32)

    return pl.pallas_call(
        body, name="ada_partial", grid=(wa // tn,),
        out_shape=jax.ShapeDtypeStruct((N_DEV, wa), F32),
        in_specs=[pl.BlockSpec((N_DEV, d_model), lambda i: (0, 0)), pl.BlockSpec((d_model, tn), lambda i: (0, i))],
        out_specs=pl.BlockSpec((N_DEV, tn), lambda i: (0, i)),
        compiler_params=_params(("parallel",)),
    )(c_all, w_ada)


def _prenorm(x, mod, g_pre):
    t, d = x.shape
    tb = ROW_TILE

    def body(x_ref, mod_ref, g_ref, h_ref, ht_ref):
        xv = x_ref[...]
        r = lax.rsqrt(jnp.mean(xv * xv, axis=-1, keepdims=True) + EPS)
        h = (xv * r) * g_ref[...] * (1.0 + mod_ref[1:2, :]) + mod_ref[0:1, :]
        h_ref[...] = h.astype(BF16)
        ht_ref[...] = h.T.astype(BF16)

    return pl.pallas_call(
        body, name="prenorm", grid=(t // tb,),
        out_shape=(jax.ShapeDtypeStruct((t, d), BF16), jax.ShapeDtypeStruct((d, t), BF16)),
        in_specs=[pl.BlockSpec((tb, d), lambda i: (i, 0)), pl.BlockSpec((3, d), lambda i: (0, 0)),
                  pl.BlockSpec((1, d), lambda i: (0, 0))],
        out_specs=(pl.BlockSpec((tb, d), lambda i: (i, 0)), pl.BlockSpec((d, tb), lambda i: (0, i))),
        compiler_params=_params(("parallel",)),
    )(x, mod, g_pre)


def _proj_chunks(proj, h, w, chunks, name):
    t, d = h.shape
    ws = w.shape[-1]
    tn = COL_TILE
    nt = ws // tn

    def body(chunk_ref, *refs):
        del chunk_ref
        a_ref, b_ref, o_ref = refs[-3:]
        o_ref[...] = jnp.dot(a_ref[...], b_ref[...].astype(BF16), preferred_element_type=F32).astype(BF16)

    if w.ndim == 3:
        w_spec = pl.BlockSpec((None, d, tn), lambda i, n, ch: (ch[i], 0, n))
    else:
        w_spec = pl.BlockSpec((d, tn), lambda i, n, ch: (0, n))
    first = proj is None
    grid_spec = pltpu.PrefetchScalarGridSpec(
        num_scalar_prefetch=1, grid=(chunks.shape[0], nt),
        in_specs=([] if first else [HBM]) + [pl.BlockSpec((t, d), lambda i, n, ch: (0, 0)), w_spec],
        out_specs=pl.BlockSpec((t, tn), lambda i, n, ch: (0, ch[i] * nt + n)))
    return pl.pallas_call(
        body, name=name, grid_spec=grid_spec,
        out_shape=jax.ShapeDtypeStruct((t, N_CHIPS * ws), BF16),
        input_output_aliases={} if first else {1: 0},
        compiler_params=_params(("parallel", "parallel")),
    )(*([chunks] if first else [chunks, proj]), h, w)


def _shift_rows(a, rows):
    idx = lax.broadcasted_iota(jnp.int32, a.shape, 0)
    prev = jnp.where(idx == 0, 0.0, pltpu.roll(a, 1, 0))
    nxt = jnp.where(idx == rows - 1, 0.0, pltpu.roll(a, rows - 1, 0))
    return prev, nxt


def _conv_fwd(conv_proj, conv_w, conv_b, dc):
    t = conv_proj.shape[0]
    ct = CONV_TILE
    nct = dc // ct

    def body(u_ref, cg_ref, w_ref, b_ref, co_ref):
        a = cg_ref[...].astype(F32) * u_ref[...].astype(F32)
        prev, nxt = _shift_rows(a, t)
        co_ref[...] = (w_ref[0:1, :] * prev + w_ref[1:2, :] * a + w_ref[2:3, :] * nxt + b_ref[...]).astype(BF16)

    return pl.pallas_call(
        body, name="conv_fwd", grid=(nct,),
        out_shape=jax.ShapeDtypeStruct((t, dc), BF16),
        in_specs=[pl.BlockSpec((t, ct), lambda i: (0, i)), pl.BlockSpec((t, ct), lambda i: (0, 2 * nct + i)),
                  pl.BlockSpec((3, ct), lambda i: (0, i)), pl.BlockSpec((1, ct), lambda i: (0, i))],
        out_specs=pl.BlockSpec((t, ct), lambda i: (0, i)),
        compiler_params=_params(("parallel",)),
    )(conv_proj, conv_proj, conv_w, conv_b)


def _to_residue_major(src_ref, dst_ref, r):
    seq = src_ref.shape[0] // r
    for res in range(r):
        dst_ref[res * seq:(res + 1) * seq, :] = src_ref[pl.ds(res, seq, stride=r), :].astype(dst_ref.dtype)


def _branch_operands(token_refs, stage, dil, r):
    if r == 1:
        return list(token_refs)
    for i, ref in enumerate(token_refs):
        stage[...] = ref[...].astype(F32)
        _to_residue_major(stage, dil.at[i], r)
    return [dil.at[i] for i in range(len(token_refs))]


def _scaled_queries(q):
    return (q.astype(F32) * (HEAD_DIM ** -0.5)).astype(BF16)


BLOCK_SHIFTS = (0, -SIDE, None)


def _band_bias(rel, slope):
    arel = jnp.abs(rel)
    return jnp.where(arel <= SIDE, arel.astype(F32) * slope, NEG_INF)


def _fill_bias_tiles(bias_ref, sl_ref, r, kw):
    base = lax.broadcasted_iota(jnp.int32, (ATT_BQ, kw), 1) - lax.broadcasted_iota(jnp.int32, (ATT_BQ, kw), 0)
    for hh in range(2):
        slope = -(sl_ref[hh:hh + 1, 0:kw] * float(r))
        for e, shift in enumerate(BLOCK_SHIFTS):
            shift = ATT_BQ - kw if shift is None else shift
            bias_ref[hh, e, :, 0:kw] = _band_bias(base + shift, slope)


def _fill_stacked_bias_tiles(bias_ref, sl_ref, r, kw):
    base = lax.broadcasted_iota(jnp.int32, (kw, ATT_BQ), 0) - lax.broadcasted_iota(jnp.int32, (kw, ATT_BQ), 1)
    for hh in range(2):
        slope = -(sl_ref[hh:hh + 1, 0:ATT_BQ] * float(r))
        for e, shift in enumerate(BLOCK_SHIFTS):
            shift = ATT_BQ - kw if shift is None else shift
            bias_ref[e, 0:kw, hh * ATT_BQ:(hh + 1) * ATT_BQ] = _band_bias(base + shift, slope)


def _first_head_lanes():
    return lax.broadcasted_iota(jnp.int32, (1, PAIR), 1) < HEAD_DIM


def _only_head(x, first, hh):
    return jnp.where(first if hh == 0 else jnp.logical_not(first), x, jnp.zeros_like(x))


def _block_place(g, seq_len, kw):
    nqb = seq_len // ATT_BQ
    if nqb == 1:
        row = pl.multiple_of(g * ATT_BQ, ATT_BQ)
        return row, row, 0
    res = g // nqb
    qb = g - res * nqb
    q0 = qb * ATT_BQ
    ks = jnp.clip(q0 - SIDE, 0, seq_len - kw)
    edge = jnp.where(qb == 0, 0, jnp.where(qb == nqb - 1, 2, 1))
    return (pl.multiple_of(res * seq_len + q0, ATT_BQ), pl.multiple_of(res * seq_len + ks, SIDE), edge)


def _qkv_specs(dc, da, t, index):
    return [pl.BlockSpec((t, PAIR), functools.partial(index, (4 * dc + comp * da) // PAIR)) for comp in range(3)]


def _attn_fwd(proj, slopes, dc, da):
    t = proj.shape[0]
    hp = da // PAIR
    n_blocks = t // ATT_BQ

    def body(q_ref, k_ref, v_ref, sl_ref, o_ref, lse_ref, stage, dil, bias, o_res, l_res, o_tok, l_tok):
        for b, (_, r) in enumerate(BRANCHES):
            seq_len = t // r
            kw = min(ATT_KW, seq_len)
            ops = _branch_operands([q_ref, k_ref, v_ref], stage, dil, r)
            _fill_bias_tiles(bias, sl_ref, r, kw)
            o_dst, l_dst = (o_tok.at[b], l_tok.at[b]) if r == 1 else (o_res, l_res)
            first = _first_head_lanes()

            def blocks(trip, carry, seq_len=seq_len, kw=kw, o_dst=o_dst, l_dst=l_dst, first=first, ops=ops):
                nt = (((1,), (1,)), ((), ()))
                places = [_block_place(trip * ATT_UNROLL + i, seq_len, kw) for i in range(ATT_UNROLL)]
                chains = [(i, hh) for i in range(ATT_UNROLL) for hh in range(2)]
                qs = [_scaled_queries(ops[0][pl.ds(qrow, ATT_BQ), :]) for qrow, _, _ in places]
                ks = [ops[1][pl.ds(krow, kw), :] for _, krow, _ in places]
                vs = [ops[2][pl.ds(krow, kw), :] for _, krow, _ in places]
                ss = [lax.dot_general(_only_head(qs[i], first, hh), ks[i], nt, preferred_element_type=F32)
                      + bias[hh, places[i][2], :, 0:kw] for i, hh in chains]
                tops = [jnp.max(s, axis=-1, keepdims=True) for s in ss]
                ps = [jnp.exp(s - m) for s, m in zip(ss, tops)]
                dens = [jnp.sum(p, axis=-1, keepdims=True) for p in ps]
                for i, (qrow, _, _) in enumerate(places):
                    weights = jnp.concatenate([ps[2 * i].astype(BF16), ps[2 * i + 1].astype(BF16)], axis=1)
                    values = jnp.concatenate([_only_head(vs[i], first, 0), _only_head(vs[i], first, 1)], axis=0)
                    den = jnp.where(first, dens[2 * i], dens[2 * i + 1])
                    o_dst[pl.ds(qrow, ATT_BQ), :] = jnp.dot(weights, values, preferred_element_type=F32) / den
                    l_dst[pl.ds(qrow, ATT_BQ), :] = jnp.where(first, tops[2 * i], tops[2 * i + 1]) + jnp.log(den)
                return carry

            lax.fori_loop(0, n_blocks // ATT_UNROLL, blocks, 0)
            if r > 1:
                for res in range(r):
                    rows = slice(res * seq_len, (res + 1) * seq_len)
                    o_tok[b, pl.ds(res, seq_len, stride=r), :] = o_res[rows, :]
                    l_tok[b, pl.ds(res, seq_len, stride=r), :] = l_res[rows, :]

        def merge(i, carry):
            rows = pl.ds(pl.multiple_of(i * ROW_TILE, ROW_TILE), ROW_TILE)
            la, lb, lc = l_tok[0, rows, :], l_tok[1, rows, :], l_tok[2, rows, :]
            m = jnp.maximum(jnp.maximum(la, lb), lc)
            wa, wb, wc = jnp.exp(la - m), jnp.exp(lb - m), jnp.exp(lc - m)
            den = wa + wb + wc
            o_ref[rows, :] = (wa * o_tok[0, rows, :] + wb * o_tok[1, rows, :] + wc * o_tok[2, rows, :]) * (1.0 / den)
            lse_ref[rows, :] = m + jnp.log(den)
            return carry

        lax.fori_loop(0, t // ROW_TILE, merge, 0)

    pair_spec = pl.BlockSpec((None, t, PAIR), lambda h: (h, 0, 0))
    return pl.pallas_call(
        body, name="attn_fwd", grid=(hp,),
        out_shape=(jax.ShapeDtypeStruct((hp, t, PAIR), F32), jax.ShapeDtypeStruct((hp, t, PAIR), F32)),
        in_specs=_qkv_specs(dc, da, t, lambda first, h: (0, first + h))
        + [pl.BlockSpec((None, 8, ATT_KW), lambda h: (h, 0, 0))],
        out_specs=(pair_spec, pair_spec),
        scratch_shapes=[pltpu.VMEM((t, PAIR), F32), pltpu.VMEM((3, t, PAIR), BF16),
                        pltpu.VMEM((2, 3, ATT_BQ, ATT_KW), F32),
                        pltpu.VMEM((t, PAIR), F32), pltpu.VMEM((t, PAIR), F32),
                        pltpu.VMEM((3, t, PAIR), F32), pltpu.VMEM((3, t, PAIR), F32)],
        compiler_params=_params(("parallel",)),
    )(proj, proj, proj, slopes)


def _attn_bwd(dproj, proj, d_o, lse, delta, slopes, dc, da, after):
    t = proj.shape[0]
    hp = da // PAIR
    n_blocks = t // ATT_BQ

    def all_branches(q_ref, k_ref, v_ref, do_ref, lse_ref, dl_ref, sl_ref,
                     stage, dil, packed, packed_res, row_vecs, bias_t, acc, tot):
        first = _first_head_lanes()
        lane = lax.broadcasted_iota(jnp.int32, (1, PAIR), 1)
        packed[...] = jnp.where((lane & (HEAD_DIM - 1)) < HEAD_DIM // 2, lse_ref[...], dl_ref[...])
        for b, (_, r) in enumerate(BRANCHES):
            seq_len = t // r
            kw = min(ATT_KW, seq_len)
            ops = _branch_operands([q_ref, k_ref, v_ref, do_ref], stage, dil, r)
            scalars = packed
            if r > 1:
                _to_residue_major(packed, packed_res, r)
                scalars = packed_res
            for g in range(n_blocks):
                flipped = scalars[g * ATT_BQ:(g + 1) * ATT_BQ, :].T
                for row in range(4):
                    row_vecs[g, row:row + 1, :] = flipped[row * (HEAD_DIM // 2):row * (HEAD_DIM // 2) + 1, :]
            _fill_stacked_bias_tiles(bias_t, sl_ref, r, kw)
            acc[1] = jnp.zeros((t, PAIR), F32)
            acc[2] = jnp.zeros((t, PAIR), F32)

            def blocks(trip, carry, seq_len=seq_len, kw=kw, ops=ops):
                nt = (((1,), (1,)), ((), ()))
                group = range(ATT_UNROLL)
                places = [_block_place(trip * ATT_UNROLL + i, seq_len, kw) for i in group]
                ks, vs, q2s, do2s, lse2s, dl2s = [], [], [], [], [], []
                for i, (qrow, krow, _) in zip(group, places):
                    q = _scaled_queries(ops[0][pl.ds(qrow, ATT_BQ), :])
                    dov = ops[3][pl.ds(qrow, ATT_BQ), :]
                    ks.append(ops[1][pl.ds(krow, kw), :])
                    vs.append(ops[2][pl.ds(krow, kw), :])
                    q2s.append(jnp.concatenate([_only_head(q, first, 0), _only_head(q, first, 1)], axis=0))
                    do2s.append(jnp.concatenate([_only_head(dov, first, 0), _only_head(dov, first, 1)], axis=0))
                    rows = row_vecs[trip * ATT_UNROLL + i]
                    lse2s.append(jnp.concatenate([rows[0:1, :], rows[2:3, :]], axis=1))
                    dl2s.append(jnp.concatenate([rows[1:2, :], rows[3:4, :]], axis=1))
                s_ts = [lax.dot_general(ks[i], q2s[i], nt, preferred_element_type=F32) for i in group]
                dp_ts = [lax.dot_general(vs[i], do2s[i], nt, preferred_element_type=F32) for i in group]
                p_ts = [jnp.exp(s_ts[i] + bias_t[places[i][2], 0:kw, :] - lse2s[i]) for i in group]
                ds_ts = [p_ts[i] * (dp_ts[i] - dl2s[i]) for i in group]
                dvs = [jnp.dot(p_ts[i].astype(BF16), do2s[i], preferred_element_type=F32) for i in group]
                dks = [jnp.dot(ds_ts[i].astype(BF16), q2s[i], preferred_element_type=F32) for i in group]
                dss = [ds_ts[i].T.astype(BF16) for i in group]
                dqs = [jnp.dot(dss[i][0:ATT_BQ, :], _only_head(ks[i], first, 0), preferred_element_type=F32)
                       + jnp.dot(dss[i][ATT_BQ:2 * ATT_BQ, :], _only_head(ks[i], first, 1), preferred_element_type=F32)
                       for i in group]
                for i, (qrow, krow, _) in zip(group, places):
                    acc[0, pl.ds(qrow, ATT_BQ), :] = dqs[i] * (HEAD_DIM ** -0.5)
                    acc[1, pl.ds(krow, kw), :] += dks[i]
                    acc[2, pl.ds(krow, kw), :] += dvs[i]
                return carry

            lax.fori_loop(0, n_blocks // ATT_UNROLL, blocks, 0)
            for comp in range(3):
                if r == 1:
                    tot[comp] = acc[comp]
                else:
                    for res in range(r):
                        tok = pl.ds(res, seq_len, stride=r)
                        tot[comp, tok, :] = tot[comp, tok, :] + acc[comp, res * seq_len:(res + 1) * seq_len, :]

    first_q = (4 * dc) // PAIR

    def body(dproj_in, q_ref, k_ref, v_ref, do_ref, lse_ref, dl_ref, sl_ref, after_ref, out_ref, *scratch):
        del dproj_in, after_ref
        work, out_stage, out_sems = scratch[:-2], scratch[-2], scratch[-1]
        h = pl.program_id(0)
        all_branches(q_ref, k_ref, v_ref, do_ref, lse_ref, dl_ref, sl_ref, *work)

        def out_copy(comp):
            cols = pl.ds(pl.multiple_of((first_q + comp * hp + h) * PAIR, PAIR), PAIR)
            return pltpu.make_async_copy(out_stage.at[comp], out_ref.at[:, cols], out_sems.at[comp])

        @pl.when(h > 0)
        def _():
            for comp in range(3):
                out_copy(comp).wait()

        for comp in range(3):
            out_stage[comp] = work[-1][comp].astype(BF16)
            out_copy(comp).start()

        @pl.when(h == hp - 1)
        def _():
            for comp in range(3):
                out_copy(comp).wait()

    pair_spec = pl.BlockSpec((None, t, PAIR), lambda h: (h, 0, 0))
    return pl.pallas_call(
        body, name="attn_bwd", grid=(hp,),
        out_shape=jax.ShapeDtypeStruct(dproj.shape, BF16),
        in_specs=[HBM] + _qkv_specs(dc, da, t, lambda first, h: (0, first + h))
        + [pair_spec, pair_spec, pair_spec, pl.BlockSpec((None, 8, ATT_KW), lambda h: (h, 0, 0)), ANY],
        out_specs=ANY,
        input_output_aliases={0: 0},
        scratch_shapes=[pltpu.VMEM((t, PAIR), F32), pltpu.VMEM((4, t, PAIR), BF16),
                        pltpu.VMEM((t, PAIR), F32), pltpu.VMEM((t, PAIR), F32),
                        pltpu.VMEM((n_blocks, 8, ATT_BQ), F32), pltpu.VMEM((3, ATT_KW, 2 * ATT_BQ), F32),
                        pltpu.VMEM((3, t, PAIR), F32), pltpu.VMEM((3, t, PAIR), F32),
                        pltpu.VMEM((3, t, PAIR), BF16), pltpu.SemaphoreType.DMA((3,))],
        compiler_params=_params(("arbitrary",)),
    )(dproj, proj, proj, proj, d_o, lse, delta, slopes, after)


def _mix_fwd(co, proj, o_mix, g_conv, g_attn_pairs):
    t, dc = co.shape
    hp = o_mix.shape[0]
    da = hp * PAIR
    tb = ROW_TILE

    def body(co_ref, bg_ref, zc_ref, za_ref, om_ref, gc_ref, ga_ref, ycat_ref, ycatt_ref):
        p = bg_ref[...].astype(F32) * co_ref[...].astype(F32)
        rc = lax.rsqrt(jnp.mean(p * p, axis=-1, keepdims=True) + EPS)
        yc = (p * rc) * gc_ref[...] * _silu(zc_ref[...].astype(F32))
        ycat_ref[:, 0:dc] = yc.astype(BF16)
        ycatt_ref[0:dc, :] = yc.T.astype(BF16)
        ssq = jnp.zeros((tb, 1), F32)
        for h in range(hp):
            o = om_ref[h]
            ssq = ssq + jnp.sum(o * o, axis=-1, keepdims=True)
        ra = lax.rsqrt(ssq * (1.0 / da) + EPS)
        for h in range(hp):
            ya = (om_ref[h] * ra) * ga_ref[h] * _silu(za_ref[:, h * PAIR:(h + 1) * PAIR].astype(F32))
            ycat_ref[:, dc + h * PAIR:dc + (h + 1) * PAIR] = ya.astype(BF16)
            ycatt_ref[dc + h * PAIR:dc + (h + 1) * PAIR, :] = ya.T.astype(BF16)

    pair_spec = pl.BlockSpec((hp, tb, PAIR), lambda i: (0, i, 0))
    return pl.pallas_call(
        body, name="mix_fwd", grid=(t // tb,),
        out_shape=(jax.ShapeDtypeStruct((t, dc + da), BF16), jax.ShapeDtypeStruct((dc + da, t), BF16)),
        in_specs=[pl.BlockSpec((tb, dc), lambda i: (i, 0)),
                  pl.BlockSpec((tb, dc), lambda i: (i, 1)),
                  pl.BlockSpec((tb, dc), lambda i: (i, 3)),
                  pl.BlockSpec((tb, da), lambda i: (i, 7)),
                  pair_spec,
                  pl.BlockSpec((1, dc), lambda i: (0, 0)),
                  pl.BlockSpec((hp, 1, PAIR), lambda i: (0, 0, 0))],
        out_specs=(pl.BlockSpec((tb, dc + da), lambda i: (i, 0)), pl.BlockSpec((dc + da, tb), lambda i: (0, i))),
        compiler_params=_params(("parallel",)),
    )(co, proj, proj, proj, o_mix, g_conv, g_attn_pairs)


def _out_fwd_bwd(ycat, woutf, x, target, mod, g_post):
    t, d = x.shape
    n = ycat.shape[1]
    tb = ROW_TILE

    def body(a_ref, w_ref, x_ref, tg_ref, mod_ref, g_ref, dout_ref, dy_ref, acc_ref):
        y = jnp.dot(a_ref[...], w_ref[...], preferred_element_type=F32)
        r = lax.rsqrt(jnp.mean(y * y, axis=-1, keepdims=True) + EPS)
        nh = y * r
        gate = mod_ref[2:3, :]
        nrm = nh * g_ref[...]
        err = x_ref[...] + gate * nrm - tg_ref[...]
        dout = err * (1.0 / d)
        dout_ref[...] = dout.astype(BF16)
        dn = dout * gate
        a = dn * g_ref[...]
        dy = r * (a - nh * jnp.mean(a * nh, axis=-1, keepdims=True))
        dy_ref[...] = dy.astype(BF16)
        loss = 0.5 * jnp.sum(jnp.sum(err * err, axis=-1, keepdims=True) * (1.0 / d), axis=0, keepdims=True)
        part = jnp.concatenate(
            [jnp.sum(dout * nrm, axis=0, keepdims=True), jnp.sum(dn * nh, axis=0, keepdims=True),
             jnp.broadcast_to(loss, (1, d)), jnp.zeros((5, d), F32)], axis=0)

        @pl.when(pl.program_id(0) == 0)
        def _():
            acc_ref[...] = jnp.zeros(acc_ref.shape, F32)

        acc_ref[...] += part

    return pl.pallas_call(
        body, name="out_fwd_bwd", grid=(t // tb,),
        out_shape=(jax.ShapeDtypeStruct((t, d), BF16), jax.ShapeDtypeStruct((t, d), BF16),
                   jax.ShapeDtypeStruct((8, d), F32)),
        in_specs=[pl.BlockSpec((tb, n), lambda i: (i, 0)), pl.BlockSpec((n, d), lambda i: (0, 0)),
                  pl.BlockSpec((tb, d), lambda i: (i, 0)), pl.BlockSpec((tb, d), lambda i: (i, 0)),
                  pl.BlockSpec((3, d), lambda i: (0, 0)), pl.BlockSpec((1, d), lambda i: (0, 0))],
        out_specs=(pl.BlockSpec((tb, d), lambda i: (i, 0)), pl.BlockSpec((tb, d), lambda i: (i, 0)),
                   pl.BlockSpec((8, d), lambda i: (0, 0))),
        compiler_params=_params(("arbitrary",)),
    )(ycat, woutf, x, target, mod, g_post)


def _matmul_nt(a, b, out_dtype, name):
    m, k = a.shape
    n = b.shape[0]
    tn = COL_TILE

    def body(a_ref, b_ref, o_ref):
        o_ref[...] = lax.dot_general(a_ref[...], b_ref[...], (((1,), (1,)), ((), ())),
                                     preferred_element_type=F32).astype(out_dtype)

    return pl.pallas_call(
        body, name=name, grid=(n // tn,),
        out_shape=jax.ShapeDtypeStruct((m, n), out_dtype),
        in_specs=[pl.BlockSpec((m, k), lambda i: (0, 0)), pl.BlockSpec((tn, k), lambda i: (i, 0))],
        out_specs=pl.BlockSpec((m, tn), lambda i: (0, i)),
        compiler_params=_params(("parallel",)),
    )(a, b)


def _mix_bwd(dycat, co, proj, o_mix, g_conv, g_attn_pairs):
    t, dc = co.shape
    hp = o_mix.shape[0]
    da = hp * PAIR
    tb = ROW_TILE

    def body(dy_ref, co_ref, bg_ref, zc_ref, za_ref, om_ref, gc_ref, ga_ref,
             dcp_ref, dco_ref, do_ref, dl_ref, dgc_ref, dga_ref):
        first = pl.program_id(0) == 0
        cov = co_ref[...].astype(F32)
        bg = bg_ref[...].astype(F32)
        zc = zc_ref[...].astype(F32)
        p = bg * cov
        rc = lax.rsqrt(jnp.mean(p * p, axis=-1, keepdims=True) + EPS)
        nh = p * rc
        dyc = dy_ref[:, 0:dc].astype(F32)
        dn = dyc * _silu(zc)
        a = dn * gc_ref[...]
        dp = rc * (a - nh * jnp.mean(a * nh, axis=-1, keepdims=True))
        dcp_ref[:, 0:dc] = jnp.zeros((tb, dc), BF16)
        dcp_ref[:, dc:2 * dc] = (dp * cov).astype(BF16)
        dcp_ref[:, 2 * dc:3 * dc] = jnp.zeros((tb, dc), BF16)
        dcp_ref[:, 3 * dc:4 * dc] = (dyc * nh * gc_ref[...] * _silu_grad(zc)).astype(BF16)
        dcp_ref[:, 4 * dc:4 * dc + 3 * da] = jnp.zeros((tb, 3 * da), BF16)
        dco_ref[...] = dp * bg

        @pl.when(first)
        def _():
            dgc_ref[...] = jnp.zeros(dgc_ref.shape, F32)
            dga_ref[...] = jnp.zeros(dga_ref.shape, F32)

        dgc_ref[...] += jnp.sum(dn * nh, axis=0, keepdims=True)

        ssq = jnp.zeros((tb, 1), F32)
        for h in range(hp):
            o = om_ref[h]
            ssq = ssq + jnp.sum(o * o, axis=-1, keepdims=True)
        ra = lax.rsqrt(ssq * (1.0 / da) + EPS)
        dot_an = jnp.zeros((tb, 1), F32)
        for h in range(hp):
            nha = om_ref[h] * ra
            za = za_ref[:, h * PAIR:(h + 1) * PAIR].astype(F32)
            dya = dy_ref[:, dc + h * PAIR:dc + (h + 1) * PAIR].astype(F32)
            dna = dya * _silu(za)
            dza = (dya * nha * ga_ref[h] * _silu_grad(za)).astype(BF16)
            dcp_ref[:, 4 * dc + 3 * da + h * PAIR:4 * dc + 3 * da + (h + 1) * PAIR] = dza
            dga_ref[h] += jnp.sum(dna * nha, axis=0, keepdims=True)
            dot_an = dot_an + jnp.sum(dna * ga_ref[h] * nha, axis=-1, keepdims=True)
        mean_an = dot_an * (1.0 / da)
        first_head = lax.broadcasted_iota(jnp.int32, (tb, PAIR), 1) < HEAD_DIM
        for h in range(hp):
            o = om_ref[h]
            nha = o * ra
            za = za_ref[:, h * PAIR:(h + 1) * PAIR].astype(F32)
            dya = dy_ref[:, dc + h * PAIR:dc + (h + 1) * PAIR].astype(F32)
            aa = dya * _silu(za) * ga_ref[h]
            d_o = ra * (aa - nha * mean_an)
            do_ref[h] = d_o.astype(BF16)
            prod = d_o * o
            both = jnp.sum(prod, axis=-1, keepdims=True)
            head0 = jnp.sum(jnp.where(first_head, prod, 0.0), axis=-1, keepdims=True)
            dl_ref[h] = jnp.where(first_head, head0, both - head0)

    pair_spec = pl.BlockSpec((hp, tb, PAIR), lambda i: (0, i, 0))
    return pl.pallas_call(
        body, name="mix_bwd", grid=(t // tb,),
        out_shape=(jax.ShapeDtypeStruct((t, 4 * dc + 4 * da), BF16), jax.ShapeDtypeStruct((t, dc), F32),
                   jax.ShapeDtypeStruct((hp, t, PAIR), BF16), jax.ShapeDtypeStruct((hp, t, PAIR), F32),
                   jax.ShapeDtypeStruct((1, dc), F32), jax.ShapeDtypeStruct((hp, 1, PAIR), F32)),
        in_specs=[pl.BlockSpec((tb, dc + da), lambda i: (i, 0)),
                  pl.BlockSpec((tb, dc), lambda i: (i, 0)),
                  pl.BlockSpec((tb, dc), lambda i: (i, 1)),
                  pl.BlockSpec((tb, dc), lambda i: (i, 3)),
                  pl.BlockSpec((tb, da), lambda i: (i, 7)),
                  pair_spec,
                  pl.BlockSpec((1, dc), lambda i: (0, 0)),
                  pl.BlockSpec((hp, 1, PAIR), lambda i: (0, 0, 0))],
        out_specs=(pl.BlockSpec((tb, 4 * dc + 4 * da), lambda i: (i, 0)), pl.BlockSpec((tb, dc), lambda i: (i, 0)),
                   pair_spec, pair_spec,
                   pl.BlockSpec((1, dc), lambda i: (0, 0)), pl.BlockSpec((hp, 1, PAIR), lambda i: (0, 0, 0))),
        compiler_params=_params(("arbitrary",)),
    )(dycat, co, proj, proj, proj, o_mix, g_conv, g_attn_pairs)


def _conv_bwd(dconv_proj, dco, conv_proj, conv_w, dc, after):
    t = dco.shape[0]
    ct = CONV_TILE
    nct = dc // ct

    def body(dcp_in_ref, dco_ref, u_ref, cg_ref, w_ref, after_ref, dcp_ref, acc_ref):
        del dcp_in_ref, after_ref
        which = pl.program_id(1)
        g = dco_ref[...]
        u = u_ref[...].astype(F32)
        cg = cg_ref[...].astype(F32)
        g_prev, g_next = _shift_rows(g, t)
        da = w_ref[0:1, :] * g_next + w_ref[1:2, :] * g + w_ref[2:3, :] * g_prev
        dcp_ref[...] = (da * jnp.where(which == 0, cg, u)).astype(BF16)
        a = cg * u
        a_prev, a_next = _shift_rows(a, t)
        acc_ref[...] = jnp.concatenate(
            [jnp.sum(g * a_prev, axis=0, keepdims=True), jnp.sum(g * a, axis=0, keepdims=True),
             jnp.sum(g * a_next, axis=0, keepdims=True), jnp.sum(g, axis=0, keepdims=True),
             jnp.zeros((4, ct), F32)], axis=0)

    return pl.pallas_call(
        body, name="conv_bwd", grid=(nct, 2),
        out_shape=(jax.ShapeDtypeStruct(dconv_proj.shape, BF16), jax.ShapeDtypeStruct((8, dc), F32)),
        in_specs=[HBM,
                  pl.BlockSpec((t, ct), lambda i, s: (0, i)),
                  pl.BlockSpec((t, ct), lambda i, s: (0, i)),
                  pl.BlockSpec((t, ct), lambda i, s: (0, 2 * nct + i)),
                  pl.BlockSpec((3, ct), lambda i, s: (0, i)), ANY],
        out_specs=(pl.BlockSpec((t, ct), lambda i, s: (0, 2 * s * nct + i)),
                   pl.BlockSpec((8, ct), lambda i, s: (0, i))),
        input_output_aliases={0: 0},
        compiler_params=_params(("arbitrary", "arbitrary")),
    )(dconv_proj, dco, conv_proj, conv_proj, conv_w, after)


def _dh(dproj, winf, after):
    t = dproj.shape[0]
    _, d, ws = winf.shape
    tm = tn = COL_TILE
    nt = (((1,), (1,)), ((), ()))

    def body(a_ref, w_ref, after_ref, o_ref):
        del after_ref
        acc = lax.dot_general(a_ref[:, 0:ws], w_ref[0], nt, preferred_element_type=F32)
        for j in range(1, N_CHIPS):
            acc = acc + lax.dot_general(a_ref[:, j * ws:(j + 1) * ws], w_ref[j], nt, preferred_element_type=F32)
        o_ref[...] = acc.astype(BF16)

    return pl.pallas_call(
        body, name="dh", grid=(d // tn, t // tm),
        out_shape=jax.ShapeDtypeStruct((t, d), BF16),
        in_specs=[pl.BlockSpec((tm, N_CHIPS * ws), lambda n, m: (m, 0)),
                  pl.BlockSpec((N_CHIPS, tn, ws), lambda n, m: (0, n, 0)), ANY],
        out_specs=pl.BlockSpec((tm, tn), lambda n, m: (m, n)),
        compiler_params=_params(("parallel", "parallel")),
    )(dproj, winf, after)


def _prenorm_bwd(x, dh, dout, mod, g_pre):
    t, d = x.shape
    tb = ROW_TILE

    def body(x_ref, dh_ref, dout_ref, mod_ref, g_ref, gx_ref, acc_ref):
        xv = x_ref[...]
        dhv = dh_ref[...].astype(F32)
        r = lax.rsqrt(jnp.mean(xv * xv, axis=-1, keepdims=True) + EPS)
        xh = xv * r
        one_scale = 1.0 + mod_ref[1:2, :]
        a = dhv * one_scale * g_ref[...]
        gx_ref[...] = dout_ref[...].astype(F32) + r * (a - xh * jnp.mean(a * xh, axis=-1, keepdims=True))
        part = jnp.concatenate(
            [jnp.sum(dhv, axis=0, keepdims=True), jnp.sum(dhv * xh * g_ref[...], axis=0, keepdims=True),
             jnp.sum(dhv * xh * one_scale, axis=0, keepdims=True), jnp.zeros((5, d), F32)], axis=0)

        @pl.when(pl.program_id(0) == 0)
        def _():
            acc_ref[...] = jnp.zeros(acc_ref.shape, F32)

        acc_ref[...] += part

    return pl.pallas_call(
        body, name="prenorm_bwd", grid=(t // tb,),
        out_shape=(jax.ShapeDtypeStruct((t, d), F32), jax.ShapeDtypeStruct((8, d), F32)),
        in_specs=[pl.BlockSpec((tb, d), lambda i: (i, 0)), pl.BlockSpec((tb, d), lambda i: (i, 0)),
                  pl.BlockSpec((tb, d), lambda i: (i, 0)), pl.BlockSpec((3, d), lambda i: (0, 0)),
                  pl.BlockSpec((1, d), lambda i: (0, 0))],
        out_specs=(pl.BlockSpec((tb, d), lambda i: (i, 0)), pl.BlockSpec((8, d), lambda i: (0, 0))),
        compiler_params=_params(("arbitrary",)),
    )(x, dh, dout, mod, g_pre)


def _chip_sums(mine, rsib, name):
    _, half, cols = mine.shape
    tr = min(half, ROW_TILE)

    def body(g_ref, r_ref, o_ref):
        o_ref[...] = (g_ref[...].astype(F32) + r_ref[...].astype(F32)).astype(BF16)

    spec = pl.BlockSpec((None, tr, cols), lambda j, i: (j, i, 0))
    return pl.pallas_call(
        body, name=name, grid=(N_CHIPS, half // tr),
        out_shape=jax.ShapeDtypeStruct(mine.shape, BF16),
        in_specs=[spec, spec], out_specs=spec,
        compiler_params=_params(("parallel", "parallel")),
    )(mine, rsib)


def _owner_sum(place, mine, rsib, rici, name):
    _, half, cols = mine.shape
    rows = 2 * half
    tr = min(half, ROW_TILE)
    nt = half // tr

    def body(place_ref, g_ref, r_ref, i_ref, o_ref):
        del place_ref
        acc = g_ref[...].astype(F32) + r_ref[...].astype(F32)
        for k in range(N_CHIPS - 1):
            acc = acc + i_ref[k].astype(F32)
        o_ref[...] = acc

    grid_spec = pltpu.PrefetchScalarGridSpec(
        num_scalar_prefetch=1, grid=(nt,),
        in_specs=[pl.BlockSpec((None, tr, cols), lambda i, p: (p[0], i, 0)),
                  pl.BlockSpec((None, tr, cols), lambda i, p: (p[0], i, 0)),
                  pl.BlockSpec((N_CHIPS - 1, tr, cols), lambda i, p: (0, i, 0))],
        out_specs=pl.BlockSpec((tr, cols), lambda i, p: (p[1] * nt + i, 0)))
    return pl.pallas_call(
        body, name=name, grid_spec=grid_spec,
        out_shape=jax.ShapeDtypeStruct((rows, cols), F32),
        compiler_params=_params(("parallel",)),
    )(place, mine, rsib, rici)


def _adam_math(w, g, m, v):
    m2 = ADAM_B1 * m + (1.0 - ADAM_B1) * g
    v2 = ADAM_B2 * v + (1.0 - ADAM_B2) * (g * g)
    m_hat = m2 / (1.0 - ADAM_B1 ** ADAM_STEP)
    v_hat = v2 / (1.0 - ADAM_B2 ** ADAM_STEP)
    delta = -ADAM_LR * (m_hat / (jnp.sqrt(v_hat) + ADAM_EPS) + ADAM_WD * w)
    return delta, m2, v2


def _adamw(w, g, m, v, name):
    rows, cols = w.shape
    tr = min(rows, ROW_TILE)

    def body(w_ref, g_ref, m_ref, v_ref, go_ref, d_ref, m2_ref, v2_ref):
        g = g_ref[...]
        go_ref[...] = g
        d_ref[...], m2_ref[...], v2_ref[...] = _adam_math(w_ref[...], g, m_ref[...], v_ref[...])

    spec = pl.BlockSpec((tr, cols), lambda i: (i, 0))
    return pl.pallas_call(
        body, name=name, grid=(rows // tr,),
        out_shape=(jax.ShapeDtypeStruct(w.shape, F32),) * 4,
        in_specs=[spec] * 4, out_specs=(spec,) * 4,
        compiler_params=_params(("parallel",)),
    )(w, g, m, v)


def _ada_grad_adamw(c_all_t, dmod_cols, w, m, v):
    d, wa = w.shape
    tr = ROW_TILE

    def body(ct_ref, dm_ref, w_ref, m_ref, v_ref, g_ref, d_ref, m2_ref, v2_ref):
        act = _silu(ct_ref[...])
        g = act[:, 0:1] * dm_ref[0:1, :]
        for b in range(1, N_DEV):
            g = g + act[:, b:b + 1] * dm_ref[b:b + 1, :]
        g_ref[...] = g
        d_ref[...], m2_ref[...], v2_ref[...] = _adam_math(w_ref[...], g, m_ref[...], v_ref[...])

    spec = pl.BlockSpec((tr, wa), lambda i: (i, 0))
    return pl.pallas_call(
        body, name="ada_grad_adamw", grid=(d // tr,),
        out_shape=(jax.ShapeDtypeStruct(w.shape, F32),) * 4,
        in_specs=[pl.BlockSpec((tr, N_DEV), lambda i: (i, 0)), pl.BlockSpec((N_DEV, wa), lambda i: (0, 0)),
                  spec, spec, spec],
        out_specs=(spec,) * 4,
        compiler_params=_params(("parallel",)),
    )(c_all_t, dmod_cols, w, m, v)


def _sum_devices(gathered):
    n = gathered.shape[1]

    def body(g_ref, o_ref):
        acc = g_ref[0:8, :]
        for dev in range(1, N_DEV):
            acc = acc + g_ref[8 * dev:8 * dev + 8, :]
        o_ref[...] = acc

    return pl.pallas_call(
        body, name="sum_devices",
        out_shape=jax.ShapeDtypeStruct((8, n), F32),
        in_specs=[VMEM], out_specs=VMEM,
    )(gathered)


def _pack_small(pieces):
    flat = [p.reshape(-1).astype(F32) for p in pieces]
    offsets, total = [], 0
    for p in flat:
        offsets.append(total)
        total += p.shape[0]
    padded = -(-total // SMALL_ALIGN) * SMALL_ALIGN
    if padded > total:
        flat.append(jnp.zeros((padded - total,), F32))
    return jnp.concatenate(flat).reshape(8, padded // 8), offsets


def _alibi_slope_rows(n_heads):
    slopes = 2.0 ** (-8.0 * jnp.arange(1, n_heads + 1, dtype=F32) / n_heads)
    rows = jnp.zeros((n_heads // 2, 8), F32).at[:, 0:2].set(slopes.reshape(n_heads // 2, 2))
    return jnp.broadcast_to(rows[:, :, None], (n_heads // 2, 8, ATT_KW))


def kernel(x, c, w_ada, b_ada, g_pre, w_in, conv_w, conv_b, g_conv, g_attn, w_out, g_post, loss_target, m_w_ada, m_b_ada, m_g_pre, m_w_in, m_conv_w, m_conv_b, m_g_conv, m_g_attn, m_w_out, m_g_post, v_w_ada, v_b_ada, v_g_pre, v_w_in, v_conv_w, v_conv_b, v_g_conv, v_g_attn, v_w_out, v_g_post):
    t, d = x.shape[1], x.shape[2]
    dc = conv_b.shape[1]
    da = g_attn.shape[1]
    hp = da // PAIR
    ws = w_in.shape[2]
    wa = w_ada.shape[2]
    cws = conv_w.shape[2]
    assert t % ROW_TILE == 0 and d % ROW_TILE == 0 and dc % COL_TILE == 0 and da % COL_TILE == 0
    assert ws == 2 * dc and dc == da and t // BRANCHES[-1][1] >= ATT_BQ

    mx, my, mc = _my_place()
    chip = _chip_of(mx, my)
    dev = 2 * chip + mc
    place = jnp.stack([chip, mc]).astype(jnp.int32)

    x2, tgt2 = x[0], loss_target[0]
    w_ada2, w_in2, w_out2 = w_ada[0], w_in[0], w_out[0]

    win_slots = _cast_into_slot(place, w_in2, "cast_w_in")
    packed, offs = _pack_small([c[0], conv_w[0]])
    seen = _allgather8(packed, "gather_inputs", after=(win_slots,)).reshape(N_DEV, -1)
    c_all = seen[:, offs[0]:offs[0] + d]
    conv_w_full = seen[0::2, offs[1]:offs[1] + 3 * cws].reshape(N_CHIPS, 3, cws).transpose(1, 0, 2).reshape(3, dc)

    ada_part = _ada_partial(c_all, w_ada2)
    ada_seen = _allgather8(ada_part, "gather_ada").reshape(N_DEV, N_DEV, wa)
    mod_flat = lax.dynamic_index_in_dim(ada_seen[0::2], dev, axis=1, keepdims=False).reshape(1, 3 * d) + b_ada
    mod = mod_flat.reshape(3, d)

    win_flight, send_in, recv_in, started = _gather_start(win_slots, mod)

    y_chip, x_chip, d_chip = (_chip_of(mx, 1 - my), _chip_of(1 - mx, my), _chip_of(1 - mx, 1 - my))
    own_chunk, near_chunks, far_chunk = (jnp.stack(js).astype(jnp.int32) for js in ([chip], [y_chip, x_chip], [d_chip]))
    h, ht = _prenorm(x2, mod + started[0, 0], g_pre)
    proj = _proj_chunks(None, h, w_in2, own_chunk, "proj_own")
    win_flight, wout_flight, relay_send_in, relay_recv_in, send_out, recv_out = _gather_relay_in(
        win_flight, _cast_into_slot(place, w_out2, "cast_w_out"), recv_in, proj)
    win_flight = _forward_halves(
        _gather_wait_direct(win_flight, send_in, recv_in, proj, "gather_wait_w_in_direct"), (0, 1), "forward_w_in_direct")
    proj = _proj_chunks(proj, h, win_flight, near_chunks, "proj_neighbours")
    winf = _forward_halves(
        _gather_wait_relayed(win_flight, relay_send_in, relay_recv_in, proj, "gather_wait_w_in_relayed"),
        (2,), "forward_w_in_relayed")
    proj = _proj_chunks(proj, h, winf, far_chunk, "proj_diagonal")
    slopes = _alibi_slope_rows(da // HEAD_DIM)
    co = _conv_fwd(proj, conv_w_full, conv_b, dc)
    wout_flight, relay_send_out, relay_recv_out = _gather_relay_out(wout_flight, recv_out, co)
    o_mix, lse = _attn_fwd(proj, slopes, dc, da)
    g_attn_pairs = g_attn.reshape(hp, 1, PAIR)
    ycat, ycat_t = _mix_fwd(co, proj, o_mix, g_conv, g_attn_pairs)
    wout_flight = _gather_wait_direct(wout_flight, send_out, recv_out, ycat, "gather_wait_w_out_direct")
    wout_flight = _gather_wait_relayed(wout_flight, relay_send_out, relay_recv_out, ycat, "gather_wait_w_out_relayed")
    woutf = _forward_halves(wout_flight, (0, 1, 2), "forward_w_out").reshape(dc + da, d)
    dout, dy, post_sums = _out_fwd_bwd(ycat, woutf, x2, tgt2, mod, g_post)

    gout, rsib_out = _dw_swapped(ycat_t, dy, N_CHIPS, 1, "dw_out")
    csum_out = _chip_sums(gout, rsib_out, "rs_chip_sum_out")
    ssem_out, rsem_out, csum_out, land_out, sent_out = _owners_start(csum_out, "rs_owners_start_out")
    dycat = _matmul_nt(dy, woutf, BF16, "dycat")
    dproj, dco, d_o, delta, dg_conv, dg_attn = _mix_bwd(dycat, co, proj, o_mix, g_conv, g_attn_pairs)
    dproj, conv_sums = _conv_bwd(dproj, dco, proj, conv_w_full, dc, sent_out)
    dproj = _attn_bwd(dproj, proj, d_o, lse, delta, slopes, dc, da, sent_out)
    gin, rsib_in = _dw_swapped(ht, dproj, 1, N_CHIPS, "dw_in")
    csum_in = _chip_sums(gin, rsib_in, "rs_chip_sum_in")
    ssem_in, rsem_in, csum_in, land_in, sent_in = _owners_start(csum_in, "rs_owners_start_in")
    dh = _dh(dproj, winf, sent_in)
    grad_x, pre_sums = _prenorm_bwd(x2, dh, dout, mod, g_pre)

    small, so = _pack_small([
        pre_sums[0], pre_sums[1], post_sums[0],
        pre_sums[2], conv_sums[0:3], conv_sums[3], dg_conv, dg_attn, post_sums[1], post_sums[2, 0:128]])
    ssem_small, rsem_small, small, land_small, sent_small = _allgather8_start(small, dev, "gather_small_start")

    rici_out = _owners_wait(ssem_out, rsem_out, csum_out, land_out, [grad_x, sent_small], "rs_owners_wait_out")
    grad_w_out = _join_halves(_owner_sum(place, gout, rsib_out, rici_out, "rs_owner_sum_out"), "rs_join_halves_out")
    grad_w_out, delta_w_out, new_m_w_out, new_v_w_out = _adamw(
        w_out2, grad_w_out, m_w_out[0], v_w_out[0], "adamw_w_out")

    small_seen = _allgather8_wait(ssem_small, rsem_small, small, land_small, [delta_w_out], "gather_small_wait")
    total = _sum_devices(small_seen).reshape(-1)
    dmod_all = small_seen.reshape(N_DEV, -1)[:, 0:3 * d]
    loss = total[so[9]]
    grad_b_ada = total[0:3 * d].reshape(1, 3 * d)
    grad_g_pre = total[so[3]:so[3] + d].reshape(1, d)
    grad_conv_w_full = total[so[4]:so[4] + 3 * dc].reshape(3, dc)
    grad_conv_w = lax.dynamic_slice_in_dim(grad_conv_w_full, chip * cws, cws, axis=1).reshape(1, 3, cws)
    grad_conv_b = total[so[5]:so[5] + dc].reshape(1, dc)
    grad_g_conv = total[so[6]:so[6] + dc].reshape(1, dc)
    grad_g_attn = total[so[7]:so[7] + da].reshape(1, da)
    grad_g_post = total[so[8]:so[8] + d].reshape(1, d)

    dmod_cols = lax.dynamic_slice_in_dim(dmod_all, chip * wa, wa, axis=1)
    grad_w_ada, delta_w_ada, new_m_w_ada, new_v_w_ada = _ada_grad_adamw(c_all.T, dmod_cols, w_ada2, m_w_ada[0], v_w_ada[0])

    small_w = [b_ada, g_pre, conv_w, conv_b, g_conv, g_attn, g_post]
    small_g = [grad_b_ada, grad_g_pre, grad_conv_w, grad_conv_b, grad_g_conv, grad_g_attn, grad_g_post]
    small_m = [m_b_ada, m_g_pre, m_conv_w, m_conv_b, m_g_conv, m_g_attn, m_g_post]
    small_v = [v_b_ada, v_g_pre, v_conv_w, v_conv_b, v_g_conv, v_g_attn, v_g_post]
    pw, po = _pack_small(small_w)
    pg, _ = _pack_small(small_g)
    pm, _ = _pack_small(small_m)
    pv, _ = _pack_small(small_v)
    sd, sm, sv = (a.reshape(-1) for a in _adamw(pw, pg, pm, pv, "adamw_small")[1:])

    def unpack(flat):
        return [flat[o:o + w.size].reshape(w.shape) for o, w in zip(po, small_w)]

    d_small, m_small, v_small = unpack(sd), unpack(sm), unpack(sv)

    rici_in = _owners_wait(ssem_in, rsem_in, csum_in, land_in, [sd, delta_w_out, delta_w_ada], "rs_owners_wait_in")
    grad_w_in = _join_halves(_owner_sum(place, gin, rsib_in, rici_in, "rs_owner_sum_in"), "rs_join_halves_in")
    grad_w_in, delta_w_in, new_m_w_in, new_v_w_in = _adamw(w_in2, grad_w_in, m_w_in[0], v_w_in[0], "adamw_w_in")

    def lead(a):
        return a.reshape((1,) + a.shape)

    grads = [lead(grad_w_ada), grad_b_ada, grad_g_pre, lead(grad_w_in), grad_conv_w, grad_conv_b, grad_g_conv,
             grad_g_attn, lead(grad_w_out), grad_g_post]
    deltas = [lead(delta_w_ada), d_small[0], d_small[1], lead(delta_w_in), d_small[2], d_small[3], d_small[4],
              d_small[5], lead(delta_w_out), d_small[6]]
    new_ms = [lead(new_m_w_ada), m_small[0], m_small[1], lead(new_m_w_in), m_small[2], m_small[3], m_small[4],
              m_small[5], lead(new_m_w_out), m_small[6]]
    new_vs = [lead(new_v_w_ada), v_small[0], v_small[1], lead(new_v_w_in), v_small[2], v_small[3], v_small[4],
              v_small[5], lead(new_v_w_out), v_small[6]]
    return (loss, lead(grad_x), *grads, *deltas, *new_ms, *new_vs)
```

```python
import functools

import jax
import jax.numpy as jnp
from jax import lax
from jax.experimental import pallas as pl
from jax.experimental.pallas import tpu as pltpu

F32 = jnp.float32
BF16 = jnp.bfloat16
MESH = pl.DeviceIdType.MESH
HBM = pl.BlockSpec(memory_space=pltpu.HBM)
VMEM = pl.BlockSpec(memory_space=pltpu.VMEM)
ANY = pl.BlockSpec(memory_space=pl.ANY)
SEM = pl.BlockSpec(memory_space=pltpu.SEMAPHORE)
EFFECT = pltpu.SideEffectType.DATAFLOW_SIDE_EFFECTING
SUBLANES, LANES = 8, 128
TOKEN = jax.ShapeDtypeStruct((SUBLANES, LANES), jnp.float32)

HEAD_DIM = 64
PAIR = 2 * HEAD_DIM
assert PAIR == LANES
BRANCHES = ((128, 1), (512, 4), (2048, 16))
SIDE = 64
EPS = 1e-6
NEG_INF = -1e30
N_CHIPS = 4
N_DEV = 8

ADAM_LR = 0.001
ADAM_B1 = 0.9
ADAM_B2 = 0.999
ADAM_EPS = 1e-08
ADAM_WD = 0.01
ADAM_STEP = 10

VMEM_LIMIT_BYTES = 56 * 1024 * 1024
ROW_TILE = 256
COL_TILE = 512
CONV_TILE = 256
ATT_BQ = 128
ATT_KW = ATT_BQ + 2 * SIDE
ATT_UNROLL = 4
SMALL_ALIGN = SUBLANES * LANES


def _params(semantics=None):
    kw = {"vmem_limit_bytes": VMEM_LIMIT_BYTES}
    if semantics is not None:
        kw["dimension_semantics"] = semantics
    return pltpu.CompilerParams(**kw)


def _silu(z):
    return z * jax.nn.sigmoid(z)


def _silu_grad(z):
    s = jax.nn.sigmoid(z)
    return s * (1.0 + z * (1.0 - s))


def _my_place():
    return lax.axis_index("x"), lax.axis_index("y"), lax.axis_index("c")


def _flip(a, bit):
    return 1 - a if bit else a


def _chip_of(x, y):
    return 2 * x + y


def _allgather8(v, name, after=()):
    rows_per, n = v.shape

    def body(v_ref, *rest):
        out_ref, send_sems, recv_sems = rest[len(after):]
        x, y, c = _my_place()
        me = 4 * x + 2 * y + c

        def rows(idx):
            return out_ref.at[pl.ds(pl.multiple_of(idx * rows_per, rows_per), rows_per), :]

        out_ref[pl.ds(pl.multiple_of(me * rows_per, rows_per), rows_per), :] = v_ref[...]
        copies = []
        for k in range(1, N_DEV):
            peer = (_flip(x, k & 4), _flip(y, k & 2), _flip(c, k & 1))
            cp = pltpu.make_async_remote_copy(
                src_ref=v_ref, dst_ref=rows(me), send_sem=send_sems.at[k - 1], recv_sem=recv_sems.at[k - 1],
                device_id=peer, device_id_type=MESH)
            cp.start()
            copies.append((cp, peer))
        for k, (cp, peer) in enumerate(copies):
            src = 4 * peer[0] + 2 * peer[1] + peer[2]
            pltpu.make_async_remote_copy(
                src_ref=v_ref, dst_ref=rows(src), send_sem=send_sems.at[k], recv_sem=recv_sems.at[k],
                device_id=peer, device_id_type=MESH).wait_recv()
        for cp, _ in copies:
            cp.wait_send()

    return pl.pallas_call(
        body, name=name,
        out_shape=jax.ShapeDtypeStruct((N_DEV * rows_per, n), v.dtype),
        in_specs=[VMEM] + [ANY] * len(after), out_specs=VMEM,
        scratch_shapes=[pltpu.SemaphoreType.DMA((N_DEV - 1,)), pltpu.SemaphoreType.DMA((N_DEV - 1,))],
    )(v, *after)


def _allgather8_start(v, me, name):
    rows_per, n = v.shape
    land = lax.dynamic_update_slice(jnp.zeros((N_DEV * rows_per, n), v.dtype), v, (me * rows_per, 0))

    def body(v_ref, land_ref, send_sems, recv_sems, v_thru, land_thru, token_ref):
        del v_thru, land_thru
        x, y, c = _my_place()
        mine = land_ref.at[pl.ds(pl.multiple_of((4 * x + 2 * y + c) * rows_per, rows_per), rows_per), :]
        for k in range(1, N_DEV):
            peer = (_flip(x, k & 4), _flip(y, k & 2), _flip(c, k & 1))
            pltpu.make_async_remote_copy(
                src_ref=v_ref, dst_ref=mine, send_sem=send_sems.at[k - 1], recv_sem=recv_sems.at[k - 1],
                device_id=peer, device_id_type=MESH).start()
        token_ref[...] = jnp.zeros(token_ref.shape, F32)

    sems = pltpu.SemaphoreType.DMA((N_DEV - 1,))
    return pl.pallas_call(
        body, name=name,
        out_shape=(sems, sems, jax.ShapeDtypeStruct(v.shape, v.dtype), jax.ShapeDtypeStruct(land.shape, land.dtype), TOKEN),
        in_specs=[HBM, HBM], out_specs=(SEM, SEM, HBM, HBM, VMEM),
        input_output_aliases={0: 2, 1: 3},
        compiler_params=pltpu.CompilerParams(has_side_effects=EFFECT),
    )(pltpu.with_memory_space_constraint(v, pltpu.HBM), pltpu.with_memory_space_constraint(land, pltpu.HBM))


def _allgather8_wait(send_sems, recv_sems, v, land, after, name):
    rows_per = v.shape[0]

    def body(v_ref, land_ref, send_ref, recv_ref, *rest):
        del rest
        x, y, c = _my_place()
        for k in range(1, N_DEV):
            peer = (_flip(x, k & 4), _flip(y, k & 2), _flip(c, k & 1))
            src = 4 * peer[0] + 2 * peer[1] + peer[2]
            cp = pltpu.make_async_remote_copy(
                src_ref=v_ref, dst_ref=land_ref.at[pl.ds(pl.multiple_of(src * rows_per, rows_per), rows_per), :],
                send_sem=send_ref.at[k - 1], recv_sem=recv_ref.at[k - 1], device_id=peer, device_id_type=MESH)
            cp.wait_send()
            cp.wait_recv()

    return pl.pallas_call(
        body, name=name,
        out_shape=(jax.ShapeDtypeStruct(v.shape, v.dtype), jax.ShapeDtypeStruct(land.shape, land.dtype)),
        in_specs=[HBM, HBM, SEM, SEM] + [ANY] * len(after), out_specs=(HBM, HBM),
        input_output_aliases={0: 0, 1: 1},
        compiler_params=pltpu.CompilerParams(has_side_effects=EFFECT),
    )(v, land, send_sems, recv_sems, *after)[1]


def _half_rows(ref, chip, which, half):
    return ref.at[chip, pl.ds(pl.multiple_of(which * half, half), half), :]


def _ici_peers(x, y, c):
    peers = [(_flip(x, k & 2), _flip(y, k & 1), c) for k in (1, 2, 3)]
    return [(peer, _chip_of(peer[0], peer[1])) for peer in peers]


def _part_rows(ref, chip, core, part):
    quarter = ref.shape[1] // 4
    return ref.at[chip, pl.ds(pl.multiple_of((2 * core + part) * quarter, quarter), quarter), :]


def _neighbours(x, y, c):
    return [((x, 1 - y, c), _chip_of(x, 1 - y)), ((1 - x, y, c), _chip_of(1 - x, y)),
            ((1 - x, 1 - y, c), _chip_of(1 - x, 1 - y))]


def _start_direct(buf, send_sems, recv_sems):
    x, y, c = _my_place()
    me = _chip_of(x, y)
    for n, (peer, _) in enumerate(_neighbours(x, y, c)[0:2]):
        for part in ((0, 1), (1, 0))[n]:
            piece = _part_rows(buf, me, c, part)
            pltpu.make_async_remote_copy(
                src_ref=piece, dst_ref=piece, send_sem=send_sems.at[2 * n + part], recv_sem=recv_sems.at[2 * n + part],
                device_id=peer, device_id_type=MESH).start()


def _relay(buf, recv_sems, relay_send, relay_recv):
    x, y, c = _my_place()
    nbrs = _neighbours(x, y, c)
    for n in range(2):
        part = n
        piece = _part_rows(buf, nbrs[n][1], c, part)
        pltpu.make_async_remote_copy(
            src_ref=piece, dst_ref=piece, send_sem=relay_send.at[part], recv_sem=recv_sems.at[2 * n + part],
            device_id=nbrs[n][0], device_id_type=MESH).wait_recv()
        pltpu.make_async_remote_copy(
            src_ref=piece, dst_ref=piece, send_sem=relay_send.at[part], recv_sem=relay_recv.at[part],
            device_id=nbrs[1 - n][0], device_id_type=MESH).start()


def _gather_start(win_slots, after):
    def body(win_in, after_ref, win_ref, send_sems, recv_sems, token_ref):
        del win_in, after_ref
        _start_direct(win_ref, send_sems, recv_sems)
        token_ref[...] = jnp.zeros(token_ref.shape, F32)

    sems = pltpu.SemaphoreType.DMA((4,))
    return pl.pallas_call(
        body, name="gather_start",
        out_shape=(jax.ShapeDtypeStruct(win_slots.shape, win_slots.dtype), sems, sems, TOKEN),
        in_specs=[HBM, ANY], out_specs=(HBM, SEM, SEM, VMEM),
        input_output_aliases={0: 0},
        compiler_params=pltpu.CompilerParams(has_side_effects=EFFECT),
    )(win_slots, after)


def _gather_relay_in(win, wout_slots, recv_in, after):
    def body(win_in, wout_in, recv_in_ref, after_ref, win_ref, wout_ref, relay_send, relay_recv, send_out, recv_out):
        del win_in, wout_in, after_ref
        _relay(win_ref, recv_in_ref, relay_send, relay_recv)
        _start_direct(wout_ref, send_out, recv_out)

    two, four = pltpu.SemaphoreType.DMA((2,)), pltpu.SemaphoreType.DMA((4,))
    return pl.pallas_call(
        body, name="gather_relay_w_in",
        out_shape=(jax.ShapeDtypeStruct(win.shape, win.dtype), jax.ShapeDtypeStruct(wout_slots.shape, wout_slots.dtype),
                   two, two, four, four),
        in_specs=[HBM, HBM, SEM, ANY], out_specs=(HBM, HBM, SEM, SEM, SEM, SEM),
        input_output_aliases={0: 0, 1: 1},
        compiler_params=pltpu.CompilerParams(has_side_effects=EFFECT),
    )(win, wout_slots, recv_in, after)


def _gather_relay_out(wout, recv_out, after):
    def body(wout_in, recv_out_ref, after_ref, wout_ref, relay_send, relay_recv):
        del wout_in, after_ref
        _relay(wout_ref, recv_out_ref, relay_send, relay_recv)

    two = pltpu.SemaphoreType.DMA((2,))
    return pl.pallas_call(
        body, name="gather_relay_w_out",
        out_shape=(jax.ShapeDtypeStruct(wout.shape, wout.dtype), two, two),
        in_specs=[HBM, SEM, ANY], out_specs=(HBM, SEM, SEM),
        input_output_aliases={0: 0},
        compiler_params=pltpu.CompilerParams(has_side_effects=EFFECT),
    )(wout, recv_out, after)


def _gather_wait_direct(buf, send_sems, recv_sems, after, name):
    def body(buf_in, send_ref, recv_ref, after_ref, buf_ref):
        del buf_in, after_ref
        x, y, c = _my_place()
        me = _chip_of(x, y)
        for n, (peer, chip) in enumerate(_neighbours(x, y, c)[0:2]):
            second = 1 - n
            pltpu.make_async_remote_copy(
                src_ref=_part_rows(buf_ref, me, c, second), dst_ref=_part_rows(buf_ref, chip, c, second),
                send_sem=send_ref.at[2 * n + second], recv_sem=recv_ref.at[2 * n + second],
                device_id=peer, device_id_type=MESH).wait_recv()
            for part in range(2):
                piece = _part_rows(buf_ref, me, c, part)
                pltpu.make_async_remote_copy(
                    src_ref=piece, dst_ref=piece, send_sem=send_ref.at[2 * n + part], recv_sem=recv_ref.at[2 * n + part],
                    device_id=peer, device_id_type=MESH).wait_send()

    return pl.pallas_call(
        body, name=name,
        out_shape=jax.ShapeDtypeStruct(buf.shape, buf.dtype),
        in_specs=[HBM, SEM, SEM, ANY], out_specs=HBM,
        input_output_aliases={0: 0},
        compiler_params=pltpu.CompilerParams(has_side_effects=EFFECT),
    )(buf, send_sems, recv_sems, after)


def _gather_wait_relayed(buf, relay_send, relay_recv, after, name):
    def body(buf_in, rsend_ref, rrecv_ref, after_ref, buf_ref):
        del buf_in, after_ref
        x, y, c = _my_place()
        nbrs = _neighbours(x, y, c)
        for n in range(2):
            relayed = _part_rows(buf_ref, nbrs[n][1], c, n)
            cp = pltpu.make_async_remote_copy(
                src_ref=relayed, dst_ref=_part_rows(buf_ref, nbrs[2][1], c, n),
                send_sem=rsend_ref.at[n], recv_sem=rrecv_ref.at[n], device_id=nbrs[1 - n][0], device_id_type=MESH)
            cp.wait_recv()
            cp.wait_send()

    return pl.pallas_call(
        body, name=name,
        out_shape=jax.ShapeDtypeStruct(buf.shape, buf.dtype),
        in_specs=[HBM, SEM, SEM, ANY], out_specs=HBM,
        input_output_aliases={0: 0},
        compiler_params=pltpu.CompilerParams(has_side_effects=EFFECT),
    )(buf, relay_send, relay_recv, after)


def _forward_halves(buf, which, name):
    half = buf.shape[1] // 2

    def body(buf_in, buf_ref, send_sems, recv_sems):
        del buf_in
        x, y, c = _my_place()
        sibling = (x, y, 1 - c)
        chips = [_neighbours(x, y, c)[n][1] for n in which]
        started = []
        for k, src_chip in enumerate(chips):
            landed = _half_rows(buf_ref, src_chip, c, half)
            fw = pltpu.make_async_remote_copy(
                src_ref=landed, dst_ref=landed, send_sem=send_sems.at[k], recv_sem=recv_sems.at[k],
                device_id=sibling, device_id_type=MESH)
            fw.start()
            started.append(fw)
        for k, src_chip in enumerate(chips):
            other = _half_rows(buf_ref, src_chip, 1 - c, half)
            pltpu.make_async_remote_copy(
                src_ref=other, dst_ref=other, send_sem=send_sems.at[k], recv_sem=recv_sems.at[k],
                device_id=sibling, device_id_type=MESH).wait_recv()
        for fw in started:
            fw.wait_send()

    return pl.pallas_call(
        body, name=name,
        out_shape=jax.ShapeDtypeStruct(buf.shape, buf.dtype),
        in_specs=[HBM], out_specs=HBM,
        input_output_aliases={0: 0},
        scratch_shapes=[pltpu.SemaphoreType.DMA((len(which),))] * 2,
    )(buf)


def _dw_swapped(a, b, row_chunks, col_chunks, name):
    r, t = a.shape
    c_all = b.shape[1]
    chunks = row_chunks * col_chunks
    rq, cq = r // row_chunks, c_all // col_chunks
    half = rq // 2
    tn = COL_TILE
    nt = cq // tn
    steps = col_chunks * nt

    def body(a_ref, b_ref, mine_ref, sib_ref, stage, send_sems, recv_sems):
        x, y, c = _my_place()
        j, n = pl.program_id(0), pl.program_id(1)
        step = j * nt + n
        slot = step % 2
        res = jnp.dot(a_ref[...], b_ref[...], preferred_element_type=F32).astype(BF16)

        def landing(jj, nn):
            cols = pl.ds(pl.multiple_of(nn * tn, tn), tn)
            return sib_ref.at[:, :, cols] if col_chunks == 1 else sib_ref.at[pl.ds(jj, 1), :, cols]

        def copy(slot_, step_, jj, nn):
            return pltpu.make_async_remote_copy(
                src_ref=stage.at[slot_], dst_ref=landing(jj, nn), send_sem=send_sems.at[slot_],
                recv_sem=recv_sems.at[step_], device_id=(x, y, 1 - c), device_id_type=MESH)

        @pl.when(step >= 2)
        def _():
            copy(slot, step, j, n).wait_send()

        for q in range(row_chunks):
            lo = res[q * rq:q * rq + half, :]
            hi = res[q * rq + half:(q + 1) * rq, :]
            mine_ref[q] = jnp.where(c == 0, lo, hi)
            stage[slot, q] = jnp.where(c == 0, hi, lo)
        copy(slot, step, j, n).start()

        @pl.when(step == steps - 1)
        def _():
            for s in range(max(steps - 2, 0), steps):
                copy(s % 2, s, j, n).wait_send()
            for s in range(steps):
                copy(s % 2, s, j, n).wait_recv()

    shape = jax.ShapeDtypeStruct((chunks, half, cq), BF16)
    return pl.pallas_call(
        body, name=name, grid=(col_chunks, nt),
        out_shape=(shape, shape),
        in_specs=[pl.BlockSpec((r, t), lambda j, n: (0, 0)), pl.BlockSpec((t, tn), lambda j, n: (0, j * nt + n))],
        out_specs=(pl.BlockSpec((row_chunks, half, tn), lambda j, n: (j, 0, n)), ANY),
        scratch_shapes=[pltpu.VMEM((2, row_chunks, half, tn), BF16), pltpu.SemaphoreType.DMA((2,)),
                        pltpu.SemaphoreType.DMA((steps,))],
        compiler_params=_params(("arbitrary", "arbitrary")),
    )(a, b)


def _owners_start(csum, name, after=()):
    land = pltpu.with_memory_space_constraint(lax.empty((N_CHIPS - 1,) + csum.shape[1:], csum.dtype), pltpu.HBM)

    def body(csum_ref, land_ref, *rest):
        send_sems, recv_sems, _, _, token_ref = rest[len(after):]
        x, y, c = _my_place()
        for k, (peer, owner) in enumerate(_ici_peers(x, y, c)):
            pltpu.make_async_remote_copy(
                src_ref=csum_ref.at[owner], dst_ref=land_ref.at[k], send_sem=send_sems.at[k], recv_sem=recv_sems.at[k],
                device_id=peer, device_id_type=MESH).start()
        token_ref[...] = jnp.zeros(token_ref.shape, F32)

    sems = pltpu.SemaphoreType.DMA((N_CHIPS - 1,))
    return pl.pallas_call(
        body, name=name,
        out_shape=(sems, sems, jax.ShapeDtypeStruct(csum.shape, csum.dtype),
                   jax.ShapeDtypeStruct(land.shape, land.dtype), TOKEN),
        in_specs=[HBM, HBM] + [ANY] * len(after), out_specs=(SEM, SEM, HBM, HBM, VMEM),
        input_output_aliases={0: 2, 1: 3},
        compiler_params=pltpu.CompilerParams(has_side_effects=EFFECT),
    )(pltpu.with_memory_space_constraint(csum, pltpu.HBM), land, *after)


def _owners_wait(send_sems, recv_sems, csum, land, after, name):
    def body(csum_ref, land_ref, send_ref, recv_ref, *rest):
        del rest
        x, y, c = _my_place()
        for k, (peer, owner) in enumerate(_ici_peers(x, y, c)):
            cp = pltpu.make_async_remote_copy(
                src_ref=csum_ref.at[owner], dst_ref=land_ref.at[k], send_sem=send_ref.at[k], recv_sem=recv_ref.at[k],
                device_id=peer, device_id_type=MESH)
            cp.wait_send()
            cp.wait_recv()

    return pl.pallas_call(
        body, name=name,
        out_shape=(jax.ShapeDtypeStruct(csum.shape, csum.dtype), jax.ShapeDtypeStruct(land.shape, land.dtype)),
        in_specs=[HBM, HBM, SEM, SEM] + [ANY] * len(after), out_specs=(HBM, HBM),
        input_output_aliases={0: 0, 1: 1},
        compiler_params=pltpu.CompilerParams(has_side_effects=EFFECT),
    )(csum, land, send_sems, recv_sems, *after)[1]


def _join_halves(full, name, part=0, parts=1):
    half = full.shape[0] // 2
    rows = half // parts

    def body(full_in, full_ref, send_sem, recv_sem):
        del full_in
        x, y, c = _my_place()
        sibling = (x, y, 1 - c)
        mine = full_ref.at[pl.ds(pl.multiple_of(c * half + part * rows, rows), rows), :]
        theirs = full_ref.at[pl.ds(pl.multiple_of((1 - c) * half + part * rows, rows), rows), :]
        cp = pltpu.make_async_remote_copy(
            src_ref=mine, dst_ref=mine, send_sem=send_sem, recv_sem=recv_sem, device_id=sibling, device_id_type=MESH)
        cp.start()
        pltpu.make_async_remote_copy(
            src_ref=theirs, dst_ref=theirs, send_sem=send_sem, recv_sem=recv_sem,
            device_id=sibling, device_id_type=MESH).wait_recv()
        cp.wait_send()

    return pl.pallas_call(
        body, name=name,
        out_shape=jax.ShapeDtypeStruct(full.shape, full.dtype),
        in_specs=[HBM], out_specs=HBM,
        input_output_aliases={0: 0},
        scratch_shapes=[pltpu.SemaphoreType.DMA, pltpu.SemaphoreType.DMA],
    )(full)


def _cast_into_slot(place, w, name):
    rows, cols = w.shape
    tr = min(rows, ROW_TILE)

    def body(place_ref, w_ref, o_ref):
        del place_ref
        o_ref[...] = w_ref[...].astype(BF16)

    grid_spec = pltpu.PrefetchScalarGridSpec(
        num_scalar_prefetch=1, grid=(rows // tr,),
        in_specs=[pl.BlockSpec((tr, cols), lambda i, p: (i, 0))],
        out_specs=pl.BlockSpec((None, tr, cols), lambda i, p: (p[0], i, 0)))
    return pl.pallas_call(
        body, name=name, grid_spec=grid_spec,
        out_shape=jax.ShapeDtypeStruct((N_CHIPS, rows, cols), BF16),
        compiler_params=_params(("parallel",)),
    )(place, w)


def _ada_partial(c_all, w_ada):
    d_model, wa = w_ada.shape
    tn = 512 if wa % 512 == 0 else 256

    def body(c_ref, w_ref, o_ref):
        o_ref[...] = jnp.dot(_silu(c_ref[...]), w_ref[...], precision=lax.Precision.HIGHEST,
                             preferred_element_type=F32)

    return pl.pallas_call(
        body, name="ada_partial", grid=(wa // tn,),
        out_shape=jax.ShapeDtypeStruct((N_DEV, wa), F32),
        in_specs=[pl.BlockSpec((N_DEV, d_model), lambda i: (0, 0)), pl.BlockSpec((d_model, tn), lambda i: (0, i))],
        out_specs=pl.BlockSpec((N_DEV, tn), lambda i: (0, i)),
        compiler_params=_params(("parallel",)),
    )(c_all, w_ada)


def _prenorm(x, mod, g_pre):
    t, d = x.shape
    tb = ROW_TILE

    def body(x_ref, mod_ref, g_ref, h_ref, ht_ref):
        xv = x_ref[...]
        r = lax.rsqrt(jnp.mean(xv * xv, axis=-1, keepdims=True) + EPS)
        h = (xv * r) * g_ref[...] * (1.0 + mod_ref[1:2, :]) + mod_ref[0:1, :]
        h_ref[...] = h.astype(BF16)
        ht_ref[...] = h.T.astype(BF16)

    return pl.pallas_call(
        body, name="prenorm", grid=(t // tb,),
        out_shape=(jax.ShapeDtypeStruct((t, d), BF16), jax.ShapeDtypeStruct((d, t), BF16)),
        in_specs=[pl.BlockSpec((tb, d), lambda i: (i, 0)), pl.BlockSpec((3, d), lambda i: (0, 0)),
                  pl.BlockSpec((1, d), lambda i: (0, 0))],
        out_specs=(pl.BlockSpec((tb, d), lambda i: (i, 0)), pl.BlockSpec((d, tb), lambda i: (0, i))),
        compiler_params=_params(("parallel",)),
    )(x, mod, g_pre)


def _proj_chunks(proj, h, w, chunks, name):
    t, d = h.shape
    ws = w.shape[-1]
    tn = COL_TILE
    nt = ws // tn

    def body(chunk_ref, *refs):
        del chunk_ref
        a_ref, b_ref, o_ref = refs[-3:]
        o_ref[...] = jnp.dot(a_ref[...], b_ref[...].astype(BF16), preferred_element_type=F32).astype(BF16)

    if w.ndim == 3:
        w_spec = pl.BlockSpec((None, d, tn), lambda i, n, ch: (ch[i], 0, n))
    else:
        w_spec = pl.BlockSpec((d, tn), lambda i, n, ch: (0, n))
    first = proj is None
    grid_spec = pltpu.PrefetchScalarGridSpec(
        num_scalar_prefetch=1, grid=(chunks.shape[0], nt),
        in_specs=([] if first else [HBM]) + [pl.BlockSpec((t, d), lambda i, n, ch: (0, 0)), w_spec],
        out_specs=pl.BlockSpec((t, tn), lambda i, n, ch: (0, ch[i] * nt + n)))
    return pl.pallas_call(
        body, name=name, grid_spec=grid_spec,
        out_shape=jax.ShapeDtypeStruct((t, N_CHIPS * ws), BF16),
        input_output_aliases={} if first else {1: 0},
        compiler_params=_params(("parallel", "parallel")),
    )(*([chunks] if first else [chunks, proj]), h, w)


def _shift_rows(a, rows):
    idx = lax.broadcasted_iota(jnp.int32, a.shape, 0)
    prev = jnp.where(idx == 0, 0.0, pltpu.roll(a, 1, 0))
    nxt = jnp.where(idx == rows - 1, 0.0, pltpu.roll(a, rows - 1, 0))
    return prev, nxt


def _conv_fwd(conv_proj, conv_w, conv_b, dc):
    t = conv_proj.shape[0]
    ct = CONV_TILE
    nct = dc // ct

    def body(u_ref, cg_ref, w_ref, b_ref, co_ref):
        a = cg_ref[...].astype(F32) * u_ref[...].astype(F32)
        prev, nxt = _shift_rows(a, t)
        co_ref[...] = (w_ref[0:1, :] * prev + w_ref[1:2, :] * a + w_ref[2:3, :] * nxt + b_ref[...]).astype(BF16)

    return pl.pallas_call(
        body, name="conv_fwd", grid=(nct,),
        out_shape=jax.ShapeDtypeStruct((t, dc), BF16),
        in_specs=[pl.BlockSpec((t, ct), lambda i: (0, i)), pl.BlockSpec((t, ct), lambda i: (0, 2 * nct + i)),
                  pl.BlockSpec((3, ct), lambda i: (0, i)), pl.BlockSpec((1, ct), lambda i: (0, i))],
        out_specs=pl.BlockSpec((t, ct), lambda i: (0, i)),
        compiler_params=_params(("parallel",)),
    )(conv_proj, conv_proj, conv_w, conv_b)


def _to_residue_major(src_ref, dst_ref, r):
    seq = src_ref.shape[0] // r
    for res in range(r):
        dst_ref[res * seq:(res + 1) * seq, :] = src_ref[pl.ds(res, seq, stride=r), :].astype(dst_ref.dtype)


def _branch_operands(token_refs, stage, dil, r):
    if r == 1:
        return list(token_refs)
    for i, ref in enumerate(token_refs):
        stage[...] = ref[...].astype(F32)
        _to_residue_major(stage, dil.at[i], r)
    return [dil.at[i] for i in range(len(token_refs))]


def _scaled_queries(q):
    return (q.astype(F32) * (HEAD_DIM ** -0.5)).astype(BF16)


BLOCK_SHIFTS = (0, -SIDE, None)


def _band_bias(rel, slope):
    arel = jnp.abs(rel)
    return jnp.where(arel <= SIDE, arel.astype(F32) * slope, NEG_INF)


def _fill_bias_tiles(bias_ref, sl_ref, r, kw):
    base = lax.broadcasted_iota(jnp.int32, (ATT_BQ, kw), 1) - lax.broadcasted_iota(jnp.int32, (ATT_BQ, kw), 0)
    for hh in range(2):
        slope = -(sl_ref[hh:hh + 1, 0:kw] * float(r))
        for e, shift in enumerate(BLOCK_SHIFTS):
            shift = ATT_BQ - kw if shift is None else shift
            bias_ref[hh, e, :, 0:kw] = _band_bias(base + shift, slope)


def _fill_stacked_bias_tiles(bias_ref, sl_ref, r, kw):
    base = lax.broadcasted_iota(jnp.int32, (kw, ATT_BQ), 0) - lax.broadcasted_iota(jnp.int32, (kw, ATT_BQ), 1)
    for hh in range(2):
        slope = -(sl_ref[hh:hh + 1, 0:ATT_BQ] * float(r))
        for e, shift in enumerate(BLOCK_SHIFTS):
            shift = ATT_BQ - kw if shift is None else shift
            bias_ref[e, 0:kw, hh * ATT_BQ:(hh + 1) * ATT_BQ] = _band_bias(base + shift, slope)


def _first_head_lanes():
    return lax.broadcasted_iota(jnp.int32, (1, PAIR), 1) < HEAD_DIM


def _only_head(x, first, hh):
    return jnp.where(first if hh == 0 else jnp.logical_not(first), x, jnp.zeros_like(x))


def _block_place(g, seq_len, kw):
    nqb = seq_len // ATT_BQ
    if nqb == 1:
        row = pl.multiple_of(g * ATT_BQ, ATT_BQ)
        return row, row, 0
    res = g // nqb
    qb = g - res * nqb
    q0 = qb * ATT_BQ
    ks = jnp.clip(q0 - SIDE, 0, seq_len - kw)
    edge = jnp.where(qb == 0, 0, jnp.where(qb == nqb - 1, 2, 1))
    return (pl.multiple_of(res * seq_len + q0, ATT_BQ), pl.multiple_of(res * seq_len + ks, SIDE), edge)


def _qkv_specs(dc, da, t, index):
    return [pl.BlockSpec((t, PAIR), functools.partial(index, (4 * dc + comp * da) // PAIR)) for comp in range(3)]


def _attn_fwd(proj, slopes, dc, da):
    t = proj.shape[0]
    hp = da // PAIR
    n_blocks = t // ATT_BQ

    def body(q_ref, k_ref, v_ref, sl_ref, o_ref, lse_ref, stage, dil, bias, o_res, l_res, o_tok, l_tok):
        for b, (_, r) in enumerate(BRANCHES):
            seq_len = t // r
            kw = min(ATT_KW, seq_len)
            ops = _branch_operands([q_ref, k_ref, v_ref], stage, dil, r)
            _fill_bias_tiles(bias, sl_ref, r, kw)
            o_dst, l_dst = (o_tok.at[b], l_tok.at[b]) if r == 1 else (o_res, l_res)
            first = _first_head_lanes()

            def blocks(trip, carry, seq_len=seq_len, kw=kw, o_dst=o_dst, l_dst=l_dst, first=first, ops=ops):
                nt = (((1,), (1,)), ((), ()))
                places = [_block_place(trip * ATT_UNROLL + i, seq_len, kw) for i in range(ATT_UNROLL)]
                chains = [(i, hh) for i in range(ATT_UNROLL) for hh in range(2)]
                qs = [_scaled_queries(ops[0][pl.ds(qrow, ATT_BQ), :]) for qrow, _, _ in places]
                ks = [ops[1][pl.ds(krow, kw), :] for _, krow, _ in places]
                vs = [ops[2][pl.ds(krow, kw), :] for _, krow, _ in places]
                ss = [lax.dot_general(_only_head(qs[i], first, hh), ks[i], nt, preferred_element_type=F32)
                      + bias[hh, places[i][2], :, 0:kw] for i, hh in chains]
                tops = [jnp.max(s, axis=-1, keepdims=True) for s in ss]
                ps = [jnp.exp(s - m) for s, m in zip(ss, tops)]
                dens = [jnp.sum(p, axis=-1, keepdims=True) for p in ps]
                for i, (qrow, _, _) in enumerate(places):
                    weights = jnp.concatenate([ps[2 * i].astype(BF16), ps[2 * i + 1].astype(BF16)], axis=1)
                    values = jnp.concatenate([_only_head(vs[i], first, 0), _only_head(vs[i], first, 1)], axis=0)
                    den = jnp.where(first, dens[2 * i], dens[2 * i + 1])
                    o_dst[pl.ds(qrow, ATT_BQ), :] = jnp.dot(weights, values, preferred_element_type=F32) / den
                    l_dst[pl.ds(qrow, ATT_BQ), :] = jnp.where(first, tops[2 * i], tops[2 * i + 1]) + jnp.log(den)
                return carry

            lax.fori_loop(0, n_blocks // ATT_UNROLL, blocks, 0)
            if r > 1:
                for res in range(r):
                    rows = slice(res * seq_len, (res + 1) * seq_len)
                    o_tok[b, pl.ds(res, seq_len, stride=r), :] = o_res[rows, :]
                    l_tok[b, pl.ds(res, seq_len, stride=r), :] = l_res[rows, :]

        def merge(i, carry):
            rows = pl.ds(pl.multiple_of(i * ROW_TILE, ROW_TILE), ROW_TILE)
            la, lb, lc = l_tok[0, rows, :], l_tok[1, rows, :], l_tok[2, rows, :]
            m = jnp.maximum(jnp.maximum(la, lb), lc)
            wa, wb, wc = jnp.exp(la - m), jnp.exp(lb - m), jnp.exp(lc - m)
            den = wa + wb + wc
            o_ref[rows, :] = (wa * o_tok[0, rows, :] + wb * o_tok[1, rows, :] + wc * o_tok[2, rows, :]) * (1.0 / den)
            lse_ref[rows, :] = m + jnp.log(den)
            return carry

        lax.fori_loop(0, t // ROW_TILE, merge, 0)

    pair_spec = pl.BlockSpec((None, t, PAIR), lambda h: (h, 0, 0))
    return pl.pallas_call(
        body, name="attn_fwd", grid=(hp,),
        out_shape=(jax.ShapeDtypeStruct((hp, t, PAIR), F32), jax.ShapeDtypeStruct((hp, t, PAIR), F32)),
        in_specs=_qkv_specs(dc, da, t, lambda first, h: (0, first + h))
        + [pl.BlockSpec((None, 8, ATT_KW), lambda h: (h, 0, 0))],
        out_specs=(pair_spec, pair_spec),
        scratch_shapes=[pltpu.VMEM((t, PAIR), F32), pltpu.VMEM((3, t, PAIR), BF16),
                        pltpu.VMEM((2, 3, ATT_BQ, ATT_KW), F32),
                        pltpu.VMEM((t, PAIR), F32), pltpu.VMEM((t, PAIR), F32),
                        pltpu.VMEM((3, t, PAIR), F32), pltpu.VMEM((3, t, PAIR), F32)],
        compiler_params=_params(("parallel",)),
    )(proj, proj, proj, slopes)


def _attn_bwd(dproj, proj, d_o, lse, delta, slopes, dc, da, after):
    t = proj.shape[0]
    hp = da // PAIR
    n_blocks = t // ATT_BQ

    def all_branches(q_ref, k_ref, v_ref, do_ref, lse_ref, dl_ref, sl_ref,
                     stage, dil, packed, packed_res, row_vecs, bias_t, acc, tot):
        first = _first_head_lanes()
        lane = lax.broadcasted_iota(jnp.int32, (1, PAIR), 1)
        packed[...] = jnp.where((lane & (HEAD_DIM - 1)) < HEAD_DIM // 2, lse_ref[...], dl_ref[...])
        for b, (_, r) in enumerate(BRANCHES):
            seq_len = t // r
            kw = min(ATT_KW, seq_len)
            ops = _branch_operands([q_ref, k_ref, v_ref, do_ref], stage, dil, r)
            scalars = packed
            if r > 1:
                _to_residue_major(packed, packed_res, r)
                scalars = packed_res
            for g in range(n_blocks):
                flipped = scalars[g * ATT_BQ:(g + 1) * ATT_BQ, :].T
                for row in range(4):
                    row_vecs[g, row:row + 1, :] = flipped[row * (HEAD_DIM // 2):row * (HEAD_DIM // 2) + 1, :]
            _fill_stacked_bias_tiles(bias_t, sl_ref, r, kw)
            acc[1] = jnp.zeros((t, PAIR), F32)
            acc[2] = jnp.zeros((t, PAIR), F32)

            def blocks(trip, carry, seq_len=seq_len, kw=kw, ops=ops):
                nt = (((1,), (1,)), ((), ()))
                group = range(ATT_UNROLL)
                places = [_block_place(trip * ATT_UNROLL + i, seq_len, kw) for i in group]
                ks, vs, q2s, do2s, lse2s, dl2s = [], [], [], [], [], []
                for i, (qrow, krow, _) in zip(group, places):
                    q = _scaled_queries(ops[0][pl.ds(qrow, ATT_BQ), :])
                    dov = ops[3][pl.ds(qrow, ATT_BQ), :]
                    ks.append(ops[1][pl.ds(krow, kw), :])
                    vs.append(ops[2][pl.ds(krow, kw), :])
                    q2s.append(jnp.concatenate([_only_head(q, first, 0), _only_head(q, first, 1)], axis=0))
                    do2s.append(jnp.concatenate([_only_head(dov, first, 0), _only_head(dov, first, 1)], axis=0))
                    rows = row_vecs[trip * ATT_UNROLL + i]
                    lse2s.append(jnp.concatenate([rows[0:1, :], rows[2:3, :]], axis=1))
                    dl2s.append(jnp.concatenate([rows[1:2, :], rows[3:4, :]], axis=1))
                s_ts = [lax.dot_general(ks[i], q2s[i], nt, preferred_element_type=F32) for i in group]
                dp_ts = [lax.dot_general(vs[i], do2s[i], nt, preferred_element_type=F32) for i in group]
                p_ts = [jnp.exp(s_ts[i] + bias_t[places[i][2], 0:kw, :] - lse2s[i]) for i in group]
                ds_ts = [p_ts[i] * (dp_ts[i] - dl2s[i]) for i in group]
                dvs = [jnp.dot(p_ts[i].astype(BF16), do2s[i], preferred_element_type=F32) for i in group]
                dks = [jnp.dot(ds_ts[i].astype(BF16), q2s[i], preferred_element_type=F32) for i in group]
                dss = [ds_ts[i].T.astype(BF16) for i in group]
                dqs = [jnp.dot(dss[i][0:ATT_BQ, :], _only_head(ks[i], first, 0), preferred_element_type=F32)
                       + jnp.dot(dss[i][ATT_BQ:2 * ATT_BQ, :], _only_head(ks[i], first, 1), preferred_element_type=F32)
                       for i in group]
                for i, (qrow, krow, _) in zip(group, places):
                    acc[0, pl.ds(qrow, ATT_BQ), :] = dqs[i] * (HEAD_DIM ** -0.5)
                    acc[1, pl.ds(krow, kw), :] += dks[i]
                    acc[2, pl.ds(krow, kw), :] += dvs[i]
                return carry

            lax.fori_loop(0, n_blocks // ATT_UNROLL, blocks, 0)
            for comp in range(3):
                if r == 1:
                    tot[comp] = acc[comp]
                else:
                    for res in range(r):
                        tok = pl.ds(res, seq_len, stride=r)
                        tot[comp, tok, :] = tot[comp, tok, :] + acc[comp, res * seq_len:(res + 1) * seq_len, :]

    first_q = (4 * dc) // PAIR

    def body(dproj_in, q_ref, k_ref, v_ref, do_ref, lse_ref, dl_ref, sl_ref, after_ref, out_ref, *scratch):
        del dproj_in, after_ref
        work, out_stage, out_sems = scratch[:-2], scratch[-2], scratch[-1]
        h = pl.program_id(0)
        all_branches(q_ref, k_ref, v_ref, do_ref, lse_ref, dl_ref, sl_ref, *work)

        def out_copy(comp):
            cols = pl.ds(pl.multiple_of((first_q + comp * hp + h) * PAIR, PAIR), PAIR)
            return pltpu.make_async_copy(out_stage.at[comp], out_ref.at[:, cols], out_sems.at[comp])

        @pl.when(h > 0)
        def _():
            for comp in range(3):
                out_copy(comp).wait()

        for comp in range(3):
            out_stage[comp] = work[-1][comp].astype(BF16)
            out_copy(comp).start()

        @pl.when(h == hp - 1)
        def _():
            for comp in range(3):
                out_copy(comp).wait()

    pair_spec = pl.BlockSpec((None, t, PAIR), lambda h: (h, 0, 0))
    return pl.pallas_call(
        body, name="attn_bwd", grid=(hp,),
        out_shape=jax.ShapeDtypeStruct(dproj.shape, BF16),
        in_specs=[HBM] + _qkv_specs(dc, da, t, lambda first, h: (0, first + h))
        + [pair_spec, pair_spec, pair_spec, pl.BlockSpec((None, 8, ATT_KW), lambda h: (h, 0, 0)), ANY],
        out_specs=ANY,
        input_output_aliases={0: 0},
        scratch_shapes=[pltpu.VMEM((t, PAIR), F32), pltpu.VMEM((4, t, PAIR), BF16),
                        pltpu.VMEM((t, PAIR), F32), pltpu.VMEM((t, PAIR), F32),
                        pltpu.VMEM((n_blocks, 8, ATT_BQ), F32), pltpu.VMEM((3, ATT_KW, 2 * ATT_BQ), F32),
                        pltpu.VMEM((3, t, PAIR), F32), pltpu.VMEM((3, t, PAIR), F32),
                        pltpu.VMEM((3, t, PAIR), BF16), pltpu.SemaphoreType.DMA((3,))],
        compiler_params=_params(("arbitrary",)),
    )(dproj, proj, proj, proj, d_o, lse, delta, slopes, after)


def _mix_fwd(co, proj, o_mix, g_conv, g_attn_pairs):
    t, dc = co.shape
    hp = o_mix.shape[0]
    da = hp * PAIR
    tb = ROW_TILE

    def body(co_ref, bg_ref, zc_ref, za_ref, om_ref, gc_ref, ga_ref, ycat_ref, ycatt_ref):
        p = bg_ref[...].astype(F32) * co_ref[...].astype(F32)
        rc = lax.rsqrt(jnp.mean(p * p, axis=-1, keepdims=True) + EPS)
        yc = (p * rc) * gc_ref[...] * _silu(zc_ref[...].astype(F32))
        ycat_ref[:, 0:dc] = yc.astype(BF16)
        ycatt_ref[0:dc, :] = yc.T.astype(BF16)
        ssq = jnp.zeros((tb, 1), F32)
        for h in range(hp):
            o = om_ref[h]
            ssq = ssq + jnp.sum(o * o, axis=-1, keepdims=True)
        ra = lax.rsqrt(ssq * (1.0 / da) + EPS)
        for h in range(hp):
            ya = (om_ref[h] * ra) * ga_ref[h] * _silu(za_ref[:, h * PAIR:(h + 1) * PAIR].astype(F32))
            ycat_ref[:, dc + h * PAIR:dc + (h + 1) * PAIR] = ya.astype(BF16)
            ycatt_ref[dc + h * PAIR:dc + (h + 1) * PAIR, :] = ya.T.astype(BF16)

    pair_spec = pl.BlockSpec((hp, tb, PAIR), lambda i: (0, i, 0))
    return pl.pallas_call(
        body, name="mix_fwd", grid=(t // tb,),
        out_shape=(jax.ShapeDtypeStruct((t, dc + da), BF16), jax.ShapeDtypeStruct((dc + da, t), BF16)),
        in_specs=[pl.BlockSpec((tb, dc), lambda i: (i, 0)),
                  pl.BlockSpec((tb, dc), lambda i: (i, 1)),
                  pl.BlockSpec((tb, dc), lambda i: (i, 3)),
                  pl.BlockSpec((tb, da), lambda i: (i, 7)),
                  pair_spec,
                  pl.BlockSpec((1, dc), lambda i: (0, 0)),
                  pl.BlockSpec((hp, 1, PAIR), lambda i: (0, 0, 0))],
        out_specs=(pl.BlockSpec((tb, dc + da), lambda i: (i, 0)), pl.BlockSpec((dc + da, tb), lambda i: (0, i))),
        compiler_params=_params(("parallel",)),
    )(co, proj, proj, proj, o_mix, g_conv, g_attn_pairs)


def _out_fwd_bwd(ycat, woutf, x, target, mod, g_post):
    t, d = x.shape
    n = ycat.shape[1]
    tb = ROW_TILE

    def body(a_ref, w_ref, x_ref, tg_ref, mod_ref, g_ref, dout_ref, dy_ref, acc_ref):
        y = jnp.dot(a_ref[...], w_ref[...], preferred_element_type=F32)
        r = lax.rsqrt(jnp.mean(y * y, axis=-1, keepdims=True) + EPS)
        nh = y * r
        gate = mod_ref[2:3, :]
        nrm = nh * g_ref[...]
        err = x_ref[...] + gate * nrm - tg_ref[...]
        dout = err * (1.0 / d)
        dout_ref[...] = dout.astype(BF16)
        dn = dout * gate
        a = dn * g_ref[...]
        dy = r * (a - nh * jnp.mean(a * nh, axis=-1, keepdims=True))
        dy_ref[...] = dy.astype(BF16)
        loss = 0.5 * jnp.sum(jnp.sum(err * err, axis=-1, keepdims=True) * (1.0 / d), axis=0, keepdims=True)
        part = jnp.concatenate(
            [jnp.sum(dout * nrm, axis=0, keepdims=True), jnp.sum(dn * nh, axis=0, keepdims=True),
             jnp.broadcast_to(loss, (1, d)), jnp.zeros((5, d), F32)], axis=0)

        @pl.when(pl.program_id(0) == 0)
        def _():
            acc_ref[...] = jnp.zeros(acc_ref.shape, F32)

        acc_ref[...] += part

    return pl.pallas_call(
        body, name="out_fwd_bwd", grid=(t // tb,),
        out_shape=(jax.ShapeDtypeStruct((t, d), BF16), jax.ShapeDtypeStruct((t, d), BF16),
                   jax.ShapeDtypeStruct((8, d), F32)),
        in_specs=[pl.BlockSpec((tb, n), lambda i: (i, 0)), pl.BlockSpec((n, d), lambda i: (0, 0)),
                  pl.BlockSpec((tb, d), lambda i: (i, 0)), pl.BlockSpec((tb, d), lambda i: (i, 0)),
                  pl.BlockSpec((3, d), lambda i: (0, 0)), pl.BlockSpec((1, d), lambda i: (0, 0))],
        out_specs=(pl.BlockSpec((tb, d), lambda i: (i, 0)), pl.BlockSpec((tb, d), lambda i: (i, 0)),
                   pl.BlockSpec((8, d), lambda i: (0, 0))),
        compiler_params=_params(("arbitrary",)),
    )(ycat, woutf, x, target, mod, g_post)


def _matmul_nt(a, b, out_dtype, name):
    m, k = a.shape
    n = b.shape[0]
    tn = COL_TILE

    def body(a_ref, b_ref, o_ref):
        o_ref[...] = lax.dot_general(a_ref[...], b_ref[...], (((1,), (1,)), ((), ())),
                                     preferred_element_type=F32).astype(out_dtype)

    return pl.pallas_call(
        body, name=name, grid=(n // tn,),
        out_shape=jax.ShapeDtypeStruct((m, n), out_dtype),
        in_specs=[pl.BlockSpec((m, k), lambda i: (0, 0)), pl.BlockSpec((tn, k), lambda i: (i, 0))],
        out_specs=pl.BlockSpec((m, tn), lambda i: (0, i)),
        compiler_params=_params(("parallel",)),
    )(a, b)


def _mix_bwd(dycat, co, proj, o_mix, g_conv, g_attn_pairs):
    t, dc = co.shape
    hp = o_mix.shape[0]
    da = hp * PAIR
    tb = ROW_TILE

    def body(dy_ref, co_ref, bg_ref, zc_ref, za_ref, om_ref, gc_ref, ga_ref,
             dcp_ref, dco_ref, do_ref, dl_ref, dgc_ref, dga_ref):
        first = pl.program_id(0) == 0
        cov = co_ref[...].astype(F32)
        bg = bg_ref[...].astype(F32)
        zc = zc_ref[...].astype(F32)
        p = bg * cov
        rc = lax.rsqrt(jnp.mean(p * p, axis=-1, keepdims=True) + EPS)
        nh = p * rc
        dyc = dy_ref[:, 0:dc].astype(F32)
        dn = dyc * _silu(zc)
        a = dn * gc_ref[...]
        dp = rc * (a - nh * jnp.mean(a * nh, axis=-1, keepdims=True))
        dcp_ref[:, 0:dc] = jnp.zeros((tb, dc), BF16)
        dcp_ref[:, dc:2 * dc] = (dp * cov).astype(BF16)
        dcp_ref[:, 2 * dc:3 * dc] = jnp.zeros((tb, dc), BF16)
        dcp_ref[:, 3 * dc:4 * dc] = (dyc * nh * gc_ref[...] * _silu_grad(zc)).astype(BF16)
        dcp_ref[:, 4 * dc:4 * dc + 3 * da] = jnp.zeros((tb, 3 * da), BF16)
        dco_ref[...] = dp * bg

        @pl.when(first)
        def _():
            dgc_ref[...] = jnp.zeros(dgc_ref.shape, F32)
            dga_ref[...] = jnp.zeros(dga_ref.shape, F32)

        dgc_ref[...] += jnp.sum(dn * nh, axis=0, keepdims=True)

        ssq = jnp.zeros((tb, 1), F32)
        for h in range(hp):
            o = om_ref[h]
            ssq = ssq + jnp.sum(o * o, axis=-1, keepdims=True)
        ra = lax.rsqrt(ssq * (1.0 / da) + EPS)
        dot_an = jnp.zeros((tb, 1), F32)
        for h in range(hp):
            nha = om_ref[h] * ra
            za = za_ref[:, h * PAIR:(h + 1) * PAIR].astype(F32)
            dya = dy_ref[:, dc + h * PAIR:dc + (h + 1) * PAIR].astype(F32)
            dna = dya * _silu(za)
            dza = (dya * nha * ga_ref[h] * _silu_grad(za)).astype(BF16)
            dcp_ref[:, 4 * dc + 3 * da + h * PAIR:4 * dc + 3 * da + (h + 1) * PAIR] = dza
            dga_ref[h] += jnp.sum(dna * nha, axis=0, keepdims=True)
            dot_an = dot_an + jnp.sum(dna * ga_ref[h] * nha, axis=-1, keepdims=True)
        mean_an = dot_an * (1.0 / da)
        first_head = lax.broadcasted_iota(jnp.int32, (tb, PAIR), 1) < HEAD_DIM
        for h in range(hp):
            o = om_ref[h]
            nha = o * ra
            za = za_ref[:, h * PAIR:(h + 1) * PAIR].astype(F32)
            dya = dy_ref[:, dc + h * PAIR:dc + (h + 1) * PAIR].astype(F32)
            aa = dya * _silu(za) * ga_ref[h]
            d_o = ra * (aa - nha * mean_an)
            do_ref[h] = d_o.astype(BF16)
            prod = d_o * o
            both = jnp.sum(prod, axis=-1, keepdims=True)
            head0 = jnp.sum(jnp.where(first_head, prod, 0.0), axis=-1, keepdims=True)
            dl_ref[h] = jnp.where(first_head, head0, both - head0)

    pair_spec = pl.BlockSpec((hp, tb, PAIR), lambda i: (0, i, 0))
    return pl.pallas_call(
        body, name="mix_bwd", grid=(t // tb,),
        out_shape=(jax.ShapeDtypeStruct((t, 4 * dc + 4 * da), BF16), jax.ShapeDtypeStruct((t, dc), F32),
                   jax.ShapeDtypeStruct((hp, t, PAIR), BF16), jax.ShapeDtypeStruct((hp, t, PAIR), F32),
                   jax.ShapeDtypeStruct((1, dc), F32), jax.ShapeDtypeStruct((hp, 1, PAIR), F32)),
        in_specs=[pl.BlockSpec((tb, dc + da), lambda i: (i, 0)),
                  pl.BlockSpec((tb, dc), lambda i: (i, 0)),
                  pl.BlockSpec((tb, dc), lambda i: (i, 1)),
                  pl.BlockSpec((tb, dc), lambda i: (i, 3)),
                  pl.BlockSpec((tb, da), lambda i: (i, 7)),
                  pair_spec,
                  pl.BlockSpec((1, dc), lambda i: (0, 0)),
                  pl.BlockSpec((hp, 1, PAIR), lambda i: (0, 0, 0))],
        out_specs=(pl.BlockSpec((tb, 4 * dc + 4 * da), lambda i: (i, 0)), pl.BlockSpec((tb, dc), lambda i: (i, 0)),
                   pair_spec, pair_spec,
                   pl.BlockSpec((1, dc), lambda i: (0, 0)), pl.BlockSpec((hp, 1, PAIR), lambda i: (0, 0, 0))),
        compiler_params=_params(("arbitrary",)),
    )(dycat, co, proj, proj, proj, o_mix, g_conv, g_attn_pairs)


def _conv_bwd(dconv_proj, dco, conv_proj, conv_w, dc, after):
    t = dco.shape[0]
    ct = CONV_TILE
    nct = dc // ct

    def body(dcp_in_ref, dco_ref, u_ref, cg_ref, w_ref, after_ref, dcp_ref, acc_ref):
        del dcp_in_ref, after_ref
        which = pl.program_id(1)
        g = dco_ref[...]
        u = u_ref[...].astype(F32)
        cg = cg_ref[...].astype(F32)
        g_prev, g_next = _shift_rows(g, t)
        da = w_ref[0:1, :] * g_next + w_ref[1:2, :] * g + w_ref[2:3, :] * g_prev
        dcp_ref[...] = (da * jnp.where(which == 0, cg, u)).astype(BF16)
        a = cg * u
        a_prev, a_next = _shift_rows(a, t)
        acc_ref[...] = jnp.concatenate(
            [jnp.sum(g * a_prev, axis=0, keepdims=True), jnp.sum(g * a, axis=0, keepdims=True),
             jnp.sum(g * a_next, axis=0, keepdims=True), jnp.sum(g, axis=0, keepdims=True),
             jnp.zeros((4, ct), F32)], axis=0)

    return pl.pallas_call(
        body, name="conv_bwd", grid=(nct, 2),
        out_shape=(jax.ShapeDtypeStruct(dconv_proj.shape, BF16), jax.ShapeDtypeStruct((8, dc), F32)),
        in_specs=[HBM,
                  pl.BlockSpec((t, ct), lambda i, s: (0, i)),
                  pl.BlockSpec((t, ct), lambda i, s: (0, i)),
                  pl.BlockSpec((t, ct), lambda i, s: (0, 2 * nct + i)),
                  pl.BlockSpec((3, ct), lambda i, s: (0, i)), ANY],
        out_specs=(pl.BlockSpec((t, ct), lambda i, s: (0, 2 * s * nct + i)),
                   pl.BlockSpec((8, ct), lambda i, s: (0, i))),
        input_output_aliases={0: 0},
        compiler_params=_params(("arbitrary", "arbitrary")),
    )(dconv_proj, dco, conv_proj, conv_proj, conv_w, after)


def _dh(dproj, winf, after):
    t = dproj.shape[0]
    _, d, ws = winf.shape
    tm = tn = COL_TILE
    nt = (((1,), (1,)), ((), ()))

    def body(a_ref, w_ref, after_ref, o_ref):
        del after_ref
        acc = lax.dot_general(a_ref[:, 0:ws], w_ref[0], nt, preferred_element_type=F32)
        for j in range(1, N_CHIPS):
            acc = acc + lax.dot_general(a_ref[:, j * ws:(j + 1) * ws], w_ref[j], nt, preferred_element_type=F32)
        o_ref[...] = acc.astype(BF16)

    return pl.pallas_call(
        body, name="dh", grid=(d // tn, t // tm),
        out_shape=jax.ShapeDtypeStruct((t, d), BF16),
        in_specs=[pl.BlockSpec((tm, N_CHIPS * ws), lambda n, m: (m, 0)),
                  pl.BlockSpec((N_CHIPS, tn, ws), lambda n, m: (0, n, 0)), ANY],
        out_specs=pl.BlockSpec((tm, tn), lambda n, m: (m, n)),
        compiler_params=_params(("parallel", "parallel")),
    )(dproj, winf, after)


def _prenorm_bwd(x, dh, dout, mod, g_pre):
    t, d = x.shape
    tb = ROW_TILE

    def body(x_ref, dh_ref, dout_ref, mod_ref, g_ref, gx_ref, acc_ref):
        xv = x_ref[...]
        dhv = dh_ref[...].astype(F32)
        r = lax.rsqrt(jnp.mean(xv * xv, axis=-1, keepdims=True) + EPS)
        xh = xv * r
        one_scale = 1.0 + mod_ref[1:2, :]
        a = dhv * one_scale * g_ref[...]
        gx_ref[...] = dout_ref[...].astype(F32) + r * (a - xh * jnp.mean(a * xh, axis=-1, keepdims=True))
        part = jnp.concatenate(
            [jnp.sum(dhv, axis=0, keepdims=True), jnp.sum(dhv * xh * g_ref[...], axis=0, keepdims=True),
             jnp.sum(dhv * xh * one_scale, axis=0, keepdims=True), jnp.zeros((5, d), F32)], axis=0)

        @pl.when(pl.program_id(0) == 0)
        def _():
            acc_ref[...] = jnp.zeros(acc_ref.shape, F32)

        acc_ref[...] += part

    return pl.pallas_call(
        body, name="prenorm_bwd", grid=(t // tb,),
        out_shape=(jax.ShapeDtypeStruct((t, d), F32), jax.ShapeDtypeStruct((8, d), F32)),
        in_specs=[pl.BlockSpec((tb, d), lambda i: (i, 0)), pl.BlockSpec((tb, d), lambda i: (i, 0)),
                  pl.BlockSpec((tb, d), lambda i: (i, 0)), pl.BlockSpec((3, d), lambda i: (0, 0)),
                  pl.BlockSpec((1, d), lambda i: (0, 0))],
        out_specs=(pl.BlockSpec((tb, d), lambda i: (i, 0)), pl.BlockSpec((8, d), lambda i: (0, 0))),
        compiler_params=_params(("arbitrary",)),
    )(x, dh, dout, mod, g_pre)


def _chip_sums(mine, rsib, name, part=0, parts=1, after=()):
    _, half, cols = mine.shape
    rows = half // parts
    tr = min(rows, ROW_TILE)
    nt = rows // tr

    def body(g_ref, r_ref, *rest):
        rest[-1][...] = (g_ref[...].astype(F32) + r_ref[...].astype(F32)).astype(BF16)

    spec = pl.BlockSpec((None, tr, cols), lambda j, i: (j, part * nt + i, 0))
    return pl.pallas_call(
        body, name=name, grid=(N_CHIPS, nt),
        out_shape=jax.ShapeDtypeStruct((N_CHIPS, rows, cols), BF16),
        in_specs=[spec, spec] + [ANY] * len(after), out_specs=pl.BlockSpec((None, tr, cols), lambda j, i: (j, i, 0)),
        compiler_params=_params(("parallel", "parallel")),
    )(mine, rsib, *after)


def _owner_sum(place, mine, rsib, rici, name, part=0, parts=1, full=None):
    _, half, cols = mine.shape
    rows = half // parts
    tr = min(rows, ROW_TILE)
    nt = rows // tr

    def body(place_ref, *refs):
        del place_ref
        g_ref, r_ref, i_ref, o_ref = refs[-4:]
        acc = g_ref[...].astype(F32) + r_ref[...].astype(F32)
        for k in range(N_CHIPS - 1):
            acc = acc + i_ref[k].astype(F32)
        o_ref[...] = acc

    own = pl.BlockSpec((None, tr, cols), lambda i, p: (p[0], part * nt + i, 0))
    grid_spec = pltpu.PrefetchScalarGridSpec(
        num_scalar_prefetch=1, grid=(nt,),
        in_specs=([] if full is None else [HBM]) + [own, own, pl.BlockSpec((N_CHIPS - 1, tr, cols), lambda i, p: (0, i, 0))],
        out_specs=pl.BlockSpec((tr, cols), lambda i, p: (p[1] * (half // tr) + part * nt + i, 0)))
    return pl.pallas_call(
        body, name=name, grid_spec=grid_spec,
        out_shape=jax.ShapeDtypeStruct((2 * half, cols), F32),
        input_output_aliases={} if full is None else {1: 0},
        compiler_params=_params(("parallel",)),
    )(*([place] if full is None else [place, full]), mine, rsib, rici)


def _adam_math(w, g, m, v):
    m2 = ADAM_B1 * m + (1.0 - ADAM_B1) * g
    v2 = ADAM_B2 * v + (1.0 - ADAM_B2) * (g * g)
    m_hat = m2 / (1.0 - ADAM_B1 ** ADAM_STEP)
    v_hat = v2 / (1.0 - ADAM_B2 ** ADAM_STEP)
    delta = -ADAM_LR * (m_hat / (jnp.sqrt(v_hat) + ADAM_EPS) + ADAM_WD * w)
    return delta, m2, v2


def _adamw(w, g, m, v, name, part=0, parts=1, prev=None):
    rows, cols = w.shape
    tr = min(rows, ROW_TILE)

    def body(*refs):
        w_ref, g_ref, m_ref, v_ref, go_ref, d_ref, m2_ref, v2_ref = refs[-8:]
        g = g_ref[...]
        go_ref[...] = g
        d_ref[...], m2_ref[...], v2_ref[...] = _adam_math(w_ref[...], g, m_ref[...], v_ref[...])

    if parts == 1:
        grid, spec = (rows // tr,), pl.BlockSpec((tr, cols), lambda i: (i, 0))
    else:
        per_half = rows // 2 // tr
        nt = per_half // parts
        grid, spec = (2, nt), pl.BlockSpec((tr, cols), lambda r, i: (r * per_half + part * nt + i, 0))
    olds = [] if prev is None else list(prev)
    return pl.pallas_call(
        body, name=name, grid=grid,
        out_shape=(jax.ShapeDtypeStruct(w.shape, F32),) * 4,
        in_specs=[HBM] * len(olds) + [spec] * 4, out_specs=(spec,) * 4,
        input_output_aliases={i: i for i in range(len(olds))},
        compiler_params=_params(("parallel",) * len(grid)),
    )(*olds, w, g, m, v)


def _ada_grad_adamw(c_all_t, dmod_cols, w, m, v):
    d, wa = w.shape
    tr = ROW_TILE

    def body(ct_ref, dm_ref, w_ref, m_ref, v_ref, g_ref, d_ref, m2_ref, v2_ref):
        act = _silu(ct_ref[...])
        g = act[:, 0:1] * dm_ref[0:1, :]
        for b in range(1, N_DEV):
            g = g + act[:, b:b + 1] * dm_ref[b:b + 1, :]
        g_ref[...] = g
        d_ref[...], m2_ref[...], v2_ref[...] = _adam_math(w_ref[...], g, m_ref[...], v_ref[...])

    spec = pl.BlockSpec((tr, wa), lambda i: (i, 0))
    return pl.pallas_call(
        body, name="ada_grad_adamw", grid=(d // tr,),
        out_shape=(jax.ShapeDtypeStruct(w.shape, F32),) * 4,
        in_specs=[pl.BlockSpec((tr, N_DEV), lambda i: (i, 0)), pl.BlockSpec((N_DEV, wa), lambda i: (0, 0)),
                  spec, spec, spec],
        out_specs=(spec,) * 4,
        compiler_params=_params(("parallel",)),
    )(c_all_t, dmod_cols, w, m, v)


def _sum_devices(gathered):
    n = gathered.shape[1]

    def body(g_ref, o_ref):
        acc = g_ref[0:8, :]
        for dev in range(1, N_DEV):
            acc = acc + g_ref[8 * dev:8 * dev + 8, :]
        o_ref[...] = acc

    return pl.pallas_call(
        body, name="sum_devices",
        out_shape=jax.ShapeDtypeStruct((8, n), F32),
        in_specs=[VMEM], out_specs=VMEM,
    )(gathered)


def _pack_small(pieces):
    flat = [p.reshape(-1).astype(F32) for p in pieces]
    offsets, total = [], 0
    for p in flat:
        offsets.append(total)
        total += p.shape[0]
    padded = -(-total // SMALL_ALIGN) * SMALL_ALIGN
    if padded > total:
        flat.append(jnp.zeros((padded - total,), F32))
    return jnp.concatenate(flat).reshape(8, padded // 8), offsets


def _alibi_slope_rows(n_heads):
    slopes = 2.0 ** (-8.0 * jnp.arange(1, n_heads + 1, dtype=F32) / n_heads)
    rows = jnp.zeros((n_heads // 2, 8), F32).at[:, 0:2].set(slopes.reshape(n_heads // 2, 2))
    return jnp.broadcast_to(rows[:, :, None], (n_heads // 2, 8, ATT_KW))


def kernel(x, c, w_ada, b_ada, g_pre, w_in, conv_w, conv_b, g_conv, g_attn, w_out, g_post, loss_target, m_w_ada, m_b_ada, m_g_pre, m_w_in, m_conv_w, m_conv_b, m_g_conv, m_g_attn, m_w_out, m_g_post, v_w_ada, v_b_ada, v_g_pre, v_w_in, v_conv_w, v_conv_b, v_g_conv, v_g_attn, v_w_out, v_g_post):
    t, d = x.shape[1], x.shape[2]
    dc = conv_b.shape[1]
    da = g_attn.shape[1]
    hp = da // PAIR
    ws = w_in.shape[2]
    wa = w_ada.shape[2]
    cws = conv_w.shape[2]
    assert t % ROW_TILE == 0 and d % ROW_TILE == 0 and dc % COL_TILE == 0 and da % COL_TILE == 0
    assert ws == 2 * dc and dc == da and t // BRANCHES[-1][1] >= ATT_BQ

    mx, my, mc = _my_place()
    chip = _chip_of(mx, my)
    dev = 2 * chip + mc
    place = jnp.stack([chip, mc]).astype(jnp.int32)

    x2, tgt2 = x[0], loss_target[0]
    w_ada2, w_in2, w_out2 = w_ada[0], w_in[0], w_out[0]

    win_slots = _cast_into_slot(place, w_in2, "cast_w_in")
    packed, offs = _pack_small([c[0], conv_w[0]])
    seen = _allgather8(packed, "gather_inputs", after=(win_slots,)).reshape(N_DEV, -1)
    c_all = seen[:, offs[0]:offs[0] + d]
    conv_w_full = seen[0::2, offs[1]:offs[1] + 3 * cws].reshape(N_CHIPS, 3, cws).transpose(1, 0, 2).reshape(3, dc)

    ada_part = _ada_partial(c_all, w_ada2)
    ada_seen = _allgather8(ada_part, "gather_ada").reshape(N_DEV, N_DEV, wa)
    mod_flat = lax.dynamic_index_in_dim(ada_seen[0::2], dev, axis=1, keepdims=False).reshape(1, 3 * d) + b_ada
    mod = mod_flat.reshape(3, d)

    win_flight, send_in, recv_in, started = _gather_start(win_slots, mod)

    y_chip, x_chip, d_chip = (_chip_of(mx, 1 - my), _chip_of(1 - mx, my), _chip_of(1 - mx, 1 - my))
    own_chunk, near_chunks, far_chunk = (jnp.stack(js).astype(jnp.int32) for js in ([chip], [y_chip, x_chip], [d_chip]))
    h, ht = _prenorm(x2, mod + started[0, 0], g_pre)
    proj = _proj_chunks(None, h, w_in2, own_chunk, "proj_own")
    win_flight, wout_flight, relay_send_in, relay_recv_in, send_out, recv_out = _gather_relay_in(
        win_flight, _cast_into_slot(place, w_out2, "cast_w_out"), recv_in, proj)
    win_flight = _forward_halves(
        _gather_wait_direct(win_flight, send_in, recv_in, proj, "gather_wait_w_in_direct"), (0, 1), "forward_w_in_direct")
    proj = _proj_chunks(proj, h, win_flight, near_chunks, "proj_neighbours")
    winf = _forward_halves(
        _gather_wait_relayed(win_flight, relay_send_in, relay_recv_in, proj, "gather_wait_w_in_relayed"),
        (2,), "forward_w_in_relayed")
    proj = _proj_chunks(proj, h, winf, far_chunk, "proj_diagonal")
    slopes = _alibi_slope_rows(da // HEAD_DIM)
    co = _conv_fwd(proj, conv_w_full, conv_b, dc)
    wout_flight, relay_send_out, relay_recv_out = _gather_relay_out(wout_flight, recv_out, co)
    o_mix, lse = _attn_fwd(proj, slopes, dc, da)
    g_attn_pairs = g_attn.reshape(hp, 1, PAIR)
    ycat, ycat_t = _mix_fwd(co, proj, o_mix, g_conv, g_attn_pairs)
    wout_flight = _gather_wait_direct(wout_flight, send_out, recv_out, ycat, "gather_wait_w_out_direct")
    wout_flight = _gather_wait_relayed(wout_flight, relay_send_out, relay_recv_out, ycat, "gather_wait_w_out_relayed")
    woutf = _forward_halves(wout_flight, (0, 1, 2), "forward_w_out").reshape(dc + da, d)
    dout, dy, post_sums = _out_fwd_bwd(ycat, woutf, x2, tgt2, mod, g_post)

    gout, rsib_out = _dw_swapped(ycat_t, dy, N_CHIPS, 1, "dw_out")
    csum_out = _chip_sums(gout, rsib_out, "rs_chip_sum_out")
    ssem_out, rsem_out, csum_out, land_out, sent_out = _owners_start(csum_out, "rs_owners_start_out")
    dycat = _matmul_nt(dy, woutf, BF16, "dycat")
    dproj, dco, d_o, delta, dg_conv, dg_attn = _mix_bwd(dycat, co, proj, o_mix, g_conv, g_attn_pairs)
    dproj, conv_sums = _conv_bwd(dproj, dco, proj, conv_w_full, dc, sent_out)
    dproj = _attn_bwd(dproj, proj, d_o, lse, delta, slopes, dc, da, sent_out)
    gin, rsib_in = _dw_swapped(ht, dproj, 1, N_CHIPS, "dw_in")
    ssem_in0, rsem_in0, csum_in0, land_in0, sent_in = _owners_start(
        _chip_sums(gin, rsib_in, "rs_chip_sum_in0", 0, 2), "rs_owners_start_in0")
    dh = _dh(dproj, winf, sent_in)
    grad_x, pre_sums = _prenorm_bwd(x2, dh, dout, mod, g_pre)

    small, so = _pack_small([
        pre_sums[0], pre_sums[1], post_sums[0],
        pre_sums[2], conv_sums[0:3], conv_sums[3], dg_conv, dg_attn, post_sums[1], post_sums[2, 0:128]])
    ssem_small, rsem_small, small, land_small, sent_small = _allgather8_start(small, dev, "gather_small_start")
    ssem_in1, rsem_in1, csum_in1, land_in1, sent_in1 = _owners_start(
        _chip_sums(gin, rsib_in, "rs_chip_sum_in1", 1, 2, after=(sent_small,)), "rs_owners_start_in1")

    rici_out = _owners_wait(ssem_out, rsem_out, csum_out, land_out, [grad_x, sent_in1], "rs_owners_wait_out")
    grad_w_out = _join_halves(_owner_sum(place, gout, rsib_out, rici_out, "rs_owner_sum_out"), "rs_join_halves_out")
    grad_w_out, delta_w_out, new_m_w_out, new_v_w_out = _adamw(
        w_out2, grad_w_out, m_w_out[0], v_w_out[0], "adamw_w_out")

    small_seen = _allgather8_wait(ssem_small, rsem_small, small, land_small, [delta_w_out], "gather_small_wait")
    total = _sum_devices(small_seen).reshape(-1)
    dmod_all = small_seen.reshape(N_DEV, -1)[:, 0:3 * d]
    loss = total[so[9]]
    grad_b_ada = total[0:3 * d].reshape(1, 3 * d)
    grad_g_pre = total[so[3]:so[3] + d].reshape(1, d)
    grad_conv_w_full = total[so[4]:so[4] + 3 * dc].reshape(3, dc)
    grad_conv_w = lax.dynamic_slice_in_dim(grad_conv_w_full, chip * cws, cws, axis=1).reshape(1, 3, cws)
    grad_conv_b = total[so[5]:so[5] + dc].reshape(1, dc)
    grad_g_conv = total[so[6]:so[6] + dc].reshape(1, dc)
    grad_g_attn = total[so[7]:so[7] + da].reshape(1, da)
    grad_g_post = total[so[8]:so[8] + d].reshape(1, d)

    dmod_cols = lax.dynamic_slice_in_dim(dmod_all, chip * wa, wa, axis=1)
    grad_w_ada, delta_w_ada, new_m_w_ada, new_v_w_ada = _ada_grad_adamw(c_all.T, dmod_cols, w_ada2, m_w_ada[0], v_w_ada[0])

    small_w = [b_ada, g_pre, conv_w, conv_b, g_conv, g_attn, g_post]
    small_g = [grad_b_ada, grad_g_pre, grad_conv_w, grad_conv_b, grad_g_conv, grad_g_attn, grad_g_post]
    small_m = [m_b_ada, m_g_pre, m_conv_w, m_conv_b, m_g_conv, m_g_attn, m_g_post]
    small_v = [v_b_ada, v_g_pre, v_conv_w, v_conv_b, v_g_conv, v_g_attn, v_g_post]
    pw, po = _pack_small(small_w)
    pg, _ = _pack_small(small_g)
    pm, _ = _pack_small(small_m)
    pv, _ = _pack_small(small_v)
    sd, sm, sv = (a.reshape(-1) for a in _adamw(pw, pg, pm, pv, "adamw_small")[1:])

    def unpack(flat):
        return [flat[o:o + w.size].reshape(w.shape) for o, w in zip(po, small_w)]

    d_small, m_small, v_small = unpack(sd), unpack(sm), unpack(sv)

    rici_in = _owners_wait(ssem_in0, rsem_in0, csum_in0, land_in0, [sd, delta_w_out, delta_w_ada], "rs_owners_wait_in0")
    full_in = _join_halves(_owner_sum(place, gin, rsib_in, rici_in, "rs_owner_sum_in0", 0, 2), "rs_join_halves_in0", 0, 2)
    updated_in = _adamw(w_in2, full_in, m_w_in[0], v_w_in[0], "adamw_w_in0", 0, 2)
    rici_in = _owners_wait(ssem_in1, rsem_in1, csum_in1, land_in1, [updated_in[1]], "rs_owners_wait_in1")
    full_in = _join_halves(
        _owner_sum(place, gin, rsib_in, rici_in, "rs_owner_sum_in1", 1, 2, full_in), "rs_join_halves_in1", 1, 2)
    grad_w_in, delta_w_in, new_m_w_in, new_v_w_in = _adamw(
        w_in2, full_in, m_w_in[0], v_w_in[0], "adamw_w_in1", 1, 2, updated_in)

    def lead(a):
        return a.reshape((1,) + a.shape)

    grads = [lead(grad_w_ada), grad_b_ada, grad_g_pre, lead(grad_w_in), grad_conv_w, grad_conv_b, grad_g_conv,
             grad_g_attn, lead(grad_w_out), grad_g_post]
    deltas = [lead(delta_w_ada), d_small[0], d_small[1], lead(delta_w_in), d_small[2], d_small[3], d_small[4],
              d_small[5], lead(delta_w_out), d_small[6]]
    new_ms = [lead(new_m_w_ada), m_small[0], m_small[1], lead(new_m_w_in), m_small[2], m_small[3], m_small[4],
              m_small[5], lead(new_m_w_out), m_small[6]]
    new_vs = [lead(new_v_w_ada), v_small[0], v_small[1], lead(new_v_w_in), v_small[2], v_small[3], v_small[4],
              v_small[5], lead(new_v_w_out), v_small[6]]
    return (loss, lead(grad_x), *grads, *deltas, *new_ms, *new_vs)
```

```python
import functools

import jax
import jax.numpy as jnp
from jax import lax
from jax.experimental import pallas as pl
from jax.experimental.pallas import tpu as pltpu

F32 = jnp.float32
BF16 = jnp.bfloat16
MESH = pl.DeviceIdType.MESH
HBM = pl.BlockSpec(memory_space=pltpu.HBM)
VMEM = pl.BlockSpec(memory_space=pltpu.VMEM)
ANY = pl.BlockSpec(memory_space=pl.ANY)
SEM = pl.BlockSpec(memory_space=pltpu.SEMAPHORE)
EFFECT = pltpu.SideEffectType.DATAFLOW_SIDE_EFFECTING
SUBLANES, LANES = 8, 128
TOKEN = jax.ShapeDtypeStruct((SUBLANES, LANES), jnp.float32)

HEAD_DIM = 64
PAIR = 2 * HEAD_DIM
assert PAIR == LANES
BRANCHES = ((128, 1), (512, 4), (2048, 16))
SIDE = 64
EPS = 1e-6
NEG_INF = -1e30
N_CHIPS = 4
N_DEV = 8

ADAM_LR = 0.001
ADAM_B1 = 0.9
ADAM_B2 = 0.999
ADAM_EPS = 1e-08
ADAM_WD = 0.01
ADAM_STEP = 10

VMEM_LIMIT_BYTES = 56 * 1024 * 1024
ROW_TILE = 256
COL_TILE = 512
CONV_TILE = 256
ATT_BQ = 128
ATT_KW = ATT_BQ + 2 * SIDE
ATT_UNROLL = 4
SMALL_ALIGN = SUBLANES * LANES


def _params(semantics=None):
    kw = {"vmem_limit_bytes": VMEM_LIMIT_BYTES}
    if semantics is not None:
        kw["dimension_semantics"] = semantics
    return pltpu.CompilerParams(**kw)


def _silu(z):
    return z * jax.nn.sigmoid(z)


def _silu_grad(z):
    s = jax.nn.sigmoid(z)
    return s * (1.0 + z * (1.0 - s))


def _my_place():
    return lax.axis_index("x"), lax.axis_index("y"), lax.axis_index("c")


def _flip(a, bit):
    return 1 - a if bit else a


def _chip_of(x, y):
    return 2 * x + y


def _allgather8(v, name, after=()):
    rows_per, n = v.shape

    def body(v_ref, *rest):
        out_ref, send_sems, recv_sems = rest[len(after):]
        x, y, c = _my_place()
        me = 4 * x + 2 * y + c

        def rows(idx):
            return out_ref.at[pl.ds(pl.multiple_of(idx * rows_per, rows_per), rows_per), :]

        out_ref[pl.ds(pl.multiple_of(me * rows_per, rows_per), rows_per), :] = v_ref[...]
        copies = []
        for k in range(1, N_DEV):
            peer = (_flip(x, k & 4), _flip(y, k & 2), _flip(c, k & 1))
            cp = pltpu.make_async_remote_copy(
                src_ref=v_ref, dst_ref=rows(me), send_sem=send_sems.at[k - 1], recv_sem=recv_sems.at[k - 1],
                device_id=peer, device_id_type=MESH)
            cp.start()
            copies.append((cp, peer))
        for k, (cp, peer) in enumerate(copies):
            src = 4 * peer[0] + 2 * peer[1] + peer[2]
            pltpu.make_async_remote_copy(
                src_ref=v_ref, dst_ref=rows(src), send_sem=send_sems.at[k], recv_sem=recv_sems.at[k],
                device_id=peer, device_id_type=MESH).wait_recv()
        for cp, _ in copies:
            cp.wait_send()

    return pl.pallas_call(
        body, name=name,
        out_shape=jax.ShapeDtypeStruct((N_DEV * rows_per, n), v.dtype),
        in_specs=[VMEM] + [ANY] * len(after), out_specs=VMEM,
        scratch_shapes=[pltpu.SemaphoreType.DMA((N_DEV - 1,)), pltpu.SemaphoreType.DMA((N_DEV - 1,))],
    )(v, *after)


def _allgather8_start(v, me, name):
    rows_per, n = v.shape
    land = lax.dynamic_update_slice(jnp.zeros((N_DEV * rows_per, n), v.dtype), v, (me * rows_per, 0))

    def body(v_ref, land_ref, send_sems, recv_sems, v_thru, land_thru, token_ref):
        del v_thru, land_thru
        x, y, c = _my_place()
        mine = land_ref.at[pl.ds(pl.multiple_of((4 * x + 2 * y + c) * rows_per, rows_per), rows_per), :]
        for k in range(1, N_DEV):
            peer = (_flip(x, k & 4), _flip(y, k & 2), _flip(c, k & 1))
            pltpu.make_async_remote_copy(
                src_ref=v_ref, dst_ref=mine, send_sem=send_sems.at[k - 1], recv_sem=recv_sems.at[k - 1],
                device_id=peer, device_id_type=MESH).start()
        token_ref[...] = jnp.zeros(token_ref.shape, F32)

    sems = pltpu.SemaphoreType.DMA((N_DEV - 1,))
    return pl.pallas_call(
        body, name=name,
        out_shape=(sems, sems, jax.ShapeDtypeStruct(v.shape, v.dtype), jax.ShapeDtypeStruct(land.shape, land.dtype), TOKEN),
        in_specs=[HBM, HBM], out_specs=(SEM, SEM, HBM, HBM, VMEM),
        input_output_aliases={0: 2, 1: 3},
        compiler_params=pltpu.CompilerParams(has_side_effects=EFFECT),
    )(pltpu.with_memory_space_constraint(v, pltpu.HBM), pltpu.with_memory_space_constraint(land, pltpu.HBM))


def _allgather8_wait(send_sems, recv_sems, v, land, after, name):
    rows_per = v.shape[0]

    def body(v_ref, land_ref, send_ref, recv_ref, *rest):
        del rest
        x, y, c = _my_place()
        for k in range(1, N_DEV):
            peer = (_flip(x, k & 4), _flip(y, k & 2), _flip(c, k & 1))
            src = 4 * peer[0] + 2 * peer[1] + peer[2]
            cp = pltpu.make_async_remote_copy(
                src_ref=v_ref, dst_ref=land_ref.at[pl.ds(pl.multiple_of(src * rows_per, rows_per), rows_per), :],
                send_sem=send_ref.at[k - 1], recv_sem=recv_ref.at[k - 1], device_id=peer, device_id_type=MESH)
            cp.wait_send()
            cp.wait_recv()

    return pl.pallas_call(
        body, name=name,
        out_shape=(jax.ShapeDtypeStruct(v.shape, v.dtype), jax.ShapeDtypeStruct(land.shape, land.dtype)),
        in_specs=[HBM, HBM, SEM, SEM] + [ANY] * len(after), out_specs=(HBM, HBM),
        input_output_aliases={0: 0, 1: 1},
        compiler_params=pltpu.CompilerParams(has_side_effects=EFFECT),
    )(v, land, send_sems, recv_sems, *after)[1]


def _half_rows(ref, chip, which, half):
    return ref.at[chip, pl.ds(pl.multiple_of(which * half, half), half), :]


def _ici_peers(x, y, c):
    peers = [(_flip(x, k & 2), _flip(y, k & 1), c) for k in (1, 2, 3)]
    return [(peer, _chip_of(peer[0], peer[1])) for peer in peers]


def _part_rows(ref, chip, core, part):
    quarter = ref.shape[1] // 4
    return ref.at[chip, pl.ds(pl.multiple_of((2 * core + part) * quarter, quarter), quarter), :]


def _neighbours(x, y, c):
    return [((x, 1 - y, c), _chip_of(x, 1 - y)), ((1 - x, y, c), _chip_of(1 - x, y)),
            ((1 - x, 1 - y, c), _chip_of(1 - x, 1 - y))]


def _start_direct(buf, send_sems, recv_sems):
    x, y, c = _my_place()
    me = _chip_of(x, y)
    for n, (peer, _) in enumerate(_neighbours(x, y, c)[0:2]):
        for part in ((0, 1), (1, 0))[n]:
            piece = _part_rows(buf, me, c, part)
            pltpu.make_async_remote_copy(
                src_ref=piece, dst_ref=piece, send_sem=send_sems.at[2 * n + part], recv_sem=recv_sems.at[2 * n + part],
                device_id=peer, device_id_type=MESH).start()


def _relay(buf, recv_sems, relay_send, relay_recv):
    x, y, c = _my_place()
    nbrs = _neighbours(x, y, c)
    for n in range(2):
        part = n
        piece = _part_rows(buf, nbrs[n][1], c, part)
        pltpu.make_async_remote_copy(
            src_ref=piece, dst_ref=piece, send_sem=relay_send.at[part], recv_sem=recv_sems.at[2 * n + part],
            device_id=nbrs[n][0], device_id_type=MESH).wait_recv()
        pltpu.make_async_remote_copy(
            src_ref=piece, dst_ref=piece, send_sem=relay_send.at[part], recv_sem=relay_recv.at[part],
            device_id=nbrs[1 - n][0], device_id_type=MESH).start()


def _gather_start(win_slots, after):
    def body(win_in, after_ref, win_ref, send_sems, recv_sems, token_ref):
        del win_in, after_ref
        _start_direct(win_ref, send_sems, recv_sems)
        token_ref[...] = jnp.zeros(token_ref.shape, F32)

    sems = pltpu.SemaphoreType.DMA((4,))
    return pl.pallas_call(
        body, name="gather_start",
        out_shape=(jax.ShapeDtypeStruct(win_slots.shape, win_slots.dtype), sems, sems, TOKEN),
        in_specs=[HBM, ANY], out_specs=(HBM, SEM, SEM, VMEM),
        input_output_aliases={0: 0},
        compiler_params=pltpu.CompilerParams(has_side_effects=EFFECT),
    )(win_slots, after)


def _gather_relay_in(win, wout_slots, recv_in, after):
    def body(win_in, wout_in, recv_in_ref, after_ref, win_ref, wout_ref, relay_send, relay_recv, send_out, recv_out):
        del win_in, wout_in, after_ref
        _relay(win_ref, recv_in_ref, relay_send, relay_recv)
        _start_direct(wout_ref, send_out, recv_out)

    two, four = pltpu.SemaphoreType.DMA((2,)), pltpu.SemaphoreType.DMA((4,))
    return pl.pallas_call(
        body, name="gather_relay_w_in",
        out_shape=(jax.ShapeDtypeStruct(win.shape, win.dtype), jax.ShapeDtypeStruct(wout_slots.shape, wout_slots.dtype),
                   two, two, four, four),
        in_specs=[HBM, HBM, SEM, ANY], out_specs=(HBM, HBM, SEM, SEM, SEM, SEM),
        input_output_aliases={0: 0, 1: 1},
        compiler_params=pltpu.CompilerParams(has_side_effects=EFFECT),
    )(win, wout_slots, recv_in, after)


def _gather_relay_out(wout, recv_out, after):
    def body(wout_in, recv_out_ref, after_ref, wout_ref, relay_send, relay_recv):
        del wout_in, after_ref
        _relay(wout_ref, recv_out_ref, relay_send, relay_recv)

    two = pltpu.SemaphoreType.DMA((2,))
    return pl.pallas_call(
        body, name="gather_relay_w_out",
        out_shape=(jax.ShapeDtypeStruct(wout.shape, wout.dtype), two, two),
        in_specs=[HBM, SEM, ANY], out_specs=(HBM, SEM, SEM),
        input_output_aliases={0: 0},
        compiler_params=pltpu.CompilerParams(has_side_effects=EFFECT),
    )(wout, recv_out, after)


def _gather_wait_direct(buf, send_sems, recv_sems, after, name):
    def body(buf_in, send_ref, recv_ref, after_ref, buf_ref):
        del buf_in, after_ref
        x, y, c = _my_place()
        me = _chip_of(x, y)
        for n, (peer, chip) in enumerate(_neighbours(x, y, c)[0:2]):
            second = 1 - n
            pltpu.make_async_remote_copy(
                src_ref=_part_rows(buf_ref, me, c, second), dst_ref=_part_rows(buf_ref, chip, c, second),
                send_sem=send_ref.at[2 * n + second], recv_sem=recv_ref.at[2 * n + second],
                device_id=peer, device_id_type=MESH).wait_recv()
            for part in range(2):
                piece = _part_rows(buf_ref, me, c, part)
                pltpu.make_async_remote_copy(
                    src_ref=piece, dst_ref=piece, send_sem=send_ref.at[2 * n + part], recv_sem=recv_ref.at[2 * n + part],
                    device_id=peer, device_id_type=MESH).wait_send()

    return pl.pallas_call(
        body, name=name,
        out_shape=jax.ShapeDtypeStruct(buf.shape, buf.dtype),
        in_specs=[HBM, SEM, SEM, ANY], out_specs=HBM,
        input_output_aliases={0: 0},
        compiler_params=pltpu.CompilerParams(has_side_effects=EFFECT),
    )(buf, send_sems, recv_sems, after)


def _gather_wait_relayed(buf, relay_send, relay_recv, after, name):
    def body(buf_in, rsend_ref, rrecv_ref, after_ref, buf_ref):
        del buf_in, after_ref
        x, y, c = _my_place()
        nbrs = _neighbours(x, y, c)
        for n in range(2):
            relayed = _part_rows(buf_ref, nbrs[n][1], c, n)
            cp = pltpu.make_async_remote_copy(
                src_ref=relayed, dst_ref=_part_rows(buf_ref, nbrs[2][1], c, n),
                send_sem=rsend_ref.at[n], recv_sem=rrecv_ref.at[n], device_id=nbrs[1 - n][0], device_id_type=MESH)
            cp.wait_recv()
            cp.wait_send()

    return pl.pallas_call(
        body, name=name,
        out_shape=jax.ShapeDtypeStruct(buf.shape, buf.dtype),
        in_specs=[HBM, SEM, SEM, ANY], out_specs=HBM,
        input_output_aliases={0: 0},
        compiler_params=pltpu.CompilerParams(has_side_effects=EFFECT),
    )(buf, relay_send, relay_recv, after)


def _forward_halves(buf, which, name):
    half = buf.shape[1] // 2

    def body(buf_in, buf_ref, send_sems, recv_sems):
        del buf_in
        x, y, c = _my_place()
        sibling = (x, y, 1 - c)
        chips = [_neighbours(x, y, c)[n][1] for n in which]
        started = []
        for k, src_chip in enumerate(chips):
            landed = _half_rows(buf_ref, src_chip, c, half)
            fw = pltpu.make_async_remote_copy(
                src_ref=landed, dst_ref=landed, send_sem=send_sems.at[k], recv_sem=recv_sems.at[k],
                device_id=sibling, device_id_type=MESH)
            fw.start()
            started.append(fw)
        for k, src_chip in enumerate(chips):
            other = _half_rows(buf_ref, src_chip, 1 - c, half)
            pltpu.make_async_remote_copy(
                src_ref=other, dst_ref=other, send_sem=send_sems.at[k], recv_sem=recv_sems.at[k],
                device_id=sibling, device_id_type=MESH).wait_recv()
        for fw in started:
            fw.wait_send()

    return pl.pallas_call(
        body, name=name,
        out_shape=jax.ShapeDtypeStruct(buf.shape, buf.dtype),
        in_specs=[HBM], out_specs=HBM,
        input_output_aliases={0: 0},
        scratch_shapes=[pltpu.SemaphoreType.DMA((len(which),))] * 2,
    )(buf)


def _dw_swapped(a, b, row_chunks, col_chunks, name):
    r, t = a.shape
    c_all = b.shape[1]
    chunks = row_chunks * col_chunks
    rq, cq = r // row_chunks, c_all // col_chunks
    half = rq // 2
    tn = COL_TILE
    nt = cq // tn
    steps = col_chunks * nt

    def body(a_ref, b_ref, mine_ref, sib_ref, stage, send_sems, recv_sems):
        x, y, c = _my_place()
        j, n = pl.program_id(0), pl.program_id(1)
        step = j * nt + n
        slot = step % 2
        res = jnp.dot(a_ref[...], b_ref[...], preferred_element_type=F32).astype(BF16)

        def landing(jj, nn):
            cols = pl.ds(pl.multiple_of(nn * tn, tn), tn)
            return sib_ref.at[:, :, cols] if col_chunks == 1 else sib_ref.at[pl.ds(jj, 1), :, cols]

        def copy(slot_, step_, jj, nn):
            return pltpu.make_async_remote_copy(
                src_ref=stage.at[slot_], dst_ref=landing(jj, nn), send_sem=send_sems.at[slot_],
                recv_sem=recv_sems.at[step_], device_id=(x, y, 1 - c), device_id_type=MESH)

        @pl.when(step >= 2)
        def _():
            copy(slot, step, j, n).wait_send()

        for q in range(row_chunks):
            lo = res[q * rq:q * rq + half, :]
            hi = res[q * rq + half:(q + 1) * rq, :]
            mine_ref[q] = jnp.where(c == 0, lo, hi)
            stage[slot, q] = jnp.where(c == 0, hi, lo)
        copy(slot, step, j, n).start()

        @pl.when(step == steps - 1)
        def _():
            for s in range(max(steps - 2, 0), steps):
                copy(s % 2, s, j, n).wait_send()
            for s in range(steps):
                copy(s % 2, s, j, n).wait_recv()

    shape = jax.ShapeDtypeStruct((chunks, half, cq), BF16)
    return pl.pallas_call(
        body, name=name, grid=(col_chunks, nt),
        out_shape=(shape, shape),
        in_specs=[pl.BlockSpec((r, t), lambda j, n: (0, 0)), pl.BlockSpec((t, tn), lambda j, n: (0, j * nt + n))],
        out_specs=(pl.BlockSpec((row_chunks, half, tn), lambda j, n: (j, 0, n)), ANY),
        scratch_shapes=[pltpu.VMEM((2, row_chunks, half, tn), BF16), pltpu.SemaphoreType.DMA((2,)),
                        pltpu.SemaphoreType.DMA((steps,))],
        compiler_params=_params(("arbitrary", "arbitrary")),
    )(a, b)


def _owners_start(csum, name, after=()):
    land = pltpu.with_memory_space_constraint(lax.empty((N_CHIPS - 1,) + csum.shape[1:], csum.dtype), pltpu.HBM)

    def body(csum_ref, land_ref, *rest):
        send_sems, recv_sems, _, _, token_ref = rest[len(after):]
        x, y, c = _my_place()
        for k, (peer, owner) in enumerate(_ici_peers(x, y, c)):
            pltpu.make_async_remote_copy(
                src_ref=csum_ref.at[owner], dst_ref=land_ref.at[k], send_sem=send_sems.at[k], recv_sem=recv_sems.at[k],
                device_id=peer, device_id_type=MESH).start()
        token_ref[...] = jnp.zeros(token_ref.shape, F32)

    sems = pltpu.SemaphoreType.DMA((N_CHIPS - 1,))
    return pl.pallas_call(
        body, name=name,
        out_shape=(sems, sems, jax.ShapeDtypeStruct(csum.shape, csum.dtype),
                   jax.ShapeDtypeStruct(land.shape, land.dtype), TOKEN),
        in_specs=[HBM, HBM] + [ANY] * len(after), out_specs=(SEM, SEM, HBM, HBM, VMEM),
        input_output_aliases={0: 2, 1: 3},
        compiler_params=pltpu.CompilerParams(has_side_effects=EFFECT),
    )(pltpu.with_memory_space_constraint(csum, pltpu.HBM), land, *after)


def _owners_wait(send_sems, recv_sems, csum, land, after, name):
    def body(csum_ref, land_ref, send_ref, recv_ref, *rest):
        del rest
        x, y, c = _my_place()
        for k, (peer, owner) in enumerate(_ici_peers(x, y, c)):
            cp = pltpu.make_async_remote_copy(
                src_ref=csum_ref.at[owner], dst_ref=land_ref.at[k], send_sem=send_ref.at[k], recv_sem=recv_ref.at[k],
                device_id=peer, device_id_type=MESH)
            cp.wait_send()
            cp.wait_recv()

    return pl.pallas_call(
        body, name=name,
        out_shape=(jax.ShapeDtypeStruct(csum.shape, csum.dtype), jax.ShapeDtypeStruct(land.shape, land.dtype)),
        in_specs=[HBM, HBM, SEM, SEM] + [ANY] * len(after), out_specs=(HBM, HBM),
        input_output_aliases={0: 0, 1: 1},
        compiler_params=pltpu.CompilerParams(has_side_effects=EFFECT),
    )(csum, land, send_sems, recv_sems, *after)[1]


def _join_halves(full, name, part=0, parts=1):
    half = full.shape[0] // 2
    rows = half // parts

    def body(full_in, full_ref, send_sem, recv_sem):
        del full_in
        x, y, c = _my_place()
        sibling = (x, y, 1 - c)
        mine = full_ref.at[pl.ds(pl.multiple_of(c * half + part * rows, rows), rows), :]
        theirs = full_ref.at[pl.ds(pl.multiple_of((1 - c) * half + part * rows, rows), rows), :]
        cp = pltpu.make_async_remote_copy(
            src_ref=mine, dst_ref=mine, send_sem=send_sem, recv_sem=recv_sem, device_id=sibling, device_id_type=MESH)
        cp.start()
        pltpu.make_async_remote_copy(
            src_ref=theirs, dst_ref=theirs, send_sem=send_sem, recv_sem=recv_sem,
            device_id=sibling, device_id_type=MESH).wait_recv()
        cp.wait_send()

    return pl.pallas_call(
        body, name=name,
        out_shape=jax.ShapeDtypeStruct(full.shape, full.dtype),
        in_specs=[HBM], out_specs=HBM,
        input_output_aliases={0: 0},
        scratch_shapes=[pltpu.SemaphoreType.DMA, pltpu.SemaphoreType.DMA],
    )(full)


def _cast_into_slot(place, w, name):
    rows, cols = w.shape
    tr = min(rows, ROW_TILE)

    def body(place_ref, w_ref, o_ref):
        del place_ref
        o_ref[...] = w_ref[...].astype(BF16)

    grid_spec = pltpu.PrefetchScalarGridSpec(
        num_scalar_prefetch=1, grid=(rows // tr,),
        in_specs=[pl.BlockSpec((tr, cols), lambda i, p: (i, 0))],
        out_specs=pl.BlockSpec((None, tr, cols), lambda i, p: (p[0], i, 0)))
    return pl.pallas_call(
        body, name=name, grid_spec=grid_spec,
        out_shape=jax.ShapeDtypeStruct((N_CHIPS, rows, cols), BF16),
        compiler_params=_params(("parallel",)),
    )(place, w)


def _ada_partial(c_all, w_ada):
    d_model, wa = w_ada.shape
    tn = 512 if wa % 512 == 0 else 256

    def body(c_ref, w_ref, o_ref):
        o_ref[...] = jnp.dot(_silu(c_ref[...]), w_ref[...], precision=lax.Precision.HIGHEST,
                             preferred_element_type=F32)

    return pl.pallas_call(
        body, name="ada_partial", grid=(wa // tn,),
        out_shape=jax.ShapeDtypeStruct((N_DEV, wa), F32),
        in_specs=[pl.BlockSpec((N_DEV, d_model), lambda i: (0, 0)), pl.BlockSpec((d_model, tn), lambda i: (0, i))],
        out_specs=pl.BlockSpec((N_DEV, tn), lambda i: (0, i)),
        compiler_params=_params(("parallel",)),
    )(c_all, w_ada)


def _prenorm(x, mod, g_pre):
    t, d = x.shape
    tb = ROW_TILE

    def body(x_ref, mod_ref, g_ref, h_ref, ht_ref):
        xv = x_ref[...]
        r = lax.rsqrt(jnp.mean(xv * xv, axis=-1, keepdims=True) + EPS)
        h = (xv * r) * g_ref[...] * (1.0 + mod_ref[1:2, :]) + mod_ref[0:1, :]
        h_ref[...] = h.astype(BF16)
        ht_ref[...] = h.T.astype(BF16)

    return pl.pallas_call(
        body, name="prenorm", grid=(t // tb,),
        out_shape=(jax.ShapeDtypeStruct((t, d), BF16), jax.ShapeDtypeStruct((d, t), BF16)),
        in_specs=[pl.BlockSpec((tb, d), lambda i: (i, 0)), pl.BlockSpec((3, d), lambda i: (0, 0)),
                  pl.BlockSpec((1, d), lambda i: (0, 0))],
        out_specs=(pl.BlockSpec((tb, d), lambda i: (i, 0)), pl.BlockSpec((d, tb), lambda i: (0, i))),
        compiler_params=_params(("parallel",)),
    )(x, mod, g_pre)


def _proj_chunks(proj, h, w, chunks, name):
    t, d = h.shape
    ws = w.shape[-1]
    tn = COL_TILE
    nt = ws // tn

    def body(chunk_ref, *refs):
        del chunk_ref
        a_ref, b_ref, o_ref = refs[-3:]
        o_ref[...] = jnp.dot(a_ref[...], b_ref[...].astype(BF16), preferred_element_type=F32).astype(BF16)

    if w.ndim == 3:
        w_spec = pl.BlockSpec((None, d, tn), lambda i, n, ch: (ch[i], 0, n))
    else:
        w_spec = pl.BlockSpec((d, tn), lambda i, n, ch: (0, n))
    first = proj is None
    grid_spec = pltpu.PrefetchScalarGridSpec(
        num_scalar_prefetch=1, grid=(chunks.shape[0], nt),
        in_specs=([] if first else [HBM]) + [pl.BlockSpec((t, d), lambda i, n, ch: (0, 0)), w_spec],
        out_specs=pl.BlockSpec((t, tn), lambda i, n, ch: (0, ch[i] * nt + n)))
    return pl.pallas_call(
        body, name=name, grid_spec=grid_spec,
        out_shape=jax.ShapeDtypeStruct((t, N_CHIPS * ws), BF16),
        input_output_aliases={} if first else {1: 0},
        compiler_params=_params(("parallel", "parallel")),
    )(*([chunks] if first else [chunks, proj]), h, w)


def _shift_rows(a, rows):
    idx = lax.broadcasted_iota(jnp.int32, a.shape, 0)
    prev = jnp.where(idx == 0, 0.0, pltpu.roll(a, 1, 0))
    nxt = jnp.where(idx == rows - 1, 0.0, pltpu.roll(a, rows - 1, 0))
    return prev, nxt


def _conv_fwd(conv_proj, conv_w, conv_b, dc):
    t = conv_proj.shape[0]
    ct = CONV_TILE
    nct = dc // ct

    def body(u_ref, cg_ref, w_ref, b_ref, co_ref):
        a = cg_ref[...].astype(F32) * u_ref[...].astype(F32)
        prev, nxt = _shift_rows(a, t)
        co_ref[...] = (w_ref[0:1, :] * prev + w_ref[1:2, :] * a + w_ref[2:3, :] * nxt + b_ref[...]).astype(BF16)

    return pl.pallas_call(
        body, name="conv_fwd", grid=(nct,),
        out_shape=jax.ShapeDtypeStruct((t, dc), BF16),
        in_specs=[pl.BlockSpec((t, ct), lambda i: (0, i)), pl.BlockSpec((t, ct), lambda i: (0, 2 * nct + i)),
                  pl.BlockSpec((3, ct), lambda i: (0, i)), pl.BlockSpec((1, ct), lambda i: (0, i))],
        out_specs=pl.BlockSpec((t, ct), lambda i: (0, i)),
        compiler_params=_params(("parallel",)),
    )(conv_proj, conv_proj, conv_w, conv_b)


def _to_residue_major(src_ref, dst_ref, r):
    seq = src_ref.shape[0] // r
    for res in range(r):
        dst_ref[res * seq:(res + 1) * seq, :] = src_ref[pl.ds(res, seq, stride=r), :].astype(dst_ref.dtype)


def _branch_operands(token_refs, stage, dil, r):
    if r == 1:
        return list(token_refs)
    for i, ref in enumerate(token_refs):
        stage[...] = ref[...].astype(F32)
        _to_residue_major(stage, dil.at[i], r)
    return [dil.at[i] for i in range(len(token_refs))]


def _scaled_queries(q):
    return (q.astype(F32) * (HEAD_DIM ** -0.5)).astype(BF16)


BLOCK_SHIFTS = (0, -SIDE, None)


def _band_bias(rel, slope):
    arel = jnp.abs(rel)
    return jnp.where(arel <= SIDE, arel.astype(F32) * slope, NEG_INF)


def _fill_bias_tiles(bias_ref, sl_ref, r, kw):
    base = lax.broadcasted_iota(jnp.int32, (ATT_BQ, kw), 1) - lax.broadcasted_iota(jnp.int32, (ATT_BQ, kw), 0)
    for hh in range(2):
        slope = -(sl_ref[hh:hh + 1, 0:kw] * float(r))
        for e, shift in enumerate(BLOCK_SHIFTS):
            shift = ATT_BQ - kw if shift is None else shift
            bias_ref[hh, e, :, 0:kw] = _band_bias(base + shift, slope)


def _fill_stacked_bias_tiles(bias_ref, sl_ref, r, kw):
    base = lax.broadcasted_iota(jnp.int32, (kw, ATT_BQ), 0) - lax.broadcasted_iota(jnp.int32, (kw, ATT_BQ), 1)
    for hh in range(2):
        slope = -(sl_ref[hh:hh + 1, 0:ATT_BQ] * float(r))
        for e, shift in enumerate(BLOCK_SHIFTS):
            shift = ATT_BQ - kw if shift is None else shift
            bias_ref[e, 0:kw, hh * ATT_BQ:(hh + 1) * ATT_BQ] = _band_bias(base + shift, slope)


def _first_head_lanes():
    return lax.broadcasted_iota(jnp.int32, (1, PAIR), 1) < HEAD_DIM


def _only_head(x, first, hh):
    return jnp.where(first if hh == 0 else jnp.logical_not(first), x, jnp.zeros_like(x))


def _block_place(g, seq_len, kw):
    nqb = seq_len // ATT_BQ
    if nqb == 1:
        row = pl.multiple_of(g * ATT_BQ, ATT_BQ)
        return row, row, 0
    res = g // nqb
    qb = g - res * nqb
    q0 = qb * ATT_BQ
    ks = jnp.clip(q0 - SIDE, 0, seq_len - kw)
    edge = jnp.where(qb == 0, 0, jnp.where(qb == nqb - 1, 2, 1))
    return (pl.multiple_of(res * seq_len + q0, ATT_BQ), pl.multiple_of(res * seq_len + ks, SIDE), edge)


def _qkv_specs(dc, da, t, index):
    return [pl.BlockSpec((t, PAIR), functools.partial(index, (4 * dc + comp * da) // PAIR)) for comp in range(3)]


def _attn_fwd(proj, slopes, dc, da):
    t = proj.shape[0]
    hp = da // PAIR
    n_blocks = t // ATT_BQ

    def body(q_ref, k_ref, v_ref, sl_ref, o_ref, lse_ref, stage, dil, bias, o_res, l_res, o_tok, l_tok):
        for b, (_, r) in enumerate(BRANCHES):
            seq_len = t // r
            kw = min(ATT_KW, seq_len)
            ops = _branch_operands([q_ref, k_ref, v_ref], stage, dil, r)
            _fill_bias_tiles(bias, sl_ref, r, kw)
            o_dst, l_dst = (o_tok.at[b], l_tok.at[b]) if r == 1 else (o_res, l_res)
            first = _first_head_lanes()

            def blocks(trip, carry, seq_len=seq_len, kw=kw, o_dst=o_dst, l_dst=l_dst, first=first, ops=ops):
                nt = (((1,), (1,)), ((), ()))
                places = [_block_place(trip * ATT_UNROLL + i, seq_len, kw) for i in range(ATT_UNROLL)]
                chains = [(i, hh) for i in range(ATT_UNROLL) for hh in range(2)]
                qs = [_scaled_queries(ops[0][pl.ds(qrow, ATT_BQ), :]) for qrow, _, _ in places]
                ks = [ops[1][pl.ds(krow, kw), :] for _, krow, _ in places]
                vs = [ops[2][pl.ds(krow, kw), :] for _, krow, _ in places]
                ss = [lax.dot_general(_only_head(qs[i], first, hh), ks[i], nt, preferred_element_type=F32)
                      + bias[hh, places[i][2], :, 0:kw] for i, hh in chains]
                tops = [jnp.max(s, axis=-1, keepdims=True) for s in ss]
                ps = [jnp.exp(s - m) for s, m in zip(ss, tops)]
                dens = [jnp.sum(p, axis=-1, keepdims=True) for p in ps]
                for i, (qrow, _, _) in enumerate(places):
                    weights = jnp.concatenate([ps[2 * i].astype(BF16), ps[2 * i + 1].astype(BF16)], axis=1)
                    values = jnp.concatenate([_only_head(vs[i], first, 0), _only_head(vs[i], first, 1)], axis=0)
                    den = jnp.where(first, dens[2 * i], dens[2 * i + 1])
                    o_dst[pl.ds(qrow, ATT_BQ), :] = jnp.dot(weights, values, preferred_element_type=F32) / den
                    l_dst[pl.ds(qrow, ATT_BQ), :] = jnp.where(first, tops[2 * i], tops[2 * i + 1]) + jnp.log(den)
                return carry

            lax.fori_loop(0, n_blocks // ATT_UNROLL, blocks, 0)
            if r > 1:
                for res in range(r):
                    rows = slice(res * seq_len, (res + 1) * seq_len)
                    o_tok[b, pl.ds(res, seq_len, stride=r), :] = o_res[rows, :]
                    l_tok[b, pl.ds(res, seq_len, stride=r), :] = l_res[rows, :]

        def merge(i, carry):
            rows = pl.ds(pl.multiple_of(i * ROW_TILE, ROW_TILE), ROW_TILE)
            la, lb, lc = l_tok[0, rows, :], l_tok[1, rows, :], l_tok[2, rows, :]
            m = jnp.maximum(jnp.maximum(la, lb), lc)
            wa, wb, wc = jnp.exp(la - m), jnp.exp(lb - m), jnp.exp(lc - m)
            den = wa + wb + wc
            o_ref[rows, :] = (wa * o_tok[0, rows, :] + wb * o_tok[1, rows, :] + wc * o_tok[2, rows, :]) * (1.0 / den)
            lse_ref[rows, :] = m + jnp.log(den)
            return carry

        lax.fori_loop(0, t // ROW_TILE, merge, 0)

    pair_spec = pl.BlockSpec((None, t, PAIR), lambda h: (h, 0, 0))
    return pl.pallas_call(
        body, name="attn_fwd", grid=(hp,),
        out_shape=(jax.ShapeDtypeStruct((hp, t, PAIR), F32), jax.ShapeDtypeStruct((hp, t, PAIR), F32)),
        in_specs=_qkv_specs(dc, da, t, lambda first, h: (0, first + h))
        + [pl.BlockSpec((None, 8, ATT_KW), lambda h: (h, 0, 0))],
        out_specs=(pair_spec, pair_spec),
        scratch_shapes=[pltpu.VMEM((t, PAIR), F32), pltpu.VMEM((3, t, PAIR), BF16),
                        pltpu.VMEM((2, 3, ATT_BQ, ATT_KW), F32),
                        pltpu.VMEM((t, PAIR), F32), pltpu.VMEM((t, PAIR), F32),
                        pltpu.VMEM((3, t, PAIR), F32), pltpu.VMEM((3, t, PAIR), F32)],
        compiler_params=_params(("parallel",)),
    )(proj, proj, proj, slopes)


def _attn_bwd(dproj, proj, d_o, lse, delta, slopes, dc, da, after):
    t = proj.shape[0]
    hp = da // PAIR
    n_blocks = t // ATT_BQ

    def all_branches(q_ref, k_ref, v_ref, do_ref, lse_ref, dl_ref, sl_ref,
                     stage, dil, packed, packed_res, row_vecs, bias_t, acc, tot):
        first = _first_head_lanes()
        lane = lax.broadcasted_iota(jnp.int32, (1, PAIR), 1)
        packed[...] = jnp.where((lane & (HEAD_DIM - 1)) < HEAD_DIM // 2, lse_ref[...], dl_ref[...])
        for b, (_, r) in enumerate(BRANCHES):
            seq_len = t // r
            kw = min(ATT_KW, seq_len)
            ops = _branch_operands([q_ref, k_ref, v_ref, do_ref], stage, dil, r)
            scalars = packed
            if r > 1:
                _to_residue_major(packed, packed_res, r)
                scalars = packed_res
            for g in range(n_blocks):
                flipped = scalars[g * ATT_BQ:(g + 1) * ATT_BQ, :].T
                for row in range(4):
                    row_vecs[g, row:row + 1, :] = flipped[row * (HEAD_DIM // 2):row * (HEAD_DIM // 2) + 1, :]
            _fill_stacked_bias_tiles(bias_t, sl_ref, r, kw)
            acc[1] = jnp.zeros((t, PAIR), F32)
            acc[2] = jnp.zeros((t, PAIR), F32)

            def blocks(trip, carry, seq_len=seq_len, kw=kw, ops=ops):
                nt = (((1,), (1,)), ((), ()))
                group = range(ATT_UNROLL)
                places = [_block_place(trip * ATT_UNROLL + i, seq_len, kw) for i in group]
                ks, vs, q2s, do2s, lse2s, dl2s = [], [], [], [], [], []
                for i, (qrow, krow, _) in zip(group, places):
                    q = _scaled_queries(ops[0][pl.ds(qrow, ATT_BQ), :])
                    dov = ops[3][pl.ds(qrow, ATT_BQ), :]
                    ks.append(ops[1][pl.ds(krow, kw), :])
                    vs.append(ops[2][pl.ds(krow, kw), :])
                    q2s.append(jnp.concatenate([_only_head(q, first, 0), _only_head(q, first, 1)], axis=0))
                    do2s.append(jnp.concatenate([_only_head(dov, first, 0), _only_head(dov, first, 1)], axis=0))
                    rows = row_vecs[trip * ATT_UNROLL + i]
                    lse2s.append(jnp.concatenate([rows[0:1, :], rows[2:3, :]], axis=1))
                    dl2s.append(jnp.concatenate([rows[1:2, :], rows[3:4, :]], axis=1))
                s_ts = [lax.dot_general(ks[i], q2s[i], nt, preferred_element_type=F32) for i in group]
                dp_ts = [lax.dot_general(vs[i], do2s[i], nt, preferred_element_type=F32) for i in group]
                p_ts = [jnp.exp(s_ts[i] + bias_t[places[i][2], 0:kw, :] - lse2s[i]) for i in group]
                ds_ts = [p_ts[i] * (dp_ts[i] - dl2s[i]) for i in group]
                dvs = [jnp.dot(p_ts[i].astype(BF16), do2s[i], preferred_element_type=F32) for i in group]
                dks = [jnp.dot(ds_ts[i].astype(BF16), q2s[i], preferred_element_type=F32) for i in group]
                dss = [ds_ts[i].T.astype(BF16) for i in group]
                dqs = [jnp.dot(dss[i][0:ATT_BQ, :], _only_head(ks[i], first, 0), preferred_element_type=F32)
                       + jnp.dot(dss[i][ATT_BQ:2 * ATT_BQ, :], _only_head(ks[i], first, 1), preferred_element_type=F32)
                       for i in group]
                for i, (qrow, krow, _) in zip(group, places):
                    acc[0, pl.ds(qrow, ATT_BQ), :] = dqs[i] * (HEAD_DIM ** -0.5)
                    acc[1, pl.ds(krow, kw), :] += dks[i]
                    acc[2, pl.ds(krow, kw), :] += dvs[i]
                return carry

            lax.fori_loop(0, n_blocks // ATT_UNROLL, blocks, 0)
            for comp in range(3):
                if r == 1:
                    tot[comp] = acc[comp]
                else:
                    for res in range(r):
                        tok = pl.ds(res, seq_len, stride=r)
                        tot[comp, tok, :] = tot[comp, tok, :] + acc[comp, res * seq_len:(res + 1) * seq_len, :]

    first_q = (4 * dc) // PAIR

    def body(dproj_in, q_ref, k_ref, v_ref, do_ref, lse_ref, dl_ref, sl_ref, after_ref, out_ref, *scratch):
        del dproj_in, after_ref
        work, out_stage, out_sems = scratch[:-2], scratch[-2], scratch[-1]
        h = pl.program_id(0)
        all_branches(q_ref, k_ref, v_ref, do_ref, lse_ref, dl_ref, sl_ref, *work)

        def out_copy(comp):
            cols = pl.ds(pl.multiple_of((first_q + comp * hp + h) * PAIR, PAIR), PAIR)
            return pltpu.make_async_copy(out_stage.at[comp], out_ref.at[:, cols], out_sems.at[comp])

        @pl.when(h > 0)
        def _():
            for comp in range(3):
                out_copy(comp).wait()

        for comp in range(3):
            out_stage[comp] = work[-1][comp].astype(BF16)
            out_copy(comp).start()

        @pl.when(h == hp - 1)
        def _():
            for comp in range(3):
                out_copy(comp).wait()

    pair_spec = pl.BlockSpec((None, t, PAIR), lambda h: (h, 0, 0))
    return pl.pallas_call(
        body, name="attn_bwd", grid=(hp,),
        out_shape=jax.ShapeDtypeStruct(dproj.shape, BF16),
        in_specs=[HBM] + _qkv_specs(dc, da, t, lambda first, h: (0, first + h))
        + [pair_spec, pair_spec, pair_spec, pl.BlockSpec((None, 8, ATT_KW), lambda h: (h, 0, 0)), ANY],
        out_specs=ANY,
        input_output_aliases={0: 0},
        scratch_shapes=[pltpu.VMEM((t, PAIR), F32), pltpu.VMEM((4, t, PAIR), BF16),
                        pltpu.VMEM((t, PAIR), F32), pltpu.VMEM((t, PAIR), F32),
                        pltpu.VMEM((n_blocks, 8, ATT_BQ), F32), pltpu.VMEM((3, ATT_KW, 2 * ATT_BQ), F32),
                        pltpu.VMEM((3, t, PAIR), F32), pltpu.VMEM((3, t, PAIR), F32),
                        pltpu.VMEM((3, t, PAIR), BF16), pltpu.SemaphoreType.DMA((3,))],
        compiler_params=_params(("arbitrary",)),
    )(dproj, proj, proj, proj, d_o, lse, delta, slopes, after)


def _mix_fwd(co, proj, o_mix, g_conv, g_attn_pairs):
    t, dc = co.shape
    hp = o_mix.shape[0]
    da = hp * PAIR
    tb = ROW_TILE

    def body(co_ref, bg_ref, zc_ref, za_ref, om_ref, gc_ref, ga_ref, ycat_ref, ycatt_ref):
        p = bg_ref[...].astype(F32) * co_ref[...].astype(F32)
        rc = lax.rsqrt(jnp.mean(p * p, axis=-1, keepdims=True) + EPS)
        yc = (p * rc) * gc_ref[...] * _silu(zc_ref[...].astype(F32))
        ycat_ref[:, 0:dc] = yc.astype(BF16)
        ycatt_ref[0:dc, :] = yc.T.astype(BF16)
        ssq = jnp.zeros((tb, 1), F32)
        for h in range(hp):
            o = om_ref[h]
            ssq = ssq + jnp.sum(o * o, axis=-1, keepdims=True)
        ra = lax.rsqrt(ssq * (1.0 / da) + EPS)
        for h in range(hp):
            ya = (om_ref[h] * ra) * ga_ref[h] * _silu(za_ref[:, h * PAIR:(h + 1) * PAIR].astype(F32))
            ycat_ref[:, dc + h * PAIR:dc + (h + 1) * PAIR] = ya.astype(BF16)
            ycatt_ref[dc + h * PAIR:dc + (h + 1) * PAIR, :] = ya.T.astype(BF16)

    pair_spec = pl.BlockSpec((hp, tb, PAIR), lambda i: (0, i, 0))
    return pl.pallas_call(
        body, name="mix_fwd", grid=(t // tb,),
        out_shape=(jax.ShapeDtypeStruct((t, dc + da), BF16), jax.ShapeDtypeStruct((dc + da, t), BF16)),
        in_specs=[pl.BlockSpec((tb, dc), lambda i: (i, 0)),
                  pl.BlockSpec((tb, dc), lambda i: (i, 1)),
                  pl.BlockSpec((tb, dc), lambda i: (i, 3)),
                  pl.BlockSpec((tb, da), lambda i: (i, 7)),
                  pair_spec,
                  pl.BlockSpec((1, dc), lambda i: (0, 0)),
                  pl.BlockSpec((hp, 1, PAIR), lambda i: (0, 0, 0))],
        out_specs=(pl.BlockSpec((tb, dc + da), lambda i: (i, 0)), pl.BlockSpec((dc + da, tb), lambda i: (0, i))),
        compiler_params=_params(("parallel",)),
    )(co, proj, proj, proj, o_mix, g_conv, g_attn_pairs)


def _out_fwd_bwd(ycat, woutf, x, target, mod, g_post):
    t, d = x.shape
    n = ycat.shape[1]
    tb = ROW_TILE

    def body(a_ref, w_ref, x_ref, tg_ref, mod_ref, g_ref, dout_ref, dy_ref, acc_ref):
        y = jnp.dot(a_ref[...], w_ref[...], preferred_element_type=F32)
        r = lax.rsqrt(jnp.mean(y * y, axis=-1, keepdims=True) + EPS)
        nh = y * r
        gate = mod_ref[2:3, :]
        nrm = nh * g_ref[...]
        err = x_ref[...] + gate * nrm - tg_ref[...]
        dout = err * (1.0 / d)
        dout_ref[...] = dout.astype(BF16)
        dn = dout * gate
        a = dn * g_ref[...]
        dy = r * (a - nh * jnp.mean(a * nh, axis=-1, keepdims=True))
        dy_ref[...] = dy.astype(BF16)
        loss = 0.5 * jnp.sum(jnp.sum(err * err, axis=-1, keepdims=True) * (1.0 / d), axis=0, keepdims=True)
        part = jnp.concatenate(
            [jnp.sum(dout * nrm, axis=0, keepdims=True), jnp.sum(dn * nh, axis=0, keepdims=True),
             jnp.broadcast_to(loss, (1, d)), jnp.zeros((5, d), F32)], axis=0)

        @pl.when(pl.program_id(0) == 0)
        def _():
            acc_ref[...] = jnp.zeros(acc_ref.shape, F32)

        acc_ref[...] += part

    return pl.pallas_call(
        body, name="out_fwd_bwd", grid=(t // tb,),
        out_shape=(jax.ShapeDtypeStruct((t, d), BF16), jax.ShapeDtypeStruct((t, d), BF16),
                   jax.ShapeDtypeStruct((8, d), F32)),
        in_specs=[pl.BlockSpec((tb, n), lambda i: (i, 0)), pl.BlockSpec((n, d), lambda i: (0, 0)),
                  pl.BlockSpec((tb, d), lambda i: (i, 0)), pl.BlockSpec((tb, d), lambda i: (i, 0)),
                  pl.BlockSpec((3, d), lambda i: (0, 0)), pl.BlockSpec((1, d), lambda i: (0, 0))],
        out_specs=(pl.BlockSpec((tb, d), lambda i: (i, 0)), pl.BlockSpec((tb, d), lambda i: (i, 0)),
                   pl.BlockSpec((8, d), lambda i: (0, 0))),
        compiler_params=_params(("arbitrary",)),
    )(ycat, woutf, x, target, mod, g_post)


def _matmul_nt(a, b, out_dtype, name):
    m, k = a.shape
    n = b.shape[0]
    tn = COL_TILE

    def body(a_ref, b_ref, o_ref):
        o_ref[...] = lax.dot_general(a_ref[...], b_ref[...], (((1,), (1,)), ((), ())),
                                     preferred_element_type=F32).astype(out_dtype)

    return pl.pallas_call(
        body, name=name, grid=(n // tn,),
        out_shape=jax.ShapeDtypeStruct((m, n), out_dtype),
        in_specs=[pl.BlockSpec((m, k), lambda i: (0, 0)), pl.BlockSpec((tn, k), lambda i: (i, 0))],
        out_specs=pl.BlockSpec((m, tn), lambda i: (0, i)),
        compiler_params=_params(("parallel",)),
    )(a, b)


def _mix_bwd(dycat, co, proj, o_mix, g_conv, g_attn_pairs):
    t, dc = co.shape
    hp = o_mix.shape[0]
    da = hp * PAIR
    tb = ROW_TILE

    def body(dy_ref, co_ref, bg_ref, zc_ref, za_ref, om_ref, gc_ref, ga_ref,
             dcp_ref, dco_ref, do_ref, dl_ref, dgc_ref, dga_ref):
        first = pl.program_id(0) == 0
        cov = co_ref[...].astype(F32)
        bg = bg_ref[...].astype(F32)
        zc = zc_ref[...].astype(F32)
        p = bg * cov
        rc = lax.rsqrt(jnp.mean(p * p, axis=-1, keepdims=True) + EPS)
        nh = p * rc
        dyc = dy_ref[:, 0:dc].astype(F32)
        dn = dyc * _silu(zc)
        a = dn * gc_ref[...]
        dp = rc * (a - nh * jnp.mean(a * nh, axis=-1, keepdims=True))
        dcp_ref[:, 0:dc] = jnp.zeros((tb, dc), BF16)
        dcp_ref[:, dc:2 * dc] = (dp * cov).astype(BF16)
        dcp_ref[:, 2 * dc:3 * dc] = jnp.zeros((tb, dc), BF16)
        dcp_ref[:, 3 * dc:4 * dc] = (dyc * nh * gc_ref[...] * _silu_grad(zc)).astype(BF16)
        dcp_ref[:, 4 * dc:4 * dc + 3 * da] = jnp.zeros((tb, 3 * da), BF16)
        dco_ref[...] = dp * bg

        @pl.when(first)
        def _():
            dgc_ref[...] = jnp.zeros(dgc_ref.shape, F32)
            dga_ref[...] = jnp.zeros(dga_ref.shape, F32)

        dgc_ref[...] += jnp.sum(dn * nh, axis=0, keepdims=True)

        ssq = jnp.zeros((tb, 1), F32)
        for h in range(hp):
            o = om_ref[h]
            ssq = ssq + jnp.sum(o * o, axis=-1, keepdims=True)
        ra = lax.rsqrt(ssq * (1.0 / da) + EPS)
        dot_an = jnp.zeros((tb, 1), F32)
        for h in range(hp):
            nha = om_ref[h] * ra
            za = za_ref[:, h * PAIR:(h + 1) * PAIR].astype(F32)
            dya = dy_ref[:, dc + h * PAIR:dc + (h + 1) * PAIR].astype(F32)
            dna = dya * _silu(za)
            dza = (dya * nha * ga_ref[h] * _silu_grad(za)).astype(BF16)
            dcp_ref[:, 4 * dc + 3 * da + h * PAIR:4 * dc + 3 * da + (h + 1) * PAIR] = dza
            dga_ref[h] += jnp.sum(dna * nha, axis=0, keepdims=True)
            dot_an = dot_an + jnp.sum(dna * ga_ref[h] * nha, axis=-1, keepdims=True)
        mean_an = dot_an * (1.0 / da)
        first_head = lax.broadcasted_iota(jnp.int32, (tb, PAIR), 1) < HEAD_DIM
        for h in range(hp):
            o = om_ref[h]
            nha = o * ra
            za = za_ref[:, h * PAIR:(h + 1) * PAIR].astype(F32)
            dya = dy_ref[:, dc + h * PAIR:dc + (h + 1) * PAIR].astype(F32)
            aa = dya * _silu(za) * ga_ref[h]
            d_o = ra * (aa - nha * mean_an)
            do_ref[h] = d_o.astype(BF16)
            prod = d_o * o
            both = jnp.sum(prod, axis=-1, keepdims=True)
            head0 = jnp.sum(jnp.where(first_head, prod, 0.0), axis=-1, keepdims=True)
            dl_ref[h] = jnp.where(first_head, head0, both - head0)

    pair_spec = pl.BlockSpec((hp, tb, PAIR), lambda i: (0, i, 0))
    return pl.pallas_call(
        body, name="mix_bwd", grid=(t // tb,),
        out_shape=(jax.ShapeDtypeStruct((t, 4 * dc + 4 * da), BF16), jax.ShapeDtypeStruct((t, dc), F32),
                   jax.ShapeDtypeStruct((hp, t, PAIR), BF16), jax.ShapeDtypeStruct((hp, t, PAIR), F32),
                   jax.ShapeDtypeStruct((1, dc), F32), jax.ShapeDtypeStruct((hp, 1, PAIR), F32)),
        in_specs=[pl.BlockSpec((tb, dc + da), lambda i: (i, 0)),
                  pl.BlockSpec((tb, dc), lambda i: (i, 0)),
                  pl.BlockSpec((tb, dc), lambda i: (i, 1)),
                  pl.BlockSpec((tb, dc), lambda i: (i, 3)),
                  pl.BlockSpec((tb, da), lambda i: (i, 7)),
                  pair_spec,
                  pl.BlockSpec((1, dc), lambda i: (0, 0)),
                  pl.BlockSpec((hp, 1, PAIR), lambda i: (0, 0, 0))],
        out_specs=(pl.BlockSpec((tb, 4 * dc + 4 * da), lambda i: (i, 0)), pl.BlockSpec((tb, dc), lambda i: (i, 0)),
                   pair_spec, pair_spec,
                   pl.BlockSpec((1, dc), lambda i: (0, 0)), pl.BlockSpec((hp, 1, PAIR), lambda i: (0, 0, 0))),
        compiler_params=_params(("arbitrary",)),
    )(dycat, co, proj, proj, proj, o_mix, g_conv, g_attn_pairs)


def _conv_bwd(dconv_proj, dco, conv_proj, conv_w, dc, after):
    t = dco.shape[0]
    ct = CONV_TILE
    nct = dc // ct

    def body(dcp_in_ref, dco_ref, u_ref, cg_ref, w_ref, after_ref, dcp_ref, acc_ref):
        del dcp_in_ref, after_ref
        which = pl.program_id(1)
        g = dco_ref[...]
        u = u_ref[...].astype(F32)
        cg = cg_ref[...].astype(F32)
        g_prev, g_next = _shift_rows(g, t)
        da = w_ref[0:1, :] * g_next + w_ref[1:2, :] * g + w_ref[2:3, :] * g_prev
        dcp_ref[...] = (da * jnp.where(which == 0, cg, u)).astype(BF16)
        a = cg * u
        a_prev, a_next = _shift_rows(a, t)
        acc_ref[...] = jnp.concatenate(
            [jnp.sum(g * a_prev, axis=0, keepdims=True), jnp.sum(g * a, axis=0, keepdims=True),
             jnp.sum(g * a_next, axis=0, keepdims=True), jnp.sum(g, axis=0, keepdims=True),
             jnp.zeros((4, ct), F32)], axis=0)

    return pl.pallas_call(
        body, name="conv_bwd", grid=(nct, 2),
        out_shape=(jax.ShapeDtypeStruct(dconv_proj.shape, BF16), jax.ShapeDtypeStruct((8, dc), F32)),
        in_specs=[HBM,
                  pl.BlockSpec((t, ct), lambda i, s: (0, i)),
                  pl.BlockSpec((t, ct), lambda i, s: (0, i)),
                  pl.BlockSpec((t, ct), lambda i, s: (0, 2 * nct + i)),
                  pl.BlockSpec((3, ct), lambda i, s: (0, i)), ANY],
        out_specs=(pl.BlockSpec((t, ct), lambda i, s: (0, 2 * s * nct + i)),
                   pl.BlockSpec((8, ct), lambda i, s: (0, i))),
        input_output_aliases={0: 0},
        compiler_params=_params(("arbitrary", "arbitrary")),
    )(dconv_proj, dco, conv_proj, conv_proj, conv_w, after)


def _dh(dproj, winf, after):
    t = dproj.shape[0]
    _, d, ws = winf.shape
    tm = tn = COL_TILE
    nt = (((1,), (1,)), ((), ()))

    def body(a_ref, w_ref, after_ref, o_ref):
        del after_ref
        acc = lax.dot_general(a_ref[:, 0:ws], w_ref[0], nt, preferred_element_type=F32)
        for j in range(1, N_CHIPS):
            acc = acc + lax.dot_general(a_ref[:, j * ws:(j + 1) * ws], w_ref[j], nt, preferred_element_type=F32)
        o_ref[...] = acc.astype(BF16)

    return pl.pallas_call(
        body, name="dh", grid=(d // tn, t // tm),
        out_shape=jax.ShapeDtypeStruct((t, d), BF16),
        in_specs=[pl.BlockSpec((tm, N_CHIPS * ws), lambda n, m: (m, 0)),
                  pl.BlockSpec((N_CHIPS, tn, ws), lambda n, m: (0, n, 0)), ANY],
        out_specs=pl.BlockSpec((tm, tn), lambda n, m: (m, n)),
        compiler_params=_params(("parallel", "parallel")),
    )(dproj, winf, after)


def _prenorm_bwd(x, dh, dout, mod, g_pre):
    t, d = x.shape
    tb = ROW_TILE

    def body(x_ref, dh_ref, dout_ref, mod_ref, g_ref, gx_ref, acc_ref):
        xv = x_ref[...]
        dhv = dh_ref[...].astype(F32)
        r = lax.rsqrt(jnp.mean(xv * xv, axis=-1, keepdims=True) + EPS)
        xh = xv * r
        one_scale = 1.0 + mod_ref[1:2, :]
        a = dhv * one_scale * g_ref[...]
        gx_ref[...] = dout_ref[...].astype(F32) + r * (a - xh * jnp.mean(a * xh, axis=-1, keepdims=True))
        part = jnp.concatenate(
            [jnp.sum(dhv, axis=0, keepdims=True), jnp.sum(dhv * xh * g_ref[...], axis=0, keepdims=True),
             jnp.sum(dhv * xh * one_scale, axis=0, keepdims=True), jnp.zeros((5, d), F32)], axis=0)

        @pl.when(pl.program_id(0) == 0)
        def _():
            acc_ref[...] = jnp.zeros(acc_ref.shape, F32)

        acc_ref[...] += part

    return pl.pallas_call(
        body, name="prenorm_bwd", grid=(t // tb,),
        out_shape=(jax.ShapeDtypeStruct((t, d), F32), jax.ShapeDtypeStruct((8, d), F32)),
        in_specs=[pl.BlockSpec((tb, d), lambda i: (i, 0)), pl.BlockSpec((tb, d), lambda i: (i, 0)),
                  pl.BlockSpec((tb, d), lambda i: (i, 0)), pl.BlockSpec((3, d), lambda i: (0, 0)),
                  pl.BlockSpec((1, d), lambda i: (0, 0))],
        out_specs=(pl.BlockSpec((tb, d), lambda i: (i, 0)), pl.BlockSpec((8, d), lambda i: (0, 0))),
        compiler_params=_params(("arbitrary",)),
    )(x, dh, dout, mod, g_pre)


def _chip_sums(mine, rsib, name, part=0, parts=1, after=()):
    _, half, cols = mine.shape
    rows = half // parts
    tr = min(rows, ROW_TILE)
    nt = rows // tr

    def body(g_ref, r_ref, *rest):
        rest[-1][...] = (g_ref[...].astype(F32) + r_ref[...].astype(F32)).astype(BF16)

    spec = pl.BlockSpec((None, tr, cols), lambda j, i: (j, part * nt + i, 0))
    return pl.pallas_call(
        body, name=name, grid=(N_CHIPS, nt),
        out_shape=jax.ShapeDtypeStruct((N_CHIPS, rows, cols), BF16),
        in_specs=[spec, spec] + [ANY] * len(after), out_specs=pl.BlockSpec((None, tr, cols), lambda j, i: (j, i, 0)),
        compiler_params=_params(("parallel", "parallel")),
    )(mine, rsib, *after)


def _owner_sum(place, mine, rsib, rici, name, part=0, parts=1, full=None):
    _, half, cols = mine.shape
    rows = half // parts
    tr = min(rows, ROW_TILE)
    nt = rows // tr

    def body(place_ref, *refs):
        del place_ref
        g_ref, r_ref, i_ref, o_ref = refs[-4:]
        acc = g_ref[...].astype(F32) + r_ref[...].astype(F32)
        for k in range(N_CHIPS - 1):
            acc = acc + i_ref[k].astype(F32)
        o_ref[...] = acc

    own = pl.BlockSpec((None, tr, cols), lambda i, p: (p[0], part * nt + i, 0))
    grid_spec = pltpu.PrefetchScalarGridSpec(
        num_scalar_prefetch=1, grid=(nt,),
        in_specs=([] if full is None else [HBM]) + [own, own, pl.BlockSpec((N_CHIPS - 1, tr, cols), lambda i, p: (0, i, 0))],
        out_specs=pl.BlockSpec((tr, cols), lambda i, p: (p[1] * (half // tr) + part * nt + i, 0)))
    return pl.pallas_call(
        body, name=name, grid_spec=grid_spec,
        out_shape=jax.ShapeDtypeStruct((2 * half, cols), F32),
        input_output_aliases={} if full is None else {1: 0},
        compiler_params=_params(("parallel",)),
    )(*([place] if full is None else [place, full]), mine, rsib, rici)


def _adam_math(w, g, m, v):
    m2 = ADAM_B1 * m + (1.0 - ADAM_B1) * g
    v2 = ADAM_B2 * v + (1.0 - ADAM_B2) * (g * g)
    m_hat = m2 / (1.0 - ADAM_B1 ** ADAM_STEP)
    v_hat = v2 / (1.0 - ADAM_B2 ** ADAM_STEP)
    delta = -ADAM_LR * (m_hat / (jnp.sqrt(v_hat) + ADAM_EPS) + ADAM_WD * w)
    return delta, m2, v2


def _adamw(w, g, m, v, name, part=0, parts=1, prev=None):
    rows, cols = w.shape
    tr = min(rows, ROW_TILE)

    def body(*refs):
        w_ref, g_ref, m_ref, v_ref, go_ref, d_ref, m2_ref, v2_ref = refs[-8:]
        g = g_ref[...]
        go_ref[...] = g
        d_ref[...], m2_ref[...], v2_ref[...] = _adam_math(w_ref[...], g, m_ref[...], v_ref[...])

    if parts == 1:
        grid, spec = (rows // tr,), pl.BlockSpec((tr, cols), lambda i: (i, 0))
    else:
        per_half = rows // 2 // tr
        nt = per_half // parts
        grid, spec = (2, nt), pl.BlockSpec((tr, cols), lambda r, i: (r * per_half + part * nt + i, 0))
    olds = [] if prev is None else list(prev)
    return pl.pallas_call(
        body, name=name, grid=grid,
        out_shape=(jax.ShapeDtypeStruct(w.shape, F32),) * 4,
        in_specs=[HBM] * len(olds) + [spec] * 4, out_specs=(spec,) * 4,
        input_output_aliases={i: i for i in range(len(olds))},
        compiler_params=_params(("parallel",) * len(grid)),
    )(*olds, w, g, m, v)


def _ada_grad_adamw(c_all_t, dmod_cols, w, m, v):
    d, wa = w.shape
    tr = ROW_TILE

    def body(ct_ref, dm_ref, w_ref, m_ref, v_ref, g_ref, d_ref, m2_ref, v2_ref):
        act = _silu(ct_ref[...])
        g = act[:, 0:1] * dm_ref[0:1, :]
        for b in range(1, N_DEV):
            g = g + act[:, b:b + 1] * dm_ref[b:b + 1, :]
        g_ref[...] = g
        d_ref[...], m2_ref[...], v2_ref[...] = _adam_math(w_ref[...], g, m_ref[...], v_ref[...])

    spec = pl.BlockSpec((tr, wa), lambda i: (i, 0))
    return pl.pallas_call(
        body, name="ada_grad_adamw", grid=(d // tr,),
        out_shape=(jax.ShapeDtypeStruct(w.shape, F32),) * 4,
        in_specs=[pl.BlockSpec((tr, N_DEV), lambda i: (i, 0)), pl.BlockSpec((N_DEV, wa), lambda i: (0, 0)),
                  spec, spec, spec],
        out_specs=(spec,) * 4,
        compiler_params=_params(("parallel",)),
    )(c_all_t, dmod_cols, w, m, v)


def _small_update(place, gathered, pieces, weights, moments_m, moments_v):
    n = gathered.shape[1]
    k = len(weights)

    def body(place_ref, g_ref, *refs):
        w_refs, m_refs, v_refs = refs[0:k], refs[k:2 * k], refs[2 * k:3 * k]
        outs = refs[3 * k:]
        total = g_ref[0:SUBLANES, :]
        for dev in range(1, N_DEV):
            total = total + g_ref[SUBLANES * dev:SUBLANES * (dev + 1), :]

        def flat(offset, length):
            segments, pos = [], offset
            while pos < offset + length:
                row, col = divmod(pos, n)
                take = min(offset + length - pos, n - col)
                segments.append(total[row:row + 1, col:col + take])
                pos += take
            return jnp.concatenate(segments, axis=1) if len(segments) > 1 else segments[0]

        chip = place_ref[0]
        for i, (w_ref, m_ref, v_ref) in enumerate(zip(w_refs, m_refs, v_refs)):
            g = flat(*pieces[i])
            if w_ref.ndim == 3:
                rows, cols = w_ref.shape[1:]
                full = pieces[i][1] // rows
                picked = []
                for r in range(rows):
                    blocks = [g[:, r * full + q * cols:r * full + (q + 1) * cols] for q in range(N_CHIPS)]
                    mine = blocks[N_CHIPS - 1]
                    for q in range(N_CHIPS - 2, -1, -1):
                        mine = jnp.where(chip == q, blocks[q], mine)
                    picked.append(mine)
                g = jnp.concatenate(picked, axis=0)
                w, m, v = w_ref[0], m_ref[0], v_ref[0]
            else:
                w, m, v = w_ref[...], m_ref[...], v_ref[...]
            delta, m2, v2 = _adam_math(w, g, m, v)
            for j, val in enumerate((g, delta, m2, v2)):
                out = outs[j * k + i]
                if w_ref.ndim == 3:
                    out[0] = val
                else:
                    out[...] = val
        outs[4 * k][...] = flat(*pieces[k])

    shapes = [jax.ShapeDtypeStruct(w.shape, F32) for w in weights]
    grid_spec = pltpu.PrefetchScalarGridSpec(
        num_scalar_prefetch=1, grid=(1,),
        in_specs=[pl.BlockSpec(gathered.shape, lambda i, p: (0, 0))]
        + [pl.BlockSpec(a.shape, functools.partial(lambda nd, i, p: (0,) * nd, a.ndim))
           for a in (*weights, *moments_m, *moments_v)],
        out_specs=tuple(pl.BlockSpec(s.shape, functools.partial(lambda nd, i, p: (0,) * nd, len(s.shape)))
                        for s in shapes * 4) + (pl.BlockSpec((1, LANES), lambda i, p: (0, 0)),))
    outs = pl.pallas_call(
        body, name="small_update", grid_spec=grid_spec,
        out_shape=tuple(shapes * 4) + (jax.ShapeDtypeStruct((1, LANES), F32),),
        compiler_params=_params(("arbitrary",)),
    )(place, gathered, *weights, *moments_m, *moments_v)
    return outs[0:k], outs[k:2 * k], outs[2 * k:3 * k], outs[3 * k:4 * k], outs[4 * k]


def _pack_small(pieces):
    flat = [p.reshape(-1).astype(F32) for p in pieces]
    offsets, total = [], 0
    for p in flat:
        offsets.append(total)
        total += p.shape[0]
    padded = -(-total // SMALL_ALIGN) * SMALL_ALIGN
    if padded > total:
        flat.append(jnp.zeros((padded - total,), F32))
    return jnp.concatenate(flat).reshape(8, padded // 8), offsets


def _alibi_slope_rows(n_heads):
    slopes = 2.0 ** (-8.0 * jnp.arange(1, n_heads + 1, dtype=F32) / n_heads)
    rows = jnp.zeros((n_heads // 2, 8), F32).at[:, 0:2].set(slopes.reshape(n_heads // 2, 2))
    return jnp.broadcast_to(rows[:, :, None], (n_heads // 2, 8, ATT_KW))


def kernel(x, c, w_ada, b_ada, g_pre, w_in, conv_w, conv_b, g_conv, g_attn, w_out, g_post, loss_target, m_w_ada, m_b_ada, m_g_pre, m_w_in, m_conv_w, m_conv_b, m_g_conv, m_g_attn, m_w_out, m_g_post, v_w_ada, v_b_ada, v_g_pre, v_w_in, v_conv_w, v_conv_b, v_g_conv, v_g_attn, v_w_out, v_g_post):
    t, d = x.shape[1], x.shape[2]
    dc = conv_b.shape[1]
    da = g_attn.shape[1]
    hp = da // PAIR
    ws = w_in.shape[2]
    wa = w_ada.shape[2]
    cws = conv_w.shape[2]
    assert t % ROW_TILE == 0 and d % ROW_TILE == 0 and dc % COL_TILE == 0 and da % COL_TILE == 0
    assert ws == 2 * dc and dc == da and t // BRANCHES[-1][1] >= ATT_BQ

    mx, my, mc = _my_place()
    chip = _chip_of(mx, my)
    dev = 2 * chip + mc
    place = jnp.stack([chip, mc]).astype(jnp.int32)

    x2, tgt2 = x[0], loss_target[0]
    w_ada2, w_in2, w_out2 = w_ada[0], w_in[0], w_out[0]

    win_slots = _cast_into_slot(place, w_in2, "cast_w_in")
    packed, offs = _pack_small([c[0], conv_w[0]])
    seen = _allgather8(packed, "gather_inputs", after=(win_slots,)).reshape(N_DEV, -1)
    c_all = seen[:, offs[0]:offs[0] + d]
    conv_w_full = seen[0::2, offs[1]:offs[1] + 3 * cws].reshape(N_CHIPS, 3, cws).transpose(1, 0, 2).reshape(3, dc)

    ada_part = _ada_partial(c_all, w_ada2)
    ada_seen = _allgather8(ada_part, "gather_ada").reshape(N_DEV, N_DEV, wa)
    mod_flat = lax.dynamic_index_in_dim(ada_seen[0::2], dev, axis=1, keepdims=False).reshape(1, 3 * d) + b_ada
    mod = mod_flat.reshape(3, d)

    win_flight, send_in, recv_in, started = _gather_start(win_slots, mod)

    y_chip, x_chip, d_chip = (_chip_of(mx, 1 - my), _chip_of(1 - mx, my), _chip_of(1 - mx, 1 - my))
    own_chunk, near_chunks, far_chunk = (jnp.stack(js).astype(jnp.int32) for js in ([chip], [y_chip, x_chip], [d_chip]))
    h, ht = _prenorm(x2, mod + started[0, 0], g_pre)
    proj = _proj_chunks(None, h, w_in2, own_chunk, "proj_own")
    win_flight, wout_flight, relay_send_in, relay_recv_in, send_out, recv_out = _gather_relay_in(
        win_flight, _cast_into_slot(place, w_out2, "cast_w_out"), recv_in, proj)
    win_flight = _forward_halves(
        _gather_wait_direct(win_flight, send_in, recv_in, proj, "gather_wait_w_in_direct"), (0, 1), "forward_w_in_direct")
    proj = _proj_chunks(proj, h, win_flight, near_chunks, "proj_neighbours")
    winf = _forward_halves(
        _gather_wait_relayed(win_flight, relay_send_in, relay_recv_in, proj, "gather_wait_w_in_relayed"),
        (2,), "forward_w_in_relayed")
    proj = _proj_chunks(proj, h, winf, far_chunk, "proj_diagonal")
    slopes = _alibi_slope_rows(da // HEAD_DIM)
    co = _conv_fwd(proj, conv_w_full, conv_b, dc)
    wout_flight, relay_send_out, relay_recv_out = _gather_relay_out(wout_flight, recv_out, co)
    o_mix, lse = _attn_fwd(proj, slopes, dc, da)
    g_attn_pairs = g_attn.reshape(hp, 1, PAIR)
    ycat, ycat_t = _mix_fwd(co, proj, o_mix, g_conv, g_attn_pairs)
    wout_flight = _gather_wait_direct(wout_flight, send_out, recv_out, ycat, "gather_wait_w_out_direct")
    wout_flight = _gather_wait_relayed(wout_flight, relay_send_out, relay_recv_out, ycat, "gather_wait_w_out_relayed")
    woutf = _forward_halves(wout_flight, (0, 1, 2), "forward_w_out").reshape(dc + da, d)
    dout, dy, post_sums = _out_fwd_bwd(ycat, woutf, x2, tgt2, mod, g_post)

    gout, rsib_out = _dw_swapped(ycat_t, dy, N_CHIPS, 1, "dw_out")
    csum_out = _chip_sums(gout, rsib_out, "rs_chip_sum_out")
    ssem_out, rsem_out, csum_out, land_out, sent_out = _owners_start(csum_out, "rs_owners_start_out")
    dycat = _matmul_nt(dy, woutf, BF16, "dycat")
    dproj, dco, d_o, delta, dg_conv, dg_attn = _mix_bwd(dycat, co, proj, o_mix, g_conv, g_attn_pairs)
    dproj, conv_sums = _conv_bwd(dproj, dco, proj, conv_w_full, dc, sent_out)
    dproj = _attn_bwd(dproj, proj, d_o, lse, delta, slopes, dc, da, sent_out)
    gin, rsib_in = _dw_swapped(ht, dproj, 1, N_CHIPS, "dw_in")
    ssem_in0, rsem_in0, csum_in0, land_in0, sent_in = _owners_start(
        _chip_sums(gin, rsib_in, "rs_chip_sum_in0", 0, 2), "rs_owners_start_in0")
    dh = _dh(dproj, winf, sent_in)
    grad_x, pre_sums = _prenorm_bwd(x2, dh, dout, mod, g_pre)

    small, so = _pack_small([
        pre_sums[0], pre_sums[1], post_sums[0],
        pre_sums[2], conv_sums[0:3], conv_sums[3], dg_conv, dg_attn, post_sums[1], post_sums[2, 0:128]])
    ssem_small, rsem_small, small, land_small, sent_small = _allgather8_start(small, dev, "gather_small_start")
    ssem_in1, rsem_in1, csum_in1, land_in1, sent_in1 = _owners_start(
        _chip_sums(gin, rsib_in, "rs_chip_sum_in1", 1, 2, after=(sent_small,)), "rs_owners_start_in1")

    rici_out = _owners_wait(ssem_out, rsem_out, csum_out, land_out, [grad_x, sent_in1], "rs_owners_wait_out")
    grad_w_out = _join_halves(_owner_sum(place, gout, rsib_out, rici_out, "rs_owner_sum_out"), "rs_join_halves_out")
    grad_w_out, delta_w_out, new_m_w_out, new_v_w_out = _adamw(
        w_out2, grad_w_out, m_w_out[0], v_w_out[0], "adamw_w_out")

    small_seen = _allgather8_wait(ssem_small, rsem_small, small, land_small, [delta_w_out], "gather_small_wait")
    small_w = [b_ada, g_pre, conv_w, conv_b, g_conv, g_attn, g_post]
    small_m = [m_b_ada, m_g_pre, m_conv_w, m_conv_b, m_g_conv, m_g_attn, m_g_post]
    small_v = [v_b_ada, v_g_pre, v_conv_w, v_conv_b, v_g_conv, v_g_attn, v_g_post]
    pieces = [(0, 3 * d), (so[3], d), (so[4], 3 * dc), (so[5], dc), (so[6], dc), (so[7], da), (so[8], d), (so[9], LANES)]
    g_small, d_small, m_small, v_small, loss_row = _small_update(place, small_seen, pieces, small_w, small_m, small_v)
    loss = loss_row[0, 0]
    grad_b_ada, grad_g_pre, grad_conv_w, grad_conv_b, grad_g_conv, grad_g_attn, grad_g_post = g_small

    dmod_cols = lax.dynamic_slice_in_dim(small_seen.reshape(N_DEV, -1), chip * wa, wa, axis=1)
    grad_w_ada, delta_w_ada, new_m_w_ada, new_v_w_ada = _ada_grad_adamw(c_all.T, dmod_cols, w_ada2, m_w_ada[0], v_w_ada[0])

    rici_in = _owners_wait(ssem_in0, rsem_in0, csum_in0, land_in0, [d_small[0], delta_w_out, delta_w_ada], "rs_owners_wait_in0")
    full_in = _join_halves(_owner_sum(place, gin, rsib_in, rici_in, "rs_owner_sum_in0", 0, 2), "rs_join_halves_in0", 0, 2)
    updated_in = _adamw(w_in2, full_in, m_w_in[0], v_w_in[0], "adamw_w_in0", 0, 2)
    rici_in = _owners_wait(ssem_in1, rsem_in1, csum_in1, land_in1, [updated_in[1]], "rs_owners_wait_in1")
    full_in = _join_halves(
        _owner_sum(place, gin, rsib_in, rici_in, "rs_owner_sum_in1", 1, 2, full_in), "rs_join_halves_in1", 1, 2)
    grad_w_in, delta_w_in, new_m_w_in, new_v_w_in = _adamw(
        w_in2, full_in, m_w_in[0], v_w_in[0], "adamw_w_in1", 1, 2, updated_in)

    def lead(a):
        return a.reshape((1,) + a.shape)

    grads = [lead(grad_w_ada), grad_b_ada, grad_g_pre, lead(grad_w_in), grad_conv_w, grad_conv_b, grad_g_conv,
             grad_g_attn, lead(grad_w_out), grad_g_post]
    deltas = [lead(delta_w_ada), d_small[0], d_small[1], lead(delta_w_in), d_small[2], d_small[3], d_small[4],
              d_small[5], lead(delta_w_out), d_small[6]]
    new_ms = [lead(new_m_w_ada), m_small[0], m_small[1], lead(new_m_w_in), m_small[2], m_small[3], m_small[4],
              m_small[5], lead(new_m_w_out), m_small[6]]
    new_vs = [lead(new_v_w_ada), v_small[0], v_small[1], lead(new_v_w_in), v_small[2], v_small[3], v_small[4],
              v_small[5], lead(new_v_w_out), v_small[6]]
    return (loss, lead(grad_x), *grads, *deltas, *new_ms, *new_vs)
```

```python
import functools

import jax
import jax.numpy as jnp
from jax import lax
from jax.experimental import pallas as pl
from jax.experimental.pallas import tpu as pltpu

F32 = jnp.float32
BF16 = jnp.bfloat16
MESH = pl.DeviceIdType.MESH
HBM = pl.BlockSpec(memory_space=pltpu.HBM)
VMEM = pl.BlockSpec(memory_space=pltpu.VMEM)
ANY = pl.BlockSpec(memory_space=pl.ANY)
SEM = pl.BlockSpec(memory_space=pltpu.SEMAPHORE)
EFFECT = pltpu.SideEffectType.DATAFLOW_SIDE_EFFECTING
SUBLANES, LANES = 8, 128
TOKEN = jax.ShapeDtypeStruct((SUBLANES, LANES), jnp.float32)

HEAD_DIM = 64
PAIR = 2 * HEAD_DIM
assert PAIR == LANES
BRANCHES = ((128, 1), (512, 4), (2048, 16))
SIDE = 64
EPS = 1e-6
NEG_INF = -1e30
N_CHIPS = 4
N_DEV = 8

ADAM_LR = 0.001
ADAM_B1 = 0.9
ADAM_B2 = 0.999
ADAM_EPS = 1e-08
ADAM_WD = 0.01
ADAM_STEP = 10

VMEM_LIMIT_BYTES = 56 * 1024 * 1024
ROW_TILE = 256
COL_TILE = 512
CONV_TILE = 256
ATT_BQ = 128
ATT_KW = ATT_BQ + 2 * SIDE
ATT_UNROLL = 4
SMALL_ALIGN = SUBLANES * LANES


def _params(semantics=None):
    kw = {"vmem_limit_bytes": VMEM_LIMIT_BYTES}
    if semantics is not None:
        kw["dimension_semantics"] = semantics
    return pltpu.CompilerParams(**kw)


def _silu(z):
    return z * jax.nn.sigmoid(z)


def _silu_grad(z):
    s = jax.nn.sigmoid(z)
    return s * (1.0 + z * (1.0 - s))


def _my_place():
    return lax.axis_index("x"), lax.axis_index("y"), lax.axis_index("c")


def _flip(a, bit):
    return 1 - a if bit else a


def _chip_of(x, y):
    return 2 * x + y


def _allgather8_start(v, me, name):
    rows_per, n = v.shape
    land = lax.dynamic_update_slice(jnp.zeros((N_DEV * rows_per, n), v.dtype), v, (me * rows_per, 0))

    def body(v_ref, land_ref, send_sems, recv_sems, v_thru, land_thru, token_ref):
        del v_thru, land_thru
        x, y, c = _my_place()
        mine = land_ref.at[pl.ds(pl.multiple_of((4 * x + 2 * y + c) * rows_per, rows_per), rows_per), :]
        for k in range(1, N_DEV):
            peer = (_flip(x, k & 4), _flip(y, k & 2), _flip(c, k & 1))
            pltpu.make_async_remote_copy(
                src_ref=v_ref, dst_ref=mine, send_sem=send_sems.at[k - 1], recv_sem=recv_sems.at[k - 1],
                device_id=peer, device_id_type=MESH).start()
        token_ref[...] = jnp.zeros(token_ref.shape, F32)

    sems = pltpu.SemaphoreType.DMA((N_DEV - 1,))
    return pl.pallas_call(
        body, name=name,
        out_shape=(sems, sems, jax.ShapeDtypeStruct(v.shape, v.dtype), jax.ShapeDtypeStruct(land.shape, land.dtype), TOKEN),
        in_specs=[HBM, HBM], out_specs=(SEM, SEM, HBM, HBM, VMEM),
        input_output_aliases={0: 2, 1: 3},
        compiler_params=pltpu.CompilerParams(has_side_effects=EFFECT),
    )(pltpu.with_memory_space_constraint(v, pltpu.HBM), pltpu.with_memory_space_constraint(land, pltpu.HBM))


def _allgather8_wait(send_sems, recv_sems, v, land, after, name):
    rows_per = v.shape[0]

    def body(v_ref, land_ref, send_ref, recv_ref, *rest):
        del rest
        x, y, c = _my_place()
        for k in range(1, N_DEV):
            peer = (_flip(x, k & 4), _flip(y, k & 2), _flip(c, k & 1))
            src = 4 * peer[0] + 2 * peer[1] + peer[2]
            cp = pltpu.make_async_remote_copy(
                src_ref=v_ref, dst_ref=land_ref.at[pl.ds(pl.multiple_of(src * rows_per, rows_per), rows_per), :],
                send_sem=send_ref.at[k - 1], recv_sem=recv_ref.at[k - 1], device_id=peer, device_id_type=MESH)
            cp.wait_send()
            cp.wait_recv()

    return pl.pallas_call(
        body, name=name,
        out_shape=(jax.ShapeDtypeStruct(v.shape, v.dtype), jax.ShapeDtypeStruct(land.shape, land.dtype)),
        in_specs=[HBM, HBM, SEM, SEM] + [ANY] * len(after), out_specs=(HBM, HBM),
        input_output_aliases={0: 0, 1: 1},
        compiler_params=pltpu.CompilerParams(has_side_effects=EFFECT),
    )(v, land, send_sems, recv_sems, *after)[1]


def _half_rows(ref, chip, which, half):
    return ref.at[chip, pl.ds(pl.multiple_of(which * half, half), half), :]


def _ici_peers(x, y, c):
    peers = [(_flip(x, k & 2), _flip(y, k & 1), c) for k in (1, 2, 3)]
    return [(peer, _chip_of(peer[0], peer[1])) for peer in peers]


def _part_rows(ref, chip, core, part):
    quarter = ref.shape[1] // 4
    return ref.at[chip, pl.ds(pl.multiple_of((2 * core + part) * quarter, quarter), quarter), :]


def _neighbours(x, y, c):
    return [((x, 1 - y, c), _chip_of(x, 1 - y)), ((1 - x, y, c), _chip_of(1 - x, y)),
            ((1 - x, 1 - y, c), _chip_of(1 - x, 1 - y))]


def _start_direct(buf, send_sems, recv_sems):
    x, y, c = _my_place()
    me = _chip_of(x, y)
    for n, (peer, _) in enumerate(_neighbours(x, y, c)[0:2]):
        for part in ((0, 1), (1, 0))[n]:
            piece = _part_rows(buf, me, c, part)
            pltpu.make_async_remote_copy(
                src_ref=piece, dst_ref=piece, send_sem=send_sems.at[2 * n + part], recv_sem=recv_sems.at[2 * n + part],
                device_id=peer, device_id_type=MESH).start()


def _relay(buf, recv_sems, relay_send, relay_recv):
    x, y, c = _my_place()
    nbrs = _neighbours(x, y, c)
    for n in range(2):
        part = n
        piece = _part_rows(buf, nbrs[n][1], c, part)
        pltpu.make_async_remote_copy(
            src_ref=piece, dst_ref=piece, send_sem=relay_send.at[part], recv_sem=recv_sems.at[2 * n + part],
            device_id=nbrs[n][0], device_id_type=MESH).wait_recv()
        pltpu.make_async_remote_copy(
            src_ref=piece, dst_ref=piece, send_sem=relay_send.at[part], recv_sem=relay_recv.at[part],
            device_id=nbrs[1 - n][0], device_id_type=MESH).start()


def _gather_start(win_slots, after):
    def body(win_in, after_ref, win_ref, send_sems, recv_sems, token_ref):
        del win_in, after_ref
        _start_direct(win_ref, send_sems, recv_sems)
        token_ref[...] = jnp.zeros(token_ref.shape, F32)

    sems = pltpu.SemaphoreType.DMA((4,))
    return pl.pallas_call(
        body, name="gather_start",
        out_shape=(jax.ShapeDtypeStruct(win_slots.shape, win_slots.dtype), sems, sems, TOKEN),
        in_specs=[HBM, ANY], out_specs=(HBM, SEM, SEM, VMEM),
        input_output_aliases={0: 0},
        compiler_params=pltpu.CompilerParams(has_side_effects=EFFECT),
    )(win_slots, after)


def _gather_relay_in(win, wout_slots, recv_in, after):
    def body(win_in, wout_in, recv_in_ref, after_ref, win_ref, wout_ref, relay_send, relay_recv, send_out, recv_out):
        del win_in, wout_in, after_ref
        _relay(win_ref, recv_in_ref, relay_send, relay_recv)
        _start_direct(wout_ref, send_out, recv_out)

    two, four = pltpu.SemaphoreType.DMA((2,)), pltpu.SemaphoreType.DMA((4,))
    return pl.pallas_call(
        body, name="gather_relay_w_in",
        out_shape=(jax.ShapeDtypeStruct(win.shape, win.dtype), jax.ShapeDtypeStruct(wout_slots.shape, wout_slots.dtype),
                   two, two, four, four),
        in_specs=[HBM, HBM, SEM, ANY], out_specs=(HBM, HBM, SEM, SEM, SEM, SEM),
        input_output_aliases={0: 0, 1: 1},
        compiler_params=pltpu.CompilerParams(has_side_effects=EFFECT),
    )(win, wout_slots, recv_in, after)


def _gather_relay_out(wout, recv_out, after):
    def body(wout_in, recv_out_ref, after_ref, wout_ref, relay_send, relay_recv):
        del wout_in, after_ref
        _relay(wout_ref, recv_out_ref, relay_send, relay_recv)

    two = pltpu.SemaphoreType.DMA((2,))
    return pl.pallas_call(
        body, name="gather_relay_w_out",
        out_shape=(jax.ShapeDtypeStruct(wout.shape, wout.dtype), two, two),
        in_specs=[HBM, SEM, ANY], out_specs=(HBM, SEM, SEM),
        input_output_aliases={0: 0},
        compiler_params=pltpu.CompilerParams(has_side_effects=EFFECT),
    )(wout, recv_out, after)


def _gather_wait_direct(buf, send_sems, recv_sems, after, name):
    def body(buf_in, send_ref, recv_ref, after_ref, buf_ref):
        del buf_in, after_ref
        x, y, c = _my_place()
        me = _chip_of(x, y)
        for n, (peer, chip) in enumerate(_neighbours(x, y, c)[0:2]):
            second = 1 - n
            pltpu.make_async_remote_copy(
                src_ref=_part_rows(buf_ref, me, c, second), dst_ref=_part_rows(buf_ref, chip, c, second),
                send_sem=send_ref.at[2 * n + second], recv_sem=recv_ref.at[2 * n + second],
                device_id=peer, device_id_type=MESH).wait_recv()
            for part in range(2):
                piece = _part_rows(buf_ref, me, c, part)
                pltpu.make_async_remote_copy(
                    src_ref=piece, dst_ref=piece, send_sem=send_ref.at[2 * n + part], recv_sem=recv_ref.at[2 * n + part],
                    device_id=peer, device_id_type=MESH).wait_send()

    return pl.pallas_call(
        body, name=name,
        out_shape=jax.ShapeDtypeStruct(buf.shape, buf.dtype),
        in_specs=[HBM, SEM, SEM, ANY], out_specs=HBM,
        input_output_aliases={0: 0},
        compiler_params=pltpu.CompilerParams(has_side_effects=EFFECT),
    )(buf, send_sems, recv_sems, after)


def _gather_wait_relayed(buf, relay_send, relay_recv, after, name):
    def body(buf_in, rsend_ref, rrecv_ref, after_ref, buf_ref):
        del buf_in, after_ref
        x, y, c = _my_place()
        nbrs = _neighbours(x, y, c)
        for n in range(2):
            relayed = _part_rows(buf_ref, nbrs[n][1], c, n)
            cp = pltpu.make_async_remote_copy(
                src_ref=relayed, dst_ref=_part_rows(buf_ref, nbrs[2][1], c, n),
                send_sem=rsend_ref.at[n], recv_sem=rrecv_ref.at[n], device_id=nbrs[1 - n][0], device_id_type=MESH)
            cp.wait_recv()
            cp.wait_send()

    return pl.pallas_call(
        body, name=name,
        out_shape=jax.ShapeDtypeStruct(buf.shape, buf.dtype),
        in_specs=[HBM, SEM, SEM, ANY], out_specs=HBM,
        input_output_aliases={0: 0},
        compiler_params=pltpu.CompilerParams(has_side_effects=EFFECT),
    )(buf, relay_send, relay_recv, after)


def _forward_halves(buf, which, name):
    half = buf.shape[1] // 2

    def body(buf_in, buf_ref, send_sems, recv_sems):
        del buf_in
        x, y, c = _my_place()
        sibling = (x, y, 1 - c)
        chips = [_neighbours(x, y, c)[n][1] for n in which]
        started = []
        for k, src_chip in enumerate(chips):
            landed = _half_rows(buf_ref, src_chip, c, half)
            fw = pltpu.make_async_remote_copy(
                src_ref=landed, dst_ref=landed, send_sem=send_sems.at[k], recv_sem=recv_sems.at[k],
                device_id=sibling, device_id_type=MESH)
            fw.start()
            started.append(fw)
        for k, src_chip in enumerate(chips):
            other = _half_rows(buf_ref, src_chip, 1 - c, half)
            pltpu.make_async_remote_copy(
                src_ref=other, dst_ref=other, send_sem=send_sems.at[k], recv_sem=recv_sems.at[k],
                device_id=sibling, device_id_type=MESH).wait_recv()
        for fw in started:
            fw.wait_send()

    return pl.pallas_call(
        body, name=name,
        out_shape=jax.ShapeDtypeStruct(buf.shape, buf.dtype),
        in_specs=[HBM], out_specs=HBM,
        input_output_aliases={0: 0},
        scratch_shapes=[pltpu.SemaphoreType.DMA((len(which),))] * 2,
    )(buf)


def _dw_swapped(a, b, row_chunks, col_chunks, name):
    r, t = a.shape
    c_all = b.shape[1]
    chunks = row_chunks * col_chunks
    rq, cq = r // row_chunks, c_all // col_chunks
    half = rq // 2
    tn = COL_TILE
    nt = cq // tn
    steps = col_chunks * nt

    def body(a_ref, b_ref, mine_ref, sib_ref, stage, send_sems, recv_sems):
        x, y, c = _my_place()
        j, n = pl.program_id(0), pl.program_id(1)
        step = j * nt + n
        slot = step % 2
        res = jnp.dot(a_ref[...], b_ref[...], preferred_element_type=F32).astype(BF16)

        def landing(jj, nn):
            cols = pl.ds(pl.multiple_of(nn * tn, tn), tn)
            return sib_ref.at[:, :, cols] if col_chunks == 1 else sib_ref.at[pl.ds(jj, 1), :, cols]

        def copy(slot_, step_, jj, nn):
            return pltpu.make_async_remote_copy(
                src_ref=stage.at[slot_], dst_ref=landing(jj, nn), send_sem=send_sems.at[slot_],
                recv_sem=recv_sems.at[step_], device_id=(x, y, 1 - c), device_id_type=MESH)

        @pl.when(step >= 2)
        def _():
            copy(slot, step, j, n).wait_send()

        for q in range(row_chunks):
            lo = res[q * rq:q * rq + half, :]
            hi = res[q * rq + half:(q + 1) * rq, :]
            mine_ref[q] = jnp.where(c == 0, lo, hi)
            stage[slot, q] = jnp.where(c == 0, hi, lo)
        copy(slot, step, j, n).start()

        @pl.when(step == steps - 1)
        def _():
            for s in range(max(steps - 2, 0), steps):
                copy(s % 2, s, j, n).wait_send()
            for s in range(steps):
                copy(s % 2, s, j, n).wait_recv()

    shape = jax.ShapeDtypeStruct((chunks, half, cq), BF16)
    return pl.pallas_call(
        body, name=name, grid=(col_chunks, nt),
        out_shape=(shape, shape),
        in_specs=[pl.BlockSpec((r, t), lambda j, n: (0, 0)), pl.BlockSpec((t, tn), lambda j, n: (0, j * nt + n))],
        out_specs=(pl.BlockSpec((row_chunks, half, tn), lambda j, n: (j, 0, n)), ANY),
        scratch_shapes=[pltpu.VMEM((2, row_chunks, half, tn), BF16), pltpu.SemaphoreType.DMA((2,)),
                        pltpu.SemaphoreType.DMA((steps,))],
        compiler_params=_params(("arbitrary", "arbitrary")),
    )(a, b)


def _owners_start(csum, name, after=()):
    land = pltpu.with_memory_space_constraint(lax.empty((N_CHIPS - 1,) + csum.shape[1:], csum.dtype), pltpu.HBM)

    def body(csum_ref, land_ref, *rest):
        send_sems, recv_sems, _, _, token_ref = rest[len(after):]
        x, y, c = _my_place()
        for k, (peer, owner) in enumerate(_ici_peers(x, y, c)):
            pltpu.make_async_remote_copy(
                src_ref=csum_ref.at[owner], dst_ref=land_ref.at[k], send_sem=send_sems.at[k], recv_sem=recv_sems.at[k],
                device_id=peer, device_id_type=MESH).start()
        token_ref[...] = jnp.zeros(token_ref.shape, F32)

    sems = pltpu.SemaphoreType.DMA((N_CHIPS - 1,))
    return pl.pallas_call(
        body, name=name,
        out_shape=(sems, sems, jax.ShapeDtypeStruct(csum.shape, csum.dtype),
                   jax.ShapeDtypeStruct(land.shape, land.dtype), TOKEN),
        in_specs=[HBM, HBM] + [ANY] * len(after), out_specs=(SEM, SEM, HBM, HBM, VMEM),
        input_output_aliases={0: 2, 1: 3},
        compiler_params=pltpu.CompilerParams(has_side_effects=EFFECT),
    )(pltpu.with_memory_space_constraint(csum, pltpu.HBM), land, *after)


def _owners_wait(send_sems, recv_sems, csum, land, after, name):
    def body(csum_ref, land_ref, send_ref, recv_ref, *rest):
        del rest
        x, y, c = _my_place()
        for k, (peer, owner) in enumerate(_ici_peers(x, y, c)):
            cp = pltpu.make_async_remote_copy(
                src_ref=csum_ref.at[owner], dst_ref=land_ref.at[k], send_sem=send_ref.at[k], recv_sem=recv_ref.at[k],
                device_id=peer, device_id_type=MESH)
            cp.wait_send()
            cp.wait_recv()

    return pl.pallas_call(
        body, name=name,
        out_shape=(jax.ShapeDtypeStruct(csum.shape, csum.dtype), jax.ShapeDtypeStruct(land.shape, land.dtype)),
        in_specs=[HBM, HBM, SEM, SEM] + [ANY] * len(after), out_specs=(HBM, HBM),
        input_output_aliases={0: 0, 1: 1},
        compiler_params=pltpu.CompilerParams(has_side_effects=EFFECT),
    )(csum, land, send_sems, recv_sems, *after)[1]


def _join_halves(full, name, part=0, parts=1):
    half = full.shape[0] // 2
    rows = half // parts

    def body(full_in, full_ref, send_sem, recv_sem):
        del full_in
        x, y, c = _my_place()
        sibling = (x, y, 1 - c)
        mine = full_ref.at[pl.ds(pl.multiple_of(c * half + part * rows, rows), rows), :]
        theirs = full_ref.at[pl.ds(pl.multiple_of((1 - c) * half + part * rows, rows), rows), :]
        cp = pltpu.make_async_remote_copy(
            src_ref=mine, dst_ref=mine, send_sem=send_sem, recv_sem=recv_sem, device_id=sibling, device_id_type=MESH)
        cp.start()
        pltpu.make_async_remote_copy(
            src_ref=theirs, dst_ref=theirs, send_sem=send_sem, recv_sem=recv_sem,
            device_id=sibling, device_id_type=MESH).wait_recv()
        cp.wait_send()

    return pl.pallas_call(
        body, name=name,
        out_shape=jax.ShapeDtypeStruct(full.shape, full.dtype),
        in_specs=[HBM], out_specs=HBM,
        input_output_aliases={0: 0},
        scratch_shapes=[pltpu.SemaphoreType.DMA, pltpu.SemaphoreType.DMA],
    )(full)


def _cast_into_slot(place, w, name):
    rows, cols = w.shape
    tr = min(rows, ROW_TILE)

    def body(place_ref, w_ref, o_ref):
        del place_ref
        o_ref[...] = w_ref[...].astype(BF16)

    grid_spec = pltpu.PrefetchScalarGridSpec(
        num_scalar_prefetch=1, grid=(rows // tr,),
        in_specs=[pl.BlockSpec((tr, cols), lambda i, p: (i, 0))],
        out_specs=pl.BlockSpec((None, tr, cols), lambda i, p: (p[0], i, 0)))
    return pl.pallas_call(
        body, name=name, grid_spec=grid_spec,
        out_shape=jax.ShapeDtypeStruct((N_CHIPS, rows, cols), BF16),
        compiler_params=_params(("parallel",)),
    )(place, w)


def _ada_modulation(packed, w_ada, b_ada, d, after=()):
    rows_per, n = packed.shape
    d_model, wa = w_ada.shape

    def body(v_ref, w_hbm, b_ref, *rest):
        all_ref, mod_ref, w_vmem, part_ref, parts_ref, load_sem, send1, recv1, send2, recv2 = rest[len(after):]
        x, y, c = _my_place()
        me = 4 * x + 2 * y + c
        chip = _chip_of(x, y)
        load = pltpu.make_async_copy(w_hbm, w_vmem, load_sem)
        load.start()

        def rows(idx):
            return all_ref.at[pl.ds(pl.multiple_of(idx * rows_per, rows_per), rows_per), :]

        all_ref[pl.ds(pl.multiple_of(me * rows_per, rows_per), rows_per), :] = v_ref[...]
        copies = []
        for k in range(1, N_DEV):
            peer = (_flip(x, k & 4), _flip(y, k & 2), _flip(c, k & 1))
            cp = pltpu.make_async_remote_copy(
                src_ref=v_ref, dst_ref=rows(me), send_sem=send1.at[k - 1], recv_sem=recv1.at[k - 1],
                device_id=peer, device_id_type=MESH)
            cp.start()
            copies.append((cp, peer))
        for k, (cp, peer) in enumerate(copies):
            pltpu.make_async_remote_copy(
                src_ref=v_ref, dst_ref=rows(4 * peer[0] + 2 * peer[1] + peer[2]), send_sem=send1.at[k],
                recv_sem=recv1.at[k], device_id=peer, device_id_type=MESH).wait_recv()
        for cp, _ in copies:
            cp.wait_send()

        def c_of(dev):
            segments, pos = [], 0
            while pos < d:
                row, col = divmod(pos, n)
                take = min(d - pos, n - col)
                segments.append(all_ref[dev * rows_per + row:dev * rows_per + row + 1, col:col + take])
                pos += take
            return jnp.concatenate(segments, axis=1)

        c_all = jnp.concatenate([c_of(dev) for dev in range(N_DEV)], axis=0)
        load.wait()
        part_ref[...] = jnp.dot(_silu(c_all), w_vmem[...], precision=lax.Precision.HIGHEST, preferred_element_type=F32)
        parts_ref[chip] = part_ref[...]
        swaps = []
        for k, (peer, _) in enumerate(_ici_peers(x, y, c)):
            cp = pltpu.make_async_remote_copy(
                src_ref=part_ref, dst_ref=parts_ref.at[chip], send_sem=send2.at[k], recv_sem=recv2.at[k],
                device_id=peer, device_id_type=MESH)
            cp.start()
            swaps.append(cp)
        for k, (peer, peer_chip) in enumerate(_ici_peers(x, y, c)):
            pltpu.make_async_remote_copy(
                src_ref=part_ref, dst_ref=parts_ref.at[peer_chip], send_sem=send2.at[k], recv_sem=recv2.at[k],
                device_id=peer, device_id_type=MESH).wait_recv()
        for cp in swaps:
            cp.wait_send()
        flat = jnp.concatenate([parts_ref[j, pl.ds(me, 1), :] for j in range(N_CHIPS)], axis=1) + b_ref[...]
        mod_ref[...] = jnp.concatenate([flat[:, i * d:(i + 1) * d] for i in range(3)], axis=0)

    return pl.pallas_call(
        body, name="ada_modulation",
        out_shape=(jax.ShapeDtypeStruct((N_DEV * rows_per, n), F32), jax.ShapeDtypeStruct((3, d), F32)),
        in_specs=[VMEM, ANY, VMEM] + [ANY] * len(after), out_specs=(VMEM, VMEM),
        scratch_shapes=[pltpu.VMEM((d_model, wa), F32), pltpu.VMEM((N_DEV, wa), F32),
                        pltpu.VMEM((N_CHIPS, N_DEV, wa), F32), pltpu.SemaphoreType.DMA,
                        pltpu.SemaphoreType.DMA((N_DEV - 1,)), pltpu.SemaphoreType.DMA((N_DEV - 1,)),
                        pltpu.SemaphoreType.DMA((N_CHIPS - 1,)), pltpu.SemaphoreType.DMA((N_CHIPS - 1,))],
        compiler_params=_params(),
    )(packed, w_ada, b_ada, *after)


def _prenorm(x, mod, g_pre):
    t, d = x.shape
    tb = ROW_TILE

    def body(x_ref, mod_ref, g_ref, h_ref, ht_ref):
        xv = x_ref[...]
        r = lax.rsqrt(jnp.mean(xv * xv, axis=-1, keepdims=True) + EPS)
        h = (xv * r) * g_ref[...] * (1.0 + mod_ref[1:2, :]) + mod_ref[0:1, :]
        h_ref[...] = h.astype(BF16)
        ht_ref[...] = h.T.astype(BF16)

    return pl.pallas_call(
        body, name="prenorm", grid=(t // tb,),
        out_shape=(jax.ShapeDtypeStruct((t, d), BF16), jax.ShapeDtypeStruct((d, t), BF16)),
        in_specs=[pl.BlockSpec((tb, d), lambda i: (i, 0)), pl.BlockSpec((3, d), lambda i: (0, 0)),
                  pl.BlockSpec((1, d), lambda i: (0, 0))],
        out_specs=(pl.BlockSpec((tb, d), lambda i: (i, 0)), pl.BlockSpec((d, tb), lambda i: (0, i))),
        compiler_params=_params(("parallel",)),
    )(x, mod, g_pre)


def _proj_chunks(proj, h, w, chunks, name):
    t, d = h.shape
    ws = w.shape[-1]
    tn = COL_TILE
    nt = ws // tn

    def body(chunk_ref, *refs):
        del chunk_ref
        a_ref, b_ref, o_ref = refs[-3:]
        o_ref[...] = jnp.dot(a_ref[...], b_ref[...].astype(BF16), preferred_element_type=F32).astype(BF16)

    if w.ndim == 3:
        w_spec = pl.BlockSpec((None, d, tn), lambda i, n, ch: (ch[i], 0, n))
    else:
        w_spec = pl.BlockSpec((d, tn), lambda i, n, ch: (0, n))
    first = proj is None
    grid_spec = pltpu.PrefetchScalarGridSpec(
        num_scalar_prefetch=1, grid=(chunks.shape[0], nt),
        in_specs=([] if first else [HBM]) + [pl.BlockSpec((t, d), lambda i, n, ch: (0, 0)), w_spec],
        out_specs=pl.BlockSpec((t, tn), lambda i, n, ch: (0, ch[i] * nt + n)))
    return pl.pallas_call(
        body, name=name, grid_spec=grid_spec,
        out_shape=jax.ShapeDtypeStruct((t, N_CHIPS * ws), BF16),
        input_output_aliases={} if first else {1: 0},
        compiler_params=_params(("parallel", "parallel")),
    )(*([chunks] if first else [chunks, proj]), h, w)


def _shift_rows(a, rows):
    idx = lax.broadcasted_iota(jnp.int32, a.shape, 0)
    prev = jnp.where(idx == 0, 0.0, pltpu.roll(a, 1, 0))
    nxt = jnp.where(idx == rows - 1, 0.0, pltpu.roll(a, rows - 1, 0))
    return prev, nxt


def _conv_fwd(conv_proj, conv_w, conv_b, dc):
    t = conv_proj.shape[0]
    ct = CONV_TILE
    nct = dc // ct

    def body(u_ref, cg_ref, w_ref, b_ref, co_ref):
        a = cg_ref[...].astype(F32) * u_ref[...].astype(F32)
        prev, nxt = _shift_rows(a, t)
        co_ref[...] = (w_ref[0:1, :] * prev + w_ref[1:2, :] * a + w_ref[2:3, :] * nxt + b_ref[...]).astype(BF16)

    return pl.pallas_call(
        body, name="conv_fwd", grid=(nct,),
        out_shape=jax.ShapeDtypeStruct((t, dc), BF16),
        in_specs=[pl.BlockSpec((t, ct), lambda i: (0, i)), pl.BlockSpec((t, ct), lambda i: (0, 2 * nct + i)),
                  pl.BlockSpec((3, ct), lambda i: (0, i)), pl.BlockSpec((1, ct), lambda i: (0, i))],
        out_specs=pl.BlockSpec((t, ct), lambda i: (0, i)),
        compiler_params=_params(("parallel",)),
    )(conv_proj, conv_proj, conv_w, conv_b)


def _to_residue_major(src_ref, dst_ref, r):
    seq = src_ref.shape[0] // r
    for res in range(r):
        dst_ref[res * seq:(res + 1) * seq, :] = src_ref[pl.ds(res, seq, stride=r), :].astype(dst_ref.dtype)


def _branch_operands(token_refs, stage, dil, r):
    if r == 1:
        return list(token_refs)
    for i, ref in enumerate(token_refs):
        stage[...] = ref[...].astype(F32)
        _to_residue_major(stage, dil.at[i], r)
    return [dil.at[i] for i in range(len(token_refs))]


def _scaled_queries(q):
    return (q.astype(F32) * (HEAD_DIM ** -0.5)).astype(BF16)


BLOCK_SHIFTS = (0, -SIDE, None)


def _band_bias(rel, slope):
    arel = jnp.abs(rel)
    return jnp.where(arel <= SIDE, arel.astype(F32) * slope, NEG_INF)


def _fill_bias_tiles(bias_ref, sl_ref, r, kw):
    base = lax.broadcasted_iota(jnp.int32, (ATT_BQ, kw), 1) - lax.broadcasted_iota(jnp.int32, (ATT_BQ, kw), 0)
    for hh in range(2):
        slope = -(sl_ref[hh:hh + 1, 0:kw] * float(r))
        for e, shift in enumerate(BLOCK_SHIFTS):
            shift = ATT_BQ - kw if shift is None else shift
            bias_ref[hh, e, :, 0:kw] = _band_bias(base + shift, slope)


def _fill_stacked_bias_tiles(bias_ref, sl_ref, r, kw):
    base = lax.broadcasted_iota(jnp.int32, (kw, ATT_BQ), 0) - lax.broadcasted_iota(jnp.int32, (kw, ATT_BQ), 1)
    for hh in range(2):
        slope = -(sl_ref[hh:hh + 1, 0:ATT_BQ] * float(r))
        for e, shift in enumerate(BLOCK_SHIFTS):
            shift = ATT_BQ - kw if shift is None else shift
            bias_ref[e, 0:kw, hh * ATT_BQ:(hh + 1) * ATT_BQ] = _band_bias(base + shift, slope)


def _first_head_lanes():
    return lax.broadcasted_iota(jnp.int32, (1, PAIR), 1) < HEAD_DIM


def _only_head(x, first, hh):
    return jnp.where(first if hh == 0 else jnp.logical_not(first), x, jnp.zeros_like(x))


def _block_place(g, seq_len, kw):
    nqb = seq_len // ATT_BQ
    if nqb == 1:
        row = pl.multiple_of(g * ATT_BQ, ATT_BQ)
        return row, row, 0
    res = g // nqb
    qb = g - res * nqb
    q0 = qb * ATT_BQ
    ks = jnp.clip(q0 - SIDE, 0, seq_len - kw)
    edge = jnp.where(qb == 0, 0, jnp.where(qb == nqb - 1, 2, 1))
    return (pl.multiple_of(res * seq_len + q0, ATT_BQ), pl.multiple_of(res * seq_len + ks, SIDE), edge)


def _qkv_specs(dc, da, t, index):
    return [pl.BlockSpec((t, PAIR), functools.partial(index, (4 * dc + comp * da) // PAIR)) for comp in range(3)]


def _attn_fwd(proj, slopes, dc, da):
    t = proj.shape[0]
    hp = da // PAIR
    n_blocks = t // ATT_BQ

    def body(q_ref, k_ref, v_ref, sl_ref, o_ref, lse_ref, stage, dil, bias, o_res, l_res, o_tok, l_tok):
        for b, (_, r) in enumerate(BRANCHES):
            seq_len = t // r
            kw = min(ATT_KW, seq_len)
            ops = _branch_operands([q_ref, k_ref, v_ref], stage, dil, r)
            _fill_bias_tiles(bias, sl_ref, r, kw)
            o_dst, l_dst = (o_tok.at[b], l_tok.at[b]) if r == 1 else (o_res, l_res)
            first = _first_head_lanes()

            def blocks(trip, carry, seq_len=seq_len, kw=kw, o_dst=o_dst, l_dst=l_dst, first=first, ops=ops):
                nt = (((1,), (1,)), ((), ()))
                places = [_block_place(trip * ATT_UNROLL + i, seq_len, kw) for i in range(ATT_UNROLL)]
                chains = [(i, hh) for i in range(ATT_UNROLL) for hh in range(2)]
                qs = [_scaled_queries(ops[0][pl.ds(qrow, ATT_BQ), :]) for qrow, _, _ in places]
                ks = [ops[1][pl.ds(krow, kw), :] for _, krow, _ in places]
                vs = [ops[2][pl.ds(krow, kw), :] for _, krow, _ in places]
                ss = [lax.dot_general(_only_head(qs[i], first, hh), ks[i], nt, preferred_element_type=F32)
                      + bias[hh, places[i][2], :, 0:kw] for i, hh in chains]
                tops = [jnp.max(s, axis=-1, keepdims=True) for s in ss]
                ps = [jnp.exp(s - m) for s, m in zip(ss, tops)]
                dens = [jnp.sum(p, axis=-1, keepdims=True) for p in ps]
                for i, (qrow, _, _) in enumerate(places):
                    weights = jnp.concatenate([ps[2 * i].astype(BF16), ps[2 * i + 1].astype(BF16)], axis=1)
                    values = jnp.concatenate([_only_head(vs[i], first, 0), _only_head(vs[i], first, 1)], axis=0)
                    den = jnp.where(first, dens[2 * i], dens[2 * i + 1])
                    o_dst[pl.ds(qrow, ATT_BQ), :] = jnp.dot(weights, values, preferred_element_type=F32) / den
                    l_dst[pl.ds(qrow, ATT_BQ), :] = jnp.where(first, tops[2 * i], tops[2 * i + 1]) + jnp.log(den)
                return carry

            lax.fori_loop(0, n_blocks // ATT_UNROLL, blocks, 0)
            if r > 1:
                for res in range(r):
                    rows = slice(res * seq_len, (res + 1) * seq_len)
                    o_tok[b, pl.ds(res, seq_len, stride=r), :] = o_res[rows, :]
                    l_tok[b, pl.ds(res, seq_len, stride=r), :] = l_res[rows, :]

        def merge(i, carry):
            rows = pl.ds(pl.multiple_of(i * ROW_TILE, ROW_TILE), ROW_TILE)
            la, lb, lc = l_tok[0, rows, :], l_tok[1, rows, :], l_tok[2, rows, :]
            m = jnp.maximum(jnp.maximum(la, lb), lc)
            wa, wb, wc = jnp.exp(la - m), jnp.exp(lb - m), jnp.exp(lc - m)
            den = wa + wb + wc
            o_ref[rows, :] = (wa * o_tok[0, rows, :] + wb * o_tok[1, rows, :] + wc * o_tok[2, rows, :]) * (1.0 / den)
            lse_ref[rows, :] = m + jnp.log(den)
            return carry

        lax.fori_loop(0, t // ROW_TILE, merge, 0)

    pair_spec = pl.BlockSpec((None, t, PAIR), lambda h: (h, 0, 0))
    return pl.pallas_call(
        body, name="attn_fwd", grid=(hp,),
        out_shape=(jax.ShapeDtypeStruct((hp, t, PAIR), F32), jax.ShapeDtypeStruct((hp, t, PAIR), F32)),
        in_specs=_qkv_specs(dc, da, t, lambda first, h: (0, first + h))
        + [pl.BlockSpec((None, 8, ATT_KW), lambda h: (h, 0, 0))],
        out_specs=(pair_spec, pair_spec),
        scratch_shapes=[pltpu.VMEM((t, PAIR), F32), pltpu.VMEM((3, t, PAIR), BF16),
                        pltpu.VMEM((2, 3, ATT_BQ, ATT_KW), F32),
                        pltpu.VMEM((t, PAIR), F32), pltpu.VMEM((t, PAIR), F32),
                        pltpu.VMEM((3, t, PAIR), F32), pltpu.VMEM((3, t, PAIR), F32)],
        compiler_params=_params(("parallel",)),
    )(proj, proj, proj, slopes)


def _attn_bwd(dproj, proj, d_o, lse, delta, slopes, dc, da, after):
    t = proj.shape[0]
    hp = da // PAIR
    n_blocks = t // ATT_BQ

    def all_branches(q_ref, k_ref, v_ref, do_ref, lse_ref, dl_ref, sl_ref,
                     stage, dil, packed, packed_res, row_vecs, bias_t, acc, tot):
        first = _first_head_lanes()
        lane = lax.broadcasted_iota(jnp.int32, (1, PAIR), 1)
        packed[...] = jnp.where((lane & (HEAD_DIM - 1)) < HEAD_DIM // 2, lse_ref[...], dl_ref[...])
        for b, (_, r) in enumerate(BRANCHES):
            seq_len = t // r
            kw = min(ATT_KW, seq_len)
            ops = _branch_operands([q_ref, k_ref, v_ref, do_ref], stage, dil, r)
            scalars = packed
            if r > 1:
                _to_residue_major(packed, packed_res, r)
                scalars = packed_res
            for g in range(n_blocks):
                flipped = scalars[g * ATT_BQ:(g + 1) * ATT_BQ, :].T
                for row in range(4):
                    row_vecs[g, row:row + 1, :] = flipped[row * (HEAD_DIM // 2):row * (HEAD_DIM // 2) + 1, :]
            _fill_stacked_bias_tiles(bias_t, sl_ref, r, kw)
            acc[1] = jnp.zeros((t, PAIR), F32)
            acc[2] = jnp.zeros((t, PAIR), F32)

            def blocks(trip, carry, seq_len=seq_len, kw=kw, ops=ops):
                nt = (((1,), (1,)), ((), ()))
                group = range(ATT_UNROLL)
                places = [_block_place(trip * ATT_UNROLL + i, seq_len, kw) for i in group]
                ks, vs, q2s, do2s, lse2s, dl2s = [], [], [], [], [], []
                for i, (qrow, krow, _) in zip(group, places):
                    q = _scaled_queries(ops[0][pl.ds(qrow, ATT_BQ), :])
                    dov = ops[3][pl.ds(qrow, ATT_BQ), :]
                    ks.append(ops[1][pl.ds(krow, kw), :])
                    vs.append(ops[2][pl.ds(krow, kw), :])
                    q2s.append(jnp.concatenate([_only_head(q, first, 0), _only_head(q, first, 1)], axis=0))
                    do2s.append(jnp.concatenate([_only_head(dov, first, 0), _only_head(dov, first, 1)], axis=0))
                    rows = row_vecs[trip * ATT_UNROLL + i]
                    lse2s.append(jnp.concatenate([rows[0:1, :], rows[2:3, :]], axis=1))
                    dl2s.append(jnp.concatenate([rows[1:2, :], rows[3:4, :]], axis=1))
                s_ts = [lax.dot_general(ks[i], q2s[i], nt, preferred_element_type=F32) for i in group]
                dp_ts = [lax.dot_general(vs[i], do2s[i], nt, preferred_element_type=F32) for i in group]
                p_ts = [jnp.exp(s_ts[i] + bias_t[places[i][2], 0:kw, :] - lse2s[i]) for i in group]
                ds_ts = [p_ts[i] * (dp_ts[i] - dl2s[i]) for i in group]
                dvs = [jnp.dot(p_ts[i].astype(BF16), do2s[i], preferred_element_type=F32) for i in group]
                dks = [jnp.dot(ds_ts[i].astype(BF16), q2s[i], preferred_element_type=F32) for i in group]
                dss = [ds_ts[i].T.astype(BF16) for i in group]
                dqs = [jnp.dot(dss[i][0:ATT_BQ, :], _only_head(ks[i], first, 0), preferred_element_type=F32)
                       + jnp.dot(dss[i][ATT_BQ:2 * ATT_BQ, :], _only_head(ks[i], first, 1), preferred_element_type=F32)
                       for i in group]
                for i, (qrow, krow, _) in zip(group, places):
                    acc[0, pl.ds(qrow, ATT_BQ), :] = dqs[i] * (HEAD_DIM ** -0.5)
                    acc[1, pl.ds(krow, kw), :] += dks[i]
                    acc[2, pl.ds(krow, kw), :] += dvs[i]
                return carry

            lax.fori_loop(0, n_blocks // ATT_UNROLL, blocks, 0)
            for comp in range(3):
                if r == 1:
                    tot[comp] = acc[comp]
                else:
                    for res in range(r):
                        tok = pl.ds(res, seq_len, stride=r)
                        tot[comp, tok, :] = tot[comp, tok, :] + acc[comp, res * seq_len:(res + 1) * seq_len, :]

    first_q = (4 * dc) // PAIR

    def body(dproj_in, q_ref, k_ref, v_ref, do_ref, lse_ref, dl_ref, sl_ref, after_ref, out_ref, *scratch):
        del dproj_in, after_ref
        work, out_stage, out_sems = scratch[:-2], scratch[-2], scratch[-1]
        h = pl.program_id(0)
        all_branches(q_ref, k_ref, v_ref, do_ref, lse_ref, dl_ref, sl_ref, *work)

        def out_copy(comp):
            cols = pl.ds(pl.multiple_of((first_q + comp * hp + h) * PAIR, PAIR), PAIR)
            return pltpu.make_async_copy(out_stage.at[comp], out_ref.at[:, cols], out_sems.at[comp])

        @pl.when(h > 0)
        def _():
            for comp in range(3):
                out_copy(comp).wait()

        for comp in range(3):
            out_stage[comp] = work[-1][comp].astype(BF16)
            out_copy(comp).start()

        @pl.when(h == hp - 1)
        def _():
            for comp in range(3):
                out_copy(comp).wait()

    pair_spec = pl.BlockSpec((None, t, PAIR), lambda h: (h, 0, 0))
    return pl.pallas_call(
        body, name="attn_bwd", grid=(hp,),
        out_shape=jax.ShapeDtypeStruct(dproj.shape, BF16),
        in_specs=[HBM] + _qkv_specs(dc, da, t, lambda first, h: (0, first + h))
        + [pair_spec, pair_spec, pair_spec, pl.BlockSpec((None, 8, ATT_KW), lambda h: (h, 0, 0)), ANY],
        out_specs=ANY,
        input_output_aliases={0: 0},
        scratch_shapes=[pltpu.VMEM((t, PAIR), F32), pltpu.VMEM((4, t, PAIR), BF16),
                        pltpu.VMEM((t, PAIR), F32), pltpu.VMEM((t, PAIR), F32),
                        pltpu.VMEM((n_blocks, 8, ATT_BQ), F32), pltpu.VMEM((3, ATT_KW, 2 * ATT_BQ), F32),
                        pltpu.VMEM((3, t, PAIR), F32), pltpu.VMEM((3, t, PAIR), F32),
                        pltpu.VMEM((3, t, PAIR), BF16), pltpu.SemaphoreType.DMA((3,))],
        compiler_params=_params(("arbitrary",)),
    )(dproj, proj, proj, proj, d_o, lse, delta, slopes, after)


def _mix_fwd(co, proj, o_mix, g_conv, g_attn_pairs):
    t, dc = co.shape
    hp = o_mix.shape[0]
    da = hp * PAIR
    tb = ROW_TILE

    def body(co_ref, bg_ref, zc_ref, za_ref, om_ref, gc_ref, ga_ref, ycat_ref, ycatt_ref):
        p = bg_ref[...].astype(F32) * co_ref[...].astype(F32)
        rc = lax.rsqrt(jnp.mean(p * p, axis=-1, keepdims=True) + EPS)
        yc = (p * rc) * gc_ref[...] * _silu(zc_ref[...].astype(F32))
        ycat_ref[:, 0:dc] = yc.astype(BF16)
        ycatt_ref[0:dc, :] = yc.T.astype(BF16)
        ssq = jnp.zeros((tb, 1), F32)
        for h in range(hp):
            o = om_ref[h]
            ssq = ssq + jnp.sum(o * o, axis=-1, keepdims=True)
        ra = lax.rsqrt(ssq * (1.0 / da) + EPS)
        for h in range(hp):
            ya = (om_ref[h] * ra) * ga_ref[h] * _silu(za_ref[:, h * PAIR:(h + 1) * PAIR].astype(F32))
            ycat_ref[:, dc + h * PAIR:dc + (h + 1) * PAIR] = ya.astype(BF16)
            ycatt_ref[dc + h * PAIR:dc + (h + 1) * PAIR, :] = ya.T.astype(BF16)

    pair_spec = pl.BlockSpec((hp, tb, PAIR), lambda i: (0, i, 0))
    return pl.pallas_call(
        body, name="mix_fwd", grid=(t // tb,),
        out_shape=(jax.ShapeDtypeStruct((t, dc + da), BF16), jax.ShapeDtypeStruct((dc + da, t), BF16)),
        in_specs=[pl.BlockSpec((tb, dc), lambda i: (i, 0)),
                  pl.BlockSpec((tb, dc), lambda i: (i, 1)),
                  pl.BlockSpec((tb, dc), lambda i: (i, 3)),
                  pl.BlockSpec((tb, da), lambda i: (i, 7)),
                  pair_spec,
                  pl.BlockSpec((1, dc), lambda i: (0, 0)),
                  pl.BlockSpec((hp, 1, PAIR), lambda i: (0, 0, 0))],
        out_specs=(pl.BlockSpec((tb, dc + da), lambda i: (i, 0)), pl.BlockSpec((dc + da, tb), lambda i: (0, i))),
        compiler_params=_params(("parallel",)),
    )(co, proj, proj, proj, o_mix, g_conv, g_attn_pairs)


def _out_fwd_bwd(ycat, woutf, x, target, mod, g_post):
    t, d = x.shape
    n = ycat.shape[1]
    tb = ROW_TILE

    def body(a_ref, w_ref, x_ref, tg_ref, mod_ref, g_ref, dout_ref, dy_ref, acc_ref):
        y = jnp.dot(a_ref[...], w_ref[...], preferred_element_type=F32)
        r = lax.rsqrt(jnp.mean(y * y, axis=-1, keepdims=True) + EPS)
        nh = y * r
        gate = mod_ref[2:3, :]
        nrm = nh * g_ref[...]
        err = x_ref[...] + gate * nrm - tg_ref[...]
        dout = err * (1.0 / d)
        dout_ref[...] = dout.astype(BF16)
        dn = dout * gate
        a = dn * g_ref[...]
        dy = r * (a - nh * jnp.mean(a * nh, axis=-1, keepdims=True))
        dy_ref[...] = dy.astype(BF16)
        loss = 0.5 * jnp.sum(jnp.sum(err * err, axis=-1, keepdims=True) * (1.0 / d), axis=0, keepdims=True)
        part = jnp.concatenate(
            [jnp.sum(dout * nrm, axis=0, keepdims=True), jnp.sum(dn * nh, axis=0, keepdims=True),
             jnp.broadcast_to(loss, (1, d)), jnp.zeros((5, d), F32)], axis=0)

        @pl.when(pl.program_id(0) == 0)
        def _():
            acc_ref[...] = jnp.zeros(acc_ref.shape, F32)

        acc_ref[...] += part

    return pl.pallas_call(
        body, name="out_fwd_bwd", grid=(t // tb,),
        out_shape=(jax.ShapeDtypeStruct((t, d), BF16), jax.ShapeDtypeStruct((t, d), BF16),
                   jax.ShapeDtypeStruct((8, d), F32)),
        in_specs=[pl.BlockSpec((tb, n), lambda i: (i, 0)), pl.BlockSpec((n, d), lambda i: (0, 0)),
                  pl.BlockSpec((tb, d), lambda i: (i, 0)), pl.BlockSpec((tb, d), lambda i: (i, 0)),
                  pl.BlockSpec((3, d), lambda i: (0, 0)), pl.BlockSpec((1, d), lambda i: (0, 0))],
        out_specs=(pl.BlockSpec((tb, d), lambda i: (i, 0)), pl.BlockSpec((tb, d), lambda i: (i, 0)),
                   pl.BlockSpec((8, d), lambda i: (0, 0))),
        compiler_params=_params(("arbitrary",)),
    )(ycat, woutf, x, target, mod, g_post)


def _matmul_nt(a, b, out_dtype, name):
    m, k = a.shape
    n = b.shape[0]
    tn = COL_TILE

    def body(a_ref, b_ref, o_ref):
        o_ref[...] = lax.dot_general(a_ref[...], b_ref[...], (((1,), (1,)), ((), ())),
                                     preferred_element_type=F32).astype(out_dtype)

    return pl.pallas_call(
        body, name=name, grid=(n // tn,),
        out_shape=jax.ShapeDtypeStruct((m, n), out_dtype),
        in_specs=[pl.BlockSpec((m, k), lambda i: (0, 0)), pl.BlockSpec((tn, k), lambda i: (i, 0))],
        out_specs=pl.BlockSpec((m, tn), lambda i: (0, i)),
        compiler_params=_params(("parallel",)),
    )(a, b)


def _mix_bwd(dycat, co, proj, o_mix, g_conv, g_attn_pairs):
    t, dc = co.shape
    hp = o_mix.shape[0]
    da = hp * PAIR
    tb = ROW_TILE

    def body(dy_ref, co_ref, bg_ref, zc_ref, za_ref, om_ref, gc_ref, ga_ref,
             dcp_ref, dco_ref, do_ref, dl_ref, dgc_ref, dga_ref):
        first = pl.program_id(0) == 0
        cov = co_ref[...].astype(F32)
        bg = bg_ref[...].astype(F32)
        zc = zc_ref[...].astype(F32)
        p = bg * cov
        rc = lax.rsqrt(jnp.mean(p * p, axis=-1, keepdims=True) + EPS)
        nh = p * rc
        dyc = dy_ref[:, 0:dc].astype(F32)
        dn = dyc * _silu(zc)
        a = dn * gc_ref[...]
        dp = rc * (a - nh * jnp.mean(a * nh, axis=-1, keepdims=True))
        dcp_ref[:, 0:dc] = jnp.zeros((tb, dc), BF16)
        dcp_ref[:, dc:2 * dc] = (dp * cov).astype(BF16)
        dcp_ref[:, 2 * dc:3 * dc] = jnp.zeros((tb, dc), BF16)
        dcp_ref[:, 3 * dc:4 * dc] = (dyc * nh * gc_ref[...] * _silu_grad(zc)).astype(BF16)
        dcp_ref[:, 4 * dc:4 * dc + 3 * da] = jnp.zeros((tb, 3 * da), BF16)
        dco_ref[...] = dp * bg

        @pl.when(first)
        def _():
            dgc_ref[...] = jnp.zeros(dgc_ref.shape, F32)
            dga_ref[...] = jnp.zeros(dga_ref.shape, F32)

        dgc_ref[...] += jnp.sum(dn * nh, axis=0, keepdims=True)

        ssq = jnp.zeros((tb, 1), F32)
        for h in range(hp):
            o = om_ref[h]
            ssq = ssq + jnp.sum(o * o, axis=-1, keepdims=True)
        ra = lax.rsqrt(ssq * (1.0 / da) + EPS)
        dot_an = jnp.zeros((tb, 1), F32)
        for h in range(hp):
            nha = om_ref[h] * ra
            za = za_ref[:, h * PAIR:(h + 1) * PAIR].astype(F32)
            dya = dy_ref[:, dc + h * PAIR:dc + (h + 1) * PAIR].astype(F32)
            dna = dya * _silu(za)
            dza = (dya * nha * ga_ref[h] * _silu_grad(za)).astype(BF16)
            dcp_ref[:, 4 * dc + 3 * da + h * PAIR:4 * dc + 3 * da + (h + 1) * PAIR] = dza
            dga_ref[h] += jnp.sum(dna * nha, axis=0, keepdims=True)
            dot_an = dot_an + jnp.sum(dna * ga_ref[h] * nha, axis=-1, keepdims=True)
        mean_an = dot_an * (1.0 / da)
        first_head = lax.broadcasted_iota(jnp.int32, (tb, PAIR), 1) < HEAD_DIM
        for h in range(hp):
            o = om_ref[h]
            nha = o * ra
            za = za_ref[:, h * PAIR:(h + 1) * PAIR].astype(F32)
            dya = dy_ref[:, dc + h * PAIR:dc + (h + 1) * PAIR].astype(F32)
            aa = dya * _silu(za) * ga_ref[h]
            d_o = ra * (aa - nha * mean_an)
            do_ref[h] = d_o.astype(BF16)
            prod = d_o * o
            both = jnp.sum(prod, axis=-1, keepdims=True)
            head0 = jnp.sum(jnp.where(first_head, prod, 0.0), axis=-1, keepdims=True)
            dl_ref[h] = jnp.where(first_head, head0, both - head0)

    pair_spec = pl.BlockSpec((hp, tb, PAIR), lambda i: (0, i, 0))
    return pl.pallas_call(
        body, name="mix_bwd", grid=(t // tb,),
        out_shape=(jax.ShapeDtypeStruct((t, 4 * dc + 4 * da), BF16), jax.ShapeDtypeStruct((t, dc), F32),
                   jax.ShapeDtypeStruct((hp, t, PAIR), BF16), jax.ShapeDtypeStruct((hp, t, PAIR), F32),
                   jax.ShapeDtypeStruct((1, dc), F32), jax.ShapeDtypeStruct((hp, 1, PAIR), F32)),
        in_specs=[pl.BlockSpec((tb, dc + da), lambda i: (i, 0)),
                  pl.BlockSpec((tb, dc), lambda i: (i, 0)),
                  pl.BlockSpec((tb, dc), lambda i: (i, 1)),
                  pl.BlockSpec((tb, dc), lambda i: (i, 3)),
                  pl.BlockSpec((tb, da), lambda i: (i, 7)),
                  pair_spec,
                  pl.BlockSpec((1, dc), lambda i: (0, 0)),
                  pl.BlockSpec((hp, 1, PAIR), lambda i: (0, 0, 0))],
        out_specs=(pl.BlockSpec((tb, 4 * dc + 4 * da), lambda i: (i, 0)), pl.BlockSpec((tb, dc), lambda i: (i, 0)),
                   pair_spec, pair_spec,
                   pl.BlockSpec((1, dc), lambda i: (0, 0)), pl.BlockSpec((hp, 1, PAIR), lambda i: (0, 0, 0))),
        compiler_params=_params(("arbitrary",)),
    )(dycat, co, proj, proj, proj, o_mix, g_conv, g_attn_pairs)


def _conv_bwd(dconv_proj, dco, conv_proj, conv_w, dc, after):
    t = dco.shape[0]
    ct = CONV_TILE
    nct = dc // ct

    def body(dcp_in_ref, dco_ref, u_ref, cg_ref, w_ref, after_ref, dcp_ref, acc_ref):
        del dcp_in_ref, after_ref
        which = pl.program_id(1)
        g = dco_ref[...]
        u = u_ref[...].astype(F32)
        cg = cg_ref[...].astype(F32)
        g_prev, g_next = _shift_rows(g, t)
        da = w_ref[0:1, :] * g_next + w_ref[1:2, :] * g + w_ref[2:3, :] * g_prev
        dcp_ref[...] = (da * jnp.where(which == 0, cg, u)).astype(BF16)
        a = cg * u
        a_prev, a_next = _shift_rows(a, t)
        acc_ref[...] = jnp.concatenate(
            [jnp.sum(g * a_prev, axis=0, keepdims=True), jnp.sum(g * a, axis=0, keepdims=True),
             jnp.sum(g * a_next, axis=0, keepdims=True), jnp.sum(g, axis=0, keepdims=True),
             jnp.zeros((4, ct), F32)], axis=0)

    return pl.pallas_call(
        body, name="conv_bwd", grid=(nct, 2),
        out_shape=(jax.ShapeDtypeStruct(dconv_proj.shape, BF16), jax.ShapeDtypeStruct((8, dc), F32)),
        in_specs=[HBM,
                  pl.BlockSpec((t, ct), lambda i, s: (0, i)),
                  pl.BlockSpec((t, ct), lambda i, s: (0, i)),
                  pl.BlockSpec((t, ct), lambda i, s: (0, 2 * nct + i)),
                  pl.BlockSpec((3, ct), lambda i, s: (0, i)), ANY],
        out_specs=(pl.BlockSpec((t, ct), lambda i, s: (0, 2 * s * nct + i)),
                   pl.BlockSpec((8, ct), lambda i, s: (0, i))),
        input_output_aliases={0: 0},
        compiler_params=_params(("arbitrary", "arbitrary")),
    )(dconv_proj, dco, conv_proj, conv_proj, conv_w, after)


def _dh(dproj, winf, after):
    t = dproj.shape[0]
    _, d, ws = winf.shape
    tm = tn = COL_TILE
    nt = (((1,), (1,)), ((), ()))

    def body(a_ref, w_ref, after_ref, o_ref):
        del after_ref
        acc = lax.dot_general(a_ref[:, 0:ws], w_ref[0], nt, preferred_element_type=F32)
        for j in range(1, N_CHIPS):
            acc = acc + lax.dot_general(a_ref[:, j * ws:(j + 1) * ws], w_ref[j], nt, preferred_element_type=F32)
        o_ref[...] = acc.astype(BF16)

    return pl.pallas_call(
        body, name="dh", grid=(d // tn, t // tm),
        out_shape=jax.ShapeDtypeStruct((t, d), BF16),
        in_specs=[pl.BlockSpec((tm, N_CHIPS * ws), lambda n, m: (m, 0)),
                  pl.BlockSpec((N_CHIPS, tn, ws), lambda n, m: (0, n, 0)), ANY],
        out_specs=pl.BlockSpec((tm, tn), lambda n, m: (m, n)),
        compiler_params=_params(("parallel", "parallel")),
    )(dproj, winf, after)


def _prenorm_bwd(x, dh, dout, mod, g_pre):
    t, d = x.shape
    tb = ROW_TILE

    def body(x_ref, dh_ref, dout_ref, mod_ref, g_ref, gx_ref, acc_ref):
        xv = x_ref[...]
        dhv = dh_ref[...].astype(F32)
        r = lax.rsqrt(jnp.mean(xv * xv, axis=-1, keepdims=True) + EPS)
        xh = xv * r
        one_scale = 1.0 + mod_ref[1:2, :]
        a = dhv * one_scale * g_ref[...]
        gx_ref[...] = dout_ref[...].astype(F32) + r * (a - xh * jnp.mean(a * xh, axis=-1, keepdims=True))
        part = jnp.concatenate(
            [jnp.sum(dhv, axis=0, keepdims=True), jnp.sum(dhv * xh * g_ref[...], axis=0, keepdims=True),
             jnp.sum(dhv * xh * one_scale, axis=0, keepdims=True), jnp.zeros((5, d), F32)], axis=0)

        @pl.when(pl.program_id(0) == 0)
        def _():
            acc_ref[...] = jnp.zeros(acc_ref.shape, F32)

        acc_ref[...] += part

    return pl.pallas_call(
        body, name="prenorm_bwd", grid=(t // tb,),
        out_shape=(jax.ShapeDtypeStruct((t, d), F32), jax.ShapeDtypeStruct((8, d), F32)),
        in_specs=[pl.BlockSpec((tb, d), lambda i: (i, 0)), pl.BlockSpec((tb, d), lambda i: (i, 0)),
                  pl.BlockSpec((tb, d), lambda i: (i, 0)), pl.BlockSpec((3, d), lambda i: (0, 0)),
                  pl.BlockSpec((1, d), lambda i: (0, 0))],
        out_specs=(pl.BlockSpec((tb, d), lambda i: (i, 0)), pl.BlockSpec((8, d), lambda i: (0, 0))),
        compiler_params=_params(("arbitrary",)),
    )(x, dh, dout, mod, g_pre)


def _chip_sums(mine, rsib, name, part=0, parts=1, after=()):
    _, half, cols = mine.shape
    rows = half // parts
    tr = min(rows, ROW_TILE)
    nt = rows // tr

    def body(g_ref, r_ref, *rest):
        rest[-1][...] = (g_ref[...].astype(F32) + r_ref[...].astype(F32)).astype(BF16)

    spec = pl.BlockSpec((None, tr, cols), lambda j, i: (j, part * nt + i, 0))
    return pl.pallas_call(
        body, name=name, grid=(N_CHIPS, nt),
        out_shape=jax.ShapeDtypeStruct((N_CHIPS, rows, cols), BF16),
        in_specs=[spec, spec] + [ANY] * len(after), out_specs=pl.BlockSpec((None, tr, cols), lambda j, i: (j, i, 0)),
        compiler_params=_params(("parallel", "parallel")),
    )(mine, rsib, *after)


def _owner_sum(place, mine, rsib, rici, name, part=0, parts=1, full=None):
    _, half, cols = mine.shape
    rows = half // parts
    tr = min(rows, ROW_TILE)
    nt = rows // tr

    def body(place_ref, *refs):
        del place_ref
        g_ref, r_ref, i_ref, o_ref = refs[-4:]
        acc = g_ref[...].astype(F32) + r_ref[...].astype(F32)
        for k in range(N_CHIPS - 1):
            acc = acc + i_ref[k].astype(F32)
        o_ref[...] = acc

    own = pl.BlockSpec((None, tr, cols), lambda i, p: (p[0], part * nt + i, 0))
    grid_spec = pltpu.PrefetchScalarGridSpec(
        num_scalar_prefetch=1, grid=(nt,),
        in_specs=([] if full is None else [HBM]) + [own, own, pl.BlockSpec((N_CHIPS - 1, tr, cols), lambda i, p: (0, i, 0))],
        out_specs=pl.BlockSpec((tr, cols), lambda i, p: (p[1] * (half // tr) + part * nt + i, 0)))
    return pl.pallas_call(
        body, name=name, grid_spec=grid_spec,
        out_shape=jax.ShapeDtypeStruct((2 * half, cols), F32),
        input_output_aliases={} if full is None else {1: 0},
        compiler_params=_params(("parallel",)),
    )(*([place] if full is None else [place, full]), mine, rsib, rici)


def _adam_math(w, g, m, v):
    m2 = ADAM_B1 * m + (1.0 - ADAM_B1) * g
    v2 = ADAM_B2 * v + (1.0 - ADAM_B2) * (g * g)
    m_hat = m2 / (1.0 - ADAM_B1 ** ADAM_STEP)
    v_hat = v2 / (1.0 - ADAM_B2 ** ADAM_STEP)
    delta = -ADAM_LR * (m_hat / (jnp.sqrt(v_hat) + ADAM_EPS) + ADAM_WD * w)
    return delta, m2, v2


def _adamw(w, g, m, v, name, part=0, parts=1, prev=None):
    rows, cols = w.shape
    tr = min(rows, ROW_TILE)

    def body(*refs):
        w_ref, g_ref, m_ref, v_ref, go_ref, d_ref, m2_ref, v2_ref = refs[-8:]
        g = g_ref[...]
        go_ref[...] = g
        d_ref[...], m2_ref[...], v2_ref[...] = _adam_math(w_ref[...], g, m_ref[...], v_ref[...])

    if parts == 1:
        grid, spec = (rows // tr,), pl.BlockSpec((tr, cols), lambda i: (i, 0))
    else:
        per_half = rows // 2 // tr
        nt = per_half // parts
        grid, spec = (2, nt), pl.BlockSpec((tr, cols), lambda r, i: (r * per_half + part * nt + i, 0))
    olds = [] if prev is None else list(prev)
    return pl.pallas_call(
        body, name=name, grid=grid,
        out_shape=(jax.ShapeDtypeStruct(w.shape, F32),) * 4,
        in_specs=[HBM] * len(olds) + [spec] * 4, out_specs=(spec,) * 4,
        input_output_aliases={i: i for i in range(len(olds))},
        compiler_params=_params(("parallel",) * len(grid)),
    )(*olds, w, g, m, v)


def _ada_grad_adamw(c_all_t, dmod_cols, w, m, v):
    d, wa = w.shape
    tr = ROW_TILE

    def body(ct_ref, dm_ref, w_ref, m_ref, v_ref, g_ref, d_ref, m2_ref, v2_ref):
        act = _silu(ct_ref[...])
        g = act[:, 0:1] * dm_ref[0:1, :]
        for b in range(1, N_DEV):
            g = g + act[:, b:b + 1] * dm_ref[b:b + 1, :]
        g_ref[...] = g
        d_ref[...], m2_ref[...], v2_ref[...] = _adam_math(w_ref[...], g, m_ref[...], v_ref[...])

    spec = pl.BlockSpec((tr, wa), lambda i: (i, 0))
    return pl.pallas_call(
        body, name="ada_grad_adamw", grid=(d // tr,),
        out_shape=(jax.ShapeDtypeStruct(w.shape, F32),) * 4,
        in_specs=[pl.BlockSpec((tr, N_DEV), lambda i: (i, 0)), pl.BlockSpec((N_DEV, wa), lambda i: (0, 0)),
                  spec, spec, spec],
        out_specs=(spec,) * 4,
        compiler_params=_params(("parallel",)),
    )(c_all_t, dmod_cols, w, m, v)


def _small_update(place, gathered, pieces, weights, moments_m, moments_v):
    n = gathered.shape[1]
    k = len(weights)

    def body(place_ref, g_ref, *refs):
        w_refs, m_refs, v_refs = refs[0:k], refs[k:2 * k], refs[2 * k:3 * k]
        outs = refs[3 * k:]
        total = g_ref[0:SUBLANES, :]
        for dev in range(1, N_DEV):
            total = total + g_ref[SUBLANES * dev:SUBLANES * (dev + 1), :]

        def flat(offset, length):
            segments, pos = [], offset
            while pos < offset + length:
                row, col = divmod(pos, n)
                take = min(offset + length - pos, n - col)
                segments.append(total[row:row + 1, col:col + take])
                pos += take
            return jnp.concatenate(segments, axis=1) if len(segments) > 1 else segments[0]

        chip = place_ref[0]
        for i, (w_ref, m_ref, v_ref) in enumerate(zip(w_refs, m_refs, v_refs)):
            g = flat(*pieces[i])
            if w_ref.ndim == 3:
                rows, cols = w_ref.shape[1:]
                full = pieces[i][1] // rows
                picked = []
                for r in range(rows):
                    blocks = [g[:, r * full + q * cols:r * full + (q + 1) * cols] for q in range(N_CHIPS)]
                    mine = blocks[N_CHIPS - 1]
                    for q in range(N_CHIPS - 2, -1, -1):
                        mine = jnp.where(chip == q, blocks[q], mine)
                    picked.append(mine)
                g = jnp.concatenate(picked, axis=0)
                w, m, v = w_ref[0], m_ref[0], v_ref[0]
            else:
                w, m, v = w_ref[...], m_ref[...], v_ref[...]
            delta, m2, v2 = _adam_math(w, g, m, v)
            for j, val in enumerate((g, delta, m2, v2)):
                out = outs[j * k + i]
                if w_ref.ndim == 3:
                    out[0] = val
                else:
                    out[...] = val
        outs[4 * k][...] = flat(*pieces[k])

    shapes = [jax.ShapeDtypeStruct(w.shape, F32) for w in weights]
    grid_spec = pltpu.PrefetchScalarGridSpec(
        num_scalar_prefetch=1, grid=(1,),
        in_specs=[pl.BlockSpec(gathered.shape, lambda i, p: (0, 0))]
        + [pl.BlockSpec(a.shape, functools.partial(lambda nd, i, p: (0,) * nd, a.ndim))
           for a in (*weights, *moments_m, *moments_v)],
        out_specs=tuple(pl.BlockSpec(s.shape, functools.partial(lambda nd, i, p: (0,) * nd, len(s.shape)))
                        for s in shapes * 4) + (pl.BlockSpec((1, LANES), lambda i, p: (0, 0)),))
    outs = pl.pallas_call(
        body, name="small_update", grid_spec=grid_spec,
        out_shape=tuple(shapes * 4) + (jax.ShapeDtypeStruct((1, LANES), F32),),
        compiler_params=_params(("arbitrary",)),
    )(place, gathered, *weights, *moments_m, *moments_v)
    return outs[0:k], outs[k:2 * k], outs[2 * k:3 * k], outs[3 * k:4 * k], outs[4 * k]


def _pack_small(pieces):
    flat = [p.reshape(-1).astype(F32) for p in pieces]
    offsets, total = [], 0
    for p in flat:
        offsets.append(total)
        total += p.shape[0]
    padded = -(-total // SMALL_ALIGN) * SMALL_ALIGN
    if padded > total:
        flat.append(jnp.zeros((padded - total,), F32))
    return jnp.concatenate(flat).reshape(8, padded // 8), offsets


def _alibi_slope_rows(n_heads):
    slopes = 2.0 ** (-8.0 * jnp.arange(1, n_heads + 1, dtype=F32) / n_heads)
    rows = jnp.zeros((n_heads // 2, 8), F32).at[:, 0:2].set(slopes.reshape(n_heads // 2, 2))
    return jnp.broadcast_to(rows[:, :, None], (n_heads // 2, 8, ATT_KW))


def kernel(x, c, w_ada, b_ada, g_pre, w_in, conv_w, conv_b, g_conv, g_attn, w_out, g_post, loss_target, m_w_ada, m_b_ada, m_g_pre, m_w_in, m_conv_w, m_conv_b, m_g_conv, m_g_attn, m_w_out, m_g_post, v_w_ada, v_b_ada, v_g_pre, v_w_in, v_conv_w, v_conv_b, v_g_conv, v_g_attn, v_w_out, v_g_post):
    t, d = x.shape[1], x.shape[2]
    dc = conv_b.shape[1]
    da = g_attn.shape[1]
    hp = da // PAIR
    ws = w_in.shape[2]
    wa = w_ada.shape[2]
    cws = conv_w.shape[2]
    assert t % ROW_TILE == 0 and d % ROW_TILE == 0 and dc % COL_TILE == 0 and da % COL_TILE == 0
    assert ws == 2 * dc and dc == da and t // BRANCHES[-1][1] >= ATT_BQ

    mx, my, mc = _my_place()
    chip = _chip_of(mx, my)
    dev = 2 * chip + mc
    place = jnp.stack([chip, mc]).astype(jnp.int32)

    x2, tgt2 = x[0], loss_target[0]
    w_ada2, w_in2, w_out2 = w_ada[0], w_in[0], w_out[0]

    win_slots = _cast_into_slot(place, w_in2, "cast_w_in")
    packed, offs = _pack_small([c[0], conv_w[0]])
    seen, mod = _ada_modulation(packed, w_ada2, b_ada, d, after=(win_slots,))
    seen = seen.reshape(N_DEV, -1)
    c_all = seen[:, offs[0]:offs[0] + d]
    conv_w_full = seen[0::2, offs[1]:offs[1] + 3 * cws].reshape(N_CHIPS, 3, cws).transpose(1, 0, 2).reshape(3, dc)

    win_flight, send_in, recv_in, started = _gather_start(win_slots, mod)

    y_chip, x_chip, d_chip = (_chip_of(mx, 1 - my), _chip_of(1 - mx, my), _chip_of(1 - mx, 1 - my))
    own_chunk, near_chunks, far_chunk = (jnp.stack(js).astype(jnp.int32) for js in ([chip], [y_chip, x_chip], [d_chip]))
    h, ht = _prenorm(x2, mod + started[0, 0], g_pre)
    proj = _proj_chunks(None, h, w_in2, own_chunk, "proj_own")
    win_flight, wout_flight, relay_send_in, relay_recv_in, send_out, recv_out = _gather_relay_in(
        win_flight, _cast_into_slot(place, w_out2, "cast_w_out"), recv_in, proj)
    win_flight = _forward_halves(
        _gather_wait_direct(win_flight, send_in, recv_in, proj, "gather_wait_w_in_direct"), (0, 1), "forward_w_in_direct")
    proj = _proj_chunks(proj, h, win_flight, near_chunks, "proj_neighbours")
    winf = _forward_halves(
        _gather_wait_relayed(win_flight, relay_send_in, relay_recv_in, proj, "gather_wait_w_in_relayed"),
        (2,), "forward_w_in_relayed")
    proj = _proj_chunks(proj, h, winf, far_chunk, "proj_diagonal")
    slopes = _alibi_slope_rows(da // HEAD_DIM)
    co = _conv_fwd(proj, conv_w_full, conv_b, dc)
    wout_flight, relay_send_out, relay_recv_out = _gather_relay_out(wout_flight, recv_out, co)
    o_mix, lse = _attn_fwd(proj, slopes, dc, da)
    g_attn_pairs = g_attn.reshape(hp, 1, PAIR)
    ycat, ycat_t = _mix_fwd(co, proj, o_mix, g_conv, g_attn_pairs)
    wout_flight = _gather_wait_direct(wout_flight, send_out, recv_out, ycat, "gather_wait_w_out_direct")
    wout_flight = _gather_wait_relayed(wout_flight, relay_send_out, relay_recv_out, ycat, "gather_wait_w_out_relayed")
    woutf = _forward_halves(wout_flight, (0, 1, 2), "forward_w_out").reshape(dc + da, d)
    dout, dy, post_sums = _out_fwd_bwd(ycat, woutf, x2, tgt2, mod, g_post)

    gout, rsib_out = _dw_swapped(ycat_t, dy, N_CHIPS, 1, "dw_out")
    csum_out = _chip_sums(gout, rsib_out, "rs_chip_sum_out")
    ssem_out, rsem_out, csum_out, land_out, sent_out = _owners_start(csum_out, "rs_owners_start_out")
    dycat = _matmul_nt(dy, woutf, BF16, "dycat")
    dproj, dco, d_o, delta, dg_conv, dg_attn = _mix_bwd(dycat, co, proj, o_mix, g_conv, g_attn_pairs)
    dproj, conv_sums = _conv_bwd(dproj, dco, proj, conv_w_full, dc, sent_out)
    dproj = _attn_bwd(dproj, proj, d_o, lse, delta, slopes, dc, da, sent_out)
    gin, rsib_in = _dw_swapped(ht, dproj, 1, N_CHIPS, "dw_in")
    ssem_in0, rsem_in0, csum_in0, land_in0, sent_in = _owners_start(
        _chip_sums(gin, rsib_in, "rs_chip_sum_in0", 0, 2), "rs_owners_start_in0")
    dh = _dh(dproj, winf, sent_in)
    grad_x, pre_sums = _prenorm_bwd(x2, dh, dout, mod, g_pre)

    small, so = _pack_small([
        pre_sums[0], pre_sums[1], post_sums[0],
        pre_sums[2], conv_sums[0:3], conv_sums[3], dg_conv, dg_attn, post_sums[1], post_sums[2, 0:128]])
    ssem_small, rsem_small, small, land_small, sent_small = _allgather8_start(small, dev, "gather_small_start")
    ssem_in1, rsem_in1, csum_in1, land_in1, sent_in1 = _owners_start(
        _chip_sums(gin, rsib_in, "rs_chip_sum_in1", 1, 2, after=(sent_small,)), "rs_owners_start_in1")

    rici_out = _owners_wait(ssem_out, rsem_out, csum_out, land_out, [grad_x, sent_in1], "rs_owners_wait_out")
    grad_w_out = _join_halves(_owner_sum(place, gout, rsib_out, rici_out, "rs_owner_sum_out"), "rs_join_halves_out")
    grad_w_out, delta_w_out, new_m_w_out, new_v_w_out = _adamw(
        w_out2, grad_w_out, m_w_out[0], v_w_out[0], "adamw_w_out")

    small_seen = _allgather8_wait(ssem_small, rsem_small, small, land_small, [delta_w_out], "gather_small_wait")
    small_w = [b_ada, g_pre, conv_w, conv_b, g_conv, g_attn, g_post]
    small_m = [m_b_ada, m_g_pre, m_conv_w, m_conv_b, m_g_conv, m_g_attn, m_g_post]
    small_v = [v_b_ada, v_g_pre, v_conv_w, v_conv_b, v_g_conv, v_g_attn, v_g_post]
    pieces = [(0, 3 * d), (so[3], d), (so[4], 3 * dc), (so[5], dc), (so[6], dc), (so[7], da), (so[8], d), (so[9], LANES)]
    g_small, d_small, m_small, v_small, loss_row = _small_update(place, small_seen, pieces, small_w, small_m, small_v)
    loss = loss_row[0, 0]
    grad_b_ada, grad_g_pre, grad_conv_w, grad_conv_b, grad_g_conv, grad_g_attn, grad_g_post = g_small

    dmod_cols = lax.dynamic_slice_in_dim(small_seen.reshape(N_DEV, -1), chip * wa, wa, axis=1)
    grad_w_ada, delta_w_ada, new_m_w_ada, new_v_w_ada = _ada_grad_adamw(c_all.T, dmod_cols, w_ada2, m_w_ada[0], v_w_ada[0])

    rici_in = _owners_wait(ssem_in0, rsem_in0, csum_in0, land_in0, [d_small[0], delta_w_out, delta_w_ada], "rs_owners_wait_in0")
    full_in = _join_halves(_owner_sum(place, gin, rsib_in, rici_in, "rs_owner_sum_in0", 0, 2), "rs_join_halves_in0", 0, 2)
    updated_in = _adamw(w_in2, full_in, m_w_in[0], v_w_in[0], "adamw_w_in0", 0, 2)
    rici_in = _owners_wait(ssem_in1, rsem_in1, csum_in1, land_in1, [updated_in[1]], "rs_owners_wait_in1")
    full_in = _join_halves(
        _owner_sum(place, gin, rsib_in, rici_in, "rs_owner_sum_in1", 1, 2, full_in), "rs_join_halves_in1", 1, 2)
    grad_w_in, delta_w_in, new_m_w_in, new_v_w_in = _adamw(
        w_in2, full_in, m_w_in[0], v_w_in[0], "adamw_w_in1", 1, 2, updated_in)

    def lead(a):
        return a.reshape((1,) + a.shape)

    grads = [lead(grad_w_ada), grad_b_ada, grad_g_pre, lead(grad_w_in), grad_conv_w, grad_conv_b, grad_g_conv,
             grad_g_attn, lead(grad_w_out), grad_g_post]
    deltas = [lead(delta_w_ada), d_small[0], d_small[1], lead(delta_w_in), d_small[2], d_small[3], d_small[4],
              d_small[5], lead(delta_w_out), d_small[6]]
    new_ms = [lead(new_m_w_ada), m_small[0], m_small[1], lead(new_m_w_in), m_small[2], m_small[3], m_small[4],
              m_small[5], lead(new_m_w_out), m_small[6]]
    new_vs = [lead(new_v_w_ada), v_small[0], v_small[1], lead(new_v_w_in), v_small[2], v_small[3], v_small[4],
              v_small[5], lead(new_v_w_out), v_small[6]]
    return (loss, lead(grad_x), *grads, *deltas, *new_ms, *new_vs)
```

```python
import functools

import jax
import jax.numpy as jnp
from jax import lax
from jax.experimental import pallas as pl
from jax.experimental.pallas import tpu as pltpu

F32 = jnp.float32
BF16 = jnp.bfloat16
MESH = pl.DeviceIdType.MESH
HBM = pl.BlockSpec(memory_space=pltpu.HBM)
VMEM = pl.BlockSpec(memory_space=pltpu.VMEM)
ANY = pl.BlockSpec(memory_space=pl.ANY)
SEM = pl.BlockSpec(memory_space=pltpu.SEMAPHORE)
EFFECT = pltpu.SideEffectType.DATAFLOW_SIDE_EFFECTING
SUBLANES, LANES = 8, 128
TOKEN = jax.ShapeDtypeStruct((SUBLANES, LANES), jnp.float32)

HEAD_DIM = 64
PAIR = 2 * HEAD_DIM
assert PAIR == LANES
BRANCHES = ((128, 1), (512, 4), (2048, 16))
SIDE = 64
EPS = 1e-6
NEG_INF = -1e30
N_CHIPS = 4
N_DEV = 8

ADAM_LR = 0.001
ADAM_B1 = 0.9
ADAM_B2 = 0.999
ADAM_EPS = 1e-08
ADAM_WD = 0.01
ADAM_STEP = 10

VMEM_LIMIT_BYTES = 56 * 1024 * 1024
ROW_TILE = 256
COL_TILE = 512
CONV_TILE = 256
ATT_BQ = 128
ATT_KW = ATT_BQ + 2 * SIDE
ATT_UNROLL = 4
SMALL_ALIGN = SUBLANES * LANES


def _params(semantics=None):
    kw = {"vmem_limit_bytes": VMEM_LIMIT_BYTES}
    if semantics is not None:
        kw["dimension_semantics"] = semantics
    return pltpu.CompilerParams(**kw)


def _silu(z):
    return z * jax.nn.sigmoid(z)


def _silu_grad(z):
    s = jax.nn.sigmoid(z)
    return s * (1.0 + z * (1.0 - s))


def _my_place():
    return lax.axis_index("x"), lax.axis_index("y"), lax.axis_index("c")


def _flip(a, bit):
    return 1 - a if bit else a


def _chip_of(x, y):
    return 2 * x + y


def _allgather8_start(v, me, name):
    rows_per, n = v.shape
    land = lax.dynamic_update_slice(jnp.zeros((N_DEV * rows_per, n), v.dtype), v, (me * rows_per, 0))

    def body(v_ref, land_ref, send_sems, recv_sems, v_thru, land_thru, token_ref):
        del v_thru, land_thru
        x, y, c = _my_place()
        mine = land_ref.at[pl.ds(pl.multiple_of((4 * x + 2 * y + c) * rows_per, rows_per), rows_per), :]
        for k in range(1, N_DEV):
            peer = (_flip(x, k & 4), _flip(y, k & 2), _flip(c, k & 1))
            pltpu.make_async_remote_copy(
                src_ref=v_ref, dst_ref=mine, send_sem=send_sems.at[k - 1], recv_sem=recv_sems.at[k - 1],
                device_id=peer, device_id_type=MESH).start()
        token_ref[...] = jnp.zeros(token_ref.shape, F32)

    sems = pltpu.SemaphoreType.DMA((N_DEV - 1,))
    return pl.pallas_call(
        body, name=name,
        out_shape=(sems, sems, jax.ShapeDtypeStruct(v.shape, v.dtype), jax.ShapeDtypeStruct(land.shape, land.dtype), TOKEN),
        in_specs=[HBM, HBM], out_specs=(SEM, SEM, HBM, HBM, VMEM),
        input_output_aliases={0: 2, 1: 3},
        compiler_params=pltpu.CompilerParams(has_side_effects=EFFECT),
    )(pltpu.with_memory_space_constraint(v, pltpu.HBM), pltpu.with_memory_space_constraint(land, pltpu.HBM))


def _allgather8_wait(send_sems, recv_sems, v, land, after, name):
    rows_per = v.shape[0]

    def body(v_ref, land_ref, send_ref, recv_ref, *rest):
        del rest
        x, y, c = _my_place()
        for k in range(1, N_DEV):
            peer = (_flip(x, k & 4), _flip(y, k & 2), _flip(c, k & 1))
            src = 4 * peer[0] + 2 * peer[1] + peer[2]
            cp = pltpu.make_async_remote_copy(
                src_ref=v_ref, dst_ref=land_ref.at[pl.ds(pl.multiple_of(src * rows_per, rows_per), rows_per), :],
                send_sem=send_ref.at[k - 1], recv_sem=recv_ref.at[k - 1], device_id=peer, device_id_type=MESH)
            cp.wait_send()
            cp.wait_recv()

    return pl.pallas_call(
        body, name=name,
        out_shape=(jax.ShapeDtypeStruct(v.shape, v.dtype), jax.ShapeDtypeStruct(land.shape, land.dtype)),
        in_specs=[HBM, HBM, SEM, SEM] + [ANY] * len(after), out_specs=(HBM, HBM),
        input_output_aliases={0: 0, 1: 1},
        compiler_params=pltpu.CompilerParams(has_side_effects=EFFECT),
    )(v, land, send_sems, recv_sems, *after)[1]


def _half_rows(ref, chip, which, half):
    return ref.at[chip, pl.ds(pl.multiple_of(which * half, half), half), :]


def _ici_peers(x, y, c):
    peers = [(_flip(x, k & 2), _flip(y, k & 1), c) for k in (1, 2, 3)]
    return [(peer, _chip_of(peer[0], peer[1])) for peer in peers]


def _part_rows(ref, chip, core, part):
    quarter = ref.shape[1] // 4
    return ref.at[chip, pl.ds(pl.multiple_of((2 * core + part) * quarter, quarter), quarter), :]


def _neighbours(x, y, c):
    return [((x, 1 - y, c), _chip_of(x, 1 - y)), ((1 - x, y, c), _chip_of(1 - x, y)),
            ((1 - x, 1 - y, c), _chip_of(1 - x, 1 - y))]


def _start_direct(buf, send_sems, recv_sems):
    x, y, c = _my_place()
    me = _chip_of(x, y)
    for n, (peer, _) in enumerate(_neighbours(x, y, c)[0:2]):
        for part in ((0, 1), (1, 0))[n]:
            piece = _part_rows(buf, me, c, part)
            pltpu.make_async_remote_copy(
                src_ref=piece, dst_ref=piece, send_sem=send_sems.at[2 * n + part], recv_sem=recv_sems.at[2 * n + part],
                device_id=peer, device_id_type=MESH).start()


def _relay(buf, recv_sems, relay_send, relay_recv):
    x, y, c = _my_place()
    nbrs = _neighbours(x, y, c)
    for n in range(2):
        part = n
        piece = _part_rows(buf, nbrs[n][1], c, part)
        pltpu.make_async_remote_copy(
            src_ref=piece, dst_ref=piece, send_sem=relay_send.at[part], recv_sem=recv_sems.at[2 * n + part],
            device_id=nbrs[n][0], device_id_type=MESH).wait_recv()
        pltpu.make_async_remote_copy(
            src_ref=piece, dst_ref=piece, send_sem=relay_send.at[part], recv_sem=relay_recv.at[part],
            device_id=nbrs[1 - n][0], device_id_type=MESH).start()


def _gather_start(win_slots, after):
    def body(win_in, after_ref, win_ref, send_sems, recv_sems, token_ref):
        del win_in, after_ref
        _start_direct(win_ref, send_sems, recv_sems)
        token_ref[...] = jnp.zeros(token_ref.shape, F32)

    sems = pltpu.SemaphoreType.DMA((4,))
    return pl.pallas_call(
        body, name="gather_start",
        out_shape=(jax.ShapeDtypeStruct(win_slots.shape, win_slots.dtype), sems, sems, TOKEN),
        in_specs=[HBM, ANY], out_specs=(HBM, SEM, SEM, VMEM),
        input_output_aliases={0: 0},
        compiler_params=pltpu.CompilerParams(has_side_effects=EFFECT),
    )(win_slots, after)


def _gather_relay_in(win, wout_slots, recv_in, after):
    def body(win_in, wout_in, recv_in_ref, after_ref, win_ref, wout_ref, relay_send, relay_recv, send_out, recv_out):
        del win_in, wout_in, after_ref
        _relay(win_ref, recv_in_ref, relay_send, relay_recv)
        _start_direct(wout_ref, send_out, recv_out)

    two, four = pltpu.SemaphoreType.DMA((2,)), pltpu.SemaphoreType.DMA((4,))
    return pl.pallas_call(
        body, name="gather_relay_w_in",
        out_shape=(jax.ShapeDtypeStruct(win.shape, win.dtype), jax.ShapeDtypeStruct(wout_slots.shape, wout_slots.dtype),
                   two, two, four, four),
        in_specs=[HBM, HBM, SEM, ANY], out_specs=(HBM, HBM, SEM, SEM, SEM, SEM),
        input_output_aliases={0: 0, 1: 1},
        compiler_params=pltpu.CompilerParams(has_side_effects=EFFECT),
    )(win, wout_slots, recv_in, after)


def _gather_relay_out(wout, recv_out, after):
    def body(wout_in, recv_out_ref, after_ref, wout_ref, relay_send, relay_recv):
        del wout_in, after_ref
        _relay(wout_ref, recv_out_ref, relay_send, relay_recv)

    two = pltpu.SemaphoreType.DMA((2,))
    return pl.pallas_call(
        body, name="gather_relay_w_out",
        out_shape=(jax.ShapeDtypeStruct(wout.shape, wout.dtype), two, two),
        in_specs=[HBM, SEM, ANY], out_specs=(HBM, SEM, SEM),
        input_output_aliases={0: 0},
        compiler_params=pltpu.CompilerParams(has_side_effects=EFFECT),
    )(wout, recv_out, after)


def _gather_wait_direct(buf, send_sems, recv_sems, after, name):
    def body(buf_in, send_ref, recv_ref, after_ref, buf_ref):
        del buf_in, after_ref
        x, y, c = _my_place()
        me = _chip_of(x, y)
        for n, (peer, chip) in enumerate(_neighbours(x, y, c)[0:2]):
            second = 1 - n
            pltpu.make_async_remote_copy(
                src_ref=_part_rows(buf_ref, me, c, second), dst_ref=_part_rows(buf_ref, chip, c, second),
                send_sem=send_ref.at[2 * n + second], recv_sem=recv_ref.at[2 * n + second],
                device_id=peer, device_id_type=MESH).wait_recv()
            for part in range(2):
                piece = _part_rows(buf_ref, me, c, part)
                pltpu.make_async_remote_copy(
                    src_ref=piece, dst_ref=piece, send_sem=send_ref.at[2 * n + part], recv_sem=recv_ref.at[2 * n + part],
                    device_id=peer, device_id_type=MESH).wait_send()

    return pl.pallas_call(
        body, name=name,
        out_shape=jax.ShapeDtypeStruct(buf.shape, buf.dtype),
        in_specs=[HBM, SEM, SEM, ANY], out_specs=HBM,
        input_output_aliases={0: 0},
        compiler_params=pltpu.CompilerParams(has_side_effects=EFFECT),
    )(buf, send_sems, recv_sems, after)


def _gather_wait_relayed(buf, relay_send, relay_recv, after, name):
    def body(buf_in, rsend_ref, rrecv_ref, after_ref, buf_ref):
        del buf_in, after_ref
        x, y, c = _my_place()
        nbrs = _neighbours(x, y, c)
        for n in range(2):
            relayed = _part_rows(buf_ref, nbrs[n][1], c, n)
            cp = pltpu.make_async_remote_copy(
                src_ref=relayed, dst_ref=_part_rows(buf_ref, nbrs[2][1], c, n),
                send_sem=rsend_ref.at[n], recv_sem=rrecv_ref.at[n], device_id=nbrs[1 - n][0], device_id_type=MESH)
            cp.wait_recv()
            cp.wait_send()

    return pl.pallas_call(
        body, name=name,
        out_shape=jax.ShapeDtypeStruct(buf.shape, buf.dtype),
        in_specs=[HBM, SEM, SEM, ANY], out_specs=HBM,
        input_output_aliases={0: 0},
        compiler_params=pltpu.CompilerParams(has_side_effects=EFFECT),
    )(buf, relay_send, relay_recv, after)


def _forward_halves(buf, which, name):
    half = buf.shape[1] // 2

    def body(buf_in, buf_ref, send_sems, recv_sems):
        del buf_in
        x, y, c = _my_place()
        sibling = (x, y, 1 - c)
        chips = [_neighbours(x, y, c)[n][1] for n in which]
        started = []
        for k, src_chip in enumerate(chips):
            landed = _half_rows(buf_ref, src_chip, c, half)
            fw = pltpu.make_async_remote_copy(
                src_ref=landed, dst_ref=landed, send_sem=send_sems.at[k], recv_sem=recv_sems.at[k],
                device_id=sibling, device_id_type=MESH)
            fw.start()
            started.append(fw)
        for k, src_chip in enumerate(chips):
            other = _half_rows(buf_ref, src_chip, 1 - c, half)
            pltpu.make_async_remote_copy(
                src_ref=other, dst_ref=other, send_sem=send_sems.at[k], recv_sem=recv_sems.at[k],
                device_id=sibling, device_id_type=MESH).wait_recv()
        for fw in started:
            fw.wait_send()

    return pl.pallas_call(
        body, name=name,
        out_shape=jax.ShapeDtypeStruct(buf.shape, buf.dtype),
        in_specs=[HBM], out_specs=HBM,
        input_output_aliases={0: 0},
        scratch_shapes=[pltpu.SemaphoreType.DMA((len(which),))] * 2,
    )(buf)


def _dw_swapped(a, b, row_chunks, col_chunks, name):
    r, t = a.shape
    c_all = b.shape[1]
    chunks = row_chunks * col_chunks
    rq, cq = r // row_chunks, c_all // col_chunks
    half = rq // 2
    tn = COL_TILE
    nt = cq // tn
    steps = col_chunks * nt

    def body(a_ref, b_ref, mine_ref, sib_ref, stage, send_sems, recv_sems):
        x, y, c = _my_place()
        j, n = pl.program_id(0), pl.program_id(1)
        step = j * nt + n
        slot = step % 2
        res = jnp.dot(a_ref[...], b_ref[...], preferred_element_type=F32).astype(BF16)

        def landing(jj, nn):
            cols = pl.ds(pl.multiple_of(nn * tn, tn), tn)
            return sib_ref.at[:, :, cols] if col_chunks == 1 else sib_ref.at[pl.ds(jj, 1), :, cols]

        def copy(slot_, step_, jj, nn):
            return pltpu.make_async_remote_copy(
                src_ref=stage.at[slot_], dst_ref=landing(jj, nn), send_sem=send_sems.at[slot_],
                recv_sem=recv_sems.at[step_], device_id=(x, y, 1 - c), device_id_type=MESH)

        @pl.when(step >= 2)
        def _():
            copy(slot, step, j, n).wait_send()

        for q in range(row_chunks):
            lo = res[q * rq:q * rq + half, :]
            hi = res[q * rq + half:(q + 1) * rq, :]
            mine_ref[q] = jnp.where(c == 0, lo, hi)
            stage[slot, q] = jnp.where(c == 0, hi, lo)
        copy(slot, step, j, n).start()

        @pl.when(step == steps - 1)
        def _():
            for s in range(max(steps - 2, 0), steps):
                copy(s % 2, s, j, n).wait_send()
            for s in range(steps):
                copy(s % 2, s, j, n).wait_recv()

    shape = jax.ShapeDtypeStruct((chunks, half, cq), BF16)
    return pl.pallas_call(
        body, name=name, grid=(col_chunks, nt),
        out_shape=(shape, shape),
        in_specs=[pl.BlockSpec((r, t), lambda j, n: (0, 0)), pl.BlockSpec((t, tn), lambda j, n: (0, j * nt + n))],
        out_specs=(pl.BlockSpec((row_chunks, half, tn), lambda j, n: (j, 0, n)), ANY),
        scratch_shapes=[pltpu.VMEM((2, row_chunks, half, tn), BF16), pltpu.SemaphoreType.DMA((2,)),
                        pltpu.SemaphoreType.DMA((steps,))],
        compiler_params=_params(("arbitrary", "arbitrary")),
    )(a, b)


def _owners_start(csum, name, after=()):
    land = pltpu.with_memory_space_constraint(lax.empty((N_CHIPS - 1,) + csum.shape[1:], csum.dtype), pltpu.HBM)

    def body(csum_ref, land_ref, *rest):
        send_sems, recv_sems, _, _, token_ref = rest[len(after):]
        x, y, c = _my_place()
        for k, (peer, owner) in enumerate(_ici_peers(x, y, c)):
            pltpu.make_async_remote_copy(
                src_ref=csum_ref.at[owner], dst_ref=land_ref.at[k], send_sem=send_sems.at[k], recv_sem=recv_sems.at[k],
                device_id=peer, device_id_type=MESH).start()
        token_ref[...] = jnp.zeros(token_ref.shape, F32)

    sems = pltpu.SemaphoreType.DMA((N_CHIPS - 1,))
    return pl.pallas_call(
        body, name=name,
        out_shape=(sems, sems, jax.ShapeDtypeStruct(csum.shape, csum.dtype),
                   jax.ShapeDtypeStruct(land.shape, land.dtype), TOKEN),
        in_specs=[HBM, HBM] + [ANY] * len(after), out_specs=(SEM, SEM, HBM, HBM, VMEM),
        input_output_aliases={0: 2, 1: 3},
        compiler_params=pltpu.CompilerParams(has_side_effects=EFFECT),
    )(pltpu.with_memory_space_constraint(csum, pltpu.HBM), land, *after)


def _owners_wait(send_sems, recv_sems, csum, land, after, name):
    def body(csum_ref, land_ref, send_ref, recv_ref, *rest):
        del rest
        x, y, c = _my_place()
        for k, (peer, owner) in enumerate(_ici_peers(x, y, c)):
            cp = pltpu.make_async_remote_copy(
                src_ref=csum_ref.at[owner], dst_ref=land_ref.at[k], send_sem=send_ref.at[k], recv_sem=recv_ref.at[k],
                device_id=peer, device_id_type=MESH)
            cp.wait_send()
            cp.wait_recv()

    return pl.pallas_call(
        body, name=name,
        out_shape=(jax.ShapeDtypeStruct(csum.shape, csum.dtype), jax.ShapeDtypeStruct(land.shape, land.dtype)),
        in_specs=[HBM, HBM, SEM, SEM] + [ANY] * len(after), out_specs=(HBM, HBM),
        input_output_aliases={0: 0, 1: 1},
        compiler_params=pltpu.CompilerParams(has_side_effects=EFFECT),
    )(csum, land, send_sems, recv_sems, *after)[1]


def _join_halves(full, name):
    rows = full.shape[0] // 2

    def body(full_in, full_ref, send_sem, recv_sem):
        del full_in
        x, y, c = _my_place()
        sibling = (x, y, 1 - c)
        mine = full_ref.at[pl.ds(pl.multiple_of(c * rows, rows), rows), :]
        theirs = full_ref.at[pl.ds(pl.multiple_of((1 - c) * rows, rows), rows), :]
        cp = pltpu.make_async_remote_copy(
            src_ref=mine, dst_ref=mine, send_sem=send_sem, recv_sem=recv_sem, device_id=sibling, device_id_type=MESH)
        cp.start()
        pltpu.make_async_remote_copy(
            src_ref=theirs, dst_ref=theirs, send_sem=send_sem, recv_sem=recv_sem,
            device_id=sibling, device_id_type=MESH).wait_recv()
        cp.wait_send()

    return pl.pallas_call(
        body, name=name,
        out_shape=jax.ShapeDtypeStruct(full.shape, full.dtype),
        in_specs=[HBM], out_specs=HBM,
        input_output_aliases={0: 0},
        scratch_shapes=[pltpu.SemaphoreType.DMA, pltpu.SemaphoreType.DMA],
    )(full)


def _join_start(full, name, part, parts):
    half = full.shape[0] // 2
    rows = half // parts

    def body(full_in, full_ref, send_sem, recv_sem, token_ref):
        del full_in
        x, y, c = _my_place()
        mine = full_ref.at[pl.ds(pl.multiple_of(c * half + part * rows, rows), rows), :]
        pltpu.make_async_remote_copy(
            src_ref=mine, dst_ref=mine, send_sem=send_sem.at[0], recv_sem=recv_sem.at[0],
            device_id=(x, y, 1 - c), device_id_type=MESH).start()
        token_ref[...] = jnp.zeros(token_ref.shape, F32)

    one = pltpu.SemaphoreType.DMA((1,))
    return pl.pallas_call(
        body, name=name,
        out_shape=(jax.ShapeDtypeStruct(full.shape, full.dtype), one, one, TOKEN),
        in_specs=[HBM], out_specs=(HBM, SEM, SEM, VMEM),
        input_output_aliases={0: 0},
        compiler_params=pltpu.CompilerParams(has_side_effects=EFFECT),
    )(full)


def _join_wait(full, send_sem, recv_sem, after, name, part, parts):
    half = full.shape[0] // 2
    rows = half // parts

    def body(full_in, send_ref, recv_ref, *rest):
        del full_in
        full_ref = rest[-1]
        x, y, c = _my_place()
        cp = pltpu.make_async_remote_copy(
            src_ref=full_ref.at[pl.ds(pl.multiple_of(c * half + part * rows, rows), rows), :],
            dst_ref=full_ref.at[pl.ds(pl.multiple_of((1 - c) * half + part * rows, rows), rows), :],
            send_sem=send_ref.at[0], recv_sem=recv_ref.at[0], device_id=(x, y, 1 - c), device_id_type=MESH)
        cp.wait_send()
        cp.wait_recv()

    return pl.pallas_call(
        body, name=name,
        out_shape=jax.ShapeDtypeStruct(full.shape, full.dtype),
        in_specs=[HBM, SEM, SEM] + [ANY] * len(after), out_specs=HBM,
        input_output_aliases={0: 0},
        compiler_params=pltpu.CompilerParams(has_side_effects=EFFECT),
    )(full, send_sem, recv_sem, *after)


def _cast_into_slot(place, w, name):
    rows, cols = w.shape
    tr = min(rows, ROW_TILE)

    def body(place_ref, w_ref, o_ref):
        del place_ref
        o_ref[...] = w_ref[...].astype(BF16)

    grid_spec = pltpu.PrefetchScalarGridSpec(
        num_scalar_prefetch=1, grid=(rows // tr,),
        in_specs=[pl.BlockSpec((tr, cols), lambda i, p: (i, 0))],
        out_specs=pl.BlockSpec((None, tr, cols), lambda i, p: (p[0], i, 0)))
    return pl.pallas_call(
        body, name=name, grid_spec=grid_spec,
        out_shape=jax.ShapeDtypeStruct((N_CHIPS, rows, cols), BF16),
        compiler_params=_params(("parallel",)),
    )(place, w)


def _ada_modulation(packed, w_ada, b_ada, d, after=()):
    rows_per, n = packed.shape
    d_model, wa = w_ada.shape

    def body(v_ref, w_hbm, b_ref, *rest):
        all_ref, mod_ref, w_vmem, part_ref, parts_ref, load_sem, send1, recv1, send2, recv2 = rest[len(after):]
        x, y, c = _my_place()
        me = 4 * x + 2 * y + c
        chip = _chip_of(x, y)
        load = pltpu.make_async_copy(w_hbm, w_vmem, load_sem)
        load.start()

        def rows(idx):
            return all_ref.at[pl.ds(pl.multiple_of(idx * rows_per, rows_per), rows_per), :]

        all_ref[pl.ds(pl.multiple_of(me * rows_per, rows_per), rows_per), :] = v_ref[...]
        copies = []
        for k in range(1, N_DEV):
            peer = (_flip(x, k & 4), _flip(y, k & 2), _flip(c, k & 1))
            cp = pltpu.make_async_remote_copy(
                src_ref=v_ref, dst_ref=rows(me), send_sem=send1.at[k - 1], recv_sem=recv1.at[k - 1],
                device_id=peer, device_id_type=MESH)
            cp.start()
            copies.append((cp, peer))
        for k, (cp, peer) in enumerate(copies):
            pltpu.make_async_remote_copy(
                src_ref=v_ref, dst_ref=rows(4 * peer[0] + 2 * peer[1] + peer[2]), send_sem=send1.at[k],
                recv_sem=recv1.at[k], device_id=peer, device_id_type=MESH).wait_recv()
        for cp, _ in copies:
            cp.wait_send()

        def c_of(dev):
            segments, pos = [], 0
            while pos < d:
                row, col = divmod(pos, n)
                take = min(d - pos, n - col)
                segments.append(all_ref[dev * rows_per + row:dev * rows_per + row + 1, col:col + take])
                pos += take
            return jnp.concatenate(segments, axis=1)

        c_all = jnp.concatenate([c_of(dev) for dev in range(N_DEV)], axis=0)
        load.wait()
        part_ref[...] = jnp.dot(_silu(c_all), w_vmem[...], precision=lax.Precision.HIGHEST, preferred_element_type=F32)
        parts_ref[chip] = part_ref[...]
        swaps = []
        for k, (peer, _) in enumerate(_ici_peers(x, y, c)):
            cp = pltpu.make_async_remote_copy(
                src_ref=part_ref, dst_ref=parts_ref.at[chip], send_sem=send2.at[k], recv_sem=recv2.at[k],
                device_id=peer, device_id_type=MESH)
            cp.start()
            swaps.append(cp)
        for k, (peer, peer_chip) in enumerate(_ici_peers(x, y, c)):
            pltpu.make_async_remote_copy(
                src_ref=part_ref, dst_ref=parts_ref.at[peer_chip], send_sem=send2.at[k], recv_sem=recv2.at[k],
                device_id=peer, device_id_type=MESH).wait_recv()
        for cp in swaps:
            cp.wait_send()
        flat = jnp.concatenate([parts_ref[j, pl.ds(me, 1), :] for j in range(N_CHIPS)], axis=1) + b_ref[...]
        mod_ref[...] = jnp.concatenate([flat[:, i * d:(i + 1) * d] for i in range(3)], axis=0)

    return pl.pallas_call(
        body, name="ada_modulation",
        out_shape=(jax.ShapeDtypeStruct((N_DEV * rows_per, n), F32), jax.ShapeDtypeStruct((3, d), F32)),
        in_specs=[VMEM, ANY, VMEM] + [ANY] * len(after), out_specs=(VMEM, VMEM),
        scratch_shapes=[pltpu.VMEM((d_model, wa), F32), pltpu.VMEM((N_DEV, wa), F32),
                        pltpu.VMEM((N_CHIPS, N_DEV, wa), F32), pltpu.SemaphoreType.DMA,
                        pltpu.SemaphoreType.DMA((N_DEV - 1,)), pltpu.SemaphoreType.DMA((N_DEV - 1,)),
                        pltpu.SemaphoreType.DMA((N_CHIPS - 1,)), pltpu.SemaphoreType.DMA((N_CHIPS - 1,))],
        compiler_params=_params(),
    )(packed, w_ada, b_ada, *after)


def _prenorm(x, mod, g_pre, after):
    t, d = x.shape
    tb = ROW_TILE

    def body(x_ref, mod_ref, g_ref, after_ref, h_ref, ht_ref):
        del after_ref
        xv = x_ref[...]
        r = lax.rsqrt(jnp.mean(xv * xv, axis=-1, keepdims=True) + EPS)
        h = (xv * r) * g_ref[...] * (1.0 + mod_ref[1:2, :]) + mod_ref[0:1, :]
        h_ref[...] = h.astype(BF16)
        ht_ref[...] = h.T.astype(BF16)

    return pl.pallas_call(
        body, name="prenorm", grid=(t // tb,),
        out_shape=(jax.ShapeDtypeStruct((t, d), BF16), jax.ShapeDtypeStruct((d, t), BF16)),
        in_specs=[pl.BlockSpec((tb, d), lambda i: (i, 0)), pl.BlockSpec((3, d), lambda i: (0, 0)),
                  pl.BlockSpec((1, d), lambda i: (0, 0)), ANY],
        out_specs=(pl.BlockSpec((tb, d), lambda i: (i, 0)), pl.BlockSpec((d, tb), lambda i: (0, i))),
        compiler_params=_params(("parallel",)),
    )(x, mod, g_pre, after)


def _proj_chunks(proj, h, w, chunks, name):
    t, d = h.shape
    ws = w.shape[-1]
    tn = COL_TILE
    nt = ws // tn

    def body(chunk_ref, *refs):
        del chunk_ref
        a_ref, b_ref, o_ref = refs[-3:]
        o_ref[...] = jnp.dot(a_ref[...], b_ref[...].astype(BF16), preferred_element_type=F32).astype(BF16)

    if w.ndim == 3:
        w_spec = pl.BlockSpec((None, d, tn), lambda i, n, ch: (ch[i], 0, n))
    else:
        w_spec = pl.BlockSpec((d, tn), lambda i, n, ch: (0, n))
    first = proj is None
    grid_spec = pltpu.PrefetchScalarGridSpec(
        num_scalar_prefetch=1, grid=(chunks.shape[0], nt),
        in_specs=([] if first else [HBM]) + [pl.BlockSpec((t, d), lambda i, n, ch: (0, 0)), w_spec],
        out_specs=pl.BlockSpec((t, tn), lambda i, n, ch: (0, ch[i] * nt + n)))
    return pl.pallas_call(
        body, name=name, grid_spec=grid_spec,
        out_shape=jax.ShapeDtypeStruct((t, N_CHIPS * ws), BF16),
        input_output_aliases={} if first else {1: 0},
        compiler_params=_params(("parallel", "parallel")),
    )(*([chunks] if first else [chunks, proj]), h, w)


def _shift_rows(a, rows):
    idx = lax.broadcasted_iota(jnp.int32, a.shape, 0)
    prev = jnp.where(idx == 0, 0.0, pltpu.roll(a, 1, 0))
    nxt = jnp.where(idx == rows - 1, 0.0, pltpu.roll(a, rows - 1, 0))
    return prev, nxt


def _conv_fwd(conv_proj, conv_w, conv_b, dc):
    t = conv_proj.shape[0]
    ct = CONV_TILE
    nct = dc // ct

    def body(u_ref, cg_ref, w_ref, b_ref, co_ref):
        a = cg_ref[...].astype(F32) * u_ref[...].astype(F32)
        prev, nxt = _shift_rows(a, t)
        co_ref[...] = (w_ref[0:1, :] * prev + w_ref[1:2, :] * a + w_ref[2:3, :] * nxt + b_ref[...]).astype(BF16)

    return pl.pallas_call(
        body, name="conv_fwd", grid=(nct,),
        out_shape=jax.ShapeDtypeStruct((t, dc), BF16),
        in_specs=[pl.BlockSpec((t, ct), lambda i: (0, i)), pl.BlockSpec((t, ct), lambda i: (0, 2 * nct + i)),
                  pl.BlockSpec((3, ct), lambda i: (0, i)), pl.BlockSpec((1, ct), lambda i: (0, i))],
        out_specs=pl.BlockSpec((t, ct), lambda i: (0, i)),
        compiler_params=_params(("parallel",)),
    )(conv_proj, conv_proj, conv_w, conv_b)


def _to_residue_major(src_ref, dst_ref, r):
    seq = src_ref.shape[0] // r
    for res in range(r):
        dst_ref[res * seq:(res + 1) * seq, :] = src_ref[pl.ds(res, seq, stride=r), :].astype(dst_ref.dtype)


def _branch_operands(token_refs, stage, dil, r):
    if r == 1:
        return list(token_refs)
    for i, ref in enumerate(token_refs):
        stage[...] = ref[...].astype(F32)
        _to_residue_major(stage, dil.at[i], r)
    return [dil.at[i] for i in range(len(token_refs))]


def _scaled_queries(q):
    return (q.astype(F32) * (HEAD_DIM ** -0.5)).astype(BF16)


BLOCK_SHIFTS = (0, -SIDE, None)


def _band_bias(rel, slope):
    arel = jnp.abs(rel)
    return jnp.where(arel <= SIDE, arel.astype(F32) * slope, NEG_INF)


def _fill_bias_tiles(bias_ref, sl_ref, r, kw):
    base = lax.broadcasted_iota(jnp.int32, (ATT_BQ, kw), 1) - lax.broadcasted_iota(jnp.int32, (ATT_BQ, kw), 0)
    for hh in range(2):
        slope = -(sl_ref[hh:hh + 1, 0:kw] * float(r))
        for e, shift in enumerate(BLOCK_SHIFTS):
            shift = ATT_BQ - kw if shift is None else shift
            bias_ref[hh, e, :, 0:kw] = _band_bias(base + shift, slope)


def _fill_stacked_bias_tiles(bias_ref, sl_ref, r, kw):
    base = lax.broadcasted_iota(jnp.int32, (kw, ATT_BQ), 0) - lax.broadcasted_iota(jnp.int32, (kw, ATT_BQ), 1)
    for hh in range(2):
        slope = -(sl_ref[hh:hh + 1, 0:ATT_BQ] * float(r))
        for e, shift in enumerate(BLOCK_SHIFTS):
            shift = ATT_BQ - kw if shift is None else shift
            bias_ref[e, 0:kw, hh * ATT_BQ:(hh + 1) * ATT_BQ] = _band_bias(base + shift, slope)


def _first_head_lanes():
    return lax.broadcasted_iota(jnp.int32, (1, PAIR), 1) < HEAD_DIM


def _only_head(x, first, hh):
    return jnp.where(first if hh == 0 else jnp.logical_not(first), x, jnp.zeros_like(x))


def _block_place(g, seq_len, kw):
    nqb = seq_len // ATT_BQ
    if nqb == 1:
        row = pl.multiple_of(g * ATT_BQ, ATT_BQ)
        return row, row, 0
    res = g // nqb
    qb = g - res * nqb
    q0 = qb * ATT_BQ
    ks = jnp.clip(q0 - SIDE, 0, seq_len - kw)
    edge = jnp.where(qb == 0, 0, jnp.where(qb == nqb - 1, 2, 1))
    return (pl.multiple_of(res * seq_len + q0, ATT_BQ), pl.multiple_of(res * seq_len + ks, SIDE), edge)


def _qkv_specs(dc, da, t, index):
    return [pl.BlockSpec((t, PAIR), functools.partial(index, (4 * dc + comp * da) // PAIR)) for comp in range(3)]


def _attn_fwd(proj, slopes, dc, da):
    t = proj.shape[0]
    hp = da // PAIR
    n_blocks = t // ATT_BQ

    def body(q_ref, k_ref, v_ref, sl_ref, o_ref, lse_ref, stage, dil, bias, o_res, l_res, o_tok, l_tok):
        for b, (_, r) in enumerate(BRANCHES):
            seq_len = t // r
            kw = min(ATT_KW, seq_len)
            ops = _branch_operands([q_ref, k_ref, v_ref], stage, dil, r)
            _fill_bias_tiles(bias, sl_ref, r, kw)
            o_dst, l_dst = (o_tok.at[b], l_tok.at[b]) if r == 1 else (o_res, l_res)
            first = _first_head_lanes()

            def blocks(trip, carry, seq_len=seq_len, kw=kw, o_dst=o_dst, l_dst=l_dst, first=first, ops=ops):
                nt = (((1,), (1,)), ((), ()))
                places = [_block_place(trip * ATT_UNROLL + i, seq_len, kw) for i in range(ATT_UNROLL)]
                chains = [(i, hh) for i in range(ATT_UNROLL) for hh in range(2)]
                qs = [_scaled_queries(ops[0][pl.ds(qrow, ATT_BQ), :]) for qrow, _, _ in places]
                ks = [ops[1][pl.ds(krow, kw), :] for _, krow, _ in places]
                vs = [ops[2][pl.ds(krow, kw), :] for _, krow, _ in places]
                ss = [lax.dot_general(_only_head(qs[i], first, hh), ks[i], nt, preferred_element_type=F32)
                      + bias[hh, places[i][2], :, 0:kw] for i, hh in chains]
                tops = [jnp.max(s, axis=-1, keepdims=True) for s in ss]
                ps = [jnp.exp(s - m) for s, m in zip(ss, tops)]
                dens = [jnp.sum(p, axis=-1, keepdims=True) for p in ps]
                for i, (qrow, _, _) in enumerate(places):
                    weights = jnp.concatenate([ps[2 * i].astype(BF16), ps[2 * i + 1].astype(BF16)], axis=1)
                    values = jnp.concatenate([_only_head(vs[i], first, 0), _only_head(vs[i], first, 1)], axis=0)
                    den = jnp.where(first, dens[2 * i], dens[2 * i + 1])
                    o_dst[pl.ds(qrow, ATT_BQ), :] = jnp.dot(weights, values, preferred_element_type=F32) / den
                    l_dst[pl.ds(qrow, ATT_BQ), :] = jnp.where(first, tops[2 * i], tops[2 * i + 1]) + jnp.log(den)
                return carry

            lax.fori_loop(0, n_blocks // ATT_UNROLL, blocks, 0)
            if r > 1:
                for res in range(r):
                    rows = slice(res * seq_len, (res + 1) * seq_len)
                    o_tok[b, pl.ds(res, seq_len, stride=r), :] = o_res[rows, :]
                    l_tok[b, pl.ds(res, seq_len, stride=r), :] = l_res[rows, :]

        def merge(i, carry):
            rows = pl.ds(pl.multiple_of(i * ROW_TILE, ROW_TILE), ROW_TILE)
            la, lb, lc = l_tok[0, rows, :], l_tok[1, rows, :], l_tok[2, rows, :]
            m = jnp.maximum(jnp.maximum(la, lb), lc)
            wa, wb, wc = jnp.exp(la - m), jnp.exp(lb - m), jnp.exp(lc - m)
            den = wa + wb + wc
            o_ref[rows, :] = (wa * o_tok[0, rows, :] + wb * o_tok[1, rows, :] + wc * o_tok[2, rows, :]) * (1.0 / den)
            lse_ref[rows, :] = m + jnp.log(den)
            return carry

        lax.fori_loop(0, t // ROW_TILE, merge, 0)

    pair_spec = pl.BlockSpec((None, t, PAIR), lambda h: (h, 0, 0))
    return pl.pallas_call(
        body, name="attn_fwd", grid=(hp,),
        out_shape=(jax.ShapeDtypeStruct((hp, t, PAIR), F32), jax.ShapeDtypeStruct((hp, t, PAIR), F32)),
        in_specs=_qkv_specs(dc, da, t, lambda first, h: (0, first + h))
        + [pl.BlockSpec((None, 8, ATT_KW), lambda h: (h, 0, 0))],
        out_specs=(pair_spec, pair_spec),
        scratch_shapes=[pltpu.VMEM((t, PAIR), F32), pltpu.VMEM((3, t, PAIR), BF16),
                        pltpu.VMEM((2, 3, ATT_BQ, ATT_KW), F32),
                        pltpu.VMEM((t, PAIR), F32), pltpu.VMEM((t, PAIR), F32),
                        pltpu.VMEM((3, t, PAIR), F32), pltpu.VMEM((3, t, PAIR), F32)],
        compiler_params=_params(("parallel",)),
    )(proj, proj, proj, slopes)


def _attn_bwd(dproj, proj, d_o, lse, delta, slopes, dc, da, after):
    t = proj.shape[0]
    hp = da // PAIR
    n_blocks = t // ATT_BQ

    def all_branches(q_ref, k_ref, v_ref, do_ref, lse_ref, dl_ref, sl_ref,
                     stage, dil, packed, packed_res, row_vecs, bias_t, acc, tot):
        first = _first_head_lanes()
        lane = lax.broadcasted_iota(jnp.int32, (1, PAIR), 1)
        packed[...] = jnp.where((lane & (HEAD_DIM - 1)) < HEAD_DIM // 2, lse_ref[...], dl_ref[...])
        for b, (_, r) in enumerate(BRANCHES):
            seq_len = t // r
            kw = min(ATT_KW, seq_len)
            ops = _branch_operands([q_ref, k_ref, v_ref, do_ref], stage, dil, r)
            scalars = packed
            if r > 1:
                _to_residue_major(packed, packed_res, r)
                scalars = packed_res
            for g in range(n_blocks):
                flipped = scalars[g * ATT_BQ:(g + 1) * ATT_BQ, :].T
                for row in range(4):
                    row_vecs[g, row:row + 1, :] = flipped[row * (HEAD_DIM // 2):row * (HEAD_DIM // 2) + 1, :]
            _fill_stacked_bias_tiles(bias_t, sl_ref, r, kw)
            acc[1] = jnp.zeros((t, PAIR), F32)
            acc[2] = jnp.zeros((t, PAIR), F32)

            def blocks(trip, carry, seq_len=seq_len, kw=kw, ops=ops):
                nt = (((1,), (1,)), ((), ()))
                group = range(ATT_UNROLL)
                places = [_block_place(trip * ATT_UNROLL + i, seq_len, kw) for i in group]
                ks, vs, q2s, do2s, lse2s, dl2s = [], [], [], [], [], []
                for i, (qrow, krow, _) in zip(group, places):
                    q = _scaled_queries(ops[0][pl.ds(qrow, ATT_BQ), :])
                    dov = ops[3][pl.ds(qrow, ATT_BQ), :]
                    ks.append(ops[1][pl.ds(krow, kw), :])
                    vs.append(ops[2][pl.ds(krow, kw), :])
                    q2s.append(jnp.concatenate([_only_head(q, first, 0), _only_head(q, first, 1)], axis=0))
                    do2s.append(jnp.concatenate([_only_head(dov, first, 0), _only_head(dov, first, 1)], axis=0))
                    rows = row_vecs[trip * ATT_UNROLL + i]
                    lse2s.append(jnp.concatenate([rows[0:1, :], rows[2:3, :]], axis=1))
                    dl2s.append(jnp.concatenate([rows[1:2, :], rows[3:4, :]], axis=1))
                s_ts = [lax.dot_general(ks[i], q2s[i], nt, preferred_element_type=F32) for i in group]
                dp_ts = [lax.dot_general(vs[i], do2s[i], nt, preferred_element_type=F32) for i in group]
                p_ts = [jnp.exp(s_ts[i] + bias_t[places[i][2], 0:kw, :] - lse2s[i]) for i in group]
                ds_ts = [p_ts[i] * (dp_ts[i] - dl2s[i]) for i in group]
                dvs = [jnp.dot(p_ts[i].astype(BF16), do2s[i], preferred_element_type=F32) for i in group]
                dks = [jnp.dot(ds_ts[i].astype(BF16), q2s[i], preferred_element_type=F32) for i in group]
                dss = [ds_ts[i].T.astype(BF16) for i in group]
                dqs = [jnp.dot(dss[i][0:ATT_BQ, :], _only_head(ks[i], first, 0), preferred_element_type=F32)
                       + jnp.dot(dss[i][ATT_BQ:2 * ATT_BQ, :], _only_head(ks[i], first, 1), preferred_element_type=F32)
                       for i in group]
                for i, (qrow, krow, _) in zip(group, places):
                    acc[0, pl.ds(qrow, ATT_BQ), :] = dqs[i] * (HEAD_DIM ** -0.5)
                    acc[1, pl.ds(krow, kw), :] += dks[i]
                    acc[2, pl.ds(krow, kw), :] += dvs[i]
                return carry

            lax.fori_loop(0, n_blocks // ATT_UNROLL, blocks, 0)
            for comp in range(3):
                if r == 1:
                    tot[comp] = acc[comp]
                else:
                    for res in range(r):
                        tok = pl.ds(res, seq_len, stride=r)
                        tot[comp, tok, :] = tot[comp, tok, :] + acc[comp, res * seq_len:(res + 1) * seq_len, :]

    first_q = (4 * dc) // PAIR

    def body(dproj_in, q_ref, k_ref, v_ref, do_ref, lse_ref, dl_ref, sl_ref, after_ref, out_ref, *scratch):
        del dproj_in, after_ref
        work, out_stage, out_sems = scratch[:-2], scratch[-2], scratch[-1]
        h = pl.program_id(0)
        all_branches(q_ref, k_ref, v_ref, do_ref, lse_ref, dl_ref, sl_ref, *work)

        def out_copy(comp):
            cols = pl.ds(pl.multiple_of((first_q + comp * hp + h) * PAIR, PAIR), PAIR)
            return pltpu.make_async_copy(out_stage.at[comp], out_ref.at[:, cols], out_sems.at[comp])

        @pl.when(h > 0)
        def _():
            for comp in range(3):
                out_copy(comp).wait()

        for comp in range(3):
            out_stage[comp] = work[-1][comp].astype(BF16)
            out_copy(comp).start()

        @pl.when(h == hp - 1)
        def _():
            for comp in range(3):
                out_copy(comp).wait()

    pair_spec = pl.BlockSpec((None, t, PAIR), lambda h: (h, 0, 0))
    return pl.pallas_call(
        body, name="attn_bwd", grid=(hp,),
        out_shape=jax.ShapeDtypeStruct(dproj.shape, BF16),
        in_specs=[HBM] + _qkv_specs(dc, da, t, lambda first, h: (0, first + h))
        + [pair_spec, pair_spec, pair_spec, pl.BlockSpec((None, 8, ATT_KW), lambda h: (h, 0, 0)), ANY],
        out_specs=ANY,
        input_output_aliases={0: 0},
        scratch_shapes=[pltpu.VMEM((t, PAIR), F32), pltpu.VMEM((4, t, PAIR), BF16),
                        pltpu.VMEM((t, PAIR), F32), pltpu.VMEM((t, PAIR), F32),
                        pltpu.VMEM((n_blocks, 8, ATT_BQ), F32), pltpu.VMEM((3, ATT_KW, 2 * ATT_BQ), F32),
                        pltpu.VMEM((3, t, PAIR), F32), pltpu.VMEM((3, t, PAIR), F32),
                        pltpu.VMEM((3, t, PAIR), BF16), pltpu.SemaphoreType.DMA((3,))],
        compiler_params=_params(("arbitrary",)),
    )(dproj, proj, proj, proj, d_o, lse, delta, slopes, after)


def _mix_fwd(co, proj, o_mix, g_conv, g_attn_pairs):
    t, dc = co.shape
    hp = o_mix.shape[0]
    da = hp * PAIR
    tb = ROW_TILE

    def body(co_ref, bg_ref, zc_ref, za_ref, om_ref, gc_ref, ga_ref, ycat_ref, ycatt_ref):
        p = bg_ref[...].astype(F32) * co_ref[...].astype(F32)
        rc = lax.rsqrt(jnp.mean(p * p, axis=-1, keepdims=True) + EPS)
        yc = (p * rc) * gc_ref[...] * _silu(zc_ref[...].astype(F32))
        ycat_ref[:, 0:dc] = yc.astype(BF16)
        ycatt_ref[0:dc, :] = yc.T.astype(BF16)
        ssq = jnp.zeros((tb, 1), F32)
        for h in range(hp):
            o = om_ref[h]
            ssq = ssq + jnp.sum(o * o, axis=-1, keepdims=True)
        ra = lax.rsqrt(ssq * (1.0 / da) + EPS)
        for h in range(hp):
            ya = (om_ref[h] * ra) * ga_ref[h] * _silu(za_ref[:, h * PAIR:(h + 1) * PAIR].astype(F32))
            ycat_ref[:, dc + h * PAIR:dc + (h + 1) * PAIR] = ya.astype(BF16)
            ycatt_ref[dc + h * PAIR:dc + (h + 1) * PAIR, :] = ya.T.astype(BF16)

    pair_spec = pl.BlockSpec((hp, tb, PAIR), lambda i: (0, i, 0))
    return pl.pallas_call(
        body, name="mix_fwd", grid=(t // tb,),
        out_shape=(jax.ShapeDtypeStruct((t, dc + da), BF16), jax.ShapeDtypeStruct((dc + da, t), BF16)),
        in_specs=[pl.BlockSpec((tb, dc), lambda i: (i, 0)),
                  pl.BlockSpec((tb, dc), lambda i: (i, 1)),
                  pl.BlockSpec((tb, dc), lambda i: (i, 3)),
                  pl.BlockSpec((tb, da), lambda i: (i, 7)),
                  pair_spec,
                  pl.BlockSpec((1, dc), lambda i: (0, 0)),
                  pl.BlockSpec((hp, 1, PAIR), lambda i: (0, 0, 0))],
        out_specs=(pl.BlockSpec((tb, dc + da), lambda i: (i, 0)), pl.BlockSpec((dc + da, tb), lambda i: (0, i))),
        compiler_params=_params(("parallel",)),
    )(co, proj, proj, proj, o_mix, g_conv, g_attn_pairs)


def _out_fwd_bwd(ycat, woutf, x, target, mod, g_post):
    t, d = x.shape
    n = ycat.shape[1]
    tb = ROW_TILE

    def body(a_ref, w_ref, x_ref, tg_ref, mod_ref, g_ref, dout_ref, dy_ref, acc_ref):
        y = jnp.dot(a_ref[...], w_ref[...], preferred_element_type=F32)
        r = lax.rsqrt(jnp.mean(y * y, axis=-1, keepdims=True) + EPS)
        nh = y * r
        gate = mod_ref[2:3, :]
        nrm = nh * g_ref[...]
        err = x_ref[...] + gate * nrm - tg_ref[...]
        dout = err * (1.0 / d)
        dout_ref[...] = dout.astype(BF16)
        dn = dout * gate
        a = dn * g_ref[...]
        dy = r * (a - nh * jnp.mean(a * nh, axis=-1, keepdims=True))
        dy_ref[...] = dy.astype(BF16)
        loss = 0.5 * jnp.sum(jnp.sum(err * err, axis=-1, keepdims=True) * (1.0 / d), axis=0, keepdims=True)
        part = jnp.concatenate(
            [jnp.sum(dout * nrm, axis=0, keepdims=True), jnp.sum(dn * nh, axis=0, keepdims=True),
             jnp.broadcast_to(loss, (1, d)), jnp.zeros((5, d), F32)], axis=0)

        @pl.when(pl.program_id(0) == 0)
        def _():
            acc_ref[...] = jnp.zeros(acc_ref.shape, F32)

        acc_ref[...] += part

    return pl.pallas_call(
        body, name="out_fwd_bwd", grid=(t // tb,),
        out_shape=(jax.ShapeDtypeStruct((t, d), BF16), jax.ShapeDtypeStruct((t, d), BF16),
                   jax.ShapeDtypeStruct((8, d), F32)),
        in_specs=[pl.BlockSpec((tb, n), lambda i: (i, 0)), pl.BlockSpec((n, d), lambda i: (0, 0)),
                  pl.BlockSpec((tb, d), lambda i: (i, 0)), pl.BlockSpec((tb, d), lambda i: (i, 0)),
                  pl.BlockSpec((3, d), lambda i: (0, 0)), pl.BlockSpec((1, d), lambda i: (0, 0))],
        out_specs=(pl.BlockSpec((tb, d), lambda i: (i, 0)), pl.BlockSpec((tb, d), lambda i: (i, 0)),
                   pl.BlockSpec((8, d), lambda i: (0, 0))),
        compiler_params=_params(("arbitrary",)),
    )(ycat, woutf, x, target, mod, g_post)


def _matmul_nt(a, b, out_dtype, name):
    m, k = a.shape
    n = b.shape[0]
    tn = COL_TILE

    def body(a_ref, b_ref, o_ref):
        o_ref[...] = lax.dot_general(a_ref[...], b_ref[...], (((1,), (1,)), ((), ())),
                                     preferred_element_type=F32).astype(out_dtype)

    return pl.pallas_call(
        body, name=name, grid=(n // tn,),
        out_shape=jax.ShapeDtypeStruct((m, n), out_dtype),
        in_specs=[pl.BlockSpec((m, k), lambda i: (0, 0)), pl.BlockSpec((tn, k), lambda i: (i, 0))],
        out_specs=pl.BlockSpec((m, tn), lambda i: (0, i)),
        compiler_params=_params(("parallel",)),
    )(a, b)


def _mix_bwd(dycat, co, proj, o_mix, g_conv, g_attn_pairs):
    t, dc = co.shape
    hp = o_mix.shape[0]
    da = hp * PAIR
    tb = ROW_TILE

    def body(dy_ref, co_ref, bg_ref, zc_ref, za_ref, om_ref, gc_ref, ga_ref,
             dcp_ref, dco_ref, do_ref, dl_ref, dgc_ref, dga_ref):
        first = pl.program_id(0) == 0
        cov = co_ref[...].astype(F32)
        bg = bg_ref[...].astype(F32)
        zc = zc_ref[...].astype(F32)
        p = bg * cov
        rc = lax.rsqrt(jnp.mean(p * p, axis=-1, keepdims=True) + EPS)
        nh = p * rc
        dyc = dy_ref[:, 0:dc].astype(F32)
        dn = dyc * _silu(zc)
        a = dn * gc_ref[...]
        dp = rc * (a - nh * jnp.mean(a * nh, axis=-1, keepdims=True))
        dcp_ref[:, 0:dc] = jnp.zeros((tb, dc), BF16)
        dcp_ref[:, dc:2 * dc] = (dp * cov).astype(BF16)
        dcp_ref[:, 2 * dc:3 * dc] = jnp.zeros((tb, dc), BF16)
        dcp_ref[:, 3 * dc:4 * dc] = (dyc * nh * gc_ref[...] * _silu_grad(zc)).astype(BF16)
        dcp_ref[:, 4 * dc:4 * dc + 3 * da] = jnp.zeros((tb, 3 * da), BF16)
        dco_ref[...] = dp * bg

        @pl.when(first)
        def _():
            dgc_ref[...] = jnp.zeros(dgc_ref.shape, F32)
            dga_ref[...] = jnp.zeros(dga_ref.shape, F32)

        dgc_ref[...] += jnp.sum(dn * nh, axis=0, keepdims=True)

        ssq = jnp.zeros((tb, 1), F32)
        for h in range(hp):
            o = om_ref[h]
            ssq = ssq + jnp.sum(o * o, axis=-1, keepdims=True)
        ra = lax.rsqrt(ssq * (1.0 / da) + EPS)
        dot_an = jnp.zeros((tb, 1), F32)
        for h in range(hp):
            nha = om_ref[h] * ra
            za = za_ref[:, h * PAIR:(h + 1) * PAIR].astype(F32)
            dya = dy_ref[:, dc + h * PAIR:dc + (h + 1) * PAIR].astype(F32)
            dna = dya * _silu(za)
            dza = (dya * nha * ga_ref[h] * _silu_grad(za)).astype(BF16)
            dcp_ref[:, 4 * dc + 3 * da + h * PAIR:4 * dc + 3 * da + (h + 1) * PAIR] = dza
            dga_ref[h] += jnp.sum(dna * nha, axis=0, keepdims=True)
            dot_an = dot_an + jnp.sum(dna * ga_ref[h] * nha, axis=-1, keepdims=True)
        mean_an = dot_an * (1.0 / da)
        first_head = lax.broadcasted_iota(jnp.int32, (tb, PAIR), 1) < HEAD_DIM
        for h in range(hp):
            o = om_ref[h]
            nha = o * ra
            za = za_ref[:, h * PAIR:(h + 1) * PAIR].astype(F32)
            dya = dy_ref[:, dc + h * PAIR:dc + (h + 1) * PAIR].astype(F32)
            aa = dya * _silu(za) * ga_ref[h]
            d_o = ra * (aa - nha * mean_an)
            do_ref[h] = d_o.astype(BF16)
            prod = d_o * o
            both = jnp.sum(prod, axis=-1, keepdims=True)
            head0 = jnp.sum(jnp.where(first_head, prod, 0.0), axis=-1, keepdims=True)
            dl_ref[h] = jnp.where(first_head, head0, both - head0)

    pair_spec = pl.BlockSpec((hp, tb, PAIR), lambda i: (0, i, 0))
    return pl.pallas_call(
        body, name="mix_bwd", grid=(t // tb,),
        out_shape=(jax.ShapeDtypeStruct((t, 4 * dc + 4 * da), BF16), jax.ShapeDtypeStruct((t, dc), F32),
                   jax.ShapeDtypeStruct((hp, t, PAIR), BF16), jax.ShapeDtypeStruct((hp, t, PAIR), F32),
                   jax.ShapeDtypeStruct((1, dc), F32), jax.ShapeDtypeStruct((hp, 1, PAIR), F32)),
        in_specs=[pl.BlockSpec((tb, dc + da), lambda i: (i, 0)),
                  pl.BlockSpec((tb, dc), lambda i: (i, 0)),
                  pl.BlockSpec((tb, dc), lambda i: (i, 1)),
                  pl.BlockSpec((tb, dc), lambda i: (i, 3)),
                  pl.BlockSpec((tb, da), lambda i: (i, 7)),
                  pair_spec,
                  pl.BlockSpec((1, dc), lambda i: (0, 0)),
                  pl.BlockSpec((hp, 1, PAIR), lambda i: (0, 0, 0))],
        out_specs=(pl.BlockSpec((tb, 4 * dc + 4 * da), lambda i: (i, 0)), pl.BlockSpec((tb, dc), lambda i: (i, 0)),
                   pair_spec, pair_spec,
                   pl.BlockSpec((1, dc), lambda i: (0, 0)), pl.BlockSpec((hp, 1, PAIR), lambda i: (0, 0, 0))),
        compiler_params=_params(("arbitrary",)),
    )(dycat, co, proj, proj, proj, o_mix, g_conv, g_attn_pairs)


def _conv_bwd(dconv_proj, dco, conv_proj, conv_w, dc, after):
    t = dco.shape[0]
    ct = CONV_TILE
    nct = dc // ct

    def body(dcp_in_ref, dco_ref, u_ref, cg_ref, w_ref, after_ref, dcp_ref, acc_ref):
        del dcp_in_ref, after_ref
        which = pl.program_id(1)
        g = dco_ref[...]
        u = u_ref[...].astype(F32)
        cg = cg_ref[...].astype(F32)
        g_prev, g_next = _shift_rows(g, t)
        da = w_ref[0:1, :] * g_next + w_ref[1:2, :] * g + w_ref[2:3, :] * g_prev
        dcp_ref[...] = (da * jnp.where(which == 0, cg, u)).astype(BF16)
        a = cg * u
        a_prev, a_next = _shift_rows(a, t)
        acc_ref[...] = jnp.concatenate(
            [jnp.sum(g * a_prev, axis=0, keepdims=True), jnp.sum(g * a, axis=0, keepdims=True),
             jnp.sum(g * a_next, axis=0, keepdims=True), jnp.sum(g, axis=0, keepdims=True),
             jnp.zeros((4, ct), F32)], axis=0)

    return pl.pallas_call(
        body, name="conv_bwd", grid=(nct, 2),
        out_shape=(jax.ShapeDtypeStruct(dconv_proj.shape, BF16), jax.ShapeDtypeStruct((8, dc), F32)),
        in_specs=[HBM,
                  pl.BlockSpec((t, ct), lambda i, s: (0, i)),
                  pl.BlockSpec((t, ct), lambda i, s: (0, i)),
                  pl.BlockSpec((t, ct), lambda i, s: (0, 2 * nct + i)),
                  pl.BlockSpec((3, ct), lambda i, s: (0, i)), ANY],
        out_specs=(pl.BlockSpec((t, ct), lambda i, s: (0, 2 * s * nct + i)),
                   pl.BlockSpec((8, ct), lambda i, s: (0, i))),
        input_output_aliases={0: 0},
        compiler_params=_params(("arbitrary", "arbitrary")),
    )(dconv_proj, dco, conv_proj, conv_proj, conv_w, after)


def _dh(dproj, winf, after):
    t = dproj.shape[0]
    _, d, ws = winf.shape
    tm = tn = COL_TILE
    nt = (((1,), (1,)), ((), ()))

    def body(a_ref, w_ref, after_ref, o_ref):
        del after_ref
        acc = lax.dot_general(a_ref[:, 0:ws], w_ref[0], nt, preferred_element_type=F32)
        for j in range(1, N_CHIPS):
            acc = acc + lax.dot_general(a_ref[:, j * ws:(j + 1) * ws], w_ref[j], nt, preferred_element_type=F32)
        o_ref[...] = acc.astype(BF16)

    return pl.pallas_call(
        body, name="dh", grid=(d // tn, t // tm),
        out_shape=jax.ShapeDtypeStruct((t, d), BF16),
        in_specs=[pl.BlockSpec((tm, N_CHIPS * ws), lambda n, m: (m, 0)),
                  pl.BlockSpec((N_CHIPS, tn, ws), lambda n, m: (0, n, 0)), ANY],
        out_specs=pl.BlockSpec((tm, tn), lambda n, m: (m, n)),
        compiler_params=_params(("parallel", "parallel")),
    )(dproj, winf, after)


def _prenorm_bwd(x, dh, dout, mod, g_pre):
    t, d = x.shape
    tb = ROW_TILE

    def body(x_ref, dh_ref, dout_ref, mod_ref, g_ref, gx_ref, acc_ref):
        xv = x_ref[...]
        dhv = dh_ref[...].astype(F32)
        r = lax.rsqrt(jnp.mean(xv * xv, axis=-1, keepdims=True) + EPS)
        xh = xv * r
        one_scale = 1.0 + mod_ref[1:2, :]
        a = dhv * one_scale * g_ref[...]
        gx_ref[...] = dout_ref[...].astype(F32) + r * (a - xh * jnp.mean(a * xh, axis=-1, keepdims=True))
        part = jnp.concatenate(
            [jnp.sum(dhv, axis=0, keepdims=True), jnp.sum(dhv * xh * g_ref[...], axis=0, keepdims=True),
             jnp.sum(dhv * xh * one_scale, axis=0, keepdims=True), jnp.zeros((5, d), F32)], axis=0)

        @pl.when(pl.program_id(0) == 0)
        def _():
            acc_ref[...] = jnp.zeros(acc_ref.shape, F32)

        acc_ref[...] += part

    return pl.pallas_call(
        body, name="prenorm_bwd", grid=(t // tb,),
        out_shape=(jax.ShapeDtypeStruct((t, d), F32), jax.ShapeDtypeStruct((8, d), F32)),
        in_specs=[pl.BlockSpec((tb, d), lambda i: (i, 0)), pl.BlockSpec((tb, d), lambda i: (i, 0)),
                  pl.BlockSpec((tb, d), lambda i: (i, 0)), pl.BlockSpec((3, d), lambda i: (0, 0)),
                  pl.BlockSpec((1, d), lambda i: (0, 0))],
        out_specs=(pl.BlockSpec((tb, d), lambda i: (i, 0)), pl.BlockSpec((8, d), lambda i: (0, 0))),
        compiler_params=_params(("arbitrary",)),
    )(x, dh, dout, mod, g_pre)


def _chip_sums(mine, rsib, name, part=0, parts=1, after=()):
    _, half, cols = mine.shape
    rows = half // parts
    tr = min(rows, ROW_TILE)
    nt = rows // tr

    def body(g_ref, r_ref, *rest):
        rest[-1][...] = (g_ref[...].astype(F32) + r_ref[...].astype(F32)).astype(BF16)

    spec = pl.BlockSpec((None, tr, cols), lambda j, i: (j, part * nt + i, 0))
    return pl.pallas_call(
        body, name=name, grid=(N_CHIPS, nt),
        out_shape=jax.ShapeDtypeStruct((N_CHIPS, rows, cols), BF16),
        in_specs=[spec, spec] + [ANY] * len(after), out_specs=pl.BlockSpec((None, tr, cols), lambda j, i: (j, i, 0)),
        compiler_params=_params(("parallel", "parallel")),
    )(mine, rsib, *after)


def _owner_sum(place, mine, rsib, rici, name, part=0, parts=1):
    _, half, cols = mine.shape
    rows = half // parts
    tr = min(rows, ROW_TILE)
    nt = rows // tr

    def body(place_ref, g_ref, r_ref, i_ref, o_ref):
        del place_ref
        acc = g_ref[...].astype(F32) + r_ref[...].astype(F32)
        for k in range(N_CHIPS - 1):
            acc = acc + i_ref[k].astype(F32)
        o_ref[...] = acc

    own = pl.BlockSpec((None, tr, cols), lambda i, p: (p[0], part * nt + i, 0))
    grid_spec = pltpu.PrefetchScalarGridSpec(
        num_scalar_prefetch=1, grid=(nt,),
        in_specs=[own, own, pl.BlockSpec((N_CHIPS - 1, tr, cols), lambda i, p: (0, i, 0))],
        out_specs=pl.BlockSpec((tr, cols), lambda i, p: (p[1] * (half // tr) + part * nt + i, 0)))
    return pl.pallas_call(
        body, name=name, grid_spec=grid_spec,
        out_shape=jax.ShapeDtypeStruct((2 * half, cols), F32),
        compiler_params=_params(("parallel",)),
    )(place, mine, rsib, rici)


def _adam_math(w, g, m, v):
    m2 = ADAM_B1 * m + (1.0 - ADAM_B1) * g
    v2 = ADAM_B2 * v + (1.0 - ADAM_B2) * (g * g)
    m_hat = m2 / (1.0 - ADAM_B1 ** ADAM_STEP)
    v_hat = v2 / (1.0 - ADAM_B2 ** ADAM_STEP)
    delta = -ADAM_LR * (m_hat / (jnp.sqrt(v_hat) + ADAM_EPS) + ADAM_WD * w)
    return delta, m2, v2


def _adamw(w, g, m, v, name, part=0, parts=1, prev=None):
    rows, cols = w.shape
    tr = min(rows, ROW_TILE)

    def body(*refs):
        w_ref, g_ref, m_ref, v_ref, go_ref, d_ref, m2_ref, v2_ref = refs[-8:]
        g = g_ref[...]
        go_ref[...] = g
        d_ref[...], m2_ref[...], v2_ref[...] = _adam_math(w_ref[...], g, m_ref[...], v_ref[...])

    if parts == 1:
        grid, spec = (rows // tr,), pl.BlockSpec((tr, cols), lambda i: (i, 0))
    else:
        per_half = rows // 2 // tr
        nt = per_half // parts
        grid, spec = (2, nt), pl.BlockSpec((tr, cols), lambda r, i: (r * per_half + part * nt + i, 0))
    olds = [] if prev is None else list(prev)
    return pl.pallas_call(
        body, name=name, grid=grid,
        out_shape=(jax.ShapeDtypeStruct(w.shape, F32),) * 4,
        in_specs=[HBM] * len(olds) + [spec] * 4, out_specs=(spec,) * 4,
        input_output_aliases={i: i for i in range(len(olds))},
        compiler_params=_params(("parallel",) * len(grid)),
    )(*olds, w, g, m, v)


def _ada_grad_adamw(c_all_t, dmod_cols, w, m, v):
    d, wa = w.shape
    tr = ROW_TILE

    def body(ct_ref, dm_ref, w_ref, m_ref, v_ref, g_ref, d_ref, m2_ref, v2_ref):
        act = _silu(ct_ref[...])
        g = act[:, 0:1] * dm_ref[0:1, :]
        for b in range(1, N_DEV):
            g = g + act[:, b:b + 1] * dm_ref[b:b + 1, :]
        g_ref[...] = g
        d_ref[...], m2_ref[...], v2_ref[...] = _adam_math(w_ref[...], g, m_ref[...], v_ref[...])

    spec = pl.BlockSpec((tr, wa), lambda i: (i, 0))
    return pl.pallas_call(
        body, name="ada_grad_adamw", grid=(d // tr,),
        out_shape=(jax.ShapeDtypeStruct(w.shape, F32),) * 4,
        in_specs=[pl.BlockSpec((tr, N_DEV), lambda i: (i, 0)), pl.BlockSpec((N_DEV, wa), lambda i: (0, 0)),
                  spec, spec, spec],
        out_specs=(spec,) * 4,
        compiler_params=_params(("parallel",)),
    )(c_all_t, dmod_cols, w, m, v)


def _small_update(place, gathered, pieces, weights, moments_m, moments_v):
    n = gathered.shape[1]
    k = len(weights)

    def body(place_ref, g_ref, *refs):
        w_refs, m_refs, v_refs = refs[0:k], refs[k:2 * k], refs[2 * k:3 * k]
        outs = refs[3 * k:]
        total = g_ref[0:SUBLANES, :]
        for dev in range(1, N_DEV):
            total = total + g_ref[SUBLANES * dev:SUBLANES * (dev + 1), :]

        def flat(offset, length):
            segments, pos = [], offset
            while pos < offset + length:
                row, col = divmod(pos, n)
                take = min(offset + length - pos, n - col)
                segments.append(total[row:row + 1, col:col + take])
                pos += take
            return jnp.concatenate(segments, axis=1) if len(segments) > 1 else segments[0]

        chip = place_ref[0]
        for i, (w_ref, m_ref, v_ref) in enumerate(zip(w_refs, m_refs, v_refs)):
            g = flat(*pieces[i])
            if w_ref.ndim == 3:
                rows, cols = w_ref.shape[1:]
                full = pieces[i][1] // rows
                picked = []
                for r in range(rows):
                    blocks = [g[:, r * full + q * cols:r * full + (q + 1) * cols] for q in range(N_CHIPS)]
                    mine = blocks[N_CHIPS - 1]
                    for q in range(N_CHIPS - 2, -1, -1):
                        mine = jnp.where(chip == q, blocks[q], mine)
                    picked.append(mine)
                g = jnp.concatenate(picked, axis=0)
                w, m, v = w_ref[0], m_ref[0], v_ref[0]
            else:
                w, m, v = w_ref[...], m_ref[...], v_ref[...]
            delta, m2, v2 = _adam_math(w, g, m, v)
            for j, val in enumerate((g, delta, m2, v2)):
                out = outs[j * k + i]
                if w_ref.ndim == 3:
                    out[0] = val
                else:
                    out[...] = val
        outs[4 * k][...] = flat(*pieces[k])

    shapes = [jax.ShapeDtypeStruct(w.shape, F32) for w in weights]
    grid_spec = pltpu.PrefetchScalarGridSpec(
        num_scalar_prefetch=1, grid=(1,),
        in_specs=[pl.BlockSpec(gathered.shape, lambda i, p: (0, 0))]
        + [pl.BlockSpec(a.shape, functools.partial(lambda nd, i, p: (0,) * nd, a.ndim))
           for a in (*weights, *moments_m, *moments_v)],
        out_specs=tuple(pl.BlockSpec(s.shape, functools.partial(lambda nd, i, p: (0,) * nd, len(s.shape)))
                        for s in shapes * 4) + (pl.BlockSpec((1, LANES), lambda i, p: (0, 0)),))
    outs = pl.pallas_call(
        body, name="small_update", grid_spec=grid_spec,
        out_shape=tuple(shapes * 4) + (jax.ShapeDtypeStruct((1, LANES), F32),),
        compiler_params=_params(("arbitrary",)),
    )(place, gathered, *weights, *moments_m, *moments_v)
    return outs[0:k], outs[k:2 * k], outs[2 * k:3 * k], outs[3 * k:4 * k], outs[4 * k]


def _pack_small(pieces):
    flat = [p.reshape(-1).astype(F32) for p in pieces]
    offsets, total = [], 0
    for p in flat:
        offsets.append(total)
        total += p.shape[0]
    padded = -(-total // SMALL_ALIGN) * SMALL_ALIGN
    if padded > total:
        flat.append(jnp.zeros((padded - total,), F32))
    return jnp.concatenate(flat).reshape(8, padded // 8), offsets


def _alibi_slope_rows(n_heads):
    slopes = 2.0 ** (-8.0 * jnp.arange(1, n_heads + 1, dtype=F32) / n_heads)
    rows = jnp.zeros((n_heads // 2, 8), F32).at[:, 0:2].set(slopes.reshape(n_heads // 2, 2))
    return jnp.broadcast_to(rows[:, :, None], (n_heads // 2, 8, ATT_KW))


def kernel(x, c, w_ada, b_ada, g_pre, w_in, conv_w, conv_b, g_conv, g_attn, w_out, g_post, loss_target, m_w_ada, m_b_ada, m_g_pre, m_w_in, m_conv_w, m_conv_b, m_g_conv, m_g_attn, m_w_out, m_g_post, v_w_ada, v_b_ada, v_g_pre, v_w_in, v_conv_w, v_conv_b, v_g_conv, v_g_attn, v_w_out, v_g_post):
    t, d = x.shape[1], x.shape[2]
    dc = conv_b.shape[1]
    da = g_attn.shape[1]
    hp = da // PAIR
    ws = w_in.shape[2]
    wa = w_ada.shape[2]
    cws = conv_w.shape[2]
    assert t % ROW_TILE == 0 and d % ROW_TILE == 0 and dc % COL_TILE == 0 and da % COL_TILE == 0
    assert ws == 2 * dc and dc == da and t // BRANCHES[-1][1] >= ATT_BQ

    mx, my, mc = _my_place()
    chip = _chip_of(mx, my)
    dev = 2 * chip + mc
    place = jnp.stack([chip, mc]).astype(jnp.int32)

    x2, tgt2 = x[0], loss_target[0]
    w_ada2, w_in2, w_out2 = w_ada[0], w_in[0], w_out[0]

    win_slots = _cast_into_slot(place, w_in2, "cast_w_in")
    packed, offs = _pack_small([c[0], conv_w[0]])
    seen, mod = _ada_modulation(packed, w_ada2, b_ada, d, after=(win_slots,))
    seen = seen.reshape(N_DEV, -1)
    c_all = seen[:, offs[0]:offs[0] + d]
    conv_w_full = seen[0::2, offs[1]:offs[1] + 3 * cws].reshape(N_CHIPS, 3, cws).transpose(1, 0, 2).reshape(3, dc)

    win_flight, send_in, recv_in, started = _gather_start(win_slots, mod)

    y_chip, x_chip, d_chip = (_chip_of(mx, 1 - my), _chip_of(1 - mx, my), _chip_of(1 - mx, 1 - my))
    own_chunk, near_chunks, far_chunk = (jnp.stack(js).astype(jnp.int32) for js in ([chip], [y_chip, x_chip], [d_chip]))
    h, ht = _prenorm(x2, mod, g_pre, started)
    proj = _proj_chunks(None, h, w_in2, own_chunk, "proj_own")
    win_flight, wout_flight, relay_send_in, relay_recv_in, send_out, recv_out = _gather_relay_in(
        win_flight, _cast_into_slot(place, w_out2, "cast_w_out"), recv_in, proj)
    win_flight = _forward_halves(
        _gather_wait_direct(win_flight, send_in, recv_in, proj, "gather_wait_w_in_direct"), (0, 1), "forward_w_in_direct")
    proj = _proj_chunks(proj, h, win_flight, near_chunks, "proj_neighbours")
    winf = _forward_halves(
        _gather_wait_relayed(win_flight, relay_send_in, relay_recv_in, proj, "gather_wait_w_in_relayed"),
        (2,), "forward_w_in_relayed")
    proj = _proj_chunks(proj, h, winf, far_chunk, "proj_diagonal")
    slopes = _alibi_slope_rows(da // HEAD_DIM)
    co = _conv_fwd(proj, conv_w_full, conv_b, dc)
    wout_flight, relay_send_out, relay_recv_out = _gather_relay_out(wout_flight, recv_out, co)
    o_mix, lse = _attn_fwd(proj, slopes, dc, da)
    g_attn_pairs = g_attn.reshape(hp, 1, PAIR)
    ycat, ycat_t = _mix_fwd(co, proj, o_mix, g_conv, g_attn_pairs)
    wout_flight = _gather_wait_direct(wout_flight, send_out, recv_out, ycat, "gather_wait_w_out_direct")
    wout_flight = _gather_wait_relayed(wout_flight, relay_send_out, relay_recv_out, ycat, "gather_wait_w_out_relayed")
    woutf = _forward_halves(wout_flight, (0, 1, 2), "forward_w_out").reshape(dc + da, d)
    dout, dy, post_sums = _out_fwd_bwd(ycat, woutf, x2, tgt2, mod, g_post)

    gout, rsib_out = _dw_swapped(ycat_t, dy, N_CHIPS, 1, "dw_out")
    csum_out = _chip_sums(gout, rsib_out, "rs_chip_sum_out")
    ssem_out, rsem_out, csum_out, land_out, sent_out = _owners_start(csum_out, "rs_owners_start_out")
    dycat = _matmul_nt(dy, woutf, BF16, "dycat")
    dproj, dco, d_o, delta, dg_conv, dg_attn = _mix_bwd(dycat, co, proj, o_mix, g_conv, g_attn_pairs)
    dproj, conv_sums = _conv_bwd(dproj, dco, proj, conv_w_full, dc, sent_out)
    dproj = _attn_bwd(dproj, proj, d_o, lse, delta, slopes, dc, da, sent_out)
    gin, rsib_in = _dw_swapped(ht, dproj, 1, N_CHIPS, "dw_in")
    ssem_in0, rsem_in0, csum_in0, land_in0, sent_in = _owners_start(
        _chip_sums(gin, rsib_in, "rs_chip_sum_in0", 0, 2), "rs_owners_start_in0")
    dh = _dh(dproj, winf, sent_in)
    grad_x, pre_sums = _prenorm_bwd(x2, dh, dout, mod, g_pre)

    small, so = _pack_small([
        pre_sums[0], pre_sums[1], post_sums[0],
        pre_sums[2], conv_sums[0:3], conv_sums[3], dg_conv, dg_attn, post_sums[1], post_sums[2, 0:128]])
    ssem_small, rsem_small, small, land_small, sent_small = _allgather8_start(small, dev, "gather_small_start")
    ssem_in1, rsem_in1, csum_in1, land_in1, sent_in1 = _owners_start(
        _chip_sums(gin, rsib_in, "rs_chip_sum_in1", 1, 2, after=(sent_small,)), "rs_owners_start_in1")

    rici_out = _owners_wait(ssem_out, rsem_out, csum_out, land_out, [grad_x, sent_in1], "rs_owners_wait_out")
    grad_w_out = _join_halves(_owner_sum(place, gout, rsib_out, rici_out, "rs_owner_sum_out"), "rs_join_halves_out")
    grad_w_out, delta_w_out, new_m_w_out, new_v_w_out = _adamw(
        w_out2, grad_w_out, m_w_out[0], v_w_out[0], "adamw_w_out")

    small_seen = _allgather8_wait(ssem_small, rsem_small, small, land_small, [delta_w_out], "gather_small_wait")
    small_w = [b_ada, g_pre, conv_w, conv_b, g_conv, g_attn, g_post]
    small_m = [m_b_ada, m_g_pre, m_conv_w, m_conv_b, m_g_conv, m_g_attn, m_g_post]
    small_v = [v_b_ada, v_g_pre, v_conv_w, v_conv_b, v_g_conv, v_g_attn, v_g_post]
    pieces = [(0, 3 * d), (so[3], d), (so[4], 3 * dc), (so[5], dc), (so[6], dc), (so[7], da), (so[8], d), (so[9], LANES)]
    g_small, d_small, m_small, v_small, loss_row = _small_update(place, small_seen, pieces, small_w, small_m, small_v)
    loss = loss_row[0, 0]
    grad_b_ada, grad_g_pre, grad_conv_w, grad_conv_b, grad_g_conv, grad_g_attn, grad_g_post = g_small

    dmod_cols = lax.dynamic_slice_in_dim(small_seen.reshape(N_DEV, -1), chip * wa, wa, axis=1)
    grad_w_ada, delta_w_ada, new_m_w_ada, new_v_w_ada = _ada_grad_adamw(c_all.T, dmod_cols, w_ada2, m_w_ada[0], v_w_ada[0])

    rici_in = _owners_wait(ssem_in0, rsem_in0, csum_in0, land_in0, [d_small[0], delta_w_out, delta_w_ada], "rs_owners_wait_in0")
    full_in0, jsend0, jrecv0, joining0 = _join_start(
        _owner_sum(place, gin, rsib_in, rici_in, "rs_owner_sum_in0", 0, 2), "rs_join_start_in0", 0, 2)
    rici_in = _owners_wait(ssem_in1, rsem_in1, csum_in1, land_in1, [joining0], "rs_owners_wait_in1")
    full_in1, jsend1, jrecv1, joining1 = _join_start(
        _owner_sum(place, gin, rsib_in, rici_in, "rs_owner_sum_in1", 1, 2), "rs_join_start_in1", 1, 2)
    full_in0 = _join_wait(full_in0, jsend0, jrecv0, [joining1], "rs_join_wait_in0", 0, 2)
    updated_in = _adamw(w_in2, full_in0, m_w_in[0], v_w_in[0], "adamw_w_in0", 0, 2)
    full_in1 = _join_wait(full_in1, jsend1, jrecv1, [updated_in[1]], "rs_join_wait_in1", 1, 2)
    grad_w_in, delta_w_in, new_m_w_in, new_v_w_in = _adamw(
        w_in2, full_in1, m_w_in[0], v_w_in[0], "adamw_w_in1", 1, 2, updated_in)

    def lead(a):
        return a.reshape((1,) + a.shape)

    grads = [lead(grad_w_ada), grad_b_ada, grad_g_pre, lead(grad_w_in), grad_conv_w, grad_conv_b, grad_g_conv,
             grad_g_attn, lead(grad_w_out), grad_g_post]
    deltas = [lead(delta_w_ada), d_small[0], d_small[1], lead(delta_w_in), d_small[2], d_small[3], d_small[4],
              d_small[5], lead(delta_w_out), d_small[6]]
    new_ms = [lead(new_m_w_ada), m_small[0], m_small[1], lead(new_m_w_in), m_small[2], m_small[3], m_small[4],
              m_small[5], lead(new_m_w_out), m_small[6]]
    new_vs = [lead(new_v_w_ada), v_small[0], v_small[1], lead(new_v_w_in), v_small[2], v_small[3], v_small[4],
              v_small[5], lead(new_v_w_out), v_small[6]]
    return (loss, lead(grad_x), *grads, *deltas, *new_ms, *new_vs)
```

```python
import functools

import jax
import jax.numpy as jnp
from jax import lax
from jax.experimental import pallas as pl
from jax.experimental.pallas import tpu as pltpu

F32 = jnp.float32
BF16 = jnp.bfloat16
MESH = pl.DeviceIdType.MESH
HBM = pl.BlockSpec(memory_space=pltpu.HBM)
VMEM = pl.BlockSpec(memory_space=pltpu.VMEM)
ANY = pl.BlockSpec(memory_space=pl.ANY)
SEM = pl.BlockSpec(memory_space=pltpu.SEMAPHORE)
EFFECT = pltpu.SideEffectType.DATAFLOW_SIDE_EFFECTING
SUBLANES, LANES = 8, 128
TOKEN = jax.ShapeDtypeStruct((SUBLANES, LANES), jnp.float32)

HEAD_DIM = 64
PAIR = 2 * HEAD_DIM
assert PAIR == LANES
BRANCHES = ((128, 1), (512, 4), (2048, 16))
SIDE = 64
EPS = 1e-6
NEG_INF = -1e30
N_CHIPS = 4
N_DEV = 8

ADAM_LR = 0.001
ADAM_B1 = 0.9
ADAM_B2 = 0.999
ADAM_EPS = 1e-08
ADAM_WD = 0.01
ADAM_STEP = 10

VMEM_LIMIT_BYTES = 56 * 1024 * 1024
ROW_TILE = 256
COL_TILE = 512
CONV_TILE = 256
ATT_BQ = 128
ATT_KW = ATT_BQ + 2 * SIDE
ATT_UNROLL = 4
SMALL_ALIGN = SUBLANES * LANES


def _params(semantics=None):
    kw = {"vmem_limit_bytes": VMEM_LIMIT_BYTES}
    if semantics is not None:
        kw["dimension_semantics"] = semantics
    return pltpu.CompilerParams(**kw)


def _silu(z):
    return z * jax.nn.sigmoid(z)


def _silu_grad(z):
    s = jax.nn.sigmoid(z)
    return s * (1.0 + z * (1.0 - s))


def _my_place():
    return lax.axis_index("x"), lax.axis_index("y"), lax.axis_index("c")


def _flip(a, bit):
    return 1 - a if bit else a


def _chip_of(x, y):
    return 2 * x + y


def _allgather8_start(v, me, name):
    rows_per, n = v.shape
    land = lax.dynamic_update_slice(jnp.zeros((N_DEV * rows_per, n), v.dtype), v, (me * rows_per, 0))

    def body(v_ref, land_ref, send_sems, recv_sems, v_thru, land_thru, token_ref):
        del v_thru, land_thru
        x, y, c = _my_place()
        mine = land_ref.at[pl.ds(pl.multiple_of((4 * x + 2 * y + c) * rows_per, rows_per), rows_per), :]
        for k in range(1, N_DEV):
            peer = (_flip(x, k & 4), _flip(y, k & 2), _flip(c, k & 1))
            pltpu.make_async_remote_copy(
                src_ref=v_ref, dst_ref=mine, send_sem=send_sems.at[k - 1], recv_sem=recv_sems.at[k - 1],
                device_id=peer, device_id_type=MESH).start()
        token_ref[...] = jnp.zeros(token_ref.shape, F32)

    sems = pltpu.SemaphoreType.DMA((N_DEV - 1,))
    return pl.pallas_call(
        body, name=name,
        out_shape=(sems, sems, jax.ShapeDtypeStruct(v.shape, v.dtype), jax.ShapeDtypeStruct(land.shape, land.dtype), TOKEN),
        in_specs=[HBM, HBM], out_specs=(SEM, SEM, HBM, HBM, VMEM),
        input_output_aliases={0: 2, 1: 3},
        compiler_params=pltpu.CompilerParams(has_side_effects=EFFECT),
    )(pltpu.with_memory_space_constraint(v, pltpu.HBM), pltpu.with_memory_space_constraint(land, pltpu.HBM))


def _allgather8_wait(send_sems, recv_sems, v, land, after, name):
    rows_per = v.shape[0]

    def body(v_ref, land_ref, send_ref, recv_ref, *rest):
        del rest
        x, y, c = _my_place()
        for k in range(1, N_DEV):
            peer = (_flip(x, k & 4), _flip(y, k & 2), _flip(c, k & 1))
            src = 4 * peer[0] + 2 * peer[1] + peer[2]
            cp = pltpu.make_async_remote_copy(
                src_ref=v_ref, dst_ref=land_ref.at[pl.ds(pl.multiple_of(src * rows_per, rows_per), rows_per), :],
                send_sem=send_ref.at[k - 1], recv_sem=recv_ref.at[k - 1], device_id=peer, device_id_type=MESH)
            cp.wait_send()
            cp.wait_recv()

    return pl.pallas_call(
        body, name=name,
        out_shape=(jax.ShapeDtypeStruct(v.shape, v.dtype), jax.ShapeDtypeStruct(land.shape, land.dtype)),
        in_specs=[HBM, HBM, SEM, SEM] + [ANY] * len(after), out_specs=(HBM, HBM),
        input_output_aliases={0: 0, 1: 1},
        compiler_params=pltpu.CompilerParams(has_side_effects=EFFECT),
    )(v, land, send_sems, recv_sems, *after)[1]


def _half_rows(ref, chip, which, half):
    return ref.at[chip, pl.ds(pl.multiple_of(which * half, half), half), :]


def _ici_peers(x, y, c):
    peers = [(_flip(x, k & 2), _flip(y, k & 1), c) for k in (1, 2, 3)]
    return [(peer, _chip_of(peer[0], peer[1])) for peer in peers]


def _part_rows(ref, chip, core, part):
    quarter = ref.shape[1] // 4
    return ref.at[chip, pl.ds(pl.multiple_of((2 * core + part) * quarter, quarter), quarter), :]


def _neighbours(x, y, c):
    return [((x, 1 - y, c), _chip_of(x, 1 - y)), ((1 - x, y, c), _chip_of(1 - x, y)),
            ((1 - x, 1 - y, c), _chip_of(1 - x, 1 - y))]


def _start_direct(buf, send_sems, recv_sems):
    x, y, c = _my_place()
    me = _chip_of(x, y)
    for n, (peer, _) in enumerate(_neighbours(x, y, c)[0:2]):
        for part in ((0, 1), (1, 0))[n]:
            piece = _part_rows(buf, me, c, part)
            pltpu.make_async_remote_copy(
                src_ref=piece, dst_ref=piece, send_sem=send_sems.at[2 * n + part], recv_sem=recv_sems.at[2 * n + part],
                device_id=peer, device_id_type=MESH).start()


def _relay(buf, recv_sems, relay_send, relay_recv):
    x, y, c = _my_place()
    nbrs = _neighbours(x, y, c)
    for n in range(2):
        part = n
        piece = _part_rows(buf, nbrs[n][1], c, part)
        pltpu.make_async_remote_copy(
            src_ref=piece, dst_ref=piece, send_sem=relay_send.at[part], recv_sem=recv_sems.at[2 * n + part],
            device_id=nbrs[n][0], device_id_type=MESH).wait_recv()
        pltpu.make_async_remote_copy(
            src_ref=piece, dst_ref=piece, send_sem=relay_send.at[part], recv_sem=relay_recv.at[part],
            device_id=nbrs[1 - n][0], device_id_type=MESH).start()


def _gather_start(win_slots, after):
    def body(win_in, after_ref, win_ref, send_sems, recv_sems, token_ref):
        del win_in, after_ref
        _start_direct(win_ref, send_sems, recv_sems)
        token_ref[...] = jnp.zeros(token_ref.shape, F32)

    sems = pltpu.SemaphoreType.DMA((4,))
    return pl.pallas_call(
        body, name="gather_start",
        out_shape=(jax.ShapeDtypeStruct(win_slots.shape, win_slots.dtype), sems, sems, TOKEN),
        in_specs=[HBM, ANY], out_specs=(HBM, SEM, SEM, VMEM),
        input_output_aliases={0: 0},
        compiler_params=pltpu.CompilerParams(has_side_effects=EFFECT),
    )(win_slots, after)


def _gather_relay_in(win, wout_slots, recv_in, after):
    def body(win_in, wout_in, recv_in_ref, after_ref, win_ref, wout_ref, relay_send, relay_recv, send_out, recv_out):
        del win_in, wout_in, after_ref
        _relay(win_ref, recv_in_ref, relay_send, relay_recv)
        _start_direct(wout_ref, send_out, recv_out)

    two, four = pltpu.SemaphoreType.DMA((2,)), pltpu.SemaphoreType.DMA((4,))
    return pl.pallas_call(
        body, name="gather_relay_w_in",
        out_shape=(jax.ShapeDtypeStruct(win.shape, win.dtype), jax.ShapeDtypeStruct(wout_slots.shape, wout_slots.dtype),
                   two, two, four, four),
        in_specs=[HBM, HBM, SEM, ANY], out_specs=(HBM, HBM, SEM, SEM, SEM, SEM),
        input_output_aliases={0: 0, 1: 1},
        compiler_params=pltpu.CompilerParams(has_side_effects=EFFECT),
    )(win, wout_slots, recv_in, after)


def _gather_relay_out(wout, recv_out, after):
    def body(wout_in, recv_out_ref, after_ref, wout_ref, relay_send, relay_recv):
        del wout_in, after_ref
        _relay(wout_ref, recv_out_ref, relay_send, relay_recv)

    two = pltpu.SemaphoreType.DMA((2,))
    return pl.pallas_call(
        body, name="gather_relay_w_out",
        out_shape=(jax.ShapeDtypeStruct(wout.shape, wout.dtype), two, two),
        in_specs=[HBM, SEM, ANY], out_specs=(HBM, SEM, SEM),
        input_output_aliases={0: 0},
        compiler_params=pltpu.CompilerParams(has_side_effects=EFFECT),
    )(wout, recv_out, after)


def _gather_wait_direct(buf, send_sems, recv_sems, after, name):
    def body(buf_in, send_ref, recv_ref, after_ref, buf_ref):
        del buf_in, after_ref
        x, y, c = _my_place()
        me = _chip_of(x, y)
        for n, (peer, chip) in enumerate(_neighbours(x, y, c)[0:2]):
            second = 1 - n
            pltpu.make_async_remote_copy(
                src_ref=_part_rows(buf_ref, me, c, second), dst_ref=_part_rows(buf_ref, chip, c, second),
                send_sem=send_ref.at[2 * n + second], recv_sem=recv_ref.at[2 * n + second],
                device_id=peer, device_id_type=MESH).wait_recv()
            for part in range(2):
                piece = _part_rows(buf_ref, me, c, part)
                pltpu.make_async_remote_copy(
                    src_ref=piece, dst_ref=piece, send_sem=send_ref.at[2 * n + part], recv_sem=recv_ref.at[2 * n + part],
                    device_id=peer, device_id_type=MESH).wait_send()

    return pl.pallas_call(
        body, name=name,
        out_shape=jax.ShapeDtypeStruct(buf.shape, buf.dtype),
        in_specs=[HBM, SEM, SEM, ANY], out_specs=HBM,
        input_output_aliases={0: 0},
        compiler_params=pltpu.CompilerParams(has_side_effects=EFFECT),
    )(buf, send_sems, recv_sems, after)


def _gather_wait_relayed(buf, relay_send, relay_recv, after, name):
    def body(buf_in, rsend_ref, rrecv_ref, after_ref, buf_ref):
        del buf_in, after_ref
        x, y, c = _my_place()
        nbrs = _neighbours(x, y, c)
        for n in range(2):
            relayed = _part_rows(buf_ref, nbrs[n][1], c, n)
            cp = pltpu.make_async_remote_copy(
                src_ref=relayed, dst_ref=_part_rows(buf_ref, nbrs[2][1], c, n),
                send_sem=rsend_ref.at[n], recv_sem=rrecv_ref.at[n], device_id=nbrs[1 - n][0], device_id_type=MESH)
            cp.wait_recv()
            cp.wait_send()

    return pl.pallas_call(
        body, name=name,
        out_shape=jax.ShapeDtypeStruct(buf.shape, buf.dtype),
        in_specs=[HBM, SEM, SEM, ANY], out_specs=HBM,
        input_output_aliases={0: 0},
        compiler_params=pltpu.CompilerParams(has_side_effects=EFFECT),
    )(buf, relay_send, relay_recv, after)


def _forward_halves(buf, which, name):
    half = buf.shape[1] // 2

    def body(buf_in, buf_ref, send_sems, recv_sems):
        del buf_in
        x, y, c = _my_place()
        sibling = (x, y, 1 - c)
        chips = [_neighbours(x, y, c)[n][1] for n in which]
        started = []
        for k, src_chip in enumerate(chips):
            landed = _half_rows(buf_ref, src_chip, c, half)
            fw = pltpu.make_async_remote_copy(
                src_ref=landed, dst_ref=landed, send_sem=send_sems.at[k], recv_sem=recv_sems.at[k],
                device_id=sibling, device_id_type=MESH)
            fw.start()
            started.append(fw)
        for k, src_chip in enumerate(chips):
            other = _half_rows(buf_ref, src_chip, 1 - c, half)
            pltpu.make_async_remote_copy(
                src_ref=other, dst_ref=other, send_sem=send_sems.at[k], recv_sem=recv_sems.at[k],
                device_id=sibling, device_id_type=MESH).wait_recv()
        for fw in started:
            fw.wait_send()

    return pl.pallas_call(
        body, name=name,
        out_shape=jax.ShapeDtypeStruct(buf.shape, buf.dtype),
        in_specs=[HBM], out_specs=HBM,
        input_output_aliases={0: 0},
        scratch_shapes=[pltpu.SemaphoreType.DMA((len(which),))] * 2,
    )(buf)


def _dw_swapped(a, b, row_chunks, col_chunks, name):
    r, t = a.shape
    c_all = b.shape[1]
    chunks = row_chunks * col_chunks
    rq, cq = r // row_chunks, c_all // col_chunks
    half = rq // 2
    tn = COL_TILE
    nt = cq // tn
    steps = col_chunks * nt

    def body(a_ref, b_ref, mine_ref, sib_ref, stage, send_sems, recv_sems):
        x, y, c = _my_place()
        j, n = pl.program_id(0), pl.program_id(1)
        step = j * nt + n
        slot = step % 2
        res = jnp.dot(a_ref[...], b_ref[...], preferred_element_type=F32).astype(BF16)

        def landing(jj, nn):
            cols = pl.ds(pl.multiple_of(nn * tn, tn), tn)
            return sib_ref.at[:, :, cols] if col_chunks == 1 else sib_ref.at[pl.ds(jj, 1), :, cols]

        def copy(slot_, step_, jj, nn):
            return pltpu.make_async_remote_copy(
                src_ref=stage.at[slot_], dst_ref=landing(jj, nn), send_sem=send_sems.at[slot_],
                recv_sem=recv_sems.at[step_], device_id=(x, y, 1 - c), device_id_type=MESH)

        @pl.when(step >= 2)
        def _():
            copy(slot, step, j, n).wait_send()

        for q in range(row_chunks):
            lo = res[q * rq:q * rq + half, :]
            hi = res[q * rq + half:(q + 1) * rq, :]
            mine_ref[q] = jnp.where(c == 0, lo, hi)
            stage[slot, q] = jnp.where(c == 0, hi, lo)
        copy(slot, step, j, n).start()

        @pl.when(step == steps - 1)
        def _():
            for s in range(max(steps - 2, 0), steps):
                copy(s % 2, s, j, n).wait_send()
            for s in range(steps):
                copy(s % 2, s, j, n).wait_recv()

    shape = jax.ShapeDtypeStruct((chunks, half, cq), BF16)
    return pl.pallas_call(
        body, name=name, grid=(col_chunks, nt),
        out_shape=(shape, shape),
        in_specs=[pl.BlockSpec((r, t), lambda j, n: (0, 0)), pl.BlockSpec((t, tn), lambda j, n: (0, j * nt + n))],
        out_specs=(pl.BlockSpec((row_chunks, half, tn), lambda j, n: (j, 0, n)), ANY),
        scratch_shapes=[pltpu.VMEM((2, row_chunks, half, tn), BF16), pltpu.SemaphoreType.DMA((2,)),
                        pltpu.SemaphoreType.DMA((steps,))],
        compiler_params=_params(("arbitrary", "arbitrary")),
    )(a, b)


def _owners_start(csum, name, after=()):
    land = pltpu.with_memory_space_constraint(lax.empty((N_CHIPS - 1,) + csum.shape[1:], csum.dtype), pltpu.HBM)

    def body(csum_ref, land_ref, *rest):
        send_sems, recv_sems, _, _, token_ref = rest[len(after):]
        x, y, c = _my_place()
        for k, (peer, owner) in enumerate(_ici_peers(x, y, c)):
            pltpu.make_async_remote_copy(
                src_ref=csum_ref.at[owner], dst_ref=land_ref.at[k], send_sem=send_sems.at[k], recv_sem=recv_sems.at[k],
                device_id=peer, device_id_type=MESH).start()
        token_ref[...] = jnp.zeros(token_ref.shape, F32)

    sems = pltpu.SemaphoreType.DMA((N_CHIPS - 1,))
    return pl.pallas_call(
        body, name=name,
        out_shape=(sems, sems, jax.ShapeDtypeStruct(csum.shape, csum.dtype),
                   jax.ShapeDtypeStruct(land.shape, land.dtype), TOKEN),
        in_specs=[HBM, HBM] + [ANY] * len(after), out_specs=(SEM, SEM, HBM, HBM, VMEM),
        input_output_aliases={0: 2, 1: 3},
        compiler_params=pltpu.CompilerParams(has_side_effects=EFFECT),
    )(pltpu.with_memory_space_constraint(csum, pltpu.HBM), land, *after)


def _owners_wait(send_sems, recv_sems, csum, land, after, name):
    def body(csum_ref, land_ref, send_ref, recv_ref, *rest):
        del rest
        x, y, c = _my_place()
        for k, (peer, owner) in enumerate(_ici_peers(x, y, c)):
            cp = pltpu.make_async_remote_copy(
                src_ref=csum_ref.at[owner], dst_ref=land_ref.at[k], send_sem=send_ref.at[k], recv_sem=recv_ref.at[k],
                device_id=peer, device_id_type=MESH)
            cp.wait_send()
            cp.wait_recv()

    return pl.pallas_call(
        body, name=name,
        out_shape=(jax.ShapeDtypeStruct(csum.shape, csum.dtype), jax.ShapeDtypeStruct(land.shape, land.dtype)),
        in_specs=[HBM, HBM, SEM, SEM] + [ANY] * len(after), out_specs=(HBM, HBM),
        input_output_aliases={0: 0, 1: 1},
        compiler_params=pltpu.CompilerParams(has_side_effects=EFFECT),
    )(csum, land, send_sems, recv_sems, *after)[1]


def _join_halves(full, name):
    rows = full.shape[0] // 2

    def body(full_in, full_ref, send_sem, recv_sem):
        del full_in
        x, y, c = _my_place()
        sibling = (x, y, 1 - c)
        mine = full_ref.at[pl.ds(pl.multiple_of(c * rows, rows), rows), :]
        theirs = full_ref.at[pl.ds(pl.multiple_of((1 - c) * rows, rows), rows), :]
        cp = pltpu.make_async_remote_copy(
            src_ref=mine, dst_ref=mine, send_sem=send_sem, recv_sem=recv_sem, device_id=sibling, device_id_type=MESH)
        cp.start()
        pltpu.make_async_remote_copy(
            src_ref=theirs, dst_ref=theirs, send_sem=send_sem, recv_sem=recv_sem,
            device_id=sibling, device_id_type=MESH).wait_recv()
        cp.wait_send()

    return pl.pallas_call(
        body, name=name,
        out_shape=jax.ShapeDtypeStruct(full.shape, full.dtype),
        in_specs=[HBM], out_specs=HBM,
        input_output_aliases={0: 0},
        scratch_shapes=[pltpu.SemaphoreType.DMA, pltpu.SemaphoreType.DMA],
    )(full)


def _join_start(full, name, part, parts):
    half = full.shape[0] // 2
    rows = half // parts

    def body(full_in, full_ref, send_sem, recv_sem, token_ref):
        del full_in
        x, y, c = _my_place()
        mine = full_ref.at[pl.ds(pl.multiple_of(c * half + part * rows, rows), rows), :]
        pltpu.make_async_remote_copy(
            src_ref=mine, dst_ref=mine, send_sem=send_sem.at[0], recv_sem=recv_sem.at[0],
            device_id=(x, y, 1 - c), device_id_type=MESH).start()
        token_ref[...] = jnp.zeros(token_ref.shape, F32)

    one = pltpu.SemaphoreType.DMA((1,))
    return pl.pallas_call(
        body, name=name,
        out_shape=(jax.ShapeDtypeStruct(full.shape, full.dtype), one, one, TOKEN),
        in_specs=[HBM], out_specs=(HBM, SEM, SEM, VMEM),
        input_output_aliases={0: 0},
        compiler_params=pltpu.CompilerParams(has_side_effects=EFFECT),
    )(full)


def _join_wait(full, send_sem, recv_sem, after, name, part, parts):
    half = full.shape[0] // 2
    rows = half // parts

    def body(full_in, send_ref, recv_ref, *rest):
        del full_in
        full_ref = rest[-1]
        x, y, c = _my_place()
        cp = pltpu.make_async_remote_copy(
            src_ref=full_ref.at[pl.ds(pl.multiple_of(c * half + part * rows, rows), rows), :],
            dst_ref=full_ref.at[pl.ds(pl.multiple_of((1 - c) * half + part * rows, rows), rows), :],
            send_sem=send_ref.at[0], recv_sem=recv_ref.at[0], device_id=(x, y, 1 - c), device_id_type=MESH)
        cp.wait_send()
        cp.wait_recv()

    return pl.pallas_call(
        body, name=name,
        out_shape=jax.ShapeDtypeStruct(full.shape, full.dtype),
        in_specs=[HBM, SEM, SEM] + [ANY] * len(after), out_specs=HBM,
        input_output_aliases={0: 0},
        compiler_params=pltpu.CompilerParams(has_side_effects=EFFECT),
    )(full, send_sem, recv_sem, *after)


def _cast_into_slot(place, w, name):
    rows, cols = w.shape
    tr = min(rows, ROW_TILE)

    def body(place_ref, w_ref, o_ref):
        del place_ref
        o_ref[...] = w_ref[...].astype(BF16)

    grid_spec = pltpu.PrefetchScalarGridSpec(
        num_scalar_prefetch=1, grid=(rows // tr,),
        in_specs=[pl.BlockSpec((tr, cols), lambda i, p: (i, 0))],
        out_specs=pl.BlockSpec((None, tr, cols), lambda i, p: (p[0], i, 0)))
    return pl.pallas_call(
        body, name=name, grid_spec=grid_spec,
        out_shape=jax.ShapeDtypeStruct((N_CHIPS, rows, cols), BF16),
        compiler_params=_params(("parallel",)),
    )(place, w)


def _ada_modulation(packed, w_ada, b_ada, d, after=()):
    rows_per, n = packed.shape
    d_model, wa = w_ada.shape

    def body(v_ref, w_hbm, b_ref, *rest):
        all_ref, mod_ref, w_vmem, part_ref, parts_ref, load_sem, send1, recv1, send2, recv2 = rest[len(after):]
        x, y, c = _my_place()
        me = 4 * x + 2 * y + c
        chip = _chip_of(x, y)
        load = pltpu.make_async_copy(w_hbm, w_vmem, load_sem)
        load.start()

        def rows(idx):
            return all_ref.at[pl.ds(pl.multiple_of(idx * rows_per, rows_per), rows_per), :]

        all_ref[pl.ds(pl.multiple_of(me * rows_per, rows_per), rows_per), :] = v_ref[...]
        copies = []
        for k in range(1, N_DEV):
            peer = (_flip(x, k & 4), _flip(y, k & 2), _flip(c, k & 1))
            cp = pltpu.make_async_remote_copy(
                src_ref=v_ref, dst_ref=rows(me), send_sem=send1.at[k - 1], recv_sem=recv1.at[k - 1],
                device_id=peer, device_id_type=MESH)
            cp.start()
            copies.append((cp, peer))
        for k, (cp, peer) in enumerate(copies):
            pltpu.make_async_remote_copy(
                src_ref=v_ref, dst_ref=rows(4 * peer[0] + 2 * peer[1] + peer[2]), send_sem=send1.at[k],
                recv_sem=recv1.at[k], device_id=peer, device_id_type=MESH).wait_recv()
        for cp, _ in copies:
            cp.wait_send()

        def c_of(dev):
            segments, pos = [], 0
            while pos < d:
                row, col = divmod(pos, n)
                take = min(d - pos, n - col)
                segments.append(all_ref[dev * rows_per + row:dev * rows_per + row + 1, col:col + take])
                pos += take
            return jnp.concatenate(segments, axis=1)

        c_all = jnp.concatenate([c_of(dev) for dev in range(N_DEV)], axis=0)
        load.wait()
        part_ref[...] = jnp.dot(_silu(c_all), w_vmem[...], precision=lax.Precision.HIGHEST, preferred_element_type=F32)
        parts_ref[chip] = part_ref[...]
        swaps = []
        for k, (peer, _) in enumerate(_ici_peers(x, y, c)):
            cp = pltpu.make_async_remote_copy(
                src_ref=part_ref, dst_ref=parts_ref.at[chip], send_sem=send2.at[k], recv_sem=recv2.at[k],
                device_id=peer, device_id_type=MESH)
            cp.start()
            swaps.append(cp)
        for k, (peer, peer_chip) in enumerate(_ici_peers(x, y, c)):
            pltpu.make_async_remote_copy(
                src_ref=part_ref, dst_ref=parts_ref.at[peer_chip], send_sem=send2.at[k], recv_sem=recv2.at[k],
                device_id=peer, device_id_type=MESH).wait_recv()
        for cp in swaps:
            cp.wait_send()
        flat = jnp.concatenate([parts_ref[j, pl.ds(me, 1), :] for j in range(N_CHIPS)], axis=1) + b_ref[...]
        mod_ref[...] = jnp.concatenate([flat[:, i * d:(i + 1) * d] for i in range(3)], axis=0)

    return pl.pallas_call(
        body, name="ada_modulation",
        out_shape=(jax.ShapeDtypeStruct((N_DEV * rows_per, n), F32), jax.ShapeDtypeStruct((3, d), F32)),
        in_specs=[VMEM, ANY, VMEM] + [ANY] * len(after), out_specs=(VMEM, VMEM),
        scratch_shapes=[pltpu.VMEM((d_model, wa), F32), pltpu.VMEM((N_DEV, wa), F32),
                        pltpu.VMEM((N_CHIPS, N_DEV, wa), F32), pltpu.SemaphoreType.DMA,
                        pltpu.SemaphoreType.DMA((N_DEV - 1,)), pltpu.SemaphoreType.DMA((N_DEV - 1,)),
                        pltpu.SemaphoreType.DMA((N_CHIPS - 1,)), pltpu.SemaphoreType.DMA((N_CHIPS - 1,))],
        compiler_params=_params(),
    )(packed, w_ada, b_ada, *after)


def _prenorm(x, mod, g_pre, after):
    t, d = x.shape
    tb = ROW_TILE

    def body(x_ref, mod_ref, g_ref, after_ref, h_ref, ht_ref):
        del after_ref
        xv = x_ref[...]
        r = lax.rsqrt(jnp.mean(xv * xv, axis=-1, keepdims=True) + EPS)
        h = (xv * r) * g_ref[...] * (1.0 + mod_ref[1:2, :]) + mod_ref[0:1, :]
        h_ref[...] = h.astype(BF16)
        ht_ref[...] = h.T.astype(BF16)

    return pl.pallas_call(
        body, name="prenorm", grid=(t // tb,),
        out_shape=(jax.ShapeDtypeStruct((t, d), BF16), jax.ShapeDtypeStruct((d, t), BF16)),
        in_specs=[pl.BlockSpec((tb, d), lambda i: (i, 0)), pl.BlockSpec((3, d), lambda i: (0, 0)),
                  pl.BlockSpec((1, d), lambda i: (0, 0)), ANY],
        out_specs=(pl.BlockSpec((tb, d), lambda i: (i, 0)), pl.BlockSpec((d, tb), lambda i: (0, i))),
        compiler_params=_params(("parallel",)),
    )(x, mod, g_pre, after)


def _proj_chunks(proj, h, w, chunks, name):
    t, d = h.shape
    ws = w.shape[-1]
    tn = COL_TILE
    nt = ws // tn

    def body(chunk_ref, *refs):
        del chunk_ref
        a_ref, b_ref, o_ref = refs[-3:]
        o_ref[...] = jnp.dot(a_ref[...], b_ref[...].astype(BF16), preferred_element_type=F32).astype(BF16)

    if w.ndim == 3:
        w_spec = pl.BlockSpec((None, d, tn), lambda i, n, ch: (ch[i], 0, n))
    else:
        w_spec = pl.BlockSpec((d, tn), lambda i, n, ch: (0, n))
    first = proj is None
    grid_spec = pltpu.PrefetchScalarGridSpec(
        num_scalar_prefetch=1, grid=(chunks.shape[0], nt),
        in_specs=([] if first else [HBM]) + [pl.BlockSpec((t, d), lambda i, n, ch: (0, 0)), w_spec],
        out_specs=pl.BlockSpec((t, tn), lambda i, n, ch: (0, ch[i] * nt + n)))
    return pl.pallas_call(
        body, name=name, grid_spec=grid_spec,
        out_shape=jax.ShapeDtypeStruct((t, N_CHIPS * ws), BF16),
        input_output_aliases={} if first else {1: 0},
        compiler_params=_params(("parallel", "parallel")),
    )(*([chunks] if first else [chunks, proj]), h, w)


def _shift_rows(a, rows):
    idx = lax.broadcasted_iota(jnp.int32, a.shape, 0)
    prev = jnp.where(idx == 0, 0.0, pltpu.roll(a, 1, 0))
    nxt = jnp.where(idx == rows - 1, 0.0, pltpu.roll(a, rows - 1, 0))
    return prev, nxt


def _conv_fwd(conv_proj, conv_w, conv_b, dc):
    t = conv_proj.shape[0]
    ct = CONV_TILE
    nct = dc // ct

    def body(u_ref, cg_ref, w_ref, b_ref, co_ref):
        a = cg_ref[...].astype(F32) * u_ref[...].astype(F32)
        prev, nxt = _shift_rows(a, t)
        co_ref[...] = (w_ref[0:1, :] * prev + w_ref[1:2, :] * a + w_ref[2:3, :] * nxt + b_ref[...]).astype(BF16)

    return pl.pallas_call(
        body, name="conv_fwd", grid=(nct,),
        out_shape=jax.ShapeDtypeStruct((t, dc), BF16),
        in_specs=[pl.BlockSpec((t, ct), lambda i: (0, i)), pl.BlockSpec((t, ct), lambda i: (0, 2 * nct + i)),
                  pl.BlockSpec((3, ct), lambda i: (0, i)), pl.BlockSpec((1, ct), lambda i: (0, i))],
        out_specs=pl.BlockSpec((t, ct), lambda i: (0, i)),
        compiler_params=_params(("parallel",)),
    )(conv_proj, conv_proj, conv_w, conv_b)


def _to_residue_major(src_ref, dst_ref, r):
    seq = src_ref.shape[0] // r
    for res in range(r):
        dst_ref[res * seq:(res + 1) * seq, :] = src_ref[pl.ds(res, seq, stride=r), :].astype(dst_ref.dtype)


def _branch_operands(token_refs, stage, dil, r):
    if r == 1:
        return list(token_refs)
    for i, ref in enumerate(token_refs):
        stage[...] = ref[...].astype(F32)
        _to_residue_major(stage, dil.at[i], r)
    return [dil.at[i] for i in range(len(token_refs))]


def _scaled_queries(q):
    return (q.astype(F32) * (HEAD_DIM ** -0.5)).astype(BF16)


BLOCK_SHIFTS = (0, -SIDE, None)


def _band_bias(rel, slope):
    arel = jnp.abs(rel)
    return jnp.where(arel <= SIDE, arel.astype(F32) * slope, NEG_INF)


def _fill_bias_tiles(bias_ref, sl_ref, r, kw):
    base = lax.broadcasted_iota(jnp.int32, (ATT_BQ, kw), 1) - lax.broadcasted_iota(jnp.int32, (ATT_BQ, kw), 0)
    for hh in range(2):
        slope = -(sl_ref[hh:hh + 1, 0:kw] * float(r))
        for e, shift in enumerate(BLOCK_SHIFTS):
            shift = ATT_BQ - kw if shift is None else shift
            bias_ref[hh, e, :, 0:kw] = _band_bias(base + shift, slope)


def _fill_stacked_bias_tiles(bias_ref, sl_ref, r, kw):
    base = lax.broadcasted_iota(jnp.int32, (kw, ATT_BQ), 0) - lax.broadcasted_iota(jnp.int32, (kw, ATT_BQ), 1)
    for hh in range(2):
        slope = -(sl_ref[hh:hh + 1, 0:ATT_BQ] * float(r))
        for e, shift in enumerate(BLOCK_SHIFTS):
            shift = ATT_BQ - kw if shift is None else shift
            bias_ref[e, 0:kw, hh * ATT_BQ:(hh + 1) * ATT_BQ] = _band_bias(base + shift, slope)


def _first_head_lanes():
    return lax.broadcasted_iota(jnp.int32, (1, PAIR), 1) < HEAD_DIM


def _only_head(x, first, hh):
    return jnp.where(first if hh == 0 else jnp.logical_not(first), x, jnp.zeros_like(x))


def _block_place(g, seq_len, kw):
    nqb = seq_len // ATT_BQ
    if nqb == 1:
        row = pl.multiple_of(g * ATT_BQ, ATT_BQ)
        return row, row, 0
    res = g // nqb
    qb = g - res * nqb
    q0 = qb * ATT_BQ
    ks = jnp.clip(q0 - SIDE, 0, seq_len - kw)
    edge = jnp.where(qb == 0, 0, jnp.where(qb == nqb - 1, 2, 1))
    return (pl.multiple_of(res * seq_len + q0, ATT_BQ), pl.multiple_of(res * seq_len + ks, SIDE), edge)


def _qkv_specs(dc, da, t, index):
    return [pl.BlockSpec((t, PAIR), functools.partial(index, (4 * dc + comp * da) // PAIR)) for comp in range(3)]


def _attn_fwd(proj, slopes, dc, da):
    t = proj.shape[0]
    hp = da // PAIR
    n_blocks = t // ATT_BQ

    def body(q_ref, k_ref, v_ref, sl_ref, o_ref, lse_ref, stage, dil, bias, o_res, l_res, o_tok, l_tok):
        for b, (_, r) in enumerate(BRANCHES):
            seq_len = t // r
            kw = min(ATT_KW, seq_len)
            ops = _branch_operands([q_ref, k_ref, v_ref], stage, dil, r)
            _fill_bias_tiles(bias, sl_ref, r, kw)
            o_dst, l_dst = (o_tok.at[b], l_tok.at[b]) if r == 1 else (o_res, l_res)
            first = _first_head_lanes()

            def blocks(trip, carry, seq_len=seq_len, kw=kw, o_dst=o_dst, l_dst=l_dst, first=first, ops=ops):
                nt = (((1,), (1,)), ((), ()))
                places = [_block_place(trip * ATT_UNROLL + i, seq_len, kw) for i in range(ATT_UNROLL)]
                chains = [(i, hh) for i in range(ATT_UNROLL) for hh in range(2)]
                qs = [_scaled_queries(ops[0][pl.ds(qrow, ATT_BQ), :]) for qrow, _, _ in places]
                ks = [ops[1][pl.ds(krow, kw), :] for _, krow, _ in places]
                vs = [ops[2][pl.ds(krow, kw), :] for _, krow, _ in places]
                ss = [lax.dot_general(_only_head(qs[i], first, hh), ks[i], nt, preferred_element_type=F32)
                      + bias[hh, places[i][2], :, 0:kw] for i, hh in chains]
                tops = [jnp.max(s, axis=-1, keepdims=True) for s in ss]
                ps = [jnp.exp(s - m) for s, m in zip(ss, tops)]
                dens = [jnp.sum(p, axis=-1, keepdims=True) for p in ps]
                for i, (qrow, _, _) in enumerate(places):
                    weights = jnp.concatenate([ps[2 * i].astype(BF16), ps[2 * i + 1].astype(BF16)], axis=1)
                    values = jnp.concatenate([_only_head(vs[i], first, 0), _only_head(vs[i], first, 1)], axis=0)
                    den = jnp.where(first, dens[2 * i], dens[2 * i + 1])
                    o_dst[pl.ds(qrow, ATT_BQ), :] = jnp.dot(weights, values, preferred_element_type=F32) / den
                    l_dst[pl.ds(qrow, ATT_BQ), :] = jnp.where(first, tops[2 * i], tops[2 * i + 1]) + jnp.log(den)
                return carry

            lax.fori_loop(0, n_blocks // ATT_UNROLL, blocks, 0)
            if r > 1:
                for res in range(r):
                    rows = slice(res * seq_len, (res + 1) * seq_len)
                    o_tok[b, pl.ds(res, seq_len, stride=r), :] = o_res[rows, :]
                    l_tok[b, pl.ds(res, seq_len, stride=r), :] = l_res[rows, :]

        def merge(i, carry):
            rows = pl.ds(pl.multiple_of(i * ROW_TILE, ROW_TILE), ROW_TILE)
            la, lb, lc = l_tok[0, rows, :], l_tok[1, rows, :], l_tok[2, rows, :]
            m = jnp.maximum(jnp.maximum(la, lb), lc)
            wa, wb, wc = jnp.exp(la - m), jnp.exp(lb - m), jnp.exp(lc - m)
            den = wa + wb + wc
            o_ref[rows, :] = (wa * o_tok[0, rows, :] + wb * o_tok[1, rows, :] + wc * o_tok[2, rows, :]) * (1.0 / den)
            lse_ref[rows, :] = m + jnp.log(den)
            return carry

        lax.fori_loop(0, t // ROW_TILE, merge, 0)

    pair_spec = pl.BlockSpec((None, t, PAIR), lambda h: (h, 0, 0))
    return pl.pallas_call(
        body, name="attn_fwd", grid=(hp,),
        out_shape=(jax.ShapeDtypeStruct((hp, t, PAIR), F32), jax.ShapeDtypeStruct((hp, t, PAIR), F32)),
        in_specs=_qkv_specs(dc, da, t, lambda first, h: (0, first + h))
        + [pl.BlockSpec((None, 8, ATT_KW), lambda h: (h, 0, 0))],
        out_specs=(pair_spec, pair_spec),
        scratch_shapes=[pltpu.VMEM((t, PAIR), F32), pltpu.VMEM((3, t, PAIR), BF16),
                        pltpu.VMEM((2, 3, ATT_BQ, ATT_KW), F32),
                        pltpu.VMEM((t, PAIR), F32), pltpu.VMEM((t, PAIR), F32),
                        pltpu.VMEM((3, t, PAIR), F32), pltpu.VMEM((3, t, PAIR), F32)],
        compiler_params=_params(("parallel",)),
    )(proj, proj, proj, slopes)


def _attn_bwd(dproj, proj, d_o, lse, delta, slopes, dc, da, after):
    t = proj.shape[0]
    hp = da // PAIR
    n_blocks = t // ATT_BQ

    def all_branches(q_ref, k_ref, v_ref, do_ref, lse_ref, dl_ref, sl_ref,
                     stage, dil, packed, packed_res, row_vecs, bias_t, acc, tot):
        first = _first_head_lanes()
        lane = lax.broadcasted_iota(jnp.int32, (1, PAIR), 1)
        packed[...] = jnp.where((lane & (HEAD_DIM - 1)) < HEAD_DIM // 2, lse_ref[...], dl_ref[...])
        for b, (_, r) in enumerate(BRANCHES):
            seq_len = t // r
            kw = min(ATT_KW, seq_len)
            ops = _branch_operands([q_ref, k_ref, v_ref, do_ref], stage, dil, r)
            scalars = packed
            if r > 1:
                _to_residue_major(packed, packed_res, r)
                scalars = packed_res
            for g in range(n_blocks):
                flipped = scalars[g * ATT_BQ:(g + 1) * ATT_BQ, :].T
                for row in range(4):
                    row_vecs[g, row:row + 1, :] = flipped[row * (HEAD_DIM // 2):row * (HEAD_DIM // 2) + 1, :]
            _fill_stacked_bias_tiles(bias_t, sl_ref, r, kw)
            acc[1] = jnp.zeros((t, PAIR), F32)
            acc[2] = jnp.zeros((t, PAIR), F32)

            def blocks(trip, carry, seq_len=seq_len, kw=kw, ops=ops):
                nt = (((1,), (1,)), ((), ()))
                group = range(ATT_UNROLL)
                places = [_block_place(trip * ATT_UNROLL + i, seq_len, kw) for i in group]
                ks, vs, q2s, do2s, lse2s, dl2s = [], [], [], [], [], []
                for i, (qrow, krow, _) in zip(group, places):
                    q = _scaled_queries(ops[0][pl.ds(qrow, ATT_BQ), :])
                    dov = ops[3][pl.ds(qrow, ATT_BQ), :]
                    ks.append(ops[1][pl.ds(krow, kw), :])
                    vs.append(ops[2][pl.ds(krow, kw), :])
                    q2s.append(jnp.concatenate([_only_head(q, first, 0), _only_head(q, first, 1)], axis=0))
                    do2s.append(jnp.concatenate([_only_head(dov, first, 0), _only_head(dov, first, 1)], axis=0))
                    rows = row_vecs[trip * ATT_UNROLL + i]
                    lse2s.append(jnp.concatenate([rows[0:1, :], rows[2:3, :]], axis=1))
                    dl2s.append(jnp.concatenate([rows[1:2, :], rows[3:4, :]], axis=1))
                s_ts = [lax.dot_general(ks[i], q2s[i], nt, preferred_element_type=F32) for i in group]
                dp_ts = [lax.dot_general(vs[i], do2s[i], nt, preferred_element_type=F32) for i in group]
                p_ts = [jnp.exp(s_ts[i] + bias_t[places[i][2], 0:kw, :] - lse2s[i]) for i in group]
                ds_ts = [p_ts[i] * (dp_ts[i] - dl2s[i]) for i in group]
                dvs = [jnp.dot(p_ts[i].astype(BF16), do2s[i], preferred_element_type=F32) for i in group]
                dks = [jnp.dot(ds_ts[i].astype(BF16), q2s[i], preferred_element_type=F32) for i in group]
                dss = [ds_ts[i].T.astype(BF16) for i in group]
                dqs = [jnp.dot(dss[i][0:ATT_BQ, :], _only_head(ks[i], first, 0), preferred_element_type=F32)
                       + jnp.dot(dss[i][ATT_BQ:2 * ATT_BQ, :], _only_head(ks[i], first, 1), preferred_element_type=F32)
                       for i in group]
                for i, (qrow, krow, _) in zip(group, places):
                    acc[0, pl.ds(qrow, ATT_BQ), :] = dqs[i] * (HEAD_DIM ** -0.5)
                    acc[1, pl.ds(krow, kw), :] += dks[i]
                    acc[2, pl.ds(krow, kw), :] += dvs[i]
                return carry

            lax.fori_loop(0, n_blocks // ATT_UNROLL, blocks, 0)
            for comp in range(3):
                if r == 1:
                    tot[comp] = acc[comp]
                else:
                    for res in range(r):
                        tok = pl.ds(res, seq_len, stride=r)
                        tot[comp, tok, :] = tot[comp, tok, :] + acc[comp, res * seq_len:(res + 1) * seq_len, :]

    first_q = (4 * dc) // PAIR

    def body(dproj_in, q_ref, k_ref, v_ref, do_ref, lse_ref, dl_ref, sl_ref, after_ref, out_ref, *scratch):
        del dproj_in, after_ref
        work, out_stage, out_sems = scratch[:-2], scratch[-2], scratch[-1]
        h = pl.program_id(0)
        all_branches(q_ref, k_ref, v_ref, do_ref, lse_ref, dl_ref, sl_ref, *work)

        def out_copy(comp):
            cols = pl.ds(pl.multiple_of((first_q + comp * hp + h) * PAIR, PAIR), PAIR)
            return pltpu.make_async_copy(out_stage.at[comp], out_ref.at[:, cols], out_sems.at[comp])

        @pl.when(h > 0)
        def _():
            for comp in range(3):
                out_copy(comp).wait()

        for comp in range(3):
            out_stage[comp] = work[-1][comp].astype(BF16)
            out_copy(comp).start()

        @pl.when(h == hp - 1)
        def _():
            for comp in range(3):
                out_copy(comp).wait()

    pair_spec = pl.BlockSpec((None, t, PAIR), lambda h: (h, 0, 0))
    return pl.pallas_call(
        body, name="attn_bwd", grid=(hp,),
        out_shape=jax.ShapeDtypeStruct(dproj.shape, BF16),
        in_specs=[HBM] + _qkv_specs(dc, da, t, lambda first, h: (0, first + h))
        + [pair_spec, pair_spec, pair_spec, pl.BlockSpec((None, 8, ATT_KW), lambda h: (h, 0, 0)), ANY],
        out_specs=ANY,
        input_output_aliases={0: 0},
        scratch_shapes=[pltpu.VMEM((t, PAIR), F32), pltpu.VMEM((4, t, PAIR), BF16),
                        pltpu.VMEM((t, PAIR), F32), pltpu.VMEM((t, PAIR), F32),
                        pltpu.VMEM((n_blocks, 8, ATT_BQ), F32), pltpu.VMEM((3, ATT_KW, 2 * ATT_BQ), F32),
                        pltpu.VMEM((3, t, PAIR), F32), pltpu.VMEM((3, t, PAIR), F32),
                        pltpu.VMEM((3, t, PAIR), BF16), pltpu.SemaphoreType.DMA((3,))],
        compiler_params=_params(("arbitrary",)),
    )(dproj, proj, proj, proj, d_o, lse, delta, slopes, after)


def _mix_fwd(co, proj, o_mix, g_conv, g_attn_pairs):
    t, dc = co.shape
    hp = o_mix.shape[0]
    da = hp * PAIR
    tb = ROW_TILE

    def body(co_ref, bg_ref, zc_ref, za_ref, om_ref, gc_ref, ga_ref, ycat_ref, ycatt_ref):
        p = bg_ref[...].astype(F32) * co_ref[...].astype(F32)
        rc = lax.rsqrt(jnp.mean(p * p, axis=-1, keepdims=True) + EPS)
        yc = (p * rc) * gc_ref[...] * _silu(zc_ref[...].astype(F32))
        ycat_ref[:, 0:dc] = yc.astype(BF16)
        ycatt_ref[0:dc, :] = yc.T.astype(BF16)
        ssq = jnp.zeros((tb, 1), F32)
        for h in range(hp):
            o = om_ref[h]
            ssq = ssq + jnp.sum(o * o, axis=-1, keepdims=True)
        ra = lax.rsqrt(ssq * (1.0 / da) + EPS)
        for h in range(hp):
            ya = (om_ref[h] * ra) * ga_ref[h] * _silu(za_ref[:, h * PAIR:(h + 1) * PAIR].astype(F32))
            ycat_ref[:, dc + h * PAIR:dc + (h + 1) * PAIR] = ya.astype(BF16)
            ycatt_ref[dc + h * PAIR:dc + (h + 1) * PAIR, :] = ya.T.astype(BF16)

    pair_spec = pl.BlockSpec((hp, tb, PAIR), lambda i: (0, i, 0))
    return pl.pallas_call(
        body, name="mix_fwd", grid=(t // tb,),
        out_shape=(jax.ShapeDtypeStruct((t, dc + da), BF16), jax.ShapeDtypeStruct((dc + da, t), BF16)),
        in_specs=[pl.BlockSpec((tb, dc), lambda i: (i, 0)),
                  pl.BlockSpec((tb, dc), lambda i: (i, 1)),
                  pl.BlockSpec((tb, dc), lambda i: (i, 3)),
                  pl.BlockSpec((tb, da), lambda i: (i, 7)),
                  pair_spec,
                  pl.BlockSpec((1, dc), lambda i: (0, 0)),
                  pl.BlockSpec((hp, 1, PAIR), lambda i: (0, 0, 0))],
        out_specs=(pl.BlockSpec((tb, dc + da), lambda i: (i, 0)), pl.BlockSpec((dc + da, tb), lambda i: (0, i))),
        compiler_params=_params(("parallel",)),
    )(co, proj, proj, proj, o_mix, g_conv, g_attn_pairs)


def _out_fwd_bwd(ycat, woutf, x, target, mod, g_post):
    t, d = x.shape
    n = ycat.shape[1]
    tb = ROW_TILE

    def body(a_ref, w_ref, x_ref, tg_ref, mod_ref, g_ref, dout_ref, dy_ref, acc_ref):
        y = jnp.dot(a_ref[...], w_ref[...], preferred_element_type=F32)
        r = lax.rsqrt(jnp.mean(y * y, axis=-1, keepdims=True) + EPS)
        nh = y * r
        gate = mod_ref[2:3, :]
        nrm = nh * g_ref[...]
        err = x_ref[...] + gate * nrm - tg_ref[...]
        dout = err * (1.0 / d)
        dout_ref[...] = dout.astype(BF16)
        dn = dout * gate
        a = dn * g_ref[...]
        dy = r * (a - nh * jnp.mean(a * nh, axis=-1, keepdims=True))
        dy_ref[...] = dy.astype(BF16)
        loss = 0.5 * jnp.sum(jnp.sum(err * err, axis=-1, keepdims=True) * (1.0 / d), axis=0, keepdims=True)
        part = jnp.concatenate(
            [jnp.sum(dout * nrm, axis=0, keepdims=True), jnp.sum(dn * nh, axis=0, keepdims=True),
             jnp.broadcast_to(loss, (1, d)), jnp.zeros((5, d), F32)], axis=0)

        @pl.when(pl.program_id(0) == 0)
        def _():
            acc_ref[...] = jnp.zeros(acc_ref.shape, F32)

        acc_ref[...] += part

    return pl.pallas_call(
        body, name="out_fwd_bwd", grid=(t // tb,),
        out_shape=(jax.ShapeDtypeStruct((t, d), BF16), jax.ShapeDtypeStruct((t, d), BF16),
                   jax.ShapeDtypeStruct((8, d), F32)),
        in_specs=[pl.BlockSpec((tb, n), lambda i: (i, 0)), pl.BlockSpec((n, d), lambda i: (0, 0)),
                  pl.BlockSpec((tb, d), lambda i: (i, 0)), pl.BlockSpec((tb, d), lambda i: (i, 0)),
                  pl.BlockSpec((3, d), lambda i: (0, 0)), pl.BlockSpec((1, d), lambda i: (0, 0))],
        out_specs=(pl.BlockSpec((tb, d), lambda i: (i, 0)), pl.BlockSpec((tb, d), lambda i: (i, 0)),
                   pl.BlockSpec((8, d), lambda i: (0, 0))),
        compiler_params=_params(("arbitrary",)),
    )(ycat, woutf, x, target, mod, g_post)


def _matmul_nt(a, b, out_dtype, name):
    m, k = a.shape
    n = b.shape[0]
    tn = COL_TILE

    def body(a_ref, b_ref, o_ref):
        o_ref[...] = lax.dot_general(a_ref[...], b_ref[...], (((1,), (1,)), ((), ())),
                                     preferred_element_type=F32).astype(out_dtype)

    return pl.pallas_call(
        body, name=name, grid=(n // tn,),
        out_shape=jax.ShapeDtypeStruct((m, n), out_dtype),
        in_specs=[pl.BlockSpec((m, k), lambda i: (0, 0)), pl.BlockSpec((tn, k), lambda i: (i, 0))],
        out_specs=pl.BlockSpec((m, tn), lambda i: (0, i)),
        compiler_params=_params(("parallel",)),
    )(a, b)


def _mix_bwd(dycat, co, proj, o_mix, g_conv, g_attn_pairs):
    t, dc = co.shape
    hp = o_mix.shape[0]
    da = hp * PAIR
    tb = ROW_TILE

    def body(dy_ref, co_ref, bg_ref, zc_ref, za_ref, om_ref, gc_ref, ga_ref,
             dcp_ref, dco_ref, do_ref, dl_ref, dgc_ref, dga_ref):
        first = pl.program_id(0) == 0
        cov = co_ref[...].astype(F32)
        bg = bg_ref[...].astype(F32)
        zc = zc_ref[...].astype(F32)
        p = bg * cov
        rc = lax.rsqrt(jnp.mean(p * p, axis=-1, keepdims=True) + EPS)
        nh = p * rc
        dyc = dy_ref[:, 0:dc].astype(F32)
        dn = dyc * _silu(zc)
        a = dn * gc_ref[...]
        dp = rc * (a - nh * jnp.mean(a * nh, axis=-1, keepdims=True))
        dcp_ref[:, 0:dc] = jnp.zeros((tb, dc), BF16)
        dcp_ref[:, dc:2 * dc] = (dp * cov).astype(BF16)
        dcp_ref[:, 2 * dc:3 * dc] = jnp.zeros((tb, dc), BF16)
        dcp_ref[:, 3 * dc:4 * dc] = (dyc * nh * gc_ref[...] * _silu_grad(zc)).astype(BF16)
        dcp_ref[:, 4 * dc:4 * dc + 3 * da] = jnp.zeros((tb, 3 * da), BF16)
        dco_ref[...] = dp * bg

        @pl.when(first)
        def _():
            dgc_ref[...] = jnp.zeros(dgc_ref.shape, F32)
            dga_ref[...] = jnp.zeros(dga_ref.shape, F32)

        dgc_ref[...] += jnp.sum(dn * nh, axis=0, keepdims=True)

        ssq = jnp.zeros((tb, 1), F32)
        for h in range(hp):
            o = om_ref[h]
            ssq = ssq + jnp.sum(o * o, axis=-1, keepdims=True)
        ra = lax.rsqrt(ssq * (1.0 / da) + EPS)
        dot_an = jnp.zeros((tb, 1), F32)
        for h in range(hp):
            nha = om_ref[h] * ra
            za = za_ref[:, h * PAIR:(h + 1) * PAIR].astype(F32)
            dya = dy_ref[:, dc + h * PAIR:dc + (h + 1) * PAIR].astype(F32)
            dna = dya * _silu(za)
            dza = (dya * nha * ga_ref[h] * _silu_grad(za)).astype(BF16)
            dcp_ref[:, 4 * dc + 3 * da + h * PAIR:4 * dc + 3 * da + (h + 1) * PAIR] = dza
            dga_ref[h] += jnp.sum(dna * nha, axis=0, keepdims=True)
            dot_an = dot_an + jnp.sum(dna * ga_ref[h] * nha, axis=-1, keepdims=True)
        mean_an = dot_an * (1.0 / da)
        first_head = lax.broadcasted_iota(jnp.int32, (tb, PAIR), 1) < HEAD_DIM
        for h in range(hp):
            o = om_ref[h]
            nha = o * ra
            za = za_ref[:, h * PAIR:(h + 1) * PAIR].astype(F32)
            dya = dy_ref[:, dc + h * PAIR:dc + (h + 1) * PAIR].astype(F32)
            aa = dya * _silu(za) * ga_ref[h]
            d_o = ra * (aa - nha * mean_an)
            do_ref[h] = d_o.astype(BF16)
            prod = d_o * o
            both = jnp.sum(prod, axis=-1, keepdims=True)
            head0 = jnp.sum(jnp.where(first_head, prod, 0.0), axis=-1, keepdims=True)
            dl_ref[h] = jnp.where(first_head, head0, both - head0)

    pair_spec = pl.BlockSpec((hp, tb, PAIR), lambda i: (0, i, 0))
    return pl.pallas_call(
        body, name="mix_bwd", grid=(t // tb,),
        out_shape=(jax.ShapeDtypeStruct((t, 4 * dc + 4 * da), BF16), jax.ShapeDtypeStruct((t, dc), F32),
                   jax.ShapeDtypeStruct((hp, t, PAIR), BF16), jax.ShapeDtypeStruct((hp, t, PAIR), F32),
                   jax.ShapeDtypeStruct((1, dc), F32), jax.ShapeDtypeStruct((hp, 1, PAIR), F32)),
        in_specs=[pl.BlockSpec((tb, dc + da), lambda i: (i, 0)),
                  pl.BlockSpec((tb, dc), lambda i: (i, 0)),
                  pl.BlockSpec((tb, dc), lambda i: (i, 1)),
                  pl.BlockSpec((tb, dc), lambda i: (i, 3)),
                  pl.BlockSpec((tb, da), lambda i: (i, 7)),
                  pair_spec,
                  pl.BlockSpec((1, dc), lambda i: (0, 0)),
                  pl.BlockSpec((hp, 1, PAIR), lambda i: (0, 0, 0))],
        out_specs=(pl.BlockSpec((tb, 4 * dc + 4 * da), lambda i: (i, 0)), pl.BlockSpec((tb, dc), lambda i: (i, 0)),
                   pair_spec, pair_spec,
                   pl.BlockSpec((1, dc), lambda i: (0, 0)), pl.BlockSpec((hp, 1, PAIR), lambda i: (0, 0, 0))),
        compiler_params=_params(("arbitrary",)),
    )(dycat, co, proj, proj, proj, o_mix, g_conv, g_attn_pairs)


def _conv_bwd(dconv_proj, dco, conv_proj, conv_w, dc, after):
    t = dco.shape[0]
    ct = CONV_TILE
    nct = dc // ct

    def body(dcp_in_ref, dco_ref, u_ref, cg_ref, w_ref, after_ref, dcp_ref, acc_ref):
        del dcp_in_ref, after_ref
        which = pl.program_id(1)
        g = dco_ref[...]
        u = u_ref[...].astype(F32)
        cg = cg_ref[...].astype(F32)
        g_prev, g_next = _shift_rows(g, t)
        da = w_ref[0:1, :] * g_next + w_ref[1:2, :] * g + w_ref[2:3, :] * g_prev
        dcp_ref[...] = (da * jnp.where(which == 0, cg, u)).astype(BF16)
        a = cg * u
        a_prev, a_next = _shift_rows(a, t)
        acc_ref[...] = jnp.concatenate(
            [jnp.sum(g * a_prev, axis=0, keepdims=True), jnp.sum(g * a, axis=0, keepdims=True),
             jnp.sum(g * a_next, axis=0, keepdims=True), jnp.sum(g, axis=0, keepdims=True),
             jnp.zeros((4, ct), F32)], axis=0)

    return pl.pallas_call(
        body, name="conv_bwd", grid=(nct, 2),
        out_shape=(jax.ShapeDtypeStruct(dconv_proj.shape, BF16), jax.ShapeDtypeStruct((8, dc), F32)),
        in_specs=[HBM,
                  pl.BlockSpec((t, ct), lambda i, s: (0, i)),
                  pl.BlockSpec((t, ct), lambda i, s: (0, i)),
                  pl.BlockSpec((t, ct), lambda i, s: (0, 2 * nct + i)),
                  pl.BlockSpec((3, ct), lambda i, s: (0, i)), ANY],
        out_specs=(pl.BlockSpec((t, ct), lambda i, s: (0, 2 * s * nct + i)),
                   pl.BlockSpec((8, ct), lambda i, s: (0, i))),
        input_output_aliases={0: 0},
        compiler_params=_params(("arbitrary", "arbitrary")),
    )(dconv_proj, dco, conv_proj, conv_proj, conv_w, after)


def _dh(dproj, winf, after):
    t = dproj.shape[0]
    _, d, ws = winf.shape
    tm = tn = COL_TILE
    nt = (((1,), (1,)), ((), ()))

    def body(a_ref, w_ref, after_ref, o_ref):
        del after_ref
        acc = lax.dot_general(a_ref[:, 0:ws], w_ref[0], nt, preferred_element_type=F32)
        for j in range(1, N_CHIPS):
            acc = acc + lax.dot_general(a_ref[:, j * ws:(j + 1) * ws], w_ref[j], nt, preferred_element_type=F32)
        o_ref[...] = acc.astype(BF16)

    return pl.pallas_call(
        body, name="dh", grid=(d // tn, t // tm),
        out_shape=jax.ShapeDtypeStruct((t, d), BF16),
        in_specs=[pl.BlockSpec((tm, N_CHIPS * ws), lambda n, m: (m, 0)),
                  pl.BlockSpec((N_CHIPS, tn, ws), lambda n, m: (0, n, 0)), ANY],
        out_specs=pl.BlockSpec((tm, tn), lambda n, m: (m, n)),
        compiler_params=_params(("parallel", "parallel")),
    )(dproj, winf, after)


def _prenorm_bwd(x, dh, dout, mod, g_pre):
    t, d = x.shape
    tb = ROW_TILE

    def body(x_ref, dh_ref, dout_ref, mod_ref, g_ref, gx_ref, acc_ref):
        xv = x_ref[...]
        dhv = dh_ref[...].astype(F32)
        r = lax.rsqrt(jnp.mean(xv * xv, axis=-1, keepdims=True) + EPS)
        xh = xv * r
        one_scale = 1.0 + mod_ref[1:2, :]
        a = dhv * one_scale * g_ref[...]
        gx_ref[...] = dout_ref[...].astype(F32) + r * (a - xh * jnp.mean(a * xh, axis=-1, keepdims=True))
        part = jnp.concatenate(
            [jnp.sum(dhv, axis=0, keepdims=True), jnp.sum(dhv * xh * g_ref[...], axis=0, keepdims=True),
             jnp.sum(dhv * xh * one_scale, axis=0, keepdims=True), jnp.zeros((5, d), F32)], axis=0)

        @pl.when(pl.program_id(0) == 0)
        def _():
            acc_ref[...] = jnp.zeros(acc_ref.shape, F32)

        acc_ref[...] += part

    return pl.pallas_call(
        body, name="prenorm_bwd", grid=(t // tb,),
        out_shape=(jax.ShapeDtypeStruct((t, d), F32), jax.ShapeDtypeStruct((8, d), F32)),
        in_specs=[pl.BlockSpec((tb, d), lambda i: (i, 0)), pl.BlockSpec((tb, d), lambda i: (i, 0)),
                  pl.BlockSpec((tb, d), lambda i: (i, 0)), pl.BlockSpec((3, d), lambda i: (0, 0)),
                  pl.BlockSpec((1, d), lambda i: (0, 0))],
        out_specs=(pl.BlockSpec((tb, d), lambda i: (i, 0)), pl.BlockSpec((8, d), lambda i: (0, 0))),
        compiler_params=_params(("arbitrary",)),
    )(x, dh, dout, mod, g_pre)


def _chip_sums(mine, rsib, name, part=0, parts=1, after=()):
    _, half, cols = mine.shape
    rows = half // parts
    tr = min(rows, ROW_TILE)
    nt = rows // tr

    def body(g_ref, r_ref, *rest):
        rest[-1][...] = (g_ref[...].astype(F32) + r_ref[...].astype(F32)).astype(BF16)

    spec = pl.BlockSpec((None, tr, cols), lambda j, i: (j, part * nt + i, 0))
    return pl.pallas_call(
        body, name=name, grid=(N_CHIPS, nt),
        out_shape=jax.ShapeDtypeStruct((N_CHIPS, rows, cols), BF16),
        in_specs=[spec, spec] + [ANY] * len(after), out_specs=pl.BlockSpec((None, tr, cols), lambda j, i: (j, i, 0)),
        compiler_params=_params(("parallel", "parallel")),
    )(mine, rsib, *after)


def _owner_sum(place, mine, rsib, rici, name, part=0, parts=1):
    _, half, cols = mine.shape
    rows = half // parts
    tr = min(rows, ROW_TILE)
    nt = rows // tr

    def body(place_ref, g_ref, r_ref, i_ref, o_ref):
        del place_ref
        acc = g_ref[...].astype(F32) + r_ref[...].astype(F32)
        for k in range(N_CHIPS - 1):
            acc = acc + i_ref[k].astype(F32)
        o_ref[...] = acc

    own = pl.BlockSpec((None, tr, cols), lambda i, p: (p[0], part * nt + i, 0))
    grid_spec = pltpu.PrefetchScalarGridSpec(
        num_scalar_prefetch=1, grid=(nt,),
        in_specs=[own, own, pl.BlockSpec((N_CHIPS - 1, tr, cols), lambda i, p: (0, i, 0))],
        out_specs=pl.BlockSpec((tr, cols), lambda i, p: (p[1] * (half // tr) + part * nt + i, 0)))
    return pl.pallas_call(
        body, name=name, grid_spec=grid_spec,
        out_shape=jax.ShapeDtypeStruct((2 * half, cols), F32),
        compiler_params=_params(("parallel",)),
    )(place, mine, rsib, rici)


def _adam_math(w, g, m, v):
    m2 = ADAM_B1 * m + (1.0 - ADAM_B1) * g
    v2 = ADAM_B2 * v + (1.0 - ADAM_B2) * (g * g)
    m_hat = m2 / (1.0 - ADAM_B1 ** ADAM_STEP)
    v_hat = v2 / (1.0 - ADAM_B2 ** ADAM_STEP)
    delta = -ADAM_LR * (m_hat / (jnp.sqrt(v_hat) + ADAM_EPS) + ADAM_WD * w)
    return delta, m2, v2


def _adamw(w, g, m, v, name, part=0, parts=1, prev=None):
    rows, cols = w.shape
    tr = min(rows, ROW_TILE)

    def body(*refs):
        w_ref, g_ref, m_ref, v_ref, go_ref, d_ref, m2_ref, v2_ref = refs[-8:]
        g = g_ref[...]
        go_ref[...] = g
        d_ref[...], m2_ref[...], v2_ref[...] = _adam_math(w_ref[...], g, m_ref[...], v_ref[...])

    if parts == 1:
        grid, spec = (rows // tr,), pl.BlockSpec((tr, cols), lambda i: (i, 0))
    else:
        per_half = rows // 2 // tr
        nt = per_half // parts
        grid, spec = (2, nt), pl.BlockSpec((tr, cols), lambda r, i: (r * per_half + part * nt + i, 0))
    olds = [] if prev is None else list(prev)
    return pl.pallas_call(
        body, name=name, grid=grid,
        out_shape=(jax.ShapeDtypeStruct(w.shape, F32),) * 4,
        in_specs=[HBM] * len(olds) + [spec] * 4, out_specs=(spec,) * 4,
        input_output_aliases={i: i for i in range(len(olds))},
        compiler_params=_params(("parallel",) * len(grid)),
    )(*olds, w, g, m, v)


def _ada_grad_adamw(c_all_t, dmod_cols, w, m, v):
    d, wa = w.shape
    tr = ROW_TILE

    def body(ct_ref, dm_ref, w_ref, m_ref, v_ref, g_ref, d_ref, m2_ref, v2_ref):
        act = _silu(ct_ref[...])
        g = act[:, 0:1] * dm_ref[0:1, :]
        for b in range(1, N_DEV):
            g = g + act[:, b:b + 1] * dm_ref[b:b + 1, :]
        g_ref[...] = g
        d_ref[...], m2_ref[...], v2_ref[...] = _adam_math(w_ref[...], g, m_ref[...], v_ref[...])

    spec = pl.BlockSpec((tr, wa), lambda i: (i, 0))
    return pl.pallas_call(
        body, name="ada_grad_adamw", grid=(d // tr,),
        out_shape=(jax.ShapeDtypeStruct(w.shape, F32),) * 4,
        in_specs=[pl.BlockSpec((tr, N_DEV), lambda i: (i, 0)), pl.BlockSpec((N_DEV, wa), lambda i: (0, 0)),
                  spec, spec, spec],
        out_specs=(spec,) * 4,
        compiler_params=_params(("parallel",)),
    )(c_all_t, dmod_cols, w, m, v)


def _small_update(place, gathered, pieces, weights, moments_m, moments_v):
    n = gathered.shape[1]
    k = len(weights)

    def body(place_ref, g_ref, *refs):
        w_refs, m_refs, v_refs = refs[0:k], refs[k:2 * k], refs[2 * k:3 * k]
        outs = refs[3 * k:]
        total = g_ref[0:SUBLANES, :]
        for dev in range(1, N_DEV):
            total = total + g_ref[SUBLANES * dev:SUBLANES * (dev + 1), :]

        def flat(offset, length):
            segments, pos = [], offset
            while pos < offset + length:
                row, col = divmod(pos, n)
                take = min(offset + length - pos, n - col)
                segments.append(total[row:row + 1, col:col + take])
                pos += take
            return jnp.concatenate(segments, axis=1) if len(segments) > 1 else segments[0]

        chip = place_ref[0]
        for i, (w_ref, m_ref, v_ref) in enumerate(zip(w_refs, m_refs, v_refs)):
            g = flat(*pieces[i])
            if w_ref.ndim == 3:
                rows, cols = w_ref.shape[1:]
                full = pieces[i][1] // rows
                picked = []
                for r in range(rows):
                    blocks = [g[:, r * full + q * cols:r * full + (q + 1) * cols] for q in range(N_CHIPS)]
                    mine = blocks[N_CHIPS - 1]
                    for q in range(N_CHIPS - 2, -1, -1):
                        mine = jnp.where(chip == q, blocks[q], mine)
                    picked.append(mine)
                g = jnp.concatenate(picked, axis=0)
                w, m, v = w_ref[0], m_ref[0], v_ref[0]
            else:
                w, m, v = w_ref[...], m_ref[...], v_ref[...]
            delta, m2, v2 = _adam_math(w, g, m, v)
            for j, val in enumerate((g, delta, m2, v2)):
                out = outs[j * k + i]
                if w_ref.ndim == 3:
                    out[0] = val
                else:
                    out[...] = val
        outs[4 * k][...] = flat(*pieces[k])

    shapes = [jax.ShapeDtypeStruct(w.shape, F32) for w in weights]
    grid_spec = pltpu.PrefetchScalarGridSpec(
        num_scalar_prefetch=1, grid=(1,),
        in_specs=[pl.BlockSpec(gathered.shape, lambda i, p: (0, 0))]
        + [pl.BlockSpec(a.shape, functools.partial(lambda nd, i, p: (0,) * nd, a.ndim))
           for a in (*weights, *moments_m, *moments_v)],
        out_specs=tuple(pl.BlockSpec(s.shape, functools.partial(lambda nd, i, p: (0,) * nd, len(s.shape)))
                        for s in shapes * 4) + (pl.BlockSpec((1, LANES), lambda i, p: (0, 0)),))
    outs = pl.pallas_call(
        body, name="small_update", grid_spec=grid_spec,
        out_shape=tuple(shapes * 4) + (jax.ShapeDtypeStruct((1, LANES), F32),),
        compiler_params=_params(("arbitrary",)),
    )(place, gathered, *weights, *moments_m, *moments_v)
    return outs[0:k], outs[k:2 * k], outs[2 * k:3 * k], outs[3 * k:4 * k], outs[4 * k]


def _pack_small(pieces):
    flat = [p.reshape(-1).astype(F32) for p in pieces]
    offsets, total = [], 0
    for p in flat:
        offsets.append(total)
        total += p.shape[0]
    padded = -(-total // SMALL_ALIGN) * SMALL_ALIGN
    if padded > total:
        flat.append(jnp.zeros((padded - total,), F32))
    return jnp.concatenate(flat).reshape(8, padded // 8), offsets


def _alibi_slope_rows(n_heads):
    slopes = 2.0 ** (-8.0 * jnp.arange(1, n_heads + 1, dtype=F32) / n_heads)
    rows = jnp.zeros((n_heads // 2, 8), F32).at[:, 0:2].set(slopes.reshape(n_heads // 2, 2))
    return jnp.broadcast_to(rows[:, :, None], (n_heads // 2, 8, ATT_KW))


def kernel(x, c, w_ada, b_ada, g_pre, w_in, conv_w, conv_b, g_conv, g_attn, w_out, g_post, loss_target, m_w_ada, m_b_ada, m_g_pre, m_w_in, m_conv_w, m_conv_b, m_g_conv, m_g_attn, m_w_out, m_g_post, v_w_ada, v_b_ada, v_g_pre, v_w_in, v_conv_w, v_conv_b, v_g_conv, v_g_attn, v_w_out, v_g_post):
    t, d = x.shape[1], x.shape[2]
    dc = conv_b.shape[1]
    da = g_attn.shape[1]
    hp = da // PAIR
    ws = w_in.shape[2]
    wa = w_ada.shape[2]
    cws = conv_w.shape[2]
    assert t % ROW_TILE == 0 and d % ROW_TILE == 0 and dc % COL_TILE == 0 and da % COL_TILE == 0
    assert ws == 2 * dc and dc == da and t // BRANCHES[-1][1] >= ATT_BQ

    mx, my, mc = _my_place()
    chip = _chip_of(mx, my)
    dev = 2 * chip + mc
    place = jnp.stack([chip, mc]).astype(jnp.int32)

    x2, tgt2 = x[0], loss_target[0]
    w_ada2, w_in2, w_out2 = w_ada[0], w_in[0], w_out[0]

    win_slots = _cast_into_slot(place, w_in2, "cast_w_in")
    packed, offs = _pack_small([c[0], conv_w[0]])
    seen, mod = _ada_modulation(packed, w_ada2, b_ada, d, after=(win_slots,))
    seen = seen.reshape(N_DEV, -1)
    c_all = seen[:, offs[0]:offs[0] + d]
    conv_w_full = seen[0::2, offs[1]:offs[1] + 3 * cws].reshape(N_CHIPS, 3, cws).transpose(1, 0, 2).reshape(3, dc)

    win_flight, send_in, recv_in, started = _gather_start(win_slots, mod)

    y_chip, x_chip, d_chip = (_chip_of(mx, 1 - my), _chip_of(1 - mx, my), _chip_of(1 - mx, 1 - my))
    own_chunk, near_chunks, far_chunk = (jnp.stack(js).astype(jnp.int32) for js in ([chip], [y_chip, x_chip], [d_chip]))
    h, ht = _prenorm(x2, mod, g_pre, started)
    proj = _proj_chunks(None, h, w_in2, own_chunk, "proj_own")
    win_flight, wout_flight, relay_send_in, relay_recv_in, send_out, recv_out = _gather_relay_in(
        win_flight, _cast_into_slot(place, w_out2, "cast_w_out"), recv_in, proj)
    win_flight = _forward_halves(
        _gather_wait_direct(win_flight, send_in, recv_in, proj, "gather_wait_w_in_direct"), (0, 1), "forward_w_in_direct")
    proj = _proj_chunks(proj, h, win_flight, near_chunks, "proj_neighbours")
    winf = _forward_halves(
        _gather_wait_relayed(win_flight, relay_send_in, relay_recv_in, proj, "gather_wait_w_in_relayed"),
        (2,), "forward_w_in_relayed")
    proj = _proj_chunks(proj, h, winf, far_chunk, "proj_diagonal")
    slopes = _alibi_slope_rows(da // HEAD_DIM)
    co = _conv_fwd(proj, conv_w_full, conv_b, dc)
    wout_flight, relay_send_out, relay_recv_out = _gather_relay_out(wout_flight, recv_out, co)
    o_mix, lse = _attn_fwd(proj, slopes, dc, da)
    g_attn_pairs = g_attn.reshape(hp, 1, PAIR)
    ycat, ycat_t = _mix_fwd(co, proj, o_mix, g_conv, g_attn_pairs)
    wout_flight = _gather_wait_direct(wout_flight, send_out, recv_out, ycat, "gather_wait_w_out_direct")
    wout_flight = _gather_wait_relayed(wout_flight, relay_send_out, relay_recv_out, ycat, "gather_wait_w_out_relayed")
    woutf = _forward_halves(wout_flight, (0, 1, 2), "forward_w_out").reshape(dc + da, d)
    dout, dy, post_sums = _out_fwd_bwd(ycat, woutf, x2, tgt2, mod, g_post)

    gout, rsib_out = _dw_swapped(ycat_t, dy, N_CHIPS, 1, "dw_out")
    csum_out = _chip_sums(gout, rsib_out, "rs_chip_sum_out")
    ssem_out, rsem_out, csum_out, land_out, sent_out = _owners_start(csum_out, "rs_owners_start_out")
    dycat = _matmul_nt(dy, woutf, BF16, "dycat")
    dproj, dco, d_o, delta, dg_conv, dg_attn = _mix_bwd(dycat, co, proj, o_mix, g_conv, g_attn_pairs)
    dproj, conv_sums = _conv_bwd(dproj, dco, proj, conv_w_full, dc, sent_out)
    dproj = _attn_bwd(dproj, proj, d_o, lse, delta, slopes, dc, da, sent_out)
    gin, rsib_in = _dw_swapped(ht, dproj, 1, N_CHIPS, "dw_in")
    ssem_in0, rsem_in0, csum_in0, land_in0, sent_in0 = _owners_start(
        _chip_sums(gin, rsib_in, "rs_chip_sum_in0", 0, 2), "rs_owners_start_in0")
    ssem_in1, rsem_in1, csum_in1, land_in1, sent_in = _owners_start(
        _chip_sums(gin, rsib_in, "rs_chip_sum_in1", 1, 2, after=(sent_in0,)), "rs_owners_start_in1")
    dh = _dh(dproj, winf, sent_in)
    grad_x, pre_sums = _prenorm_bwd(x2, dh, dout, mod, g_pre)

    small, so = _pack_small([
        pre_sums[0], pre_sums[1], post_sums[0],
        pre_sums[2], conv_sums[0:3], conv_sums[3], dg_conv, dg_attn, post_sums[1], post_sums[2, 0:128]])
    ssem_small, rsem_small, small, land_small, sent_small = _allgather8_start(small, dev, "gather_small_start")

    rici_out = _owners_wait(ssem_out, rsem_out, csum_out, land_out, [grad_x, sent_small], "rs_owners_wait_out")
    grad_w_out = _join_halves(_owner_sum(place, gout, rsib_out, rici_out, "rs_owner_sum_out"), "rs_join_halves_out")
    grad_w_out, delta_w_out, new_m_w_out, new_v_w_out = _adamw(
        w_out2, grad_w_out, m_w_out[0], v_w_out[0], "adamw_w_out")

    rici_in = _owners_wait(ssem_in0, rsem_in0, csum_in0, land_in0, [delta_w_out], "rs_owners_wait_in0")
    full_in0, jsend0, jrecv0, joining0 = _join_start(
        _owner_sum(place, gin, rsib_in, rici_in, "rs_owner_sum_in0", 0, 2), "rs_join_start_in0", 0, 2)
    full_in0 = _join_wait(full_in0, jsend0, jrecv0, [joining0], "rs_join_wait_in0", 0, 2)
    updated_in = _adamw(w_in2, full_in0, m_w_in[0], v_w_in[0], "adamw_w_in0", 0, 2)
    rici_in = _owners_wait(ssem_in1, rsem_in1, csum_in1, land_in1, [updated_in[1]], "rs_owners_wait_in1")
    full_in1, jsend1, jrecv1, joining1 = _join_start(
        _owner_sum(place, gin, rsib_in, rici_in, "rs_owner_sum_in1", 1, 2), "rs_join_start_in1", 1, 2)

    small_seen = _allgather8_wait(ssem_small, rsem_small, small, land_small, [joining1], "gather_small_wait")
    small_w = [b_ada, g_pre, conv_w, conv_b, g_conv, g_attn, g_post]
    small_m = [m_b_ada, m_g_pre, m_conv_w, m_conv_b, m_g_conv, m_g_attn, m_g_post]
    small_v = [v_b_ada, v_g_pre, v_conv_w, v_conv_b, v_g_conv, v_g_attn, v_g_post]
    pieces = [(0, 3 * d), (so[3], d), (so[4], 3 * dc), (so[5], dc), (so[6], dc), (so[7], da), (so[8], d), (so[9], LANES)]
    g_small, d_small, m_small, v_small, loss_row = _small_update(place, small_seen, pieces, small_w, small_m, small_v)
    loss = loss_row[0, 0]
    grad_b_ada, grad_g_pre, grad_conv_w, grad_conv_b, grad_g_conv, grad_g_attn, grad_g_post = g_small
    dmod_cols = lax.dynamic_slice_in_dim(small_seen.reshape(N_DEV, -1), chip * wa, wa, axis=1)
    grad_w_ada, delta_w_ada, new_m_w_ada, new_v_w_ada = _ada_grad_adamw(c_all.T, dmod_cols, w_ada2, m_w_ada[0], v_w_ada[0])

    full_in1 = _join_wait(full_in1, jsend1, jrecv1, [delta_w_ada, d_small[0]], "rs_join_wait_in1", 1, 2)
    grad_w_in, delta_w_in, new_m_w_in, new_v_w_in = _adamw(
        w_in2, full_in1, m_w_in[0], v_w_in[0], "adamw_w_in1", 1, 2, updated_in)

    def lead(a):
        return a.reshape((1,) + a.shape)

    grads = [lead(grad_w_ada), grad_b_ada, grad_g_pre, lead(grad_w_in), grad_conv_w, grad_conv_b, grad_g_conv,
             grad_g_attn, lead(grad_w_out), grad_g_post]
    deltas = [lead(delta_w_ada), d_small[0], d_small[1], lead(delta_w_in), d_small[2], d_small[3], d_small[4],
              d_small[5], lead(delta_w_out), d_small[6]]
    new_ms = [lead(new_m_w_ada), m_small[0], m_small[1], lead(new_m_w_in), m_small[2], m_small[3], m_small[4],
              m_small[5], lead(new_m_w_out), m_small[6]]
    new_vs = [lead(new_v_w_ada), v_small[0], v_small[1], lead(new_v_w_in), v_small[2], v_small[3], v_small[4],
              v_small[5], lead(new_v_w_out), v_small[6]]
    return (loss, lead(grad_x), *grads, *deltas, *new_ms, *new_vs)
```

```python
import functools

import jax
import jax.numpy as jnp
from jax import lax
from jax.experimental import pallas as pl
from jax.experimental.pallas import tpu as pltpu

F32 = jnp.float32
BF16 = jnp.bfloat16
MESH = pl.DeviceIdType.MESH
HBM = pl.BlockSpec(memory_space=pltpu.HBM)
VMEM = pl.BlockSpec(memory_space=pltpu.VMEM)
ANY = pl.BlockSpec(memory_space=pl.ANY)
SEM = pl.BlockSpec(memory_space=pltpu.SEMAPHORE)
EFFECT = pltpu.SideEffectType.DATAFLOW_SIDE_EFFECTING
SUBLANES, LANES = 8, 128
TOKEN = jax.ShapeDtypeStruct((SUBLANES, LANES), jnp.float32)

HEAD_DIM = 64
PAIR = 2 * HEAD_DIM
assert PAIR == LANES
BRANCHES = ((128, 1), (512, 4), (2048, 16))
SIDE = 64
EPS = 1e-6
NEG_INF = -1e30
N_CHIPS = 4
N_DEV = 8

ADAM_LR = 0.001
ADAM_B1 = 0.9
ADAM_B2 = 0.999
ADAM_EPS = 1e-08
ADAM_WD = 0.01
ADAM_STEP = 10

VMEM_LIMIT_BYTES = 56 * 1024 * 1024
ROW_TILE = 256
ROW_PIECE = 128
ROW_GROUP = 16
LANE_CHUNK = 512
GROUP_UNROLL = 4
COL_TILE = 512
CONV_TILE = 256
ATT_BQ = 128
ATT_KW = ATT_BQ + 2 * SIDE
ATT_UNROLL = 4
SMALL_ALIGN = SUBLANES * LANES


def _params(semantics=None):
    kw = {"vmem_limit_bytes": VMEM_LIMIT_BYTES}
    if semantics is not None:
        kw["dimension_semantics"] = semantics
    return pltpu.CompilerParams(**kw)


def _silu(z):
    return z * jax.nn.sigmoid(z)


def _silu_grad(z):
    s = jax.nn.sigmoid(z)
    return s * (1.0 + z * (1.0 - s))


def _my_place():
    return lax.axis_index("x"), lax.axis_index("y"), lax.axis_index("c")


def _flip(a, bit):
    return 1 - a if bit else a


def _chip_of(x, y):
    return 2 * x + y


def _allgather8_start(v, me, name):
    rows_per, n = v.shape
    land = lax.dynamic_update_slice(jnp.zeros((N_DEV * rows_per, n), v.dtype), v, (me * rows_per, 0))

    def body(v_ref, land_ref, send_sems, recv_sems, v_thru, land_thru, token_ref):
        del v_thru, land_thru
        x, y, c = _my_place()
        mine = land_ref.at[pl.ds(pl.multiple_of((4 * x + 2 * y + c) * rows_per, rows_per), rows_per), :]
        for k in range(1, N_DEV):
            peer = (_flip(x, k & 4), _flip(y, k & 2), _flip(c, k & 1))
            pltpu.make_async_remote_copy(
                src_ref=v_ref, dst_ref=mine, send_sem=send_sems.at[k - 1], recv_sem=recv_sems.at[k - 1],
                device_id=peer, device_id_type=MESH).start()
        token_ref[...] = jnp.zeros(token_ref.shape, F32)

    sems = pltpu.SemaphoreType.DMA((N_DEV - 1,))
    return pl.pallas_call(
        body, name=name,
        out_shape=(sems, sems, jax.ShapeDtypeStruct(v.shape, v.dtype), jax.ShapeDtypeStruct(land.shape, land.dtype), TOKEN),
        in_specs=[HBM, HBM], out_specs=(SEM, SEM, HBM, HBM, VMEM),
        input_output_aliases={0: 2, 1: 3},
        compiler_params=pltpu.CompilerParams(has_side_effects=EFFECT),
    )(pltpu.with_memory_space_constraint(v, pltpu.HBM), pltpu.with_memory_space_constraint(land, pltpu.HBM))


def _allgather8_wait(send_sems, recv_sems, v, land, after, name):
    rows_per = v.shape[0]

    def body(v_ref, land_ref, send_ref, recv_ref, *rest):
        del rest
        x, y, c = _my_place()
        for k in range(1, N_DEV):
            peer = (_flip(x, k & 4), _flip(y, k & 2), _flip(c, k & 1))
            src = 4 * peer[0] + 2 * peer[1] + peer[2]
            cp = pltpu.make_async_remote_copy(
                src_ref=v_ref, dst_ref=land_ref.at[pl.ds(pl.multiple_of(src * rows_per, rows_per), rows_per), :],
                send_sem=send_ref.at[k - 1], recv_sem=recv_ref.at[k - 1], device_id=peer, device_id_type=MESH)
            cp.wait_send()
            cp.wait_recv()

    return pl.pallas_call(
        body, name=name,
        out_shape=(jax.ShapeDtypeStruct(v.shape, v.dtype), jax.ShapeDtypeStruct(land.shape, land.dtype)),
        in_specs=[HBM, HBM, SEM, SEM] + [ANY] * len(after), out_specs=(HBM, HBM),
        input_output_aliases={0: 0, 1: 1},
        compiler_params=pltpu.CompilerParams(has_side_effects=EFFECT),
    )(v, land, send_sems, recv_sems, *after)[1]


def _half_rows(ref, chip, which, half):
    return ref.at[chip, pl.ds(pl.multiple_of(which * half, half), half), :]


def _ici_peers(x, y, c):
    peers = [(_flip(x, k & 2), _flip(y, k & 1), c) for k in (1, 2, 3)]
    return [(peer, _chip_of(peer[0], peer[1])) for peer in peers]


def _part_rows(ref, chip, core, part):
    quarter = ref.shape[1] // 4
    return ref.at[chip, pl.ds(pl.multiple_of((2 * core + part) * quarter, quarter), quarter), :]


def _neighbours(x, y, c):
    return [((x, 1 - y, c), _chip_of(x, 1 - y)), ((1 - x, y, c), _chip_of(1 - x, y)),
            ((1 - x, 1 - y, c), _chip_of(1 - x, 1 - y))]


def _start_direct(buf, send_sems, recv_sems):
    x, y, c = _my_place()
    me = _chip_of(x, y)
    for n, (peer, _) in enumerate(_neighbours(x, y, c)[0:2]):
        for part in ((0, 1), (1, 0))[n]:
            piece = _part_rows(buf, me, c, part)
            pltpu.make_async_remote_copy(
                src_ref=piece, dst_ref=piece, send_sem=send_sems.at[2 * n + part], recv_sem=recv_sems.at[2 * n + part],
                device_id=peer, device_id_type=MESH).start()


def _relay(buf, recv_sems, relay_send, relay_recv):
    x, y, c = _my_place()
    nbrs = _neighbours(x, y, c)
    for n in range(2):
        part = n
        piece = _part_rows(buf, nbrs[n][1], c, part)
        pltpu.make_async_remote_copy(
            src_ref=piece, dst_ref=piece, send_sem=relay_send.at[part], recv_sem=recv_sems.at[2 * n + part],
            device_id=nbrs[n][0], device_id_type=MESH).wait_recv()
        pltpu.make_async_remote_copy(
            src_ref=piece, dst_ref=piece, send_sem=relay_send.at[part], recv_sem=relay_recv.at[part],
            device_id=nbrs[1 - n][0], device_id_type=MESH).start()


def _gather_start(win_slots, after):
    def body(win_in, after_ref, win_ref, send_sems, recv_sems, token_ref):
        del win_in, after_ref
        _start_direct(win_ref, send_sems, recv_sems)
        token_ref[...] = jnp.zeros(token_ref.shape, F32)

    sems = pltpu.SemaphoreType.DMA((4,))
    return pl.pallas_call(
        body, name="gather_start",
        out_shape=(jax.ShapeDtypeStruct(win_slots.shape, win_slots.dtype), sems, sems, TOKEN),
        in_specs=[HBM, ANY], out_specs=(HBM, SEM, SEM, VMEM),
        input_output_aliases={0: 0},
        compiler_params=pltpu.CompilerParams(has_side_effects=EFFECT),
    )(win_slots, after)


def _gather_relay_in(win, wout_slots, recv_in, after):
    def body(win_in, wout_in, recv_in_ref, after_ref, win_ref, wout_ref, relay_send, relay_recv, send_out, recv_out):
        del win_in, wout_in, after_ref
        _relay(win_ref, recv_in_ref, relay_send, relay_recv)
        _start_direct(wout_ref, send_out, recv_out)

    two, four = pltpu.SemaphoreType.DMA((2,)), pltpu.SemaphoreType.DMA((4,))
    return pl.pallas_call(
        body, name="gather_relay_w_in",
        out_shape=(jax.ShapeDtypeStruct(win.shape, win.dtype), jax.ShapeDtypeStruct(wout_slots.shape, wout_slots.dtype),
                   two, two, four, four),
        in_specs=[HBM, HBM, SEM, ANY], out_specs=(HBM, HBM, SEM, SEM, SEM, SEM),
        input_output_aliases={0: 0, 1: 1},
        compiler_params=pltpu.CompilerParams(has_side_effects=EFFECT),
    )(win, wout_slots, recv_in, after)


def _gather_relay_out(wout, recv_out, after):
    def body(wout_in, recv_out_ref, after_ref, wout_ref, relay_send, relay_recv):
        del wout_in, after_ref
        _relay(wout_ref, recv_out_ref, relay_send, relay_recv)

    two = pltpu.SemaphoreType.DMA((2,))
    return pl.pallas_call(
        body, name="gather_relay_w_out",
        out_shape=(jax.ShapeDtypeStruct(wout.shape, wout.dtype), two, two),
        in_specs=[HBM, SEM, ANY], out_specs=(HBM, SEM, SEM),
        input_output_aliases={0: 0},
        compiler_params=pltpu.CompilerParams(has_side_effects=EFFECT),
    )(wout, recv_out, after)


def _gather_wait_direct(buf, send_sems, recv_sems, after, name):
    def body(buf_in, send_ref, recv_ref, after_ref, buf_ref):
        del buf_in, after_ref
        x, y, c = _my_place()
        me = _chip_of(x, y)
        for n, (peer, chip) in enumerate(_neighbours(x, y, c)[0:2]):
            second = 1 - n
            pltpu.make_async_remote_copy(
                src_ref=_part_rows(buf_ref, me, c, second), dst_ref=_part_rows(buf_ref, chip, c, second),
                send_sem=send_ref.at[2 * n + second], recv_sem=recv_ref.at[2 * n + second],
                device_id=peer, device_id_type=MESH).wait_recv()
            for part in range(2):
                piece = _part_rows(buf_ref, me, c, part)
                pltpu.make_async_remote_copy(
                    src_ref=piece, dst_ref=piece, send_sem=send_ref.at[2 * n + part], recv_sem=recv_ref.at[2 * n + part],
                    device_id=peer, device_id_type=MESH).wait_send()

    return pl.pallas_call(
        body, name=name,
        out_shape=jax.ShapeDtypeStruct(buf.shape, buf.dtype),
        in_specs=[HBM, SEM, SEM, ANY], out_specs=HBM,
        input_output_aliases={0: 0},
        compiler_params=pltpu.CompilerParams(has_side_effects=EFFECT),
    )(buf, send_sems, recv_sems, after)


def _gather_wait_relayed(buf, relay_send, relay_recv, after, name):
    def body(buf_in, rsend_ref, rrecv_ref, after_ref, buf_ref):
        del buf_in, after_ref
        x, y, c = _my_place()
        nbrs = _neighbours(x, y, c)
        for n in range(2):
            relayed = _part_rows(buf_ref, nbrs[n][1], c, n)
            cp = pltpu.make_async_remote_copy(
                src_ref=relayed, dst_ref=_part_rows(buf_ref, nbrs[2][1], c, n),
                send_sem=rsend_ref.at[n], recv_sem=rrecv_ref.at[n], device_id=nbrs[1 - n][0], device_id_type=MESH)
            cp.wait_recv()
            cp.wait_send()

    return pl.pallas_call(
        body, name=name,
        out_shape=jax.ShapeDtypeStruct(buf.shape, buf.dtype),
        in_specs=[HBM, SEM, SEM, ANY], out_specs=HBM,
        input_output_aliases={0: 0},
        compiler_params=pltpu.CompilerParams(has_side_effects=EFFECT),
    )(buf, relay_send, relay_recv, after)


def _forward_halves(buf, which, name):
    half = buf.shape[1] // 2

    def body(buf_in, buf_ref, send_sems, recv_sems):
        del buf_in
        x, y, c = _my_place()
        sibling = (x, y, 1 - c)
        chips = [_neighbours(x, y, c)[n][1] for n in which]
        started = []
        for k, src_chip in enumerate(chips):
            landed = _half_rows(buf_ref, src_chip, c, half)
            fw = pltpu.make_async_remote_copy(
                src_ref=landed, dst_ref=landed, send_sem=send_sems.at[k], recv_sem=recv_sems.at[k],
                device_id=sibling, device_id_type=MESH)
            fw.start()
            started.append(fw)
        for k, src_chip in enumerate(chips):
            other = _half_rows(buf_ref, src_chip, 1 - c, half)
            pltpu.make_async_remote_copy(
                src_ref=other, dst_ref=other, send_sem=send_sems.at[k], recv_sem=recv_sems.at[k],
                device_id=sibling, device_id_type=MESH).wait_recv()
        for fw in started:
            fw.wait_send()

    return pl.pallas_call(
        body, name=name,
        out_shape=jax.ShapeDtypeStruct(buf.shape, buf.dtype),
        in_specs=[HBM], out_specs=HBM,
        input_output_aliases={0: 0},
        scratch_shapes=[pltpu.SemaphoreType.DMA((len(which),))] * 2,
    )(buf)


def _dw_swapped(a, b, row_chunks, col_chunks, name):
    r, t = a.shape
    c_all = b.shape[1]
    chunks = row_chunks * col_chunks
    rq, cq = r // row_chunks, c_all // col_chunks
    half = rq // 2
    tn = COL_TILE
    nt = cq // tn
    steps = col_chunks * nt

    def body(a_ref, b_ref, mine_ref, sib_ref, stage, send_sems, recv_sems):
        x, y, c = _my_place()
        j, n = pl.program_id(0), pl.program_id(1)
        step = j * nt + n
        slot = step % 2
        res = jnp.dot(a_ref[...], b_ref[...], preferred_element_type=F32).astype(BF16)

        def landing(jj, nn):
            cols = pl.ds(pl.multiple_of(nn * tn, tn), tn)
            return sib_ref.at[:, :, cols] if col_chunks == 1 else sib_ref.at[pl.ds(jj, 1), :, cols]

        def copy(slot_, step_, jj, nn):
            return pltpu.make_async_remote_copy(
                src_ref=stage.at[slot_], dst_ref=landing(jj, nn), send_sem=send_sems.at[slot_],
                recv_sem=recv_sems.at[step_], device_id=(x, y, 1 - c), device_id_type=MESH)

        @pl.when(step >= 2)
        def _():
            copy(slot, step, j, n).wait_send()

        for q in range(row_chunks):
            lo = res[q * rq:q * rq + half, :]
            hi = res[q * rq + half:(q + 1) * rq, :]
            mine_ref[q] = jnp.where(c == 0, lo, hi)
            stage[slot, q] = jnp.where(c == 0, hi, lo)
        copy(slot, step, j, n).start()

        @pl.when(step == steps - 1)
        def _():
            for s in range(max(steps - 2, 0), steps):
                copy(s % 2, s, j, n).wait_send()
            for s in range(steps):
                copy(s % 2, s, j, n).wait_recv()

    shape = jax.ShapeDtypeStruct((chunks, half, cq), BF16)
    return pl.pallas_call(
        body, name=name, grid=(col_chunks, nt),
        out_shape=(shape, shape),
        in_specs=[pl.BlockSpec((r, t), lambda j, n: (0, 0)), pl.BlockSpec((t, tn), lambda j, n: (0, j * nt + n))],
        out_specs=(pl.BlockSpec((row_chunks, half, tn), lambda j, n: (j, 0, n)), ANY),
        scratch_shapes=[pltpu.VMEM((2, row_chunks, half, tn), BF16), pltpu.SemaphoreType.DMA((2,)),
                        pltpu.SemaphoreType.DMA((steps,))],
        compiler_params=_params(("arbitrary", "arbitrary")),
    )(a, b)


def _owners_start(csum, name, after=()):
    land = pltpu.with_memory_space_constraint(lax.empty((N_CHIPS - 1,) + csum.shape[1:], csum.dtype), pltpu.HBM)

    def body(csum_ref, land_ref, *rest):
        send_sems, recv_sems, _, _, token_ref = rest[len(after):]
        x, y, c = _my_place()
        for k, (peer, owner) in enumerate(_ici_peers(x, y, c)):
            pltpu.make_async_remote_copy(
                src_ref=csum_ref.at[owner], dst_ref=land_ref.at[k], send_sem=send_sems.at[k], recv_sem=recv_sems.at[k],
                device_id=peer, device_id_type=MESH).start()
        token_ref[...] = jnp.zeros(token_ref.shape, F32)

    sems = pltpu.SemaphoreType.DMA((N_CHIPS - 1,))
    return pl.pallas_call(
        body, name=name,
        out_shape=(sems, sems, jax.ShapeDtypeStruct(csum.shape, csum.dtype),
                   jax.ShapeDtypeStruct(land.shape, land.dtype), TOKEN),
        in_specs=[HBM, HBM] + [ANY] * len(after), out_specs=(SEM, SEM, HBM, HBM, VMEM),
        input_output_aliases={0: 2, 1: 3},
        compiler_params=pltpu.CompilerParams(has_side_effects=EFFECT),
    )(pltpu.with_memory_space_constraint(csum, pltpu.HBM), land, *after)


def _owners_wait(send_sems, recv_sems, csum, land, after, name):
    def body(csum_ref, land_ref, send_ref, recv_ref, *rest):
        del rest
        x, y, c = _my_place()
        for k, (peer, owner) in enumerate(_ici_peers(x, y, c)):
            cp = pltpu.make_async_remote_copy(
                src_ref=csum_ref.at[owner], dst_ref=land_ref.at[k], send_sem=send_ref.at[k], recv_sem=recv_ref.at[k],
                device_id=peer, device_id_type=MESH)
            cp.wait_send()
            cp.wait_recv()

    return pl.pallas_call(
        body, name=name,
        out_shape=(jax.ShapeDtypeStruct(csum.shape, csum.dtype), jax.ShapeDtypeStruct(land.shape, land.dtype)),
        in_specs=[HBM, HBM, SEM, SEM] + [ANY] * len(after), out_specs=(HBM, HBM),
        input_output_aliases={0: 0, 1: 1},
        compiler_params=pltpu.CompilerParams(has_side_effects=EFFECT),
    )(csum, land, send_sems, recv_sems, *after)[1]


def _join_halves(full, name):
    rows = full.shape[0] // 2

    def body(full_in, full_ref, send_sem, recv_sem):
        del full_in
        x, y, c = _my_place()
        sibling = (x, y, 1 - c)
        mine = full_ref.at[pl.ds(pl.multiple_of(c * rows, rows), rows), :]
        theirs = full_ref.at[pl.ds(pl.multiple_of((1 - c) * rows, rows), rows), :]
        cp = pltpu.make_async_remote_copy(
            src_ref=mine, dst_ref=mine, send_sem=send_sem, recv_sem=recv_sem, device_id=sibling, device_id_type=MESH)
        cp.start()
        pltpu.make_async_remote_copy(
            src_ref=theirs, dst_ref=theirs, send_sem=send_sem, recv_sem=recv_sem,
            device_id=sibling, device_id_type=MESH).wait_recv()
        cp.wait_send()

    return pl.pallas_call(
        body, name=name,
        out_shape=jax.ShapeDtypeStruct(full.shape, full.dtype),
        in_specs=[HBM], out_specs=HBM,
        input_output_aliases={0: 0},
        scratch_shapes=[pltpu.SemaphoreType.DMA, pltpu.SemaphoreType.DMA],
    )(full)


def _join_start(full, name, part, parts):
    half = full.shape[0] // 2
    rows = half // parts

    def body(full_in, full_ref, send_sem, recv_sem, token_ref):
        del full_in
        x, y, c = _my_place()
        mine = full_ref.at[pl.ds(pl.multiple_of(c * half + part * rows, rows), rows), :]
        pltpu.make_async_remote_copy(
            src_ref=mine, dst_ref=mine, send_sem=send_sem.at[0], recv_sem=recv_sem.at[0],
            device_id=(x, y, 1 - c), device_id_type=MESH).start()
        token_ref[...] = jnp.zeros(token_ref.shape, F32)

    one = pltpu.SemaphoreType.DMA((1,))
    return pl.pallas_call(
        body, name=name,
        out_shape=(jax.ShapeDtypeStruct(full.shape, full.dtype), one, one, TOKEN),
        in_specs=[HBM], out_specs=(HBM, SEM, SEM, VMEM),
        input_output_aliases={0: 0},
        compiler_params=pltpu.CompilerParams(has_side_effects=EFFECT),
    )(full)


def _join_wait(full, send_sem, recv_sem, after, name, part, parts):
    half = full.shape[0] // 2
    rows = half // parts

    def body(full_in, send_ref, recv_ref, *rest):
        del full_in
        full_ref = rest[-1]
        x, y, c = _my_place()
        cp = pltpu.make_async_remote_copy(
            src_ref=full_ref.at[pl.ds(pl.multiple_of(c * half + part * rows, rows), rows), :],
            dst_ref=full_ref.at[pl.ds(pl.multiple_of((1 - c) * half + part * rows, rows), rows), :],
            send_sem=send_ref.at[0], recv_sem=recv_ref.at[0], device_id=(x, y, 1 - c), device_id_type=MESH)
        cp.wait_send()
        cp.wait_recv()

    return pl.pallas_call(
        body, name=name,
        out_shape=jax.ShapeDtypeStruct(full.shape, full.dtype),
        in_specs=[HBM, SEM, SEM] + [ANY] * len(after), out_specs=HBM,
        input_output_aliases={0: 0},
        compiler_params=pltpu.CompilerParams(has_side_effects=EFFECT),
    )(full, send_sem, recv_sem, *after)


def _cast_into_slot(place, w, name):
    rows, cols = w.shape
    tr = min(rows, ROW_TILE)

    def body(place_ref, w_ref, o_ref):
        del place_ref
        o_ref[...] = w_ref[...].astype(BF16)

    grid_spec = pltpu.PrefetchScalarGridSpec(
        num_scalar_prefetch=1, grid=(rows // tr,),
        in_specs=[pl.BlockSpec((tr, cols), lambda i, p: (i, 0))],
        out_specs=pl.BlockSpec((None, tr, cols), lambda i, p: (p[0], i, 0)))
    return pl.pallas_call(
        body, name=name, grid_spec=grid_spec,
        out_shape=jax.ShapeDtypeStruct((N_CHIPS, rows, cols), BF16),
        compiler_params=_params(("parallel",)),
    )(place, w)


def _ada_modulation(packed, w_ada, b_ada, d, after=()):
    rows_per, n = packed.shape
    d_model, wa = w_ada.shape

    def body(v_ref, w_hbm, b_ref, *rest):
        all_ref, mod_ref, w_vmem, part_ref, parts_ref, load_sem, send1, recv1, send2, recv2 = rest[len(after):]
        x, y, c = _my_place()
        me = 4 * x + 2 * y + c
        chip = _chip_of(x, y)
        load = pltpu.make_async_copy(w_hbm, w_vmem, load_sem)
        load.start()

        def rows(idx):
            return all_ref.at[pl.ds(pl.multiple_of(idx * rows_per, rows_per), rows_per), :]

        all_ref[pl.ds(pl.multiple_of(me * rows_per, rows_per), rows_per), :] = v_ref[...]
        copies = []
        for k in range(1, N_DEV):
            peer = (_flip(x, k & 4), _flip(y, k & 2), _flip(c, k & 1))
            cp = pltpu.make_async_remote_copy(
                src_ref=v_ref, dst_ref=rows(me), send_sem=send1.at[k - 1], recv_sem=recv1.at[k - 1],
                device_id=peer, device_id_type=MESH)
            cp.start()
            copies.append((cp, peer))
        for k, (cp, peer) in enumerate(copies):
            pltpu.make_async_remote_copy(
                src_ref=v_ref, dst_ref=rows(4 * peer[0] + 2 * peer[1] + peer[2]), send_sem=send1.at[k],
                recv_sem=recv1.at[k], device_id=peer, device_id_type=MESH).wait_recv()
        for cp, _ in copies:
            cp.wait_send()

        def c_of(dev):
            segments, pos = [], 0
            while pos < d:
                row, col = divmod(pos, n)
                take = min(d - pos, n - col)
                segments.append(all_ref[dev * rows_per + row:dev * rows_per + row + 1, col:col + take])
                pos += take
            return jnp.concatenate(segments, axis=1)

        c_all = jnp.concatenate([c_of(dev) for dev in range(N_DEV)], axis=0)
        load.wait()
        part_ref[...] = jnp.dot(_silu(c_all), w_vmem[...], precision=lax.Precision.HIGHEST, preferred_element_type=F32)
        parts_ref[chip] = part_ref[...]
        swaps = []
        for k, (peer, _) in enumerate(_ici_peers(x, y, c)):
            cp = pltpu.make_async_remote_copy(
                src_ref=part_ref, dst_ref=parts_ref.at[chip], send_sem=send2.at[k], recv_sem=recv2.at[k],
                device_id=peer, device_id_type=MESH)
            cp.start()
            swaps.append(cp)
        for k, (peer, peer_chip) in enumerate(_ici_peers(x, y, c)):
            pltpu.make_async_remote_copy(
                src_ref=part_ref, dst_ref=parts_ref.at[peer_chip], send_sem=send2.at[k], recv_sem=recv2.at[k],
                device_id=peer, device_id_type=MESH).wait_recv()
        for cp in swaps:
            cp.wait_send()
        flat = jnp.concatenate([parts_ref[j, pl.ds(me, 1), :] for j in range(N_CHIPS)], axis=1) + b_ref[...]
        mod_ref[...] = jnp.concatenate([flat[:, i * d:(i + 1) * d] for i in range(3)], axis=0)

    return pl.pallas_call(
        body, name="ada_modulation",
        out_shape=(jax.ShapeDtypeStruct((N_DEV * rows_per, n), F32), jax.ShapeDtypeStruct((3, d), F32)),
        in_specs=[VMEM, ANY, VMEM] + [ANY] * len(after), out_specs=(VMEM, VMEM),
        scratch_shapes=[pltpu.VMEM((d_model, wa), F32), pltpu.VMEM((N_DEV, wa), F32),
                        pltpu.VMEM((N_CHIPS, N_DEV, wa), F32), pltpu.SemaphoreType.DMA,
                        pltpu.SemaphoreType.DMA((N_DEV - 1,)), pltpu.SemaphoreType.DMA((N_DEV - 1,)),
                        pltpu.SemaphoreType.DMA((N_CHIPS - 1,)), pltpu.SemaphoreType.DMA((N_CHIPS - 1,))],
        compiler_params=_params(),
    )(packed, w_ada, b_ada, *after)


def _prenorm(x, mod, g_pre, after):
    t, d = x.shape
    tb = ROW_TILE

    def body(x_ref, mod_ref, g_ref, after_ref, h_ref, ht_ref):
        del after_ref
        xv = x_ref[...]
        r = lax.rsqrt(jnp.mean(xv * xv, axis=-1, keepdims=True) + EPS)
        h = (xv * r) * g_ref[...] * (1.0 + mod_ref[1:2, :]) + mod_ref[0:1, :]
        h_ref[...] = h.astype(BF16)
        ht_ref[...] = h.T.astype(BF16)

    return pl.pallas_call(
        body, name="prenorm", grid=(t // tb,),
        out_shape=(jax.ShapeDtypeStruct((t, d), BF16), jax.ShapeDtypeStruct((d, t), BF16)),
        in_specs=[pl.BlockSpec((tb, d), lambda i: (i, 0)), pl.BlockSpec((3, d), lambda i: (0, 0)),
                  pl.BlockSpec((1, d), lambda i: (0, 0)), ANY],
        out_specs=(pl.BlockSpec((tb, d), lambda i: (i, 0)), pl.BlockSpec((d, tb), lambda i: (0, i))),
        compiler_params=_params(("parallel",)),
    )(x, mod, g_pre, after)


def _proj_chunks(proj, h, w, chunks, name):
    t, d = h.shape
    ws = w.shape[-1]
    tn = COL_TILE
    nt = ws // tn

    def body(chunk_ref, *refs):
        del chunk_ref
        a_ref, b_ref, o_ref = refs[-3:]
        o_ref[...] = jnp.dot(a_ref[...], b_ref[...].astype(BF16), preferred_element_type=F32).astype(BF16)

    if w.ndim == 3:
        w_spec = pl.BlockSpec((None, d, tn), lambda i, n, ch: (ch[i], 0, n))
    else:
        w_spec = pl.BlockSpec((d, tn), lambda i, n, ch: (0, n))
    first = proj is None
    grid_spec = pltpu.PrefetchScalarGridSpec(
        num_scalar_prefetch=1, grid=(chunks.shape[0], nt),
        in_specs=([] if first else [HBM]) + [pl.BlockSpec((t, d), lambda i, n, ch: (0, 0)), w_spec],
        out_specs=pl.BlockSpec((t, tn), lambda i, n, ch: (0, ch[i] * nt + n)))
    return pl.pallas_call(
        body, name=name, grid_spec=grid_spec,
        out_shape=jax.ShapeDtypeStruct((t, N_CHIPS * ws), BF16),
        input_output_aliases={} if first else {1: 0},
        compiler_params=_params(("parallel", "parallel")),
    )(*([chunks] if first else [chunks, proj]), h, w)


def _shift_rows(a, rows):
    idx = lax.broadcasted_iota(jnp.int32, a.shape, 0)
    prev = jnp.where(idx == 0, 0.0, pltpu.roll(a, 1, 0))
    nxt = jnp.where(idx == rows - 1, 0.0, pltpu.roll(a, rows - 1, 0))
    return prev, nxt


def _conv_fwd(conv_proj, conv_w, conv_b, dc):
    t = conv_proj.shape[0]
    ct = CONV_TILE
    nct = dc // ct

    def body(u_ref, cg_ref, w_ref, b_ref, co_ref):
        a = cg_ref[...].astype(F32) * u_ref[...].astype(F32)
        prev, nxt = _shift_rows(a, t)
        co_ref[...] = (w_ref[0:1, :] * prev + w_ref[1:2, :] * a + w_ref[2:3, :] * nxt + b_ref[...]).astype(BF16)

    return pl.pallas_call(
        body, name="conv_fwd", grid=(nct,),
        out_shape=jax.ShapeDtypeStruct((t, dc), BF16),
        in_specs=[pl.BlockSpec((t, ct), lambda i: (0, i)), pl.BlockSpec((t, ct), lambda i: (0, 2 * nct + i)),
                  pl.BlockSpec((3, ct), lambda i: (0, i)), pl.BlockSpec((1, ct), lambda i: (0, i))],
        out_specs=pl.BlockSpec((t, ct), lambda i: (0, i)),
        compiler_params=_params(("parallel",)),
    )(conv_proj, conv_proj, conv_w, conv_b)


def _to_residue_major(src_ref, dst_ref, r):
    seq = src_ref.shape[0] // r
    for res in range(r):
        dst_ref[res * seq:(res + 1) * seq, :] = src_ref[pl.ds(res, seq, stride=r), :].astype(dst_ref.dtype)


def _branch_operands(token_refs, stage, dil, r):
    if r == 1:
        return list(token_refs)
    for i, ref in enumerate(token_refs):
        stage[...] = ref[...].astype(F32)
        _to_residue_major(stage, dil.at[i], r)
    return [dil.at[i] for i in range(len(token_refs))]


def _scaled_queries(q):
    return (q.astype(F32) * (HEAD_DIM ** -0.5)).astype(BF16)


BLOCK_SHIFTS = (0, -SIDE, None)


def _band_bias(rel, slope):
    arel = jnp.abs(rel)
    return jnp.where(arel <= SIDE, arel.astype(F32) * slope, NEG_INF)


def _fill_bias_tiles(bias_ref, sl_ref, r, kw):
    base = lax.broadcasted_iota(jnp.int32, (ATT_BQ, kw), 1) - lax.broadcasted_iota(jnp.int32, (ATT_BQ, kw), 0)
    for hh in range(2):
        slope = -(sl_ref[hh:hh + 1, 0:kw] * float(r))
        for e, shift in enumerate(BLOCK_SHIFTS):
            shift = ATT_BQ - kw if shift is None else shift
            bias_ref[hh, e, :, 0:kw] = _band_bias(base + shift, slope)


def _fill_stacked_bias_tiles(bias_ref, sl_ref, r, kw):
    base = lax.broadcasted_iota(jnp.int32, (kw, ATT_BQ), 0) - lax.broadcasted_iota(jnp.int32, (kw, ATT_BQ), 1)
    for hh in range(2):
        slope = -(sl_ref[hh:hh + 1, 0:ATT_BQ] * float(r))
        for e, shift in enumerate(BLOCK_SHIFTS):
            shift = ATT_BQ - kw if shift is None else shift
            bias_ref[e, 0:kw, hh * ATT_BQ:(hh + 1) * ATT_BQ] = _band_bias(base + shift, slope)


def _first_head_lanes():
    return lax.broadcasted_iota(jnp.int32, (1, PAIR), 1) < HEAD_DIM


def _only_head(x, first, hh):
    return jnp.where(first if hh == 0 else jnp.logical_not(first), x, jnp.zeros_like(x))


def _block_place(g, seq_len, kw):
    nqb = seq_len // ATT_BQ
    if nqb == 1:
        row = pl.multiple_of(g * ATT_BQ, ATT_BQ)
        return row, row, 0
    res = g // nqb
    qb = g - res * nqb
    q0 = qb * ATT_BQ
    ks = jnp.clip(q0 - SIDE, 0, seq_len - kw)
    edge = jnp.where(qb == 0, 0, jnp.where(qb == nqb - 1, 2, 1))
    return (pl.multiple_of(res * seq_len + q0, ATT_BQ), pl.multiple_of(res * seq_len + ks, SIDE), edge)


def _qkv_specs(dc, da, t, index):
    return [pl.BlockSpec((t, PAIR), functools.partial(index, (4 * dc + comp * da) // PAIR)) for comp in range(3)]


def _attn_fwd(proj, slopes, dc, da):
    t = proj.shape[0]
    hp = da // PAIR
    n_blocks = t // ATT_BQ

    def body(q_ref, k_ref, v_ref, sl_ref, o_ref, lse_ref, stage, dil, bias, o_res, l_res, o_tok, l_tok):
        for b, (_, r) in enumerate(BRANCHES):
            seq_len = t // r
            kw = min(ATT_KW, seq_len)
            ops = _branch_operands([q_ref, k_ref, v_ref], stage, dil, r)
            _fill_bias_tiles(bias, sl_ref, r, kw)
            o_dst, l_dst = (o_tok.at[b], l_tok.at[b]) if r == 1 else (o_res, l_res)
            first = _first_head_lanes()

            def blocks(trip, carry, seq_len=seq_len, kw=kw, o_dst=o_dst, l_dst=l_dst, first=first, ops=ops):
                nt = (((1,), (1,)), ((), ()))
                places = [_block_place(trip * ATT_UNROLL + i, seq_len, kw) for i in range(ATT_UNROLL)]
                chains = [(i, hh) for i in range(ATT_UNROLL) for hh in range(2)]
                qs = [_scaled_queries(ops[0][pl.ds(qrow, ATT_BQ), :]) for qrow, _, _ in places]
                ks = [ops[1][pl.ds(krow, kw), :] for _, krow, _ in places]
                vs = [ops[2][pl.ds(krow, kw), :] for _, krow, _ in places]
                ss = [lax.dot_general(_only_head(qs[i], first, hh), ks[i], nt, preferred_element_type=F32)
                      + bias[hh, places[i][2], :, 0:kw] for i, hh in chains]
                tops = [jnp.max(s, axis=-1, keepdims=True) for s in ss]
                ps = [jnp.exp(s - m) for s, m in zip(ss, tops)]
                dens = [jnp.sum(p, axis=-1, keepdims=True) for p in ps]
                for i, (qrow, _, _) in enumerate(places):
                    weights = jnp.concatenate([ps[2 * i].astype(BF16), ps[2 * i + 1].astype(BF16)], axis=1)
                    values = jnp.concatenate([_only_head(vs[i], first, 0), _only_head(vs[i], first, 1)], axis=0)
                    den = jnp.where(first, dens[2 * i], dens[2 * i + 1])
                    o_dst[pl.ds(qrow, ATT_BQ), :] = jnp.dot(weights, values, preferred_element_type=F32) / den
                    l_dst[pl.ds(qrow, ATT_BQ), :] = jnp.where(first, tops[2 * i], tops[2 * i + 1]) + jnp.log(den)
                return carry

            lax.fori_loop(0, n_blocks // ATT_UNROLL, blocks, 0)
            if r > 1:
                for res in range(r):
                    rows = slice(res * seq_len, (res + 1) * seq_len)
                    o_tok[b, pl.ds(res, seq_len, stride=r), :] = o_res[rows, :]
                    l_tok[b, pl.ds(res, seq_len, stride=r), :] = l_res[rows, :]

        def merge(i, carry):
            rows = pl.ds(pl.multiple_of(i * ROW_TILE, ROW_TILE), ROW_TILE)
            la, lb, lc = l_tok[0, rows, :], l_tok[1, rows, :], l_tok[2, rows, :]
            m = jnp.maximum(jnp.maximum(la, lb), lc)
            wa, wb, wc = jnp.exp(la - m), jnp.exp(lb - m), jnp.exp(lc - m)
            den = wa + wb + wc
            o_ref[rows, :] = (wa * o_tok[0, rows, :] + wb * o_tok[1, rows, :] + wc * o_tok[2, rows, :]) * (1.0 / den)
            lse_ref[rows, :] = m + jnp.log(den)
            return carry

        lax.fori_loop(0, t // ROW_TILE, merge, 0)

    pair_spec = pl.BlockSpec((None, t, PAIR), lambda h: (h, 0, 0))
    return pl.pallas_call(
        body, name="attn_fwd", grid=(hp,),
        out_shape=(jax.ShapeDtypeStruct((hp, t, PAIR), F32), jax.ShapeDtypeStruct((hp, t, PAIR), F32)),
        in_specs=_qkv_specs(dc, da, t, lambda first, h: (0, first + h))
        + [pl.BlockSpec((None, 8, ATT_KW), lambda h: (h, 0, 0))],
        out_specs=(pair_spec, pair_spec),
        scratch_shapes=[pltpu.VMEM((t, PAIR), F32), pltpu.VMEM((3, t, PAIR), BF16),
                        pltpu.VMEM((2, 3, ATT_BQ, ATT_KW), F32),
                        pltpu.VMEM((t, PAIR), F32), pltpu.VMEM((t, PAIR), F32),
                        pltpu.VMEM((3, t, PAIR), F32), pltpu.VMEM((3, t, PAIR), F32)],
        compiler_params=_params(("parallel",)),
    )(proj, proj, proj, slopes)


def _attn_bwd(dproj, proj, d_o, lse, delta, slopes, dc, da, after):
    t = proj.shape[0]
    hp = da // PAIR
    n_blocks = t // ATT_BQ

    def all_branches(q_ref, k_ref, v_ref, do_ref, lse_ref, dl_ref, sl_ref,
                     stage, dil, packed, packed_res, row_vecs, bias_t, acc, tot):
        first = _first_head_lanes()
        lane = lax.broadcasted_iota(jnp.int32, (1, PAIR), 1)
        packed[...] = jnp.where((lane & (HEAD_DIM - 1)) < HEAD_DIM // 2, lse_ref[...], dl_ref[...])
        for b, (_, r) in enumerate(BRANCHES):
            seq_len = t // r
            kw = min(ATT_KW, seq_len)
            ops = _branch_operands([q_ref, k_ref, v_ref, do_ref], stage, dil, r)
            scalars = packed
            if r > 1:
                _to_residue_major(packed, packed_res, r)
                scalars = packed_res
            for g in range(n_blocks):
                flipped = scalars[g * ATT_BQ:(g + 1) * ATT_BQ, :].T
                for row in range(4):
                    row_vecs[g, row:row + 1, :] = flipped[row * (HEAD_DIM // 2):row * (HEAD_DIM // 2) + 1, :]
            _fill_stacked_bias_tiles(bias_t, sl_ref, r, kw)
            acc[1] = jnp.zeros((t, PAIR), F32)
            acc[2] = jnp.zeros((t, PAIR), F32)

            def blocks(trip, carry, seq_len=seq_len, kw=kw, ops=ops):
                nt = (((1,), (1,)), ((), ()))
                group = range(ATT_UNROLL)
                places = [_block_place(trip * ATT_UNROLL + i, seq_len, kw) for i in group]
                ks, vs, q2s, do2s, lse2s, dl2s = [], [], [], [], [], []
                for i, (qrow, krow, _) in zip(group, places):
                    q = _scaled_queries(ops[0][pl.ds(qrow, ATT_BQ), :])
                    dov = ops[3][pl.ds(qrow, ATT_BQ), :]
                    ks.append(ops[1][pl.ds(krow, kw), :])
                    vs.append(ops[2][pl.ds(krow, kw), :])
                    q2s.append(jnp.concatenate([_only_head(q, first, 0), _only_head(q, first, 1)], axis=0))
                    do2s.append(jnp.concatenate([_only_head(dov, first, 0), _only_head(dov, first, 1)], axis=0))
                    rows = row_vecs[trip * ATT_UNROLL + i]
                    lse2s.append(jnp.concatenate([rows[0:1, :], rows[2:3, :]], axis=1))
                    dl2s.append(jnp.concatenate([rows[1:2, :], rows[3:4, :]], axis=1))
                s_ts = [lax.dot_general(ks[i], q2s[i], nt, preferred_element_type=F32) for i in group]
                dp_ts = [lax.dot_general(vs[i], do2s[i], nt, preferred_element_type=F32) for i in group]
                p_ts = [jnp.exp(s_ts[i] + bias_t[places[i][2], 0:kw, :] - lse2s[i]) for i in group]
                ds_ts = [p_ts[i] * (dp_ts[i] - dl2s[i]) for i in group]
                dvs = [jnp.dot(p_ts[i].astype(BF16), do2s[i], preferred_element_type=F32) for i in group]
                dks = [jnp.dot(ds_ts[i].astype(BF16), q2s[i], preferred_element_type=F32) for i in group]
                dss = [ds_ts[i].T.astype(BF16) for i in group]
                dqs = [jnp.dot(dss[i][0:ATT_BQ, :], _only_head(ks[i], first, 0), preferred_element_type=F32)
                       + jnp.dot(dss[i][ATT_BQ:2 * ATT_BQ, :], _only_head(ks[i], first, 1), preferred_element_type=F32)
                       for i in group]
                for i, (qrow, krow, _) in zip(group, places):
                    acc[0, pl.ds(qrow, ATT_BQ), :] = dqs[i] * (HEAD_DIM ** -0.5)
                    acc[1, pl.ds(krow, kw), :] += dks[i]
                    acc[2, pl.ds(krow, kw), :] += dvs[i]
                return carry

            lax.fori_loop(0, n_blocks // ATT_UNROLL, blocks, 0)
            for comp in range(3):
                if r == 1:
                    tot[comp] = acc[comp]
                else:
                    for res in range(r):
                        tok = pl.ds(res, seq_len, stride=r)
                        tot[comp, tok, :] = tot[comp, tok, :] + acc[comp, res * seq_len:(res + 1) * seq_len, :]

    first_q = (4 * dc) // PAIR

    def body(dproj_in, q_ref, k_ref, v_ref, do_ref, lse_ref, dl_ref, sl_ref, after_ref, out_ref, *scratch):
        del dproj_in, after_ref
        work, out_stage, out_sems = scratch[:-2], scratch[-2], scratch[-1]
        h = pl.program_id(0)
        all_branches(q_ref, k_ref, v_ref, do_ref, lse_ref, dl_ref, sl_ref, *work)

        def out_copy(comp):
            cols = pl.ds(pl.multiple_of((first_q + comp * hp + h) * PAIR, PAIR), PAIR)
            return pltpu.make_async_copy(out_stage.at[comp], out_ref.at[:, cols], out_sems.at[comp])

        @pl.when(h > 0)
        def _():
            for comp in range(3):
                out_copy(comp).wait()

        for comp in range(3):
            out_stage[comp] = work[-1][comp].astype(BF16)
            out_copy(comp).start()

        @pl.when(h == hp - 1)
        def _():
            for comp in range(3):
                out_copy(comp).wait()

    pair_spec = pl.BlockSpec((None, t, PAIR), lambda h: (h, 0, 0))
    return pl.pallas_call(
        body, name="attn_bwd", grid=(hp,),
        out_shape=jax.ShapeDtypeStruct(dproj.shape, BF16),
        in_specs=[HBM] + _qkv_specs(dc, da, t, lambda first, h: (0, first + h))
        + [pair_spec, pair_spec, pair_spec, pl.BlockSpec((None, 8, ATT_KW), lambda h: (h, 0, 0)), ANY],
        out_specs=ANY,
        input_output_aliases={0: 0},
        scratch_shapes=[pltpu.VMEM((t, PAIR), F32), pltpu.VMEM((4, t, PAIR), BF16),
                        pltpu.VMEM((t, PAIR), F32), pltpu.VMEM((t, PAIR), F32),
                        pltpu.VMEM((n_blocks, 8, ATT_BQ), F32), pltpu.VMEM((3, ATT_KW, 2 * ATT_BQ), F32),
                        pltpu.VMEM((3, t, PAIR), F32), pltpu.VMEM((3, t, PAIR), F32),
                        pltpu.VMEM((3, t, PAIR), BF16), pltpu.SemaphoreType.DMA((3,))],
        compiler_params=_params(("arbitrary",)),
    )(dproj, proj, proj, proj, d_o, lse, delta, slopes, after)


def _mix_fwd(co, proj, o_mix, g_conv, g_attn_pairs):
    t, dc = co.shape
    hp = o_mix.shape[0]
    da = hp * PAIR
    tb = ROW_TILE

    def body(co_ref, bg_ref, zc_ref, za_ref, om_ref, gc_ref, ga_ref, ycat_ref, ycatt_ref):
        p = bg_ref[...].astype(F32) * co_ref[...].astype(F32)
        rc = lax.rsqrt(jnp.mean(p * p, axis=-1, keepdims=True) + EPS)
        yc = (p * rc) * gc_ref[...] * _silu(zc_ref[...].astype(F32))
        ycat_ref[:, 0:dc] = yc.astype(BF16)
        ycatt_ref[0:dc, :] = yc.T.astype(BF16)
        ssq = jnp.zeros((tb, 1), F32)
        for h in range(hp):
            o = om_ref[h]
            ssq = ssq + jnp.sum(o * o, axis=-1, keepdims=True)
        ra = lax.rsqrt(ssq * (1.0 / da) + EPS)
        for h in range(hp):
            ya = (om_ref[h] * ra) * ga_ref[h] * _silu(za_ref[:, h * PAIR:(h + 1) * PAIR].astype(F32))
            ycat_ref[:, dc + h * PAIR:dc + (h + 1) * PAIR] = ya.astype(BF16)
            ycatt_ref[dc + h * PAIR:dc + (h + 1) * PAIR, :] = ya.T.astype(BF16)

    pair_spec = pl.BlockSpec((hp, tb, PAIR), lambda i: (0, i, 0))
    return pl.pallas_call(
        body, name="mix_fwd", grid=(t // tb,),
        out_shape=(jax.ShapeDtypeStruct((t, dc + da), BF16), jax.ShapeDtypeStruct((dc + da, t), BF16)),
        in_specs=[pl.BlockSpec((tb, dc), lambda i: (i, 0)),
                  pl.BlockSpec((tb, dc), lambda i: (i, 1)),
                  pl.BlockSpec((tb, dc), lambda i: (i, 3)),
                  pl.BlockSpec((tb, da), lambda i: (i, 7)),
                  pair_spec,
                  pl.BlockSpec((1, dc), lambda i: (0, 0)),
                  pl.BlockSpec((hp, 1, PAIR), lambda i: (0, 0, 0))],
        out_specs=(pl.BlockSpec((tb, dc + da), lambda i: (i, 0)), pl.BlockSpec((dc + da, tb), lambda i: (0, i))),
        compiler_params=_params(("parallel",)),
    )(co, proj, proj, proj, o_mix, g_conv, g_attn_pairs)


def _out_fwd_bwd(ycat, woutf, x, target, mod, g_post):
    t, d = x.shape
    n = ycat.shape[1]
    tb = 2 * ROW_TILE

    chunks = [slice(c * LANE_CHUNK, (c + 1) * LANE_CHUNK) for c in range(d // LANE_CHUNK)]

    def body(a_ref, w_ref, x_ref, tg_ref, mod_ref, g_ref, dout_ref, dy_ref, acc_ref, gg_ref, sums_ref, *y_refs):
        gg_ref[...] = mod_ref[2:3, :] * g_ref[...]

        @pl.when(pl.program_id(0) == 0)
        def _():
            sums_ref[...] = jnp.zeros(sums_ref.shape, F32)

        def group(y_ref, piece, g):
            mine = slice(g * ROW_GROUP, (g + 1) * ROW_GROUP)
            rows = slice(piece * ROW_PIECE + g * ROW_GROUP, piece * ROW_PIECE + (g + 1) * ROW_GROUP)
            ssq = jnp.zeros((ROW_GROUP, 1), F32)
            for cs in chunks:
                y = y_ref[mine, cs]
                ssq = ssq + jnp.sum(y * y, axis=-1, keepdims=True)
            r = lax.rsqrt(ssq * (1.0 / d) + EPS)
            dot_an = jnp.zeros((ROW_GROUP, 1), F32)
            for cs in chunks:
                nh = y_ref[mine, cs] * r
                err = x_ref[rows, cs] + nh * gg_ref[:, cs] - tg_ref[rows, cs]
                dout = err * (1.0 / d)
                dout_ref[rows, cs] = dout.astype(BF16)
                prod = dout * nh
                dot_an = dot_an + jnp.sum(prod * gg_ref[:, cs], axis=-1, keepdims=True)
                sums_ref[0, :, cs] += prod
                sums_ref[1, :, cs] += err * err
            mean_an = dot_an * (1.0 / d)
            for cs in chunks:
                nh = y_ref[mine, cs] * r
                dout = (x_ref[rows, cs] + nh * gg_ref[:, cs] - tg_ref[rows, cs]) * (1.0 / d)
                dy_ref[rows, cs] = (r * (dout * gg_ref[:, cs] - nh * mean_an)).astype(BF16)

        for piece in range(len(y_refs) + 1):
            if piece < len(y_refs):
                y_refs[piece][...] = jnp.dot(a_ref[piece * ROW_PIECE:(piece + 1) * ROW_PIECE, :], w_ref[...],
                                             preferred_element_type=F32)
            if piece > 0:
                for g in range(ROW_PIECE // ROW_GROUP):
                    group(y_refs[piece - 1], piece - 1, g)

        @pl.when(pl.program_id(0) == pl.num_programs(0) - 1)
        def _():
            prod_sum = jnp.sum(sums_ref[0], axis=0, keepdims=True)
            loss = (0.5 / d) * jnp.sum(jnp.sum(sums_ref[1], axis=0, keepdims=True), axis=-1, keepdims=True)
            acc_ref[...] = jnp.concatenate(
                [prod_sum * g_ref[...], prod_sum * mod_ref[2:3, :], jnp.broadcast_to(loss, (1, d)),
                 jnp.zeros((SUBLANES - 3, d), F32)], axis=0)

    return pl.pallas_call(
        body, name="out_fwd_bwd", grid=(t // tb,),
        out_shape=(jax.ShapeDtypeStruct((t, d), BF16), jax.ShapeDtypeStruct((t, d), BF16),
                   jax.ShapeDtypeStruct((SUBLANES, d), F32)),
        in_specs=[pl.BlockSpec((tb, n), lambda i: (i, 0)), pl.BlockSpec((n, d), lambda i: (0, 0)),
                  pl.BlockSpec((tb, d), lambda i: (i, 0)), pl.BlockSpec((tb, d), lambda i: (i, 0)),
                  pl.BlockSpec((3, d), lambda i: (0, 0)), pl.BlockSpec((1, d), lambda i: (0, 0))],
        out_specs=(pl.BlockSpec((tb, d), lambda i: (i, 0)), pl.BlockSpec((tb, d), lambda i: (i, 0)),
                   pl.BlockSpec((SUBLANES, d), lambda i: (0, 0))),
        scratch_shapes=[pltpu.VMEM((1, d), F32), pltpu.VMEM((2, ROW_GROUP, d), F32)]
        + [pltpu.VMEM((ROW_PIECE, d), F32)] * (tb // ROW_PIECE),
        compiler_params=_params(("arbitrary",)),
    )(ycat, woutf, x, target, mod, g_post)


def _matmul_nt(a, b, out_dtype, name):
    m, k = a.shape
    n = b.shape[0]
    tn = COL_TILE

    def body(a_ref, b_ref, o_ref):
        o_ref[...] = lax.dot_general(a_ref[...], b_ref[...], (((1,), (1,)), ((), ())),
                                     preferred_element_type=F32).astype(out_dtype)

    return pl.pallas_call(
        body, name=name, grid=(n // tn,),
        out_shape=jax.ShapeDtypeStruct((m, n), out_dtype),
        in_specs=[pl.BlockSpec((m, k), lambda i: (0, 0)), pl.BlockSpec((tn, k), lambda i: (i, 0))],
        out_specs=pl.BlockSpec((m, tn), lambda i: (0, i)),
        compiler_params=_params(("parallel",)),
    )(a, b)


def _mix_bwd(dycat, co, proj, o_mix, g_conv, g_attn_pairs):
    t, dc = co.shape
    hp = o_mix.shape[0]
    da = hp * PAIR
    tb = ROW_TILE

    def body(dy_ref, co_ref, bg_ref, zc_ref, za_ref, om_ref, gc_ref, ga_ref,
             dcp_ref, dco_ref, do_ref, dl_ref, dgc_ref, dga_ref):
        first = pl.program_id(0) == 0
        cov = co_ref[...].astype(F32)
        bg = bg_ref[...].astype(F32)
        zc = zc_ref[...].astype(F32)
        p = bg * cov
        rc = lax.rsqrt(jnp.mean(p * p, axis=-1, keepdims=True) + EPS)
        nh = p * rc
        dyc = dy_ref[:, 0:dc].astype(F32)
        dn = dyc * _silu(zc)
        a = dn * gc_ref[...]
        dp = rc * (a - nh * jnp.mean(a * nh, axis=-1, keepdims=True))
        dcp_ref[:, 0:dc] = jnp.zeros((tb, dc), BF16)
        dcp_ref[:, dc:2 * dc] = (dp * cov).astype(BF16)
        dcp_ref[:, 2 * dc:3 * dc] = jnp.zeros((tb, dc), BF16)
        dcp_ref[:, 3 * dc:4 * dc] = (dyc * nh * gc_ref[...] * _silu_grad(zc)).astype(BF16)
        dcp_ref[:, 4 * dc:4 * dc + 3 * da] = jnp.zeros((tb, 3 * da), BF16)
        dco_ref[...] = dp * bg

        @pl.when(first)
        def _():
            dgc_ref[...] = jnp.zeros(dgc_ref.shape, F32)
            dga_ref[...] = jnp.zeros(dga_ref.shape, F32)

        dgc_ref[...] += jnp.sum(dn * nh, axis=0, keepdims=True)

        ssq = jnp.zeros((tb, 1), F32)
        for h in range(hp):
            o = om_ref[h]
            ssq = ssq + jnp.sum(o * o, axis=-1, keepdims=True)
        ra = lax.rsqrt(ssq * (1.0 / da) + EPS)
        dot_an = jnp.zeros((tb, 1), F32)
        for h in range(hp):
            nha = om_ref[h] * ra
            za = za_ref[:, h * PAIR:(h + 1) * PAIR].astype(F32)
            dya = dy_ref[:, dc + h * PAIR:dc + (h + 1) * PAIR].astype(F32)
            dna = dya * _silu(za)
            dza = (dya * nha * ga_ref[h] * _silu_grad(za)).astype(BF16)
            dcp_ref[:, 4 * dc + 3 * da + h * PAIR:4 * dc + 3 * da + (h + 1) * PAIR] = dza
            dga_ref[h] += jnp.sum(dna * nha, axis=0, keepdims=True)
            dot_an = dot_an + jnp.sum(dna * ga_ref[h] * nha, axis=-1, keepdims=True)
        mean_an = dot_an * (1.0 / da)
        first_head = lax.broadcasted_iota(jnp.int32, (tb, PAIR), 1) < HEAD_DIM
        for h in range(hp):
            o = om_ref[h]
            nha = o * ra
            za = za_ref[:, h * PAIR:(h + 1) * PAIR].astype(F32)
            dya = dy_ref[:, dc + h * PAIR:dc + (h + 1) * PAIR].astype(F32)
            aa = dya * _silu(za) * ga_ref[h]
            d_o = ra * (aa - nha * mean_an)
            do_ref[h] = d_o.astype(BF16)
            prod = d_o * o
            both = jnp.sum(prod, axis=-1, keepdims=True)
            head0 = jnp.sum(jnp.where(first_head, prod, 0.0), axis=-1, keepdims=True)
            dl_ref[h] = jnp.where(first_head, head0, both - head0)

    pair_spec = pl.BlockSpec((hp, tb, PAIR), lambda i: (0, i, 0))
    return pl.pallas_call(
        body, name="mix_bwd", grid=(t // tb,),
        out_shape=(jax.ShapeDtypeStruct((t, 4 * dc + 4 * da), BF16), jax.ShapeDtypeStruct((t, dc), F32),
                   jax.ShapeDtypeStruct((hp, t, PAIR), BF16), jax.ShapeDtypeStruct((hp, t, PAIR), F32),
                   jax.ShapeDtypeStruct((1, dc), F32), jax.ShapeDtypeStruct((hp, 1, PAIR), F32)),
        in_specs=[pl.BlockSpec((tb, dc + da), lambda i: (i, 0)),
                  pl.BlockSpec((tb, dc), lambda i: (i, 0)),
                  pl.BlockSpec((tb, dc), lambda i: (i, 1)),
                  pl.BlockSpec((tb, dc), lambda i: (i, 3)),
                  pl.BlockSpec((tb, da), lambda i: (i, 7)),
                  pair_spec,
                  pl.BlockSpec((1, dc), lambda i: (0, 0)),
                  pl.BlockSpec((hp, 1, PAIR), lambda i: (0, 0, 0))],
        out_specs=(pl.BlockSpec((tb, 4 * dc + 4 * da), lambda i: (i, 0)), pl.BlockSpec((tb, dc), lambda i: (i, 0)),
                   pair_spec, pair_spec,
                   pl.BlockSpec((1, dc), lambda i: (0, 0)), pl.BlockSpec((hp, 1, PAIR), lambda i: (0, 0, 0))),
        compiler_params=_params(("arbitrary",)),
    )(dycat, co, proj, proj, proj, o_mix, g_conv, g_attn_pairs)


def _conv_bwd(dconv_proj, dco, conv_proj, conv_w, dc, after):
    t = dco.shape[0]
    ct = CONV_TILE
    nct = dc // ct

    def body(dcp_in_ref, dco_ref, u_ref, cg_ref, w_ref, after_ref, dcp_ref, acc_ref):
        del dcp_in_ref, after_ref
        which = pl.program_id(1)
        g = dco_ref[...]
        u = u_ref[...].astype(F32)
        cg = cg_ref[...].astype(F32)
        g_prev, g_next = _shift_rows(g, t)
        da = w_ref[0:1, :] * g_next + w_ref[1:2, :] * g + w_ref[2:3, :] * g_prev
        dcp_ref[...] = (da * jnp.where(which == 0, cg, u)).astype(BF16)
        a = cg * u
        a_prev, a_next = _shift_rows(a, t)
        acc_ref[...] = jnp.concatenate(
            [jnp.sum(g * a_prev, axis=0, keepdims=True), jnp.sum(g * a, axis=0, keepdims=True),
             jnp.sum(g * a_next, axis=0, keepdims=True), jnp.sum(g, axis=0, keepdims=True),
             jnp.zeros((4, ct), F32)], axis=0)

    return pl.pallas_call(
        body, name="conv_bwd", grid=(nct, 2),
        out_shape=(jax.ShapeDtypeStruct(dconv_proj.shape, BF16), jax.ShapeDtypeStruct((8, dc), F32)),
        in_specs=[HBM,
                  pl.BlockSpec((t, ct), lambda i, s: (0, i)),
                  pl.BlockSpec((t, ct), lambda i, s: (0, i)),
                  pl.BlockSpec((t, ct), lambda i, s: (0, 2 * nct + i)),
                  pl.BlockSpec((3, ct), lambda i, s: (0, i)), ANY],
        out_specs=(pl.BlockSpec((t, ct), lambda i, s: (0, 2 * s * nct + i)),
                   pl.BlockSpec((8, ct), lambda i, s: (0, i))),
        input_output_aliases={0: 0},
        compiler_params=_params(("arbitrary", "arbitrary")),
    )(dconv_proj, dco, conv_proj, conv_proj, conv_w, after)


def _dh(dproj, winf, after):
    t = dproj.shape[0]
    _, d, ws = winf.shape
    tm = tn = COL_TILE
    nt = (((1,), (1,)), ((), ()))

    def body(a_ref, w_ref, after_ref, o_ref):
        del after_ref
        acc = lax.dot_general(a_ref[:, 0:ws], w_ref[0], nt, preferred_element_type=F32)
        for j in range(1, N_CHIPS):
            acc = acc + lax.dot_general(a_ref[:, j * ws:(j + 1) * ws], w_ref[j], nt, preferred_element_type=F32)
        o_ref[...] = acc.astype(BF16)

    return pl.pallas_call(
        body, name="dh", grid=(d // tn, t // tm),
        out_shape=jax.ShapeDtypeStruct((t, d), BF16),
        in_specs=[pl.BlockSpec((tm, N_CHIPS * ws), lambda n, m: (m, 0)),
                  pl.BlockSpec((N_CHIPS, tn, ws), lambda n, m: (0, n, 0)), ANY],
        out_specs=pl.BlockSpec((tm, tn), lambda n, m: (m, n)),
        compiler_params=_params(("parallel", "parallel")),
    )(dproj, winf, after)


def _prenorm_bwd(x, dh, dout, mod, g_pre):
    t, d = x.shape
    tb = ROW_TILE

    chunks = [slice(c * LANE_CHUNK, (c + 1) * LANE_CHUNK) for c in range(d // LANE_CHUNK)]

    def body(x_ref, dh_ref, dout_ref, mod_ref, g_ref, gx_ref, acc_ref, sg_ref, sums_ref):
        sg_ref[...] = (1.0 + mod_ref[1:2, :]) * g_ref[...]

        @pl.when(pl.program_id(0) == 0)
        def _():
            sums_ref[...] = jnp.zeros(sums_ref.shape, F32)

        def group(i, carry):
            rows = pl.ds(pl.multiple_of(i * ROW_GROUP, ROW_GROUP), ROW_GROUP)
            ssq = jnp.zeros((ROW_GROUP, 1), F32)
            for cs in chunks:
                xv = x_ref[rows, cs]
                ssq = ssq + jnp.sum(xv * xv, axis=-1, keepdims=True)
            r = lax.rsqrt(ssq * (1.0 / d) + EPS)
            dot_ax = jnp.zeros((ROW_GROUP, 1), F32)
            for cs in chunks:
                xh = x_ref[rows, cs] * r
                dhv = dh_ref[rows, cs].astype(F32)
                prod = dhv * xh
                dot_ax = dot_ax + jnp.sum(prod * sg_ref[:, cs], axis=-1, keepdims=True)
                sums_ref[0, :, cs] += dhv
                sums_ref[1, :, cs] += prod
            mean_ax = dot_ax * (1.0 / d)
            for cs in chunks:
                xh = x_ref[rows, cs] * r
                a = dh_ref[rows, cs].astype(F32) * sg_ref[:, cs]
                gx_ref[rows, cs] = dout_ref[rows, cs].astype(F32) + r * (a - xh * mean_ax)
            return carry

        lax.fori_loop(0, tb // ROW_GROUP, group, 0, unroll=GROUP_UNROLL)

        @pl.when(pl.program_id(0) == pl.num_programs(0) - 1)
        def _():
            dh_sum = jnp.sum(sums_ref[0], axis=0, keepdims=True)
            prod_sum = jnp.sum(sums_ref[1], axis=0, keepdims=True)
            acc_ref[...] = jnp.concatenate(
                [dh_sum, prod_sum * g_ref[...], prod_sum * (1.0 + mod_ref[1:2, :]),
                 jnp.zeros((SUBLANES - 3, d), F32)], axis=0)

    return pl.pallas_call(
        body, name="prenorm_bwd", grid=(t // tb,),
        out_shape=(jax.ShapeDtypeStruct((t, d), F32), jax.ShapeDtypeStruct((SUBLANES, d), F32)),
        in_specs=[pl.BlockSpec((tb, d), lambda i: (i, 0)), pl.BlockSpec((tb, d), lambda i: (i, 0)),
                  pl.BlockSpec((tb, d), lambda i: (i, 0)), pl.BlockSpec((3, d), lambda i: (0, 0)),
                  pl.BlockSpec((1, d), lambda i: (0, 0))],
        out_specs=(pl.BlockSpec((tb, d), lambda i: (i, 0)), pl.BlockSpec((SUBLANES, d), lambda i: (0, 0))),
        scratch_shapes=[pltpu.VMEM((1, d), F32), pltpu.VMEM((2, ROW_GROUP, d), F32)],
        compiler_params=_params(("arbitrary",)),
    )(x, dh, dout, mod, g_pre)


def _chip_sums(mine, rsib, name, part=0, parts=1, after=()):
    _, half, cols = mine.shape
    rows = half // parts
    tr = min(rows, ROW_TILE)
    nt = rows // tr

    def body(g_ref, r_ref, *rest):
        rest[-1][...] = (g_ref[...].astype(F32) + r_ref[...].astype(F32)).astype(BF16)

    spec = pl.BlockSpec((None, tr, cols), lambda j, i: (j, part * nt + i, 0))
    return pl.pallas_call(
        body, name=name, grid=(N_CHIPS, nt),
        out_shape=jax.ShapeDtypeStruct((N_CHIPS, rows, cols), BF16),
        in_specs=[spec, spec] + [ANY] * len(after), out_specs=pl.BlockSpec((None, tr, cols), lambda j, i: (j, i, 0)),
        compiler_params=_params(("parallel", "parallel")),
    )(mine, rsib, *after)


def _owner_sum(place, mine, rsib, rici, name, part=0, parts=1):
    _, half, cols = mine.shape
    rows = half // parts
    tr = min(rows, ROW_TILE)
    nt = rows // tr

    def body(place_ref, g_ref, r_ref, i_ref, o_ref):
        del place_ref
        acc = g_ref[...].astype(F32) + r_ref[...].astype(F32)
        for k in range(N_CHIPS - 1):
            acc = acc + i_ref[k].astype(F32)
        o_ref[...] = acc

    own = pl.BlockSpec((None, tr, cols), lambda i, p: (p[0], part * nt + i, 0))
    grid_spec = pltpu.PrefetchScalarGridSpec(
        num_scalar_prefetch=1, grid=(nt,),
        in_specs=[own, own, pl.BlockSpec((N_CHIPS - 1, tr, cols), lambda i, p: (0, i, 0))],
        out_specs=pl.BlockSpec((tr, cols), lambda i, p: (p[1] * (half // tr) + part * nt + i, 0)))
    return pl.pallas_call(
        body, name=name, grid_spec=grid_spec,
        out_shape=jax.ShapeDtypeStruct((2 * half, cols), F32),
        compiler_params=_params(("parallel",)),
    )(place, mine, rsib, rici)


def _adam_math(w, g, m, v):
    m2 = ADAM_B1 * m + (1.0 - ADAM_B1) * g
    v2 = ADAM_B2 * v + (1.0 - ADAM_B2) * (g * g)
    m_hat = m2 / (1.0 - ADAM_B1 ** ADAM_STEP)
    v_hat = v2 / (1.0 - ADAM_B2 ** ADAM_STEP)
    delta = -ADAM_LR * (m_hat / (jnp.sqrt(v_hat) + ADAM_EPS) + ADAM_WD * w)
    return delta, m2, v2


def _adamw(w, g, m, v, name, part=0, parts=1, prev=None):
    rows, cols = w.shape
    tr = min(rows, ROW_TILE)

    def body(*refs):
        w_ref, g_ref, m_ref, v_ref, go_ref, d_ref, m2_ref, v2_ref = refs[-8:]
        g = g_ref[...]
        go_ref[...] = g
        d_ref[...], m2_ref[...], v2_ref[...] = _adam_math(w_ref[...], g, m_ref[...], v_ref[...])

    if parts == 1:
        grid, spec = (rows // tr,), pl.BlockSpec((tr, cols), lambda i: (i, 0))
    else:
        per_half = rows // 2 // tr
        nt = per_half // parts
        grid, spec = (2, nt), pl.BlockSpec((tr, cols), lambda r, i: (r * per_half + part * nt + i, 0))
    olds = [] if prev is None else list(prev)
    return pl.pallas_call(
        body, name=name, grid=grid,
        out_shape=(jax.ShapeDtypeStruct(w.shape, F32),) * 4,
        in_specs=[HBM] * len(olds) + [spec] * 4, out_specs=(spec,) * 4,
        input_output_aliases={i: i for i in range(len(olds))},
        compiler_params=_params(("parallel",) * len(grid)),
    )(*olds, w, g, m, v)


def _ada_grad_adamw(c_all_t, dmod_cols, w, m, v):
    d, wa = w.shape
    tr = ROW_TILE

    def body(ct_ref, dm_ref, w_ref, m_ref, v_ref, g_ref, d_ref, m2_ref, v2_ref):
        act = _silu(ct_ref[...])
        g = act[:, 0:1] * dm_ref[0:1, :]
        for b in range(1, N_DEV):
            g = g + act[:, b:b + 1] * dm_ref[b:b + 1, :]
        g_ref[...] = g
        d_ref[...], m2_ref[...], v2_ref[...] = _adam_math(w_ref[...], g, m_ref[...], v_ref[...])

    spec = pl.BlockSpec((tr, wa), lambda i: (i, 0))
    return pl.pallas_call(
        body, name="ada_grad_adamw", grid=(d // tr,),
        out_shape=(jax.ShapeDtypeStruct(w.shape, F32),) * 4,
        in_specs=[pl.BlockSpec((tr, N_DEV), lambda i: (i, 0)), pl.BlockSpec((N_DEV, wa), lambda i: (0, 0)),
                  spec, spec, spec],
        out_specs=(spec,) * 4,
        compiler_params=_params(("parallel",)),
    )(c_all_t, dmod_cols, w, m, v)


def _small_update(place, gathered, pieces, weights, moments_m, moments_v):
    n = gathered.shape[1]
    k = len(weights)

    def body(place_ref, g_ref, *refs):
        w_refs, m_refs, v_refs = refs[0:k], refs[k:2 * k], refs[2 * k:3 * k]
        outs = refs[3 * k:]
        total = g_ref[0:SUBLANES, :]
        for dev in range(1, N_DEV):
            total = total + g_ref[SUBLANES * dev:SUBLANES * (dev + 1), :]

        def flat(offset, length):
            segments, pos = [], offset
            while pos < offset + length:
                row, col = divmod(pos, n)
                take = min(offset + length - pos, n - col)
                segments.append(total[row:row + 1, col:col + take])
                pos += take
            return jnp.concatenate(segments, axis=1) if len(segments) > 1 else segments[0]

        chip = place_ref[0]
        for i, (w_ref, m_ref, v_ref) in enumerate(zip(w_refs, m_refs, v_refs)):
            g = flat(*pieces[i])
            if w_ref.ndim == 3:
                rows, cols = w_ref.shape[1:]
                full = pieces[i][1] // rows
                picked = []
                for r in range(rows):
                    blocks = [g[:, r * full + q * cols:r * full + (q + 1) * cols] for q in range(N_CHIPS)]
                    mine = blocks[N_CHIPS - 1]
                    for q in range(N_CHIPS - 2, -1, -1):
                        mine = jnp.where(chip == q, blocks[q], mine)
                    picked.append(mine)
                g = jnp.concatenate(picked, axis=0)
                w, m, v = w_ref[0], m_ref[0], v_ref[0]
            else:
                w, m, v = w_ref[...], m_ref[...], v_ref[...]
            delta, m2, v2 = _adam_math(w, g, m, v)
            for j, val in enumerate((g, delta, m2, v2)):
                out = outs[j * k + i]
                if w_ref.ndim == 3:
                    out[0] = val
                else:
                    out[...] = val
        outs[4 * k][...] = flat(*pieces[k])

    shapes = [jax.ShapeDtypeStruct(w.shape, F32) for w in weights]
    grid_spec = pltpu.PrefetchScalarGridSpec(
        num_scalar_prefetch=1, grid=(1,),
        in_specs=[pl.BlockSpec(gathered.shape, lambda i, p: (0, 0))]
        + [pl.BlockSpec(a.shape, functools.partial(lambda nd, i, p: (0,) * nd, a.ndim))
           for a in (*weights, *moments_m, *moments_v)],
        out_specs=tuple(pl.BlockSpec(s.shape, functools.partial(lambda nd, i, p: (0,) * nd, len(s.shape)))
                        for s in shapes * 4) + (pl.BlockSpec((1, LANES), lambda i, p: (0, 0)),))
    outs = pl.pallas_call(
        body, name="small_update", grid_spec=grid_spec,
        out_shape=tuple(shapes * 4) + (jax.ShapeDtypeStruct((1, LANES), F32),),
        compiler_params=_params(("arbitrary",)),
    )(place, gathered, *weights, *moments_m, *moments_v)
    return outs[0:k], outs[k:2 * k], outs[2 * k:3 * k], outs[3 * k:4 * k], outs[4 * k]


def _pack_small(pieces):
    flat = [p.reshape(-1).astype(F32) for p in pieces]
    offsets, total = [], 0
    for p in flat:
        offsets.append(total)
        total += p.shape[0]
    padded = -(-total // SMALL_ALIGN) * SMALL_ALIGN
    if padded > total:
        flat.append(jnp.zeros((padded - total,), F32))
    return jnp.concatenate(flat).reshape(8, padded // 8), offsets


def _alibi_slope_rows(n_heads):
    slopes = 2.0 ** (-8.0 * jnp.arange(1, n_heads + 1, dtype=F32) / n_heads)
    rows = jnp.zeros((n_heads // 2, 8), F32).at[:, 0:2].set(slopes.reshape(n_heads // 2, 2))
    return jnp.broadcast_to(rows[:, :, None], (n_heads // 2, 8, ATT_KW))


def kernel(x, c, w_ada, b_ada, g_pre, w_in, conv_w, conv_b, g_conv, g_attn, w_out, g_post, loss_target, m_w_ada, m_b_ada, m_g_pre, m_w_in, m_conv_w, m_conv_b, m_g_conv, m_g_attn, m_w_out, m_g_post, v_w_ada, v_b_ada, v_g_pre, v_w_in, v_conv_w, v_conv_b, v_g_conv, v_g_attn, v_w_out, v_g_post):
    t, d = x.shape[1], x.shape[2]
    dc = conv_b.shape[1]
    da = g_attn.shape[1]
    hp = da // PAIR
    ws = w_in.shape[2]
    wa = w_ada.shape[2]
    cws = conv_w.shape[2]
    assert t % ROW_TILE == 0 and d % ROW_TILE == 0 and dc % COL_TILE == 0 and da % COL_TILE == 0
    assert ws == 2 * dc and dc == da and t // BRANCHES[-1][1] >= ATT_BQ

    mx, my, mc = _my_place()
    chip = _chip_of(mx, my)
    dev = 2 * chip + mc
    place = jnp.stack([chip, mc]).astype(jnp.int32)

    x2, tgt2 = x[0], loss_target[0]
    w_ada2, w_in2, w_out2 = w_ada[0], w_in[0], w_out[0]

    win_slots = _cast_into_slot(place, w_in2, "cast_w_in")
    packed, offs = _pack_small([c[0], conv_w[0]])
    seen, mod = _ada_modulation(packed, w_ada2, b_ada, d, after=(win_slots,))
    seen = seen.reshape(N_DEV, -1)
    c_all = seen[:, offs[0]:offs[0] + d]
    conv_w_full = seen[0::2, offs[1]:offs[1] + 3 * cws].reshape(N_CHIPS, 3, cws).transpose(1, 0, 2).reshape(3, dc)

    win_flight, send_in, recv_in, started = _gather_start(win_slots, mod)

    y_chip, x_chip, d_chip = (_chip_of(mx, 1 - my), _chip_of(1 - mx, my), _chip_of(1 - mx, 1 - my))
    own_chunk, near_chunks, far_chunk = (jnp.stack(js).astype(jnp.int32) for js in ([chip], [y_chip, x_chip], [d_chip]))
    h, ht = _prenorm(x2, mod, g_pre, started)
    proj = _proj_chunks(None, h, w_in2, own_chunk, "proj_own")
    win_flight, wout_flight, relay_send_in, relay_recv_in, send_out, recv_out = _gather_relay_in(
        win_flight, _cast_into_slot(place, w_out2, "cast_w_out"), recv_in, proj)
    win_flight = _forward_halves(
        _gather_wait_direct(win_flight, send_in, recv_in, proj, "gather_wait_w_in_direct"), (0, 1), "forward_w_in_direct")
    proj = _proj_chunks(proj, h, win_flight, near_chunks, "proj_neighbours")
    winf = _forward_halves(
        _gather_wait_relayed(win_flight, relay_send_in, relay_recv_in, proj, "gather_wait_w_in_relayed"),
        (2,), "forward_w_in_relayed")
    proj = _proj_chunks(proj, h, winf, far_chunk, "proj_diagonal")
    slopes = _alibi_slope_rows(da // HEAD_DIM)
    co = _conv_fwd(proj, conv_w_full, conv_b, dc)
    wout_flight, relay_send_out, relay_recv_out = _gather_relay_out(wout_flight, recv_out, co)
    o_mix, lse = _attn_fwd(proj, slopes, dc, da)
    g_attn_pairs = g_attn.reshape(hp, 1, PAIR)
    ycat, ycat_t = _mix_fwd(co, proj, o_mix, g_conv, g_attn_pairs)
    wout_flight = _gather_wait_direct(wout_flight, send_out, recv_out, ycat, "gather_wait_w_out_direct")
    wout_flight = _gather_wait_relayed(wout_flight, relay_send_out, relay_recv_out, ycat, "gather_wait_w_out_relayed")
    woutf = _forward_halves(wout_flight, (0, 1, 2), "forward_w_out").reshape(dc + da, d)
    dout, dy, post_sums = _out_fwd_bwd(ycat, woutf, x2, tgt2, mod, g_post)

    gout, rsib_out = _dw_swapped(ycat_t, dy, N_CHIPS, 1, "dw_out")
    csum_out = _chip_sums(gout, rsib_out, "rs_chip_sum_out")
    ssem_out, rsem_out, csum_out, land_out, sent_out = _owners_start(csum_out, "rs_owners_start_out")
    dycat = _matmul_nt(dy, woutf, BF16, "dycat")
    dproj, dco, d_o, delta, dg_conv, dg_attn = _mix_bwd(dycat, co, proj, o_mix, g_conv, g_attn_pairs)
    dproj, conv_sums = _conv_bwd(dproj, dco, proj, conv_w_full, dc, sent_out)
    dproj = _attn_bwd(dproj, proj, d_o, lse, delta, slopes, dc, da, sent_out)
    gin, rsib_in = _dw_swapped(ht, dproj, 1, N_CHIPS, "dw_in")
    ssem_in0, rsem_in0, csum_in0, land_in0, sent_in0 = _owners_start(
        _chip_sums(gin, rsib_in, "rs_chip_sum_in0", 0, 2), "rs_owners_start_in0")
    ssem_in1, rsem_in1, csum_in1, land_in1, sent_in = _owners_start(
        _chip_sums(gin, rsib_in, "rs_chip_sum_in1", 1, 2, after=(sent_in0,)), "rs_owners_start_in1")
    dh = _dh(dproj, winf, sent_in)
    grad_x, pre_sums = _prenorm_bwd(x2, dh, dout, mod, g_pre)

    small, so = _pack_small([
        pre_sums[0], pre_sums[1], post_sums[0],
        pre_sums[2], conv_sums[0:3], conv_sums[3], dg_conv, dg_attn, post_sums[1], post_sums[2, 0:128]])
    ssem_small, rsem_small, small, land_small, sent_small = _allgather8_start(small, dev, "gather_small_start")

    rici_out = _owners_wait(ssem_out, rsem_out, csum_out, land_out, [grad_x, sent_small], "rs_owners_wait_out")
    grad_w_out = _join_halves(_owner_sum(place, gout, rsib_out, rici_out, "rs_owner_sum_out"), "rs_join_halves_out")
    grad_w_out, delta_w_out, new_m_w_out, new_v_w_out = _adamw(
        w_out2, grad_w_out, m_w_out[0], v_w_out[0], "adamw_w_out")

    rici_in = _owners_wait(ssem_in0, rsem_in0, csum_in0, land_in0, [delta_w_out], "rs_owners_wait_in0")
    full_in0, jsend0, jrecv0, joining0 = _join_start(
        _owner_sum(place, gin, rsib_in, rici_in, "rs_owner_sum_in0", 0, 2), "rs_join_start_in0", 0, 2)
    full_in0 = _join_wait(full_in0, jsend0, jrecv0, [joining0], "rs_join_wait_in0", 0, 2)
    updated_in = _adamw(w_in2, full_in0, m_w_in[0], v_w_in[0], "adamw_w_in0", 0, 2)
    rici_in = _owners_wait(ssem_in1, rsem_in1, csum_in1, land_in1, [updated_in[1]], "rs_owners_wait_in1")
    full_in1, jsend1, jrecv1, joining1 = _join_start(
        _owner_sum(place, gin, rsib_in, rici_in, "rs_owner_sum_in1", 1, 2), "rs_join_start_in1", 1, 2)

    small_seen = _allgather8_wait(ssem_small, rsem_small, small, land_small, [joining1], "gather_small_wait")
    small_w = [b_ada, g_pre, conv_w, conv_b, g_conv, g_attn, g_post]
    small_m = [m_b_ada, m_g_pre, m_conv_w, m_conv_b, m_g_conv, m_g_attn, m_g_post]
    small_v = [v_b_ada, v_g_pre, v_conv_w, v_conv_b, v_g_conv, v_g_attn, v_g_post]
    pieces = [(0, 3 * d), (so[3], d), (so[4], 3 * dc), (so[5], dc), (so[6], dc), (so[7], da), (so[8], d), (so[9], LANES)]
    g_small, d_small, m_small, v_small, loss_row = _small_update(place, small_seen, pieces, small_w, small_m, small_v)
    loss = loss_row[0, 0]
    grad_b_ada, grad_g_pre, grad_conv_w, grad_conv_b, grad_g_conv, grad_g_attn, grad_g_post = g_small
    dmod_cols = lax.dynamic_slice_in_dim(small_seen.reshape(N_DEV, -1), chip * wa, wa, axis=1)
    grad_w_ada, delta_w_ada, new_m_w_ada, new_v_w_ada = _ada_grad_adamw(c_all.T, dmod_cols, w_ada2, m_w_ada[0], v_w_ada[0])

    full_in1 = _join_wait(full_in1, jsend1, jrecv1, [delta_w_ada, d_small[0]], "rs_join_wait_in1", 1, 2)
    grad_w_in, delta_w_in, new_m_w_in, new_v_w_in = _adamw(
        w_in2, full_in1, m_w_in[0], v_w_in[0], "adamw_w_in1", 1, 2, updated_in)

    def lead(a):
        return a.reshape((1,) + a.shape)

    grads = [lead(grad_w_ada), grad_b_ada, grad_g_pre, lead(grad_w_in), grad_conv_w, grad_conv_b, grad_g_conv,
             grad_g_attn, lead(grad_w_out), grad_g_post]
    deltas = [lead(delta_w_ada), d_small[0], d_small[1], lead(delta_w_in), d_small[2], d_small[3], d_small[4],
              d_small[5], lead(delta_w_out), d_small[6]]
    new_ms = [lead(new_m_w_ada), m_small[0], m_small[1], lead(new_m_w_in), m_small[2], m_small[3], m_small[4],
              m_small[5], lead(new_m_w_out), m_small[6]]
    new_vs = [lead(new_v_w_ada), v_small[0], v_small[1], lead(new_v_w_in), v_small[2], v_small[3], v_small[4],
              v_small[5], lead(new_v_w_out), v_small[6]]
    return (loss, lead(grad_x), *grads, *deltas, *new_ms, *new_vs)
```

```python
import functools

import jax
import jax.numpy as jnp
from jax import lax
from jax.experimental import pallas as pl
from jax.experimental.pallas import tpu as pltpu

F32 = jnp.float32
BF16 = jnp.bfloat16
MESH = pl.DeviceIdType.MESH
HBM = pl.BlockSpec(memory_space=pltpu.HBM)
VMEM = pl.BlockSpec(memory_space=pltpu.VMEM)
ANY = pl.BlockSpec(memory_space=pl.ANY)
SEM = pl.BlockSpec(memory_space=pltpu.SEMAPHORE)
EFFECT = pltpu.SideEffectType.DATAFLOW_SIDE_EFFECTING
SUBLANES, LANES = 8, 128
TOKEN = jax.ShapeDtypeStruct((SUBLANES, LANES), jnp.float32)

HEAD_DIM = 64
PAIR = 2 * HEAD_DIM
assert PAIR == LANES
BRANCHES = ((128, 1), (512, 4), (2048, 16))
SIDE = 64
EPS = 1e-6
NEG_INF = -1e30
N_CHIPS = 4
N_DEV = 8

ADAM_LR = 0.001
ADAM_B1 = 0.9
ADAM_B2 = 0.999
ADAM_EPS = 1e-08
ADAM_WD = 0.01
ADAM_STEP = 10

VMEM_LIMIT_BYTES = 56 * 1024 * 1024
ROW_TILE = 256
COL_TILE = 512
CONV_TILE = 256
ATT_BQ = 128
ATT_KW = ATT_BQ + 2 * SIDE
ATT_UNROLL = 4
SMALL_ALIGN = SUBLANES * LANES


def _params(semantics=None):
    kw = {"vmem_limit_bytes": VMEM_LIMIT_BYTES}
    if semantics is not None:
        kw["dimension_semantics"] = semantics
    return pltpu.CompilerParams(**kw)


def _silu(z):
    return z * jax.nn.sigmoid(z)


def _silu_grad(z):
    s = jax.nn.sigmoid(z)
    return s * (1.0 + z * (1.0 - s))


def _my_place():
    return lax.axis_index("x"), lax.axis_index("y"), lax.axis_index("c")


def _flip(a, bit):
    return 1 - a if bit else a


def _chip_of(x, y):
    return 2 * x + y


def _allgather8_start(v, me, name):
    rows_per, n = v.shape
    land = lax.dynamic_update_slice(jnp.zeros((N_DEV * rows_per, n), v.dtype), v, (me * rows_per, 0))

    def body(v_ref, land_ref, send_sems, recv_sems, v_thru, land_thru, token_ref):
        del v_thru, land_thru
        x, y, c = _my_place()
        mine = land_ref.at[pl.ds(pl.multiple_of((4 * x + 2 * y + c) * rows_per, rows_per), rows_per), :]
        for k in range(1, N_DEV):
            peer = (_flip(x, k & 4), _flip(y, k & 2), _flip(c, k & 1))
            pltpu.make_async_remote_copy(
                src_ref=v_ref, dst_ref=mine, send_sem=send_sems.at[k - 1], recv_sem=recv_sems.at[k - 1],
                device_id=peer, device_id_type=MESH).start()
        token_ref[...] = jnp.zeros(token_ref.shape, F32)

    sems = pltpu.SemaphoreType.DMA((N_DEV - 1,))
    return pl.pallas_call(
        body, name=name,
        out_shape=(sems, sems, jax.ShapeDtypeStruct(v.shape, v.dtype), jax.ShapeDtypeStruct(land.shape, land.dtype), TOKEN),
        in_specs=[HBM, HBM], out_specs=(SEM, SEM, HBM, HBM, VMEM),
        input_output_aliases={0: 2, 1: 3},
        compiler_params=pltpu.CompilerParams(has_side_effects=EFFECT),
    )(pltpu.with_memory_space_constraint(v, pltpu.HBM), pltpu.with_memory_space_constraint(land, pltpu.HBM))


def _allgather8_wait(send_sems, recv_sems, v, land, after, name):
    rows_per = v.shape[0]

    def body(v_ref, land_ref, send_ref, recv_ref, *rest):
        del rest
        x, y, c = _my_place()
        for k in range(1, N_DEV):
            peer = (_flip(x, k & 4), _flip(y, k & 2), _flip(c, k & 1))
            src = 4 * peer[0] + 2 * peer[1] + peer[2]
            cp = pltpu.make_async_remote_copy(
                src_ref=v_ref, dst_ref=land_ref.at[pl.ds(pl.multiple_of(src * rows_per, rows_per), rows_per), :],
                send_sem=send_ref.at[k - 1], recv_sem=recv_ref.at[k - 1], device_id=peer, device_id_type=MESH)
            cp.wait_send()
            cp.wait_recv()

    return pl.pallas_call(
        body, name=name,
        out_shape=(jax.ShapeDtypeStruct(v.shape, v.dtype), jax.ShapeDtypeStruct(land.shape, land.dtype)),
        in_specs=[HBM, HBM, SEM, SEM] + [ANY] * len(after), out_specs=(HBM, HBM),
        input_output_aliases={0: 0, 1: 1},
        compiler_params=pltpu.CompilerParams(has_side_effects=EFFECT),
    )(v, land, send_sems, recv_sems, *after)[1]


def _half_rows(ref, chip, which, half):
    return ref.at[chip, pl.ds(pl.multiple_of(which * half, half), half), :]


def _ici_peers(x, y, c):
    peers = [(_flip(x, k & 2), _flip(y, k & 1), c) for k in (1, 2, 3)]
    return [(peer, _chip_of(peer[0], peer[1])) for peer in peers]


def _part_of_half(ref, chip, core, part):
    half, cols = ref.shape[1] // 2, ref.shape[2] // 2
    return ref.at[chip, pl.ds(pl.multiple_of(core * half, half), half), pl.ds(part * cols, cols)]


def _neighbours(x, y, c):
    return [((x, 1 - y, c), _chip_of(x, 1 - y)), ((1 - x, y, c), _chip_of(1 - x, y)),
            ((1 - x, 1 - y, c), _chip_of(1 - x, 1 - y))]


def _start_direct(buf, send_sems, recv_sems):
    x, y, c = _my_place()
    me = _chip_of(x, y)
    for n, (peer, _) in enumerate(_neighbours(x, y, c)[0:2]):
        for part in ((0, 1), (1, 0))[n]:
            piece = _part_of_half(buf, me, c, part)
            pltpu.make_async_remote_copy(
                src_ref=piece, dst_ref=piece, send_sem=send_sems.at[2 * n + part], recv_sem=recv_sems.at[2 * n + part],
                device_id=peer, device_id_type=MESH).start()


def _relay(buf, recv_sems, relay_send, relay_recv):
    x, y, c = _my_place()
    nbrs = _neighbours(x, y, c)
    for n in range(2):
        part = n
        piece = _part_of_half(buf, nbrs[n][1], c, part)
        pltpu.make_async_remote_copy(
            src_ref=piece, dst_ref=piece, send_sem=relay_send.at[part], recv_sem=recv_sems.at[2 * n + part],
            device_id=nbrs[n][0], device_id_type=MESH).wait_recv()
        pltpu.make_async_remote_copy(
            src_ref=piece, dst_ref=piece, send_sem=relay_send.at[part], recv_sem=relay_recv.at[part],
            device_id=nbrs[1 - n][0], device_id_type=MESH).start()


def _gather_start(win_slots, after):
    def body(win_in, after_ref, win_ref, send_sems, recv_sems, token_ref):
        del win_in, after_ref
        _start_direct(win_ref, send_sems, recv_sems)
        token_ref[...] = jnp.zeros(token_ref.shape, F32)

    sems = pltpu.SemaphoreType.DMA((4,))
    return pl.pallas_call(
        body, name="gather_start",
        out_shape=(jax.ShapeDtypeStruct(win_slots.shape, win_slots.dtype), sems, sems, TOKEN),
        in_specs=[HBM, ANY], out_specs=(HBM, SEM, SEM, VMEM),
        input_output_aliases={0: 0},
        compiler_params=pltpu.CompilerParams(has_side_effects=EFFECT),
    )(win_slots, after)


def _gather_relay_in(win, wout_slots, recv_in, after):
    def body(win_in, wout_in, recv_in_ref, after_ref, win_ref, wout_ref, relay_send, relay_recv, send_out, recv_out):
        del win_in, wout_in, after_ref
        _relay(win_ref, recv_in_ref, relay_send, relay_recv)
        _start_direct(wout_ref, send_out, recv_out)

    two, four = pltpu.SemaphoreType.DMA((2,)), pltpu.SemaphoreType.DMA((4,))
    return pl.pallas_call(
        body, name="gather_relay_w_in",
        out_shape=(jax.ShapeDtypeStruct(win.shape, win.dtype), jax.ShapeDtypeStruct(wout_slots.shape, wout_slots.dtype),
                   two, two, four, four),
        in_specs=[HBM, HBM, SEM, ANY], out_specs=(HBM, HBM, SEM, SEM, SEM, SEM),
        input_output_aliases={0: 0, 1: 1},
        compiler_params=pltpu.CompilerParams(has_side_effects=EFFECT),
    )(win, wout_slots, recv_in, after)


def _gather_relay_out(wout, recv_out, after):
    def body(wout_in, recv_out_ref, after_ref, wout_ref, relay_send, relay_recv):
        del wout_in, after_ref
        _relay(wout_ref, recv_out_ref, relay_send, relay_recv)

    two = pltpu.SemaphoreType.DMA((2,))
    return pl.pallas_call(
        body, name="gather_relay_w_out",
        out_shape=(jax.ShapeDtypeStruct(wout.shape, wout.dtype), two, two),
        in_specs=[HBM, SEM, ANY], out_specs=(HBM, SEM, SEM),
        input_output_aliases={0: 0},
        compiler_params=pltpu.CompilerParams(has_side_effects=EFFECT),
    )(wout, recv_out, after)


def _gather_wait_direct(buf, send_sems, recv_sems, after, name):
    def body(buf_in, send_ref, recv_ref, after_ref, buf_ref):
        del buf_in, after_ref
        x, y, c = _my_place()
        me = _chip_of(x, y)
        for n, (peer, chip) in enumerate(_neighbours(x, y, c)[0:2]):
            second = 1 - n
            pltpu.make_async_remote_copy(
                src_ref=_part_of_half(buf_ref, me, c, second), dst_ref=_part_of_half(buf_ref, chip, c, second),
                send_sem=send_ref.at[2 * n + second], recv_sem=recv_ref.at[2 * n + second],
                device_id=peer, device_id_type=MESH).wait_recv()
            for part in range(2):
                piece = _part_of_half(buf_ref, me, c, part)
                pltpu.make_async_remote_copy(
                    src_ref=piece, dst_ref=piece, send_sem=send_ref.at[2 * n + part], recv_sem=recv_ref.at[2 * n + part],
                    device_id=peer, device_id_type=MESH).wait_send()

    return pl.pallas_call(
        body, name=name,
        out_shape=jax.ShapeDtypeStruct(buf.shape, buf.dtype),
        in_specs=[HBM, SEM, SEM, ANY], out_specs=HBM,
        input_output_aliases={0: 0},
        compiler_params=pltpu.CompilerParams(has_side_effects=EFFECT),
    )(buf, send_sems, recv_sems, after)


def _gather_wait_relayed(buf, relay_send, relay_recv, after, name):
    def body(buf_in, rsend_ref, rrecv_ref, after_ref, buf_ref):
        del buf_in, after_ref
        x, y, c = _my_place()
        nbrs = _neighbours(x, y, c)
        for n in range(2):
            relayed = _part_of_half(buf_ref, nbrs[n][1], c, n)
            cp = pltpu.make_async_remote_copy(
                src_ref=relayed, dst_ref=_part_of_half(buf_ref, nbrs[2][1], c, n),
                send_sem=rsend_ref.at[n], recv_sem=rrecv_ref.at[n], device_id=nbrs[1 - n][0], device_id_type=MESH)
            cp.wait_recv()
            cp.wait_send()

    return pl.pallas_call(
        body, name=name,
        out_shape=jax.ShapeDtypeStruct(buf.shape, buf.dtype),
        in_specs=[HBM, SEM, SEM, ANY], out_specs=HBM,
        input_output_aliases={0: 0},
        compiler_params=pltpu.CompilerParams(has_side_effects=EFFECT),
    )(buf, relay_send, relay_recv, after)


def _forward_halves(buf, which, name):
    half = buf.shape[1] // 2

    def piece(buf_ref, chip, core, part):
        return _half_rows(buf_ref, chip, core, half) if part is None else _part_of_half(buf_ref, chip, core, part)

    def body(buf_in, buf_ref, send_sems, recv_sems):
        del buf_in
        x, y, c = _my_place()
        sibling = (x, y, 1 - c)
        pieces = [(_neighbours(x, y, c)[n][1], part) for n, part in which]
        started = []
        for k, (src_chip, part) in enumerate(pieces):
            landed = piece(buf_ref, src_chip, c, part)
            fw = pltpu.make_async_remote_copy(
                src_ref=landed, dst_ref=landed, send_sem=send_sems.at[k], recv_sem=recv_sems.at[k],
                device_id=sibling, device_id_type=MESH)
            fw.start()
            started.append(fw)
        for k, (src_chip, part) in enumerate(pieces):
            other = piece(buf_ref, src_chip, 1 - c, part)
            pltpu.make_async_remote_copy(
                src_ref=other, dst_ref=other, send_sem=send_sems.at[k], recv_sem=recv_sems.at[k],
                device_id=sibling, device_id_type=MESH).wait_recv()
        for fw in started:
            fw.wait_send()

    return pl.pallas_call(
        body, name=name,
        out_shape=jax.ShapeDtypeStruct(buf.shape, buf.dtype),
        in_specs=[HBM], out_specs=HBM,
        input_output_aliases={0: 0},
        scratch_shapes=[pltpu.SemaphoreType.DMA((len(which),))] * 2,
    )(buf)


def _dw_swapped(a, b, row_chunks, col_chunks, name):
    r, t = a.shape
    c_all = b.shape[1]
    chunks = row_chunks * col_chunks
    rq, cq = r // row_chunks, c_all // col_chunks
    half = rq // 2
    tn = COL_TILE
    nt = cq // tn
    steps = col_chunks * nt

    def body(a_ref, b_ref, mine_ref, sib_ref, stage, send_sems, recv_sems):
        x, y, c = _my_place()
        j, n = pl.program_id(0), pl.program_id(1)
        step = j * nt + n
        slot = step % 2
        res = jnp.dot(a_ref[...], b_ref[...], preferred_element_type=F32).astype(BF16)

        def landing(jj, nn):
            cols = pl.ds(pl.multiple_of(nn * tn, tn), tn)
            return sib_ref.at[:, :, cols] if col_chunks == 1 else sib_ref.at[pl.ds(jj, 1), :, cols]

        def copy(slot_, step_, jj, nn):
            return pltpu.make_async_remote_copy(
                src_ref=stage.at[slot_], dst_ref=landing(jj, nn), send_sem=send_sems.at[slot_],
                recv_sem=recv_sems.at[step_], device_id=(x, y, 1 - c), device_id_type=MESH)

        @pl.when(step >= 2)
        def _():
            copy(slot, step, j, n).wait_send()

        for q in range(row_chunks):
            lo = res[q * rq:q * rq + half, :]
            hi = res[q * rq + half:(q + 1) * rq, :]
            mine_ref[q] = jnp.where(c == 0, lo, hi)
            stage[slot, q] = jnp.where(c == 0, hi, lo)
        copy(slot, step, j, n).start()

        @pl.when(step == steps - 1)
        def _():
            for s in range(max(steps - 2, 0), steps):
                copy(s % 2, s, j, n).wait_send()
            for s in range(steps):
                copy(s % 2, s, j, n).wait_recv()

    shape = jax.ShapeDtypeStruct((chunks, half, cq), BF16)
    return pl.pallas_call(
        body, name=name, grid=(col_chunks, nt),
        out_shape=(shape, shape),
        in_specs=[pl.BlockSpec((r, t), lambda j, n: (0, 0)), pl.BlockSpec((t, tn), lambda j, n: (0, j * nt + n))],
        out_specs=(pl.BlockSpec((row_chunks, half, tn), lambda j, n: (j, 0, n)), ANY),
        scratch_shapes=[pltpu.VMEM((2, row_chunks, half, tn), BF16), pltpu.SemaphoreType.DMA((2,)),
                        pltpu.SemaphoreType.DMA((steps,))],
        compiler_params=_params(("arbitrary", "arbitrary")),
    )(a, b)


def _owners_start(csum, name, after=()):
    land = pltpu.with_memory_space_constraint(lax.empty((N_CHIPS - 1,) + csum.shape[1:], csum.dtype), pltpu.HBM)

    def body(csum_ref, land_ref, *rest):
        send_sems, recv_sems, _, _, token_ref = rest[len(after):]
        x, y, c = _my_place()
        for k, (peer, owner) in enumerate(_ici_peers(x, y, c)):
            pltpu.make_async_remote_copy(
                src_ref=csum_ref.at[owner], dst_ref=land_ref.at[k], send_sem=send_sems.at[k], recv_sem=recv_sems.at[k],
                device_id=peer, device_id_type=MESH).start()
        token_ref[...] = jnp.zeros(token_ref.shape, F32)

    sems = pltpu.SemaphoreType.DMA((N_CHIPS - 1,))
    return pl.pallas_call(
        body, name=name,
        out_shape=(sems, sems, jax.ShapeDtypeStruct(csum.shape, csum.dtype),
                   jax.ShapeDtypeStruct(land.shape, land.dtype), TOKEN),
        in_specs=[HBM, HBM] + [ANY] * len(after), out_specs=(SEM, SEM, HBM, HBM, VMEM),
        input_output_aliases={0: 2, 1: 3},
        compiler_params=pltpu.CompilerParams(has_side_effects=EFFECT),
    )(pltpu.with_memory_space_constraint(csum, pltpu.HBM), land, *after)


def _owners_wait(send_sems, recv_sems, csum, land, after, name):
    def body(csum_ref, land_ref, send_ref, recv_ref, *rest):
        del rest
        x, y, c = _my_place()
        for k, (peer, owner) in enumerate(_ici_peers(x, y, c)):
            cp = pltpu.make_async_remote_copy(
                src_ref=csum_ref.at[owner], dst_ref=land_ref.at[k], send_sem=send_ref.at[k], recv_sem=recv_ref.at[k],
                device_id=peer, device_id_type=MESH)
            cp.wait_send()
            cp.wait_recv()

    return pl.pallas_call(
        body, name=name,
        out_shape=(jax.ShapeDtypeStruct(csum.shape, csum.dtype), jax.ShapeDtypeStruct(land.shape, land.dtype)),
        in_specs=[HBM, HBM, SEM, SEM] + [ANY] * len(after), out_specs=(HBM, HBM),
        input_output_aliases={0: 0, 1: 1},
        compiler_params=pltpu.CompilerParams(has_side_effects=EFFECT),
    )(csum, land, send_sems, recv_sems, *after)[1]


def _join_halves(full, name):
    rows = full.shape[0] // 2

    def body(full_in, full_ref, send_sem, recv_sem):
        del full_in
        x, y, c = _my_place()
        sibling = (x, y, 1 - c)
        mine = full_ref.at[pl.ds(pl.multiple_of(c * rows, rows), rows), :]
        theirs = full_ref.at[pl.ds(pl.multiple_of((1 - c) * rows, rows), rows), :]
        cp = pltpu.make_async_remote_copy(
            src_ref=mine, dst_ref=mine, send_sem=send_sem, recv_sem=recv_sem, device_id=sibling, device_id_type=MESH)
        cp.start()
        pltpu.make_async_remote_copy(
            src_ref=theirs, dst_ref=theirs, send_sem=send_sem, recv_sem=recv_sem,
            device_id=sibling, device_id_type=MESH).wait_recv()
        cp.wait_send()

    return pl.pallas_call(
        body, name=name,
        out_shape=jax.ShapeDtypeStruct(full.shape, full.dtype),
        in_specs=[HBM], out_specs=HBM,
        input_output_aliases={0: 0},
        scratch_shapes=[pltpu.SemaphoreType.DMA, pltpu.SemaphoreType.DMA],
    )(full)


def _join_start(full, name, part, parts):
    half = full.shape[0] // 2
    rows = half // parts

    def body(full_in, full_ref, send_sem, recv_sem, token_ref):
        del full_in
        x, y, c = _my_place()
        mine = full_ref.at[pl.ds(pl.multiple_of(c * half + part * rows, rows), rows), :]
        pltpu.make_async_remote_copy(
            src_ref=mine, dst_ref=mine, send_sem=send_sem.at[0], recv_sem=recv_sem.at[0],
            device_id=(x, y, 1 - c), device_id_type=MESH).start()
        token_ref[...] = jnp.zeros(token_ref.shape, F32)

    one = pltpu.SemaphoreType.DMA((1,))
    return pl.pallas_call(
        body, name=name,
        out_shape=(jax.ShapeDtypeStruct(full.shape, full.dtype), one, one, TOKEN),
        in_specs=[HBM], out_specs=(HBM, SEM, SEM, VMEM),
        input_output_aliases={0: 0},
        compiler_params=pltpu.CompilerParams(has_side_effects=EFFECT),
    )(full)


def _join_wait(full, send_sem, recv_sem, after, name, part, parts):
    half = full.shape[0] // 2
    rows = half // parts

    def body(full_in, send_ref, recv_ref, *rest):
        del full_in
        full_ref = rest[-1]
        x, y, c = _my_place()
        cp = pltpu.make_async_remote_copy(
            src_ref=full_ref.at[pl.ds(pl.multiple_of(c * half + part * rows, rows), rows), :],
            dst_ref=full_ref.at[pl.ds(pl.multiple_of((1 - c) * half + part * rows, rows), rows), :],
            send_sem=send_ref.at[0], recv_sem=recv_ref.at[0], device_id=(x, y, 1 - c), device_id_type=MESH)
        cp.wait_send()
        cp.wait_recv()

    return pl.pallas_call(
        body, name=name,
        out_shape=jax.ShapeDtypeStruct(full.shape, full.dtype),
        in_specs=[HBM, SEM, SEM] + [ANY] * len(after), out_specs=HBM,
        input_output_aliases={0: 0},
        compiler_params=pltpu.CompilerParams(has_side_effects=EFFECT),
    )(full, send_sem, recv_sem, *after)


def _cast_into_slot(place, w, name):
    rows, cols = w.shape
    tr = min(rows, ROW_TILE)

    def body(place_ref, w_ref, o_ref):
        del place_ref
        o_ref[...] = w_ref[...].astype(BF16)

    grid_spec = pltpu.PrefetchScalarGridSpec(
        num_scalar_prefetch=1, grid=(rows // tr,),
        in_specs=[pl.BlockSpec((tr, cols), lambda i, p: (i, 0))],
        out_specs=pl.BlockSpec((None, tr, cols), lambda i, p: (p[0], i, 0)))
    return pl.pallas_call(
        body, name=name, grid_spec=grid_spec,
        out_shape=jax.ShapeDtypeStruct((N_CHIPS, rows, cols), BF16),
        compiler_params=_params(("parallel",)),
    )(place, w)


def _ada_modulation(packed, w_ada, b_ada, d, after=()):
    rows_per, n = packed.shape
    d_model, wa = w_ada.shape

    def body(v_ref, w_hbm, b_ref, *rest):
        all_ref, mod_ref, w_vmem, part_ref, parts_ref, load_sem, send1, recv1, send2, recv2 = rest[len(after):]
        x, y, c = _my_place()
        me = 4 * x + 2 * y + c
        chip = _chip_of(x, y)
        load = pltpu.make_async_copy(w_hbm, w_vmem, load_sem)
        load.start()

        def rows(idx):
            return all_ref.at[pl.ds(pl.multiple_of(idx * rows_per, rows_per), rows_per), :]

        all_ref[pl.ds(pl.multiple_of(me * rows_per, rows_per), rows_per), :] = v_ref[...]
        copies = []
        for k in range(1, N_DEV):
            peer = (_flip(x, k & 4), _flip(y, k & 2), _flip(c, k & 1))
            cp = pltpu.make_async_remote_copy(
                src_ref=v_ref, dst_ref=rows(me), send_sem=send1.at[k - 1], recv_sem=recv1.at[k - 1],
                device_id=peer, device_id_type=MESH)
            cp.start()
            copies.append((cp, peer))
        for k, (cp, peer) in enumerate(copies):
            pltpu.make_async_remote_copy(
                src_ref=v_ref, dst_ref=rows(4 * peer[0] + 2 * peer[1] + peer[2]), send_sem=send1.at[k],
                recv_sem=recv1.at[k], device_id=peer, device_id_type=MESH).wait_recv()
        for cp, _ in copies:
            cp.wait_send()

        def c_of(dev):
            segments, pos = [], 0
            while pos < d:
                row, col = divmod(pos, n)
                take = min(d - pos, n - col)
                segments.append(all_ref[dev * rows_per + row:dev * rows_per + row + 1, col:col + take])
                pos += take
            return jnp.concatenate(segments, axis=1)

        c_all = jnp.concatenate([c_of(dev) for dev in range(N_DEV)], axis=0)
        load.wait()
        part_ref[...] = jnp.dot(_silu(c_all), w_vmem[...], precision=lax.Precision.HIGHEST, preferred_element_type=F32)
        parts_ref[chip] = part_ref[...]
        swaps = []
        for k, (peer, _) in enumerate(_ici_peers(x, y, c)):
            cp = pltpu.make_async_remote_copy(
                src_ref=part_ref, dst_ref=parts_ref.at[chip], send_sem=send2.at[k], recv_sem=recv2.at[k],
                device_id=peer, device_id_type=MESH)
            cp.start()
            swaps.append(cp)
        for k, (peer, peer_chip) in enumerate(_ici_peers(x, y, c)):
            pltpu.make_async_remote_copy(
                src_ref=part_ref, dst_ref=parts_ref.at[peer_chip], send_sem=send2.at[k], recv_sem=recv2.at[k],
                device_id=peer, device_id_type=MESH).wait_recv()
        for cp in swaps:
            cp.wait_send()
        flat = jnp.concatenate([parts_ref[j, pl.ds(me, 1), :] for j in range(N_CHIPS)], axis=1) + b_ref[...]
        mod_ref[...] = jnp.concatenate([flat[:, i * d:(i + 1) * d] for i in range(3)], axis=0)

    return pl.pallas_call(
        body, name="ada_modulation",
        out_shape=(jax.ShapeDtypeStruct((N_DEV * rows_per, n), F32), jax.ShapeDtypeStruct((3, d), F32)),
        in_specs=[VMEM, ANY, VMEM] + [ANY] * len(after), out_specs=(VMEM, VMEM),
        scratch_shapes=[pltpu.VMEM((d_model, wa), F32), pltpu.VMEM((N_DEV, wa), F32),
                        pltpu.VMEM((N_CHIPS, N_DEV, wa), F32), pltpu.SemaphoreType.DMA,
                        pltpu.SemaphoreType.DMA((N_DEV - 1,)), pltpu.SemaphoreType.DMA((N_DEV - 1,)),
                        pltpu.SemaphoreType.DMA((N_CHIPS - 1,)), pltpu.SemaphoreType.DMA((N_CHIPS - 1,))],
        compiler_params=_params(),
    )(packed, w_ada, b_ada, *after)


def _prenorm(x, mod, g_pre, after):
    t, d = x.shape
    tb = ROW_TILE

    def body(x_ref, mod_ref, g_ref, after_ref, h_ref, ht_ref):
        del after_ref
        xv = x_ref[...]
        r = lax.rsqrt(jnp.mean(xv * xv, axis=-1, keepdims=True) + EPS)
        h = (xv * r) * g_ref[...] * (1.0 + mod_ref[1:2, :]) + mod_ref[0:1, :]
        h_ref[...] = h.astype(BF16)
        ht_ref[...] = h.T.astype(BF16)

    return pl.pallas_call(
        body, name="prenorm", grid=(t // tb,),
        out_shape=(jax.ShapeDtypeStruct((t, d), BF16), jax.ShapeDtypeStruct((d, t), BF16)),
        in_specs=[pl.BlockSpec((tb, d), lambda i: (i, 0)), pl.BlockSpec((3, d), lambda i: (0, 0)),
                  pl.BlockSpec((1, d), lambda i: (0, 0)), ANY],
        out_specs=(pl.BlockSpec((tb, d), lambda i: (i, 0)), pl.BlockSpec((d, tb), lambda i: (0, i))),
        compiler_params=_params(("parallel",)),
    )(x, mod, g_pre, after)


def _proj_tiles(proj, h, w, tiles, name):
    t, d = h.shape
    ws = w.shape[-1]
    tn = COL_TILE
    nt = ws // tn

    def body(tile_ref, *refs):
        del tile_ref
        a_ref, b_ref, o_ref = refs[-3:]
        o_ref[...] = jnp.dot(a_ref[...], b_ref[...].astype(BF16), preferred_element_type=F32).astype(BF16)

    if w.ndim == 3:
        w_spec = pl.BlockSpec((None, d, tn), lambda i, tl: (tl[i] // nt, 0, tl[i] % nt))
    else:
        w_spec = pl.BlockSpec((d, tn), lambda i, tl: (0, tl[i] % nt))
    first = proj is None
    grid_spec = pltpu.PrefetchScalarGridSpec(
        num_scalar_prefetch=1, grid=(tiles.shape[0],),
        in_specs=([] if first else [HBM]) + [pl.BlockSpec((t, d), lambda i, tl: (0, 0)), w_spec],
        out_specs=pl.BlockSpec((t, tn), lambda i, tl: (0, tl[i])))
    return pl.pallas_call(
        body, name=name, grid_spec=grid_spec,
        out_shape=jax.ShapeDtypeStruct((t, N_CHIPS * ws), BF16),
        input_output_aliases={} if first else {1: 0},
        compiler_params=_params(("parallel",)),
    )(*([tiles] if first else [tiles, proj]), h, w)


def _shift_rows(a, rows):
    idx = lax.broadcasted_iota(jnp.int32, a.shape, 0)
    prev = jnp.where(idx == 0, 0.0, pltpu.roll(a, 1, 0))
    nxt = jnp.where(idx == rows - 1, 0.0, pltpu.roll(a, rows - 1, 0))
    return prev, nxt


def _conv_fwd(conv_proj, conv_w, conv_b, dc):
    t = conv_proj.shape[0]
    ct = CONV_TILE
    nct = dc // ct

    def body(u_ref, cg_ref, w_ref, b_ref, co_ref):
        a = cg_ref[...].astype(F32) * u_ref[...].astype(F32)
        prev, nxt = _shift_rows(a, t)
        co_ref[...] = (w_ref[0:1, :] * prev + w_ref[1:2, :] * a + w_ref[2:3, :] * nxt + b_ref[...]).astype(BF16)

    return pl.pallas_call(
        body, name="conv_fwd", grid=(nct,),
        out_shape=jax.ShapeDtypeStruct((t, dc), BF16),
        in_specs=[pl.BlockSpec((t, ct), lambda i: (0, i)), pl.BlockSpec((t, ct), lambda i: (0, 2 * nct + i)),
                  pl.BlockSpec((3, ct), lambda i: (0, i)), pl.BlockSpec((1, ct), lambda i: (0, i))],
        out_specs=pl.BlockSpec((t, ct), lambda i: (0, i)),
        compiler_params=_params(("parallel",)),
    )(conv_proj, conv_proj, conv_w, conv_b)


def _to_residue_major(src_ref, dst_ref, r):
    seq = src_ref.shape[0] // r
    for res in range(r):
        dst_ref[res * seq:(res + 1) * seq, :] = src_ref[pl.ds(res, seq, stride=r), :].astype(dst_ref.dtype)


def _branch_operands(token_refs, stage, dil, r):
    if r == 1:
        return list(token_refs)
    for i, ref in enumerate(token_refs):
        stage[...] = ref[...].astype(F32)
        _to_residue_major(stage, dil.at[i], r)
    return [dil.at[i] for i in range(len(token_refs))]


def _scaled_queries(q):
    return (q.astype(F32) * (HEAD_DIM ** -0.5)).astype(BF16)


BLOCK_SHIFTS = (0, -SIDE, None)


def _band_bias(rel, slope):
    arel = jnp.abs(rel)
    return jnp.where(arel <= SIDE, arel.astype(F32) * slope, NEG_INF)


def _fill_bias_tiles(bias_ref, sl_ref, r, kw):
    base = lax.broadcasted_iota(jnp.int32, (ATT_BQ, kw), 1) - lax.broadcasted_iota(jnp.int32, (ATT_BQ, kw), 0)
    for hh in range(2):
        slope = -(sl_ref[hh:hh + 1, 0:kw] * float(r))
        for e, shift in enumerate(BLOCK_SHIFTS):
            shift = ATT_BQ - kw if shift is None else shift
            bias_ref[hh, e, :, 0:kw] = _band_bias(base + shift, slope)


def _fill_stacked_bias_tiles(bias_ref, sl_ref, r, kw):
    base = lax.broadcasted_iota(jnp.int32, (kw, ATT_BQ), 0) - lax.broadcasted_iota(jnp.int32, (kw, ATT_BQ), 1)
    for hh in range(2):
        slope = -(sl_ref[hh:hh + 1, 0:ATT_BQ] * float(r))
        for e, shift in enumerate(BLOCK_SHIFTS):
            shift = ATT_BQ - kw if shift is None else shift
            bias_ref[e, 0:kw, hh * ATT_BQ:(hh + 1) * ATT_BQ] = _band_bias(base + shift, slope)


def _first_head_lanes():
    return lax.broadcasted_iota(jnp.int32, (1, PAIR), 1) < HEAD_DIM


def _only_head(x, first, hh):
    return jnp.where(first if hh == 0 else jnp.logical_not(first), x, jnp.zeros_like(x))


def _block_place(g, seq_len, kw):
    nqb = seq_len // ATT_BQ
    if nqb == 1:
        row = pl.multiple_of(g * ATT_BQ, ATT_BQ)
        return row, row, 0
    res = g // nqb
    qb = g - res * nqb
    q0 = qb * ATT_BQ
    ks = jnp.clip(q0 - SIDE, 0, seq_len - kw)
    edge = jnp.where(qb == 0, 0, jnp.where(qb == nqb - 1, 2, 1))
    return (pl.multiple_of(res * seq_len + q0, ATT_BQ), pl.multiple_of(res * seq_len + ks, SIDE), edge)


def _qkv_specs(dc, da, t, index):
    return [pl.BlockSpec((t, PAIR), functools.partial(index, (4 * dc + comp * da) // PAIR)) for comp in range(3)]


def _attn_fwd(proj, slopes, dc, da):
    t = proj.shape[0]
    hp = da // PAIR
    n_blocks = t // ATT_BQ

    def body(q_ref, k_ref, v_ref, sl_ref, o_ref, lse_ref, stage, dil, bias, o_res, l_res, o_tok, l_tok):
        for b, (_, r) in enumerate(BRANCHES):
            seq_len = t // r
            kw = min(ATT_KW, seq_len)
            ops = _branch_operands([q_ref, k_ref, v_ref], stage, dil, r)
            _fill_bias_tiles(bias, sl_ref, r, kw)
            o_dst, l_dst = (o_tok.at[b], l_tok.at[b]) if r == 1 else (o_res, l_res)
            first = _first_head_lanes()

            def blocks(trip, carry, seq_len=seq_len, kw=kw, o_dst=o_dst, l_dst=l_dst, first=first, ops=ops):
                nt = (((1,), (1,)), ((), ()))
                places = [_block_place(trip * ATT_UNROLL + i, seq_len, kw) for i in range(ATT_UNROLL)]
                chains = [(i, hh) for i in range(ATT_UNROLL) for hh in range(2)]
                qs = [_scaled_queries(ops[0][pl.ds(qrow, ATT_BQ), :]) for qrow, _, _ in places]
                ks = [ops[1][pl.ds(krow, kw), :] for _, krow, _ in places]
                vs = [ops[2][pl.ds(krow, kw), :] for _, krow, _ in places]
                ss = [lax.dot_general(_only_head(qs[i], first, hh), ks[i], nt, preferred_element_type=F32)
                      + bias[hh, places[i][2], :, 0:kw] for i, hh in chains]
                tops = [jnp.max(s, axis=-1, keepdims=True) for s in ss]
                ps = [jnp.exp(s - m) for s, m in zip(ss, tops)]
                dens = [jnp.sum(p, axis=-1, keepdims=True) for p in ps]
                for i, (qrow, _, _) in enumerate(places):
                    weights = jnp.concatenate([ps[2 * i].astype(BF16), ps[2 * i + 1].astype(BF16)], axis=1)
                    values = jnp.concatenate([_only_head(vs[i], first, 0), _only_head(vs[i], first, 1)], axis=0)
                    den = jnp.where(first, dens[2 * i], dens[2 * i + 1])
                    o_dst[pl.ds(qrow, ATT_BQ), :] = jnp.dot(weights, values, preferred_element_type=F32) / den
                    l_dst[pl.ds(qrow, ATT_BQ), :] = jnp.where(first, tops[2 * i], tops[2 * i + 1]) + jnp.log(den)
                return carry

            lax.fori_loop(0, n_blocks // ATT_UNROLL, blocks, 0)
            if r > 1:
                for res in range(r):
                    rows = slice(res * seq_len, (res + 1) * seq_len)
                    o_tok[b, pl.ds(res, seq_len, stride=r), :] = o_res[rows, :]
                    l_tok[b, pl.ds(res, seq_len, stride=r), :] = l_res[rows, :]

        def merge(i, carry):
            rows = pl.ds(pl.multiple_of(i * ROW_TILE, ROW_TILE), ROW_TILE)
            la, lb, lc = l_tok[0, rows, :], l_tok[1, rows, :], l_tok[2, rows, :]
            m = jnp.maximum(jnp.maximum(la, lb), lc)
            wa, wb, wc = jnp.exp(la - m), jnp.exp(lb - m), jnp.exp(lc - m)
            den = wa + wb + wc
            o_ref[rows, :] = (wa * o_tok[0, rows, :] + wb * o_tok[1, rows, :] + wc * o_tok[2, rows, :]) * (1.0 / den)
            lse_ref[rows, :] = m + jnp.log(den)
            return carry

        lax.fori_loop(0, t // ROW_TILE, merge, 0)

    pair_spec = pl.BlockSpec((None, t, PAIR), lambda h: (h, 0, 0))
    return pl.pallas_call(
        body, name="attn_fwd", grid=(hp,),
        out_shape=(jax.ShapeDtypeStruct((hp, t, PAIR), F32), jax.ShapeDtypeStruct((hp, t, PAIR), F32)),
        in_specs=_qkv_specs(dc, da, t, lambda first, h: (0, first + h))
        + [pl.BlockSpec((None, 8, ATT_KW), lambda h: (h, 0, 0))],
        out_specs=(pair_spec, pair_spec),
        scratch_shapes=[pltpu.VMEM((t, PAIR), F32), pltpu.VMEM((3, t, PAIR), BF16),
                        pltpu.VMEM((2, 3, ATT_BQ, ATT_KW), F32),
                        pltpu.VMEM((t, PAIR), F32), pltpu.VMEM((t, PAIR), F32),
                        pltpu.VMEM((3, t, PAIR), F32), pltpu.VMEM((3, t, PAIR), F32)],
        compiler_params=_params(("parallel",)),
    )(proj, proj, proj, slopes)


def _attn_bwd(dproj, proj, d_o, lse, delta, slopes, dc, da, after):
    t = proj.shape[0]
    hp = da // PAIR
    n_blocks = t // ATT_BQ

    def all_branches(q_ref, k_ref, v_ref, do_ref, lse_ref, dl_ref, sl_ref,
                     stage, dil, packed, packed_res, row_vecs, bias_t, acc, tot):
        first = _first_head_lanes()
        lane = lax.broadcasted_iota(jnp.int32, (1, PAIR), 1)
        packed[...] = jnp.where((lane & (HEAD_DIM - 1)) < HEAD_DIM // 2, lse_ref[...], dl_ref[...])
        for b, (_, r) in enumerate(BRANCHES):
            seq_len = t // r
            kw = min(ATT_KW, seq_len)
            ops = _branch_operands([q_ref, k_ref, v_ref, do_ref], stage, dil, r)
            scalars = packed
            if r > 1:
                _to_residue_major(packed, packed_res, r)
                scalars = packed_res
            for g in range(n_blocks):
                flipped = scalars[g * ATT_BQ:(g + 1) * ATT_BQ, :].T
                for row in range(4):
                    row_vecs[g, row:row + 1, :] = flipped[row * (HEAD_DIM // 2):row * (HEAD_DIM // 2) + 1, :]
            _fill_stacked_bias_tiles(bias_t, sl_ref, r, kw)
            acc[1] = jnp.zeros((t, PAIR), F32)
            acc[2] = jnp.zeros((t, PAIR), F32)

            def blocks(trip, carry, seq_len=seq_len, kw=kw, ops=ops):
                nt = (((1,), (1,)), ((), ()))
                group = range(ATT_UNROLL)
                places = [_block_place(trip * ATT_UNROLL + i, seq_len, kw) for i in group]
                ks, vs, q2s, do2s, lse2s, dl2s = [], [], [], [], [], []
                for i, (qrow, krow, _) in zip(group, places):
                    q = _scaled_queries(ops[0][pl.ds(qrow, ATT_BQ), :])
                    dov = ops[3][pl.ds(qrow, ATT_BQ), :]
                    ks.append(ops[1][pl.ds(krow, kw), :])
                    vs.append(ops[2][pl.ds(krow, kw), :])
                    q2s.append(jnp.concatenate([_only_head(q, first, 0), _only_head(q, first, 1)], axis=0))
                    do2s.append(jnp.concatenate([_only_head(dov, first, 0), _only_head(dov, first, 1)], axis=0))
                    rows = row_vecs[trip * ATT_UNROLL + i]
                    lse2s.append(jnp.concatenate([rows[0:1, :], rows[2:3, :]], axis=1))
                    dl2s.append(jnp.concatenate([rows[1:2, :], rows[3:4, :]], axis=1))
                s_ts = [lax.dot_general(ks[i], q2s[i], nt, preferred_element_type=F32) for i in group]
                dp_ts = [lax.dot_general(vs[i], do2s[i], nt, preferred_element_type=F32) for i in group]
                p_ts = [jnp.exp(s_ts[i] + bias_t[places[i][2], 0:kw, :] - lse2s[i]) for i in group]
                ds_ts = [p_ts[i] * (dp_ts[i] - dl2s[i]) for i in group]
                dvs = [jnp.dot(p_ts[i].astype(BF16), do2s[i], preferred_element_type=F32) for i in group]
                dks = [jnp.dot(ds_ts[i].astype(BF16), q2s[i], preferred_element_type=F32) for i in group]
                dss = [ds_ts[i].T.astype(BF16) for i in group]
                dqs = [jnp.dot(dss[i][0:ATT_BQ, :], _only_head(ks[i], first, 0), preferred_element_type=F32)
                       + jnp.dot(dss[i][ATT_BQ:2 * ATT_BQ, :], _only_head(ks[i], first, 1), preferred_element_type=F32)
                       for i in group]
                for i, (qrow, krow, _) in zip(group, places):
                    acc[0, pl.ds(qrow, ATT_BQ), :] = dqs[i] * (HEAD_DIM ** -0.5)
                    acc[1, pl.ds(krow, kw), :] += dks[i]
                    acc[2, pl.ds(krow, kw), :] += dvs[i]
                return carry

            lax.fori_loop(0, n_blocks // ATT_UNROLL, blocks, 0)
            for comp in range(3):
                if r == 1:
                    tot[comp] = acc[comp]
                else:
                    for res in range(r):
                        tok = pl.ds(res, seq_len, stride=r)
                        tot[comp, tok, :] = tot[comp, tok, :] + acc[comp, res * seq_len:(res + 1) * seq_len, :]

    first_q = (4 * dc) // PAIR

    def body(dproj_in, q_ref, k_ref, v_ref, do_ref, lse_ref, dl_ref, sl_ref, after_ref, out_ref, *scratch):
        del dproj_in, after_ref
        work, out_stage, out_sems = scratch[:-2], scratch[-2], scratch[-1]
        h = pl.program_id(0)
        all_branches(q_ref, k_ref, v_ref, do_ref, lse_ref, dl_ref, sl_ref, *work)

        def out_copy(comp):
            cols = pl.ds(pl.multiple_of((first_q + comp * hp + h) * PAIR, PAIR), PAIR)
            return pltpu.make_async_copy(out_stage.at[comp], out_ref.at[:, cols], out_sems.at[comp])

        @pl.when(h > 0)
        def _():
            for comp in range(3):
                out_copy(comp).wait()

        for comp in range(3):
            out_stage[comp] = work[-1][comp].astype(BF16)
            out_copy(comp).start()

        @pl.when(h == hp - 1)
        def _():
            for comp in range(3):
                out_copy(comp).wait()

    pair_spec = pl.BlockSpec((None, t, PAIR), lambda h: (h, 0, 0))
    return pl.pallas_call(
        body, name="attn_bwd", grid=(hp,),
        out_shape=jax.ShapeDtypeStruct(dproj.shape, BF16),
        in_specs=[HBM] + _qkv_specs(dc, da, t, lambda first, h: (0, first + h))
        + [pair_spec, pair_spec, pair_spec, pl.BlockSpec((None, 8, ATT_KW), lambda h: (h, 0, 0)), ANY],
        out_specs=ANY,
        input_output_aliases={0: 0},
        scratch_shapes=[pltpu.VMEM((t, PAIR), F32), pltpu.VMEM((4, t, PAIR), BF16),
                        pltpu.VMEM((t, PAIR), F32), pltpu.VMEM((t, PAIR), F32),
                        pltpu.VMEM((n_blocks, 8, ATT_BQ), F32), pltpu.VMEM((3, ATT_KW, 2 * ATT_BQ), F32),
                        pltpu.VMEM((3, t, PAIR), F32), pltpu.VMEM((3, t, PAIR), F32),
                        pltpu.VMEM((3, t, PAIR), BF16), pltpu.SemaphoreType.DMA((3,))],
        compiler_params=_params(("arbitrary",)),
    )(dproj, proj, proj, proj, d_o, lse, delta, slopes, after)


def _mix_fwd(co, proj, o_mix, g_conv, g_attn_pairs):
    t, dc = co.shape
    hp = o_mix.shape[0]
    da = hp * PAIR
    tb = ROW_TILE

    def body(co_ref, bg_ref, zc_ref, za_ref, om_ref, gc_ref, ga_ref, ycat_ref, ycatt_ref):
        p = bg_ref[...].astype(F32) * co_ref[...].astype(F32)
        rc = lax.rsqrt(jnp.mean(p * p, axis=-1, keepdims=True) + EPS)
        yc = (p * rc) * gc_ref[...] * _silu(zc_ref[...].astype(F32))
        ycat_ref[:, 0:dc] = yc.astype(BF16)
        ycatt_ref[0:dc, :] = yc.T.astype(BF16)
        ssq = jnp.zeros((tb, 1), F32)
        for h in range(hp):
            o = om_ref[h]
            ssq = ssq + jnp.sum(o * o, axis=-1, keepdims=True)
        ra = lax.rsqrt(ssq * (1.0 / da) + EPS)
        for h in range(hp):
            ya = (om_ref[h] * ra) * ga_ref[h] * _silu(za_ref[:, h * PAIR:(h + 1) * PAIR].astype(F32))
            ycat_ref[:, dc + h * PAIR:dc + (h + 1) * PAIR] = ya.astype(BF16)
            ycatt_ref[dc + h * PAIR:dc + (h + 1) * PAIR, :] = ya.T.astype(BF16)

    pair_spec = pl.BlockSpec((hp, tb, PAIR), lambda i: (0, i, 0))
    return pl.pallas_call(
        body, name="mix_fwd", grid=(t // tb,),
        out_shape=(jax.ShapeDtypeStruct((t, dc + da), BF16), jax.ShapeDtypeStruct((dc + da, t), BF16)),
        in_specs=[pl.BlockSpec((tb, dc), lambda i: (i, 0)),
                  pl.BlockSpec((tb, dc), lambda i: (i, 1)),
                  pl.BlockSpec((tb, dc), lambda i: (i, 3)),
                  pl.BlockSpec((tb, da), lambda i: (i, 7)),
                  pair_spec,
                  pl.BlockSpec((1, dc), lambda i: (0, 0)),
                  pl.BlockSpec((hp, 1, PAIR), lambda i: (0, 0, 0))],
        out_specs=(pl.BlockSpec((tb, dc + da), lambda i: (i, 0)), pl.BlockSpec((dc + da, tb), lambda i: (0, i))),
        compiler_params=_params(("parallel",)),
    )(co, proj, proj, proj, o_mix, g_conv, g_attn_pairs)


def _out_fwd_bwd(ycat, woutf, x, target, mod, g_post):
    t, d = x.shape
    n = ycat.shape[1]
    tb = ROW_TILE

    def body(a_ref, w_ref, x_ref, tg_ref, mod_ref, g_ref, dout_ref, dy_ref, acc_ref):
        y = jnp.dot(a_ref[...], w_ref[...], preferred_element_type=F32)
        r = lax.rsqrt(jnp.mean(y * y, axis=-1, keepdims=True) + EPS)
        nh = y * r
        gate = mod_ref[2:3, :]
        nrm = nh * g_ref[...]
        err = x_ref[...] + gate * nrm - tg_ref[...]
        dout = err * (1.0 / d)
        dout_ref[...] = dout.astype(BF16)
        dn = dout * gate
        a = dn * g_ref[...]
        dy = r * (a - nh * jnp.mean(a * nh, axis=-1, keepdims=True))
        dy_ref[...] = dy.astype(BF16)
        loss = 0.5 * jnp.sum(jnp.sum(err * err, axis=-1, keepdims=True) * (1.0 / d), axis=0, keepdims=True)
        part = jnp.concatenate(
            [jnp.sum(dout * nrm, axis=0, keepdims=True), jnp.sum(dn * nh, axis=0, keepdims=True),
             jnp.broadcast_to(loss, (1, d)), jnp.zeros((5, d), F32)], axis=0)

        @pl.when(pl.program_id(0) == 0)
        def _():
            acc_ref[...] = jnp.zeros(acc_ref.shape, F32)

        acc_ref[...] += part

    return pl.pallas_call(
        body, name="out_fwd_bwd", grid=(t // tb,),
        out_shape=(jax.ShapeDtypeStruct((t, d), BF16), jax.ShapeDtypeStruct((t, d), BF16),
                   jax.ShapeDtypeStruct((8, d), F32)),
        in_specs=[pl.BlockSpec((tb, n), lambda i: (i, 0)), pl.BlockSpec((n, d), lambda i: (0, 0)),
                  pl.BlockSpec((tb, d), lambda i: (i, 0)), pl.BlockSpec((tb, d), lambda i: (i, 0)),
                  pl.BlockSpec((3, d), lambda i: (0, 0)), pl.BlockSpec((1, d), lambda i: (0, 0))],
        out_specs=(pl.BlockSpec((tb, d), lambda i: (i, 0)), pl.BlockSpec((tb, d), lambda i: (i, 0)),
                   pl.BlockSpec((8, d), lambda i: (0, 0))),
        compiler_params=_params(("arbitrary",)),
    )(ycat, woutf, x, target, mod, g_post)


def _matmul_nt(a, b, out_dtype, name):
    m, k = a.shape
    n = b.shape[0]
    tn = COL_TILE

    def body(a_ref, b_ref, o_ref):
        o_ref[...] = lax.dot_general(a_ref[...], b_ref[...], (((1,), (1,)), ((), ())),
                                     preferred_element_type=F32).astype(out_dtype)

    return pl.pallas_call(
        body, name=name, grid=(n // tn,),
        out_shape=jax.ShapeDtypeStruct((m, n), out_dtype),
        in_specs=[pl.BlockSpec((m, k), lambda i: (0, 0)), pl.BlockSpec((tn, k), lambda i: (i, 0))],
        out_specs=pl.BlockSpec((m, tn), lambda i: (0, i)),
        compiler_params=_params(("parallel",)),
    )(a, b)


def _mix_bwd(dycat, co, proj, o_mix, g_conv, g_attn_pairs):
    t, dc = co.shape
    hp = o_mix.shape[0]
    da = hp * PAIR
    tb = ROW_TILE

    def body(dy_ref, co_ref, bg_ref, zc_ref, za_ref, om_ref, gc_ref, ga_ref,
             dcp_ref, dco_ref, do_ref, dl_ref, dgc_ref, dga_ref):
        first = pl.program_id(0) == 0
        cov = co_ref[...].astype(F32)
        bg = bg_ref[...].astype(F32)
        zc = zc_ref[...].astype(F32)
        p = bg * cov
        rc = lax.rsqrt(jnp.mean(p * p, axis=-1, keepdims=True) + EPS)
        nh = p * rc
        dyc = dy_ref[:, 0:dc].astype(F32)
        dn = dyc * _silu(zc)
        a = dn * gc_ref[...]
        dp = rc * (a - nh * jnp.mean(a * nh, axis=-1, keepdims=True))
        dcp_ref[:, 0:dc] = jnp.zeros((tb, dc), BF16)
        dcp_ref[:, dc:2 * dc] = (dp * cov).astype(BF16)
        dcp_ref[:, 2 * dc:3 * dc] = jnp.zeros((tb, dc), BF16)
        dcp_ref[:, 3 * dc:4 * dc] = (dyc * nh * gc_ref[...] * _silu_grad(zc)).astype(BF16)
        dcp_ref[:, 4 * dc:4 * dc + 3 * da] = jnp.zeros((tb, 3 * da), BF16)
        dco_ref[...] = dp * bg

        @pl.when(first)
        def _():
            dgc_ref[...] = jnp.zeros(dgc_ref.shape, F32)
            dga_ref[...] = jnp.zeros(dga_ref.shape, F32)

        dgc_ref[...] += jnp.sum(dn * nh, axis=0, keepdims=True)

        ssq = jnp.zeros((tb, 1), F32)
        for h in range(hp):
            o = om_ref[h]
            ssq = ssq + jnp.sum(o * o, axis=-1, keepdims=True)
        ra = lax.rsqrt(ssq * (1.0 / da) + EPS)
        dot_an = jnp.zeros((tb, 1), F32)
        for h in range(hp):
            nha = om_ref[h] * ra
            za = za_ref[:, h * PAIR:(h + 1) * PAIR].astype(F32)
            dya = dy_ref[:, dc + h * PAIR:dc + (h + 1) * PAIR].astype(F32)
            dna = dya * _silu(za)
            dza = (dya * nha * ga_ref[h] * _silu_grad(za)).astype(BF16)
            dcp_ref[:, 4 * dc + 3 * da + h * PAIR:4 * dc + 3 * da + (h + 1) * PAIR] = dza
            dga_ref[h] += jnp.sum(dna * nha, axis=0, keepdims=True)
            dot_an = dot_an + jnp.sum(dna * ga_ref[h] * nha, axis=-1, keepdims=True)
        mean_an = dot_an * (1.0 / da)
        first_head = lax.broadcasted_iota(jnp.int32, (tb, PAIR), 1) < HEAD_DIM
        for h in range(hp):
            o = om_ref[h]
            nha = o * ra
            za = za_ref[:, h * PAIR:(h + 1) * PAIR].astype(F32)
            dya = dy_ref[:, dc + h * PAIR:dc + (h + 1) * PAIR].astype(F32)
            aa = dya * _silu(za) * ga_ref[h]
            d_o = ra * (aa - nha * mean_an)
            do_ref[h] = d_o.astype(BF16)
            prod = d_o * o
            both = jnp.sum(prod, axis=-1, keepdims=True)
            head0 = jnp.sum(jnp.where(first_head, prod, 0.0), axis=-1, keepdims=True)
            dl_ref[h] = jnp.where(first_head, head0, both - head0)

    pair_spec = pl.BlockSpec((hp, tb, PAIR), lambda i: (0, i, 0))
    return pl.pallas_call(
        body, name="mix_bwd", grid=(t // tb,),
        out_shape=(jax.ShapeDtypeStruct((t, 4 * dc + 4 * da), BF16), jax.ShapeDtypeStruct((t, dc), F32),
                   jax.ShapeDtypeStruct((hp, t, PAIR), BF16), jax.ShapeDtypeStruct((hp, t, PAIR), F32),
                   jax.ShapeDtypeStruct((1, dc), F32), jax.ShapeDtypeStruct((hp, 1, PAIR), F32)),
        in_specs=[pl.BlockSpec((tb, dc + da), lambda i: (i, 0)),
                  pl.BlockSpec((tb, dc), lambda i: (i, 0)),
                  pl.BlockSpec((tb, dc), lambda i: (i, 1)),
                  pl.BlockSpec((tb, dc), lambda i: (i, 3)),
                  pl.BlockSpec((tb, da), lambda i: (i, 7)),
                  pair_spec,
                  pl.BlockSpec((1, dc), lambda i: (0, 0)),
                  pl.BlockSpec((hp, 1, PAIR), lambda i: (0, 0, 0))],
        out_specs=(pl.BlockSpec((tb, 4 * dc + 4 * da), lambda i: (i, 0)), pl.BlockSpec((tb, dc), lambda i: (i, 0)),
                   pair_spec, pair_spec,
                   pl.BlockSpec((1, dc), lambda i: (0, 0)), pl.BlockSpec((hp, 1, PAIR), lambda i: (0, 0, 0))),
        compiler_params=_params(("arbitrary",)),
    )(dycat, co, proj, proj, proj, o_mix, g_conv, g_attn_pairs)


def _conv_bwd(dconv_proj, dco, conv_proj, conv_w, dc, after):
    t = dco.shape[0]
    ct = CONV_TILE
    nct = dc // ct

    def body(dcp_in_ref, dco_ref, u_ref, cg_ref, w_ref, after_ref, dcp_ref, acc_ref):
        del dcp_in_ref, after_ref
        which = pl.program_id(1)
        g = dco_ref[...]
        u = u_ref[...].astype(F32)
        cg = cg_ref[...].astype(F32)
        g_prev, g_next = _shift_rows(g, t)
        da = w_ref[0:1, :] * g_next + w_ref[1:2, :] * g + w_ref[2:3, :] * g_prev
        dcp_ref[...] = (da * jnp.where(which == 0, cg, u)).astype(BF16)
        a = cg * u
        a_prev, a_next = _shift_rows(a, t)
        acc_ref[...] = jnp.concatenate(
            [jnp.sum(g * a_prev, axis=0, keepdims=True), jnp.sum(g * a, axis=0, keepdims=True),
             jnp.sum(g * a_next, axis=0, keepdims=True), jnp.sum(g, axis=0, keepdims=True),
             jnp.zeros((4, ct), F32)], axis=0)

    return pl.pallas_call(
        body, name="conv_bwd", grid=(nct, 2),
        out_shape=(jax.ShapeDtypeStruct(dconv_proj.shape, BF16), jax.ShapeDtypeStruct((8, dc), F32)),
        in_specs=[HBM,
                  pl.BlockSpec((t, ct), lambda i, s: (0, i)),
                  pl.BlockSpec((t, ct), lambda i, s: (0, i)),
                  pl.BlockSpec((t, ct), lambda i, s: (0, 2 * nct + i)),
                  pl.BlockSpec((3, ct), lambda i, s: (0, i)), ANY],
        out_specs=(pl.BlockSpec((t, ct), lambda i, s: (0, 2 * s * nct + i)),
                   pl.BlockSpec((8, ct), lambda i, s: (0, i))),
        input_output_aliases={0: 0},
        compiler_params=_params(("arbitrary", "arbitrary")),
    )(dconv_proj, dco, conv_proj, conv_proj, conv_w, after)


def _dh(dproj, winf, after):
    t = dproj.shape[0]
    _, d, ws = winf.shape
    tm = tn = COL_TILE
    nt = (((1,), (1,)), ((), ()))

    def body(a_ref, w_ref, after_ref, o_ref):
        del after_ref
        acc = lax.dot_general(a_ref[:, 0:ws], w_ref[0], nt, preferred_element_type=F32)
        for j in range(1, N_CHIPS):
            acc = acc + lax.dot_general(a_ref[:, j * ws:(j + 1) * ws], w_ref[j], nt, preferred_element_type=F32)
        o_ref[...] = acc.astype(BF16)

    return pl.pallas_call(
        body, name="dh", grid=(d // tn, t // tm),
        out_shape=jax.ShapeDtypeStruct((t, d), BF16),
        in_specs=[pl.BlockSpec((tm, N_CHIPS * ws), lambda n, m: (m, 0)),
                  pl.BlockSpec((N_CHIPS, tn, ws), lambda n, m: (0, n, 0)), ANY],
        out_specs=pl.BlockSpec((tm, tn), lambda n, m: (m, n)),
        compiler_params=_params(("parallel", "parallel")),
    )(dproj, winf, after)


def _prenorm_bwd(x, dh, dout, mod, g_pre):
    t, d = x.shape
    tb = ROW_TILE

    def body(x_ref, dh_ref, dout_ref, mod_ref, g_ref, gx_ref, acc_ref):
        xv = x_ref[...]
        dhv = dh_ref[...].astype(F32)
        r = lax.rsqrt(jnp.mean(xv * xv, axis=-1, keepdims=True) + EPS)
        xh = xv * r
        one_scale = 1.0 + mod_ref[1:2, :]
        a = dhv * one_scale * g_ref[...]
        gx_ref[...] = dout_ref[...].astype(F32) + r * (a - xh * jnp.mean(a * xh, axis=-1, keepdims=True))
        part = jnp.concatenate(
            [jnp.sum(dhv, axis=0, keepdims=True), jnp.sum(dhv * xh * g_ref[...], axis=0, keepdims=True),
             jnp.sum(dhv * xh * one_scale, axis=0, keepdims=True), jnp.zeros((5, d), F32)], axis=0)

        @pl.when(pl.program_id(0) == 0)
        def _():
            acc_ref[...] = jnp.zeros(acc_ref.shape, F32)

        acc_ref[...] += part

    return pl.pallas_call(
        body, name="prenorm_bwd", grid=(t // tb,),
        out_shape=(jax.ShapeDtypeStruct((t, d), F32), jax.ShapeDtypeStruct((8, d), F32)),
        in_specs=[pl.BlockSpec((tb, d), lambda i: (i, 0)), pl.BlockSpec((tb, d), lambda i: (i, 0)),
                  pl.BlockSpec((tb, d), lambda i: (i, 0)), pl.BlockSpec((3, d), lambda i: (0, 0)),
                  pl.BlockSpec((1, d), lambda i: (0, 0))],
        out_specs=(pl.BlockSpec((tb, d), lambda i: (i, 0)), pl.BlockSpec((8, d), lambda i: (0, 0))),
        compiler_params=_params(("arbitrary",)),
    )(x, dh, dout, mod, g_pre)


def _chip_sums(mine, rsib, name, part=0, parts=1, after=()):
    _, half, cols = mine.shape
    rows = half // parts
    tr = min(rows, ROW_TILE)
    nt = rows // tr

    def body(g_ref, r_ref, *rest):
        rest[-1][...] = (g_ref[...].astype(F32) + r_ref[...].astype(F32)).astype(BF16)

    spec = pl.BlockSpec((None, tr, cols), lambda j, i: (j, part * nt + i, 0))
    return pl.pallas_call(
        body, name=name, grid=(N_CHIPS, nt),
        out_shape=jax.ShapeDtypeStruct((N_CHIPS, rows, cols), BF16),
        in_specs=[spec, spec] + [ANY] * len(after), out_specs=pl.BlockSpec((None, tr, cols), lambda j, i: (j, i, 0)),
        compiler_params=_params(("parallel", "parallel")),
    )(mine, rsib, *after)


def _owner_sum(place, mine, rsib, rici, name, part=0, parts=1):
    _, half, cols = mine.shape
    rows = half // parts
    tr = min(rows, ROW_TILE)
    nt = rows // tr

    def body(place_ref, g_ref, r_ref, i_ref, o_ref):
        del place_ref
        acc = g_ref[...].astype(F32) + r_ref[...].astype(F32)
        for k in range(N_CHIPS - 1):
            acc = acc + i_ref[k].astype(F32)
        o_ref[...] = acc

    own = pl.BlockSpec((None, tr, cols), lambda i, p: (p[0], part * nt + i, 0))
    grid_spec = pltpu.PrefetchScalarGridSpec(
        num_scalar_prefetch=1, grid=(nt,),
        in_specs=[own, own, pl.BlockSpec((N_CHIPS - 1, tr, cols), lambda i, p: (0, i, 0))],
        out_specs=pl.BlockSpec((tr, cols), lambda i, p: (p[1] * (half // tr) + part * nt + i, 0)))
    return pl.pallas_call(
        body, name=name, grid_spec=grid_spec,
        out_shape=jax.ShapeDtypeStruct((2 * half, cols), F32),
        compiler_params=_params(("parallel",)),
    )(place, mine, rsib, rici)


def _adam_math(w, g, m, v):
    m2 = ADAM_B1 * m + (1.0 - ADAM_B1) * g
    v2 = ADAM_B2 * v + (1.0 - ADAM_B2) * (g * g)
    m_hat = m2 / (1.0 - ADAM_B1 ** ADAM_STEP)
    v_hat = v2 / (1.0 - ADAM_B2 ** ADAM_STEP)
    delta = -ADAM_LR * (m_hat / (jnp.sqrt(v_hat) + ADAM_EPS) + ADAM_WD * w)
    return delta, m2, v2


def _adamw(w, g, m, v, name, part=0, parts=1, prev=None):
    rows, cols = w.shape
    tr = min(rows, ROW_TILE)

    def body(*refs):
        w_ref, g_ref, m_ref, v_ref, go_ref, d_ref, m2_ref, v2_ref = refs[-8:]
        g = g_ref[...]
        go_ref[...] = g
        d_ref[...], m2_ref[...], v2_ref[...] = _adam_math(w_ref[...], g, m_ref[...], v_ref[...])

    if parts == 1:
        grid, spec = (rows // tr,), pl.BlockSpec((tr, cols), lambda i: (i, 0))
    else:
        per_half = rows // 2 // tr
        nt = per_half // parts
        grid, spec = (2, nt), pl.BlockSpec((tr, cols), lambda r, i: (r * per_half + part * nt + i, 0))
    olds = [] if prev is None else list(prev)
    return pl.pallas_call(
        body, name=name, grid=grid,
        out_shape=(jax.ShapeDtypeStruct(w.shape, F32),) * 4,
        in_specs=[HBM] * len(olds) + [spec] * 4, out_specs=(spec,) * 4,
        input_output_aliases={i: i for i in range(len(olds))},
        compiler_params=_params(("parallel",) * len(grid)),
    )(*olds, w, g, m, v)


def _ada_grad_adamw(c_all_t, dmod_cols, w, m, v):
    d, wa = w.shape
    tr = ROW_TILE

    def body(ct_ref, dm_ref, w_ref, m_ref, v_ref, g_ref, d_ref, m2_ref, v2_ref):
        act = _silu(ct_ref[...])
        g = act[:, 0:1] * dm_ref[0:1, :]
        for b in range(1, N_DEV):
            g = g + act[:, b:b + 1] * dm_ref[b:b + 1, :]
        g_ref[...] = g
        d_ref[...], m2_ref[...], v2_ref[...] = _adam_math(w_ref[...], g, m_ref[...], v_ref[...])

    spec = pl.BlockSpec((tr, wa), lambda i: (i, 0))
    return pl.pallas_call(
        body, name="ada_grad_adamw", grid=(d // tr,),
        out_shape=(jax.ShapeDtypeStruct(w.shape, F32),) * 4,
        in_specs=[pl.BlockSpec((tr, N_DEV), lambda i: (i, 0)), pl.BlockSpec((N_DEV, wa), lambda i: (0, 0)),
                  spec, spec, spec],
        out_specs=(spec,) * 4,
        compiler_params=_params(("parallel",)),
    )(c_all_t, dmod_cols, w, m, v)


def _small_update(place, gathered, pieces, weights, moments_m, moments_v):
    n = gathered.shape[1]
    k = len(weights)

    def body(place_ref, g_ref, *refs):
        w_refs, m_refs, v_refs = refs[0:k], refs[k:2 * k], refs[2 * k:3 * k]
        outs = refs[3 * k:]
        total = g_ref[0:SUBLANES, :]
        for dev in range(1, N_DEV):
            total = total + g_ref[SUBLANES * dev:SUBLANES * (dev + 1), :]

        def flat(offset, length):
            segments, pos = [], offset
            while pos < offset + length:
                row, col = divmod(pos, n)
                take = min(offset + length - pos, n - col)
                segments.append(total[row:row + 1, col:col + take])
                pos += take
            return jnp.concatenate(segments, axis=1) if len(segments) > 1 else segments[0]

        chip = place_ref[0]
        for i, (w_ref, m_ref, v_ref) in enumerate(zip(w_refs, m_refs, v_refs)):
            g = flat(*pieces[i])
            if w_ref.ndim == 3:
                rows, cols = w_ref.shape[1:]
                full = pieces[i][1] // rows
                picked = []
                for r in range(rows):
                    blocks = [g[:, r * full + q * cols:r * full + (q + 1) * cols] for q in range(N_CHIPS)]
                    mine = blocks[N_CHIPS - 1]
                    for q in range(N_CHIPS - 2, -1, -1):
                        mine = jnp.where(chip == q, blocks[q], mine)
                    picked.append(mine)
                g = jnp.concatenate(picked, axis=0)
                w, m, v = w_ref[0], m_ref[0], v_ref[0]
            else:
                w, m, v = w_ref[...], m_ref[...], v_ref[...]
            delta, m2, v2 = _adam_math(w, g, m, v)
            for j, val in enumerate((g, delta, m2, v2)):
                out = outs[j * k + i]
                if w_ref.ndim == 3:
                    out[0] = val
                else:
                    out[...] = val
        outs[4 * k][...] = flat(*pieces[k])

    shapes = [jax.ShapeDtypeStruct(w.shape, F32) for w in weights]
    grid_spec = pltpu.PrefetchScalarGridSpec(
        num_scalar_prefetch=1, grid=(1,),
        in_specs=[pl.BlockSpec(gathered.shape, lambda i, p: (0, 0))]
        + [pl.BlockSpec(a.shape, functools.partial(lambda nd, i, p: (0,) * nd, a.ndim))
           for a in (*weights, *moments_m, *moments_v)],
        out_specs=tuple(pl.BlockSpec(s.shape, functools.partial(lambda nd, i, p: (0,) * nd, len(s.shape)))
                        for s in shapes * 4) + (pl.BlockSpec((1, LANES), lambda i, p: (0, 0)),))
    outs = pl.pallas_call(
        body, name="small_update", grid_spec=grid_spec,
        out_shape=tuple(shapes * 4) + (jax.ShapeDtypeStruct((1, LANES), F32),),
        compiler_params=_params(("arbitrary",)),
    )(place, gathered, *weights, *moments_m, *moments_v)
    return outs[0:k], outs[k:2 * k], outs[2 * k:3 * k], outs[3 * k:4 * k], outs[4 * k]


def _pack_small(pieces):
    flat = [p.reshape(-1).astype(F32) for p in pieces]
    offsets, total = [], 0
    for p in flat:
        offsets.append(total)
        total += p.shape[0]
    padded = -(-total // SMALL_ALIGN) * SMALL_ALIGN
    if padded > total:
        flat.append(jnp.zeros((padded - total,), F32))
    return jnp.concatenate(flat).reshape(8, padded // 8), offsets


def _alibi_slope_rows(n_heads):
    slopes = 2.0 ** (-8.0 * jnp.arange(1, n_heads + 1, dtype=F32) / n_heads)
    rows = jnp.zeros((n_heads // 2, 8), F32).at[:, 0:2].set(slopes.reshape(n_heads // 2, 2))
    return jnp.broadcast_to(rows[:, :, None], (n_heads // 2, 8, ATT_KW))


def kernel(x, c, w_ada, b_ada, g_pre, w_in, conv_w, conv_b, g_conv, g_attn, w_out, g_post, loss_target, m_w_ada, m_b_ada, m_g_pre, m_w_in, m_conv_w, m_conv_b, m_g_conv, m_g_attn, m_w_out, m_g_post, v_w_ada, v_b_ada, v_g_pre, v_w_in, v_conv_w, v_conv_b, v_g_conv, v_g_attn, v_w_out, v_g_post):
    t, d = x.shape[1], x.shape[2]
    dc = conv_b.shape[1]
    da = g_attn.shape[1]
    hp = da // PAIR
    ws = w_in.shape[2]
    wa = w_ada.shape[2]
    cws = conv_w.shape[2]
    assert t % ROW_TILE == 0 and d % ROW_TILE == 0 and dc % COL_TILE == 0 and da % COL_TILE == 0
    assert ws == 2 * dc and dc == da and t // BRANCHES[-1][1] >= ATT_BQ

    mx, my, mc = _my_place()
    chip = _chip_of(mx, my)
    dev = 2 * chip + mc
    place = jnp.stack([chip, mc]).astype(jnp.int32)

    x2, tgt2 = x[0], loss_target[0]
    w_ada2, w_in2, w_out2 = w_ada[0], w_in[0], w_out[0]

    win_slots = _cast_into_slot(place, w_in2, "cast_w_in")
    packed, offs = _pack_small([c[0], conv_w[0]])
    seen, mod = _ada_modulation(packed, w_ada2, b_ada, d, after=(win_slots,))
    seen = seen.reshape(N_DEV, -1)
    c_all = seen[:, offs[0]:offs[0] + d]
    conv_w_full = seen[0::2, offs[1]:offs[1] + 3 * cws].reshape(N_CHIPS, 3, cws).transpose(1, 0, 2).reshape(3, dc)

    win_flight, send_in, recv_in, started = _gather_start(win_slots, mod)

    y_chip, x_chip, d_chip = (_chip_of(mx, 1 - my), _chip_of(1 - mx, my), _chip_of(1 - mx, 1 - my))
    tiles_per_part = ws // COL_TILE // 2

    def tiles_of(chunk, parts):
        return [(2 * chunk + part) * tiles_per_part + k for part in parts for k in range(tiles_per_part)]

    own_tiles, first_tiles, second_tiles, far_tiles = (jnp.stack(tiles).astype(jnp.int32) for tiles in (
        tiles_of(chip, (0, 1)), tiles_of(y_chip, (0,)) + tiles_of(x_chip, (1,)),
        tiles_of(y_chip, (1,)) + tiles_of(x_chip, (0,)), tiles_of(d_chip, (0, 1))))
    h, ht = _prenorm(x2, mod, g_pre, started)
    proj = _proj_tiles(None, h, w_in2, own_tiles, "proj_own")
    win_flight, wout_flight, relay_send_in, relay_recv_in, send_out, recv_out = _gather_relay_in(
        win_flight, _cast_into_slot(place, w_out2, "cast_w_out"), recv_in, proj)
    win_flight = _forward_halves(win_flight, ((0, 0), (1, 1)), "forward_w_in_first")
    proj = _proj_tiles(proj, h, win_flight, first_tiles, "proj_first_parts")
    win_flight = _forward_halves(
        _gather_wait_direct(win_flight, send_in, recv_in, proj, "gather_wait_w_in_direct"),
        ((0, 1), (1, 0)), "forward_w_in_second")
    proj = _proj_tiles(proj, h, win_flight, second_tiles, "proj_second_parts")
    winf = _forward_halves(
        _gather_wait_relayed(win_flight, relay_send_in, relay_recv_in, proj, "gather_wait_w_in_relayed"),
        ((2, None),), "forward_w_in_relayed")
    proj = _proj_tiles(proj, h, winf, far_tiles, "proj_diagonal")
    slopes = _alibi_slope_rows(da // HEAD_DIM)
    co = _conv_fwd(proj, conv_w_full, conv_b, dc)
    wout_flight, relay_send_out, relay_recv_out = _gather_relay_out(wout_flight, recv_out, co)
    o_mix, lse = _attn_fwd(proj, slopes, dc, da)
    g_attn_pairs = g_attn.reshape(hp, 1, PAIR)
    ycat, ycat_t = _mix_fwd(co, proj, o_mix, g_conv, g_attn_pairs)
    wout_flight = _gather_wait_direct(wout_flight, send_out, recv_out, ycat, "gather_wait_w_out_direct")
    wout_flight = _gather_wait_relayed(wout_flight, relay_send_out, relay_recv_out, ycat, "gather_wait_w_out_relayed")
    woutf = _forward_halves(wout_flight, ((0, None), (1, None), (2, None)), "forward_w_out").reshape(dc + da, d)
    dout, dy, post_sums = _out_fwd_bwd(ycat, woutf, x2, tgt2, mod, g_post)

    gout, rsib_out = _dw_swapped(ycat_t, dy, N_CHIPS, 1, "dw_out")
    csum_out = _chip_sums(gout, rsib_out, "rs_chip_sum_out")
    ssem_out, rsem_out, csum_out, land_out, sent_out = _owners_start(csum_out, "rs_owners_start_out")
    dycat = _matmul_nt(dy, woutf, BF16, "dycat")
    dproj, dco, d_o, delta, dg_conv, dg_attn = _mix_bwd(dycat, co, proj, o_mix, g_conv, g_attn_pairs)
    dproj, conv_sums = _conv_bwd(dproj, dco, proj, conv_w_full, dc, sent_out)
    dproj = _attn_bwd(dproj, proj, d_o, lse, delta, slopes, dc, da, sent_out)
    gin, rsib_in = _dw_swapped(ht, dproj, 1, N_CHIPS, "dw_in")
    ssem_in0, rsem_in0, csum_in0, land_in0, sent_in0 = _owners_start(
        _chip_sums(gin, rsib_in, "rs_chip_sum_in0", 0, 2), "rs_owners_start_in0")
    ssem_in1, rsem_in1, csum_in1, land_in1, sent_in = _owners_start(
        _chip_sums(gin, rsib_in, "rs_chip_sum_in1", 1, 2, after=(sent_in0,)), "rs_owners_start_in1")
    dh = _dh(dproj, winf, sent_in)
    grad_x, pre_sums = _prenorm_bwd(x2, dh, dout, mod, g_pre)

    small, so = _pack_small([
        pre_sums[0], pre_sums[1], post_sums[0],
        pre_sums[2], conv_sums[0:3], conv_sums[3], dg_conv, dg_attn, post_sums[1], post_sums[2, 0:128]])
    ssem_small, rsem_small, small, land_small, sent_small = _allgather8_start(small, dev, "gather_small_start")

    rici_out = _owners_wait(ssem_out, rsem_out, csum_out, land_out, [grad_x, sent_small], "rs_owners_wait_out")
    grad_w_out = _join_halves(_owner_sum(place, gout, rsib_out, rici_out, "rs_owner_sum_out"), "rs_join_halves_out")
    grad_w_out, delta_w_out, new_m_w_out, new_v_w_out = _adamw(
        w_out2, grad_w_out, m_w_out[0], v_w_out[0], "adamw_w_out")

    rici_in = _owners_wait(ssem_in0, rsem_in0, csum_in0, land_in0, [delta_w_out], "rs_owners_wait_in0")
    full_in0, jsend0, jrecv0, joining0 = _join_start(
        _owner_sum(place, gin, rsib_in, rici_in, "rs_owner_sum_in0", 0, 2), "rs_join_start_in0", 0, 2)
    full_in0 = _join_wait(full_in0, jsend0, jrecv0, [joining0], "rs_join_wait_in0", 0, 2)
    updated_in = _adamw(w_in2, full_in0, m_w_in[0], v_w_in[0], "adamw_w_in0", 0, 2)
    rici_in = _owners_wait(ssem_in1, rsem_in1, csum_in1, land_in1, [updated_in[1]], "rs_owners_wait_in1")
    full_in1, jsend1, jrecv1, joining1 = _join_start(
        _owner_sum(place, gin, rsib_in, rici_in, "rs_owner_sum_in1", 1, 2), "rs_join_start_in1", 1, 2)

    small_seen = _allgather8_wait(ssem_small, rsem_small, small, land_small, [joining1], "gather_small_wait")
    small_w = [b_ada, g_pre, conv_w, conv_b, g_conv, g_attn, g_post]
    small_m = [m_b_ada, m_g_pre, m_conv_w, m_conv_b, m_g_conv, m_g_attn, m_g_post]
    small_v = [v_b_ada, v_g_pre, v_conv_w, v_conv_b, v_g_conv, v_g_attn, v_g_post]
    pieces = [(0, 3 * d), (so[3], d), (so[4], 3 * dc), (so[5], dc), (so[6], dc), (so[7], da), (so[8], d), (so[9], LANES)]
    g_small, d_small, m_small, v_small, loss_row = _small_update(place, small_seen, pieces, small_w, small_m, small_v)
    loss = loss_row[0, 0]
    grad_b_ada, grad_g_pre, grad_conv_w, grad_conv_b, grad_g_conv, grad_g_attn, grad_g_post = g_small
    dmod_cols = lax.dynamic_slice_in_dim(small_seen.reshape(N_DEV, -1), chip * wa, wa, axis=1)
    grad_w_ada, delta_w_ada, new_m_w_ada, new_v_w_ada = _ada_grad_adamw(c_all.T, dmod_cols, w_ada2, m_w_ada[0], v_w_ada[0])

    full_in1 = _join_wait(full_in1, jsend1, jrecv1, [delta_w_ada, d_small[0]], "rs_join_wait_in1", 1, 2)
    grad_w_in, delta_w_in, new_m_w_in, new_v_w_in = _adamw(
        w_in2, full_in1, m_w_in[0], v_w_in[0], "adamw_w_in1", 1, 2, updated_in)

    def lead(a):
        return a.reshape((1,) + a.shape)

    grads = [lead(grad_w_ada), grad_b_ada, grad_g_pre, lead(grad_w_in), grad_conv_w, grad_conv_b, grad_g_conv,
             grad_g_attn, lead(grad_w_out), grad_g_post]
    deltas = [lead(delta_w_ada), d_small[0], d_small[1], lead(delta_w_in), d_small[2], d_small[3], d_small[4],
              d_small[5], lead(delta_w_out), d_small[6]]
    new_ms = [lead(new_m_w_ada), m_small[0], m_small[1], lead(new_m_w_in), m_small[2], m_small[3], m_small[4],
              m_small[5], lead(new_m_w_out), m_small[6]]
    new_vs = [lead(new_v_w_ada), v_small[0], v_small[1], lead(new_v_w_in), v_small[2], v_small[3], v_small[4],
              v_small[5], lead(new_v_w_out), v_small[6]]
    return (loss, lead(grad_x), *grads, *deltas, *new_ms, *new_vs)
```

```python
import functools

import jax
import jax.numpy as jnp
from jax import lax
from jax.experimental import pallas as pl
from jax.experimental.pallas import tpu as pltpu

F32 = jnp.float32
BF16 = jnp.bfloat16
MESH = pl.DeviceIdType.MESH
HBM = pl.BlockSpec(memory_space=pltpu.HBM)
VMEM = pl.BlockSpec(memory_space=pltpu.VMEM)
ANY = pl.BlockSpec(memory_space=pl.ANY)
SEM = pl.BlockSpec(memory_space=pltpu.SEMAPHORE)
EFFECT = pltpu.SideEffectType.DATAFLOW_SIDE_EFFECTING
SUBLANES, LANES = 8, 128
TOKEN = jax.ShapeDtypeStruct((SUBLANES, LANES), jnp.float32)

HEAD_DIM = 64
PAIR = 2 * HEAD_DIM
assert PAIR == LANES
BRANCHES = ((128, 1), (512, 4), (2048, 16))
SIDE = 64
EPS = 1e-6
NEG_INF = -1e30
N_CHIPS = 4
N_DEV = 8

ADAM_LR = 0.001
ADAM_B1 = 0.9
ADAM_B2 = 0.999
ADAM_EPS = 1e-08
ADAM_WD = 0.01
ADAM_STEP = 10

VMEM_LIMIT_BYTES = 56 * 1024 * 1024
ROW_TILE = 256
COL_TILE = 512
CONV_TILE = 256
ATT_BQ = 128
ATT_KW = ATT_BQ + 2 * SIDE
ATT_UNROLL = 4
SMALL_ALIGN = SUBLANES * LANES


def _params(semantics=None):
    kw = {"vmem_limit_bytes": VMEM_LIMIT_BYTES}
    if semantics is not None:
        kw["dimension_semantics"] = semantics
    return pltpu.CompilerParams(**kw)


def _silu(z):
    return z * jax.nn.sigmoid(z)


def _silu_grad(z):
    s = jax.nn.sigmoid(z)
    return s * (1.0 + z * (1.0 - s))


def _my_place():
    return lax.axis_index("x"), lax.axis_index("y"), lax.axis_index("c")


def _flip(a, bit):
    return 1 - a if bit else a


def _chip_of(x, y):
    return 2 * x + y


def _allgather8_start(v, me, name):
    rows_per, n = v.shape
    land = lax.dynamic_update_slice(jnp.zeros((N_DEV * rows_per, n), v.dtype), v, (me * rows_per, 0))

    def body(v_ref, land_ref, send_sems, recv_sems, v_thru, land_thru, token_ref):
        del v_thru, land_thru
        x, y, c = _my_place()
        mine = land_ref.at[pl.ds(pl.multiple_of((4 * x + 2 * y + c) * rows_per, rows_per), rows_per), :]
        for k in range(1, N_DEV):
            peer = (_flip(x, k & 4), _flip(y, k & 2), _flip(c, k & 1))
            pltpu.make_async_remote_copy(
                src_ref=v_ref, dst_ref=mine, send_sem=send_sems.at[k - 1], recv_sem=recv_sems.at[k - 1],
                device_id=peer, device_id_type=MESH).start()
        token_ref[...] = jnp.zeros(token_ref.shape, F32)

    sems = pltpu.SemaphoreType.DMA((N_DEV - 1,))
    return pl.pallas_call(
        body, name=name,
        out_shape=(sems, sems, jax.ShapeDtypeStruct(v.shape, v.dtype), jax.ShapeDtypeStruct(land.shape, land.dtype), TOKEN),
        in_specs=[HBM, HBM], out_specs=(SEM, SEM, HBM, HBM, VMEM),
        input_output_aliases={0: 2, 1: 3},
        compiler_params=pltpu.CompilerParams(has_side_effects=EFFECT),
    )(pltpu.with_memory_space_constraint(v, pltpu.HBM), pltpu.with_memory_space_constraint(land, pltpu.HBM))


def _allgather8_wait(send_sems, recv_sems, v, land, after, name):
    rows_per = v.shape[0]

    def body(v_ref, land_ref, send_ref, recv_ref, *rest):
        del rest
        x, y, c = _my_place()
        for k in range(1, N_DEV):
            peer = (_flip(x, k & 4), _flip(y, k & 2), _flip(c, k & 1))
            src = 4 * peer[0] + 2 * peer[1] + peer[2]
            cp = pltpu.make_async_remote_copy(
                src_ref=v_ref, dst_ref=land_ref.at[pl.ds(pl.multiple_of(src * rows_per, rows_per), rows_per), :],
                send_sem=send_ref.at[k - 1], recv_sem=recv_ref.at[k - 1], device_id=peer, device_id_type=MESH)
            cp.wait_send()
            cp.wait_recv()

    return pl.pallas_call(
        body, name=name,
        out_shape=(jax.ShapeDtypeStruct(v.shape, v.dtype), jax.ShapeDtypeStruct(land.shape, land.dtype)),
        in_specs=[HBM, HBM, SEM, SEM] + [ANY] * len(after), out_specs=(HBM, HBM),
        input_output_aliases={0: 0, 1: 1},
        compiler_params=pltpu.CompilerParams(has_side_effects=EFFECT),
    )(v, land, send_sems, recv_sems, *after)[1]


def _half_rows(ref, chip, which, half):
    return ref.at[chip, pl.ds(pl.multiple_of(which * half, half), half), :]


def _ici_peers(x, y, c):
    peers = [(_flip(x, k & 2), _flip(y, k & 1), c) for k in (1, 2, 3)]
    return [(peer, _chip_of(peer[0], peer[1])) for peer in peers]


def _part_of_half(ref, chip, core, part):
    half, cols = ref.shape[1] // 2, ref.shape[2] // 2
    return ref.at[chip, pl.ds(pl.multiple_of(core * half, half), half), pl.ds(part * cols, cols)]


def _neighbours(x, y, c):
    return [((x, 1 - y, c), _chip_of(x, 1 - y)), ((1 - x, y, c), _chip_of(1 - x, y)),
            ((1 - x, 1 - y, c), _chip_of(1 - x, 1 - y))]


def _start_direct(buf, send_sems, recv_sems):
    x, y, c = _my_place()
    me = _chip_of(x, y)
    for n, (peer, _) in enumerate(_neighbours(x, y, c)[0:2]):
        for part in ((0, 1), (1, 0))[n]:
            piece = _part_of_half(buf, me, c, part)
            pltpu.make_async_remote_copy(
                src_ref=piece, dst_ref=piece, send_sem=send_sems.at[2 * n + part], recv_sem=recv_sems.at[2 * n + part],
                device_id=peer, device_id_type=MESH).start()


def _relay(buf, recv_sems, relay_send, relay_recv):
    x, y, c = _my_place()
    nbrs = _neighbours(x, y, c)
    for n in range(2):
        part = n
        piece = _part_of_half(buf, nbrs[n][1], c, part)
        pltpu.make_async_remote_copy(
            src_ref=piece, dst_ref=piece, send_sem=relay_send.at[part], recv_sem=recv_sems.at[2 * n + part],
            device_id=nbrs[n][0], device_id_type=MESH).wait_recv()
        pltpu.make_async_remote_copy(
            src_ref=piece, dst_ref=piece, send_sem=relay_send.at[part], recv_sem=relay_recv.at[part],
            device_id=nbrs[1 - n][0], device_id_type=MESH).start()


def _gather_start(win_slots, after):
    def body(win_in, after_ref, win_ref, send_sems, recv_sems, token_ref):
        del win_in, after_ref
        _start_direct(win_ref, send_sems, recv_sems)
        token_ref[...] = jnp.zeros(token_ref.shape, F32)

    sems = pltpu.SemaphoreType.DMA((4,))
    return pl.pallas_call(
        body, name="gather_start",
        out_shape=(jax.ShapeDtypeStruct(win_slots.shape, win_slots.dtype), sems, sems, TOKEN),
        in_specs=[HBM, ANY], out_specs=(HBM, SEM, SEM, VMEM),
        input_output_aliases={0: 0},
        compiler_params=pltpu.CompilerParams(has_side_effects=EFFECT),
    )(win_slots, after)


def _gather_relay_in(win, wout_slots, recv_in, after):
    def body(win_in, wout_in, recv_in_ref, after_ref, win_ref, wout_ref, relay_send, relay_recv, send_out, recv_out):
        del win_in, wout_in, after_ref
        _relay(win_ref, recv_in_ref, relay_send, relay_recv)
        _start_direct(wout_ref, send_out, recv_out)

    two, four = pltpu.SemaphoreType.DMA((2,)), pltpu.SemaphoreType.DMA((4,))
    return pl.pallas_call(
        body, name="gather_relay_w_in",
        out_shape=(jax.ShapeDtypeStruct(win.shape, win.dtype), jax.ShapeDtypeStruct(wout_slots.shape, wout_slots.dtype),
                   two, two, four, four),
        in_specs=[HBM, HBM, SEM, ANY], out_specs=(HBM, HBM, SEM, SEM, SEM, SEM),
        input_output_aliases={0: 0, 1: 1},
        compiler_params=pltpu.CompilerParams(has_side_effects=EFFECT),
    )(win, wout_slots, recv_in, after)


def _gather_relay_out(wout, recv_out, after):
    def body(wout_in, recv_out_ref, after_ref, wout_ref, relay_send, relay_recv):
        del wout_in, after_ref
        _relay(wout_ref, recv_out_ref, relay_send, relay_recv)

    two = pltpu.SemaphoreType.DMA((2,))
    return pl.pallas_call(
        body, name="gather_relay_w_out",
        out_shape=(jax.ShapeDtypeStruct(wout.shape, wout.dtype), two, two),
        in_specs=[HBM, SEM, ANY], out_specs=(HBM, SEM, SEM),
        input_output_aliases={0: 0},
        compiler_params=pltpu.CompilerParams(has_side_effects=EFFECT),
    )(wout, recv_out, after)


def _gather_wait_direct(buf, send_sems, recv_sems, after, name):
    def body(buf_in, send_ref, recv_ref, after_ref, buf_ref):
        del buf_in, after_ref
        x, y, c = _my_place()
        me = _chip_of(x, y)
        for n, (peer, chip) in enumerate(_neighbours(x, y, c)[0:2]):
            second = 1 - n
            pltpu.make_async_remote_copy(
                src_ref=_part_of_half(buf_ref, me, c, second), dst_ref=_part_of_half(buf_ref, chip, c, second),
                send_sem=send_ref.at[2 * n + second], recv_sem=recv_ref.at[2 * n + second],
                device_id=peer, device_id_type=MESH).wait_recv()
            for part in range(2):
                piece = _part_of_half(buf_ref, me, c, part)
                pltpu.make_async_remote_copy(
                    src_ref=piece, dst_ref=piece, send_sem=send_ref.at[2 * n + part], recv_sem=recv_ref.at[2 * n + part],
                    device_id=peer, device_id_type=MESH).wait_send()

    return pl.pallas_call(
        body, name=name,
        out_shape=jax.ShapeDtypeStruct(buf.shape, buf.dtype),
        in_specs=[HBM, SEM, SEM, ANY], out_specs=HBM,
        input_output_aliases={0: 0},
        compiler_params=pltpu.CompilerParams(has_side_effects=EFFECT),
    )(buf, send_sems, recv_sems, after)


def _gather_wait_relayed(buf, relay_send, relay_recv, after, name):
    def body(buf_in, rsend_ref, rrecv_ref, after_ref, buf_ref):
        del buf_in, after_ref
        x, y, c = _my_place()
        nbrs = _neighbours(x, y, c)
        for n in range(2):
            relayed = _part_of_half(buf_ref, nbrs[n][1], c, n)
            cp = pltpu.make_async_remote_copy(
                src_ref=relayed, dst_ref=_part_of_half(buf_ref, nbrs[2][1], c, n),
                send_sem=rsend_ref.at[n], recv_sem=rrecv_ref.at[n], device_id=nbrs[1 - n][0], device_id_type=MESH)
            cp.wait_recv()
            cp.wait_send()

    return pl.pallas_call(
        body, name=name,
        out_shape=jax.ShapeDtypeStruct(buf.shape, buf.dtype),
        in_specs=[HBM, SEM, SEM, ANY], out_specs=HBM,
        input_output_aliases={0: 0},
        compiler_params=pltpu.CompilerParams(has_side_effects=EFFECT),
    )(buf, relay_send, relay_recv, after)


def _forward_halves(buf, which, name):
    half = buf.shape[1] // 2

    def piece(buf_ref, chip, core, part):
        return _half_rows(buf_ref, chip, core, half) if part is None else _part_of_half(buf_ref, chip, core, part)

    def body(buf_in, buf_ref, send_sems, recv_sems):
        del buf_in
        x, y, c = _my_place()
        sibling = (x, y, 1 - c)
        pieces = [(_neighbours(x, y, c)[n][1], part) for n, part in which]
        started = []
        for k, (src_chip, part) in enumerate(pieces):
            landed = piece(buf_ref, src_chip, c, part)
            fw = pltpu.make_async_remote_copy(
                src_ref=landed, dst_ref=landed, send_sem=send_sems.at[k], recv_sem=recv_sems.at[k],
                device_id=sibling, device_id_type=MESH)
            fw.start()
            started.append(fw)
        for k, (src_chip, part) in enumerate(pieces):
            other = piece(buf_ref, src_chip, 1 - c, part)
            pltpu.make_async_remote_copy(
                src_ref=other, dst_ref=other, send_sem=send_sems.at[k], recv_sem=recv_sems.at[k],
                device_id=sibling, device_id_type=MESH).wait_recv()
        for fw in started:
            fw.wait_send()

    return pl.pallas_call(
        body, name=name,
        out_shape=jax.ShapeDtypeStruct(buf.shape, buf.dtype),
        in_specs=[HBM], out_specs=HBM,
        input_output_aliases={0: 0},
        scratch_shapes=[pltpu.SemaphoreType.DMA((len(which),))] * 2,
    )(buf)


def _dw_swapped(a, b, row_chunks, col_chunks, name):
    r, t = a.shape
    c_all = b.shape[1]
    chunks = row_chunks * col_chunks
    rq, cq = r // row_chunks, c_all // col_chunks
    half = rq // 2
    tn = COL_TILE
    nt = cq // tn
    steps = col_chunks * nt

    def body(a_ref, b_ref, mine_ref, sib_ref, stage, send_sems, recv_sems):
        x, y, c = _my_place()
        j, n = pl.program_id(0), pl.program_id(1)
        step = j * nt + n
        slot = step % 2
        res = jnp.dot(a_ref[...], b_ref[...], preferred_element_type=F32).astype(BF16)

        def landing(jj, nn):
            cols = pl.ds(pl.multiple_of(nn * tn, tn), tn)
            return sib_ref.at[:, :, cols] if col_chunks == 1 else sib_ref.at[pl.ds(jj, 1), :, cols]

        def copy(slot_, step_, jj, nn):
            return pltpu.make_async_remote_copy(
                src_ref=stage.at[slot_], dst_ref=landing(jj, nn), send_sem=send_sems.at[slot_],
                recv_sem=recv_sems.at[step_], device_id=(x, y, 1 - c), device_id_type=MESH)

        @pl.when(step >= 2)
        def _():
            copy(slot, step, j, n).wait_send()

        for q in range(row_chunks):
            lo = res[q * rq:q * rq + half, :]
            hi = res[q * rq + half:(q + 1) * rq, :]
            mine_ref[q] = jnp.where(c == 0, lo, hi)
            stage[slot, q] = jnp.where(c == 0, hi, lo)
        copy(slot, step, j, n).start()

        @pl.when(step == steps - 1)
        def _():
            for s in range(max(steps - 2, 0), steps):
                copy(s % 2, s, j, n).wait_send()
            for s in range(steps):
                copy(s % 2, s, j, n).wait_recv()

    shape = jax.ShapeDtypeStruct((chunks, half, cq), BF16)
    return pl.pallas_call(
        body, name=name, grid=(col_chunks, nt),
        out_shape=(shape, shape),
        in_specs=[pl.BlockSpec((r, t), lambda j, n: (0, 0)), pl.BlockSpec((t, tn), lambda j, n: (0, j * nt + n))],
        out_specs=(pl.BlockSpec((row_chunks, half, tn), lambda j, n: (j, 0, n)), ANY),
        scratch_shapes=[pltpu.VMEM((2, row_chunks, half, tn), BF16), pltpu.SemaphoreType.DMA((2,)),
                        pltpu.SemaphoreType.DMA((steps,))],
        compiler_params=_params(("arbitrary", "arbitrary")),
    )(a, b)


def _owners_start(csum, name, after=()):
    land = pltpu.with_memory_space_constraint(lax.empty((N_CHIPS - 1,) + csum.shape[1:], csum.dtype), pltpu.HBM)

    def body(csum_ref, land_ref, *rest):
        send_sems, recv_sems, _, _, token_ref = rest[len(after):]
        x, y, c = _my_place()
        for k, (peer, owner) in enumerate(_ici_peers(x, y, c)):
            pltpu.make_async_remote_copy(
                src_ref=csum_ref.at[owner], dst_ref=land_ref.at[k], send_sem=send_sems.at[k], recv_sem=recv_sems.at[k],
                device_id=peer, device_id_type=MESH).start()
        token_ref[...] = jnp.zeros(token_ref.shape, F32)

    sems = pltpu.SemaphoreType.DMA((N_CHIPS - 1,))
    return pl.pallas_call(
        body, name=name,
        out_shape=(sems, sems, jax.ShapeDtypeStruct(csum.shape, csum.dtype),
                   jax.ShapeDtypeStruct(land.shape, land.dtype), TOKEN),
        in_specs=[HBM, HBM] + [ANY] * len(after), out_specs=(SEM, SEM, HBM, HBM, VMEM),
        input_output_aliases={0: 2, 1: 3},
        compiler_params=pltpu.CompilerParams(has_side_effects=EFFECT),
    )(pltpu.with_memory_space_constraint(csum, pltpu.HBM), land, *after)


def _owners_wait(send_sems, recv_sems, csum, land, after, name):
    def body(csum_ref, land_ref, send_ref, recv_ref, *rest):
        del rest
        x, y, c = _my_place()
        for k, (peer, owner) in enumerate(_ici_peers(x, y, c)):
            cp = pltpu.make_async_remote_copy(
                src_ref=csum_ref.at[owner], dst_ref=land_ref.at[k], send_sem=send_ref.at[k], recv_sem=recv_ref.at[k],
                device_id=peer, device_id_type=MESH)
            cp.wait_send()
            cp.wait_recv()

    return pl.pallas_call(
        body, name=name,
        out_shape=(jax.ShapeDtypeStruct(csum.shape, csum.dtype), jax.ShapeDtypeStruct(land.shape, land.dtype)),
        in_specs=[HBM, HBM, SEM, SEM] + [ANY] * len(after), out_specs=(HBM, HBM),
        input_output_aliases={0: 0, 1: 1},
        compiler_params=pltpu.CompilerParams(has_side_effects=EFFECT),
    )(csum, land, send_sems, recv_sems, *after)[1]


def _join_start(full, name, part, parts):
    half = full.shape[0] // 2
    rows = half // parts

    def body(full_in, full_ref, send_sem, recv_sem, token_ref):
        del full_in
        x, y, c = _my_place()
        mine = full_ref.at[pl.ds(pl.multiple_of(c * half + part * rows, rows), rows), :]
        pltpu.make_async_remote_copy(
            src_ref=mine, dst_ref=mine, send_sem=send_sem.at[0], recv_sem=recv_sem.at[0],
            device_id=(x, y, 1 - c), device_id_type=MESH).start()
        token_ref[...] = jnp.zeros(token_ref.shape, F32)

    one = pltpu.SemaphoreType.DMA((1,))
    return pl.pallas_call(
        body, name=name,
        out_shape=(jax.ShapeDtypeStruct(full.shape, full.dtype), one, one, TOKEN),
        in_specs=[HBM], out_specs=(HBM, SEM, SEM, VMEM),
        input_output_aliases={0: 0},
        compiler_params=pltpu.CompilerParams(has_side_effects=EFFECT),
    )(full)


def _join_wait(full, send_sem, recv_sem, after, name, part, parts):
    half = full.shape[0] // 2
    rows = half // parts

    def body(full_in, send_ref, recv_ref, *rest):
        del full_in
        full_ref = rest[-1]
        x, y, c = _my_place()
        cp = pltpu.make_async_remote_copy(
            src_ref=full_ref.at[pl.ds(pl.multiple_of(c * half + part * rows, rows), rows), :],
            dst_ref=full_ref.at[pl.ds(pl.multiple_of((1 - c) * half + part * rows, rows), rows), :],
            send_sem=send_ref.at[0], recv_sem=recv_ref.at[0], device_id=(x, y, 1 - c), device_id_type=MESH)
        cp.wait_send()
        cp.wait_recv()

    return pl.pallas_call(
        body, name=name,
        out_shape=jax.ShapeDtypeStruct(full.shape, full.dtype),
        in_specs=[HBM, SEM, SEM] + [ANY] * len(after), out_specs=HBM,
        input_output_aliases={0: 0},
        compiler_params=pltpu.CompilerParams(has_side_effects=EFFECT),
    )(full, send_sem, recv_sem, *after)


def _cast_into_slot(place, w, name):
    rows, cols = w.shape
    tr = min(rows, ROW_TILE)

    def body(place_ref, w_ref, o_ref):
        del place_ref
        o_ref[...] = w_ref[...].astype(BF16)

    grid_spec = pltpu.PrefetchScalarGridSpec(
        num_scalar_prefetch=1, grid=(rows // tr,),
        in_specs=[pl.BlockSpec((tr, cols), lambda i, p: (i, 0))],
        out_specs=pl.BlockSpec((None, tr, cols), lambda i, p: (p[0], i, 0)))
    return pl.pallas_call(
        body, name=name, grid_spec=grid_spec,
        out_shape=jax.ShapeDtypeStruct((N_CHIPS, rows, cols), BF16),
        compiler_params=_params(("parallel",)),
    )(place, w)


def _ada_modulation(packed, w_ada, b_ada, d, after=()):
    rows_per, n = packed.shape
    d_model, wa = w_ada.shape

    def body(v_ref, w_hbm, b_ref, *rest):
        all_ref, mod_ref, w_vmem, part_ref, parts_ref, load_sem, send1, recv1, send2, recv2 = rest[len(after):]
        x, y, c = _my_place()
        me = 4 * x + 2 * y + c
        chip = _chip_of(x, y)
        load = pltpu.make_async_copy(w_hbm, w_vmem, load_sem)
        load.start()

        def rows(idx):
            return all_ref.at[pl.ds(pl.multiple_of(idx * rows_per, rows_per), rows_per), :]

        all_ref[pl.ds(pl.multiple_of(me * rows_per, rows_per), rows_per), :] = v_ref[...]
        copies = []
        for k in range(1, N_DEV):
            peer = (_flip(x, k & 4), _flip(y, k & 2), _flip(c, k & 1))
            cp = pltpu.make_async_remote_copy(
                src_ref=v_ref, dst_ref=rows(me), send_sem=send1.at[k - 1], recv_sem=recv1.at[k - 1],
                device_id=peer, device_id_type=MESH)
            cp.start()
            copies.append((cp, peer))
        for k, (cp, peer) in enumerate(copies):
            pltpu.make_async_remote_copy(
                src_ref=v_ref, dst_ref=rows(4 * peer[0] + 2 * peer[1] + peer[2]), send_sem=send1.at[k],
                recv_sem=recv1.at[k], device_id=peer, device_id_type=MESH).wait_recv()
        for cp, _ in copies:
            cp.wait_send()

        def c_of(dev):
            segments, pos = [], 0
            while pos < d:
                row, col = divmod(pos, n)
                take = min(d - pos, n - col)
                segments.append(all_ref[dev * rows_per + row:dev * rows_per + row + 1, col:col + take])
                pos += take
            return jnp.concatenate(segments, axis=1)

        c_all = jnp.concatenate([c_of(dev) for dev in range(N_DEV)], axis=0)
        load.wait()
        part_ref[...] = jnp.dot(_silu(c_all), w_vmem[...], precision=lax.Precision.HIGHEST, preferred_element_type=F32)
        parts_ref[chip] = part_ref[...]
        swaps = []
        for k, (peer, _) in enumerate(_ici_peers(x, y, c)):
            cp = pltpu.make_async_remote_copy(
                src_ref=part_ref, dst_ref=parts_ref.at[chip], send_sem=send2.at[k], recv_sem=recv2.at[k],
                device_id=peer, device_id_type=MESH)
            cp.start()
            swaps.append(cp)
        for k, (peer, peer_chip) in enumerate(_ici_peers(x, y, c)):
            pltpu.make_async_remote_copy(
                src_ref=part_ref, dst_ref=parts_ref.at[peer_chip], send_sem=send2.at[k], recv_sem=recv2.at[k],
                device_id=peer, device_id_type=MESH).wait_recv()
        for cp in swaps:
            cp.wait_send()
        flat = jnp.concatenate([parts_ref[j, pl.ds(me, 1), :] for j in range(N_CHIPS)], axis=1) + b_ref[...]
        mod_ref[...] = jnp.concatenate([flat[:, i * d:(i + 1) * d] for i in range(3)], axis=0)

    return pl.pallas_call(
        body, name="ada_modulation",
        out_shape=(jax.ShapeDtypeStruct((N_DEV * rows_per, n), F32), jax.ShapeDtypeStruct((3, d), F32)),
        in_specs=[VMEM, ANY, VMEM] + [ANY] * len(after), out_specs=(VMEM, VMEM),
        scratch_shapes=[pltpu.VMEM((d_model, wa), F32), pltpu.VMEM((N_DEV, wa), F32),
                        pltpu.VMEM((N_CHIPS, N_DEV, wa), F32), pltpu.SemaphoreType.DMA,
                        pltpu.SemaphoreType.DMA((N_DEV - 1,)), pltpu.SemaphoreType.DMA((N_DEV - 1,)),
                        pltpu.SemaphoreType.DMA((N_CHIPS - 1,)), pltpu.SemaphoreType.DMA((N_CHIPS - 1,))],
        compiler_params=_params(),
    )(packed, w_ada, b_ada, *after)


def _prenorm(x, mod, g_pre, after):
    t, d = x.shape
    tb = ROW_TILE

    def body(x_ref, mod_ref, g_ref, after_ref, h_ref, ht_ref):
        del after_ref
        xv = x_ref[...]
        r = lax.rsqrt(jnp.mean(xv * xv, axis=-1, keepdims=True) + EPS)
        h = (xv * r) * g_ref[...] * (1.0 + mod_ref[1:2, :]) + mod_ref[0:1, :]
        h_ref[...] = h.astype(BF16)
        ht_ref[...] = h.T.astype(BF16)

    return pl.pallas_call(
        body, name="prenorm", grid=(t // tb,),
        out_shape=(jax.ShapeDtypeStruct((t, d), BF16), jax.ShapeDtypeStruct((d, t), BF16)),
        in_specs=[pl.BlockSpec((tb, d), lambda i: (i, 0)), pl.BlockSpec((3, d), lambda i: (0, 0)),
                  pl.BlockSpec((1, d), lambda i: (0, 0)), ANY],
        out_specs=(pl.BlockSpec((tb, d), lambda i: (i, 0)), pl.BlockSpec((d, tb), lambda i: (0, i))),
        compiler_params=_params(("parallel",)),
    )(x, mod, g_pre, after)


def _proj_tiles(proj, h, w, tiles, name):
    t, d = h.shape
    ws = w.shape[-1]
    tn = COL_TILE
    nt = ws // tn

    def body(tile_ref, *refs):
        del tile_ref
        a_ref, b_ref, o_ref = refs[-3:]
        o_ref[...] = jnp.dot(a_ref[...], b_ref[...].astype(BF16), preferred_element_type=F32).astype(BF16)

    if w.ndim == 3:
        w_spec = pl.BlockSpec((None, d, tn), lambda i, tl: (tl[i] // nt, 0, tl[i] % nt))
    else:
        w_spec = pl.BlockSpec((d, tn), lambda i, tl: (0, tl[i] % nt))
    first = proj is None
    grid_spec = pltpu.PrefetchScalarGridSpec(
        num_scalar_prefetch=1, grid=(tiles.shape[0],),
        in_specs=([] if first else [HBM]) + [pl.BlockSpec((t, d), lambda i, tl: (0, 0)), w_spec],
        out_specs=pl.BlockSpec((t, tn), lambda i, tl: (0, tl[i])))
    return pl.pallas_call(
        body, name=name, grid_spec=grid_spec,
        out_shape=jax.ShapeDtypeStruct((t, N_CHIPS * ws), BF16),
        input_output_aliases={} if first else {1: 0},
        compiler_params=_params(("parallel",)),
    )(*([tiles] if first else [tiles, proj]), h, w)


def _shift_rows(a, rows):
    idx = lax.broadcasted_iota(jnp.int32, a.shape, 0)
    prev = jnp.where(idx == 0, 0.0, pltpu.roll(a, 1, 0))
    nxt = jnp.where(idx == rows - 1, 0.0, pltpu.roll(a, rows - 1, 0))
    return prev, nxt


def _conv_fwd(conv_proj, conv_w, conv_b, dc):
    t = conv_proj.shape[0]
    ct = CONV_TILE
    nct = dc // ct

    def body(u_ref, cg_ref, w_ref, b_ref, co_ref):
        a = cg_ref[...].astype(F32) * u_ref[...].astype(F32)
        prev, nxt = _shift_rows(a, t)
        co_ref[...] = (w_ref[0:1, :] * prev + w_ref[1:2, :] * a + w_ref[2:3, :] * nxt + b_ref[...]).astype(BF16)

    return pl.pallas_call(
        body, name="conv_fwd", grid=(nct,),
        out_shape=jax.ShapeDtypeStruct((t, dc), BF16),
        in_specs=[pl.BlockSpec((t, ct), lambda i: (0, i)), pl.BlockSpec((t, ct), lambda i: (0, 2 * nct + i)),
                  pl.BlockSpec((3, ct), lambda i: (0, i)), pl.BlockSpec((1, ct), lambda i: (0, i))],
        out_specs=pl.BlockSpec((t, ct), lambda i: (0, i)),
        compiler_params=_params(("parallel",)),
    )(conv_proj, conv_proj, conv_w, conv_b)


def _to_residue_major(src_ref, dst_ref, r):
    seq = src_ref.shape[0] // r
    for res in range(r):
        dst_ref[res * seq:(res + 1) * seq, :] = src_ref[pl.ds(res, seq, stride=r), :].astype(dst_ref.dtype)


def _branch_operands(token_refs, stage, dil, r):
    if r == 1:
        return list(token_refs)
    for i, ref in enumerate(token_refs):
        stage[...] = ref[...].astype(F32)
        _to_residue_major(stage, dil.at[i], r)
    return [dil.at[i] for i in range(len(token_refs))]


def _scaled_queries(q):
    return (q.astype(F32) * (HEAD_DIM ** -0.5)).astype(BF16)


BLOCK_SHIFTS = (0, -SIDE, None)


def _band_bias(rel, slope):
    arel = jnp.abs(rel)
    return jnp.where(arel <= SIDE, arel.astype(F32) * slope, NEG_INF)


def _fill_bias_tiles(bias_ref, sl_ref, r, kw):
    base = lax.broadcasted_iota(jnp.int32, (ATT_BQ, kw), 1) - lax.broadcasted_iota(jnp.int32, (ATT_BQ, kw), 0)
    for hh in range(2):
        slope = -(sl_ref[hh:hh + 1, 0:kw] * float(r))
        for e, shift in enumerate(BLOCK_SHIFTS):
            shift = ATT_BQ - kw if shift is None else shift
            bias_ref[hh, e, :, 0:kw] = _band_bias(base + shift, slope)


def _fill_stacked_bias_tiles(bias_ref, sl_ref, r, kw):
    base = lax.broadcasted_iota(jnp.int32, (kw, ATT_BQ), 0) - lax.broadcasted_iota(jnp.int32, (kw, ATT_BQ), 1)
    for hh in range(2):
        slope = -(sl_ref[hh:hh + 1, 0:ATT_BQ] * float(r))
        for e, shift in enumerate(BLOCK_SHIFTS):
            shift = ATT_BQ - kw if shift is None else shift
            bias_ref[e, 0:kw, hh * ATT_BQ:(hh + 1) * ATT_BQ] = _band_bias(base + shift, slope)


def _first_head_lanes():
    return lax.broadcasted_iota(jnp.int32, (1, PAIR), 1) < HEAD_DIM


def _only_head(x, first, hh):
    return jnp.where(first if hh == 0 else jnp.logical_not(first), x, jnp.zeros_like(x))


def _block_place(g, seq_len, kw):
    nqb = seq_len // ATT_BQ
    if nqb == 1:
        row = pl.multiple_of(g * ATT_BQ, ATT_BQ)
        return row, row, 0
    res = g // nqb
    qb = g - res * nqb
    q0 = qb * ATT_BQ
    ks = jnp.clip(q0 - SIDE, 0, seq_len - kw)
    edge = jnp.where(qb == 0, 0, jnp.where(qb == nqb - 1, 2, 1))
    return (pl.multiple_of(res * seq_len + q0, ATT_BQ), pl.multiple_of(res * seq_len + ks, SIDE), edge)


def _qkv_specs(dc, da, t, index):
    return [pl.BlockSpec((t, PAIR), functools.partial(index, (4 * dc + comp * da) // PAIR)) for comp in range(3)]


def _attn_fwd(proj, slopes, dc, da):
    t = proj.shape[0]
    hp = da // PAIR
    n_blocks = t // ATT_BQ

    def body(q_ref, k_ref, v_ref, sl_ref, o_ref, lse_ref, stage, dil, bias, o_res, l_res, o_tok, l_tok):
        for b, (_, r) in enumerate(BRANCHES):
            seq_len = t // r
            kw = min(ATT_KW, seq_len)
            ops = _branch_operands([q_ref, k_ref, v_ref], stage, dil, r)
            _fill_bias_tiles(bias, sl_ref, r, kw)
            o_dst, l_dst = (o_tok.at[b], l_tok.at[b]) if r == 1 else (o_res, l_res)
            first = _first_head_lanes()

            def blocks(trip, carry, seq_len=seq_len, kw=kw, o_dst=o_dst, l_dst=l_dst, first=first, ops=ops):
                nt = (((1,), (1,)), ((), ()))
                places = [_block_place(trip * ATT_UNROLL + i, seq_len, kw) for i in range(ATT_UNROLL)]
                chains = [(i, hh) for i in range(ATT_UNROLL) for hh in range(2)]
                qs = [_scaled_queries(ops[0][pl.ds(qrow, ATT_BQ), :]) for qrow, _, _ in places]
                ks = [ops[1][pl.ds(krow, kw), :] for _, krow, _ in places]
                vs = [ops[2][pl.ds(krow, kw), :] for _, krow, _ in places]
                ss = [lax.dot_general(_only_head(qs[i], first, hh), ks[i], nt, preferred_element_type=F32)
                      + bias[hh, places[i][2], :, 0:kw] for i, hh in chains]
                tops = [jnp.max(s, axis=-1, keepdims=True) for s in ss]
                ps = [jnp.exp(s - m) for s, m in zip(ss, tops)]
                dens = [jnp.sum(p, axis=-1, keepdims=True) for p in ps]
                for i, (qrow, _, _) in enumerate(places):
                    weights = jnp.concatenate([ps[2 * i].astype(BF16), ps[2 * i + 1].astype(BF16)], axis=1)
                    values = jnp.concatenate([_only_head(vs[i], first, 0), _only_head(vs[i], first, 1)], axis=0)
                    den = jnp.where(first, dens[2 * i], dens[2 * i + 1])
                    o_dst[pl.ds(qrow, ATT_BQ), :] = jnp.dot(weights, values, preferred_element_type=F32) / den
                    l_dst[pl.ds(qrow, ATT_BQ), :] = jnp.where(first, tops[2 * i], tops[2 * i + 1]) + jnp.log(den)
                return carry

            lax.fori_loop(0, n_blocks // ATT_UNROLL, blocks, 0)
            if r > 1:
                for res in range(r):
                    rows = slice(res * seq_len, (res + 1) * seq_len)
                    o_tok[b, pl.ds(res, seq_len, stride=r), :] = o_res[rows, :]
                    l_tok[b, pl.ds(res, seq_len, stride=r), :] = l_res[rows, :]

        def merge(i, carry):
            rows = pl.ds(pl.multiple_of(i * ROW_TILE, ROW_TILE), ROW_TILE)
            la, lb, lc = l_tok[0, rows, :], l_tok[1, rows, :], l_tok[2, rows, :]
            m = jnp.maximum(jnp.maximum(la, lb), lc)
            wa, wb, wc = jnp.exp(la - m), jnp.exp(lb - m), jnp.exp(lc - m)
            den = wa + wb + wc
            o_ref[rows, :] = (wa * o_tok[0, rows, :] + wb * o_tok[1, rows, :] + wc * o_tok[2, rows, :]) * (1.0 / den)
            lse_ref[rows, :] = m + jnp.log(den)
            return carry

        lax.fori_loop(0, t // ROW_TILE, merge, 0)

    pair_spec = pl.BlockSpec((None, t, PAIR), lambda h: (h, 0, 0))
    return pl.pallas_call(
        body, name="attn_fwd", grid=(hp,),
        out_shape=(jax.ShapeDtypeStruct((hp, t, PAIR), F32), jax.ShapeDtypeStruct((hp, t, PAIR), F32)),
        in_specs=_qkv_specs(dc, da, t, lambda first, h: (0, first + h))
        + [pl.BlockSpec((None, 8, ATT_KW), lambda h: (h, 0, 0))],
        out_specs=(pair_spec, pair_spec),
        scratch_shapes=[pltpu.VMEM((t, PAIR), F32), pltpu.VMEM((3, t, PAIR), BF16),
                        pltpu.VMEM((2, 3, ATT_BQ, ATT_KW), F32),
                        pltpu.VMEM((t, PAIR), F32), pltpu.VMEM((t, PAIR), F32),
                        pltpu.VMEM((3, t, PAIR), F32), pltpu.VMEM((3, t, PAIR), F32)],
        compiler_params=_params(("parallel",)),
    )(proj, proj, proj, slopes)


def _attn_bwd(dproj, proj, d_o, lse, delta, slopes, dc, da, after):
    t = proj.shape[0]
    hp = da // PAIR
    n_blocks = t // ATT_BQ

    def all_branches(q_ref, k_ref, v_ref, do_ref, lse_ref, dl_ref, sl_ref,
                     stage, dil, packed, packed_res, row_vecs, bias_t, acc, tot):
        first = _first_head_lanes()
        lane = lax.broadcasted_iota(jnp.int32, (1, PAIR), 1)
        packed[...] = jnp.where((lane & (HEAD_DIM - 1)) < HEAD_DIM // 2, lse_ref[...], dl_ref[...])
        for b, (_, r) in enumerate(BRANCHES):
            seq_len = t // r
            kw = min(ATT_KW, seq_len)
            ops = _branch_operands([q_ref, k_ref, v_ref, do_ref], stage, dil, r)
            scalars = packed
            if r > 1:
                _to_residue_major(packed, packed_res, r)
                scalars = packed_res
            for g in range(n_blocks):
                flipped = scalars[g * ATT_BQ:(g + 1) * ATT_BQ, :].T
                for row in range(4):
                    row_vecs[g, row:row + 1, :] = flipped[row * (HEAD_DIM // 2):row * (HEAD_DIM // 2) + 1, :]
            _fill_stacked_bias_tiles(bias_t, sl_ref, r, kw)
            acc[1] = jnp.zeros((t, PAIR), F32)
            acc[2] = jnp.zeros((t, PAIR), F32)

            def blocks(trip, carry, seq_len=seq_len, kw=kw, ops=ops):
                nt = (((1,), (1,)), ((), ()))
                group = range(ATT_UNROLL)
                places = [_block_place(trip * ATT_UNROLL + i, seq_len, kw) for i in group]
                ks, vs, q2s, do2s, lse2s, dl2s = [], [], [], [], [], []
                for i, (qrow, krow, _) in zip(group, places):
                    q = _scaled_queries(ops[0][pl.ds(qrow, ATT_BQ), :])
                    dov = ops[3][pl.ds(qrow, ATT_BQ), :]
                    ks.append(ops[1][pl.ds(krow, kw), :])
                    vs.append(ops[2][pl.ds(krow, kw), :])
                    q2s.append(jnp.concatenate([_only_head(q, first, 0), _only_head(q, first, 1)], axis=0))
                    do2s.append(jnp.concatenate([_only_head(dov, first, 0), _only_head(dov, first, 1)], axis=0))
                    rows = row_vecs[trip * ATT_UNROLL + i]
                    lse2s.append(jnp.concatenate([rows[0:1, :], rows[2:3, :]], axis=1))
                    dl2s.append(jnp.concatenate([rows[1:2, :], rows[3:4, :]], axis=1))
                s_ts = [lax.dot_general(ks[i], q2s[i], nt, preferred_element_type=F32) for i in group]
                dp_ts = [lax.dot_general(vs[i], do2s[i], nt, preferred_element_type=F32) for i in group]
                p_ts = [jnp.exp(s_ts[i] + bias_t[places[i][2], 0:kw, :] - lse2s[i]) for i in group]
                ds_ts = [p_ts[i] * (dp_ts[i] - dl2s[i]) for i in group]
                dvs = [jnp.dot(p_ts[i].astype(BF16), do2s[i], preferred_element_type=F32) for i in group]
                dks = [jnp.dot(ds_ts[i].astype(BF16), q2s[i], preferred_element_type=F32) for i in group]
                dss = [ds_ts[i].T.astype(BF16) for i in group]
                dqs = [jnp.dot(dss[i][0:ATT_BQ, :], _only_head(ks[i], first, 0), preferred_element_type=F32)
                       + jnp.dot(dss[i][ATT_BQ:2 * ATT_BQ, :], _only_head(ks[i], first, 1), preferred_element_type=F32)
                       for i in group]
                for i, (qrow, krow, _) in zip(group, places):
                    acc[0, pl.ds(qrow, ATT_BQ), :] = dqs[i] * (HEAD_DIM ** -0.5)
                    acc[1, pl.ds(krow, kw), :] += dks[i]
                    acc[2, pl.ds(krow, kw), :] += dvs[i]
                return carry

            lax.fori_loop(0, n_blocks // ATT_UNROLL, blocks, 0)
            for comp in range(3):
                if r == 1:
                    tot[comp] = acc[comp]
                else:
                    for res in range(r):
                        tok = pl.ds(res, seq_len, stride=r)
                        tot[comp, tok, :] = tot[comp, tok, :] + acc[comp, res * seq_len:(res + 1) * seq_len, :]

    first_q = (4 * dc) // PAIR

    def body(dproj_in, q_ref, k_ref, v_ref, do_ref, lse_ref, dl_ref, sl_ref, after_ref, out_ref, *scratch):
        del dproj_in, after_ref
        work, out_stage, out_sems = scratch[:-2], scratch[-2], scratch[-1]
        h = pl.program_id(0)
        all_branches(q_ref, k_ref, v_ref, do_ref, lse_ref, dl_ref, sl_ref, *work)

        def out_copy(comp):
            cols = pl.ds(pl.multiple_of((first_q + comp * hp + h) * PAIR, PAIR), PAIR)
            return pltpu.make_async_copy(out_stage.at[comp], out_ref.at[:, cols], out_sems.at[comp])

        @pl.when(h > 0)
        def _():
            for comp in range(3):
                out_copy(comp).wait()

        for comp in range(3):
            out_stage[comp] = work[-1][comp].astype(BF16)
            out_copy(comp).start()

        @pl.when(h == hp - 1)
        def _():
            for comp in range(3):
                out_copy(comp).wait()

    pair_spec = pl.BlockSpec((None, t, PAIR), lambda h: (h, 0, 0))
    return pl.pallas_call(
        body, name="attn_bwd", grid=(hp,),
        out_shape=jax.ShapeDtypeStruct(dproj.shape, BF16),
        in_specs=[HBM] + _qkv_specs(dc, da, t, lambda first, h: (0, first + h))
        + [pair_spec, pair_spec, pair_spec, pl.BlockSpec((None, 8, ATT_KW), lambda h: (h, 0, 0)), ANY],
        out_specs=ANY,
        input_output_aliases={0: 0},
        scratch_shapes=[pltpu.VMEM((t, PAIR), F32), pltpu.VMEM((4, t, PAIR), BF16),
                        pltpu.VMEM((t, PAIR), F32), pltpu.VMEM((t, PAIR), F32),
                        pltpu.VMEM((n_blocks, 8, ATT_BQ), F32), pltpu.VMEM((3, ATT_KW, 2 * ATT_BQ), F32),
                        pltpu.VMEM((3, t, PAIR), F32), pltpu.VMEM((3, t, PAIR), F32),
                        pltpu.VMEM((3, t, PAIR), BF16), pltpu.SemaphoreType.DMA((3,))],
        compiler_params=_params(("arbitrary",)),
    )(dproj, proj, proj, proj, d_o, lse, delta, slopes, after)


def _mix_fwd(co, proj, o_mix, g_conv, g_attn_pairs):
    t, dc = co.shape
    hp = o_mix.shape[0]
    da = hp * PAIR
    tb = ROW_TILE

    def body(co_ref, bg_ref, zc_ref, za_ref, om_ref, gc_ref, ga_ref, ycat_ref, ycatt_ref):
        p = bg_ref[...].astype(F32) * co_ref[...].astype(F32)
        rc = lax.rsqrt(jnp.mean(p * p, axis=-1, keepdims=True) + EPS)
        yc = (p * rc) * gc_ref[...] * _silu(zc_ref[...].astype(F32))
        ycat_ref[:, 0:dc] = yc.astype(BF16)
        ycatt_ref[0:dc, :] = yc.T.astype(BF16)
        ssq = jnp.zeros((tb, 1), F32)
        for h in range(hp):
            o = om_ref[h]
            ssq = ssq + jnp.sum(o * o, axis=-1, keepdims=True)
        ra = lax.rsqrt(ssq * (1.0 / da) + EPS)
        for h in range(hp):
            ya = (om_ref[h] * ra) * ga_ref[h] * _silu(za_ref[:, h * PAIR:(h + 1) * PAIR].astype(F32))
            ycat_ref[:, dc + h * PAIR:dc + (h + 1) * PAIR] = ya.astype(BF16)
            ycatt_ref[dc + h * PAIR:dc + (h + 1) * PAIR, :] = ya.T.astype(BF16)

    pair_spec = pl.BlockSpec((hp, tb, PAIR), lambda i: (0, i, 0))
    return pl.pallas_call(
        body, name="mix_fwd", grid=(t // tb,),
        out_shape=(jax.ShapeDtypeStruct((t, dc + da), BF16), jax.ShapeDtypeStruct((dc + da, t), BF16)),
        in_specs=[pl.BlockSpec((tb, dc), lambda i: (i, 0)),
                  pl.BlockSpec((tb, dc), lambda i: (i, 1)),
                  pl.BlockSpec((tb, dc), lambda i: (i, 3)),
                  pl.BlockSpec((tb, da), lambda i: (i, 7)),
                  pair_spec,
                  pl.BlockSpec((1, dc), lambda i: (0, 0)),
                  pl.BlockSpec((hp, 1, PAIR), lambda i: (0, 0, 0))],
        out_specs=(pl.BlockSpec((tb, dc + da), lambda i: (i, 0)), pl.BlockSpec((dc + da, tb), lambda i: (0, i))),
        compiler_params=_params(("parallel",)),
    )(co, proj, proj, proj, o_mix, g_conv, g_attn_pairs)


def _out_fwd_bwd(ycat, woutf, x, target, mod, g_post):
    t, d = x.shape
    n = ycat.shape[1]
    tb = ROW_TILE

    def body(a_ref, w_ref, x_ref, tg_ref, mod_ref, g_ref, dout_ref, dy_ref, acc_ref):
        y = jnp.dot(a_ref[...], w_ref[...], preferred_element_type=F32)
        r = lax.rsqrt(jnp.mean(y * y, axis=-1, keepdims=True) + EPS)
        nh = y * r
        gate = mod_ref[2:3, :]
        nrm = nh * g_ref[...]
        err = x_ref[...] + gate * nrm - tg_ref[...]
        dout = err * (1.0 / d)
        dout_ref[...] = dout.astype(BF16)
        dn = dout * gate
        a = dn * g_ref[...]
        dy = r * (a - nh * jnp.mean(a * nh, axis=-1, keepdims=True))
        dy_ref[...] = dy.astype(BF16)
        loss = 0.5 * jnp.sum(jnp.sum(err * err, axis=-1, keepdims=True) * (1.0 / d), axis=0, keepdims=True)
        part = jnp.concatenate(
            [jnp.sum(dout * nrm, axis=0, keepdims=True), jnp.sum(dn * nh, axis=0, keepdims=True),
             jnp.broadcast_to(loss, (1, d)), jnp.zeros((5, d), F32)], axis=0)

        @pl.when(pl.program_id(0) == 0)
        def _():
            acc_ref[...] = jnp.zeros(acc_ref.shape, F32)

        acc_ref[...] += part

    return pl.pallas_call(
        body, name="out_fwd_bwd", grid=(t // tb,),
        out_shape=(jax.ShapeDtypeStruct((t, d), BF16), jax.ShapeDtypeStruct((t, d), BF16),
                   jax.ShapeDtypeStruct((8, d), F32)),
        in_specs=[pl.BlockSpec((tb, n), lambda i: (i, 0)), pl.BlockSpec((n, d), lambda i: (0, 0)),
                  pl.BlockSpec((tb, d), lambda i: (i, 0)), pl.BlockSpec((tb, d), lambda i: (i, 0)),
                  pl.BlockSpec((3, d), lambda i: (0, 0)), pl.BlockSpec((1, d), lambda i: (0, 0))],
        out_specs=(pl.BlockSpec((tb, d), lambda i: (i, 0)), pl.BlockSpec((tb, d), lambda i: (i, 0)),
                   pl.BlockSpec((8, d), lambda i: (0, 0))),
        compiler_params=_params(("arbitrary",)),
    )(ycat, woutf, x, target, mod, g_post)


def _matmul_nt(a, b, out_dtype, name):
    m, k = a.shape
    n = b.shape[0]
    tn = COL_TILE

    def body(a_ref, b_ref, o_ref):
        o_ref[...] = lax.dot_general(a_ref[...], b_ref[...], (((1,), (1,)), ((), ())),
                                     preferred_element_type=F32).astype(out_dtype)

    return pl.pallas_call(
        body, name=name, grid=(n // tn,),
        out_shape=jax.ShapeDtypeStruct((m, n), out_dtype),
        in_specs=[pl.BlockSpec((m, k), lambda i: (0, 0)), pl.BlockSpec((tn, k), lambda i: (i, 0))],
        out_specs=pl.BlockSpec((m, tn), lambda i: (0, i)),
        compiler_params=_params(("parallel",)),
    )(a, b)


def _mix_bwd(dycat, co, proj, o_mix, g_conv, g_attn_pairs):
    t, dc = co.shape
    hp = o_mix.shape[0]
    da = hp * PAIR
    tb = ROW_TILE

    def body(dy_ref, co_ref, bg_ref, zc_ref, za_ref, om_ref, gc_ref, ga_ref,
             dcp_ref, dco_ref, do_ref, dl_ref, dgc_ref, dga_ref):
        first = pl.program_id(0) == 0
        cov = co_ref[...].astype(F32)
        bg = bg_ref[...].astype(F32)
        zc = zc_ref[...].astype(F32)
        p = bg * cov
        rc = lax.rsqrt(jnp.mean(p * p, axis=-1, keepdims=True) + EPS)
        nh = p * rc
        dyc = dy_ref[:, 0:dc].astype(F32)
        dn = dyc * _silu(zc)
        a = dn * gc_ref[...]
        dp = rc * (a - nh * jnp.mean(a * nh, axis=-1, keepdims=True))
        dcp_ref[:, 0:dc] = jnp.zeros((tb, dc), BF16)
        dcp_ref[:, dc:2 * dc] = (dp * cov).astype(BF16)
        dcp_ref[:, 2 * dc:3 * dc] = jnp.zeros((tb, dc), BF16)
        dcp_ref[:, 3 * dc:4 * dc] = (dyc * nh * gc_ref[...] * _silu_grad(zc)).astype(BF16)
        dcp_ref[:, 4 * dc:4 * dc + 3 * da] = jnp.zeros((tb, 3 * da), BF16)
        dco_ref[...] = dp * bg

        @pl.when(first)
        def _():
            dgc_ref[...] = jnp.zeros(dgc_ref.shape, F32)
            dga_ref[...] = jnp.zeros(dga_ref.shape, F32)

        dgc_ref[...] += jnp.sum(dn * nh, axis=0, keepdims=True)

        ssq = jnp.zeros((tb, 1), F32)
        for h in range(hp):
            o = om_ref[h]
            ssq = ssq + jnp.sum(o * o, axis=-1, keepdims=True)
        ra = lax.rsqrt(ssq * (1.0 / da) + EPS)
        dot_an = jnp.zeros((tb, 1), F32)
        for h in range(hp):
            nha = om_ref[h] * ra
            za = za_ref[:, h * PAIR:(h + 1) * PAIR].astype(F32)
            dya = dy_ref[:, dc + h * PAIR:dc + (h + 1) * PAIR].astype(F32)
            dna = dya * _silu(za)
            dza = (dya * nha * ga_ref[h] * _silu_grad(za)).astype(BF16)
            dcp_ref[:, 4 * dc + 3 * da + h * PAIR:4 * dc + 3 * da + (h + 1) * PAIR] = dza
            dga_ref[h] += jnp.sum(dna * nha, axis=0, keepdims=True)
            dot_an = dot_an + jnp.sum(dna * ga_ref[h] * nha, axis=-1, keepdims=True)
        mean_an = dot_an * (1.0 / da)
        first_head = lax.broadcasted_iota(jnp.int32, (tb, PAIR), 1) < HEAD_DIM
        for h in range(hp):
            o = om_ref[h]
            nha = o * ra
            za = za_ref[:, h * PAIR:(h + 1) * PAIR].astype(F32)
            dya = dy_ref[:, dc + h * PAIR:dc + (h + 1) * PAIR].astype(F32)
            aa = dya * _silu(za) * ga_ref[h]
            d_o = ra * (aa - nha * mean_an)
            do_ref[h] = d_o.astype(BF16)
            prod = d_o * o
            both = jnp.sum(prod, axis=-1, keepdims=True)
            head0 = jnp.sum(jnp.where(first_head, prod, 0.0), axis=-1, keepdims=True)
            dl_ref[h] = jnp.where(first_head, head0, both - head0)

    pair_spec = pl.BlockSpec((hp, tb, PAIR), lambda i: (0, i, 0))
    return pl.pallas_call(
        body, name="mix_bwd", grid=(t // tb,),
        out_shape=(jax.ShapeDtypeStruct((t, 4 * dc + 4 * da), BF16), jax.ShapeDtypeStruct((t, dc), F32),
                   jax.ShapeDtypeStruct((hp, t, PAIR), BF16), jax.ShapeDtypeStruct((hp, t, PAIR), F32),
                   jax.ShapeDtypeStruct((1, dc), F32), jax.ShapeDtypeStruct((hp, 1, PAIR), F32)),
        in_specs=[pl.BlockSpec((tb, dc + da), lambda i: (i, 0)),
                  pl.BlockSpec((tb, dc), lambda i: (i, 0)),
                  pl.BlockSpec((tb, dc), lambda i: (i, 1)),
                  pl.BlockSpec((tb, dc), lambda i: (i, 3)),
                  pl.BlockSpec((tb, da), lambda i: (i, 7)),
                  pair_spec,
                  pl.BlockSpec((1, dc), lambda i: (0, 0)),
                  pl.BlockSpec((hp, 1, PAIR), lambda i: (0, 0, 0))],
        out_specs=(pl.BlockSpec((tb, 4 * dc + 4 * da), lambda i: (i, 0)), pl.BlockSpec((tb, dc), lambda i: (i, 0)),
                   pair_spec, pair_spec,
                   pl.BlockSpec((1, dc), lambda i: (0, 0)), pl.BlockSpec((hp, 1, PAIR), lambda i: (0, 0, 0))),
        compiler_params=_params(("arbitrary",)),
    )(dycat, co, proj, proj, proj, o_mix, g_conv, g_attn_pairs)


def _conv_bwd(dconv_proj, dco, conv_proj, conv_w, dc, after):
    t = dco.shape[0]
    ct = CONV_TILE
    nct = dc // ct

    def body(dcp_in_ref, dco_ref, u_ref, cg_ref, w_ref, after_ref, dcp_ref, acc_ref):
        del dcp_in_ref, after_ref
        which = pl.program_id(1)
        g = dco_ref[...]
        u = u_ref[...].astype(F32)
        cg = cg_ref[...].astype(F32)
        g_prev, g_next = _shift_rows(g, t)
        da = w_ref[0:1, :] * g_next + w_ref[1:2, :] * g + w_ref[2:3, :] * g_prev
        dcp_ref[...] = (da * jnp.where(which == 0, cg, u)).astype(BF16)
        a = cg * u
        a_prev, a_next = _shift_rows(a, t)
        acc_ref[...] = jnp.concatenate(
            [jnp.sum(g * a_prev, axis=0, keepdims=True), jnp.sum(g * a, axis=0, keepdims=True),
             jnp.sum(g * a_next, axis=0, keepdims=True), jnp.sum(g, axis=0, keepdims=True),
             jnp.zeros((4, ct), F32)], axis=0)

    return pl.pallas_call(
        body, name="conv_bwd", grid=(nct, 2),
        out_shape=(jax.ShapeDtypeStruct(dconv_proj.shape, BF16), jax.ShapeDtypeStruct((8, dc), F32)),
        in_specs=[HBM,
                  pl.BlockSpec((t, ct), lambda i, s: (0, i)),
                  pl.BlockSpec((t, ct), lambda i, s: (0, i)),
                  pl.BlockSpec((t, ct), lambda i, s: (0, 2 * nct + i)),
                  pl.BlockSpec((3, ct), lambda i, s: (0, i)), ANY],
        out_specs=(pl.BlockSpec((t, ct), lambda i, s: (0, 2 * s * nct + i)),
                   pl.BlockSpec((8, ct), lambda i, s: (0, i))),
        input_output_aliases={0: 0},
        compiler_params=_params(("arbitrary", "arbitrary")),
    )(dconv_proj, dco, conv_proj, conv_proj, conv_w, after)


def _dh(dproj, winf, after):
    t = dproj.shape[0]
    _, d, ws = winf.shape
    tm = tn = COL_TILE
    nt = (((1,), (1,)), ((), ()))

    def body(a_ref, w_ref, after_ref, o_ref):
        del after_ref
        acc = lax.dot_general(a_ref[:, 0:ws], w_ref[0], nt, preferred_element_type=F32)
        for j in range(1, N_CHIPS):
            acc = acc + lax.dot_general(a_ref[:, j * ws:(j + 1) * ws], w_ref[j], nt, preferred_element_type=F32)
        o_ref[...] = acc.astype(BF16)

    return pl.pallas_call(
        body, name="dh", grid=(d // tn, t // tm),
        out_shape=jax.ShapeDtypeStruct((t, d), BF16),
        in_specs=[pl.BlockSpec((tm, N_CHIPS * ws), lambda n, m: (m, 0)),
                  pl.BlockSpec((N_CHIPS, tn, ws), lambda n, m: (0, n, 0)), ANY],
        out_specs=pl.BlockSpec((tm, tn), lambda n, m: (m, n)),
        compiler_params=_params(("parallel", "parallel")),
    )(dproj, winf, after)


def _prenorm_bwd(x, dh, dout, mod, g_pre):
    t, d = x.shape
    tb = ROW_TILE

    def body(x_ref, dh_ref, dout_ref, mod_ref, g_ref, gx_ref, acc_ref):
        xv = x_ref[...]
        dhv = dh_ref[...].astype(F32)
        r = lax.rsqrt(jnp.mean(xv * xv, axis=-1, keepdims=True) + EPS)
        xh = xv * r
        one_scale = 1.0 + mod_ref[1:2, :]
        a = dhv * one_scale * g_ref[...]
        gx_ref[...] = dout_ref[...].astype(F32) + r * (a - xh * jnp.mean(a * xh, axis=-1, keepdims=True))
        part = jnp.concatenate(
            [jnp.sum(dhv, axis=0, keepdims=True), jnp.sum(dhv * xh * g_ref[...], axis=0, keepdims=True),
             jnp.sum(dhv * xh * one_scale, axis=0, keepdims=True), jnp.zeros((5, d), F32)], axis=0)

        @pl.when(pl.program_id(0) == 0)
        def _():
            acc_ref[...] = jnp.zeros(acc_ref.shape, F32)

        acc_ref[...] += part

    return pl.pallas_call(
        body, name="prenorm_bwd", grid=(t // tb,),
        out_shape=(jax.ShapeDtypeStruct((t, d), F32), jax.ShapeDtypeStruct((8, d), F32)),
        in_specs=[pl.BlockSpec((tb, d), lambda i: (i, 0)), pl.BlockSpec((tb, d), lambda i: (i, 0)),
                  pl.BlockSpec((tb, d), lambda i: (i, 0)), pl.BlockSpec((3, d), lambda i: (0, 0)),
                  pl.BlockSpec((1, d), lambda i: (0, 0))],
        out_specs=(pl.BlockSpec((tb, d), lambda i: (i, 0)), pl.BlockSpec((8, d), lambda i: (0, 0))),
        compiler_params=_params(("arbitrary",)),
    )(x, dh, dout, mod, g_pre)


def _chip_sums(mine, rsib, name, part=0, parts=1, after=()):
    _, half, cols = mine.shape
    rows = half // parts
    tr = min(rows, ROW_TILE)
    nt = rows // tr

    def body(g_ref, r_ref, *rest):
        rest[-1][...] = (g_ref[...].astype(F32) + r_ref[...].astype(F32)).astype(BF16)

    spec = pl.BlockSpec((None, tr, cols), lambda j, i: (j, part * nt + i, 0))
    return pl.pallas_call(
        body, name=name, grid=(N_CHIPS, nt),
        out_shape=jax.ShapeDtypeStruct((N_CHIPS, rows, cols), BF16),
        in_specs=[spec, spec] + [ANY] * len(after), out_specs=pl.BlockSpec((None, tr, cols), lambda j, i: (j, i, 0)),
        compiler_params=_params(("parallel", "parallel")),
    )(mine, rsib, *after)


def _owner_sum(place, mine, rsib, rici, name, part=0, parts=1):
    _, half, cols = mine.shape
    rows = half // parts
    tr = min(rows, ROW_TILE)
    nt = rows // tr

    def body(place_ref, g_ref, r_ref, i_ref, o_ref):
        del place_ref
        acc = g_ref[...].astype(F32) + r_ref[...].astype(F32)
        for k in range(N_CHIPS - 1):
            acc = acc + i_ref[k].astype(F32)
        o_ref[...] = acc

    own = pl.BlockSpec((None, tr, cols), lambda i, p: (p[0], part * nt + i, 0))
    grid_spec = pltpu.PrefetchScalarGridSpec(
        num_scalar_prefetch=1, grid=(nt,),
        in_specs=[own, own, pl.BlockSpec((N_CHIPS - 1, tr, cols), lambda i, p: (0, i, 0))],
        out_specs=pl.BlockSpec((tr, cols), lambda i, p: (p[1] * (half // tr) + part * nt + i, 0)))
    return pl.pallas_call(
        body, name=name, grid_spec=grid_spec,
        out_shape=jax.ShapeDtypeStruct((2 * half, cols), F32),
        compiler_params=_params(("parallel",)),
    )(place, mine, rsib, rici)


def _adam_math(w, g, m, v):
    m2 = ADAM_B1 * m + (1.0 - ADAM_B1) * g
    v2 = ADAM_B2 * v + (1.0 - ADAM_B2) * (g * g)
    m_hat = m2 / (1.0 - ADAM_B1 ** ADAM_STEP)
    v_hat = v2 / (1.0 - ADAM_B2 ** ADAM_STEP)
    delta = -ADAM_LR * (m_hat / (jnp.sqrt(v_hat) + ADAM_EPS) + ADAM_WD * w)
    return delta, m2, v2


def _adamw(w, g, m, v, name, part=0, parts=1, prev=None):
    rows, cols = w.shape
    tr = min(rows, ROW_TILE)

    def body(*refs):
        w_ref, g_ref, m_ref, v_ref, go_ref, d_ref, m2_ref, v2_ref = refs[-8:]
        g = g_ref[...]
        go_ref[...] = g
        d_ref[...], m2_ref[...], v2_ref[...] = _adam_math(w_ref[...], g, m_ref[...], v_ref[...])

    if parts == 1:
        grid, spec = (rows // tr,), pl.BlockSpec((tr, cols), lambda i: (i, 0))
    else:
        per_half = rows // 2 // tr
        nt = per_half // parts
        grid, spec = (2, nt), pl.BlockSpec((tr, cols), lambda r, i: (r * per_half + part * nt + i, 0))
    olds = [] if prev is None else list(prev)
    return pl.pallas_call(
        body, name=name, grid=grid,
        out_shape=(jax.ShapeDtypeStruct(w.shape, F32),) * 4,
        in_specs=[HBM] * len(olds) + [spec] * 4, out_specs=(spec,) * 4,
        input_output_aliases={i: i for i in range(len(olds))},
        compiler_params=_params(("parallel",) * len(grid)),
    )(*olds, w, g, m, v)


def _ada_grad_adamw(c_all_t, dmod_cols, w, m, v):
    d, wa = w.shape
    tr = ROW_TILE

    def body(ct_ref, dm_ref, w_ref, m_ref, v_ref, g_ref, d_ref, m2_ref, v2_ref):
        act = _silu(ct_ref[...])
        g = act[:, 0:1] * dm_ref[0:1, :]
        for b in range(1, N_DEV):
            g = g + act[:, b:b + 1] * dm_ref[b:b + 1, :]
        g_ref[...] = g
        d_ref[...], m2_ref[...], v2_ref[...] = _adam_math(w_ref[...], g, m_ref[...], v_ref[...])

    spec = pl.BlockSpec((tr, wa), lambda i: (i, 0))
    return pl.pallas_call(
        body, name="ada_grad_adamw", grid=(d // tr,),
        out_shape=(jax.ShapeDtypeStruct(w.shape, F32),) * 4,
        in_specs=[pl.BlockSpec((tr, N_DEV), lambda i: (i, 0)), pl.BlockSpec((N_DEV, wa), lambda i: (0, 0)),
                  spec, spec, spec],
        out_specs=(spec,) * 4,
        compiler_params=_params(("parallel",)),
    )(c_all_t, dmod_cols, w, m, v)


def _small_update(place, gathered, pieces, weights, moments_m, moments_v):
    n = gathered.shape[1]
    k = len(weights)

    def body(place_ref, g_ref, *refs):
        w_refs, m_refs, v_refs = refs[0:k], refs[k:2 * k], refs[2 * k:3 * k]
        outs = refs[3 * k:]
        total = g_ref[0:SUBLANES, :]
        for dev in range(1, N_DEV):
            total = total + g_ref[SUBLANES * dev:SUBLANES * (dev + 1), :]

        def flat(offset, length):
            segments, pos = [], offset
            while pos < offset + length:
                row, col = divmod(pos, n)
                take = min(offset + length - pos, n - col)
                segments.append(total[row:row + 1, col:col + take])
                pos += take
            return jnp.concatenate(segments, axis=1) if len(segments) > 1 else segments[0]

        chip = place_ref[0]
        for i, (w_ref, m_ref, v_ref) in enumerate(zip(w_refs, m_refs, v_refs)):
            g = flat(*pieces[i])
            if w_ref.ndim == 3:
                rows, cols = w_ref.shape[1:]
                full = pieces[i][1] // rows
                picked = []
                for r in range(rows):
                    blocks = [g[:, r * full + q * cols:r * full + (q + 1) * cols] for q in range(N_CHIPS)]
                    mine = blocks[N_CHIPS - 1]
                    for q in range(N_CHIPS - 2, -1, -1):
                        mine = jnp.where(chip == q, blocks[q], mine)
                    picked.append(mine)
                g = jnp.concatenate(picked, axis=0)
                w, m, v = w_ref[0], m_ref[0], v_ref[0]
            else:
                w, m, v = w_ref[...], m_ref[...], v_ref[...]
            delta, m2, v2 = _adam_math(w, g, m, v)
            for j, val in enumerate((g, delta, m2, v2)):
                out = outs[j * k + i]
                if w_ref.ndim == 3:
                    out[0] = val
                else:
                    out[...] = val
        outs[4 * k][...] = flat(*pieces[k])

    shapes = [jax.ShapeDtypeStruct(w.shape, F32) for w in weights]
    grid_spec = pltpu.PrefetchScalarGridSpec(
        num_scalar_prefetch=1, grid=(1,),
        in_specs=[pl.BlockSpec(gathered.shape, lambda i, p: (0, 0))]
        + [pl.BlockSpec(a.shape, functools.partial(lambda nd, i, p: (0,) * nd, a.ndim))
           for a in (*weights, *moments_m, *moments_v)],
        out_specs=tuple(pl.BlockSpec(s.shape, functools.partial(lambda nd, i, p: (0,) * nd, len(s.shape)))
                        for s in shapes * 4) + (pl.BlockSpec((1, LANES), lambda i, p: (0, 0)),))
    outs = pl.pallas_call(
        body, name="small_update", grid_spec=grid_spec,
        out_shape=tuple(shapes * 4) + (jax.ShapeDtypeStruct((1, LANES), F32),),
        compiler_params=_params(("arbitrary",)),
    )(place, gathered, *weights, *moments_m, *moments_v)
    return outs[0:k], outs[k:2 * k], outs[2 * k:3 * k], outs[3 * k:4 * k], outs[4 * k]


def _pack_small(pieces):
    flat = [p.reshape(-1).astype(F32) for p in pieces]
    offsets, total = [], 0
    for p in flat:
        offsets.append(total)
        total += p.shape[0]
    padded = -(-total // SMALL_ALIGN) * SMALL_ALIGN
    if padded > total:
        flat.append(jnp.zeros((padded - total,), F32))
    return jnp.concatenate(flat).reshape(8, padded // 8), offsets


def _alibi_slope_rows(n_heads):
    slopes = 2.0 ** (-8.0 * jnp.arange(1, n_heads + 1, dtype=F32) / n_heads)
    rows = jnp.zeros((n_heads // 2, 8), F32).at[:, 0:2].set(slopes.reshape(n_heads // 2, 2))
    return jnp.broadcast_to(rows[:, :, None], (n_heads // 2, 8, ATT_KW))


def kernel(x, c, w_ada, b_ada, g_pre, w_in, conv_w, conv_b, g_conv, g_attn, w_out, g_post, loss_target, m_w_ada, m_b_ada, m_g_pre, m_w_in, m_conv_w, m_conv_b, m_g_conv, m_g_attn, m_w_out, m_g_post, v_w_ada, v_b_ada, v_g_pre, v_w_in, v_conv_w, v_conv_b, v_g_conv, v_g_attn, v_w_out, v_g_post):
    t, d = x.shape[1], x.shape[2]
    dc = conv_b.shape[1]
    da = g_attn.shape[1]
    hp = da // PAIR
    ws = w_in.shape[2]
    wa = w_ada.shape[2]
    cws = conv_w.shape[2]
    assert t % ROW_TILE == 0 and d % ROW_TILE == 0 and dc % COL_TILE == 0 and da % COL_TILE == 0
    assert ws == 2 * dc and dc == da and t // BRANCHES[-1][1] >= ATT_BQ

    mx, my, mc = _my_place()
    chip = _chip_of(mx, my)
    dev = 2 * chip + mc
    place = jnp.stack([chip, mc]).astype(jnp.int32)

    x2, tgt2 = x[0], loss_target[0]
    w_ada2, w_in2, w_out2 = w_ada[0], w_in[0], w_out[0]

    win_slots = _cast_into_slot(place, w_in2, "cast_w_in")
    packed, offs = _pack_small([c[0], conv_w[0]])
    seen, mod = _ada_modulation(packed, w_ada2, b_ada, d, after=(win_slots,))
    seen = seen.reshape(N_DEV, -1)
    c_all = seen[:, offs[0]:offs[0] + d]
    conv_w_full = seen[0::2, offs[1]:offs[1] + 3 * cws].reshape(N_CHIPS, 3, cws).transpose(1, 0, 2).reshape(3, dc)

    win_flight, send_in, recv_in, started = _gather_start(win_slots, mod)

    y_chip, x_chip, d_chip = (_chip_of(mx, 1 - my), _chip_of(1 - mx, my), _chip_of(1 - mx, 1 - my))
    tiles_per_part = ws // COL_TILE // 2

    def tiles_of(chunk, parts):
        return [(2 * chunk + part) * tiles_per_part + k for part in parts for k in range(tiles_per_part)]

    own_tiles, first_tiles, second_tiles, far_tiles = (jnp.stack(tiles).astype(jnp.int32) for tiles in (
        tiles_of(chip, (0, 1)), tiles_of(y_chip, (0,)) + tiles_of(x_chip, (1,)),
        tiles_of(y_chip, (1,)) + tiles_of(x_chip, (0,)), tiles_of(d_chip, (0, 1))))
    h, ht = _prenorm(x2, mod, g_pre, started)
    proj = _proj_tiles(None, h, w_in2, own_tiles, "proj_own")
    win_flight, wout_flight, relay_send_in, relay_recv_in, send_out, recv_out = _gather_relay_in(
        win_flight, _cast_into_slot(place, w_out2, "cast_w_out"), recv_in, proj)
    win_flight = _forward_halves(win_flight, ((0, 0), (1, 1)), "forward_w_in_first")
    proj = _proj_tiles(proj, h, win_flight, first_tiles, "proj_first_parts")
    win_flight = _forward_halves(
        _gather_wait_direct(win_flight, send_in, recv_in, proj, "gather_wait_w_in_direct"),
        ((0, 1), (1, 0)), "forward_w_in_second")
    proj = _proj_tiles(proj, h, win_flight, second_tiles, "proj_second_parts")
    winf = _forward_halves(
        _gather_wait_relayed(win_flight, relay_send_in, relay_recv_in, proj, "gather_wait_w_in_relayed"),
        ((2, None),), "forward_w_in_relayed")
    proj = _proj_tiles(proj, h, winf, far_tiles, "proj_diagonal")
    slopes = _alibi_slope_rows(da // HEAD_DIM)
    co = _conv_fwd(proj, conv_w_full, conv_b, dc)
    wout_flight, relay_send_out, relay_recv_out = _gather_relay_out(wout_flight, recv_out, co)
    o_mix, lse = _attn_fwd(proj, slopes, dc, da)
    g_attn_pairs = g_attn.reshape(hp, 1, PAIR)
    ycat, ycat_t = _mix_fwd(co, proj, o_mix, g_conv, g_attn_pairs)
    wout_flight = _gather_wait_direct(wout_flight, send_out, recv_out, ycat, "gather_wait_w_out_direct")
    wout_flight = _gather_wait_relayed(wout_flight, relay_send_out, relay_recv_out, ycat, "gather_wait_w_out_relayed")
    woutf = _forward_halves(wout_flight, ((0, None), (1, None), (2, None)), "forward_w_out").reshape(dc + da, d)
    dout, dy, post_sums = _out_fwd_bwd(ycat, woutf, x2, tgt2, mod, g_post)

    gout, rsib_out = _dw_swapped(ycat_t, dy, N_CHIPS, 1, "dw_out")
    csum_out = _chip_sums(gout, rsib_out, "rs_chip_sum_out")
    ssem_out, rsem_out, csum_out, land_out, sent_out = _owners_start(csum_out, "rs_owners_start_out")
    dycat = _matmul_nt(dy, woutf, BF16, "dycat")
    dproj, dco, d_o, delta, dg_conv, dg_attn = _mix_bwd(dycat, co, proj, o_mix, g_conv, g_attn_pairs)
    dproj, conv_sums = _conv_bwd(dproj, dco, proj, conv_w_full, dc, sent_out)
    dproj = _attn_bwd(dproj, proj, d_o, lse, delta, slopes, dc, da, sent_out)
    gin, rsib_in = _dw_swapped(ht, dproj, 1, N_CHIPS, "dw_in")
    ssem_in0, rsem_in0, csum_in0, land_in0, sent_in0 = _owners_start(
        _chip_sums(gin, rsib_in, "rs_chip_sum_in0", 0, 2), "rs_owners_start_in0")
    ssem_in1, rsem_in1, csum_in1, land_in1, sent_in = _owners_start(
        _chip_sums(gin, rsib_in, "rs_chip_sum_in1", 1, 2, after=(sent_in0,)), "rs_owners_start_in1")
    dh = _dh(dproj, winf, sent_in)
    grad_x, pre_sums = _prenorm_bwd(x2, dh, dout, mod, g_pre)

    small, so = _pack_small([
        pre_sums[0], pre_sums[1], post_sums[0],
        pre_sums[2], conv_sums[0:3], conv_sums[3], dg_conv, dg_attn, post_sums[1], post_sums[2, 0:128]])
    ssem_small, rsem_small, small, land_small, sent_small = _allgather8_start(small, dev, "gather_small_start")

    rici_out = _owners_wait(ssem_out, rsem_out, csum_out, land_out, [grad_x, sent_small], "rs_owners_wait_out")
    full_out, jsend_out, jrecv_out, joining_out = _join_start(
        _owner_sum(place, gout, rsib_out, rici_out, "rs_owner_sum_out"), "rs_join_start_out", 0, 1)
    rici_in = _owners_wait(ssem_in0, rsem_in0, csum_in0, land_in0, [joining_out], "rs_owners_wait_in0")
    full_in0, jsend0, jrecv0, joining0 = _join_start(
        _owner_sum(place, gin, rsib_in, rici_in, "rs_owner_sum_in0", 0, 2), "rs_join_start_in0", 0, 2)
    grad_w_out = _join_wait(full_out, jsend_out, jrecv_out, [joining0], "rs_join_wait_out", 0, 1)
    grad_w_out, delta_w_out, new_m_w_out, new_v_w_out = _adamw(
        w_out2, grad_w_out, m_w_out[0], v_w_out[0], "adamw_w_out")
    full_in0 = _join_wait(full_in0, jsend0, jrecv0, [delta_w_out], "rs_join_wait_in0", 0, 2)
    updated_in = _adamw(w_in2, full_in0, m_w_in[0], v_w_in[0], "adamw_w_in0", 0, 2)
    rici_in = _owners_wait(ssem_in1, rsem_in1, csum_in1, land_in1, [updated_in[1]], "rs_owners_wait_in1")
    full_in1, jsend1, jrecv1, joining1 = _join_start(
        _owner_sum(place, gin, rsib_in, rici_in, "rs_owner_sum_in1", 1, 2), "rs_join_start_in1", 1, 2)

    small_seen = _allgather8_wait(ssem_small, rsem_small, small, land_small, [joining1], "gather_small_wait")
    small_w = [b_ada, g_pre, conv_w, conv_b, g_conv, g_attn, g_post]
    small_m = [m_b_ada, m_g_pre, m_conv_w, m_conv_b, m_g_conv, m_g_attn, m_g_post]
    small_v = [v_b_ada, v_g_pre, v_conv_w, v_conv_b, v_g_conv, v_g_attn, v_g_post]
    pieces = [(0, 3 * d), (so[3], d), (so[4], 3 * dc), (so[5], dc), (so[6], dc), (so[7], da), (so[8], d), (so[9], LANES)]
    g_small, d_small, m_small, v_small, loss_row = _small_update(place, small_seen, pieces, small_w, small_m, small_v)
    loss = loss_row[0, 0]
    grad_b_ada, grad_g_pre, grad_conv_w, grad_conv_b, grad_g_conv, grad_g_attn, grad_g_post = g_small
    dmod_cols = lax.dynamic_slice_in_dim(small_seen.reshape(N_DEV, -1), chip * wa, wa, axis=1)
    grad_w_ada, delta_w_ada, new_m_w_ada, new_v_w_ada = _ada_grad_adamw(c_all.T, dmod_cols, w_ada2, m_w_ada[0], v_w_ada[0])

    full_in1 = _join_wait(full_in1, jsend1, jrecv1, [delta_w_ada, d_small[0]], "rs_join_wait_in1", 1, 2)
    grad_w_in, delta_w_in, new_m_w_in, new_v_w_in = _adamw(
        w_in2, full_in1, m_w_in[0], v_w_in[0], "adamw_w_in1", 1, 2, updated_in)

    def lead(a):
        return a.reshape((1,) + a.shape)

    grads = [lead(grad_w_ada), grad_b_ada, grad_g_pre, lead(grad_w_in), grad_conv_w, grad_conv_b, grad_g_conv,
             grad_g_attn, lead(grad_w_out), grad_g_post]
    deltas = [lead(delta_w_ada), d_small[0], d_small[1], lead(delta_w_in), d_small[2], d_small[3], d_small[4],
              d_small[5], lead(delta_w_out), d_small[6]]
    new_ms = [lead(new_m_w_ada), m_small[0], m_small[1], lead(new_m_w_in), m_small[2], m_small[3], m_small[4],
              m_small[5], lead(new_m_w_out), m_small[6]]
    new_vs = [lead(new_v_w_ada), v_small[0], v_small[1], lead(new_v_w_in), v_small[2], v_small[3], v_small[4],
              v_small[5], lead(new_v_w_out), v_small[6]]
    return (loss, lead(grad_x), *grads, *deltas, *new_ms, *new_vs)
```

```python
import functools

import jax
import jax.numpy as jnp
from jax import lax
from jax.experimental import pallas as pl
from jax.experimental.pallas import tpu as pltpu

F32 = jnp.float32
BF16 = jnp.bfloat16
MESH = pl.DeviceIdType.MESH
HBM = pl.BlockSpec(memory_space=pltpu.HBM)
VMEM = pl.BlockSpec(memory_space=pltpu.VMEM)
ANY = pl.BlockSpec(memory_space=pl.ANY)
SEM = pl.BlockSpec(memory_space=pltpu.SEMAPHORE)
EFFECT = pltpu.SideEffectType.DATAFLOW_SIDE_EFFECTING
SUBLANES, LANES = 8, 128
TOKEN = jax.ShapeDtypeStruct((SUBLANES, LANES), jnp.float32)

HEAD_DIM = 64
PAIR = 2 * HEAD_DIM
assert PAIR == LANES
BRANCHES = ((128, 1), (512, 4), (2048, 16))
SIDE = 64
EPS = 1e-6
NEG_INF = -1e30
N_CHIPS = 4
N_DEV = 8

ADAM_LR = 0.001
ADAM_B1 = 0.9
ADAM_B2 = 0.999
ADAM_EPS = 1e-08
ADAM_WD = 0.01
ADAM_STEP = 10

VMEM_LIMIT_BYTES = 56 * 1024 * 1024
ROW_TILE = 256
COL_TILE = 512
CONV_TILE = 256
ATT_BQ = 128
ATT_KW = ATT_BQ + 2 * SIDE
ATT_UNROLL = 4
SMALL_ALIGN = SUBLANES * LANES


def _params(semantics=None):
    kw = {"vmem_limit_bytes": VMEM_LIMIT_BYTES}
    if semantics is not None:
        kw["dimension_semantics"] = semantics
    return pltpu.CompilerParams(**kw)


def _silu(z):
    return z * jax.nn.sigmoid(z)


def _silu_grad(z):
    s = jax.nn.sigmoid(z)
    return s * (1.0 + z * (1.0 - s))


def _my_place():
    return lax.axis_index("x"), lax.axis_index("y"), lax.axis_index("c")


def _flip(a, bit):
    return 1 - a if bit else a


def _chip_of(x, y):
    return 2 * x + y


def _allgather8_start(v, me, name):
    rows_per, n = v.shape
    land = lax.dynamic_update_slice(jnp.zeros((N_DEV * rows_per, n), v.dtype), v, (me * rows_per, 0))

    def body(v_ref, land_ref, send_sems, recv_sems, v_thru, land_thru, token_ref):
        del v_thru, land_thru
        x, y, c = _my_place()
        mine = land_ref.at[pl.ds(pl.multiple_of((4 * x + 2 * y + c) * rows_per, rows_per), rows_per), :]
        for k in range(1, N_DEV):
            peer = (_flip(x, k & 4), _flip(y, k & 2), _flip(c, k & 1))
            pltpu.make_async_remote_copy(
                src_ref=v_ref, dst_ref=mine, send_sem=send_sems.at[k - 1], recv_sem=recv_sems.at[k - 1],
                device_id=peer, device_id_type=MESH).start()
        token_ref[...] = jnp.zeros(token_ref.shape, F32)

    sems = pltpu.SemaphoreType.DMA((N_DEV - 1,))
    return pl.pallas_call(
        body, name=name,
        out_shape=(sems, sems, jax.ShapeDtypeStruct(v.shape, v.dtype), jax.ShapeDtypeStruct(land.shape, land.dtype), TOKEN),
        in_specs=[HBM, HBM], out_specs=(SEM, SEM, HBM, HBM, VMEM),
        input_output_aliases={0: 2, 1: 3},
        compiler_params=pltpu.CompilerParams(has_side_effects=EFFECT),
    )(pltpu.with_memory_space_constraint(v, pltpu.HBM), pltpu.with_memory_space_constraint(land, pltpu.HBM))


def _allgather8_wait(send_sems, recv_sems, v, land, after, name):
    rows_per = v.shape[0]

    def body(v_ref, land_ref, send_ref, recv_ref, *rest):
        del rest
        x, y, c = _my_place()
        for k in range(1, N_DEV):
            peer = (_flip(x, k & 4), _flip(y, k & 2), _flip(c, k & 1))
            src = 4 * peer[0] + 2 * peer[1] + peer[2]
            cp = pltpu.make_async_remote_copy(
                src_ref=v_ref, dst_ref=land_ref.at[pl.ds(pl.multiple_of(src * rows_per, rows_per), rows_per), :],
                send_sem=send_ref.at[k - 1], recv_sem=recv_ref.at[k - 1], device_id=peer, device_id_type=MESH)
            cp.wait_send()
            cp.wait_recv()

    return pl.pallas_call(
        body, name=name,
        out_shape=(jax.ShapeDtypeStruct(v.shape, v.dtype), jax.ShapeDtypeStruct(land.shape, land.dtype)),
        in_specs=[HBM, HBM, SEM, SEM] + [ANY] * len(after), out_specs=(HBM, HBM),
        input_output_aliases={0: 0, 1: 1},
        compiler_params=pltpu.CompilerParams(has_side_effects=EFFECT),
    )(v, land, send_sems, recv_sems, *after)[1]


def _half_rows(ref, chip, which, half):
    return ref.at[chip, pl.ds(pl.multiple_of(which * half, half), half), :]


def _ici_peers(x, y, c):
    peers = [(_flip(x, k & 2), _flip(y, k & 1), c) for k in (1, 2, 3)]
    return [(peer, _chip_of(peer[0], peer[1])) for peer in peers]


def _part_of_half(ref, chip, core, part):
    half, cols = ref.shape[1] // 2, ref.shape[2] // 2
    return ref.at[chip, pl.ds(pl.multiple_of(core * half, half), half), pl.ds(part * cols, cols)]


def _neighbours(x, y, c):
    return [((x, 1 - y, c), _chip_of(x, 1 - y)), ((1 - x, y, c), _chip_of(1 - x, y)),
            ((1 - x, 1 - y, c), _chip_of(1 - x, 1 - y))]


def _start_direct(buf, send_sems, recv_sems):
    x, y, c = _my_place()
    me = _chip_of(x, y)
    for n, (peer, _) in enumerate(_neighbours(x, y, c)[0:2]):
        for part in ((0, 1), (1, 0))[n]:
            piece = _part_of_half(buf, me, c, part)
            pltpu.make_async_remote_copy(
                src_ref=piece, dst_ref=piece, send_sem=send_sems.at[2 * n + part], recv_sem=recv_sems.at[2 * n + part],
                device_id=peer, device_id_type=MESH).start()


def _relay(buf, recv_sems, relay_send, relay_recv):
    x, y, c = _my_place()
    nbrs = _neighbours(x, y, c)
    for n in range(2):
        part = n
        piece = _part_of_half(buf, nbrs[n][1], c, part)
        pltpu.make_async_remote_copy(
            src_ref=piece, dst_ref=piece, send_sem=relay_send.at[part], recv_sem=recv_sems.at[2 * n + part],
            device_id=nbrs[n][0], device_id_type=MESH).wait_recv()
        pltpu.make_async_remote_copy(
            src_ref=piece, dst_ref=piece, send_sem=relay_send.at[part], recv_sem=relay_recv.at[part],
            device_id=nbrs[1 - n][0], device_id_type=MESH).start()


def _gather_start(win_slots, after):
    def body(win_in, after_ref, win_ref, send_sems, recv_sems, token_ref):
        del win_in, after_ref
        _start_direct(win_ref, send_sems, recv_sems)
        token_ref[...] = jnp.zeros(token_ref.shape, F32)

    sems = pltpu.SemaphoreType.DMA((4,))
    return pl.pallas_call(
        body, name="gather_start",
        out_shape=(jax.ShapeDtypeStruct(win_slots.shape, win_slots.dtype), sems, sems, TOKEN),
        in_specs=[HBM, ANY], out_specs=(HBM, SEM, SEM, VMEM),
        input_output_aliases={0: 0},
        compiler_params=pltpu.CompilerParams(has_side_effects=EFFECT),
    )(win_slots, after)


def _gather_relay_in(win, wout_slots, recv_in, after):
    def body(win_in, wout_in, recv_in_ref, after_ref, win_ref, wout_ref, relay_send, relay_recv, send_out, recv_out):
        del win_in, wout_in, after_ref
        _relay(win_ref, recv_in_ref, relay_send, relay_recv)
        _start_direct(wout_ref, send_out, recv_out)

    two, four = pltpu.SemaphoreType.DMA((2,)), pltpu.SemaphoreType.DMA((4,))
    return pl.pallas_call(
        body, name="gather_relay_w_in",
        out_shape=(jax.ShapeDtypeStruct(win.shape, win.dtype), jax.ShapeDtypeStruct(wout_slots.shape, wout_slots.dtype),
                   two, two, four, four),
        in_specs=[HBM, HBM, SEM, ANY], out_specs=(HBM, HBM, SEM, SEM, SEM, SEM),
        input_output_aliases={0: 0, 1: 1},
        compiler_params=pltpu.CompilerParams(has_side_effects=EFFECT),
    )(win, wout_slots, recv_in, after)


def _gather_relay_out(wout, recv_out, after):
    def body(wout_in, recv_out_ref, after_ref, wout_ref, relay_send, relay_recv):
        del wout_in, after_ref
        _relay(wout_ref, recv_out_ref, relay_send, relay_recv)

    two = pltpu.SemaphoreType.DMA((2,))
    return pl.pallas_call(
        body, name="gather_relay_w_out",
        out_shape=(jax.ShapeDtypeStruct(wout.shape, wout.dtype), two, two),
        in_specs=[HBM, SEM, ANY], out_specs=(HBM, SEM, SEM),
        input_output_aliases={0: 0},
        compiler_params=pltpu.CompilerParams(has_side_effects=EFFECT),
    )(wout, recv_out, after)


def _gather_wait_direct(buf, send_sems, recv_sems, after, name):
    def body(buf_in, send_ref, recv_ref, after_ref, buf_ref):
        del buf_in, after_ref
        x, y, c = _my_place()
        me = _chip_of(x, y)
        for n, (peer, chip) in enumerate(_neighbours(x, y, c)[0:2]):
            second = 1 - n
            pltpu.make_async_remote_copy(
                src_ref=_part_of_half(buf_ref, me, c, second), dst_ref=_part_of_half(buf_ref, chip, c, second),
                send_sem=send_ref.at[2 * n + second], recv_sem=recv_ref.at[2 * n + second],
                device_id=peer, device_id_type=MESH).wait_recv()
            for part in range(2):
                piece = _part_of_half(buf_ref, me, c, part)
                pltpu.make_async_remote_copy(
                    src_ref=piece, dst_ref=piece, send_sem=send_ref.at[2 * n + part], recv_sem=recv_ref.at[2 * n + part],
                    device_id=peer, device_id_type=MESH).wait_send()

    return pl.pallas_call(
        body, name=name,
        out_shape=jax.ShapeDtypeStruct(buf.shape, buf.dtype),
        in_specs=[HBM, SEM, SEM, ANY], out_specs=HBM,
        input_output_aliases={0: 0},
        compiler_params=pltpu.CompilerParams(has_side_effects=EFFECT),
    )(buf, send_sems, recv_sems, after)


def _gather_wait_relayed(buf, relay_send, relay_recv, after, name):
    def body(buf_in, rsend_ref, rrecv_ref, after_ref, buf_ref):
        del buf_in, after_ref
        x, y, c = _my_place()
        nbrs = _neighbours(x, y, c)
        for n in range(2):
            relayed = _part_of_half(buf_ref, nbrs[n][1], c, n)
            cp = pltpu.make_async_remote_copy(
                src_ref=relayed, dst_ref=_part_of_half(buf_ref, nbrs[2][1], c, n),
                send_sem=rsend_ref.at[n], recv_sem=rrecv_ref.at[n], device_id=nbrs[1 - n][0], device_id_type=MESH)
            cp.wait_recv()
            cp.wait_send()

    return pl.pallas_call(
        body, name=name,
        out_shape=jax.ShapeDtypeStruct(buf.shape, buf.dtype),
        in_specs=[HBM, SEM, SEM, ANY], out_specs=HBM,
        input_output_aliases={0: 0},
        compiler_params=pltpu.CompilerParams(has_side_effects=EFFECT),
    )(buf, relay_send, relay_recv, after)


def _forward_copies(buf_ref, which, send_sems, recv_sems):
    half = buf_ref.shape[1] // 2
    x, y, c = _my_place()

    def copy(k, chip, core, part):
        piece = _half_rows(buf_ref, chip, core, half) if part is None else _part_of_half(buf_ref, chip, core, part)
        return pltpu.make_async_remote_copy(
            src_ref=piece, dst_ref=piece, send_sem=send_sems.at[k], recv_sem=recv_sems.at[k],
            device_id=(x, y, 1 - c), device_id_type=MESH)

    chips = [_neighbours(x, y, c)[n][1] for n, _ in which]
    return [(copy(k, chip, c, part), copy(k, chip, 1 - c, part)) for k, (chip, (_, part)) in enumerate(zip(chips, which))]


def _forward_halves(buf, which, name):
    def body(buf_in, buf_ref, send_sems, recv_sems):
        del buf_in
        copies = _forward_copies(buf_ref, which, send_sems, recv_sems)
        for mine, _ in copies:
            mine.start()
        for mine, theirs in copies:
            theirs.wait_recv()
        for mine, _ in copies:
            mine.wait_send()

    return pl.pallas_call(
        body, name=name,
        out_shape=jax.ShapeDtypeStruct(buf.shape, buf.dtype),
        in_specs=[HBM], out_specs=HBM,
        input_output_aliases={0: 0},
        scratch_shapes=[pltpu.SemaphoreType.DMA((len(which),))] * 2,
    )(buf)


def _forward_start(buf, which, name):
    def body(buf_in, buf_ref, send_sems, recv_sems, token_ref):
        del buf_in
        for mine, _ in _forward_copies(buf_ref, which, send_sems, recv_sems):
            mine.start()
        token_ref[...] = jnp.zeros(token_ref.shape, F32)

    sems = pltpu.SemaphoreType.DMA((len(which),))
    return pl.pallas_call(
        body, name=name,
        out_shape=(jax.ShapeDtypeStruct(buf.shape, buf.dtype), sems, sems, TOKEN),
        in_specs=[HBM], out_specs=(HBM, SEM, SEM, VMEM),
        input_output_aliases={0: 0},
        compiler_params=pltpu.CompilerParams(has_side_effects=EFFECT),
    )(buf)


def _forward_wait(buf, which, send_sems, recv_sems, after, name):
    def body(buf_in, send_ref, recv_ref, after_ref, buf_ref):
        del buf_in, after_ref
        for mine, theirs in _forward_copies(buf_ref, which, send_ref, recv_ref):
            theirs.wait_recv()
            mine.wait_send()

    return pl.pallas_call(
        body, name=name,
        out_shape=jax.ShapeDtypeStruct(buf.shape, buf.dtype),
        in_specs=[HBM, SEM, SEM, ANY], out_specs=HBM,
        input_output_aliases={0: 0},
        compiler_params=pltpu.CompilerParams(has_side_effects=EFFECT),
    )(buf, send_sems, recv_sems, after)


def _dw_swapped(a, b, row_chunks, col_chunks, name):
    r, t = a.shape
    c_all = b.shape[1]
    chunks = row_chunks * col_chunks
    rq, cq = r // row_chunks, c_all // col_chunks
    half = rq // 2
    tn = COL_TILE
    nt = cq // tn
    steps = col_chunks * nt

    def body(a_ref, b_ref, mine_ref, sib_ref, stage, send_sems, recv_sems):
        x, y, c = _my_place()
        j, n = pl.program_id(0), pl.program_id(1)
        step = j * nt + n
        slot = step % 2
        res = jnp.dot(a_ref[...], b_ref[...], preferred_element_type=F32).astype(BF16)

        def landing(jj, nn):
            cols = pl.ds(pl.multiple_of(nn * tn, tn), tn)
            return sib_ref.at[:, :, cols] if col_chunks == 1 else sib_ref.at[pl.ds(jj, 1), :, cols]

        def copy(slot_, step_, jj, nn):
            return pltpu.make_async_remote_copy(
                src_ref=stage.at[slot_], dst_ref=landing(jj, nn), send_sem=send_sems.at[slot_],
                recv_sem=recv_sems.at[step_], device_id=(x, y, 1 - c), device_id_type=MESH)

        @pl.when(step >= 2)
        def _():
            copy(slot, step, j, n).wait_send()

        for q in range(row_chunks):
            lo = res[q * rq:q * rq + half, :]
            hi = res[q * rq + half:(q + 1) * rq, :]
            mine_ref[q] = jnp.where(c == 0, lo, hi)
            stage[slot, q] = jnp.where(c == 0, hi, lo)
        copy(slot, step, j, n).start()

        @pl.when(step == steps - 1)
        def _():
            for s in range(max(steps - 2, 0), steps):
                copy(s % 2, s, j, n).wait_send()
            for s in range(steps):
                copy(s % 2, s, j, n).wait_recv()

    shape = jax.ShapeDtypeStruct((chunks, half, cq), BF16)
    return pl.pallas_call(
        body, name=name, grid=(col_chunks, nt),
        out_shape=(shape, shape),
        in_specs=[pl.BlockSpec((r, t), lambda j, n: (0, 0)), pl.BlockSpec((t, tn), lambda j, n: (0, j * nt + n))],
        out_specs=(pl.BlockSpec((row_chunks, half, tn), lambda j, n: (j, 0, n)), ANY),
        scratch_shapes=[pltpu.VMEM((2, row_chunks, half, tn), BF16), pltpu.SemaphoreType.DMA((2,)),
                        pltpu.SemaphoreType.DMA((steps,))],
        compiler_params=_params(("arbitrary", "arbitrary")),
    )(a, b)


def _owners_start(csum, name, after=()):
    land = pltpu.with_memory_space_constraint(lax.empty((N_CHIPS - 1,) + csum.shape[1:], csum.dtype), pltpu.HBM)

    def body(csum_ref, land_ref, *rest):
        send_sems, recv_sems, _, _, token_ref = rest[len(after):]
        x, y, c = _my_place()
        for k, (peer, owner) in enumerate(_ici_peers(x, y, c)):
            pltpu.make_async_remote_copy(
                src_ref=csum_ref.at[owner], dst_ref=land_ref.at[k], send_sem=send_sems.at[k], recv_sem=recv_sems.at[k],
                device_id=peer, device_id_type=MESH).start()
        token_ref[...] = jnp.zeros(token_ref.shape, F32)

    sems = pltpu.SemaphoreType.DMA((N_CHIPS - 1,))
    return pl.pallas_call(
        body, name=name,
        out_shape=(sems, sems, jax.ShapeDtypeStruct(csum.shape, csum.dtype),
                   jax.ShapeDtypeStruct(land.shape, land.dtype), TOKEN),
        in_specs=[HBM, HBM] + [ANY] * len(after), out_specs=(SEM, SEM, HBM, HBM, VMEM),
        input_output_aliases={0: 2, 1: 3},
        compiler_params=pltpu.CompilerParams(has_side_effects=EFFECT),
    )(pltpu.with_memory_space_constraint(csum, pltpu.HBM), land, *after)


def _owners_wait(send_sems, recv_sems, csum, land, after, name):
    def body(csum_ref, land_ref, send_ref, recv_ref, *rest):
        del rest
        x, y, c = _my_place()
        for k, (peer, owner) in enumerate(_ici_peers(x, y, c)):
            cp = pltpu.make_async_remote_copy(
                src_ref=csum_ref.at[owner], dst_ref=land_ref.at[k], send_sem=send_ref.at[k], recv_sem=recv_ref.at[k],
                device_id=peer, device_id_type=MESH)
            cp.wait_send()
            cp.wait_recv()

    return pl.pallas_call(
        body, name=name,
        out_shape=(jax.ShapeDtypeStruct(csum.shape, csum.dtype), jax.ShapeDtypeStruct(land.shape, land.dtype)),
        in_specs=[HBM, HBM, SEM, SEM] + [ANY] * len(after), out_specs=(HBM, HBM),
        input_output_aliases={0: 0, 1: 1},
        compiler_params=pltpu.CompilerParams(has_side_effects=EFFECT),
    )(csum, land, send_sems, recv_sems, *after)[1]


def _join_start(full, name, part, parts):
    half = full.shape[0] // 2
    rows = half // parts

    def body(full_in, full_ref, send_sem, recv_sem, token_ref):
        del full_in
        x, y, c = _my_place()
        mine = full_ref.at[pl.ds(pl.multiple_of(c * half + part * rows, rows), rows), :]
        pltpu.make_async_remote_copy(
            src_ref=mine, dst_ref=mine, send_sem=send_sem.at[0], recv_sem=recv_sem.at[0],
            device_id=(x, y, 1 - c), device_id_type=MESH).start()
        token_ref[...] = jnp.zeros(token_ref.shape, F32)

    one = pltpu.SemaphoreType.DMA((1,))
    return pl.pallas_call(
        body, name=name,
        out_shape=(jax.ShapeDtypeStruct(full.shape, full.dtype), one, one, TOKEN),
        in_specs=[HBM], out_specs=(HBM, SEM, SEM, VMEM),
        input_output_aliases={0: 0},
        compiler_params=pltpu.CompilerParams(has_side_effects=EFFECT),
    )(full)


def _join_wait(full, send_sem, recv_sem, after, name, part, parts):
    half = full.shape[0] // 2
    rows = half // parts

    def body(full_in, send_ref, recv_ref, *rest):
        del full_in
        full_ref = rest[-1]
        x, y, c = _my_place()
        cp = pltpu.make_async_remote_copy(
            src_ref=full_ref.at[pl.ds(pl.multiple_of(c * half + part * rows, rows), rows), :],
            dst_ref=full_ref.at[pl.ds(pl.multiple_of((1 - c) * half + part * rows, rows), rows), :],
            send_sem=send_ref.at[0], recv_sem=recv_ref.at[0], device_id=(x, y, 1 - c), device_id_type=MESH)
        cp.wait_send()
        cp.wait_recv()

    return pl.pallas_call(
        body, name=name,
        out_shape=jax.ShapeDtypeStruct(full.shape, full.dtype),
        in_specs=[HBM, SEM, SEM] + [ANY] * len(after), out_specs=HBM,
        input_output_aliases={0: 0},
        compiler_params=pltpu.CompilerParams(has_side_effects=EFFECT),
    )(full, send_sem, recv_sem, *after)


def _cast_into_slot(place, w, name):
    rows, cols = w.shape
    tr = min(rows, ROW_TILE)

    def body(place_ref, w_ref, o_ref):
        del place_ref
        o_ref[...] = w_ref[...].astype(BF16)

    grid_spec = pltpu.PrefetchScalarGridSpec(
        num_scalar_prefetch=1, grid=(rows // tr,),
        in_specs=[pl.BlockSpec((tr, cols), lambda i, p: (i, 0))],
        out_specs=pl.BlockSpec((None, tr, cols), lambda i, p: (p[0], i, 0)))
    return pl.pallas_call(
        body, name=name, grid_spec=grid_spec,
        out_shape=jax.ShapeDtypeStruct((N_CHIPS, rows, cols), BF16),
        compiler_params=_params(("parallel",)),
    )(place, w)


def _ada_modulation(packed, w_ada, b_ada, d, after=()):
    rows_per, n = packed.shape
    d_model, wa = w_ada.shape

    def body(v_ref, w_hbm, b_ref, *rest):
        all_ref, mod_ref, w_vmem, part_ref, parts_ref, load_sem, send1, recv1, send2, recv2 = rest[len(after):]
        x, y, c = _my_place()
        me = 4 * x + 2 * y + c
        chip = _chip_of(x, y)
        load = pltpu.make_async_copy(w_hbm, w_vmem, load_sem)
        load.start()

        def rows(idx):
            return all_ref.at[pl.ds(pl.multiple_of(idx * rows_per, rows_per), rows_per), :]

        all_ref[pl.ds(pl.multiple_of(me * rows_per, rows_per), rows_per), :] = v_ref[...]
        copies = []
        for k in range(1, N_DEV):
            peer = (_flip(x, k & 4), _flip(y, k & 2), _flip(c, k & 1))
            cp = pltpu.make_async_remote_copy(
                src_ref=v_ref, dst_ref=rows(me), send_sem=send1.at[k - 1], recv_sem=recv1.at[k - 1],
                device_id=peer, device_id_type=MESH)
            cp.start()
            copies.append((cp, peer))
        for k, (cp, peer) in enumerate(copies):
            pltpu.make_async_remote_copy(
                src_ref=v_ref, dst_ref=rows(4 * peer[0] + 2 * peer[1] + peer[2]), send_sem=send1.at[k],
                recv_sem=recv1.at[k], device_id=peer, device_id_type=MESH).wait_recv()
        for cp, _ in copies:
            cp.wait_send()

        def c_of(dev):
            segments, pos = [], 0
            while pos < d:
                row, col = divmod(pos, n)
                take = min(d - pos, n - col)
                segments.append(all_ref[dev * rows_per + row:dev * rows_per + row + 1, col:col + take])
                pos += take
            return jnp.concatenate(segments, axis=1)

        c_all = jnp.concatenate([c_of(dev) for dev in range(N_DEV)], axis=0)
        load.wait()
        part_ref[...] = jnp.dot(_silu(c_all), w_vmem[...], precision=lax.Precision.HIGHEST, preferred_element_type=F32)
        parts_ref[chip] = part_ref[...]
        swaps = []
        for k, (peer, _) in enumerate(_ici_peers(x, y, c)):
            cp = pltpu.make_async_remote_copy(
                src_ref=part_ref, dst_ref=parts_ref.at[chip], send_sem=send2.at[k], recv_sem=recv2.at[k],
                device_id=peer, device_id_type=MESH)
            cp.start()
            swaps.append(cp)
        for k, (peer, peer_chip) in enumerate(_ici_peers(x, y, c)):
            pltpu.make_async_remote_copy(
                src_ref=part_ref, dst_ref=parts_ref.at[peer_chip], send_sem=send2.at[k], recv_sem=recv2.at[k],
                device_id=peer, device_id_type=MESH).wait_recv()
        for cp in swaps:
            cp.wait_send()
        flat = jnp.concatenate([parts_ref[j, pl.ds(me, 1), :] for j in range(N_CHIPS)], axis=1) + b_ref[...]
        mod_ref[...] = jnp.concatenate([flat[:, i * d:(i + 1) * d] for i in range(3)], axis=0)

    return pl.pallas_call(
        body, name="ada_modulation",
        out_shape=(jax.ShapeDtypeStruct((N_DEV * rows_per, n), F32), jax.ShapeDtypeStruct((3, d), F32)),
        in_specs=[VMEM, ANY, VMEM] + [ANY] * len(after), out_specs=(VMEM, VMEM),
        scratch_shapes=[pltpu.VMEM((d_model, wa), F32), pltpu.VMEM((N_DEV, wa), F32),
                        pltpu.VMEM((N_CHIPS, N_DEV, wa), F32), pltpu.SemaphoreType.DMA,
                        pltpu.SemaphoreType.DMA((N_DEV - 1,)), pltpu.SemaphoreType.DMA((N_DEV - 1,)),
                        pltpu.SemaphoreType.DMA((N_CHIPS - 1,)), pltpu.SemaphoreType.DMA((N_CHIPS - 1,))],
        compiler_params=_params(),
    )(packed, w_ada, b_ada, *after)


def _prenorm(x, mod, g_pre, after):
    t, d = x.shape
    tb = ROW_TILE

    def body(x_ref, mod_ref, g_ref, after_ref, h_ref, ht_ref):
        del after_ref
        xv = x_ref[...]
        r = lax.rsqrt(jnp.mean(xv * xv, axis=-1, keepdims=True) + EPS)
        h = (xv * r) * g_ref[...] * (1.0 + mod_ref[1:2, :]) + mod_ref[0:1, :]
        h_ref[...] = h.astype(BF16)
        ht_ref[...] = h.T.astype(BF16)

    return pl.pallas_call(
        body, name="prenorm", grid=(t // tb,),
        out_shape=(jax.ShapeDtypeStruct((t, d), BF16), jax.ShapeDtypeStruct((d, t), BF16)),
        in_specs=[pl.BlockSpec((tb, d), lambda i: (i, 0)), pl.BlockSpec((3, d), lambda i: (0, 0)),
                  pl.BlockSpec((1, d), lambda i: (0, 0)), ANY],
        out_specs=(pl.BlockSpec((tb, d), lambda i: (i, 0)), pl.BlockSpec((d, tb), lambda i: (0, i))),
        compiler_params=_params(("parallel",)),
    )(x, mod, g_pre, after)


def _proj_tiles(proj, h, w, tiles, name):
    t, d = h.shape
    ws = w.shape[-1]
    tn = COL_TILE
    nt = ws // tn

    def body(tile_ref, *refs):
        del tile_ref
        a_ref, b_ref, o_ref = refs[-3:]
        o_ref[...] = jnp.dot(a_ref[...], b_ref[...].astype(BF16), preferred_element_type=F32).astype(BF16)

    if w.ndim == 3:
        w_spec = pl.BlockSpec((None, d, tn), lambda i, tl: (tl[i] // nt, 0, tl[i] % nt))
    else:
        w_spec = pl.BlockSpec((d, tn), lambda i, tl: (0, tl[i] % nt))
    first = proj is None
    grid_spec = pltpu.PrefetchScalarGridSpec(
        num_scalar_prefetch=1, grid=(tiles.shape[0],),
        in_specs=([] if first else [HBM]) + [pl.BlockSpec((t, d), lambda i, tl: (0, 0)), w_spec],
        out_specs=pl.BlockSpec((t, tn), lambda i, tl: (0, tl[i])))
    return pl.pallas_call(
        body, name=name, grid_spec=grid_spec,
        out_shape=jax.ShapeDtypeStruct((t, N_CHIPS * ws), BF16),
        input_output_aliases={} if first else {1: 0},
        compiler_params=_params(("parallel",)),
    )(*([tiles] if first else [tiles, proj]), h, w)


def _shift_rows(a, rows):
    idx = lax.broadcasted_iota(jnp.int32, a.shape, 0)
    prev = jnp.where(idx == 0, 0.0, pltpu.roll(a, 1, 0))
    nxt = jnp.where(idx == rows - 1, 0.0, pltpu.roll(a, rows - 1, 0))
    return prev, nxt


def _conv_fwd(conv_proj, conv_w, conv_b, dc):
    t = conv_proj.shape[0]
    ct = CONV_TILE
    nct = dc // ct

    def body(u_ref, cg_ref, w_ref, b_ref, co_ref):
        a = cg_ref[...].astype(F32) * u_ref[...].astype(F32)
        prev, nxt = _shift_rows(a, t)
        co_ref[...] = (w_ref[0:1, :] * prev + w_ref[1:2, :] * a + w_ref[2:3, :] * nxt + b_ref[...]).astype(BF16)

    return pl.pallas_call(
        body, name="conv_fwd", grid=(nct,),
        out_shape=jax.ShapeDtypeStruct((t, dc), BF16),
        in_specs=[pl.BlockSpec((t, ct), lambda i: (0, i)), pl.BlockSpec((t, ct), lambda i: (0, 2 * nct + i)),
                  pl.BlockSpec((3, ct), lambda i: (0, i)), pl.BlockSpec((1, ct), lambda i: (0, i))],
        out_specs=pl.BlockSpec((t, ct), lambda i: (0, i)),
        compiler_params=_params(("parallel",)),
    )(conv_proj, conv_proj, conv_w, conv_b)


def _to_residue_major(src_ref, dst_ref, r):
    seq = src_ref.shape[0] // r
    for res in range(r):
        dst_ref[res * seq:(res + 1) * seq, :] = src_ref[pl.ds(res, seq, stride=r), :].astype(dst_ref.dtype)


def _branch_operands(token_refs, stage, dil, r):
    if r == 1:
        return list(token_refs)
    for i, ref in enumerate(token_refs):
        stage[...] = ref[...].astype(F32)
        _to_residue_major(stage, dil.at[i], r)
    return [dil.at[i] for i in range(len(token_refs))]


def _scaled_queries(q):
    return (q.astype(F32) * (HEAD_DIM ** -0.5)).astype(BF16)


BLOCK_SHIFTS = (0, -SIDE, None)


def _band_bias(rel, slope):
    arel = jnp.abs(rel)
    return jnp.where(arel <= SIDE, arel.astype(F32) * slope, NEG_INF)


def _fill_bias_tiles(bias_ref, sl_ref, r, kw):
    base = lax.broadcasted_iota(jnp.int32, (ATT_BQ, kw), 1) - lax.broadcasted_iota(jnp.int32, (ATT_BQ, kw), 0)
    for hh in range(2):
        slope = -(sl_ref[hh:hh + 1, 0:kw] * float(r))
        for e, shift in enumerate(BLOCK_SHIFTS):
            shift = ATT_BQ - kw if shift is None else shift
            bias_ref[hh, e, :, 0:kw] = _band_bias(base + shift, slope)


def _fill_stacked_bias_tiles(bias_ref, sl_ref, r, kw):
    base = lax.broadcasted_iota(jnp.int32, (kw, ATT_BQ), 0) - lax.broadcasted_iota(jnp.int32, (kw, ATT_BQ), 1)
    for hh in range(2):
        slope = -(sl_ref[hh:hh + 1, 0:ATT_BQ] * float(r))
        for e, shift in enumerate(BLOCK_SHIFTS):
            shift = ATT_BQ - kw if shift is None else shift
            bias_ref[e, 0:kw, hh * ATT_BQ:(hh + 1) * ATT_BQ] = _band_bias(base + shift, slope)


def _first_head_lanes():
    return lax.broadcasted_iota(jnp.int32, (1, PAIR), 1) < HEAD_DIM


def _only_head(x, first, hh):
    return jnp.where(first if hh == 0 else jnp.logical_not(first), x, jnp.zeros_like(x))


def _block_place(g, seq_len, kw):
    nqb = seq_len // ATT_BQ
    if nqb == 1:
        row = pl.multiple_of(g * ATT_BQ, ATT_BQ)
        return row, row, 0
    res = g // nqb
    qb = g - res * nqb
    q0 = qb * ATT_BQ
    ks = jnp.clip(q0 - SIDE, 0, seq_len - kw)
    edge = jnp.where(qb == 0, 0, jnp.where(qb == nqb - 1, 2, 1))
    return (pl.multiple_of(res * seq_len + q0, ATT_BQ), pl.multiple_of(res * seq_len + ks, SIDE), edge)


def _qkv_specs(dc, da, t, index):
    return [pl.BlockSpec((t, PAIR), functools.partial(index, (4 * dc + comp * da) // PAIR)) for comp in range(3)]


def _attn_fwd(proj, slopes, dc, da):
    t = proj.shape[0]
    hp = da // PAIR
    n_blocks = t // ATT_BQ

    def body(q_ref, k_ref, v_ref, sl_ref, o_ref, lse_ref, stage, dil, bias, o_res, l_res, o_tok, l_tok):
        for b, (_, r) in enumerate(BRANCHES):
            seq_len = t // r
            kw = min(ATT_KW, seq_len)
            ops = _branch_operands([q_ref, k_ref, v_ref], stage, dil, r)
            _fill_bias_tiles(bias, sl_ref, r, kw)
            o_dst, l_dst = (o_tok.at[b], l_tok.at[b]) if r == 1 else (o_res, l_res)
            first = _first_head_lanes()

            def blocks(trip, carry, seq_len=seq_len, kw=kw, o_dst=o_dst, l_dst=l_dst, first=first, ops=ops):
                nt = (((1,), (1,)), ((), ()))
                places = [_block_place(trip * ATT_UNROLL + i, seq_len, kw) for i in range(ATT_UNROLL)]
                chains = [(i, hh) for i in range(ATT_UNROLL) for hh in range(2)]
                qs = [_scaled_queries(ops[0][pl.ds(qrow, ATT_BQ), :]) for qrow, _, _ in places]
                ks = [ops[1][pl.ds(krow, kw), :] for _, krow, _ in places]
                vs = [ops[2][pl.ds(krow, kw), :] for _, krow, _ in places]
                ss = [lax.dot_general(_only_head(qs[i], first, hh), ks[i], nt, preferred_element_type=F32)
                      + bias[hh, places[i][2], :, 0:kw] for i, hh in chains]
                tops = [jnp.max(s, axis=-1, keepdims=True) for s in ss]
                ps = [jnp.exp(s - m) for s, m in zip(ss, tops)]
                dens = [jnp.sum(p, axis=-1, keepdims=True) for p in ps]
                for i, (qrow, _, _) in enumerate(places):
                    weights = jnp.concatenate([ps[2 * i].astype(BF16), ps[2 * i + 1].astype(BF16)], axis=1)
                    values = jnp.concatenate([_only_head(vs[i], first, 0), _only_head(vs[i], first, 1)], axis=0)
                    den = jnp.where(first, dens[2 * i], dens[2 * i + 1])
                    o_dst[pl.ds(qrow, ATT_BQ), :] = jnp.dot(weights, values, preferred_element_type=F32) / den
                    l_dst[pl.ds(qrow, ATT_BQ), :] = jnp.where(first, tops[2 * i], tops[2 * i + 1]) + jnp.log(den)
                return carry

            lax.fori_loop(0, n_blocks // ATT_UNROLL, blocks, 0)
            if r > 1:
                for res in range(r):
                    rows = slice(res * seq_len, (res + 1) * seq_len)
                    o_tok[b, pl.ds(res, seq_len, stride=r), :] = o_res[rows, :]
                    l_tok[b, pl.ds(res, seq_len, stride=r), :] = l_res[rows, :]

        def merge(i, carry):
            rows = pl.ds(pl.multiple_of(i * ROW_TILE, ROW_TILE), ROW_TILE)
            la, lb, lc = l_tok[0, rows, :], l_tok[1, rows, :], l_tok[2, rows, :]
            m = jnp.maximum(jnp.maximum(la, lb), lc)
            wa, wb, wc = jnp.exp(la - m), jnp.exp(lb - m), jnp.exp(lc - m)
            den = wa + wb + wc
            o_ref[rows, :] = (wa * o_tok[0, rows, :] + wb * o_tok[1, rows, :] + wc * o_tok[2, rows, :]) * (1.0 / den)
            lse_ref[rows, :] = m + jnp.log(den)
            return carry

        lax.fori_loop(0, t // ROW_TILE, merge, 0)

    pair_spec = pl.BlockSpec((None, t, PAIR), lambda h: (h, 0, 0))
    return pl.pallas_call(
        body, name="attn_fwd", grid=(hp,),
        out_shape=(jax.ShapeDtypeStruct((hp, t, PAIR), F32), jax.ShapeDtypeStruct((hp, t, PAIR), F32)),
        in_specs=_qkv_specs(dc, da, t, lambda first, h: (0, first + h))
        + [pl.BlockSpec((None, 8, ATT_KW), lambda h: (h, 0, 0))],
        out_specs=(pair_spec, pair_spec),
        scratch_shapes=[pltpu.VMEM((t, PAIR), F32), pltpu.VMEM((3, t, PAIR), BF16),
                        pltpu.VMEM((2, 3, ATT_BQ, ATT_KW), F32),
                        pltpu.VMEM((t, PAIR), F32), pltpu.VMEM((t, PAIR), F32),
                        pltpu.VMEM((3, t, PAIR), F32), pltpu.VMEM((3, t, PAIR), F32)],
        compiler_params=_params(("parallel",)),
    )(proj, proj, proj, slopes)


def _attn_bwd(dproj, proj, d_o, lse, delta, slopes, dc, da, after):
    t = proj.shape[0]
    hp = da // PAIR
    n_blocks = t // ATT_BQ

    def all_branches(q_ref, k_ref, v_ref, do_ref, lse_ref, dl_ref, sl_ref,
                     stage, dil, packed, packed_res, row_vecs, bias_t, acc, tot):
        first = _first_head_lanes()
        lane = lax.broadcasted_iota(jnp.int32, (1, PAIR), 1)
        packed[...] = jnp.where((lane & (HEAD_DIM - 1)) < HEAD_DIM // 2, lse_ref[...], dl_ref[...])
        for b, (_, r) in enumerate(BRANCHES):
            seq_len = t // r
            kw = min(ATT_KW, seq_len)
            ops = _branch_operands([q_ref, k_ref, v_ref, do_ref], stage, dil, r)
            scalars = packed
            if r > 1:
                _to_residue_major(packed, packed_res, r)
                scalars = packed_res
            for g in range(n_blocks):
                flipped = scalars[g * ATT_BQ:(g + 1) * ATT_BQ, :].T
                for row in range(4):
                    row_vecs[g, row:row + 1, :] = flipped[row * (HEAD_DIM // 2):row * (HEAD_DIM // 2) + 1, :]
            _fill_stacked_bias_tiles(bias_t, sl_ref, r, kw)
            acc[1] = jnp.zeros((t, PAIR), F32)
            acc[2] = jnp.zeros((t, PAIR), F32)

            def blocks(trip, carry, seq_len=seq_len, kw=kw, ops=ops):
                nt = (((1,), (1,)), ((), ()))
                group = range(ATT_UNROLL)
                places = [_block_place(trip * ATT_UNROLL + i, seq_len, kw) for i in group]
                ks, vs, q2s, do2s, lse2s, dl2s = [], [], [], [], [], []
                for i, (qrow, krow, _) in zip(group, places):
                    q = _scaled_queries(ops[0][pl.ds(qrow, ATT_BQ), :])
                    dov = ops[3][pl.ds(qrow, ATT_BQ), :]
                    ks.append(ops[1][pl.ds(krow, kw), :])
                    vs.append(ops[2][pl.ds(krow, kw), :])
                    q2s.append(jnp.concatenate([_only_head(q, first, 0), _only_head(q, first, 1)], axis=0))
                    do2s.append(jnp.concatenate([_only_head(dov, first, 0), _only_head(dov, first, 1)], axis=0))
                    rows = row_vecs[trip * ATT_UNROLL + i]
                    lse2s.append(jnp.concatenate([rows[0:1, :], rows[2:3, :]], axis=1))
                    dl2s.append(jnp.concatenate([rows[1:2, :], rows[3:4, :]], axis=1))
                s_ts = [lax.dot_general(ks[i], q2s[i], nt, preferred_element_type=F32) for i in group]
                dp_ts = [lax.dot_general(vs[i], do2s[i], nt, preferred_element_type=F32) for i in group]
                p_ts = [jnp.exp(s_ts[i] + bias_t[places[i][2], 0:kw, :] - lse2s[i]) for i in group]
                ds_ts = [p_ts[i] * (dp_ts[i] - dl2s[i]) for i in group]
                dvs = [jnp.dot(p_ts[i].astype(BF16), do2s[i], preferred_element_type=F32) for i in group]
                dks = [jnp.dot(ds_ts[i].astype(BF16), q2s[i], preferred_element_type=F32) for i in group]
                dss = [ds_ts[i].T.astype(BF16) for i in group]
                dqs = [jnp.dot(dss[i][0:ATT_BQ, :], _only_head(ks[i], first, 0), preferred_element_type=F32)
                       + jnp.dot(dss[i][ATT_BQ:2 * ATT_BQ, :], _only_head(ks[i], first, 1), preferred_element_type=F32)
                       for i in group]
                for i, (qrow, krow, _) in zip(group, places):
                    acc[0, pl.ds(qrow, ATT_BQ), :] = dqs[i] * (HEAD_DIM ** -0.5)
                    acc[1, pl.ds(krow, kw), :] += dks[i]
                    acc[2, pl.ds(krow, kw), :] += dvs[i]
                return carry

            lax.fori_loop(0, n_blocks // ATT_UNROLL, blocks, 0)
            for comp in range(3):
                if r == 1:
                    tot[comp] = acc[comp]
                else:
                    for res in range(r):
                        tok = pl.ds(res, seq_len, stride=r)
                        tot[comp, tok, :] = tot[comp, tok, :] + acc[comp, res * seq_len:(res + 1) * seq_len, :]

    first_q = (4 * dc) // PAIR

    def body(dproj_in, q_ref, k_ref, v_ref, do_ref, lse_ref, dl_ref, sl_ref, after_ref, out_ref, *scratch):
        del dproj_in, after_ref
        work, out_stage, out_sems = scratch[:-2], scratch[-2], scratch[-1]
        h = pl.program_id(0)
        all_branches(q_ref, k_ref, v_ref, do_ref, lse_ref, dl_ref, sl_ref, *work)

        def out_copy(comp):
            cols = pl.ds(pl.multiple_of((first_q + comp * hp + h) * PAIR, PAIR), PAIR)
            return pltpu.make_async_copy(out_stage.at[comp], out_ref.at[:, cols], out_sems.at[comp])

        @pl.when(h > 0)
        def _():
            for comp in range(3):
                out_copy(comp).wait()

        for comp in range(3):
            out_stage[comp] = work[-1][comp].astype(BF16)
            out_copy(comp).start()

        @pl.when(h == hp - 1)
        def _():
            for comp in range(3):
                out_copy(comp).wait()

    pair_spec = pl.BlockSpec((None, t, PAIR), lambda h: (h, 0, 0))
    return pl.pallas_call(
        body, name="attn_bwd", grid=(hp,),
        out_shape=jax.ShapeDtypeStruct(dproj.shape, BF16),
        in_specs=[HBM] + _qkv_specs(dc, da, t, lambda first, h: (0, first + h))
        + [pair_spec, pair_spec, pair_spec, pl.BlockSpec((None, 8, ATT_KW), lambda h: (h, 0, 0)), ANY],
        out_specs=ANY,
        input_output_aliases={0: 0},
        scratch_shapes=[pltpu.VMEM((t, PAIR), F32), pltpu.VMEM((4, t, PAIR), BF16),
                        pltpu.VMEM((t, PAIR), F32), pltpu.VMEM((t, PAIR), F32),
                        pltpu.VMEM((n_blocks, 8, ATT_BQ), F32), pltpu.VMEM((3, ATT_KW, 2 * ATT_BQ), F32),
                        pltpu.VMEM((3, t, PAIR), F32), pltpu.VMEM((3, t, PAIR), F32),
                        pltpu.VMEM((3, t, PAIR), BF16), pltpu.SemaphoreType.DMA((3,))],
        compiler_params=_params(("arbitrary",)),
    )(dproj, proj, proj, proj, d_o, lse, delta, slopes, after)


def _mix_fwd(co, proj, o_mix, g_conv, g_attn_pairs, after):
    t, dc = co.shape
    hp = o_mix.shape[0]
    da = hp * PAIR
    tb = ROW_TILE

    def body(co_ref, bg_ref, zc_ref, za_ref, om_ref, gc_ref, ga_ref, after_ref, ycat_ref, ycatt_ref):
        del after_ref
        p = bg_ref[...].astype(F32) * co_ref[...].astype(F32)
        rc = lax.rsqrt(jnp.mean(p * p, axis=-1, keepdims=True) + EPS)
        yc = (p * rc) * gc_ref[...] * _silu(zc_ref[...].astype(F32))
        ycat_ref[:, 0:dc] = yc.astype(BF16)
        ycatt_ref[0:dc, :] = yc.T.astype(BF16)
        ssq = jnp.zeros((tb, 1), F32)
        for h in range(hp):
            o = om_ref[h]
            ssq = ssq + jnp.sum(o * o, axis=-1, keepdims=True)
        ra = lax.rsqrt(ssq * (1.0 / da) + EPS)
        for h in range(hp):
            ya = (om_ref[h] * ra) * ga_ref[h] * _silu(za_ref[:, h * PAIR:(h + 1) * PAIR].astype(F32))
            ycat_ref[:, dc + h * PAIR:dc + (h + 1) * PAIR] = ya.astype(BF16)
            ycatt_ref[dc + h * PAIR:dc + (h + 1) * PAIR, :] = ya.T.astype(BF16)

    pair_spec = pl.BlockSpec((hp, tb, PAIR), lambda i: (0, i, 0))
    return pl.pallas_call(
        body, name="mix_fwd", grid=(t // tb,),
        out_shape=(jax.ShapeDtypeStruct((t, dc + da), BF16), jax.ShapeDtypeStruct((dc + da, t), BF16)),
        in_specs=[pl.BlockSpec((tb, dc), lambda i: (i, 0)),
                  pl.BlockSpec((tb, dc), lambda i: (i, 1)),
                  pl.BlockSpec((tb, dc), lambda i: (i, 3)),
                  pl.BlockSpec((tb, da), lambda i: (i, 7)),
                  pair_spec,
                  pl.BlockSpec((1, dc), lambda i: (0, 0)),
                  pl.BlockSpec((hp, 1, PAIR), lambda i: (0, 0, 0)), ANY],
        out_specs=(pl.BlockSpec((tb, dc + da), lambda i: (i, 0)), pl.BlockSpec((dc + da, tb), lambda i: (0, i))),
        compiler_params=_params(("parallel",)),
    )(co, proj, proj, proj, o_mix, g_conv, g_attn_pairs, after)


def _out_fwd_bwd(ycat, woutf, x, target, mod, g_post):
    t, d = x.shape
    n = ycat.shape[1]
    tb = ROW_TILE

    def body(a_ref, w_ref, x_ref, tg_ref, mod_ref, g_ref, dout_ref, dy_ref, acc_ref):
        y = jnp.dot(a_ref[...], w_ref[...], preferred_element_type=F32)
        r = lax.rsqrt(jnp.mean(y * y, axis=-1, keepdims=True) + EPS)
        nh = y * r
        gate = mod_ref[2:3, :]
        nrm = nh * g_ref[...]
        err = x_ref[...] + gate * nrm - tg_ref[...]
        dout = err * (1.0 / d)
        dout_ref[...] = dout.astype(BF16)
        dn = dout * gate
        a = dn * g_ref[...]
        dy = r * (a - nh * jnp.mean(a * nh, axis=-1, keepdims=True))
        dy_ref[...] = dy.astype(BF16)
        loss = 0.5 * jnp.sum(jnp.sum(err * err, axis=-1, keepdims=True) * (1.0 / d), axis=0, keepdims=True)
        part = jnp.concatenate(
            [jnp.sum(dout * nrm, axis=0, keepdims=True), jnp.sum(dn * nh, axis=0, keepdims=True),
             jnp.broadcast_to(loss, (1, d)), jnp.zeros((5, d), F32)], axis=0)

        @pl.when(pl.program_id(0) == 0)
        def _():
            acc_ref[...] = jnp.zeros(acc_ref.shape, F32)

        acc_ref[...] += part

    return pl.pallas_call(
        body, name="out_fwd_bwd", grid=(t // tb,),
        out_shape=(jax.ShapeDtypeStruct((t, d), BF16), jax.ShapeDtypeStruct((t, d), BF16),
                   jax.ShapeDtypeStruct((8, d), F32)),
        in_specs=[pl.BlockSpec((tb, n), lambda i: (i, 0)), pl.BlockSpec((n, d), lambda i: (0, 0)),
                  pl.BlockSpec((tb, d), lambda i: (i, 0)), pl.BlockSpec((tb, d), lambda i: (i, 0)),
                  pl.BlockSpec((3, d), lambda i: (0, 0)), pl.BlockSpec((1, d), lambda i: (0, 0))],
        out_specs=(pl.BlockSpec((tb, d), lambda i: (i, 0)), pl.BlockSpec((tb, d), lambda i: (i, 0)),
                   pl.BlockSpec((8, d), lambda i: (0, 0))),
        compiler_params=_params(("arbitrary",)),
    )(ycat, woutf, x, target, mod, g_post)


def _matmul_nt(a, b, out_dtype, name):
    m, k = a.shape
    n = b.shape[0]
    tn = COL_TILE

    def body(a_ref, b_ref, o_ref):
        o_ref[...] = lax.dot_general(a_ref[...], b_ref[...], (((1,), (1,)), ((), ())),
                                     preferred_element_type=F32).astype(out_dtype)

    return pl.pallas_call(
        body, name=name, grid=(n // tn,),
        out_shape=jax.ShapeDtypeStruct((m, n), out_dtype),
        in_specs=[pl.BlockSpec((m, k), lambda i: (0, 0)), pl.BlockSpec((tn, k), lambda i: (i, 0))],
        out_specs=pl.BlockSpec((m, tn), lambda i: (0, i)),
        compiler_params=_params(("parallel",)),
    )(a, b)


def _mix_bwd(dycat, co, proj, o_mix, g_conv, g_attn_pairs):
    t, dc = co.shape
    hp = o_mix.shape[0]
    da = hp * PAIR
    tb = ROW_TILE

    def body(dy_ref, co_ref, bg_ref, zc_ref, za_ref, om_ref, gc_ref, ga_ref,
             dcp_ref, dco_ref, do_ref, dl_ref, dgc_ref, dga_ref):
        first = pl.program_id(0) == 0
        cov = co_ref[...].astype(F32)
        bg = bg_ref[...].astype(F32)
        zc = zc_ref[...].astype(F32)
        p = bg * cov
        rc = lax.rsqrt(jnp.mean(p * p, axis=-1, keepdims=True) + EPS)
        nh = p * rc
        dyc = dy_ref[:, 0:dc].astype(F32)
        dn = dyc * _silu(zc)
        a = dn * gc_ref[...]
        dp = rc * (a - nh * jnp.mean(a * nh, axis=-1, keepdims=True))
        dcp_ref[:, 0:dc] = jnp.zeros((tb, dc), BF16)
        dcp_ref[:, dc:2 * dc] = (dp * cov).astype(BF16)
        dcp_ref[:, 2 * dc:3 * dc] = jnp.zeros((tb, dc), BF16)
        dcp_ref[:, 3 * dc:4 * dc] = (dyc * nh * gc_ref[...] * _silu_grad(zc)).astype(BF16)
        dcp_ref[:, 4 * dc:4 * dc + 3 * da] = jnp.zeros((tb, 3 * da), BF16)
        dco_ref[...] = dp * bg

        @pl.when(first)
        def _():
            dgc_ref[...] = jnp.zeros(dgc_ref.shape, F32)
            dga_ref[...] = jnp.zeros(dga_ref.shape, F32)

        dgc_ref[...] += jnp.sum(dn * nh, axis=0, keepdims=True)

        ssq = jnp.zeros((tb, 1), F32)
        for h in range(hp):
            o = om_ref[h]
            ssq = ssq + jnp.sum(o * o, axis=-1, keepdims=True)
        ra = lax.rsqrt(ssq * (1.0 / da) + EPS)
        dot_an = jnp.zeros((tb, 1), F32)
        for h in range(hp):
            nha = om_ref[h] * ra
            za = za_ref[:, h * PAIR:(h + 1) * PAIR].astype(F32)
            dya = dy_ref[:, dc + h * PAIR:dc + (h + 1) * PAIR].astype(F32)
            dna = dya * _silu(za)
            dza = (dya * nha * ga_ref[h] * _silu_grad(za)).astype(BF16)
            dcp_ref[:, 4 * dc + 3 * da + h * PAIR:4 * dc + 3 * da + (h + 1) * PAIR] = dza
            dga_ref[h] += jnp.sum(dna * nha, axis=0, keepdims=True)
            dot_an = dot_an + jnp.sum(dna * ga_ref[h] * nha, axis=-1, keepdims=True)
        mean_an = dot_an * (1.0 / da)
        first_head = lax.broadcasted_iota(jnp.int32, (tb, PAIR), 1) < HEAD_DIM
        for h in range(hp):
            o = om_ref[h]
            nha = o * ra
            za = za_ref[:, h * PAIR:(h + 1) * PAIR].astype(F32)
            dya = dy_ref[:, dc + h * PAIR:dc + (h + 1) * PAIR].astype(F32)
            aa = dya * _silu(za) * ga_ref[h]
            d_o = ra * (aa - nha * mean_an)
            do_ref[h] = d_o.astype(BF16)
            prod = d_o * o
            both = jnp.sum(prod, axis=-1, keepdims=True)
            head0 = jnp.sum(jnp.where(first_head, prod, 0.0), axis=-1, keepdims=True)
            dl_ref[h] = jnp.where(first_head, head0, both - head0)

    pair_spec = pl.BlockSpec((hp, tb, PAIR), lambda i: (0, i, 0))
    return pl.pallas_call(
        body, name="mix_bwd", grid=(t // tb,),
        out_shape=(jax.ShapeDtypeStruct((t, 4 * dc + 4 * da), BF16), jax.ShapeDtypeStruct((t, dc), F32),
                   jax.ShapeDtypeStruct((hp, t, PAIR), BF16), jax.ShapeDtypeStruct((hp, t, PAIR), F32),
                   jax.ShapeDtypeStruct((1, dc), F32), jax.ShapeDtypeStruct((hp, 1, PAIR), F32)),
        in_specs=[pl.BlockSpec((tb, dc + da), lambda i: (i, 0)),
                  pl.BlockSpec((tb, dc), lambda i: (i, 0)),
                  pl.BlockSpec((tb, dc), lambda i: (i, 1)),
                  pl.BlockSpec((tb, dc), lambda i: (i, 3)),
                  pl.BlockSpec((tb, da), lambda i: (i, 7)),
                  pair_spec,
                  pl.BlockSpec((1, dc), lambda i: (0, 0)),
                  pl.BlockSpec((hp, 1, PAIR), lambda i: (0, 0, 0))],
        out_specs=(pl.BlockSpec((tb, 4 * dc + 4 * da), lambda i: (i, 0)), pl.BlockSpec((tb, dc), lambda i: (i, 0)),
                   pair_spec, pair_spec,
                   pl.BlockSpec((1, dc), lambda i: (0, 0)), pl.BlockSpec((hp, 1, PAIR), lambda i: (0, 0, 0))),
        compiler_params=_params(("arbitrary",)),
    )(dycat, co, proj, proj, proj, o_mix, g_conv, g_attn_pairs)


def _conv_bwd(dconv_proj, dco, conv_proj, conv_w, dc, after):
    t = dco.shape[0]
    ct = CONV_TILE
    nct = dc // ct

    def body(dcp_in_ref, dco_ref, u_ref, cg_ref, w_ref, after_ref, dcp_ref, acc_ref):
        del dcp_in_ref, after_ref
        which = pl.program_id(1)
        g = dco_ref[...]
        u = u_ref[...].astype(F32)
        cg = cg_ref[...].astype(F32)
        g_prev, g_next = _shift_rows(g, t)
        da = w_ref[0:1, :] * g_next + w_ref[1:2, :] * g + w_ref[2:3, :] * g_prev
        dcp_ref[...] = (da * jnp.where(which == 0, cg, u)).astype(BF16)
        a = cg * u
        a_prev, a_next = _shift_rows(a, t)
        acc_ref[...] = jnp.concatenate(
            [jnp.sum(g * a_prev, axis=0, keepdims=True), jnp.sum(g * a, axis=0, keepdims=True),
             jnp.sum(g * a_next, axis=0, keepdims=True), jnp.sum(g, axis=0, keepdims=True),
             jnp.zeros((4, ct), F32)], axis=0)

    return pl.pallas_call(
        body, name="conv_bwd", grid=(nct, 2),
        out_shape=(jax.ShapeDtypeStruct(dconv_proj.shape, BF16), jax.ShapeDtypeStruct((8, dc), F32)),
        in_specs=[HBM,
                  pl.BlockSpec((t, ct), lambda i, s: (0, i)),
                  pl.BlockSpec((t, ct), lambda i, s: (0, i)),
                  pl.BlockSpec((t, ct), lambda i, s: (0, 2 * nct + i)),
                  pl.BlockSpec((3, ct), lambda i, s: (0, i)), ANY],
        out_specs=(pl.BlockSpec((t, ct), lambda i, s: (0, 2 * s * nct + i)),
                   pl.BlockSpec((8, ct), lambda i, s: (0, i))),
        input_output_aliases={0: 0},
        compiler_params=_params(("arbitrary", "arbitrary")),
    )(dconv_proj, dco, conv_proj, conv_proj, conv_w, after)


def _dh(dproj, winf, after):
    t = dproj.shape[0]
    _, d, ws = winf.shape
    tm = tn = COL_TILE
    nt = (((1,), (1,)), ((), ()))

    def body(a_ref, w_ref, after_ref, o_ref):
        del after_ref
        acc = lax.dot_general(a_ref[:, 0:ws], w_ref[0], nt, preferred_element_type=F32)
        for j in range(1, N_CHIPS):
            acc = acc + lax.dot_general(a_ref[:, j * ws:(j + 1) * ws], w_ref[j], nt, preferred_element_type=F32)
        o_ref[...] = acc.astype(BF16)

    return pl.pallas_call(
        body, name="dh", grid=(d // tn, t // tm),
        out_shape=jax.ShapeDtypeStruct((t, d), BF16),
        in_specs=[pl.BlockSpec((tm, N_CHIPS * ws), lambda n, m: (m, 0)),
                  pl.BlockSpec((N_CHIPS, tn, ws), lambda n, m: (0, n, 0)), ANY],
        out_specs=pl.BlockSpec((tm, tn), lambda n, m: (m, n)),
        compiler_params=_params(("parallel", "parallel")),
    )(dproj, winf, after)


def _prenorm_bwd(x, dh, dout, mod, g_pre):
    t, d = x.shape
    tb = ROW_TILE

    def body(x_ref, dh_ref, dout_ref, mod_ref, g_ref, gx_ref, acc_ref):
        xv = x_ref[...]
        dhv = dh_ref[...].astype(F32)
        r = lax.rsqrt(jnp.mean(xv * xv, axis=-1, keepdims=True) + EPS)
        xh = xv * r
        one_scale = 1.0 + mod_ref[1:2, :]
        a = dhv * one_scale * g_ref[...]
        gx_ref[...] = dout_ref[...].astype(F32) + r * (a - xh * jnp.mean(a * xh, axis=-1, keepdims=True))
        part = jnp.concatenate(
            [jnp.sum(dhv, axis=0, keepdims=True), jnp.sum(dhv * xh * g_ref[...], axis=0, keepdims=True),
             jnp.sum(dhv * xh * one_scale, axis=0, keepdims=True), jnp.zeros((5, d), F32)], axis=0)

        @pl.when(pl.program_id(0) == 0)
        def _():
            acc_ref[...] = jnp.zeros(acc_ref.shape, F32)

        acc_ref[...] += part

    return pl.pallas_call(
        body, name="prenorm_bwd", grid=(t // tb,),
        out_shape=(jax.ShapeDtypeStruct((t, d), F32), jax.ShapeDtypeStruct((8, d), F32)),
        in_specs=[pl.BlockSpec((tb, d), lambda i: (i, 0)), pl.BlockSpec((tb, d), lambda i: (i, 0)),
                  pl.BlockSpec((tb, d), lambda i: (i, 0)), pl.BlockSpec((3, d), lambda i: (0, 0)),
                  pl.BlockSpec((1, d), lambda i: (0, 0))],
        out_specs=(pl.BlockSpec((tb, d), lambda i: (i, 0)), pl.BlockSpec((8, d), lambda i: (0, 0))),
        compiler_params=_params(("arbitrary",)),
    )(x, dh, dout, mod, g_pre)


def _chip_sums(mine, rsib, name, part=0, parts=1, after=()):
    _, half, cols = mine.shape
    rows = half // parts
    tr = min(rows, ROW_TILE)
    nt = rows // tr

    def body(g_ref, r_ref, *rest):
        rest[-1][...] = (g_ref[...].astype(F32) + r_ref[...].astype(F32)).astype(BF16)

    spec = pl.BlockSpec((None, tr, cols), lambda j, i: (j, part * nt + i, 0))
    return pl.pallas_call(
        body, name=name, grid=(N_CHIPS, nt),
        out_shape=jax.ShapeDtypeStruct((N_CHIPS, rows, cols), BF16),
        in_specs=[spec, spec] + [ANY] * len(after), out_specs=pl.BlockSpec((None, tr, cols), lambda j, i: (j, i, 0)),
        compiler_params=_params(("parallel", "parallel")),
    )(mine, rsib, *after)


def _owner_sum(place, mine, rsib, rici, name, part=0, parts=1):
    _, half, cols = mine.shape
    rows = half // parts
    tr = min(rows, ROW_TILE)
    nt = rows // tr

    def body(place_ref, g_ref, r_ref, i_ref, o_ref):
        del place_ref
        acc = g_ref[...].astype(F32) + r_ref[...].astype(F32)
        for k in range(N_CHIPS - 1):
            acc = acc + i_ref[k].astype(F32)
        o_ref[...] = acc

    own = pl.BlockSpec((None, tr, cols), lambda i, p: (p[0], part * nt + i, 0))
    grid_spec = pltpu.PrefetchScalarGridSpec(
        num_scalar_prefetch=1, grid=(nt,),
        in_specs=[own, own, pl.BlockSpec((N_CHIPS - 1, tr, cols), lambda i, p: (0, i, 0))],
        out_specs=pl.BlockSpec((tr, cols), lambda i, p: (p[1] * (half // tr) + part * nt + i, 0)))
    return pl.pallas_call(
        body, name=name, grid_spec=grid_spec,
        out_shape=jax.ShapeDtypeStruct((2 * half, cols), F32),
        compiler_params=_params(("parallel",)),
    )(place, mine, rsib, rici)


def _adam_math(w, g, m, v):
    m2 = ADAM_B1 * m + (1.0 - ADAM_B1) * g
    v2 = ADAM_B2 * v + (1.0 - ADAM_B2) * (g * g)
    m_hat = m2 / (1.0 - ADAM_B1 ** ADAM_STEP)
    v_hat = v2 / (1.0 - ADAM_B2 ** ADAM_STEP)
    delta = -ADAM_LR * (m_hat / (jnp.sqrt(v_hat) + ADAM_EPS) + ADAM_WD * w)
    return delta, m2, v2


def _adamw(w, g, m, v, name, part=0, parts=1, prev=None):
    rows, cols = w.shape
    tr = min(rows, ROW_TILE)

    def body(*refs):
        w_ref, g_ref, m_ref, v_ref, go_ref, d_ref, m2_ref, v2_ref = refs[-8:]
        g = g_ref[...]
        go_ref[...] = g
        d_ref[...], m2_ref[...], v2_ref[...] = _adam_math(w_ref[...], g, m_ref[...], v_ref[...])

    if parts == 1:
        grid, spec = (rows // tr,), pl.BlockSpec((tr, cols), lambda i: (i, 0))
    else:
        per_half = rows // 2 // tr
        nt = per_half // parts
        grid, spec = (2, nt), pl.BlockSpec((tr, cols), lambda r, i: (r * per_half + part * nt + i, 0))
    olds = [] if prev is None else list(prev)
    return pl.pallas_call(
        body, name=name, grid=grid,
        out_shape=(jax.ShapeDtypeStruct(w.shape, F32),) * 4,
        in_specs=[HBM] * len(olds) + [spec] * 4, out_specs=(spec,) * 4,
        input_output_aliases={i: i for i in range(len(olds))},
        compiler_params=_params(("parallel",) * len(grid)),
    )(*olds, w, g, m, v)


def _ada_grad_adamw(c_all_t, dmod_cols, w, m, v):
    d, wa = w.shape
    tr = ROW_TILE

    def body(ct_ref, dm_ref, w_ref, m_ref, v_ref, g_ref, d_ref, m2_ref, v2_ref):
        act = _silu(ct_ref[...])
        g = act[:, 0:1] * dm_ref[0:1, :]
        for b in range(1, N_DEV):
            g = g + act[:, b:b + 1] * dm_ref[b:b + 1, :]
        g_ref[...] = g
        d_ref[...], m2_ref[...], v2_ref[...] = _adam_math(w_ref[...], g, m_ref[...], v_ref[...])

    spec = pl.BlockSpec((tr, wa), lambda i: (i, 0))
    return pl.pallas_call(
        body, name="ada_grad_adamw", grid=(d // tr,),
        out_shape=(jax.ShapeDtypeStruct(w.shape, F32),) * 4,
        in_specs=[pl.BlockSpec((tr, N_DEV), lambda i: (i, 0)), pl.BlockSpec((N_DEV, wa), lambda i: (0, 0)),
                  spec, spec, spec],
        out_specs=(spec,) * 4,
        compiler_params=_params(("parallel",)),
    )(c_all_t, dmod_cols, w, m, v)


def _small_update(place, gathered, pieces, weights, moments_m, moments_v):
    n = gathered.shape[1]
    k = len(weights)

    def body(place_ref, g_ref, *refs):
        w_refs, m_refs, v_refs = refs[0:k], refs[k:2 * k], refs[2 * k:3 * k]
        outs = refs[3 * k:]
        total = g_ref[0:SUBLANES, :]
        for dev in range(1, N_DEV):
            total = total + g_ref[SUBLANES * dev:SUBLANES * (dev + 1), :]

        def flat(offset, length):
            segments, pos = [], offset
            while pos < offset + length:
                row, col = divmod(pos, n)
                take = min(offset + length - pos, n - col)
                segments.append(total[row:row + 1, col:col + take])
                pos += take
            return jnp.concatenate(segments, axis=1) if len(segments) > 1 else segments[0]

        chip = place_ref[0]
        for i, (w_ref, m_ref, v_ref) in enumerate(zip(w_refs, m_refs, v_refs)):
            g = flat(*pieces[i])
            if w_ref.ndim == 3:
                rows, cols = w_ref.shape[1:]
                full = pieces[i][1] // rows
                picked = []
                for r in range(rows):
                    blocks = [g[:, r * full + q * cols:r * full + (q + 1) * cols] for q in range(N_CHIPS)]
                    mine = blocks[N_CHIPS - 1]
                    for q in range(N_CHIPS - 2, -1, -1):
                        mine = jnp.where(chip == q, blocks[q], mine)
                    picked.append(mine)
                g = jnp.concatenate(picked, axis=0)
                w, m, v = w_ref[0], m_ref[0], v_ref[0]
            else:
                w, m, v = w_ref[...], m_ref[...], v_ref[...]
            delta, m2, v2 = _adam_math(w, g, m, v)
            for j, val in enumerate((g, delta, m2, v2)):
                out = outs[j * k + i]
                if w_ref.ndim == 3:
                    out[0] = val
                else:
                    out[...] = val
        outs[4 * k][...] = flat(*pieces[k])

    shapes = [jax.ShapeDtypeStruct(w.shape, F32) for w in weights]
    grid_spec = pltpu.PrefetchScalarGridSpec(
        num_scalar_prefetch=1, grid=(1,),
        in_specs=[pl.BlockSpec(gathered.shape, lambda i, p: (0, 0))]
        + [pl.BlockSpec(a.shape, functools.partial(lambda nd, i, p: (0,) * nd, a.ndim))
           for a in (*weights, *moments_m, *moments_v)],
        out_specs=tuple(pl.BlockSpec(s.shape, functools.partial(lambda nd, i, p: (0,) * nd, len(s.shape)))
                        for s in shapes * 4) + (pl.BlockSpec((1, LANES), lambda i, p: (0, 0)),))
    outs = pl.pallas_call(
        body, name="small_update", grid_spec=grid_spec,
        out_shape=tuple(shapes * 4) + (jax.ShapeDtypeStruct((1, LANES), F32),),
        compiler_params=_params(("arbitrary",)),
    )(place, gathered, *weights, *moments_m, *moments_v)
    return outs[0:k], outs[k:2 * k], outs[2 * k:3 * k], outs[3 * k:4 * k], outs[4 * k]


def _pack_small(pieces):
    flat = [p.reshape(-1).astype(F32) for p in pieces]
    offsets, total = [], 0
    for p in flat:
        offsets.append(total)
        total += p.shape[0]
    padded = -(-total // SMALL_ALIGN) * SMALL_ALIGN
    if padded > total:
        flat.append(jnp.zeros((padded - total,), F32))
    return jnp.concatenate(flat).reshape(8, padded // 8), offsets


def _alibi_slope_rows(n_heads):
    slopes = 2.0 ** (-8.0 * jnp.arange(1, n_heads + 1, dtype=F32) / n_heads)
    rows = jnp.zeros((n_heads // 2, 8), F32).at[:, 0:2].set(slopes.reshape(n_heads // 2, 2))
    return jnp.broadcast_to(rows[:, :, None], (n_heads // 2, 8, ATT_KW))


def kernel(x, c, w_ada, b_ada, g_pre, w_in, conv_w, conv_b, g_conv, g_attn, w_out, g_post, loss_target, m_w_ada, m_b_ada, m_g_pre, m_w_in, m_conv_w, m_conv_b, m_g_conv, m_g_attn, m_w_out, m_g_post, v_w_ada, v_b_ada, v_g_pre, v_w_in, v_conv_w, v_conv_b, v_g_conv, v_g_attn, v_w_out, v_g_post):
    t, d = x.shape[1], x.shape[2]
    dc = conv_b.shape[1]
    da = g_attn.shape[1]
    hp = da // PAIR
    ws = w_in.shape[2]
    wa = w_ada.shape[2]
    cws = conv_w.shape[2]
    assert t % ROW_TILE == 0 and d % ROW_TILE == 0 and dc % COL_TILE == 0 and da % COL_TILE == 0
    assert ws == 2 * dc and dc == da and t // BRANCHES[-1][1] >= ATT_BQ

    mx, my, mc = _my_place()
    chip = _chip_of(mx, my)
    dev = 2 * chip + mc
    place = jnp.stack([chip, mc]).astype(jnp.int32)

    x2, tgt2 = x[0], loss_target[0]
    w_ada2, w_in2, w_out2 = w_ada[0], w_in[0], w_out[0]

    win_slots = _cast_into_slot(place, w_in2, "cast_w_in")
    packed, offs = _pack_small([c[0], conv_w[0]])
    seen, mod = _ada_modulation(packed, w_ada2, b_ada, d, after=(win_slots,))
    seen = seen.reshape(N_DEV, -1)
    c_all = seen[:, offs[0]:offs[0] + d]
    conv_w_full = seen[0::2, offs[1]:offs[1] + 3 * cws].reshape(N_CHIPS, 3, cws).transpose(1, 0, 2).reshape(3, dc)

    win_flight, send_in, recv_in, started = _gather_start(win_slots, mod)

    y_chip, x_chip, d_chip = (_chip_of(mx, 1 - my), _chip_of(1 - mx, my), _chip_of(1 - mx, 1 - my))
    tiles_per_part = ws // COL_TILE // 2

    def tiles_of(chunk, parts):
        return [(2 * chunk + part) * tiles_per_part + k for part in parts for k in range(tiles_per_part)]

    own_tiles, first_tiles, second_tiles, far_tiles = (jnp.stack(tiles).astype(jnp.int32) for tiles in (
        tiles_of(chip, (0, 1)), tiles_of(y_chip, (0,)) + tiles_of(x_chip, (1,)),
        tiles_of(y_chip, (1,)) + tiles_of(x_chip, (0,)), tiles_of(d_chip, (0, 1))))
    h, ht = _prenorm(x2, mod, g_pre, started)
    proj = _proj_tiles(None, h, w_in2, own_tiles, "proj_own")
    win_flight, wout_flight, relay_send_in, relay_recv_in, send_out, recv_out = _gather_relay_in(
        win_flight, _cast_into_slot(place, w_out2, "cast_w_out"), recv_in, proj)
    win_flight = _forward_halves(win_flight, ((0, 0), (1, 1)), "forward_w_in_first")
    proj = _proj_tiles(proj, h, win_flight, first_tiles, "proj_first_parts")
    win_flight = _forward_halves(
        _gather_wait_direct(win_flight, send_in, recv_in, proj, "gather_wait_w_in_direct"),
        ((0, 1), (1, 0)), "forward_w_in_second")
    proj = _proj_tiles(proj, h, win_flight, second_tiles, "proj_second_parts")
    winf = _forward_halves(
        _gather_wait_relayed(win_flight, relay_send_in, relay_recv_in, proj, "gather_wait_w_in_relayed"),
        ((2, None),), "forward_w_in_relayed")
    proj = _proj_tiles(proj, h, winf, far_tiles, "proj_diagonal")
    slopes = _alibi_slope_rows(da // HEAD_DIM)
    co = _conv_fwd(proj, conv_w_full, conv_b, dc)
    wout_flight, relay_send_out, relay_recv_out = _gather_relay_out(wout_flight, recv_out, co)
    o_mix, lse = _attn_fwd(proj, slopes, dc, da)
    g_attn_pairs = g_attn.reshape(hp, 1, PAIR)
    wout_flight = _gather_wait_direct(wout_flight, send_out, recv_out, o_mix, "gather_wait_w_out_direct")
    wout_flight = _gather_wait_relayed(wout_flight, relay_send_out, relay_recv_out, o_mix, "gather_wait_w_out_relayed")
    all_halves = ((0, None), (1, None), (2, None))
    wout_flight, fsend_out, frecv_out, forwarding = _forward_start(wout_flight, all_halves, "forward_w_out_start")
    ycat, ycat_t = _mix_fwd(co, proj, o_mix, g_conv, g_attn_pairs, forwarding)
    woutf = _forward_wait(wout_flight, all_halves, fsend_out, frecv_out, ycat, "forward_w_out_wait").reshape(dc + da, d)
    dout, dy, post_sums = _out_fwd_bwd(ycat, woutf, x2, tgt2, mod, g_post)

    gout, rsib_out = _dw_swapped(ycat_t, dy, N_CHIPS, 1, "dw_out")
    csum_out = _chip_sums(gout, rsib_out, "rs_chip_sum_out")
    ssem_out, rsem_out, csum_out, land_out, sent_out = _owners_start(csum_out, "rs_owners_start_out")
    dycat = _matmul_nt(dy, woutf, BF16, "dycat")
    dproj, dco, d_o, delta, dg_conv, dg_attn = _mix_bwd(dycat, co, proj, o_mix, g_conv, g_attn_pairs)
    dproj, conv_sums = _conv_bwd(dproj, dco, proj, conv_w_full, dc, sent_out)
    dproj = _attn_bwd(dproj, proj, d_o, lse, delta, slopes, dc, da, sent_out)
    gin, rsib_in = _dw_swapped(ht, dproj, 1, N_CHIPS, "dw_in")
    ssem_in0, rsem_in0, csum_in0, land_in0, sent_in0 = _owners_start(
        _chip_sums(gin, rsib_in, "rs_chip_sum_in0", 0, 2), "rs_owners_start_in0")
    ssem_in1, rsem_in1, csum_in1, land_in1, sent_in = _owners_start(
        _chip_sums(gin, rsib_in, "rs_chip_sum_in1", 1, 2, after=(sent_in0,)), "rs_owners_start_in1")
    dh = _dh(dproj, winf, sent_in)
    grad_x, pre_sums = _prenorm_bwd(x2, dh, dout, mod, g_pre)

    small, so = _pack_small([
        pre_sums[0], pre_sums[1], post_sums[0],
        pre_sums[2], conv_sums[0:3], conv_sums[3], dg_conv, dg_attn, post_sums[1], post_sums[2, 0:128]])
    ssem_small, rsem_small, small, land_small, sent_small = _allgather8_start(small, dev, "gather_small_start")

    rici_out = _owners_wait(ssem_out, rsem_out, csum_out, land_out, [grad_x, sent_small], "rs_owners_wait_out")
    full_out, jsend_out, jrecv_out, joining_out = _join_start(
        _owner_sum(place, gout, rsib_out, rici_out, "rs_owner_sum_out"), "rs_join_start_out", 0, 1)
    rici_in = _owners_wait(ssem_in0, rsem_in0, csum_in0, land_in0, [joining_out], "rs_owners_wait_in0")
    full_in0, jsend0, jrecv0, joining0 = _join_start(
        _owner_sum(place, gin, rsib_in, rici_in, "rs_owner_sum_in0", 0, 2), "rs_join_start_in0", 0, 2)
    grad_w_out = _join_wait(full_out, jsend_out, jrecv_out, [joining0], "rs_join_wait_out", 0, 1)
    grad_w_out, delta_w_out, new_m_w_out, new_v_w_out = _adamw(
        w_out2, grad_w_out, m_w_out[0], v_w_out[0], "adamw_w_out")
    full_in0 = _join_wait(full_in0, jsend0, jrecv0, [delta_w_out], "rs_join_wait_in0", 0, 2)
    updated_in = _adamw(w_in2, full_in0, m_w_in[0], v_w_in[0], "adamw_w_in0", 0, 2)
    rici_in = _owners_wait(ssem_in1, rsem_in1, csum_in1, land_in1, [updated_in[1]], "rs_owners_wait_in1")
    full_in1, jsend1, jrecv1, joining1 = _join_start(
        _owner_sum(place, gin, rsib_in, rici_in, "rs_owner_sum_in1", 1, 2), "rs_join_start_in1", 1, 2)

    small_seen = _allgather8_wait(ssem_small, rsem_small, small, land_small, [joining1], "gather_small_wait")
    small_w = [b_ada, g_pre, conv_w, conv_b, g_conv, g_attn, g_post]
    small_m = [m_b_ada, m_g_pre, m_conv_w, m_conv_b, m_g_conv, m_g_attn, m_g_post]
    small_v = [v_b_ada, v_g_pre, v_conv_w, v_conv_b, v_g_conv, v_g_attn, v_g_post]
    pieces = [(0, 3 * d), (so[3], d), (so[4], 3 * dc), (so[5], dc), (so[6], dc), (so[7], da), (so[8], d), (so[9], LANES)]
    g_small, d_small, m_small, v_small, loss_row = _small_update(place, small_seen, pieces, small_w, small_m, small_v)
    loss = loss_row[0, 0]
    grad_b_ada, grad_g_pre, grad_conv_w, grad_conv_b, grad_g_conv, grad_g_attn, grad_g_post = g_small
    dmod_cols = lax.dynamic_slice_in_dim(small_seen.reshape(N_DEV, -1), chip * wa, wa, axis=1)
    grad_w_ada, delta_w_ada, new_m_w_ada, new_v_w_ada = _ada_grad_adamw(c_all.T, dmod_cols, w_ada2, m_w_ada[0], v_w_ada[0])

    full_in1 = _join_wait(full_in1, jsend1, jrecv1, [delta_w_ada, d_small[0]], "rs_join_wait_in1", 1, 2)
    grad_w_in, delta_w_in, new_m_w_in, new_v_w_in = _adamw(
        w_in2, full_in1, m_w_in[0], v_w_in[0], "adamw_w_in1", 1, 2, updated_in)

    def lead(a):
        return a.reshape((1,) + a.shape)

    grads = [lead(grad_w_ada), grad_b_ada, grad_g_pre, lead(grad_w_in), grad_conv_w, grad_conv_b, grad_g_conv,
             grad_g_attn, lead(grad_w_out), grad_g_post]
    deltas = [lead(delta_w_ada), d_small[0], d_small[1], lead(delta_w_in), d_small[2], d_small[3], d_small[4],
              d_small[5], lead(delta_w_out), d_small[6]]
    new_ms = [lead(new_m_w_ada), m_small[0], m_small[1], lead(new_m_w_in), m_small[2], m_small[3], m_small[4],
              m_small[5], lead(new_m_w_out), m_small[6]]
    new_vs = [lead(new_v_w_ada), v_small[0], v_small[1], lead(new_v_w_in), v_small[2], v_small[3], v_small[4],
              v_small[5], lead(new_v_w_out), v_small[6]]
    return (loss, lead(grad_x), *grads, *deltas, *new_ms, *new_vs)
```

```python
import functools

import jax
import jax.numpy as jnp
from jax import lax
from jax.experimental import pallas as pl
from jax.experimental.pallas import tpu as pltpu

F32 = jnp.float32
BF16 = jnp.bfloat16
MESH = pl.DeviceIdType.MESH
HBM = pl.BlockSpec(memory_space=pltpu.HBM)
VMEM = pl.BlockSpec(memory_space=pltpu.VMEM)
ANY = pl.BlockSpec(memory_space=pl.ANY)
SEM = pl.BlockSpec(memory_space=pltpu.SEMAPHORE)
EFFECT = pltpu.SideEffectType.DATAFLOW_SIDE_EFFECTING
SUBLANES, LANES = 8, 128
TOKEN = jax.ShapeDtypeStruct((SUBLANES, LANES), jnp.float32)

HEAD_DIM = 64
PAIR = 2 * HEAD_DIM
assert PAIR == LANES
BRANCHES = ((128, 1), (512, 4), (2048, 16))
SIDE = 64
EPS = 1e-6
NEG_INF = -1e30
N_CHIPS = 4
N_DEV = 8

ADAM_LR = 0.001
ADAM_B1 = 0.9
ADAM_B2 = 0.999
ADAM_EPS = 1e-08
ADAM_WD = 0.01
ADAM_STEP = 10

VMEM_LIMIT_BYTES = 56 * 1024 * 1024
ROW_TILE = 256
COL_TILE = 512
CONV_TILE = 256
ATT_BQ = 128
ATT_KW = ATT_BQ + 2 * SIDE
ATT_UNROLL = 4
SMALL_ALIGN = SUBLANES * LANES


def _params(semantics=None):
    kw = {"vmem_limit_bytes": VMEM_LIMIT_BYTES}
    if semantics is not None:
        kw["dimension_semantics"] = semantics
    return pltpu.CompilerParams(**kw)


def _silu(z):
    return z * jax.nn.sigmoid(z)


def _silu_grad(z):
    s = jax.nn.sigmoid(z)
    return s * (1.0 + z * (1.0 - s))


def _my_place():
    return lax.axis_index("x"), lax.axis_index("y"), lax.axis_index("c")


def _flip(a, bit):
    return 1 - a if bit else a


def _chip_of(x, y):
    return 2 * x + y


def _allgather8_start(v, me, name):
    rows_per, n = v.shape
    land = lax.dynamic_update_slice(jnp.zeros((N_DEV * rows_per, n), v.dtype), v, (me * rows_per, 0))

    def body(v_ref, land_ref, send_sems, recv_sems, v_thru, land_thru, token_ref):
        del v_thru, land_thru
        x, y, c = _my_place()
        mine = land_ref.at[pl.ds(pl.multiple_of((4 * x + 2 * y + c) * rows_per, rows_per), rows_per), :]
        for k in range(1, N_DEV):
            peer = (_flip(x, k & 4), _flip(y, k & 2), _flip(c, k & 1))
            pltpu.make_async_remote_copy(
                src_ref=v_ref, dst_ref=mine, send_sem=send_sems.at[k - 1], recv_sem=recv_sems.at[k - 1],
                device_id=peer, device_id_type=MESH).start()
        token_ref[...] = jnp.zeros(token_ref.shape, F32)

    sems = pltpu.SemaphoreType.DMA((N_DEV - 1,))
    return pl.pallas_call(
        body, name=name,
        out_shape=(sems, sems, jax.ShapeDtypeStruct(v.shape, v.dtype), jax.ShapeDtypeStruct(land.shape, land.dtype), TOKEN),
        in_specs=[HBM, HBM], out_specs=(SEM, SEM, HBM, HBM, VMEM),
        input_output_aliases={0: 2, 1: 3},
        compiler_params=pltpu.CompilerParams(has_side_effects=EFFECT),
    )(pltpu.with_memory_space_constraint(v, pltpu.HBM), pltpu.with_memory_space_constraint(land, pltpu.HBM))


def _allgather8_wait(send_sems, recv_sems, v, land, after, name):
    rows_per = v.shape[0]

    def body(v_ref, land_ref, send_ref, recv_ref, *rest):
        del rest
        x, y, c = _my_place()
        for k in range(1, N_DEV):
            peer = (_flip(x, k & 4), _flip(y, k & 2), _flip(c, k & 1))
            src = 4 * peer[0] + 2 * peer[1] + peer[2]
            cp = pltpu.make_async_remote_copy(
                src_ref=v_ref, dst_ref=land_ref.at[pl.ds(pl.multiple_of(src * rows_per, rows_per), rows_per), :],
                send_sem=send_ref.at[k - 1], recv_sem=recv_ref.at[k - 1], device_id=peer, device_id_type=MESH)
            cp.wait_send()
            cp.wait_recv()

    return pl.pallas_call(
        body, name=name,
        out_shape=(jax.ShapeDtypeStruct(v.shape, v.dtype), jax.ShapeDtypeStruct(land.shape, land.dtype)),
        in_specs=[HBM, HBM, SEM, SEM] + [ANY] * len(after), out_specs=(HBM, HBM),
        input_output_aliases={0: 0, 1: 1},
        compiler_params=pltpu.CompilerParams(has_side_effects=EFFECT),
    )(v, land, send_sems, recv_sems, *after)[1]


def _half_rows(ref, chip, which, half):
    return ref.at[chip, pl.ds(pl.multiple_of(which * half, half), half), :]


def _ici_peers(x, y, c):
    peers = [(_flip(x, k & 2), _flip(y, k & 1), c) for k in (1, 2, 3)]
    return [(peer, _chip_of(peer[0], peer[1])) for peer in peers]


def _part_of_half(ref, chip, core, part):
    half, cols = ref.shape[1] // 2, ref.shape[2] // 2
    return ref.at[chip, pl.ds(pl.multiple_of(core * half, half), half), pl.ds(part * cols, cols)]


def _neighbours(x, y, c):
    return [((x, 1 - y, c), _chip_of(x, 1 - y)), ((1 - x, y, c), _chip_of(1 - x, y)),
            ((1 - x, 1 - y, c), _chip_of(1 - x, 1 - y))]


def _start_direct(buf, send_sems, recv_sems):
    x, y, c = _my_place()
    me = _chip_of(x, y)
    for n, (peer, _) in enumerate(_neighbours(x, y, c)[0:2]):
        for part in ((0, 1), (1, 0))[n]:
            piece = _part_of_half(buf, me, c, part)
            pltpu.make_async_remote_copy(
                src_ref=piece, dst_ref=piece, send_sem=send_sems.at[2 * n + part], recv_sem=recv_sems.at[2 * n + part],
                device_id=peer, device_id_type=MESH).start()


def _relay(buf, recv_sems, relay_send, relay_recv):
    x, y, c = _my_place()
    nbrs = _neighbours(x, y, c)
    for n in range(2):
        part = n
        piece = _part_of_half(buf, nbrs[n][1], c, part)
        pltpu.make_async_remote_copy(
            src_ref=piece, dst_ref=piece, send_sem=relay_send.at[part], recv_sem=recv_sems.at[2 * n + part],
            device_id=nbrs[n][0], device_id_type=MESH).wait_recv()
        pltpu.make_async_remote_copy(
            src_ref=piece, dst_ref=piece, send_sem=relay_send.at[part], recv_sem=relay_recv.at[part],
            device_id=nbrs[1 - n][0], device_id_type=MESH).start()


def _gather_start(win_slots, after):
    def body(win_in, after_ref, win_ref, send_sems, recv_sems, token_ref):
        del win_in, after_ref
        _start_direct(win_ref, send_sems, recv_sems)
        token_ref[...] = jnp.zeros(token_ref.shape, F32)

    sems = pltpu.SemaphoreType.DMA((4,))
    return pl.pallas_call(
        body, name="gather_start",
        out_shape=(jax.ShapeDtypeStruct(win_slots.shape, win_slots.dtype), sems, sems, TOKEN),
        in_specs=[HBM, ANY], out_specs=(HBM, SEM, SEM, VMEM),
        input_output_aliases={0: 0},
        compiler_params=pltpu.CompilerParams(has_side_effects=EFFECT),
    )(win_slots, after)


def _gather_relay_in(win, wout_slots, recv_in, after):
    def body(win_in, wout_in, recv_in_ref, after_ref, win_ref, wout_ref, relay_send, relay_recv, send_out, recv_out):
        del win_in, wout_in, after_ref
        _relay(win_ref, recv_in_ref, relay_send, relay_recv)
        _start_direct(wout_ref, send_out, recv_out)

    two, four = pltpu.SemaphoreType.DMA((2,)), pltpu.SemaphoreType.DMA((4,))
    return pl.pallas_call(
        body, name="gather_relay_w_in",
        out_shape=(jax.ShapeDtypeStruct(win.shape, win.dtype), jax.ShapeDtypeStruct(wout_slots.shape, wout_slots.dtype),
                   two, two, four, four),
        in_specs=[HBM, HBM, SEM, ANY], out_specs=(HBM, HBM, SEM, SEM, SEM, SEM),
        input_output_aliases={0: 0, 1: 1},
        compiler_params=pltpu.CompilerParams(has_side_effects=EFFECT),
    )(win, wout_slots, recv_in, after)


def _gather_relay_out(wout, recv_out, after):
    def body(wout_in, recv_out_ref, after_ref, wout_ref, relay_send, relay_recv):
        del wout_in, after_ref
        _relay(wout_ref, recv_out_ref, relay_send, relay_recv)

    two = pltpu.SemaphoreType.DMA((2,))
    return pl.pallas_call(
        body, name="gather_relay_w_out",
        out_shape=(jax.ShapeDtypeStruct(wout.shape, wout.dtype), two, two),
        in_specs=[HBM, SEM, ANY], out_specs=(HBM, SEM, SEM),
        input_output_aliases={0: 0},
        compiler_params=pltpu.CompilerParams(has_side_effects=EFFECT),
    )(wout, recv_out, after)


def _gather_wait_direct(buf, send_sems, recv_sems, after, name):
    def body(buf_in, send_ref, recv_ref, after_ref, buf_ref):
        del buf_in, after_ref
        x, y, c = _my_place()
        me = _chip_of(x, y)
        for n, (peer, chip) in enumerate(_neighbours(x, y, c)[0:2]):
            second = 1 - n
            pltpu.make_async_remote_copy(
                src_ref=_part_of_half(buf_ref, me, c, second), dst_ref=_part_of_half(buf_ref, chip, c, second),
                send_sem=send_ref.at[2 * n + second], recv_sem=recv_ref.at[2 * n + second],
                device_id=peer, device_id_type=MESH).wait_recv()
            for part in range(2):
                piece = _part_of_half(buf_ref, me, c, part)
                pltpu.make_async_remote_copy(
                    src_ref=piece, dst_ref=piece, send_sem=send_ref.at[2 * n + part], recv_sem=recv_ref.at[2 * n + part],
                    device_id=peer, device_id_type=MESH).wait_send()

    return pl.pallas_call(
        body, name=name,
        out_shape=jax.ShapeDtypeStruct(buf.shape, buf.dtype),
        in_specs=[HBM, SEM, SEM, ANY], out_specs=HBM,
        input_output_aliases={0: 0},
        compiler_params=pltpu.CompilerParams(has_side_effects=EFFECT),
    )(buf, send_sems, recv_sems, after)


def _gather_wait_relayed(buf, relay_send, relay_recv, after, name):
    def body(buf_in, rsend_ref, rrecv_ref, after_ref, buf_ref):
        del buf_in, after_ref
        x, y, c = _my_place()
        nbrs = _neighbours(x, y, c)
        for n in range(2):
            relayed = _part_of_half(buf_ref, nbrs[n][1], c, n)
            cp = pltpu.make_async_remote_copy(
                src_ref=relayed, dst_ref=_part_of_half(buf_ref, nbrs[2][1], c, n),
                send_sem=rsend_ref.at[n], recv_sem=rrecv_ref.at[n], device_id=nbrs[1 - n][0], device_id_type=MESH)
            cp.wait_recv()
            cp.wait_send()

    return pl.pallas_call(
        body, name=name,
        out_shape=jax.ShapeDtypeStruct(buf.shape, buf.dtype),
        in_specs=[HBM, SEM, SEM, ANY], out_specs=HBM,
        input_output_aliases={0: 0},
        compiler_params=pltpu.CompilerParams(has_side_effects=EFFECT),
    )(buf, relay_send, relay_recv, after)


def _forward_copies(buf_ref, which, send_sems, recv_sems):
    half = buf_ref.shape[1] // 2
    x, y, c = _my_place()

    def copy(k, chip, core, part):
        piece = _half_rows(buf_ref, chip, core, half) if part is None else _part_of_half(buf_ref, chip, core, part)
        return pltpu.make_async_remote_copy(
            src_ref=piece, dst_ref=piece, send_sem=send_sems.at[k], recv_sem=recv_sems.at[k],
            device_id=(x, y, 1 - c), device_id_type=MESH)

    chips = [_neighbours(x, y, c)[n][1] for n, _ in which]
    return [(copy(k, chip, c, part), copy(k, chip, 1 - c, part)) for k, (chip, (_, part)) in enumerate(zip(chips, which))]


def _forward_halves(buf, which, name):
    def body(buf_in, buf_ref, send_sems, recv_sems):
        del buf_in
        copies = _forward_copies(buf_ref, which, send_sems, recv_sems)
        for mine, _ in copies:
            mine.start()
        for mine, theirs in copies:
            theirs.wait_recv()
        for mine, _ in copies:
            mine.wait_send()

    return pl.pallas_call(
        body, name=name,
        out_shape=jax.ShapeDtypeStruct(buf.shape, buf.dtype),
        in_specs=[HBM], out_specs=HBM,
        input_output_aliases={0: 0},
        scratch_shapes=[pltpu.SemaphoreType.DMA((len(which),))] * 2,
    )(buf)


def _forward_start(buf, which, name):
    def body(buf_in, buf_ref, send_sems, recv_sems, token_ref):
        del buf_in
        for mine, _ in _forward_copies(buf_ref, which, send_sems, recv_sems):
            mine.start()
        token_ref[...] = jnp.zeros(token_ref.shape, F32)

    sems = pltpu.SemaphoreType.DMA((len(which),))
    return pl.pallas_call(
        body, name=name,
        out_shape=(jax.ShapeDtypeStruct(buf.shape, buf.dtype), sems, sems, TOKEN),
        in_specs=[HBM], out_specs=(HBM, SEM, SEM, VMEM),
        input_output_aliases={0: 0},
        compiler_params=pltpu.CompilerParams(has_side_effects=EFFECT),
    )(buf)


def _forward_wait(buf, which, send_sems, recv_sems, after, name):
    def body(buf_in, send_ref, recv_ref, after_ref, buf_ref):
        del buf_in, after_ref
        for mine, theirs in _forward_copies(buf_ref, which, send_ref, recv_ref):
            theirs.wait_recv()
            mine.wait_send()

    return pl.pallas_call(
        body, name=name,
        out_shape=jax.ShapeDtypeStruct(buf.shape, buf.dtype),
        in_specs=[HBM, SEM, SEM, ANY], out_specs=HBM,
        input_output_aliases={0: 0},
        compiler_params=pltpu.CompilerParams(has_side_effects=EFFECT),
    )(buf, send_sems, recv_sems, after)


def _dw_swapped(a, b, row_chunks, col_chunks, name):
    r, t = a.shape
    c_all = b.shape[1]
    chunks = row_chunks * col_chunks
    rq, cq = r // row_chunks, c_all // col_chunks
    half = rq // 2
    tn = COL_TILE
    nt = cq // tn
    steps = col_chunks * nt

    def body(a_ref, b_ref, mine_ref, sib_ref, stage, send_sems, recv_sems):
        x, y, c = _my_place()
        j, n = pl.program_id(0), pl.program_id(1)
        step = j * nt + n
        slot = step % 2
        res = jnp.dot(a_ref[...], b_ref[...], preferred_element_type=F32).astype(BF16)

        def landing(jj, nn):
            cols = pl.ds(pl.multiple_of(nn * tn, tn), tn)
            return sib_ref.at[:, :, cols] if col_chunks == 1 else sib_ref.at[pl.ds(jj, 1), :, cols]

        def copy(slot_, step_, jj, nn):
            return pltpu.make_async_remote_copy(
                src_ref=stage.at[slot_], dst_ref=landing(jj, nn), send_sem=send_sems.at[slot_],
                recv_sem=recv_sems.at[step_], device_id=(x, y, 1 - c), device_id_type=MESH)

        @pl.when(step >= 2)
        def _():
            copy(slot, step, j, n).wait_send()

        for q in range(row_chunks):
            lo = res[q * rq:q * rq + half, :]
            hi = res[q * rq + half:(q + 1) * rq, :]
            mine_ref[q] = jnp.where(c == 0, lo, hi)
            stage[slot, q] = jnp.where(c == 0, hi, lo)
        copy(slot, step, j, n).start()

        @pl.when(step == steps - 1)
        def _():
            for s in range(max(steps - 2, 0), steps):
                copy(s % 2, s, j, n).wait_send()
            for s in range(steps):
                copy(s % 2, s, j, n).wait_recv()

    shape = jax.ShapeDtypeStruct((chunks, half, cq), BF16)
    return pl.pallas_call(
        body, name=name, grid=(col_chunks, nt),
        out_shape=(shape, shape),
        in_specs=[pl.BlockSpec((r, t), lambda j, n: (0, 0)), pl.BlockSpec((t, tn), lambda j, n: (0, j * nt + n))],
        out_specs=(pl.BlockSpec((row_chunks, half, tn), lambda j, n: (j, 0, n)), ANY),
        scratch_shapes=[pltpu.VMEM((2, row_chunks, half, tn), BF16), pltpu.SemaphoreType.DMA((2,)),
                        pltpu.SemaphoreType.DMA((steps,))],
        compiler_params=_params(("arbitrary", "arbitrary")),
    )(a, b)


def _owners_start(csum, name, after=()):
    land = pltpu.with_memory_space_constraint(lax.empty((N_CHIPS - 1,) + csum.shape[1:], csum.dtype), pltpu.HBM)

    def body(csum_ref, land_ref, *rest):
        send_sems, recv_sems, _, _, token_ref = rest[len(after):]
        x, y, c = _my_place()
        for k, (peer, owner) in enumerate(_ici_peers(x, y, c)):
            pltpu.make_async_remote_copy(
                src_ref=csum_ref.at[owner], dst_ref=land_ref.at[k], send_sem=send_sems.at[k], recv_sem=recv_sems.at[k],
                device_id=peer, device_id_type=MESH).start()
        token_ref[...] = jnp.zeros(token_ref.shape, F32)

    sems = pltpu.SemaphoreType.DMA((N_CHIPS - 1,))
    return pl.pallas_call(
        body, name=name,
        out_shape=(sems, sems, jax.ShapeDtypeStruct(csum.shape, csum.dtype),
                   jax.ShapeDtypeStruct(land.shape, land.dtype), TOKEN),
        in_specs=[HBM, HBM] + [ANY] * len(after), out_specs=(SEM, SEM, HBM, HBM, VMEM),
        input_output_aliases={0: 2, 1: 3},
        compiler_params=pltpu.CompilerParams(has_side_effects=EFFECT),
    )(pltpu.with_memory_space_constraint(csum, pltpu.HBM), land, *after)


def _owners_wait(send_sems, recv_sems, csum, land, after, name):
    def body(csum_ref, land_ref, send_ref, recv_ref, *rest):
        del rest
        x, y, c = _my_place()
        for k, (peer, owner) in enumerate(_ici_peers(x, y, c)):
            cp = pltpu.make_async_remote_copy(
                src_ref=csum_ref.at[owner], dst_ref=land_ref.at[k], send_sem=send_ref.at[k], recv_sem=recv_ref.at[k],
                device_id=peer, device_id_type=MESH)
            cp.wait_send()
            cp.wait_recv()

    return pl.pallas_call(
        body, name=name,
        out_shape=(jax.ShapeDtypeStruct(csum.shape, csum.dtype), jax.ShapeDtypeStruct(land.shape, land.dtype)),
        in_specs=[HBM, HBM, SEM, SEM] + [ANY] * len(after), out_specs=(HBM, HBM),
        input_output_aliases={0: 0, 1: 1},
        compiler_params=pltpu.CompilerParams(has_side_effects=EFFECT),
    )(csum, land, send_sems, recv_sems, *after)[1]


def _join_start(full, name, part, parts):
    half = full.shape[0] // 2
    rows = half // parts

    def body(full_in, full_ref, send_sem, recv_sem, token_ref):
        del full_in
        x, y, c = _my_place()
        mine = full_ref.at[pl.ds(pl.multiple_of(c * half + part * rows, rows), rows), :]
        pltpu.make_async_remote_copy(
            src_ref=mine, dst_ref=mine, send_sem=send_sem.at[0], recv_sem=recv_sem.at[0],
            device_id=(x, y, 1 - c), device_id_type=MESH).start()
        token_ref[...] = jnp.zeros(token_ref.shape, F32)

    one = pltpu.SemaphoreType.DMA((1,))
    return pl.pallas_call(
        body, name=name,
        out_shape=(jax.ShapeDtypeStruct(full.shape, full.dtype), one, one, TOKEN),
        in_specs=[HBM], out_specs=(HBM, SEM, SEM, VMEM),
        input_output_aliases={0: 0},
        compiler_params=pltpu.CompilerParams(has_side_effects=EFFECT),
    )(full)


def _join_wait(full, send_sem, recv_sem, after, name, part, parts):
    half = full.shape[0] // 2
    rows = half // parts

    def body(full_in, send_ref, recv_ref, *rest):
        del full_in
        full_ref = rest[-1]
        x, y, c = _my_place()
        cp = pltpu.make_async_remote_copy(
            src_ref=full_ref.at[pl.ds(pl.multiple_of(c * half + part * rows, rows), rows), :],
            dst_ref=full_ref.at[pl.ds(pl.multiple_of((1 - c) * half + part * rows, rows), rows), :],
            send_sem=send_ref.at[0], recv_sem=recv_ref.at[0], device_id=(x, y, 1 - c), device_id_type=MESH)
        cp.wait_send()
        cp.wait_recv()

    return pl.pallas_call(
        body, name=name,
        out_shape=jax.ShapeDtypeStruct(full.shape, full.dtype),
        in_specs=[HBM, SEM, SEM] + [ANY] * len(after), out_specs=HBM,
        input_output_aliases={0: 0},
        compiler_params=pltpu.CompilerParams(has_side_effects=EFFECT),
    )(full, send_sem, recv_sem, *after)


def _cast_into_slot(place, w, name):
    rows, cols = w.shape
    tr = min(rows, ROW_TILE)

    def body(place_ref, w_ref, o_ref):
        del place_ref
        o_ref[...] = w_ref[...].astype(BF16)

    grid_spec = pltpu.PrefetchScalarGridSpec(
        num_scalar_prefetch=1, grid=(rows // tr,),
        in_specs=[pl.BlockSpec((tr, cols), lambda i, p: (i, 0))],
        out_specs=pl.BlockSpec((None, tr, cols), lambda i, p: (p[0], i, 0)))
    return pl.pallas_call(
        body, name=name, grid_spec=grid_spec,
        out_shape=jax.ShapeDtypeStruct((N_CHIPS, rows, cols), BF16),
        compiler_params=_params(("parallel",)),
    )(place, w)


def _ada_modulation(packed, w_ada, b_ada, d, w_big):
    rows_per, n = packed.shape
    d_model, wa = w_ada.shape
    big_rows, big_cols = w_big.shape
    n_chunks = big_rows // ROW_TILE
    first_chunks = (2 * n_chunks) // 3

    def body(v_ref, w_hbm, b_ref, big_hbm, all_ref, mod_ref, slots_hbm, w_vmem, part_ref, parts_ref, wide, narrow,
             load_sem, send1, recv1, send2, recv2, in_sems, out_sems):
        x, y, c = _my_place()
        me = 4 * x + 2 * y + c
        chip = _chip_of(x, y)
        load = pltpu.make_async_copy(w_hbm, w_vmem, load_sem)
        load.start()

        def chunk_in(i):
            return pltpu.make_async_copy(big_hbm.at[i * ROW_TILE:(i + 1) * ROW_TILE, :], wide.at[i % 2], in_sems.at[i % 2])

        def chunk_out(i):
            return pltpu.make_async_copy(
                narrow.at[i % 2], slots_hbm.at[chip, i * ROW_TILE:(i + 1) * ROW_TILE, :], out_sems.at[i % 2])

        def cast_chunk(i):
            if i + 1 < n_chunks:
                chunk_in(i + 1).start()
            chunk_in(i).wait()
            if i >= 2:
                chunk_out(i - 2).wait()
            narrow[i % 2] = wide[i % 2].astype(BF16)
            chunk_out(i).start()

        chunk_in(0).start()

        def rows(idx):
            return all_ref.at[pl.ds(pl.multiple_of(idx * rows_per, rows_per), rows_per), :]

        all_ref[pl.ds(pl.multiple_of(me * rows_per, rows_per), rows_per), :] = v_ref[...]
        copies = []
        for k in range(1, N_DEV):
            peer = (_flip(x, k & 4), _flip(y, k & 2), _flip(c, k & 1))
            cp = pltpu.make_async_remote_copy(
                src_ref=v_ref, dst_ref=rows(me), send_sem=send1.at[k - 1], recv_sem=recv1.at[k - 1],
                device_id=peer, device_id_type=MESH)
            cp.start()
            copies.append((cp, peer))
        for i in range(first_chunks):
            cast_chunk(i)
        for k, (cp, peer) in enumerate(copies):
            pltpu.make_async_remote_copy(
                src_ref=v_ref, dst_ref=rows(4 * peer[0] + 2 * peer[1] + peer[2]), send_sem=send1.at[k],
                recv_sem=recv1.at[k], device_id=peer, device_id_type=MESH).wait_recv()
        for cp, _ in copies:
            cp.wait_send()

        def c_of(dev):
            segments, pos = [], 0
            while pos < d:
                row, col = divmod(pos, n)
                take = min(d - pos, n - col)
                segments.append(all_ref[dev * rows_per + row:dev * rows_per + row + 1, col:col + take])
                pos += take
            return jnp.concatenate(segments, axis=1)

        c_all = jnp.concatenate([c_of(dev) for dev in range(N_DEV)], axis=0)
        load.wait()
        part_ref[...] = jnp.dot(_silu(c_all), w_vmem[...], precision=lax.Precision.HIGHEST, preferred_element_type=F32)
        parts_ref[chip] = part_ref[...]
        swaps = []
        for k, (peer, _) in enumerate(_ici_peers(x, y, c)):
            cp = pltpu.make_async_remote_copy(
                src_ref=part_ref, dst_ref=parts_ref.at[chip], send_sem=send2.at[k], recv_sem=recv2.at[k],
                device_id=peer, device_id_type=MESH)
            cp.start()
            swaps.append(cp)
        for i in range(first_chunks, n_chunks):
            cast_chunk(i)
        for i in range(n_chunks - 2, n_chunks):
            chunk_out(i).wait()
        for k, (peer, peer_chip) in enumerate(_ici_peers(x, y, c)):
            pltpu.make_async_remote_copy(
                src_ref=part_ref, dst_ref=parts_ref.at[peer_chip], send_sem=send2.at[k], recv_sem=recv2.at[k],
                device_id=peer, device_id_type=MESH).wait_recv()
        for cp in swaps:
            cp.wait_send()
        flat = jnp.concatenate([parts_ref[j, pl.ds(me, 1), :] for j in range(N_CHIPS)], axis=1) + b_ref[...]
        mod_ref[...] = jnp.concatenate([flat[:, i * d:(i + 1) * d] for i in range(3)], axis=0)

    return pl.pallas_call(
        body, name="ada_modulation",
        out_shape=(jax.ShapeDtypeStruct((N_DEV * rows_per, n), F32), jax.ShapeDtypeStruct((3, d), F32),
                   jax.ShapeDtypeStruct((N_CHIPS, big_rows, big_cols), BF16)),
        in_specs=[VMEM, ANY, VMEM, ANY], out_specs=(VMEM, VMEM, ANY),
        scratch_shapes=[pltpu.VMEM((d_model, wa), F32), pltpu.VMEM((N_DEV, wa), F32),
                        pltpu.VMEM((N_CHIPS, N_DEV, wa), F32),
                        pltpu.VMEM((2, ROW_TILE, big_cols), F32), pltpu.VMEM((2, ROW_TILE, big_cols), BF16),
                        pltpu.SemaphoreType.DMA,
                        pltpu.SemaphoreType.DMA((N_DEV - 1,)), pltpu.SemaphoreType.DMA((N_DEV - 1,)),
                        pltpu.SemaphoreType.DMA((N_CHIPS - 1,)), pltpu.SemaphoreType.DMA((N_CHIPS - 1,)),
                        pltpu.SemaphoreType.DMA((2,)), pltpu.SemaphoreType.DMA((2,))],
        compiler_params=_params(),
    )(packed, w_ada, b_ada, w_big)


def _prenorm(x, mod, g_pre, after):
    t, d = x.shape
    tb = ROW_TILE

    def body(x_ref, mod_ref, g_ref, after_ref, h_ref, ht_ref):
        del after_ref
        xv = x_ref[...]
        r = lax.rsqrt(jnp.mean(xv * xv, axis=-1, keepdims=True) + EPS)
        h = (xv * r) * g_ref[...] * (1.0 + mod_ref[1:2, :]) + mod_ref[0:1, :]
        h_ref[...] = h.astype(BF16)
        ht_ref[...] = h.T.astype(BF16)

    return pl.pallas_call(
        body, name="prenorm", grid=(t // tb,),
        out_shape=(jax.ShapeDtypeStruct((t, d), BF16), jax.ShapeDtypeStruct((d, t), BF16)),
        in_specs=[pl.BlockSpec((tb, d), lambda i: (i, 0)), pl.BlockSpec((3, d), lambda i: (0, 0)),
                  pl.BlockSpec((1, d), lambda i: (0, 0)), ANY],
        out_specs=(pl.BlockSpec((tb, d), lambda i: (i, 0)), pl.BlockSpec((d, tb), lambda i: (0, i))),
        compiler_params=_params(("parallel",)),
    )(x, mod, g_pre, after)


def _proj_tiles(proj, h, w, tiles, name):
    t, d = h.shape
    ws = w.shape[-1]
    tn = COL_TILE
    nt = ws // tn

    def body(tile_ref, *refs):
        del tile_ref
        a_ref, b_ref, o_ref = refs[-3:]
        o_ref[...] = jnp.dot(a_ref[...], b_ref[...].astype(BF16), preferred_element_type=F32).astype(BF16)

    if w.ndim == 3:
        w_spec = pl.BlockSpec((None, d, tn), lambda i, tl: (tl[i] // nt, 0, tl[i] % nt))
    else:
        w_spec = pl.BlockSpec((d, tn), lambda i, tl: (0, tl[i] % nt))
    first = proj is None
    grid_spec = pltpu.PrefetchScalarGridSpec(
        num_scalar_prefetch=1, grid=(tiles.shape[0],),
        in_specs=([] if first else [HBM]) + [pl.BlockSpec((t, d), lambda i, tl: (0, 0)), w_spec],
        out_specs=pl.BlockSpec((t, tn), lambda i, tl: (0, tl[i])))
    return pl.pallas_call(
        body, name=name, grid_spec=grid_spec,
        out_shape=jax.ShapeDtypeStruct((t, N_CHIPS * ws), BF16),
        input_output_aliases={} if first else {1: 0},
        compiler_params=_params(("parallel",)),
    )(*([tiles] if first else [tiles, proj]), h, w)


def _shift_rows(a, rows):
    idx = lax.broadcasted_iota(jnp.int32, a.shape, 0)
    prev = jnp.where(idx == 0, 0.0, pltpu.roll(a, 1, 0))
    nxt = jnp.where(idx == rows - 1, 0.0, pltpu.roll(a, rows - 1, 0))
    return prev, nxt


def _conv_fwd(conv_proj, conv_w, conv_b, dc):
    t = conv_proj.shape[0]
    ct = CONV_TILE
    nct = dc // ct

    def body(u_ref, cg_ref, w_ref, b_ref, co_ref):
        a = cg_ref[...].astype(F32) * u_ref[...].astype(F32)
        prev, nxt = _shift_rows(a, t)
        co_ref[...] = (w_ref[0:1, :] * prev + w_ref[1:2, :] * a + w_ref[2:3, :] * nxt + b_ref[...]).astype(BF16)

    return pl.pallas_call(
        body, name="conv_fwd", grid=(nct,),
        out_shape=jax.ShapeDtypeStruct((t, dc), BF16),
        in_specs=[pl.BlockSpec((t, ct), lambda i: (0, i)), pl.BlockSpec((t, ct), lambda i: (0, 2 * nct + i)),
                  pl.BlockSpec((3, ct), lambda i: (0, i)), pl.BlockSpec((1, ct), lambda i: (0, i))],
        out_specs=pl.BlockSpec((t, ct), lambda i: (0, i)),
        compiler_params=_params(("parallel",)),
    )(conv_proj, conv_proj, conv_w, conv_b)


def _to_residue_major(src_ref, dst_ref, r):
    seq = src_ref.shape[0] // r
    for res in range(r):
        dst_ref[res * seq:(res + 1) * seq, :] = src_ref[pl.ds(res, seq, stride=r), :].astype(dst_ref.dtype)


def _branch_operands(token_refs, stage, dil, r):
    if r == 1:
        return list(token_refs)
    for i, ref in enumerate(token_refs):
        stage[...] = ref[...].astype(F32)
        _to_residue_major(stage, dil.at[i], r)
    return [dil.at[i] for i in range(len(token_refs))]


def _scaled_queries(q):
    return (q.astype(F32) * (HEAD_DIM ** -0.5)).astype(BF16)


BLOCK_SHIFTS = (0, -SIDE, None)


def _band_bias(rel, slope):
    arel = jnp.abs(rel)
    return jnp.where(arel <= SIDE, arel.astype(F32) * slope, NEG_INF)


def _fill_bias_tiles(bias_ref, sl_ref, r, kw):
    base = lax.broadcasted_iota(jnp.int32, (ATT_BQ, kw), 1) - lax.broadcasted_iota(jnp.int32, (ATT_BQ, kw), 0)
    for hh in range(2):
        slope = -(sl_ref[hh:hh + 1, 0:kw] * float(r))
        for e, shift in enumerate(BLOCK_SHIFTS):
            shift = ATT_BQ - kw if shift is None else shift
            bias_ref[hh, e, :, 0:kw] = _band_bias(base + shift, slope)


def _fill_stacked_bias_tiles(bias_ref, sl_ref, r, kw):
    base = lax.broadcasted_iota(jnp.int32, (kw, ATT_BQ), 0) - lax.broadcasted_iota(jnp.int32, (kw, ATT_BQ), 1)
    for hh in range(2):
        slope = -(sl_ref[hh:hh + 1, 0:ATT_BQ] * float(r))
        for e, shift in enumerate(BLOCK_SHIFTS):
            shift = ATT_BQ - kw if shift is None else shift
            bias_ref[e, 0:kw, hh * ATT_BQ:(hh + 1) * ATT_BQ] = _band_bias(base + shift, slope)


def _first_head_lanes():
    return lax.broadcasted_iota(jnp.int32, (1, PAIR), 1) < HEAD_DIM


def _only_head(x, first, hh):
    return jnp.where(first if hh == 0 else jnp.logical_not(first), x, jnp.zeros_like(x))


def _block_place(g, seq_len, kw):
    nqb = seq_len // ATT_BQ
    if nqb == 1:
        row = pl.multiple_of(g * ATT_BQ, ATT_BQ)
        return row, row, 0
    res = g // nqb
    qb = g - res * nqb
    q0 = qb * ATT_BQ
    ks = jnp.clip(q0 - SIDE, 0, seq_len - kw)
    edge = jnp.where(qb == 0, 0, jnp.where(qb == nqb - 1, 2, 1))
    return (pl.multiple_of(res * seq_len + q0, ATT_BQ), pl.multiple_of(res * seq_len + ks, SIDE), edge)


def _qkv_specs(dc, da, t, index):
    return [pl.BlockSpec((t, PAIR), functools.partial(index, (4 * dc + comp * da) // PAIR)) for comp in range(3)]


def _attn_fwd(proj, slopes, dc, da):
    t = proj.shape[0]
    hp = da // PAIR
    n_blocks = t // ATT_BQ

    def body(q_ref, k_ref, v_ref, sl_ref, o_ref, lse_ref, stage, dil, bias, o_res, l_res, o_tok, l_tok):
        for b, (_, r) in enumerate(BRANCHES):
            seq_len = t // r
            kw = min(ATT_KW, seq_len)
            ops = _branch_operands([q_ref, k_ref, v_ref], stage, dil, r)
            _fill_bias_tiles(bias, sl_ref, r, kw)
            o_dst, l_dst = (o_tok.at[b], l_tok.at[b]) if r == 1 else (o_res, l_res)
            first = _first_head_lanes()

            def blocks(trip, carry, seq_len=seq_len, kw=kw, o_dst=o_dst, l_dst=l_dst, first=first, ops=ops):
                nt = (((1,), (1,)), ((), ()))
                places = [_block_place(trip * ATT_UNROLL + i, seq_len, kw) for i in range(ATT_UNROLL)]
                chains = [(i, hh) for i in range(ATT_UNROLL) for hh in range(2)]
                qs = [_scaled_queries(ops[0][pl.ds(qrow, ATT_BQ), :]) for qrow, _, _ in places]
                ks = [ops[1][pl.ds(krow, kw), :] for _, krow, _ in places]
                vs = [ops[2][pl.ds(krow, kw), :] for _, krow, _ in places]
                ss = [lax.dot_general(_only_head(qs[i], first, hh), ks[i], nt, preferred_element_type=F32)
                      + bias[hh, places[i][2], :, 0:kw] for i, hh in chains]
                tops = [jnp.max(s, axis=-1, keepdims=True) for s in ss]
                ps = [jnp.exp(s - m) for s, m in zip(ss, tops)]
                dens = [jnp.sum(p, axis=-1, keepdims=True) for p in ps]
                for i, (qrow, _, _) in enumerate(places):
                    weights = jnp.concatenate([ps[2 * i].astype(BF16), ps[2 * i + 1].astype(BF16)], axis=1)
                    values = jnp.concatenate([_only_head(vs[i], first, 0), _only_head(vs[i], first, 1)], axis=0)
                    den = jnp.where(first, dens[2 * i], dens[2 * i + 1])
                    o_dst[pl.ds(qrow, ATT_BQ), :] = jnp.dot(weights, values, preferred_element_type=F32) / den
                    l_dst[pl.ds(qrow, ATT_BQ), :] = jnp.where(first, tops[2 * i], tops[2 * i + 1]) + jnp.log(den)
                return carry

            lax.fori_loop(0, n_blocks // ATT_UNROLL, blocks, 0)
            if r > 1:
                for res in range(r):
                    rows = slice(res * seq_len, (res + 1) * seq_len)
                    o_tok[b, pl.ds(res, seq_len, stride=r), :] = o_res[rows, :]
                    l_tok[b, pl.ds(res, seq_len, stride=r), :] = l_res[rows, :]

        def merge(i, carry):
            rows = pl.ds(pl.multiple_of(i * ROW_TILE, ROW_TILE), ROW_TILE)
            la, lb, lc = l_tok[0, rows, :], l_tok[1, rows, :], l_tok[2, rows, :]
            m = jnp.maximum(jnp.maximum(la, lb), lc)
            wa, wb, wc = jnp.exp(la - m), jnp.exp(lb - m), jnp.exp(lc - m)
            den = wa + wb + wc
            o_ref[rows, :] = (wa * o_tok[0, rows, :] + wb * o_tok[1, rows, :] + wc * o_tok[2, rows, :]) * (1.0 / den)
            lse_ref[rows, :] = m + jnp.log(den)
            return carry

        lax.fori_loop(0, t // ROW_TILE, merge, 0)

    pair_spec = pl.BlockSpec((None, t, PAIR), lambda h: (h, 0, 0))
    return pl.pallas_call(
        body, name="attn_fwd", grid=(hp,),
        out_shape=(jax.ShapeDtypeStruct((hp, t, PAIR), F32), jax.ShapeDtypeStruct((hp, t, PAIR), F32)),
        in_specs=_qkv_specs(dc, da, t, lambda first, h: (0, first + h))
        + [pl.BlockSpec((None, 8, ATT_KW), lambda h: (h, 0, 0))],
        out_specs=(pair_spec, pair_spec),
        scratch_shapes=[pltpu.VMEM((t, PAIR), F32), pltpu.VMEM((3, t, PAIR), BF16),
                        pltpu.VMEM((2, 3, ATT_BQ, ATT_KW), F32),
                        pltpu.VMEM((t, PAIR), F32), pltpu.VMEM((t, PAIR), F32),
                        pltpu.VMEM((3, t, PAIR), F32), pltpu.VMEM((3, t, PAIR), F32)],
        compiler_params=_params(("parallel",)),
    )(proj, proj, proj, slopes)


def _attn_bwd(dproj, proj, d_o, lse, delta, slopes, dc, da, after):
    t = proj.shape[0]
    hp = da // PAIR
    n_blocks = t // ATT_BQ

    def all_branches(q_ref, k_ref, v_ref, do_ref, lse_ref, dl_ref, sl_ref,
                     stage, dil, packed, packed_res, row_vecs, bias_t, acc, tot):
        first = _first_head_lanes()
        lane = lax.broadcasted_iota(jnp.int32, (1, PAIR), 1)
        packed[...] = jnp.where((lane & (HEAD_DIM - 1)) < HEAD_DIM // 2, lse_ref[...], dl_ref[...])
        for b, (_, r) in enumerate(BRANCHES):
            seq_len = t // r
            kw = min(ATT_KW, seq_len)
            ops = _branch_operands([q_ref, k_ref, v_ref, do_ref], stage, dil, r)
            scalars = packed
            if r > 1:
                _to_residue_major(packed, packed_res, r)
                scalars = packed_res
            for g in range(n_blocks):
                flipped = scalars[g * ATT_BQ:(g + 1) * ATT_BQ, :].T
                for row in range(4):
                    row_vecs[g, row:row + 1, :] = flipped[row * (HEAD_DIM // 2):row * (HEAD_DIM // 2) + 1, :]
            _fill_stacked_bias_tiles(bias_t, sl_ref, r, kw)
            acc[1] = jnp.zeros((t, PAIR), F32)
            acc[2] = jnp.zeros((t, PAIR), F32)

            def blocks(trip, carry, seq_len=seq_len, kw=kw, ops=ops):
                nt = (((1,), (1,)), ((), ()))
                group = range(ATT_UNROLL)
                places = [_block_place(trip * ATT_UNROLL + i, seq_len, kw) for i in group]
                ks, vs, q2s, do2s, lse2s, dl2s = [], [], [], [], [], []
                for i, (qrow, krow, _) in zip(group, places):
                    q = _scaled_queries(ops[0][pl.ds(qrow, ATT_BQ), :])
                    dov = ops[3][pl.ds(qrow, ATT_BQ), :]
                    ks.append(ops[1][pl.ds(krow, kw), :])
                    vs.append(ops[2][pl.ds(krow, kw), :])
                    q2s.append(jnp.concatenate([_only_head(q, first, 0), _only_head(q, first, 1)], axis=0))
                    do2s.append(jnp.concatenate([_only_head(dov, first, 0), _only_head(dov, first, 1)], axis=0))
                    rows = row_vecs[trip * ATT_UNROLL + i]
                    lse2s.append(jnp.concatenate([rows[0:1, :], rows[2:3, :]], axis=1))
                    dl2s.append(jnp.concatenate([rows[1:2, :], rows[3:4, :]], axis=1))
                s_ts = [lax.dot_general(ks[i], q2s[i], nt, preferred_element_type=F32) for i in group]
                dp_ts = [lax.dot_general(vs[i], do2s[i], nt, preferred_element_type=F32) for i in group]
                p_ts = [jnp.exp(s_ts[i] + bias_t[places[i][2], 0:kw, :] - lse2s[i]) for i in group]
                ds_ts = [p_ts[i] * (dp_ts[i] - dl2s[i]) for i in group]
                dvs = [jnp.dot(p_ts[i].astype(BF16), do2s[i], preferred_element_type=F32) for i in group]
                dks = [jnp.dot(ds_ts[i].astype(BF16), q2s[i], preferred_element_type=F32) for i in group]
                dss = [ds_ts[i].T.astype(BF16) for i in group]
                dqs = [jnp.dot(dss[i][0:ATT_BQ, :], _only_head(ks[i], first, 0), preferred_element_type=F32)
                       + jnp.dot(dss[i][ATT_BQ:2 * ATT_BQ, :], _only_head(ks[i], first, 1), preferred_element_type=F32)
                       for i in group]
                for i, (qrow, krow, _) in zip(group, places):
                    acc[0, pl.ds(qrow, ATT_BQ), :] = dqs[i] * (HEAD_DIM ** -0.5)
                    acc[1, pl.ds(krow, kw), :] += dks[i]
                    acc[2, pl.ds(krow, kw), :] += dvs[i]
                return carry

            lax.fori_loop(0, n_blocks // ATT_UNROLL, blocks, 0)
            for comp in range(3):
                if r == 1:
                    tot[comp] = acc[comp]
                else:
                    for res in range(r):
                        tok = pl.ds(res, seq_len, stride=r)
                        tot[comp, tok, :] = tot[comp, tok, :] + acc[comp, res * seq_len:(res + 1) * seq_len, :]

    first_q = (4 * dc) // PAIR

    def body(dproj_in, q_ref, k_ref, v_ref, do_ref, lse_ref, dl_ref, sl_ref, after_ref, out_ref, *scratch):
        del dproj_in, after_ref
        work, out_stage, out_sems = scratch[:-2], scratch[-2], scratch[-1]
        h = pl.program_id(0)
        all_branches(q_ref, k_ref, v_ref, do_ref, lse_ref, dl_ref, sl_ref, *work)

        def out_copy(comp):
            cols = pl.ds(pl.multiple_of((first_q + comp * hp + h) * PAIR, PAIR), PAIR)
            return pltpu.make_async_copy(out_stage.at[comp], out_ref.at[:, cols], out_sems.at[comp])

        @pl.when(h > 0)
        def _():
            for comp in range(3):
                out_copy(comp).wait()

        for comp in range(3):
            out_stage[comp] = work[-1][comp].astype(BF16)
            out_copy(comp).start()

        @pl.when(h == hp - 1)
        def _():
            for comp in range(3):
                out_copy(comp).wait()

    pair_spec = pl.BlockSpec((None, t, PAIR), lambda h: (h, 0, 0))
    return pl.pallas_call(
        body, name="attn_bwd", grid=(hp,),
        out_shape=jax.ShapeDtypeStruct(dproj.shape, BF16),
        in_specs=[HBM] + _qkv_specs(dc, da, t, lambda first, h: (0, first + h))
        + [pair_spec, pair_spec, pair_spec, pl.BlockSpec((None, 8, ATT_KW), lambda h: (h, 0, 0)), ANY],
        out_specs=ANY,
        input_output_aliases={0: 0},
        scratch_shapes=[pltpu.VMEM((t, PAIR), F32), pltpu.VMEM((4, t, PAIR), BF16),
                        pltpu.VMEM((t, PAIR), F32), pltpu.VMEM((t, PAIR), F32),
                        pltpu.VMEM((n_blocks, 8, ATT_BQ), F32), pltpu.VMEM((3, ATT_KW, 2 * ATT_BQ), F32),
                        pltpu.VMEM((3, t, PAIR), F32), pltpu.VMEM((3, t, PAIR), F32),
                        pltpu.VMEM((3, t, PAIR), BF16), pltpu.SemaphoreType.DMA((3,))],
        compiler_params=_params(("arbitrary",)),
    )(dproj, proj, proj, proj, d_o, lse, delta, slopes, after)


def _mix_fwd(co, proj, o_mix, g_conv, g_attn_pairs, after):
    t, dc = co.shape
    hp = o_mix.shape[0]
    da = hp * PAIR
    tb = ROW_TILE

    def body(co_ref, bg_ref, zc_ref, za_ref, om_ref, gc_ref, ga_ref, after_ref, ycat_ref, ycatt_ref):
        del after_ref
        p = bg_ref[...].astype(F32) * co_ref[...].astype(F32)
        rc = lax.rsqrt(jnp.mean(p * p, axis=-1, keepdims=True) + EPS)
        yc = (p * rc) * gc_ref[...] * _silu(zc_ref[...].astype(F32))
        ycat_ref[:, 0:dc] = yc.astype(BF16)
        ycatt_ref[0:dc, :] = yc.T.astype(BF16)
        ssq = jnp.zeros((tb, 1), F32)
        for h in range(hp):
            o = om_ref[h]
            ssq = ssq + jnp.sum(o * o, axis=-1, keepdims=True)
        ra = lax.rsqrt(ssq * (1.0 / da) + EPS)
        for h in range(hp):
            ya = (om_ref[h] * ra) * ga_ref[h] * _silu(za_ref[:, h * PAIR:(h + 1) * PAIR].astype(F32))
            ycat_ref[:, dc + h * PAIR:dc + (h + 1) * PAIR] = ya.astype(BF16)
            ycatt_ref[dc + h * PAIR:dc + (h + 1) * PAIR, :] = ya.T.astype(BF16)

    pair_spec = pl.BlockSpec((hp, tb, PAIR), lambda i: (0, i, 0))
    return pl.pallas_call(
        body, name="mix_fwd", grid=(t // tb,),
        out_shape=(jax.ShapeDtypeStruct((t, dc + da), BF16), jax.ShapeDtypeStruct((dc + da, t), BF16)),
        in_specs=[pl.BlockSpec((tb, dc), lambda i: (i, 0)),
                  pl.BlockSpec((tb, dc), lambda i: (i, 1)),
                  pl.BlockSpec((tb, dc), lambda i: (i, 3)),
                  pl.BlockSpec((tb, da), lambda i: (i, 7)),
                  pair_spec,
                  pl.BlockSpec((1, dc), lambda i: (0, 0)),
                  pl.BlockSpec((hp, 1, PAIR), lambda i: (0, 0, 0)), ANY],
        out_specs=(pl.BlockSpec((tb, dc + da), lambda i: (i, 0)), pl.BlockSpec((dc + da, tb), lambda i: (0, i))),
        compiler_params=_params(("parallel",)),
    )(co, proj, proj, proj, o_mix, g_conv, g_attn_pairs, after)


def _out_fwd_bwd(ycat, woutf, x, target, mod, g_post):
    t, d = x.shape
    n = ycat.shape[1]
    tb = ROW_TILE

    def body(a_ref, w_ref, x_ref, tg_ref, mod_ref, g_ref, dout_ref, dy_ref, acc_ref):
        y = jnp.dot(a_ref[...], w_ref[...], preferred_element_type=F32)
        r = lax.rsqrt(jnp.mean(y * y, axis=-1, keepdims=True) + EPS)
        nh = y * r
        gate = mod_ref[2:3, :]
        nrm = nh * g_ref[...]
        err = x_ref[...] + gate * nrm - tg_ref[...]
        dout = err * (1.0 / d)
        dout_ref[...] = dout.astype(BF16)
        dn = dout * gate
        a = dn * g_ref[...]
        dy = r * (a - nh * jnp.mean(a * nh, axis=-1, keepdims=True))
        dy_ref[...] = dy.astype(BF16)
        loss = 0.5 * jnp.sum(jnp.sum(err * err, axis=-1, keepdims=True) * (1.0 / d), axis=0, keepdims=True)
        part = jnp.concatenate(
            [jnp.sum(dout * nrm, axis=0, keepdims=True), jnp.sum(dn * nh, axis=0, keepdims=True),
             jnp.broadcast_to(loss, (1, d)), jnp.zeros((5, d), F32)], axis=0)

        @pl.when(pl.program_id(0) == 0)
        def _():
            acc_ref[...] = jnp.zeros(acc_ref.shape, F32)

        acc_ref[...] += part

    return pl.pallas_call(
        body, name="out_fwd_bwd", grid=(t // tb,),
        out_shape=(jax.ShapeDtypeStruct((t, d), BF16), jax.ShapeDtypeStruct((t, d), BF16),
                   jax.ShapeDtypeStruct((8, d), F32)),
        in_specs=[pl.BlockSpec((tb, n), lambda i: (i, 0)), pl.BlockSpec((n, d), lambda i: (0, 0)),
                  pl.BlockSpec((tb, d), lambda i: (i, 0)), pl.BlockSpec((tb, d), lambda i: (i, 0)),
                  pl.BlockSpec((3, d), lambda i: (0, 0)), pl.BlockSpec((1, d), lambda i: (0, 0))],
        out_specs=(pl.BlockSpec((tb, d), lambda i: (i, 0)), pl.BlockSpec((tb, d), lambda i: (i, 0)),
                   pl.BlockSpec((8, d), lambda i: (0, 0))),
        compiler_params=_params(("arbitrary",)),
    )(ycat, woutf, x, target, mod, g_post)


def _matmul_nt(a, b, out_dtype, name):
    m, k = a.shape
    n = b.shape[0]
    tn = COL_TILE

    def body(a_ref, b_ref, o_ref):
        o_ref[...] = lax.dot_general(a_ref[...], b_ref[...], (((1,), (1,)), ((), ())),
                                     preferred_element_type=F32).astype(out_dtype)

    return pl.pallas_call(
        body, name=name, grid=(n // tn,),
        out_shape=jax.ShapeDtypeStruct((m, n), out_dtype),
        in_specs=[pl.BlockSpec((m, k), lambda i: (0, 0)), pl.BlockSpec((tn, k), lambda i: (i, 0))],
        out_specs=pl.BlockSpec((m, tn), lambda i: (0, i)),
        compiler_params=_params(("parallel",)),
    )(a, b)


def _mix_bwd(dycat, co, proj, o_mix, g_conv, g_attn_pairs):
    t, dc = co.shape
    hp = o_mix.shape[0]
    da = hp * PAIR
    tb = ROW_TILE

    def body(dy_ref, co_ref, bg_ref, zc_ref, za_ref, om_ref, gc_ref, ga_ref,
             dcp_ref, dco_ref, do_ref, dl_ref, dgc_ref, dga_ref):
        first = pl.program_id(0) == 0
        cov = co_ref[...].astype(F32)
        bg = bg_ref[...].astype(F32)
        zc = zc_ref[...].astype(F32)
        p = bg * cov
        rc = lax.rsqrt(jnp.mean(p * p, axis=-1, keepdims=True) + EPS)
        nh = p * rc
        dyc = dy_ref[:, 0:dc].astype(F32)
        dn = dyc * _silu(zc)
        a = dn * gc_ref[...]
        dp = rc * (a - nh * jnp.mean(a * nh, axis=-1, keepdims=True))
        dcp_ref[:, 0:dc] = jnp.zeros((tb, dc), BF16)
        dcp_ref[:, dc:2 * dc] = (dp * cov).astype(BF16)
        dcp_ref[:, 2 * dc:3 * dc] = jnp.zeros((tb, dc), BF16)
        dcp_ref[:, 3 * dc:4 * dc] = (dyc * nh * gc_ref[...] * _silu_grad(zc)).astype(BF16)
        dcp_ref[:, 4 * dc:4 * dc + 3 * da] = jnp.zeros((tb, 3 * da), BF16)
        dco_ref[...] = dp * bg

        @pl.when(first)
        def _():
            dgc_ref[...] = jnp.zeros(dgc_ref.shape, F32)
            dga_ref[...] = jnp.zeros(dga_ref.shape, F32)

        dgc_ref[...] += jnp.sum(dn * nh, axis=0, keepdims=True)

        ssq = jnp.zeros((tb, 1), F32)
        for h in range(hp):
            o = om_ref[h]
            ssq = ssq + jnp.sum(o * o, axis=-1, keepdims=True)
        ra = lax.rsqrt(ssq * (1.0 / da) + EPS)
        dot_an = jnp.zeros((tb, 1), F32)
        for h in range(hp):
            nha = om_ref[h] * ra
            za = za_ref[:, h * PAIR:(h + 1) * PAIR].astype(F32)
            dya = dy_ref[:, dc + h * PAIR:dc + (h + 1) * PAIR].astype(F32)
            dna = dya * _silu(za)
            dza = (dya * nha * ga_ref[h] * _silu_grad(za)).astype(BF16)
            dcp_ref[:, 4 * dc + 3 * da + h * PAIR:4 * dc + 3 * da + (h + 1) * PAIR] = dza
            dga_ref[h] += jnp.sum(dna * nha, axis=0, keepdims=True)
            dot_an = dot_an + jnp.sum(dna * ga_ref[h] * nha, axis=-1, keepdims=True)
        mean_an = dot_an * (1.0 / da)
        first_head = lax.broadcasted_iota(jnp.int32, (tb, PAIR), 1) < HEAD_DIM
        for h in range(hp):
            o = om_ref[h]
            nha = o * ra
            za = za_ref[:, h * PAIR:(h + 1) * PAIR].astype(F32)
            dya = dy_ref[:, dc + h * PAIR:dc + (h + 1) * PAIR].astype(F32)
            aa = dya * _silu(za) * ga_ref[h]
            d_o = ra * (aa - nha * mean_an)
            do_ref[h] = d_o.astype(BF16)
            prod = d_o * o
            both = jnp.sum(prod, axis=-1, keepdims=True)
            head0 = jnp.sum(jnp.where(first_head, prod, 0.0), axis=-1, keepdims=True)
            dl_ref[h] = jnp.where(first_head, head0, both - head0)

    pair_spec = pl.BlockSpec((hp, tb, PAIR), lambda i: (0, i, 0))
    return pl.pallas_call(
        body, name="mix_bwd", grid=(t // tb,),
        out_shape=(jax.ShapeDtypeStruct((t, 4 * dc + 4 * da), BF16), jax.ShapeDtypeStruct((t, dc), F32),
                   jax.ShapeDtypeStruct((hp, t, PAIR), BF16), jax.ShapeDtypeStruct((hp, t, PAIR), F32),
                   jax.ShapeDtypeStruct((1, dc), F32), jax.ShapeDtypeStruct((hp, 1, PAIR), F32)),
        in_specs=[pl.BlockSpec((tb, dc + da), lambda i: (i, 0)),
                  pl.BlockSpec((tb, dc), lambda i: (i, 0)),
                  pl.BlockSpec((tb, dc), lambda i: (i, 1)),
                  pl.BlockSpec((tb, dc), lambda i: (i, 3)),
                  pl.BlockSpec((tb, da), lambda i: (i, 7)),
                  pair_spec,
                  pl.BlockSpec((1, dc), lambda i: (0, 0)),
                  pl.BlockSpec((hp, 1, PAIR), lambda i: (0, 0, 0))],
        out_specs=(pl.BlockSpec((tb, 4 * dc + 4 * da), lambda i: (i, 0)), pl.BlockSpec((tb, dc), lambda i: (i, 0)),
                   pair_spec, pair_spec,
                   pl.BlockSpec((1, dc), lambda i: (0, 0)), pl.BlockSpec((hp, 1, PAIR), lambda i: (0, 0, 0))),
        compiler_params=_params(("arbitrary",)),
    )(dycat, co, proj, proj, proj, o_mix, g_conv, g_attn_pairs)


def _conv_bwd(dconv_proj, dco, conv_proj, conv_w, dc, after):
    t = dco.shape[0]
    ct = CONV_TILE
    nct = dc // ct

    def body(dcp_in_ref, dco_ref, u_ref, cg_ref, w_ref, after_ref, dcp_ref, acc_ref):
        del dcp_in_ref, after_ref
        which = pl.program_id(1)
        g = dco_ref[...]
        u = u_ref[...].astype(F32)
        cg = cg_ref[...].astype(F32)
        g_prev, g_next = _shift_rows(g, t)
        da = w_ref[0:1, :] * g_next + w_ref[1:2, :] * g + w_ref[2:3, :] * g_prev
        dcp_ref[...] = (da * jnp.where(which == 0, cg, u)).astype(BF16)
        a = cg * u
        a_prev, a_next = _shift_rows(a, t)
        acc_ref[...] = jnp.concatenate(
            [jnp.sum(g * a_prev, axis=0, keepdims=True), jnp.sum(g * a, axis=0, keepdims=True),
             jnp.sum(g * a_next, axis=0, keepdims=True), jnp.sum(g, axis=0, keepdims=True),
             jnp.zeros((4, ct), F32)], axis=0)

    return pl.pallas_call(
        body, name="conv_bwd", grid=(nct, 2),
        out_shape=(jax.ShapeDtypeStruct(dconv_proj.shape, BF16), jax.ShapeDtypeStruct((8, dc), F32)),
        in_specs=[HBM,
                  pl.BlockSpec((t, ct), lambda i, s: (0, i)),
                  pl.BlockSpec((t, ct), lambda i, s: (0, i)),
                  pl.BlockSpec((t, ct), lambda i, s: (0, 2 * nct + i)),
                  pl.BlockSpec((3, ct), lambda i, s: (0, i)), ANY],
        out_specs=(pl.BlockSpec((t, ct), lambda i, s: (0, 2 * s * nct + i)),
                   pl.BlockSpec((8, ct), lambda i, s: (0, i))),
        input_output_aliases={0: 0},
        compiler_params=_params(("arbitrary", "arbitrary")),
    )(dconv_proj, dco, conv_proj, conv_proj, conv_w, after)


def _dh(dproj, winf, after):
    t = dproj.shape[0]
    _, d, ws = winf.shape
    tm = tn = COL_TILE
    nt = (((1,), (1,)), ((), ()))

    def body(a_ref, w_ref, after_ref, o_ref):
        del after_ref
        acc = lax.dot_general(a_ref[:, 0:ws], w_ref[0], nt, preferred_element_type=F32)
        for j in range(1, N_CHIPS):
            acc = acc + lax.dot_general(a_ref[:, j * ws:(j + 1) * ws], w_ref[j], nt, preferred_element_type=F32)
        o_ref[...] = acc.astype(BF16)

    return pl.pallas_call(
        body, name="dh", grid=(d // tn, t // tm),
        out_shape=jax.ShapeDtypeStruct((t, d), BF16),
        in_specs=[pl.BlockSpec((tm, N_CHIPS * ws), lambda n, m: (m, 0)),
                  pl.BlockSpec((N_CHIPS, tn, ws), lambda n, m: (0, n, 0)), ANY],
        out_specs=pl.BlockSpec((tm, tn), lambda n, m: (m, n)),
        compiler_params=_params(("parallel", "parallel")),
    )(dproj, winf, after)


def _prenorm_bwd(x, dh, dout, mod, g_pre):
    t, d = x.shape
    tb = ROW_TILE

    def body(x_ref, dh_ref, dout_ref, mod_ref, g_ref, gx_ref, acc_ref):
        xv = x_ref[...]
        dhv = dh_ref[...].astype(F32)
        r = lax.rsqrt(jnp.mean(xv * xv, axis=-1, keepdims=True) + EPS)
        xh = xv * r
        one_scale = 1.0 + mod_ref[1:2, :]
        a = dhv * one_scale * g_ref[...]
        gx_ref[...] = dout_ref[...].astype(F32) + r * (a - xh * jnp.mean(a * xh, axis=-1, keepdims=True))
        part = jnp.concatenate(
            [jnp.sum(dhv, axis=0, keepdims=True), jnp.sum(dhv * xh * g_ref[...], axis=0, keepdims=True),
             jnp.sum(dhv * xh * one_scale, axis=0, keepdims=True), jnp.zeros((5, d), F32)], axis=0)

        @pl.when(pl.program_id(0) == 0)
        def _():
            acc_ref[...] = jnp.zeros(acc_ref.shape, F32)

        acc_ref[...] += part

    return pl.pallas_call(
        body, name="prenorm_bwd", grid=(t // tb,),
        out_shape=(jax.ShapeDtypeStruct((t, d), F32), jax.ShapeDtypeStruct((8, d), F32)),
        in_specs=[pl.BlockSpec((tb, d), lambda i: (i, 0)), pl.BlockSpec((tb, d), lambda i: (i, 0)),
                  pl.BlockSpec((tb, d), lambda i: (i, 0)), pl.BlockSpec((3, d), lambda i: (0, 0)),
                  pl.BlockSpec((1, d), lambda i: (0, 0))],
        out_specs=(pl.BlockSpec((tb, d), lambda i: (i, 0)), pl.BlockSpec((8, d), lambda i: (0, 0))),
        compiler_params=_params(("arbitrary",)),
    )(x, dh, dout, mod, g_pre)


def _chip_sums(mine, rsib, name, part=0, parts=1, after=()):
    _, half, cols = mine.shape
    rows = half // parts
    tr = min(rows, ROW_TILE)
    nt = rows // tr

    def body(g_ref, r_ref, *rest):
        rest[-1][...] = (g_ref[...].astype(F32) + r_ref[...].astype(F32)).astype(BF16)

    spec = pl.BlockSpec((None, tr, cols), lambda j, i: (j, part * nt + i, 0))
    return pl.pallas_call(
        body, name=name, grid=(N_CHIPS, nt),
        out_shape=jax.ShapeDtypeStruct((N_CHIPS, rows, cols), BF16),
        in_specs=[spec, spec] + [ANY] * len(after), out_specs=pl.BlockSpec((None, tr, cols), lambda j, i: (j, i, 0)),
        compiler_params=_params(("parallel", "parallel")),
    )(mine, rsib, *after)


def _owner_sum(place, mine, rsib, rici, name, part=0, parts=1):
    _, half, cols = mine.shape
    rows = half // parts
    tr = min(rows, ROW_TILE)
    nt = rows // tr

    def body(place_ref, g_ref, r_ref, i_ref, o_ref):
        del place_ref
        acc = g_ref[...].astype(F32) + r_ref[...].astype(F32)
        for k in range(N_CHIPS - 1):
            acc = acc + i_ref[k].astype(F32)
        o_ref[...] = acc

    own = pl.BlockSpec((None, tr, cols), lambda i, p: (p[0], part * nt + i, 0))
    grid_spec = pltpu.PrefetchScalarGridSpec(
        num_scalar_prefetch=1, grid=(nt,),
        in_specs=[own, own, pl.BlockSpec((N_CHIPS - 1, tr, cols), lambda i, p: (0, i, 0))],
        out_specs=pl.BlockSpec((tr, cols), lambda i, p: (p[1] * (half // tr) + part * nt + i, 0)))
    return pl.pallas_call(
        body, name=name, grid_spec=grid_spec,
        out_shape=jax.ShapeDtypeStruct((2 * half, cols), F32),
        compiler_params=_params(("parallel",)),
    )(place, mine, rsib, rici)


def _adam_math(w, g, m, v):
    m2 = ADAM_B1 * m + (1.0 - ADAM_B1) * g
    v2 = ADAM_B2 * v + (1.0 - ADAM_B2) * (g * g)
    m_hat = m2 / (1.0 - ADAM_B1 ** ADAM_STEP)
    v_hat = v2 / (1.0 - ADAM_B2 ** ADAM_STEP)
    delta = -ADAM_LR * (m_hat / (jnp.sqrt(v_hat) + ADAM_EPS) + ADAM_WD * w)
    return delta, m2, v2


def _adamw(w, g, m, v, name, part=0, parts=1, prev=None):
    rows, cols = w.shape
    tr = min(rows, ROW_TILE)

    def body(*refs):
        w_ref, g_ref, m_ref, v_ref, go_ref, d_ref, m2_ref, v2_ref = refs[-8:]
        g = g_ref[...]
        go_ref[...] = g
        d_ref[...], m2_ref[...], v2_ref[...] = _adam_math(w_ref[...], g, m_ref[...], v_ref[...])

    if parts == 1:
        grid, spec = (rows // tr,), pl.BlockSpec((tr, cols), lambda i: (i, 0))
    else:
        per_half = rows // 2 // tr
        nt = per_half // parts
        grid, spec = (2, nt), pl.BlockSpec((tr, cols), lambda r, i: (r * per_half + part * nt + i, 0))
    olds = [] if prev is None else list(prev)
    return pl.pallas_call(
        body, name=name, grid=grid,
        out_shape=(jax.ShapeDtypeStruct(w.shape, F32),) * 4,
        in_specs=[HBM] * len(olds) + [spec] * 4, out_specs=(spec,) * 4,
        input_output_aliases={i: i for i in range(len(olds))},
        compiler_params=_params(("parallel",) * len(grid)),
    )(*olds, w, g, m, v)


def _ada_grad_adamw(c_all_t, dmod_cols, w, m, v):
    d, wa = w.shape
    tr = ROW_TILE

    def body(ct_ref, dm_ref, w_ref, m_ref, v_ref, g_ref, d_ref, m2_ref, v2_ref):
        act = _silu(ct_ref[...])
        g = act[:, 0:1] * dm_ref[0:1, :]
        for b in range(1, N_DEV):
            g = g + act[:, b:b + 1] * dm_ref[b:b + 1, :]
        g_ref[...] = g
        d_ref[...], m2_ref[...], v2_ref[...] = _adam_math(w_ref[...], g, m_ref[...], v_ref[...])

    spec = pl.BlockSpec((tr, wa), lambda i: (i, 0))
    return pl.pallas_call(
        body, name="ada_grad_adamw", grid=(d // tr,),
        out_shape=(jax.ShapeDtypeStruct(w.shape, F32),) * 4,
        in_specs=[pl.BlockSpec((tr, N_DEV), lambda i: (i, 0)), pl.BlockSpec((N_DEV, wa), lambda i: (0, 0)),
                  spec, spec, spec],
        out_specs=(spec,) * 4,
        compiler_params=_params(("parallel",)),
    )(c_all_t, dmod_cols, w, m, v)


def _small_update(place, gathered, pieces, weights, moments_m, moments_v):
    n = gathered.shape[1]
    k = len(weights)

    def body(place_ref, g_ref, *refs):
        w_refs, m_refs, v_refs = refs[0:k], refs[k:2 * k], refs[2 * k:3 * k]
        outs = refs[3 * k:]
        total = g_ref[0:SUBLANES, :]
        for dev in range(1, N_DEV):
            total = total + g_ref[SUBLANES * dev:SUBLANES * (dev + 1), :]

        def flat(offset, length):
            segments, pos = [], offset
            while pos < offset + length:
                row, col = divmod(pos, n)
                take = min(offset + length - pos, n - col)
                segments.append(total[row:row + 1, col:col + take])
                pos += take
            return jnp.concatenate(segments, axis=1) if len(segments) > 1 else segments[0]

        chip = place_ref[0]
        for i, (w_ref, m_ref, v_ref) in enumerate(zip(w_refs, m_refs, v_refs)):
            g = flat(*pieces[i])
            if w_ref.ndim == 3:
                rows, cols = w_ref.shape[1:]
                full = pieces[i][1] // rows
                picked = []
                for r in range(rows):
                    blocks = [g[:, r * full + q * cols:r * full + (q + 1) * cols] for q in range(N_CHIPS)]
                    mine = blocks[N_CHIPS - 1]
                    for q in range(N_CHIPS - 2, -1, -1):
                        mine = jnp.where(chip == q, blocks[q], mine)
                    picked.append(mine)
                g = jnp.concatenate(picked, axis=0)
                w, m, v = w_ref[0], m_ref[0], v_ref[0]
            else:
                w, m, v = w_ref[...], m_ref[...], v_ref[...]
            delta, m2, v2 = _adam_math(w, g, m, v)
            for j, val in enumerate((g, delta, m2, v2)):
                out = outs[j * k + i]
                if w_ref.ndim == 3:
                    out[0] = val
                else:
                    out[...] = val
        outs[4 * k][...] = flat(*pieces[k])

    shapes = [jax.ShapeDtypeStruct(w.shape, F32) for w in weights]
    grid_spec = pltpu.PrefetchScalarGridSpec(
        num_scalar_prefetch=1, grid=(1,),
        in_specs=[pl.BlockSpec(gathered.shape, lambda i, p: (0, 0))]
        + [pl.BlockSpec(a.shape, functools.partial(lambda nd, i, p: (0,) * nd, a.ndim))
           for a in (*weights, *moments_m, *moments_v)],
        out_specs=tuple(pl.BlockSpec(s.shape, functools.partial(lambda nd, i, p: (0,) * nd, len(s.shape)))
                        for s in shapes * 4) + (pl.BlockSpec((1, LANES), lambda i, p: (0, 0)),))
    outs = pl.pallas_call(
        body, name="small_update", grid_spec=grid_spec,
        out_shape=tuple(shapes * 4) + (jax.ShapeDtypeStruct((1, LANES), F32),),
        compiler_params=_params(("arbitrary",)),
    )(place, gathered, *weights, *moments_m, *moments_v)
    return outs[0:k], outs[k:2 * k], outs[2 * k:3 * k], outs[3 * k:4 * k], outs[4 * k]


def _pack_small(pieces):
    flat = [p.reshape(-1).astype(F32) for p in pieces]
    offsets, total = [], 0
    for p in flat:
        offsets.append(total)
        total += p.shape[0]
    padded = -(-total // SMALL_ALIGN) * SMALL_ALIGN
    if padded > total:
        flat.append(jnp.zeros((padded - total,), F32))
    return jnp.concatenate(flat).reshape(8, padded // 8), offsets


def _alibi_slope_rows(n_heads):
    slopes = 2.0 ** (-8.0 * jnp.arange(1, n_heads + 1, dtype=F32) / n_heads)
    rows = jnp.zeros((n_heads // 2, 8), F32).at[:, 0:2].set(slopes.reshape(n_heads // 2, 2))
    return jnp.broadcast_to(rows[:, :, None], (n_heads // 2, 8, ATT_KW))


def kernel(x, c, w_ada, b_ada, g_pre, w_in, conv_w, conv_b, g_conv, g_attn, w_out, g_post, loss_target, m_w_ada, m_b_ada, m_g_pre, m_w_in, m_conv_w, m_conv_b, m_g_conv, m_g_attn, m_w_out, m_g_post, v_w_ada, v_b_ada, v_g_pre, v_w_in, v_conv_w, v_conv_b, v_g_conv, v_g_attn, v_w_out, v_g_post):
    t, d = x.shape[1], x.shape[2]
    dc = conv_b.shape[1]
    da = g_attn.shape[1]
    hp = da // PAIR
    ws = w_in.shape[2]
    wa = w_ada.shape[2]
    cws = conv_w.shape[2]
    assert t % ROW_TILE == 0 and d % ROW_TILE == 0 and dc % COL_TILE == 0 and da % COL_TILE == 0
    assert ws == 2 * dc and dc == da and t // BRANCHES[-1][1] >= ATT_BQ

    mx, my, mc = _my_place()
    chip = _chip_of(mx, my)
    dev = 2 * chip + mc
    place = jnp.stack([chip, mc]).astype(jnp.int32)

    x2, tgt2 = x[0], loss_target[0]
    w_ada2, w_in2, w_out2 = w_ada[0], w_in[0], w_out[0]

    packed, offs = _pack_small([c[0], conv_w[0]])
    seen, mod, win_slots = _ada_modulation(packed, w_ada2, b_ada, d, w_in2)
    seen = seen.reshape(N_DEV, -1)
    c_all = seen[:, offs[0]:offs[0] + d]
    conv_w_full = seen[0::2, offs[1]:offs[1] + 3 * cws].reshape(N_CHIPS, 3, cws).transpose(1, 0, 2).reshape(3, dc)

    win_flight, send_in, recv_in, started = _gather_start(win_slots, mod)

    y_chip, x_chip, d_chip = (_chip_of(mx, 1 - my), _chip_of(1 - mx, my), _chip_of(1 - mx, 1 - my))
    tiles_per_part = ws // COL_TILE // 2

    def tiles_of(chunk, parts):
        return [(2 * chunk + part) * tiles_per_part + k for part in parts for k in range(tiles_per_part)]

    own_tiles, first_tiles, second_tiles, far_tiles = (jnp.stack(tiles).astype(jnp.int32) for tiles in (
        tiles_of(chip, (0, 1)), tiles_of(y_chip, (0,)) + tiles_of(x_chip, (1,)),
        tiles_of(y_chip, (1,)) + tiles_of(x_chip, (0,)), tiles_of(d_chip, (0, 1))))
    h, ht = _prenorm(x2, mod, g_pre, started)
    proj = _proj_tiles(None, h, w_in2, own_tiles, "proj_own")
    win_flight, wout_flight, relay_send_in, relay_recv_in, send_out, recv_out = _gather_relay_in(
        win_flight, _cast_into_slot(place, w_out2, "cast_w_out"), recv_in, proj)
    win_flight = _forward_halves(win_flight, ((0, 0), (1, 1)), "forward_w_in_first")
    proj = _proj_tiles(proj, h, win_flight, first_tiles, "proj_first_parts")
    win_flight = _forward_halves(
        _gather_wait_direct(win_flight, send_in, recv_in, proj, "gather_wait_w_in_direct"),
        ((0, 1), (1, 0)), "forward_w_in_second")
    proj = _proj_tiles(proj, h, win_flight, second_tiles, "proj_second_parts")
    winf = _forward_halves(
        _gather_wait_relayed(win_flight, relay_send_in, relay_recv_in, proj, "gather_wait_w_in_relayed"),
        ((2, None),), "forward_w_in_relayed")
    proj = _proj_tiles(proj, h, winf, far_tiles, "proj_diagonal")
    slopes = _alibi_slope_rows(da // HEAD_DIM)
    co = _conv_fwd(proj, conv_w_full, conv_b, dc)
    wout_flight, relay_send_out, relay_recv_out = _gather_relay_out(wout_flight, recv_out, co)
    o_mix, lse = _attn_fwd(proj, slopes, dc, da)
    g_attn_pairs = g_attn.reshape(hp, 1, PAIR)
    wout_flight = _gather_wait_direct(wout_flight, send_out, recv_out, o_mix, "gather_wait_w_out_direct")
    wout_flight = _gather_wait_relayed(wout_flight, relay_send_out, relay_recv_out, o_mix, "gather_wait_w_out_relayed")
    all_halves = ((0, None), (1, None), (2, None))
    wout_flight, fsend_out, frecv_out, forwarding = _forward_start(wout_flight, all_halves, "forward_w_out_start")
    ycat, ycat_t = _mix_fwd(co, proj, o_mix, g_conv, g_attn_pairs, forwarding)
    woutf = _forward_wait(wout_flight, all_halves, fsend_out, frecv_out, ycat, "forward_w_out_wait").reshape(dc + da, d)
    dout, dy, post_sums = _out_fwd_bwd(ycat, woutf, x2, tgt2, mod, g_post)

    gout, rsib_out = _dw_swapped(ycat_t, dy, N_CHIPS, 1, "dw_out")
    csum_out = _chip_sums(gout, rsib_out, "rs_chip_sum_out")
    ssem_out, rsem_out, csum_out, land_out, sent_out = _owners_start(csum_out, "rs_owners_start_out")
    dycat = _matmul_nt(dy, woutf, BF16, "dycat")
    dproj, dco, d_o, delta, dg_conv, dg_attn = _mix_bwd(dycat, co, proj, o_mix, g_conv, g_attn_pairs)
    dproj, conv_sums = _conv_bwd(dproj, dco, proj, conv_w_full, dc, sent_out)
    dproj = _attn_bwd(dproj, proj, d_o, lse, delta, slopes, dc, da, sent_out)
    gin, rsib_in = _dw_swapped(ht, dproj, 1, N_CHIPS, "dw_in")
    ssem_in0, rsem_in0, csum_in0, land_in0, sent_in0 = _owners_start(
        _chip_sums(gin, rsib_in, "rs_chip_sum_in0", 0, 2), "rs_owners_start_in0")
    ssem_in1, rsem_in1, csum_in1, land_in1, sent_in = _owners_start(
        _chip_sums(gin, rsib_in, "rs_chip_sum_in1", 1, 2, after=(sent_in0,)), "rs_owners_start_in1")
    dh = _dh(dproj, winf, sent_in)
    grad_x, pre_sums = _prenorm_bwd(x2, dh, dout, mod, g_pre)

    small, so = _pack_small([
        pre_sums[0], pre_sums[1], post_sums[0],
        pre_sums[2], conv_sums[0:3], conv_sums[3], dg_conv, dg_attn, post_sums[1], post_sums[2, 0:128]])
    ssem_small, rsem_small, small, land_small, sent_small = _allgather8_start(small, dev, "gather_small_start")

    rici_out = _owners_wait(ssem_out, rsem_out, csum_out, land_out, [grad_x, sent_small], "rs_owners_wait_out")
    full_out, jsend_out, jrecv_out, joining_out = _join_start(
        _owner_sum(place, gout, rsib_out, rici_out, "rs_owner_sum_out"), "rs_join_start_out", 0, 1)
    rici_in = _owners_wait(ssem_in0, rsem_in0, csum_in0, land_in0, [joining_out], "rs_owners_wait_in0")
    full_in0, jsend0, jrecv0, joining0 = _join_start(
        _owner_sum(place, gin, rsib_in, rici_in, "rs_owner_sum_in0", 0, 2), "rs_join_start_in0", 0, 2)
    grad_w_out = _join_wait(full_out, jsend_out, jrecv_out, [joining0], "rs_join_wait_out", 0, 1)
    grad_w_out, delta_w_out, new_m_w_out, new_v_w_out = _adamw(
        w_out2, grad_w_out, m_w_out[0], v_w_out[0], "adamw_w_out")
    full_in0 = _join_wait(full_in0, jsend0, jrecv0, [delta_w_out], "rs_join_wait_in0", 0, 2)
    updated_in = _adamw(w_in2, full_in0, m_w_in[0], v_w_in[0], "adamw_w_in0", 0, 2)
    rici_in = _owners_wait(ssem_in1, rsem_in1, csum_in1, land_in1, [updated_in[1]], "rs_owners_wait_in1")
    full_in1, jsend1, jrecv1, joining1 = _join_start(
        _owner_sum(place, gin, rsib_in, rici_in, "rs_owner_sum_in1", 1, 2), "rs_join_start_in1", 1, 2)

    small_seen = _allgather8_wait(ssem_small, rsem_small, small, land_small, [joining1], "gather_small_wait")
    small_w = [b_ada, g_pre, conv_w, conv_b, g_conv, g_attn, g_post]
    small_m = [m_b_ada, m_g_pre, m_conv_w, m_conv_b, m_g_conv, m_g_attn, m_g_post]
    small_v = [v_b_ada, v_g_pre, v_conv_w, v_conv_b, v_g_conv, v_g_attn, v_g_post]
    pieces = [(0, 3 * d), (so[3], d), (so[4], 3 * dc), (so[5], dc), (so[6], dc), (so[7], da), (so[8], d), (so[9], LANES)]
    g_small, d_small, m_small, v_small, loss_row = _small_update(place, small_seen, pieces, small_w, small_m, small_v)
    loss = loss_row[0, 0]
    grad_b_ada, grad_g_pre, grad_conv_w, grad_conv_b, grad_g_conv, grad_g_attn, grad_g_post = g_small
    dmod_cols = lax.dynamic_slice_in_dim(small_seen.reshape(N_DEV, -1), chip * wa, wa, axis=1)
    grad_w_ada, delta_w_ada, new_m_w_ada, new_v_w_ada = _ada_grad_adamw(c_all.T, dmod_cols, w_ada2, m_w_ada[0], v_w_ada[0])

    full_in1 = _join_wait(full_in1, jsend1, jrecv1, [delta_w_ada, d_small[0]], "rs_join_wait_in1", 1, 2)
    grad_w_in, delta_w_in, new_m_w_in, new_v_w_in = _adamw(
        w_in2, full_in1, m_w_in[0], v_w_in[0], "adamw_w_in1", 1, 2, updated_in)

    def lead(a):
        return a.reshape((1,) + a.shape)

    grads = [lead(grad_w_ada), grad_b_ada, grad_g_pre, lead(grad_w_in), grad_conv_w, grad_conv_b, grad_g_conv,
             grad_g_attn, lead(grad_w_out), grad_g_post]
    deltas = [lead(delta_w_ada), d_small[0], d_small[1], lead(delta_w_in), d_small[2], d_small[3], d_small[4],
              d_small[5], lead(delta_w_out), d_small[6]]
    new_ms = [lead(new_m_w_ada), m_small[0], m_small[1], lead(new_m_w_in), m_small[2], m_small[3], m_small[4],
              m_small[5], lead(new_m_w_out), m_small[6]]
    new_vs = [lead(new_v_w_ada), v_small[0], v_small[1], lead(new_v_w_in), v_small[2], v_small[3], v_small[4],
              v_small[5], lead(new_v_w_out), v_small[6]]
    return (loss, lead(grad_x), *grads, *deltas, *new_ms, *new_vs)
```

```python
import functools

import jax
import jax.numpy as jnp
from jax import lax
from jax.experimental import pallas as pl
from jax.experimental.pallas import tpu as pltpu

F32 = jnp.float32
BF16 = jnp.bfloat16
MESH = pl.DeviceIdType.MESH
HBM = pl.BlockSpec(memory_space=pltpu.HBM)
VMEM = pl.BlockSpec(memory_space=pltpu.VMEM)
ANY = pl.BlockSpec(memory_space=pl.ANY)
SEM = pl.BlockSpec(memory_space=pltpu.SEMAPHORE)
EFFECT = pltpu.SideEffectType.DATAFLOW_SIDE_EFFECTING
SUBLANES, LANES = 8, 128
TOKEN = jax.ShapeDtypeStruct((SUBLANES, LANES), jnp.float32)

HEAD_DIM = 64
PAIR = 2 * HEAD_DIM
assert PAIR == LANES
BRANCHES = ((128, 1), (512, 4), (2048, 16))
SIDE = 64
EPS = 1e-6
NEG_INF = -1e30
N_CHIPS = 4
N_DEV = 8

ADAM_LR = 0.001
ADAM_B1 = 0.9
ADAM_B2 = 0.999
ADAM_EPS = 1e-08
ADAM_WD = 0.01
ADAM_STEP = 10

VMEM_LIMIT_BYTES = 56 * 1024 * 1024
ROW_TILE = 256
COL_TILE = 512
CONV_TILE = 256
ATT_BQ = 128
ATT_KW = ATT_BQ + 2 * SIDE
ATT_UNROLL = 4
SMALL_ALIGN = SUBLANES * LANES


def _params(semantics=None):
    kw = {"vmem_limit_bytes": VMEM_LIMIT_BYTES}
    if semantics is not None:
        kw["dimension_semantics"] = semantics
    return pltpu.CompilerParams(**kw)


def _silu(z):
    return z * jax.nn.sigmoid(z)


def _silu_grad(z):
    s = jax.nn.sigmoid(z)
    return s * (1.0 + z * (1.0 - s))


def _my_place():
    return lax.axis_index("x"), lax.axis_index("y"), lax.axis_index("c")


def _flip(a, bit):
    return 1 - a if bit else a


def _chip_of(x, y):
    return 2 * x + y


def _allgather8_start(v, me, name):
    rows_per, n = v.shape
    land = lax.dynamic_update_slice(jnp.zeros((N_DEV * rows_per, n), v.dtype), v, (me * rows_per, 0))

    def body(v_ref, land_ref, send_sems, recv_sems, v_thru, land_thru, token_ref):
        del v_thru, land_thru
        x, y, c = _my_place()
        mine = land_ref.at[pl.ds(pl.multiple_of((4 * x + 2 * y + c) * rows_per, rows_per), rows_per), :]
        for k in range(1, N_DEV):
            peer = (_flip(x, k & 4), _flip(y, k & 2), _flip(c, k & 1))
            pltpu.make_async_remote_copy(
                src_ref=v_ref, dst_ref=mine, send_sem=send_sems.at[k - 1], recv_sem=recv_sems.at[k - 1],
                device_id=peer, device_id_type=MESH).start()
        token_ref[...] = jnp.zeros(token_ref.shape, F32)

    sems = pltpu.SemaphoreType.DMA((N_DEV - 1,))
    return pl.pallas_call(
        body, name=name,
        out_shape=(sems, sems, jax.ShapeDtypeStruct(v.shape, v.dtype), jax.ShapeDtypeStruct(land.shape, land.dtype), TOKEN),
        in_specs=[HBM, HBM], out_specs=(SEM, SEM, HBM, HBM, VMEM),
        input_output_aliases={0: 2, 1: 3},
        compiler_params=pltpu.CompilerParams(has_side_effects=EFFECT),
    )(pltpu.with_memory_space_constraint(v, pltpu.HBM), pltpu.with_memory_space_constraint(land, pltpu.HBM))


def _allgather8_wait(send_sems, recv_sems, v, land, after, name):
    rows_per = v.shape[0]

    def body(v_ref, land_ref, send_ref, recv_ref, *rest):
        del rest
        x, y, c = _my_place()
        for k in range(1, N_DEV):
            peer = (_flip(x, k & 4), _flip(y, k & 2), _flip(c, k & 1))
            src = 4 * peer[0] + 2 * peer[1] + peer[2]
            cp = pltpu.make_async_remote_copy(
                src_ref=v_ref, dst_ref=land_ref.at[pl.ds(pl.multiple_of(src * rows_per, rows_per), rows_per), :],
                send_sem=send_ref.at[k - 1], recv_sem=recv_ref.at[k - 1], device_id=peer, device_id_type=MESH)
            cp.wait_send()
            cp.wait_recv()

    return pl.pallas_call(
        body, name=name,
        out_shape=(jax.ShapeDtypeStruct(v.shape, v.dtype), jax.ShapeDtypeStruct(land.shape, land.dtype)),
        in_specs=[HBM, HBM, SEM, SEM] + [ANY] * len(after), out_specs=(HBM, HBM),
        input_output_aliases={0: 0, 1: 1},
        compiler_params=pltpu.CompilerParams(has_side_effects=EFFECT),
    )(v, land, send_sems, recv_sems, *after)[1]


def _half_rows(ref, chip, which, half):
    return ref.at[chip, pl.ds(pl.multiple_of(which * half, half), half), :]


def _ici_peers(x, y, c):
    peers = [(_flip(x, k & 2), _flip(y, k & 1), c) for k in (1, 2, 3)]
    return [(peer, _chip_of(peer[0], peer[1])) for peer in peers]


def _part_of_half(ref, chip, core, part):
    half, cols = ref.shape[1] // 2, ref.shape[2] // 2
    return ref.at[chip, pl.ds(pl.multiple_of(core * half, half), half), pl.ds(part * cols, cols)]


def _neighbours(x, y, c):
    return [((x, 1 - y, c), _chip_of(x, 1 - y)), ((1 - x, y, c), _chip_of(1 - x, y)),
            ((1 - x, 1 - y, c), _chip_of(1 - x, 1 - y))]


def _start_direct(buf, send_sems, recv_sems):
    x, y, c = _my_place()
    me = _chip_of(x, y)
    for n, (peer, _) in enumerate(_neighbours(x, y, c)[0:2]):
        for part in ((0, 1), (1, 0))[n]:
            piece = _part_of_half(buf, me, c, part)
            pltpu.make_async_remote_copy(
                src_ref=piece, dst_ref=piece, send_sem=send_sems.at[2 * n + part], recv_sem=recv_sems.at[2 * n + part],
                device_id=peer, device_id_type=MESH).start()


def _relay(buf, recv_sems, relay_send, relay_recv):
    x, y, c = _my_place()
    nbrs = _neighbours(x, y, c)
    for n in range(2):
        part = n
        piece = _part_of_half(buf, nbrs[n][1], c, part)
        pltpu.make_async_remote_copy(
            src_ref=piece, dst_ref=piece, send_sem=relay_send.at[part], recv_sem=recv_sems.at[2 * n + part],
            device_id=nbrs[n][0], device_id_type=MESH).wait_recv()
        pltpu.make_async_remote_copy(
            src_ref=piece, dst_ref=piece, send_sem=relay_send.at[part], recv_sem=relay_recv.at[part],
            device_id=nbrs[1 - n][0], device_id_type=MESH).start()


def _gather_start(win_slots, after):
    def body(win_in, after_ref, win_ref, send_sems, recv_sems, token_ref):
        del win_in, after_ref
        _start_direct(win_ref, send_sems, recv_sems)
        token_ref[...] = jnp.zeros(token_ref.shape, F32)

    sems = pltpu.SemaphoreType.DMA((4,))
    return pl.pallas_call(
        body, name="gather_start",
        out_shape=(jax.ShapeDtypeStruct(win_slots.shape, win_slots.dtype), sems, sems, TOKEN),
        in_specs=[HBM, ANY], out_specs=(HBM, SEM, SEM, VMEM),
        input_output_aliases={0: 0},
        compiler_params=pltpu.CompilerParams(has_side_effects=EFFECT),
    )(win_slots, after)


def _gather_relay_in(win, wout_slots, recv_in, after):
    def body(win_in, wout_in, recv_in_ref, after_ref, win_ref, wout_ref, relay_send, relay_recv, send_out, recv_out):
        del win_in, wout_in, after_ref
        _relay(win_ref, recv_in_ref, relay_send, relay_recv)
        _start_direct(wout_ref, send_out, recv_out)

    two, four = pltpu.SemaphoreType.DMA((2,)), pltpu.SemaphoreType.DMA((4,))
    return pl.pallas_call(
        body, name="gather_relay_w_in",
        out_shape=(jax.ShapeDtypeStruct(win.shape, win.dtype), jax.ShapeDtypeStruct(wout_slots.shape, wout_slots.dtype),
                   two, two, four, four),
        in_specs=[HBM, HBM, SEM, ANY], out_specs=(HBM, HBM, SEM, SEM, SEM, SEM),
        input_output_aliases={0: 0, 1: 1},
        compiler_params=pltpu.CompilerParams(has_side_effects=EFFECT),
    )(win, wout_slots, recv_in, after)


def _gather_relay_out(wout, recv_out, after):
    def body(wout_in, recv_out_ref, after_ref, wout_ref, relay_send, relay_recv):
        del wout_in, after_ref
        _relay(wout_ref, recv_out_ref, relay_send, relay_recv)

    two = pltpu.SemaphoreType.DMA((2,))
    return pl.pallas_call(
        body, name="gather_relay_w_out",
        out_shape=(jax.ShapeDtypeStruct(wout.shape, wout.dtype), two, two),
        in_specs=[HBM, SEM, ANY], out_specs=(HBM, SEM, SEM),
        input_output_aliases={0: 0},
        compiler_params=pltpu.CompilerParams(has_side_effects=EFFECT),
    )(wout, recv_out, after)


def _gather_wait_direct(buf, send_sems, recv_sems, after, name):
    def body(buf_in, send_ref, recv_ref, after_ref, buf_ref):
        del buf_in, after_ref
        x, y, c = _my_place()
        me = _chip_of(x, y)
        for n, (peer, chip) in enumerate(_neighbours(x, y, c)[0:2]):
            second = 1 - n
            pltpu.make_async_remote_copy(
                src_ref=_part_of_half(buf_ref, me, c, second), dst_ref=_part_of_half(buf_ref, chip, c, second),
                send_sem=send_ref.at[2 * n + second], recv_sem=recv_ref.at[2 * n + second],
                device_id=peer, device_id_type=MESH).wait_recv()
            for part in range(2):
                piece = _part_of_half(buf_ref, me, c, part)
                pltpu.make_async_remote_copy(
                    src_ref=piece, dst_ref=piece, send_sem=send_ref.at[2 * n + part], recv_sem=recv_ref.at[2 * n + part],
                    device_id=peer, device_id_type=MESH).wait_send()

    return pl.pallas_call(
        body, name=name,
        out_shape=jax.ShapeDtypeStruct(buf.shape, buf.dtype),
        in_specs=[HBM, SEM, SEM, ANY], out_specs=HBM,
        input_output_aliases={0: 0},
        compiler_params=pltpu.CompilerParams(has_side_effects=EFFECT),
    )(buf, send_sems, recv_sems, after)


def _gather_wait_relayed(buf, relay_send, relay_recv, after, name):
    def body(buf_in, rsend_ref, rrecv_ref, after_ref, buf_ref):
        del buf_in, after_ref
        x, y, c = _my_place()
        nbrs = _neighbours(x, y, c)
        for n in range(2):
            relayed = _part_of_half(buf_ref, nbrs[n][1], c, n)
            cp = pltpu.make_async_remote_copy(
                src_ref=relayed, dst_ref=_part_of_half(buf_ref, nbrs[2][1], c, n),
                send_sem=rsend_ref.at[n], recv_sem=rrecv_ref.at[n], device_id=nbrs[1 - n][0], device_id_type=MESH)
            cp.wait_recv()
            cp.wait_send()

    return pl.pallas_call(
        body, name=name,
        out_shape=jax.ShapeDtypeStruct(buf.shape, buf.dtype),
        in_specs=[HBM, SEM, SEM, ANY], out_specs=HBM,
        input_output_aliases={0: 0},
        compiler_params=pltpu.CompilerParams(has_side_effects=EFFECT),
    )(buf, relay_send, relay_recv, after)


def _forward_copies(buf_ref, which, send_sems, recv_sems):
    half = buf_ref.shape[1] // 2
    x, y, c = _my_place()

    def copy(k, chip, core, part):
        piece = _half_rows(buf_ref, chip, core, half) if part is None else _part_of_half(buf_ref, chip, core, part)
        return pltpu.make_async_remote_copy(
            src_ref=piece, dst_ref=piece, send_sem=send_sems.at[k], recv_sem=recv_sems.at[k],
            device_id=(x, y, 1 - c), device_id_type=MESH)

    chips = [_neighbours(x, y, c)[n][1] for n, _ in which]
    return [(copy(k, chip, c, part), copy(k, chip, 1 - c, part)) for k, (chip, (_, part)) in enumerate(zip(chips, which))]


def _forward_halves(buf, which, name):
    def body(buf_in, buf_ref, send_sems, recv_sems):
        del buf_in
        copies = _forward_copies(buf_ref, which, send_sems, recv_sems)
        for mine, _ in copies:
            mine.start()
        for mine, theirs in copies:
            theirs.wait_recv()
        for mine, _ in copies:
            mine.wait_send()

    return pl.pallas_call(
        body, name=name,
        out_shape=jax.ShapeDtypeStruct(buf.shape, buf.dtype),
        in_specs=[HBM], out_specs=HBM,
        input_output_aliases={0: 0},
        scratch_shapes=[pltpu.SemaphoreType.DMA((len(which),))] * 2,
    )(buf)


def _forward_start(buf, which, name):
    def body(buf_in, buf_ref, send_sems, recv_sems, token_ref):
        del buf_in
        for mine, _ in _forward_copies(buf_ref, which, send_sems, recv_sems):
            mine.start()
        token_ref[...] = jnp.zeros(token_ref.shape, F32)

    sems = pltpu.SemaphoreType.DMA((len(which),))
    return pl.pallas_call(
        body, name=name,
        out_shape=(jax.ShapeDtypeStruct(buf.shape, buf.dtype), sems, sems, TOKEN),
        in_specs=[HBM], out_specs=(HBM, SEM, SEM, VMEM),
        input_output_aliases={0: 0},
        compiler_params=pltpu.CompilerParams(has_side_effects=EFFECT),
    )(buf)


def _forward_wait(buf, which, send_sems, recv_sems, after, name):
    def body(buf_in, send_ref, recv_ref, after_ref, buf_ref):
        del buf_in, after_ref
        for mine, theirs in _forward_copies(buf_ref, which, send_ref, recv_ref):
            theirs.wait_recv()
            mine.wait_send()

    return pl.pallas_call(
        body, name=name,
        out_shape=jax.ShapeDtypeStruct(buf.shape, buf.dtype),
        in_specs=[HBM, SEM, SEM, ANY], out_specs=HBM,
        input_output_aliases={0: 0},
        compiler_params=pltpu.CompilerParams(has_side_effects=EFFECT),
    )(buf, send_sems, recv_sems, after)


def _dw_swapped(a, b, row_chunks, col_chunks, name):
    r, t = a.shape
    c_all = b.shape[1]
    chunks = row_chunks * col_chunks
    rq, cq = r // row_chunks, c_all // col_chunks
    half = rq // 2
    tn = COL_TILE
    nt = cq // tn
    steps = col_chunks * nt

    def body(a_ref, b_ref, mine_ref, sib_ref, stage, send_sems, recv_sems):
        x, y, c = _my_place()
        j, n = pl.program_id(0), pl.program_id(1)
        step = j * nt + n
        slot = step % 2
        res = jnp.dot(a_ref[...], b_ref[...], preferred_element_type=F32).astype(BF16)

        def landing(jj, nn):
            cols = pl.ds(pl.multiple_of(nn * tn, tn), tn)
            return sib_ref.at[:, :, cols] if col_chunks == 1 else sib_ref.at[pl.ds(jj, 1), :, cols]

        def copy(slot_, step_, jj, nn):
            return pltpu.make_async_remote_copy(
                src_ref=stage.at[slot_], dst_ref=landing(jj, nn), send_sem=send_sems.at[slot_],
                recv_sem=recv_sems.at[step_], device_id=(x, y, 1 - c), device_id_type=MESH)

        @pl.when(step >= 2)
        def _():
            copy(slot, step, j, n).wait_send()

        for q in range(row_chunks):
            lo = res[q * rq:q * rq + half, :]
            hi = res[q * rq + half:(q + 1) * rq, :]
            mine_ref[q] = jnp.where(c == 0, lo, hi)
            stage[slot, q] = jnp.where(c == 0, hi, lo)
        copy(slot, step, j, n).start()

        @pl.when(step == steps - 1)
        def _():
            for s in range(max(steps - 2, 0), steps):
                copy(s % 2, s, j, n).wait_send()
            for s in range(steps):
                copy(s % 2, s, j, n).wait_recv()

    shape = jax.ShapeDtypeStruct((chunks, half, cq), BF16)
    return pl.pallas_call(
        body, name=name, grid=(col_chunks, nt),
        out_shape=(shape, shape),
        in_specs=[pl.BlockSpec((r, t), lambda j, n: (0, 0)), pl.BlockSpec((t, tn), lambda j, n: (0, j * nt + n))],
        out_specs=(pl.BlockSpec((row_chunks, half, tn), lambda j, n: (j, 0, n)), ANY),
        scratch_shapes=[pltpu.VMEM((2, row_chunks, half, tn), BF16), pltpu.SemaphoreType.DMA((2,)),
                        pltpu.SemaphoreType.DMA((steps,))],
        compiler_params=_params(("arbitrary", "arbitrary")),
    )(a, b)


def _owners_start(csum, name, after=()):
    land = pltpu.with_memory_space_constraint(lax.empty((N_CHIPS - 1,) + csum.shape[1:], csum.dtype), pltpu.HBM)

    def body(csum_ref, land_ref, *rest):
        send_sems, recv_sems, _, _, token_ref = rest[len(after):]
        x, y, c = _my_place()
        for k, (peer, owner) in enumerate(_ici_peers(x, y, c)):
            pltpu.make_async_remote_copy(
                src_ref=csum_ref.at[owner], dst_ref=land_ref.at[k], send_sem=send_sems.at[k], recv_sem=recv_sems.at[k],
                device_id=peer, device_id_type=MESH).start()
        token_ref[...] = jnp.zeros(token_ref.shape, F32)

    sems = pltpu.SemaphoreType.DMA((N_CHIPS - 1,))
    return pl.pallas_call(
        body, name=name,
        out_shape=(sems, sems, jax.ShapeDtypeStruct(csum.shape, csum.dtype),
                   jax.ShapeDtypeStruct(land.shape, land.dtype), TOKEN),
        in_specs=[HBM, HBM] + [ANY] * len(after), out_specs=(SEM, SEM, HBM, HBM, VMEM),
        input_output_aliases={0: 2, 1: 3},
        compiler_params=pltpu.CompilerParams(has_side_effects=EFFECT),
    )(pltpu.with_memory_space_constraint(csum, pltpu.HBM), land, *after)


def _owners_wait(send_sems, recv_sems, csum, land, after, name):
    def body(csum_ref, land_ref, send_ref, recv_ref, *rest):
        del rest
        x, y, c = _my_place()
        for k, (peer, owner) in enumerate(_ici_peers(x, y, c)):
            cp = pltpu.make_async_remote_copy(
                src_ref=csum_ref.at[owner], dst_ref=land_ref.at[k], send_sem=send_ref.at[k], recv_sem=recv_ref.at[k],
                device_id=peer, device_id_type=MESH)
            cp.wait_send()
            cp.wait_recv()

    return pl.pallas_call(
        body, name=name,
        out_shape=(jax.ShapeDtypeStruct(csum.shape, csum.dtype), jax.ShapeDtypeStruct(land.shape, land.dtype)),
        in_specs=[HBM, HBM, SEM, SEM] + [ANY] * len(after), out_specs=(HBM, HBM),
        input_output_aliases={0: 0, 1: 1},
        compiler_params=pltpu.CompilerParams(has_side_effects=EFFECT),
    )(csum, land, send_sems, recv_sems, *after)[1]


def _join_start(full, name, part, parts):
    half = full.shape[0] // 2
    rows = half // parts

    def body(full_in, full_ref, send_sem, recv_sem, token_ref):
        del full_in
        x, y, c = _my_place()
        mine = full_ref.at[pl.ds(pl.multiple_of(c * half + part * rows, rows), rows), :]
        pltpu.make_async_remote_copy(
            src_ref=mine, dst_ref=mine, send_sem=send_sem.at[0], recv_sem=recv_sem.at[0],
            device_id=(x, y, 1 - c), device_id_type=MESH).start()
        token_ref[...] = jnp.zeros(token_ref.shape, F32)

    one = pltpu.SemaphoreType.DMA((1,))
    return pl.pallas_call(
        body, name=name,
        out_shape=(jax.ShapeDtypeStruct(full.shape, full.dtype), one, one, TOKEN),
        in_specs=[HBM], out_specs=(HBM, SEM, SEM, VMEM),
        input_output_aliases={0: 0},
        compiler_params=pltpu.CompilerParams(has_side_effects=EFFECT),
    )(full)


def _join_wait(full, send_sem, recv_sem, after, name, part, parts):
    half = full.shape[0] // 2
    rows = half // parts

    def body(full_in, send_ref, recv_ref, *rest):
        del full_in
        full_ref = rest[-1]
        x, y, c = _my_place()
        cp = pltpu.make_async_remote_copy(
            src_ref=full_ref.at[pl.ds(pl.multiple_of(c * half + part * rows, rows), rows), :],
            dst_ref=full_ref.at[pl.ds(pl.multiple_of((1 - c) * half + part * rows, rows), rows), :],
            send_sem=send_ref.at[0], recv_sem=recv_ref.at[0], device_id=(x, y, 1 - c), device_id_type=MESH)
        cp.wait_send()
        cp.wait_recv()

    return pl.pallas_call(
        body, name=name,
        out_shape=jax.ShapeDtypeStruct(full.shape, full.dtype),
        in_specs=[HBM, SEM, SEM] + [ANY] * len(after), out_specs=HBM,
        input_output_aliases={0: 0},
        compiler_params=pltpu.CompilerParams(has_side_effects=EFFECT),
    )(full, send_sem, recv_sem, *after)


def _cast_into_slot(place, w, name):
    rows, cols = w.shape
    tr = min(rows, ROW_TILE)

    def body(place_ref, w_ref, o_ref):
        del place_ref
        o_ref[...] = w_ref[...].astype(BF16)

    grid_spec = pltpu.PrefetchScalarGridSpec(
        num_scalar_prefetch=1, grid=(rows // tr,),
        in_specs=[pl.BlockSpec((tr, cols), lambda i, p: (i, 0))],
        out_specs=pl.BlockSpec((None, tr, cols), lambda i, p: (p[0], i, 0)))
    return pl.pallas_call(
        body, name=name, grid_spec=grid_spec,
        out_shape=jax.ShapeDtypeStruct((N_CHIPS, rows, cols), BF16),
        compiler_params=_params(("parallel",)),
    )(place, w)


def _ada_modulation(packed, w_ada, b_ada, d, w_big):
    rows_per, n = packed.shape
    d_model, wa = w_ada.shape
    big_rows, big_cols = w_big.shape
    n_chunks = big_rows // ROW_TILE
    first_chunks = (2 * n_chunks) // 3

    def body(v_ref, w_hbm, b_ref, big_hbm, all_ref, mod_ref, slots_hbm, w_vmem, part_ref, parts_ref, wide, narrow,
             load_sem, send1, recv1, send2, recv2, in_sems, out_sems):
        x, y, c = _my_place()
        me = 4 * x + 2 * y + c
        chip = _chip_of(x, y)
        load = pltpu.make_async_copy(w_hbm, w_vmem, load_sem)
        load.start()

        def chunk_in(i):
            return pltpu.make_async_copy(big_hbm.at[i * ROW_TILE:(i + 1) * ROW_TILE, :], wide.at[i % 2], in_sems.at[i % 2])

        def chunk_out(i):
            return pltpu.make_async_copy(
                narrow.at[i % 2], slots_hbm.at[chip, i * ROW_TILE:(i + 1) * ROW_TILE, :], out_sems.at[i % 2])

        def cast_chunk(i):
            if i + 1 < n_chunks:
                chunk_in(i + 1).start()
            chunk_in(i).wait()
            if i >= 2:
                chunk_out(i - 2).wait()
            narrow[i % 2] = wide[i % 2].astype(BF16)
            chunk_out(i).start()

        chunk_in(0).start()

        def rows(idx):
            return all_ref.at[pl.ds(pl.multiple_of(idx * rows_per, rows_per), rows_per), :]

        all_ref[pl.ds(pl.multiple_of(me * rows_per, rows_per), rows_per), :] = v_ref[...]
        copies = []
        for k in range(1, N_DEV):
            peer = (_flip(x, k & 4), _flip(y, k & 2), _flip(c, k & 1))
            cp = pltpu.make_async_remote_copy(
                src_ref=v_ref, dst_ref=rows(me), send_sem=send1.at[k - 1], recv_sem=recv1.at[k - 1],
                device_id=peer, device_id_type=MESH)
            cp.start()
            copies.append((cp, peer))
        for i in range(first_chunks):
            cast_chunk(i)
        for k, (cp, peer) in enumerate(copies):
            pltpu.make_async_remote_copy(
                src_ref=v_ref, dst_ref=rows(4 * peer[0] + 2 * peer[1] + peer[2]), send_sem=send1.at[k],
                recv_sem=recv1.at[k], device_id=peer, device_id_type=MESH).wait_recv()
        for cp, _ in copies:
            cp.wait_send()

        def c_of(dev):
            segments, pos = [], 0
            while pos < d:
                row, col = divmod(pos, n)
                take = min(d - pos, n - col)
                segments.append(all_ref[dev * rows_per + row:dev * rows_per + row + 1, col:col + take])
                pos += take
            return jnp.concatenate(segments, axis=1)

        c_all = jnp.concatenate([c_of(dev) for dev in range(N_DEV)], axis=0)
        load.wait()
        part_ref[...] = jnp.dot(_silu(c_all), w_vmem[...], precision=lax.Precision.HIGHEST, preferred_element_type=F32)
        parts_ref[chip] = part_ref[...]
        swaps = []
        for k, (peer, _) in enumerate(_ici_peers(x, y, c)):
            cp = pltpu.make_async_remote_copy(
                src_ref=part_ref, dst_ref=parts_ref.at[chip], send_sem=send2.at[k], recv_sem=recv2.at[k],
                device_id=peer, device_id_type=MESH)
            cp.start()
            swaps.append(cp)
        for i in range(first_chunks, n_chunks):
            cast_chunk(i)
        for i in range(n_chunks - 2, n_chunks):
            chunk_out(i).wait()
        for k, (peer, peer_chip) in enumerate(_ici_peers(x, y, c)):
            pltpu.make_async_remote_copy(
                src_ref=part_ref, dst_ref=parts_ref.at[peer_chip], send_sem=send2.at[k], recv_sem=recv2.at[k],
                device_id=peer, device_id_type=MESH).wait_recv()
        for cp in swaps:
            cp.wait_send()
        flat = jnp.concatenate([parts_ref[j, pl.ds(me, 1), :] for j in range(N_CHIPS)], axis=1) + b_ref[...]
        mod_ref[...] = jnp.concatenate([flat[:, i * d:(i + 1) * d] for i in range(3)], axis=0)

    return pl.pallas_call(
        body, name="ada_modulation",
        out_shape=(jax.ShapeDtypeStruct((N_DEV * rows_per, n), F32), jax.ShapeDtypeStruct((3, d), F32),
                   jax.ShapeDtypeStruct((N_CHIPS, big_rows, big_cols), BF16)),
        in_specs=[VMEM, ANY, VMEM, ANY], out_specs=(VMEM, VMEM, ANY),
        scratch_shapes=[pltpu.VMEM((d_model, wa), F32), pltpu.VMEM((N_DEV, wa), F32),
                        pltpu.VMEM((N_CHIPS, N_DEV, wa), F32),
                        pltpu.VMEM((2, ROW_TILE, big_cols), F32), pltpu.VMEM((2, ROW_TILE, big_cols), BF16),
                        pltpu.SemaphoreType.DMA,
                        pltpu.SemaphoreType.DMA((N_DEV - 1,)), pltpu.SemaphoreType.DMA((N_DEV - 1,)),
                        pltpu.SemaphoreType.DMA((N_CHIPS - 1,)), pltpu.SemaphoreType.DMA((N_CHIPS - 1,)),
                        pltpu.SemaphoreType.DMA((2,)), pltpu.SemaphoreType.DMA((2,))],
        compiler_params=_params(),
    )(packed, w_ada, b_ada, w_big)


def _prenorm(x, mod, g_pre, after):
    t, d = x.shape
    tb = ROW_TILE

    def body(x_ref, mod_ref, g_ref, after_ref, h_ref, ht_ref):
        del after_ref
        xv = x_ref[...]
        r = lax.rsqrt(jnp.mean(xv * xv, axis=-1, keepdims=True) + EPS)
        h = (xv * r) * g_ref[...] * (1.0 + mod_ref[1:2, :]) + mod_ref[0:1, :]
        h_ref[...] = h.astype(BF16)
        ht_ref[...] = h.T.astype(BF16)

    return pl.pallas_call(
        body, name="prenorm", grid=(t // tb,),
        out_shape=(jax.ShapeDtypeStruct((t, d), BF16), jax.ShapeDtypeStruct((d, t), BF16)),
        in_specs=[pl.BlockSpec((tb, d), lambda i: (i, 0)), pl.BlockSpec((3, d), lambda i: (0, 0)),
                  pl.BlockSpec((1, d), lambda i: (0, 0)), ANY],
        out_specs=(pl.BlockSpec((tb, d), lambda i: (i, 0)), pl.BlockSpec((d, tb), lambda i: (0, i))),
        compiler_params=_params(("parallel",)),
    )(x, mod, g_pre, after)


def _proj_tiles(proj, h, w, tiles, name):
    t, d = h.shape
    ws = w.shape[-1]
    tn = COL_TILE
    nt = ws // tn

    def body(tile_ref, *refs):
        del tile_ref
        a_ref, b_ref, o_ref = refs[-3:]
        o_ref[...] = jnp.dot(a_ref[...], b_ref[...].astype(BF16), preferred_element_type=F32).astype(BF16)

    if w.ndim == 3:
        w_spec = pl.BlockSpec((None, d, tn), lambda i, tl: (tl[i] // nt, 0, tl[i] % nt))
    else:
        w_spec = pl.BlockSpec((d, tn), lambda i, tl: (0, tl[i] % nt))
    first = proj is None
    grid_spec = pltpu.PrefetchScalarGridSpec(
        num_scalar_prefetch=1, grid=(tiles.shape[0],),
        in_specs=([] if first else [HBM]) + [pl.BlockSpec((t, d), lambda i, tl: (0, 0)), w_spec],
        out_specs=pl.BlockSpec((t, tn), lambda i, tl: (0, tl[i])))
    return pl.pallas_call(
        body, name=name, grid_spec=grid_spec,
        out_shape=jax.ShapeDtypeStruct((t, N_CHIPS * ws), BF16),
        input_output_aliases={} if first else {1: 0},
        compiler_params=_params(("parallel",)),
    )(*([tiles] if first else [tiles, proj]), h, w)


def _shift_rows(a, rows):
    idx = lax.broadcasted_iota(jnp.int32, a.shape, 0)
    prev = jnp.where(idx == 0, 0.0, pltpu.roll(a, 1, 0))
    nxt = jnp.where(idx == rows - 1, 0.0, pltpu.roll(a, rows - 1, 0))
    return prev, nxt


def _conv_fwd(conv_proj, conv_w, conv_b, dc):
    t = conv_proj.shape[0]
    ct = CONV_TILE
    nct = dc // ct

    def body(u_ref, cg_ref, w_ref, b_ref, co_ref):
        a = cg_ref[...].astype(F32) * u_ref[...].astype(F32)
        prev, nxt = _shift_rows(a, t)
        co_ref[...] = (w_ref[0:1, :] * prev + w_ref[1:2, :] * a + w_ref[2:3, :] * nxt + b_ref[...]).astype(BF16)

    return pl.pallas_call(
        body, name="conv_fwd", grid=(nct,),
        out_shape=jax.ShapeDtypeStruct((t, dc), BF16),
        in_specs=[pl.BlockSpec((t, ct), lambda i: (0, i)), pl.BlockSpec((t, ct), lambda i: (0, 2 * nct + i)),
                  pl.BlockSpec((3, ct), lambda i: (0, i)), pl.BlockSpec((1, ct), lambda i: (0, i))],
        out_specs=pl.BlockSpec((t, ct), lambda i: (0, i)),
        compiler_params=_params(("parallel",)),
    )(conv_proj, conv_proj, conv_w, conv_b)


def _to_residue_major(src_ref, dst_ref, r):
    seq = src_ref.shape[0] // r
    for res in range(r):
        dst_ref[res * seq:(res + 1) * seq, :] = src_ref[pl.ds(res, seq, stride=r), :].astype(dst_ref.dtype)


def _branch_operands(token_refs, stage, dil, r):
    if r == 1:
        return list(token_refs)
    for i, ref in enumerate(token_refs):
        stage[...] = ref[...].astype(F32)
        _to_residue_major(stage, dil.at[i], r)
    return [dil.at[i] for i in range(len(token_refs))]


def _scaled_queries(q):
    return (q.astype(F32) * (HEAD_DIM ** -0.5)).astype(BF16)


BLOCK_SHIFTS = (0, -SIDE, None)


def _band_bias(rel, slope):
    arel = jnp.abs(rel)
    return jnp.where(arel <= SIDE, arel.astype(F32) * slope, NEG_INF)


def _fill_bias_tiles(bias_ref, sl_ref, r, kw):
    base = lax.broadcasted_iota(jnp.int32, (ATT_BQ, kw), 1) - lax.broadcasted_iota(jnp.int32, (ATT_BQ, kw), 0)
    for hh in range(2):
        slope = -(sl_ref[hh:hh + 1, 0:kw] * float(r))
        for e, shift in enumerate(BLOCK_SHIFTS):
            shift = ATT_BQ - kw if shift is None else shift
            bias_ref[hh, e, :, 0:kw] = _band_bias(base + shift, slope)


def _fill_stacked_bias_tiles(bias_ref, sl_ref, r, kw):
    base = lax.broadcasted_iota(jnp.int32, (kw, ATT_BQ), 0) - lax.broadcasted_iota(jnp.int32, (kw, ATT_BQ), 1)
    for hh in range(2):
        slope = -(sl_ref[hh:hh + 1, 0:ATT_BQ] * float(r))
        for e, shift in enumerate(BLOCK_SHIFTS):
            shift = ATT_BQ - kw if shift is None else shift
            bias_ref[e, 0:kw, hh * ATT_BQ:(hh + 1) * ATT_BQ] = _band_bias(base + shift, slope)


def _first_head_lanes():
    return lax.broadcasted_iota(jnp.int32, (1, PAIR), 1) < HEAD_DIM


def _only_head(x, first, hh):
    return jnp.where(first if hh == 0 else jnp.logical_not(first), x, jnp.zeros_like(x))


def _block_place(g, seq_len, kw):
    nqb = seq_len // ATT_BQ
    if nqb == 1:
        row = pl.multiple_of(g * ATT_BQ, ATT_BQ)
        return row, row, 0
    res = g // nqb
    qb = g - res * nqb
    q0 = qb * ATT_BQ
    ks = jnp.clip(q0 - SIDE, 0, seq_len - kw)
    edge = jnp.where(qb == 0, 0, jnp.where(qb == nqb - 1, 2, 1))
    return (pl.multiple_of(res * seq_len + q0, ATT_BQ), pl.multiple_of(res * seq_len + ks, SIDE), edge)


def _qkv_specs(dc, da, t, index):
    return [pl.BlockSpec((t, PAIR), functools.partial(index, (4 * dc + comp * da) // PAIR)) for comp in range(3)]


def _attn_fwd(proj, slopes, dc, da):
    t = proj.shape[0]
    hp = da // PAIR
    n_blocks = t // ATT_BQ

    def body(q_ref, k_ref, v_ref, sl_ref, o_ref, lse_ref, stage, dil, bias, o_res, l_res, o_tok, l_tok):
        for b, (_, r) in enumerate(BRANCHES):
            seq_len = t // r
            kw = min(ATT_KW, seq_len)
            ops = _branch_operands([q_ref, k_ref, v_ref], stage, dil, r)
            _fill_bias_tiles(bias, sl_ref, r, kw)
            o_dst, l_dst = (o_tok.at[b], l_tok.at[b]) if r == 1 else (o_res, l_res)
            first = _first_head_lanes()

            def blocks(trip, carry, seq_len=seq_len, kw=kw, o_dst=o_dst, l_dst=l_dst, first=first, ops=ops):
                nt = (((1,), (1,)), ((), ()))
                places = [_block_place(trip * ATT_UNROLL + i, seq_len, kw) for i in range(ATT_UNROLL)]
                chains = [(i, hh) for i in range(ATT_UNROLL) for hh in range(2)]
                qs = [_scaled_queries(ops[0][pl.ds(qrow, ATT_BQ), :]) for qrow, _, _ in places]
                ks = [ops[1][pl.ds(krow, kw), :] for _, krow, _ in places]
                vs = [ops[2][pl.ds(krow, kw), :] for _, krow, _ in places]
                ss = [lax.dot_general(_only_head(qs[i], first, hh), ks[i], nt, preferred_element_type=F32)
                      + bias[hh, places[i][2], :, 0:kw] for i, hh in chains]
                tops = [jnp.max(s, axis=-1, keepdims=True) for s in ss]
                ps = [jnp.exp(s - m) for s, m in zip(ss, tops)]
                dens = [jnp.sum(p, axis=-1, keepdims=True) for p in ps]
                for i, (qrow, _, _) in enumerate(places):
                    weights = jnp.concatenate([ps[2 * i].astype(BF16), ps[2 * i + 1].astype(BF16)], axis=1)
                    values = jnp.concatenate([_only_head(vs[i], first, 0), _only_head(vs[i], first, 1)], axis=0)
                    den = jnp.where(first, dens[2 * i], dens[2 * i + 1])
                    o_dst[pl.ds(qrow, ATT_BQ), :] = jnp.dot(weights, values, preferred_element_type=F32) / den
                    l_dst[pl.ds(qrow, ATT_BQ), :] = jnp.where(first, tops[2 * i], tops[2 * i + 1]) + jnp.log(den)
                return carry

            lax.fori_loop(0, n_blocks // ATT_UNROLL, blocks, 0)
            if r > 1:
                for res in range(r):
                    rows = slice(res * seq_len, (res + 1) * seq_len)
                    o_tok[b, pl.ds(res, seq_len, stride=r), :] = o_res[rows, :]
                    l_tok[b, pl.ds(res, seq_len, stride=r), :] = l_res[rows, :]

        def merge(i, carry):
            rows = pl.ds(pl.multiple_of(i * ROW_TILE, ROW_TILE), ROW_TILE)
            la, lb, lc = l_tok[0, rows, :], l_tok[1, rows, :], l_tok[2, rows, :]
            m = jnp.maximum(jnp.maximum(la, lb), lc)
            wa, wb, wc = jnp.exp(la - m), jnp.exp(lb - m), jnp.exp(lc - m)
            den = wa + wb + wc
            o_ref[rows, :] = (wa * o_tok[0, rows, :] + wb * o_tok[1, rows, :] + wc * o_tok[2, rows, :]) * (1.0 / den)
            lse_ref[rows, :] = m + jnp.log(den)
            return carry

        lax.fori_loop(0, t // ROW_TILE, merge, 0)

    pair_spec = pl.BlockSpec((None, t, PAIR), lambda h: (h, 0, 0))
    return pl.pallas_call(
        body, name="attn_fwd", grid=(hp,),
        out_shape=(jax.ShapeDtypeStruct((hp, t, PAIR), F32), jax.ShapeDtypeStruct((hp, t, PAIR), F32)),
        in_specs=_qkv_specs(dc, da, t, lambda first, h: (0, first + h))
        + [pl.BlockSpec((None, 8, ATT_KW), lambda h: (h, 0, 0))],
        out_specs=(pair_spec, pair_spec),
        scratch_shapes=[pltpu.VMEM((t, PAIR), F32), pltpu.VMEM((3, t, PAIR), BF16),
                        pltpu.VMEM((2, 3, ATT_BQ, ATT_KW), F32),
                        pltpu.VMEM((t, PAIR), F32), pltpu.VMEM((t, PAIR), F32),
                        pltpu.VMEM((3, t, PAIR), F32), pltpu.VMEM((3, t, PAIR), F32)],
        compiler_params=_params(("parallel",)),
    )(proj, proj, proj, slopes)


def _attn_bwd(dproj, proj, d_o, lse, delta, slopes, dc, da, after):
    t = proj.shape[0]
    hp = da // PAIR
    n_blocks = t // ATT_BQ

    def all_branches(q_ref, k_ref, v_ref, do_ref, lse_ref, dl_ref, sl_ref,
                     stage, dil, packed, packed_res, row_vecs, bias_t, acc, tot):
        first = _first_head_lanes()
        lane = lax.broadcasted_iota(jnp.int32, (1, PAIR), 1)
        packed[...] = jnp.where((lane & (HEAD_DIM - 1)) < HEAD_DIM // 2, lse_ref[...], dl_ref[...])
        for b, (_, r) in enumerate(BRANCHES):
            seq_len = t // r
            kw = min(ATT_KW, seq_len)
            ops = _branch_operands([q_ref, k_ref, v_ref, do_ref], stage, dil, r)
            scalars = packed
            if r > 1:
                _to_residue_major(packed, packed_res, r)
                scalars = packed_res
            for g in range(n_blocks):
                flipped = scalars[g * ATT_BQ:(g + 1) * ATT_BQ, :].T
                for row in range(4):
                    row_vecs[g, row:row + 1, :] = flipped[row * (HEAD_DIM // 2):row * (HEAD_DIM // 2) + 1, :]
            _fill_stacked_bias_tiles(bias_t, sl_ref, r, kw)
            acc[1] = jnp.zeros((t, PAIR), F32)
            acc[2] = jnp.zeros((t, PAIR), F32)

            def blocks(trip, carry, seq_len=seq_len, kw=kw, ops=ops):
                nt = (((1,), (1,)), ((), ()))
                group = range(ATT_UNROLL)
                places = [_block_place(trip * ATT_UNROLL + i, seq_len, kw) for i in group]
                ks, vs, q2s, do2s, lse2s, dl2s = [], [], [], [], [], []
                for i, (qrow, krow, _) in zip(group, places):
                    q = _scaled_queries(ops[0][pl.ds(qrow, ATT_BQ), :])
                    dov = ops[3][pl.ds(qrow, ATT_BQ), :]
                    ks.append(ops[1][pl.ds(krow, kw), :])
                    vs.append(ops[2][pl.ds(krow, kw), :])
                    q2s.append(jnp.concatenate([_only_head(q, first, 0), _only_head(q, first, 1)], axis=0))
                    do2s.append(jnp.concatenate([_only_head(dov, first, 0), _only_head(dov, first, 1)], axis=0))
                    rows = row_vecs[trip * ATT_UNROLL + i]
                    lse2s.append(jnp.concatenate([rows[0:1, :], rows[2:3, :]], axis=1))
                    dl2s.append(jnp.concatenate([rows[1:2, :], rows[3:4, :]], axis=1))
                s_ts = [lax.dot_general(ks[i], q2s[i], nt, preferred_element_type=F32) for i in group]
                dp_ts = [lax.dot_general(vs[i], do2s[i], nt, preferred_element_type=F32) for i in group]
                p_ts = [jnp.exp(s_ts[i] + bias_t[places[i][2], 0:kw, :] - lse2s[i]) for i in group]
                ds_ts = [p_ts[i] * (dp_ts[i] - dl2s[i]) for i in group]
                dvs = [jnp.dot(p_ts[i].astype(BF16), do2s[i], preferred_element_type=F32) for i in group]
                dks = [jnp.dot(ds_ts[i].astype(BF16), q2s[i], preferred_element_type=F32) for i in group]
                dss = [ds_ts[i].T.astype(BF16) for i in group]
                dqs = [jnp.dot(dss[i][0:ATT_BQ, :], _only_head(ks[i], first, 0), preferred_element_type=F32)
                       + jnp.dot(dss[i][ATT_BQ:2 * ATT_BQ, :], _only_head(ks[i], first, 1), preferred_element_type=F32)
                       for i in group]
                for i, (qrow, krow, _) in zip(group, places):
                    acc[0, pl.ds(qrow, ATT_BQ), :] = dqs[i] * (HEAD_DIM ** -0.5)
                    acc[1, pl.ds(krow, kw), :] += dks[i]
                    acc[2, pl.ds(krow, kw), :] += dvs[i]
                return carry

            lax.fori_loop(0, n_blocks // ATT_UNROLL, blocks, 0)
            for comp in range(3):
                if r == 1:
                    tot[comp] = acc[comp]
                else:
                    for res in range(r):
                        tok = pl.ds(res, seq_len, stride=r)
                        tot[comp, tok, :] = tot[comp, tok, :] + acc[comp, res * seq_len:(res + 1) * seq_len, :]

    first_q = (4 * dc) // PAIR

    def body(dproj_in, q_ref, k_ref, v_ref, do_ref, lse_ref, dl_ref, sl_ref, after_ref, out_ref, *scratch):
        del dproj_in, after_ref
        work, out_stage, out_sems = scratch[:-2], scratch[-2], scratch[-1]
        h = pl.program_id(0)
        all_branches(q_ref, k_ref, v_ref, do_ref, lse_ref, dl_ref, sl_ref, *work)

        def out_copy(comp):
            cols = pl.ds(pl.multiple_of((first_q + comp * hp + h) * PAIR, PAIR), PAIR)
            return pltpu.make_async_copy(out_stage.at[comp], out_ref.at[:, cols], out_sems.at[comp])

        @pl.when(h > 0)
        def _():
            for comp in range(3):
                out_copy(comp).wait()

        for comp in range(3):
            out_stage[comp] = work[-1][comp].astype(BF16)
            out_copy(comp).start()

        @pl.when(h == hp - 1)
        def _():
            for comp in range(3):
                out_copy(comp).wait()

    pair_spec = pl.BlockSpec((None, t, PAIR), lambda h: (h, 0, 0))
    return pl.pallas_call(
        body, name="attn_bwd", grid=(hp,),
        out_shape=jax.ShapeDtypeStruct(dproj.shape, BF16),
        in_specs=[HBM] + _qkv_specs(dc, da, t, lambda first, h: (0, first + h))
        + [pair_spec, pair_spec, pair_spec, pl.BlockSpec((None, 8, ATT_KW), lambda h: (h, 0, 0)), ANY],
        out_specs=ANY,
        input_output_aliases={0: 0},
        scratch_shapes=[pltpu.VMEM((t, PAIR), F32), pltpu.VMEM((4, t, PAIR), BF16),
                        pltpu.VMEM((t, PAIR), F32), pltpu.VMEM((t, PAIR), F32),
                        pltpu.VMEM((n_blocks, 8, ATT_BQ), F32), pltpu.VMEM((3, ATT_KW, 2 * ATT_BQ), F32),
                        pltpu.VMEM((3, t, PAIR), F32), pltpu.VMEM((3, t, PAIR), F32),
                        pltpu.VMEM((3, t, PAIR), BF16), pltpu.SemaphoreType.DMA((3,))],
        compiler_params=_params(("arbitrary",)),
    )(dproj, proj, proj, proj, d_o, lse, delta, slopes, after)


def _mix_fwd(co, proj, o_mix, g_conv, g_attn_pairs, after):
    t, dc = co.shape
    hp = o_mix.shape[0]
    da = hp * PAIR
    tb = ROW_TILE

    def body(co_ref, bg_ref, zc_ref, za_ref, om_ref, gc_ref, ga_ref, after_ref, ycat_ref, ycatt_ref):
        del after_ref
        p = bg_ref[...].astype(F32) * co_ref[...].astype(F32)
        rc = lax.rsqrt(jnp.mean(p * p, axis=-1, keepdims=True) + EPS)
        yc = (p * rc) * gc_ref[...] * _silu(zc_ref[...].astype(F32))
        ycat_ref[:, 0:dc] = yc.astype(BF16)
        ycatt_ref[0:dc, :] = yc.T.astype(BF16)
        ssq = jnp.zeros((tb, 1), F32)
        for h in range(hp):
            o = om_ref[h]
            ssq = ssq + jnp.sum(o * o, axis=-1, keepdims=True)
        ra = lax.rsqrt(ssq * (1.0 / da) + EPS)
        for h in range(hp):
            ya = (om_ref[h] * ra) * ga_ref[h] * _silu(za_ref[:, h * PAIR:(h + 1) * PAIR].astype(F32))
            ycat_ref[:, dc + h * PAIR:dc + (h + 1) * PAIR] = ya.astype(BF16)
            ycatt_ref[dc + h * PAIR:dc + (h + 1) * PAIR, :] = ya.T.astype(BF16)

    pair_spec = pl.BlockSpec((hp, tb, PAIR), lambda i: (0, i, 0))
    return pl.pallas_call(
        body, name="mix_fwd", grid=(t // tb,),
        out_shape=(jax.ShapeDtypeStruct((t, dc + da), BF16), jax.ShapeDtypeStruct((dc + da, t), BF16)),
        in_specs=[pl.BlockSpec((tb, dc), lambda i: (i, 0)),
                  pl.BlockSpec((tb, dc), lambda i: (i, 1)),
                  pl.BlockSpec((tb, dc), lambda i: (i, 3)),
                  pl.BlockSpec((tb, da), lambda i: (i, 7)),
                  pair_spec,
                  pl.BlockSpec((1, dc), lambda i: (0, 0)),
                  pl.BlockSpec((hp, 1, PAIR), lambda i: (0, 0, 0)), ANY],
        out_specs=(pl.BlockSpec((tb, dc + da), lambda i: (i, 0)), pl.BlockSpec((dc + da, tb), lambda i: (0, i))),
        compiler_params=_params(("parallel",)),
    )(co, proj, proj, proj, o_mix, g_conv, g_attn_pairs, after)


def _out_fwd_bwd(ycat, woutf, x, target, mod, g_post):
    t, d = x.shape
    n = ycat.shape[1]
    tb = ROW_TILE

    def body(a_ref, w_ref, x_ref, tg_ref, mod_ref, g_ref, dout_ref, dy_ref, acc_ref):
        y = jnp.dot(a_ref[...], w_ref[...], preferred_element_type=F32)
        r = lax.rsqrt(jnp.mean(y * y, axis=-1, keepdims=True) + EPS)
        nh = y * r
        gate = mod_ref[2:3, :]
        nrm = nh * g_ref[...]
        err = x_ref[...] + gate * nrm - tg_ref[...]
        dout = err * (1.0 / d)
        dout_ref[...] = dout.astype(BF16)
        dn = dout * gate
        a = dn * g_ref[...]
        dy = r * (a - nh * jnp.mean(a * nh, axis=-1, keepdims=True))
        dy_ref[...] = dy.astype(BF16)
        loss = 0.5 * jnp.sum(jnp.sum(err * err, axis=-1, keepdims=True) * (1.0 / d), axis=0, keepdims=True)
        part = jnp.concatenate(
            [jnp.sum(dout * nrm, axis=0, keepdims=True), jnp.sum(dn * nh, axis=0, keepdims=True),
             jnp.broadcast_to(loss, (1, d)), jnp.zeros((5, d), F32)], axis=0)

        @pl.when(pl.program_id(0) == 0)
        def _():
            acc_ref[...] = jnp.zeros(acc_ref.shape, F32)

        acc_ref[...] += part

    return pl.pallas_call(
        body, name="out_fwd_bwd", grid=(t // tb,),
        out_shape=(jax.ShapeDtypeStruct((t, d), BF16), jax.ShapeDtypeStruct((t, d), BF16),
                   jax.ShapeDtypeStruct((8, d), F32)),
        in_specs=[pl.BlockSpec((tb, n), lambda i: (i, 0)), pl.BlockSpec((n, d), lambda i: (0, 0)),
                  pl.BlockSpec((tb, d), lambda i: (i, 0)), pl.BlockSpec((tb, d), lambda i: (i, 0)),
                  pl.BlockSpec((3, d), lambda i: (0, 0)), pl.BlockSpec((1, d), lambda i: (0, 0))],
        out_specs=(pl.BlockSpec((tb, d), lambda i: (i, 0)), pl.BlockSpec((tb, d), lambda i: (i, 0)),
                   pl.BlockSpec((8, d), lambda i: (0, 0))),
        compiler_params=_params(("arbitrary",)),
    )(ycat, woutf, x, target, mod, g_post)


def _matmul_nt(a, b, out_dtype, name):
    m, k = a.shape
    n = b.shape[0]
    tn = COL_TILE

    def body(a_ref, b_ref, o_ref):
        o_ref[...] = lax.dot_general(a_ref[...], b_ref[...], (((1,), (1,)), ((), ())),
                                     preferred_element_type=F32).astype(out_dtype)

    return pl.pallas_call(
        body, name=name, grid=(n // tn,),
        out_shape=jax.ShapeDtypeStruct((m, n), out_dtype),
        in_specs=[pl.BlockSpec((m, k), lambda i: (0, 0)), pl.BlockSpec((tn, k), lambda i: (i, 0))],
        out_specs=pl.BlockSpec((m, tn), lambda i: (0, i)),
        compiler_params=_params(("parallel",)),
    )(a, b)


def _mix_bwd(dycat, co, proj, o_mix, g_conv, g_attn_pairs):
    t, dc = co.shape
    hp = o_mix.shape[0]
    da = hp * PAIR
    tb = ROW_TILE

    def body(dy_ref, co_ref, bg_ref, zc_ref, za_ref, om_ref, gc_ref, ga_ref,
             dcp_ref, dco_ref, do_ref, dl_ref, dgc_ref, dga_ref):
        first = pl.program_id(0) == 0
        cov = co_ref[...].astype(F32)
        bg = bg_ref[...].astype(F32)
        zc = zc_ref[...].astype(F32)
        p = bg * cov
        rc = lax.rsqrt(jnp.mean(p * p, axis=-1, keepdims=True) + EPS)
        nh = p * rc
        dyc = dy_ref[:, 0:dc].astype(F32)
        dn = dyc * _silu(zc)
        a = dn * gc_ref[...]
        dp = rc * (a - nh * jnp.mean(a * nh, axis=-1, keepdims=True))
        dcp_ref[:, 0:dc] = jnp.zeros((tb, dc), BF16)
        dcp_ref[:, dc:2 * dc] = (dp * cov).astype(BF16)
        dcp_ref[:, 2 * dc:3 * dc] = jnp.zeros((tb, dc), BF16)
        dcp_ref[:, 3 * dc:4 * dc] = (dyc * nh * gc_ref[...] * _silu_grad(zc)).astype(BF16)
        dcp_ref[:, 4 * dc:4 * dc + 3 * da] = jnp.zeros((tb, 3 * da), BF16)
        dco_ref[...] = dp * bg

        @pl.when(first)
        def _():
            dgc_ref[...] = jnp.zeros(dgc_ref.shape, F32)
            dga_ref[...] = jnp.zeros(dga_ref.shape, F32)

        dgc_ref[...] += jnp.sum(dn * nh, axis=0, keepdims=True)

        ssq = jnp.zeros((tb, 1), F32)
        for h in range(hp):
            o = om_ref[h]
            ssq = ssq + jnp.sum(o * o, axis=-1, keepdims=True)
        ra = lax.rsqrt(ssq * (1.0 / da) + EPS)
        dot_an = jnp.zeros((tb, 1), F32)
        for h in range(hp):
            nha = om_ref[h] * ra
            za = za_ref[:, h * PAIR:(h + 1) * PAIR].astype(F32)
            dya = dy_ref[:, dc + h * PAIR:dc + (h + 1) * PAIR].astype(F32)
            dna = dya * _silu(za)
            dza = (dya * nha * ga_ref[h] * _silu_grad(za)).astype(BF16)
            dcp_ref[:, 4 * dc + 3 * da + h * PAIR:4 * dc + 3 * da + (h + 1) * PAIR] = dza
            dga_ref[h] += jnp.sum(dna * nha, axis=0, keepdims=True)
            dot_an = dot_an + jnp.sum(dna * ga_ref[h] * nha, axis=-1, keepdims=True)
        mean_an = dot_an * (1.0 / da)
        first_head = lax.broadcasted_iota(jnp.int32, (tb, PAIR), 1) < HEAD_DIM
        for h in range(hp):
            o = om_ref[h]
            nha = o * ra
            za = za_ref[:, h * PAIR:(h + 1) * PAIR].astype(F32)
            dya = dy_ref[:, dc + h * PAIR:dc + (h + 1) * PAIR].astype(F32)
            aa = dya * _silu(za) * ga_ref[h]
            d_o = ra * (aa - nha * mean_an)
            do_ref[h] = d_o.astype(BF16)
            prod = d_o * o
            both = jnp.sum(prod, axis=-1, keepdims=True)
            head0 = jnp.sum(jnp.where(first_head, prod, 0.0), axis=-1, keepdims=True)
            dl_ref[h] = jnp.where(first_head, head0, both - head0)

    pair_spec = pl.BlockSpec((hp, tb, PAIR), lambda i: (0, i, 0))
    return pl.pallas_call(
        body, name="mix_bwd", grid=(t // tb,),
        out_shape=(jax.ShapeDtypeStruct((t, 4 * dc + 4 * da), BF16), jax.ShapeDtypeStruct((t, dc), F32),
                   jax.ShapeDtypeStruct((hp, t, PAIR), BF16), jax.ShapeDtypeStruct((hp, t, PAIR), F32),
                   jax.ShapeDtypeStruct((1, dc), F32), jax.ShapeDtypeStruct((hp, 1, PAIR), F32)),
        in_specs=[pl.BlockSpec((tb, dc + da), lambda i: (i, 0)),
                  pl.BlockSpec((tb, dc), lambda i: (i, 0)),
                  pl.BlockSpec((tb, dc), lambda i: (i, 1)),
                  pl.BlockSpec((tb, dc), lambda i: (i, 3)),
                  pl.BlockSpec((tb, da), lambda i: (i, 7)),
                  pair_spec,
                  pl.BlockSpec((1, dc), lambda i: (0, 0)),
                  pl.BlockSpec((hp, 1, PAIR), lambda i: (0, 0, 0))],
        out_specs=(pl.BlockSpec((tb, 4 * dc + 4 * da), lambda i: (i, 0)), pl.BlockSpec((tb, dc), lambda i: (i, 0)),
                   pair_spec, pair_spec,
                   pl.BlockSpec((1, dc), lambda i: (0, 0)), pl.BlockSpec((hp, 1, PAIR), lambda i: (0, 0, 0))),
        compiler_params=_params(("arbitrary",)),
    )(dycat, co, proj, proj, proj, o_mix, g_conv, g_attn_pairs)


def _conv_bwd(dconv_proj, dco, conv_proj, conv_w, dc, after):
    t = dco.shape[0]
    ct = CONV_TILE
    nct = dc // ct

    def body(dcp_in_ref, dco_ref, u_ref, cg_ref, w_ref, after_ref, dcp_ref, acc_ref):
        del dcp_in_ref, after_ref
        which = pl.program_id(1)
        g = dco_ref[...]
        u = u_ref[...].astype(F32)
        cg = cg_ref[...].astype(F32)
        g_prev, g_next = _shift_rows(g, t)
        da = w_ref[0:1, :] * g_next + w_ref[1:2, :] * g + w_ref[2:3, :] * g_prev
        dcp_ref[...] = (da * jnp.where(which == 0, cg, u)).astype(BF16)
        a = cg * u
        a_prev, a_next = _shift_rows(a, t)
        acc_ref[...] = jnp.concatenate(
            [jnp.sum(g * a_prev, axis=0, keepdims=True), jnp.sum(g * a, axis=0, keepdims=True),
             jnp.sum(g * a_next, axis=0, keepdims=True), jnp.sum(g, axis=0, keepdims=True),
             jnp.zeros((4, ct), F32)], axis=0)

    return pl.pallas_call(
        body, name="conv_bwd", grid=(nct, 2),
        out_shape=(jax.ShapeDtypeStruct(dconv_proj.shape, BF16), jax.ShapeDtypeStruct((8, dc), F32)),
        in_specs=[HBM,
                  pl.BlockSpec((t, ct), lambda i, s: (0, i)),
                  pl.BlockSpec((t, ct), lambda i, s: (0, i)),
                  pl.BlockSpec((t, ct), lambda i, s: (0, 2 * nct + i)),
                  pl.BlockSpec((3, ct), lambda i, s: (0, i)), ANY],
        out_specs=(pl.BlockSpec((t, ct), lambda i, s: (0, 2 * s * nct + i)),
                   pl.BlockSpec((8, ct), lambda i, s: (0, i))),
        input_output_aliases={0: 0},
        compiler_params=_params(("arbitrary", "arbitrary")),
    )(dconv_proj, dco, conv_proj, conv_proj, conv_w, after)


def _dh(dproj, winf, after):
    t = dproj.shape[0]
    _, d, ws = winf.shape
    tm = tn = COL_TILE
    nt = (((1,), (1,)), ((), ()))

    def body(a_ref, w_ref, after_ref, o_ref):
        del after_ref
        acc = lax.dot_general(a_ref[:, 0:ws], w_ref[0], nt, preferred_element_type=F32)
        for j in range(1, N_CHIPS):
            acc = acc + lax.dot_general(a_ref[:, j * ws:(j + 1) * ws], w_ref[j], nt, preferred_element_type=F32)
        o_ref[...] = acc.astype(BF16)

    return pl.pallas_call(
        body, name="dh", grid=(d // tn, t // tm),
        out_shape=jax.ShapeDtypeStruct((t, d), BF16),
        in_specs=[pl.BlockSpec((tm, N_CHIPS * ws), lambda n, m: (m, 0)),
                  pl.BlockSpec((N_CHIPS, tn, ws), lambda n, m: (0, n, 0)), ANY],
        out_specs=pl.BlockSpec((tm, tn), lambda n, m: (m, n)),
        compiler_params=_params(("parallel", "parallel")),
    )(dproj, winf, after)


def _prenorm_bwd(x, dh, dout, mod, g_pre):
    t, d = x.shape
    tb = ROW_TILE

    def body(x_ref, dh_ref, dout_ref, mod_ref, g_ref, gx_ref, acc_ref):
        xv = x_ref[...]
        dhv = dh_ref[...].astype(F32)
        r = lax.rsqrt(jnp.mean(xv * xv, axis=-1, keepdims=True) + EPS)
        xh = xv * r
        one_scale = 1.0 + mod_ref[1:2, :]
        a = dhv * one_scale * g_ref[...]
        gx_ref[...] = dout_ref[...].astype(F32) + r * (a - xh * jnp.mean(a * xh, axis=-1, keepdims=True))
        part = jnp.concatenate(
            [jnp.sum(dhv, axis=0, keepdims=True), jnp.sum(dhv * xh * g_ref[...], axis=0, keepdims=True),
             jnp.sum(dhv * xh * one_scale, axis=0, keepdims=True), jnp.zeros((5, d), F32)], axis=0)

        @pl.when(pl.program_id(0) == 0)
        def _():
            acc_ref[...] = jnp.zeros(acc_ref.shape, F32)

        acc_ref[...] += part

    return pl.pallas_call(
        body, name="prenorm_bwd", grid=(t // tb,),
        out_shape=(jax.ShapeDtypeStruct((t, d), F32), jax.ShapeDtypeStruct((8, d), F32)),
        in_specs=[pl.BlockSpec((tb, d), lambda i: (i, 0)), pl.BlockSpec((tb, d), lambda i: (i, 0)),
                  pl.BlockSpec((tb, d), lambda i: (i, 0)), pl.BlockSpec((3, d), lambda i: (0, 0)),
                  pl.BlockSpec((1, d), lambda i: (0, 0))],
        out_specs=(pl.BlockSpec((tb, d), lambda i: (i, 0)), pl.BlockSpec((8, d), lambda i: (0, 0))),
        compiler_params=_params(("arbitrary",)),
    )(x, dh, dout, mod, g_pre)


def _chip_sums(mine, rsib, name, part=0, parts=1, after=()):
    _, half, cols = mine.shape
    rows = half // parts
    tr = min(rows, ROW_TILE)
    nt = rows // tr

    def body(g_ref, r_ref, *rest):
        rest[-1][...] = (g_ref[...].astype(F32) + r_ref[...].astype(F32)).astype(BF16)

    spec = pl.BlockSpec((None, tr, cols), lambda j, i: (j, part * nt + i, 0))
    return pl.pallas_call(
        body, name=name, grid=(N_CHIPS, nt),
        out_shape=jax.ShapeDtypeStruct((N_CHIPS, rows, cols), BF16),
        in_specs=[spec, spec] + [ANY] * len(after), out_specs=pl.BlockSpec((None, tr, cols), lambda j, i: (j, i, 0)),
        compiler_params=_params(("parallel", "parallel")),
    )(mine, rsib, *after)


def _owner_sum(place, mine, rsib, rici, name, part=0, parts=1):
    _, half, cols = mine.shape
    rows = half // parts
    tr = min(rows, ROW_TILE)
    nt = rows // tr

    def body(place_ref, g_ref, r_ref, i_ref, o_ref):
        del place_ref
        acc = g_ref[...].astype(F32) + r_ref[...].astype(F32)
        for k in range(N_CHIPS - 1):
            acc = acc + i_ref[k].astype(F32)
        o_ref[...] = acc

    own = pl.BlockSpec((None, tr, cols), lambda i, p: (p[0], part * nt + i, 0))
    grid_spec = pltpu.PrefetchScalarGridSpec(
        num_scalar_prefetch=1, grid=(nt,),
        in_specs=[own, own, pl.BlockSpec((N_CHIPS - 1, tr, cols), lambda i, p: (0, i, 0))],
        out_specs=pl.BlockSpec((tr, cols), lambda i, p: (p[1] * (half // tr) + part * nt + i, 0)))
    return pl.pallas_call(
        body, name=name, grid_spec=grid_spec,
        out_shape=jax.ShapeDtypeStruct((2 * half, cols), F32),
        compiler_params=_params(("parallel",)),
    )(place, mine, rsib, rici)


def _adam_math(w, g, m, v):
    m2 = ADAM_B1 * m + (1.0 - ADAM_B1) * g
    v2 = ADAM_B2 * v + (1.0 - ADAM_B2) * (g * g)
    m_hat = m2 / (1.0 - ADAM_B1 ** ADAM_STEP)
    v_hat = v2 / (1.0 - ADAM_B2 ** ADAM_STEP)
    delta = -ADAM_LR * (m_hat / (jnp.sqrt(v_hat) + ADAM_EPS) + ADAM_WD * w)
    return delta, m2, v2


def _adamw(w, g, m, v, name, part=0, parts=1, prev=None):
    rows, cols = w.shape
    tr = min(rows, ROW_TILE)

    def body(*refs):
        w_ref, g_ref, m_ref, v_ref, go_ref, d_ref, m2_ref, v2_ref = refs[-8:]
        g = g_ref[...]
        go_ref[...] = g
        d_ref[...], m2_ref[...], v2_ref[...] = _adam_math(w_ref[...], g, m_ref[...], v_ref[...])

    if parts == 1:
        grid, spec = (rows // tr,), pl.BlockSpec((tr, cols), lambda i: (i, 0))
    else:
        per_half = rows // 2 // tr
        nt = per_half // parts
        grid, spec = (2, nt), pl.BlockSpec((tr, cols), lambda r, i: (r * per_half + part * nt + i, 0))
    olds = [] if prev is None else list(prev)
    return pl.pallas_call(
        body, name=name, grid=grid,
        out_shape=(jax.ShapeDtypeStruct(w.shape, F32),) * 4,
        in_specs=[HBM] * len(olds) + [spec] * 4, out_specs=(spec,) * 4,
        input_output_aliases={i: i for i in range(len(olds))},
        compiler_params=_params(("parallel",) * len(grid)),
    )(*olds, w, g, m, v)


def _ada_grad_adamw(c_all_t, dmod_cols, w, m, v):
    d, wa = w.shape
    tr = ROW_TILE

    def body(ct_ref, dm_ref, w_ref, m_ref, v_ref, g_ref, d_ref, m2_ref, v2_ref):
        act = _silu(ct_ref[...])
        g = act[:, 0:1] * dm_ref[0:1, :]
        for b in range(1, N_DEV):
            g = g + act[:, b:b + 1] * dm_ref[b:b + 1, :]
        g_ref[...] = g
        d_ref[...], m2_ref[...], v2_ref[...] = _adam_math(w_ref[...], g, m_ref[...], v_ref[...])

    spec = pl.BlockSpec((tr, wa), lambda i: (i, 0))
    return pl.pallas_call(
        body, name="ada_grad_adamw", grid=(d // tr,),
        out_shape=(jax.ShapeDtypeStruct(w.shape, F32),) * 4,
        in_specs=[pl.BlockSpec((tr, N_DEV), lambda i: (i, 0)), pl.BlockSpec((N_DEV, wa), lambda i: (0, 0)),
                  spec, spec, spec],
        out_specs=(spec,) * 4,
        compiler_params=_params(("parallel",)),
    )(c_all_t, dmod_cols, w, m, v)


def _small_update(place, gathered, pieces, weights, moments_m, moments_v):
    n = gathered.shape[1]
    k = len(weights)
    final_shapes = [w.shape for w in weights]
    row_counts = [s[1] if len(s) == 3 else 1 for s in final_shapes]
    weights, moments_m, moments_v = ([a.reshape(1, -1) for a in arrays] for arrays in (weights, moments_m, moments_v))

    def body(place_ref, g_ref, *refs):
        w_refs, m_refs, v_refs = refs[0:k], refs[k:2 * k], refs[2 * k:3 * k]
        outs = refs[3 * k:]
        total = g_ref[0:SUBLANES, :]
        for dev in range(1, N_DEV):
            total = total + g_ref[SUBLANES * dev:SUBLANES * (dev + 1), :]

        def flat(offset, length):
            segments, pos = [], offset
            while pos < offset + length:
                row, col = divmod(pos, n)
                take = min(offset + length - pos, n - col)
                segments.append(total[row:row + 1, col:col + take])
                pos += take
            return jnp.concatenate(segments, axis=1) if len(segments) > 1 else segments[0]

        chip = place_ref[0]
        for i, (w_ref, m_ref, v_ref) in enumerate(zip(w_refs, m_refs, v_refs)):
            g = flat(*pieces[i])
            if pieces[i][1] > w_ref.shape[1]:
                rows = row_counts[i]
                cols, full = w_ref.shape[1] // rows, pieces[i][1] // rows
                picked = []
                for r in range(rows):
                    blocks = [g[:, r * full + q * cols:r * full + (q + 1) * cols] for q in range(N_CHIPS)]
                    mine = blocks[N_CHIPS - 1]
                    for q in range(N_CHIPS - 2, -1, -1):
                        mine = jnp.where(chip == q, blocks[q], mine)
                    picked.append(mine)
                g = jnp.concatenate(picked, axis=1)
            delta, m2, v2 = _adam_math(w_ref[...], g, m_ref[...], v_ref[...])
            for j, val in enumerate((g, delta, m2, v2)):
                outs[j * k + i][...] = val
        outs[4 * k][...] = flat(*pieces[k])

    shapes = [jax.ShapeDtypeStruct(w.shape, F32) for w in weights]
    grid_spec = pltpu.PrefetchScalarGridSpec(
        num_scalar_prefetch=1, grid=(1,),
        in_specs=[pl.BlockSpec(gathered.shape, lambda i, p: (0, 0))]
        + [pl.BlockSpec(a.shape, functools.partial(lambda nd, i, p: (0,) * nd, a.ndim))
           for a in (*weights, *moments_m, *moments_v)],
        out_specs=tuple(pl.BlockSpec(s.shape, functools.partial(lambda nd, i, p: (0,) * nd, len(s.shape)))
                        for s in shapes * 4) + (pl.BlockSpec((1, LANES), lambda i, p: (0, 0)),))
    outs = pl.pallas_call(
        body, name="small_update", grid_spec=grid_spec,
        out_shape=tuple(shapes * 4) + (jax.ShapeDtypeStruct((1, LANES), F32),),
        compiler_params=_params(("arbitrary",)),
    )(place, gathered, *weights, *moments_m, *moments_v)
    shaped = [out.reshape(final_shapes[i % k]) for i, out in enumerate(outs[0:4 * k])]
    return shaped[0:k], shaped[k:2 * k], shaped[2 * k:3 * k], shaped[3 * k:4 * k], outs[4 * k]


def _pack_small(pieces):
    flat = [p.reshape(-1).astype(F32) for p in pieces]
    offsets, total = [], 0
    for p in flat:
        offsets.append(total)
        total += p.shape[0]
    padded = -(-total // SMALL_ALIGN) * SMALL_ALIGN
    if padded > total:
        flat.append(jnp.zeros((padded - total,), F32))
    return jnp.concatenate(flat).reshape(8, padded // 8), offsets


def _alibi_slope_rows(n_heads):
    slopes = 2.0 ** (-8.0 * jnp.arange(1, n_heads + 1, dtype=F32) / n_heads)
    rows = jnp.zeros((n_heads // 2, 8), F32).at[:, 0:2].set(slopes.reshape(n_heads // 2, 2))
    return jnp.broadcast_to(rows[:, :, None], (n_heads // 2, 8, ATT_KW))


def kernel(x, c, w_ada, b_ada, g_pre, w_in, conv_w, conv_b, g_conv, g_attn, w_out, g_post, loss_target, m_w_ada, m_b_ada, m_g_pre, m_w_in, m_conv_w, m_conv_b, m_g_conv, m_g_attn, m_w_out, m_g_post, v_w_ada, v_b_ada, v_g_pre, v_w_in, v_conv_w, v_conv_b, v_g_conv, v_g_attn, v_w_out, v_g_post):
    t, d = x.shape[1], x.shape[2]
    dc = conv_b.shape[1]
    da = g_attn.shape[1]
    hp = da // PAIR
    ws = w_in.shape[2]
    wa = w_ada.shape[2]
    cws = conv_w.shape[2]
    assert t % ROW_TILE == 0 and d % ROW_TILE == 0 and dc % COL_TILE == 0 and da % COL_TILE == 0
    assert ws == 2 * dc and dc == da and t // BRANCHES[-1][1] >= ATT_BQ

    mx, my, mc = _my_place()
    chip = _chip_of(mx, my)
    dev = 2 * chip + mc
    place = jnp.stack([chip, mc]).astype(jnp.int32)

    x2, tgt2 = x[0], loss_target[0]
    w_ada2, w_in2, w_out2 = w_ada[0], w_in[0], w_out[0]

    packed, offs = _pack_small([c[0], conv_w[0]])
    seen, mod, win_slots = _ada_modulation(packed, w_ada2, b_ada, d, w_in2)
    seen = seen.reshape(N_DEV, -1)
    c_all = seen[:, offs[0]:offs[0] + d]
    conv_w_full = seen[0::2, offs[1]:offs[1] + 3 * cws].reshape(N_CHIPS, 3, cws).transpose(1, 0, 2).reshape(3, dc)

    win_flight, send_in, recv_in, started = _gather_start(win_slots, mod)

    y_chip, x_chip, d_chip = (_chip_of(mx, 1 - my), _chip_of(1 - mx, my), _chip_of(1 - mx, 1 - my))
    tiles_per_part = ws // COL_TILE // 2

    def tiles_of(chunk, parts):
        return [(2 * chunk + part) * tiles_per_part + k for part in parts for k in range(tiles_per_part)]

    own_tiles, first_tiles, second_tiles, far_tiles = (jnp.stack(tiles).astype(jnp.int32) for tiles in (
        tiles_of(chip, (0, 1)), tiles_of(y_chip, (0,)) + tiles_of(x_chip, (1,)),
        tiles_of(y_chip, (1,)) + tiles_of(x_chip, (0,)), tiles_of(d_chip, (0, 1))))
    h, ht = _prenorm(x2, mod, g_pre, started)
    proj = _proj_tiles(None, h, w_in2, own_tiles, "proj_own")
    win_flight, wout_flight, relay_send_in, relay_recv_in, send_out, recv_out = _gather_relay_in(
        win_flight, _cast_into_slot(place, w_out2, "cast_w_out"), recv_in, proj)
    win_flight = _forward_halves(win_flight, ((0, 0), (1, 1)), "forward_w_in_first")
    proj = _proj_tiles(proj, h, win_flight, first_tiles, "proj_first_parts")
    win_flight = _forward_halves(
        _gather_wait_direct(win_flight, send_in, recv_in, proj, "gather_wait_w_in_direct"),
        ((0, 1), (1, 0)), "forward_w_in_second")
    proj = _proj_tiles(proj, h, win_flight, second_tiles, "proj_second_parts")
    winf = _forward_halves(
        _gather_wait_relayed(win_flight, relay_send_in, relay_recv_in, proj, "gather_wait_w_in_relayed"),
        ((2, None),), "forward_w_in_relayed")
    proj = _proj_tiles(proj, h, winf, far_tiles, "proj_diagonal")
    slopes = _alibi_slope_rows(da // HEAD_DIM)
    co = _conv_fwd(proj, conv_w_full, conv_b, dc)
    wout_flight, relay_send_out, relay_recv_out = _gather_relay_out(wout_flight, recv_out, co)
    o_mix, lse = _attn_fwd(proj, slopes, dc, da)
    g_attn_pairs = g_attn.reshape(hp, 1, PAIR)
    wout_flight = _gather_wait_direct(wout_flight, send_out, recv_out, o_mix, "gather_wait_w_out_direct")
    wout_flight = _gather_wait_relayed(wout_flight, relay_send_out, relay_recv_out, o_mix, "gather_wait_w_out_relayed")
    all_halves = ((0, None), (1, None), (2, None))
    wout_flight, fsend_out, frecv_out, forwarding = _forward_start(wout_flight, all_halves, "forward_w_out_start")
    ycat, ycat_t = _mix_fwd(co, proj, o_mix, g_conv, g_attn_pairs, forwarding)
    woutf = _forward_wait(wout_flight, all_halves, fsend_out, frecv_out, ycat, "forward_w_out_wait").reshape(dc + da, d)
    dout, dy, post_sums = _out_fwd_bwd(ycat, woutf, x2, tgt2, mod, g_post)

    gout, rsib_out = _dw_swapped(ycat_t, dy, N_CHIPS, 1, "dw_out")
    csum_out = _chip_sums(gout, rsib_out, "rs_chip_sum_out")
    ssem_out, rsem_out, csum_out, land_out, sent_out = _owners_start(csum_out, "rs_owners_start_out")
    dycat = _matmul_nt(dy, woutf, BF16, "dycat")
    dproj, dco, d_o, delta, dg_conv, dg_attn = _mix_bwd(dycat, co, proj, o_mix, g_conv, g_attn_pairs)
    dproj, conv_sums = _conv_bwd(dproj, dco, proj, conv_w_full, dc, sent_out)
    dproj = _attn_bwd(dproj, proj, d_o, lse, delta, slopes, dc, da, sent_out)
    gin, rsib_in = _dw_swapped(ht, dproj, 1, N_CHIPS, "dw_in")
    ssem_in0, rsem_in0, csum_in0, land_in0, sent_in0 = _owners_start(
        _chip_sums(gin, rsib_in, "rs_chip_sum_in0", 0, 2), "rs_owners_start_in0")
    ssem_in1, rsem_in1, csum_in1, land_in1, sent_in = _owners_start(
        _chip_sums(gin, rsib_in, "rs_chip_sum_in1", 1, 2, after=(sent_in0,)), "rs_owners_start_in1")
    dh = _dh(dproj, winf, sent_in)
    grad_x, pre_sums = _prenorm_bwd(x2, dh, dout, mod, g_pre)

    small, so = _pack_small([
        pre_sums[0], pre_sums[1], post_sums[0],
        pre_sums[2], conv_sums[0:3], conv_sums[3], dg_conv, dg_attn, post_sums[1], post_sums[2, 0:128]])
    ssem_small, rsem_small, small, land_small, sent_small = _allgather8_start(small, dev, "gather_small_start")

    rici_out = _owners_wait(ssem_out, rsem_out, csum_out, land_out, [grad_x, sent_small], "rs_owners_wait_out")
    full_out, jsend_out, jrecv_out, joining_out = _join_start(
        _owner_sum(place, gout, rsib_out, rici_out, "rs_owner_sum_out"), "rs_join_start_out", 0, 1)
    rici_in = _owners_wait(ssem_in0, rsem_in0, csum_in0, land_in0, [joining_out], "rs_owners_wait_in0")
    full_in0, jsend0, jrecv0, joining0 = _join_start(
        _owner_sum(place, gin, rsib_in, rici_in, "rs_owner_sum_in0", 0, 2), "rs_join_start_in0", 0, 2)
    grad_w_out = _join_wait(full_out, jsend_out, jrecv_out, [joining0], "rs_join_wait_out", 0, 1)
    grad_w_out, delta_w_out, new_m_w_out, new_v_w_out = _adamw(
        w_out2, grad_w_out, m_w_out[0], v_w_out[0], "adamw_w_out")
    full_in0 = _join_wait(full_in0, jsend0, jrecv0, [delta_w_out], "rs_join_wait_in0", 0, 2)
    updated_in = _adamw(w_in2, full_in0, m_w_in[0], v_w_in[0], "adamw_w_in0", 0, 2)
    rici_in = _owners_wait(ssem_in1, rsem_in1, csum_in1, land_in1, [updated_in[1]], "rs_owners_wait_in1")
    full_in1, jsend1, jrecv1, joining1 = _join_start(
        _owner_sum(place, gin, rsib_in, rici_in, "rs_owner_sum_in1", 1, 2), "rs_join_start_in1", 1, 2)

    small_seen = _allgather8_wait(ssem_small, rsem_small, small, land_small, [joining1], "gather_small_wait")
    small_w = [b_ada, g_pre, conv_w, conv_b, g_conv, g_attn, g_post]
    small_m = [m_b_ada, m_g_pre, m_conv_w, m_conv_b, m_g_conv, m_g_attn, m_g_post]
    small_v = [v_b_ada, v_g_pre, v_conv_w, v_conv_b, v_g_conv, v_g_attn, v_g_post]
    pieces = [(0, 3 * d), (so[3], d), (so[4], 3 * dc), (so[5], dc), (so[6], dc), (so[7], da), (so[8], d), (so[9], LANES)]
    g_small, d_small, m_small, v_small, loss_row = _small_update(place, small_seen, pieces, small_w, small_m, small_v)
    loss = loss_row[0, 0]
    grad_b_ada, grad_g_pre, grad_conv_w, grad_conv_b, grad_g_conv, grad_g_attn, grad_g_post = g_small
    dmod_cols = lax.dynamic_slice_in_dim(small_seen.reshape(N_DEV, -1), chip * wa, wa, axis=1)
    grad_w_ada, delta_w_ada, new_m_w_ada, new_v_w_ada = _ada_grad_adamw(c_all.T, dmod_cols, w_ada2, m_w_ada[0], v_w_ada[0])

    full_in1 = _join_wait(full_in1, jsend1, jrecv1, [delta_w_ada, d_small[0]], "rs_join_wait_in1", 1, 2)
    grad_w_in, delta_w_in, new_m_w_in, new_v_w_in = _adamw(
        w_in2, full_in1, m_w_in[0], v_w_in[0], "adamw_w_in1", 1, 2, updated_in)

    def lead(a):
        return a.reshape((1,) + a.shape)

    grads = [lead(grad_w_ada), grad_b_ada, grad_g_pre, lead(grad_w_in), grad_conv_w, grad_conv_b, grad_g_conv,
             grad_g_attn, lead(grad_w_out), grad_g_post]
    deltas = [lead(delta_w_ada), d_small[0], d_small[1], lead(delta_w_in), d_small[2], d_small[3], d_small[4],
              d_small[5], lead(delta_w_out), d_small[6]]
    new_ms = [lead(new_m_w_ada), m_small[0], m_small[1], lead(new_m_w_in), m_small[2], m_small[3], m_small[4],
              m_small[5], lead(new_m_w_out), m_small[6]]
    new_vs = [lead(new_v_w_ada), v_small[0], v_small[1], lead(new_v_w_in), v_small[2], v_small[3], v_small[4],
              v_small[5], lead(new_v_w_out), v_small[6]]
    return (loss, lead(grad_x), *grads, *deltas, *new_ms, *new_vs)
```

```python
import functools

import jax
import jax.numpy as jnp
from jax import lax
from jax.experimental import pallas as pl
from jax.experimental.pallas import tpu as pltpu

F32 = jnp.float32
BF16 = jnp.bfloat16
MESH = pl.DeviceIdType.MESH
HBM = pl.BlockSpec(memory_space=pltpu.HBM)
VMEM = pl.BlockSpec(memory_space=pltpu.VMEM)
ANY = pl.BlockSpec(memory_space=pl.ANY)
SEM = pl.BlockSpec(memory_space=pltpu.SEMAPHORE)
EFFECT = pltpu.SideEffectType.DATAFLOW_SIDE_EFFECTING
SUBLANES, LANES = 8, 128
TOKEN = jax.ShapeDtypeStruct((SUBLANES, LANES), jnp.float32)

HEAD_DIM = 64
PAIR = 2 * HEAD_DIM
assert PAIR == LANES
BRANCHES = ((128, 1), (512, 4), (2048, 16))
SIDE = 64
EPS = 1e-6
NEG_INF = -1e30
N_CHIPS = 4
N_DEV = 8

ADAM_LR = 0.001
ADAM_B1 = 0.9
ADAM_B2 = 0.999
ADAM_EPS = 1e-08
ADAM_WD = 0.01
ADAM_STEP = 10

VMEM_LIMIT_BYTES = 56 * 1024 * 1024
ROW_TILE = 256
COL_TILE = 512
CONV_TILE = 256
ATT_BQ = 128
ATT_KW = ATT_BQ + 2 * SIDE
ATT_UNROLL = 4
SMALL_ALIGN = SUBLANES * LANES


def _params(semantics=None):
    kw = {"vmem_limit_bytes": VMEM_LIMIT_BYTES}
    if semantics is not None:
        kw["dimension_semantics"] = semantics
    return pltpu.CompilerParams(**kw)


def _silu(z):
    return z * jax.nn.sigmoid(z)


def _silu_grad(z):
    s = jax.nn.sigmoid(z)
    return s * (1.0 + z * (1.0 - s))


def _my_place():
    return lax.axis_index("x"), lax.axis_index("y"), lax.axis_index("c")


def _flip(a, bit):
    return 1 - a if bit else a


def _chip_of(x, y):
    return 2 * x + y


def _allgather8_start(v, me, name):
    rows_per, n = v.shape
    land = lax.dynamic_update_slice(jnp.zeros((N_DEV * rows_per, n), v.dtype), v, (me * rows_per, 0))

    def body(v_ref, land_ref, send_sems, recv_sems, v_thru, land_thru, token_ref):
        del v_thru, land_thru
        x, y, c = _my_place()
        mine = land_ref.at[pl.ds(pl.multiple_of((4 * x + 2 * y + c) * rows_per, rows_per), rows_per), :]
        for k in range(1, N_DEV):
            peer = (_flip(x, k & 4), _flip(y, k & 2), _flip(c, k & 1))
            pltpu.make_async_remote_copy(
                src_ref=v_ref, dst_ref=mine, send_sem=send_sems.at[k - 1], recv_sem=recv_sems.at[k - 1],
                device_id=peer, device_id_type=MESH).start()
        token_ref[...] = jnp.zeros(token_ref.shape, F32)

    sems = pltpu.SemaphoreType.DMA((N_DEV - 1,))
    return pl.pallas_call(
        body, name=name,
        out_shape=(sems, sems, jax.ShapeDtypeStruct(v.shape, v.dtype), jax.ShapeDtypeStruct(land.shape, land.dtype), TOKEN),
        in_specs=[HBM, HBM], out_specs=(SEM, SEM, HBM, HBM, VMEM),
        input_output_aliases={0: 2, 1: 3},
        compiler_params=pltpu.CompilerParams(has_side_effects=EFFECT),
    )(pltpu.with_memory_space_constraint(v, pltpu.HBM), pltpu.with_memory_space_constraint(land, pltpu.HBM))


def _allgather8_wait(send_sems, recv_sems, v, land, after, name):
    rows_per = v.shape[0]

    def body(v_ref, land_ref, send_ref, recv_ref, *rest):
        del rest
        x, y, c = _my_place()
        for k in range(1, N_DEV):
            peer = (_flip(x, k & 4), _flip(y, k & 2), _flip(c, k & 1))
            src = 4 * peer[0] + 2 * peer[1] + peer[2]
            cp = pltpu.make_async_remote_copy(
                src_ref=v_ref, dst_ref=land_ref.at[pl.ds(pl.multiple_of(src * rows_per, rows_per), rows_per), :],
                send_sem=send_ref.at[k - 1], recv_sem=recv_ref.at[k - 1], device_id=peer, device_id_type=MESH)
            cp.wait_send()
            cp.wait_recv()

    return pl.pallas_call(
        body, name=name,
        out_shape=(jax.ShapeDtypeStruct(v.shape, v.dtype), jax.ShapeDtypeStruct(land.shape, land.dtype)),
        in_specs=[HBM, HBM, SEM, SEM] + [ANY] * len(after), out_specs=(HBM, HBM),
        input_output_aliases={0: 0, 1: 1},
        compiler_params=pltpu.CompilerParams(has_side_effects=EFFECT),
    )(v, land, send_sems, recv_sems, *after)[1]


def _half_rows(ref, chip, which, half):
    return ref.at[chip, pl.ds(pl.multiple_of(which * half, half), half), :]


def _ici_peers(x, y, c):
    peers = [(_flip(x, k & 2), _flip(y, k & 1), c) for k in (1, 2, 3)]
    return [(peer, _chip_of(peer[0], peer[1])) for peer in peers]


def _part_of_half(ref, chip, core, part):
    half, cols = ref.shape[1] // 2, ref.shape[2] // 2
    return ref.at[chip, pl.ds(pl.multiple_of(core * half, half), half), pl.ds(part * cols, cols)]


def _neighbours(x, y, c):
    return [((x, 1 - y, c), _chip_of(x, 1 - y)), ((1 - x, y, c), _chip_of(1 - x, y)),
            ((1 - x, 1 - y, c), _chip_of(1 - x, 1 - y))]


def _start_direct(buf, send_sems, recv_sems):
    x, y, c = _my_place()
    me = _chip_of(x, y)
    for n, (peer, _) in enumerate(_neighbours(x, y, c)[0:2]):
        for part in ((0, 1), (1, 0))[n]:
            piece = _part_of_half(buf, me, c, part)
            pltpu.make_async_remote_copy(
                src_ref=piece, dst_ref=piece, send_sem=send_sems.at[2 * n + part], recv_sem=recv_sems.at[2 * n + part],
                device_id=peer, device_id_type=MESH).start()


def _relay(buf, recv_sems, relay_send, relay_recv):
    x, y, c = _my_place()
    nbrs = _neighbours(x, y, c)
    for n in range(2):
        part = n
        piece = _part_of_half(buf, nbrs[n][1], c, part)
        pltpu.make_async_remote_copy(
            src_ref=piece, dst_ref=piece, send_sem=relay_send.at[part], recv_sem=recv_sems.at[2 * n + part],
            device_id=nbrs[n][0], device_id_type=MESH).wait_recv()
        pltpu.make_async_remote_copy(
            src_ref=piece, dst_ref=piece, send_sem=relay_send.at[part], recv_sem=relay_recv.at[part],
            device_id=nbrs[1 - n][0], device_id_type=MESH).start()


def _gather_start(win_slots, after):
    def body(win_in, after_ref, win_ref, send_sems, recv_sems, token_ref):
        del win_in, after_ref
        _start_direct(win_ref, send_sems, recv_sems)
        token_ref[...] = jnp.zeros(token_ref.shape, F32)

    sems = pltpu.SemaphoreType.DMA((4,))
    return pl.pallas_call(
        body, name="gather_start",
        out_shape=(jax.ShapeDtypeStruct(win_slots.shape, win_slots.dtype), sems, sems, TOKEN),
        in_specs=[HBM, ANY], out_specs=(HBM, SEM, SEM, VMEM),
        input_output_aliases={0: 0},
        compiler_params=pltpu.CompilerParams(has_side_effects=EFFECT),
    )(win_slots, after)


def _gather_relay_in(win, wout_slots, recv_in, after):
    def body(win_in, wout_in, recv_in_ref, after_ref, win_ref, wout_ref, relay_send, relay_recv, send_out, recv_out):
        del win_in, wout_in, after_ref
        _relay(win_ref, recv_in_ref, relay_send, relay_recv)
        _start_direct(wout_ref, send_out, recv_out)

    two, four = pltpu.SemaphoreType.DMA((2,)), pltpu.SemaphoreType.DMA((4,))
    return pl.pallas_call(
        body, name="gather_relay_w_in",
        out_shape=(jax.ShapeDtypeStruct(win.shape, win.dtype), jax.ShapeDtypeStruct(wout_slots.shape, wout_slots.dtype),
                   two, two, four, four),
        in_specs=[HBM, HBM, SEM, ANY], out_specs=(HBM, HBM, SEM, SEM, SEM, SEM),
        input_output_aliases={0: 0, 1: 1},
        compiler_params=pltpu.CompilerParams(has_side_effects=EFFECT),
    )(win, wout_slots, recv_in, after)


def _gather_relay_out(wout, recv_out, after):
    def body(wout_in, recv_out_ref, after_ref, wout_ref, relay_send, relay_recv):
        del wout_in, after_ref
        _relay(wout_ref, recv_out_ref, relay_send, relay_recv)

    two = pltpu.SemaphoreType.DMA((2,))
    return pl.pallas_call(
        body, name="gather_relay_w_out",
        out_shape=(jax.ShapeDtypeStruct(wout.shape, wout.dtype), two, two),
        in_specs=[HBM, SEM, ANY], out_specs=(HBM, SEM, SEM),
        input_output_aliases={0: 0},
        compiler_params=pltpu.CompilerParams(has_side_effects=EFFECT),
    )(wout, recv_out, after)


def _gather_wait_direct(buf, send_sems, recv_sems, after, name):
    def body(buf_in, send_ref, recv_ref, after_ref, buf_ref):
        del buf_in, after_ref
        x, y, c = _my_place()
        me = _chip_of(x, y)
        for n, (peer, chip) in enumerate(_neighbours(x, y, c)[0:2]):
            second = 1 - n
            pltpu.make_async_remote_copy(
                src_ref=_part_of_half(buf_ref, me, c, second), dst_ref=_part_of_half(buf_ref, chip, c, second),
                send_sem=send_ref.at[2 * n + second], recv_sem=recv_ref.at[2 * n + second],
                device_id=peer, device_id_type=MESH).wait_recv()
            for part in range(2):
                piece = _part_of_half(buf_ref, me, c, part)
                pltpu.make_async_remote_copy(
                    src_ref=piece, dst_ref=piece, send_sem=send_ref.at[2 * n + part], recv_sem=recv_ref.at[2 * n + part],
                    device_id=peer, device_id_type=MESH).wait_send()

    return pl.pallas_call(
        body, name=name,
        out_shape=jax.ShapeDtypeStruct(buf.shape, buf.dtype),
        in_specs=[HBM, SEM, SEM, ANY], out_specs=HBM,
        input_output_aliases={0: 0},
        compiler_params=pltpu.CompilerParams(has_side_effects=EFFECT),
    )(buf, send_sems, recv_sems, after)


def _gather_wait_relayed(buf, relay_send, relay_recv, after, name):
    def body(buf_in, rsend_ref, rrecv_ref, after_ref, buf_ref):
        del buf_in, after_ref
        x, y, c = _my_place()
        nbrs = _neighbours(x, y, c)
        for n in range(2):
            relayed = _part_of_half(buf_ref, nbrs[n][1], c, n)
            cp = pltpu.make_async_remote_copy(
                src_ref=relayed, dst_ref=_part_of_half(buf_ref, nbrs[2][1], c, n),
                send_sem=rsend_ref.at[n], recv_sem=rrecv_ref.at[n], device_id=nbrs[1 - n][0], device_id_type=MESH)
            cp.wait_recv()
            cp.wait_send()

    return pl.pallas_call(
        body, name=name,
        out_shape=jax.ShapeDtypeStruct(buf.shape, buf.dtype),
        in_specs=[HBM, SEM, SEM, ANY], out_specs=HBM,
        input_output_aliases={0: 0},
        compiler_params=pltpu.CompilerParams(has_side_effects=EFFECT),
    )(buf, relay_send, relay_recv, after)


def _forward_copies(buf_ref, which, send_sems, recv_sems):
    half = buf_ref.shape[1] // 2
    x, y, c = _my_place()

    def copy(k, chip, core, part):
        piece = _half_rows(buf_ref, chip, core, half) if part is None else _part_of_half(buf_ref, chip, core, part)
        return pltpu.make_async_remote_copy(
            src_ref=piece, dst_ref=piece, send_sem=send_sems.at[k], recv_sem=recv_sems.at[k],
            device_id=(x, y, 1 - c), device_id_type=MESH)

    chips = [_neighbours(x, y, c)[n][1] for n, _ in which]
    return [(copy(k, chip, c, part), copy(k, chip, 1 - c, part)) for k, (chip, (_, part)) in enumerate(zip(chips, which))]


def _forward_halves(buf, which, name):
    def body(buf_in, buf_ref, send_sems, recv_sems):
        del buf_in
        copies = _forward_copies(buf_ref, which, send_sems, recv_sems)
        for mine, _ in copies:
            mine.start()
        for mine, theirs in copies:
            theirs.wait_recv()
        for mine, _ in copies:
            mine.wait_send()

    return pl.pallas_call(
        body, name=name,
        out_shape=jax.ShapeDtypeStruct(buf.shape, buf.dtype),
        in_specs=[HBM], out_specs=HBM,
        input_output_aliases={0: 0},
        scratch_shapes=[pltpu.SemaphoreType.DMA((len(which),))] * 2,
    )(buf)


def _forward_start(buf, which, name):
    def body(buf_in, buf_ref, send_sems, recv_sems, token_ref):
        del buf_in
        for mine, _ in _forward_copies(buf_ref, which, send_sems, recv_sems):
            mine.start()
        token_ref[...] = jnp.zeros(token_ref.shape, F32)

    sems = pltpu.SemaphoreType.DMA((len(which),))
    return pl.pallas_call(
        body, name=name,
        out_shape=(jax.ShapeDtypeStruct(buf.shape, buf.dtype), sems, sems, TOKEN),
        in_specs=[HBM], out_specs=(HBM, SEM, SEM, VMEM),
        input_output_aliases={0: 0},
        compiler_params=pltpu.CompilerParams(has_side_effects=EFFECT),
    )(buf)


def _forward_wait(buf, which, send_sems, recv_sems, after, name):
    def body(buf_in, send_ref, recv_ref, after_ref, buf_ref):
        del buf_in, after_ref
        for mine, theirs in _forward_copies(buf_ref, which, send_ref, recv_ref):
            theirs.wait_recv()
            mine.wait_send()

    return pl.pallas_call(
        body, name=name,
        out_shape=jax.ShapeDtypeStruct(buf.shape, buf.dtype),
        in_specs=[HBM, SEM, SEM, ANY], out_specs=HBM,
        input_output_aliases={0: 0},
        compiler_params=pltpu.CompilerParams(has_side_effects=EFFECT),
    )(buf, send_sems, recv_sems, after)


def _dw_swapped(a, b, row_chunks, col_chunks, name):
    r, t = a.shape
    c_all = b.shape[1]
    chunks = row_chunks * col_chunks
    rq, cq = r // row_chunks, c_all // col_chunks
    half = rq // 2
    tn = COL_TILE
    nt = cq // tn
    steps = col_chunks * nt

    def body(a_ref, b_ref, mine_ref, sib_ref, stage, send_sems, recv_sems):
        x, y, c = _my_place()
        j, n = pl.program_id(0), pl.program_id(1)
        step = j * nt + n
        slot = step % 2
        res = jnp.dot(a_ref[...], b_ref[...], preferred_element_type=F32).astype(BF16)

        def landing(jj, nn):
            cols = pl.ds(pl.multiple_of(nn * tn, tn), tn)
            return sib_ref.at[:, :, cols] if col_chunks == 1 else sib_ref.at[pl.ds(jj, 1), :, cols]

        def copy(slot_, step_, jj, nn):
            return pltpu.make_async_remote_copy(
                src_ref=stage.at[slot_], dst_ref=landing(jj, nn), send_sem=send_sems.at[slot_],
                recv_sem=recv_sems.at[step_], device_id=(x, y, 1 - c), device_id_type=MESH)

        @pl.when(step >= 2)
        def _():
            copy(slot, step, j, n).wait_send()

        for q in range(row_chunks):
            lo = res[q * rq:q * rq + half, :]
            hi = res[q * rq + half:(q + 1) * rq, :]
            mine_ref[q] = jnp.where(c == 0, lo, hi)
            stage[slot, q] = jnp.where(c == 0, hi, lo)
        copy(slot, step, j, n).start()

        @pl.when(step == steps - 1)
        def _():
            for s in range(max(steps - 2, 0), steps):
                copy(s % 2, s, j, n).wait_send()
            for s in range(steps):
                copy(s % 2, s, j, n).wait_recv()

    shape = jax.ShapeDtypeStruct((chunks, half, cq), BF16)
    return pl.pallas_call(
        body, name=name, grid=(col_chunks, nt),
        out_shape=(shape, shape),
        in_specs=[pl.BlockSpec((r, t), lambda j, n: (0, 0)), pl.BlockSpec((t, tn), lambda j, n: (0, j * nt + n))],
        out_specs=(pl.BlockSpec((row_chunks, half, tn), lambda j, n: (j, 0, n)), ANY),
        scratch_shapes=[pltpu.VMEM((2, row_chunks, half, tn), BF16), pltpu.SemaphoreType.DMA((2,)),
                        pltpu.SemaphoreType.DMA((steps,))],
        compiler_params=_params(("arbitrary", "arbitrary")),
    )(a, b)


def _owners_start(csum, name, after=()):
    land = pltpu.with_memory_space_constraint(lax.empty((N_CHIPS - 1,) + csum.shape[1:], csum.dtype), pltpu.HBM)

    def body(csum_ref, land_ref, *rest):
        send_sems, recv_sems, _, _, token_ref = rest[len(after):]
        x, y, c = _my_place()
        for k, (peer, owner) in enumerate(_ici_peers(x, y, c)):
            pltpu.make_async_remote_copy(
                src_ref=csum_ref.at[owner], dst_ref=land_ref.at[k], send_sem=send_sems.at[k], recv_sem=recv_sems.at[k],
                device_id=peer, device_id_type=MESH).start()
        token_ref[...] = jnp.zeros(token_ref.shape, F32)

    sems = pltpu.SemaphoreType.DMA((N_CHIPS - 1,))
    return pl.pallas_call(
        body, name=name,
        out_shape=(sems, sems, jax.ShapeDtypeStruct(csum.shape, csum.dtype),
                   jax.ShapeDtypeStruct(land.shape, land.dtype), TOKEN),
        in_specs=[HBM, HBM] + [ANY] * len(after), out_specs=(SEM, SEM, HBM, HBM, VMEM),
        input_output_aliases={0: 2, 1: 3},
        compiler_params=pltpu.CompilerParams(has_side_effects=EFFECT),
    )(pltpu.with_memory_space_constraint(csum, pltpu.HBM), land, *after)


def _owners_wait(send_sems, recv_sems, csum, land, after, name):
    def body(csum_ref, land_ref, send_ref, recv_ref, *rest):
        del rest
        x, y, c = _my_place()
        for k, (peer, owner) in enumerate(_ici_peers(x, y, c)):
            cp = pltpu.make_async_remote_copy(
                src_ref=csum_ref.at[owner], dst_ref=land_ref.at[k], send_sem=send_ref.at[k], recv_sem=recv_ref.at[k],
                device_id=peer, device_id_type=MESH)
            cp.wait_send()
            cp.wait_recv()

    return pl.pallas_call(
        body, name=name,
        out_shape=(jax.ShapeDtypeStruct(csum.shape, csum.dtype), jax.ShapeDtypeStruct(land.shape, land.dtype)),
        in_specs=[HBM, HBM, SEM, SEM] + [ANY] * len(after), out_specs=(HBM, HBM),
        input_output_aliases={0: 0, 1: 1},
        compiler_params=pltpu.CompilerParams(has_side_effects=EFFECT),
    )(csum, land, send_sems, recv_sems, *after)[1]


def _join_start(full, name, part, parts):
    half = full.shape[0] // 2
    rows = half // parts

    def body(full_in, full_ref, send_sem, recv_sem, token_ref):
        del full_in
        x, y, c = _my_place()
        mine = full_ref.at[pl.ds(pl.multiple_of(c * half + part * rows, rows), rows), :]
        pltpu.make_async_remote_copy(
            src_ref=mine, dst_ref=mine, send_sem=send_sem.at[0], recv_sem=recv_sem.at[0],
            device_id=(x, y, 1 - c), device_id_type=MESH).start()
        token_ref[...] = jnp.zeros(token_ref.shape, F32)

    one = pltpu.SemaphoreType.DMA((1,))
    return pl.pallas_call(
        body, name=name,
        out_shape=(jax.ShapeDtypeStruct(full.shape, full.dtype), one, one, TOKEN),
        in_specs=[HBM], out_specs=(HBM, SEM, SEM, VMEM),
        input_output_aliases={0: 0},
        compiler_params=pltpu.CompilerParams(has_side_effects=EFFECT),
    )(full)


def _join_wait(full, send_sem, recv_sem, after, name, part, parts):
    half = full.shape[0] // 2
    rows = half // parts

    def body(full_in, send_ref, recv_ref, *rest):
        del full_in
        full_ref = rest[-1]
        x, y, c = _my_place()
        cp = pltpu.make_async_remote_copy(
            src_ref=full_ref.at[pl.ds(pl.multiple_of(c * half + part * rows, rows), rows), :],
            dst_ref=full_ref.at[pl.ds(pl.multiple_of((1 - c) * half + part * rows, rows), rows), :],
            send_sem=send_ref.at[0], recv_sem=recv_ref.at[0], device_id=(x, y, 1 - c), device_id_type=MESH)
        cp.wait_send()
        cp.wait_recv()

    return pl.pallas_call(
        body, name=name,
        out_shape=jax.ShapeDtypeStruct(full.shape, full.dtype),
        in_specs=[HBM, SEM, SEM] + [ANY] * len(after), out_specs=HBM,
        input_output_aliases={0: 0},
        compiler_params=pltpu.CompilerParams(has_side_effects=EFFECT),
    )(full, send_sem, recv_sem, *after)


def _cast_into_slot(place, w, name, after):
    rows, cols = w.shape
    tr = min(rows, ROW_TILE)

    def body(place_ref, w_ref, after_ref, o_ref):
        del place_ref, after_ref
        o_ref[...] = w_ref[...].astype(BF16)

    grid_spec = pltpu.PrefetchScalarGridSpec(
        num_scalar_prefetch=1, grid=(rows // tr,),
        in_specs=[pl.BlockSpec((tr, cols), lambda i, p: (i, 0)), ANY],
        out_specs=pl.BlockSpec((None, tr, cols), lambda i, p: (p[0], i, 0)))
    return pl.pallas_call(
        body, name=name, grid_spec=grid_spec,
        out_shape=jax.ShapeDtypeStruct((N_CHIPS, rows, cols), BF16),
        compiler_params=_params(("parallel",)),
    )(place, w, after)


def _ada_modulation(packed, w_ada, b_ada, d, w_big):
    rows_per, n = packed.shape
    d_model, wa = w_ada.shape
    big_rows, big_cols = w_big.shape
    n_chunks = big_rows // ROW_TILE
    first_chunks = (2 * n_chunks) // 3

    def body(v_ref, w_hbm, b_ref, big_hbm, all_ref, mod_ref, slots_hbm, w_vmem, part_ref, parts_ref, wide, narrow,
             load_sem, send1, recv1, send2, recv2, in_sems, out_sems):
        x, y, c = _my_place()
        me = 4 * x + 2 * y + c
        chip = _chip_of(x, y)
        load = pltpu.make_async_copy(w_hbm, w_vmem, load_sem)
        load.start()

        def chunk_in(i):
            return pltpu.make_async_copy(big_hbm.at[i * ROW_TILE:(i + 1) * ROW_TILE, :], wide.at[i % 2], in_sems.at[i % 2])

        def chunk_out(i):
            return pltpu.make_async_copy(
                narrow.at[i % 2], slots_hbm.at[chip, i * ROW_TILE:(i + 1) * ROW_TILE, :], out_sems.at[i % 2])

        def cast_chunk(i):
            if i + 1 < n_chunks:
                chunk_in(i + 1).start()
            chunk_in(i).wait()
            if i >= 2:
                chunk_out(i - 2).wait()
            narrow[i % 2] = wide[i % 2].astype(BF16)
            chunk_out(i).start()

        chunk_in(0).start()

        def rows(idx):
            return all_ref.at[pl.ds(pl.multiple_of(idx * rows_per, rows_per), rows_per), :]

        all_ref[pl.ds(pl.multiple_of(me * rows_per, rows_per), rows_per), :] = v_ref[...]
        copies = []
        for k in range(1, N_DEV):
            peer = (_flip(x, k & 4), _flip(y, k & 2), _flip(c, k & 1))
            cp = pltpu.make_async_remote_copy(
                src_ref=v_ref, dst_ref=rows(me), send_sem=send1.at[k - 1], recv_sem=recv1.at[k - 1],
                device_id=peer, device_id_type=MESH)
            cp.start()
            copies.append((cp, peer))
        for i in range(first_chunks):
            cast_chunk(i)
        for k, (cp, peer) in enumerate(copies):
            pltpu.make_async_remote_copy(
                src_ref=v_ref, dst_ref=rows(4 * peer[0] + 2 * peer[1] + peer[2]), send_sem=send1.at[k],
                recv_sem=recv1.at[k], device_id=peer, device_id_type=MESH).wait_recv()
        for cp, _ in copies:
            cp.wait_send()

        def c_of(dev):
            segments, pos = [], 0
            while pos < d:
                row, col = divmod(pos, n)
                take = min(d - pos, n - col)
                segments.append(all_ref[dev * rows_per + row:dev * rows_per + row + 1, col:col + take])
                pos += take
            return jnp.concatenate(segments, axis=1)

        c_all = jnp.concatenate([c_of(dev) for dev in range(N_DEV)], axis=0)
        load.wait()
        part_ref[...] = jnp.dot(_silu(c_all), w_vmem[...], precision=lax.Precision.HIGHEST, preferred_element_type=F32)
        parts_ref[chip] = part_ref[...]
        swaps = []
        for k, (peer, _) in enumerate(_ici_peers(x, y, c)):
            cp = pltpu.make_async_remote_copy(
                src_ref=part_ref, dst_ref=parts_ref.at[chip], send_sem=send2.at[k], recv_sem=recv2.at[k],
                device_id=peer, device_id_type=MESH)
            cp.start()
            swaps.append(cp)
        for i in range(first_chunks, n_chunks):
            cast_chunk(i)
        for i in range(n_chunks - 2, n_chunks):
            chunk_out(i).wait()
        for k, (peer, peer_chip) in enumerate(_ici_peers(x, y, c)):
            pltpu.make_async_remote_copy(
                src_ref=part_ref, dst_ref=parts_ref.at[peer_chip], send_sem=send2.at[k], recv_sem=recv2.at[k],
                device_id=peer, device_id_type=MESH).wait_recv()
        for cp in swaps:
            cp.wait_send()
        flat = jnp.concatenate([parts_ref[j, pl.ds(me, 1), :] for j in range(N_CHIPS)], axis=1) + b_ref[...]
        mod_ref[...] = jnp.concatenate([flat[:, i * d:(i + 1) * d] for i in range(3)], axis=0)

    return pl.pallas_call(
        body, name="ada_modulation",
        out_shape=(jax.ShapeDtypeStruct((N_DEV * rows_per, n), F32), jax.ShapeDtypeStruct((3, d), F32),
                   jax.ShapeDtypeStruct((N_CHIPS, big_rows, big_cols), BF16)),
        in_specs=[VMEM, ANY, VMEM, ANY], out_specs=(VMEM, VMEM, ANY),
        scratch_shapes=[pltpu.VMEM((d_model, wa), F32), pltpu.VMEM((N_DEV, wa), F32),
                        pltpu.VMEM((N_CHIPS, N_DEV, wa), F32),
                        pltpu.VMEM((2, ROW_TILE, big_cols), F32), pltpu.VMEM((2, ROW_TILE, big_cols), BF16),
                        pltpu.SemaphoreType.DMA,
                        pltpu.SemaphoreType.DMA((N_DEV - 1,)), pltpu.SemaphoreType.DMA((N_DEV - 1,)),
                        pltpu.SemaphoreType.DMA((N_CHIPS - 1,)), pltpu.SemaphoreType.DMA((N_CHIPS - 1,)),
                        pltpu.SemaphoreType.DMA((2,)), pltpu.SemaphoreType.DMA((2,))],
        compiler_params=_params(),
    )(packed, w_ada, b_ada, w_big)


def _prenorm(x, mod, g_pre, after):
    t, d = x.shape
    tb = ROW_TILE

    def body(x_ref, mod_ref, g_ref, after_ref, h_ref, ht_ref):
        del after_ref
        xv = x_ref[...]
        r = lax.rsqrt(jnp.mean(xv * xv, axis=-1, keepdims=True) + EPS)
        h = (xv * r) * g_ref[...] * (1.0 + mod_ref[1:2, :]) + mod_ref[0:1, :]
        h_ref[...] = h.astype(BF16)
        ht_ref[...] = h.T.astype(BF16)

    return pl.pallas_call(
        body, name="prenorm", grid=(t // tb,),
        out_shape=(jax.ShapeDtypeStruct((t, d), BF16), jax.ShapeDtypeStruct((d, t), BF16)),
        in_specs=[pl.BlockSpec((tb, d), lambda i: (i, 0)), pl.BlockSpec((3, d), lambda i: (0, 0)),
                  pl.BlockSpec((1, d), lambda i: (0, 0)), ANY],
        out_specs=(pl.BlockSpec((tb, d), lambda i: (i, 0)), pl.BlockSpec((d, tb), lambda i: (0, i))),
        compiler_params=_params(("parallel",)),
    )(x, mod, g_pre, after)


def _proj_tiles(proj, h, w, tiles, step, name):
    t, d = h.shape
    ws = w.shape[-1]
    tn = COL_TILE
    nt = ws // tn

    def body(tile_ref, *refs):
        del tile_ref
        a_ref, b_ref, o_ref = refs[-3:]
        o_ref[...] = jnp.dot(a_ref[...], b_ref[...].astype(BF16), preferred_element_type=F32).astype(BF16)

    if w.ndim == 3:
        w_spec = pl.BlockSpec((None, d, tn), lambda i, tl: (tl[step, i] // nt, 0, tl[step, i] % nt))
    else:
        w_spec = pl.BlockSpec((d, tn), lambda i, tl: (0, tl[step, i] % nt))
    first = proj is None
    grid_spec = pltpu.PrefetchScalarGridSpec(
        num_scalar_prefetch=1, grid=(tiles.shape[1],),
        in_specs=([] if first else [HBM]) + [pl.BlockSpec((t, d), lambda i, tl: (0, 0)), w_spec],
        out_specs=pl.BlockSpec((t, tn), lambda i, tl: (0, tl[step, i])))
    return pl.pallas_call(
        body, name=name, grid_spec=grid_spec,
        out_shape=jax.ShapeDtypeStruct((t, N_CHIPS * ws), BF16),
        input_output_aliases={} if first else {1: 0},
        compiler_params=_params(("parallel",)),
    )(*([tiles] if first else [tiles, proj]), h, w)


def _shift_rows(a, rows):
    idx = lax.broadcasted_iota(jnp.int32, a.shape, 0)
    prev = jnp.where(idx == 0, 0.0, pltpu.roll(a, 1, 0))
    nxt = jnp.where(idx == rows - 1, 0.0, pltpu.roll(a, rows - 1, 0))
    return prev, nxt


def _conv_fwd(conv_proj, conv_w, conv_b, dc):
    t = conv_proj.shape[0]
    ct = CONV_TILE
    nct = dc // ct

    def body(u_ref, cg_ref, w_ref, b_ref, co_ref):
        a = cg_ref[...].astype(F32) * u_ref[...].astype(F32)
        prev, nxt = _shift_rows(a, t)
        co_ref[...] = (w_ref[0:1, :] * prev + w_ref[1:2, :] * a + w_ref[2:3, :] * nxt + b_ref[...]).astype(BF16)

    return pl.pallas_call(
        body, name="conv_fwd", grid=(nct,),
        out_shape=jax.ShapeDtypeStruct((t, dc), BF16),
        in_specs=[pl.BlockSpec((t, ct), lambda i: (0, i)), pl.BlockSpec((t, ct), lambda i: (0, 2 * nct + i)),
                  pl.BlockSpec((3, ct), lambda i: (0, i)), pl.BlockSpec((1, ct), lambda i: (0, i))],
        out_specs=pl.BlockSpec((t, ct), lambda i: (0, i)),
        compiler_params=_params(("parallel",)),
    )(conv_proj, conv_proj, conv_w, conv_b)


def _to_residue_major(src_ref, dst_ref, r):
    seq = src_ref.shape[0] // r
    for res in range(r):
        dst_ref[res * seq:(res + 1) * seq, :] = src_ref[pl.ds(res, seq, stride=r), :].astype(dst_ref.dtype)


def _branch_operands(token_refs, stage, dil, r):
    if r == 1:
        return list(token_refs)
    for i, ref in enumerate(token_refs):
        stage[...] = ref[...].astype(F32)
        _to_residue_major(stage, dil.at[i], r)
    return [dil.at[i] for i in range(len(token_refs))]


def _scaled_queries(q):
    return (q.astype(F32) * (HEAD_DIM ** -0.5)).astype(BF16)


BLOCK_SHIFTS = (0, -SIDE, None)


def _band_bias(rel, slope):
    arel = jnp.abs(rel)
    return jnp.where(arel <= SIDE, arel.astype(F32) * slope, NEG_INF)


def _fill_bias_tiles(bias_ref, sl_ref, r, kw):
    base = lax.broadcasted_iota(jnp.int32, (ATT_BQ, kw), 1) - lax.broadcasted_iota(jnp.int32, (ATT_BQ, kw), 0)
    for hh in range(2):
        slope = -(sl_ref[hh:hh + 1, 0:kw] * float(r))
        for e, shift in enumerate(BLOCK_SHIFTS):
            shift = ATT_BQ - kw if shift is None else shift
            bias_ref[hh, e, :, 0:kw] = _band_bias(base + shift, slope)


def _fill_stacked_bias_tiles(bias_ref, sl_ref, r, kw):
    base = lax.broadcasted_iota(jnp.int32, (kw, ATT_BQ), 0) - lax.broadcasted_iota(jnp.int32, (kw, ATT_BQ), 1)
    for hh in range(2):
        slope = -(sl_ref[hh:hh + 1, 0:ATT_BQ] * float(r))
        for e, shift in enumerate(BLOCK_SHIFTS):
            shift = ATT_BQ - kw if shift is None else shift
            bias_ref[e, 0:kw, hh * ATT_BQ:(hh + 1) * ATT_BQ] = _band_bias(base + shift, slope)


def _first_head_lanes():
    return lax.broadcasted_iota(jnp.int32, (1, PAIR), 1) < HEAD_DIM


def _only_head(x, first, hh):
    return jnp.where(first if hh == 0 else jnp.logical_not(first), x, jnp.zeros_like(x))


def _block_place(g, seq_len, kw):
    nqb = seq_len // ATT_BQ
    if nqb == 1:
        row = pl.multiple_of(g * ATT_BQ, ATT_BQ)
        return row, row, 0
    res = g // nqb
    qb = g - res * nqb
    q0 = qb * ATT_BQ
    ks = jnp.clip(q0 - SIDE, 0, seq_len - kw)
    edge = jnp.where(qb == 0, 0, jnp.where(qb == nqb - 1, 2, 1))
    return (pl.multiple_of(res * seq_len + q0, ATT_BQ), pl.multiple_of(res * seq_len + ks, SIDE), edge)


def _qkv_specs(dc, da, t, index):
    return [pl.BlockSpec((t, PAIR), functools.partial(index, (4 * dc + comp * da) // PAIR)) for comp in range(3)]


def _attn_fwd(proj, slopes, dc, da):
    t = proj.shape[0]
    hp = da // PAIR
    n_blocks = t // ATT_BQ

    def body(q_ref, k_ref, v_ref, sl_ref, o_ref, lse_ref, stage, dil, bias, o_res, l_res, o_tok, l_tok):
        for b, (_, r) in enumerate(BRANCHES):
            seq_len = t // r
            kw = min(ATT_KW, seq_len)
            ops = _branch_operands([q_ref, k_ref, v_ref], stage, dil, r)
            _fill_bias_tiles(bias, sl_ref, r, kw)
            o_dst, l_dst = (o_tok.at[b], l_tok.at[b]) if r == 1 else (o_res, l_res)
            first = _first_head_lanes()

            def blocks(trip, carry, seq_len=seq_len, kw=kw, o_dst=o_dst, l_dst=l_dst, first=first, ops=ops):
                nt = (((1,), (1,)), ((), ()))
                places = [_block_place(trip * ATT_UNROLL + i, seq_len, kw) for i in range(ATT_UNROLL)]
                chains = [(i, hh) for i in range(ATT_UNROLL) for hh in range(2)]
                qs = [_scaled_queries(ops[0][pl.ds(qrow, ATT_BQ), :]) for qrow, _, _ in places]
                ks = [ops[1][pl.ds(krow, kw), :] for _, krow, _ in places]
                vs = [ops[2][pl.ds(krow, kw), :] for _, krow, _ in places]
                ss = [lax.dot_general(_only_head(qs[i], first, hh), ks[i], nt, preferred_element_type=F32)
                      + bias[hh, places[i][2], :, 0:kw] for i, hh in chains]
                tops = [jnp.max(s, axis=-1, keepdims=True) for s in ss]
                ps = [jnp.exp(s - m) for s, m in zip(ss, tops)]
                dens = [jnp.sum(p, axis=-1, keepdims=True) for p in ps]
                for i, (qrow, _, _) in enumerate(places):
                    weights = jnp.concatenate([ps[2 * i].astype(BF16), ps[2 * i + 1].astype(BF16)], axis=1)
                    values = jnp.concatenate([_only_head(vs[i], first, 0), _only_head(vs[i], first, 1)], axis=0)
                    den = jnp.where(first, dens[2 * i], dens[2 * i + 1])
                    o_dst[pl.ds(qrow, ATT_BQ), :] = jnp.dot(weights, values, preferred_element_type=F32) / den
                    l_dst[pl.ds(qrow, ATT_BQ), :] = jnp.where(first, tops[2 * i], tops[2 * i + 1]) + jnp.log(den)
                return carry

            lax.fori_loop(0, n_blocks // ATT_UNROLL, blocks, 0)
            if r > 1:
                for res in range(r):
                    rows = slice(res * seq_len, (res + 1) * seq_len)
                    o_tok[b, pl.ds(res, seq_len, stride=r), :] = o_res[rows, :]
                    l_tok[b, pl.ds(res, seq_len, stride=r), :] = l_res[rows, :]

        def merge(i, carry):
            rows = pl.ds(pl.multiple_of(i * ROW_TILE, ROW_TILE), ROW_TILE)
            la, lb, lc = l_tok[0, rows, :], l_tok[1, rows, :], l_tok[2, rows, :]
            m = jnp.maximum(jnp.maximum(la, lb), lc)
            wa, wb, wc = jnp.exp(la - m), jnp.exp(lb - m), jnp.exp(lc - m)
            den = wa + wb + wc
            o_ref[rows, :] = (wa * o_tok[0, rows, :] + wb * o_tok[1, rows, :] + wc * o_tok[2, rows, :]) * (1.0 / den)
            lse_ref[rows, :] = m + jnp.log(den)
            return carry

        lax.fori_loop(0, t // ROW_TILE, merge, 0)

    pair_spec = pl.BlockSpec((None, t, PAIR), lambda h: (h, 0, 0))
    return pl.pallas_call(
        body, name="attn_fwd", grid=(hp,),
        out_shape=(jax.ShapeDtypeStruct((hp, t, PAIR), F32), jax.ShapeDtypeStruct((hp, t, PAIR), F32)),
        in_specs=_qkv_specs(dc, da, t, lambda first, h: (0, first + h))
        + [pl.BlockSpec((None, 8, ATT_KW), lambda h: (h, 0, 0))],
        out_specs=(pair_spec, pair_spec),
        scratch_shapes=[pltpu.VMEM((t, PAIR), F32), pltpu.VMEM((3, t, PAIR), BF16),
                        pltpu.VMEM((2, 3, ATT_BQ, ATT_KW), F32),
                        pltpu.VMEM((t, PAIR), F32), pltpu.VMEM((t, PAIR), F32),
                        pltpu.VMEM((3, t, PAIR), F32), pltpu.VMEM((3, t, PAIR), F32)],
        compiler_params=_params(("parallel",)),
    )(proj, proj, proj, slopes)


def _attn_bwd(dproj, proj, d_o, lse, delta, slopes, dc, da, after):
    t = proj.shape[0]
    hp = da // PAIR
    n_blocks = t // ATT_BQ

    def all_branches(q_ref, k_ref, v_ref, do_ref, lse_ref, dl_ref, sl_ref,
                     stage, dil, packed, packed_res, row_vecs, bias_t, acc, tot):
        first = _first_head_lanes()
        lane = lax.broadcasted_iota(jnp.int32, (1, PAIR), 1)
        packed[...] = jnp.where((lane & (HEAD_DIM - 1)) < HEAD_DIM // 2, lse_ref[...], dl_ref[...])
        for b, (_, r) in enumerate(BRANCHES):
            seq_len = t // r
            kw = min(ATT_KW, seq_len)
            ops = _branch_operands([q_ref, k_ref, v_ref, do_ref], stage, dil, r)
            scalars = packed
            if r > 1:
                _to_residue_major(packed, packed_res, r)
                scalars = packed_res
            for g in range(n_blocks):
                flipped = scalars[g * ATT_BQ:(g + 1) * ATT_BQ, :].T
                for row in range(4):
                    row_vecs[g, row:row + 1, :] = flipped[row * (HEAD_DIM // 2):row * (HEAD_DIM // 2) + 1, :]
            _fill_stacked_bias_tiles(bias_t, sl_ref, r, kw)
            acc[1] = jnp.zeros((t, PAIR), F32)
            acc[2] = jnp.zeros((t, PAIR), F32)

            def blocks(trip, carry, seq_len=seq_len, kw=kw, ops=ops):
                nt = (((1,), (1,)), ((), ()))
                group = range(ATT_UNROLL)
                places = [_block_place(trip * ATT_UNROLL + i, seq_len, kw) for i in group]
                ks, vs, q2s, do2s, lse2s, dl2s = [], [], [], [], [], []
                for i, (qrow, krow, _) in zip(group, places):
                    q = _scaled_queries(ops[0][pl.ds(qrow, ATT_BQ), :])
                    dov = ops[3][pl.ds(qrow, ATT_BQ), :]
                    ks.append(ops[1][pl.ds(krow, kw), :])
                    vs.append(ops[2][pl.ds(krow, kw), :])
                    q2s.append(jnp.concatenate([_only_head(q, first, 0), _only_head(q, first, 1)], axis=0))
                    do2s.append(jnp.concatenate([_only_head(dov, first, 0), _only_head(dov, first, 1)], axis=0))
                    rows = row_vecs[trip * ATT_UNROLL + i]
                    lse2s.append(jnp.concatenate([rows[0:1, :], rows[2:3, :]], axis=1))
                    dl2s.append(jnp.concatenate([rows[1:2, :], rows[3:4, :]], axis=1))
                s_ts = [lax.dot_general(ks[i], q2s[i], nt, preferred_element_type=F32) for i in group]
                dp_ts = [lax.dot_general(vs[i], do2s[i], nt, preferred_element_type=F32) for i in group]
                p_ts = [jnp.exp(s_ts[i] + bias_t[places[i][2], 0:kw, :] - lse2s[i]) for i in group]
                ds_ts = [p_ts[i] * (dp_ts[i] - dl2s[i]) for i in group]
                dvs = [jnp.dot(p_ts[i].astype(BF16), do2s[i], preferred_element_type=F32) for i in group]
                dks = [jnp.dot(ds_ts[i].astype(BF16), q2s[i], preferred_element_type=F32) for i in group]
                dss = [ds_ts[i].T.astype(BF16) for i in group]
                dqs = [jnp.dot(dss[i][0:ATT_BQ, :], _only_head(ks[i], first, 0), preferred_element_type=F32)
                       + jnp.dot(dss[i][ATT_BQ:2 * ATT_BQ, :], _only_head(ks[i], first, 1), preferred_element_type=F32)
                       for i in group]
                for i, (qrow, krow, _) in zip(group, places):
                    acc[0, pl.ds(qrow, ATT_BQ), :] = dqs[i] * (HEAD_DIM ** -0.5)
                    acc[1, pl.ds(krow, kw), :] += dks[i]
                    acc[2, pl.ds(krow, kw), :] += dvs[i]
                return carry

            lax.fori_loop(0, n_blocks // ATT_UNROLL, blocks, 0)
            for comp in range(3):
                if r == 1:
                    tot[comp] = acc[comp]
                else:
                    for res in range(r):
                        tok = pl.ds(res, seq_len, stride=r)
                        tot[comp, tok, :] = tot[comp, tok, :] + acc[comp, res * seq_len:(res + 1) * seq_len, :]

    first_q = (4 * dc) // PAIR

    def body(dproj_in, q_ref, k_ref, v_ref, do_ref, lse_ref, dl_ref, sl_ref, after_ref, out_ref, *scratch):
        del dproj_in, after_ref
        work, out_stage, out_sems = scratch[:-2], scratch[-2], scratch[-1]
        h = pl.program_id(0)
        all_branches(q_ref, k_ref, v_ref, do_ref, lse_ref, dl_ref, sl_ref, *work)

        def out_copy(comp):
            cols = pl.ds(pl.multiple_of((first_q + comp * hp + h) * PAIR, PAIR), PAIR)
            return pltpu.make_async_copy(out_stage.at[comp], out_ref.at[:, cols], out_sems.at[comp])

        @pl.when(h > 0)
        def _():
            for comp in range(3):
                out_copy(comp).wait()

        for comp in range(3):
            out_stage[comp] = work[-1][comp].astype(BF16)
            out_copy(comp).start()

        @pl.when(h == hp - 1)
        def _():
            for comp in range(3):
                out_copy(comp).wait()

    pair_spec = pl.BlockSpec((None, t, PAIR), lambda h: (h, 0, 0))
    return pl.pallas_call(
        body, name="attn_bwd", grid=(hp,),
        out_shape=jax.ShapeDtypeStruct(dproj.shape, BF16),
        in_specs=[HBM] + _qkv_specs(dc, da, t, lambda first, h: (0, first + h))
        + [pair_spec, pair_spec, pair_spec, pl.BlockSpec((None, 8, ATT_KW), lambda h: (h, 0, 0)), ANY],
        out_specs=ANY,
        input_output_aliases={0: 0},
        scratch_shapes=[pltpu.VMEM((t, PAIR), F32), pltpu.VMEM((4, t, PAIR), BF16),
                        pltpu.VMEM((t, PAIR), F32), pltpu.VMEM((t, PAIR), F32),
                        pltpu.VMEM((n_blocks, 8, ATT_BQ), F32), pltpu.VMEM((3, ATT_KW, 2 * ATT_BQ), F32),
                        pltpu.VMEM((3, t, PAIR), F32), pltpu.VMEM((3, t, PAIR), F32),
                        pltpu.VMEM((3, t, PAIR), BF16), pltpu.SemaphoreType.DMA((3,))],
        compiler_params=_params(("arbitrary",)),
    )(dproj, proj, proj, proj, d_o, lse, delta, slopes, after)


def _mix_fwd(co, proj, o_mix, g_conv, g_attn_pairs, after):
    t, dc = co.shape
    hp = o_mix.shape[0]
    da = hp * PAIR
    tb = ROW_TILE

    def body(co_ref, bg_ref, zc_ref, za_ref, om_ref, gc_ref, ga_ref, after_ref, ycat_ref, ycatt_ref):
        del after_ref
        p = bg_ref[...].astype(F32) * co_ref[...].astype(F32)
        rc = lax.rsqrt(jnp.mean(p * p, axis=-1, keepdims=True) + EPS)
        yc = (p * rc) * gc_ref[...] * _silu(zc_ref[...].astype(F32))
        ycat_ref[:, 0:dc] = yc.astype(BF16)
        ycatt_ref[0:dc, :] = yc.T.astype(BF16)
        ssq = jnp.zeros((tb, 1), F32)
        for h in range(hp):
            o = om_ref[h]
            ssq = ssq + jnp.sum(o * o, axis=-1, keepdims=True)
        ra = lax.rsqrt(ssq * (1.0 / da) + EPS)
        for h in range(hp):
            ya = (om_ref[h] * ra) * ga_ref[h] * _silu(za_ref[:, h * PAIR:(h + 1) * PAIR].astype(F32))
            ycat_ref[:, dc + h * PAIR:dc + (h + 1) * PAIR] = ya.astype(BF16)
            ycatt_ref[dc + h * PAIR:dc + (h + 1) * PAIR, :] = ya.T.astype(BF16)

    pair_spec = pl.BlockSpec((hp, tb, PAIR), lambda i: (0, i, 0))
    return pl.pallas_call(
        body, name="mix_fwd", grid=(t // tb,),
        out_shape=(jax.ShapeDtypeStruct((t, dc + da), BF16), jax.ShapeDtypeStruct((dc + da, t), BF16)),
        in_specs=[pl.BlockSpec((tb, dc), lambda i: (i, 0)),
                  pl.BlockSpec((tb, dc), lambda i: (i, 1)),
                  pl.BlockSpec((tb, dc), lambda i: (i, 3)),
                  pl.BlockSpec((tb, da), lambda i: (i, 7)),
                  pair_spec,
                  pl.BlockSpec((1, dc), lambda i: (0, 0)),
                  pl.BlockSpec((hp, 1, PAIR), lambda i: (0, 0, 0)), ANY],
        out_specs=(pl.BlockSpec((tb, dc + da), lambda i: (i, 0)), pl.BlockSpec((dc + da, tb), lambda i: (0, i))),
        compiler_params=_params(("parallel",)),
    )(co, proj, proj, proj, o_mix, g_conv, g_attn_pairs, after)


def _out_fwd_bwd(ycat, woutf, x, target, mod, g_post):
    t, d = x.shape
    n = ycat.shape[1]
    tb = ROW_TILE

    def body(a_ref, w_ref, x_ref, tg_ref, mod_ref, g_ref, dout_ref, dy_ref, acc_ref):
        y = jnp.dot(a_ref[...], w_ref[...], preferred_element_type=F32)
        r = lax.rsqrt(jnp.mean(y * y, axis=-1, keepdims=True) + EPS)
        nh = y * r
        gate = mod_ref[2:3, :]
        nrm = nh * g_ref[...]
        err = x_ref[...] + gate * nrm - tg_ref[...]
        dout = err * (1.0 / d)
        dout_ref[...] = dout.astype(BF16)
        dn = dout * gate
        a = dn * g_ref[...]
        dy = r * (a - nh * jnp.mean(a * nh, axis=-1, keepdims=True))
        dy_ref[...] = dy.astype(BF16)
        loss = 0.5 * jnp.sum(jnp.sum(err * err, axis=-1, keepdims=True) * (1.0 / d), axis=0, keepdims=True)
        part = jnp.concatenate(
            [jnp.sum(dout * nrm, axis=0, keepdims=True), jnp.sum(dn * nh, axis=0, keepdims=True),
             jnp.broadcast_to(loss, (1, d)), jnp.zeros((5, d), F32)], axis=0)

        @pl.when(pl.program_id(0) == 0)
        def _():
            acc_ref[...] = jnp.zeros(acc_ref.shape, F32)

        acc_ref[...] += part

    return pl.pallas_call(
        body, name="out_fwd_bwd", grid=(t // tb,),
        out_shape=(jax.ShapeDtypeStruct((t, d), BF16), jax.ShapeDtypeStruct((t, d), BF16),
                   jax.ShapeDtypeStruct((8, d), F32)),
        in_specs=[pl.BlockSpec((tb, n), lambda i: (i, 0)), pl.BlockSpec((n, d), lambda i: (0, 0)),
                  pl.BlockSpec((tb, d), lambda i: (i, 0)), pl.BlockSpec((tb, d), lambda i: (i, 0)),
                  pl.BlockSpec((3, d), lambda i: (0, 0)), pl.BlockSpec((1, d), lambda i: (0, 0))],
        out_specs=(pl.BlockSpec((tb, d), lambda i: (i, 0)), pl.BlockSpec((tb, d), lambda i: (i, 0)),
                   pl.BlockSpec((8, d), lambda i: (0, 0))),
        compiler_params=_params(("arbitrary",)),
    )(ycat, woutf, x, target, mod, g_post)


def _matmul_nt(a, b, out_dtype, name):
    m, k = a.shape
    n = b.shape[0]
    tn = COL_TILE

    def body(a_ref, b_ref, o_ref):
        o_ref[...] = lax.dot_general(a_ref[...], b_ref[...], (((1,), (1,)), ((), ())),
                                     preferred_element_type=F32).astype(out_dtype)

    return pl.pallas_call(
        body, name=name, grid=(n // tn,),
        out_shape=jax.ShapeDtypeStruct((m, n), out_dtype),
        in_specs=[pl.BlockSpec((m, k), lambda i: (0, 0)), pl.BlockSpec((tn, k), lambda i: (i, 0))],
        out_specs=pl.BlockSpec((m, tn), lambda i: (0, i)),
        compiler_params=_params(("parallel",)),
    )(a, b)


def _mix_bwd(dycat, co, proj, o_mix, g_conv, g_attn_pairs):
    t, dc = co.shape
    hp = o_mix.shape[0]
    da = hp * PAIR
    tb = ROW_TILE

    def body(dy_ref, co_ref, bg_ref, zc_ref, za_ref, om_ref, gc_ref, ga_ref,
             dcp_ref, dco_ref, do_ref, dl_ref, dgc_ref, dga_ref):
        first = pl.program_id(0) == 0
        cov = co_ref[...].astype(F32)
        bg = bg_ref[...].astype(F32)
        zc = zc_ref[...].astype(F32)
        p = bg * cov
        rc = lax.rsqrt(jnp.mean(p * p, axis=-1, keepdims=True) + EPS)
        nh = p * rc
        dyc = dy_ref[:, 0:dc].astype(F32)
        dn = dyc * _silu(zc)
        a = dn * gc_ref[...]
        dp = rc * (a - nh * jnp.mean(a * nh, axis=-1, keepdims=True))
        dcp_ref[:, 0:dc] = jnp.zeros((tb, dc), BF16)
        dcp_ref[:, dc:2 * dc] = (dp * cov).astype(BF16)
        dcp_ref[:, 2 * dc:3 * dc] = jnp.zeros((tb, dc), BF16)
        dcp_ref[:, 3 * dc:4 * dc] = (dyc * nh * gc_ref[...] * _silu_grad(zc)).astype(BF16)
        dcp_ref[:, 4 * dc:4 * dc + 3 * da] = jnp.zeros((tb, 3 * da), BF16)
        dco_ref[...] = dp * bg

        @pl.when(first)
        def _():
            dgc_ref[...] = jnp.zeros(dgc_ref.shape, F32)
            dga_ref[...] = jnp.zeros(dga_ref.shape, F32)

        dgc_ref[...] += jnp.sum(dn * nh, axis=0, keepdims=True)

        ssq = jnp.zeros((tb, 1), F32)
        for h in range(hp):
            o = om_ref[h]
            ssq = ssq + jnp.sum(o * o, axis=-1, keepdims=True)
        ra = lax.rsqrt(ssq * (1.0 / da) + EPS)
        dot_an = jnp.zeros((tb, 1), F32)
        for h in range(hp):
            nha = om_ref[h] * ra
            za = za_ref[:, h * PAIR:(h + 1) * PAIR].astype(F32)
            dya = dy_ref[:, dc + h * PAIR:dc + (h + 1) * PAIR].astype(F32)
            dna = dya * _silu(za)
            dza = (dya * nha * ga_ref[h] * _silu_grad(za)).astype(BF16)
            dcp_ref[:, 4 * dc + 3 * da + h * PAIR:4 * dc + 3 * da + (h + 1) * PAIR] = dza
            dga_ref[h] += jnp.sum(dna * nha, axis=0, keepdims=True)
            dot_an = dot_an + jnp.sum(dna * ga_ref[h] * nha, axis=-1, keepdims=True)
        mean_an = dot_an * (1.0 / da)
        first_head = lax.broadcasted_iota(jnp.int32, (tb, PAIR), 1) < HEAD_DIM
        for h in range(hp):
            o = om_ref[h]
            nha = o * ra
            za = za_ref[:, h * PAIR:(h + 1) * PAIR].astype(F32)
            dya = dy_ref[:, dc + h * PAIR:dc + (h + 1) * PAIR].astype(F32)
            aa = dya * _silu(za) * ga_ref[h]
            d_o = ra * (aa - nha * mean_an)
            do_ref[h] = d_o.astype(BF16)
            prod = d_o * o
            both = jnp.sum(prod, axis=-1, keepdims=True)
            head0 = jnp.sum(jnp.where(first_head, prod, 0.0), axis=-1, keepdims=True)
            dl_ref[h] = jnp.where(first_head, head0, both - head0)

    pair_spec = pl.BlockSpec((hp, tb, PAIR), lambda i: (0, i, 0))
    return pl.pallas_call(
        body, name="mix_bwd", grid=(t // tb,),
        out_shape=(jax.ShapeDtypeStruct((t, 4 * dc + 4 * da), BF16), jax.ShapeDtypeStruct((t, dc), F32),
                   jax.ShapeDtypeStruct((hp, t, PAIR), BF16), jax.ShapeDtypeStruct((hp, t, PAIR), F32),
                   jax.ShapeDtypeStruct((1, dc), F32), jax.ShapeDtypeStruct((hp, 1, PAIR), F32)),
        in_specs=[pl.BlockSpec((tb, dc + da), lambda i: (i, 0)),
                  pl.BlockSpec((tb, dc), lambda i: (i, 0)),
                  pl.BlockSpec((tb, dc), lambda i: (i, 1)),
                  pl.BlockSpec((tb, dc), lambda i: (i, 3)),
                  pl.BlockSpec((tb, da), lambda i: (i, 7)),
                  pair_spec,
                  pl.BlockSpec((1, dc), lambda i: (0, 0)),
                  pl.BlockSpec((hp, 1, PAIR), lambda i: (0, 0, 0))],
        out_specs=(pl.BlockSpec((tb, 4 * dc + 4 * da), lambda i: (i, 0)), pl.BlockSpec((tb, dc), lambda i: (i, 0)),
                   pair_spec, pair_spec,
                   pl.BlockSpec((1, dc), lambda i: (0, 0)), pl.BlockSpec((hp, 1, PAIR), lambda i: (0, 0, 0))),
        compiler_params=_params(("arbitrary",)),
    )(dycat, co, proj, proj, proj, o_mix, g_conv, g_attn_pairs)


def _conv_bwd(dconv_proj, dco, conv_proj, conv_w, dc, after):
    t = dco.shape[0]
    ct = CONV_TILE
    nct = dc // ct

    def body(dcp_in_ref, dco_ref, u_ref, cg_ref, w_ref, after_ref, dcp_ref, acc_ref):
        del dcp_in_ref, after_ref
        which = pl.program_id(1)
        g = dco_ref[...]
        u = u_ref[...].astype(F32)
        cg = cg_ref[...].astype(F32)
        g_prev, g_next = _shift_rows(g, t)
        da = w_ref[0:1, :] * g_next + w_ref[1:2, :] * g + w_ref[2:3, :] * g_prev
        dcp_ref[...] = (da * jnp.where(which == 0, cg, u)).astype(BF16)
        a = cg * u
        a_prev, a_next = _shift_rows(a, t)
        acc_ref[...] = jnp.concatenate(
            [jnp.sum(g * a_prev, axis=0, keepdims=True), jnp.sum(g * a, axis=0, keepdims=True),
             jnp.sum(g * a_next, axis=0, keepdims=True), jnp.sum(g, axis=0, keepdims=True),
             jnp.zeros((4, ct), F32)], axis=0)

    return pl.pallas_call(
        body, name="conv_bwd", grid=(nct, 2),
        out_shape=(jax.ShapeDtypeStruct(dconv_proj.shape, BF16), jax.ShapeDtypeStruct((8, dc), F32)),
        in_specs=[HBM,
                  pl.BlockSpec((t, ct), lambda i, s: (0, i)),
                  pl.BlockSpec((t, ct), lambda i, s: (0, i)),
                  pl.BlockSpec((t, ct), lambda i, s: (0, 2 * nct + i)),
                  pl.BlockSpec((3, ct), lambda i, s: (0, i)), ANY],
        out_specs=(pl.BlockSpec((t, ct), lambda i, s: (0, 2 * s * nct + i)),
                   pl.BlockSpec((8, ct), lambda i, s: (0, i))),
        input_output_aliases={0: 0},
        compiler_params=_params(("arbitrary", "arbitrary")),
    )(dconv_proj, dco, conv_proj, conv_proj, conv_w, after)


def _dh(dproj, winf, after):
    t = dproj.shape[0]
    _, d, ws = winf.shape
    tm = tn = COL_TILE
    nt = (((1,), (1,)), ((), ()))

    def body(a_ref, w_ref, after_ref, o_ref):
        del after_ref
        acc = lax.dot_general(a_ref[:, 0:ws], w_ref[0], nt, preferred_element_type=F32)
        for j in range(1, N_CHIPS):
            acc = acc + lax.dot_general(a_ref[:, j * ws:(j + 1) * ws], w_ref[j], nt, preferred_element_type=F32)
        o_ref[...] = acc.astype(BF16)

    return pl.pallas_call(
        body, name="dh", grid=(d // tn, t // tm),
        out_shape=jax.ShapeDtypeStruct((t, d), BF16),
        in_specs=[pl.BlockSpec((tm, N_CHIPS * ws), lambda n, m: (m, 0)),
                  pl.BlockSpec((N_CHIPS, tn, ws), lambda n, m: (0, n, 0)), ANY],
        out_specs=pl.BlockSpec((tm, tn), lambda n, m: (m, n)),
        compiler_params=_params(("parallel", "parallel")),
    )(dproj, winf, after)


def _prenorm_bwd(x, dh, dout, mod, g_pre):
    t, d = x.shape
    tb = ROW_TILE

    def body(x_ref, dh_ref, dout_ref, mod_ref, g_ref, gx_ref, acc_ref):
        xv = x_ref[...]
        dhv = dh_ref[...].astype(F32)
        r = lax.rsqrt(jnp.mean(xv * xv, axis=-1, keepdims=True) + EPS)
        xh = xv * r
        one_scale = 1.0 + mod_ref[1:2, :]
        a = dhv * one_scale * g_ref[...]
        gx_ref[...] = dout_ref[...].astype(F32) + r * (a - xh * jnp.mean(a * xh, axis=-1, keepdims=True))
        part = jnp.concatenate(
            [jnp.sum(dhv, axis=0, keepdims=True), jnp.sum(dhv * xh * g_ref[...], axis=0, keepdims=True),
             jnp.sum(dhv * xh * one_scale, axis=0, keepdims=True), jnp.zeros((5, d), F32)], axis=0)

        @pl.when(pl.program_id(0) == 0)
        def _():
            acc_ref[...] = jnp.zeros(acc_ref.shape, F32)

        acc_ref[...] += part

    return pl.pallas_call(
        body, name="prenorm_bwd", grid=(t // tb,),
        out_shape=(jax.ShapeDtypeStruct((t, d), F32), jax.ShapeDtypeStruct((8, d), F32)),
        in_specs=[pl.BlockSpec((tb, d), lambda i: (i, 0)), pl.BlockSpec((tb, d), lambda i: (i, 0)),
                  pl.BlockSpec((tb, d), lambda i: (i, 0)), pl.BlockSpec((3, d), lambda i: (0, 0)),
                  pl.BlockSpec((1, d), lambda i: (0, 0))],
        out_specs=(pl.BlockSpec((tb, d), lambda i: (i, 0)), pl.BlockSpec((8, d), lambda i: (0, 0))),
        compiler_params=_params(("arbitrary",)),
    )(x, dh, dout, mod, g_pre)


def _chip_sums(mine, rsib, name, part=0, parts=1, after=()):
    _, half, cols = mine.shape
    rows = half // parts
    tr = min(rows, ROW_TILE)
    nt = rows // tr

    def body(g_ref, r_ref, *rest):
        rest[-1][...] = (g_ref[...].astype(F32) + r_ref[...].astype(F32)).astype(BF16)

    spec = pl.BlockSpec((None, tr, cols), lambda j, i: (j, part * nt + i, 0))
    return pl.pallas_call(
        body, name=name, grid=(N_CHIPS, nt),
        out_shape=jax.ShapeDtypeStruct((N_CHIPS, rows, cols), BF16),
        in_specs=[spec, spec] + [ANY] * len(after), out_specs=pl.BlockSpec((None, tr, cols), lambda j, i: (j, i, 0)),
        compiler_params=_params(("parallel", "parallel")),
    )(mine, rsib, *after)


def _owner_sum(place, mine, rsib, rici, name, part=0, parts=1):
    _, half, cols = mine.shape
    rows = half // parts
    tr = min(rows, ROW_TILE)
    nt = rows // tr

    def body(place_ref, g_ref, r_ref, i_ref, o_ref):
        del place_ref
        acc = g_ref[...].astype(F32) + r_ref[...].astype(F32)
        for k in range(N_CHIPS - 1):
            acc = acc + i_ref[k].astype(F32)
        o_ref[...] = acc

    own = pl.BlockSpec((None, tr, cols), lambda i, p: (p[0], part * nt + i, 0))
    grid_spec = pltpu.PrefetchScalarGridSpec(
        num_scalar_prefetch=1, grid=(nt,),
        in_specs=[own, own, pl.BlockSpec((N_CHIPS - 1, tr, cols), lambda i, p: (0, i, 0))],
        out_specs=pl.BlockSpec((tr, cols), lambda i, p: (p[1] * (half // tr) + part * nt + i, 0)))
    return pl.pallas_call(
        body, name=name, grid_spec=grid_spec,
        out_shape=jax.ShapeDtypeStruct((2 * half, cols), F32),
        compiler_params=_params(("parallel",)),
    )(place, mine, rsib, rici)


def _adam_math(w, g, m, v):
    m2 = ADAM_B1 * m + (1.0 - ADAM_B1) * g
    v2 = ADAM_B2 * v + (1.0 - ADAM_B2) * (g * g)
    m_hat = m2 / (1.0 - ADAM_B1 ** ADAM_STEP)
    v_hat = v2 / (1.0 - ADAM_B2 ** ADAM_STEP)
    delta = -ADAM_LR * (m_hat / (jnp.sqrt(v_hat) + ADAM_EPS) + ADAM_WD * w)
    return delta, m2, v2


def _adamw(w, g, m, v, name, part=0, parts=1, prev=None):
    rows, cols = w.shape
    tr = min(rows, ROW_TILE)

    def body(*refs):
        w_ref, g_ref, m_ref, v_ref, go_ref, d_ref, m2_ref, v2_ref = refs[-8:]
        g = g_ref[...]
        go_ref[...] = g
        d_ref[...], m2_ref[...], v2_ref[...] = _adam_math(w_ref[...], g, m_ref[...], v_ref[...])

    if parts == 1:
        grid, spec = (rows // tr,), pl.BlockSpec((tr, cols), lambda i: (i, 0))
    else:
        per_half = rows // 2 // tr
        nt = per_half // parts
        grid, spec = (2, nt), pl.BlockSpec((tr, cols), lambda r, i: (r * per_half + part * nt + i, 0))
    olds = [] if prev is None else list(prev)
    return pl.pallas_call(
        body, name=name, grid=grid,
        out_shape=(jax.ShapeDtypeStruct(w.shape, F32),) * 4,
        in_specs=[HBM] * len(olds) + [spec] * 4, out_specs=(spec,) * 4,
        input_output_aliases={i: i for i in range(len(olds))},
        compiler_params=_params(("parallel",) * len(grid)),
    )(*olds, w, g, m, v)


def _ada_grad_adamw(c_all_t, dmod_cols, w, m, v):
    d, wa = w.shape
    tr = ROW_TILE

    def body(ct_ref, dm_ref, w_ref, m_ref, v_ref, g_ref, d_ref, m2_ref, v2_ref):
        act = _silu(ct_ref[...])
        g = act[:, 0:1] * dm_ref[0:1, :]
        for b in range(1, N_DEV):
            g = g + act[:, b:b + 1] * dm_ref[b:b + 1, :]
        g_ref[...] = g
        d_ref[...], m2_ref[...], v2_ref[...] = _adam_math(w_ref[...], g, m_ref[...], v_ref[...])

    spec = pl.BlockSpec((tr, wa), lambda i: (i, 0))
    return pl.pallas_call(
        body, name="ada_grad_adamw", grid=(d // tr,),
        out_shape=(jax.ShapeDtypeStruct(w.shape, F32),) * 4,
        in_specs=[pl.BlockSpec((tr, N_DEV), lambda i: (i, 0)), pl.BlockSpec((N_DEV, wa), lambda i: (0, 0)),
                  spec, spec, spec],
        out_specs=(spec,) * 4,
        compiler_params=_params(("parallel",)),
    )(c_all_t, dmod_cols, w, m, v)


def _small_update(place, gathered, pieces, weights, moments_m, moments_v):
    n = gathered.shape[1]
    k = len(weights)
    final_shapes = [w.shape for w in weights]
    row_counts = [s[1] if len(s) == 3 else 1 for s in final_shapes]
    weights, moments_m, moments_v = ([a.reshape(1, -1) for a in arrays] for arrays in (weights, moments_m, moments_v))

    def body(place_ref, g_ref, *refs):
        w_refs, m_refs, v_refs = refs[0:k], refs[k:2 * k], refs[2 * k:3 * k]
        outs = refs[3 * k:]
        total = g_ref[0:SUBLANES, :]
        for dev in range(1, N_DEV):
            total = total + g_ref[SUBLANES * dev:SUBLANES * (dev + 1), :]

        def flat(offset, length):
            segments, pos = [], offset
            while pos < offset + length:
                row, col = divmod(pos, n)
                take = min(offset + length - pos, n - col)
                segments.append(total[row:row + 1, col:col + take])
                pos += take
            return jnp.concatenate(segments, axis=1) if len(segments) > 1 else segments[0]

        chip = place_ref[0]
        for i, (w_ref, m_ref, v_ref) in enumerate(zip(w_refs, m_refs, v_refs)):
            g = flat(*pieces[i])
            if pieces[i][1] > w_ref.shape[1]:
                rows = row_counts[i]
                cols, full = w_ref.shape[1] // rows, pieces[i][1] // rows
                picked = []
                for r in range(rows):
                    blocks = [g[:, r * full + q * cols:r * full + (q + 1) * cols] for q in range(N_CHIPS)]
                    mine = blocks[N_CHIPS - 1]
                    for q in range(N_CHIPS - 2, -1, -1):
                        mine = jnp.where(chip == q, blocks[q], mine)
                    picked.append(mine)
                g = jnp.concatenate(picked, axis=1)
            delta, m2, v2 = _adam_math(w_ref[...], g, m_ref[...], v_ref[...])
            for j, val in enumerate((g, delta, m2, v2)):
                outs[j * k + i][...] = val
        outs[4 * k][...] = flat(*pieces[k])

    shapes = [jax.ShapeDtypeStruct(w.shape, F32) for w in weights]
    grid_spec = pltpu.PrefetchScalarGridSpec(
        num_scalar_prefetch=1, grid=(1,),
        in_specs=[pl.BlockSpec(gathered.shape, lambda i, p: (0, 0))]
        + [pl.BlockSpec(a.shape, functools.partial(lambda nd, i, p: (0,) * nd, a.ndim))
           for a in (*weights, *moments_m, *moments_v)],
        out_specs=tuple(pl.BlockSpec(s.shape, functools.partial(lambda nd, i, p: (0,) * nd, len(s.shape)))
                        for s in shapes * 4) + (pl.BlockSpec((1, LANES), lambda i, p: (0, 0)),))
    outs = pl.pallas_call(
        body, name="small_update", grid_spec=grid_spec,
        out_shape=tuple(shapes * 4) + (jax.ShapeDtypeStruct((1, LANES), F32),),
        compiler_params=_params(("arbitrary",)),
    )(place, gathered, *weights, *moments_m, *moments_v)
    shaped = [out.reshape(final_shapes[i % k]) for i, out in enumerate(outs[0:4 * k])]
    return shaped[0:k], shaped[k:2 * k], shaped[2 * k:3 * k], shaped[3 * k:4 * k], outs[4 * k]


def _pack_small(pieces):
    flat = [p.reshape(-1).astype(F32) for p in pieces]
    offsets, total = [], 0
    for p in flat:
        offsets.append(total)
        total += p.shape[0]
    padded = -(-total // SMALL_ALIGN) * SMALL_ALIGN
    if padded > total:
        flat.append(jnp.zeros((padded - total,), F32))
    return jnp.concatenate(flat).reshape(8, padded // 8), offsets


def _alibi_slope_rows(n_heads):
    slopes = 2.0 ** (-8.0 * jnp.arange(1, n_heads + 1, dtype=F32) / n_heads)
    rows = jnp.zeros((n_heads // 2, 8), F32).at[:, 0:2].set(slopes.reshape(n_heads // 2, 2))
    return jnp.broadcast_to(rows[:, :, None], (n_heads // 2, 8, ATT_KW))


def kernel(x, c, w_ada, b_ada, g_pre, w_in, conv_w, conv_b, g_conv, g_attn, w_out, g_post, loss_target, m_w_ada, m_b_ada, m_g_pre, m_w_in, m_conv_w, m_conv_b, m_g_conv, m_g_attn, m_w_out, m_g_post, v_w_ada, v_b_ada, v_g_pre, v_w_in, v_conv_w, v_conv_b, v_g_conv, v_g_attn, v_w_out, v_g_post):
    t, d = x.shape[1], x.shape[2]
    dc = conv_b.shape[1]
    da = g_attn.shape[1]
    hp = da // PAIR
    ws = w_in.shape[2]
    wa = w_ada.shape[2]
    cws = conv_w.shape[2]
    assert t % ROW_TILE == 0 and d % ROW_TILE == 0 and dc % COL_TILE == 0 and da % COL_TILE == 0
    assert ws == 2 * dc and dc == da and t // BRANCHES[-1][1] >= ATT_BQ

    mx, my, mc = _my_place()
    chip = _chip_of(mx, my)
    dev = 2 * chip + mc
    place = jnp.stack([chip, mc]).astype(jnp.int32)

    x2, tgt2 = x[0], loss_target[0]
    w_ada2, w_in2, w_out2 = w_ada[0], w_in[0], w_out[0]

    packed, offs = _pack_small([c[0], conv_w[0]])
    seen, mod, win_slots = _ada_modulation(packed, w_ada2, b_ada, d, w_in2)
    seen = seen.reshape(N_DEV, -1)
    c_all = seen[:, offs[0]:offs[0] + d]
    conv_w_full = seen[0::2, offs[1]:offs[1] + 3 * cws].reshape(N_CHIPS, 3, cws).transpose(1, 0, 2).reshape(3, dc)

    win_flight, send_in, recv_in, started = _gather_start(win_slots, mod)

    y_chip, x_chip, d_chip = (_chip_of(mx, 1 - my), _chip_of(1 - mx, my), _chip_of(1 - mx, 1 - my))
    tiles_per_part = ws // COL_TILE // 2

    def tiles_of(chunk, parts):
        return [(2 * chunk + part) * tiles_per_part + k for part in parts for k in range(tiles_per_part)]

    tiles = jnp.stack([jnp.stack(step) for step in (
        tiles_of(chip, (0, 1)), tiles_of(y_chip, (0,)) + tiles_of(x_chip, (1,)),
        tiles_of(y_chip, (1,)) + tiles_of(x_chip, (0,)), tiles_of(d_chip, (0, 1)))]).astype(jnp.int32)
    h, ht = _prenorm(x2, mod, g_pre, started)
    proj = _proj_tiles(None, h, w_in2, tiles, 0, "proj_own")
    win_flight, wout_flight, relay_send_in, relay_recv_in, send_out, recv_out = _gather_relay_in(
        win_flight, _cast_into_slot(place, w_out2, "cast_w_out", proj), recv_in, proj)
    win_flight = _forward_halves(win_flight, ((0, 0), (1, 1)), "forward_w_in_first")
    proj = _proj_tiles(proj, h, win_flight, tiles, 1, "proj_first_parts")
    win_flight = _forward_halves(
        _gather_wait_direct(win_flight, send_in, recv_in, proj, "gather_wait_w_in_direct"),
        ((0, 1), (1, 0)), "forward_w_in_second")
    proj = _proj_tiles(proj, h, win_flight, tiles, 2, "proj_second_parts")
    winf = _forward_halves(
        _gather_wait_relayed(win_flight, relay_send_in, relay_recv_in, proj, "gather_wait_w_in_relayed"),
        ((2, None),), "forward_w_in_relayed")
    proj = _proj_tiles(proj, h, winf, tiles, 3, "proj_diagonal")
    slopes = _alibi_slope_rows(da // HEAD_DIM)
    co = _conv_fwd(proj, conv_w_full, conv_b, dc)
    wout_flight, relay_send_out, relay_recv_out = _gather_relay_out(wout_flight, recv_out, co)
    o_mix, lse = _attn_fwd(proj, slopes, dc, da)
    g_attn_pairs = g_attn.reshape(hp, 1, PAIR)
    wout_flight = _gather_wait_direct(wout_flight, send_out, recv_out, o_mix, "gather_wait_w_out_direct")
    wout_flight = _gather_wait_relayed(wout_flight, relay_send_out, relay_recv_out, o_mix, "gather_wait_w_out_relayed")
    all_halves = ((0, None), (1, None), (2, None))
    wout_flight, fsend_out, frecv_out, forwarding = _forward_start(wout_flight, all_halves, "forward_w_out_start")
    ycat, ycat_t = _mix_fwd(co, proj, o_mix, g_conv, g_attn_pairs, forwarding)
    woutf = _forward_wait(wout_flight, all_halves, fsend_out, frecv_out, ycat, "forward_w_out_wait").reshape(dc + da, d)
    dout, dy, post_sums = _out_fwd_bwd(ycat, woutf, x2, tgt2, mod, g_post)

    gout, rsib_out = _dw_swapped(ycat_t, dy, N_CHIPS, 1, "dw_out")
    csum_out = _chip_sums(gout, rsib_out, "rs_chip_sum_out")
    ssem_out, rsem_out, csum_out, land_out, sent_out = _owners_start(csum_out, "rs_owners_start_out")
    dycat = _matmul_nt(dy, woutf, BF16, "dycat")
    dproj, dco, d_o, delta, dg_conv, dg_attn = _mix_bwd(dycat, co, proj, o_mix, g_conv, g_attn_pairs)
    dproj, conv_sums = _conv_bwd(dproj, dco, proj, conv_w_full, dc, sent_out)
    dproj = _attn_bwd(dproj, proj, d_o, lse, delta, slopes, dc, da, sent_out)
    gin, rsib_in = _dw_swapped(ht, dproj, 1, N_CHIPS, "dw_in")
    ssem_in0, rsem_in0, csum_in0, land_in0, sent_in0 = _owners_start(
        _chip_sums(gin, rsib_in, "rs_chip_sum_in0", 0, 2), "rs_owners_start_in0")
    ssem_in1, rsem_in1, csum_in1, land_in1, sent_in = _owners_start(
        _chip_sums(gin, rsib_in, "rs_chip_sum_in1", 1, 2, after=(sent_in0,)), "rs_owners_start_in1")
    dh = _dh(dproj, winf, sent_in)
    grad_x, pre_sums = _prenorm_bwd(x2, dh, dout, mod, g_pre)

    small, so = _pack_small([
        pre_sums[0], pre_sums[1], post_sums[0],
        pre_sums[2], conv_sums[0:3], conv_sums[3], dg_conv, dg_attn, post_sums[1], post_sums[2, 0:128]])
    ssem_small, rsem_small, small, land_small, sent_small = _allgather8_start(small, dev, "gather_small_start")

    rici_out = _owners_wait(ssem_out, rsem_out, csum_out, land_out, [grad_x, sent_small], "rs_owners_wait_out")
    full_out, jsend_out, jrecv_out, joining_out = _join_start(
        _owner_sum(place, gout, rsib_out, rici_out, "rs_owner_sum_out"), "rs_join_start_out", 0, 1)
    rici_in = _owners_wait(ssem_in0, rsem_in0, csum_in0, land_in0, [joining_out], "rs_owners_wait_in0")
    full_in0, jsend0, jrecv0, joining0 = _join_start(
        _owner_sum(place, gin, rsib_in, rici_in, "rs_owner_sum_in0", 0, 2), "rs_join_start_in0", 0, 2)
    grad_w_out = _join_wait(full_out, jsend_out, jrecv_out, [joining0], "rs_join_wait_out", 0, 1)
    grad_w_out, delta_w_out, new_m_w_out, new_v_w_out = _adamw(
        w_out2, grad_w_out, m_w_out[0], v_w_out[0], "adamw_w_out")
    full_in0 = _join_wait(full_in0, jsend0, jrecv0, [delta_w_out], "rs_join_wait_in0", 0, 2)
    updated_in = _adamw(w_in2, full_in0, m_w_in[0], v_w_in[0], "adamw_w_in0", 0, 2)
    rici_in = _owners_wait(ssem_in1, rsem_in1, csum_in1, land_in1, [updated_in[1]], "rs_owners_wait_in1")
    full_in1, jsend1, jrecv1, joining1 = _join_start(
        _owner_sum(place, gin, rsib_in, rici_in, "rs_owner_sum_in1", 1, 2), "rs_join_start_in1", 1, 2)

    small_seen = _allgather8_wait(ssem_small, rsem_small, small, land_small, [joining1], "gather_small_wait")
    small_w = [b_ada, g_pre, conv_w, conv_b, g_conv, g_attn, g_post]
    small_m = [m_b_ada, m_g_pre, m_conv_w, m_conv_b, m_g_conv, m_g_attn, m_g_post]
    small_v = [v_b_ada, v_g_pre, v_conv_w, v_conv_b, v_g_conv, v_g_attn, v_g_post]
    pieces = [(0, 3 * d), (so[3], d), (so[4], 3 * dc), (so[5], dc), (so[6], dc), (so[7], da), (so[8], d), (so[9], LANES)]
    g_small, d_small, m_small, v_small, loss_row = _small_update(place, small_seen, pieces, small_w, small_m, small_v)
    loss = loss_row[0, 0]
    grad_b_ada, grad_g_pre, grad_conv_w, grad_conv_b, grad_g_conv, grad_g_attn, grad_g_post = g_small
    dmod_cols = lax.dynamic_slice_in_dim(small_seen.reshape(N_DEV, -1), chip * wa, wa, axis=1)
    grad_w_ada, delta_w_ada, new_m_w_ada, new_v_w_ada = _ada_grad_adamw(c_all.T, dmod_cols, w_ada2, m_w_ada[0], v_w_ada[0])

    full_in1 = _join_wait(full_in1, jsend1, jrecv1, [delta_w_ada, d_small[0]], "rs_join_wait_in1", 1, 2)
    grad_w_in, delta_w_in, new_m_w_in, new_v_w_in = _adamw(
        w_in2, full_in1, m_w_in[0], v_w_in[0], "adamw_w_in1", 1, 2, updated_in)

    def lead(a):
        return a.reshape((1,) + a.shape)

    grads = [lead(grad_w_ada), grad_b_ada, grad_g_pre, lead(grad_w_in), grad_conv_w, grad_conv_b, grad_g_conv,
             grad_g_attn, lead(grad_w_out), grad_g_post]
    deltas = [lead(delta_w_ada), d_small[0], d_small[1], lead(delta_w_in), d_small[2], d_small[3], d_small[4],
              d_small[5], lead(delta_w_out), d_small[6]]
    new_ms = [lead(new_m_w_ada), m_small[0], m_small[1], lead(new_m_w_in), m_small[2], m_small[3], m_small[4],
              m_small[5], lead(new_m_w_out), m_small[6]]
    new_vs = [lead(new_v_w_ada), v_small[0], v_small[1], lead(new_v_w_in), v_small[2], v_small[3], v_small[4],
              v_small[5], lead(new_v_w_out), v_small[6]]
    return (loss, lead(grad_x), *grads, *deltas, *new_ms, *new_vs)
```

```python
import functools

import jax
import jax.numpy as jnp
import numpy as np
from jax import lax
from jax.experimental import pallas as pl
from jax.experimental.pallas import tpu as pltpu

F32 = jnp.float32
BF16 = jnp.bfloat16
MESH = pl.DeviceIdType.MESH
HBM = pl.BlockSpec(memory_space=pltpu.HBM)
VMEM = pl.BlockSpec(memory_space=pltpu.VMEM)
ANY = pl.BlockSpec(memory_space=pl.ANY)
SEM = pl.BlockSpec(memory_space=pltpu.SEMAPHORE)
EFFECT = pltpu.SideEffectType.DATAFLOW_SIDE_EFFECTING
SUBLANES, LANES = 8, 128
TOKEN = jax.ShapeDtypeStruct((SUBLANES, LANES), jnp.float32)

HEAD_DIM = 64
PAIR = 2 * HEAD_DIM
assert PAIR == LANES
BRANCHES = ((128, 1), (512, 4), (2048, 16))
SIDE = 64
EPS = 1e-6
NEG_INF = -1e30
N_CHIPS = 4
N_DEV = 8

ADAM_LR = 0.001
ADAM_B1 = 0.9
ADAM_B2 = 0.999
ADAM_EPS = 1e-08
ADAM_WD = 0.01
ADAM_STEP = 10

VMEM_LIMIT_BYTES = 56 * 1024 * 1024
ROW_TILE = 256
COL_TILE = 512
CONV_TILE = 256
ATT_BQ = 128
ATT_KW = ATT_BQ + 2 * SIDE
ATT_UNROLL = 4
SMALL_ALIGN = SUBLANES * LANES


def _params(semantics=None):
    kw = {"vmem_limit_bytes": VMEM_LIMIT_BYTES}
    if semantics is not None:
        kw["dimension_semantics"] = semantics
    return pltpu.CompilerParams(**kw)


def _silu(z):
    return z * jax.nn.sigmoid(z)


def _silu_grad(z):
    s = jax.nn.sigmoid(z)
    return s * (1.0 + z * (1.0 - s))


def _my_place():
    return lax.axis_index("x"), lax.axis_index("y"), lax.axis_index("c")


def _flip(a, bit):
    return 1 - a if bit else a


def _chip_of(x, y):
    return 2 * x + y


def _allgather8_start(v, me, name):
    rows_per, n = v.shape
    land = lax.dynamic_update_slice(jnp.zeros((N_DEV * rows_per, n), v.dtype), v, (me * rows_per, 0))

    def body(v_ref, land_ref, send_sems, recv_sems, v_thru, land_thru, token_ref):
        del v_thru, land_thru
        x, y, c = _my_place()
        mine = land_ref.at[pl.ds(pl.multiple_of((4 * x + 2 * y + c) * rows_per, rows_per), rows_per), :]
        for k in range(1, N_DEV):
            peer = (_flip(x, k & 4), _flip(y, k & 2), _flip(c, k & 1))
            pltpu.make_async_remote_copy(
                src_ref=v_ref, dst_ref=mine, send_sem=send_sems.at[k - 1], recv_sem=recv_sems.at[k - 1],
                device_id=peer, device_id_type=MESH).start()
        token_ref[...] = jnp.zeros(token_ref.shape, F32)

    sems = pltpu.SemaphoreType.DMA((N_DEV - 1,))
    return pl.pallas_call(
        body, name=name,
        out_shape=(sems, sems, jax.ShapeDtypeStruct(v.shape, v.dtype), jax.ShapeDtypeStruct(land.shape, land.dtype), TOKEN),
        in_specs=[HBM, HBM], out_specs=(SEM, SEM, HBM, HBM, VMEM),
        input_output_aliases={0: 2, 1: 3},
        compiler_params=pltpu.CompilerParams(has_side_effects=EFFECT),
    )(pltpu.with_memory_space_constraint(v, pltpu.HBM), pltpu.with_memory_space_constraint(land, pltpu.HBM))


def _allgather8_wait(send_sems, recv_sems, v, land, after, name):
    rows_per = v.shape[0]

    def body(v_ref, land_ref, send_ref, recv_ref, *rest):
        del rest
        x, y, c = _my_place()
        for k in range(1, N_DEV):
            peer = (_flip(x, k & 4), _flip(y, k & 2), _flip(c, k & 1))
            src = 4 * peer[0] + 2 * peer[1] + peer[2]
            cp = pltpu.make_async_remote_copy(
                src_ref=v_ref, dst_ref=land_ref.at[pl.ds(pl.multiple_of(src * rows_per, rows_per), rows_per), :],
                send_sem=send_ref.at[k - 1], recv_sem=recv_ref.at[k - 1], device_id=peer, device_id_type=MESH)
            cp.wait_send()
            cp.wait_recv()

    return pl.pallas_call(
        body, name=name,
        out_shape=(jax.ShapeDtypeStruct(v.shape, v.dtype), jax.ShapeDtypeStruct(land.shape, land.dtype)),
        in_specs=[HBM, HBM, SEM, SEM] + [ANY] * len(after), out_specs=(HBM, HBM),
        input_output_aliases={0: 0, 1: 1},
        compiler_params=pltpu.CompilerParams(has_side_effects=EFFECT),
    )(v, land, send_sems, recv_sems, *after)[1]


def _half_rows(ref, chip, which, half):
    return ref.at[chip, pl.ds(pl.multiple_of(which * half, half), half), :]


def _ici_peers(x, y, c):
    peers = [(_flip(x, k & 2), _flip(y, k & 1), c) for k in (1, 2, 3)]
    return [(peer, _chip_of(peer[0], peer[1])) for peer in peers]


def _part_of_half(ref, chip, core, part):
    half, cols = ref.shape[1] // 2, ref.shape[2] // 2
    return ref.at[chip, pl.ds(pl.multiple_of(core * half, half), half), pl.ds(part * cols, cols)]


def _neighbours(x, y, c):
    return [((x, 1 - y, c), _chip_of(x, 1 - y)), ((1 - x, y, c), _chip_of(1 - x, y)),
            ((1 - x, 1 - y, c), _chip_of(1 - x, 1 - y))]


def _start_direct(buf, send_sems, recv_sems):
    x, y, c = _my_place()
    me = _chip_of(x, y)
    for n, (peer, _) in enumerate(_neighbours(x, y, c)[0:2]):
        for part in ((0, 1), (1, 0))[n]:
            piece = _part_of_half(buf, me, c, part)
            pltpu.make_async_remote_copy(
                src_ref=piece, dst_ref=piece, send_sem=send_sems.at[2 * n + part], recv_sem=recv_sems.at[2 * n + part],
                device_id=peer, device_id_type=MESH).start()


def _relay(buf, recv_sems, relay_send, relay_recv):
    x, y, c = _my_place()
    nbrs = _neighbours(x, y, c)
    for n in range(2):
        part = n
        piece = _part_of_half(buf, nbrs[n][1], c, part)
        pltpu.make_async_remote_copy(
            src_ref=piece, dst_ref=piece, send_sem=relay_send.at[part], recv_sem=recv_sems.at[2 * n + part],
            device_id=nbrs[n][0], device_id_type=MESH).wait_recv()
        pltpu.make_async_remote_copy(
            src_ref=piece, dst_ref=piece, send_sem=relay_send.at[part], recv_sem=relay_recv.at[part],
            device_id=nbrs[1 - n][0], device_id_type=MESH).start()


def _gather_start(win_slots, after):
    def body(win_in, after_ref, win_ref, send_sems, recv_sems, token_ref):
        del win_in, after_ref
        _start_direct(win_ref, send_sems, recv_sems)
        token_ref[...] = jnp.zeros(token_ref.shape, F32)

    sems = pltpu.SemaphoreType.DMA((4,))
    return pl.pallas_call(
        body, name="gather_start",
        out_shape=(jax.ShapeDtypeStruct(win_slots.shape, win_slots.dtype), sems, sems, TOKEN),
        in_specs=[HBM, ANY], out_specs=(HBM, SEM, SEM, VMEM),
        input_output_aliases={0: 0},
        compiler_params=pltpu.CompilerParams(has_side_effects=EFFECT),
    )(win_slots, after)


def _gather_relay_in(win, wout_slots, recv_in, after):
    def body(win_in, wout_in, recv_in_ref, after_ref, win_ref, wout_ref, relay_send, relay_recv, send_out, recv_out):
        del win_in, wout_in, after_ref
        _relay(win_ref, recv_in_ref, relay_send, relay_recv)
        _start_direct(wout_ref, send_out, recv_out)

    two, four = pltpu.SemaphoreType.DMA((2,)), pltpu.SemaphoreType.DMA((4,))
    return pl.pallas_call(
        body, name="gather_relay_w_in",
        out_shape=(jax.ShapeDtypeStruct(win.shape, win.dtype), jax.ShapeDtypeStruct(wout_slots.shape, wout_slots.dtype),
                   two, two, four, four),
        in_specs=[HBM, HBM, SEM, ANY], out_specs=(HBM, HBM, SEM, SEM, SEM, SEM),
        input_output_aliases={0: 0, 1: 1},
        compiler_params=pltpu.CompilerParams(has_side_effects=EFFECT),
    )(win, wout_slots, recv_in, after)


def _gather_relay_out(wout, recv_out, after):
    def body(wout_in, recv_out_ref, after_ref, wout_ref, relay_send, relay_recv):
        del wout_in, after_ref
        _relay(wout_ref, recv_out_ref, relay_send, relay_recv)

    two = pltpu.SemaphoreType.DMA((2,))
    return pl.pallas_call(
        body, name="gather_relay_w_out",
        out_shape=(jax.ShapeDtypeStruct(wout.shape, wout.dtype), two, two),
        in_specs=[HBM, SEM, ANY], out_specs=(HBM, SEM, SEM),
        input_output_aliases={0: 0},
        compiler_params=pltpu.CompilerParams(has_side_effects=EFFECT),
    )(wout, recv_out, after)


def _gather_wait_direct(buf, send_sems, recv_sems, after, name):
    def body(buf_in, send_ref, recv_ref, after_ref, buf_ref):
        del buf_in, after_ref
        x, y, c = _my_place()
        me = _chip_of(x, y)
        for n, (peer, chip) in enumerate(_neighbours(x, y, c)[0:2]):
            second = 1 - n
            pltpu.make_async_remote_copy(
                src_ref=_part_of_half(buf_ref, me, c, second), dst_ref=_part_of_half(buf_ref, chip, c, second),
                send_sem=send_ref.at[2 * n + second], recv_sem=recv_ref.at[2 * n + second],
                device_id=peer, device_id_type=MESH).wait_recv()
            for part in range(2):
                piece = _part_of_half(buf_ref, me, c, part)
                pltpu.make_async_remote_copy(
                    src_ref=piece, dst_ref=piece, send_sem=send_ref.at[2 * n + part], recv_sem=recv_ref.at[2 * n + part],
                    device_id=peer, device_id_type=MESH).wait_send()

    return pl.pallas_call(
        body, name=name,
        out_shape=jax.ShapeDtypeStruct(buf.shape, buf.dtype),
        in_specs=[HBM, SEM, SEM, ANY], out_specs=HBM,
        input_output_aliases={0: 0},
        compiler_params=pltpu.CompilerParams(has_side_effects=EFFECT),
    )(buf, send_sems, recv_sems, after)


def _gather_wait_relayed(buf, relay_send, relay_recv, after, name):
    def body(buf_in, rsend_ref, rrecv_ref, after_ref, buf_ref):
        del buf_in, after_ref
        x, y, c = _my_place()
        nbrs = _neighbours(x, y, c)
        for n in range(2):
            relayed = _part_of_half(buf_ref, nbrs[n][1], c, n)
            cp = pltpu.make_async_remote_copy(
                src_ref=relayed, dst_ref=_part_of_half(buf_ref, nbrs[2][1], c, n),
                send_sem=rsend_ref.at[n], recv_sem=rrecv_ref.at[n], device_id=nbrs[1 - n][0], device_id_type=MESH)
            cp.wait_recv()
            cp.wait_send()

    return pl.pallas_call(
        body, name=name,
        out_shape=jax.ShapeDtypeStruct(buf.shape, buf.dtype),
        in_specs=[HBM, SEM, SEM, ANY], out_specs=HBM,
        input_output_aliases={0: 0},
        compiler_params=pltpu.CompilerParams(has_side_effects=EFFECT),
    )(buf, relay_send, relay_recv, after)


def _forward_copies(buf_ref, which, send_sems, recv_sems):
    half = buf_ref.shape[1] // 2
    x, y, c = _my_place()

    def copy(k, chip, core, part):
        piece = _half_rows(buf_ref, chip, core, half) if part is None else _part_of_half(buf_ref, chip, core, part)
        return pltpu.make_async_remote_copy(
            src_ref=piece, dst_ref=piece, send_sem=send_sems.at[k], recv_sem=recv_sems.at[k],
            device_id=(x, y, 1 - c), device_id_type=MESH)

    chips = [_neighbours(x, y, c)[n][1] for n, _ in which]
    return [(copy(k, chip, c, part), copy(k, chip, 1 - c, part)) for k, (chip, (_, part)) in enumerate(zip(chips, which))]


def _forward_halves(buf, which, name):
    def body(buf_in, buf_ref, send_sems, recv_sems):
        del buf_in
        copies = _forward_copies(buf_ref, which, send_sems, recv_sems)
        for mine, _ in copies:
            mine.start()
        for mine, theirs in copies:
            theirs.wait_recv()
        for mine, _ in copies:
            mine.wait_send()

    return pl.pallas_call(
        body, name=name,
        out_shape=jax.ShapeDtypeStruct(buf.shape, buf.dtype),
        in_specs=[HBM], out_specs=HBM,
        input_output_aliases={0: 0},
        scratch_shapes=[pltpu.SemaphoreType.DMA((len(which),))] * 2,
    )(buf)


def _forward_start(buf, which, name):
    def body(buf_in, buf_ref, send_sems, recv_sems, token_ref):
        del buf_in
        for mine, _ in _forward_copies(buf_ref, which, send_sems, recv_sems):
            mine.start()
        token_ref[...] = jnp.zeros(token_ref.shape, F32)

    sems = pltpu.SemaphoreType.DMA((len(which),))
    return pl.pallas_call(
        body, name=name,
        out_shape=(jax.ShapeDtypeStruct(buf.shape, buf.dtype), sems, sems, TOKEN),
        in_specs=[HBM], out_specs=(HBM, SEM, SEM, VMEM),
        input_output_aliases={0: 0},
        compiler_params=pltpu.CompilerParams(has_side_effects=EFFECT),
    )(buf)


def _forward_wait(buf, which, send_sems, recv_sems, after, name):
    def body(buf_in, send_ref, recv_ref, after_ref, buf_ref):
        del buf_in, after_ref
        for mine, theirs in _forward_copies(buf_ref, which, send_ref, recv_ref):
            theirs.wait_recv()
            mine.wait_send()

    return pl.pallas_call(
        body, name=name,
        out_shape=jax.ShapeDtypeStruct(buf.shape, buf.dtype),
        in_specs=[HBM, SEM, SEM, ANY], out_specs=HBM,
        input_output_aliases={0: 0},
        compiler_params=pltpu.CompilerParams(has_side_effects=EFFECT),
    )(buf, send_sems, recv_sems, after)


def _dw_swapped(a, b, row_chunks, col_chunks, name):
    r, t = a.shape
    c_all = b.shape[1]
    chunks = row_chunks * col_chunks
    rq, cq = r // row_chunks, c_all // col_chunks
    half = rq // 2
    tn = COL_TILE
    nt = cq // tn
    steps = col_chunks * nt

    def body(a_ref, b_ref, mine_ref, sib_ref, stage, send_sems, recv_sems):
        x, y, c = _my_place()
        j, n = pl.program_id(0), pl.program_id(1)
        step = j * nt + n
        slot = step % 2
        res = jnp.dot(a_ref[...], b_ref[...], preferred_element_type=F32).astype(BF16)

        def landing(jj, nn):
            cols = pl.ds(pl.multiple_of(nn * tn, tn), tn)
            return sib_ref.at[:, :, cols] if col_chunks == 1 else sib_ref.at[pl.ds(jj, 1), :, cols]

        def copy(slot_, step_, jj, nn):
            return pltpu.make_async_remote_copy(
                src_ref=stage.at[slot_], dst_ref=landing(jj, nn), send_sem=send_sems.at[slot_],
                recv_sem=recv_sems.at[step_], device_id=(x, y, 1 - c), device_id_type=MESH)

        @pl.when(step >= 2)
        def _():
            copy(slot, step, j, n).wait_send()

        for q in range(row_chunks):
            lo = res[q * rq:q * rq + half, :]
            hi = res[q * rq + half:(q + 1) * rq, :]
            mine_ref[q] = jnp.where(c == 0, lo, hi)
            stage[slot, q] = jnp.where(c == 0, hi, lo)
        copy(slot, step, j, n).start()

        @pl.when(step == steps - 1)
        def _():
            for s in range(max(steps - 2, 0), steps):
                copy(s % 2, s, j, n).wait_send()
            for s in range(steps):
                copy(s % 2, s, j, n).wait_recv()

    shape = jax.ShapeDtypeStruct((chunks, half, cq), BF16)
    return pl.pallas_call(
        body, name=name, grid=(col_chunks, nt),
        out_shape=(shape, shape),
        in_specs=[pl.BlockSpec((r, t), lambda j, n: (0, 0)), pl.BlockSpec((t, tn), lambda j, n: (0, j * nt + n))],
        out_specs=(pl.BlockSpec((row_chunks, half, tn), lambda j, n: (j, 0, n)), ANY),
        scratch_shapes=[pltpu.VMEM((2, row_chunks, half, tn), BF16), pltpu.SemaphoreType.DMA((2,)),
                        pltpu.SemaphoreType.DMA((steps,))],
        compiler_params=_params(("arbitrary", "arbitrary")),
    )(a, b)


def _owners_start(csum, name, after=()):
    land = pltpu.with_memory_space_constraint(lax.empty((N_CHIPS - 1,) + csum.shape[1:], csum.dtype), pltpu.HBM)

    def body(csum_ref, land_ref, *rest):
        send_sems, recv_sems, _, _, token_ref = rest[len(after):]
        x, y, c = _my_place()
        for k, (peer, owner) in enumerate(_ici_peers(x, y, c)):
            pltpu.make_async_remote_copy(
                src_ref=csum_ref.at[owner], dst_ref=land_ref.at[k], send_sem=send_sems.at[k], recv_sem=recv_sems.at[k],
                device_id=peer, device_id_type=MESH).start()
        token_ref[...] = jnp.zeros(token_ref.shape, F32)

    sems = pltpu.SemaphoreType.DMA((N_CHIPS - 1,))
    return pl.pallas_call(
        body, name=name,
        out_shape=(sems, sems, jax.ShapeDtypeStruct(csum.shape, csum.dtype),
                   jax.ShapeDtypeStruct(land.shape, land.dtype), TOKEN),
        in_specs=[HBM, HBM] + [ANY] * len(after), out_specs=(SEM, SEM, HBM, HBM, VMEM),
        input_output_aliases={0: 2, 1: 3},
        compiler_params=pltpu.CompilerParams(has_side_effects=EFFECT),
    )(pltpu.with_memory_space_constraint(csum, pltpu.HBM), land, *after)


def _owners_wait(send_sems, recv_sems, csum, land, after, name):
    def body(csum_ref, land_ref, send_ref, recv_ref, *rest):
        del rest
        x, y, c = _my_place()
        for k, (peer, owner) in enumerate(_ici_peers(x, y, c)):
            cp = pltpu.make_async_remote_copy(
                src_ref=csum_ref.at[owner], dst_ref=land_ref.at[k], send_sem=send_ref.at[k], recv_sem=recv_ref.at[k],
                device_id=peer, device_id_type=MESH)
            cp.wait_send()
            cp.wait_recv()

    return pl.pallas_call(
        body, name=name,
        out_shape=(jax.ShapeDtypeStruct(csum.shape, csum.dtype), jax.ShapeDtypeStruct(land.shape, land.dtype)),
        in_specs=[HBM, HBM, SEM, SEM] + [ANY] * len(after), out_specs=(HBM, HBM),
        input_output_aliases={0: 0, 1: 1},
        compiler_params=pltpu.CompilerParams(has_side_effects=EFFECT),
    )(csum, land, send_sems, recv_sems, *after)[1]


def _join_start(full, name, part, parts):
    half = full.shape[0] // 2
    rows = half // parts

    def body(full_in, full_ref, send_sem, recv_sem, token_ref):
        del full_in
        x, y, c = _my_place()
        mine = full_ref.at[pl.ds(pl.multiple_of(c * half + part * rows, rows), rows), :]
        pltpu.make_async_remote_copy(
            src_ref=mine, dst_ref=mine, send_sem=send_sem.at[0], recv_sem=recv_sem.at[0],
            device_id=(x, y, 1 - c), device_id_type=MESH).start()
        token_ref[...] = jnp.zeros(token_ref.shape, F32)

    one = pltpu.SemaphoreType.DMA((1,))
    return pl.pallas_call(
        body, name=name,
        out_shape=(jax.ShapeDtypeStruct(full.shape, full.dtype), one, one, TOKEN),
        in_specs=[HBM], out_specs=(HBM, SEM, SEM, VMEM),
        input_output_aliases={0: 0},
        compiler_params=pltpu.CompilerParams(has_side_effects=EFFECT),
    )(full)


def _join_wait(full, send_sem, recv_sem, after, name, part, parts):
    half = full.shape[0] // 2
    rows = half // parts

    def body(full_in, send_ref, recv_ref, *rest):
        del full_in
        full_ref = rest[-1]
        x, y, c = _my_place()
        cp = pltpu.make_async_remote_copy(
            src_ref=full_ref.at[pl.ds(pl.multiple_of(c * half + part * rows, rows), rows), :],
            dst_ref=full_ref.at[pl.ds(pl.multiple_of((1 - c) * half + part * rows, rows), rows), :],
            send_sem=send_ref.at[0], recv_sem=recv_ref.at[0], device_id=(x, y, 1 - c), device_id_type=MESH)
        cp.wait_send()
        cp.wait_recv()

    return pl.pallas_call(
        body, name=name,
        out_shape=jax.ShapeDtypeStruct(full.shape, full.dtype),
        in_specs=[HBM, SEM, SEM] + [ANY] * len(after), out_specs=HBM,
        input_output_aliases={0: 0},
        compiler_params=pltpu.CompilerParams(has_side_effects=EFFECT),
    )(full, send_sem, recv_sem, *after)


def _cast_into_slot(place, w, name, after):
    rows, cols = w.shape
    tr = min(rows, ROW_TILE)

    def body(place_ref, w_ref, after_ref, o_ref):
        del place_ref, after_ref
        o_ref[...] = w_ref[...].astype(BF16)

    grid_spec = pltpu.PrefetchScalarGridSpec(
        num_scalar_prefetch=1, grid=(rows // tr,),
        in_specs=[pl.BlockSpec((tr, cols), lambda i, p: (i, 0)), ANY],
        out_specs=pl.BlockSpec((None, tr, cols), lambda i, p: (p[0], i, 0)))
    return pl.pallas_call(
        body, name=name, grid_spec=grid_spec,
        out_shape=jax.ShapeDtypeStruct((N_CHIPS, rows, cols), BF16),
        compiler_params=_params(("parallel",)),
    )(place, w, after)


def _ada_modulation(packed, w_ada, b_ada, d, w_big):
    rows_per, n = packed.shape
    d_model, wa = w_ada.shape
    big_rows, big_cols = w_big.shape
    n_chunks = big_rows // ROW_TILE
    first_chunks = (2 * n_chunks) // 3

    def body(v_ref, w_hbm, b_ref, big_hbm, all_ref, mod_ref, slots_hbm, w_vmem, part_ref, parts_ref, wide, narrow,
             load_sem, send1, recv1, send2, recv2, in_sems, out_sems):
        x, y, c = _my_place()
        me = 4 * x + 2 * y + c
        chip = _chip_of(x, y)
        load = pltpu.make_async_copy(w_hbm, w_vmem, load_sem)
        load.start()

        def chunk_in(i):
            return pltpu.make_async_copy(big_hbm.at[i * ROW_TILE:(i + 1) * ROW_TILE, :], wide.at[i % 2], in_sems.at[i % 2])

        def chunk_out(i):
            return pltpu.make_async_copy(
                narrow.at[i % 2], slots_hbm.at[chip, i * ROW_TILE:(i + 1) * ROW_TILE, :], out_sems.at[i % 2])

        def cast_chunk(i):
            if i + 1 < n_chunks:
                chunk_in(i + 1).start()
            chunk_in(i).wait()
            if i >= 2:
                chunk_out(i - 2).wait()
            narrow[i % 2] = wide[i % 2].astype(BF16)
            chunk_out(i).start()

        chunk_in(0).start()

        def rows(idx):
            return all_ref.at[pl.ds(pl.multiple_of(idx * rows_per, rows_per), rows_per), :]

        all_ref[pl.ds(pl.multiple_of(me * rows_per, rows_per), rows_per), :] = v_ref[...]
        copies = []
        for k in range(1, N_DEV):
            peer = (_flip(x, k & 4), _flip(y, k & 2), _flip(c, k & 1))
            cp = pltpu.make_async_remote_copy(
                src_ref=v_ref, dst_ref=rows(me), send_sem=send1.at[k - 1], recv_sem=recv1.at[k - 1],
                device_id=peer, device_id_type=MESH)
            cp.start()
            copies.append((cp, peer))
        for i in range(first_chunks):
            cast_chunk(i)
        for k, (cp, peer) in enumerate(copies):
            pltpu.make_async_remote_copy(
                src_ref=v_ref, dst_ref=rows(4 * peer[0] + 2 * peer[1] + peer[2]), send_sem=send1.at[k],
                recv_sem=recv1.at[k], device_id=peer, device_id_type=MESH).wait_recv()
        for cp, _ in copies:
            cp.wait_send()

        def c_of(dev):
            segments, pos = [], 0
            while pos < d:
                row, col = divmod(pos, n)
                take = min(d - pos, n - col)
                segments.append(all_ref[dev * rows_per + row:dev * rows_per + row + 1, col:col + take])
                pos += take
            return jnp.concatenate(segments, axis=1)

        c_all = jnp.concatenate([c_of(dev) for dev in range(N_DEV)], axis=0)
        load.wait()
        part_ref[...] = jnp.dot(_silu(c_all), w_vmem[...], precision=lax.Precision.HIGHEST, preferred_element_type=F32)
        parts_ref[chip] = part_ref[...]
        swaps = []
        for k, (peer, _) in enumerate(_ici_peers(x, y, c)):
            cp = pltpu.make_async_remote_copy(
                src_ref=part_ref, dst_ref=parts_ref.at[chip], send_sem=send2.at[k], recv_sem=recv2.at[k],
                device_id=peer, device_id_type=MESH)
            cp.start()
            swaps.append(cp)
        for i in range(first_chunks, n_chunks):
            cast_chunk(i)
        for i in range(n_chunks - 2, n_chunks):
            chunk_out(i).wait()
        for k, (peer, peer_chip) in enumerate(_ici_peers(x, y, c)):
            pltpu.make_async_remote_copy(
                src_ref=part_ref, dst_ref=parts_ref.at[peer_chip], send_sem=send2.at[k], recv_sem=recv2.at[k],
                device_id=peer, device_id_type=MESH).wait_recv()
        for cp in swaps:
            cp.wait_send()
        flat = jnp.concatenate([parts_ref[j, pl.ds(me, 1), :] for j in range(N_CHIPS)], axis=1) + b_ref[...]
        mod_ref[...] = jnp.concatenate([flat[:, i * d:(i + 1) * d] for i in range(3)], axis=0)

    return pl.pallas_call(
        body, name="ada_modulation",
        out_shape=(jax.ShapeDtypeStruct((N_DEV * rows_per, n), F32), jax.ShapeDtypeStruct((3, d), F32),
                   jax.ShapeDtypeStruct((N_CHIPS, big_rows, big_cols), BF16)),
        in_specs=[VMEM, ANY, VMEM, ANY], out_specs=(VMEM, VMEM, ANY),
        scratch_shapes=[pltpu.VMEM((d_model, wa), F32), pltpu.VMEM((N_DEV, wa), F32),
                        pltpu.VMEM((N_CHIPS, N_DEV, wa), F32),
                        pltpu.VMEM((2, ROW_TILE, big_cols), F32), pltpu.VMEM((2, ROW_TILE, big_cols), BF16),
                        pltpu.SemaphoreType.DMA,
                        pltpu.SemaphoreType.DMA((N_DEV - 1,)), pltpu.SemaphoreType.DMA((N_DEV - 1,)),
                        pltpu.SemaphoreType.DMA((N_CHIPS - 1,)), pltpu.SemaphoreType.DMA((N_CHIPS - 1,)),
                        pltpu.SemaphoreType.DMA((2,)), pltpu.SemaphoreType.DMA((2,))],
        compiler_params=_params(),
    )(packed, w_ada, b_ada, w_big)


def _prenorm(x, mod, g_pre, after):
    t, d = x.shape
    tb = ROW_TILE

    def body(x_ref, mod_ref, g_ref, after_ref, h_ref, ht_ref):
        del after_ref
        xv = x_ref[...]
        r = lax.rsqrt(jnp.mean(xv * xv, axis=-1, keepdims=True) + EPS)
        h = (xv * r) * g_ref[...] * (1.0 + mod_ref[1:2, :]) + mod_ref[0:1, :]
        h_ref[...] = h.astype(BF16)
        ht_ref[...] = h.T.astype(BF16)

    return pl.pallas_call(
        body, name="prenorm", grid=(t // tb,),
        out_shape=(jax.ShapeDtypeStruct((t, d), BF16), jax.ShapeDtypeStruct((d, t), BF16)),
        in_specs=[pl.BlockSpec((tb, d), lambda i: (i, 0)), pl.BlockSpec((3, d), lambda i: (0, 0)),
                  pl.BlockSpec((1, d), lambda i: (0, 0)), ANY],
        out_specs=(pl.BlockSpec((tb, d), lambda i: (i, 0)), pl.BlockSpec((d, tb), lambda i: (0, i))),
        compiler_params=_params(("parallel",)),
    )(x, mod, g_pre, after)


def _proj_tiles(proj, h, w, tiles, step, name):
    t, d = h.shape
    ws = w.shape[-1]
    tn = COL_TILE
    nt = ws // tn

    def body(tile_ref, *refs):
        del tile_ref
        a_ref, b_ref, o_ref = refs[-3:]
        o_ref[...] = jnp.dot(a_ref[...], b_ref[...].astype(BF16), preferred_element_type=F32).astype(BF16)

    if w.ndim == 3:
        w_spec = pl.BlockSpec((None, d, tn), lambda i, tl: (tl[step, i] // nt, 0, tl[step, i] % nt))
    else:
        w_spec = pl.BlockSpec((d, tn), lambda i, tl: (0, tl[step, i] % nt))
    first = proj is None
    grid_spec = pltpu.PrefetchScalarGridSpec(
        num_scalar_prefetch=1, grid=(tiles.shape[1],),
        in_specs=([] if first else [HBM]) + [pl.BlockSpec((t, d), lambda i, tl: (0, 0)), w_spec],
        out_specs=pl.BlockSpec((t, tn), lambda i, tl: (0, tl[step, i])))
    return pl.pallas_call(
        body, name=name, grid_spec=grid_spec,
        out_shape=jax.ShapeDtypeStruct((t, N_CHIPS * ws), BF16),
        input_output_aliases={} if first else {1: 0},
        compiler_params=_params(("parallel",)),
    )(*([tiles] if first else [tiles, proj]), h, w)


def _shift_rows(a, rows):
    idx = lax.broadcasted_iota(jnp.int32, a.shape, 0)
    prev = jnp.where(idx == 0, 0.0, pltpu.roll(a, 1, 0))
    nxt = jnp.where(idx == rows - 1, 0.0, pltpu.roll(a, rows - 1, 0))
    return prev, nxt


def _conv_fwd(conv_proj, conv_w, conv_b, dc):
    t = conv_proj.shape[0]
    ct = CONV_TILE
    nct = dc // ct

    def body(u_ref, cg_ref, w_ref, b_ref, co_ref):
        a = cg_ref[...].astype(F32) * u_ref[...].astype(F32)
        prev, nxt = _shift_rows(a, t)
        co_ref[...] = (w_ref[0:1, :] * prev + w_ref[1:2, :] * a + w_ref[2:3, :] * nxt + b_ref[...]).astype(BF16)

    return pl.pallas_call(
        body, name="conv_fwd", grid=(nct,),
        out_shape=jax.ShapeDtypeStruct((t, dc), BF16),
        in_specs=[pl.BlockSpec((t, ct), lambda i: (0, i)), pl.BlockSpec((t, ct), lambda i: (0, 2 * nct + i)),
                  pl.BlockSpec((3, ct), lambda i: (0, i)), pl.BlockSpec((1, ct), lambda i: (0, i))],
        out_specs=pl.BlockSpec((t, ct), lambda i: (0, i)),
        compiler_params=_params(("parallel",)),
    )(conv_proj, conv_proj, conv_w, conv_b)


def _to_residue_major(src_ref, dst_ref, r):
    seq = src_ref.shape[0] // r
    for res in range(r):
        dst_ref[res * seq:(res + 1) * seq, :] = src_ref[pl.ds(res, seq, stride=r), :].astype(dst_ref.dtype)


def _branch_operands(token_refs, stage, dil, r):
    if r == 1:
        return list(token_refs)
    for i, ref in enumerate(token_refs):
        stage[...] = ref[...].astype(F32)
        _to_residue_major(stage, dil.at[i], r)
    return [dil.at[i] for i in range(len(token_refs))]


def _scaled_queries(q):
    return (q.astype(F32) * (HEAD_DIM ** -0.5)).astype(BF16)


BLOCK_SHIFTS = (0, -SIDE, None)


def _band_bias(rel, slope):
    arel = jnp.abs(rel)
    return jnp.where(arel <= SIDE, arel.astype(F32) * slope, NEG_INF)


def _fill_bias_tiles(bias_ref, sl_ref, r, kw):
    base = lax.broadcasted_iota(jnp.int32, (ATT_BQ, kw), 1) - lax.broadcasted_iota(jnp.int32, (ATT_BQ, kw), 0)
    for hh in range(2):
        slope = -(sl_ref[hh:hh + 1, 0:kw] * float(r))
        for e, shift in enumerate(BLOCK_SHIFTS):
            shift = ATT_BQ - kw if shift is None else shift
            bias_ref[hh, e, :, 0:kw] = _band_bias(base + shift, slope)


def _fill_stacked_bias_tiles(bias_ref, sl_ref, r, kw):
    base = lax.broadcasted_iota(jnp.int32, (kw, ATT_BQ), 0) - lax.broadcasted_iota(jnp.int32, (kw, ATT_BQ), 1)
    for hh in range(2):
        slope = -(sl_ref[hh:hh + 1, 0:ATT_BQ] * float(r))
        for e, shift in enumerate(BLOCK_SHIFTS):
            shift = ATT_BQ - kw if shift is None else shift
            bias_ref[e, 0:kw, hh * ATT_BQ:(hh + 1) * ATT_BQ] = _band_bias(base + shift, slope)


def _first_head_lanes():
    return lax.broadcasted_iota(jnp.int32, (1, PAIR), 1) < HEAD_DIM


def _only_head(x, first, hh):
    return jnp.where(first if hh == 0 else jnp.logical_not(first), x, jnp.zeros_like(x))


def _block_place(g, seq_len, kw):
    nqb = seq_len // ATT_BQ
    if nqb == 1:
        row = pl.multiple_of(g * ATT_BQ, ATT_BQ)
        return row, row, 0
    res = g // nqb
    qb = g - res * nqb
    q0 = qb * ATT_BQ
    ks = jnp.clip(q0 - SIDE, 0, seq_len - kw)
    edge = jnp.where(qb == 0, 0, jnp.where(qb == nqb - 1, 2, 1))
    return (pl.multiple_of(res * seq_len + q0, ATT_BQ), pl.multiple_of(res * seq_len + ks, SIDE), edge)


def _qkv_specs(dc, da, t, index):
    return [pl.BlockSpec((t, PAIR), functools.partial(index, (4 * dc + comp * da) // PAIR)) for comp in range(3)]


def _attn_fwd(proj, slopes, dc, da):
    t = proj.shape[0]
    hp = da // PAIR
    n_blocks = t // ATT_BQ

    def body(q_ref, k_ref, v_ref, sl_ref, o_ref, lse_ref, stage, dil, bias, o_res, l_res, o_tok, l_tok):
        for b, (_, r) in enumerate(BRANCHES):
            seq_len = t // r
            kw = min(ATT_KW, seq_len)
            ops = _branch_operands([q_ref, k_ref, v_ref], stage, dil, r)
            _fill_bias_tiles(bias, sl_ref, r, kw)
            o_dst, l_dst = (o_tok.at[b], l_tok.at[b]) if r == 1 else (o_res, l_res)
            first = _first_head_lanes()

            def blocks(trip, carry, seq_len=seq_len, kw=kw, o_dst=o_dst, l_dst=l_dst, first=first, ops=ops):
                nt = (((1,), (1,)), ((), ()))
                places = [_block_place(trip * ATT_UNROLL + i, seq_len, kw) for i in range(ATT_UNROLL)]
                chains = [(i, hh) for i in range(ATT_UNROLL) for hh in range(2)]
                qs = [_scaled_queries(ops[0][pl.ds(qrow, ATT_BQ), :]) for qrow, _, _ in places]
                ks = [ops[1][pl.ds(krow, kw), :] for _, krow, _ in places]
                vs = [ops[2][pl.ds(krow, kw), :] for _, krow, _ in places]
                ss = [lax.dot_general(_only_head(qs[i], first, hh), ks[i], nt, preferred_element_type=F32)
                      + bias[hh, places[i][2], :, 0:kw] for i, hh in chains]
                tops = [jnp.max(s, axis=-1, keepdims=True) for s in ss]
                ps = [jnp.exp(s - m) for s, m in zip(ss, tops)]
                dens = [jnp.sum(p, axis=-1, keepdims=True) for p in ps]
                for i, (qrow, _, _) in enumerate(places):
                    weights = jnp.concatenate([ps[2 * i].astype(BF16), ps[2 * i + 1].astype(BF16)], axis=1)
                    values = jnp.concatenate([_only_head(vs[i], first, 0), _only_head(vs[i], first, 1)], axis=0)
                    den = jnp.where(first, dens[2 * i], dens[2 * i + 1])
                    o_dst[pl.ds(qrow, ATT_BQ), :] = jnp.dot(weights, values, preferred_element_type=F32) / den
                    l_dst[pl.ds(qrow, ATT_BQ), :] = jnp.where(first, tops[2 * i], tops[2 * i + 1]) + jnp.log(den)
                return carry

            lax.fori_loop(0, n_blocks // ATT_UNROLL, blocks, 0)
            if r > 1:
                for res in range(r):
                    rows = slice(res * seq_len, (res + 1) * seq_len)
                    o_tok[b, pl.ds(res, seq_len, stride=r), :] = o_res[rows, :]
                    l_tok[b, pl.ds(res, seq_len, stride=r), :] = l_res[rows, :]

        def merge(i, carry):
            rows = pl.ds(pl.multiple_of(i * ROW_TILE, ROW_TILE), ROW_TILE)
            la, lb, lc = l_tok[0, rows, :], l_tok[1, rows, :], l_tok[2, rows, :]
            m = jnp.maximum(jnp.maximum(la, lb), lc)
            wa, wb, wc = jnp.exp(la - m), jnp.exp(lb - m), jnp.exp(lc - m)
            den = wa + wb + wc
            o_ref[rows, :] = (wa * o_tok[0, rows, :] + wb * o_tok[1, rows, :] + wc * o_tok[2, rows, :]) * (1.0 / den)
            lse_ref[rows, :] = m + jnp.log(den)
            return carry

        lax.fori_loop(0, t // ROW_TILE, merge, 0)

    pair_spec = pl.BlockSpec((None, t, PAIR), lambda h: (h, 0, 0))
    return pl.pallas_call(
        body, name="attn_fwd", grid=(hp,),
        out_shape=(jax.ShapeDtypeStruct((hp, t, PAIR), F32), jax.ShapeDtypeStruct((hp, t, PAIR), F32)),
        in_specs=_qkv_specs(dc, da, t, lambda first, h: (0, first + h))
        + [pl.BlockSpec((None, 8, ATT_KW), lambda h: (h, 0, 0))],
        out_specs=(pair_spec, pair_spec),
        scratch_shapes=[pltpu.VMEM((t, PAIR), F32), pltpu.VMEM((3, t, PAIR), BF16),
                        pltpu.VMEM((2, 3, ATT_BQ, ATT_KW), F32),
                        pltpu.VMEM((t, PAIR), F32), pltpu.VMEM((t, PAIR), F32),
                        pltpu.VMEM((3, t, PAIR), F32), pltpu.VMEM((3, t, PAIR), F32)],
        compiler_params=_params(("parallel",)),
    )(proj, proj, proj, slopes)


def _attn_bwd(dproj, proj, d_o, lse, delta, slopes, dc, da, after):
    t = proj.shape[0]
    hp = da // PAIR
    n_blocks = t // ATT_BQ

    def all_branches(q_ref, k_ref, v_ref, do_ref, lse_ref, dl_ref, sl_ref,
                     stage, dil, packed, packed_res, row_vecs, bias_t, acc, tot):
        first = _first_head_lanes()
        lane = lax.broadcasted_iota(jnp.int32, (1, PAIR), 1)
        packed[...] = jnp.where((lane & (HEAD_DIM - 1)) < HEAD_DIM // 2, lse_ref[...], dl_ref[...])
        for b, (_, r) in enumerate(BRANCHES):
            seq_len = t // r
            kw = min(ATT_KW, seq_len)
            ops = _branch_operands([q_ref, k_ref, v_ref, do_ref], stage, dil, r)
            scalars = packed
            if r > 1:
                _to_residue_major(packed, packed_res, r)
                scalars = packed_res
            for g in range(n_blocks):
                flipped = scalars[g * ATT_BQ:(g + 1) * ATT_BQ, :].T
                for row in range(4):
                    row_vecs[g, row:row + 1, :] = flipped[row * (HEAD_DIM // 2):row * (HEAD_DIM // 2) + 1, :]
            _fill_stacked_bias_tiles(bias_t, sl_ref, r, kw)
            acc[1] = jnp.zeros((t, PAIR), F32)
            acc[2] = jnp.zeros((t, PAIR), F32)

            def blocks(trip, carry, seq_len=seq_len, kw=kw, ops=ops):
                nt = (((1,), (1,)), ((), ()))
                group = range(ATT_UNROLL)
                places = [_block_place(trip * ATT_UNROLL + i, seq_len, kw) for i in group]
                ks, vs, q2s, do2s, lse2s, dl2s = [], [], [], [], [], []
                for i, (qrow, krow, _) in zip(group, places):
                    q = _scaled_queries(ops[0][pl.ds(qrow, ATT_BQ), :])
                    dov = ops[3][pl.ds(qrow, ATT_BQ), :]
                    ks.append(ops[1][pl.ds(krow, kw), :])
                    vs.append(ops[2][pl.ds(krow, kw), :])
                    q2s.append(jnp.concatenate([_only_head(q, first, 0), _only_head(q, first, 1)], axis=0))
                    do2s.append(jnp.concatenate([_only_head(dov, first, 0), _only_head(dov, first, 1)], axis=0))
                    rows = row_vecs[trip * ATT_UNROLL + i]
                    lse2s.append(jnp.concatenate([rows[0:1, :], rows[2:3, :]], axis=1))
                    dl2s.append(jnp.concatenate([rows[1:2, :], rows[3:4, :]], axis=1))
                s_ts = [lax.dot_general(ks[i], q2s[i], nt, preferred_element_type=F32) for i in group]
                dp_ts = [lax.dot_general(vs[i], do2s[i], nt, preferred_element_type=F32) for i in group]
                p_ts = [jnp.exp(s_ts[i] + bias_t[places[i][2], 0:kw, :] - lse2s[i]) for i in group]
                ds_ts = [p_ts[i] * (dp_ts[i] - dl2s[i]) for i in group]
                dvs = [jnp.dot(p_ts[i].astype(BF16), do2s[i], preferred_element_type=F32) for i in group]
                dks = [jnp.dot(ds_ts[i].astype(BF16), q2s[i], preferred_element_type=F32) for i in group]
                dss = [ds_ts[i].T.astype(BF16) for i in group]
                dqs = [jnp.dot(dss[i][0:ATT_BQ, :], _only_head(ks[i], first, 0), preferred_element_type=F32)
                       + jnp.dot(dss[i][ATT_BQ:2 * ATT_BQ, :], _only_head(ks[i], first, 1), preferred_element_type=F32)
                       for i in group]
                for i, (qrow, krow, _) in zip(group, places):
                    acc[0, pl.ds(qrow, ATT_BQ), :] = dqs[i] * (HEAD_DIM ** -0.5)
                    acc[1, pl.ds(krow, kw), :] += dks[i]
                    acc[2, pl.ds(krow, kw), :] += dvs[i]
                return carry

            lax.fori_loop(0, n_blocks // ATT_UNROLL, blocks, 0)
            for comp in range(3):
                if r == 1:
                    tot[comp] = acc[comp]
                else:
                    for res in range(r):
                        tok = pl.ds(res, seq_len, stride=r)
                        tot[comp, tok, :] = tot[comp, tok, :] + acc[comp, res * seq_len:(res + 1) * seq_len, :]

    first_q = (4 * dc) // PAIR

    def body(dproj_in, q_ref, k_ref, v_ref, do_ref, lse_ref, dl_ref, sl_ref, after_ref, out_ref, *scratch):
        del dproj_in, after_ref
        work, out_stage, out_sems = scratch[:-2], scratch[-2], scratch[-1]
        h = pl.program_id(0)
        all_branches(q_ref, k_ref, v_ref, do_ref, lse_ref, dl_ref, sl_ref, *work)

        def out_copy(comp):
            cols = pl.ds(pl.multiple_of((first_q + comp * hp + h) * PAIR, PAIR), PAIR)
            return pltpu.make_async_copy(out_stage.at[comp], out_ref.at[:, cols], out_sems.at[comp])

        @pl.when(h > 0)
        def _():
            for comp in range(3):
                out_copy(comp).wait()

        for comp in range(3):
            out_stage[comp] = work[-1][comp].astype(BF16)
            out_copy(comp).start()

        @pl.when(h == hp - 1)
        def _():
            for comp in range(3):
                out_copy(comp).wait()

    pair_spec = pl.BlockSpec((None, t, PAIR), lambda h: (h, 0, 0))
    return pl.pallas_call(
        body, name="attn_bwd", grid=(hp,),
        out_shape=jax.ShapeDtypeStruct(dproj.shape, BF16),
        in_specs=[HBM] + _qkv_specs(dc, da, t, lambda first, h: (0, first + h))
        + [pair_spec, pair_spec, pair_spec, pl.BlockSpec((None, 8, ATT_KW), lambda h: (h, 0, 0)), ANY],
        out_specs=ANY,
        input_output_aliases={0: 0},
        scratch_shapes=[pltpu.VMEM((t, PAIR), F32), pltpu.VMEM((4, t, PAIR), BF16),
                        pltpu.VMEM((t, PAIR), F32), pltpu.VMEM((t, PAIR), F32),
                        pltpu.VMEM((n_blocks, 8, ATT_BQ), F32), pltpu.VMEM((3, ATT_KW, 2 * ATT_BQ), F32),
                        pltpu.VMEM((3, t, PAIR), F32), pltpu.VMEM((3, t, PAIR), F32),
                        pltpu.VMEM((3, t, PAIR), BF16), pltpu.SemaphoreType.DMA((3,))],
        compiler_params=_params(("arbitrary",)),
    )(dproj, proj, proj, proj, d_o, lse, delta, slopes, after)


def _mix_fwd(co, proj, o_mix, g_conv, g_attn_pairs, after):
    t, dc = co.shape
    hp = o_mix.shape[0]
    da = hp * PAIR
    tb = ROW_TILE

    def body(co_ref, bg_ref, zc_ref, za_ref, om_ref, gc_ref, ga_ref, after_ref, ycat_ref, ycatt_ref):
        del after_ref
        p = bg_ref[...].astype(F32) * co_ref[...].astype(F32)
        rc = lax.rsqrt(jnp.mean(p * p, axis=-1, keepdims=True) + EPS)
        yc = (p * rc) * gc_ref[...] * _silu(zc_ref[...].astype(F32))
        ycat_ref[:, 0:dc] = yc.astype(BF16)
        ycatt_ref[0:dc, :] = yc.T.astype(BF16)
        ssq = jnp.zeros((tb, 1), F32)
        for h in range(hp):
            o = om_ref[h]
            ssq = ssq + jnp.sum(o * o, axis=-1, keepdims=True)
        ra = lax.rsqrt(ssq * (1.0 / da) + EPS)
        for h in range(hp):
            ya = (om_ref[h] * ra) * ga_ref[h] * _silu(za_ref[:, h * PAIR:(h + 1) * PAIR].astype(F32))
            ycat_ref[:, dc + h * PAIR:dc + (h + 1) * PAIR] = ya.astype(BF16)
            ycatt_ref[dc + h * PAIR:dc + (h + 1) * PAIR, :] = ya.T.astype(BF16)

    pair_spec = pl.BlockSpec((hp, tb, PAIR), lambda i: (0, i, 0))
    return pl.pallas_call(
        body, name="mix_fwd", grid=(t // tb,),
        out_shape=(jax.ShapeDtypeStruct((t, dc + da), BF16), jax.ShapeDtypeStruct((dc + da, t), BF16)),
        in_specs=[pl.BlockSpec((tb, dc), lambda i: (i, 0)),
                  pl.BlockSpec((tb, dc), lambda i: (i, 1)),
                  pl.BlockSpec((tb, dc), lambda i: (i, 3)),
                  pl.BlockSpec((tb, da), lambda i: (i, 7)),
                  pair_spec,
                  pl.BlockSpec((1, dc), lambda i: (0, 0)),
                  pl.BlockSpec((hp, 1, PAIR), lambda i: (0, 0, 0)), ANY],
        out_specs=(pl.BlockSpec((tb, dc + da), lambda i: (i, 0)), pl.BlockSpec((dc + da, tb), lambda i: (0, i))),
        compiler_params=_params(("parallel",)),
    )(co, proj, proj, proj, o_mix, g_conv, g_attn_pairs, after)


def _out_fwd_bwd(ycat, woutf, x, target, mod, g_post):
    t, d = x.shape
    n = ycat.shape[1]
    tb = ROW_TILE

    def body(a_ref, w_ref, x_ref, tg_ref, mod_ref, g_ref, dout_ref, dy_ref, acc_ref):
        y = jnp.dot(a_ref[...], w_ref[...], preferred_element_type=F32)
        r = lax.rsqrt(jnp.mean(y * y, axis=-1, keepdims=True) + EPS)
        nh = y * r
        gate = mod_ref[2:3, :]
        nrm = nh * g_ref[...]
        err = x_ref[...] + gate * nrm - tg_ref[...]
        dout = err * (1.0 / d)
        dout_ref[...] = dout.astype(BF16)
        dn = dout * gate
        a = dn * g_ref[...]
        dy = r * (a - nh * jnp.mean(a * nh, axis=-1, keepdims=True))
        dy_ref[...] = dy.astype(BF16)
        loss = 0.5 * jnp.sum(jnp.sum(err * err, axis=-1, keepdims=True) * (1.0 / d), axis=0, keepdims=True)
        part = jnp.concatenate(
            [jnp.sum(dout * nrm, axis=0, keepdims=True), jnp.sum(dn * nh, axis=0, keepdims=True),
             jnp.broadcast_to(loss, (1, d)), jnp.zeros((5, d), F32)], axis=0)

        @pl.when(pl.program_id(0) == 0)
        def _():
            acc_ref[...] = jnp.zeros(acc_ref.shape, F32)

        acc_ref[...] += part

    return pl.pallas_call(
        body, name="out_fwd_bwd", grid=(t // tb,),
        out_shape=(jax.ShapeDtypeStruct((t, d), BF16), jax.ShapeDtypeStruct((t, d), BF16),
                   jax.ShapeDtypeStruct((8, d), F32)),
        in_specs=[pl.BlockSpec((tb, n), lambda i: (i, 0)), pl.BlockSpec((n, d), lambda i: (0, 0)),
                  pl.BlockSpec((tb, d), lambda i: (i, 0)), pl.BlockSpec((tb, d), lambda i: (i, 0)),
                  pl.BlockSpec((3, d), lambda i: (0, 0)), pl.BlockSpec((1, d), lambda i: (0, 0))],
        out_specs=(pl.BlockSpec((tb, d), lambda i: (i, 0)), pl.BlockSpec((tb, d), lambda i: (i, 0)),
                   pl.BlockSpec((8, d), lambda i: (0, 0))),
        compiler_params=_params(("arbitrary",)),
    )(ycat, woutf, x, target, mod, g_post)


def _matmul_nt(a, b, out_dtype, name):
    m, k = a.shape
    n = b.shape[0]
    tn = COL_TILE

    def body(a_ref, b_ref, o_ref):
        o_ref[...] = lax.dot_general(a_ref[...], b_ref[...], (((1,), (1,)), ((), ())),
                                     preferred_element_type=F32).astype(out_dtype)

    return pl.pallas_call(
        body, name=name, grid=(n // tn,),
        out_shape=jax.ShapeDtypeStruct((m, n), out_dtype),
        in_specs=[pl.BlockSpec((m, k), lambda i: (0, 0)), pl.BlockSpec((tn, k), lambda i: (i, 0))],
        out_specs=pl.BlockSpec((m, tn), lambda i: (0, i)),
        compiler_params=_params(("parallel",)),
    )(a, b)


def _mix_bwd(dycat, co, proj, o_mix, g_conv, g_attn_pairs):
    t, dc = co.shape
    hp = o_mix.shape[0]
    da = hp * PAIR
    tb = ROW_TILE

    def body(dy_ref, co_ref, bg_ref, zc_ref, za_ref, om_ref, gc_ref, ga_ref,
             dcp_ref, dco_ref, do_ref, dl_ref, dgc_ref, dga_ref):
        first = pl.program_id(0) == 0
        cov = co_ref[...].astype(F32)
        bg = bg_ref[...].astype(F32)
        zc = zc_ref[...].astype(F32)
        p = bg * cov
        rc = lax.rsqrt(jnp.mean(p * p, axis=-1, keepdims=True) + EPS)
        nh = p * rc
        dyc = dy_ref[:, 0:dc].astype(F32)
        dn = dyc * _silu(zc)
        a = dn * gc_ref[...]
        dp = rc * (a - nh * jnp.mean(a * nh, axis=-1, keepdims=True))
        dcp_ref[:, 0:dc] = jnp.zeros((tb, dc), BF16)
        dcp_ref[:, dc:2 * dc] = (dp * cov).astype(BF16)
        dcp_ref[:, 2 * dc:3 * dc] = jnp.zeros((tb, dc), BF16)
        dcp_ref[:, 3 * dc:4 * dc] = (dyc * nh * gc_ref[...] * _silu_grad(zc)).astype(BF16)
        dcp_ref[:, 4 * dc:4 * dc + 3 * da] = jnp.zeros((tb, 3 * da), BF16)
        dco_ref[...] = dp * bg

        @pl.when(first)
        def _():
            dgc_ref[...] = jnp.zeros(dgc_ref.shape, F32)
            dga_ref[...] = jnp.zeros(dga_ref.shape, F32)

        dgc_ref[...] += jnp.sum(dn * nh, axis=0, keepdims=True)

        ssq = jnp.zeros((tb, 1), F32)
        for h in range(hp):
            o = om_ref[h]
            ssq = ssq + jnp.sum(o * o, axis=-1, keepdims=True)
        ra = lax.rsqrt(ssq * (1.0 / da) + EPS)
        dot_an = jnp.zeros((tb, 1), F32)
        for h in range(hp):
            nha = om_ref[h] * ra
            za = za_ref[:, h * PAIR:(h + 1) * PAIR].astype(F32)
            dya = dy_ref[:, dc + h * PAIR:dc + (h + 1) * PAIR].astype(F32)
            dna = dya * _silu(za)
            dza = (dya * nha * ga_ref[h] * _silu_grad(za)).astype(BF16)
            dcp_ref[:, 4 * dc + 3 * da + h * PAIR:4 * dc + 3 * da + (h + 1) * PAIR] = dza
            dga_ref[h] += jnp.sum(dna * nha, axis=0, keepdims=True)
            dot_an = dot_an + jnp.sum(dna * ga_ref[h] * nha, axis=-1, keepdims=True)
        mean_an = dot_an * (1.0 / da)
        first_head = lax.broadcasted_iota(jnp.int32, (tb, PAIR), 1) < HEAD_DIM
        for h in range(hp):
            o = om_ref[h]
            nha = o * ra
            za = za_ref[:, h * PAIR:(h + 1) * PAIR].astype(F32)
            dya = dy_ref[:, dc + h * PAIR:dc + (h + 1) * PAIR].astype(F32)
            aa = dya * _silu(za) * ga_ref[h]
            d_o = ra * (aa - nha * mean_an)
            do_ref[h] = d_o.astype(BF16)
            prod = d_o * o
            both = jnp.sum(prod, axis=-1, keepdims=True)
            head0 = jnp.sum(jnp.where(first_head, prod, 0.0), axis=-1, keepdims=True)
            dl_ref[h] = jnp.where(first_head, head0, both - head0)

    pair_spec = pl.BlockSpec((hp, tb, PAIR), lambda i: (0, i, 0))
    return pl.pallas_call(
        body, name="mix_bwd", grid=(t // tb,),
        out_shape=(jax.ShapeDtypeStruct((t, 4 * dc + 4 * da), BF16), jax.ShapeDtypeStruct((t, dc), F32),
                   jax.ShapeDtypeStruct((hp, t, PAIR), BF16), jax.ShapeDtypeStruct((hp, t, PAIR), F32),
                   jax.ShapeDtypeStruct((1, dc), F32), jax.ShapeDtypeStruct((hp, 1, PAIR), F32)),
        in_specs=[pl.BlockSpec((tb, dc + da), lambda i: (i, 0)),
                  pl.BlockSpec((tb, dc), lambda i: (i, 0)),
                  pl.BlockSpec((tb, dc), lambda i: (i, 1)),
                  pl.BlockSpec((tb, dc), lambda i: (i, 3)),
                  pl.BlockSpec((tb, da), lambda i: (i, 7)),
                  pair_spec,
                  pl.BlockSpec((1, dc), lambda i: (0, 0)),
                  pl.BlockSpec((hp, 1, PAIR), lambda i: (0, 0, 0))],
        out_specs=(pl.BlockSpec((tb, 4 * dc + 4 * da), lambda i: (i, 0)), pl.BlockSpec((tb, dc), lambda i: (i, 0)),
                   pair_spec, pair_spec,
                   pl.BlockSpec((1, dc), lambda i: (0, 0)), pl.BlockSpec((hp, 1, PAIR), lambda i: (0, 0, 0))),
        compiler_params=_params(("arbitrary",)),
    )(dycat, co, proj, proj, proj, o_mix, g_conv, g_attn_pairs)


def _conv_bwd(dconv_proj, dco, conv_proj, conv_w, dc, after):
    t = dco.shape[0]
    ct = CONV_TILE
    nct = dc // ct

    def body(dcp_in_ref, dco_ref, u_ref, cg_ref, w_ref, after_ref, dcp_ref, acc_ref):
        del dcp_in_ref, after_ref
        which = pl.program_id(1)
        g = dco_ref[...]
        u = u_ref[...].astype(F32)
        cg = cg_ref[...].astype(F32)
        g_prev, g_next = _shift_rows(g, t)
        da = w_ref[0:1, :] * g_next + w_ref[1:2, :] * g + w_ref[2:3, :] * g_prev
        dcp_ref[...] = (da * jnp.where(which == 0, cg, u)).astype(BF16)
        a = cg * u
        a_prev, a_next = _shift_rows(a, t)
        acc_ref[...] = jnp.concatenate(
            [jnp.sum(g * a_prev, axis=0, keepdims=True), jnp.sum(g * a, axis=0, keepdims=True),
             jnp.sum(g * a_next, axis=0, keepdims=True), jnp.sum(g, axis=0, keepdims=True),
             jnp.zeros((4, ct), F32)], axis=0)

    return pl.pallas_call(
        body, name="conv_bwd", grid=(nct, 2),
        out_shape=(jax.ShapeDtypeStruct(dconv_proj.shape, BF16), jax.ShapeDtypeStruct((8, dc), F32)),
        in_specs=[HBM,
                  pl.BlockSpec((t, ct), lambda i, s: (0, i)),
                  pl.BlockSpec((t, ct), lambda i, s: (0, i)),
                  pl.BlockSpec((t, ct), lambda i, s: (0, 2 * nct + i)),
                  pl.BlockSpec((3, ct), lambda i, s: (0, i)), ANY],
        out_specs=(pl.BlockSpec((t, ct), lambda i, s: (0, 2 * s * nct + i)),
                   pl.BlockSpec((8, ct), lambda i, s: (0, i))),
        input_output_aliases={0: 0},
        compiler_params=_params(("arbitrary", "arbitrary")),
    )(dconv_proj, dco, conv_proj, conv_proj, conv_w, after)


def _dh(dproj, winf, after):
    t = dproj.shape[0]
    _, d, ws = winf.shape
    tm = tn = COL_TILE
    nt = (((1,), (1,)), ((), ()))

    def body(a_ref, w_ref, after_ref, o_ref):
        del after_ref
        acc = lax.dot_general(a_ref[:, 0:ws], w_ref[0], nt, preferred_element_type=F32)
        for j in range(1, N_CHIPS):
            acc = acc + lax.dot_general(a_ref[:, j * ws:(j + 1) * ws], w_ref[j], nt, preferred_element_type=F32)
        o_ref[...] = acc.astype(BF16)

    return pl.pallas_call(
        body, name="dh", grid=(d // tn, t // tm),
        out_shape=jax.ShapeDtypeStruct((t, d), BF16),
        in_specs=[pl.BlockSpec((tm, N_CHIPS * ws), lambda n, m: (m, 0)),
                  pl.BlockSpec((N_CHIPS, tn, ws), lambda n, m: (0, n, 0)), ANY],
        out_specs=pl.BlockSpec((tm, tn), lambda n, m: (m, n)),
        compiler_params=_params(("parallel", "parallel")),
    )(dproj, winf, after)


def _prenorm_bwd(x, dh, dout, mod, g_pre):
    t, d = x.shape
    tb = ROW_TILE

    def body(x_ref, dh_ref, dout_ref, mod_ref, g_ref, gx_ref, acc_ref):
        xv = x_ref[...]
        dhv = dh_ref[...].astype(F32)
        r = lax.rsqrt(jnp.mean(xv * xv, axis=-1, keepdims=True) + EPS)
        xh = xv * r
        one_scale = 1.0 + mod_ref[1:2, :]
        a = dhv * one_scale * g_ref[...]
        gx_ref[...] = dout_ref[...].astype(F32) + r * (a - xh * jnp.mean(a * xh, axis=-1, keepdims=True))
        part = jnp.concatenate(
            [jnp.sum(dhv, axis=0, keepdims=True), jnp.sum(dhv * xh * g_ref[...], axis=0, keepdims=True),
             jnp.sum(dhv * xh * one_scale, axis=0, keepdims=True), jnp.zeros((5, d), F32)], axis=0)

        @pl.when(pl.program_id(0) == 0)
        def _():
            acc_ref[...] = jnp.zeros(acc_ref.shape, F32)

        acc_ref[...] += part

    return pl.pallas_call(
        body, name="prenorm_bwd", grid=(t // tb,),
        out_shape=(jax.ShapeDtypeStruct((t, d), F32), jax.ShapeDtypeStruct((8, d), F32)),
        in_specs=[pl.BlockSpec((tb, d), lambda i: (i, 0)), pl.BlockSpec((tb, d), lambda i: (i, 0)),
                  pl.BlockSpec((tb, d), lambda i: (i, 0)), pl.BlockSpec((3, d), lambda i: (0, 0)),
                  pl.BlockSpec((1, d), lambda i: (0, 0))],
        out_specs=(pl.BlockSpec((tb, d), lambda i: (i, 0)), pl.BlockSpec((8, d), lambda i: (0, 0))),
        compiler_params=_params(("arbitrary",)),
    )(x, dh, dout, mod, g_pre)


def _chip_sums(mine, rsib, name, part=0, parts=1, after=()):
    _, half, cols = mine.shape
    rows = half // parts
    tr = min(rows, ROW_TILE)
    nt = rows // tr

    def body(g_ref, r_ref, *rest):
        rest[-1][...] = (g_ref[...].astype(F32) + r_ref[...].astype(F32)).astype(BF16)

    spec = pl.BlockSpec((None, tr, cols), lambda j, i: (j, part * nt + i, 0))
    return pl.pallas_call(
        body, name=name, grid=(N_CHIPS, nt),
        out_shape=jax.ShapeDtypeStruct((N_CHIPS, rows, cols), BF16),
        in_specs=[spec, spec] + [ANY] * len(after), out_specs=pl.BlockSpec((None, tr, cols), lambda j, i: (j, i, 0)),
        compiler_params=_params(("parallel", "parallel")),
    )(mine, rsib, *after)


def _owner_sum(place, mine, rsib, rici, name, part=0, parts=1):
    _, half, cols = mine.shape
    rows = half // parts
    tr = min(rows, ROW_TILE)
    nt = rows // tr

    def body(place_ref, g_ref, r_ref, i_ref, o_ref):
        del place_ref
        acc = g_ref[...].astype(F32) + r_ref[...].astype(F32)
        for k in range(N_CHIPS - 1):
            acc = acc + i_ref[k].astype(F32)
        o_ref[...] = acc

    own = pl.BlockSpec((None, tr, cols), lambda i, p: (p[0], part * nt + i, 0))
    grid_spec = pltpu.PrefetchScalarGridSpec(
        num_scalar_prefetch=1, grid=(nt,),
        in_specs=[own, own, pl.BlockSpec((N_CHIPS - 1, tr, cols), lambda i, p: (0, i, 0))],
        out_specs=pl.BlockSpec((tr, cols), lambda i, p: (p[1] * (half // tr) + part * nt + i, 0)))
    return pl.pallas_call(
        body, name=name, grid_spec=grid_spec,
        out_shape=jax.ShapeDtypeStruct((2 * half, cols), F32),
        compiler_params=_params(("parallel",)),
    )(place, mine, rsib, rici)


def _adam_math(w, g, m, v):
    m2 = ADAM_B1 * m + (1.0 - ADAM_B1) * g
    v2 = ADAM_B2 * v + (1.0 - ADAM_B2) * (g * g)
    m_hat = m2 / (1.0 - ADAM_B1 ** ADAM_STEP)
    v_hat = v2 / (1.0 - ADAM_B2 ** ADAM_STEP)
    delta = -ADAM_LR * (m_hat / (jnp.sqrt(v_hat) + ADAM_EPS) + ADAM_WD * w)
    return delta, m2, v2


def _adamw(w, g, m, v, name, part=0, parts=1, prev=None):
    rows, cols = w.shape
    tr = min(rows, ROW_TILE)

    def body(*refs):
        w_ref, g_ref, m_ref, v_ref, go_ref, d_ref, m2_ref, v2_ref = refs[-8:]
        g = g_ref[...]
        go_ref[...] = g
        d_ref[...], m2_ref[...], v2_ref[...] = _adam_math(w_ref[...], g, m_ref[...], v_ref[...])

    if parts == 1:
        grid, spec = (rows // tr,), pl.BlockSpec((tr, cols), lambda i: (i, 0))
    else:
        per_half = rows // 2 // tr
        nt = per_half // parts
        grid, spec = (2, nt), pl.BlockSpec((tr, cols), lambda r, i: (r * per_half + part * nt + i, 0))
    olds = [] if prev is None else list(prev)
    return pl.pallas_call(
        body, name=name, grid=grid,
        out_shape=(jax.ShapeDtypeStruct(w.shape, F32),) * 4,
        in_specs=[HBM] * len(olds) + [spec] * 4, out_specs=(spec,) * 4,
        input_output_aliases={i: i for i in range(len(olds))},
        compiler_params=_params(("parallel",) * len(grid)),
    )(*olds, w, g, m, v)


def _ada_grad_adamw(c_all, dmod_cols, w, m, v):
    d, wa = w.shape
    tr = ROW_TILE

    def body(c_ref, dm_ref, w_ref, m_ref, v_ref, g_ref, d_ref, m2_ref, v2_ref):
        act = _silu(c_ref[...]).T
        g = act[:, 0:1] * dm_ref[0:1, :]
        for b in range(1, N_DEV):
            g = g + act[:, b:b + 1] * dm_ref[b:b + 1, :]
        g_ref[...] = g
        d_ref[...], m2_ref[...], v2_ref[...] = _adam_math(w_ref[...], g, m_ref[...], v_ref[...])

    spec = pl.BlockSpec((tr, wa), lambda i: (i, 0))
    return pl.pallas_call(
        body, name="ada_grad_adamw", grid=(d // tr,),
        out_shape=(jax.ShapeDtypeStruct(w.shape, F32),) * 4,
        in_specs=[pl.BlockSpec((N_DEV, tr), lambda i: (0, i)), pl.BlockSpec((N_DEV, wa), lambda i: (0, 0)),
                  spec, spec, spec],
        out_specs=(spec,) * 4,
        compiler_params=_params(("parallel",)),
    )(c_all, dmod_cols, w, m, v)


def _small_update(place, gathered, pieces, weights, moments_m, moments_v):
    n = gathered.shape[1]
    k = len(weights)
    final_shapes = [w.shape for w in weights]
    row_counts = [s[1] if len(s) == 3 else 1 for s in final_shapes]
    weights, moments_m, moments_v = ([a.reshape(1, -1) for a in arrays] for arrays in (weights, moments_m, moments_v))

    def body(place_ref, g_ref, *refs):
        w_refs, m_refs, v_refs = refs[0:k], refs[k:2 * k], refs[2 * k:3 * k]
        outs = refs[3 * k:]
        total = g_ref[0:SUBLANES, :]
        for dev in range(1, N_DEV):
            total = total + g_ref[SUBLANES * dev:SUBLANES * (dev + 1), :]

        def flat(offset, length):
            segments, pos = [], offset
            while pos < offset + length:
                row, col = divmod(pos, n)
                take = min(offset + length - pos, n - col)
                segments.append(total[row:row + 1, col:col + take])
                pos += take
            return jnp.concatenate(segments, axis=1) if len(segments) > 1 else segments[0]

        chip = place_ref[0]
        for i, (w_ref, m_ref, v_ref) in enumerate(zip(w_refs, m_refs, v_refs)):
            g = flat(*pieces[i])
            if pieces[i][1] > w_ref.shape[1]:
                rows = row_counts[i]
                cols, full = w_ref.shape[1] // rows, pieces[i][1] // rows
                picked = []
                for r in range(rows):
                    blocks = [g[:, r * full + q * cols:r * full + (q + 1) * cols] for q in range(N_CHIPS)]
                    mine = blocks[N_CHIPS - 1]
                    for q in range(N_CHIPS - 2, -1, -1):
                        mine = jnp.where(chip == q, blocks[q], mine)
                    picked.append(mine)
                g = jnp.concatenate(picked, axis=1)
            delta, m2, v2 = _adam_math(w_ref[...], g, m_ref[...], v_ref[...])
            for j, val in enumerate((g, delta, m2, v2)):
                outs[j * k + i][...] = val
        outs[4 * k][...] = flat(*pieces[k])

    shapes = [jax.ShapeDtypeStruct(w.shape, F32) for w in weights]
    grid_spec = pltpu.PrefetchScalarGridSpec(
        num_scalar_prefetch=1, grid=(1,),
        in_specs=[pl.BlockSpec(gathered.shape, lambda i, p: (0, 0))]
        + [pl.BlockSpec(a.shape, functools.partial(lambda nd, i, p: (0,) * nd, a.ndim))
           for a in (*weights, *moments_m, *moments_v)],
        out_specs=tuple(pl.BlockSpec(s.shape, functools.partial(lambda nd, i, p: (0,) * nd, len(s.shape)))
                        for s in shapes * 4) + (pl.BlockSpec((1, LANES), lambda i, p: (0, 0)),))
    outs = pl.pallas_call(
        body, name="small_update", grid_spec=grid_spec,
        out_shape=tuple(shapes * 4) + (jax.ShapeDtypeStruct((1, LANES), F32),),
        compiler_params=_params(("arbitrary",)),
    )(place, gathered, *weights, *moments_m, *moments_v)
    shaped = [out.reshape(final_shapes[i % k]) for i, out in enumerate(outs[0:4 * k])]
    return shaped[0:k], shaped[k:2 * k], shaped[2 * k:3 * k], shaped[3 * k:4 * k], outs[4 * k]


def _pack_small(pieces):
    flat = [p.reshape(-1).astype(F32) for p in pieces]
    offsets, total = [], 0
    for p in flat:
        offsets.append(total)
        total += p.shape[0]
    padded = -(-total // SMALL_ALIGN) * SMALL_ALIGN
    if padded > total:
        flat.append(jnp.zeros((padded - total,), F32))
    return jnp.concatenate(flat).reshape(8, padded // 8), offsets


def _alibi_slope_rows(n_heads):
    slopes = 2.0 ** (-8.0 * np.arange(1, n_heads + 1, dtype=np.float64) / n_heads)
    rows = np.zeros((n_heads // 2, SUBLANES), np.float32)
    rows[:, 0:2] = slopes.reshape(n_heads // 2, 2)
    return jnp.asarray(np.broadcast_to(rows[:, :, None], (n_heads // 2, SUBLANES, ATT_KW)))


def kernel(x, c, w_ada, b_ada, g_pre, w_in, conv_w, conv_b, g_conv, g_attn, w_out, g_post, loss_target, m_w_ada, m_b_ada, m_g_pre, m_w_in, m_conv_w, m_conv_b, m_g_conv, m_g_attn, m_w_out, m_g_post, v_w_ada, v_b_ada, v_g_pre, v_w_in, v_conv_w, v_conv_b, v_g_conv, v_g_attn, v_w_out, v_g_post):
    t, d = x.shape[1], x.shape[2]
    dc = conv_b.shape[1]
    da = g_attn.shape[1]
    hp = da // PAIR
    ws = w_in.shape[2]
    wa = w_ada.shape[2]
    cws = conv_w.shape[2]
    assert t % ROW_TILE == 0 and d % ROW_TILE == 0 and dc % COL_TILE == 0 and da % COL_TILE == 0
    assert ws == 2 * dc and dc == da and t // BRANCHES[-1][1] >= ATT_BQ

    mx, my, mc = _my_place()
    chip = _chip_of(mx, my)
    dev = 2 * chip + mc
    place = jnp.stack([chip, mc]).astype(jnp.int32)

    x2, tgt2 = x[0], loss_target[0]
    w_ada2, w_in2, w_out2 = w_ada[0], w_in[0], w_out[0]

    packed, offs = _pack_small([c[0], conv_w[0]])
    seen, mod, win_slots = _ada_modulation(packed, w_ada2, b_ada, d, w_in2)
    seen = seen.reshape(N_DEV, -1)
    c_all = seen[:, offs[0]:offs[0] + d]
    conv_w_full = seen[0::2, offs[1]:offs[1] + 3 * cws].reshape(N_CHIPS, 3, cws).transpose(1, 0, 2).reshape(3, dc)

    win_flight, send_in, recv_in, started = _gather_start(win_slots, mod)

    y_chip, x_chip, d_chip = (_chip_of(mx, 1 - my), _chip_of(1 - mx, my), _chip_of(1 - mx, 1 - my))
    tiles_per_part = ws // COL_TILE // 2

    def tiles_of(chunk, parts):
        return [(2 * chunk + part) * tiles_per_part + k for part in parts for k in range(tiles_per_part)]

    tiles = jnp.stack([jnp.stack(step) for step in (
        tiles_of(chip, (0, 1)), tiles_of(y_chip, (0,)) + tiles_of(x_chip, (1,)),
        tiles_of(y_chip, (1,)) + tiles_of(x_chip, (0,)), tiles_of(d_chip, (0, 1)))]).astype(jnp.int32)
    h, ht = _prenorm(x2, mod, g_pre, started)
    proj = _proj_tiles(None, h, w_in2, tiles, 0, "proj_own")
    win_flight, wout_flight, relay_send_in, relay_recv_in, send_out, recv_out = _gather_relay_in(
        win_flight, _cast_into_slot(place, w_out2, "cast_w_out", proj), recv_in, proj)
    win_flight = _forward_halves(win_flight, ((0, 0), (1, 1)), "forward_w_in_first")
    proj = _proj_tiles(proj, h, win_flight, tiles, 1, "proj_first_parts")
    win_flight = _forward_halves(
        _gather_wait_direct(win_flight, send_in, recv_in, proj, "gather_wait_w_in_direct"),
        ((0, 1), (1, 0)), "forward_w_in_second")
    proj = _proj_tiles(proj, h, win_flight, tiles, 2, "proj_second_parts")
    winf = _forward_halves(
        _gather_wait_relayed(win_flight, relay_send_in, relay_recv_in, proj, "gather_wait_w_in_relayed"),
        ((2, None),), "forward_w_in_relayed")
    proj = _proj_tiles(proj, h, winf, tiles, 3, "proj_diagonal")
    slopes = _alibi_slope_rows(da // HEAD_DIM)
    co = _conv_fwd(proj, conv_w_full, conv_b, dc)
    wout_flight, relay_send_out, relay_recv_out = _gather_relay_out(wout_flight, recv_out, co)
    o_mix, lse = _attn_fwd(proj, slopes, dc, da)
    g_attn_pairs = g_attn.reshape(hp, 1, PAIR)
    wout_flight = _gather_wait_direct(wout_flight, send_out, recv_out, o_mix, "gather_wait_w_out_direct")
    wout_flight = _gather_wait_relayed(wout_flight, relay_send_out, relay_recv_out, o_mix, "gather_wait_w_out_relayed")
    all_halves = ((0, None), (1, None), (2, None))
    wout_flight, fsend_out, frecv_out, forwarding = _forward_start(wout_flight, all_halves, "forward_w_out_start")
    ycat, ycat_t = _mix_fwd(co, proj, o_mix, g_conv, g_attn_pairs, forwarding)
    woutf = _forward_wait(wout_flight, all_halves, fsend_out, frecv_out, ycat, "forward_w_out_wait").reshape(dc + da, d)
    dout, dy, post_sums = _out_fwd_bwd(ycat, woutf, x2, tgt2, mod, g_post)

    gout, rsib_out = _dw_swapped(ycat_t, dy, N_CHIPS, 1, "dw_out")
    csum_out = _chip_sums(gout, rsib_out, "rs_chip_sum_out")
    ssem_out, rsem_out, csum_out, land_out, sent_out = _owners_start(csum_out, "rs_owners_start_out")
    dycat = _matmul_nt(dy, woutf, BF16, "dycat")
    dproj, dco, d_o, delta, dg_conv, dg_attn = _mix_bwd(dycat, co, proj, o_mix, g_conv, g_attn_pairs)
    dproj, conv_sums = _conv_bwd(dproj, dco, proj, conv_w_full, dc, sent_out)
    dproj = _attn_bwd(dproj, proj, d_o, lse, delta, slopes, dc, da, sent_out)
    gin, rsib_in = _dw_swapped(ht, dproj, 1, N_CHIPS, "dw_in")
    ssem_in0, rsem_in0, csum_in0, land_in0, sent_in0 = _owners_start(
        _chip_sums(gin, rsib_in, "rs_chip_sum_in0", 0, 2), "rs_owners_start_in0")
    ssem_in1, rsem_in1, csum_in1, land_in1, sent_in = _owners_start(
        _chip_sums(gin, rsib_in, "rs_chip_sum_in1", 1, 2, after=(sent_in0,)), "rs_owners_start_in1")
    dh = _dh(dproj, winf, sent_in)
    grad_x, pre_sums = _prenorm_bwd(x2, dh, dout, mod, g_pre)

    small, so = _pack_small([
        pre_sums[0], pre_sums[1], post_sums[0],
        pre_sums[2], conv_sums[0:3], conv_sums[3], dg_conv, dg_attn, post_sums[1], post_sums[2, 0:128]])
    ssem_small, rsem_small, small, land_small, sent_small = _allgather8_start(small, dev, "gather_small_start")

    rici_out = _owners_wait(ssem_out, rsem_out, csum_out, land_out, [grad_x, sent_small], "rs_owners_wait_out")
    full_out, jsend_out, jrecv_out, joining_out = _join_start(
        _owner_sum(place, gout, rsib_out, rici_out, "rs_owner_sum_out"), "rs_join_start_out", 0, 1)
    rici_in = _owners_wait(ssem_in0, rsem_in0, csum_in0, land_in0, [joining_out], "rs_owners_wait_in0")
    full_in0, jsend0, jrecv0, joining0 = _join_start(
        _owner_sum(place, gin, rsib_in, rici_in, "rs_owner_sum_in0", 0, 2), "rs_join_start_in0", 0, 2)
    grad_w_out = _join_wait(full_out, jsend_out, jrecv_out, [joining0], "rs_join_wait_out", 0, 1)
    grad_w_out, delta_w_out, new_m_w_out, new_v_w_out = _adamw(
        w_out2, grad_w_out, m_w_out[0], v_w_out[0], "adamw_w_out")
    full_in0 = _join_wait(full_in0, jsend0, jrecv0, [delta_w_out], "rs_join_wait_in0", 0, 2)
    updated_in = _adamw(w_in2, full_in0, m_w_in[0], v_w_in[0], "adamw_w_in0", 0, 2)
    rici_in = _owners_wait(ssem_in1, rsem_in1, csum_in1, land_in1, [updated_in[1]], "rs_owners_wait_in1")
    full_in1, jsend1, jrecv1, joining1 = _join_start(
        _owner_sum(place, gin, rsib_in, rici_in, "rs_owner_sum_in1", 1, 2), "rs_join_start_in1", 1, 2)

    small_seen = _allgather8_wait(ssem_small, rsem_small, small, land_small, [joining1], "gather_small_wait")
    small_w = [b_ada, g_pre, conv_w, conv_b, g_conv, g_attn, g_post]
    small_m = [m_b_ada, m_g_pre, m_conv_w, m_conv_b, m_g_conv, m_g_attn, m_g_post]
    small_v = [v_b_ada, v_g_pre, v_conv_w, v_conv_b, v_g_conv, v_g_attn, v_g_post]
    pieces = [(0, 3 * d), (so[3], d), (so[4], 3 * dc), (so[5], dc), (so[6], dc), (so[7], da), (so[8], d), (so[9], LANES)]
    g_small, d_small, m_small, v_small, loss_row = _small_update(place, small_seen, pieces, small_w, small_m, small_v)
    loss = loss_row[0, 0]
    grad_b_ada, grad_g_pre, grad_conv_w, grad_conv_b, grad_g_conv, grad_g_attn, grad_g_post = g_small
    dmod_cols = lax.dynamic_slice_in_dim(small_seen.reshape(N_DEV, -1), chip * wa, wa, axis=1)
    grad_w_ada, delta_w_ada, new_m_w_ada, new_v_w_ada = _ada_grad_adamw(c_all, dmod_cols, w_ada2, m_w_ada[0], v_w_ada[0])

    full_in1 = _join_wait(full_in1, jsend1, jrecv1, [delta_w_ada, d_small[0]], "rs_join_wait_in1", 1, 2)
    grad_w_in, delta_w_in, new_m_w_in, new_v_w_in = _adamw(
        w_in2, full_in1, m_w_in[0], v_w_in[0], "adamw_w_in1", 1, 2, updated_in)

    def lead(a):
        return a.reshape((1,) + a.shape)

    grads = [lead(grad_w_ada), grad_b_ada, grad_g_pre, lead(grad_w_in), grad_conv_w, grad_conv_b, grad_g_conv,
             grad_g_attn, lead(grad_w_out), grad_g_post]
    deltas = [lead(delta_w_ada), d_small[0], d_small[1], lead(delta_w_in), d_small[2], d_small[3], d_small[4],
              d_small[5], lead(delta_w_out), d_small[6]]
    new_ms = [lead(new_m_w_ada), m_small[0], m_small[1], lead(new_m_w_in), m_small[2], m_small[3], m_small[4],
              m_small[5], lead(new_m_w_out), m_small[6]]
    new_vs = [lead(new_v_w_ada), v_small[0], v_small[1], lead(new_v_w_in), v_small[2], v_small[3], v_small[4],
              v_small[5], lead(new_v_w_out), v_small[6]]
    return (loss, lead(grad_x), *grads, *deltas, *new_ms, *new_vs)
```

```python
import functools

import jax
import jax.numpy as jnp
import numpy as np
from jax import lax
from jax.experimental import pallas as pl
from jax.experimental.pallas import tpu as pltpu

F32 = jnp.float32
BF16 = jnp.bfloat16
MESH = pl.DeviceIdType.MESH
HBM = pl.BlockSpec(memory_space=pltpu.HBM)
VMEM = pl.BlockSpec(memory_space=pltpu.VMEM)
ANY = pl.BlockSpec(memory_space=pl.ANY)
SEM = pl.BlockSpec(memory_space=pltpu.SEMAPHORE)
EFFECT = pltpu.SideEffectType.DATAFLOW_SIDE_EFFECTING
SUBLANES, LANES = 8, 128
TOKEN = jax.ShapeDtypeStruct((SUBLANES, LANES), jnp.float32)

HEAD_DIM = 64
PAIR = 2 * HEAD_DIM
assert PAIR == LANES
BRANCHES = ((128, 1), (512, 4), (2048, 16))
SIDE = 64
EPS = 1e-6
NEG_INF = -1e30
N_CHIPS = 4
N_DEV = 8

ADAM_LR = 0.001
ADAM_B1 = 0.9
ADAM_B2 = 0.999
ADAM_EPS = 1e-08
ADAM_WD = 0.01
ADAM_STEP = 10

VMEM_LIMIT_BYTES = 56 * 1024 * 1024
ROW_TILE = 256
COL_TILE = 512
CONV_TILE = 256
ATT_BQ = 128
ATT_KW = ATT_BQ + 2 * SIDE
ATT_UNROLL = 4
STAGE_PAD = 4
SMALL_ALIGN = SUBLANES * LANES


def _params(semantics=None):
    kw = {"vmem_limit_bytes": VMEM_LIMIT_BYTES}
    if semantics is not None:
        kw["dimension_semantics"] = semantics
    return pltpu.CompilerParams(**kw)


def _silu(z):
    return z * jax.nn.sigmoid(z)


def _silu_grad(z):
    s = jax.nn.sigmoid(z)
    return s * (1.0 + z * (1.0 - s))


def _my_place():
    return lax.axis_index("x"), lax.axis_index("y"), lax.axis_index("c")


def _flip(a, bit):
    return 1 - a if bit else a


def _chip_of(x, y):
    return 2 * x + y


def _allgather8_start(v, me, name):
    rows_per, n = v.shape
    land = lax.dynamic_update_slice(jnp.zeros((N_DEV * rows_per, n), v.dtype), v, (me * rows_per, 0))

    def body(v_ref, land_ref, send_sems, recv_sems, v_thru, land_thru, token_ref):
        del v_thru, land_thru
        x, y, c = _my_place()
        mine = land_ref.at[pl.ds(pl.multiple_of((4 * x + 2 * y + c) * rows_per, rows_per), rows_per), :]
        for k in range(1, N_DEV):
            peer = (_flip(x, k & 4), _flip(y, k & 2), _flip(c, k & 1))
            pltpu.make_async_remote_copy(
                src_ref=v_ref, dst_ref=mine, send_sem=send_sems.at[k - 1], recv_sem=recv_sems.at[k - 1],
                device_id=peer, device_id_type=MESH).start()
        token_ref[...] = jnp.zeros(token_ref.shape, F32)

    sems = pltpu.SemaphoreType.DMA((N_DEV - 1,))
    return pl.pallas_call(
        body, name=name,
        out_shape=(sems, sems, jax.ShapeDtypeStruct(v.shape, v.dtype), jax.ShapeDtypeStruct(land.shape, land.dtype), TOKEN),
        in_specs=[HBM, HBM], out_specs=(SEM, SEM, HBM, HBM, VMEM),
        input_output_aliases={0: 2, 1: 3},
        compiler_params=pltpu.CompilerParams(has_side_effects=EFFECT),
    )(pltpu.with_memory_space_constraint(v, pltpu.HBM), pltpu.with_memory_space_constraint(land, pltpu.HBM))


def _allgather8_wait(send_sems, recv_sems, v, land, after, name):
    rows_per = v.shape[0]

    def body(v_ref, land_ref, send_ref, recv_ref, *rest):
        del rest
        x, y, c = _my_place()
        for k in range(1, N_DEV):
            peer = (_flip(x, k & 4), _flip(y, k & 2), _flip(c, k & 1))
            src = 4 * peer[0] + 2 * peer[1] + peer[2]
            cp = pltpu.make_async_remote_copy(
                src_ref=v_ref, dst_ref=land_ref.at[pl.ds(pl.multiple_of(src * rows_per, rows_per), rows_per), :],
                send_sem=send_ref.at[k - 1], recv_sem=recv_ref.at[k - 1], device_id=peer, device_id_type=MESH)
            cp.wait_send()
            cp.wait_recv()

    return pl.pallas_call(
        body, name=name,
        out_shape=(jax.ShapeDtypeStruct(v.shape, v.dtype), jax.ShapeDtypeStruct(land.shape, land.dtype)),
        in_specs=[HBM, HBM, SEM, SEM] + [ANY] * len(after), out_specs=(HBM, HBM),
        input_output_aliases={0: 0, 1: 1},
        compiler_params=pltpu.CompilerParams(has_side_effects=EFFECT),
    )(v, land, send_sems, recv_sems, *after)[1]


def _half_rows(ref, chip, which, half):
    return ref.at[chip, pl.ds(pl.multiple_of(which * half, half), half), :]


def _ici_peers(x, y, c):
    peers = [(_flip(x, k & 2), _flip(y, k & 1), c) for k in (1, 2, 3)]
    return [(peer, _chip_of(peer[0], peer[1])) for peer in peers]


def _part_of_half(ref, chip, core, part):
    half, cols = ref.shape[1] // 2, ref.shape[2] // 2
    return ref.at[chip, pl.ds(pl.multiple_of(core * half, half), half), pl.ds(part * cols, cols)]


def _neighbours(x, y, c):
    return [((x, 1 - y, c), _chip_of(x, 1 - y)), ((1 - x, y, c), _chip_of(1 - x, y)),
            ((1 - x, 1 - y, c), _chip_of(1 - x, 1 - y))]


def _start_direct(buf, send_sems, recv_sems):
    x, y, c = _my_place()
    me = _chip_of(x, y)
    for n, (peer, _) in enumerate(_neighbours(x, y, c)[0:2]):
        for part in ((0, 1), (1, 0))[n]:
            piece = _part_of_half(buf, me, c, part)
            pltpu.make_async_remote_copy(
                src_ref=piece, dst_ref=piece, send_sem=send_sems.at[2 * n + part], recv_sem=recv_sems.at[2 * n + part],
                device_id=peer, device_id_type=MESH).start()


def _relay(buf, recv_sems, relay_send, relay_recv):
    x, y, c = _my_place()
    nbrs = _neighbours(x, y, c)
    for n in range(2):
        part = n
        piece = _part_of_half(buf, nbrs[n][1], c, part)
        pltpu.make_async_remote_copy(
            src_ref=piece, dst_ref=piece, send_sem=relay_send.at[part], recv_sem=recv_sems.at[2 * n + part],
            device_id=nbrs[n][0], device_id_type=MESH).wait_recv()
        pltpu.make_async_remote_copy(
            src_ref=piece, dst_ref=piece, send_sem=relay_send.at[part], recv_sem=relay_recv.at[part],
            device_id=nbrs[1 - n][0], device_id_type=MESH).start()


def _gather_start(win_slots, after):
    def body(win_in, after_ref, win_ref, send_sems, recv_sems, token_ref):
        del win_in, after_ref
        _start_direct(win_ref, send_sems, recv_sems)
        token_ref[...] = jnp.zeros(token_ref.shape, F32)

    sems = pltpu.SemaphoreType.DMA((4,))
    return pl.pallas_call(
        body, name="gather_start",
        out_shape=(jax.ShapeDtypeStruct(win_slots.shape, win_slots.dtype), sems, sems, TOKEN),
        in_specs=[HBM, ANY], out_specs=(HBM, SEM, SEM, VMEM),
        input_output_aliases={0: 0},
        compiler_params=pltpu.CompilerParams(has_side_effects=EFFECT),
    )(win_slots, after)


def _gather_relay_in(win, wout_slots, recv_in, after):
    def body(win_in, wout_in, recv_in_ref, after_ref, win_ref, wout_ref, relay_send, relay_recv, send_out, recv_out):
        del win_in, wout_in, after_ref
        _relay(win_ref, recv_in_ref, relay_send, relay_recv)
        _start_direct(wout_ref, send_out, recv_out)

    two, four = pltpu.SemaphoreType.DMA((2,)), pltpu.SemaphoreType.DMA((4,))
    return pl.pallas_call(
        body, name="gather_relay_w_in",
        out_shape=(jax.ShapeDtypeStruct(win.shape, win.dtype), jax.ShapeDtypeStruct(wout_slots.shape, wout_slots.dtype),
                   two, two, four, four),
        in_specs=[HBM, HBM, SEM, ANY], out_specs=(HBM, HBM, SEM, SEM, SEM, SEM),
        input_output_aliases={0: 0, 1: 1},
        compiler_params=pltpu.CompilerParams(has_side_effects=EFFECT),
    )(win, wout_slots, recv_in, after)


def _gather_relay_out(wout, recv_out, after):
    def body(wout_in, recv_out_ref, after_ref, wout_ref, relay_send, relay_recv):
        del wout_in, after_ref
        _relay(wout_ref, recv_out_ref, relay_send, relay_recv)

    two = pltpu.SemaphoreType.DMA((2,))
    return pl.pallas_call(
        body, name="gather_relay_w_out",
        out_shape=(jax.ShapeDtypeStruct(wout.shape, wout.dtype), two, two),
        in_specs=[HBM, SEM, ANY], out_specs=(HBM, SEM, SEM),
        input_output_aliases={0: 0},
        compiler_params=pltpu.CompilerParams(has_side_effects=EFFECT),
    )(wout, recv_out, after)


def _gather_wait_direct(buf, send_sems, recv_sems, after, name):
    def body(buf_in, send_ref, recv_ref, after_ref, buf_ref):
        del buf_in, after_ref
        x, y, c = _my_place()
        me = _chip_of(x, y)
        for n, (peer, chip) in enumerate(_neighbours(x, y, c)[0:2]):
            second = 1 - n
            pltpu.make_async_remote_copy(
                src_ref=_part_of_half(buf_ref, me, c, second), dst_ref=_part_of_half(buf_ref, chip, c, second),
                send_sem=send_ref.at[2 * n + second], recv_sem=recv_ref.at[2 * n + second],
                device_id=peer, device_id_type=MESH).wait_recv()
            for part in range(2):
                piece = _part_of_half(buf_ref, me, c, part)
                pltpu.make_async_remote_copy(
                    src_ref=piece, dst_ref=piece, send_sem=send_ref.at[2 * n + part], recv_sem=recv_ref.at[2 * n + part],
                    device_id=peer, device_id_type=MESH).wait_send()

    return pl.pallas_call(
        body, name=name,
        out_shape=jax.ShapeDtypeStruct(buf.shape, buf.dtype),
        in_specs=[HBM, SEM, SEM, ANY], out_specs=HBM,
        input_output_aliases={0: 0},
        compiler_params=pltpu.CompilerParams(has_side_effects=EFFECT),
    )(buf, send_sems, recv_sems, after)


def _gather_wait_relayed(buf, relay_send, relay_recv, after, name):
    def body(buf_in, rsend_ref, rrecv_ref, after_ref, buf_ref):
        del buf_in, after_ref
        x, y, c = _my_place()
        nbrs = _neighbours(x, y, c)
        for n in range(2):
            relayed = _part_of_half(buf_ref, nbrs[n][1], c, n)
            cp = pltpu.make_async_remote_copy(
                src_ref=relayed, dst_ref=_part_of_half(buf_ref, nbrs[2][1], c, n),
                send_sem=rsend_ref.at[n], recv_sem=rrecv_ref.at[n], device_id=nbrs[1 - n][0], device_id_type=MESH)
            cp.wait_recv()
            cp.wait_send()

    return pl.pallas_call(
        body, name=name,
        out_shape=jax.ShapeDtypeStruct(buf.shape, buf.dtype),
        in_specs=[HBM, SEM, SEM, ANY], out_specs=HBM,
        input_output_aliases={0: 0},
        compiler_params=pltpu.CompilerParams(has_side_effects=EFFECT),
    )(buf, relay_send, relay_recv, after)


def _forward_copies(buf_ref, which, send_sems, recv_sems):
    half = buf_ref.shape[1] // 2
    x, y, c = _my_place()

    def copy(k, chip, core, part):
        piece = _half_rows(buf_ref, chip, core, half) if part is None else _part_of_half(buf_ref, chip, core, part)
        return pltpu.make_async_remote_copy(
            src_ref=piece, dst_ref=piece, send_sem=send_sems.at[k], recv_sem=recv_sems.at[k],
            device_id=(x, y, 1 - c), device_id_type=MESH)

    chips = [_neighbours(x, y, c)[n][1] for n, _ in which]
    return [(copy(k, chip, c, part), copy(k, chip, 1 - c, part)) for k, (chip, (_, part)) in enumerate(zip(chips, which))]


def _forward_halves(buf, which, name):
    def body(buf_in, buf_ref, send_sems, recv_sems):
        del buf_in
        copies = _forward_copies(buf_ref, which, send_sems, recv_sems)
        for mine, _ in copies:
            mine.start()
        for mine, theirs in copies:
            theirs.wait_recv()
        for mine, _ in copies:
            mine.wait_send()

    return pl.pallas_call(
        body, name=name,
        out_shape=jax.ShapeDtypeStruct(buf.shape, buf.dtype),
        in_specs=[HBM], out_specs=HBM,
        input_output_aliases={0: 0},
        scratch_shapes=[pltpu.SemaphoreType.DMA((len(which),))] * 2,
    )(buf)


def _forward_start(buf, which, name):
    def body(buf_in, buf_ref, send_sems, recv_sems, token_ref):
        del buf_in
        for mine, _ in _forward_copies(buf_ref, which, send_sems, recv_sems):
            mine.start()
        token_ref[...] = jnp.zeros(token_ref.shape, F32)

    sems = pltpu.SemaphoreType.DMA((len(which),))
    return pl.pallas_call(
        body, name=name,
        out_shape=(jax.ShapeDtypeStruct(buf.shape, buf.dtype), sems, sems, TOKEN),
        in_specs=[HBM], out_specs=(HBM, SEM, SEM, VMEM),
        input_output_aliases={0: 0},
        compiler_params=pltpu.CompilerParams(has_side_effects=EFFECT),
    )(buf)


def _forward_wait(buf, which, send_sems, recv_sems, after, name):
    def body(buf_in, send_ref, recv_ref, after_ref, buf_ref):
        del buf_in, after_ref
        for mine, theirs in _forward_copies(buf_ref, which, send_ref, recv_ref):
            theirs.wait_recv()
            mine.wait_send()

    return pl.pallas_call(
        body, name=name,
        out_shape=jax.ShapeDtypeStruct(buf.shape, buf.dtype),
        in_specs=[HBM, SEM, SEM, ANY], out_specs=HBM,
        input_output_aliases={0: 0},
        compiler_params=pltpu.CompilerParams(has_side_effects=EFFECT),
    )(buf, send_sems, recv_sems, after)


def _dw_swapped(a, b, row_chunks, col_chunks, name):
    r, t = a.shape
    c_all = b.shape[1]
    chunks = row_chunks * col_chunks
    rq, cq = r // row_chunks, c_all // col_chunks
    half = rq // 2
    tn = COL_TILE
    nt = cq // tn
    steps = col_chunks * nt

    def body(a_ref, b_ref, mine_ref, sib_ref, stage, send_sems, recv_sems):
        x, y, c = _my_place()
        j, n = pl.program_id(0), pl.program_id(1)
        step = j * nt + n
        slot = step % 2
        res = jnp.dot(a_ref[...], b_ref[...], preferred_element_type=F32).astype(BF16)

        def landing(jj, nn):
            cols = pl.ds(pl.multiple_of(nn * tn, tn), tn)
            return sib_ref.at[:, :, cols] if col_chunks == 1 else sib_ref.at[pl.ds(jj, 1), :, cols]

        def copy(slot_, step_, jj, nn):
            return pltpu.make_async_remote_copy(
                src_ref=stage.at[slot_], dst_ref=landing(jj, nn), send_sem=send_sems.at[slot_],
                recv_sem=recv_sems.at[step_], device_id=(x, y, 1 - c), device_id_type=MESH)

        @pl.when(step >= 2)
        def _():
            copy(slot, step, j, n).wait_send()

        for q in range(row_chunks):
            lo = res[q * rq:q * rq + half, :]
            hi = res[q * rq + half:(q + 1) * rq, :]
            mine_ref[q] = jnp.where(c == 0, lo, hi)
            stage[slot, q] = jnp.where(c == 0, hi, lo)
        copy(slot, step, j, n).start()

        @pl.when(step == steps - 1)
        def _():
            for s in range(max(steps - 2, 0), steps):
                copy(s % 2, s, j, n).wait_send()
            for s in range(steps):
                copy(s % 2, s, j, n).wait_recv()

    shape = jax.ShapeDtypeStruct((chunks, half, cq), BF16)
    return pl.pallas_call(
        body, name=name, grid=(col_chunks, nt),
        out_shape=(shape, shape),
        in_specs=[pl.BlockSpec((r, t), lambda j, n: (0, 0)), pl.BlockSpec((t, tn), lambda j, n: (0, j * nt + n))],
        out_specs=(pl.BlockSpec((row_chunks, half, tn), lambda j, n: (j, 0, n)), ANY),
        scratch_shapes=[pltpu.VMEM((2, row_chunks, half, tn), BF16), pltpu.SemaphoreType.DMA((2,)),
                        pltpu.SemaphoreType.DMA((steps,))],
        compiler_params=_params(("arbitrary", "arbitrary")),
    )(a, b)


def _owners_start(csum, name, after=()):
    land = pltpu.with_memory_space_constraint(lax.empty((N_CHIPS - 1,) + csum.shape[1:], csum.dtype), pltpu.HBM)

    def body(csum_ref, land_ref, *rest):
        send_sems, recv_sems, _, _, token_ref = rest[len(after):]
        x, y, c = _my_place()
        for k, (peer, owner) in enumerate(_ici_peers(x, y, c)):
            pltpu.make_async_remote_copy(
                src_ref=csum_ref.at[owner], dst_ref=land_ref.at[k], send_sem=send_sems.at[k], recv_sem=recv_sems.at[k],
                device_id=peer, device_id_type=MESH).start()
        token_ref[...] = jnp.zeros(token_ref.shape, F32)

    sems = pltpu.SemaphoreType.DMA((N_CHIPS - 1,))
    return pl.pallas_call(
        body, name=name,
        out_shape=(sems, sems, jax.ShapeDtypeStruct(csum.shape, csum.dtype),
                   jax.ShapeDtypeStruct(land.shape, land.dtype), TOKEN),
        in_specs=[HBM, HBM] + [ANY] * len(after), out_specs=(SEM, SEM, HBM, HBM, VMEM),
        input_output_aliases={0: 2, 1: 3},
        compiler_params=pltpu.CompilerParams(has_side_effects=EFFECT),
    )(pltpu.with_memory_space_constraint(csum, pltpu.HBM), land, *after)


def _owners_wait(send_sems, recv_sems, csum, land, after, name):
    def body(csum_ref, land_ref, send_ref, recv_ref, *rest):
        del rest
        x, y, c = _my_place()
        for k, (peer, owner) in enumerate(_ici_peers(x, y, c)):
            cp = pltpu.make_async_remote_copy(
                src_ref=csum_ref.at[owner], dst_ref=land_ref.at[k], send_sem=send_ref.at[k], recv_sem=recv_ref.at[k],
                device_id=peer, device_id_type=MESH)
            cp.wait_send()
            cp.wait_recv()

    return pl.pallas_call(
        body, name=name,
        out_shape=(jax.ShapeDtypeStruct(csum.shape, csum.dtype), jax.ShapeDtypeStruct(land.shape, land.dtype)),
        in_specs=[HBM, HBM, SEM, SEM] + [ANY] * len(after), out_specs=(HBM, HBM),
        input_output_aliases={0: 0, 1: 1},
        compiler_params=pltpu.CompilerParams(has_side_effects=EFFECT),
    )(csum, land, send_sems, recv_sems, *after)[1]


def _join_start(full, name, part, parts):
    half = full.shape[0] // 2
    rows = half // parts

    def body(full_in, full_ref, send_sem, recv_sem, token_ref):
        del full_in
        x, y, c = _my_place()
        mine = full_ref.at[pl.ds(pl.multiple_of(c * half + part * rows, rows), rows), :]
        pltpu.make_async_remote_copy(
            src_ref=mine, dst_ref=mine, send_sem=send_sem.at[0], recv_sem=recv_sem.at[0],
            device_id=(x, y, 1 - c), device_id_type=MESH).start()
        token_ref[...] = jnp.zeros(token_ref.shape, F32)

    one = pltpu.SemaphoreType.DMA((1,))
    return pl.pallas_call(
        body, name=name,
        out_shape=(jax.ShapeDtypeStruct(full.shape, full.dtype), one, one, TOKEN),
        in_specs=[HBM], out_specs=(HBM, SEM, SEM, VMEM),
        input_output_aliases={0: 0},
        compiler_params=pltpu.CompilerParams(has_side_effects=EFFECT),
    )(full)


def _join_wait(full, send_sem, recv_sem, after, name, part, parts):
    half = full.shape[0] // 2
    rows = half // parts

    def body(full_in, send_ref, recv_ref, *rest):
        del full_in
        full_ref = rest[-1]
        x, y, c = _my_place()
        cp = pltpu.make_async_remote_copy(
            src_ref=full_ref.at[pl.ds(pl.multiple_of(c * half + part * rows, rows), rows), :],
            dst_ref=full_ref.at[pl.ds(pl.multiple_of((1 - c) * half + part * rows, rows), rows), :],
            send_sem=send_ref.at[0], recv_sem=recv_ref.at[0], device_id=(x, y, 1 - c), device_id_type=MESH)
        cp.wait_send()
        cp.wait_recv()

    return pl.pallas_call(
        body, name=name,
        out_shape=jax.ShapeDtypeStruct(full.shape, full.dtype),
        in_specs=[HBM, SEM, SEM] + [ANY] * len(after), out_specs=HBM,
        input_output_aliases={0: 0},
        compiler_params=pltpu.CompilerParams(has_side_effects=EFFECT),
    )(full, send_sem, recv_sem, *after)


def _cast_into_slot(place, w, name, after):
    rows, cols = w.shape
    tr = min(rows, ROW_TILE)

    def body(place_ref, w_ref, after_ref, o_ref):
        del place_ref, after_ref
        o_ref[...] = w_ref[...].astype(BF16)

    grid_spec = pltpu.PrefetchScalarGridSpec(
        num_scalar_prefetch=1, grid=(rows // tr,),
        in_specs=[pl.BlockSpec((tr, cols), lambda i, p: (i, 0)), ANY],
        out_specs=pl.BlockSpec((None, tr, cols), lambda i, p: (p[0], i, 0)))
    return pl.pallas_call(
        body, name=name, grid_spec=grid_spec,
        out_shape=jax.ShapeDtypeStruct((N_CHIPS, rows, cols), BF16),
        compiler_params=_params(("parallel",)),
    )(place, w, after)


def _ada_modulation(packed, w_ada, b_ada, d, w_big):
    rows_per, n = packed.shape
    d_model, wa = w_ada.shape
    big_rows, big_cols = w_big.shape
    n_chunks = big_rows // ROW_TILE
    first_chunks = (2 * n_chunks) // 3

    def body(v_ref, w_hbm, b_ref, big_hbm, all_ref, mod_ref, slots_hbm, w_vmem, part_ref, parts_ref, wide, narrow,
             load_sem, send1, recv1, send2, recv2, in_sems, out_sems):
        x, y, c = _my_place()
        me = 4 * x + 2 * y + c
        chip = _chip_of(x, y)
        load = pltpu.make_async_copy(w_hbm, w_vmem, load_sem)
        load.start()

        def chunk_in(i):
            return pltpu.make_async_copy(big_hbm.at[i * ROW_TILE:(i + 1) * ROW_TILE, :], wide.at[i % 2], in_sems.at[i % 2])

        def chunk_out(i):
            return pltpu.make_async_copy(
                narrow.at[i % 2], slots_hbm.at[chip, i * ROW_TILE:(i + 1) * ROW_TILE, :], out_sems.at[i % 2])

        def cast_chunk(i):
            if i + 1 < n_chunks:
                chunk_in(i + 1).start()
            chunk_in(i).wait()
            if i >= 2:
                chunk_out(i - 2).wait()
            narrow[i % 2] = wide[i % 2].astype(BF16)
            chunk_out(i).start()

        chunk_in(0).start()

        def rows(idx):
            return all_ref.at[pl.ds(pl.multiple_of(idx * rows_per, rows_per), rows_per), :]

        all_ref[pl.ds(pl.multiple_of(me * rows_per, rows_per), rows_per), :] = v_ref[...]
        copies = []
        for k in range(1, N_DEV):
            peer = (_flip(x, k & 4), _flip(y, k & 2), _flip(c, k & 1))
            cp = pltpu.make_async_remote_copy(
                src_ref=v_ref, dst_ref=rows(me), send_sem=send1.at[k - 1], recv_sem=recv1.at[k - 1],
                device_id=peer, device_id_type=MESH)
            cp.start()
            copies.append((cp, peer))
        for i in range(first_chunks):
            cast_chunk(i)
        for k, (cp, peer) in enumerate(copies):
            pltpu.make_async_remote_copy(
                src_ref=v_ref, dst_ref=rows(4 * peer[0] + 2 * peer[1] + peer[2]), send_sem=send1.at[k],
                recv_sem=recv1.at[k], device_id=peer, device_id_type=MESH).wait_recv()
        for cp, _ in copies:
            cp.wait_send()

        def c_of(dev):
            segments, pos = [], 0
            while pos < d:
                row, col = divmod(pos, n)
                take = min(d - pos, n - col)
                segments.append(all_ref[dev * rows_per + row:dev * rows_per + row + 1, col:col + take])
                pos += take
            return jnp.concatenate(segments, axis=1)

        c_all = jnp.concatenate([c_of(dev) for dev in range(N_DEV)], axis=0)
        load.wait()
        part_ref[...] = jnp.dot(_silu(c_all), w_vmem[...], precision=lax.Precision.HIGHEST, preferred_element_type=F32)
        parts_ref[chip] = part_ref[...]
        swaps = []
        for k, (peer, _) in enumerate(_ici_peers(x, y, c)):
            cp = pltpu.make_async_remote_copy(
                src_ref=part_ref, dst_ref=parts_ref.at[chip], send_sem=send2.at[k], recv_sem=recv2.at[k],
                device_id=peer, device_id_type=MESH)
            cp.start()
            swaps.append(cp)
        for i in range(first_chunks, n_chunks):
            cast_chunk(i)
        for i in range(n_chunks - 2, n_chunks):
            chunk_out(i).wait()
        for k, (peer, peer_chip) in enumerate(_ici_peers(x, y, c)):
            pltpu.make_async_remote_copy(
                src_ref=part_ref, dst_ref=parts_ref.at[peer_chip], send_sem=send2.at[k], recv_sem=recv2.at[k],
                device_id=peer, device_id_type=MESH).wait_recv()
        for cp in swaps:
            cp.wait_send()
        flat = jnp.concatenate([parts_ref[j, pl.ds(me, 1), :] for j in range(N_CHIPS)], axis=1) + b_ref[...]
        mod_ref[...] = jnp.concatenate([flat[:, i * d:(i + 1) * d] for i in range(3)], axis=0)

    return pl.pallas_call(
        body, name="ada_modulation",
        out_shape=(jax.ShapeDtypeStruct((N_DEV * rows_per, n), F32), jax.ShapeDtypeStruct((3, d), F32),
                   jax.ShapeDtypeStruct((N_CHIPS, big_rows, big_cols), BF16)),
        in_specs=[VMEM, ANY, VMEM, ANY], out_specs=(VMEM, VMEM, ANY),
        scratch_shapes=[pltpu.VMEM((d_model, wa), F32), pltpu.VMEM((N_DEV, wa), F32),
                        pltpu.VMEM((N_CHIPS, N_DEV, wa), F32),
                        pltpu.VMEM((2, ROW_TILE, big_cols), F32), pltpu.VMEM((2, ROW_TILE, big_cols), BF16),
                        pltpu.SemaphoreType.DMA,
                        pltpu.SemaphoreType.DMA((N_DEV - 1,)), pltpu.SemaphoreType.DMA((N_DEV - 1,)),
                        pltpu.SemaphoreType.DMA((N_CHIPS - 1,)), pltpu.SemaphoreType.DMA((N_CHIPS - 1,)),
                        pltpu.SemaphoreType.DMA((2,)), pltpu.SemaphoreType.DMA((2,))],
        compiler_params=_params(),
    )(packed, w_ada, b_ada, w_big)


def _prenorm(x, mod, g_pre, after):
    t, d = x.shape
    tb = ROW_TILE

    def body(x_ref, mod_ref, g_ref, after_ref, h_ref, ht_ref):
        del after_ref
        xv = x_ref[...]
        r = lax.rsqrt(jnp.mean(xv * xv, axis=-1, keepdims=True) + EPS)
        h = (xv * r) * g_ref[...] * (1.0 + mod_ref[1:2, :]) + mod_ref[0:1, :]
        h_ref[...] = h.astype(BF16)
        ht_ref[...] = h.T.astype(BF16)

    return pl.pallas_call(
        body, name="prenorm", grid=(t // tb,),
        out_shape=(jax.ShapeDtypeStruct((t, d), BF16), jax.ShapeDtypeStruct((d, t), BF16)),
        in_specs=[pl.BlockSpec((tb, d), lambda i: (i, 0)), pl.BlockSpec((3, d), lambda i: (0, 0)),
                  pl.BlockSpec((1, d), lambda i: (0, 0)), ANY],
        out_specs=(pl.BlockSpec((tb, d), lambda i: (i, 0)), pl.BlockSpec((d, tb), lambda i: (0, i))),
        compiler_params=_params(("parallel",)),
    )(x, mod, g_pre, after)


def _proj_tiles(proj, h, w, tiles, step, name):
    t, d = h.shape
    ws = w.shape[-1]
    tn = COL_TILE
    nt = ws // tn

    def body(tile_ref, *refs):
        del tile_ref
        a_ref, b_ref, o_ref = refs[-3:]
        o_ref[...] = jnp.dot(a_ref[...], b_ref[...].astype(BF16), preferred_element_type=F32).astype(BF16)

    if w.ndim == 3:
        w_spec = pl.BlockSpec((None, d, tn), lambda i, tl: (tl[step, i] // nt, 0, tl[step, i] % nt))
    else:
        w_spec = pl.BlockSpec((d, tn), lambda i, tl: (0, tl[step, i] % nt))
    first = proj is None
    grid_spec = pltpu.PrefetchScalarGridSpec(
        num_scalar_prefetch=1, grid=(tiles.shape[1],),
        in_specs=([] if first else [HBM]) + [pl.BlockSpec((t, d), lambda i, tl: (0, 0)), w_spec],
        out_specs=pl.BlockSpec((t, tn), lambda i, tl: (0, tl[step, i])))
    return pl.pallas_call(
        body, name=name, grid_spec=grid_spec,
        out_shape=jax.ShapeDtypeStruct((t, N_CHIPS * ws), BF16),
        input_output_aliases={} if first else {1: 0},
        compiler_params=_params(("parallel",)),
    )(*([tiles] if first else [tiles, proj]), h, w)


def _shift_rows(a, rows):
    idx = lax.broadcasted_iota(jnp.int32, a.shape, 0)
    prev = jnp.where(idx == 0, 0.0, pltpu.roll(a, 1, 0))
    nxt = jnp.where(idx == rows - 1, 0.0, pltpu.roll(a, rows - 1, 0))
    return prev, nxt


def _conv_fwd(conv_proj, conv_w, conv_b, dc):
    t = conv_proj.shape[0]
    ct = CONV_TILE
    nct = dc // ct

    def body(u_ref, cg_ref, w_ref, b_ref, co_ref):
        a = cg_ref[...].astype(F32) * u_ref[...].astype(F32)
        prev, nxt = _shift_rows(a, t)
        co_ref[...] = (w_ref[0:1, :] * prev + w_ref[1:2, :] * a + w_ref[2:3, :] * nxt + b_ref[...]).astype(BF16)

    return pl.pallas_call(
        body, name="conv_fwd", grid=(nct,),
        out_shape=jax.ShapeDtypeStruct((t, dc), BF16),
        in_specs=[pl.BlockSpec((t, ct), lambda i: (0, i)), pl.BlockSpec((t, ct), lambda i: (0, 2 * nct + i)),
                  pl.BlockSpec((3, ct), lambda i: (0, i)), pl.BlockSpec((1, ct), lambda i: (0, i))],
        out_specs=pl.BlockSpec((t, ct), lambda i: (0, i)),
        compiler_params=_params(("parallel",)),
    )(conv_proj, conv_proj, conv_w, conv_b)


def _to_residue_major(src_ref, dst_ref, r):
    seq = src_ref.shape[0] // r
    for res in range(r):
        dst_ref[res * seq:(res + 1) * seq, :] = src_ref[pl.ds(res, seq, stride=r), :].astype(dst_ref.dtype)


def _stage_pitch(r):
    return r + STAGE_PAD if r % SUBLANES == 0 else r


def _stage_rows(t):
    return max(t // r * _stage_pitch(r) for _, r in BRANCHES)


def _branch_operands(token_refs, stage, dil, r):
    if r == 1:
        return list(token_refs)
    t = token_refs[0].shape[0]
    seq, pitch = t // r, _stage_pitch(r)
    for i, ref in enumerate(token_refs):
        if pitch == r:
            stage[0:t, :] = ref[...].astype(F32)
        else:
            for g in range(seq):
                stage[g * pitch:g * pitch + r, :] = ref[g * r:(g + 1) * r, :].astype(F32)
        for res in range(r):
            dil[i, res * seq:(res + 1) * seq, :] = stage[pl.ds(res, seq, stride=pitch), :].astype(BF16)
    return [dil.at[i] for i in range(len(token_refs))]


def _scaled_queries(q):
    return (q.astype(F32) * (HEAD_DIM ** -0.5)).astype(BF16)


BLOCK_SHIFTS = (0, -SIDE, None)


def _band_bias(rel, slope):
    arel = jnp.abs(rel)
    return jnp.where(arel <= SIDE, arel.astype(F32) * slope, NEG_INF)


def _fill_bias_tiles(bias_ref, sl_ref, r, kw):
    base = lax.broadcasted_iota(jnp.int32, (ATT_BQ, kw), 1) - lax.broadcasted_iota(jnp.int32, (ATT_BQ, kw), 0)
    for hh in range(2):
        slope = -(sl_ref[hh:hh + 1, 0:kw] * float(r))
        for e, shift in enumerate(BLOCK_SHIFTS):
            shift = ATT_BQ - kw if shift is None else shift
            bias_ref[hh, e, :, 0:kw] = _band_bias(base + shift, slope)


def _fill_stacked_bias_tiles(bias_ref, sl_ref, r, kw):
    base = lax.broadcasted_iota(jnp.int32, (kw, ATT_BQ), 0) - lax.broadcasted_iota(jnp.int32, (kw, ATT_BQ), 1)
    for hh in range(2):
        slope = -(sl_ref[hh:hh + 1, 0:ATT_BQ] * float(r))
        for e, shift in enumerate(BLOCK_SHIFTS):
            shift = ATT_BQ - kw if shift is None else shift
            bias_ref[e, 0:kw, hh * ATT_BQ:(hh + 1) * ATT_BQ] = _band_bias(base + shift, slope)


def _first_head_lanes():
    return lax.broadcasted_iota(jnp.int32, (1, PAIR), 1) < HEAD_DIM


def _only_head(x, first, hh):
    return jnp.where(first if hh == 0 else jnp.logical_not(first), x, jnp.zeros_like(x))


def _block_place(g, seq_len, kw):
    nqb = seq_len // ATT_BQ
    if nqb == 1:
        row = pl.multiple_of(g * ATT_BQ, ATT_BQ)
        return row, row, 0
    res = g // nqb
    qb = g - res * nqb
    q0 = qb * ATT_BQ
    ks = jnp.clip(q0 - SIDE, 0, seq_len - kw)
    edge = jnp.where(qb == 0, 0, jnp.where(qb == nqb - 1, 2, 1))
    return (pl.multiple_of(res * seq_len + q0, ATT_BQ), pl.multiple_of(res * seq_len + ks, SIDE), edge)


def _qkv_specs(dc, da, t, index):
    return [pl.BlockSpec((t, PAIR), functools.partial(index, (4 * dc + comp * da) // PAIR)) for comp in range(3)]


def _attn_fwd(proj, slopes, dc, da):
    t = proj.shape[0]
    hp = da // PAIR
    n_blocks = t // ATT_BQ

    def body(q_ref, k_ref, v_ref, sl_ref, o_ref, lse_ref, stage, dil, bias, o_res, l_res, o_tok, l_tok):
        for b, (_, r) in enumerate(BRANCHES):
            seq_len = t // r
            kw = min(ATT_KW, seq_len)
            ops = _branch_operands([q_ref, k_ref, v_ref], stage, dil, r)
            _fill_bias_tiles(bias, sl_ref, r, kw)
            o_dst, l_dst = (o_tok.at[b], l_tok.at[b]) if r == 1 else (o_res, l_res)
            first = _first_head_lanes()

            def blocks(trip, carry, seq_len=seq_len, kw=kw, o_dst=o_dst, l_dst=l_dst, first=first, ops=ops):
                nt = (((1,), (1,)), ((), ()))
                places = [_block_place(trip * ATT_UNROLL + i, seq_len, kw) for i in range(ATT_UNROLL)]
                chains = [(i, hh) for i in range(ATT_UNROLL) for hh in range(2)]
                qs = [_scaled_queries(ops[0][pl.ds(qrow, ATT_BQ), :]) for qrow, _, _ in places]
                ks = [ops[1][pl.ds(krow, kw), :] for _, krow, _ in places]
                vs = [ops[2][pl.ds(krow, kw), :] for _, krow, _ in places]
                ss = [lax.dot_general(_only_head(qs[i], first, hh), ks[i], nt, preferred_element_type=F32)
                      + bias[hh, places[i][2], :, 0:kw] for i, hh in chains]
                tops = [jnp.max(s, axis=-1, keepdims=True) for s in ss]
                ps = [jnp.exp(s - m) for s, m in zip(ss, tops)]
                dens = [jnp.sum(p, axis=-1, keepdims=True) for p in ps]
                for i, (qrow, _, _) in enumerate(places):
                    weights = jnp.concatenate([ps[2 * i].astype(BF16), ps[2 * i + 1].astype(BF16)], axis=1)
                    values = jnp.concatenate([_only_head(vs[i], first, 0), _only_head(vs[i], first, 1)], axis=0)
                    den = jnp.where(first, dens[2 * i], dens[2 * i + 1])
                    o_dst[pl.ds(qrow, ATT_BQ), :] = jnp.dot(weights, values, preferred_element_type=F32) / den
                    l_dst[pl.ds(qrow, ATT_BQ), :] = jnp.where(first, tops[2 * i], tops[2 * i + 1]) + jnp.log(den)
                return carry

            lax.fori_loop(0, n_blocks // ATT_UNROLL, blocks, 0)
            if r > 1:
                for res in range(r):
                    rows = slice(res * seq_len, (res + 1) * seq_len)
                    o_tok[b, pl.ds(res, seq_len, stride=r), :] = o_res[rows, :]
                    l_tok[b, pl.ds(res, seq_len, stride=r), :] = l_res[rows, :]

        def merge(i, carry):
            rows = pl.ds(pl.multiple_of(i * ROW_TILE, ROW_TILE), ROW_TILE)
            la, lb, lc = l_tok[0, rows, :], l_tok[1, rows, :], l_tok[2, rows, :]
            m = jnp.maximum(jnp.maximum(la, lb), lc)
            wa, wb, wc = jnp.exp(la - m), jnp.exp(lb - m), jnp.exp(lc - m)
            den = wa + wb + wc
            o_ref[rows, :] = (wa * o_tok[0, rows, :] + wb * o_tok[1, rows, :] + wc * o_tok[2, rows, :]) * (1.0 / den)
            lse_ref[rows, :] = m + jnp.log(den)
            return carry

        lax.fori_loop(0, t // ROW_TILE, merge, 0)

    pair_spec = pl.BlockSpec((None, t, PAIR), lambda h: (h, 0, 0))
    return pl.pallas_call(
        body, name="attn_fwd", grid=(hp,),
        out_shape=(jax.ShapeDtypeStruct((hp, t, PAIR), F32), jax.ShapeDtypeStruct((hp, t, PAIR), F32)),
        in_specs=_qkv_specs(dc, da, t, lambda first, h: (0, first + h))
        + [pl.BlockSpec((None, 8, ATT_KW), lambda h: (h, 0, 0))],
        out_specs=(pair_spec, pair_spec),
        scratch_shapes=[pltpu.VMEM((_stage_rows(t), PAIR), F32), pltpu.VMEM((3, t, PAIR), BF16),
                        pltpu.VMEM((2, 3, ATT_BQ, ATT_KW), F32),
                        pltpu.VMEM((t, PAIR), F32), pltpu.VMEM((t, PAIR), F32),
                        pltpu.VMEM((3, t, PAIR), F32), pltpu.VMEM((3, t, PAIR), F32)],
        compiler_params=_params(("parallel",)),
    )(proj, proj, proj, slopes)


def _attn_bwd(dproj, proj, d_o, lse, delta, slopes, dc, da, after):
    t = proj.shape[0]
    hp = da // PAIR
    n_blocks = t // ATT_BQ

    def all_branches(q_ref, k_ref, v_ref, do_ref, lse_ref, dl_ref, sl_ref,
                     stage, dil, packed, packed_res, row_vecs, bias_t, acc, tot):
        first = _first_head_lanes()
        lane = lax.broadcasted_iota(jnp.int32, (1, PAIR), 1)
        packed[...] = jnp.where((lane & (HEAD_DIM - 1)) < HEAD_DIM // 2, lse_ref[...], dl_ref[...])
        for b, (_, r) in enumerate(BRANCHES):
            seq_len = t // r
            kw = min(ATT_KW, seq_len)
            ops = _branch_operands([q_ref, k_ref, v_ref, do_ref], stage, dil, r)
            scalars = packed
            if r > 1:
                _to_residue_major(packed, packed_res, r)
                scalars = packed_res
            for g in range(n_blocks):
                flipped = scalars[g * ATT_BQ:(g + 1) * ATT_BQ, :].T
                for row in range(4):
                    row_vecs[g, row:row + 1, :] = flipped[row * (HEAD_DIM // 2):row * (HEAD_DIM // 2) + 1, :]
            _fill_stacked_bias_tiles(bias_t, sl_ref, r, kw)
            acc[1] = jnp.zeros((t, PAIR), F32)
            acc[2] = jnp.zeros((t, PAIR), F32)

            def blocks(trip, carry, seq_len=seq_len, kw=kw, ops=ops):
                nt = (((1,), (1,)), ((), ()))
                group = range(ATT_UNROLL)
                places = [_block_place(trip * ATT_UNROLL + i, seq_len, kw) for i in group]
                ks, vs, q2s, do2s, lse2s, dl2s = [], [], [], [], [], []
                for i, (qrow, krow, _) in zip(group, places):
                    q = _scaled_queries(ops[0][pl.ds(qrow, ATT_BQ), :])
                    dov = ops[3][pl.ds(qrow, ATT_BQ), :]
                    ks.append(ops[1][pl.ds(krow, kw), :])
                    vs.append(ops[2][pl.ds(krow, kw), :])
                    q2s.append(jnp.concatenate([_only_head(q, first, 0), _only_head(q, first, 1)], axis=0))
                    do2s.append(jnp.concatenate([_only_head(dov, first, 0), _only_head(dov, first, 1)], axis=0))
                    rows = row_vecs[trip * ATT_UNROLL + i]
                    lse2s.append(jnp.concatenate([rows[0:1, :], rows[2:3, :]], axis=1))
                    dl2s.append(jnp.concatenate([rows[1:2, :], rows[3:4, :]], axis=1))
                s_ts = [lax.dot_general(ks[i], q2s[i], nt, preferred_element_type=F32) for i in group]
                dp_ts = [lax.dot_general(vs[i], do2s[i], nt, preferred_element_type=F32) for i in group]
                p_ts = [jnp.exp(s_ts[i] + bias_t[places[i][2], 0:kw, :] - lse2s[i]) for i in group]
                ds_ts = [p_ts[i] * (dp_ts[i] - dl2s[i]) for i in group]
                dvs = [jnp.dot(p_ts[i].astype(BF16), do2s[i], preferred_element_type=F32) for i in group]
                dks = [jnp.dot(ds_ts[i].astype(BF16), q2s[i], preferred_element_type=F32) for i in group]
                dss = [ds_ts[i].T.astype(BF16) for i in group]
                dqs = [jnp.dot(dss[i][0:ATT_BQ, :], _only_head(ks[i], first, 0), preferred_element_type=F32)
                       + jnp.dot(dss[i][ATT_BQ:2 * ATT_BQ, :], _only_head(ks[i], first, 1), preferred_element_type=F32)
                       for i in group]
                for i, (qrow, krow, _) in zip(group, places):
                    acc[0, pl.ds(qrow, ATT_BQ), :] = dqs[i] * (HEAD_DIM ** -0.5)
                    acc[1, pl.ds(krow, kw), :] += dks[i]
                    acc[2, pl.ds(krow, kw), :] += dvs[i]
                return carry

            lax.fori_loop(0, n_blocks // ATT_UNROLL, blocks, 0)
            for comp in range(3):
                if r == 1:
                    tot[comp] = acc[comp]
                else:
                    for res in range(r):
                        tok = pl.ds(res, seq_len, stride=r)
                        tot[comp, tok, :] = tot[comp, tok, :] + acc[comp, res * seq_len:(res + 1) * seq_len, :]

    first_q = (4 * dc) // PAIR

    def body(dproj_in, q_ref, k_ref, v_ref, do_ref, lse_ref, dl_ref, sl_ref, after_ref, out_ref, *scratch):
        del dproj_in, after_ref
        work, out_stage, out_sems = scratch[:-2], scratch[-2], scratch[-1]
        h = pl.program_id(0)
        all_branches(q_ref, k_ref, v_ref, do_ref, lse_ref, dl_ref, sl_ref, *work)

        def out_copy(comp):
            cols = pl.ds(pl.multiple_of((first_q + comp * hp + h) * PAIR, PAIR), PAIR)
            return pltpu.make_async_copy(out_stage.at[comp], out_ref.at[:, cols], out_sems.at[comp])

        @pl.when(h > 0)
        def _():
            for comp in range(3):
                out_copy(comp).wait()

        for comp in range(3):
            out_stage[comp] = work[-1][comp].astype(BF16)
            out_copy(comp).start()

        @pl.when(h == hp - 1)
        def _():
            for comp in range(3):
                out_copy(comp).wait()

    pair_spec = pl.BlockSpec((None, t, PAIR), lambda h: (h, 0, 0))
    return pl.pallas_call(
        body, name="attn_bwd", grid=(hp,),
        out_shape=jax.ShapeDtypeStruct(dproj.shape, BF16),
        in_specs=[HBM] + _qkv_specs(dc, da, t, lambda first, h: (0, first + h))
        + [pair_spec, pair_spec, pair_spec, pl.BlockSpec((None, 8, ATT_KW), lambda h: (h, 0, 0)), ANY],
        out_specs=ANY,
        input_output_aliases={0: 0},
        scratch_shapes=[pltpu.VMEM((_stage_rows(t), PAIR), F32), pltpu.VMEM((4, t, PAIR), BF16),
                        pltpu.VMEM((t, PAIR), F32), pltpu.VMEM((t, PAIR), F32),
                        pltpu.VMEM((n_blocks, 8, ATT_BQ), F32), pltpu.VMEM((3, ATT_KW, 2 * ATT_BQ), F32),
                        pltpu.VMEM((3, t, PAIR), F32), pltpu.VMEM((3, t, PAIR), F32),
                        pltpu.VMEM((3, t, PAIR), BF16), pltpu.SemaphoreType.DMA((3,))],
        compiler_params=_params(("arbitrary",)),
    )(dproj, proj, proj, proj, d_o, lse, delta, slopes, after)


def _mix_fwd(co, proj, o_mix, g_conv, g_attn_pairs, after):
    t, dc = co.shape
    hp = o_mix.shape[0]
    da = hp * PAIR
    tb = ROW_TILE

    def body(co_ref, bg_ref, zc_ref, za_ref, om_ref, gc_ref, ga_ref, after_ref, ycat_ref, ycatt_ref):
        del after_ref
        p = bg_ref[...].astype(F32) * co_ref[...].astype(F32)
        rc = lax.rsqrt(jnp.mean(p * p, axis=-1, keepdims=True) + EPS)
        yc = (p * rc) * gc_ref[...] * _silu(zc_ref[...].astype(F32))
        ycat_ref[:, 0:dc] = yc.astype(BF16)
        ycatt_ref[0:dc, :] = yc.T.astype(BF16)
        ssq = jnp.zeros((tb, 1), F32)
        for h in range(hp):
            o = om_ref[h]
            ssq = ssq + jnp.sum(o * o, axis=-1, keepdims=True)
        ra = lax.rsqrt(ssq * (1.0 / da) + EPS)
        for h in range(hp):
            ya = (om_ref[h] * ra) * ga_ref[h] * _silu(za_ref[:, h * PAIR:(h + 1) * PAIR].astype(F32))
            ycat_ref[:, dc + h * PAIR:dc + (h + 1) * PAIR] = ya.astype(BF16)
            ycatt_ref[dc + h * PAIR:dc + (h + 1) * PAIR, :] = ya.T.astype(BF16)

    pair_spec = pl.BlockSpec((hp, tb, PAIR), lambda i: (0, i, 0))
    return pl.pallas_call(
        body, name="mix_fwd", grid=(t // tb,),
        out_shape=(jax.ShapeDtypeStruct((t, dc + da), BF16), jax.ShapeDtypeStruct((dc + da, t), BF16)),
        in_specs=[pl.BlockSpec((tb, dc), lambda i: (i, 0)),
                  pl.BlockSpec((tb, dc), lambda i: (i, 1)),
                  pl.BlockSpec((tb, dc), lambda i: (i, 3)),
                  pl.BlockSpec((tb, da), lambda i: (i, 7)),
                  pair_spec,
                  pl.BlockSpec((1, dc), lambda i: (0, 0)),
                  pl.BlockSpec((hp, 1, PAIR), lambda i: (0, 0, 0)), ANY],
        out_specs=(pl.BlockSpec((tb, dc + da), lambda i: (i, 0)), pl.BlockSpec((dc + da, tb), lambda i: (0, i))),
        compiler_params=_params(("parallel",)),
    )(co, proj, proj, proj, o_mix, g_conv, g_attn_pairs, after)


def _out_fwd_bwd(ycat, woutf, x, target, mod, g_post):
    t, d = x.shape
    n = ycat.shape[1]
    tb = ROW_TILE

    def body(a_ref, w_ref, x_ref, tg_ref, mod_ref, g_ref, dout_ref, dy_ref, acc_ref):
        y = jnp.dot(a_ref[...], w_ref[...], preferred_element_type=F32)
        r = lax.rsqrt(jnp.mean(y * y, axis=-1, keepdims=True) + EPS)
        nh = y * r
        gate = mod_ref[2:3, :]
        nrm = nh * g_ref[...]
        err = x_ref[...] + gate * nrm - tg_ref[...]
        dout = err * (1.0 / d)
        dout_ref[...] = dout.astype(BF16)
        dn = dout * gate
        a = dn * g_ref[...]
        dy = r * (a - nh * jnp.mean(a * nh, axis=-1, keepdims=True))
        dy_ref[...] = dy.astype(BF16)
        loss = 0.5 * jnp.sum(jnp.sum(err * err, axis=-1, keepdims=True) * (1.0 / d), axis=0, keepdims=True)
        part = jnp.concatenate(
            [jnp.sum(dout * nrm, axis=0, keepdims=True), jnp.sum(dn * nh, axis=0, keepdims=True),
             jnp.broadcast_to(loss, (1, d)), jnp.zeros((5, d), F32)], axis=0)

        @pl.when(pl.program_id(0) == 0)
        def _():
            acc_ref[...] = jnp.zeros(acc_ref.shape, F32)

        acc_ref[...] += part

    return pl.pallas_call(
        body, name="out_fwd_bwd", grid=(t // tb,),
        out_shape=(jax.ShapeDtypeStruct((t, d), BF16), jax.ShapeDtypeStruct((t, d), BF16),
                   jax.ShapeDtypeStruct((8, d), F32)),
        in_specs=[pl.BlockSpec((tb, n), lambda i: (i, 0)), pl.BlockSpec((n, d), lambda i: (0, 0)),
                  pl.BlockSpec((tb, d), lambda i: (i, 0)), pl.BlockSpec((tb, d), lambda i: (i, 0)),
                  pl.BlockSpec((3, d), lambda i: (0, 0)), pl.BlockSpec((1, d), lambda i: (0, 0))],
        out_specs=(pl.BlockSpec((tb, d), lambda i: (i, 0)), pl.BlockSpec((tb, d), lambda i: (i, 0)),
                   pl.BlockSpec((8, d), lambda i: (0, 0))),
        compiler_params=_params(("arbitrary",)),
    )(ycat, woutf, x, target, mod, g_post)


def _matmul_nt(a, b, out_dtype, name):
    m, k = a.shape
    n = b.shape[0]
    tn = COL_TILE

    def body(a_ref, b_ref, o_ref):
        o_ref[...] = lax.dot_general(a_ref[...], b_ref[...], (((1,), (1,)), ((), ())),
                                     preferred_element_type=F32).astype(out_dtype)

    return pl.pallas_call(
        body, name=name, grid=(n // tn,),
        out_shape=jax.ShapeDtypeStruct((m, n), out_dtype),
        in_specs=[pl.BlockSpec((m, k), lambda i: (0, 0)), pl.BlockSpec((tn, k), lambda i: (i, 0))],
        out_specs=pl.BlockSpec((m, tn), lambda i: (0, i)),
        compiler_params=_params(("parallel",)),
    )(a, b)


def _mix_bwd(dycat, co, proj, o_mix, g_conv, g_attn_pairs):
    t, dc = co.shape
    hp = o_mix.shape[0]
    da = hp * PAIR
    tb = ROW_TILE

    def body(dy_ref, co_ref, bg_ref, zc_ref, za_ref, om_ref, gc_ref, ga_ref,
             dcp_ref, dco_ref, do_ref, dl_ref, dgc_ref, dga_ref):
        first = pl.program_id(0) == 0
        cov = co_ref[...].astype(F32)
        bg = bg_ref[...].astype(F32)
        zc = zc_ref[...].astype(F32)
        p = bg * cov
        rc = lax.rsqrt(jnp.mean(p * p, axis=-1, keepdims=True) + EPS)
        nh = p * rc
        dyc = dy_ref[:, 0:dc].astype(F32)
        dn = dyc * _silu(zc)
        a = dn * gc_ref[...]
        dp = rc * (a - nh * jnp.mean(a * nh, axis=-1, keepdims=True))
        dcp_ref[:, 0:dc] = jnp.zeros((tb, dc), BF16)
        dcp_ref[:, dc:2 * dc] = (dp * cov).astype(BF16)
        dcp_ref[:, 2 * dc:3 * dc] = jnp.zeros((tb, dc), BF16)
        dcp_ref[:, 3 * dc:4 * dc] = (dyc * nh * gc_ref[...] * _silu_grad(zc)).astype(BF16)
        dcp_ref[:, 4 * dc:4 * dc + 3 * da] = jnp.zeros((tb, 3 * da), BF16)
        dco_ref[...] = dp * bg

        @pl.when(first)
        def _():
            dgc_ref[...] = jnp.zeros(dgc_ref.shape, F32)
            dga_ref[...] = jnp.zeros(dga_ref.shape, F32)

        dgc_ref[...] += jnp.sum(dn * nh, axis=0, keepdims=True)

        ssq = jnp.zeros((tb, 1), F32)
        for h in range(hp):
            o = om_ref[h]
            ssq = ssq + jnp.sum(o * o, axis=-1, keepdims=True)
        ra = lax.rsqrt(ssq * (1.0 / da) + EPS)
        dot_an = jnp.zeros((tb, 1), F32)
        for h in range(hp):
            nha = om_ref[h] * ra
            za = za_ref[:, h * PAIR:(h + 1) * PAIR].astype(F32)
            dya = dy_ref[:, dc + h * PAIR:dc + (h + 1) * PAIR].astype(F32)
            dna = dya * _silu(za)
            dza = (dya * nha * ga_ref[h] * _silu_grad(za)).astype(BF16)
            dcp_ref[:, 4 * dc + 3 * da + h * PAIR:4 * dc + 3 * da + (h + 1) * PAIR] = dza
            dga_ref[h] += jnp.sum(dna * nha, axis=0, keepdims=True)
            dot_an = dot_an + jnp.sum(dna * ga_ref[h] * nha, axis=-1, keepdims=True)
        mean_an = dot_an * (1.0 / da)
        first_head = lax.broadcasted_iota(jnp.int32, (tb, PAIR), 1) < HEAD_DIM
        for h in range(hp):
            o = om_ref[h]
            nha = o * ra
            za = za_ref[:, h * PAIR:(h + 1) * PAIR].astype(F32)
            dya = dy_ref[:, dc + h * PAIR:dc + (h + 1) * PAIR].astype(F32)
            aa = dya * _silu(za) * ga_ref[h]
            d_o = ra * (aa - nha * mean_an)
            do_ref[h] = d_o.astype(BF16)
            prod = d_o * o
            both = jnp.sum(prod, axis=-1, keepdims=True)
            head0 = jnp.sum(jnp.where(first_head, prod, 0.0), axis=-1, keepdims=True)
            dl_ref[h] = jnp.where(first_head, head0, both - head0)

    pair_spec = pl.BlockSpec((hp, tb, PAIR), lambda i: (0, i, 0))
    return pl.pallas_call(
        body, name="mix_bwd", grid=(t // tb,),
        out_shape=(jax.ShapeDtypeStruct((t, 4 * dc + 4 * da), BF16), jax.ShapeDtypeStruct((t, dc), F32),
                   jax.ShapeDtypeStruct((hp, t, PAIR), BF16), jax.ShapeDtypeStruct((hp, t, PAIR), F32),
                   jax.ShapeDtypeStruct((1, dc), F32), jax.ShapeDtypeStruct((hp, 1, PAIR), F32)),
        in_specs=[pl.BlockSpec((tb, dc + da), lambda i: (i, 0)),
                  pl.BlockSpec((tb, dc), lambda i: (i, 0)),
                  pl.BlockSpec((tb, dc), lambda i: (i, 1)),
                  pl.BlockSpec((tb, dc), lambda i: (i, 3)),
                  pl.BlockSpec((tb, da), lambda i: (i, 7)),
                  pair_spec,
                  pl.BlockSpec((1, dc), lambda i: (0, 0)),
                  pl.BlockSpec((hp, 1, PAIR), lambda i: (0, 0, 0))],
        out_specs=(pl.BlockSpec((tb, 4 * dc + 4 * da), lambda i: (i, 0)), pl.BlockSpec((tb, dc), lambda i: (i, 0)),
                   pair_spec, pair_spec,
                   pl.BlockSpec((1, dc), lambda i: (0, 0)), pl.BlockSpec((hp, 1, PAIR), lambda i: (0, 0, 0))),
        compiler_params=_params(("arbitrary",)),
    )(dycat, co, proj, proj, proj, o_mix, g_conv, g_attn_pairs)


def _conv_bwd(dconv_proj, dco, conv_proj, conv_w, dc, after):
    t = dco.shape[0]
    ct = CONV_TILE
    nct = dc // ct

    def body(dcp_in_ref, dco_ref, u_ref, cg_ref, w_ref, after_ref, dcp_ref, acc_ref):
        del dcp_in_ref, after_ref
        which = pl.program_id(1)
        g = dco_ref[...]
        u = u_ref[...].astype(F32)
        cg = cg_ref[...].astype(F32)
        g_prev, g_next = _shift_rows(g, t)
        da = w_ref[0:1, :] * g_next + w_ref[1:2, :] * g + w_ref[2:3, :] * g_prev
        dcp_ref[...] = (da * jnp.where(which == 0, cg, u)).astype(BF16)
        a = cg * u
        a_prev, a_next = _shift_rows(a, t)
        acc_ref[...] = jnp.concatenate(
            [jnp.sum(g * a_prev, axis=0, keepdims=True), jnp.sum(g * a, axis=0, keepdims=True),
             jnp.sum(g * a_next, axis=0, keepdims=True), jnp.sum(g, axis=0, keepdims=True),
             jnp.zeros((4, ct), F32)], axis=0)

    return pl.pallas_call(
        body, name="conv_bwd", grid=(nct, 2),
        out_shape=(jax.ShapeDtypeStruct(dconv_proj.shape, BF16), jax.ShapeDtypeStruct((8, dc), F32)),
        in_specs=[HBM,
                  pl.BlockSpec((t, ct), lambda i, s: (0, i)),
                  pl.BlockSpec((t, ct), lambda i, s: (0, i)),
                  pl.BlockSpec((t, ct), lambda i, s: (0, 2 * nct + i)),
                  pl.BlockSpec((3, ct), lambda i, s: (0, i)), ANY],
        out_specs=(pl.BlockSpec((t, ct), lambda i, s: (0, 2 * s * nct + i)),
                   pl.BlockSpec((8, ct), lambda i, s: (0, i))),
        input_output_aliases={0: 0},
        compiler_params=_params(("arbitrary", "arbitrary")),
    )(dconv_proj, dco, conv_proj, conv_proj, conv_w, after)


def _dh(dproj, winf, after):
    t = dproj.shape[0]
    _, d, ws = winf.shape
    tm = tn = COL_TILE
    nt = (((1,), (1,)), ((), ()))

    def body(a_ref, w_ref, after_ref, o_ref):
        del after_ref
        acc = lax.dot_general(a_ref[:, 0:ws], w_ref[0], nt, preferred_element_type=F32)
        for j in range(1, N_CHIPS):
            acc = acc + lax.dot_general(a_ref[:, j * ws:(j + 1) * ws], w_ref[j], nt, preferred_element_type=F32)
        o_ref[...] = acc.astype(BF16)

    return pl.pallas_call(
        body, name="dh", grid=(d // tn, t // tm),
        out_shape=jax.ShapeDtypeStruct((t, d), BF16),
        in_specs=[pl.BlockSpec((tm, N_CHIPS * ws), lambda n, m: (m, 0)),
                  pl.BlockSpec((N_CHIPS, tn, ws), lambda n, m: (0, n, 0)), ANY],
        out_specs=pl.BlockSpec((tm, tn), lambda n, m: (m, n)),
        compiler_params=_params(("parallel", "parallel")),
    )(dproj, winf, after)


def _prenorm_bwd(x, dh, dout, mod, g_pre):
    t, d = x.shape
    tb = ROW_TILE

    def body(x_ref, dh_ref, dout_ref, mod_ref, g_ref, gx_ref, acc_ref):
        xv = x_ref[...]
        dhv = dh_ref[...].astype(F32)
        r = lax.rsqrt(jnp.mean(xv * xv, axis=-1, keepdims=True) + EPS)
        xh = xv * r
        one_scale = 1.0 + mod_ref[1:2, :]
        a = dhv * one_scale * g_ref[...]
        gx_ref[...] = dout_ref[...].astype(F32) + r * (a - xh * jnp.mean(a * xh, axis=-1, keepdims=True))
        part = jnp.concatenate(
            [jnp.sum(dhv, axis=0, keepdims=True), jnp.sum(dhv * xh * g_ref[...], axis=0, keepdims=True),
             jnp.sum(dhv * xh * one_scale, axis=0, keepdims=True), jnp.zeros((5, d), F32)], axis=0)

        @pl.when(pl.program_id(0) == 0)
        def _():
            acc_ref[...] = jnp.zeros(acc_ref.shape, F32)

        acc_ref[...] += part

    return pl.pallas_call(
        body, name="prenorm_bwd", grid=(t // tb,),
        out_shape=(jax.ShapeDtypeStruct((t, d), F32), jax.ShapeDtypeStruct((8, d), F32)),
        in_specs=[pl.BlockSpec((tb, d), lambda i: (i, 0)), pl.BlockSpec((tb, d), lambda i: (i, 0)),
                  pl.BlockSpec((tb, d), lambda i: (i, 0)), pl.BlockSpec((3, d), lambda i: (0, 0)),
                  pl.BlockSpec((1, d), lambda i: (0, 0))],
        out_specs=(pl.BlockSpec((tb, d), lambda i: (i, 0)), pl.BlockSpec((8, d), lambda i: (0, 0))),
        compiler_params=_params(("arbitrary",)),
    )(x, dh, dout, mod, g_pre)


def _chip_sums(mine, rsib, name, part=0, parts=1, after=()):
    _, half, cols = mine.shape
    rows = half // parts
    tr = min(rows, ROW_TILE)
    nt = rows // tr

    def body(g_ref, r_ref, *rest):
        rest[-1][...] = (g_ref[...].astype(F32) + r_ref[...].astype(F32)).astype(BF16)

    spec = pl.BlockSpec((None, tr, cols), lambda j, i: (j, part * nt + i, 0))
    return pl.pallas_call(
        body, name=name, grid=(N_CHIPS, nt),
        out_shape=jax.ShapeDtypeStruct((N_CHIPS, rows, cols), BF16),
        in_specs=[spec, spec] + [ANY] * len(after), out_specs=pl.BlockSpec((None, tr, cols), lambda j, i: (j, i, 0)),
        compiler_params=_params(("parallel", "parallel")),
    )(mine, rsib, *after)


def _owner_sum(place, mine, rsib, rici, name, part=0, parts=1):
    _, half, cols = mine.shape
    rows = half // parts
    tr = min(rows, ROW_TILE)
    nt = rows // tr

    def body(place_ref, g_ref, r_ref, i_ref, o_ref):
        del place_ref
        acc = g_ref[...].astype(F32) + r_ref[...].astype(F32)
        for k in range(N_CHIPS - 1):
            acc = acc + i_ref[k].astype(F32)
        o_ref[...] = acc

    own = pl.BlockSpec((None, tr, cols), lambda i, p: (p[0], part * nt + i, 0))
    grid_spec = pltpu.PrefetchScalarGridSpec(
        num_scalar_prefetch=1, grid=(nt,),
        in_specs=[own, own, pl.BlockSpec((N_CHIPS - 1, tr, cols), lambda i, p: (0, i, 0))],
        out_specs=pl.BlockSpec((tr, cols), lambda i, p: (p[1] * (half // tr) + part * nt + i, 0)))
    return pl.pallas_call(
        body, name=name, grid_spec=grid_spec,
        out_shape=jax.ShapeDtypeStruct((2 * half, cols), F32),
        compiler_params=_params(("parallel",)),
    )(place, mine, rsib, rici)


def _adam_math(w, g, m, v):
    m2 = ADAM_B1 * m + (1.0 - ADAM_B1) * g
    v2 = ADAM_B2 * v + (1.0 - ADAM_B2) * (g * g)
    m_hat = m2 / (1.0 - ADAM_B1 ** ADAM_STEP)
    v_hat = v2 / (1.0 - ADAM_B2 ** ADAM_STEP)
    delta = -ADAM_LR * (m_hat / (jnp.sqrt(v_hat) + ADAM_EPS) + ADAM_WD * w)
    return delta, m2, v2


def _adamw(w, g, m, v, name, part=0, parts=1, prev=None):
    rows, cols = w.shape
    tr = min(rows, ROW_TILE)

    def body(*refs):
        w_ref, g_ref, m_ref, v_ref, go_ref, d_ref, m2_ref, v2_ref = refs[-8:]
        g = g_ref[...]
        go_ref[...] = g
        d_ref[...], m2_ref[...], v2_ref[...] = _adam_math(w_ref[...], g, m_ref[...], v_ref[...])

    if parts == 1:
        grid, spec = (rows // tr,), pl.BlockSpec((tr, cols), lambda i: (i, 0))
    else:
        per_half = rows // 2 // tr
        nt = per_half // parts
        grid, spec = (2, nt), pl.BlockSpec((tr, cols), lambda r, i: (r * per_half + part * nt + i, 0))
    olds = [] if prev is None else list(prev)
    return pl.pallas_call(
        body, name=name, grid=grid,
        out_shape=(jax.ShapeDtypeStruct(w.shape, F32),) * 4,
        in_specs=[HBM] * len(olds) + [spec] * 4, out_specs=(spec,) * 4,
        input_output_aliases={i: i for i in range(len(olds))},
        compiler_params=_params(("parallel",) * len(grid)),
    )(*olds, w, g, m, v)


def _ada_grad_adamw(c_all, dmod_cols, w, m, v):
    d, wa = w.shape
    tr = ROW_TILE

    def body(c_ref, dm_ref, w_ref, m_ref, v_ref, g_ref, d_ref, m2_ref, v2_ref):
        act = _silu(c_ref[...]).T
        g = act[:, 0:1] * dm_ref[0:1, :]
        for b in range(1, N_DEV):
            g = g + act[:, b:b + 1] * dm_ref[b:b + 1, :]
        g_ref[...] = g
        d_ref[...], m2_ref[...], v2_ref[...] = _adam_math(w_ref[...], g, m_ref[...], v_ref[...])

    spec = pl.BlockSpec((tr, wa), lambda i: (i, 0))
    return pl.pallas_call(
        body, name="ada_grad_adamw", grid=(d // tr,),
        out_shape=(jax.ShapeDtypeStruct(w.shape, F32),) * 4,
        in_specs=[pl.BlockSpec((N_DEV, tr), lambda i: (0, i)), pl.BlockSpec((N_DEV, wa), lambda i: (0, 0)),
                  spec, spec, spec],
        out_specs=(spec,) * 4,
        compiler_params=_params(("parallel",)),
    )(c_all, dmod_cols, w, m, v)


def _small_update(place, gathered, pieces, weights, moments_m, moments_v):
    n = gathered.shape[1]
    k = len(weights)
    final_shapes = [w.shape for w in weights]
    row_counts = [s[1] if len(s) == 3 else 1 for s in final_shapes]
    weights, moments_m, moments_v = ([a.reshape(1, -1) for a in arrays] for arrays in (weights, moments_m, moments_v))

    def body(place_ref, g_ref, *refs):
        w_refs, m_refs, v_refs = refs[0:k], refs[k:2 * k], refs[2 * k:3 * k]
        outs = refs[3 * k:]
        total = g_ref[0:SUBLANES, :]
        for dev in range(1, N_DEV):
            total = total + g_ref[SUBLANES * dev:SUBLANES * (dev + 1), :]

        def flat(offset, length):
            segments, pos = [], offset
            while pos < offset + length:
                row, col = divmod(pos, n)
                take = min(offset + length - pos, n - col)
                segments.append(total[row:row + 1, col:col + take])
                pos += take
            return jnp.concatenate(segments, axis=1) if len(segments) > 1 else segments[0]

        chip = place_ref[0]
        for i, (w_ref, m_ref, v_ref) in enumerate(zip(w_refs, m_refs, v_refs)):
            g = flat(*pieces[i])
            if pieces[i][1] > w_ref.shape[1]:
                rows = row_counts[i]
                cols, full = w_ref.shape[1] // rows, pieces[i][1] // rows
                picked = []
                for r in range(rows):
                    blocks = [g[:, r * full + q * cols:r * full + (q + 1) * cols] for q in range(N_CHIPS)]
                    mine = blocks[N_CHIPS - 1]
                    for q in range(N_CHIPS - 2, -1, -1):
                        mine = jnp.where(chip == q, blocks[q], mine)
                    picked.append(mine)
                g = jnp.concatenate(picked, axis=1)
            delta, m2, v2 = _adam_math(w_ref[...], g, m_ref[...], v_ref[...])
            for j, val in enumerate((g, delta, m2, v2)):
                outs[j * k + i][...] = val
        outs[4 * k][...] = flat(*pieces[k])

    shapes = [jax.ShapeDtypeStruct(w.shape, F32) for w in weights]
    grid_spec = pltpu.PrefetchScalarGridSpec(
        num_scalar_prefetch=1, grid=(1,),
        in_specs=[pl.BlockSpec(gathered.shape, lambda i, p: (0, 0))]
        + [pl.BlockSpec(a.shape, functools.partial(lambda nd, i, p: (0,) * nd, a.ndim))
           for a in (*weights, *moments_m, *moments_v)],
        out_specs=tuple(pl.BlockSpec(s.shape, functools.partial(lambda nd, i, p: (0,) * nd, len(s.shape)))
                        for s in shapes * 4) + (pl.BlockSpec((1, LANES), lambda i, p: (0, 0)),))
    outs = pl.pallas_call(
        body, name="small_update", grid_spec=grid_spec,
        out_shape=tuple(shapes * 4) + (jax.ShapeDtypeStruct((1, LANES), F32),),
        compiler_params=_params(("arbitrary",)),
    )(place, gathered, *weights, *moments_m, *moments_v)
    shaped = [out.reshape(final_shapes[i % k]) for i, out in enumerate(outs[0:4 * k])]
    return shaped[0:k], shaped[k:2 * k], shaped[2 * k:3 * k], shaped[3 * k:4 * k], outs[4 * k]


def _pack_small(pieces):
    flat = [p.reshape(-1).astype(F32) for p in pieces]
    offsets, total = [], 0
    for p in flat:
        offsets.append(total)
        total += p.shape[0]
    padded = -(-total // SMALL_ALIGN) * SMALL_ALIGN
    if padded > total:
        flat.append(jnp.zeros((padded - total,), F32))
    return jnp.concatenate(flat).reshape(8, padded // 8), offsets


def _alibi_slope_rows(n_heads):
    slopes = 2.0 ** (-8.0 * np.arange(1, n_heads + 1, dtype=np.float64) / n_heads)
    rows = np.zeros((n_heads // 2, SUBLANES), np.float32)
    rows[:, 0:2] = slopes.reshape(n_heads // 2, 2)
    return jnp.asarray(np.broadcast_to(rows[:, :, None], (n_heads // 2, SUBLANES, ATT_KW)))


def kernel(x, c, w_ada, b_ada, g_pre, w_in, conv_w, conv_b, g_conv, g_attn, w_out, g_post, loss_target, m_w_ada, m_b_ada, m_g_pre, m_w_in, m_conv_w, m_conv_b, m_g_conv, m_g_attn, m_w_out, m_g_post, v_w_ada, v_b_ada, v_g_pre, v_w_in, v_conv_w, v_conv_b, v_g_conv, v_g_attn, v_w_out, v_g_post):
    t, d = x.shape[1], x.shape[2]
    dc = conv_b.shape[1]
    da = g_attn.shape[1]
    hp = da // PAIR
    ws = w_in.shape[2]
    wa = w_ada.shape[2]
    cws = conv_w.shape[2]
    assert t % ROW_TILE == 0 and d % ROW_TILE == 0 and dc % COL_TILE == 0 and da % COL_TILE == 0
    assert ws == 2 * dc and dc == da and t // BRANCHES[-1][1] >= ATT_BQ

    mx, my, mc = _my_place()
    chip = _chip_of(mx, my)
    dev = 2 * chip + mc
    place = jnp.stack([chip, mc]).astype(jnp.int32)

    x2, tgt2 = x[0], loss_target[0]
    w_ada2, w_in2, w_out2 = w_ada[0], w_in[0], w_out[0]

    packed, offs = _pack_small([c[0], conv_w[0]])
    seen, mod, win_slots = _ada_modulation(packed, w_ada2, b_ada, d, w_in2)
    seen = seen.reshape(N_DEV, -1)
    c_all = seen[:, offs[0]:offs[0] + d]
    conv_w_full = seen[0::2, offs[1]:offs[1] + 3 * cws].reshape(N_CHIPS, 3, cws).transpose(1, 0, 2).reshape(3, dc)

    win_flight, send_in, recv_in, started = _gather_start(win_slots, mod)

    y_chip, x_chip, d_chip = (_chip_of(mx, 1 - my), _chip_of(1 - mx, my), _chip_of(1 - mx, 1 - my))
    tiles_per_part = ws // COL_TILE // 2

    def tiles_of(chunk, parts):
        return [(2 * chunk + part) * tiles_per_part + k for part in parts for k in range(tiles_per_part)]

    tiles = jnp.stack([jnp.stack(step) for step in (
        tiles_of(chip, (0, 1)), tiles_of(y_chip, (0,)) + tiles_of(x_chip, (1,)),
        tiles_of(y_chip, (1,)) + tiles_of(x_chip, (0,)), tiles_of(d_chip, (0, 1)))]).astype(jnp.int32)
    h, ht = _prenorm(x2, mod, g_pre, started)
    proj = _proj_tiles(None, h, w_in2, tiles, 0, "proj_own")
    win_flight, wout_flight, relay_send_in, relay_recv_in, send_out, recv_out = _gather_relay_in(
        win_flight, _cast_into_slot(place, w_out2, "cast_w_out", proj), recv_in, proj)
    win_flight = _forward_halves(win_flight, ((0, 0), (1, 1)), "forward_w_in_first")
    proj = _proj_tiles(proj, h, win_flight, tiles, 1, "proj_first_parts")
    win_flight = _forward_halves(
        _gather_wait_direct(win_flight, send_in, recv_in, proj, "gather_wait_w_in_direct"),
        ((0, 1), (1, 0)), "forward_w_in_second")
    proj = _proj_tiles(proj, h, win_flight, tiles, 2, "proj_second_parts")
    winf = _forward_halves(
        _gather_wait_relayed(win_flight, relay_send_in, relay_recv_in, proj, "gather_wait_w_in_relayed"),
        ((2, None),), "forward_w_in_relayed")
    proj = _proj_tiles(proj, h, winf, tiles, 3, "proj_diagonal")
    slopes = _alibi_slope_rows(da // HEAD_DIM)
    co = _conv_fwd(proj, conv_w_full, conv_b, dc)
    wout_flight, relay_send_out, relay_recv_out = _gather_relay_out(wout_flight, recv_out, co)
    o_mix, lse = _attn_fwd(proj, slopes, dc, da)
    g_attn_pairs = g_attn.reshape(hp, 1, PAIR)
    wout_flight = _gather_wait_direct(wout_flight, send_out, recv_out, o_mix, "gather_wait_w_out_direct")
    wout_flight = _gather_wait_relayed(wout_flight, relay_send_out, relay_recv_out, o_mix, "gather_wait_w_out_relayed")
    all_halves = ((0, None), (1, None), (2, None))
    wout_flight, fsend_out, frecv_out, forwarding = _forward_start(wout_flight, all_halves, "forward_w_out_start")
    ycat, ycat_t = _mix_fwd(co, proj, o_mix, g_conv, g_attn_pairs, forwarding)
    woutf = _forward_wait(wout_flight, all_halves, fsend_out, frecv_out, ycat, "forward_w_out_wait").reshape(dc + da, d)
    dout, dy, post_sums = _out_fwd_bwd(ycat, woutf, x2, tgt2, mod, g_post)

    gout, rsib_out = _dw_swapped(ycat_t, dy, N_CHIPS, 1, "dw_out")
    csum_out = _chip_sums(gout, rsib_out, "rs_chip_sum_out")
    ssem_out, rsem_out, csum_out, land_out, sent_out = _owners_start(csum_out, "rs_owners_start_out")
    dycat = _matmul_nt(dy, woutf, BF16, "dycat")
    dproj, dco, d_o, delta, dg_conv, dg_attn = _mix_bwd(dycat, co, proj, o_mix, g_conv, g_attn_pairs)
    dproj, conv_sums = _conv_bwd(dproj, dco, proj, conv_w_full, dc, sent_out)
    dproj = _attn_bwd(dproj, proj, d_o, lse, delta, slopes, dc, da, sent_out)
    gin, rsib_in = _dw_swapped(ht, dproj, 1, N_CHIPS, "dw_in")
    ssem_in0, rsem_in0, csum_in0, land_in0, sent_in0 = _owners_start(
        _chip_sums(gin, rsib_in, "rs_chip_sum_in0", 0, 2), "rs_owners_start_in0")
    ssem_in1, rsem_in1, csum_in1, land_in1, sent_in = _owners_start(
        _chip_sums(gin, rsib_in, "rs_chip_sum_in1", 1, 2, after=(sent_in0,)), "rs_owners_start_in1")
    dh = _dh(dproj, winf, sent_in)
    grad_x, pre_sums = _prenorm_bwd(x2, dh, dout, mod, g_pre)

    small, so = _pack_small([
        pre_sums[0], pre_sums[1], post_sums[0],
        pre_sums[2], conv_sums[0:3], conv_sums[3], dg_conv, dg_attn, post_sums[1], post_sums[2, 0:128]])
    ssem_small, rsem_small, small, land_small, sent_small = _allgather8_start(small, dev, "gather_small_start")

    rici_out = _owners_wait(ssem_out, rsem_out, csum_out, land_out, [grad_x, sent_small], "rs_owners_wait_out")
    full_out, jsend_out, jrecv_out, joining_out = _join_start(
        _owner_sum(place, gout, rsib_out, rici_out, "rs_owner_sum_out"), "rs_join_start_out", 0, 1)
    rici_in = _owners_wait(ssem_in0, rsem_in0, csum_in0, land_in0, [joining_out], "rs_owners_wait_in0")
    full_in0, jsend0, jrecv0, joining0 = _join_start(
        _owner_sum(place, gin, rsib_in, rici_in, "rs_owner_sum_in0", 0, 2), "rs_join_start_in0", 0, 2)
    grad_w_out = _join_wait(full_out, jsend_out, jrecv_out, [joining0], "rs_join_wait_out", 0, 1)
    grad_w_out, delta_w_out, new_m_w_out, new_v_w_out = _adamw(
        w_out2, grad_w_out, m_w_out[0], v_w_out[0], "adamw_w_out")
    full_in0 = _join_wait(full_in0, jsend0, jrecv0, [delta_w_out], "rs_join_wait_in0", 0, 2)
    updated_in = _adamw(w_in2, full_in0, m_w_in[0], v_w_in[0], "adamw_w_in0", 0, 2)
    rici_in = _owners_wait(ssem_in1, rsem_in1, csum_in1, land_in1, [updated_in[1]], "rs_owners_wait_in1")
    full_in1, jsend1, jrecv1, joining1 = _join_start(
        _owner_sum(place, gin, rsib_in, rici_in, "rs_owner_sum_in1", 1, 2), "rs_join_start_in1", 1, 2)

    small_seen = _allgather8_wait(ssem_small, rsem_small, small, land_small, [joining1], "gather_small_wait")
    small_w = [b_ada, g_pre, conv_w, conv_b, g_conv, g_attn, g_post]
    small_m = [m_b_ada, m_g_pre, m_conv_w, m_conv_b, m_g_conv, m_g_attn, m_g_post]
    small_v = [v_b_ada, v_g_pre, v_conv_w, v_conv_b, v_g_conv, v_g_attn, v_g_post]
    pieces = [(0, 3 * d), (so[3], d), (so[4], 3 * dc), (so[5], dc), (so[6], dc), (so[7], da), (so[8], d), (so[9], LANES)]
    g_small, d_small, m_small, v_small, loss_row = _small_update(place, small_seen, pieces, small_w, small_m, small_v)
    loss = loss_row[0, 0]
    grad_b_ada, grad_g_pre, grad_conv_w, grad_conv_b, grad_g_conv, grad_g_attn, grad_g_post = g_small
    dmod_cols = lax.dynamic_slice_in_dim(small_seen.reshape(N_DEV, -1), chip * wa, wa, axis=1)
    grad_w_ada, delta_w_ada, new_m_w_ada, new_v_w_ada = _ada_grad_adamw(c_all, dmod_cols, w_ada2, m_w_ada[0], v_w_ada[0])

    full_in1 = _join_wait(full_in1, jsend1, jrecv1, [delta_w_ada, d_small[0]], "rs_join_wait_in1", 1, 2)
    grad_w_in, delta_w_in, new_m_w_in, new_v_w_in = _adamw(
        w_in2, full_in1, m_w_in[0], v_w_in[0], "adamw_w_in1", 1, 2, updated_in)

    def lead(a):
        return a.reshape((1,) + a.shape)

    grads = [lead(grad_w_ada), grad_b_ada, grad_g_pre, lead(grad_w_in), grad_conv_w, grad_conv_b, grad_g_conv,
             grad_g_attn, lead(grad_w_out), grad_g_post]
    deltas = [lead(delta_w_ada), d_small[0], d_small[1], lead(delta_w_in), d_small[2], d_small[3], d_small[4],
              d_small[5], lead(delta_w_out), d_small[6]]
    new_ms = [lead(new_m_w_ada), m_small[0], m_small[1], lead(new_m_w_in), m_small[2], m_small[3], m_small[4],
              m_small[5], lead(new_m_w_out), m_small[6]]
    new_vs = [lead(new_v_w_ada), v_small[0], v_small[1], lead(new_v_w_in), v_small[2], v_small[3], v_small[4],
              v_small[5], lead(new_v_w_out), v_small[6]]
    return (loss, lead(grad_x), *grads, *deltas, *new_ms, *new_vs)
```

```python
import functools

import jax
import jax.numpy as jnp
import numpy as np
from jax import lax
from jax.experimental import pallas as pl
from jax.experimental.pallas import tpu as pltpu

F32 = jnp.float32
BF16 = jnp.bfloat16
MESH = pl.DeviceIdType.MESH
HBM = pl.BlockSpec(memory_space=pltpu.HBM)
VMEM = pl.BlockSpec(memory_space=pltpu.VMEM)
ANY = pl.BlockSpec(memory_space=pl.ANY)
SEM = pl.BlockSpec(memory_space=pltpu.SEMAPHORE)
EFFECT = pltpu.SideEffectType.DATAFLOW_SIDE_EFFECTING
SUBLANES, LANES = 8, 128
TOKEN = jax.ShapeDtypeStruct((SUBLANES, LANES), jnp.float32)

HEAD_DIM = 64
PAIR = 2 * HEAD_DIM
assert PAIR == LANES
BRANCHES = ((128, 1), (512, 4), (2048, 16))
SIDE = 64
EPS = 1e-6
NEG_INF = -1e30
N_CHIPS = 4
N_DEV = 8

ADAM_LR = 0.001
ADAM_B1 = 0.9
ADAM_B2 = 0.999
ADAM_EPS = 1e-08
ADAM_WD = 0.01
ADAM_STEP = 10

VMEM_LIMIT_BYTES = 56 * 1024 * 1024
ROW_TILE = 256
COL_TILE = 512
CONV_TILE = 256
ATT_BQ = 128
ATT_KW = ATT_BQ + 2 * SIDE
ATT_UNROLL = 4
STAGE_PAD = 4
SMALL_ALIGN = SUBLANES * LANES


def _params(semantics=None):
    kw = {"vmem_limit_bytes": VMEM_LIMIT_BYTES}
    if semantics is not None:
        kw["dimension_semantics"] = semantics
    return pltpu.CompilerParams(**kw)


def _silu(z):
    return z * jax.nn.sigmoid(z)


def _silu_grad(z):
    s = jax.nn.sigmoid(z)
    return s * (1.0 + z * (1.0 - s))


def _my_place():
    return lax.axis_index("x"), lax.axis_index("y"), lax.axis_index("c")


def _flip(a, bit):
    return 1 - a if bit else a


def _chip_of(x, y):
    return 2 * x + y


def _allgather8_start(v, me, name):
    rows_per, n = v.shape
    land = lax.dynamic_update_slice(jnp.zeros((N_DEV * rows_per, n), v.dtype), v, (me * rows_per, 0))

    def body(v_ref, land_ref, send_sems, recv_sems, v_thru, land_thru, token_ref):
        del v_thru, land_thru
        x, y, c = _my_place()
        mine = land_ref.at[pl.ds(pl.multiple_of((4 * x + 2 * y + c) * rows_per, rows_per), rows_per), :]
        for k in range(1, N_DEV):
            peer = (_flip(x, k & 4), _flip(y, k & 2), _flip(c, k & 1))
            pltpu.make_async_remote_copy(
                src_ref=v_ref, dst_ref=mine, send_sem=send_sems.at[k - 1], recv_sem=recv_sems.at[k - 1],
                device_id=peer, device_id_type=MESH).start()
        token_ref[...] = jnp.zeros(token_ref.shape, F32)

    sems = pltpu.SemaphoreType.DMA((N_DEV - 1,))
    return pl.pallas_call(
        body, name=name,
        out_shape=(sems, sems, jax.ShapeDtypeStruct(v.shape, v.dtype), jax.ShapeDtypeStruct(land.shape, land.dtype), TOKEN),
        in_specs=[HBM, HBM], out_specs=(SEM, SEM, HBM, HBM, VMEM),
        input_output_aliases={0: 2, 1: 3},
        compiler_params=pltpu.CompilerParams(has_side_effects=EFFECT),
    )(pltpu.with_memory_space_constraint(v, pltpu.HBM), pltpu.with_memory_space_constraint(land, pltpu.HBM))


def _allgather8_wait(send_sems, recv_sems, v, land, after, name):
    rows_per = v.shape[0]

    def body(v_ref, land_ref, send_ref, recv_ref, *rest):
        del rest
        x, y, c = _my_place()
        for k in range(1, N_DEV):
            peer = (_flip(x, k & 4), _flip(y, k & 2), _flip(c, k & 1))
            src = 4 * peer[0] + 2 * peer[1] + peer[2]
            cp = pltpu.make_async_remote_copy(
                src_ref=v_ref, dst_ref=land_ref.at[pl.ds(pl.multiple_of(src * rows_per, rows_per), rows_per), :],
                send_sem=send_ref.at[k - 1], recv_sem=recv_ref.at[k - 1], device_id=peer, device_id_type=MESH)
            cp.wait_send()
            cp.wait_recv()

    return pl.pallas_call(
        body, name=name,
        out_shape=(jax.ShapeDtypeStruct(v.shape, v.dtype), jax.ShapeDtypeStruct(land.shape, land.dtype)),
        in_specs=[HBM, HBM, SEM, SEM] + [ANY] * len(after), out_specs=(HBM, HBM),
        input_output_aliases={0: 0, 1: 1},
        compiler_params=pltpu.CompilerParams(has_side_effects=EFFECT),
    )(v, land, send_sems, recv_sems, *after)[1]


def _half_rows(ref, chip, which, half):
    return ref.at[chip, pl.ds(pl.multiple_of(which * half, half), half), :]


def _ici_peers(x, y, c):
    peers = [(_flip(x, k & 2), _flip(y, k & 1), c) for k in (1, 2, 3)]
    return [(peer, _chip_of(peer[0], peer[1])) for peer in peers]


def _part_of_half(ref, chip, core, part):
    half, cols = ref.shape[1] // 2, ref.shape[2] // 2
    return ref.at[chip, pl.ds(pl.multiple_of(core * half, half), half), pl.ds(part * cols, cols)]


def _neighbours(x, y, c):
    return [((x, 1 - y, c), _chip_of(x, 1 - y)), ((1 - x, y, c), _chip_of(1 - x, y)),
            ((1 - x, 1 - y, c), _chip_of(1 - x, 1 - y))]


def _start_direct(buf, send_sems, recv_sems):
    x, y, c = _my_place()
    me = _chip_of(x, y)
    for n, (peer, _) in enumerate(_neighbours(x, y, c)[0:2]):
        for part in ((0, 1), (1, 0))[n]:
            piece = _part_of_half(buf, me, c, part)
            pltpu.make_async_remote_copy(
                src_ref=piece, dst_ref=piece, send_sem=send_sems.at[2 * n + part], recv_sem=recv_sems.at[2 * n + part],
                device_id=peer, device_id_type=MESH).start()


def _relay(buf, recv_sems, relay_send, relay_recv):
    x, y, c = _my_place()
    nbrs = _neighbours(x, y, c)
    for n in range(2):
        part = n
        piece = _part_of_half(buf, nbrs[n][1], c, part)
        pltpu.make_async_remote_copy(
            src_ref=piece, dst_ref=piece, send_sem=relay_send.at[part], recv_sem=recv_sems.at[2 * n + part],
            device_id=nbrs[n][0], device_id_type=MESH).wait_recv()
        pltpu.make_async_remote_copy(
            src_ref=piece, dst_ref=piece, send_sem=relay_send.at[part], recv_sem=relay_recv.at[part],
            device_id=nbrs[1 - n][0], device_id_type=MESH).start()


def _gather_start(win_slots, after):
    def body(win_in, after_ref, win_ref, send_sems, recv_sems, token_ref):
        del win_in, after_ref
        _start_direct(win_ref, send_sems, recv_sems)
        token_ref[...] = jnp.zeros(token_ref.shape, F32)

    sems = pltpu.SemaphoreType.DMA((4,))
    return pl.pallas_call(
        body, name="gather_start",
        out_shape=(jax.ShapeDtypeStruct(win_slots.shape, win_slots.dtype), sems, sems, TOKEN),
        in_specs=[HBM, ANY], out_specs=(HBM, SEM, SEM, VMEM),
        input_output_aliases={0: 0},
        compiler_params=pltpu.CompilerParams(has_side_effects=EFFECT),
    )(win_slots, after)


def _gather_relay_in(win, wout_slots, recv_in, after):
    def body(win_in, wout_in, recv_in_ref, after_ref, win_ref, wout_ref, relay_send, relay_recv, send_out, recv_out):
        del win_in, wout_in, after_ref
        _relay(win_ref, recv_in_ref, relay_send, relay_recv)
        _start_direct(wout_ref, send_out, recv_out)

    two, four = pltpu.SemaphoreType.DMA((2,)), pltpu.SemaphoreType.DMA((4,))
    return pl.pallas_call(
        body, name="gather_relay_w_in",
        out_shape=(jax.ShapeDtypeStruct(win.shape, win.dtype), jax.ShapeDtypeStruct(wout_slots.shape, wout_slots.dtype),
                   two, two, four, four),
        in_specs=[HBM, HBM, SEM, ANY], out_specs=(HBM, HBM, SEM, SEM, SEM, SEM),
        input_output_aliases={0: 0, 1: 1},
        compiler_params=pltpu.CompilerParams(has_side_effects=EFFECT),
    )(win, wout_slots, recv_in, after)


def _gather_relay_out(wout, recv_out, after):
    def body(wout_in, recv_out_ref, after_ref, wout_ref, relay_send, relay_recv):
        del wout_in, after_ref
        _relay(wout_ref, recv_out_ref, relay_send, relay_recv)

    two = pltpu.SemaphoreType.DMA((2,))
    return pl.pallas_call(
        body, name="gather_relay_w_out",
        out_shape=(jax.ShapeDtypeStruct(wout.shape, wout.dtype), two, two),
        in_specs=[HBM, SEM, ANY], out_specs=(HBM, SEM, SEM),
        input_output_aliases={0: 0},
        compiler_params=pltpu.CompilerParams(has_side_effects=EFFECT),
    )(wout, recv_out, after)


def _gather_wait_direct(buf, send_sems, recv_sems, after, name):
    def body(buf_in, send_ref, recv_ref, after_ref, buf_ref):
        del buf_in, after_ref
        x, y, c = _my_place()
        me = _chip_of(x, y)
        for n, (peer, chip) in enumerate(_neighbours(x, y, c)[0:2]):
            second = 1 - n
            pltpu.make_async_remote_copy(
                src_ref=_part_of_half(buf_ref, me, c, second), dst_ref=_part_of_half(buf_ref, chip, c, second),
                send_sem=send_ref.at[2 * n + second], recv_sem=recv_ref.at[2 * n + second],
                device_id=peer, device_id_type=MESH).wait_recv()
            for part in range(2):
                piece = _part_of_half(buf_ref, me, c, part)
                pltpu.make_async_remote_copy(
                    src_ref=piece, dst_ref=piece, send_sem=send_ref.at[2 * n + part], recv_sem=recv_ref.at[2 * n + part],
                    device_id=peer, device_id_type=MESH).wait_send()

    return pl.pallas_call(
        body, name=name,
        out_shape=jax.ShapeDtypeStruct(buf.shape, buf.dtype),
        in_specs=[HBM, SEM, SEM, ANY], out_specs=HBM,
        input_output_aliases={0: 0},
        compiler_params=pltpu.CompilerParams(has_side_effects=EFFECT),
    )(buf, send_sems, recv_sems, after)


def _gather_wait_relayed(buf, relay_send, relay_recv, after, name):
    def body(buf_in, rsend_ref, rrecv_ref, after_ref, buf_ref):
        del buf_in, after_ref
        x, y, c = _my_place()
        nbrs = _neighbours(x, y, c)
        for n in range(2):
            relayed = _part_of_half(buf_ref, nbrs[n][1], c, n)
            cp = pltpu.make_async_remote_copy(
                src_ref=relayed, dst_ref=_part_of_half(buf_ref, nbrs[2][1], c, n),
                send_sem=rsend_ref.at[n], recv_sem=rrecv_ref.at[n], device_id=nbrs[1 - n][0], device_id_type=MESH)
            cp.wait_recv()
            cp.wait_send()

    return pl.pallas_call(
        body, name=name,
        out_shape=jax.ShapeDtypeStruct(buf.shape, buf.dtype),
        in_specs=[HBM, SEM, SEM, ANY], out_specs=HBM,
        input_output_aliases={0: 0},
        compiler_params=pltpu.CompilerParams(has_side_effects=EFFECT),
    )(buf, relay_send, relay_recv, after)


def _forward_copies(buf_ref, which, send_sems, recv_sems):
    half = buf_ref.shape[1] // 2
    x, y, c = _my_place()

    def copy(k, chip, core, part):
        piece = _half_rows(buf_ref, chip, core, half) if part is None else _part_of_half(buf_ref, chip, core, part)
        return pltpu.make_async_remote_copy(
            src_ref=piece, dst_ref=piece, send_sem=send_sems.at[k], recv_sem=recv_sems.at[k],
            device_id=(x, y, 1 - c), device_id_type=MESH)

    chips = [_neighbours(x, y, c)[n][1] for n, _ in which]
    return [(copy(k, chip, c, part), copy(k, chip, 1 - c, part)) for k, (chip, (_, part)) in enumerate(zip(chips, which))]


def _forward_halves(buf, which, name):
    def body(buf_in, buf_ref, send_sems, recv_sems):
        del buf_in
        copies = _forward_copies(buf_ref, which, send_sems, recv_sems)
        for mine, _ in copies:
            mine.start()
        for mine, theirs in copies:
            theirs.wait_recv()
        for mine, _ in copies:
            mine.wait_send()

    return pl.pallas_call(
        body, name=name,
        out_shape=jax.ShapeDtypeStruct(buf.shape, buf.dtype),
        in_specs=[HBM], out_specs=HBM,
        input_output_aliases={0: 0},
        scratch_shapes=[pltpu.SemaphoreType.DMA((len(which),))] * 2,
    )(buf)


def _forward_start(buf, which, name):
    def body(buf_in, buf_ref, send_sems, recv_sems, token_ref):
        del buf_in
        for mine, _ in _forward_copies(buf_ref, which, send_sems, recv_sems):
            mine.start()
        token_ref[...] = jnp.zeros(token_ref.shape, F32)

    sems = pltpu.SemaphoreType.DMA((len(which),))
    return pl.pallas_call(
        body, name=name,
        out_shape=(jax.ShapeDtypeStruct(buf.shape, buf.dtype), sems, sems, TOKEN),
        in_specs=[HBM], out_specs=(HBM, SEM, SEM, VMEM),
        input_output_aliases={0: 0},
        compiler_params=pltpu.CompilerParams(has_side_effects=EFFECT),
    )(buf)


def _forward_wait(buf, which, send_sems, recv_sems, after, name):
    def body(buf_in, send_ref, recv_ref, after_ref, buf_ref):
        del buf_in, after_ref
        for mine, theirs in _forward_copies(buf_ref, which, send_ref, recv_ref):
            theirs.wait_recv()
            mine.wait_send()

    return pl.pallas_call(
        body, name=name,
        out_shape=jax.ShapeDtypeStruct(buf.shape, buf.dtype),
        in_specs=[HBM, SEM, SEM, ANY], out_specs=HBM,
        input_output_aliases={0: 0},
        compiler_params=pltpu.CompilerParams(has_side_effects=EFFECT),
    )(buf, send_sems, recv_sems, after)


def _dw_swapped(a, b, row_chunks, col_chunks, name):
    r, t = a.shape
    c_all = b.shape[1]
    chunks = row_chunks * col_chunks
    rq, cq = r // row_chunks, c_all // col_chunks
    half = rq // 2
    tn = COL_TILE
    nt = cq // tn
    steps = col_chunks * nt

    def body(a_ref, b_ref, mine_ref, sib_ref, stage, send_sems, recv_sems):
        x, y, c = _my_place()
        j, n = pl.program_id(0), pl.program_id(1)
        step = j * nt + n
        slot = step % 2
        res = jnp.dot(a_ref[...], b_ref[...], preferred_element_type=F32).astype(BF16)

        def landing(jj, nn):
            cols = pl.ds(pl.multiple_of(nn * tn, tn), tn)
            return sib_ref.at[:, :, cols] if col_chunks == 1 else sib_ref.at[pl.ds(jj, 1), :, cols]

        def copy(slot_, step_, jj, nn):
            return pltpu.make_async_remote_copy(
                src_ref=stage.at[slot_], dst_ref=landing(jj, nn), send_sem=send_sems.at[slot_],
                recv_sem=recv_sems.at[step_], device_id=(x, y, 1 - c), device_id_type=MESH)

        @pl.when(step >= 2)
        def _():
            copy(slot, step, j, n).wait_send()

        for q in range(row_chunks):
            lo = res[q * rq:q * rq + half, :]
            hi = res[q * rq + half:(q + 1) * rq, :]
            mine_ref[q] = jnp.where(c == 0, lo, hi)
            stage[slot, q] = jnp.where(c == 0, hi, lo)
        copy(slot, step, j, n).start()

        @pl.when(step == steps - 1)
        def _():
            for s in range(max(steps - 2, 0), steps):
                copy(s % 2, s, j, n).wait_send()
            for s in range(steps):
                copy(s % 2, s, j, n).wait_recv()

    shape = jax.ShapeDtypeStruct((chunks, half, cq), BF16)
    return pl.pallas_call(
        body, name=name, grid=(col_chunks, nt),
        out_shape=(shape, shape),
        in_specs=[pl.BlockSpec((r, t), lambda j, n: (0, 0)), pl.BlockSpec((t, tn), lambda j, n: (0, j * nt + n))],
        out_specs=(pl.BlockSpec((row_chunks, half, tn), lambda j, n: (j, 0, n)), ANY),
        scratch_shapes=[pltpu.VMEM((2, row_chunks, half, tn), BF16), pltpu.SemaphoreType.DMA((2,)),
                        pltpu.SemaphoreType.DMA((steps,))],
        compiler_params=_params(("arbitrary", "arbitrary")),
    )(a, b)


def _owners_start(csum, name, after=()):
    land = pltpu.with_memory_space_constraint(lax.empty((N_CHIPS - 1,) + csum.shape[1:], csum.dtype), pltpu.HBM)

    def body(csum_ref, land_ref, *rest):
        send_sems, recv_sems, _, _, token_ref = rest[len(after):]
        x, y, c = _my_place()
        for k, (peer, owner) in enumerate(_ici_peers(x, y, c)):
            pltpu.make_async_remote_copy(
                src_ref=csum_ref.at[owner], dst_ref=land_ref.at[k], send_sem=send_sems.at[k], recv_sem=recv_sems.at[k],
                device_id=peer, device_id_type=MESH).start()
        token_ref[...] = jnp.zeros(token_ref.shape, F32)

    sems = pltpu.SemaphoreType.DMA((N_CHIPS - 1,))
    return pl.pallas_call(
        body, name=name,
        out_shape=(sems, sems, jax.ShapeDtypeStruct(csum.shape, csum.dtype),
                   jax.ShapeDtypeStruct(land.shape, land.dtype), TOKEN),
        in_specs=[HBM, HBM] + [ANY] * len(after), out_specs=(SEM, SEM, HBM, HBM, VMEM),
        input_output_aliases={0: 2, 1: 3},
        compiler_params=pltpu.CompilerParams(has_side_effects=EFFECT),
    )(pltpu.with_memory_space_constraint(csum, pltpu.HBM), land, *after)


def _owners_wait(send_sems, recv_sems, csum, land, after, name):
    def body(csum_ref, land_ref, send_ref, recv_ref, *rest):
        del rest
        x, y, c = _my_place()
        for k, (peer, owner) in enumerate(_ici_peers(x, y, c)):
            cp = pltpu.make_async_remote_copy(
                src_ref=csum_ref.at[owner], dst_ref=land_ref.at[k], send_sem=send_ref.at[k], recv_sem=recv_ref.at[k],
                device_id=peer, device_id_type=MESH)
            cp.wait_send()
            cp.wait_recv()

    return pl.pallas_call(
        body, name=name,
        out_shape=(jax.ShapeDtypeStruct(csum.shape, csum.dtype), jax.ShapeDtypeStruct(land.shape, land.dtype)),
        in_specs=[HBM, HBM, SEM, SEM] + [ANY] * len(after), out_specs=(HBM, HBM),
        input_output_aliases={0: 0, 1: 1},
        compiler_params=pltpu.CompilerParams(has_side_effects=EFFECT),
    )(csum, land, send_sems, recv_sems, *after)[1]


def _join_start(full, name, part, parts):
    half = full.shape[0] // 2
    rows = half // parts

    def body(full_in, full_ref, send_sem, recv_sem, token_ref):
        del full_in
        x, y, c = _my_place()
        mine = full_ref.at[pl.ds(pl.multiple_of(c * half + part * rows, rows), rows), :]
        pltpu.make_async_remote_copy(
            src_ref=mine, dst_ref=mine, send_sem=send_sem.at[0], recv_sem=recv_sem.at[0],
            device_id=(x, y, 1 - c), device_id_type=MESH).start()
        token_ref[...] = jnp.zeros(token_ref.shape, F32)

    one = pltpu.SemaphoreType.DMA((1,))
    return pl.pallas_call(
        body, name=name,
        out_shape=(jax.ShapeDtypeStruct(full.shape, full.dtype), one, one, TOKEN),
        in_specs=[HBM], out_specs=(HBM, SEM, SEM, VMEM),
        input_output_aliases={0: 0},
        compiler_params=pltpu.CompilerParams(has_side_effects=EFFECT),
    )(full)


def _join_wait(full, send_sem, recv_sem, after, name, part, parts):
    half = full.shape[0] // 2
    rows = half // parts

    def body(full_in, send_ref, recv_ref, *rest):
        del full_in
        full_ref = rest[-1]
        x, y, c = _my_place()
        cp = pltpu.make_async_remote_copy(
            src_ref=full_ref.at[pl.ds(pl.multiple_of(c * half + part * rows, rows), rows), :],
            dst_ref=full_ref.at[pl.ds(pl.multiple_of((1 - c) * half + part * rows, rows), rows), :],
            send_sem=send_ref.at[0], recv_sem=recv_ref.at[0], device_id=(x, y, 1 - c), device_id_type=MESH)
        cp.wait_send()
        cp.wait_recv()

    return pl.pallas_call(
        body, name=name,
        out_shape=jax.ShapeDtypeStruct(full.shape, full.dtype),
        in_specs=[HBM, SEM, SEM] + [ANY] * len(after), out_specs=HBM,
        input_output_aliases={0: 0},
        compiler_params=pltpu.CompilerParams(has_side_effects=EFFECT),
    )(full, send_sem, recv_sem, *after)


def _cast_into_slot(place, w, name, after):
    rows, cols = w.shape
    tr = min(rows, ROW_TILE)

    def body(place_ref, w_ref, after_ref, o_ref):
        del place_ref, after_ref
        o_ref[...] = w_ref[...].astype(BF16)

    grid_spec = pltpu.PrefetchScalarGridSpec(
        num_scalar_prefetch=1, grid=(rows // tr,),
        in_specs=[pl.BlockSpec((tr, cols), lambda i, p: (i, 0)), ANY],
        out_specs=pl.BlockSpec((None, tr, cols), lambda i, p: (p[0], i, 0)))
    return pl.pallas_call(
        body, name=name, grid_spec=grid_spec,
        out_shape=jax.ShapeDtypeStruct((N_CHIPS, rows, cols), BF16),
        compiler_params=_params(("parallel",)),
    )(place, w, after)


def _ada_modulation(packed, w_ada, b_ada, d, w_big):
    rows_per, n = packed.shape
    d_model, wa = w_ada.shape
    big_rows, big_cols = w_big.shape
    n_chunks = big_rows // ROW_TILE
    first_chunks = (2 * n_chunks) // 3

    def body(v_ref, w_hbm, b_ref, big_hbm, all_ref, mod_ref, slots_hbm, w_vmem, part_ref, parts_ref, wide, narrow,
             load_sem, send1, recv1, send2, recv2, in_sems, out_sems):
        x, y, c = _my_place()
        me = 4 * x + 2 * y + c
        chip = _chip_of(x, y)
        load = pltpu.make_async_copy(w_hbm, w_vmem, load_sem)
        load.start()

        def chunk_in(i):
            return pltpu.make_async_copy(big_hbm.at[i * ROW_TILE:(i + 1) * ROW_TILE, :], wide.at[i % 2], in_sems.at[i % 2])

        def chunk_out(i):
            return pltpu.make_async_copy(
                narrow.at[i % 2], slots_hbm.at[chip, i * ROW_TILE:(i + 1) * ROW_TILE, :], out_sems.at[i % 2])

        def cast_chunk(i):
            if i + 1 < n_chunks:
                chunk_in(i + 1).start()
            chunk_in(i).wait()
            if i >= 2:
                chunk_out(i - 2).wait()
            narrow[i % 2] = wide[i % 2].astype(BF16)
            chunk_out(i).start()

        chunk_in(0).start()

        def rows(idx):
            return all_ref.at[pl.ds(pl.multiple_of(idx * rows_per, rows_per), rows_per), :]

        all_ref[pl.ds(pl.multiple_of(me * rows_per, rows_per), rows_per), :] = v_ref[...]
        copies = []
        for k in range(1, N_DEV):
            peer = (_flip(x, k & 4), _flip(y, k & 2), _flip(c, k & 1))
            cp = pltpu.make_async_remote_copy(
                src_ref=v_ref, dst_ref=rows(me), send_sem=send1.at[k - 1], recv_sem=recv1.at[k - 1],
                device_id=peer, device_id_type=MESH)
            cp.start()
            copies.append((cp, peer))
        for i in range(first_chunks):
            cast_chunk(i)
        for k, (cp, peer) in enumerate(copies):
            pltpu.make_async_remote_copy(
                src_ref=v_ref, dst_ref=rows(4 * peer[0] + 2 * peer[1] + peer[2]), send_sem=send1.at[k],
                recv_sem=recv1.at[k], device_id=peer, device_id_type=MESH).wait_recv()
        for cp, _ in copies:
            cp.wait_send()

        def c_of(dev):
            segments, pos = [], 0
            while pos < d:
                row, col = divmod(pos, n)
                take = min(d - pos, n - col)
                segments.append(all_ref[dev * rows_per + row:dev * rows_per + row + 1, col:col + take])
                pos += take
            return jnp.concatenate(segments, axis=1)

        c_all = jnp.concatenate([c_of(dev) for dev in range(N_DEV)], axis=0)
        load.wait()
        part_ref[...] = jnp.dot(_silu(c_all), w_vmem[...], precision=lax.Precision.HIGHEST, preferred_element_type=F32)
        parts_ref[chip] = part_ref[...]
        swaps = []
        for k, (peer, _) in enumerate(_ici_peers(x, y, c)):
            cp = pltpu.make_async_remote_copy(
                src_ref=part_ref, dst_ref=parts_ref.at[chip], send_sem=send2.at[k], recv_sem=recv2.at[k],
                device_id=peer, device_id_type=MESH)
            cp.start()
            swaps.append(cp)
        for i in range(first_chunks, n_chunks):
            cast_chunk(i)
        for i in range(n_chunks - 2, n_chunks):
            chunk_out(i).wait()
        for k, (peer, peer_chip) in enumerate(_ici_peers(x, y, c)):
            pltpu.make_async_remote_copy(
                src_ref=part_ref, dst_ref=parts_ref.at[peer_chip], send_sem=send2.at[k], recv_sem=recv2.at[k],
                device_id=peer, device_id_type=MESH).wait_recv()
        for cp in swaps:
            cp.wait_send()
        flat = jnp.concatenate([parts_ref[j, pl.ds(me, 1), :] for j in range(N_CHIPS)], axis=1) + b_ref[...]
        mod_ref[...] = jnp.concatenate([flat[:, i * d:(i + 1) * d] for i in range(3)], axis=0)

    return pl.pallas_call(
        body, name="ada_modulation",
        out_shape=(jax.ShapeDtypeStruct((N_DEV * rows_per, n), F32), jax.ShapeDtypeStruct((3, d), F32),
                   jax.ShapeDtypeStruct((N_CHIPS, big_rows, big_cols), BF16)),
        in_specs=[VMEM, ANY, VMEM, ANY], out_specs=(VMEM, VMEM, ANY),
        scratch_shapes=[pltpu.VMEM((d_model, wa), F32), pltpu.VMEM((N_DEV, wa), F32),
                        pltpu.VMEM((N_CHIPS, N_DEV, wa), F32),
                        pltpu.VMEM((2, ROW_TILE, big_cols), F32), pltpu.VMEM((2, ROW_TILE, big_cols), BF16),
                        pltpu.SemaphoreType.DMA,
                        pltpu.SemaphoreType.DMA((N_DEV - 1,)), pltpu.SemaphoreType.DMA((N_DEV - 1,)),
                        pltpu.SemaphoreType.DMA((N_CHIPS - 1,)), pltpu.SemaphoreType.DMA((N_CHIPS - 1,)),
                        pltpu.SemaphoreType.DMA((2,)), pltpu.SemaphoreType.DMA((2,))],
        compiler_params=_params(),
    )(packed, w_ada, b_ada, w_big)


def _prenorm(x, mod, g_pre, after):
    t, d = x.shape
    tb = ROW_TILE

    def body(x_ref, mod_ref, g_ref, after_ref, h_ref, ht_ref):
        del after_ref
        xv = x_ref[...]
        r = lax.rsqrt(jnp.mean(xv * xv, axis=-1, keepdims=True) + EPS)
        h = (xv * r) * g_ref[...] * (1.0 + mod_ref[1:2, :]) + mod_ref[0:1, :]
        h_ref[...] = h.astype(BF16)
        ht_ref[...] = h.T.astype(BF16)

    return pl.pallas_call(
        body, name="prenorm", grid=(t // tb,),
        out_shape=(jax.ShapeDtypeStruct((t, d), BF16), jax.ShapeDtypeStruct((d, t), BF16)),
        in_specs=[pl.BlockSpec((tb, d), lambda i: (i, 0)), pl.BlockSpec((3, d), lambda i: (0, 0)),
                  pl.BlockSpec((1, d), lambda i: (0, 0)), ANY],
        out_specs=(pl.BlockSpec((tb, d), lambda i: (i, 0)), pl.BlockSpec((d, tb), lambda i: (0, i))),
        compiler_params=_params(("parallel",)),
    )(x, mod, g_pre, after)


def _proj_tiles(proj, h, w, tiles, step, name):
    t, d = h.shape
    ws = w.shape[-1]
    tn = COL_TILE
    nt = ws // tn

    def body(tile_ref, *refs):
        del tile_ref
        a_ref, b_ref, o_ref = refs[-3:]
        o_ref[...] = jnp.dot(a_ref[...], b_ref[...].astype(BF16), preferred_element_type=F32).astype(BF16)

    if w.ndim == 3:
        w_spec = pl.BlockSpec((None, d, tn), lambda i, tl: (tl[step, i] // nt, 0, tl[step, i] % nt))
    else:
        w_spec = pl.BlockSpec((d, tn), lambda i, tl: (0, tl[step, i] % nt))
    first = proj is None
    grid_spec = pltpu.PrefetchScalarGridSpec(
        num_scalar_prefetch=1, grid=(tiles.shape[1],),
        in_specs=([] if first else [HBM]) + [pl.BlockSpec((t, d), lambda i, tl: (0, 0)), w_spec],
        out_specs=pl.BlockSpec((t, tn), lambda i, tl: (0, tl[step, i])))
    return pl.pallas_call(
        body, name=name, grid_spec=grid_spec,
        out_shape=jax.ShapeDtypeStruct((t, N_CHIPS * ws), BF16),
        input_output_aliases={} if first else {1: 0},
        compiler_params=_params(("parallel",)),
    )(*([tiles] if first else [tiles, proj]), h, w)


def _shift_rows(a, rows):
    idx = lax.broadcasted_iota(jnp.int32, a.shape, 0)
    prev = jnp.where(idx == 0, 0.0, pltpu.roll(a, 1, 0))
    nxt = jnp.where(idx == rows - 1, 0.0, pltpu.roll(a, rows - 1, 0))
    return prev, nxt


def _conv_fwd(conv_proj, conv_w, conv_b, dc):
    t = conv_proj.shape[0]
    ct = CONV_TILE
    nct = dc // ct

    def body(u_ref, cg_ref, w_ref, b_ref, co_ref):
        a = cg_ref[...].astype(F32) * u_ref[...].astype(F32)
        prev, nxt = _shift_rows(a, t)
        co_ref[...] = (w_ref[0:1, :] * prev + w_ref[1:2, :] * a + w_ref[2:3, :] * nxt + b_ref[...]).astype(BF16)

    return pl.pallas_call(
        body, name="conv_fwd", grid=(nct,),
        out_shape=jax.ShapeDtypeStruct((t, dc), BF16),
        in_specs=[pl.BlockSpec((t, ct), lambda i: (0, i)), pl.BlockSpec((t, ct), lambda i: (0, 2 * nct + i)),
                  pl.BlockSpec((3, ct), lambda i: (0, i)), pl.BlockSpec((1, ct), lambda i: (0, i))],
        out_specs=pl.BlockSpec((t, ct), lambda i: (0, i)),
        compiler_params=_params(("parallel",)),
    )(conv_proj, conv_proj, conv_w, conv_b)


def _stage_pitch(r):
    return r + STAGE_PAD if r % SUBLANES == 0 else r


def _stage_rows(t):
    return max(t // r * _stage_pitch(r) for _, r in BRANCHES)


def _gather_residues(src_ref, stage, dst_ref, r):
    t = src_ref.shape[0]
    seq, pitch = t // r, _stage_pitch(r)
    from_ref = stage
    if pitch == r and src_ref.dtype == F32:
        from_ref = src_ref
    elif pitch == r:
        stage[0:t, :] = src_ref[...].astype(F32)
    else:
        for g in range(seq):
            stage[g * pitch:g * pitch + r, :] = src_ref[g * r:(g + 1) * r, :].astype(F32)
    for res in range(r):
        dst_ref[res * seq:(res + 1) * seq, :] = from_ref[pl.ds(res, seq, stride=pitch), :].astype(dst_ref.dtype)


def _scatter_residues(src_ref, stage, dst_ref, r, add):
    t = src_ref.shape[0]
    seq, pitch = t // r, _stage_pitch(r)
    if pitch == r:
        for res in range(r):
            tok = pl.ds(res, seq, stride=r)
            val = src_ref[res * seq:(res + 1) * seq, :]
            dst_ref[tok, :] = dst_ref[tok, :] + val if add else val
        return
    for res in range(r):
        stage[pl.ds(res, seq, stride=pitch), :] = src_ref[res * seq:(res + 1) * seq, :]
    for g in range(seq):
        rows = slice(g * r, (g + 1) * r)
        val = stage[g * pitch:g * pitch + r, :]
        dst_ref[rows, :] = dst_ref[rows, :] + val if add else val


def _branch_operands(token_refs, stage, dil, r):
    if r == 1:
        return list(token_refs)
    for i, ref in enumerate(token_refs):
        _gather_residues(ref, stage, dil.at[i], r)
    return [dil.at[i] for i in range(len(token_refs))]


def _scaled_queries(q):
    return (q.astype(F32) * (HEAD_DIM ** -0.5)).astype(BF16)


BLOCK_SHIFTS = (0, -SIDE, None)


def _band_bias(rel, slope):
    arel = jnp.abs(rel)
    return jnp.where(arel <= SIDE, arel.astype(F32) * slope, NEG_INF)


def _fill_bias_tiles(bias_ref, sl_ref, r, kw):
    base = lax.broadcasted_iota(jnp.int32, (ATT_BQ, kw), 1) - lax.broadcasted_iota(jnp.int32, (ATT_BQ, kw), 0)
    for hh in range(2):
        slope = -(sl_ref[hh:hh + 1, 0:kw] * float(r))
        for e, shift in enumerate(BLOCK_SHIFTS):
            shift = ATT_BQ - kw if shift is None else shift
            bias_ref[hh, e, :, 0:kw] = _band_bias(base + shift, slope)


def _fill_stacked_bias_tiles(bias_ref, sl_ref, r, kw):
    base = lax.broadcasted_iota(jnp.int32, (kw, ATT_BQ), 0) - lax.broadcasted_iota(jnp.int32, (kw, ATT_BQ), 1)
    for hh in range(2):
        slope = -(sl_ref[hh:hh + 1, 0:ATT_BQ] * float(r))
        for e, shift in enumerate(BLOCK_SHIFTS):
            shift = ATT_BQ - kw if shift is None else shift
            bias_ref[e, 0:kw, hh * ATT_BQ:(hh + 1) * ATT_BQ] = _band_bias(base + shift, slope)


def _first_head_lanes():
    return lax.broadcasted_iota(jnp.int32, (1, PAIR), 1) < HEAD_DIM


def _only_head(x, first, hh):
    return jnp.where(first if hh == 0 else jnp.logical_not(first), x, jnp.zeros_like(x))


def _block_place(g, seq_len, kw):
    nqb = seq_len // ATT_BQ
    if nqb == 1:
        row = pl.multiple_of(g * ATT_BQ, ATT_BQ)
        return row, row, 0
    res = g // nqb
    qb = g - res * nqb
    q0 = qb * ATT_BQ
    ks = jnp.clip(q0 - SIDE, 0, seq_len - kw)
    edge = jnp.where(qb == 0, 0, jnp.where(qb == nqb - 1, 2, 1))
    return (pl.multiple_of(res * seq_len + q0, ATT_BQ), pl.multiple_of(res * seq_len + ks, SIDE), edge)


def _qkv_specs(dc, da, t, index):
    return [pl.BlockSpec((t, PAIR), functools.partial(index, (4 * dc + comp * da) // PAIR)) for comp in range(3)]


def _attn_fwd(proj, slopes, dc, da):
    t = proj.shape[0]
    hp = da // PAIR
    n_blocks = t // ATT_BQ

    def body(q_ref, k_ref, v_ref, sl_ref, o_ref, lse_ref, stage, dil, bias, o_res, l_res, o_tok, l_tok):
        for b, (_, r) in enumerate(BRANCHES):
            seq_len = t // r
            kw = min(ATT_KW, seq_len)
            ops = _branch_operands([q_ref, k_ref, v_ref], stage, dil, r)
            _fill_bias_tiles(bias, sl_ref, r, kw)
            o_dst, l_dst = (o_tok.at[b], l_tok.at[b]) if r == 1 else (o_res, l_res)
            first = _first_head_lanes()

            def blocks(trip, carry, seq_len=seq_len, kw=kw, o_dst=o_dst, l_dst=l_dst, first=first, ops=ops):
                nt = (((1,), (1,)), ((), ()))
                places = [_block_place(trip * ATT_UNROLL + i, seq_len, kw) for i in range(ATT_UNROLL)]
                chains = [(i, hh) for i in range(ATT_UNROLL) for hh in range(2)]
                qs = [_scaled_queries(ops[0][pl.ds(qrow, ATT_BQ), :]) for qrow, _, _ in places]
                ks = [ops[1][pl.ds(krow, kw), :] for _, krow, _ in places]
                vs = [ops[2][pl.ds(krow, kw), :] for _, krow, _ in places]
                ss = [lax.dot_general(_only_head(qs[i], first, hh), ks[i], nt, preferred_element_type=F32)
                      + bias[hh, places[i][2], :, 0:kw] for i, hh in chains]
                tops = [jnp.max(s, axis=-1, keepdims=True) for s in ss]
                ps = [jnp.exp(s - m) for s, m in zip(ss, tops)]
                dens = [jnp.sum(p, axis=-1, keepdims=True) for p in ps]
                for i, (qrow, _, _) in enumerate(places):
                    weights = jnp.concatenate([ps[2 * i].astype(BF16), ps[2 * i + 1].astype(BF16)], axis=1)
                    values = jnp.concatenate([_only_head(vs[i], first, 0), _only_head(vs[i], first, 1)], axis=0)
                    den = jnp.where(first, dens[2 * i], dens[2 * i + 1])
                    o_dst[pl.ds(qrow, ATT_BQ), :] = jnp.dot(weights, values, preferred_element_type=F32) / den
                    l_dst[pl.ds(qrow, ATT_BQ), :] = jnp.where(first, tops[2 * i], tops[2 * i + 1]) + jnp.log(den)
                return carry

            lax.fori_loop(0, n_blocks // ATT_UNROLL, blocks, 0)
            if r > 1:
                _scatter_residues(o_res, stage, o_tok.at[b], r, add=False)
                _scatter_residues(l_res, stage, l_tok.at[b], r, add=False)

        def merge(i, carry):
            rows = pl.ds(pl.multiple_of(i * ROW_TILE, ROW_TILE), ROW_TILE)
            la, lb, lc = l_tok[0, rows, :], l_tok[1, rows, :], l_tok[2, rows, :]
            m = jnp.maximum(jnp.maximum(la, lb), lc)
            wa, wb, wc = jnp.exp(la - m), jnp.exp(lb - m), jnp.exp(lc - m)
            den = wa + wb + wc
            o_ref[rows, :] = (wa * o_tok[0, rows, :] + wb * o_tok[1, rows, :] + wc * o_tok[2, rows, :]) * (1.0 / den)
            lse_ref[rows, :] = m + jnp.log(den)
            return carry

        lax.fori_loop(0, t // ROW_TILE, merge, 0)

    pair_spec = pl.BlockSpec((None, t, PAIR), lambda h: (h, 0, 0))
    return pl.pallas_call(
        body, name="attn_fwd", grid=(hp,),
        out_shape=(jax.ShapeDtypeStruct((hp, t, PAIR), F32), jax.ShapeDtypeStruct((hp, t, PAIR), F32)),
        in_specs=_qkv_specs(dc, da, t, lambda first, h: (0, first + h))
        + [pl.BlockSpec((None, 8, ATT_KW), lambda h: (h, 0, 0))],
        out_specs=(pair_spec, pair_spec),
        scratch_shapes=[pltpu.VMEM((_stage_rows(t), PAIR), F32), pltpu.VMEM((3, t, PAIR), BF16),
                        pltpu.VMEM((2, 3, ATT_BQ, ATT_KW), F32),
                        pltpu.VMEM((t, PAIR), F32), pltpu.VMEM((t, PAIR), F32),
                        pltpu.VMEM((3, t, PAIR), F32), pltpu.VMEM((3, t, PAIR), F32)],
        compiler_params=_params(("parallel",)),
    )(proj, proj, proj, slopes)


def _attn_bwd(dproj, proj, d_o, lse, delta, slopes, dc, da, after):
    t = proj.shape[0]
    hp = da // PAIR
    n_blocks = t // ATT_BQ

    def all_branches(q_ref, k_ref, v_ref, do_ref, lse_ref, dl_ref, sl_ref,
                     stage, dil, packed, packed_res, row_vecs, bias_t, acc, tot):
        first = _first_head_lanes()
        lane = lax.broadcasted_iota(jnp.int32, (1, PAIR), 1)
        packed[...] = jnp.where((lane & (HEAD_DIM - 1)) < HEAD_DIM // 2, lse_ref[...], dl_ref[...])
        for b, (_, r) in enumerate(BRANCHES):
            seq_len = t // r
            kw = min(ATT_KW, seq_len)
            ops = _branch_operands([q_ref, k_ref, v_ref, do_ref], stage, dil, r)
            scalars = packed
            if r > 1:
                _gather_residues(packed, stage, packed_res, r)
                scalars = packed_res
            for g in range(n_blocks):
                flipped = scalars[g * ATT_BQ:(g + 1) * ATT_BQ, :].T
                for row in range(4):
                    row_vecs[g, row:row + 1, :] = flipped[row * (HEAD_DIM // 2):row * (HEAD_DIM // 2) + 1, :]
            _fill_stacked_bias_tiles(bias_t, sl_ref, r, kw)
            acc[1] = jnp.zeros((t, PAIR), F32)
            acc[2] = jnp.zeros((t, PAIR), F32)

            def blocks(trip, carry, seq_len=seq_len, kw=kw, ops=ops):
                nt = (((1,), (1,)), ((), ()))
                group = range(ATT_UNROLL)
                places = [_block_place(trip * ATT_UNROLL + i, seq_len, kw) for i in group]
                ks, vs, q2s, do2s, lse2s, dl2s = [], [], [], [], [], []
                for i, (qrow, krow, _) in zip(group, places):
                    q = _scaled_queries(ops[0][pl.ds(qrow, ATT_BQ), :])
                    dov = ops[3][pl.ds(qrow, ATT_BQ), :]
                    ks.append(ops[1][pl.ds(krow, kw), :])
                    vs.append(ops[2][pl.ds(krow, kw), :])
                    q2s.append(jnp.concatenate([_only_head(q, first, 0), _only_head(q, first, 1)], axis=0))
                    do2s.append(jnp.concatenate([_only_head(dov, first, 0), _only_head(dov, first, 1)], axis=0))
                    rows = row_vecs[trip * ATT_UNROLL + i]
                    lse2s.append(jnp.concatenate([rows[0:1, :], rows[2:3, :]], axis=1))
                    dl2s.append(jnp.concatenate([rows[1:2, :], rows[3:4, :]], axis=1))
                s_ts = [lax.dot_general(ks[i], q2s[i], nt, preferred_element_type=F32) for i in group]
                dp_ts = [lax.dot_general(vs[i], do2s[i], nt, preferred_element_type=F32) for i in group]
                p_ts = [jnp.exp(s_ts[i] + bias_t[places[i][2], 0:kw, :] - lse2s[i]) for i in group]
                ds_ts = [p_ts[i] * (dp_ts[i] - dl2s[i]) for i in group]
                dvs = [jnp.dot(p_ts[i].astype(BF16), do2s[i], preferred_element_type=F32) for i in group]
                dks = [jnp.dot(ds_ts[i].astype(BF16), q2s[i], preferred_element_type=F32) for i in group]
                dss = [ds_ts[i].T.astype(BF16) for i in group]
                dqs = [jnp.dot(dss[i][0:ATT_BQ, :], _only_head(ks[i], first, 0), preferred_element_type=F32)
                       + jnp.dot(dss[i][ATT_BQ:2 * ATT_BQ, :], _only_head(ks[i], first, 1), preferred_element_type=F32)
                       for i in group]
                for i, (qrow, krow, _) in zip(group, places):
                    acc[0, pl.ds(qrow, ATT_BQ), :] = dqs[i] * (HEAD_DIM ** -0.5)
                    acc[1, pl.ds(krow, kw), :] += dks[i]
                    acc[2, pl.ds(krow, kw), :] += dvs[i]
                return carry

            lax.fori_loop(0, n_blocks // ATT_UNROLL, blocks, 0)
            for comp in range(3):
                if r == 1:
                    tot[comp] = acc[comp]
                else:
                    _scatter_residues(acc.at[comp], stage, tot.at[comp], r, add=True)

    first_q = (4 * dc) // PAIR

    def body(dproj_in, q_ref, k_ref, v_ref, do_ref, lse_ref, dl_ref, sl_ref, after_ref, out_ref, *scratch):
        del dproj_in, after_ref
        work, out_stage, out_sems = scratch[:-2], scratch[-2], scratch[-1]
        h = pl.program_id(0)
        all_branches(q_ref, k_ref, v_ref, do_ref, lse_ref, dl_ref, sl_ref, *work)

        def out_copy(comp):
            cols = pl.ds(pl.multiple_of((first_q + comp * hp + h) * PAIR, PAIR), PAIR)
            return pltpu.make_async_copy(out_stage.at[comp], out_ref.at[:, cols], out_sems.at[comp])

        @pl.when(h > 0)
        def _():
            for comp in range(3):
                out_copy(comp).wait()

        for comp in range(3):
            out_stage[comp] = work[-1][comp].astype(BF16)
            out_copy(comp).start()

        @pl.when(h == hp - 1)
        def _():
            for comp in range(3):
                out_copy(comp).wait()

    pair_spec = pl.BlockSpec((None, t, PAIR), lambda h: (h, 0, 0))
    return pl.pallas_call(
        body, name="attn_bwd", grid=(hp,),
        out_shape=jax.ShapeDtypeStruct(dproj.shape, BF16),
        in_specs=[HBM] + _qkv_specs(dc, da, t, lambda first, h: (0, first + h))
        + [pair_spec, pair_spec, pair_spec, pl.BlockSpec((None, 8, ATT_KW), lambda h: (h, 0, 0)), ANY],
        out_specs=ANY,
        input_output_aliases={0: 0},
        scratch_shapes=[pltpu.VMEM((_stage_rows(t), PAIR), F32), pltpu.VMEM((4, t, PAIR), BF16),
                        pltpu.VMEM((t, PAIR), F32), pltpu.VMEM((t, PAIR), F32),
                        pltpu.VMEM((n_blocks, 8, ATT_BQ), F32), pltpu.VMEM((3, ATT_KW, 2 * ATT_BQ), F32),
                        pltpu.VMEM((3, t, PAIR), F32), pltpu.VMEM((3, t, PAIR), F32),
                        pltpu.VMEM((3, t, PAIR), BF16), pltpu.SemaphoreType.DMA((3,))],
        compiler_params=_params(("arbitrary",)),
    )(dproj, proj, proj, proj, d_o, lse, delta, slopes, after)


def _mix_fwd(co, proj, o_mix, g_conv, g_attn_pairs, after):
    t, dc = co.shape
    hp = o_mix.shape[0]
    da = hp * PAIR
    tb = ROW_TILE

    def body(co_ref, bg_ref, zc_ref, za_ref, om_ref, gc_ref, ga_ref, after_ref, ycat_ref, ycatt_ref):
        del after_ref
        p = bg_ref[...].astype(F32) * co_ref[...].astype(F32)
        rc = lax.rsqrt(jnp.mean(p * p, axis=-1, keepdims=True) + EPS)
        yc = (p * rc) * gc_ref[...] * _silu(zc_ref[...].astype(F32))
        ycat_ref[:, 0:dc] = yc.astype(BF16)
        ycatt_ref[0:dc, :] = yc.T.astype(BF16)
        ssq = jnp.zeros((tb, 1), F32)
        for h in range(hp):
            o = om_ref[h]
            ssq = ssq + jnp.sum(o * o, axis=-1, keepdims=True)
        ra = lax.rsqrt(ssq * (1.0 / da) + EPS)
        for h in range(hp):
            ya = (om_ref[h] * ra) * ga_ref[h] * _silu(za_ref[:, h * PAIR:(h + 1) * PAIR].astype(F32))
            ycat_ref[:, dc + h * PAIR:dc + (h + 1) * PAIR] = ya.astype(BF16)
            ycatt_ref[dc + h * PAIR:dc + (h + 1) * PAIR, :] = ya.T.astype(BF16)

    pair_spec = pl.BlockSpec((hp, tb, PAIR), lambda i: (0, i, 0))
    return pl.pallas_call(
        body, name="mix_fwd", grid=(t // tb,),
        out_shape=(jax.ShapeDtypeStruct((t, dc + da), BF16), jax.ShapeDtypeStruct((dc + da, t), BF16)),
        in_specs=[pl.BlockSpec((tb, dc), lambda i: (i, 0)),
                  pl.BlockSpec((tb, dc), lambda i: (i, 1)),
                  pl.BlockSpec((tb, dc), lambda i: (i, 3)),
                  pl.BlockSpec((tb, da), lambda i: (i, 7)),
                  pair_spec,
                  pl.BlockSpec((1, dc), lambda i: (0, 0)),
                  pl.BlockSpec((hp, 1, PAIR), lambda i: (0, 0, 0)), ANY],
        out_specs=(pl.BlockSpec((tb, dc + da), lambda i: (i, 0)), pl.BlockSpec((dc + da, tb), lambda i: (0, i))),
        compiler_params=_params(("parallel",)),
    )(co, proj, proj, proj, o_mix, g_conv, g_attn_pairs, after)


def _out_fwd_bwd(ycat, woutf, x, target, mod, g_post):
    t, d = x.shape
    n = ycat.shape[1]
    tb = ROW_TILE

    def body(a_ref, w_ref, x_ref, tg_ref, mod_ref, g_ref, dout_ref, dy_ref, acc_ref):
        y = jnp.dot(a_ref[...], w_ref[...], preferred_element_type=F32)
        r = lax.rsqrt(jnp.mean(y * y, axis=-1, keepdims=True) + EPS)
        nh = y * r
        gate = mod_ref[2:3, :]
        nrm = nh * g_ref[...]
        err = x_ref[...] + gate * nrm - tg_ref[...]
        dout = err * (1.0 / d)
        dout_ref[...] = dout.astype(BF16)
        dn = dout * gate
        a = dn * g_ref[...]
        dy = r * (a - nh * jnp.mean(a * nh, axis=-1, keepdims=True))
        dy_ref[...] = dy.astype(BF16)
        loss = 0.5 * jnp.sum(jnp.sum(err * err, axis=-1, keepdims=True) * (1.0 / d), axis=0, keepdims=True)
        part = jnp.concatenate(
            [jnp.sum(dout * nrm, axis=0, keepdims=True), jnp.sum(dn * nh, axis=0, keepdims=True),
             jnp.broadcast_to(loss, (1, d)), jnp.zeros((5, d), F32)], axis=0)

        @pl.when(pl.program_id(0) == 0)
        def _():
            acc_ref[...] = jnp.zeros(acc_ref.shape, F32)

        acc_ref[...] += part

    return pl.pallas_call(
        body, name="out_fwd_bwd", grid=(t // tb,),
        out_shape=(jax.ShapeDtypeStruct((t, d), BF16), jax.ShapeDtypeStruct((t, d), BF16),
                   jax.ShapeDtypeStruct((8, d), F32)),
        in_specs=[pl.BlockSpec((tb, n), lambda i: (i, 0)), pl.BlockSpec((n, d), lambda i: (0, 0)),
                  pl.BlockSpec((tb, d), lambda i: (i, 0)), pl.BlockSpec((tb, d), lambda i: (i, 0)),
                  pl.BlockSpec((3, d), lambda i: (0, 0)), pl.BlockSpec((1, d), lambda i: (0, 0))],
        out_specs=(pl.BlockSpec((tb, d), lambda i: (i, 0)), pl.BlockSpec((tb, d), lambda i: (i, 0)),
                   pl.BlockSpec((8, d), lambda i: (0, 0))),
        compiler_params=_params(("arbitrary",)),
    )(ycat, woutf, x, target, mod, g_post)


def _matmul_nt(a, b, out_dtype, name):
    m, k = a.shape
    n = b.shape[0]
    tn = COL_TILE

    def body(a_ref, b_ref, o_ref):
        o_ref[...] = lax.dot_general(a_ref[...], b_ref[...], (((1,), (1,)), ((), ())),
                                     preferred_element_type=F32).astype(out_dtype)

    return pl.pallas_call(
        body, name=name, grid=(n // tn,),
        out_shape=jax.ShapeDtypeStruct((m, n), out_dtype),
        in_specs=[pl.BlockSpec((m, k), lambda i: (0, 0)), pl.BlockSpec((tn, k), lambda i: (i, 0))],
        out_specs=pl.BlockSpec((m, tn), lambda i: (0, i)),
        compiler_params=_params(("parallel",)),
    )(a, b)


def _mix_bwd(dycat, co, proj, o_mix, g_conv, g_attn_pairs):
    t, dc = co.shape
    hp = o_mix.shape[0]
    da = hp * PAIR
    tb = ROW_TILE

    def body(dy_ref, co_ref, bg_ref, zc_ref, za_ref, om_ref, gc_ref, ga_ref,
             dcp_ref, dco_ref, do_ref, dl_ref, dgc_ref, dga_ref):
        first = pl.program_id(0) == 0
        cov = co_ref[...].astype(F32)
        bg = bg_ref[...].astype(F32)
        zc = zc_ref[...].astype(F32)
        p = bg * cov
        rc = lax.rsqrt(jnp.mean(p * p, axis=-1, keepdims=True) + EPS)
        nh = p * rc
        dyc = dy_ref[:, 0:dc].astype(F32)
        dn = dyc * _silu(zc)
        a = dn * gc_ref[...]
        dp = rc * (a - nh * jnp.mean(a * nh, axis=-1, keepdims=True))
        dcp_ref[:, 0:dc] = jnp.zeros((tb, dc), BF16)
        dcp_ref[:, dc:2 * dc] = (dp * cov).astype(BF16)
        dcp_ref[:, 2 * dc:3 * dc] = jnp.zeros((tb, dc), BF16)
        dcp_ref[:, 3 * dc:4 * dc] = (dyc * nh * gc_ref[...] * _silu_grad(zc)).astype(BF16)
        dcp_ref[:, 4 * dc:4 * dc + 3 * da] = jnp.zeros((tb, 3 * da), BF16)
        dco_ref[...] = dp * bg

        @pl.when(first)
        def _():
            dgc_ref[...] = jnp.zeros(dgc_ref.shape, F32)
            dga_ref[...] = jnp.zeros(dga_ref.shape, F32)

        dgc_ref[...] += jnp.sum(dn * nh, axis=0, keepdims=True)

        ssq = jnp.zeros((tb, 1), F32)
        for h in range(hp):
            o = om_ref[h]
            ssq = ssq + jnp.sum(o * o, axis=-1, keepdims=True)
        ra = lax.rsqrt(ssq * (1.0 / da) + EPS)
        dot_an = jnp.zeros((tb, 1), F32)
        for h in range(hp):
            nha = om_ref[h] * ra
            za = za_ref[:, h * PAIR:(h + 1) * PAIR].astype(F32)
            dya = dy_ref[:, dc + h * PAIR:dc + (h + 1) * PAIR].astype(F32)
            dna = dya * _silu(za)
            dza = (dya * nha * ga_ref[h] * _silu_grad(za)).astype(BF16)
            dcp_ref[:, 4 * dc + 3 * da + h * PAIR:4 * dc + 3 * da + (h + 1) * PAIR] = dza
            dga_ref[h] += jnp.sum(dna * nha, axis=0, keepdims=True)
            dot_an = dot_an + jnp.sum(dna * ga_ref[h] * nha, axis=-1, keepdims=True)
        mean_an = dot_an * (1.0 / da)
        first_head = lax.broadcasted_iota(jnp.int32, (tb, PAIR), 1) < HEAD_DIM
        for h in range(hp):
            o = om_ref[h]
            nha = o * ra
            za = za_ref[:, h * PAIR:(h + 1) * PAIR].astype(F32)
            dya = dy_ref[:, dc + h * PAIR:dc + (h + 1) * PAIR].astype(F32)
            aa = dya * _silu(za) * ga_ref[h]
            d_o = ra * (aa - nha * mean_an)
            do_ref[h] = d_o.astype(BF16)
            prod = d_o * o
            both = jnp.sum(prod, axis=-1, keepdims=True)
            head0 = jnp.sum(jnp.where(first_head, prod, 0.0), axis=-1, keepdims=True)
            dl_ref[h] = jnp.where(first_head, head0, both - head0)

    pair_spec = pl.BlockSpec((hp, tb, PAIR), lambda i: (0, i, 0))
    return pl.pallas_call(
        body, name="mix_bwd", grid=(t // tb,),
        out_shape=(jax.ShapeDtypeStruct((t, 4 * dc + 4 * da), BF16), jax.ShapeDtypeStruct((t, dc), F32),
                   jax.ShapeDtypeStruct((hp, t, PAIR), BF16), jax.ShapeDtypeStruct((hp, t, PAIR), F32),
                   jax.ShapeDtypeStruct((1, dc), F32), jax.ShapeDtypeStruct((hp, 1, PAIR), F32)),
        in_specs=[pl.BlockSpec((tb, dc + da), lambda i: (i, 0)),
                  pl.BlockSpec((tb, dc), lambda i: (i, 0)),
                  pl.BlockSpec((tb, dc), lambda i: (i, 1)),
                  pl.BlockSpec((tb, dc), lambda i: (i, 3)),
                  pl.BlockSpec((tb, da), lambda i: (i, 7)),
                  pair_spec,
                  pl.BlockSpec((1, dc), lambda i: (0, 0)),
                  pl.BlockSpec((hp, 1, PAIR), lambda i: (0, 0, 0))],
        out_specs=(pl.BlockSpec((tb, 4 * dc + 4 * da), lambda i: (i, 0)), pl.BlockSpec((tb, dc), lambda i: (i, 0)),
                   pair_spec, pair_spec,
                   pl.BlockSpec((1, dc), lambda i: (0, 0)), pl.BlockSpec((hp, 1, PAIR), lambda i: (0, 0, 0))),
        compiler_params=_params(("arbitrary",)),
    )(dycat, co, proj, proj, proj, o_mix, g_conv, g_attn_pairs)


def _conv_bwd(dconv_proj, dco, conv_proj, conv_w, dc, after):
    t = dco.shape[0]
    ct = CONV_TILE
    nct = dc // ct

    def body(dcp_in_ref, dco_ref, u_ref, cg_ref, w_ref, after_ref, dcp_ref, acc_ref):
        del dcp_in_ref, after_ref
        which = pl.program_id(1)
        g = dco_ref[...]
        u = u_ref[...].astype(F32)
        cg = cg_ref[...].astype(F32)
        g_prev, g_next = _shift_rows(g, t)
        da = w_ref[0:1, :] * g_next + w_ref[1:2, :] * g + w_ref[2:3, :] * g_prev
        dcp_ref[...] = (da * jnp.where(which == 0, cg, u)).astype(BF16)
        a = cg * u
        a_prev, a_next = _shift_rows(a, t)
        acc_ref[...] = jnp.concatenate(
            [jnp.sum(g * a_prev, axis=0, keepdims=True), jnp.sum(g * a, axis=0, keepdims=True),
             jnp.sum(g * a_next, axis=0, keepdims=True), jnp.sum(g, axis=0, keepdims=True),
             jnp.zeros((4, ct), F32)], axis=0)

    return pl.pallas_call(
        body, name="conv_bwd", grid=(nct, 2),
        out_shape=(jax.ShapeDtypeStruct(dconv_proj.shape, BF16), jax.ShapeDtypeStruct((8, dc), F32)),
        in_specs=[HBM,
                  pl.BlockSpec((t, ct), lambda i, s: (0, i)),
                  pl.BlockSpec((t, ct), lambda i, s: (0, i)),
                  pl.BlockSpec((t, ct), lambda i, s: (0, 2 * nct + i)),
                  pl.BlockSpec((3, ct), lambda i, s: (0, i)), ANY],
        out_specs=(pl.BlockSpec((t, ct), lambda i, s: (0, 2 * s * nct + i)),
                   pl.BlockSpec((8, ct), lambda i, s: (0, i))),
        input_output_aliases={0: 0},
        compiler_params=_params(("arbitrary", "arbitrary")),
    )(dconv_proj, dco, conv_proj, conv_proj, conv_w, after)


def _dh(dproj, winf, after):
    t = dproj.shape[0]
    _, d, ws = winf.shape
    tm = tn = COL_TILE
    nt = (((1,), (1,)), ((), ()))

    def body(a_ref, w_ref, after_ref, o_ref):
        del after_ref
        acc = lax.dot_general(a_ref[:, 0:ws], w_ref[0], nt, preferred_element_type=F32)
        for j in range(1, N_CHIPS):
            acc = acc + lax.dot_general(a_ref[:, j * ws:(j + 1) * ws], w_ref[j], nt, preferred_element_type=F32)
        o_ref[...] = acc.astype(BF16)

    return pl.pallas_call(
        body, name="dh", grid=(d // tn, t // tm),
        out_shape=jax.ShapeDtypeStruct((t, d), BF16),
        in_specs=[pl.BlockSpec((tm, N_CHIPS * ws), lambda n, m: (m, 0)),
                  pl.BlockSpec((N_CHIPS, tn, ws), lambda n, m: (0, n, 0)), ANY],
        out_specs=pl.BlockSpec((tm, tn), lambda n, m: (m, n)),
        compiler_params=_params(("parallel", "parallel")),
    )(dproj, winf, after)


def _prenorm_bwd(x, dh, dout, mod, g_pre):
    t, d = x.shape
    tb = ROW_TILE

    def body(x_ref, dh_ref, dout_ref, mod_ref, g_ref, gx_ref, acc_ref):
        xv = x_ref[...]
        dhv = dh_ref[...].astype(F32)
        r = lax.rsqrt(jnp.mean(xv * xv, axis=-1, keepdims=True) + EPS)
        xh = xv * r
        one_scale = 1.0 + mod_ref[1:2, :]
        a = dhv * one_scale * g_ref[...]
        gx_ref[...] = dout_ref[...].astype(F32) + r * (a - xh * jnp.mean(a * xh, axis=-1, keepdims=True))
        part = jnp.concatenate(
            [jnp.sum(dhv, axis=0, keepdims=True), jnp.sum(dhv * xh * g_ref[...], axis=0, keepdims=True),
             jnp.sum(dhv * xh * one_scale, axis=0, keepdims=True), jnp.zeros((5, d), F32)], axis=0)

        @pl.when(pl.program_id(0) == 0)
        def _():
            acc_ref[...] = jnp.zeros(acc_ref.shape, F32)

        acc_ref[...] += part

    return pl.pallas_call(
        body, name="prenorm_bwd", grid=(t // tb,),
        out_shape=(jax.ShapeDtypeStruct((t, d), F32), jax.ShapeDtypeStruct((8, d), F32)),
        in_specs=[pl.BlockSpec((tb, d), lambda i: (i, 0)), pl.BlockSpec((tb, d), lambda i: (i, 0)),
                  pl.BlockSpec((tb, d), lambda i: (i, 0)), pl.BlockSpec((3, d), lambda i: (0, 0)),
                  pl.BlockSpec((1, d), lambda i: (0, 0))],
        out_specs=(pl.BlockSpec((tb, d), lambda i: (i, 0)), pl.BlockSpec((8, d), lambda i: (0, 0))),
        compiler_params=_params(("arbitrary",)),
    )(x, dh, dout, mod, g_pre)


def _chip_sums(mine, rsib, name, part=0, parts=1, after=()):
    _, half, cols = mine.shape
    rows = half // parts
    tr = min(rows, ROW_TILE)
    nt = rows // tr

    def body(g_ref, r_ref, *rest):
        rest[-1][...] = (g_ref[...].astype(F32) + r_ref[...].astype(F32)).astype(BF16)

    spec = pl.BlockSpec((None, tr, cols), lambda j, i: (j, part * nt + i, 0))
    return pl.pallas_call(
        body, name=name, grid=(N_CHIPS, nt),
        out_shape=jax.ShapeDtypeStruct((N_CHIPS, rows, cols), BF16),
        in_specs=[spec, spec] + [ANY] * len(after), out_specs=pl.BlockSpec((None, tr, cols), lambda j, i: (j, i, 0)),
        compiler_params=_params(("parallel", "parallel")),
    )(mine, rsib, *after)


def _owner_sum(place, mine, rsib, rici, name, part=0, parts=1):
    _, half, cols = mine.shape
    rows = half // parts
    tr = min(rows, ROW_TILE)
    nt = rows // tr

    def body(place_ref, g_ref, r_ref, i_ref, o_ref):
        del place_ref
        acc = g_ref[...].astype(F32) + r_ref[...].astype(F32)
        for k in range(N_CHIPS - 1):
            acc = acc + i_ref[k].astype(F32)
        o_ref[...] = acc

    own = pl.BlockSpec((None, tr, cols), lambda i, p: (p[0], part * nt + i, 0))
    grid_spec = pltpu.PrefetchScalarGridSpec(
        num_scalar_prefetch=1, grid=(nt,),
        in_specs=[own, own, pl.BlockSpec((N_CHIPS - 1, tr, cols), lambda i, p: (0, i, 0))],
        out_specs=pl.BlockSpec((tr, cols), lambda i, p: (p[1] * (half // tr) + part * nt + i, 0)))
    return pl.pallas_call(
        body, name=name, grid_spec=grid_spec,
        out_shape=jax.ShapeDtypeStruct((2 * half, cols), F32),
        compiler_params=_params(("parallel",)),
    )(place, mine, rsib, rici)


def _adam_math(w, g, m, v):
    m2 = ADAM_B1 * m + (1.0 - ADAM_B1) * g
    v2 = ADAM_B2 * v + (1.0 - ADAM_B2) * (g * g)
    m_hat = m2 / (1.0 - ADAM_B1 ** ADAM_STEP)
    v_hat = v2 / (1.0 - ADAM_B2 ** ADAM_STEP)
    delta = -ADAM_LR * (m_hat / (jnp.sqrt(v_hat) + ADAM_EPS) + ADAM_WD * w)
    return delta, m2, v2


def _adamw(w, g, m, v, name, part=0, parts=1, prev=None):
    rows, cols = w.shape
    tr = min(rows, ROW_TILE)

    def body(*refs):
        w_ref, g_ref, m_ref, v_ref, go_ref, d_ref, m2_ref, v2_ref = refs[-8:]
        g = g_ref[...]
        go_ref[...] = g
        d_ref[...], m2_ref[...], v2_ref[...] = _adam_math(w_ref[...], g, m_ref[...], v_ref[...])

    if parts == 1:
        grid, spec = (rows // tr,), pl.BlockSpec((tr, cols), lambda i: (i, 0))
    else:
        per_half = rows // 2 // tr
        nt = per_half // parts
        grid, spec = (2, nt), pl.BlockSpec((tr, cols), lambda r, i: (r * per_half + part * nt + i, 0))
    olds = [] if prev is None else list(prev)
    return pl.pallas_call(
        body, name=name, grid=grid,
        out_shape=(jax.ShapeDtypeStruct(w.shape, F32),) * 4,
        in_specs=[HBM] * len(olds) + [spec] * 4, out_specs=(spec,) * 4,
        input_output_aliases={i: i for i in range(len(olds))},
        compiler_params=_params(("parallel",) * len(grid)),
    )(*olds, w, g, m, v)


def _ada_grad_adamw(c_all, dmod_cols, w, m, v):
    d, wa = w.shape
    tr = ROW_TILE

    def body(c_ref, dm_ref, w_ref, m_ref, v_ref, g_ref, d_ref, m2_ref, v2_ref):
        act = _silu(c_ref[...]).T
        g = act[:, 0:1] * dm_ref[0:1, :]
        for b in range(1, N_DEV):
            g = g + act[:, b:b + 1] * dm_ref[b:b + 1, :]
        g_ref[...] = g
        d_ref[...], m2_ref[...], v2_ref[...] = _adam_math(w_ref[...], g, m_ref[...], v_ref[...])

    spec = pl.BlockSpec((tr, wa), lambda i: (i, 0))
    return pl.pallas_call(
        body, name="ada_grad_adamw", grid=(d // tr,),
        out_shape=(jax.ShapeDtypeStruct(w.shape, F32),) * 4,
        in_specs=[pl.BlockSpec((N_DEV, tr), lambda i: (0, i)), pl.BlockSpec((N_DEV, wa), lambda i: (0, 0)),
                  spec, spec, spec],
        out_specs=(spec,) * 4,
        compiler_params=_params(("parallel",)),
    )(c_all, dmod_cols, w, m, v)


def _small_update(place, gathered, pieces, weights, moments_m, moments_v):
    n = gathered.shape[1]
    k = len(weights)
    final_shapes = [w.shape for w in weights]
    row_counts = [s[1] if len(s) == 3 else 1 for s in final_shapes]
    weights, moments_m, moments_v = ([a.reshape(1, -1) for a in arrays] for arrays in (weights, moments_m, moments_v))

    def body(place_ref, g_ref, *refs):
        w_refs, m_refs, v_refs = refs[0:k], refs[k:2 * k], refs[2 * k:3 * k]
        outs = refs[3 * k:]
        total = g_ref[0:SUBLANES, :]
        for dev in range(1, N_DEV):
            total = total + g_ref[SUBLANES * dev:SUBLANES * (dev + 1), :]

        def flat(offset, length):
            segments, pos = [], offset
            while pos < offset + length:
                row, col = divmod(pos, n)
                take = min(offset + length - pos, n - col)
                segments.append(total[row:row + 1, col:col + take])
                pos += take
            return jnp.concatenate(segments, axis=1) if len(segments) > 1 else segments[0]

        chip = place_ref[0]
        for i, (w_ref, m_ref, v_ref) in enumerate(zip(w_refs, m_refs, v_refs)):
            g = flat(*pieces[i])
            if pieces[i][1] > w_ref.shape[1]:
                rows = row_counts[i]
                cols, full = w_ref.shape[1] // rows, pieces[i][1] // rows
                picked = []
                for r in range(rows):
                    blocks = [g[:, r * full + q * cols:r * full + (q + 1) * cols] for q in range(N_CHIPS)]
                    mine = blocks[N_CHIPS - 1]
                    for q in range(N_CHIPS - 2, -1, -1):
                        mine = jnp.where(chip == q, blocks[q], mine)
                    picked.append(mine)
                g = jnp.concatenate(picked, axis=1)
            delta, m2, v2 = _adam_math(w_ref[...], g, m_ref[...], v_ref[...])
            for j, val in enumerate((g, delta, m2, v2)):
                outs[j * k + i][...] = val
        outs[4 * k][...] = flat(*pieces[k])

    shapes = [jax.ShapeDtypeStruct(w.shape, F32) for w in weights]
    grid_spec = pltpu.PrefetchScalarGridSpec(
        num_scalar_prefetch=1, grid=(1,),
        in_specs=[pl.BlockSpec(gathered.shape, lambda i, p: (0, 0))]
        + [pl.BlockSpec(a.shape, functools.partial(lambda nd, i, p: (0,) * nd, a.ndim))
           for a in (*weights, *moments_m, *moments_v)],
        out_specs=tuple(pl.BlockSpec(s.shape, functools.partial(lambda nd, i, p: (0,) * nd, len(s.shape)))
                        for s in shapes * 4) + (pl.BlockSpec((1, LANES), lambda i, p: (0, 0)),))
    outs = pl.pallas_call(
        body, name="small_update", grid_spec=grid_spec,
        out_shape=tuple(shapes * 4) + (jax.ShapeDtypeStruct((1, LANES), F32),),
        compiler_params=_params(("arbitrary",)),
    )(place, gathered, *weights, *moments_m, *moments_v)
    shaped = [out.reshape(final_shapes[i % k]) for i, out in enumerate(outs[0:4 * k])]
    return shaped[0:k], shaped[k:2 * k], shaped[2 * k:3 * k], shaped[3 * k:4 * k], outs[4 * k]


def _pack_small(pieces):
    flat = [p.reshape(-1).astype(F32) for p in pieces]
    offsets, total = [], 0
    for p in flat:
        offsets.append(total)
        total += p.shape[0]
    padded = -(-total // SMALL_ALIGN) * SMALL_ALIGN
    if padded > total:
        flat.append(jnp.zeros((padded - total,), F32))
    return jnp.concatenate(flat).reshape(8, padded // 8), offsets


def _alibi_slope_rows(n_heads):
    slopes = 2.0 ** (-8.0 * np.arange(1, n_heads + 1, dtype=np.float64) / n_heads)
    rows = np.zeros((n_heads // 2, SUBLANES), np.float32)
    rows[:, 0:2] = slopes.reshape(n_heads // 2, 2)
    return jnp.asarray(np.broadcast_to(rows[:, :, None], (n_heads // 2, SUBLANES, ATT_KW)))


def kernel(x, c, w_ada, b_ada, g_pre, w_in, conv_w, conv_b, g_conv, g_attn, w_out, g_post, loss_target, m_w_ada, m_b_ada, m_g_pre, m_w_in, m_conv_w, m_conv_b, m_g_conv, m_g_attn, m_w_out, m_g_post, v_w_ada, v_b_ada, v_g_pre, v_w_in, v_conv_w, v_conv_b, v_g_conv, v_g_attn, v_w_out, v_g_post):
    t, d = x.shape[1], x.shape[2]
    dc = conv_b.shape[1]
    da = g_attn.shape[1]
    hp = da // PAIR
    ws = w_in.shape[2]
    wa = w_ada.shape[2]
    cws = conv_w.shape[2]
    assert t % ROW_TILE == 0 and d % ROW_TILE == 0 and dc % COL_TILE == 0 and da % COL_TILE == 0
    assert ws == 2 * dc and dc == da and t // BRANCHES[-1][1] >= ATT_BQ

    mx, my, mc = _my_place()
    chip = _chip_of(mx, my)
    dev = 2 * chip + mc
    place = jnp.stack([chip, mc]).astype(jnp.int32)

    x2, tgt2 = x[0], loss_target[0]
    w_ada2, w_in2, w_out2 = w_ada[0], w_in[0], w_out[0]

    packed, offs = _pack_small([c[0], conv_w[0]])
    seen, mod, win_slots = _ada_modulation(packed, w_ada2, b_ada, d, w_in2)
    seen = seen.reshape(N_DEV, -1)
    c_all = seen[:, offs[0]:offs[0] + d]
    conv_w_full = seen[0::2, offs[1]:offs[1] + 3 * cws].reshape(N_CHIPS, 3, cws).transpose(1, 0, 2).reshape(3, dc)

    win_flight, send_in, recv_in, started = _gather_start(win_slots, mod)

    y_chip, x_chip, d_chip = (_chip_of(mx, 1 - my), _chip_of(1 - mx, my), _chip_of(1 - mx, 1 - my))
    tiles_per_part = ws // COL_TILE // 2

    def tiles_of(chunk, parts):
        return [(2 * chunk + part) * tiles_per_part + k for part in parts for k in range(tiles_per_part)]

    tiles = jnp.stack([jnp.stack(step) for step in (
        tiles_of(chip, (0, 1)), tiles_of(y_chip, (0,)) + tiles_of(x_chip, (1,)),
        tiles_of(y_chip, (1,)) + tiles_of(x_chip, (0,)), tiles_of(d_chip, (0, 1)))]).astype(jnp.int32)
    h, ht = _prenorm(x2, mod, g_pre, started)
    proj = _proj_tiles(None, h, w_in2, tiles, 0, "proj_own")
    win_flight, wout_flight, relay_send_in, relay_recv_in, send_out, recv_out = _gather_relay_in(
        win_flight, _cast_into_slot(place, w_out2, "cast_w_out", proj), recv_in, proj)
    win_flight = _forward_halves(win_flight, ((0, 0), (1, 1)), "forward_w_in_first")
    proj = _proj_tiles(proj, h, win_flight, tiles, 1, "proj_first_parts")
    win_flight = _forward_halves(
        _gather_wait_direct(win_flight, send_in, recv_in, proj, "gather_wait_w_in_direct"),
        ((0, 1), (1, 0)), "forward_w_in_second")
    proj = _proj_tiles(proj, h, win_flight, tiles, 2, "proj_second_parts")
    winf = _forward_halves(
        _gather_wait_relayed(win_flight, relay_send_in, relay_recv_in, proj, "gather_wait_w_in_relayed"),
        ((2, None),), "forward_w_in_relayed")
    proj = _proj_tiles(proj, h, winf, tiles, 3, "proj_diagonal")
    slopes = _alibi_slope_rows(da // HEAD_DIM)
    co = _conv_fwd(proj, conv_w_full, conv_b, dc)
    wout_flight, relay_send_out, relay_recv_out = _gather_relay_out(wout_flight, recv_out, co)
    o_mix, lse = _attn_fwd(proj, slopes, dc, da)
    g_attn_pairs = g_attn.reshape(hp, 1, PAIR)
    wout_flight = _gather_wait_direct(wout_flight, send_out, recv_out, o_mix, "gather_wait_w_out_direct")
    wout_flight = _gather_wait_relayed(wout_flight, relay_send_out, relay_recv_out, o_mix, "gather_wait_w_out_relayed")
    all_halves = ((0, None), (1, None), (2, None))
    wout_flight, fsend_out, frecv_out, forwarding = _forward_start(wout_flight, all_halves, "forward_w_out_start")
    ycat, ycat_t = _mix_fwd(co, proj, o_mix, g_conv, g_attn_pairs, forwarding)
    woutf = _forward_wait(wout_flight, all_halves, fsend_out, frecv_out, ycat, "forward_w_out_wait").reshape(dc + da, d)
    dout, dy, post_sums = _out_fwd_bwd(ycat, woutf, x2, tgt2, mod, g_post)

    gout, rsib_out = _dw_swapped(ycat_t, dy, N_CHIPS, 1, "dw_out")
    csum_out = _chip_sums(gout, rsib_out, "rs_chip_sum_out")
    ssem_out, rsem_out, csum_out, land_out, sent_out = _owners_start(csum_out, "rs_owners_start_out")
    dycat = _matmul_nt(dy, woutf, BF16, "dycat")
    dproj, dco, d_o, delta, dg_conv, dg_attn = _mix_bwd(dycat, co, proj, o_mix, g_conv, g_attn_pairs)
    dproj, conv_sums = _conv_bwd(dproj, dco, proj, conv_w_full, dc, sent_out)
    dproj = _attn_bwd(dproj, proj, d_o, lse, delta, slopes, dc, da, sent_out)
    gin, rsib_in = _dw_swapped(ht, dproj, 1, N_CHIPS, "dw_in")
    ssem_in0, rsem_in0, csum_in0, land_in0, sent_in0 = _owners_start(
        _chip_sums(gin, rsib_in, "rs_chip_sum_in0", 0, 2), "rs_owners_start_in0")
    ssem_in1, rsem_in1, csum_in1, land_in1, sent_in = _owners_start(
        _chip_sums(gin, rsib_in, "rs_chip_sum_in1", 1, 2, after=(sent_in0,)), "rs_owners_start_in1")
    dh = _dh(dproj, winf, sent_in)
    grad_x, pre_sums = _prenorm_bwd(x2, dh, dout, mod, g_pre)

    small, so = _pack_small([
        pre_sums[0], pre_sums[1], post_sums[0],
        pre_sums[2], conv_sums[0:3], conv_sums[3], dg_conv, dg_attn, post_sums[1], post_sums[2, 0:128]])
    ssem_small, rsem_small, small, land_small, sent_small = _allgather8_start(small, dev, "gather_small_start")

    rici_out = _owners_wait(ssem_out, rsem_out, csum_out, land_out, [grad_x, sent_small], "rs_owners_wait_out")
    full_out, jsend_out, jrecv_out, joining_out = _join_start(
        _owner_sum(place, gout, rsib_out, rici_out, "rs_owner_sum_out"), "rs_join_start_out", 0, 1)
    rici_in = _owners_wait(ssem_in0, rsem_in0, csum_in0, land_in0, [joining_out], "rs_owners_wait_in0")
    full_in0, jsend0, jrecv0, joining0 = _join_start(
        _owner_sum(place, gin, rsib_in, rici_in, "rs_owner_sum_in0", 0, 2), "rs_join_start_in0", 0, 2)
    grad_w_out = _join_wait(full_out, jsend_out, jrecv_out, [joining0], "rs_join_wait_out", 0, 1)
    grad_w_out, delta_w_out, new_m_w_out, new_v_w_out = _adamw(
        w_out2, grad_w_out, m_w_out[0], v_w_out[0], "adamw_w_out")
    full_in0 = _join_wait(full_in0, jsend0, jrecv0, [delta_w_out], "rs_join_wait_in0", 0, 2)
    updated_in = _adamw(w_in2, full_in0, m_w_in[0], v_w_in[0], "adamw_w_in0", 0, 2)
    rici_in = _owners_wait(ssem_in1, rsem_in1, csum_in1, land_in1, [updated_in[1]], "rs_owners_wait_in1")
    full_in1, jsend1, jrecv1, joining1 = _join_start(
        _owner_sum(place, gin, rsib_in, rici_in, "rs_owner_sum_in1", 1, 2), "rs_join_start_in1", 1, 2)

    small_seen = _allgather8_wait(ssem_small, rsem_small, small, land_small, [joining1], "gather_small_wait")
    small_w = [b_ada, g_pre, conv_w, conv_b, g_conv, g_attn, g_post]
    small_m = [m_b_ada, m_g_pre, m_conv_w, m_conv_b, m_g_conv, m_g_attn, m_g_post]
    small_v = [v_b_ada, v_g_pre, v_conv_w, v_conv_b, v_g_conv, v_g_attn, v_g_post]
    pieces = [(0, 3 * d), (so[3], d), (so[4], 3 * dc), (so[5], dc), (so[6], dc), (so[7], da), (so[8], d), (so[9], LANES)]
    g_small, d_small, m_small, v_small, loss_row = _small_update(place, small_seen, pieces, small_w, small_m, small_v)
    loss = loss_row[0, 0]
    grad_b_ada, grad_g_pre, grad_conv_w, grad_conv_b, grad_g_conv, grad_g_attn, grad_g_post = g_small
    dmod_cols = lax.dynamic_slice_in_dim(small_seen.reshape(N_DEV, -1), chip * wa, wa, axis=1)
    grad_w_ada, delta_w_ada, new_m_w_ada, new_v_w_ada = _ada_grad_adamw(c_all, dmod_cols, w_ada2, m_w_ada[0], v_w_ada[0])

    full_in1 = _join_wait(full_in1, jsend1, jrecv1, [delta_w_ada, d_small[0]], "rs_join_wait_in1", 1, 2)
    grad_w_in, delta_w_in, new_m_w_in, new_v_w_in = _adamw(
        w_in2, full_in1, m_w_in[0], v_w_in[0], "adamw_w_in1", 1, 2, updated_in)

    def lead(a):
        return a.reshape((1,) + a.shape)

    grads = [lead(grad_w_ada), grad_b_ada, grad_g_pre, lead(grad_w_in), grad_conv_w, grad_conv_b, grad_g_conv,
             grad_g_attn, lead(grad_w_out), grad_g_post]
    deltas = [lead(delta_w_ada), d_small[0], d_small[1], lead(delta_w_in), d_small[2], d_small[3], d_small[4],
              d_small[5], lead(delta_w_out), d_small[6]]
    new_ms = [lead(new_m_w_ada), m_small[0], m_small[1], lead(new_m_w_in), m_small[2], m_small[3], m_small[4],
              m_small[5], lead(new_m_w_out), m_small[6]]
    new_vs = [lead(new_v_w_ada), v_small[0], v_small[1], lead(new_v_w_in), v_small[2], v_small[3], v_small[4],
              v_small[5], lead(new_v_w_out), v_small[6]]
    return (loss, lead(grad_x), *grads, *deltas, *new_ms, *new_vs)
```

```python
import functools

import jax
import jax.numpy as jnp
import numpy as np
from jax import lax
from jax.experimental import pallas as pl
from jax.experimental.pallas import tpu as pltpu

F32 = jnp.float32
BF16 = jnp.bfloat16
MESH = pl.DeviceIdType.MESH
HBM = pl.BlockSpec(memory_space=pltpu.HBM)
VMEM = pl.BlockSpec(memory_space=pltpu.VMEM)
ANY = pl.BlockSpec(memory_space=pl.ANY)
SEM = pl.BlockSpec(memory_space=pltpu.SEMAPHORE)
EFFECT = pltpu.SideEffectType.DATAFLOW_SIDE_EFFECTING
SUBLANES, LANES = 8, 128
TOKEN = jax.ShapeDtypeStruct((SUBLANES, LANES), jnp.float32)

HEAD_DIM = 64
PAIR = 2 * HEAD_DIM
assert PAIR == LANES
BRANCHES = ((128, 1), (512, 4), (2048, 16))
assert BRANCHES[0][1] == 1
SIDE = 64
EPS = 1e-6
NEG_INF = -1e30
N_CHIPS = 4
N_DEV = 8

ADAM_LR = 0.001
ADAM_B1 = 0.9
ADAM_B2 = 0.999
ADAM_EPS = 1e-08
ADAM_WD = 0.01
ADAM_STEP = 10

VMEM_LIMIT_BYTES = 56 * 1024 * 1024
ROW_TILE = 256
COL_TILE = 512
CONV_TILE = 256
ATT_BQ = 128
ATT_KW = ATT_BQ + 2 * SIDE
ATT_UNROLL = 4
STAGE_PAD = 4
SMALL_ALIGN = SUBLANES * LANES


def _params(semantics=None):
    kw = {"vmem_limit_bytes": VMEM_LIMIT_BYTES}
    if semantics is not None:
        kw["dimension_semantics"] = semantics
    return pltpu.CompilerParams(**kw)


def _silu(z):
    return z * jax.nn.sigmoid(z)


def _silu_grad(z):
    s = jax.nn.sigmoid(z)
    return s * (1.0 + z * (1.0 - s))


def _my_place():
    return lax.axis_index("x"), lax.axis_index("y"), lax.axis_index("c")


def _flip(a, bit):
    return 1 - a if bit else a


def _chip_of(x, y):
    return 2 * x + y


def _allgather8_start(v, me, name):
    rows_per, n = v.shape
    land = lax.dynamic_update_slice(jnp.zeros((N_DEV * rows_per, n), v.dtype), v, (me * rows_per, 0))

    def body(v_ref, land_ref, send_sems, recv_sems, v_thru, land_thru, token_ref):
        del v_thru, land_thru
        x, y, c = _my_place()
        mine = land_ref.at[pl.ds(pl.multiple_of((4 * x + 2 * y + c) * rows_per, rows_per), rows_per), :]
        for k in range(1, N_DEV):
            peer = (_flip(x, k & 4), _flip(y, k & 2), _flip(c, k & 1))
            pltpu.make_async_remote_copy(
                src_ref=v_ref, dst_ref=mine, send_sem=send_sems.at[k - 1], recv_sem=recv_sems.at[k - 1],
                device_id=peer, device_id_type=MESH).start()
        token_ref[...] = jnp.zeros(token_ref.shape, F32)

    sems = pltpu.SemaphoreType.DMA((N_DEV - 1,))
    return pl.pallas_call(
        body, name=name,
        out_shape=(sems, sems, jax.ShapeDtypeStruct(v.shape, v.dtype), jax.ShapeDtypeStruct(land.shape, land.dtype), TOKEN),
        in_specs=[HBM, HBM], out_specs=(SEM, SEM, HBM, HBM, VMEM),
        input_output_aliases={0: 2, 1: 3},
        compiler_params=pltpu.CompilerParams(has_side_effects=EFFECT),
    )(pltpu.with_memory_space_constraint(v, pltpu.HBM), pltpu.with_memory_space_constraint(land, pltpu.HBM))


def _allgather8_wait(send_sems, recv_sems, v, land, after, name):
    rows_per = v.shape[0]

    def body(v_ref, land_ref, send_ref, recv_ref, *rest):
        del rest
        x, y, c = _my_place()
        for k in range(1, N_DEV):
            peer = (_flip(x, k & 4), _flip(y, k & 2), _flip(c, k & 1))
            src = 4 * peer[0] + 2 * peer[1] + peer[2]
            cp = pltpu.make_async_remote_copy(
                src_ref=v_ref, dst_ref=land_ref.at[pl.ds(pl.multiple_of(src * rows_per, rows_per), rows_per), :],
                send_sem=send_ref.at[k - 1], recv_sem=recv_ref.at[k - 1], device_id=peer, device_id_type=MESH)
            cp.wait_send()
            cp.wait_recv()

    return pl.pallas_call(
        body, name=name,
        out_shape=(jax.ShapeDtypeStruct(v.shape, v.dtype), jax.ShapeDtypeStruct(land.shape, land.dtype)),
        in_specs=[HBM, HBM, SEM, SEM] + [ANY] * len(after), out_specs=(HBM, HBM),
        input_output_aliases={0: 0, 1: 1},
        compiler_params=pltpu.CompilerParams(has_side_effects=EFFECT),
    )(v, land, send_sems, recv_sems, *after)[1]


def _half_rows(ref, chip, which, half):
    return ref.at[chip, pl.ds(pl.multiple_of(which * half, half), half), :]


def _ici_peers(x, y, c):
    peers = [(_flip(x, k & 2), _flip(y, k & 1), c) for k in (1, 2, 3)]
    return [(peer, _chip_of(peer[0], peer[1])) for peer in peers]


def _part_of_half(ref, chip, core, part):
    half, cols = ref.shape[1] // 2, ref.shape[2] // 2
    return ref.at[chip, pl.ds(pl.multiple_of(core * half, half), half), pl.ds(part * cols, cols)]


def _neighbours(x, y, c):
    return [((x, 1 - y, c), _chip_of(x, 1 - y)), ((1 - x, y, c), _chip_of(1 - x, y)),
            ((1 - x, 1 - y, c), _chip_of(1 - x, 1 - y))]


def _start_direct(buf, send_sems, recv_sems):
    x, y, c = _my_place()
    me = _chip_of(x, y)
    for n, (peer, _) in enumerate(_neighbours(x, y, c)[0:2]):
        for part in ((0, 1), (1, 0))[n]:
            piece = _part_of_half(buf, me, c, part)
            pltpu.make_async_remote_copy(
                src_ref=piece, dst_ref=piece, send_sem=send_sems.at[2 * n + part], recv_sem=recv_sems.at[2 * n + part],
                device_id=peer, device_id_type=MESH).start()


def _relay(buf, recv_sems, relay_send, relay_recv):
    x, y, c = _my_place()
    nbrs = _neighbours(x, y, c)
    for n in range(2):
        part = n
        piece = _part_of_half(buf, nbrs[n][1], c, part)
        pltpu.make_async_remote_copy(
            src_ref=piece, dst_ref=piece, send_sem=relay_send.at[part], recv_sem=recv_sems.at[2 * n + part],
            device_id=nbrs[n][0], device_id_type=MESH).wait_recv()
        pltpu.make_async_remote_copy(
            src_ref=piece, dst_ref=piece, send_sem=relay_send.at[part], recv_sem=relay_recv.at[part],
            device_id=nbrs[1 - n][0], device_id_type=MESH).start()


def _gather_start(win_slots, after):
    def body(win_in, after_ref, win_ref, send_sems, recv_sems, token_ref):
        del win_in, after_ref
        _start_direct(win_ref, send_sems, recv_sems)
        token_ref[...] = jnp.zeros(token_ref.shape, F32)

    sems = pltpu.SemaphoreType.DMA((4,))
    return pl.pallas_call(
        body, name="gather_start",
        out_shape=(jax.ShapeDtypeStruct(win_slots.shape, win_slots.dtype), sems, sems, TOKEN),
        in_specs=[HBM, ANY], out_specs=(HBM, SEM, SEM, VMEM),
        input_output_aliases={0: 0},
        compiler_params=pltpu.CompilerParams(has_side_effects=EFFECT),
    )(win_slots, after)


def _gather_relay_in(win, wout_slots, recv_in, after):
    def body(win_in, wout_in, recv_in_ref, after_ref, win_ref, wout_ref, relay_send, relay_recv, send_out, recv_out):
        del win_in, wout_in, after_ref
        _relay(win_ref, recv_in_ref, relay_send, relay_recv)
        _start_direct(wout_ref, send_out, recv_out)

    two, four = pltpu.SemaphoreType.DMA((2,)), pltpu.SemaphoreType.DMA((4,))
    return pl.pallas_call(
        body, name="gather_relay_w_in",
        out_shape=(jax.ShapeDtypeStruct(win.shape, win.dtype), jax.ShapeDtypeStruct(wout_slots.shape, wout_slots.dtype),
                   two, two, four, four),
        in_specs=[HBM, HBM, SEM, ANY], out_specs=(HBM, HBM, SEM, SEM, SEM, SEM),
        input_output_aliases={0: 0, 1: 1},
        compiler_params=pltpu.CompilerParams(has_side_effects=EFFECT),
    )(win, wout_slots, recv_in, after)


def _gather_relay_out(wout, recv_out, after):
    def body(wout_in, recv_out_ref, after_ref, wout_ref, relay_send, relay_recv):
        del wout_in, after_ref
        _relay(wout_ref, recv_out_ref, relay_send, relay_recv)

    two = pltpu.SemaphoreType.DMA((2,))
    return pl.pallas_call(
        body, name="gather_relay_w_out",
        out_shape=(jax.ShapeDtypeStruct(wout.shape, wout.dtype), two, two),
        in_specs=[HBM, SEM, ANY], out_specs=(HBM, SEM, SEM),
        input_output_aliases={0: 0},
        compiler_params=pltpu.CompilerParams(has_side_effects=EFFECT),
    )(wout, recv_out, after)


def _gather_wait_direct(buf, send_sems, recv_sems, after, name):
    def body(buf_in, send_ref, recv_ref, after_ref, buf_ref):
        del buf_in, after_ref
        x, y, c = _my_place()
        me = _chip_of(x, y)
        for n, (peer, chip) in enumerate(_neighbours(x, y, c)[0:2]):
            second = 1 - n
            pltpu.make_async_remote_copy(
                src_ref=_part_of_half(buf_ref, me, c, second), dst_ref=_part_of_half(buf_ref, chip, c, second),
                send_sem=send_ref.at[2 * n + second], recv_sem=recv_ref.at[2 * n + second],
                device_id=peer, device_id_type=MESH).wait_recv()
            for part in range(2):
                piece = _part_of_half(buf_ref, me, c, part)
                pltpu.make_async_remote_copy(
                    src_ref=piece, dst_ref=piece, send_sem=send_ref.at[2 * n + part], recv_sem=recv_ref.at[2 * n + part],
                    device_id=peer, device_id_type=MESH).wait_send()

    return pl.pallas_call(
        body, name=name,
        out_shape=jax.ShapeDtypeStruct(buf.shape, buf.dtype),
        in_specs=[HBM, SEM, SEM, ANY], out_specs=HBM,
        input_output_aliases={0: 0},
        compiler_params=pltpu.CompilerParams(has_side_effects=EFFECT),
    )(buf, send_sems, recv_sems, after)


def _gather_wait_relayed(buf, relay_send, relay_recv, after, name):
    def body(buf_in, rsend_ref, rrecv_ref, after_ref, buf_ref):
        del buf_in, after_ref
        x, y, c = _my_place()
        nbrs = _neighbours(x, y, c)
        for n in range(2):
            relayed = _part_of_half(buf_ref, nbrs[n][1], c, n)
            cp = pltpu.make_async_remote_copy(
                src_ref=relayed, dst_ref=_part_of_half(buf_ref, nbrs[2][1], c, n),
                send_sem=rsend_ref.at[n], recv_sem=rrecv_ref.at[n], device_id=nbrs[1 - n][0], device_id_type=MESH)
            cp.wait_recv()
            cp.wait_send()

    return pl.pallas_call(
        body, name=name,
        out_shape=jax.ShapeDtypeStruct(buf.shape, buf.dtype),
        in_specs=[HBM, SEM, SEM, ANY], out_specs=HBM,
        input_output_aliases={0: 0},
        compiler_params=pltpu.CompilerParams(has_side_effects=EFFECT),
    )(buf, relay_send, relay_recv, after)


def _forward_copies(buf_ref, which, send_sems, recv_sems):
    half = buf_ref.shape[1] // 2
    x, y, c = _my_place()

    def copy(k, chip, core, part):
        piece = _half_rows(buf_ref, chip, core, half) if part is None else _part_of_half(buf_ref, chip, core, part)
        return pltpu.make_async_remote_copy(
            src_ref=piece, dst_ref=piece, send_sem=send_sems.at[k], recv_sem=recv_sems.at[k],
            device_id=(x, y, 1 - c), device_id_type=MESH)

    chips = [_neighbours(x, y, c)[n][1] for n, _ in which]
    return [(copy(k, chip, c, part), copy(k, chip, 1 - c, part)) for k, (chip, (_, part)) in enumerate(zip(chips, which))]


def _forward_halves(buf, which, name):
    def body(buf_in, buf_ref, send_sems, recv_sems):
        del buf_in
        copies = _forward_copies(buf_ref, which, send_sems, recv_sems)
        for mine, _ in copies:
            mine.start()
        for mine, theirs in copies:
            theirs.wait_recv()
        for mine, _ in copies:
            mine.wait_send()

    return pl.pallas_call(
        body, name=name,
        out_shape=jax.ShapeDtypeStruct(buf.shape, buf.dtype),
        in_specs=[HBM], out_specs=HBM,
        input_output_aliases={0: 0},
        scratch_shapes=[pltpu.SemaphoreType.DMA((len(which),))] * 2,
    )(buf)


def _forward_start(buf, which, name):
    def body(buf_in, buf_ref, send_sems, recv_sems, token_ref):
        del buf_in
        for mine, _ in _forward_copies(buf_ref, which, send_sems, recv_sems):
            mine.start()
        token_ref[...] = jnp.zeros(token_ref.shape, F32)

    sems = pltpu.SemaphoreType.DMA((len(which),))
    return pl.pallas_call(
        body, name=name,
        out_shape=(jax.ShapeDtypeStruct(buf.shape, buf.dtype), sems, sems, TOKEN),
        in_specs=[HBM], out_specs=(HBM, SEM, SEM, VMEM),
        input_output_aliases={0: 0},
        compiler_params=pltpu.CompilerParams(has_side_effects=EFFECT),
    )(buf)


def _forward_wait(buf, which, send_sems, recv_sems, after, name):
    def body(buf_in, send_ref, recv_ref, after_ref, buf_ref):
        del buf_in, after_ref
        for mine, theirs in _forward_copies(buf_ref, which, send_ref, recv_ref):
            theirs.wait_recv()
            mine.wait_send()

    return pl.pallas_call(
        body, name=name,
        out_shape=jax.ShapeDtypeStruct(buf.shape, buf.dtype),
        in_specs=[HBM, SEM, SEM, ANY], out_specs=HBM,
        input_output_aliases={0: 0},
        compiler_params=pltpu.CompilerParams(has_side_effects=EFFECT),
    )(buf, send_sems, recv_sems, after)


def _dw_swapped(a, b, row_chunks, col_chunks, name):
    r, t = a.shape
    c_all = b.shape[1]
    chunks = row_chunks * col_chunks
    rq, cq = r // row_chunks, c_all // col_chunks
    half = rq // 2
    tn = COL_TILE
    nt = cq // tn
    steps = col_chunks * nt

    def body(a_ref, b_ref, mine_ref, sib_ref, stage, send_sems, recv_sems):
        x, y, c = _my_place()
        j, n = pl.program_id(0), pl.program_id(1)
        step = j * nt + n
        slot = step % 2
        res = jnp.dot(a_ref[...], b_ref[...], preferred_element_type=F32).astype(BF16)

        def landing(jj, nn):
            cols = pl.ds(pl.multiple_of(nn * tn, tn), tn)
            return sib_ref.at[:, :, cols] if col_chunks == 1 else sib_ref.at[pl.ds(jj, 1), :, cols]

        def copy(slot_, step_, jj, nn):
            return pltpu.make_async_remote_copy(
                src_ref=stage.at[slot_], dst_ref=landing(jj, nn), send_sem=send_sems.at[slot_],
                recv_sem=recv_sems.at[step_], device_id=(x, y, 1 - c), device_id_type=MESH)

        @pl.when(step >= 2)
        def _():
            copy(slot, step, j, n).wait_send()

        for q in range(row_chunks):
            lo = res[q * rq:q * rq + half, :]
            hi = res[q * rq + half:(q + 1) * rq, :]
            mine_ref[q] = jnp.where(c == 0, lo, hi)
            stage[slot, q] = jnp.where(c == 0, hi, lo)
        copy(slot, step, j, n).start()

        @pl.when(step == steps - 1)
        def _():
            for s in range(max(steps - 2, 0), steps):
                copy(s % 2, s, j, n).wait_send()
            for s in range(steps):
                copy(s % 2, s, j, n).wait_recv()

    shape = jax.ShapeDtypeStruct((chunks, half, cq), BF16)
    return pl.pallas_call(
        body, name=name, grid=(col_chunks, nt),
        out_shape=(shape, shape),
        in_specs=[pl.BlockSpec((r, t), lambda j, n: (0, 0)), pl.BlockSpec((t, tn), lambda j, n: (0, j * nt + n))],
        out_specs=(pl.BlockSpec((row_chunks, half, tn), lambda j, n: (j, 0, n)), ANY),
        scratch_shapes=[pltpu.VMEM((2, row_chunks, half, tn), BF16), pltpu.SemaphoreType.DMA((2,)),
                        pltpu.SemaphoreType.DMA((steps,))],
        compiler_params=_params(("arbitrary", "arbitrary")),
    )(a, b)


def _owners_start(csum, name, after=()):
    land = pltpu.with_memory_space_constraint(lax.empty((N_CHIPS - 1,) + csum.shape[1:], csum.dtype), pltpu.HBM)

    def body(csum_ref, land_ref, *rest):
        send_sems, recv_sems, _, _, token_ref = rest[len(after):]
        x, y, c = _my_place()
        for k, (peer, owner) in enumerate(_ici_peers(x, y, c)):
            pltpu.make_async_remote_copy(
                src_ref=csum_ref.at[owner], dst_ref=land_ref.at[k], send_sem=send_sems.at[k], recv_sem=recv_sems.at[k],
                device_id=peer, device_id_type=MESH).start()
        token_ref[...] = jnp.zeros(token_ref.shape, F32)

    sems = pltpu.SemaphoreType.DMA((N_CHIPS - 1,))
    return pl.pallas_call(
        body, name=name,
        out_shape=(sems, sems, jax.ShapeDtypeStruct(csum.shape, csum.dtype),
                   jax.ShapeDtypeStruct(land.shape, land.dtype), TOKEN),
        in_specs=[HBM, HBM] + [ANY] * len(after), out_specs=(SEM, SEM, HBM, HBM, VMEM),
        input_output_aliases={0: 2, 1: 3},
        compiler_params=pltpu.CompilerParams(has_side_effects=EFFECT),
    )(pltpu.with_memory_space_constraint(csum, pltpu.HBM), land, *after)


def _owners_wait(send_sems, recv_sems, csum, land, after, name):
    def body(csum_ref, land_ref, send_ref, recv_ref, *rest):
        del rest
        x, y, c = _my_place()
        for k, (peer, owner) in enumerate(_ici_peers(x, y, c)):
            cp = pltpu.make_async_remote_copy(
                src_ref=csum_ref.at[owner], dst_ref=land_ref.at[k], send_sem=send_ref.at[k], recv_sem=recv_ref.at[k],
                device_id=peer, device_id_type=MESH)
            cp.wait_send()
            cp.wait_recv()

    return pl.pallas_call(
        body, name=name,
        out_shape=(jax.ShapeDtypeStruct(csum.shape, csum.dtype), jax.ShapeDtypeStruct(land.shape, land.dtype)),
        in_specs=[HBM, HBM, SEM, SEM] + [ANY] * len(after), out_specs=(HBM, HBM),
        input_output_aliases={0: 0, 1: 1},
        compiler_params=pltpu.CompilerParams(has_side_effects=EFFECT),
    )(csum, land, send_sems, recv_sems, *after)[1]


def _join_start(full, name, part, parts):
    half = full.shape[0] // 2
    rows = half // parts

    def body(full_in, full_ref, send_sem, recv_sem, token_ref):
        del full_in
        x, y, c = _my_place()
        mine = full_ref.at[pl.ds(pl.multiple_of(c * half + part * rows, rows), rows), :]
        pltpu.make_async_remote_copy(
            src_ref=mine, dst_ref=mine, send_sem=send_sem.at[0], recv_sem=recv_sem.at[0],
            device_id=(x, y, 1 - c), device_id_type=MESH).start()
        token_ref[...] = jnp.zeros(token_ref.shape, F32)

    one = pltpu.SemaphoreType.DMA((1,))
    return pl.pallas_call(
        body, name=name,
        out_shape=(jax.ShapeDtypeStruct(full.shape, full.dtype), one, one, TOKEN),
        in_specs=[HBM], out_specs=(HBM, SEM, SEM, VMEM),
        input_output_aliases={0: 0},
        compiler_params=pltpu.CompilerParams(has_side_effects=EFFECT),
    )(full)


def _join_wait(full, send_sem, recv_sem, after, name, part, parts):
    half = full.shape[0] // 2
    rows = half // parts

    def body(full_in, send_ref, recv_ref, *rest):
        del full_in
        full_ref = rest[-1]
        x, y, c = _my_place()
        cp = pltpu.make_async_remote_copy(
            src_ref=full_ref.at[pl.ds(pl.multiple_of(c * half + part * rows, rows), rows), :],
            dst_ref=full_ref.at[pl.ds(pl.multiple_of((1 - c) * half + part * rows, rows), rows), :],
            send_sem=send_ref.at[0], recv_sem=recv_ref.at[0], device_id=(x, y, 1 - c), device_id_type=MESH)
        cp.wait_send()
        cp.wait_recv()

    return pl.pallas_call(
        body, name=name,
        out_shape=jax.ShapeDtypeStruct(full.shape, full.dtype),
        in_specs=[HBM, SEM, SEM] + [ANY] * len(after), out_specs=HBM,
        input_output_aliases={0: 0},
        compiler_params=pltpu.CompilerParams(has_side_effects=EFFECT),
    )(full, send_sem, recv_sem, *after)


def _cast_into_slot(place, w, name, after):
    rows, cols = w.shape
    tr = min(rows, ROW_TILE)

    def body(place_ref, w_ref, after_ref, o_ref):
        del place_ref, after_ref
        o_ref[...] = w_ref[...].astype(BF16)

    grid_spec = pltpu.PrefetchScalarGridSpec(
        num_scalar_prefetch=1, grid=(rows // tr,),
        in_specs=[pl.BlockSpec((tr, cols), lambda i, p: (i, 0)), ANY],
        out_specs=pl.BlockSpec((None, tr, cols), lambda i, p: (p[0], i, 0)))
    return pl.pallas_call(
        body, name=name, grid_spec=grid_spec,
        out_shape=jax.ShapeDtypeStruct((N_CHIPS, rows, cols), BF16),
        compiler_params=_params(("parallel",)),
    )(place, w, after)


def _ada_modulation(packed, w_ada, b_ada, d, w_big):
    rows_per, n = packed.shape
    d_model, wa = w_ada.shape
    big_rows, big_cols = w_big.shape
    n_chunks = big_rows // ROW_TILE
    first_chunks = (2 * n_chunks) // 3

    def body(v_ref, w_hbm, b_ref, big_hbm, all_ref, mod_ref, slots_hbm, w_vmem, part_ref, parts_ref, wide, narrow,
             load_sem, send1, recv1, send2, recv2, in_sems, out_sems):
        x, y, c = _my_place()
        me = 4 * x + 2 * y + c
        chip = _chip_of(x, y)
        load = pltpu.make_async_copy(w_hbm, w_vmem, load_sem)
        load.start()

        def chunk_in(i):
            return pltpu.make_async_copy(big_hbm.at[i * ROW_TILE:(i + 1) * ROW_TILE, :], wide.at[i % 2], in_sems.at[i % 2])

        def chunk_out(i):
            return pltpu.make_async_copy(
                narrow.at[i % 2], slots_hbm.at[chip, i * ROW_TILE:(i + 1) * ROW_TILE, :], out_sems.at[i % 2])

        def cast_chunk(i):
            if i + 1 < n_chunks:
                chunk_in(i + 1).start()
            chunk_in(i).wait()
            if i >= 2:
                chunk_out(i - 2).wait()
            narrow[i % 2] = wide[i % 2].astype(BF16)
            chunk_out(i).start()

        chunk_in(0).start()

        def rows(idx):
            return all_ref.at[pl.ds(pl.multiple_of(idx * rows_per, rows_per), rows_per), :]

        all_ref[pl.ds(pl.multiple_of(me * rows_per, rows_per), rows_per), :] = v_ref[...]
        copies = []
        for k in range(1, N_DEV):
            peer = (_flip(x, k & 4), _flip(y, k & 2), _flip(c, k & 1))
            cp = pltpu.make_async_remote_copy(
                src_ref=v_ref, dst_ref=rows(me), send_sem=send1.at[k - 1], recv_sem=recv1.at[k - 1],
                device_id=peer, device_id_type=MESH)
            cp.start()
            copies.append((cp, peer))
        for i in range(first_chunks):
            cast_chunk(i)
        for k, (cp, peer) in enumerate(copies):
            pltpu.make_async_remote_copy(
                src_ref=v_ref, dst_ref=rows(4 * peer[0] + 2 * peer[1] + peer[2]), send_sem=send1.at[k],
                recv_sem=recv1.at[k], device_id=peer, device_id_type=MESH).wait_recv()
        for cp, _ in copies:
            cp.wait_send()

        def c_of(dev):
            segments, pos = [], 0
            while pos < d:
                row, col = divmod(pos, n)
                take = min(d - pos, n - col)
                segments.append(all_ref[dev * rows_per + row:dev * rows_per + row + 1, col:col + take])
                pos += take
            return jnp.concatenate(segments, axis=1)

        c_all = jnp.concatenate([c_of(dev) for dev in range(N_DEV)], axis=0)
        load.wait()
        part_ref[...] = jnp.dot(_silu(c_all), w_vmem[...], precision=lax.Precision.HIGHEST, preferred_element_type=F32)
        parts_ref[chip] = part_ref[...]
        swaps = []
        for k, (peer, _) in enumerate(_ici_peers(x, y, c)):
            cp = pltpu.make_async_remote_copy(
                src_ref=part_ref, dst_ref=parts_ref.at[chip], send_sem=send2.at[k], recv_sem=recv2.at[k],
                device_id=peer, device_id_type=MESH)
            cp.start()
            swaps.append(cp)
        for i in range(first_chunks, n_chunks):
            cast_chunk(i)
        for i in range(n_chunks - 2, n_chunks):
            chunk_out(i).wait()
        for k, (peer, peer_chip) in enumerate(_ici_peers(x, y, c)):
            pltpu.make_async_remote_copy(
                src_ref=part_ref, dst_ref=parts_ref.at[peer_chip], send_sem=send2.at[k], recv_sem=recv2.at[k],
                device_id=peer, device_id_type=MESH).wait_recv()
        for cp in swaps:
            cp.wait_send()
        flat = jnp.concatenate([parts_ref[j, pl.ds(me, 1), :] for j in range(N_CHIPS)], axis=1) + b_ref[...]
        mod_ref[...] = jnp.concatenate([flat[:, i * d:(i + 1) * d] for i in range(3)], axis=0)

    return pl.pallas_call(
        body, name="ada_modulation",
        out_shape=(jax.ShapeDtypeStruct((N_DEV * rows_per, n), F32), jax.ShapeDtypeStruct((3, d), F32),
                   jax.ShapeDtypeStruct((N_CHIPS, big_rows, big_cols), BF16)),
        in_specs=[VMEM, ANY, VMEM, ANY], out_specs=(VMEM, VMEM, ANY),
        scratch_shapes=[pltpu.VMEM((d_model, wa), F32), pltpu.VMEM((N_DEV, wa), F32),
                        pltpu.VMEM((N_CHIPS, N_DEV, wa), F32),
                        pltpu.VMEM((2, ROW_TILE, big_cols), F32), pltpu.VMEM((2, ROW_TILE, big_cols), BF16),
                        pltpu.SemaphoreType.DMA,
                        pltpu.SemaphoreType.DMA((N_DEV - 1,)), pltpu.SemaphoreType.DMA((N_DEV - 1,)),
                        pltpu.SemaphoreType.DMA((N_CHIPS - 1,)), pltpu.SemaphoreType.DMA((N_CHIPS - 1,)),
                        pltpu.SemaphoreType.DMA((2,)), pltpu.SemaphoreType.DMA((2,))],
        compiler_params=_params(),
    )(packed, w_ada, b_ada, w_big)


def _prenorm(x, mod, g_pre, after):
    t, d = x.shape
    tb = ROW_TILE

    def body(x_ref, mod_ref, g_ref, after_ref, h_ref, ht_ref):
        del after_ref
        xv = x_ref[...]
        r = lax.rsqrt(jnp.mean(xv * xv, axis=-1, keepdims=True) + EPS)
        h = (xv * r) * g_ref[...] * (1.0 + mod_ref[1:2, :]) + mod_ref[0:1, :]
        h_ref[...] = h.astype(BF16)
        ht_ref[...] = h.T.astype(BF16)

    return pl.pallas_call(
        body, name="prenorm", grid=(t // tb,),
        out_shape=(jax.ShapeDtypeStruct((t, d), BF16), jax.ShapeDtypeStruct((d, t), BF16)),
        in_specs=[pl.BlockSpec((tb, d), lambda i: (i, 0)), pl.BlockSpec((3, d), lambda i: (0, 0)),
                  pl.BlockSpec((1, d), lambda i: (0, 0)), ANY],
        out_specs=(pl.BlockSpec((tb, d), lambda i: (i, 0)), pl.BlockSpec((d, tb), lambda i: (0, i))),
        compiler_params=_params(("parallel",)),
    )(x, mod, g_pre, after)


def _proj_tiles(proj, h, w, tiles, step, name):
    t, d = h.shape
    ws = w.shape[-1]
    tn = COL_TILE
    nt = ws // tn

    def body(tile_ref, *refs):
        del tile_ref
        a_ref, b_ref, o_ref = refs[-3:]
        o_ref[...] = jnp.dot(a_ref[...], b_ref[...].astype(BF16), preferred_element_type=F32).astype(BF16)

    if w.ndim == 3:
        w_spec = pl.BlockSpec((None, d, tn), lambda i, tl: (tl[step, i] // nt, 0, tl[step, i] % nt))
    else:
        w_spec = pl.BlockSpec((d, tn), lambda i, tl: (0, tl[step, i] % nt))
    first = proj is None
    grid_spec = pltpu.PrefetchScalarGridSpec(
        num_scalar_prefetch=1, grid=(tiles.shape[1],),
        in_specs=([] if first else [HBM]) + [pl.BlockSpec((t, d), lambda i, tl: (0, 0)), w_spec],
        out_specs=pl.BlockSpec((t, tn), lambda i, tl: (0, tl[step, i])))
    return pl.pallas_call(
        body, name=name, grid_spec=grid_spec,
        out_shape=jax.ShapeDtypeStruct((t, N_CHIPS * ws), BF16),
        input_output_aliases={} if first else {1: 0},
        compiler_params=_params(("parallel",)),
    )(*([tiles] if first else [tiles, proj]), h, w)


def _shift_rows(a, rows):
    idx = lax.broadcasted_iota(jnp.int32, a.shape, 0)
    prev = jnp.where(idx == 0, 0.0, pltpu.roll(a, 1, 0))
    nxt = jnp.where(idx == rows - 1, 0.0, pltpu.roll(a, rows - 1, 0))
    return prev, nxt


def _conv_fwd(conv_proj, conv_w, conv_b, dc):
    t = conv_proj.shape[0]
    ct = CONV_TILE
    nct = dc // ct

    def body(u_ref, cg_ref, w_ref, b_ref, co_ref):
        a = cg_ref[...].astype(F32) * u_ref[...].astype(F32)
        prev, nxt = _shift_rows(a, t)
        co_ref[...] = (w_ref[0:1, :] * prev + w_ref[1:2, :] * a + w_ref[2:3, :] * nxt + b_ref[...]).astype(BF16)

    return pl.pallas_call(
        body, name="conv_fwd", grid=(nct,),
        out_shape=jax.ShapeDtypeStruct((t, dc), BF16),
        in_specs=[pl.BlockSpec((t, ct), lambda i: (0, i)), pl.BlockSpec((t, ct), lambda i: (0, 2 * nct + i)),
                  pl.BlockSpec((3, ct), lambda i: (0, i)), pl.BlockSpec((1, ct), lambda i: (0, i))],
        out_specs=pl.BlockSpec((t, ct), lambda i: (0, i)),
        compiler_params=_params(("parallel",)),
    )(conv_proj, conv_proj, conv_w, conv_b)


def _stage_pitch(r):
    return r + STAGE_PAD if r % SUBLANES == 0 else r


def _stage_rows(t):
    return max(t // r * _stage_pitch(r) for _, r in BRANCHES)


def _gather_residues(src_ref, stage, dst_ref, r):
    t = src_ref.shape[0]
    seq, pitch = t // r, _stage_pitch(r)
    from_ref = stage
    if pitch == r and src_ref.dtype == F32:
        from_ref = src_ref
    elif pitch == r:
        stage[0:t, :] = src_ref[...].astype(F32)
    else:
        for g in range(seq):
            stage[g * pitch:g * pitch + r, :] = src_ref[g * r:(g + 1) * r, :].astype(F32)
    for res in range(r):
        dst_ref[res * seq:(res + 1) * seq, :] = from_ref[pl.ds(res, seq, stride=pitch), :].astype(dst_ref.dtype)


def _scatter_residues(src_ref, stage, dst_ref, r, add):
    t = src_ref.shape[0]
    seq, pitch = t // r, _stage_pitch(r)
    if pitch == r:
        for res in range(r):
            tok = pl.ds(res, seq, stride=r)
            val = src_ref[res * seq:(res + 1) * seq, :]
            dst_ref[tok, :] = dst_ref[tok, :] + val if add else val
        return
    for res in range(r):
        stage[pl.ds(res, seq, stride=pitch), :] = src_ref[res * seq:(res + 1) * seq, :]
    for g in range(seq):
        rows = slice(g * r, (g + 1) * r)
        val = stage[g * pitch:g * pitch + r, :]
        dst_ref[rows, :] = dst_ref[rows, :] + val if add else val


def _branch_operands(token_refs, stage, dil, r):
    if r == 1:
        return list(token_refs)
    for i, ref in enumerate(token_refs):
        _gather_residues(ref, stage, dil.at[i], r)
    return [dil.at[i] for i in range(len(token_refs))]


def _scaled_queries(q):
    return (q.astype(F32) * (HEAD_DIM ** -0.5)).astype(BF16)


BLOCK_SHIFTS = (0, -SIDE, None)


def _band_bias(rel, slope):
    arel = jnp.abs(rel)
    return jnp.where(arel <= SIDE, arel.astype(F32) * slope, NEG_INF)


def _fill_bias_tiles(bias_ref, sl_ref, r, kw):
    base = lax.broadcasted_iota(jnp.int32, (ATT_BQ, kw), 1) - lax.broadcasted_iota(jnp.int32, (ATT_BQ, kw), 0)
    for hh in range(2):
        slope = -(sl_ref[hh:hh + 1, 0:kw] * float(r))
        for e, shift in enumerate(BLOCK_SHIFTS):
            shift = ATT_BQ - kw if shift is None else shift
            bias_ref[hh, e, :, 0:kw] = _band_bias(base + shift, slope)


def _fill_stacked_bias_tiles(bias_ref, sl_ref, r, kw):
    base = lax.broadcasted_iota(jnp.int32, (kw, ATT_BQ), 0) - lax.broadcasted_iota(jnp.int32, (kw, ATT_BQ), 1)
    for hh in range(2):
        slope = -(sl_ref[hh:hh + 1, 0:ATT_BQ] * float(r))
        for e, shift in enumerate(BLOCK_SHIFTS):
            shift = ATT_BQ - kw if shift is None else shift
            bias_ref[e, 0:kw, hh * ATT_BQ:(hh + 1) * ATT_BQ] = _band_bias(base + shift, slope)


def _first_head_lanes():
    return lax.broadcasted_iota(jnp.int32, (1, PAIR), 1) < HEAD_DIM


def _only_head(x, first, hh):
    return jnp.where(first if hh == 0 else jnp.logical_not(first), x, jnp.zeros_like(x))


def _block_place(g, seq_len, kw):
    nqb = seq_len // ATT_BQ
    if nqb == 1:
        row = pl.multiple_of(g * ATT_BQ, ATT_BQ)
        return row, row, 0
    res = g // nqb
    qb = g - res * nqb
    q0 = qb * ATT_BQ
    ks = jnp.clip(q0 - SIDE, 0, seq_len - kw)
    edge = jnp.where(qb == 0, 0, jnp.where(qb == nqb - 1, 2, 1))
    return (pl.multiple_of(res * seq_len + q0, ATT_BQ), pl.multiple_of(res * seq_len + ks, SIDE), edge)


def _qkv_specs(dc, da, t, index):
    return [pl.BlockSpec((t, PAIR), functools.partial(index, (4 * dc + comp * da) // PAIR)) for comp in range(3)]


def _attn_fwd(proj, slopes, dc, da):
    t = proj.shape[0]
    hp = da // PAIR
    n_blocks = t // ATT_BQ

    def body(q_ref, k_ref, v_ref, sl_ref, o_ref, lse_ref, stage, dil, bias, o_res, l_res, o_tok, l_tok):
        for b, (_, r) in enumerate(BRANCHES):
            seq_len = t // r
            kw = min(ATT_KW, seq_len)
            ops = _branch_operands([q_ref, k_ref, v_ref], stage, dil, r)
            _fill_bias_tiles(bias, sl_ref, r, kw)
            o_dst, l_dst = (o_tok.at[b], l_tok.at[b]) if r == 1 else (o_res, l_res)
            first = _first_head_lanes()

            def blocks(trip, carry, seq_len=seq_len, kw=kw, o_dst=o_dst, l_dst=l_dst, first=first, ops=ops):
                nt = (((1,), (1,)), ((), ()))
                places = [_block_place(trip * ATT_UNROLL + i, seq_len, kw) for i in range(ATT_UNROLL)]
                chains = [(i, hh) for i in range(ATT_UNROLL) for hh in range(2)]
                qs = [_scaled_queries(ops[0][pl.ds(qrow, ATT_BQ), :]) for qrow, _, _ in places]
                ks = [ops[1][pl.ds(krow, kw), :] for _, krow, _ in places]
                vs = [ops[2][pl.ds(krow, kw), :] for _, krow, _ in places]
                ss = [lax.dot_general(_only_head(qs[i], first, hh), ks[i], nt, preferred_element_type=F32)
                      + bias[hh, places[i][2], :, 0:kw] for i, hh in chains]
                tops = [jnp.max(s, axis=-1, keepdims=True) for s in ss]
                ps = [jnp.exp(s - m) for s, m in zip(ss, tops)]
                dens = [jnp.sum(p, axis=-1, keepdims=True) for p in ps]
                for i, (qrow, _, _) in enumerate(places):
                    weights = jnp.concatenate([ps[2 * i].astype(BF16), ps[2 * i + 1].astype(BF16)], axis=1)
                    values = jnp.concatenate([_only_head(vs[i], first, 0), _only_head(vs[i], first, 1)], axis=0)
                    den = jnp.where(first, dens[2 * i], dens[2 * i + 1])
                    o_dst[pl.ds(qrow, ATT_BQ), :] = jnp.dot(weights, values, preferred_element_type=F32) / den
                    l_dst[pl.ds(qrow, ATT_BQ), :] = jnp.where(first, tops[2 * i], tops[2 * i + 1]) + jnp.log(den)
                return carry

            lax.fori_loop(0, n_blocks // ATT_UNROLL, blocks, 0)
            if r > 1:
                _scatter_residues(o_res, stage, o_tok.at[b], r, add=False)
                _scatter_residues(l_res, stage, l_tok.at[b], r, add=False)

        def merge(i, carry):
            rows = pl.ds(pl.multiple_of(i * ROW_TILE, ROW_TILE), ROW_TILE)
            la, lb, lc = l_tok[0, rows, :], l_tok[1, rows, :], l_tok[2, rows, :]
            m = jnp.maximum(jnp.maximum(la, lb), lc)
            wa, wb, wc = jnp.exp(la - m), jnp.exp(lb - m), jnp.exp(lc - m)
            den = wa + wb + wc
            o_ref[rows, :] = (wa * o_tok[0, rows, :] + wb * o_tok[1, rows, :] + wc * o_tok[2, rows, :]) * (1.0 / den)
            lse_ref[rows, :] = m + jnp.log(den)
            return carry

        lax.fori_loop(0, t // ROW_TILE, merge, 0)

    pair_spec = pl.BlockSpec((None, t, PAIR), lambda h: (h, 0, 0))
    return pl.pallas_call(
        body, name="attn_fwd", grid=(hp,),
        out_shape=(jax.ShapeDtypeStruct((hp, t, PAIR), F32), jax.ShapeDtypeStruct((hp, t, PAIR), F32)),
        in_specs=_qkv_specs(dc, da, t, lambda first, h: (0, first + h))
        + [pl.BlockSpec((None, 8, ATT_KW), lambda h: (h, 0, 0))],
        out_specs=(pair_spec, pair_spec),
        scratch_shapes=[pltpu.VMEM((_stage_rows(t), PAIR), F32), pltpu.VMEM((3, t, PAIR), BF16),
                        pltpu.VMEM((2, 3, ATT_BQ, ATT_KW), F32),
                        pltpu.VMEM((t, PAIR), F32), pltpu.VMEM((t, PAIR), F32),
                        pltpu.VMEM((3, t, PAIR), F32), pltpu.VMEM((3, t, PAIR), F32)],
        compiler_params=_params(("parallel",)),
    )(proj, proj, proj, slopes)


def _attn_bwd(dproj, proj, d_o, lse, delta, slopes, dc, da, after):
    t = proj.shape[0]
    hp = da // PAIR
    n_blocks = t // ATT_BQ

    def all_branches(q_ref, k_ref, v_ref, do_ref, lse_ref, dl_ref, sl_ref,
                     stage, dil, packed, packed_res, row_vecs, bias_t, acc, tot):
        first = _first_head_lanes()
        lane = lax.broadcasted_iota(jnp.int32, (1, PAIR), 1)
        packed[...] = jnp.where((lane & (HEAD_DIM - 1)) < HEAD_DIM // 2, lse_ref[...], dl_ref[...])
        for b, (_, r) in enumerate(BRANCHES):
            seq_len = t // r
            kw = min(ATT_KW, seq_len)
            ops = _branch_operands([q_ref, k_ref, v_ref, do_ref], stage, dil, r)
            scalars = packed
            if r > 1:
                _gather_residues(packed, stage, packed_res, r)
                scalars = packed_res
            for g in range(n_blocks):
                flipped = scalars[g * ATT_BQ:(g + 1) * ATT_BQ, :].T
                for row in range(4):
                    row_vecs[g, row:row + 1, :] = flipped[row * (HEAD_DIM // 2):row * (HEAD_DIM // 2) + 1, :]
            _fill_stacked_bias_tiles(bias_t, sl_ref, r, kw)
            dst = tot if r == 1 else acc
            dst[1] = jnp.zeros((t, PAIR), F32)
            dst[2] = jnp.zeros((t, PAIR), F32)

            def blocks(trip, carry, seq_len=seq_len, kw=kw, ops=ops, dst=dst):
                nt = (((1,), (1,)), ((), ()))
                group = range(ATT_UNROLL)
                places = [_block_place(trip * ATT_UNROLL + i, seq_len, kw) for i in group]
                ks, vs, q2s, do2s, lse2s, dl2s = [], [], [], [], [], []
                for i, (qrow, krow, _) in zip(group, places):
                    q = _scaled_queries(ops[0][pl.ds(qrow, ATT_BQ), :])
                    dov = ops[3][pl.ds(qrow, ATT_BQ), :]
                    ks.append(ops[1][pl.ds(krow, kw), :])
                    vs.append(ops[2][pl.ds(krow, kw), :])
                    q2s.append(jnp.concatenate([_only_head(q, first, 0), _only_head(q, first, 1)], axis=0))
                    do2s.append(jnp.concatenate([_only_head(dov, first, 0), _only_head(dov, first, 1)], axis=0))
                    rows = row_vecs[trip * ATT_UNROLL + i]
                    lse2s.append(jnp.concatenate([rows[0:1, :], rows[2:3, :]], axis=1))
                    dl2s.append(jnp.concatenate([rows[1:2, :], rows[3:4, :]], axis=1))
                s_ts = [lax.dot_general(ks[i], q2s[i], nt, preferred_element_type=F32) for i in group]
                dp_ts = [lax.dot_general(vs[i], do2s[i], nt, preferred_element_type=F32) for i in group]
                p_ts = [jnp.exp(s_ts[i] + bias_t[places[i][2], 0:kw, :] - lse2s[i]) for i in group]
                ds_ts = [p_ts[i] * (dp_ts[i] - dl2s[i]) for i in group]
                dvs = [jnp.dot(p_ts[i].astype(BF16), do2s[i], preferred_element_type=F32) for i in group]
                dks = [jnp.dot(ds_ts[i].astype(BF16), q2s[i], preferred_element_type=F32) for i in group]
                dss = [ds_ts[i].T.astype(BF16) for i in group]
                dqs = [jnp.dot(dss[i][0:ATT_BQ, :], _only_head(ks[i], first, 0), preferred_element_type=F32)
                       + jnp.dot(dss[i][ATT_BQ:2 * ATT_BQ, :], _only_head(ks[i], first, 1), preferred_element_type=F32)
                       for i in group]
                for i, (qrow, krow, _) in zip(group, places):
                    dst[0, pl.ds(qrow, ATT_BQ), :] = dqs[i] * (HEAD_DIM ** -0.5)
                    dst[1, pl.ds(krow, kw), :] += dks[i]
                    dst[2, pl.ds(krow, kw), :] += dvs[i]
                return carry

            lax.fori_loop(0, n_blocks // ATT_UNROLL, blocks, 0)
            if r > 1:
                for comp in range(3):
                    _scatter_residues(acc.at[comp], stage, tot.at[comp], r, add=True)

    first_q = (4 * dc) // PAIR

    def body(dproj_in, q_ref, k_ref, v_ref, do_ref, lse_ref, dl_ref, sl_ref, after_ref, out_ref, *scratch):
        del dproj_in, after_ref
        work, out_stage, out_sems = scratch[:-2], scratch[-2], scratch[-1]
        h = pl.program_id(0)
        all_branches(q_ref, k_ref, v_ref, do_ref, lse_ref, dl_ref, sl_ref, *work)

        def out_copy(comp):
            cols = pl.ds(pl.multiple_of((first_q + comp * hp + h) * PAIR, PAIR), PAIR)
            return pltpu.make_async_copy(out_stage.at[comp], out_ref.at[:, cols], out_sems.at[comp])

        @pl.when(h > 0)
        def _():
            for comp in range(3):
                out_copy(comp).wait()

        for comp in range(3):
            out_stage[comp] = work[-1][comp].astype(BF16)
            out_copy(comp).start()

        @pl.when(h == hp - 1)
        def _():
            for comp in range(3):
                out_copy(comp).wait()

    pair_spec = pl.BlockSpec((None, t, PAIR), lambda h: (h, 0, 0))
    return pl.pallas_call(
        body, name="attn_bwd", grid=(hp,),
        out_shape=jax.ShapeDtypeStruct(dproj.shape, BF16),
        in_specs=[HBM] + _qkv_specs(dc, da, t, lambda first, h: (0, first + h))
        + [pair_spec, pair_spec, pair_spec, pl.BlockSpec((None, 8, ATT_KW), lambda h: (h, 0, 0)), ANY],
        out_specs=ANY,
        input_output_aliases={0: 0},
        scratch_shapes=[pltpu.VMEM((_stage_rows(t), PAIR), F32), pltpu.VMEM((4, t, PAIR), BF16),
                        pltpu.VMEM((t, PAIR), F32), pltpu.VMEM((t, PAIR), F32),
                        pltpu.VMEM((n_blocks, 8, ATT_BQ), F32), pltpu.VMEM((3, ATT_KW, 2 * ATT_BQ), F32),
                        pltpu.VMEM((3, t, PAIR), F32), pltpu.VMEM((3, t, PAIR), F32),
                        pltpu.VMEM((3, t, PAIR), BF16), pltpu.SemaphoreType.DMA((3,))],
        compiler_params=_params(("arbitrary",)),
    )(dproj, proj, proj, proj, d_o, lse, delta, slopes, after)


def _mix_fwd(co, proj, o_mix, g_conv, g_attn_pairs, after):
    t, dc = co.shape
    hp = o_mix.shape[0]
    da = hp * PAIR
    tb = ROW_TILE

    def body(co_ref, bg_ref, zc_ref, za_ref, om_ref, gc_ref, ga_ref, after_ref, ycat_ref, ycatt_ref):
        del after_ref
        p = bg_ref[...].astype(F32) * co_ref[...].astype(F32)
        rc = lax.rsqrt(jnp.mean(p * p, axis=-1, keepdims=True) + EPS)
        yc = (p * rc) * gc_ref[...] * _silu(zc_ref[...].astype(F32))
        ycat_ref[:, 0:dc] = yc.astype(BF16)
        ycatt_ref[0:dc, :] = yc.T.astype(BF16)
        ssq = jnp.zeros((tb, 1), F32)
        for h in range(hp):
            o = om_ref[h]
            ssq = ssq + jnp.sum(o * o, axis=-1, keepdims=True)
        ra = lax.rsqrt(ssq * (1.0 / da) + EPS)
        for h in range(hp):
            ya = (om_ref[h] * ra) * ga_ref[h] * _silu(za_ref[:, h * PAIR:(h + 1) * PAIR].astype(F32))
            ycat_ref[:, dc + h * PAIR:dc + (h + 1) * PAIR] = ya.astype(BF16)
            ycatt_ref[dc + h * PAIR:dc + (h + 1) * PAIR, :] = ya.T.astype(BF16)

    pair_spec = pl.BlockSpec((hp, tb, PAIR), lambda i: (0, i, 0))
    return pl.pallas_call(
        body, name="mix_fwd", grid=(t // tb,),
        out_shape=(jax.ShapeDtypeStruct((t, dc + da), BF16), jax.ShapeDtypeStruct((dc + da, t), BF16)),
        in_specs=[pl.BlockSpec((tb, dc), lambda i: (i, 0)),
                  pl.BlockSpec((tb, dc), lambda i: (i, 1)),
                  pl.BlockSpec((tb, dc), lambda i: (i, 3)),
                  pl.BlockSpec((tb, da), lambda i: (i, 7)),
                  pair_spec,
                  pl.BlockSpec((1, dc), lambda i: (0, 0)),
                  pl.BlockSpec((hp, 1, PAIR), lambda i: (0, 0, 0)), ANY],
        out_specs=(pl.BlockSpec((tb, dc + da), lambda i: (i, 0)), pl.BlockSpec((dc + da, tb), lambda i: (0, i))),
        compiler_params=_params(("parallel",)),
    )(co, proj, proj, proj, o_mix, g_conv, g_attn_pairs, after)


def _out_fwd_bwd(ycat, woutf, x, target, mod, g_post):
    t, d = x.shape
    n = ycat.shape[1]
    tb = ROW_TILE

    def body(a_ref, w_ref, x_ref, tg_ref, mod_ref, g_ref, dout_ref, dy_ref, acc_ref):
        y = jnp.dot(a_ref[...], w_ref[...], preferred_element_type=F32)
        r = lax.rsqrt(jnp.mean(y * y, axis=-1, keepdims=True) + EPS)
        nh = y * r
        gate = mod_ref[2:3, :]
        nrm = nh * g_ref[...]
        err = x_ref[...] + gate * nrm - tg_ref[...]
        dout = err * (1.0 / d)
        dout_ref[...] = dout.astype(BF16)
        dn = dout * gate
        a = dn * g_ref[...]
        dy = r * (a - nh * jnp.mean(a * nh, axis=-1, keepdims=True))
        dy_ref[...] = dy.astype(BF16)
        loss = 0.5 * jnp.sum(jnp.sum(err * err, axis=-1, keepdims=True) * (1.0 / d), axis=0, keepdims=True)
        part = jnp.concatenate(
            [jnp.sum(dout * nrm, axis=0, keepdims=True), jnp.sum(dn * nh, axis=0, keepdims=True),
             jnp.broadcast_to(loss, (1, d)), jnp.zeros((5, d), F32)], axis=0)

        @pl.when(pl.program_id(0) == 0)
        def _():
            acc_ref[...] = jnp.zeros(acc_ref.shape, F32)

        acc_ref[...] += part

    return pl.pallas_call(
        body, name="out_fwd_bwd", grid=(t // tb,),
        out_shape=(jax.ShapeDtypeStruct((t, d), BF16), jax.ShapeDtypeStruct((t, d), BF16),
                   jax.ShapeDtypeStruct((8, d), F32)),
        in_specs=[pl.BlockSpec((tb, n), lambda i: (i, 0)), pl.BlockSpec((n, d), lambda i: (0, 0)),
                  pl.BlockSpec((tb, d), lambda i: (i, 0)), pl.BlockSpec((tb, d), lambda i: (i, 0)),
                  pl.BlockSpec((3, d), lambda i: (0, 0)), pl.BlockSpec((1, d), lambda i: (0, 0))],
        out_specs=(pl.BlockSpec((tb, d), lambda i: (i, 0)), pl.BlockSpec((tb, d), lambda i: (i, 0)),
                   pl.BlockSpec((8, d), lambda i: (0, 0))),
        compiler_params=_params(("arbitrary",)),
    )(ycat, woutf, x, target, mod, g_post)


def _matmul_nt(a, b, out_dtype, name):
    m, k = a.shape
    n = b.shape[0]
    tn = COL_TILE

    def body(a_ref, b_ref, o_ref):
        o_ref[...] = lax.dot_general(a_ref[...], b_ref[...], (((1,), (1,)), ((), ())),
                                     preferred_element_type=F32).astype(out_dtype)

    return pl.pallas_call(
        body, name=name, grid=(n // tn,),
        out_shape=jax.ShapeDtypeStruct((m, n), out_dtype),
        in_specs=[pl.BlockSpec((m, k), lambda i: (0, 0)), pl.BlockSpec((tn, k), lambda i: (i, 0))],
        out_specs=pl.BlockSpec((m, tn), lambda i: (0, i)),
        compiler_params=_params(("parallel",)),
    )(a, b)


def _mix_bwd(dycat, co, proj, o_mix, g_conv, g_attn_pairs):
    t, dc = co.shape
    hp = o_mix.shape[0]
    da = hp * PAIR
    tb = ROW_TILE

    def body(dy_ref, co_ref, bg_ref, zc_ref, za_ref, om_ref, gc_ref, ga_ref,
             dcp_ref, dco_ref, do_ref, dl_ref, dgc_ref, dga_ref):
        first = pl.program_id(0) == 0
        cov = co_ref[...].astype(F32)
        bg = bg_ref[...].astype(F32)
        zc = zc_ref[...].astype(F32)
        p = bg * cov
        rc = lax.rsqrt(jnp.mean(p * p, axis=-1, keepdims=True) + EPS)
        nh = p * rc
        dyc = dy_ref[:, 0:dc].astype(F32)
        dn = dyc * _silu(zc)
        a = dn * gc_ref[...]
        dp = rc * (a - nh * jnp.mean(a * nh, axis=-1, keepdims=True))
        dcp_ref[:, 0:dc] = jnp.zeros((tb, dc), BF16)
        dcp_ref[:, dc:2 * dc] = (dp * cov).astype(BF16)
        dcp_ref[:, 2 * dc:3 * dc] = jnp.zeros((tb, dc), BF16)
        dcp_ref[:, 3 * dc:4 * dc] = (dyc * nh * gc_ref[...] * _silu_grad(zc)).astype(BF16)
        dcp_ref[:, 4 * dc:4 * dc + 3 * da] = jnp.zeros((tb, 3 * da), BF16)
        dco_ref[...] = dp * bg

        @pl.when(first)
        def _():
            dgc_ref[...] = jnp.zeros(dgc_ref.shape, F32)
            dga_ref[...] = jnp.zeros(dga_ref.shape, F32)

        dgc_ref[...] += jnp.sum(dn * nh, axis=0, keepdims=True)

        ssq = jnp.zeros((tb, 1), F32)
        for h in range(hp):
            o = om_ref[h]
            ssq = ssq + jnp.sum(o * o, axis=-1, keepdims=True)
        ra = lax.rsqrt(ssq * (1.0 / da) + EPS)
        dot_an = jnp.zeros((tb, 1), F32)
        for h in range(hp):
            nha = om_ref[h] * ra
            za = za_ref[:, h * PAIR:(h + 1) * PAIR].astype(F32)
            dya = dy_ref[:, dc + h * PAIR:dc + (h + 1) * PAIR].astype(F32)
            dna = dya * _silu(za)
            dza = (dya * nha * ga_ref[h] * _silu_grad(za)).astype(BF16)
            dcp_ref[:, 4 * dc + 3 * da + h * PAIR:4 * dc + 3 * da + (h + 1) * PAIR] = dza
            dga_ref[h] += jnp.sum(dna * nha, axis=0, keepdims=True)
            dot_an = dot_an + jnp.sum(dna * ga_ref[h] * nha, axis=-1, keepdims=True)
        mean_an = dot_an * (1.0 / da)
        first_head = lax.broadcasted_iota(jnp.int32, (tb, PAIR), 1) < HEAD_DIM
        for h in range(hp):
            o = om_ref[h]
            nha = o * ra
            za = za_ref[:, h * PAIR:(h + 1) * PAIR].astype(F32)
            dya = dy_ref[:, dc + h * PAIR:dc + (h + 1) * PAIR].astype(F32)
            aa = dya * _silu(za) * ga_ref[h]
            d_o = ra * (aa - nha * mean_an)
            do_ref[h] = d_o.astype(BF16)
            prod = d_o * o
            both = jnp.sum(prod, axis=-1, keepdims=True)
            head0 = jnp.sum(jnp.where(first_head, prod, 0.0), axis=-1, keepdims=True)
            dl_ref[h] = jnp.where(first_head, head0, both - head0)

    pair_spec = pl.BlockSpec((hp, tb, PAIR), lambda i: (0, i, 0))
    return pl.pallas_call(
        body, name="mix_bwd", grid=(t // tb,),
        out_shape=(jax.ShapeDtypeStruct((t, 4 * dc + 4 * da), BF16), jax.ShapeDtypeStruct((t, dc), F32),
                   jax.ShapeDtypeStruct((hp, t, PAIR), BF16), jax.ShapeDtypeStruct((hp, t, PAIR), F32),
                   jax.ShapeDtypeStruct((1, dc), F32), jax.ShapeDtypeStruct((hp, 1, PAIR), F32)),
        in_specs=[pl.BlockSpec((tb, dc + da), lambda i: (i, 0)),
                  pl.BlockSpec((tb, dc), lambda i: (i, 0)),
                  pl.BlockSpec((tb, dc), lambda i: (i, 1)),
                  pl.BlockSpec((tb, dc), lambda i: (i, 3)),
                  pl.BlockSpec((tb, da), lambda i: (i, 7)),
                  pair_spec,
                  pl.BlockSpec((1, dc), lambda i: (0, 0)),
                  pl.BlockSpec((hp, 1, PAIR), lambda i: (0, 0, 0))],
        out_specs=(pl.BlockSpec((tb, 4 * dc + 4 * da), lambda i: (i, 0)), pl.BlockSpec((tb, dc), lambda i: (i, 0)),
                   pair_spec, pair_spec,
                   pl.BlockSpec((1, dc), lambda i: (0, 0)), pl.BlockSpec((hp, 1, PAIR), lambda i: (0, 0, 0))),
        compiler_params=_params(("arbitrary",)),
    )(dycat, co, proj, proj, proj, o_mix, g_conv, g_attn_pairs)


def _conv_bwd(dconv_proj, dco, conv_proj, conv_w, dc, after):
    t = dco.shape[0]
    ct = CONV_TILE
    nct = dc // ct

    def body(dcp_in_ref, dco_ref, u_ref, cg_ref, w_ref, after_ref, dcp_ref, acc_ref):
        del dcp_in_ref, after_ref
        which = pl.program_id(1)
        g = dco_ref[...]
        u = u_ref[...].astype(F32)
        cg = cg_ref[...].astype(F32)
        g_prev, g_next = _shift_rows(g, t)
        da = w_ref[0:1, :] * g_next + w_ref[1:2, :] * g + w_ref[2:3, :] * g_prev
        dcp_ref[...] = (da * jnp.where(which == 0, cg, u)).astype(BF16)
        a = cg * u
        a_prev, a_next = _shift_rows(a, t)
        acc_ref[...] = jnp.concatenate(
            [jnp.sum(g * a_prev, axis=0, keepdims=True), jnp.sum(g * a, axis=0, keepdims=True),
             jnp.sum(g * a_next, axis=0, keepdims=True), jnp.sum(g, axis=0, keepdims=True),
             jnp.zeros((4, ct), F32)], axis=0)

    return pl.pallas_call(
        body, name="conv_bwd", grid=(nct, 2),
        out_shape=(jax.ShapeDtypeStruct(dconv_proj.shape, BF16), jax.ShapeDtypeStruct((8, dc), F32)),
        in_specs=[HBM,
                  pl.BlockSpec((t, ct), lambda i, s: (0, i)),
                  pl.BlockSpec((t, ct), lambda i, s: (0, i)),
                  pl.BlockSpec((t, ct), lambda i, s: (0, 2 * nct + i)),
                  pl.BlockSpec((3, ct), lambda i, s: (0, i)), ANY],
        out_specs=(pl.BlockSpec((t, ct), lambda i, s: (0, 2 * s * nct + i)),
                   pl.BlockSpec((8, ct), lambda i, s: (0, i))),
        input_output_aliases={0: 0},
        compiler_params=_params(("arbitrary", "arbitrary")),
    )(dconv_proj, dco, conv_proj, conv_proj, conv_w, after)


def _dh(dproj, winf, after):
    t = dproj.shape[0]
    _, d, ws = winf.shape
    tm = tn = COL_TILE
    nt = (((1,), (1,)), ((), ()))

    def body(a_ref, w_ref, after_ref, o_ref):
        del after_ref
        acc = lax.dot_general(a_ref[:, 0:ws], w_ref[0], nt, preferred_element_type=F32)
        for j in range(1, N_CHIPS):
            acc = acc + lax.dot_general(a_ref[:, j * ws:(j + 1) * ws], w_ref[j], nt, preferred_element_type=F32)
        o_ref[...] = acc.astype(BF16)

    return pl.pallas_call(
        body, name="dh", grid=(d // tn, t // tm),
        out_shape=jax.ShapeDtypeStruct((t, d), BF16),
        in_specs=[pl.BlockSpec((tm, N_CHIPS * ws), lambda n, m: (m, 0)),
                  pl.BlockSpec((N_CHIPS, tn, ws), lambda n, m: (0, n, 0)), ANY],
        out_specs=pl.BlockSpec((tm, tn), lambda n, m: (m, n)),
        compiler_params=_params(("parallel", "parallel")),
    )(dproj, winf, after)


def _prenorm_bwd(x, dh, dout, mod, g_pre):
    t, d = x.shape
    tb = ROW_TILE

    def body(x_ref, dh_ref, dout_ref, mod_ref, g_ref, gx_ref, acc_ref):
        xv = x_ref[...]
        dhv = dh_ref[...].astype(F32)
        r = lax.rsqrt(jnp.mean(xv * xv, axis=-1, keepdims=True) + EPS)
        xh = xv * r
        one_scale = 1.0 + mod_ref[1:2, :]
        a = dhv * one_scale * g_ref[...]
        gx_ref[...] = dout_ref[...].astype(F32) + r * (a - xh * jnp.mean(a * xh, axis=-1, keepdims=True))
        part = jnp.concatenate(
            [jnp.sum(dhv, axis=0, keepdims=True), jnp.sum(dhv * xh * g_ref[...], axis=0, keepdims=True),
             jnp.sum(dhv * xh * one_scale, axis=0, keepdims=True), jnp.zeros((5, d), F32)], axis=0)

        @pl.when(pl.program_id(0) == 0)
        def _():
            acc_ref[...] = jnp.zeros(acc_ref.shape, F32)

        acc_ref[...] += part

    return pl.pallas_call(
        body, name="prenorm_bwd", grid=(t // tb,),
        out_shape=(jax.ShapeDtypeStruct((t, d), F32), jax.ShapeDtypeStruct((8, d), F32)),
        in_specs=[pl.BlockSpec((tb, d), lambda i: (i, 0)), pl.BlockSpec((tb, d), lambda i: (i, 0)),
                  pl.BlockSpec((tb, d), lambda i: (i, 0)), pl.BlockSpec((3, d), lambda i: (0, 0)),
                  pl.BlockSpec((1, d), lambda i: (0, 0))],
        out_specs=(pl.BlockSpec((tb, d), lambda i: (i, 0)), pl.BlockSpec((8, d), lambda i: (0, 0))),
        compiler_params=_params(("arbitrary",)),
    )(x, dh, dout, mod, g_pre)


def _chip_sums(mine, rsib, name, part=0, parts=1, after=()):
    _, half, cols = mine.shape
    rows = half // parts
    tr = min(rows, ROW_TILE)
    nt = rows // tr

    def body(g_ref, r_ref, *rest):
        rest[-1][...] = (g_ref[...].astype(F32) + r_ref[...].astype(F32)).astype(BF16)

    spec = pl.BlockSpec((None, tr, cols), lambda j, i: (j, part * nt + i, 0))
    return pl.pallas_call(
        body, name=name, grid=(N_CHIPS, nt),
        out_shape=jax.ShapeDtypeStruct((N_CHIPS, rows, cols), BF16),
        in_specs=[spec, spec] + [ANY] * len(after), out_specs=pl.BlockSpec((None, tr, cols), lambda j, i: (j, i, 0)),
        compiler_params=_params(("parallel", "parallel")),
    )(mine, rsib, *after)


def _owner_sum(place, mine, rsib, rici, name, part=0, parts=1):
    _, half, cols = mine.shape
    rows = half // parts
    tr = min(rows, ROW_TILE)
    nt = rows // tr

    def body(place_ref, g_ref, r_ref, i_ref, o_ref):
        del place_ref
        acc = g_ref[...].astype(F32) + r_ref[...].astype(F32)
        for k in range(N_CHIPS - 1):
            acc = acc + i_ref[k].astype(F32)
        o_ref[...] = acc

    own = pl.BlockSpec((None, tr, cols), lambda i, p: (p[0], part * nt + i, 0))
    grid_spec = pltpu.PrefetchScalarGridSpec(
        num_scalar_prefetch=1, grid=(nt,),
        in_specs=[own, own, pl.BlockSpec((N_CHIPS - 1, tr, cols), lambda i, p: (0, i, 0))],
        out_specs=pl.BlockSpec((tr, cols), lambda i, p: (p[1] * (half // tr) + part * nt + i, 0)))
    return pl.pallas_call(
        body, name=name, grid_spec=grid_spec,
        out_shape=jax.ShapeDtypeStruct((2 * half, cols), F32),
        compiler_params=_params(("parallel",)),
    )(place, mine, rsib, rici)


def _adam_math(w, g, m, v):
    m2 = ADAM_B1 * m + (1.0 - ADAM_B1) * g
    v2 = ADAM_B2 * v + (1.0 - ADAM_B2) * (g * g)
    m_hat = m2 / (1.0 - ADAM_B1 ** ADAM_STEP)
    v_hat = v2 / (1.0 - ADAM_B2 ** ADAM_STEP)
    delta = -ADAM_LR * (m_hat / (jnp.sqrt(v_hat) + ADAM_EPS) + ADAM_WD * w)
    return delta, m2, v2


def _adamw(w, g, m, v, name, part=0, parts=1, prev=None):
    rows, cols = w.shape
    tr = min(rows, ROW_TILE)

    def body(*refs):
        w_ref, g_ref, m_ref, v_ref, go_ref, d_ref, m2_ref, v2_ref = refs[-8:]
        g = g_ref[...]
        go_ref[...] = g
        d_ref[...], m2_ref[...], v2_ref[...] = _adam_math(w_ref[...], g, m_ref[...], v_ref[...])

    if parts == 1:
        grid, spec = (rows // tr,), pl.BlockSpec((tr, cols), lambda i: (i, 0))
    else:
        per_half = rows // 2 // tr
        nt = per_half // parts
        grid, spec = (2, nt), pl.BlockSpec((tr, cols), lambda r, i: (r * per_half + part * nt + i, 0))
    olds = [] if prev is None else list(prev)
    return pl.pallas_call(
        body, name=name, grid=grid,
        out_shape=(jax.ShapeDtypeStruct(w.shape, F32),) * 4,
        in_specs=[HBM] * len(olds) + [spec] * 4, out_specs=(spec,) * 4,
        input_output_aliases={i: i for i in range(len(olds))},
        compiler_params=_params(("parallel",) * len(grid)),
    )(*olds, w, g, m, v)


def _ada_grad_adamw(c_all, dmod_cols, w, m, v):
    d, wa = w.shape
    tr = ROW_TILE

    def body(c_ref, dm_ref, w_ref, m_ref, v_ref, g_ref, d_ref, m2_ref, v2_ref):
        act = _silu(c_ref[...]).T
        g = act[:, 0:1] * dm_ref[0:1, :]
        for b in range(1, N_DEV):
            g = g + act[:, b:b + 1] * dm_ref[b:b + 1, :]
        g_ref[...] = g
        d_ref[...], m2_ref[...], v2_ref[...] = _adam_math(w_ref[...], g, m_ref[...], v_ref[...])

    spec = pl.BlockSpec((tr, wa), lambda i: (i, 0))
    return pl.pallas_call(
        body, name="ada_grad_adamw", grid=(d // tr,),
        out_shape=(jax.ShapeDtypeStruct(w.shape, F32),) * 4,
        in_specs=[pl.BlockSpec((N_DEV, tr), lambda i: (0, i)), pl.BlockSpec((N_DEV, wa), lambda i: (0, 0)),
                  spec, spec, spec],
        out_specs=(spec,) * 4,
        compiler_params=_params(("parallel",)),
    )(c_all, dmod_cols, w, m, v)


def _small_update(place, gathered, pieces, weights, moments_m, moments_v):
    n = gathered.shape[1]
    k = len(weights)
    final_shapes = [w.shape for w in weights]
    row_counts = [s[1] if len(s) == 3 else 1 for s in final_shapes]
    weights, moments_m, moments_v = ([a.reshape(1, -1) for a in arrays] for arrays in (weights, moments_m, moments_v))

    def body(place_ref, g_ref, *refs):
        w_refs, m_refs, v_refs = refs[0:k], refs[k:2 * k], refs[2 * k:3 * k]
        outs = refs[3 * k:]
        total = g_ref[0:SUBLANES, :]
        for dev in range(1, N_DEV):
            total = total + g_ref[SUBLANES * dev:SUBLANES * (dev + 1), :]

        def flat(offset, length):
            segments, pos = [], offset
            while pos < offset + length:
                row, col = divmod(pos, n)
                take = min(offset + length - pos, n - col)
                segments.append(total[row:row + 1, col:col + take])
                pos += take
            return jnp.concatenate(segments, axis=1) if len(segments) > 1 else segments[0]

        chip = place_ref[0]
        for i, (w_ref, m_ref, v_ref) in enumerate(zip(w_refs, m_refs, v_refs)):
            g = flat(*pieces[i])
            if pieces[i][1] > w_ref.shape[1]:
                rows = row_counts[i]
                cols, full = w_ref.shape[1] // rows, pieces[i][1] // rows
                picked = []
                for r in range(rows):
                    blocks = [g[:, r * full + q * cols:r * full + (q + 1) * cols] for q in range(N_CHIPS)]
                    mine = blocks[N_CHIPS - 1]
                    for q in range(N_CHIPS - 2, -1, -1):
                        mine = jnp.where(chip == q, blocks[q], mine)
                    picked.append(mine)
                g = jnp.concatenate(picked, axis=1)
            delta, m2, v2 = _adam_math(w_ref[...], g, m_ref[...], v_ref[...])
            for j, val in enumerate((g, delta, m2, v2)):
                outs[j * k + i][...] = val
        outs[4 * k][...] = flat(*pieces[k])

    shapes = [jax.ShapeDtypeStruct(w.shape, F32) for w in weights]
    grid_spec = pltpu.PrefetchScalarGridSpec(
        num_scalar_prefetch=1, grid=(1,),
        in_specs=[pl.BlockSpec(gathered.shape, lambda i, p: (0, 0))]
        + [pl.BlockSpec(a.shape, functools.partial(lambda nd, i, p: (0,) * nd, a.ndim))
           for a in (*weights, *moments_m, *moments_v)],
        out_specs=tuple(pl.BlockSpec(s.shape, functools.partial(lambda nd, i, p: (0,) * nd, len(s.shape)))
                        for s in shapes * 4) + (pl.BlockSpec((1, LANES), lambda i, p: (0, 0)),))
    outs = pl.pallas_call(
        body, name="small_update", grid_spec=grid_spec,
        out_shape=tuple(shapes * 4) + (jax.ShapeDtypeStruct((1, LANES), F32),),
        compiler_params=_params(("arbitrary",)),
    )(place, gathered, *weights, *moments_m, *moments_v)
    shaped = [out.reshape(final_shapes[i % k]) for i, out in enumerate(outs[0:4 * k])]
    return shaped[0:k], shaped[k:2 * k], shaped[2 * k:3 * k], shaped[3 * k:4 * k], outs[4 * k]


def _pack_small(pieces):
    flat = [p.reshape(-1).astype(F32) for p in pieces]
    offsets, total = [], 0
    for p in flat:
        offsets.append(total)
        total += p.shape[0]
    padded = -(-total // SMALL_ALIGN) * SMALL_ALIGN
    if padded > total:
        flat.append(jnp.zeros((padded - total,), F32))
    return jnp.concatenate(flat).reshape(8, padded // 8), offsets


def _alibi_slope_rows(n_heads):
    slopes = 2.0 ** (-8.0 * np.arange(1, n_heads + 1, dtype=np.float64) / n_heads)
    rows = np.zeros((n_heads // 2, SUBLANES), np.float32)
    rows[:, 0:2] = slopes.reshape(n_heads // 2, 2)
    return jnp.asarray(np.broadcast_to(rows[:, :, None], (n_heads // 2, SUBLANES, ATT_KW)))


def kernel(x, c, w_ada, b_ada, g_pre, w_in, conv_w, conv_b, g_conv, g_attn, w_out, g_post, loss_target, m_w_ada, m_b_ada, m_g_pre, m_w_in, m_conv_w, m_conv_b, m_g_conv, m_g_attn, m_w_out, m_g_post, v_w_ada, v_b_ada, v_g_pre, v_w_in, v_conv_w, v_conv_b, v_g_conv, v_g_attn, v_w_out, v_g_post):
    t, d = x.shape[1], x.shape[2]
    dc = conv_b.shape[1]
    da = g_attn.shape[1]
    hp = da // PAIR
    ws = w_in.shape[2]
    wa = w_ada.shape[2]
    cws = conv_w.shape[2]
    assert t % ROW_TILE == 0 and d % ROW_TILE == 0 and dc % COL_TILE == 0 and da % COL_TILE == 0
    assert ws == 2 * dc and dc == da and t // BRANCHES[-1][1] >= ATT_BQ

    mx, my, mc = _my_place()
    chip = _chip_of(mx, my)
    dev = 2 * chip + mc
    place = jnp.stack([chip, mc]).astype(jnp.int32)

    x2, tgt2 = x[0], loss_target[0]
    w_ada2, w_in2, w_out2 = w_ada[0], w_in[0], w_out[0]

    packed, offs = _pack_small([c[0], conv_w[0]])
    seen, mod, win_slots = _ada_modulation(packed, w_ada2, b_ada, d, w_in2)
    seen = seen.reshape(N_DEV, -1)
    c_all = seen[:, offs[0]:offs[0] + d]
    conv_w_full = seen[0::2, offs[1]:offs[1] + 3 * cws].reshape(N_CHIPS, 3, cws).transpose(1, 0, 2).reshape(3, dc)

    win_flight, send_in, recv_in, started = _gather_start(win_slots, mod)

    y_chip, x_chip, d_chip = (_chip_of(mx, 1 - my), _chip_of(1 - mx, my), _chip_of(1 - mx, 1 - my))
    tiles_per_part = ws // COL_TILE // 2

    def tiles_of(chunk, parts):
        return [(2 * chunk + part) * tiles_per_part + k for part in parts for k in range(tiles_per_part)]

    tiles = jnp.stack([jnp.stack(step) for step in (
        tiles_of(chip, (0, 1)), tiles_of(y_chip, (0,)) + tiles_of(x_chip, (1,)),
        tiles_of(y_chip, (1,)) + tiles_of(x_chip, (0,)), tiles_of(d_chip, (0, 1)))]).astype(jnp.int32)
    h, ht = _prenorm(x2, mod, g_pre, started)
    proj = _proj_tiles(None, h, w_in2, tiles, 0, "proj_own")
    win_flight, wout_flight, relay_send_in, relay_recv_in, send_out, recv_out = _gather_relay_in(
        win_flight, _cast_into_slot(place, w_out2, "cast_w_out", proj), recv_in, proj)
    win_flight = _forward_halves(win_flight, ((0, 0), (1, 1)), "forward_w_in_first")
    proj = _proj_tiles(proj, h, win_flight, tiles, 1, "proj_first_parts")
    win_flight = _forward_halves(
        _gather_wait_direct(win_flight, send_in, recv_in, proj, "gather_wait_w_in_direct"),
        ((0, 1), (1, 0)), "forward_w_in_second")
    proj = _proj_tiles(proj, h, win_flight, tiles, 2, "proj_second_parts")
    winf = _forward_halves(
        _gather_wait_relayed(win_flight, relay_send_in, relay_recv_in, proj, "gather_wait_w_in_relayed"),
        ((2, None),), "forward_w_in_relayed")
    proj = _proj_tiles(proj, h, winf, tiles, 3, "proj_diagonal")
    slopes = _alibi_slope_rows(da // HEAD_DIM)
    co = _conv_fwd(proj, conv_w_full, conv_b, dc)
    wout_flight, relay_send_out, relay_recv_out = _gather_relay_out(wout_flight, recv_out, co)
    o_mix, lse = _attn_fwd(proj, slopes, dc, da)
    g_attn_pairs = g_attn.reshape(hp, 1, PAIR)
    wout_flight = _gather_wait_direct(wout_flight, send_out, recv_out, o_mix, "gather_wait_w_out_direct")
    wout_flight = _gather_wait_relayed(wout_flight, relay_send_out, relay_recv_out, o_mix, "gather_wait_w_out_relayed")
    all_halves = ((0, None), (1, None), (2, None))
    wout_flight, fsend_out, frecv_out, forwarding = _forward_start(wout_flight, all_halves, "forward_w_out_start")
    ycat, ycat_t = _mix_fwd(co, proj, o_mix, g_conv, g_attn_pairs, forwarding)
    woutf = _forward_wait(wout_flight, all_halves, fsend_out, frecv_out, ycat, "forward_w_out_wait").reshape(dc + da, d)
    dout, dy, post_sums = _out_fwd_bwd(ycat, woutf, x2, tgt2, mod, g_post)

    gout, rsib_out = _dw_swapped(ycat_t, dy, N_CHIPS, 1, "dw_out")
    csum_out = _chip_sums(gout, rsib_out, "rs_chip_sum_out")
    ssem_out, rsem_out, csum_out, land_out, sent_out = _owners_start(csum_out, "rs_owners_start_out")
    dycat = _matmul_nt(dy, woutf, BF16, "dycat")
    dproj, dco, d_o, delta, dg_conv, dg_attn = _mix_bwd(dycat, co, proj, o_mix, g_conv, g_attn_pairs)
    dproj, conv_sums = _conv_bwd(dproj, dco, proj, conv_w_full, dc, sent_out)
    dproj = _attn_bwd(dproj, proj, d_o, lse, delta, slopes, dc, da, sent_out)
    gin, rsib_in = _dw_swapped(ht, dproj, 1, N_CHIPS, "dw_in")
    ssem_in0, rsem_in0, csum_in0, land_in0, sent_in0 = _owners_start(
        _chip_sums(gin, rsib_in, "rs_chip_sum_in0", 0, 2), "rs_owners_start_in0")
    ssem_in1, rsem_in1, csum_in1, land_in1, sent_in = _owners_start(
        _chip_sums(gin, rsib_in, "rs_chip_sum_in1", 1, 2, after=(sent_in0,)), "rs_owners_start_in1")
    dh = _dh(dproj, winf, sent_in)
    grad_x, pre_sums = _prenorm_bwd(x2, dh, dout, mod, g_pre)

    small, so = _pack_small([
        pre_sums[0], pre_sums[1], post_sums[0],
        pre_sums[2], conv_sums[0:3], conv_sums[3], dg_conv, dg_attn, post_sums[1], post_sums[2, 0:128]])
    ssem_small, rsem_small, small, land_small, sent_small = _allgather8_start(small, dev, "gather_small_start")

    rici_out = _owners_wait(ssem_out, rsem_out, csum_out, land_out, [grad_x, sent_small], "rs_owners_wait_out")
    full_out, jsend_out, jrecv_out, joining_out = _join_start(
        _owner_sum(place, gout, rsib_out, rici_out, "rs_owner_sum_out"), "rs_join_start_out", 0, 1)
    rici_in = _owners_wait(ssem_in0, rsem_in0, csum_in0, land_in0, [joining_out], "rs_owners_wait_in0")
    full_in0, jsend0, jrecv0, joining0 = _join_start(
        _owner_sum(place, gin, rsib_in, rici_in, "rs_owner_sum_in0", 0, 2), "rs_join_start_in0", 0, 2)
    grad_w_out = _join_wait(full_out, jsend_out, jrecv_out, [joining0], "rs_join_wait_out", 0, 1)
    grad_w_out, delta_w_out, new_m_w_out, new_v_w_out = _adamw(
        w_out2, grad_w_out, m_w_out[0], v_w_out[0], "adamw_w_out")
    full_in0 = _join_wait(full_in0, jsend0, jrecv0, [delta_w_out], "rs_join_wait_in0", 0, 2)
    updated_in = _adamw(w_in2, full_in0, m_w_in[0], v_w_in[0], "adamw_w_in0", 0, 2)
    rici_in = _owners_wait(ssem_in1, rsem_in1, csum_in1, land_in1, [updated_in[1]], "rs_owners_wait_in1")
    full_in1, jsend1, jrecv1, joining1 = _join_start(
        _owner_sum(place, gin, rsib_in, rici_in, "rs_owner_sum_in1", 1, 2), "rs_join_start_in1", 1, 2)

    small_seen = _allgather8_wait(ssem_small, rsem_small, small, land_small, [joining1], "gather_small_wait")
    small_w = [b_ada, g_pre, conv_w, conv_b, g_conv, g_attn, g_post]
    small_m = [m_b_ada, m_g_pre, m_conv_w, m_conv_b, m_g_conv, m_g_attn, m_g_post]
    small_v = [v_b_ada, v_g_pre, v_conv_w, v_conv_b, v_g_conv, v_g_attn, v_g_post]
    pieces = [(0, 3 * d), (so[3], d), (so[4], 3 * dc), (so[5], dc), (so[6], dc), (so[7], da), (so[8], d), (so[9], LANES)]
    g_small, d_small, m_small, v_small, loss_row = _small_update(place, small_seen, pieces, small_w, small_m, small_v)
    loss = loss_row[0, 0]
    grad_b_ada, grad_g_pre, grad_conv_w, grad_conv_b, grad_g_conv, grad_g_attn, grad_g_post = g_small
    dmod_cols = lax.dynamic_slice_in_dim(small_seen.reshape(N_DEV, -1), chip * wa, wa, axis=1)
    grad_w_ada, delta_w_ada, new_m_w_ada, new_v_w_ada = _ada_grad_adamw(c_all, dmod_cols, w_ada2, m_w_ada[0], v_w_ada[0])

    full_in1 = _join_wait(full_in1, jsend1, jrecv1, [delta_w_ada, d_small[0]], "rs_join_wait_in1", 1, 2)
    grad_w_in, delta_w_in, new_m_w_in, new_v_w_in = _adamw(
        w_in2, full_in1, m_w_in[0], v_w_in[0], "adamw_w_in1", 1, 2, updated_in)

    def lead(a):
        return a.reshape((1,) + a.shape)

    grads = [lead(grad_w_ada), grad_b_ada, grad_g_pre, lead(grad_w_in), grad_conv_w, grad_conv_b, grad_g_conv,
             grad_g_attn, lead(grad_w_out), grad_g_post]
    deltas = [lead(delta_w_ada), d_small[0], d_small[1], lead(delta_w_in), d_small[2], d_small[3], d_small[4],
              d_small[5], lead(delta_w_out), d_small[6]]
    new_ms = [lead(new_m_w_ada), m_small[0], m_small[1], lead(new_m_w_in), m_small[2], m_small[3], m_small[4],
              m_small[5], lead(new_m_w_out), m_small[6]]
    new_vs = [lead(new_v_w_ada), v_small[0], v_small[1], lead(new_v_w_in), v_small[2], v_small[3], v_small[4],
              v_small[5], lead(new_v_w_out), v_small[6]]
    return (loss, lead(grad_x), *grads, *deltas, *new_ms, *new_vs)
```

```python
import functools

import jax
import jax.numpy as jnp
import numpy as np
from jax import lax
from jax.experimental import pallas as pl
from jax.experimental.pallas import tpu as pltpu

F32 = jnp.float32
BF16 = jnp.bfloat16
MESH = pl.DeviceIdType.MESH
HBM = pl.BlockSpec(memory_space=pltpu.HBM)
VMEM = pl.BlockSpec(memory_space=pltpu.VMEM)
ANY = pl.BlockSpec(memory_space=pl.ANY)
SEM = pl.BlockSpec(memory_space=pltpu.SEMAPHORE)
EFFECT = pltpu.SideEffectType.DATAFLOW_SIDE_EFFECTING
SUBLANES, LANES = 8, 128
TOKEN = jax.ShapeDtypeStruct((SUBLANES, LANES), jnp.float32)

HEAD_DIM = 64
PAIR = 2 * HEAD_DIM
assert PAIR == LANES
BRANCHES = ((128, 1), (512, 4), (2048, 16))
SIDE = 64
EPS = 1e-6
NEG_INF = -1e30
N_CHIPS = 4
N_DEV = 8

ADAM_LR = 0.001
ADAM_B1 = 0.9
ADAM_B2 = 0.999
ADAM_EPS = 1e-08
ADAM_WD = 0.01
ADAM_STEP = 10

VMEM_LIMIT_BYTES = 56 * 1024 * 1024
ROW_TILE = 256
COL_TILE = 512
CONV_TILE = 256
ATT_BQ = 128
ATT_KW = ATT_BQ + 2 * SIDE
ATT_UNROLL = 4
STAGE_PAD = 4
SMALL_ALIGN = SUBLANES * LANES


def _params(semantics=None):
    kw = {"vmem_limit_bytes": VMEM_LIMIT_BYTES}
    if semantics is not None:
        kw["dimension_semantics"] = semantics
    return pltpu.CompilerParams(**kw)


def _silu(z):
    return z * jax.nn.sigmoid(z)


def _silu_grad(z):
    s = jax.nn.sigmoid(z)
    return s * (1.0 + z * (1.0 - s))


def _my_place():
    return lax.axis_index("x"), lax.axis_index("y"), lax.axis_index("c")


def _flip(a, bit):
    return 1 - a if bit else a


def _chip_of(x, y):
    return 2 * x + y


def _allgather8_start(v, me, name):
    rows_per, n = v.shape
    land = lax.dynamic_update_slice(jnp.zeros((N_DEV * rows_per, n), v.dtype), v, (me * rows_per, 0))

    def body(v_ref, land_ref, send_sems, recv_sems, v_thru, land_thru, token_ref):
        del v_thru, land_thru
        x, y, c = _my_place()
        mine = land_ref.at[pl.ds(pl.multiple_of((4 * x + 2 * y + c) * rows_per, rows_per), rows_per), :]
        for k in range(1, N_DEV):
            peer = (_flip(x, k & 4), _flip(y, k & 2), _flip(c, k & 1))
            pltpu.make_async_remote_copy(
                src_ref=v_ref, dst_ref=mine, send_sem=send_sems.at[k - 1], recv_sem=recv_sems.at[k - 1],
                device_id=peer, device_id_type=MESH).start()
        token_ref[...] = jnp.zeros(token_ref.shape, F32)

    sems = pltpu.SemaphoreType.DMA((N_DEV - 1,))
    return pl.pallas_call(
        body, name=name,
        out_shape=(sems, sems, jax.ShapeDtypeStruct(v.shape, v.dtype), jax.ShapeDtypeStruct(land.shape, land.dtype), TOKEN),
        in_specs=[HBM, HBM], out_specs=(SEM, SEM, HBM, HBM, VMEM),
        input_output_aliases={0: 2, 1: 3},
        compiler_params=pltpu.CompilerParams(has_side_effects=EFFECT),
    )(pltpu.with_memory_space_constraint(v, pltpu.HBM), pltpu.with_memory_space_constraint(land, pltpu.HBM))


def _allgather8_wait(send_sems, recv_sems, v, land, after, name):
    rows_per = v.shape[0]

    def body(v_ref, land_ref, send_ref, recv_ref, *rest):
        del rest
        x, y, c = _my_place()
        for k in range(1, N_DEV):
            peer = (_flip(x, k & 4), _flip(y, k & 2), _flip(c, k & 1))
            src = 4 * peer[0] + 2 * peer[1] + peer[2]
            cp = pltpu.make_async_remote_copy(
                src_ref=v_ref, dst_ref=land_ref.at[pl.ds(pl.multiple_of(src * rows_per, rows_per), rows_per), :],
                send_sem=send_ref.at[k - 1], recv_sem=recv_ref.at[k - 1], device_id=peer, device_id_type=MESH)
            cp.wait_send()
            cp.wait_recv()

    return pl.pallas_call(
        body, name=name,
        out_shape=(jax.ShapeDtypeStruct(v.shape, v.dtype), jax.ShapeDtypeStruct(land.shape, land.dtype)),
        in_specs=[HBM, HBM, SEM, SEM] + [ANY] * len(after), out_specs=(HBM, HBM),
        input_output_aliases={0: 0, 1: 1},
        compiler_params=pltpu.CompilerParams(has_side_effects=EFFECT),
    )(v, land, send_sems, recv_sems, *after)[1]


def _half_rows(ref, chip, which, half):
    return ref.at[chip, pl.ds(pl.multiple_of(which * half, half), half), :]


def _ici_peers(x, y, c):
    peers = [(_flip(x, k & 2), _flip(y, k & 1), c) for k in (1, 2, 3)]
    return [(peer, _chip_of(peer[0], peer[1])) for peer in peers]


def _part_of_half(ref, chip, core, part):
    half, cols = ref.shape[1] // 2, ref.shape[2] // 2
    return ref.at[chip, pl.ds(pl.multiple_of(core * half, half), half), pl.ds(part * cols, cols)]


def _neighbours(x, y, c):
    return [((x, 1 - y, c), _chip_of(x, 1 - y)), ((1 - x, y, c), _chip_of(1 - x, y)),
            ((1 - x, 1 - y, c), _chip_of(1 - x, 1 - y))]


def _start_direct(buf, send_sems, recv_sems):
    x, y, c = _my_place()
    me = _chip_of(x, y)
    for n, (peer, _) in enumerate(_neighbours(x, y, c)[0:2]):
        for part in ((0, 1), (1, 0))[n]:
            piece = _part_of_half(buf, me, c, part)
            pltpu.make_async_remote_copy(
                src_ref=piece, dst_ref=piece, send_sem=send_sems.at[2 * n + part], recv_sem=recv_sems.at[2 * n + part],
                device_id=peer, device_id_type=MESH).start()


def _relay(buf, recv_sems, relay_send, relay_recv):
    x, y, c = _my_place()
    nbrs = _neighbours(x, y, c)
    for n in range(2):
        part = n
        piece = _part_of_half(buf, nbrs[n][1], c, part)
        pltpu.make_async_remote_copy(
            src_ref=piece, dst_ref=piece, send_sem=relay_send.at[part], recv_sem=recv_sems.at[2 * n + part],
            device_id=nbrs[n][0], device_id_type=MESH).wait_recv()
        pltpu.make_async_remote_copy(
            src_ref=piece, dst_ref=piece, send_sem=relay_send.at[part], recv_sem=relay_recv.at[part],
            device_id=nbrs[1 - n][0], device_id_type=MESH).start()


def _gather_start(win_slots, after):
    def body(win_in, after_ref, win_ref, send_sems, recv_sems, token_ref):
        del win_in, after_ref
        _start_direct(win_ref, send_sems, recv_sems)
        token_ref[...] = jnp.zeros(token_ref.shape, F32)

    sems = pltpu.SemaphoreType.DMA((4,))
    return pl.pallas_call(
        body, name="gather_start",
        out_shape=(jax.ShapeDtypeStruct(win_slots.shape, win_slots.dtype), sems, sems, TOKEN),
        in_specs=[HBM, ANY], out_specs=(HBM, SEM, SEM, VMEM),
        input_output_aliases={0: 0},
        compiler_params=pltpu.CompilerParams(has_side_effects=EFFECT),
    )(win_slots, after)


def _gather_relay_in(win, wout_slots, recv_in, after):
    def body(win_in, wout_in, recv_in_ref, after_ref, win_ref, wout_ref, relay_send, relay_recv, send_out, recv_out):
        del win_in, wout_in, after_ref
        _relay(win_ref, recv_in_ref, relay_send, relay_recv)
        _start_direct(wout_ref, send_out, recv_out)

    two, four = pltpu.SemaphoreType.DMA((2,)), pltpu.SemaphoreType.DMA((4,))
    return pl.pallas_call(
        body, name="gather_relay_w_in",
        out_shape=(jax.ShapeDtypeStruct(win.shape, win.dtype), jax.ShapeDtypeStruct(wout_slots.shape, wout_slots.dtype),
                   two, two, four, four),
        in_specs=[HBM, HBM, SEM, ANY], out_specs=(HBM, HBM, SEM, SEM, SEM, SEM),
        input_output_aliases={0: 0, 1: 1},
        compiler_params=pltpu.CompilerParams(has_side_effects=EFFECT),
    )(win, wout_slots, recv_in, after)


def _gather_relay_out(wout, recv_out, after):
    def body(wout_in, recv_out_ref, after_ref, wout_ref, relay_send, relay_recv):
        del wout_in, after_ref
        _relay(wout_ref, recv_out_ref, relay_send, relay_recv)

    two = pltpu.SemaphoreType.DMA((2,))
    return pl.pallas_call(
        body, name="gather_relay_w_out",
        out_shape=(jax.ShapeDtypeStruct(wout.shape, wout.dtype), two, two),
        in_specs=[HBM, SEM, ANY], out_specs=(HBM, SEM, SEM),
        input_output_aliases={0: 0},
        compiler_params=pltpu.CompilerParams(has_side_effects=EFFECT),
    )(wout, recv_out, after)


def _gather_wait_direct(buf, send_sems, recv_sems, after, name):
    def body(buf_in, send_ref, recv_ref, after_ref, buf_ref):
        del buf_in, after_ref
        x, y, c = _my_place()
        me = _chip_of(x, y)
        for n, (peer, chip) in enumerate(_neighbours(x, y, c)[0:2]):
            second = 1 - n
            pltpu.make_async_remote_copy(
                src_ref=_part_of_half(buf_ref, me, c, second), dst_ref=_part_of_half(buf_ref, chip, c, second),
                send_sem=send_ref.at[2 * n + second], recv_sem=recv_ref.at[2 * n + second],
                device_id=peer, device_id_type=MESH).wait_recv()
            for part in range(2):
                piece = _part_of_half(buf_ref, me, c, part)
                pltpu.make_async_remote_copy(
                    src_ref=piece, dst_ref=piece, send_sem=send_ref.at[2 * n + part], recv_sem=recv_ref.at[2 * n + part],
                    device_id=peer, device_id_type=MESH).wait_send()

    return pl.pallas_call(
        body, name=name,
        out_shape=jax.ShapeDtypeStruct(buf.shape, buf.dtype),
        in_specs=[HBM, SEM, SEM, ANY], out_specs=HBM,
        input_output_aliases={0: 0},
        compiler_params=pltpu.CompilerParams(has_side_effects=EFFECT),
    )(buf, send_sems, recv_sems, after)


def _gather_wait_relayed(buf, relay_send, relay_recv, after, name):
    def body(buf_in, rsend_ref, rrecv_ref, after_ref, buf_ref):
        del buf_in, after_ref
        x, y, c = _my_place()
        nbrs = _neighbours(x, y, c)
        for n in range(2):
            relayed = _part_of_half(buf_ref, nbrs[n][1], c, n)
            cp = pltpu.make_async_remote_copy(
                src_ref=relayed, dst_ref=_part_of_half(buf_ref, nbrs[2][1], c, n),
                send_sem=rsend_ref.at[n], recv_sem=rrecv_ref.at[n], device_id=nbrs[1 - n][0], device_id_type=MESH)
            cp.wait_recv()
            cp.wait_send()

    return pl.pallas_call(
        body, name=name,
        out_shape=jax.ShapeDtypeStruct(buf.shape, buf.dtype),
        in_specs=[HBM, SEM, SEM, ANY], out_specs=HBM,
        input_output_aliases={0: 0},
        compiler_params=pltpu.CompilerParams(has_side_effects=EFFECT),
    )(buf, relay_send, relay_recv, after)


def _forward_copies(buf_ref, which, send_sems, recv_sems):
    half = buf_ref.shape[1] // 2
    x, y, c = _my_place()

    def copy(k, chip, core, part):
        piece = _half_rows(buf_ref, chip, core, half) if part is None else _part_of_half(buf_ref, chip, core, part)
        return pltpu.make_async_remote_copy(
            src_ref=piece, dst_ref=piece, send_sem=send_sems.at[k], recv_sem=recv_sems.at[k],
            device_id=(x, y, 1 - c), device_id_type=MESH)

    chips = [_neighbours(x, y, c)[n][1] for n, _ in which]
    return [(copy(k, chip, c, part), copy(k, chip, 1 - c, part)) for k, (chip, (_, part)) in enumerate(zip(chips, which))]


def _forward_halves(buf, which, name):
    def body(buf_in, buf_ref, send_sems, recv_sems):
        del buf_in
        copies = _forward_copies(buf_ref, which, send_sems, recv_sems)
        for mine, _ in copies:
            mine.start()
        for mine, theirs in copies:
            theirs.wait_recv()
        for mine, _ in copies:
            mine.wait_send()

    return pl.pallas_call(
        body, name=name,
        out_shape=jax.ShapeDtypeStruct(buf.shape, buf.dtype),
        in_specs=[HBM], out_specs=HBM,
        input_output_aliases={0: 0},
        scratch_shapes=[pltpu.SemaphoreType.DMA((len(which),))] * 2,
    )(buf)


def _forward_start(buf, which, name):
    def body(buf_in, buf_ref, send_sems, recv_sems, token_ref):
        del buf_in
        for mine, _ in _forward_copies(buf_ref, which, send_sems, recv_sems):
            mine.start()
        token_ref[...] = jnp.zeros(token_ref.shape, F32)

    sems = pltpu.SemaphoreType.DMA((len(which),))
    return pl.pallas_call(
        body, name=name,
        out_shape=(jax.ShapeDtypeStruct(buf.shape, buf.dtype), sems, sems, TOKEN),
        in_specs=[HBM], out_specs=(HBM, SEM, SEM, VMEM),
        input_output_aliases={0: 0},
        compiler_params=pltpu.CompilerParams(has_side_effects=EFFECT),
    )(buf)


def _forward_wait(buf, which, send_sems, recv_sems, after, name):
    def body(buf_in, send_ref, recv_ref, after_ref, buf_ref):
        del buf_in, after_ref
        for mine, theirs in _forward_copies(buf_ref, which, send_ref, recv_ref):
            theirs.wait_recv()
            mine.wait_send()

    return pl.pallas_call(
        body, name=name,
        out_shape=jax.ShapeDtypeStruct(buf.shape, buf.dtype),
        in_specs=[HBM, SEM, SEM, ANY], out_specs=HBM,
        input_output_aliases={0: 0},
        compiler_params=pltpu.CompilerParams(has_side_effects=EFFECT),
    )(buf, send_sems, recv_sems, after)


def _dw_swapped(a, b, row_chunks, col_chunks, name):
    r, t = a.shape
    c_all = b.shape[1]
    chunks = row_chunks * col_chunks
    rq, cq = r // row_chunks, c_all // col_chunks
    half = rq // 2
    tn = COL_TILE
    nt = cq // tn
    steps = col_chunks * nt

    def body(a_ref, b_ref, mine_ref, sib_ref, stage, send_sems, recv_sems):
        x, y, c = _my_place()
        j, n = pl.program_id(0), pl.program_id(1)
        step = j * nt + n
        slot = step % 2
        res = jnp.dot(a_ref[...], b_ref[...], preferred_element_type=F32).astype(BF16)

        def landing(jj, nn):
            cols = pl.ds(pl.multiple_of(nn * tn, tn), tn)
            return sib_ref.at[:, :, cols] if col_chunks == 1 else sib_ref.at[pl.ds(jj, 1), :, cols]

        def copy(slot_, step_, jj, nn):
            return pltpu.make_async_remote_copy(
                src_ref=stage.at[slot_], dst_ref=landing(jj, nn), send_sem=send_sems.at[slot_],
                recv_sem=recv_sems.at[step_], device_id=(x, y, 1 - c), device_id_type=MESH)

        @pl.when(step >= 2)
        def _():
            copy(slot, step, j, n).wait_send()

        for q in range(row_chunks):
            lo = res[q * rq:q * rq + half, :]
            hi = res[q * rq + half:(q + 1) * rq, :]
            mine_ref[q] = jnp.where(c == 0, lo, hi)
            stage[slot, q] = jnp.where(c == 0, hi, lo)
        copy(slot, step, j, n).start()

        @pl.when(step == steps - 1)
        def _():
            for s in range(max(steps - 2, 0), steps):
                copy(s % 2, s, j, n).wait_send()
            for s in range(steps):
                copy(s % 2, s, j, n).wait_recv()

    shape = jax.ShapeDtypeStruct((chunks, half, cq), BF16)
    return pl.pallas_call(
        body, name=name, grid=(col_chunks, nt),
        out_shape=(shape, shape),
        in_specs=[pl.BlockSpec((r, t), lambda j, n: (0, 0)), pl.BlockSpec((t, tn), lambda j, n: (0, j * nt + n))],
        out_specs=(pl.BlockSpec((row_chunks, half, tn), lambda j, n: (j, 0, n)), ANY),
        scratch_shapes=[pltpu.VMEM((2, row_chunks, half, tn), BF16), pltpu.SemaphoreType.DMA((2,)),
                        pltpu.SemaphoreType.DMA((steps,))],
        compiler_params=_params(("arbitrary", "arbitrary")),
    )(a, b)


def _owners_start(csum, name, after=()):
    land = pltpu.with_memory_space_constraint(lax.empty((N_CHIPS - 1,) + csum.shape[1:], csum.dtype), pltpu.HBM)

    def body(csum_ref, land_ref, *rest):
        send_sems, recv_sems, _, _, token_ref = rest[len(after):]
        x, y, c = _my_place()
        for k, (peer, owner) in enumerate(_ici_peers(x, y, c)):
            pltpu.make_async_remote_copy(
                src_ref=csum_ref.at[owner], dst_ref=land_ref.at[k], send_sem=send_sems.at[k], recv_sem=recv_sems.at[k],
                device_id=peer, device_id_type=MESH).start()
        token_ref[...] = jnp.zeros(token_ref.shape, F32)

    sems = pltpu.SemaphoreType.DMA((N_CHIPS - 1,))
    return pl.pallas_call(
        body, name=name,
        out_shape=(sems, sems, jax.ShapeDtypeStruct(csum.shape, csum.dtype),
                   jax.ShapeDtypeStruct(land.shape, land.dtype), TOKEN),
        in_specs=[HBM, HBM] + [ANY] * len(after), out_specs=(SEM, SEM, HBM, HBM, VMEM),
        input_output_aliases={0: 2, 1: 3},
        compiler_params=pltpu.CompilerParams(has_side_effects=EFFECT),
    )(pltpu.with_memory_space_constraint(csum, pltpu.HBM), land, *after)


def _owners_wait(send_sems, recv_sems, csum, land, after, name):
    def body(csum_ref, land_ref, send_ref, recv_ref, *rest):
        del rest
        x, y, c = _my_place()
        for k, (peer, owner) in enumerate(_ici_peers(x, y, c)):
            cp = pltpu.make_async_remote_copy(
                src_ref=csum_ref.at[owner], dst_ref=land_ref.at[k], send_sem=send_ref.at[k], recv_sem=recv_ref.at[k],
                device_id=peer, device_id_type=MESH)
            cp.wait_send()
            cp.wait_recv()

    return pl.pallas_call(
        body, name=name,
        out_shape=(jax.ShapeDtypeStruct(csum.shape, csum.dtype), jax.ShapeDtypeStruct(land.shape, land.dtype)),
        in_specs=[HBM, HBM, SEM, SEM] + [ANY] * len(after), out_specs=(HBM, HBM),
        input_output_aliases={0: 0, 1: 1},
        compiler_params=pltpu.CompilerParams(has_side_effects=EFFECT),
    )(csum, land, send_sems, recv_sems, *after)[1]


def _join_start(full, name, part, parts):
    half = full.shape[0] // 2
    rows = half // parts

    def body(full_in, full_ref, send_sem, recv_sem, token_ref):
        del full_in
        x, y, c = _my_place()
        mine = full_ref.at[pl.ds(pl.multiple_of(c * half + part * rows, rows), rows), :]
        pltpu.make_async_remote_copy(
            src_ref=mine, dst_ref=mine, send_sem=send_sem.at[0], recv_sem=recv_sem.at[0],
            device_id=(x, y, 1 - c), device_id_type=MESH).start()
        token_ref[...] = jnp.zeros(token_ref.shape, F32)

    one = pltpu.SemaphoreType.DMA((1,))
    return pl.pallas_call(
        body, name=name,
        out_shape=(jax.ShapeDtypeStruct(full.shape, full.dtype), one, one, TOKEN),
        in_specs=[HBM], out_specs=(HBM, SEM, SEM, VMEM),
        input_output_aliases={0: 0},
        compiler_params=pltpu.CompilerParams(has_side_effects=EFFECT),
    )(full)


def _join_wait(full, send_sem, recv_sem, after, name, part, parts):
    half = full.shape[0] // 2
    rows = half // parts

    def body(full_in, send_ref, recv_ref, *rest):
        del full_in
        full_ref = rest[-1]
        x, y, c = _my_place()
        cp = pltpu.make_async_remote_copy(
            src_ref=full_ref.at[pl.ds(pl.multiple_of(c * half + part * rows, rows), rows), :],
            dst_ref=full_ref.at[pl.ds(pl.multiple_of((1 - c) * half + part * rows, rows), rows), :],
            send_sem=send_ref.at[0], recv_sem=recv_ref.at[0], device_id=(x, y, 1 - c), device_id_type=MESH)
        cp.wait_send()
        cp.wait_recv()

    return pl.pallas_call(
        body, name=name,
        out_shape=jax.ShapeDtypeStruct(full.shape, full.dtype),
        in_specs=[HBM, SEM, SEM] + [ANY] * len(after), out_specs=HBM,
        input_output_aliases={0: 0},
        compiler_params=pltpu.CompilerParams(has_side_effects=EFFECT),
    )(full, send_sem, recv_sem, *after)


def _cast_into_slot(place, w, name, after):
    rows, cols = w.shape
    tr = min(rows, ROW_TILE)

    def body(place_ref, w_ref, after_ref, o_ref):
        del place_ref, after_ref
        o_ref[...] = w_ref[...].astype(BF16)

    grid_spec = pltpu.PrefetchScalarGridSpec(
        num_scalar_prefetch=1, grid=(rows // tr,),
        in_specs=[pl.BlockSpec((tr, cols), lambda i, p: (i, 0)), ANY],
        out_specs=pl.BlockSpec((None, tr, cols), lambda i, p: (p[0], i, 0)))
    return pl.pallas_call(
        body, name=name, grid_spec=grid_spec,
        out_shape=jax.ShapeDtypeStruct((N_CHIPS, rows, cols), BF16),
        compiler_params=_params(("parallel",)),
    )(place, w, after)


def _ada_modulation(packed, w_ada, b_ada, d, w_big):
    rows_per, n = packed.shape
    d_model, wa = w_ada.shape
    big_rows, big_cols = w_big.shape
    n_chunks = big_rows // ROW_TILE
    first_chunks = (2 * n_chunks) // 3

    def body(v_ref, w_hbm, b_ref, big_hbm, all_ref, mod_ref, slots_hbm, w_vmem, part_ref, parts_ref, wide, narrow,
             load_sem, send1, recv1, send2, recv2, in_sems, out_sems):
        x, y, c = _my_place()
        me = 4 * x + 2 * y + c
        chip = _chip_of(x, y)
        load = pltpu.make_async_copy(w_hbm, w_vmem, load_sem)
        load.start()

        def chunk_in(i):
            return pltpu.make_async_copy(big_hbm.at[i * ROW_TILE:(i + 1) * ROW_TILE, :], wide.at[i % 2], in_sems.at[i % 2])

        def chunk_out(i):
            return pltpu.make_async_copy(
                narrow.at[i % 2], slots_hbm.at[chip, i * ROW_TILE:(i + 1) * ROW_TILE, :], out_sems.at[i % 2])

        def cast_chunk(i):
            if i + 1 < n_chunks:
                chunk_in(i + 1).start()
            chunk_in(i).wait()
            if i >= 2:
                chunk_out(i - 2).wait()
            narrow[i % 2] = wide[i % 2].astype(BF16)
            chunk_out(i).start()

        chunk_in(0).start()

        def rows(idx):
            return all_ref.at[pl.ds(pl.multiple_of(idx * rows_per, rows_per), rows_per), :]

        all_ref[pl.ds(pl.multiple_of(me * rows_per, rows_per), rows_per), :] = v_ref[...]
        copies = []
        for k in range(1, N_DEV):
            peer = (_flip(x, k & 4), _flip(y, k & 2), _flip(c, k & 1))
            cp = pltpu.make_async_remote_copy(
                src_ref=v_ref, dst_ref=rows(me), send_sem=send1.at[k - 1], recv_sem=recv1.at[k - 1],
                device_id=peer, device_id_type=MESH)
            cp.start()
            copies.append((cp, peer))
        for i in range(first_chunks):
            cast_chunk(i)
        for k, (cp, peer) in enumerate(copies):
            pltpu.make_async_remote_copy(
                src_ref=v_ref, dst_ref=rows(4 * peer[0] + 2 * peer[1] + peer[2]), send_sem=send1.at[k],
                recv_sem=recv1.at[k], device_id=peer, device_id_type=MESH).wait_recv()
        for cp, _ in copies:
            cp.wait_send()

        def c_of(dev):
            segments, pos = [], 0
            while pos < d:
                row, col = divmod(pos, n)
                take = min(d - pos, n - col)
                segments.append(all_ref[dev * rows_per + row:dev * rows_per + row + 1, col:col + take])
                pos += take
            return jnp.concatenate(segments, axis=1)

        c_all = jnp.concatenate([c_of(dev) for dev in range(N_DEV)], axis=0)
        load.wait()
        part_ref[...] = jnp.dot(_silu(c_all), w_vmem[...], precision=lax.Precision.HIGHEST, preferred_element_type=F32)
        parts_ref[chip] = part_ref[...]
        swaps = []
        for k, (peer, _) in enumerate(_ici_peers(x, y, c)):
            cp = pltpu.make_async_remote_copy(
                src_ref=part_ref, dst_ref=parts_ref.at[chip], send_sem=send2.at[k], recv_sem=recv2.at[k],
                device_id=peer, device_id_type=MESH)
            cp.start()
            swaps.append(cp)
        for i in range(first_chunks, n_chunks):
            cast_chunk(i)
        for i in range(n_chunks - 2, n_chunks):
            chunk_out(i).wait()
        for k, (peer, peer_chip) in enumerate(_ici_peers(x, y, c)):
            pltpu.make_async_remote_copy(
                src_ref=part_ref, dst_ref=parts_ref.at[peer_chip], send_sem=send2.at[k], recv_sem=recv2.at[k],
                device_id=peer, device_id_type=MESH).wait_recv()
        for cp in swaps:
            cp.wait_send()
        flat = jnp.concatenate([parts_ref[j, pl.ds(me, 1), :] for j in range(N_CHIPS)], axis=1) + b_ref[...]
        mod_ref[...] = jnp.concatenate([flat[:, i * d:(i + 1) * d] for i in range(3)], axis=0)

    return pl.pallas_call(
        body, name="ada_modulation",
        out_shape=(jax.ShapeDtypeStruct((N_DEV * rows_per, n), F32), jax.ShapeDtypeStruct((3, d), F32),
                   jax.ShapeDtypeStruct((N_CHIPS, big_rows, big_cols), BF16)),
        in_specs=[VMEM, ANY, VMEM, ANY], out_specs=(VMEM, VMEM, ANY),
        scratch_shapes=[pltpu.VMEM((d_model, wa), F32), pltpu.VMEM((N_DEV, wa), F32),
                        pltpu.VMEM((N_CHIPS, N_DEV, wa), F32),
                        pltpu.VMEM((2, ROW_TILE, big_cols), F32), pltpu.VMEM((2, ROW_TILE, big_cols), BF16),
                        pltpu.SemaphoreType.DMA,
                        pltpu.SemaphoreType.DMA((N_DEV - 1,)), pltpu.SemaphoreType.DMA((N_DEV - 1,)),
                        pltpu.SemaphoreType.DMA((N_CHIPS - 1,)), pltpu.SemaphoreType.DMA((N_CHIPS - 1,)),
                        pltpu.SemaphoreType.DMA((2,)), pltpu.SemaphoreType.DMA((2,))],
        compiler_params=_params(),
    )(packed, w_ada, b_ada, w_big)


def _prenorm(x, mod, g_pre, after):
    t, d = x.shape
    tb = ROW_TILE

    def body(x_ref, mod_ref, g_ref, after_ref, h_ref, ht_ref):
        del after_ref
        xv = x_ref[...]
        r = lax.rsqrt(jnp.mean(xv * xv, axis=-1, keepdims=True) + EPS)
        h = (xv * r) * g_ref[...] * (1.0 + mod_ref[1:2, :]) + mod_ref[0:1, :]
        h_ref[...] = h.astype(BF16)
        ht_ref[...] = h.T.astype(BF16)

    return pl.pallas_call(
        body, name="prenorm", grid=(t // tb,),
        out_shape=(jax.ShapeDtypeStruct((t, d), BF16), jax.ShapeDtypeStruct((d, t), BF16)),
        in_specs=[pl.BlockSpec((tb, d), lambda i: (i, 0)), pl.BlockSpec((3, d), lambda i: (0, 0)),
                  pl.BlockSpec((1, d), lambda i: (0, 0)), ANY],
        out_specs=(pl.BlockSpec((tb, d), lambda i: (i, 0)), pl.BlockSpec((d, tb), lambda i: (0, i))),
        compiler_params=_params(("parallel",)),
    )(x, mod, g_pre, after)


def _proj_tiles(proj, h, w, tiles, step, name):
    t, d = h.shape
    ws = w.shape[-1]
    tn = COL_TILE
    nt = ws // tn

    def body(tile_ref, *refs):
        del tile_ref
        a_ref, b_ref, o_ref = refs[-3:]
        o_ref[...] = jnp.dot(a_ref[...], b_ref[...].astype(BF16), preferred_element_type=F32).astype(BF16)

    if w.ndim == 3:
        w_spec = pl.BlockSpec((None, d, tn), lambda i, tl: (tl[step, i] // nt, 0, tl[step, i] % nt))
    else:
        w_spec = pl.BlockSpec((d, tn), lambda i, tl: (0, tl[step, i] % nt))
    first = proj is None
    grid_spec = pltpu.PrefetchScalarGridSpec(
        num_scalar_prefetch=1, grid=(tiles.shape[1],),
        in_specs=([] if first else [HBM]) + [pl.BlockSpec((t, d), lambda i, tl: (0, 0)), w_spec],
        out_specs=pl.BlockSpec((t, tn), lambda i, tl: (0, tl[step, i])))
    return pl.pallas_call(
        body, name=name, grid_spec=grid_spec,
        out_shape=jax.ShapeDtypeStruct((t, N_CHIPS * ws), BF16),
        input_output_aliases={} if first else {1: 0},
        compiler_params=_params(("parallel",)),
    )(*([tiles] if first else [tiles, proj]), h, w)


def _shift_rows(a, rows):
    idx = lax.broadcasted_iota(jnp.int32, a.shape, 0)
    prev = jnp.where(idx == 0, 0.0, pltpu.roll(a, 1, 0))
    nxt = jnp.where(idx == rows - 1, 0.0, pltpu.roll(a, rows - 1, 0))
    return prev, nxt


def _conv_fwd(conv_proj, conv_w, conv_b, dc):
    t = conv_proj.shape[0]
    ct = CONV_TILE
    nct = dc // ct

    def body(u_ref, cg_ref, w_ref, b_ref, co_ref):
        a = cg_ref[...].astype(F32) * u_ref[...].astype(F32)
        prev, nxt = _shift_rows(a, t)
        co_ref[...] = (w_ref[0:1, :] * prev + w_ref[1:2, :] * a + w_ref[2:3, :] * nxt + b_ref[...]).astype(BF16)

    return pl.pallas_call(
        body, name="conv_fwd", grid=(nct,),
        out_shape=jax.ShapeDtypeStruct((t, dc), BF16),
        in_specs=[pl.BlockSpec((t, ct), lambda i: (0, i)), pl.BlockSpec((t, ct), lambda i: (0, 2 * nct + i)),
                  pl.BlockSpec((3, ct), lambda i: (0, i)), pl.BlockSpec((1, ct), lambda i: (0, i))],
        out_specs=pl.BlockSpec((t, ct), lambda i: (0, i)),
        compiler_params=_params(("parallel",)),
    )(conv_proj, conv_proj, conv_w, conv_b)


def _stage_pitch(r):
    return r + STAGE_PAD if r % SUBLANES == 0 else r


def _stage_rows(t):
    return max(t // r * _stage_pitch(r) for _, r in BRANCHES)


def _gather_residues(src_ref, stage, dst_ref, r):
    t = src_ref.shape[0]
    seq, pitch = t // r, _stage_pitch(r)
    from_ref = stage
    if pitch == r and src_ref.dtype == F32:
        from_ref = src_ref
    elif pitch == r:
        stage[0:t, :] = src_ref[...].astype(F32)
    else:
        for g in range(seq):
            stage[g * pitch:g * pitch + r, :] = src_ref[g * r:(g + 1) * r, :].astype(F32)
    for res in range(r):
        dst_ref[res * seq:(res + 1) * seq, :] = from_ref[pl.ds(res, seq, stride=pitch), :].astype(dst_ref.dtype)


def _scatter_residues(src_ref, stage, dst_ref, r, add, rounded_ref=None):
    t = src_ref.shape[0]
    seq, pitch = t // r, _stage_pitch(r)
    assert rounded_ref is None or pitch != r
    if pitch == r:
        for res in range(r):
            tok = pl.ds(res, seq, stride=r)
            val = src_ref[res * seq:(res + 1) * seq, :]
            dst_ref[tok, :] = dst_ref[tok, :] + val if add else val
        return
    for res in range(r):
        stage[pl.ds(res, seq, stride=pitch), :] = src_ref[res * seq:(res + 1) * seq, :]
    for g in range(seq):
        rows = slice(g * r, (g + 1) * r)
        val = stage[g * pitch:g * pitch + r, :]
        val = dst_ref[rows, :] + val if add else val
        if rounded_ref is None:
            dst_ref[rows, :] = val
        else:
            rounded_ref[rows, :] = val.astype(rounded_ref.dtype)


def _branch_operands(token_refs, stage, dil, r):
    if r == 1:
        return list(token_refs)
    for i, ref in enumerate(token_refs):
        _gather_residues(ref, stage, dil.at[i], r)
    return [dil.at[i] for i in range(len(token_refs))]


def _scaled_queries(q):
    return (q.astype(F32) * (HEAD_DIM ** -0.5)).astype(BF16)


BLOCK_SHIFTS = (0, -SIDE, None)


def _band_bias(rel, slope):
    arel = jnp.abs(rel)
    return jnp.where(arel <= SIDE, arel.astype(F32) * slope, NEG_INF)


def _fill_bias_tiles(bias_ref, sl_ref, r, kw):
    base = lax.broadcasted_iota(jnp.int32, (ATT_BQ, kw), 1) - lax.broadcasted_iota(jnp.int32, (ATT_BQ, kw), 0)
    for hh in range(2):
        slope = -(sl_ref[hh:hh + 1, 0:kw] * float(r))
        for e, shift in enumerate(BLOCK_SHIFTS):
            shift = ATT_BQ - kw if shift is None else shift
            bias_ref[hh, e, :, 0:kw] = _band_bias(base + shift, slope)


def _fill_stacked_bias_tiles(bias_ref, sl_ref, r, kw):
    base = lax.broadcasted_iota(jnp.int32, (kw, ATT_BQ), 0) - lax.broadcasted_iota(jnp.int32, (kw, ATT_BQ), 1)
    for hh in range(2):
        slope = -(sl_ref[hh:hh + 1, 0:ATT_BQ] * float(r))
        for e, shift in enumerate(BLOCK_SHIFTS):
            shift = ATT_BQ - kw if shift is None else shift
            bias_ref[e, 0:kw, hh * ATT_BQ:(hh + 1) * ATT_BQ] = _band_bias(base + shift, slope)


def _first_head_lanes():
    return lax.broadcasted_iota(jnp.int32, (1, PAIR), 1) < HEAD_DIM


def _only_head(x, first, hh):
    return jnp.where(first if hh == 0 else jnp.logical_not(first), x, jnp.zeros_like(x))


def _block_place(g, seq_len, kw):
    nqb = seq_len // ATT_BQ
    if nqb == 1:
        row = pl.multiple_of(g * ATT_BQ, ATT_BQ)
        return row, row, 0
    res = g // nqb
    qb = g - res * nqb
    q0 = qb * ATT_BQ
    ks = jnp.clip(q0 - SIDE, 0, seq_len - kw)
    edge = jnp.where(qb == 0, 0, jnp.where(qb == nqb - 1, 2, 1))
    return (pl.multiple_of(res * seq_len + q0, ATT_BQ), pl.multiple_of(res * seq_len + ks, SIDE), edge)


def _qkv_specs(dc, da, t, index):
    return [pl.BlockSpec((t, PAIR), functools.partial(index, (4 * dc + comp * da) // PAIR)) for comp in range(3)]


def _attn_fwd(proj, slopes, dc, da):
    t = proj.shape[0]
    hp = da // PAIR
    n_blocks = t // ATT_BQ

    def body(q_ref, k_ref, v_ref, sl_ref, o_ref, lse_ref, stage, dil, bias, o_res, l_res, o_tok, l_tok):
        for b, (_, r) in enumerate(BRANCHES):
            seq_len = t // r
            kw = min(ATT_KW, seq_len)
            ops = _branch_operands([q_ref, k_ref, v_ref], stage, dil, r)
            _fill_bias_tiles(bias, sl_ref, r, kw)
            o_dst, l_dst = (o_tok.at[b], l_tok.at[b]) if r == 1 else (o_res, l_res)
            first = _first_head_lanes()

            def blocks(trip, carry, seq_len=seq_len, kw=kw, o_dst=o_dst, l_dst=l_dst, first=first, ops=ops):
                nt = (((1,), (1,)), ((), ()))
                places = [_block_place(trip * ATT_UNROLL + i, seq_len, kw) for i in range(ATT_UNROLL)]
                chains = [(i, hh) for i in range(ATT_UNROLL) for hh in range(2)]
                qs = [_scaled_queries(ops[0][pl.ds(qrow, ATT_BQ), :]) for qrow, _, _ in places]
                ks = [ops[1][pl.ds(krow, kw), :] for _, krow, _ in places]
                vs = [ops[2][pl.ds(krow, kw), :] for _, krow, _ in places]
                ss = [lax.dot_general(_only_head(qs[i], first, hh), ks[i], nt, preferred_element_type=F32)
                      + bias[hh, places[i][2], :, 0:kw] for i, hh in chains]
                tops = [jnp.max(s, axis=-1, keepdims=True) for s in ss]
                ps = [jnp.exp(s - m) for s, m in zip(ss, tops)]
                dens = [jnp.sum(p, axis=-1, keepdims=True) for p in ps]
                for i, (qrow, _, _) in enumerate(places):
                    weights = jnp.concatenate([ps[2 * i].astype(BF16), ps[2 * i + 1].astype(BF16)], axis=1)
                    values = jnp.concatenate([_only_head(vs[i], first, 0), _only_head(vs[i], first, 1)], axis=0)
                    den = jnp.where(first, dens[2 * i], dens[2 * i + 1])
                    o_dst[pl.ds(qrow, ATT_BQ), :] = jnp.dot(weights, values, preferred_element_type=F32) / den
                    l_dst[pl.ds(qrow, ATT_BQ), :] = jnp.where(first, tops[2 * i], tops[2 * i + 1]) + jnp.log(den)
                return carry

            lax.fori_loop(0, n_blocks // ATT_UNROLL, blocks, 0)
            if r > 1:
                _scatter_residues(o_res, stage, o_tok.at[b], r, add=False)
                _scatter_residues(l_res, stage, l_tok.at[b], r, add=False)

        def merge(i, carry):
            rows = pl.ds(pl.multiple_of(i * ROW_TILE, ROW_TILE), ROW_TILE)
            la, lb, lc = l_tok[0, rows, :], l_tok[1, rows, :], l_tok[2, rows, :]
            m = jnp.maximum(jnp.maximum(la, lb), lc)
            wa, wb, wc = jnp.exp(la - m), jnp.exp(lb - m), jnp.exp(lc - m)
            den = wa + wb + wc
            o_ref[rows, :] = (wa * o_tok[0, rows, :] + wb * o_tok[1, rows, :] + wc * o_tok[2, rows, :]) * (1.0 / den)
            lse_ref[rows, :] = m + jnp.log(den)
            return carry

        lax.fori_loop(0, t // ROW_TILE, merge, 0)

    pair_spec = pl.BlockSpec((None, t, PAIR), lambda h: (h, 0, 0))
    return pl.pallas_call(
        body, name="attn_fwd", grid=(hp,),
        out_shape=(jax.ShapeDtypeStruct((hp, t, PAIR), F32), jax.ShapeDtypeStruct((hp, t, PAIR), F32)),
        in_specs=_qkv_specs(dc, da, t, lambda first, h: (0, first + h))
        + [pl.BlockSpec((None, 8, ATT_KW), lambda h: (h, 0, 0))],
        out_specs=(pair_spec, pair_spec),
        scratch_shapes=[pltpu.VMEM((_stage_rows(t), PAIR), F32), pltpu.VMEM((3, t, PAIR), BF16),
                        pltpu.VMEM((2, 3, ATT_BQ, ATT_KW), F32),
                        pltpu.VMEM((t, PAIR), F32), pltpu.VMEM((t, PAIR), F32),
                        pltpu.VMEM((3, t, PAIR), F32), pltpu.VMEM((3, t, PAIR), F32)],
        compiler_params=_params(("parallel",)),
    )(proj, proj, proj, slopes)


def _attn_bwd(dproj, proj, d_o, lse, delta, slopes, dc, da, after):
    t = proj.shape[0]
    hp = da // PAIR
    n_blocks = t // ATT_BQ

    def all_branches(q_ref, k_ref, v_ref, do_ref, lse_ref, dl_ref, sl_ref,
                     stage, dil, packed, packed_res, row_vecs, bias_t, acc, tot, rounded, before_rounded):
        first = _first_head_lanes()
        lane = lax.broadcasted_iota(jnp.int32, (1, PAIR), 1)
        packed[...] = jnp.where((lane & (HEAD_DIM - 1)) < HEAD_DIM // 2, lse_ref[...], dl_ref[...])
        for b, (_, r) in enumerate(BRANCHES):
            seq_len = t // r
            kw = min(ATT_KW, seq_len)
            ops = _branch_operands([q_ref, k_ref, v_ref, do_ref], stage, dil, r)
            scalars = packed
            if r > 1:
                _gather_residues(packed, stage, packed_res, r)
                scalars = packed_res
            for g in range(n_blocks):
                flipped = scalars[g * ATT_BQ:(g + 1) * ATT_BQ, :].T
                for row in range(4):
                    row_vecs[g, row:row + 1, :] = flipped[row * (HEAD_DIM // 2):row * (HEAD_DIM // 2) + 1, :]
            _fill_stacked_bias_tiles(bias_t, sl_ref, r, kw)
            acc[1] = jnp.zeros((t, PAIR), F32)
            acc[2] = jnp.zeros((t, PAIR), F32)

            def blocks(trip, carry, seq_len=seq_len, kw=kw, ops=ops):
                nt = (((1,), (1,)), ((), ()))
                group = range(ATT_UNROLL)
                places = [_block_place(trip * ATT_UNROLL + i, seq_len, kw) for i in group]
                ks, vs, q2s, do2s, lse2s, dl2s = [], [], [], [], [], []
                for i, (qrow, krow, _) in zip(group, places):
                    q = _scaled_queries(ops[0][pl.ds(qrow, ATT_BQ), :])
                    dov = ops[3][pl.ds(qrow, ATT_BQ), :]
                    ks.append(ops[1][pl.ds(krow, kw), :])
                    vs.append(ops[2][pl.ds(krow, kw), :])
                    q2s.append(jnp.concatenate([_only_head(q, first, 0), _only_head(q, first, 1)], axis=0))
                    do2s.append(jnp.concatenate([_only_head(dov, first, 0), _only_head(dov, first, 1)], axis=0))
                    rows = row_vecs[trip * ATT_UNROLL + i]
                    lse2s.append(jnp.concatenate([rows[0:1, :], rows[2:3, :]], axis=1))
                    dl2s.append(jnp.concatenate([rows[1:2, :], rows[3:4, :]], axis=1))
                s_ts = [lax.dot_general(ks[i], q2s[i], nt, preferred_element_type=F32) for i in group]
                dp_ts = [lax.dot_general(vs[i], do2s[i], nt, preferred_element_type=F32) for i in group]
                p_ts = [jnp.exp(s_ts[i] + bias_t[places[i][2], 0:kw, :] - lse2s[i]) for i in group]
                ds_ts = [p_ts[i] * (dp_ts[i] - dl2s[i]) for i in group]
                dvs = [jnp.dot(p_ts[i].astype(BF16), do2s[i], preferred_element_type=F32) for i in group]
                dks = [jnp.dot(ds_ts[i].astype(BF16), q2s[i], preferred_element_type=F32) for i in group]
                dss = [ds_ts[i].T.astype(BF16) for i in group]
                dqs = [jnp.dot(dss[i][0:ATT_BQ, :], _only_head(ks[i], first, 0), preferred_element_type=F32)
                       + jnp.dot(dss[i][ATT_BQ:2 * ATT_BQ, :], _only_head(ks[i], first, 1), preferred_element_type=F32)
                       for i in group]
                for i, (qrow, krow, _) in zip(group, places):
                    acc[0, pl.ds(qrow, ATT_BQ), :] = dqs[i] * (HEAD_DIM ** -0.5)
                    acc[1, pl.ds(krow, kw), :] += dks[i]
                    acc[2, pl.ds(krow, kw), :] += dvs[i]
                return carry

            lax.fori_loop(0, n_blocks // ATT_UNROLL, blocks, 0)
            for comp in range(3):
                if r == 1:
                    tot[comp] = acc[comp]
                else:
                    last = b == len(BRANCHES) - 1
                    if last and comp == 0:
                        before_rounded()
                    _scatter_residues(acc.at[comp], stage, tot.at[comp], r, True, rounded.at[comp] if last else None)

    first_q = (4 * dc) // PAIR

    def body(dproj_in, q_ref, k_ref, v_ref, do_ref, lse_ref, dl_ref, sl_ref, after_ref, out_ref, *scratch):
        del dproj_in, after_ref
        work, out_stage, out_sems = scratch[:-2], scratch[-2], scratch[-1]
        h = pl.program_id(0)

        def out_copy(comp):
            cols = pl.ds(pl.multiple_of((first_q + comp * hp + h) * PAIR, PAIR), PAIR)
            return pltpu.make_async_copy(out_stage.at[comp], out_ref.at[:, cols], out_sems.at[comp])

        def stage_is_free():
            @pl.when(h > 0)
            def _():
                for comp in range(3):
                    out_copy(comp).wait()

        all_branches(q_ref, k_ref, v_ref, do_ref, lse_ref, dl_ref, sl_ref, *work, out_stage, stage_is_free)
        for comp in range(3):
            out_copy(comp).start()

        @pl.when(h == hp - 1)
        def _():
            for comp in range(3):
                out_copy(comp).wait()

    pair_spec = pl.BlockSpec((None, t, PAIR), lambda h: (h, 0, 0))
    return pl.pallas_call(
        body, name="attn_bwd", grid=(hp,),
        out_shape=jax.ShapeDtypeStruct(dproj.shape, BF16),
        in_specs=[HBM] + _qkv_specs(dc, da, t, lambda first, h: (0, first + h))
        + [pair_spec, pair_spec, pair_spec, pl.BlockSpec((None, 8, ATT_KW), lambda h: (h, 0, 0)), ANY],
        out_specs=ANY,
        input_output_aliases={0: 0},
        scratch_shapes=[pltpu.VMEM((_stage_rows(t), PAIR), F32), pltpu.VMEM((4, t, PAIR), BF16),
                        pltpu.VMEM((t, PAIR), F32), pltpu.VMEM((t, PAIR), F32),
                        pltpu.VMEM((n_blocks, 8, ATT_BQ), F32), pltpu.VMEM((3, ATT_KW, 2 * ATT_BQ), F32),
                        pltpu.VMEM((3, t, PAIR), F32), pltpu.VMEM((3, t, PAIR), F32),
                        pltpu.VMEM((3, t, PAIR), BF16), pltpu.SemaphoreType.DMA((3,))],
        compiler_params=_params(("arbitrary",)),
    )(dproj, proj, proj, proj, d_o, lse, delta, slopes, after)


def _mix_fwd(co, proj, o_mix, g_conv, g_attn_pairs, after):
    t, dc = co.shape
    hp = o_mix.shape[0]
    da = hp * PAIR
    tb = ROW_TILE

    def body(co_ref, bg_ref, zc_ref, za_ref, om_ref, gc_ref, ga_ref, after_ref, ycat_ref, ycatt_ref):
        del after_ref
        p = bg_ref[...].astype(F32) * co_ref[...].astype(F32)
        rc = lax.rsqrt(jnp.mean(p * p, axis=-1, keepdims=True) + EPS)
        yc = (p * rc) * gc_ref[...] * _silu(zc_ref[...].astype(F32))
        ycat_ref[:, 0:dc] = yc.astype(BF16)
        ycatt_ref[0:dc, :] = yc.T.astype(BF16)
        ssq = jnp.zeros((tb, 1), F32)
        for h in range(hp):
            o = om_ref[h]
            ssq = ssq + jnp.sum(o * o, axis=-1, keepdims=True)
        ra = lax.rsqrt(ssq * (1.0 / da) + EPS)
        for h in range(hp):
            ya = (om_ref[h] * ra) * ga_ref[h] * _silu(za_ref[:, h * PAIR:(h + 1) * PAIR].astype(F32))
            ycat_ref[:, dc + h * PAIR:dc + (h + 1) * PAIR] = ya.astype(BF16)
            ycatt_ref[dc + h * PAIR:dc + (h + 1) * PAIR, :] = ya.T.astype(BF16)

    pair_spec = pl.BlockSpec((hp, tb, PAIR), lambda i: (0, i, 0))
    return pl.pallas_call(
        body, name="mix_fwd", grid=(t // tb,),
        out_shape=(jax.ShapeDtypeStruct((t, dc + da), BF16), jax.ShapeDtypeStruct((dc + da, t), BF16)),
        in_specs=[pl.BlockSpec((tb, dc), lambda i: (i, 0)),
                  pl.BlockSpec((tb, dc), lambda i: (i, 1)),
                  pl.BlockSpec((tb, dc), lambda i: (i, 3)),
                  pl.BlockSpec((tb, da), lambda i: (i, 7)),
                  pair_spec,
                  pl.BlockSpec((1, dc), lambda i: (0, 0)),
                  pl.BlockSpec((hp, 1, PAIR), lambda i: (0, 0, 0)), ANY],
        out_specs=(pl.BlockSpec((tb, dc + da), lambda i: (i, 0)), pl.BlockSpec((dc + da, tb), lambda i: (0, i))),
        compiler_params=_params(("parallel",)),
    )(co, proj, proj, proj, o_mix, g_conv, g_attn_pairs, after)


def _out_fwd_bwd(ycat, woutf, x, target, mod, g_post):
    t, d = x.shape
    n = ycat.shape[1]
    tb = ROW_TILE

    def body(a_ref, w_ref, x_ref, tg_ref, mod_ref, g_ref, dout_ref, dy_ref, acc_ref):
        y = jnp.dot(a_ref[...], w_ref[...], preferred_element_type=F32)
        r = lax.rsqrt(jnp.mean(y * y, axis=-1, keepdims=True) + EPS)
        nh = y * r
        gate = mod_ref[2:3, :]
        nrm = nh * g_ref[...]
        err = x_ref[...] + gate * nrm - tg_ref[...]
        dout = err * (1.0 / d)
        dout_ref[...] = dout.astype(BF16)
        dn = dout * gate
        a = dn * g_ref[...]
        dy = r * (a - nh * jnp.mean(a * nh, axis=-1, keepdims=True))
        dy_ref[...] = dy.astype(BF16)
        loss = 0.5 * jnp.sum(jnp.sum(err * err, axis=-1, keepdims=True) * (1.0 / d), axis=0, keepdims=True)
        part = jnp.concatenate(
            [jnp.sum(dout * nrm, axis=0, keepdims=True), jnp.sum(dn * nh, axis=0, keepdims=True),
             jnp.broadcast_to(loss, (1, d)), jnp.zeros((5, d), F32)], axis=0)

        @pl.when(pl.program_id(0) == 0)
        def _():
            acc_ref[...] = jnp.zeros(acc_ref.shape, F32)

        acc_ref[...] += part

    return pl.pallas_call(
        body, name="out_fwd_bwd", grid=(t // tb,),
        out_shape=(jax.ShapeDtypeStruct((t, d), BF16), jax.ShapeDtypeStruct((t, d), BF16),
                   jax.ShapeDtypeStruct((8, d), F32)),
        in_specs=[pl.BlockSpec((tb, n), lambda i: (i, 0)), pl.BlockSpec((n, d), lambda i: (0, 0)),
                  pl.BlockSpec((tb, d), lambda i: (i, 0)), pl.BlockSpec((tb, d), lambda i: (i, 0)),
                  pl.BlockSpec((3, d), lambda i: (0, 0)), pl.BlockSpec((1, d), lambda i: (0, 0))],
        out_specs=(pl.BlockSpec((tb, d), lambda i: (i, 0)), pl.BlockSpec((tb, d), lambda i: (i, 0)),
                   pl.BlockSpec((8, d), lambda i: (0, 0))),
        compiler_params=_params(("arbitrary",)),
    )(ycat, woutf, x, target, mod, g_post)


def _matmul_nt(a, b, out_dtype, name):
    m, k = a.shape
    n = b.shape[0]
    tn = COL_TILE

    def body(a_ref, b_ref, o_ref):
        o_ref[...] = lax.dot_general(a_ref[...], b_ref[...], (((1,), (1,)), ((), ())),
                                     preferred_element_type=F32).astype(out_dtype)

    return pl.pallas_call(
        body, name=name, grid=(n // tn,),
        out_shape=jax.ShapeDtypeStruct((m, n), out_dtype),
        in_specs=[pl.BlockSpec((m, k), lambda i: (0, 0)), pl.BlockSpec((tn, k), lambda i: (i, 0))],
        out_specs=pl.BlockSpec((m, tn), lambda i: (0, i)),
        compiler_params=_params(("parallel",)),
    )(a, b)


def _mix_bwd(dycat, co, proj, o_mix, g_conv, g_attn_pairs):
    t, dc = co.shape
    hp = o_mix.shape[0]
    da = hp * PAIR
    tb = ROW_TILE

    def body(dy_ref, co_ref, bg_ref, zc_ref, za_ref, om_ref, gc_ref, ga_ref,
             dcp_ref, dco_ref, do_ref, dl_ref, dgc_ref, dga_ref):
        first = pl.program_id(0) == 0
        cov = co_ref[...].astype(F32)
        bg = bg_ref[...].astype(F32)
        zc = zc_ref[...].astype(F32)
        p = bg * cov
        rc = lax.rsqrt(jnp.mean(p * p, axis=-1, keepdims=True) + EPS)
        nh = p * rc
        dyc = dy_ref[:, 0:dc].astype(F32)
        dn = dyc * _silu(zc)
        a = dn * gc_ref[...]
        dp = rc * (a - nh * jnp.mean(a * nh, axis=-1, keepdims=True))
        dcp_ref[:, 0:dc] = jnp.zeros((tb, dc), BF16)
        dcp_ref[:, dc:2 * dc] = (dp * cov).astype(BF16)
        dcp_ref[:, 2 * dc:3 * dc] = jnp.zeros((tb, dc), BF16)
        dcp_ref[:, 3 * dc:4 * dc] = (dyc * nh * gc_ref[...] * _silu_grad(zc)).astype(BF16)
        dcp_ref[:, 4 * dc:4 * dc + 3 * da] = jnp.zeros((tb, 3 * da), BF16)
        dco_ref[...] = dp * bg

        @pl.when(first)
        def _():
            dgc_ref[...] = jnp.zeros(dgc_ref.shape, F32)
            dga_ref[...] = jnp.zeros(dga_ref.shape, F32)

        dgc_ref[...] += jnp.sum(dn * nh, axis=0, keepdims=True)

        ssq = jnp.zeros((tb, 1), F32)
        for h in range(hp):
            o = om_ref[h]
            ssq = ssq + jnp.sum(o * o, axis=-1, keepdims=True)
        ra = lax.rsqrt(ssq * (1.0 / da) + EPS)
        dot_an = jnp.zeros((tb, 1), F32)
        for h in range(hp):
            nha = om_ref[h] * ra
            za = za_ref[:, h * PAIR:(h + 1) * PAIR].astype(F32)
            dya = dy_ref[:, dc + h * PAIR:dc + (h + 1) * PAIR].astype(F32)
            dna = dya * _silu(za)
            dza = (dya * nha * ga_ref[h] * _silu_grad(za)).astype(BF16)
            dcp_ref[:, 4 * dc + 3 * da + h * PAIR:4 * dc + 3 * da + (h + 1) * PAIR] = dza
            dga_ref[h] += jnp.sum(dna * nha, axis=0, keepdims=True)
            dot_an = dot_an + jnp.sum(dna * ga_ref[h] * nha, axis=-1, keepdims=True)
        mean_an = dot_an * (1.0 / da)
        first_head = lax.broadcasted_iota(jnp.int32, (tb, PAIR), 1) < HEAD_DIM
        for h in range(hp):
            o = om_ref[h]
            nha = o * ra
            za = za_ref[:, h * PAIR:(h + 1) * PAIR].astype(F32)
            dya = dy_ref[:, dc + h * PAIR:dc + (h + 1) * PAIR].astype(F32)
            aa = dya * _silu(za) * ga_ref[h]
            d_o = ra * (aa - nha * mean_an)
            do_ref[h] = d_o.astype(BF16)
            prod = d_o * o
            both = jnp.sum(prod, axis=-1, keepdims=True)
            head0 = jnp.sum(jnp.where(first_head, prod, 0.0), axis=-1, keepdims=True)
            dl_ref[h] = jnp.where(first_head, head0, both - head0)

    pair_spec = pl.BlockSpec((hp, tb, PAIR), lambda i: (0, i, 0))
    return pl.pallas_call(
        body, name="mix_bwd", grid=(t // tb,),
        out_shape=(jax.ShapeDtypeStruct((t, 4 * dc + 4 * da), BF16), jax.ShapeDtypeStruct((t, dc), F32),
                   jax.ShapeDtypeStruct((hp, t, PAIR), BF16), jax.ShapeDtypeStruct((hp, t, PAIR), F32),
                   jax.ShapeDtypeStruct((1, dc), F32), jax.ShapeDtypeStruct((hp, 1, PAIR), F32)),
        in_specs=[pl.BlockSpec((tb, dc + da), lambda i: (i, 0)),
                  pl.BlockSpec((tb, dc), lambda i: (i, 0)),
                  pl.BlockSpec((tb, dc), lambda i: (i, 1)),
                  pl.BlockSpec((tb, dc), lambda i: (i, 3)),
                  pl.BlockSpec((tb, da), lambda i: (i, 7)),
                  pair_spec,
                  pl.BlockSpec((1, dc), lambda i: (0, 0)),
                  pl.BlockSpec((hp, 1, PAIR), lambda i: (0, 0, 0))],
        out_specs=(pl.BlockSpec((tb, 4 * dc + 4 * da), lambda i: (i, 0)), pl.BlockSpec((tb, dc), lambda i: (i, 0)),
                   pair_spec, pair_spec,
                   pl.BlockSpec((1, dc), lambda i: (0, 0)), pl.BlockSpec((hp, 1, PAIR), lambda i: (0, 0, 0))),
        compiler_params=_params(("arbitrary",)),
    )(dycat, co, proj, proj, proj, o_mix, g_conv, g_attn_pairs)


def _conv_bwd(dconv_proj, dco, conv_proj, conv_w, dc, after):
    t = dco.shape[0]
    ct = CONV_TILE
    nct = dc // ct

    def body(dcp_in_ref, dco_ref, u_ref, cg_ref, w_ref, after_ref, dcp_ref, acc_ref):
        del dcp_in_ref, after_ref
        which = pl.program_id(1)
        g = dco_ref[...]
        u = u_ref[...].astype(F32)
        cg = cg_ref[...].astype(F32)
        g_prev, g_next = _shift_rows(g, t)
        da = w_ref[0:1, :] * g_next + w_ref[1:2, :] * g + w_ref[2:3, :] * g_prev
        dcp_ref[...] = (da * jnp.where(which == 0, cg, u)).astype(BF16)
        a = cg * u
        a_prev, a_next = _shift_rows(a, t)
        acc_ref[...] = jnp.concatenate(
            [jnp.sum(g * a_prev, axis=0, keepdims=True), jnp.sum(g * a, axis=0, keepdims=True),
             jnp.sum(g * a_next, axis=0, keepdims=True), jnp.sum(g, axis=0, keepdims=True),
             jnp.zeros((4, ct), F32)], axis=0)

    return pl.pallas_call(
        body, name="conv_bwd", grid=(nct, 2),
        out_shape=(jax.ShapeDtypeStruct(dconv_proj.shape, BF16), jax.ShapeDtypeStruct((8, dc), F32)),
        in_specs=[HBM,
                  pl.BlockSpec((t, ct), lambda i, s: (0, i)),
                  pl.BlockSpec((t, ct), lambda i, s: (0, i)),
                  pl.BlockSpec((t, ct), lambda i, s: (0, 2 * nct + i)),
                  pl.BlockSpec((3, ct), lambda i, s: (0, i)), ANY],
        out_specs=(pl.BlockSpec((t, ct), lambda i, s: (0, 2 * s * nct + i)),
                   pl.BlockSpec((8, ct), lambda i, s: (0, i))),
        input_output_aliases={0: 0},
        compiler_params=_params(("arbitrary", "arbitrary")),
    )(dconv_proj, dco, conv_proj, conv_proj, conv_w, after)


def _dh(dproj, winf, after):
    t = dproj.shape[0]
    _, d, ws = winf.shape
    tm = tn = COL_TILE
    nt = (((1,), (1,)), ((), ()))

    def body(a_ref, w_ref, after_ref, o_ref):
        del after_ref
        acc = lax.dot_general(a_ref[:, 0:ws], w_ref[0], nt, preferred_element_type=F32)
        for j in range(1, N_CHIPS):
            acc = acc + lax.dot_general(a_ref[:, j * ws:(j + 1) * ws], w_ref[j], nt, preferred_element_type=F32)
        o_ref[...] = acc.astype(BF16)

    return pl.pallas_call(
        body, name="dh", grid=(d // tn, t // tm),
        out_shape=jax.ShapeDtypeStruct((t, d), BF16),
        in_specs=[pl.BlockSpec((tm, N_CHIPS * ws), lambda n, m: (m, 0)),
                  pl.BlockSpec((N_CHIPS, tn, ws), lambda n, m: (0, n, 0)), ANY],
        out_specs=pl.BlockSpec((tm, tn), lambda n, m: (m, n)),
        compiler_params=_params(("parallel", "parallel")),
    )(dproj, winf, after)


def _prenorm_bwd(x, dh, dout, mod, g_pre):
    t, d = x.shape
    tb = ROW_TILE

    def body(x_ref, dh_ref, dout_ref, mod_ref, g_ref, gx_ref, acc_ref):
        xv = x_ref[...]
        dhv = dh_ref[...].astype(F32)
        r = lax.rsqrt(jnp.mean(xv * xv, axis=-1, keepdims=True) + EPS)
        xh = xv * r
        one_scale = 1.0 + mod_ref[1:2, :]
        a = dhv * one_scale * g_ref[...]
        gx_ref[...] = dout_ref[...].astype(F32) + r * (a - xh * jnp.mean(a * xh, axis=-1, keepdims=True))
        part = jnp.concatenate(
            [jnp.sum(dhv, axis=0, keepdims=True), jnp.sum(dhv * xh * g_ref[...], axis=0, keepdims=True),
             jnp.sum(dhv * xh * one_scale, axis=0, keepdims=True), jnp.zeros((5, d), F32)], axis=0)

        @pl.when(pl.program_id(0) == 0)
        def _():
            acc_ref[...] = jnp.zeros(acc_ref.shape, F32)

        acc_ref[...] += part

    return pl.pallas_call(
        body, name="prenorm_bwd", grid=(t // tb,),
        out_shape=(jax.ShapeDtypeStruct((t, d), F32), jax.ShapeDtypeStruct((8, d), F32)),
        in_specs=[pl.BlockSpec((tb, d), lambda i: (i, 0)), pl.BlockSpec((tb, d), lambda i: (i, 0)),
                  pl.BlockSpec((tb, d), lambda i: (i, 0)), pl.BlockSpec((3, d), lambda i: (0, 0)),
                  pl.BlockSpec((1, d), lambda i: (0, 0))],
        out_specs=(pl.BlockSpec((tb, d), lambda i: (i, 0)), pl.BlockSpec((8, d), lambda i: (0, 0))),
        compiler_params=_params(("arbitrary",)),
    )(x, dh, dout, mod, g_pre)


def _chip_sums(mine, rsib, name, part=0, parts=1, after=()):
    _, half, cols = mine.shape
    rows = half // parts
    tr = min(rows, ROW_TILE)
    nt = rows // tr

    def body(g_ref, r_ref, *rest):
        rest[-1][...] = (g_ref[...].astype(F32) + r_ref[...].astype(F32)).astype(BF16)

    spec = pl.BlockSpec((None, tr, cols), lambda j, i: (j, part * nt + i, 0))
    return pl.pallas_call(
        body, name=name, grid=(N_CHIPS, nt),
        out_shape=jax.ShapeDtypeStruct((N_CHIPS, rows, cols), BF16),
        in_specs=[spec, spec] + [ANY] * len(after), out_specs=pl.BlockSpec((None, tr, cols), lambda j, i: (j, i, 0)),
        compiler_params=_params(("parallel", "parallel")),
    )(mine, rsib, *after)


def _owner_sum(place, mine, rsib, rici, name, part=0, parts=1):
    _, half, cols = mine.shape
    rows = half // parts
    tr = min(rows, ROW_TILE)
    nt = rows // tr

    def body(place_ref, g_ref, r_ref, i_ref, o_ref):
        del place_ref
        acc = g_ref[...].astype(F32) + r_ref[...].astype(F32)
        for k in range(N_CHIPS - 1):
            acc = acc + i_ref[k].astype(F32)
        o_ref[...] = acc

    own = pl.BlockSpec((None, tr, cols), lambda i, p: (p[0], part * nt + i, 0))
    grid_spec = pltpu.PrefetchScalarGridSpec(
        num_scalar_prefetch=1, grid=(nt,),
        in_specs=[own, own, pl.BlockSpec((N_CHIPS - 1, tr, cols), lambda i, p: (0, i, 0))],
        out_specs=pl.BlockSpec((tr, cols), lambda i, p: (p[1] * (half // tr) + part * nt + i, 0)))
    return pl.pallas_call(
        body, name=name, grid_spec=grid_spec,
        out_shape=jax.ShapeDtypeStruct((2 * half, cols), F32),
        compiler_params=_params(("parallel",)),
    )(place, mine, rsib, rici)


def _adam_math(w, g, m, v):
    m2 = ADAM_B1 * m + (1.0 - ADAM_B1) * g
    v2 = ADAM_B2 * v + (1.0 - ADAM_B2) * (g * g)
    m_hat = m2 / (1.0 - ADAM_B1 ** ADAM_STEP)
    v_hat = v2 / (1.0 - ADAM_B2 ** ADAM_STEP)
    delta = -ADAM_LR * (m_hat / (jnp.sqrt(v_hat) + ADAM_EPS) + ADAM_WD * w)
    return delta, m2, v2


def _adamw(w, g, m, v, name, part=0, parts=1, prev=None):
    rows, cols = w.shape
    tr = min(rows, ROW_TILE)

    def body(*refs):
        w_ref, g_ref, m_ref, v_ref, go_ref, d_ref, m2_ref, v2_ref = refs[-8:]
        g = g_ref[...]
        go_ref[...] = g
        d_ref[...], m2_ref[...], v2_ref[...] = _adam_math(w_ref[...], g, m_ref[...], v_ref[...])

    if parts == 1:
        grid, spec = (rows // tr,), pl.BlockSpec((tr, cols), lambda i: (i, 0))
    else:
        per_half = rows // 2 // tr
        nt = per_half // parts
        grid, spec = (2, nt), pl.BlockSpec((tr, cols), lambda r, i: (r * per_half + part * nt + i, 0))
    olds = [] if prev is None else list(prev)
    return pl.pallas_call(
        body, name=name, grid=grid,
        out_shape=(jax.ShapeDtypeStruct(w.shape, F32),) * 4,
        in_specs=[HBM] * len(olds) + [spec] * 4, out_specs=(spec,) * 4,
        input_output_aliases={i: i for i in range(len(olds))},
        compiler_params=_params(("parallel",) * len(grid)),
    )(*olds, w, g, m, v)


def _ada_grad_adamw(c_all, dmod_cols, w, m, v):
    d, wa = w.shape
    tr = ROW_TILE

    def body(c_ref, dm_ref, w_ref, m_ref, v_ref, g_ref, d_ref, m2_ref, v2_ref):
        act = _silu(c_ref[...]).T
        g = act[:, 0:1] * dm_ref[0:1, :]
        for b in range(1, N_DEV):
            g = g + act[:, b:b + 1] * dm_ref[b:b + 1, :]
        g_ref[...] = g
        d_ref[...], m2_ref[...], v2_ref[...] = _adam_math(w_ref[...], g, m_ref[...], v_ref[...])

    spec = pl.BlockSpec((tr, wa), lambda i: (i, 0))
    return pl.pallas_call(
        body, name="ada_grad_adamw", grid=(d // tr,),
        out_shape=(jax.ShapeDtypeStruct(w.shape, F32),) * 4,
        in_specs=[pl.BlockSpec((N_DEV, tr), lambda i: (0, i)), pl.BlockSpec((N_DEV, wa), lambda i: (0, 0)),
                  spec, spec, spec],
        out_specs=(spec,) * 4,
        compiler_params=_params(("parallel",)),
    )(c_all, dmod_cols, w, m, v)


def _small_update(place, gathered, pieces, weights, moments_m, moments_v):
    n = gathered.shape[1]
    k = len(weights)
    final_shapes = [w.shape for w in weights]
    row_counts = [s[1] if len(s) == 3 else 1 for s in final_shapes]
    weights, moments_m, moments_v = ([a.reshape(1, -1) for a in arrays] for arrays in (weights, moments_m, moments_v))

    def body(place_ref, g_ref, *refs):
        w_refs, m_refs, v_refs = refs[0:k], refs[k:2 * k], refs[2 * k:3 * k]
        outs = refs[3 * k:]
        total = g_ref[0:SUBLANES, :]
        for dev in range(1, N_DEV):
            total = total + g_ref[SUBLANES * dev:SUBLANES * (dev + 1), :]

        def flat(offset, length):
            segments, pos = [], offset
            while pos < offset + length:
                row, col = divmod(pos, n)
                take = min(offset + length - pos, n - col)
                segments.append(total[row:row + 1, col:col + take])
                pos += take
            return jnp.concatenate(segments, axis=1) if len(segments) > 1 else segments[0]

        chip = place_ref[0]
        for i, (w_ref, m_ref, v_ref) in enumerate(zip(w_refs, m_refs, v_refs)):
            g = flat(*pieces[i])
            if pieces[i][1] > w_ref.shape[1]:
                rows = row_counts[i]
                cols, full = w_ref.shape[1] // rows, pieces[i][1] // rows
                picked = []
                for r in range(rows):
                    blocks = [g[:, r * full + q * cols:r * full + (q + 1) * cols] for q in range(N_CHIPS)]
                    mine = blocks[N_CHIPS - 1]
                    for q in range(N_CHIPS - 2, -1, -1):
                        mine = jnp.where(chip == q, blocks[q], mine)
                    picked.append(mine)
                g = jnp.concatenate(picked, axis=1)
            delta, m2, v2 = _adam_math(w_ref[...], g, m_ref[...], v_ref[...])
            for j, val in enumerate((g, delta, m2, v2)):
                outs[j * k + i][...] = val
        outs[4 * k][...] = flat(*pieces[k])

    shapes = [jax.ShapeDtypeStruct(w.shape, F32) for w in weights]
    grid_spec = pltpu.PrefetchScalarGridSpec(
        num_scalar_prefetch=1, grid=(1,),
        in_specs=[pl.BlockSpec(gathered.shape, lambda i, p: (0, 0))]
        + [pl.BlockSpec(a.shape, functools.partial(lambda nd, i, p: (0,) * nd, a.ndim))
           for a in (*weights, *moments_m, *moments_v)],
        out_specs=tuple(pl.BlockSpec(s.shape, functools.partial(lambda nd, i, p: (0,) * nd, len(s.shape)))
                        for s in shapes * 4) + (pl.BlockSpec((1, LANES), lambda i, p: (0, 0)),))
    outs = pl.pallas_call(
        body, name="small_update", grid_spec=grid_spec,
        out_shape=tuple(shapes * 4) + (jax.ShapeDtypeStruct((1, LANES), F32),),
        compiler_params=_params(("arbitrary",)),
    )(place, gathered, *weights, *moments_m, *moments_v)
    shaped = [out.reshape(final_shapes[i % k]) for i, out in enumerate(outs[0:4 * k])]
    return shaped[0:k], shaped[k:2 * k], shaped[2 * k:3 * k], shaped[3 * k:4 * k], outs[4 * k]


def _pack_small(pieces):
    flat = [p.reshape(-1).astype(F32) for p in pieces]
    offsets, total = [], 0
    for p in flat:
        offsets.append(total)
        total += p.shape[0]
    padded = -(-total // SMALL_ALIGN) * SMALL_ALIGN
    if padded > total:
        flat.append(jnp.zeros((padded - total,), F32))
    return jnp.concatenate(flat).reshape(8, padded // 8), offsets


def _alibi_slope_rows(n_heads):
    slopes = 2.0 ** (-8.0 * np.arange(1, n_heads + 1, dtype=np.float64) / n_heads)
    rows = np.zeros((n_heads // 2, SUBLANES), np.float32)
    rows[:, 0:2] = slopes.reshape(n_heads // 2, 2)
    return jnp.asarray(np.broadcast_to(rows[:, :, None], (n_heads // 2, SUBLANES, ATT_KW)))


def kernel(x, c, w_ada, b_ada, g_pre, w_in, conv_w, conv_b, g_conv, g_attn, w_out, g_post, loss_target, m_w_ada, m_b_ada, m_g_pre, m_w_in, m_conv_w, m_conv_b, m_g_conv, m_g_attn, m_w_out, m_g_post, v_w_ada, v_b_ada, v_g_pre, v_w_in, v_conv_w, v_conv_b, v_g_conv, v_g_attn, v_w_out, v_g_post):
    t, d = x.shape[1], x.shape[2]
    dc = conv_b.shape[1]
    da = g_attn.shape[1]
    hp = da // PAIR
    ws = w_in.shape[2]
    wa = w_ada.shape[2]
    cws = conv_w.shape[2]
    assert t % ROW_TILE == 0 and d % ROW_TILE == 0 and dc % COL_TILE == 0 and da % COL_TILE == 0
    assert ws == 2 * dc and dc == da and t // BRANCHES[-1][1] >= ATT_BQ

    mx, my, mc = _my_place()
    chip = _chip_of(mx, my)
    dev = 2 * chip + mc
    place = jnp.stack([chip, mc]).astype(jnp.int32)

    x2, tgt2 = x[0], loss_target[0]
    w_ada2, w_in2, w_out2 = w_ada[0], w_in[0], w_out[0]

    packed, offs = _pack_small([c[0], conv_w[0]])
    seen, mod, win_slots = _ada_modulation(packed, w_ada2, b_ada, d, w_in2)
    seen = seen.reshape(N_DEV, -1)
    c_all = seen[:, offs[0]:offs[0] + d]
    conv_w_full = seen[0::2, offs[1]:offs[1] + 3 * cws].reshape(N_CHIPS, 3, cws).transpose(1, 0, 2).reshape(3, dc)

    win_flight, send_in, recv_in, started = _gather_start(win_slots, mod)

    y_chip, x_chip, d_chip = (_chip_of(mx, 1 - my), _chip_of(1 - mx, my), _chip_of(1 - mx, 1 - my))
    tiles_per_part = ws // COL_TILE // 2

    def tiles_of(chunk, parts):
        return [(2 * chunk + part) * tiles_per_part + k for part in parts for k in range(tiles_per_part)]

    tiles = jnp.stack([jnp.stack(step) for step in (
        tiles_of(chip, (0, 1)), tiles_of(y_chip, (0,)) + tiles_of(x_chip, (1,)),
        tiles_of(y_chip, (1,)) + tiles_of(x_chip, (0,)), tiles_of(d_chip, (0, 1)))]).astype(jnp.int32)
    h, ht = _prenorm(x2, mod, g_pre, started)
    proj = _proj_tiles(None, h, w_in2, tiles, 0, "proj_own")
    win_flight, wout_flight, relay_send_in, relay_recv_in, send_out, recv_out = _gather_relay_in(
        win_flight, _cast_into_slot(place, w_out2, "cast_w_out", proj), recv_in, proj)
    win_flight = _forward_halves(win_flight, ((0, 0), (1, 1)), "forward_w_in_first")
    proj = _proj_tiles(proj, h, win_flight, tiles, 1, "proj_first_parts")
    win_flight = _forward_halves(
        _gather_wait_direct(win_flight, send_in, recv_in, proj, "gather_wait_w_in_direct"),
        ((0, 1), (1, 0)), "forward_w_in_second")
    proj = _proj_tiles(proj, h, win_flight, tiles, 2, "proj_second_parts")
    winf = _forward_halves(
        _gather_wait_relayed(win_flight, relay_send_in, relay_recv_in, proj, "gather_wait_w_in_relayed"),
        ((2, None),), "forward_w_in_relayed")
    proj = _proj_tiles(proj, h, winf, tiles, 3, "proj_diagonal")
    slopes = _alibi_slope_rows(da // HEAD_DIM)
    co = _conv_fwd(proj, conv_w_full, conv_b, dc)
    wout_flight, relay_send_out, relay_recv_out = _gather_relay_out(wout_flight, recv_out, co)
    o_mix, lse = _attn_fwd(proj, slopes, dc, da)
    g_attn_pairs = g_attn.reshape(hp, 1, PAIR)
    wout_flight = _gather_wait_direct(wout_flight, send_out, recv_out, o_mix, "gather_wait_w_out_direct")
    wout_flight = _gather_wait_relayed(wout_flight, relay_send_out, relay_recv_out, o_mix, "gather_wait_w_out_relayed")
    all_halves = ((0, None), (1, None), (2, None))
    wout_flight, fsend_out, frecv_out, forwarding = _forward_start(wout_flight, all_halves, "forward_w_out_start")
    ycat, ycat_t = _mix_fwd(co, proj, o_mix, g_conv, g_attn_pairs, forwarding)
    woutf = _forward_wait(wout_flight, all_halves, fsend_out, frecv_out, ycat, "forward_w_out_wait").reshape(dc + da, d)
    dout, dy, post_sums = _out_fwd_bwd(ycat, woutf, x2, tgt2, mod, g_post)

    gout, rsib_out = _dw_swapped(ycat_t, dy, N_CHIPS, 1, "dw_out")
    csum_out = _chip_sums(gout, rsib_out, "rs_chip_sum_out")
    ssem_out, rsem_out, csum_out, land_out, sent_out = _owners_start(csum_out, "rs_owners_start_out")
    dycat = _matmul_nt(dy, woutf, BF16, "dycat")
    dproj, dco, d_o, delta, dg_conv, dg_attn = _mix_bwd(dycat, co, proj, o_mix, g_conv, g_attn_pairs)
    dproj, conv_sums = _conv_bwd(dproj, dco, proj, conv_w_full, dc, sent_out)
    dproj = _attn_bwd(dproj, proj, d_o, lse, delta, slopes, dc, da, sent_out)
    gin, rsib_in = _dw_swapped(ht, dproj, 1, N_CHIPS, "dw_in")
    ssem_in0, rsem_in0, csum_in0, land_in0, sent_in0 = _owners_start(
        _chip_sums(gin, rsib_in, "rs_chip_sum_in0", 0, 2), "rs_owners_start_in0")
    ssem_in1, rsem_in1, csum_in1, land_in1, sent_in = _owners_start(
        _chip_sums(gin, rsib_in, "rs_chip_sum_in1", 1, 2, after=(sent_in0,)), "rs_owners_start_in1")
    dh = _dh(dproj, winf, sent_in)
    grad_x, pre_sums = _prenorm_bwd(x2, dh, dout, mod, g_pre)

    small, so = _pack_small([
        pre_sums[0], pre_sums[1], post_sums[0],
        pre_sums[2], conv_sums[0:3], conv_sums[3], dg_conv, dg_attn, post_sums[1], post_sums[2, 0:128]])
    ssem_small, rsem_small, small, land_small, sent_small = _allgather8_start(small, dev, "gather_small_start")

    rici_out = _owners_wait(ssem_out, rsem_out, csum_out, land_out, [grad_x, sent_small], "rs_owners_wait_out")
    full_out, jsend_out, jrecv_out, joining_out = _join_start(
        _owner_sum(place, gout, rsib_out, rici_out, "rs_owner_sum_out"), "rs_join_start_out", 0, 1)
    rici_in = _owners_wait(ssem_in0, rsem_in0, csum_in0, land_in0, [joining_out], "rs_owners_wait_in0")
    full_in0, jsend0, jrecv0, joining0 = _join_start(
        _owner_sum(place, gin, rsib_in, rici_in, "rs_owner_sum_in0", 0, 2), "rs_join_start_in0", 0, 2)
    grad_w_out = _join_wait(full_out, jsend_out, jrecv_out, [joining0], "rs_join_wait_out", 0, 1)
    grad_w_out, delta_w_out, new_m_w_out, new_v_w_out = _adamw(
        w_out2, grad_w_out, m_w_out[0], v_w_out[0], "adamw_w_out")
    full_in0 = _join_wait(full_in0, jsend0, jrecv0, [delta_w_out], "rs_join_wait_in0", 0, 2)
    updated_in = _adamw(w_in2, full_in0, m_w_in[0], v_w_in[0], "adamw_w_in0", 0, 2)
    rici_in = _owners_wait(ssem_in1, rsem_in1, csum_in1, land_in1, [updated_in[1]], "rs_owners_wait_in1")
    full_in1, jsend1, jrecv1, joining1 = _join_start(
        _owner_sum(place, gin, rsib_in, rici_in, "rs_owner_sum_in1", 1, 2), "rs_join_start_in1", 1, 2)

    small_seen = _allgather8_wait(ssem_small, rsem_small, small, land_small, [joining1], "gather_small_wait")
    small_w = [b_ada, g_pre, conv_w, conv_b, g_conv, g_attn, g_post]
    small_m = [m_b_ada, m_g_pre, m_conv_w, m_conv_b, m_g_conv, m_g_attn, m_g_post]
    small_v = [v_b_ada, v_g_pre, v_conv_w, v_conv_b, v_g_conv, v_g_attn, v_g_post]
    pieces = [(0, 3 * d), (so[3], d), (so[4], 3 * dc), (so[5], dc), (so[6], dc), (so[7], da), (so[8], d), (so[9], LANES)]
    g_small, d_small, m_small, v_small, loss_row = _small_update(place, small_seen, pieces, small_w, small_m, small_v)
    loss = loss_row[0, 0]
    grad_b_ada, grad_g_pre, grad_conv_w, grad_conv_b, grad_g_conv, grad_g_attn, grad_g_post = g_small
    dmod_cols = lax.dynamic_slice_in_dim(small_seen.reshape(N_DEV, -1), chip * wa, wa, axis=1)
    grad_w_ada, delta_w_ada, new_m_w_ada, new_v_w_ada = _ada_grad_adamw(c_all, dmod_cols, w_ada2, m_w_ada[0], v_w_ada[0])

    full_in1 = _join_wait(full_in1, jsend1, jrecv1, [delta_w_ada, d_small[0]], "rs_join_wait_in1", 1, 2)
    grad_w_in, delta_w_in, new_m_w_in, new_v_w_in = _adamw(
        w_in2, full_in1, m_w_in[0], v_w_in[0], "adamw_w_in1", 1, 2, updated_in)

    def lead(a):
        return a.reshape((1,) + a.shape)

    grads = [lead(grad_w_ada), grad_b_ada, grad_g_pre, lead(grad_w_in), grad_conv_w, grad_conv_b, grad_g_conv,
             grad_g_attn, lead(grad_w_out), grad_g_post]
    deltas = [lead(delta_w_ada), d_small[0], d_small[1], lead(delta_w_in), d_small[2], d_small[3], d_small[4],
              d_small[5], lead(delta_w_out), d_small[6]]
    new_ms = [lead(new_m_w_ada), m_small[0], m_small[1], lead(new_m_w_in), m_small[2], m_small[3], m_small[4],
              m_small[5], lead(new_m_w_out), m_small[6]]
    new_vs = [lead(new_v_w_ada), v_small[0], v_small[1], lead(new_v_w_in), v_small[2], v_small[3], v_small[4],
              v_small[5], lead(new_v_w_out), v_small[6]]
    return (loss, lead(grad_x), *grads, *deltas, *new_ms, *new_vs)
```
